```python
import math
import jax, jax.numpy as jnp
from jax import lax
import numpy as np

D_MODEL = 1024
BATCH = 8
SEQ = 4096
DEPTH = 1

HEAD_DIM = 64
N_HEADS_FOX = 8
N_HEADS_DIL = 8
FOX_WIDTH = N_HEADS_FOX * HEAD_DIM
DIL_WIDTH = N_HEADS_DIL * HEAD_DIM
DIL_PATTERNS = ((128, 1), (512, 4), (2048, 16))
ROPE_DIM = HEAD_DIM // 4
ROPE_THETA = 500000.0
Q_BLOCK = 128
D_FF = 2816
CONV_WIDTH = 3
RMS_EPS = 1e-6
NEG_INF = -1e30
IN_SPLITS = (FOX_WIDTH, FOX_WIDTH, FOX_WIDTH, N_HEADS_FOX,
             DIL_WIDTH, DIL_WIDTH, DIL_WIDTH, D_MODEL, D_MODEL)
IN_WIDTH = sum(IN_SPLITS)

kernel_name = "hybrid_fox_dilated_gated_convffn"


def rmsnorm(x, g):
    xf = x.astype(jnp.float32)
    inv = lax.rsqrt(jnp.mean(xf * xf, axis=-1, keepdims=True) + RMS_EPS)
    return (xf * inv * g.astype(jnp.float32)).astype(x.dtype)


def partial_rope(t):
    S = t.shape[1]
    half = ROPE_DIM // 2
    inv_freq = ROPE_THETA ** (-jnp.arange(half, dtype=jnp.float32) * 2.0 / ROPE_DIM)
    ang = jnp.arange(S, dtype=jnp.float32)[:, None] * inv_freq[None, :]
    cos = jnp.cos(ang)[:, None, :]
    sin = jnp.sin(ang)[:, None, :]
    tf = t.astype(jnp.float32)
    t1, t2, rest = tf[..., :half], tf[..., half:ROPE_DIM], tf[..., ROPE_DIM:]
    out = jnp.concatenate([t1 * cos - t2 * sin, t2 * cos + t1 * sin, rest], axis=-1)
    return out.astype(t.dtype)


def split_heads(t, n_heads):
    B, S, _ = t.shape
    return t.reshape(B, S, n_heads, HEAD_DIM)


def fox_attention(q, k, v, log_f):
    B, S, H, dh = q.shape
    nb = S // Q_BLOCK
    scale = 1.0 / math.sqrt(dh)
    F = jnp.cumsum(log_f, axis=1).transpose(0, 2, 1)
    kt = k.transpose(0, 2, 1, 3)
    vt = v.transpose(0, 2, 1, 3)
    q_blocks = q.transpose(0, 2, 1, 3).reshape(B, H, nb, Q_BLOCK, dh).transpose(2, 0, 1, 3, 4)
    f_blocks = F.reshape(B, H, nb, Q_BLOCK).transpose(2, 0, 1, 3)
    starts = jnp.arange(nb, dtype=jnp.int32) * Q_BLOCK
    kpos = jnp.arange(S, dtype=jnp.int32)

    def one_block(args):
        qb, fqb, start = args
        s = jnp.einsum('bhqd,bhkd->bhqk', qb, kt).astype(jnp.float32) * scale
        s = s + fqb[..., None] - F[:, :, None, :]
        qpos = start + jnp.arange(Q_BLOCK, dtype=jnp.int32)
        causal = kpos[None, :] <= qpos[:, None]
        s = jnp.where(causal[None, None], s, NEG_INF)
        p = jax.nn.softmax(s, axis=-1)
        return jnp.einsum('bhqk,bhkd->bhqd', p.astype(vt.dtype), vt)

    out = lax.map(one_block, (q_blocks, f_blocks, starts))
    return out.transpose(1, 0, 3, 2, 4).reshape(B, S, H, dh)


def dilated_branch(q, k, v, window, dilation):
    B, S, H, dh = q.shape
    L = S // dilation
    w_sub = window // dilation
    blk = w_sub
    nb = -(-L // blk)
    Lp = nb * blk
    scale = 1.0 / math.sqrt(dh)

    def prep(t):
        t = t.reshape(B, L, dilation, H, dh)
        t = jnp.pad(t, ((0, 0), (0, Lp - L), (0, 0), (0, 0), (0, 0)))
        return t.reshape(B, nb, blk, dilation, H, dh)

    def with_prev(t):
        prev = jnp.pad(t[:, :-1], ((0, 0), (1, 0), (0, 0), (0, 0), (0, 0), (0, 0)))
        return jnp.concatenate([prev, t], axis=2)

    qs = prep(q)
    kk = with_prev(prep(k))
    vv = with_prev(prep(v))
    s = jnp.einsum('bnqrhd,bnkrhd->bnrhqk', qs, kk).astype(jnp.float32) * scale
    qi = jnp.arange(blk)[:, None]
    ki = jnp.arange(2 * blk)[None, :]
    dist = qi + blk - ki
    band = (dist >= 0) & (dist <= w_sub)
    exists = (jnp.arange(nb)[:, None, None] > 0) | (ki[None] >= blk)
    valid = band[None] & exists
    s = jnp.where(valid[None, :, None, None], s, NEG_INF)
    lse = jax.nn.logsumexp(s, axis=-1)
    p = jnp.exp(s - lse[..., None])
    o = jnp.einsum('bnrhqk,bnkrhd->bnqrhd', p.astype(vv.dtype), vv)
    o = o.reshape(B, Lp, dilation, H, dh)[:, :L].reshape(B, S, H, dh)
    lse = lse.transpose(0, 1, 4, 2, 3).reshape(B, Lp, dilation, H)[:, :L].reshape(B, S, H)
    return o, lse


def dilated_attention(q, k, v):
    outs, lses = [], []
    for window, dilation in DIL_PATTERNS:
        o, l = dilated_branch(q, k, v, window, dilation)
        outs.append(o)
        lses.append(l)
    lse = jnp.stack(lses, axis=0)
    alpha = jax.nn.softmax(lse, axis=0)
    out = jnp.stack(outs, axis=0).astype(jnp.float32)
    return jnp.sum(alpha[..., None] * out, axis=0).astype(q.dtype)


def causal_dwconv(u, w, b):
    S = u.shape[1]
    up = jnp.pad(u, ((0, 0), (CONV_WIDTH - 1, 0), (0, 0)))
    y = sum(up[:, i:i + S] * w[i] for i in range(CONV_WIDTH))
    return y + b


def _fwd_setup_inputs(seed: int = 0) -> dict:
    key = jax.random.key(seed)
    ks = jax.random.split(key, 20)
    f32 = jnp.float32

    def nrm(k, shape, fan_in):
        return jax.random.normal(k, shape, f32) * (fan_in ** -0.5)

    def gain(k):
        return 1.0 + 0.05 * jax.random.normal(k, (DEPTH, D_MODEL), f32)

    return {
        "x": jax.random.normal(ks[0], (BATCH, SEQ, D_MODEL), f32),
        "g_pre_mix": gain(ks[1]),
        "w_in": nrm(ks[2], (DEPTH, D_MODEL, IN_WIDTH), D_MODEL),
        "b_forget": 2.0 + 0.5 * jax.random.normal(ks[3], (DEPTH, N_HEADS_FOX), f32),
        "w_o_fox": nrm(ks[4], (DEPTH, FOX_WIDTH, D_MODEL), FOX_WIDTH),
        "w_o_dil": nrm(ks[5], (DEPTH, DIL_WIDTH, D_MODEL), DIL_WIDTH),
        "w_out": nrm(ks[6], (DEPTH, D_MODEL, D_MODEL), D_MODEL),
        "g_post_mix": gain(ks[7]),
        "g_pre_ffn": gain(ks[8]),
        "w_up": nrm(ks[9], (DEPTH, D_MODEL, 2 * D_FF), D_MODEL),
        "conv_w": nrm(ks[10], (DEPTH, CONV_WIDTH, 2 * D_FF), CONV_WIDTH),
        "conv_b": 0.02 * jax.random.normal(ks[11], (DEPTH, 2 * D_FF), f32),
        "w_down": nrm(ks[12], (DEPTH, D_FF, D_MODEL), D_FF),
        "g_post_ffn": gain(ks[13]),
    }


def _fwd_reference(x, g_pre_mix, w_in, b_forget, w_o_fox, w_o_dil, w_out, g_post_mix,
              g_pre_ffn, w_up, conv_w, conv_b, w_down, g_post_ffn):
    B, S, _ = x.shape
    offsets = np.cumsum((0,) + IN_SPLITS)
    for l in range(DEPTH):
        h = rmsnorm(x, g_pre_mix[l])
        z = h @ w_in[l]
        qa, ka, va, fa, qb, kb, vb, ga, gb = [z[..., offsets[i]:offsets[i + 1]]
                                              for i in range(len(IN_SPLITS))]
        log_f = jax.nn.log_sigmoid((fa + b_forget[l]).astype(jnp.float32))
        ya = fox_attention(split_heads(qa, N_HEADS_FOX), split_heads(ka, N_HEADS_FOX),
                           split_heads(va, N_HEADS_FOX), log_f)
        ya = ya.reshape(B, S, FOX_WIDTH) @ w_o_fox[l]
        qd = partial_rope(split_heads(qb, N_HEADS_DIL))
        kd = partial_rope(split_heads(kb, N_HEADS_DIL))
        yb = dilated_attention(qd, kd, split_heads(vb, N_HEADS_DIL))
        yb = yb.reshape(B, S, DIL_WIDTH) @ w_o_dil[l]
        mixed = jax.nn.sigmoid(ga) * ya + jax.nn.sigmoid(gb) * yb
        x = x + rmsnorm(mixed @ w_out[l], g_post_mix[l])
        h = rmsnorm(x, g_pre_ffn[l])
        u = causal_dwconv(h @ w_up[l], conv_w[l], conv_b[l])
        a, b = u[..., :D_FF], u[..., D_FF:]
        m = jax.nn.gelu(a, approximate=True) * b
        x = x + rmsnorm(m @ w_down[l], g_post_ffn[l])
    return x


import jax as _jax
import jax.numpy as _jnp

TWIN_FORMAT = 'train_step'
FWD_PARAMS = ['x', 'g_pre_mix', 'w_in', 'b_forget', 'w_o_fox', 'w_o_dil', 'w_out', 'g_post_mix', 'g_pre_ffn', 'w_up', 'conv_w', 'conv_b', 'w_down', 'g_post_ffn']
TWIN_WEIGHTS = ['g_pre_mix', 'w_in', 'b_forget', 'w_o_fox', 'w_o_dil', 'w_out', 'g_post_mix', 'g_pre_ffn', 'w_up', 'conv_w', 'conv_b', 'w_down', 'g_post_ffn']
TWIN_DIFF_INPUT = 'x'
TWIN_INPUTS = ['x', 'g_pre_mix', 'w_in', 'b_forget', 'w_o_fox', 'w_o_dil', 'w_out', 'g_post_mix', 'g_pre_ffn', 'w_up', 'conv_w', 'conv_b', 'w_down', 'g_post_ffn', 'loss_target', 'm_g_pre_mix', 'm_w_in', 'm_b_forget', 'm_w_o_fox', 'm_w_o_dil', 'm_w_out', 'm_g_post_mix', 'm_g_pre_ffn', 'm_w_up', 'm_conv_w', 'm_conv_b', 'm_w_down', 'm_g_post_ffn', 'v_g_pre_mix', 'v_w_in', 'v_b_forget', 'v_w_o_fox', 'v_w_o_dil', 'v_w_out', 'v_g_post_mix', 'v_g_pre_ffn', 'v_w_up', 'v_conv_w', 'v_conv_b', 'v_w_down', 'v_g_post_ffn']
TWIN_OUTPUTS = ['loss', 'grad_x', 'grad_g_pre_mix', 'grad_w_in', 'grad_b_forget', 'grad_w_o_fox', 'grad_w_o_dil', 'grad_w_out', 'grad_g_post_mix', 'grad_g_pre_ffn', 'grad_w_up', 'grad_conv_w', 'grad_conv_b', 'grad_w_down', 'grad_g_post_ffn', 'delta_g_pre_mix', 'delta_w_in', 'delta_b_forget', 'delta_w_o_fox', 'delta_w_o_dil', 'delta_w_out', 'delta_g_post_mix', 'delta_g_pre_ffn', 'delta_w_up', 'delta_conv_w', 'delta_conv_b', 'delta_w_down', 'delta_g_post_ffn', 'new_m_g_pre_mix', 'new_m_w_in', 'new_m_b_forget', 'new_m_w_o_fox', 'new_m_w_o_dil', 'new_m_w_out', 'new_m_g_post_mix', 'new_m_g_pre_ffn', 'new_m_w_up', 'new_m_conv_w', 'new_m_conv_b', 'new_m_w_down', 'new_m_g_post_ffn', 'new_v_g_pre_mix', 'new_v_w_in', 'new_v_b_forget', 'new_v_w_o_fox', 'new_v_w_o_dil', 'new_v_w_out', 'new_v_g_post_mix', 'new_v_g_pre_ffn', 'new_v_w_up', 'new_v_conv_w', 'new_v_conv_b', 'new_v_w_down', 'new_v_g_post_ffn']
TWIN_LEAF_KINDS = {'loss': 'loss', 'grad_x': 'grad_x', 'grad_g_pre_mix': 'grad_w', 'grad_w_in': 'grad_w', 'grad_b_forget': 'grad_w', 'grad_w_o_fox': 'grad_w', 'grad_w_o_dil': 'grad_w', 'grad_w_out': 'grad_w', 'grad_g_post_mix': 'grad_w', 'grad_g_pre_ffn': 'grad_w', 'grad_w_up': 'grad_w', 'grad_conv_w': 'grad_w', 'grad_conv_b': 'grad_w', 'grad_w_down': 'grad_w', 'grad_g_post_ffn': 'grad_w', 'delta_g_pre_mix': 'delta_w', 'delta_w_in': 'delta_w', 'delta_b_forget': 'delta_w', 'delta_w_o_fox': 'delta_w', 'delta_w_o_dil': 'delta_w', 'delta_w_out': 'delta_w', 'delta_g_post_mix': 'delta_w', 'delta_g_pre_ffn': 'delta_w', 'delta_w_up': 'delta_w', 'delta_conv_w': 'delta_w', 'delta_conv_b': 'delta_w', 'delta_w_down': 'delta_w', 'delta_g_post_ffn': 'delta_w', 'new_m_g_pre_mix': 'new_m', 'new_m_w_in': 'new_m', 'new_m_b_forget': 'new_m', 'new_m_w_o_fox': 'new_m', 'new_m_w_o_dil': 'new_m', 'new_m_w_out': 'new_m', 'new_m_g_post_mix': 'new_m', 'new_m_g_pre_ffn': 'new_m', 'new_m_w_up': 'new_m', 'new_m_conv_w': 'new_m', 'new_m_conv_b': 'new_m', 'new_m_w_down': 'new_m', 'new_m_g_post_ffn': 'new_m', 'new_v_g_pre_mix': 'new_v', 'new_v_w_in': 'new_v', 'new_v_b_forget': 'new_v', 'new_v_w_o_fox': 'new_v', 'new_v_w_o_dil': 'new_v', 'new_v_w_out': 'new_v', 'new_v_g_post_mix': 'new_v', 'new_v_g_pre_ffn': 'new_v', 'new_v_w_up': 'new_v', 'new_v_conv_w': 'new_v', 'new_v_conv_b': 'new_v', 'new_v_w_down': 'new_v', 'new_v_g_post_ffn': 'new_v'}


def _forward(args):
    return _fwd_reference(*[args[k] for k in FWD_PARAMS])


def _output_shape():
    def fwd():
        inp = _fwd_setup_inputs(0)
        return _fwd_reference(*[inp[k] for k in FWD_PARAMS])
    out = _jax.eval_shape(fwd)
    return out.shape, out.dtype

N_MICROBATCH = 1
ADAM_LR = 0.001
ADAM_B1 = 0.9
ADAM_B2 = 0.999
ADAM_EPS = 1e-08
ADAM_WD = 0.01
ADAM_STEP = 10
PER_EXAMPLE_BATCH_AXIS = {'x': 0, 'loss_target': 0}
SHARED_INPUTS = []
_WEIGHT_DTYPES = {'g_pre_mix': _jnp.float32, 'w_in': _jnp.float32, 'b_forget': _jnp.float32, 'w_o_fox': _jnp.float32, 'w_o_dil': _jnp.float32, 'w_out': _jnp.float32, 'g_post_mix': _jnp.float32, 'g_pre_ffn': _jnp.float32, 'w_up': _jnp.float32, 'conv_w': _jnp.float32, 'conv_b': _jnp.float32, 'w_down': _jnp.float32, 'g_post_ffn': _jnp.float32}
MOMENT_SCALE = {'g_pre_mix': 8.503191e-01, 'w_in': 3.364379e-01, 'b_forget': 6.035263e+00, 'w_o_fox': 6.138175e-01, 'w_o_dil': 2.064231e-01, 'w_out': 6.481100e-01, 'g_post_mix': 3.191607e+01, 'g_pre_ffn': 6.198958e-01, 'w_up': 2.505209e-01, 'conv_w': 2.745870e-01, 'conv_b': 4.402688e-01, 'w_down': 5.070431e-01, 'g_post_ffn': 3.205995e+01}


def _to_microbatches(a, axis):
    t = _jnp.moveaxis(a, axis, 0)
    t = t.reshape((N_MICROBATCH, t.shape[0] // N_MICROBATCH) + t.shape[1:])
    return _jnp.moveaxis(t, 1, axis + 1)


def setup_inputs(seed: int = 0) -> dict:
    inp = _fwd_setup_inputs(seed)
    key = _jax.random.fold_in(_jax.random.key(seed), 7919)
    shape, _ = _output_shape()
    out = dict(inp)
    out["loss_target"] = _jax.random.normal(_jax.random.fold_in(key, 0), shape, _jnp.float32)
    for i, name in enumerate(TWIN_WEIGHTS):
        w = inp[name].astype(_jnp.float32)
        if MOMENT_SCALE is None:
            s = _jnp.sqrt(_jnp.mean(_jnp.square(w)) + 1e-30)
        else:
            s = MOMENT_SCALE[name]
        km, kv = _jax.random.split(_jax.random.fold_in(key, i + 1))
        out[name] = w
        out["m_" + name] = s * _jax.random.normal(km, w.shape, _jnp.float32)
        out["v_" + name] = (s * s) * _jax.random.uniform(kv, w.shape, _jnp.float32, 0.5, 1.5)
    if N_MICROBATCH > 1:
        for name, axis in PER_EXAMPLE_BATCH_AXIS.items():
            out[name] = _to_microbatches(out[name], axis)
    return {'x': out['x'], 'g_pre_mix': out['g_pre_mix'], 'w_in': out['w_in'], 'b_forget': out['b_forget'], 'w_o_fox': out['w_o_fox'], 'w_o_dil': out['w_o_dil'], 'w_out': out['w_out'], 'g_post_mix': out['g_post_mix'], 'g_pre_ffn': out['g_pre_ffn'], 'w_up': out['w_up'], 'conv_w': out['conv_w'], 'conv_b': out['conv_b'], 'w_down': out['w_down'], 'g_post_ffn': out['g_post_ffn'], 'loss_target': out['loss_target'], 'm_g_pre_mix': out['m_g_pre_mix'], 'm_w_in': out['m_w_in'], 'm_b_forget': out['m_b_forget'], 'm_w_o_fox': out['m_w_o_fox'], 'm_w_o_dil': out['m_w_o_dil'], 'm_w_out': out['m_w_out'], 'm_g_post_mix': out['m_g_post_mix'], 'm_g_pre_ffn': out['m_g_pre_ffn'], 'm_w_up': out['m_w_up'], 'm_conv_w': out['m_conv_w'], 'm_conv_b': out['m_conv_b'], 'm_w_down': out['m_w_down'], 'm_g_post_ffn': out['m_g_post_ffn'], 'v_g_pre_mix': out['v_g_pre_mix'], 'v_w_in': out['v_w_in'], 'v_b_forget': out['v_b_forget'], 'v_w_o_fox': out['v_w_o_fox'], 'v_w_o_dil': out['v_w_o_dil'], 'v_w_out': out['v_w_out'], 'v_g_post_mix': out['v_g_post_mix'], 'v_g_pre_ffn': out['v_g_pre_ffn'], 'v_w_up': out['v_w_up'], 'v_conv_w': out['v_conv_w'], 'v_conv_b': out['v_conv_b'], 'v_w_down': out['v_w_down'], 'v_g_post_ffn': out['v_g_post_ffn']}


def _loss(weights, diff, rest, loss_target):
    with _jax.named_scope("forward"):
        args = {**rest, TWIN_DIFF_INPUT: diff, **{k: w.astype(_WEIGHT_DTYPES[k]) for k, w in weights.items()}}
        y = _forward(args)
    with _jax.named_scope("loss_head"):
        err = _jnp.square(y.astype(_jnp.float32) - loss_target)
        return 0.5 * _jnp.sum(_jnp.mean(err, axis=-1)) if err.ndim else 0.5 * err


def _adamw(w, g, m, v):
    m = ADAM_B1 * m + (1.0 - ADAM_B1) * g
    v = ADAM_B2 * v + (1.0 - ADAM_B2) * _jnp.square(g)
    m_hat = m / (1.0 - ADAM_B1 ** ADAM_STEP)
    v_hat = v / (1.0 - ADAM_B2 ** ADAM_STEP)
    delta = -ADAM_LR * (m_hat / (_jnp.sqrt(v_hat) + ADAM_EPS) + ADAM_WD * w)
    return delta, m, v


def reference(x, g_pre_mix, w_in, b_forget, w_o_fox, w_o_dil, w_out, g_post_mix, g_pre_ffn, w_up, conv_w, conv_b, w_down, g_post_ffn, loss_target, m_g_pre_mix, m_w_in, m_b_forget, m_w_o_fox, m_w_o_dil, m_w_out, m_g_post_mix, m_g_pre_ffn, m_w_up, m_conv_w, m_conv_b, m_w_down, m_g_post_ffn, v_g_pre_mix, v_w_in, v_b_forget, v_w_o_fox, v_w_o_dil, v_w_out, v_g_post_mix, v_g_pre_ffn, v_w_up, v_conv_w, v_conv_b, v_w_down, v_g_post_ffn):
    given = dict(x=x, g_pre_mix=g_pre_mix, w_in=w_in, b_forget=b_forget, w_o_fox=w_o_fox, w_o_dil=w_o_dil, w_out=w_out, g_post_mix=g_post_mix, g_pre_ffn=g_pre_ffn, w_up=w_up, conv_w=conv_w, conv_b=conv_b, w_down=w_down, g_post_ffn=g_post_ffn, loss_target=loss_target, m_g_pre_mix=m_g_pre_mix, m_w_in=m_w_in, m_b_forget=m_b_forget, m_w_o_fox=m_w_o_fox, m_w_o_dil=m_w_o_dil, m_w_out=m_w_out, m_g_post_mix=m_g_post_mix, m_g_pre_ffn=m_g_pre_ffn, m_w_up=m_w_up, m_conv_w=m_conv_w, m_conv_b=m_conv_b, m_w_down=m_w_down, m_g_post_ffn=m_g_post_ffn, v_g_pre_mix=v_g_pre_mix, v_w_in=v_w_in, v_b_forget=v_b_forget, v_w_o_fox=v_w_o_fox, v_w_o_dil=v_w_o_dil, v_w_out=v_w_out, v_g_post_mix=v_g_post_mix, v_g_pre_ffn=v_g_pre_ffn, v_w_up=v_w_up, v_conv_w=v_conv_w, v_conv_b=v_conv_b, v_w_down=v_w_down, v_g_post_ffn=v_g_post_ffn)
    weights = {n: given[n] for n in TWIN_WEIGHTS}
    shared = {n: given[n] for n in SHARED_INPUTS}
    per_example = {n: given[n] for n in ['x']}
    grad_fn = _jax.value_and_grad(_loss, argnums=(0, 1))

    def one_microbatch(ex, loss_target):
        ex = dict(ex)
        diff = ex.pop(TWIN_DIFF_INPUT)
        return grad_fn(weights, diff, {**shared, **ex}, loss_target)

    if N_MICROBATCH == 1:
        loss, (grad_w, grad_x) = one_microbatch(per_example, given["loss_target"])
    else:
        def body(carry, xs):
            loss_sum, grad_sum = carry
            l_k, (gw_k, gx_k) = one_microbatch(xs[0], xs[1])
            with _jax.named_scope("update"):
                return (loss_sum + l_k, _jax.tree.map(_jnp.add, grad_sum, gw_k)), gx_k

        init = (_jnp.zeros((), _jnp.float32), _jax.tree.map(_jnp.zeros_like, weights))
        (loss, grad_w), grad_x = _jax.lax.scan(body, init, (per_example, given["loss_target"]))
    with _jax.named_scope("update"):
        delta_w, new_m, new_v = {}, {}, {}
        for n in TWIN_WEIGHTS:
            delta_w[n], new_m[n], new_v[n] = _adamw(weights[n], grad_w[n], given["m_" + n], given["v_" + n])
    return (loss, grad_x, *[grad_w[n] for n in TWIN_WEIGHTS], *[delta_w[n] for n in TWIN_WEIGHTS],
            *[new_m[n] for n in TWIN_WEIGHTS], *[new_v[n] for n in TWIN_WEIGHTS])
```

```python
import functools
import math

import numpy as np
import jax
import jax.numpy as jnp
from jax import lax
from jax.experimental import pallas as pl
from jax.experimental.pallas import tpu as pltpu

F32 = jnp.float32
BF16 = jnp.bfloat16
SDS = jax.ShapeDtypeStruct
MESH = pl.DeviceIdType.MESH

HEAD_DIM = 64
N_HEADS = 8
LANES = 128
ATT_W = N_HEADS * HEAD_DIM
DIL_PATTERNS = ((128, 1), (512, 4), (2048, 16))
DIL_BLK = 128
ROPE_DIM = HEAD_DIM // 4
ROPE_THETA = 500000.0
RMS_EPS = 1e-6
NEG = -1e30
QK_SCALE = 1.0 / math.sqrt(HEAD_DIM)
ADAM_LR, ADAM_B1, ADAM_B2, ADAM_EPS, ADAM_WD, ADAM_STEP = 0.001, 0.9, 0.999, 1e-08, 0.01, 10
VMEM_LIMIT = 56 * 1024 * 1024

Z_QA, Z_KA, Z_VA, Z_QB, Z_KB, Z_VB = 0, 1, 2, 3, 4, 5
Z_W = 5120


def _cp(sem):
    return pltpu.CompilerParams(dimension_semantics=sem, vmem_limit_bytes=VMEM_LIMIT)


def _nt(a, b):
    return lax.dot_general(a, b, (((1,), (1,)), ((), ())), preferred_element_type=F32)


def _tn(a, b):
    return lax.dot_general(a, b, (((0,), (0,)), ((), ())), preferred_element_type=F32)


def _nn(a, b):
    return jnp.dot(a, b, preferred_element_type=F32)


def _lane(shape):
    return lax.broadcasted_iota(jnp.int32, shape, 1)


def _row(shape):
    return lax.broadcasted_iota(jnp.int32, shape, 0)


def rmsnorm_fwd(x, g, *, tm=512):
    s, d = x.shape

    def body(x_ref, g_ref, h_ref):
        xv = x_ref[...]
        inv = lax.rsqrt(jnp.mean(xv * xv, axis=-1, keepdims=True) + RMS_EPS)
        h_ref[...] = (xv * inv * g_ref[...]).astype(h_ref.dtype)

    return pl.pallas_call(
        body, grid=(s // tm,),
        in_specs=[pl.BlockSpec((tm, d), lambda i: (i, 0)), pl.BlockSpec((1, d), lambda i: (0, 0))],
        out_specs=pl.BlockSpec((tm, d), lambda i: (i, 0)),
        out_shape=SDS((s, d), BF16), name="rmsnorm_fwd", compiler_params=_cp(("parallel",)))(x, g)


def rmsnorm_bwd(dh, x, g, res, *, out_dtype, tm=256, name):
    s, d = x.shape
    n = s // tm
    has_res = res is not None

    def body(*refs):
        if has_res:
            dh_ref, x_ref, g_ref, res_ref, dx_ref, dg_ref, acc = refs
        else:
            dh_ref, x_ref, g_ref, dx_ref, dg_ref, acc = refs
        i = pl.program_id(0)

        @pl.when(i == 0)
        def _():
            acc[...] = jnp.zeros_like(acc)

        xv = x_ref[...]
        inv = lax.rsqrt(jnp.mean(xv * xv, axis=-1, keepdims=True) + RMS_EPS)
        xh = xv * inv
        dhv = dh_ref[...].astype(F32)
        dxh = dhv * g_ref[...]
        dot = jnp.mean(dxh * xh, axis=-1, keepdims=True)
        dx = inv * (dxh - xh * dot)
        if has_res:
            dx = dx + res_ref[...]
        dx_ref[...] = dx.astype(dx_ref.dtype)
        acc[...] += jnp.sum((dhv * xh).reshape(tm // 8, 8, d), axis=0)

        @pl.when(i == n - 1)
        def _():
            dg_ref[...] = jnp.sum(acc[...], axis=0, keepdims=True)

    row = pl.BlockSpec((tm, d), lambda i: (i, 0))
    in_specs = [row, row, pl.BlockSpec((1, d), lambda i: (0, 0))] + ([row] if has_res else [])
    args = [dh, x, g] + ([res] if has_res else [])
    return pl.pallas_call(
        body, grid=(n,), in_specs=in_specs,
        out_specs=[row, pl.BlockSpec((1, d), lambda i: (0, 0))],
        out_shape=[SDS((s, d), out_dtype), SDS((1, d), F32)],
        scratch_shapes=[pltpu.VMEM((8, d), F32)],
        name=name, compiler_params=_cp(("arbitrary",)))(*args)


def mm(a_views, b_views, *, nt, out_dtype, tm, tn, name):
    n_p = len(a_views)
    m = a_views[0][0].shape[0]
    n = b_views[0][0].shape[0] if nt else b_views[0][0].shape[1]

    def body(*refs):
        o_ref = refs[-1]
        acc = None
        for p in range(n_p):
            av = refs[p][...].astype(BF16)
            bv = refs[n_p + p][...].astype(BF16)
            dv = _nt(av, bv) if nt else _nn(av, bv)
            acc = dv if acc is None else acc + dv
        o_ref[...] = acc.astype(o_ref.dtype)

    in_specs = []
    for arr, w, blk in a_views:
        in_specs.append(pl.BlockSpec((tm, w), functools.partial(lambda i, j, blk: (i, blk), blk=blk)))
    for arr, w, blk in b_views:
        if nt:
            in_specs.append(pl.BlockSpec((tn, w), functools.partial(lambda i, j, blk: (j, blk), blk=blk)))
        else:
            in_specs.append(pl.BlockSpec((w, tn), lambda i, j: (0, j)))
    return pl.pallas_call(
        body, grid=(m // tm, n // tn), in_specs=in_specs,
        out_specs=pl.BlockSpec((tm, tn), lambda i, j: (i, j)),
        out_shape=SDS((m, n), out_dtype), name=name,
        compiler_params=_cp(("parallel", "parallel")))(*[a[0] for a in a_views], *[b[0] for b in b_views])


def wgrad(a_view, g, *, tk, tn, ts, name, chip_major=False):
    arr, ka, blk = a_view
    s, n = g.shape
    ns = s // ts

    def body(a_ref, g_ref, o_ref):
        @pl.when(pl.program_id(2) == 0)
        def _():
            o_ref[...] = jnp.zeros_like(o_ref)

        o_ref[...] += _tn(a_ref[...].astype(BF16), g_ref[...].astype(BF16))

    if chip_major:
        out_spec = pl.BlockSpec((None, tk, tn), lambda i, j, k: (j, i, 0))
        out_shape = SDS((n // tn, ka, tn), F32)
    else:
        out_spec = pl.BlockSpec((tk, tn), lambda i, j, k: (i, j))
        out_shape = SDS((ka, n), F32)
    return pl.pallas_call(
        body, grid=(ka // tk, n // tn, ns),
        in_specs=[pl.BlockSpec((ts, tk), lambda i, j, k: (k, blk * (ka // tk) + i)),
                  pl.BlockSpec((ts, tn), lambda i, j, k: (k, j))],
        out_specs=out_spec, out_shape=out_shape, name=name,
        compiler_params=_cp(("parallel", "parallel", "arbitrary")))(arr, g)


def mm_rms_res(a, w, g, xres, target=None, *, tm=256, name):
    s, k = a.shape
    d = w.shape[1]
    n = s // tm
    with_loss = target is not None

    def body(*refs):
        if with_loss:
            a_ref, w_ref, g_ref, x_ref, t_ref, y_ref, o_ref, l_ref = refs
        else:
            a_ref, w_ref, g_ref, x_ref, y_ref, o_ref = refs
        y = _nn(a_ref[...], w_ref[...])
        inv = lax.rsqrt(jnp.mean(y * y, axis=-1, keepdims=True) + RMS_EPS)
        xn = x_ref[...] + y * inv * g_ref[...]
        y_ref[...] = y
        if with_loss:
            err = xn - t_ref[...]
            o_ref[...] = err * (1.0 / d)

            @pl.when(pl.program_id(0) == 0)
            def _():
                l_ref[...] = jnp.zeros_like(l_ref)

            l_ref[...] += jnp.sum(jnp.sum(err * err, axis=1, keepdims=True), axis=0, keepdims=True)
        else:
            o_ref[...] = xn

    row = pl.BlockSpec((tm, d), lambda i: (i, 0))
    in_specs = [pl.BlockSpec((tm, k), lambda i: (i, 0)), pl.BlockSpec((k, d), lambda i: (0, 0)),
                pl.BlockSpec((1, d), lambda i: (0, 0)), row]
    out_specs = [row, row]
    out_shape = [SDS((s, d), F32), SDS((s, d), F32)]
    args = [a, w, g, xres]
    if with_loss:
        in_specs.append(row)
        out_specs.append(pl.BlockSpec((1, 1), lambda i: (0, 0)))
        out_shape.append(SDS((1, 1), F32))
        args.append(target)
    return pl.pallas_call(
        body, grid=(n,), in_specs=in_specs, out_specs=out_specs, out_shape=out_shape, name=name,
        compiler_params=_cp(("arbitrary",)))(*args)


def _split3(v):
    hi = v.astype(BF16).astype(F32)
    r = v - hi
    mid = r.astype(BF16).astype(F32)
    lo = (r - mid).astype(BF16).astype(F32)
    return hi, mid, lo


def _tri(n, upper):
    r = np.arange(n)
    m = (r[:, None] <= r[None, :]) if upper else (r[:, None] >= r[None, :])
    return jnp.asarray(m.astype(np.float32))


def fox_prep(z, fa, bfo, *, tb=512):
    s = z.shape[0]
    n = s // tb

    def body(q_ref, k_ref, fa_ref, b_ref, tri_ref, qa_ref, ka_ref, carry):
        @pl.when(pl.program_id(0) == 0)
        def _():
            carry[...] = jnp.zeros_like(carry)

        xv = fa_ref[...] + b_ref[...]
        logf = jnp.minimum(xv, 0.0) - jnp.log(1.0 + jnp.exp(-jnp.abs(xv)))
        csum = jnp.dot(tri_ref[...], logf, preferred_element_type=F32, precision=lax.Precision.HIGHEST) + carry[0:1, :]
        carry[0:1, :] = csum[tb - 1:tb, :]
        lane = _lane((tb, LANES))
        for h in range(N_HEADS):
            hi, mid, lo = _split3(csum[:, h:h + 1])
            pair = (h // 2) * LANES
            qv = q_ref[:, pair:pair + LANES].astype(F32)
            kv = k_ref[:, pair:pair + LANES].astype(F32)
            if h % 2:
                qv = pltpu.roll(qv, 64, axis=1)
                kv = pltpu.roll(kv, 64, axis=1)
            one = jnp.where((lane >= 67) & (lane < 70), 1.0, 0.0)
            q_x = jnp.where(lane == 64, hi, jnp.where(lane == 65, mid, jnp.where(lane == 66, lo, one)))
            one = jnp.where((lane >= 64) & (lane < 67), 1.0, 0.0)
            k_x = jnp.where(lane == 67, -hi, jnp.where(lane == 68, -mid, jnp.where(lane == 69, -lo, one)))
            qa_ref[:, h * LANES:(h + 1) * LANES] = jnp.where(lane < 64, qv * QK_SCALE, q_x).astype(BF16)
            ka_ref[:, h * LANES:(h + 1) * LANES] = jnp.where(lane < 64, kv, k_x).astype(BF16)

    return pl.pallas_call(
        body, grid=(n,),
        in_specs=[pl.BlockSpec((tb, ATT_W), lambda i: (i, Z_QA)), pl.BlockSpec((tb, ATT_W), lambda i: (i, Z_KA)),
                  pl.BlockSpec((tb, LANES), lambda i: (i, 0)), pl.BlockSpec((1, LANES), lambda i: (0, 0)),
                  pl.BlockSpec((tb, tb), lambda i: (0, 0))],
        out_specs=[pl.BlockSpec((tb, N_HEADS * LANES), lambda i: (i, 0))] * 2,
        out_shape=[SDS((s, N_HEADS * LANES), BF16)] * 2,
        scratch_shapes=[pltpu.VMEM((8, LANES), F32)],
        name="fox_prep", compiler_params=_cp(("arbitrary",)))(z, z, fa, bfo, _tri(tb, False))


def _causal_pairs(n, k_major):
    if k_major:
        pairs = [(qi, kj) for kj in range(n) for qi in range(kj, n)]
    else:
        pairs = [(qi, kj) for qi in range(n) for kj in range(qi + 1)]
    return (jnp.asarray([p[0] for p in pairs], jnp.int32), jnp.asarray([p[1] for p in pairs], jnp.int32), len(pairs))


def fox_fwd(q_aug, k_aug, z, *, t=512):
    s = z.shape[0]
    qi_arr, kj_arr, n_pairs = _causal_pairs(s // t, False)

    def body(qi_ref, kj_ref, q_ref, k_ref, v_ref, o_ref, lse_ref, m_scr, l_scr, acc_scr):
        step = pl.program_id(1)
        qi = qi_ref[step]
        kj = kj_ref[step]

        @pl.when(kj == 0)
        def _():
            m_scr[...] = jnp.full_like(m_scr, NEG)
            l_scr[...] = jnp.zeros_like(l_scr)
            acc_scr[...] = jnp.zeros_like(acc_scr)

        def update(masked):
            for i in range(2):
                sc = _nt(q_ref[:, i * LANES:(i + 1) * LANES], k_ref[:, i * LANES:(i + 1) * LANES])
                if masked:
                    sc = jnp.where(_row((t, t)) >= _lane((t, t)), sc, NEG)
                m_prev = m_scr[i]
                m_new = jnp.maximum(m_prev, jnp.max(sc, axis=-1, keepdims=True))
                alpha = jnp.exp(m_prev - m_new)
                p = jnp.exp(sc - m_new)
                l_scr[i] = alpha * l_scr[i] + jnp.sum(p, axis=-1, keepdims=True)
                acc_scr[i] = alpha * acc_scr[i] + _nn(p.astype(BF16), v_ref[...])
                m_scr[i] = m_new

        @pl.when(kj < qi)
        def _():
            update(False)

        @pl.when(kj == qi)
        def _():
            update(True)
            lane = _lane((t, LANES))
            o_ref[...] = jnp.where(lane < 64, acc_scr[0] / l_scr[0], acc_scr[1] / l_scr[1]).astype(o_ref.dtype)
            lse_ref[...] = jnp.where(lane < 64, m_scr[0] + jnp.log(l_scr[0]), m_scr[1] + jnp.log(l_scr[1]))

    grid_spec = pltpu.PrefetchScalarGridSpec(
        num_scalar_prefetch=2, grid=(4, n_pairs),
        in_specs=[pl.BlockSpec((t, 2 * LANES), lambda hp, st, qi, kj: (qi[st], hp)),
                  pl.BlockSpec((t, 2 * LANES), lambda hp, st, qi, kj: (kj[st], hp)),
                  pl.BlockSpec((t, LANES), lambda hp, st, qi, kj: (kj[st], 4 * Z_VA + hp))],
        out_specs=[pl.BlockSpec((t, LANES), lambda hp, st, qi, kj: (qi[st], hp))] * 2,
        scratch_shapes=[pltpu.VMEM((2, t, 1), F32), pltpu.VMEM((2, t, 1), F32), pltpu.VMEM((2, t, LANES), F32)])
    return pl.pallas_call(
        body, grid_spec=grid_spec, out_shape=[SDS((s, ATT_W), BF16), SDS((s, ATT_W), F32)],
        name="fox_fwd", compiler_params=_cp(("parallel", "arbitrary")))(qi_arr, kj_arr, q_aug, k_aug, z)


def fox_bwd(q_aug, k_aug, z, dy, lse, dd, *, t=512):
    s = z.shape[0]
    qi_arr, kj_arr, n_pairs = _causal_pairs(s // t, True)

    def body(qi_ref, kj_ref, q_ref, k_ref, v_ref, do_ref, lse_ref, dd_ref, dq_ref, dk_ref, dv_ref):
        step = pl.program_id(1)
        qi = qi_ref[step]
        kj = kj_ref[step]

        @pl.when(step == 0)
        def _():
            dq_ref[...] = jnp.zeros_like(dq_ref)

        @pl.when(qi == kj)
        def _():
            dk_ref[...] = jnp.zeros_like(dk_ref)
            dv_ref[...] = jnp.zeros_like(dv_ref)

        def update(masked):
            lane = _lane((t, LANES))
            rows = pl.ds(pl.multiple_of(qi * t, t), t)
            dov = do_ref[...]
            dv_new = None
            for i in range(2):
                head = (lane < 64) if i == 0 else (lane >= 64)
                qv = q_ref[:, i * LANES:(i + 1) * LANES]
                kv = k_ref[:, i * LANES:(i + 1) * LANES]
                sc = _nt(qv, kv)
                if masked:
                    sc = jnp.where(_row((t, t)) >= _lane((t, t)), sc, NEG)
                p = jnp.exp(sc - lse_ref[:, i * 64:i * 64 + 1])
                dp = _nt(jnp.where(head, dov, jnp.zeros_like(dov)), v_ref[...])
                ds = (p * (dp - dd_ref[:, i * 64:i * 64 + 1])).astype(BF16)
                dq_ref[rows, i * LANES:(i + 1) * LANES] += _nn(ds, kv)
                dk_ref[:, i * LANES:(i + 1) * LANES] += _tn(ds, qv)
                dvi = _tn(p.astype(BF16), dov)
                dv_new = dvi if dv_new is None else jnp.where(head, dvi, dv_new)
            dv_ref[...] += dv_new

        @pl.when(kj < qi)
        def _():
            update(False)

        @pl.when(kj == qi)
        def _():
            update(True)

    grid_spec = pltpu.PrefetchScalarGridSpec(
        num_scalar_prefetch=2, grid=(4, n_pairs),
        in_specs=[pl.BlockSpec((t, 2 * LANES), lambda hp, st, qi, kj: (qi[st], hp)),
                  pl.BlockSpec((t, 2 * LANES), lambda hp, st, qi, kj: (kj[st], hp)),
                  pl.BlockSpec((t, LANES), lambda hp, st, qi, kj: (kj[st], 4 * Z_VA + hp)),
                  pl.BlockSpec((t, LANES), lambda hp, st, qi, kj: (qi[st], hp)),
                  pl.BlockSpec((t, LANES), lambda hp, st, qi, kj: (qi[st], hp)),
                  pl.BlockSpec((t, LANES), lambda hp, st, qi, kj: (qi[st], hp))],
        out_specs=[pl.BlockSpec((s, 2 * LANES), lambda hp, st, qi, kj: (0, hp)),
                   pl.BlockSpec((t, 2 * LANES), lambda hp, st, qi, kj: (kj[st], hp)),
                   pl.BlockSpec((t, LANES), lambda hp, st, qi, kj: (kj[st], hp))])
    return pl.pallas_call(
        body, grid_spec=grid_spec,
        out_shape=[SDS((s, N_HEADS * LANES), F32), SDS((s, N_HEADS * LANES), F32), SDS((s, ATT_W), F32)],
        name="fox_bwd", compiler_params=_cp(("parallel", "arbitrary")))(qi_arr, kj_arr, q_aug, k_aug, z, dy, lse, dd)


def head_rowsum(a, b, *, tm=512, name):
    s = a.shape[0]

    def body(a_ref, b_ref, o_ref):
        prod = a_ref[...].astype(F32) * b_ref[...].astype(F32)
        lane = _lane((tm, LANES))
        lo = jnp.sum(jnp.where(lane < 64, prod, 0.0), axis=-1, keepdims=True)
        hi = jnp.sum(jnp.where(lane >= 64, prod, 0.0), axis=-1, keepdims=True)
        o_ref[...] = jnp.where(lane < 64, lo, hi)

    blk = pl.BlockSpec((tm, LANES), lambda i, j: (i, j))
    return pl.pallas_call(body, grid=(s // tm, 4), in_specs=[blk, blk], out_specs=blk, out_shape=SDS((s, ATT_W), F32),
                          name=name, compiler_params=_cp(("parallel", "parallel")))(a, b)


def fox_post(dq_aug, dk_aug, dv, fa, bfo, *, tb=512):
    s = dv.shape[0]
    n = s // tb

    def body(dq_ref, dk_ref, dv_ref, fa_ref, b_ref, tri_ref, dz_ref, dfa_ref, gb_ref, carry, acc):
        i = pl.program_id(0)

        @pl.when(i == 0)
        def _():
            carry[...] = jnp.zeros_like(carry)
            acc[...] = jnp.zeros_like(acc)

        lane = _lane((tb, LANES))
        d_f = jnp.zeros((tb, LANES), F32)
        for h in range(N_HEADS):
            col = dq_ref[:, h * LANES + 64:h * LANES + 65] - dk_ref[:, h * LANES + 67:h * LANES + 68]
            d_f = jnp.where(lane == h, col, d_f)
        suffix = jnp.dot(tri_ref[...], d_f, preferred_element_type=F32, precision=lax.Precision.HIGHEST) + carry[0:1, :]
        carry[0:1, :] = suffix[0:1, :]
        xv = fa_ref[...] + b_ref[...]
        dx = suffix * (1.0 / (1.0 + jnp.exp(xv)))
        dfa_ref[...] = dx.astype(dfa_ref.dtype)
        acc[...] += jnp.sum(dx.reshape(tb // 8, 8, LANES), axis=0)
        for hp in range(4):
            for src, off, scale in ((dq_ref, 0, QK_SCALE), (dk_ref, ATT_W, 1.0)):
                even = src[:, (2 * hp) * LANES:(2 * hp + 1) * LANES]
                odd = pltpu.roll(src[:, (2 * hp + 1) * LANES:(2 * hp + 2) * LANES], 64, axis=1)
                dz_ref[:, off + hp * LANES:off + (hp + 1) * LANES] = (jnp.where(lane < 64, even, odd) * scale).astype(BF16)
        dz_ref[:, 2 * ATT_W:3 * ATT_W] = dv_ref[...].astype(BF16)

        @pl.when(i == n - 1)
        def _():
            gb_ref[...] = jnp.sum(acc[...], axis=0, keepdims=True)

    rev = lambda i: (n - 1 - i, 0)
    return pl.pallas_call(
        body, grid=(n,),
        in_specs=[pl.BlockSpec((tb, N_HEADS * LANES), rev), pl.BlockSpec((tb, N_HEADS * LANES), rev),
                  pl.BlockSpec((tb, ATT_W), rev), pl.BlockSpec((tb, LANES), rev),
                  pl.BlockSpec((1, LANES), lambda i: (0, 0)), pl.BlockSpec((tb, tb), lambda i: (0, 0))],
        out_specs=[pl.BlockSpec((tb, 3 * ATT_W), rev), pl.BlockSpec((tb, LANES), rev),
                   pl.BlockSpec((1, LANES), lambda i: (0, 0))],
        out_shape=[SDS((s, 3 * ATT_W), BF16), SDS((s, LANES), BF16), SDS((1, LANES), F32)],
        scratch_shapes=[pltpu.VMEM((8, LANES), F32), pltpu.VMEM((8, LANES), F32)],
        name="fox_post", compiler_params=_cp(("arbitrary",)))(dq_aug, dk_aug, dv, fa, bfo, _tri(tb, True))


def rope_tables(s, sign):
    half = ROPE_DIM // 2
    inv_freq = ROPE_THETA ** (-jnp.arange(half, dtype=F32) * 2.0 / ROPE_DIM)
    ang = jnp.arange(s, dtype=F32)[:, None] * inv_freq[None, :]
    l64 = np.arange(LANES) % HEAD_DIM
    cos = jnp.cos(ang)[:, l64 % half]
    sin = jnp.sin(ang)[:, l64 % half] * sign
    first = jnp.asarray(l64 < half)[None, :]
    second = jnp.asarray((l64 >= half) & (l64 < ROPE_DIM))[None, :]
    return (jnp.where(first | second, cos, 1.0), jnp.where(first, -sin, 0.0), jnp.where(second, sin, 0.0))


def rope_apply(items, tabs, *, tm=512, name):
    s = items[0][0].shape[0]
    n_i = len(items)

    def body(*refs):
        c_ref, sn_ref, sp_ref = refs[n_i:n_i + 3]
        o_ref = refs[-1]
        for j, (_, _, scale, rotate) in enumerate(items):
            for b in range(4):
                xv = refs[j][:, b * LANES:(b + 1) * LANES].astype(F32)
                if rotate:
                    xv = xv * c_ref[...] + pltpu.roll(xv, LANES - 8, axis=1) * sn_ref[...] + pltpu.roll(xv, 8, axis=1) * sp_ref[...]
                o_ref[:, j * ATT_W + b * LANES:j * ATT_W + (b + 1) * LANES] = (xv * scale).astype(o_ref.dtype)

    in_specs = [pl.BlockSpec((tm, ATT_W), functools.partial(lambda i, blk: (i, blk), blk=it[1])) for it in items]
    in_specs += [pl.BlockSpec((tm, LANES), lambda i: (i, 0))] * 3
    return pl.pallas_call(
        body, grid=(s // tm,), in_specs=in_specs, out_specs=pl.BlockSpec((tm, n_i * ATT_W), lambda i: (i, 0)),
        out_shape=SDS((s, n_i * ATT_W), BF16), name=name, compiler_params=_cp(("parallel",)))(*[it[0] for it in items], *tabs)


def _dil_views(qk, z, r):
    s = z.shape[0]
    return qk.reshape(s // r, r * 2 * ATT_W), z.reshape(s // r, r * Z_W)


def _dil_cols(r):
    q_col = lambda rho, hp: rho * 8 + hp
    k_col = lambda rho, hp: rho * 8 + 4 + hp
    v_col = lambda rho, hp: rho * (Z_W // LANES) + 4 * Z_VB + hp
    return q_col, k_col, v_col


def _dil_scores(qv, kp, kc, head, has_prev):
    b = DIL_BLK
    qm = jnp.where(head, qv, jnp.zeros_like(qv))
    row, col = _row((b, b)), _lane((b, b))
    sp = jnp.where((col >= row) & has_prev, _nt(qm, kp), NEG)
    sc = jnp.where(col <= row, _nt(qm, kc), NEG)
    return sp, sc


def dil_fwd(qk, z, prev, *, r):
    s = z.shape[0]
    b = DIL_BLK
    l_sub = s // r
    nb = l_sub // b
    qk_v, z_v = _dil_views(qk, z, r)
    q_col, k_col, v_col = _dil_cols(r)
    merge = prev is not None

    def body(*refs):
        if merge:
            q_ref, kp_ref, kc_ref, vp_ref, vc_ref, op_ref, lp_ref, o_ref, l_ref = refs
        else:
            q_ref, kp_ref, kc_ref, vp_ref, vc_ref, o_ref, l_ref = refs
        has_prev = pl.program_id(2) > 0
        lane = _lane((b, LANES))
        res = []
        for i in range(2):
            head = (lane < 64) if i == 0 else (lane >= 64)
            sp, sc = _dil_scores(q_ref[...], kp_ref[...], kc_ref[...], head, has_prev)
            m = jnp.maximum(jnp.max(sp, axis=-1, keepdims=True), jnp.max(sc, axis=-1, keepdims=True))
            pp = jnp.exp(sp - m)
            pc = jnp.exp(sc - m)
            den = jnp.sum(pp, axis=-1, keepdims=True) + jnp.sum(pc, axis=-1, keepdims=True)
            ov = (_nn(pp.astype(BF16), vp_ref[...]) + _nn(pc.astype(BF16), vc_ref[...])) / den
            res.append((ov, m + jnp.log(den)))
        ov = jnp.where(lane < 64, res[0][0], res[1][0])
        lse = jnp.where(lane < 64, res[0][1], res[1][1])
        if merge:
            lp = lp_ref[...]
            m2 = jnp.maximum(lp, lse)
            wp = jnp.exp(lp - m2)
            wn = jnp.exp(lse - m2)
            ov = (wp * op_ref[...] + wn * ov) / (wp + wn)
            lse = m2 + jnp.log(wp + wn)
        o_ref[...] = ov
        l_ref[...] = lse

    blk = lambda f: pl.BlockSpec((b, LANES), f)
    in_specs = [blk(lambda rho, hp, n: (n, q_col(rho, hp))), blk(lambda rho, hp, n: (jnp.maximum(n - 1, 0), k_col(rho, hp))),
                blk(lambda rho, hp, n: (n, k_col(rho, hp))), blk(lambda rho, hp, n: (jnp.maximum(n - 1, 0), v_col(rho, hp))),
                blk(lambda rho, hp, n: (n, v_col(rho, hp)))]
    args = [qk_v, qk_v, qk_v, z_v, z_v]
    nat = blk(lambda rho, hp, n: (n, rho * 4 + hp))
    if merge:
        in_specs += [nat, nat]
        args += [prev[0].reshape(l_sub, r * ATT_W), prev[1].reshape(l_sub, r * ATT_W)]
    o, lse = pl.pallas_call(
        body, grid=(r, 4, nb), in_specs=in_specs, out_specs=[nat, nat],
        out_shape=[SDS((l_sub, r * ATT_W), F32)] * 2, name=f"dil_fwd_r{r}",
        compiler_params=_cp(("parallel", "parallel", "arbitrary")))(*args)
    return o.reshape(s, ATT_W), lse.reshape(s, ATT_W)


def dil_bwd_dq(qk, z, dy, lse, dd, acc, *, r):
    s = z.shape[0]
    b = DIL_BLK
    l_sub = s // r
    nb = l_sub // b
    qk_v, z_v = _dil_views(qk, z, r)
    q_col, k_col, v_col = _dil_cols(r)
    add = acc is not None

    def body(*refs):
        q_ref, kp_ref, kc_ref, vp_ref, vc_ref, do_ref, l_ref, dd_ref = refs[:8]
        dq_ref = refs[-1]
        has_prev = pl.program_id(2) > 0
        lane = _lane((b, LANES))
        dov = do_ref[...]
        parts = []
        for i in range(2):
            head = (lane < 64) if i == 0 else (lane >= 64)
            sp, sc = _dil_scores(q_ref[...], kp_ref[...], kc_ref[...], head, has_prev)
            lse_i = l_ref[:, i * 64:i * 64 + 1]
            dd_i = dd_ref[:, i * 64:i * 64 + 1]
            dom = jnp.where(head, dov, jnp.zeros_like(dov))
            dsp = (jnp.exp(sp - lse_i) * (_nt(dom, vp_ref[...]) - dd_i)).astype(BF16)
            dsc = (jnp.exp(sc - lse_i) * (_nt(dom, vc_ref[...]) - dd_i)).astype(BF16)
            parts.append(_nn(dsp, kp_ref[...]) + _nn(dsc, kc_ref[...]))
        dq = jnp.where(lane < 64, parts[0], parts[1])
        if add:
            dq = dq + refs[8][...]
        dq_ref[...] = dq

    blk = lambda f: pl.BlockSpec((b, LANES), f)
    nat = blk(lambda rho, hp, n: (n, rho * 4 + hp))
    in_specs = [blk(lambda rho, hp, n: (n, q_col(rho, hp))), blk(lambda rho, hp, n: (jnp.maximum(n - 1, 0), k_col(rho, hp))),
                blk(lambda rho, hp, n: (n, k_col(rho, hp))), blk(lambda rho, hp, n: (jnp.maximum(n - 1, 0), v_col(rho, hp))),
                blk(lambda rho, hp, n: (n, v_col(rho, hp))), nat, nat, nat]
    nview = lambda a: a.reshape(l_sub, r * ATT_W)
    args = [qk_v, qk_v, qk_v, z_v, z_v, nview(dy), nview(lse), nview(dd)]
    if add:
        in_specs.append(nat)
        args.append(nview(acc))
    dq = pl.pallas_call(
        body, grid=(r, 4, nb), in_specs=in_specs, out_specs=nat, out_shape=SDS((l_sub, r * ATT_W), F32),
        name=f"dil_bwd_dq_r{r}", compiler_params=_cp(("parallel", "parallel", "arbitrary")))(*args)
    return dq.reshape(s, ATT_W)


def dil_bwd_dkv(qk, z, dy, lse, dd, acc, *, r):
    s = z.shape[0]
    b = DIL_BLK
    l_sub = s // r
    nb = l_sub // b
    qk_v, z_v = _dil_views(qk, z, r)
    q_col, k_col, v_col = _dil_cols(r)
    add = acc is not None

    def body(*refs):
        k_ref, v_ref, qc_ref, qn_ref, doc_ref, don_ref, lc_ref, ln_ref, ddc_ref, ddn_ref = refs[:10]
        dk_ref, dv_ref = refs[-2:]
        has_next = pl.program_id(2) < nb - 1
        lane = _lane((b, LANES))
        row, col = _row((b, b)), _lane((b, b))
        kv = k_ref[...]
        vv = v_ref[...]
        dk_parts, dv_parts = [], []
        for i in range(2):
            head = (lane < 64) if i == 0 else (lane >= 64)
            dk_i = jnp.zeros((b, LANES), F32)
            dv_i = jnp.zeros((b, LANES), F32)
            for q_ref, do_ref, l_ref, d_ref, mask in ((qc_ref, doc_ref, lc_ref, ddc_ref, col <= row),
                                                      (qn_ref, don_ref, ln_ref, ddn_ref, (col >= row) & has_next)):
                qv = q_ref[...]
                dov = do_ref[...]
                sc = jnp.where(mask, _nt(jnp.where(head, qv, jnp.zeros_like(qv)), kv), NEG)
                p = jnp.exp(sc - l_ref[:, i * 64:i * 64 + 1])
                dp = _nt(jnp.where(head, dov, jnp.zeros_like(dov)), vv)
                ds = (p * (dp - d_ref[:, i * 64:i * 64 + 1])).astype(BF16)
                dv_i = dv_i + _tn(p.astype(BF16), dov)
                dk_i = dk_i + _tn(ds, qv)
            dk_parts.append(dk_i)
            dv_parts.append(dv_i)
        dk = jnp.where(lane < 64, dk_parts[0], dk_parts[1])
        dv = jnp.where(lane < 64, dv_parts[0], dv_parts[1])
        if add:
            dk = dk + refs[10][...]
            dv = dv + refs[11][...]
        dk_ref[...] = dk
        dv_ref[...] = dv

    blk = lambda f: pl.BlockSpec((b, LANES), f)
    nat = blk(lambda rho, hp, n: (n, rho * 4 + hp))
    nxt = blk(lambda rho, hp, n: (jnp.minimum(n + 1, nb - 1), rho * 4 + hp))
    in_specs = [blk(lambda rho, hp, n: (n, k_col(rho, hp))), blk(lambda rho, hp, n: (n, v_col(rho, hp))),
                blk(lambda rho, hp, n: (n, q_col(rho, hp))), blk(lambda rho, hp, n: (jnp.minimum(n + 1, nb - 1), q_col(rho, hp))),
                nat, nxt, nat, nxt, nat, nxt]
    nview = lambda a: a.reshape(l_sub, r * ATT_W)
    args = [qk_v, z_v, qk_v, qk_v, nview(dy), nview(dy), nview(lse), nview(lse), nview(dd), nview(dd)]
    if add:
        in_specs += [nat, nat]
        args += [nview(acc[0]), nview(acc[1])]
    dk, dv = pl.pallas_call(
        body, grid=(r, 4, nb), in_specs=in_specs, out_specs=[nat, nat],
        out_shape=[SDS((l_sub, r * ATT_W), F32)] * 2, name=f"dil_bwd_dkv_r{r}",
        compiler_params=_cp(("parallel", "parallel", "arbitrary")))(*args)
    return dk.reshape(s, ATT_W), dv.reshape(s, ATT_W)


def _sigmoid(v):
    return 1.0 / (1.0 + jnp.exp(-v))


def gate_mix(ya, yb, wa, wb, z, *, tm=512, tn=512):
    s = ya.shape[0]
    d = wa.shape[1]
    ga_blk = 3 * ATT_W * 2 // tn
    gb_blk = ga_blk + d // tn

    def body(ya_ref, yb_ref, wa_ref, wb_ref, ga_ref, gb_ref, pa_ref, pb_ref, mx_ref):
        pa = _nn(ya_ref[...], wa_ref[...])
        pb = _nn(yb_ref[...].astype(BF16), wb_ref[...])
        pa_ref[...] = pa.astype(BF16)
        pb_ref[...] = pb.astype(BF16)
        mx_ref[...] = (_sigmoid(ga_ref[...].astype(F32)) * pa + _sigmoid(gb_ref[...].astype(F32)) * pb).astype(BF16)

    out = pl.BlockSpec((tm, tn), lambda i, j: (i, j))
    return pl.pallas_call(
        body, grid=(s // tm, d // tn),
        in_specs=[pl.BlockSpec((tm, ATT_W), lambda i, j: (i, 0)), pl.BlockSpec((tm, ATT_W), lambda i, j: (i, 0)),
                  pl.BlockSpec((ATT_W, tn), lambda i, j: (0, j)), pl.BlockSpec((ATT_W, tn), lambda i, j: (0, j)),
                  pl.BlockSpec((tm, tn), lambda i, j: (i, ga_blk + j)), pl.BlockSpec((tm, tn), lambda i, j: (i, gb_blk + j))],
        out_specs=[out, out, out], out_shape=[SDS((s, d), BF16)] * 3, name="gate_mix",
        compiler_params=_cp(("parallel", "parallel")))(ya, yb, wa, wb, z, z)


def gate_bwd(dmx, z, pa, pb, *, tm=256):
    s, d = dmx.shape

    def body(dm_ref, ga_ref, gb_ref, pa_ref, pb_ref, dpa_ref, dpb_ref, dg_ref):
        dm = dm_ref[...].astype(F32)
        sa = _sigmoid(ga_ref[...].astype(F32))
        sb = _sigmoid(gb_ref[...].astype(F32))
        dpa_ref[...] = (dm * sa).astype(BF16)
        dpb_ref[...] = (dm * sb).astype(BF16)
        dg_ref[:, 0:d] = (dm * pa_ref[...].astype(F32) * sa * (1.0 - sa)).astype(BF16)
        dg_ref[:, d:2 * d] = (dm * pb_ref[...].astype(F32) * sb * (1.0 - sb)).astype(BF16)

    row = pl.BlockSpec((tm, d), lambda i: (i, 0))
    return pl.pallas_call(
        body, grid=(s // tm,),
        in_specs=[row, pl.BlockSpec((tm, d), lambda i: (i, 3)), pl.BlockSpec((tm, d), lambda i: (i, 4)), row, row],
        out_specs=[row, row, pl.BlockSpec((tm, 2 * d), lambda i: (i, 0))],
        out_shape=[SDS((s, d), BF16), SDS((s, d), BF16), SDS((s, 2 * d), BF16)], name="gate_bwd",
        compiler_params=_cp(("parallel",)))(dmx, z, z, pa, pb)


GELU_C = math.sqrt(2.0 / math.pi)


def _gelu_parts(a):
    inner = GELU_C * (a + 0.044715 * a * a * a)
    th = jnp.tanh(inner)
    gelu = 0.5 * a * (1.0 + th)
    dgelu = 0.5 * (1.0 + th) + 0.5 * a * (1.0 - th * th) * GELU_C * (1.0 + 3.0 * 0.044715 * a * a)
    return gelu, dgelu


def _causal_taps(u, before):
    row = _row(u.shape)
    r1 = jnp.where(row == 0, before[7:8, :], pltpu.roll(u, 1, axis=0))
    r2 = jnp.where(row == 0, before[6:7, :], jnp.where(row == 1, before[7:8, :], pltpu.roll(u, 2, axis=0)))
    return r1, r2


def ffn_up(h, wa, wb, cw, cb, *, tm=512, tn=256):
    s, d = h.shape
    f = wa.shape[1]
    nj = f // tn

    def body(h_ref, wa_ref, wb_ref, cwa_ref, cwb_ref, cba_ref, cbb_ref, ua_ref, ub_ref, m_ref, carry):
        @pl.when(pl.program_id(1) == 0)
        def _():
            carry[...] = jnp.zeros_like(carry)

        conv = []
        for k, (w_ref, cw_ref, cb_ref, u_ref) in enumerate(((wa_ref, cwa_ref, cba_ref, ua_ref), (wb_ref, cwb_ref, cbb_ref, ub_ref))):
            u16 = _nn(h_ref[...], w_ref[...]).astype(BF16)
            u_ref[...] = u16
            u = u16.astype(F32)
            r1, r2 = _causal_taps(u, carry[k])
            carry[k] = u[tm - 8:tm, :]
            conv.append(cw_ref[0:1, :] * r2 + cw_ref[1:2, :] * r1 + cw_ref[2:3, :] * u + cb_ref[...])
        m_ref[...] = (_gelu_parts(conv[0])[0] * conv[1]).astype(BF16)

    out = pl.BlockSpec((tm, tn), lambda j, i: (i, j))
    return pl.pallas_call(
        body, grid=(nj, s // tm),
        in_specs=[pl.BlockSpec((tm, d), lambda j, i: (i, 0)),
                  pl.BlockSpec((d, tn), lambda j, i: (0, j)), pl.BlockSpec((d, tn), lambda j, i: (0, j)),
                  pl.BlockSpec((3, tn), lambda j, i: (0, j)), pl.BlockSpec((3, tn), lambda j, i: (0, nj + j)),
                  pl.BlockSpec((1, tn), lambda j, i: (0, j)), pl.BlockSpec((1, tn), lambda j, i: (0, nj + j))],
        out_specs=[out, out, out], out_shape=[SDS((s, f), BF16)] * 3,
        scratch_shapes=[pltpu.VMEM((2, 8, tn), F32)], name="ffn_up",
        compiler_params=_cp(("parallel", "arbitrary")))(h, wa, wb, cw, cw, cb, cb)


def ffn_bwd(dm, ua, ub, cw, cb, *, tm=512, tn=256):
    s, f = dm.shape
    nj = f // tn
    ni = s // tm
    halo = 16

    def body(dm_ref, ua_ref, ub_ref, ha_ref, hb_ref, cwa_ref, cwb_ref, cba_ref, cbb_ref,
             dua_ref, dub_ref, ga_ref, gb_ref, carry):
        i = pl.program_id(1)

        @pl.when(i == 0)
        def _():
            carry[...] = jnp.zeros_like(carry)
            ga_ref[...] = jnp.zeros_like(ga_ref)
            gb_ref[...] = jnp.zeros_like(gb_ref)

        first_tile = i == ni - 1
        row = _row((tm, tn))
        dmv = dm_ref[...].astype(F32)
        us, taps, convs = [], [], []
        for u_ref, h_ref, cw_ref, cb_ref in ((ua_ref, ha_ref, cwa_ref, cba_ref), (ub_ref, hb_ref, cwb_ref, cbb_ref)):
            u = u_ref[...].astype(F32)
            before = jnp.where(first_tile, 0.0, h_ref[halo - 8:halo, :].astype(F32))
            r1, r2 = _causal_taps(u, before)
            us.append(u)
            taps.append((r1, r2))
            convs.append(cw_ref[0:1, :] * r2 + cw_ref[1:2, :] * r1 + cw_ref[2:3, :] * u + cb_ref[...])
        gelu, dgelu = _gelu_parts(convs[0])
        dcs = (dmv * convs[1] * dgelu, dmv * gelu)
        for k, (dc, cw_ref, du_ref, g_ref) in enumerate(((dcs[0], cwa_ref, dua_ref, ga_ref), (dcs[1], cwb_ref, dub_ref, gb_ref))):
            r1, r2 = taps[k]
            g_ref[0:1, :] += jnp.sum(dc * r2, axis=0, keepdims=True)
            g_ref[1:2, :] += jnp.sum(dc * r1, axis=0, keepdims=True)
            g_ref[2:3, :] += jnp.sum(dc * us[k], axis=0, keepdims=True)
            g_ref[3:4, :] += jnp.sum(dc, axis=0, keepdims=True)
            after = carry[k]
            n1 = jnp.where(row == tm - 1, after[0:1, :], pltpu.roll(dc, tm - 1, axis=0))
            n2 = jnp.where(row == tm - 2, after[0:1, :], jnp.where(row == tm - 1, after[1:2, :], pltpu.roll(dc, tm - 2, axis=0)))
            du_ref[...] = (cw_ref[2:3, :] * dc + cw_ref[1:2, :] * n1 + cw_ref[0:1, :] * n2).astype(BF16)
            carry[k] = dc[0:8, :]

    tile = pl.BlockSpec((tm, tn), lambda j, i: (ni - 1 - i, j))
    halo_spec = pl.BlockSpec((halo, tn), lambda j, i: (jnp.maximum((ni - 1 - i) * (tm // halo) - 1, 0), j))
    gspec = pl.BlockSpec((8, tn), lambda j, i: (0, j))
    return pl.pallas_call(
        body, grid=(nj, ni),
        in_specs=[tile, tile, tile, halo_spec, halo_spec,
                  pl.BlockSpec((3, tn), lambda j, i: (0, j)), pl.BlockSpec((3, tn), lambda j, i: (0, nj + j)),
                  pl.BlockSpec((1, tn), lambda j, i: (0, j)), pl.BlockSpec((1, tn), lambda j, i: (0, nj + j))],
        out_specs=[tile, tile, gspec, gspec],
        out_shape=[SDS((s, f), BF16), SDS((s, f), BF16), SDS((8, f), F32), SDS((8, f), F32)],
        scratch_shapes=[pltpu.VMEM((2, 8, tn), F32)], name="ffn_bwd",
        compiler_params=_cp(("parallel", "arbitrary")))(dm, ua, ub, ua, ub, cw, cw, cb, cb)


def adamw(w, g, m, v, *, name):
    r, c = w.shape
    tr = r
    for cand in (256, 128, 64, 32, 16, 8):
        if r % cand == 0:
            tr = cand
            break

    def body(w_ref, g_ref, m_ref, v_ref, d_ref, nm_ref, nv_ref):
        gv = g_ref[...]
        mn = ADAM_B1 * m_ref[...] + (1.0 - ADAM_B1) * gv
        vn = ADAM_B2 * v_ref[...] + (1.0 - ADAM_B2) * (gv * gv)
        m_hat = mn / (1.0 - ADAM_B1 ** ADAM_STEP)
        v_hat = vn / (1.0 - ADAM_B2 ** ADAM_STEP)
        d_ref[...] = -ADAM_LR * (m_hat / (jnp.sqrt(v_hat) + ADAM_EPS) + ADAM_WD * w_ref[...])
        nm_ref[...] = mn
        nv_ref[...] = vn

    blk = pl.BlockSpec((tr, c), lambda i: (i, 0))
    return pl.pallas_call(body, grid=(r // tr,), in_specs=[blk] * 4, out_specs=[blk] * 3, out_shape=[SDS((r, c), F32)] * 3,
                          name=name, compiler_params=_cp(("parallel",)))(w, g, m, v)


ANY = pl.BlockSpec(memory_space=pl.ANY)
ICI_KINDS = ("x", "y", "xy")


def _coords():
    return lax.axis_index("x"), lax.axis_index("y"), lax.axis_index("c")


def _peer(kind, x, y, c):
    if kind == "c":
        return (x, y, 1 - c)
    if kind == "x":
        return (1 - x, y, c)
    if kind == "y":
        return (x, 1 - y, c)
    return (1 - x, 1 - y, c)


def _chip_of(p):
    return 2 * p[0] + p[1]


def _half(rows, which):
    h = rows // 2
    return pl.ds(pl.multiple_of(which * h, 16), h)


def _remote(src, dst, send_sem, recv_sem, to):
    return pltpu.make_async_remote_copy(src_ref=src, dst_ref=dst, send_sem=send_sem, recv_sem=recv_sem,
                                        device_id=to, device_id_type=MESH)


def allgather_chips(shards, halved, *, name):
    n = len(shards)

    def body(*refs):
        ins, outs = refs[:n], refs[n:2 * n]
        send_sems, recv_sems, loc_sems = refs[2 * n:]
        x, y, c = _coords()
        me = (x, y, c)
        my_chip = 2 * x + y

        def rows(w, which):
            r = shards[w].shape[0]
            return _half(r, which) if halved[w] else pl.ds(0, r)

        local = [pltpu.make_async_copy(ins[w], outs[w].at[my_chip], loc_sems.at[w]) for w in range(n)]
        for cp in local:
            cp.start()
        first = []
        for w in range(n):
            for k, kind in enumerate(ICI_KINDS):
                cp = _remote(ins[w].at[rows(w, c)], outs[w].at[my_chip, rows(w, c)], send_sems.at[w, k], recv_sems.at[w, k],
                             _peer(kind, x, y, c))
                cp.start()
                first.append(cp)
        second = []
        for w in range(n):
            for k, kind in enumerate(ICI_KINDS):
                landed = outs[w].at[_chip_of(_peer(kind, x, y, c)), rows(w, c)]
                _remote(landed, landed, send_sems.at[w, k], recv_sems.at[w, k], me).wait_recv()
                if halved[w]:
                    cp = _remote(landed, landed, send_sems.at[w, 3 + k], recv_sems.at[w, 3 + k], _peer("c", x, y, c))
                    cp.start()
                    second.append(cp)
        for w in range(n):
            if halved[w]:
                for k, kind in enumerate(ICI_KINDS):
                    other = outs[w].at[_chip_of(_peer(kind, x, y, c)), rows(w, 1 - c)]
                    _remote(other, other, send_sems.at[w, 3 + k], recv_sems.at[w, 3 + k], me).wait_recv()
        for cp in first + second:
            cp.wait_send()
        for cp in local:
            cp.wait()

    return pl.pallas_call(
        body, in_specs=[ANY] * n, out_specs=[ANY] * n,
        out_shape=[SDS((4,) + a.shape, a.dtype) for a in shards],
        scratch_shapes=[pltpu.SemaphoreType.DMA((n, 6)), pltpu.SemaphoreType.DMA((n, 6)), pltpu.SemaphoreType.DMA((n,))],
        name=name)(*shards)


def grads_to_sibling(gs, *, name):
    n = len(gs)

    def body(*refs):
        ins, outs = refs[:n], refs[n:2 * n]
        send_sems, recv_sems = refs[2 * n:]
        x, y, c = _coords()
        cps = []
        for w in range(n):
            cp = _remote(ins[w].at[:, _half(gs[w].shape[1], 1 - c)], outs[w], send_sems.at[w], recv_sems.at[w],
                         _peer("c", x, y, c))
            cp.start()
            cps.append(cp)
        for cp in cps:
            cp.wait()

    return pl.pallas_call(
        body, in_specs=[ANY] * n, out_specs=[ANY] * n,
        out_shape=[SDS((4, a.shape[1] // 2, a.shape[2]), a.dtype) for a in gs],
        scratch_shapes=[pltpu.SemaphoreType.DMA((n,)), pltpu.SemaphoreType.DMA((n,))], name=name)(*gs)


def grads_to_chips(ps, *, name):
    n = len(ps)

    def body(*refs):
        ins, outs = refs[:n], refs[n:2 * n]
        send_sems, recv_sems = refs[2 * n:]
        x, y, c = _coords()
        cps = []
        for w in range(n):
            for k, kind in enumerate(ICI_KINDS):
                to = _peer(kind, x, y, c)
                cp = _remote(ins[w].at[_chip_of(to)], outs[w].at[k], send_sems.at[w, k], recv_sems.at[w, k], to)
                cp.start()
                cps.append(cp)
        for cp in cps:
            cp.wait()

    return pl.pallas_call(
        body, in_specs=[ANY] * n, out_specs=[ANY] * n,
        out_shape=[SDS((3,) + a.shape[1:], a.dtype) for a in ps],
        scratch_shapes=[pltpu.SemaphoreType.DMA((n, 3)), pltpu.SemaphoreType.DMA((n, 3))], name=name)(*ps)


def halves_to_full(hs, *, name):
    n = len(hs)

    def body(*refs):
        ins, outs = refs[:n], refs[n:2 * n]
        send_sems, recv_sems, loc_sems = refs[2 * n:]
        x, y, c = _coords()
        cps = []
        for w in range(n):
            mine = outs[w].at[_half(2 * hs[w].shape[0], c)]
            lc = pltpu.make_async_copy(ins[w], mine, loc_sems.at[w])
            lc.start()
            cp = _remote(ins[w], mine, send_sems.at[w], recv_sems.at[w], _peer("c", x, y, c))
            cp.start()
            cps += [lc, cp]
        for cp in cps:
            cp.wait()

    return pl.pallas_call(
        body, in_specs=[ANY] * n, out_specs=[ANY] * n,
        out_shape=[SDS((2 * a.shape[0], a.shape[1]), a.dtype) for a in hs],
        scratch_shapes=[pltpu.SemaphoreType.DMA((n,)), pltpu.SemaphoreType.DMA((n,)), pltpu.SemaphoreType.DMA((n,))],
        name=name)(*hs)


def _row_tile(rows):
    for cand in (256, 192, 176, 128, 64, 32, 16):
        if rows % cand == 0:
            return cand
    return rows


def chip_sum(g, recv, c_arr, *, name):
    _, r, cols = g.shape
    h = r // 2
    tr = _row_tile(h)
    nblk = h // tr

    def body(c_ref, g_ref, r_ref, f_ref, b_ref):
        tot = g_ref[...] + r_ref[...]
        f_ref[...] = tot
        b_ref[...] = tot.astype(BF16)

    blk = pl.BlockSpec((None, tr, cols), lambda j, i, c_ref: (j, i, 0))
    grid_spec = pltpu.PrefetchScalarGridSpec(
        num_scalar_prefetch=1, grid=(4, nblk),
        in_specs=[pl.BlockSpec((None, tr, cols), lambda j, i, c_ref: (j, c_ref[0] * nblk + i, 0)), blk],
        out_specs=[blk, blk])
    return pl.pallas_call(body, grid_spec=grid_spec, out_shape=[SDS((4, h, cols), F32), SDS((4, h, cols), BF16)],
                          name=name, compiler_params=_cp(("parallel", "parallel")))(c_arr, g, recv)


def final_sum(pf, recv, chip_arr, *, name):
    _, h, cols = pf.shape
    tr = _row_tile(h)

    def body(chip_ref, p_ref, r_ref, o_ref):
        o_ref[...] = ((p_ref[...] + r_ref[0].astype(F32)) + r_ref[1].astype(F32)) + r_ref[2].astype(F32)

    grid_spec = pltpu.PrefetchScalarGridSpec(
        num_scalar_prefetch=1, grid=(h // tr,),
        in_specs=[pl.BlockSpec((None, tr, cols), lambda i, chip_ref: (chip_ref[0], i, 0)),
                  pl.BlockSpec((3, tr, cols), lambda i, chip_ref: (0, i, 0))],
        out_specs=pl.BlockSpec((tr, cols), lambda i, chip_ref: (i, 0)))
    return pl.pallas_call(body, grid_spec=grid_spec, out_shape=SDS((h, cols), F32), name=name,
                          compiler_params=_cp(("parallel",)))(chip_arr, pf, recv)


def allreduce_small(v, *, name):
    rws, cols = v.shape

    def body(v_ref, all_ref, sum_ref, send_sems, recv_sems, local_sem):
        x, y, c = _coords()
        me, sibling = (x, y, c), (x, y, 1 - c)
        chips = [(1 - x, y), (x, 1 - y), (1 - x, 1 - y)]

        def rows(px, py, pc):
            return all_ref.at[pl.ds(pl.multiple_of((4 * px + 2 * py + pc) * rws, 8), rws), :]

        def copy(k, block, to, src=None):
            return _remote(rows(*block) if src is None else src, rows(*block), send_sems.at[k], recv_sems.at[k], to)

        mine = pltpu.make_async_copy(v_ref, rows(*me), local_sem)
        mine.start()
        first = [copy(0, me, sibling, src=v_ref)]
        first += [copy(1 + j, me, (*chip, c), src=v_ref) for j, chip in enumerate(chips)]
        for cp in first:
            cp.start()
        passed = [copy(4 + j, (*chip, c), sibling) for j, chip in enumerate(chips)]
        for j, chip in enumerate(chips):
            copy(1 + j, (*chip, c), me).wait_recv()
            passed[j].start()
        copy(0, sibling, me).wait_recv()
        for j, chip in enumerate(chips):
            copy(4 + j, (*chip, 1 - c), me).wait_recv()
        for cp in first + passed:
            cp.wait_send()
        mine.wait()
        tot = all_ref[0:rws, :]
        for dev in range(1, 8):
            tot = tot + all_ref[dev * rws:(dev + 1) * rws, :]
        sum_ref[...] = tot

    vm = pl.BlockSpec(memory_space=pltpu.VMEM)
    return pl.pallas_call(
        body, in_specs=[vm], out_specs=[vm, vm],
        out_shape=[SDS((8 * rws, cols), v.dtype), SDS((rws, cols), v.dtype)],
        scratch_shapes=[pltpu.SemaphoreType.DMA((7,)), pltpu.SemaphoreType.DMA((7,)), pltpu.SemaphoreType.DMA],
        name=name)(v)[1]


def _pack_rows(parts, rows):
    out = []
    for a, r in zip(parts, rows):
        flat = a.reshape(-1)
        flat = jnp.pad(flat, (0, r * LANES - flat.shape[0]))
        out.append(flat.reshape(r, LANES))
    return jnp.concatenate(out, axis=0)


def _unpack_rows(packed, shapes, rows):
    out, at = [], 0
    for shp, r in zip(shapes, rows):
        size = int(np.prod(shp))
        out.append(packed[at:at + r].reshape(-1)[:size].reshape(shp))
        at += r
    return out


def kernel(x, g_pre_mix, w_in, b_forget, w_o_fox, w_o_dil, w_out, g_post_mix, g_pre_ffn, w_up, conv_w, conv_b, w_down, g_post_ffn, loss_target, m_g_pre_mix, m_w_in, m_b_forget, m_w_o_fox, m_w_o_dil, m_w_out, m_g_post_mix, m_g_pre_ffn, m_w_up, m_conv_w, m_conv_b, m_w_down, m_g_post_ffn, v_g_pre_mix, v_w_in, v_b_forget, v_w_o_fox, v_w_o_dil, v_w_out, v_g_post_mix, v_g_pre_ffn, v_w_up, v_conv_w, v_conv_b, v_w_down, v_g_post_ffn):
    xi, yi, ci = _coords()
    chip = 2 * xi + yi
    c_arr = jnp.reshape(ci, (1,)).astype(jnp.int32)
    chip_arr = jnp.reshape(chip, (1,)).astype(jnp.int32)
    xs = x[0]
    target = loss_target[0]
    s, d = xs.shape
    f_half = w_down.shape[1] * 4
    cols_in = w_in.shape[2]

    big = (w_in, w_o_fox, w_o_dil, w_out, w_up, w_down)
    shards = [w[0].astype(BF16) for w in big] + [conv_w[0]]
    a_in, a_of, a_od, a_out, a_up, a_down, a_cw = allgather_chips(shards, [True] * 6 + [False], name="allgather_weights")
    w_in_full = jnp.concatenate([a_in[j] for j in range(4)], axis=1)
    nf = N_HEADS
    e_a, e_b = 3 * ATT_W, 3 * ATT_W + nf
    wz = jnp.concatenate([w_in_full[:, :e_a], w_in_full[:, e_b:]], axis=1)
    wf = jnp.pad(w_in_full[:, e_a:e_b], ((0, 0), (0, LANES - nf)))
    wo_a = jnp.concatenate([a_of[j] for j in range(4)], axis=1)
    wo_b = jnp.concatenate([a_od[j] for j in range(4)], axis=1)
    w_o = a_out.reshape(d, d)
    w_dn = a_down.reshape(f_half, d)
    wu_a = jnp.concatenate([a_up[0], a_up[1]], axis=1)
    wu_b = jnp.concatenate([a_up[2], a_up[3]], axis=1)
    cw = jnp.concatenate([a_cw[j] for j in range(4)], axis=1)
    cb = conv_b
    bfo = jnp.pad(b_forget, ((0, 0), (0, LANES - nf)))

    h1 = rmsnorm_fwd(xs, g_pre_mix)
    z = mm([(h1, d, 0)], [(wz, d, 0)], nt=False, out_dtype=BF16, tm=1024, tn=512, name="in_proj")
    fa = mm([(h1, d, 0)], [(wf, d, 0)], nt=False, out_dtype=F32, tm=1024, tn=LANES, name="in_proj_forget")
    q_aug, k_aug = fox_prep(z, fa, bfo)
    ya, lse_a = fox_fwd(q_aug, k_aug, z)
    qk = rope_apply([(z, Z_QB, QK_SCALE, True), (z, Z_KB, 1.0, True)], rope_tables(s, 1.0), name="rope_fwd")
    run = None
    for _, r in DIL_PATTERNS:
        run = dil_fwd(qk, z, run, r=r)
    yb, lse_b = run
    pa, pb, mixed = gate_mix(ya, yb, wo_a, wo_b, z)
    y1, x1 = mm_rms_res(mixed, w_o, g_post_mix, xs, name="out_proj")
    h2 = rmsnorm_fwd(x1, g_pre_ffn)
    ua, ub, mid = ffn_up(h2, wu_a, wu_b, cw, cb)
    y2, dout, sq = mm_rms_res(mid, w_dn, g_post_ffn, x1, target, name="down_proj")
    loss = lax.psum(0.5 * sq[0, 0] / d, ("x", "y", "c"))

    dy2, gg_post_ffn = rmsnorm_bwd(dout, y2, g_post_ffn, None, out_dtype=BF16, name="norm_bwd_post_ffn")
    dmid = mm([(dy2, d, 0)], [(w_dn, d, 0)], nt=True, out_dtype=BF16, tm=512, tn=256, name="down_dgrad")
    dw_down = wgrad((mid, f_half, 0), dy2, tk=f_half // 2, tn=512, ts=512, name="down_wgrad")
    dua, dub, gc_a, gc_b = ffn_bwd(dmid, ua, ub, cw, cb)
    dh2 = mm([(dua, f_half, 0), (dub, f_half, 0)], [(wu_a, f_half, 0), (wu_b, f_half, 0)], nt=True, out_dtype=BF16,
             tm=512, tn=512, name="up_dgrad")
    dw_up = jnp.concatenate(
        [wgrad((h2, d, 0), du, tk=512, tn=f_half // 2, ts=512, name=f"up_wgrad_{k}", chip_major=True)
         for k, du in enumerate((dua, dub))], axis=0)
    dx1, gg_pre_ffn = rmsnorm_bwd(dh2, x1, g_pre_ffn, dout, out_dtype=F32, name="norm_bwd_pre_ffn")
    dy1, gg_post_mix = rmsnorm_bwd(dx1, y1, g_post_mix, None, out_dtype=BF16, name="norm_bwd_post_mix")
    dmixed = mm([(dy1, d, 0)], [(w_o, d, 0)], nt=True, out_dtype=BF16, tm=512, tn=512, name="out_dgrad")
    dw_out = wgrad((mixed, d, 0), dy1, tk=512, tn=512, ts=512, name="out_wgrad")
    dpa, dpb, dz_g = gate_bwd(dmixed, z, pa, pb)
    dya = mm([(dpa, d, 0)], [(wo_a, d, 0)], nt=True, out_dtype=BF16, tm=512, tn=ATT_W, name="fox_o_dgrad")
    dyb = mm([(dpb, d, 0)], [(wo_b, d, 0)], nt=True, out_dtype=BF16, tm=512, tn=ATT_W, name="dil_o_dgrad")
    dw_of = wgrad((ya, ATT_W, 0), dpa, tk=ATT_W, tn=d // 4, ts=512, name="fox_o_wgrad", chip_major=True)
    dw_od = wgrad((yb, ATT_W, 0), dpb, tk=ATT_W, tn=d // 4, ts=512, name="dil_o_wgrad", chip_major=True)
    dd_a = head_rowsum(dya, ya, name="fox_delta")
    dq_aug, dk_aug, dv_a = fox_bwd(q_aug, k_aug, z, dya, lse_a, dd_a)
    dz_a, dfa, gg_bf = fox_post(dq_aug, dk_aug, dv_a, fa, bfo)
    dd_b = head_rowsum(dyb, yb, name="dil_delta")
    dq_b, dkv_b = None, None
    for _, r in DIL_PATTERNS:
        dq_b = dil_bwd_dq(qk, z, dyb, lse_b, dd_b, dq_b, r=r)
        dkv_b = dil_bwd_dkv(qk, z, dyb, lse_b, dd_b, dkv_b, r=r)
    dz_b = rope_apply([(dq_b, 0, QK_SCALE, True), (dkv_b[0], 0, 1.0, True), (dkv_b[1], 0, 1.0, False)],
                      rope_tables(s, -1.0), name="rope_bwd")
    dh1 = mm([(dz_a, e_a, 0), (dz_b, e_a, 0), (dz_g, d, 0), (dz_g, d, 1), (dfa, LANES, 0)],
             [(wz, e_a, 0), (wz, e_a, 1), (wz, d, 3), (wz, d, 4), (wf, LANES, 0)], nt=True, out_dtype=BF16,
             tm=512, tn=512, name="in_dgrad")
    dw_a = wgrad((h1, d, 0), dz_a, tk=512, tn=512, ts=512, name="in_wgrad_a")
    dw_b = wgrad((h1, d, 0), dz_b, tk=512, tn=512, ts=512, name="in_wgrad_b")
    dw_g = wgrad((h1, d, 0), dz_g, tk=512, tn=512, ts=512, name="in_wgrad_g")
    dw_f = wgrad((h1, d, 0), dfa, tk=512, tn=LANES, ts=512, name="in_wgrad_f")
    grad_x, gg_pre_mix = rmsnorm_bwd(dh1, xs, g_pre_mix, dx1, out_dtype=F32, name="norm_bwd_pre_mix")
    dw_in_full = jnp.concatenate([dw_a, dw_f[:, :nf], dw_b, dw_g], axis=1)
    dw_in = jnp.stack([dw_in_full[:, j * cols_in:(j + 1) * cols_in] for j in range(4)], axis=0)

    gs = [dw_in, dw_of, dw_od, dw_out.reshape(4, d // 4, d), dw_up, dw_down.reshape(4, f_half // 4, d)]
    names = ("w_in", "w_o_fox", "w_o_dil", "w_out", "w_up", "w_down")
    from_sib = grads_to_sibling(gs, name="grads_to_sibling")
    sums = [chip_sum(g, r, c_arr, name=f"chip_sum_{nm}") for g, r, nm in zip(gs, from_sib, names)]
    from_chips = grads_to_chips([p[1] for p in sums], name="grads_to_chips")
    halves = [final_sum(p[0], r, chip_arr, name=f"final_sum_{nm}") for p, r, nm in zip(sums, from_chips, names)]
    g_big = halves_to_full(halves, name="halves_to_full")
    upd_big = [adamw(w[0], g, m[0], v[0], name=f"adamw_{nm}") for w, g, m, v, nm in zip(
        big, g_big, (m_w_in, m_w_o_fox, m_w_o_dil, m_w_out, m_w_up, m_w_down),
        (v_w_in, v_w_o_fox, v_w_o_dil, v_w_out, v_w_up, v_w_down), names)]

    g_cw_loc = jnp.concatenate([gc_a[0:3], gc_b[0:3]], axis=1)
    g_cb_loc = jnp.concatenate([gc_a[3:4], gc_b[3:4]], axis=1)
    small_loc = [gg_pre_mix, gg_post_mix, gg_pre_ffn, gg_post_ffn, g_cb_loc, gg_bf[:, :nf], g_cw_loc]
    red_rows = (8, 8, 8, 8, 48, 8, 136)
    red = allreduce_small(_pack_rows(small_loc, red_rows), name="allreduce_small")
    g_pm, g_qm, g_pf, g_qf, g_cb, g_bf, g_cw_full = _unpack_rows(red, [a.shape for a in small_loc], red_rows)
    cols_cw = conv_w.shape[2]
    g_cw = lax.dynamic_slice_in_dim(g_cw_full, chip * cols_cw, cols_cw, axis=1)
    small_w = (g_pre_mix, g_post_mix, g_pre_ffn, g_post_ffn, conv_b, b_forget, conv_w[0])
    small_m = (m_g_pre_mix, m_g_post_mix, m_g_pre_ffn, m_g_post_ffn, m_conv_b, m_b_forget, m_conv_w[0])
    small_v = (v_g_pre_mix, v_g_post_mix, v_g_pre_ffn, v_g_post_ffn, v_conv_b, v_b_forget, v_conv_w[0])
    small_g = (g_pm, g_qm, g_pf, g_qf, g_cb, g_bf, g_cw)
    ad_rows = (8, 8, 8, 8, 48, 8, 40)
    packed = [_pack_rows(t, ad_rows) for t in (small_w, small_g, small_m, small_v)]
    upd_small = [_unpack_rows(o, [a.shape for a in small_w], ad_rows) for o in adamw(*packed, name="adamw_small")]

    order = ("g_pre_mix", "w_in", "b_forget", "w_o_fox", "w_o_dil", "w_out", "g_post_mix", "g_pre_ffn", "w_up", "conv_w",
             "conv_b", "w_down", "g_post_ffn")
    small_names = ("g_pre_mix", "g_post_mix", "g_pre_ffn", "g_post_ffn", "conv_b", "b_forget", "conv_w")
    grads, deltas, new_ms, new_vs = {}, {}, {}, {}
    for k, nm in enumerate(names):
        grads[nm] = g_big[k][None]
        deltas[nm], new_ms[nm], new_vs[nm] = (a[None] for a in upd_big[k])
    for k, nm in enumerate(small_names):
        lead = (lambda a: a[None]) if nm == "conv_w" else (lambda a: a)
        grads[nm] = lead(small_g[k])
        deltas[nm], new_ms[nm], new_vs[nm] = (lead(upd_small[j][k]) for j in range(3))
    return (loss, grad_x[None], *[grads[nm] for nm in order], *[deltas[nm] for nm in order],
            *[new_ms[nm] for nm in order], *[new_vs[nm] for nm in order])
```

```python
import functools
import math

import numpy as np
import jax
import jax.numpy as jnp
from jax import lax
from jax.experimental import pallas as pl
from jax.experimental.pallas import tpu as pltpu

F32 = jnp.float32
BF16 = jnp.bfloat16
SDS = jax.ShapeDtypeStruct
MESH = pl.DeviceIdType.MESH

HEAD_DIM = 64
N_HEADS = 8
LANES = 128
ATT_W = N_HEADS * HEAD_DIM
DIL_PATTERNS = ((128, 1), (512, 4), (2048, 16))
DIL_BLK = 128
ROPE_DIM = HEAD_DIM // 4
ROPE_THETA = 500000.0
RMS_EPS = 1e-6
NEG = -1e30
QK_SCALE = 1.0 / math.sqrt(HEAD_DIM)
ADAM_LR, ADAM_B1, ADAM_B2, ADAM_EPS, ADAM_WD, ADAM_STEP = 0.001, 0.9, 0.999, 1e-08, 0.01, 10
VMEM_LIMIT = 56 * 1024 * 1024

Z_QA, Z_KA, Z_VA, Z_QB, Z_KB, Z_VB = 0, 1, 2, 3, 4, 5
Z_W = 5120


def _cp(sem):
    return pltpu.CompilerParams(dimension_semantics=sem, vmem_limit_bytes=VMEM_LIMIT)


def _nt(a, b):
    return lax.dot_general(a, b, (((1,), (1,)), ((), ())), preferred_element_type=F32)


def _tn(a, b):
    return lax.dot_general(a, b, (((0,), (0,)), ((), ())), preferred_element_type=F32)


def _nn(a, b):
    return jnp.dot(a, b, preferred_element_type=F32)


def _lane(shape):
    return lax.broadcasted_iota(jnp.int32, shape, 1)


def _row(shape):
    return lax.broadcasted_iota(jnp.int32, shape, 0)


def rmsnorm_fwd(x, g, *, tm=512):
    s, d = x.shape

    def body(x_ref, g_ref, h_ref):
        xv = x_ref[...]
        inv = lax.rsqrt(jnp.mean(xv * xv, axis=-1, keepdims=True) + RMS_EPS)
        h_ref[...] = (xv * inv * g_ref[...]).astype(h_ref.dtype)

    return pl.pallas_call(
        body, grid=(s // tm,),
        in_specs=[pl.BlockSpec((tm, d), lambda i: (i, 0)), pl.BlockSpec((1, d), lambda i: (0, 0))],
        out_specs=pl.BlockSpec((tm, d), lambda i: (i, 0)),
        out_shape=SDS((s, d), BF16), name="rmsnorm_fwd", compiler_params=_cp(("parallel",)))(x, g)


def rmsnorm_bwd(dh, x, g, res, *, out_dtype, tm=256, name):
    s, d = x.shape
    n = s // tm
    has_res = res is not None

    def body(*refs):
        if has_res:
            dh_ref, x_ref, g_ref, res_ref, dx_ref, dg_ref, acc = refs
        else:
            dh_ref, x_ref, g_ref, dx_ref, dg_ref, acc = refs
        i = pl.program_id(0)

        @pl.when(i == 0)
        def _():
            acc[...] = jnp.zeros_like(acc)

        xv = x_ref[...]
        inv = lax.rsqrt(jnp.mean(xv * xv, axis=-1, keepdims=True) + RMS_EPS)
        xh = xv * inv
        dhv = dh_ref[...].astype(F32)
        dxh = dhv * g_ref[...]
        dot = jnp.mean(dxh * xh, axis=-1, keepdims=True)
        dx = inv * (dxh - xh * dot)
        if has_res:
            dx = dx + res_ref[...]
        dx_ref[...] = dx.astype(dx_ref.dtype)
        acc[...] += jnp.sum((dhv * xh).reshape(tm // 8, 8, d), axis=0)

        @pl.when(i == n - 1)
        def _():
            dg_ref[...] = jnp.sum(acc[...], axis=0, keepdims=True)

    row = pl.BlockSpec((tm, d), lambda i: (i, 0))
    in_specs = [row, row, pl.BlockSpec((1, d), lambda i: (0, 0))] + ([row] if has_res else [])
    args = [dh, x, g] + ([res] if has_res else [])
    return pl.pallas_call(
        body, grid=(n,), in_specs=in_specs,
        out_specs=[row, pl.BlockSpec((1, d), lambda i: (0, 0))],
        out_shape=[SDS((s, d), out_dtype), SDS((1, d), F32)],
        scratch_shapes=[pltpu.VMEM((8, d), F32)],
        name=name, compiler_params=_cp(("arbitrary",)))(*args)


def mm(a_views, b_views, *, nt, out_dtype, tm, tn, name):
    n_p = len(a_views)
    m = a_views[0][0].shape[0]
    n = b_views[0][0].shape[0] if nt else b_views[0][0].shape[1]

    def body(*refs):
        o_ref = refs[-1]
        acc = None
        for p in range(n_p):
            av = refs[p][...].astype(BF16)
            bv = refs[n_p + p][...].astype(BF16)
            dv = _nt(av, bv) if nt else _nn(av, bv)
            acc = dv if acc is None else acc + dv
        o_ref[...] = acc.astype(o_ref.dtype)

    in_specs = []
    for arr, w, blk in a_views:
        in_specs.append(pl.BlockSpec((tm, w), functools.partial(lambda i, j, blk: (i, blk), blk=blk)))
    for arr, w, blk in b_views:
        if nt:
            in_specs.append(pl.BlockSpec((tn, w), functools.partial(lambda i, j, blk: (j, blk), blk=blk)))
        else:
            in_specs.append(pl.BlockSpec((w, tn), lambda i, j: (0, j)))
    return pl.pallas_call(
        body, grid=(m // tm, n // tn), in_specs=in_specs,
        out_specs=pl.BlockSpec((tm, tn), lambda i, j: (i, j)),
        out_shape=SDS((m, n), out_dtype), name=name,
        compiler_params=_cp(("parallel", "parallel")))(*[a[0] for a in a_views], *[b[0] for b in b_views])


def wgrad(a_view, g, *, tk, tn, ts, name, chip_major=False):
    arr, ka, blk = a_view
    s, n = g.shape
    ns = s // ts

    def body(a_ref, g_ref, o_ref):
        @pl.when(pl.program_id(2) == 0)
        def _():
            o_ref[...] = jnp.zeros_like(o_ref)

        o_ref[...] += _tn(a_ref[...].astype(BF16), g_ref[...].astype(BF16))

    if chip_major:
        out_spec = pl.BlockSpec((None, tk, tn), lambda i, j, k: (j, i, 0))
        out_shape = SDS((n // tn, ka, tn), F32)
    else:
        out_spec = pl.BlockSpec((tk, tn), lambda i, j, k: (i, j))
        out_shape = SDS((ka, n), F32)
    return pl.pallas_call(
        body, grid=(ka // tk, n // tn, ns),
        in_specs=[pl.BlockSpec((ts, tk), lambda i, j, k: (k, blk * (ka // tk) + i)),
                  pl.BlockSpec((ts, tn), lambda i, j, k: (k, j))],
        out_specs=out_spec, out_shape=out_shape, name=name,
        compiler_params=_cp(("parallel", "parallel", "arbitrary")))(arr, g)


def mm_rms_res(a, w, g, xres, target=None, *, tm=256, name):
    s, k = a.shape
    d = w.shape[1]
    n = s // tm
    with_loss = target is not None

    def body(*refs):
        if with_loss:
            a_ref, w_ref, g_ref, x_ref, t_ref, y_ref, o_ref, l_ref = refs
        else:
            a_ref, w_ref, g_ref, x_ref, y_ref, o_ref = refs
        y = _nn(a_ref[...], w_ref[...])
        inv = lax.rsqrt(jnp.mean(y * y, axis=-1, keepdims=True) + RMS_EPS)
        xn = x_ref[...] + y * inv * g_ref[...]
        y_ref[...] = y
        if with_loss:
            err = xn - t_ref[...]
            o_ref[...] = err * (1.0 / d)

            @pl.when(pl.program_id(0) == 0)
            def _():
                l_ref[...] = jnp.zeros_like(l_ref)

            l_ref[...] += jnp.sum(jnp.sum(err * err, axis=1, keepdims=True), axis=0, keepdims=True)
        else:
            o_ref[...] = xn

    row = pl.BlockSpec((tm, d), lambda i: (i, 0))
    in_specs = [pl.BlockSpec((tm, k), lambda i: (i, 0)), pl.BlockSpec((k, d), lambda i: (0, 0)),
                pl.BlockSpec((1, d), lambda i: (0, 0)), row]
    out_specs = [row, row]
    out_shape = [SDS((s, d), F32), SDS((s, d), F32)]
    args = [a, w, g, xres]
    if with_loss:
        in_specs.append(row)
        out_specs.append(pl.BlockSpec((1, 1), lambda i: (0, 0)))
        out_shape.append(SDS((1, 1), F32))
        args.append(target)
    return pl.pallas_call(
        body, grid=(n,), in_specs=in_specs, out_specs=out_specs, out_shape=out_shape, name=name,
        compiler_params=_cp(("arbitrary",)))(*args)


def _split3(v):
    hi = v.astype(BF16).astype(F32)
    r = v - hi
    mid = r.astype(BF16).astype(F32)
    lo = (r - mid).astype(BF16).astype(F32)
    return hi, mid, lo


def _tri(n, upper):
    r = np.arange(n)
    m = (r[:, None] <= r[None, :]) if upper else (r[:, None] >= r[None, :])
    return jnp.asarray(m.astype(np.float32))


def fox_prep(z, fa, bfo, *, tb=512):
    s = z.shape[0]
    n = s // tb

    def body(q_ref, k_ref, fa_ref, b_ref, tri_ref, qa_ref, ka_ref, carry):
        @pl.when(pl.program_id(0) == 0)
        def _():
            carry[...] = jnp.zeros_like(carry)

        xv = fa_ref[...] + b_ref[...]
        logf = jnp.minimum(xv, 0.0) - jnp.log(1.0 + jnp.exp(-jnp.abs(xv)))
        csum = jnp.dot(tri_ref[...], logf, preferred_element_type=F32, precision=lax.Precision.HIGHEST) + carry[0:1, :]
        carry[0:1, :] = csum[tb - 1:tb, :]
        lane = _lane((tb, LANES))
        for h in range(N_HEADS):
            hi, mid, lo = _split3(csum[:, h:h + 1])
            pair = (h // 2) * LANES
            qv = q_ref[:, pair:pair + LANES].astype(F32)
            kv = k_ref[:, pair:pair + LANES].astype(F32)
            if h % 2:
                qv = pltpu.roll(qv, 64, axis=1)
                kv = pltpu.roll(kv, 64, axis=1)
            one = jnp.where((lane >= 67) & (lane < 70), 1.0, 0.0)
            q_x = jnp.where(lane == 64, hi, jnp.where(lane == 65, mid, jnp.where(lane == 66, lo, one)))
            one = jnp.where((lane >= 64) & (lane < 67), 1.0, 0.0)
            k_x = jnp.where(lane == 67, -hi, jnp.where(lane == 68, -mid, jnp.where(lane == 69, -lo, one)))
            qa_ref[:, h * LANES:(h + 1) * LANES] = jnp.where(lane < 64, qv * QK_SCALE, q_x).astype(BF16)
            ka_ref[:, h * LANES:(h + 1) * LANES] = jnp.where(lane < 64, kv, k_x).astype(BF16)

    return pl.pallas_call(
        body, grid=(n,),
        in_specs=[pl.BlockSpec((tb, ATT_W), lambda i: (i, Z_QA)), pl.BlockSpec((tb, ATT_W), lambda i: (i, Z_KA)),
                  pl.BlockSpec((tb, LANES), lambda i: (i, 0)), pl.BlockSpec((1, LANES), lambda i: (0, 0)),
                  pl.BlockSpec((tb, tb), lambda i: (0, 0))],
        out_specs=[pl.BlockSpec((tb, N_HEADS * LANES), lambda i: (i, 0))] * 2,
        out_shape=[SDS((s, N_HEADS * LANES), BF16)] * 2,
        scratch_shapes=[pltpu.VMEM((8, LANES), F32)],
        name="fox_prep", compiler_params=_cp(("arbitrary",)))(z, z, fa, bfo, _tri(tb, False))


def _causal_pairs(n, k_major):
    if k_major:
        pairs = [(qi, kj) for kj in range(n) for qi in range(kj, n)]
    else:
        pairs = [(qi, kj) for qi in range(n) for kj in range(qi + 1)]
    return (jnp.asarray([p[0] for p in pairs], jnp.int32), jnp.asarray([p[1] for p in pairs], jnp.int32), len(pairs))


def fox_fwd(q_aug, k_aug, z, *, t=512):
    s = z.shape[0]
    qi_arr, kj_arr, n_pairs = _causal_pairs(s // t, False)

    def body(qi_ref, kj_ref, q_ref, k_ref, v_ref, o_ref, lse_ref, m_scr, l_scr, acc_scr):
        step = pl.program_id(1)
        qi = qi_ref[step]
        kj = kj_ref[step]

        @pl.when(kj == 0)
        def _():
            m_scr[...] = jnp.full_like(m_scr, NEG)
            l_scr[...] = jnp.zeros_like(l_scr)
            acc_scr[...] = jnp.zeros_like(acc_scr)

        def update(masked):
            for i in range(2):
                sc = _nt(q_ref[:, i * LANES:(i + 1) * LANES], k_ref[:, i * LANES:(i + 1) * LANES])
                if masked:
                    sc = jnp.where(_row((t, t)) >= _lane((t, t)), sc, NEG)
                m_prev = m_scr[i]
                m_new = jnp.maximum(m_prev, jnp.max(sc, axis=-1, keepdims=True))
                alpha = jnp.exp(m_prev - m_new)
                p = jnp.exp(sc - jnp.tile(m_new, (1, t // LANES)))
                l_scr[i] = alpha * l_scr[i] + jnp.sum(p, axis=-1, keepdims=True)
                acc_scr[i] = alpha * acc_scr[i] + _nn(p.astype(BF16), v_ref[...])
                m_scr[i] = m_new

        @pl.when(kj < qi)
        def _():
            update(False)

        @pl.when(kj == qi)
        def _():
            update(True)
            lane = _lane((t, LANES))
            o_ref[...] = jnp.where(lane < 64, acc_scr[0] / l_scr[0], acc_scr[1] / l_scr[1]).astype(o_ref.dtype)
            lse_ref[...] = jnp.where(lane < 64, m_scr[0] + jnp.log(l_scr[0]), m_scr[1] + jnp.log(l_scr[1]))

    grid_spec = pltpu.PrefetchScalarGridSpec(
        num_scalar_prefetch=2, grid=(4, n_pairs),
        in_specs=[pl.BlockSpec((t, 2 * LANES), lambda hp, st, qi, kj: (qi[st], hp)),
                  pl.BlockSpec((t, 2 * LANES), lambda hp, st, qi, kj: (kj[st], hp)),
                  pl.BlockSpec((t, LANES), lambda hp, st, qi, kj: (kj[st], 4 * Z_VA + hp))],
        out_specs=[pl.BlockSpec((t, LANES), lambda hp, st, qi, kj: (qi[st], hp))] * 2,
        scratch_shapes=[pltpu.VMEM((2, t, LANES), F32)] * 3)
    return pl.pallas_call(
        body, grid_spec=grid_spec, out_shape=[SDS((s, ATT_W), BF16), SDS((s, ATT_W), F32)],
        name="fox_fwd", compiler_params=_cp(("parallel", "arbitrary")))(qi_arr, kj_arr, q_aug, k_aug, z)


def fox_bwd(q_aug, k_aug, z, dy, lse, dd, *, t=512):
    s = z.shape[0]
    qi_arr, kj_arr, n_pairs = _causal_pairs(s // t, True)

    def body(qi_ref, kj_ref, q_ref, k_ref, v_ref, do_ref, lse_ref, dd_ref, dq_ref, dk_ref, dv_ref):
        step = pl.program_id(1)
        qi = qi_ref[step]
        kj = kj_ref[step]

        @pl.when(step == 0)
        def _():
            dq_ref[...] = jnp.zeros_like(dq_ref)

        @pl.when(qi == kj)
        def _():
            dk_ref[...] = jnp.zeros_like(dk_ref)
            dv_ref[...] = jnp.zeros_like(dv_ref)

        def update(masked):
            lane = _lane((t, LANES))
            rows = pl.ds(pl.multiple_of(qi * t, t), t)
            dov = do_ref[...]
            dv_new = None
            for i in range(2):
                head = (lane < 64) if i == 0 else (lane >= 64)
                qv = q_ref[:, i * LANES:(i + 1) * LANES]
                kv = k_ref[:, i * LANES:(i + 1) * LANES]
                sc = _nt(qv, kv)
                if masked:
                    sc = jnp.where(_row((t, t)) >= _lane((t, t)), sc, NEG)
                p = jnp.exp(sc - lse_ref[:, i * 64:i * 64 + 1])
                dp = _nt(jnp.where(head, dov, jnp.zeros_like(dov)), v_ref[...])
                ds = (p * (dp - dd_ref[:, i * 64:i * 64 + 1])).astype(BF16)
                dq_ref[rows, i * LANES:(i + 1) * LANES] += _nn(ds, kv)
                dk_ref[:, i * LANES:(i + 1) * LANES] += _tn(ds, qv)
                dvi = _tn(p.astype(BF16), dov)
                dv_new = dvi if dv_new is None else jnp.where(head, dvi, dv_new)
            dv_ref[...] += dv_new

        @pl.when(kj < qi)
        def _():
            update(False)

        @pl.when(kj == qi)
        def _():
            update(True)

    grid_spec = pltpu.PrefetchScalarGridSpec(
        num_scalar_prefetch=2, grid=(4, n_pairs),
        in_specs=[pl.BlockSpec((t, 2 * LANES), lambda hp, st, qi, kj: (qi[st], hp)),
                  pl.BlockSpec((t, 2 * LANES), lambda hp, st, qi, kj: (kj[st], hp)),
                  pl.BlockSpec((t, LANES), lambda hp, st, qi, kj: (kj[st], 4 * Z_VA + hp)),
                  pl.BlockSpec((t, LANES), lambda hp, st, qi, kj: (qi[st], hp)),
                  pl.BlockSpec((t, LANES), lambda hp, st, qi, kj: (qi[st], hp)),
                  pl.BlockSpec((t, LANES), lambda hp, st, qi, kj: (qi[st], hp))],
        out_specs=[pl.BlockSpec((s, 2 * LANES), lambda hp, st, qi, kj: (0, hp)),
                   pl.BlockSpec((t, 2 * LANES), lambda hp, st, qi, kj: (kj[st], hp)),
                   pl.BlockSpec((t, LANES), lambda hp, st, qi, kj: (kj[st], hp))])
    return pl.pallas_call(
        body, grid_spec=grid_spec,
        out_shape=[SDS((s, N_HEADS * LANES), F32), SDS((s, N_HEADS * LANES), F32), SDS((s, ATT_W), F32)],
        name="fox_bwd", compiler_params=_cp(("parallel", "arbitrary")))(qi_arr, kj_arr, q_aug, k_aug, z, dy, lse, dd)


def head_rowsum(a, b, *, tm=512, name):
    s = a.shape[0]

    def body(a_ref, b_ref, o_ref):
        prod = a_ref[...].astype(F32) * b_ref[...].astype(F32)
        lane = _lane((tm, LANES))
        lo = jnp.sum(jnp.where(lane < 64, prod, 0.0), axis=-1, keepdims=True)
        hi = jnp.sum(jnp.where(lane >= 64, prod, 0.0), axis=-1, keepdims=True)
        o_ref[...] = jnp.where(lane < 64, lo, hi)

    blk = pl.BlockSpec((tm, LANES), lambda i, j: (i, j))
    return pl.pallas_call(body, grid=(s // tm, 4), in_specs=[blk, blk], out_specs=blk, out_shape=SDS((s, ATT_W), F32),
                          name=name, compiler_params=_cp(("parallel", "parallel")))(a, b)


def fox_post(dq_aug, dk_aug, dv, fa, bfo, *, tb=512):
    s = dv.shape[0]
    n = s // tb

    def body(dq_ref, dk_ref, dv_ref, fa_ref, b_ref, tri_ref, dz_ref, dfa_ref, gb_ref, carry, acc):
        i = pl.program_id(0)

        @pl.when(i == 0)
        def _():
            carry[...] = jnp.zeros_like(carry)
            acc[...] = jnp.zeros_like(acc)

        lane = _lane((tb, LANES))
        d_f = jnp.zeros((tb, LANES), F32)
        for h in range(N_HEADS):
            col = dq_ref[:, h * LANES + 64:h * LANES + 65] - dk_ref[:, h * LANES + 67:h * LANES + 68]
            d_f = jnp.where(lane == h, col, d_f)
        suffix = jnp.dot(tri_ref[...], d_f, preferred_element_type=F32, precision=lax.Precision.HIGHEST) + carry[0:1, :]
        carry[0:1, :] = suffix[0:1, :]
        xv = fa_ref[...] + b_ref[...]
        dx = suffix * (1.0 / (1.0 + jnp.exp(xv)))
        dfa_ref[...] = dx.astype(dfa_ref.dtype)
        acc[...] += jnp.sum(dx.reshape(tb // 8, 8, LANES), axis=0)
        for hp in range(4):
            for src, off, scale in ((dq_ref, 0, QK_SCALE), (dk_ref, ATT_W, 1.0)):
                even = src[:, (2 * hp) * LANES:(2 * hp + 1) * LANES]
                odd = pltpu.roll(src[:, (2 * hp + 1) * LANES:(2 * hp + 2) * LANES], 64, axis=1)
                dz_ref[:, off + hp * LANES:off + (hp + 1) * LANES] = (jnp.where(lane < 64, even, odd) * scale).astype(BF16)
        dz_ref[:, 2 * ATT_W:3 * ATT_W] = dv_ref[...].astype(BF16)

        @pl.when(i == n - 1)
        def _():
            gb_ref[...] = jnp.sum(acc[...], axis=0, keepdims=True)

    rev = lambda i: (n - 1 - i, 0)
    return pl.pallas_call(
        body, grid=(n,),
        in_specs=[pl.BlockSpec((tb, N_HEADS * LANES), rev), pl.BlockSpec((tb, N_HEADS * LANES), rev),
                  pl.BlockSpec((tb, ATT_W), rev), pl.BlockSpec((tb, LANES), rev),
                  pl.BlockSpec((1, LANES), lambda i: (0, 0)), pl.BlockSpec((tb, tb), lambda i: (0, 0))],
        out_specs=[pl.BlockSpec((tb, 3 * ATT_W), rev), pl.BlockSpec((tb, LANES), rev),
                   pl.BlockSpec((1, LANES), lambda i: (0, 0))],
        out_shape=[SDS((s, 3 * ATT_W), BF16), SDS((s, LANES), BF16), SDS((1, LANES), F32)],
        scratch_shapes=[pltpu.VMEM((8, LANES), F32), pltpu.VMEM((8, LANES), F32)],
        name="fox_post", compiler_params=_cp(("arbitrary",)))(dq_aug, dk_aug, dv, fa, bfo, _tri(tb, True))


def rope_tables(s, sign):
    half = ROPE_DIM // 2
    inv_freq = ROPE_THETA ** (-jnp.arange(half, dtype=F32) * 2.0 / ROPE_DIM)
    ang = jnp.arange(s, dtype=F32)[:, None] * inv_freq[None, :]
    l64 = np.arange(LANES) % HEAD_DIM
    cos = jnp.cos(ang)[:, l64 % half]
    sin = jnp.sin(ang)[:, l64 % half] * sign
    first = jnp.asarray(l64 < half)[None, :]
    second = jnp.asarray((l64 >= half) & (l64 < ROPE_DIM))[None, :]
    return (jnp.where(first | second, cos, 1.0), jnp.where(first, -sin, 0.0), jnp.where(second, sin, 0.0))


def rope_apply(items, tabs, *, out_dtype, tm=512, name):
    s = items[0][0].shape[0]
    n_i = len(items)

    def body(*refs):
        c_ref, sn_ref, sp_ref = refs[n_i:n_i + 3]
        o_ref = refs[-1]
        for j, (_, _, scale, rotate) in enumerate(items):
            for b in range(4):
                xv = refs[j][:, b * LANES:(b + 1) * LANES].astype(F32)
                if rotate:
                    xv = xv * c_ref[...] + pltpu.roll(xv, LANES - 8, axis=1) * sn_ref[...] + pltpu.roll(xv, 8, axis=1) * sp_ref[...]
                o_ref[:, j * ATT_W + b * LANES:j * ATT_W + (b + 1) * LANES] = (xv * scale).astype(o_ref.dtype)

    in_specs = [pl.BlockSpec((tm, ATT_W), functools.partial(lambda i, blk: (i, blk), blk=it[1])) for it in items]
    in_specs += [pl.BlockSpec((tm, LANES), lambda i: (i, 0))] * 3
    return pl.pallas_call(
        body, grid=(s // tm,), in_specs=in_specs, out_specs=pl.BlockSpec((tm, n_i * ATT_W), lambda i: (i, 0)),
        out_shape=SDS((s, n_i * ATT_W), out_dtype), name=name, compiler_params=_cp(("parallel",)))(*[it[0] for it in items], *tabs)


def _dil_views(qk, z, r):
    s = z.shape[0]
    return qk.reshape(s // r, r * 2 * ATT_W), z.reshape(s // r, r * Z_W)


def _dil_cols(r):
    q_col = lambda rho, hp: rho * 8 + hp
    k_col = lambda rho, hp: rho * 8 + 4 + hp
    v_col = lambda rho, hp: rho * (Z_W // LANES) + 4 * Z_VB + hp
    return q_col, k_col, v_col


def _dil_scores(qv, kp, kc, head, has_prev):
    b = DIL_BLK
    qm = jnp.where(head, qv, jnp.zeros_like(qv))
    row, col = _row((b, b)), _lane((b, b))
    sp = jnp.where((col >= row) & has_prev, _nt(qm, kp), NEG)
    sc = jnp.where(col <= row, _nt(qm, kc), NEG)
    return sp, sc


def dil_fwd(qk, z, prev, *, r):
    s = z.shape[0]
    b = DIL_BLK
    l_sub = s // r
    nb = l_sub // b
    qk_v, z_v = _dil_views(qk, z, r)
    q_col, k_col, v_col = _dil_cols(r)
    merge = prev is not None

    def body(*refs):
        if merge:
            q_ref, kp_ref, kc_ref, vp_ref, vc_ref, op_ref, lp_ref, o_ref, l_ref = refs
        else:
            q_ref, kp_ref, kc_ref, vp_ref, vc_ref, o_ref, l_ref = refs
        has_prev = pl.program_id(2) > 0
        lane = _lane((b, LANES))
        res = []
        for i in range(2):
            head = (lane < 64) if i == 0 else (lane >= 64)
            sp, sc = _dil_scores(q_ref[...], kp_ref[...], kc_ref[...], head, has_prev)
            m = jnp.maximum(jnp.max(sp, axis=-1, keepdims=True), jnp.max(sc, axis=-1, keepdims=True))
            pp = jnp.exp(sp - m)
            pc = jnp.exp(sc - m)
            den = jnp.sum(pp, axis=-1, keepdims=True) + jnp.sum(pc, axis=-1, keepdims=True)
            ov = (_nn(pp.astype(BF16), vp_ref[...]) + _nn(pc.astype(BF16), vc_ref[...])) / den
            res.append((ov, m + jnp.log(den)))
        ov = jnp.where(lane < 64, res[0][0], res[1][0])
        lse = jnp.where(lane < 64, res[0][1], res[1][1])
        if merge:
            lp = lp_ref[...]
            m2 = jnp.maximum(lp, lse)
            wp = jnp.exp(lp - m2)
            wn = jnp.exp(lse - m2)
            ov = (wp * op_ref[...] + wn * ov) / (wp + wn)
            lse = m2 + jnp.log(wp + wn)
        o_ref[...] = ov
        l_ref[...] = lse

    blk = lambda f: pl.BlockSpec((b, LANES), f)
    in_specs = [blk(lambda rho, hp, n: (n, q_col(rho, hp))), blk(lambda rho, hp, n: (jnp.maximum(n - 1, 0), k_col(rho, hp))),
                blk(lambda rho, hp, n: (n, k_col(rho, hp))), blk(lambda rho, hp, n: (jnp.maximum(n - 1, 0), v_col(rho, hp))),
                blk(lambda rho, hp, n: (n, v_col(rho, hp)))]
    args = [qk_v, qk_v, qk_v, z_v, z_v]
    nat = blk(lambda rho, hp, n: (n, rho * 4 + hp))
    if merge:
        in_specs += [nat, nat]
        args += [prev[0].reshape(l_sub, r * ATT_W), prev[1].reshape(l_sub, r * ATT_W)]
    o, lse = pl.pallas_call(
        body, grid=(r, 4, nb), in_specs=in_specs, out_specs=[nat, nat],
        out_shape=[SDS((l_sub, r * ATT_W), F32)] * 2, name=f"dil_fwd_r{r}",
        compiler_params=_cp(("parallel", "parallel", "arbitrary")))(*args)
    return o.reshape(s, ATT_W), lse.reshape(s, ATT_W)


def dil_bwd_dq(qk, z, dy, lse, dd, acc, *, r):
    s = z.shape[0]
    b = DIL_BLK
    l_sub = s // r
    nb = l_sub // b
    qk_v, z_v = _dil_views(qk, z, r)
    q_col, k_col, v_col = _dil_cols(r)
    add = acc is not None

    def body(*refs):
        q_ref, kp_ref, kc_ref, vp_ref, vc_ref, do_ref, l_ref, dd_ref = refs[:8]
        dq_ref = refs[-1]
        has_prev = pl.program_id(2) > 0
        lane = _lane((b, LANES))
        dov = do_ref[...]
        parts = []
        for i in range(2):
            head = (lane < 64) if i == 0 else (lane >= 64)
            sp, sc = _dil_scores(q_ref[...], kp_ref[...], kc_ref[...], head, has_prev)
            lse_i = l_ref[:, i * 64:i * 64 + 1]
            dd_i = dd_ref[:, i * 64:i * 64 + 1]
            dom = jnp.where(head, dov, jnp.zeros_like(dov))
            dsp = (jnp.exp(sp - lse_i) * (_nt(dom, vp_ref[...]) - dd_i)).astype(BF16)
            dsc = (jnp.exp(sc - lse_i) * (_nt(dom, vc_ref[...]) - dd_i)).astype(BF16)
            parts.append(_nn(dsp, kp_ref[...]) + _nn(dsc, kc_ref[...]))
        dq = jnp.where(lane < 64, parts[0], parts[1])
        if add:
            dq = dq + refs[8][...]
        dq_ref[...] = dq

    blk = lambda f: pl.BlockSpec((b, LANES), f)
    nat = blk(lambda rho, hp, n: (n, rho * 4 + hp))
    in_specs = [blk(lambda rho, hp, n: (n, q_col(rho, hp))), blk(lambda rho, hp, n: (jnp.maximum(n - 1, 0), k_col(rho, hp))),
                blk(lambda rho, hp, n: (n, k_col(rho, hp))), blk(lambda rho, hp, n: (jnp.maximum(n - 1, 0), v_col(rho, hp))),
                blk(lambda rho, hp, n: (n, v_col(rho, hp))), nat, nat, nat]
    nview = lambda a: a.reshape(l_sub, r * ATT_W)
    args = [qk_v, qk_v, qk_v, z_v, z_v, nview(dy), nview(lse), nview(dd)]
    if add:
        in_specs.append(nat)
        args.append(nview(acc))
    dq = pl.pallas_call(
        body, grid=(r, 4, nb), in_specs=in_specs, out_specs=nat, out_shape=SDS((l_sub, r * ATT_W), F32),
        name=f"dil_bwd_dq_r{r}", compiler_params=_cp(("parallel", "parallel", "arbitrary")))(*args)
    return dq.reshape(s, ATT_W)


def dil_bwd_dkv(qk, z, dy, lse, dd, acc, *, r):
    s = z.shape[0]
    b = DIL_BLK
    l_sub = s // r
    nb = l_sub // b
    qk_v, z_v = _dil_views(qk, z, r)
    q_col, k_col, v_col = _dil_cols(r)
    add = acc is not None

    def body(*refs):
        k_ref, v_ref, qc_ref, qn_ref, doc_ref, don_ref, lc_ref, ln_ref, ddc_ref, ddn_ref = refs[:10]
        dk_ref, dv_ref = refs[-2:]
        has_next = pl.program_id(2) < nb - 1
        lane = _lane((b, LANES))
        row, col = _row((b, b)), _lane((b, b))
        kv = k_ref[...]
        vv = v_ref[...]
        dk_parts, dv_parts = [], []
        for i in range(2):
            head = (lane < 64) if i == 0 else (lane >= 64)
            dk_i = jnp.zeros((b, LANES), F32)
            dv_i = jnp.zeros((b, LANES), F32)
            for q_ref, do_ref, l_ref, d_ref, mask in ((qc_ref, doc_ref, lc_ref, ddc_ref, col <= row),
                                                      (qn_ref, don_ref, ln_ref, ddn_ref, (col >= row) & has_next)):
                qv = q_ref[...]
                dov = do_ref[...]
                sc = jnp.where(mask, _nt(jnp.where(head, qv, jnp.zeros_like(qv)), kv), NEG)
                p = jnp.exp(sc - l_ref[:, i * 64:i * 64 + 1])
                dp = _nt(jnp.where(head, dov, jnp.zeros_like(dov)), vv)
                ds = (p * (dp - d_ref[:, i * 64:i * 64 + 1])).astype(BF16)
                dv_i = dv_i + _tn(p.astype(BF16), dov)
                dk_i = dk_i + _tn(ds, qv)
            dk_parts.append(dk_i)
            dv_parts.append(dv_i)
        dk = jnp.where(lane < 64, dk_parts[0], dk_parts[1])
        dv = jnp.where(lane < 64, dv_parts[0], dv_parts[1])
        if add:
            dk = dk + refs[10][...]
            dv = dv + refs[11][...]
        dk_ref[...] = dk
        dv_ref[...] = dv

    blk = lambda f: pl.BlockSpec((b, LANES), f)
    nat = blk(lambda rho, hp, n: (n, rho * 4 + hp))
    nxt = blk(lambda rho, hp, n: (jnp.minimum(n + 1, nb - 1), rho * 4 + hp))
    in_specs = [blk(lambda rho, hp, n: (n, k_col(rho, hp))), blk(lambda rho, hp, n: (n, v_col(rho, hp))),
                blk(lambda rho, hp, n: (n, q_col(rho, hp))), blk(lambda rho, hp, n: (jnp.minimum(n + 1, nb - 1), q_col(rho, hp))),
                nat, nxt, nat, nxt, nat, nxt]
    nview = lambda a: a.reshape(l_sub, r * ATT_W)
    args = [qk_v, z_v, qk_v, qk_v, nview(dy), nview(dy), nview(lse), nview(lse), nview(dd), nview(dd)]
    if add:
        in_specs += [nat, nat]
        args += [nview(acc[0]), nview(acc[1])]
    dk, dv = pl.pallas_call(
        body, grid=(r, 4, nb), in_specs=in_specs, out_specs=[nat, nat],
        out_shape=[SDS((l_sub, r * ATT_W), F32)] * 2, name=f"dil_bwd_dkv_r{r}",
        compiler_params=_cp(("parallel", "parallel", "arbitrary")))(*args)
    return dk.reshape(s, ATT_W), dv.reshape(s, ATT_W)


def _dil_rows(base, r):
    if r == 1:
        return pl.ds(pl.multiple_of(base, DIL_BLK), DIL_BLK)
    return pl.ds(base, DIL_BLK, stride=r)


def _dil_block(idx, r, nb):
    shift = nb.bit_length() - 1
    rho = idx >> shift
    n = idx & (nb - 1)
    base = rho + n * (r * DIL_BLK)
    return _dil_rows(base, r), _dil_rows(jnp.maximum(base - r * DIL_BLK, rho), r), n > 0


def dil_fwd_all(qkv, *, unroll=4):
    s = qkv.shape[0]
    b = DIL_BLK
    n_blk = s // b

    def body(q_ref, k_ref, v_ref, o_ref, l_ref):
        lane = _lane((b, LANES))
        heads = (lane < 64, lane >= 64)
        row, col = _row((b, b)), _lane((b, b))
        band_prev, band_cur = col >= row, col <= row
        for g, (_, r) in enumerate(DIL_PATTERNS):
            nb = n_blk // r

            def group(it, carry, g=g, r=r, nb=nb):
                loaded = []
                for u in range(unroll):
                    rows_c, rows_p, has_prev = _dil_block(it * unroll + u, r, nb)
                    vals = [q_ref[rows_c, :].astype(BF16), k_ref[rows_p, :].astype(BF16), k_ref[rows_c, :].astype(BF16),
                            v_ref[rows_p, :].astype(BF16), v_ref[rows_c, :].astype(BF16)]
                    state = (o_ref[rows_c, :], l_ref[rows_c, :]) if g else None
                    loaded.append((rows_c, has_prev, vals, state))
                done = []
                for rows_c, has_prev, (qv, kp, kc, vp, vc), state in loaded:
                    res = []
                    for head in heads:
                        qm = jnp.where(head, qv, jnp.zeros_like(qv))
                        sp = jnp.where(band_prev & has_prev, _nt(qm, kp), NEG)
                        sc = jnp.where(band_cur, _nt(qm, kc), NEG)
                        m = jnp.maximum(jnp.max(sp, axis=-1, keepdims=True), jnp.max(sc, axis=-1, keepdims=True))
                        pp = jnp.exp(sp - m)
                        pc = jnp.exp(sc - m)
                        den = jnp.sum(pp, axis=-1, keepdims=True) + jnp.sum(pc, axis=-1, keepdims=True)
                        res.append(((_nn(pp.astype(BF16), vp) + _nn(pc.astype(BF16), vc)) / den, m + jnp.log(den)))
                    ov = jnp.where(heads[0], res[0][0], res[1][0])
                    lse = jnp.where(heads[0], res[0][1], res[1][1])
                    if state is not None:
                        m2 = jnp.maximum(state[1], lse)
                        wp = jnp.exp(state[1] - m2)
                        wn = jnp.exp(lse - m2)
                        ov = (wp * state[0] + wn * ov) / (wp + wn)
                        lse = m2 + jnp.log(wp + wn)
                    done.append((rows_c, ov, lse))
                for rows_c, ov, lse in done:
                    o_ref[rows_c, :] = ov
                    l_ref[rows_c, :] = lse
                return carry

            lax.fori_loop(0, n_blk // unroll, group, 0)

    col_blk = lambda k: pl.BlockSpec((s, LANES), lambda hp: (0, 4 * k + hp))
    out = pl.BlockSpec((s, LANES), lambda hp: (0, hp))
    return pl.pallas_call(
        body, grid=(4,), in_specs=[col_blk(0), col_blk(1), col_blk(2)], out_specs=[out, out],
        out_shape=[SDS((s, ATT_W), F32)] * 2, name="dil_fwd", compiler_params=_cp(("parallel",)))(qkv, qkv, qkv)


def dil_bwd_all(qkv, dy, lse, dd, *, unroll=2):
    s = qkv.shape[0]
    b = DIL_BLK
    n_blk = s // b

    def body(q_ref, k_ref, v_ref, do_ref, l_ref, dd_ref, dq_ref, dk_ref, dv_ref):
        dq_ref[...] = jnp.zeros_like(dq_ref)
        dk_ref[...] = jnp.zeros_like(dk_ref)
        dv_ref[...] = jnp.zeros_like(dv_ref)
        lane = _lane((b, LANES))
        heads = (lane < 64, lane >= 64)
        row, col = _row((b, b)), _lane((b, b))
        band_prev, band_cur = col >= row, col <= row
        for _, r in DIL_PATTERNS:
            nb = n_blk // r

            def group(it, carry, r=r, nb=nb):
                loaded = []
                for u in range(unroll):
                    rows_c, rows_p, has_prev = _dil_block(it * unroll + u, r, nb)
                    vals = [q_ref[rows_c, :].astype(BF16), k_ref[rows_p, :].astype(BF16), k_ref[rows_c, :].astype(BF16),
                            v_ref[rows_p, :].astype(BF16), v_ref[rows_c, :].astype(BF16), do_ref[rows_c, :].astype(BF16),
                            l_ref[rows_c, :], dd_ref[rows_c, :]]
                    loaded.append((rows_c, rows_p, has_prev, vals))
                done = []
                for rows_c, rows_p, has_prev, (qv, kp, kc, vp, vc, dov, lv, ddv) in loaded:
                    parts = []
                    for i, head in enumerate(heads):
                        qm = jnp.where(head, qv, jnp.zeros_like(qv))
                        dom = jnp.where(head, dov, jnp.zeros_like(dov))
                        lse_i = lv[:, i * 64:i * 64 + 1]
                        dd_i = ddv[:, i * 64:i * 64 + 1]
                        pp = jnp.exp(jnp.where(band_prev & has_prev, _nt(qm, kp), NEG) - lse_i)
                        pc = jnp.exp(jnp.where(band_cur, _nt(qm, kc), NEG) - lse_i)
                        dsp = (pp * (_nt(dom, vp) - dd_i)).astype(BF16)
                        dsc = (pc * (_nt(dom, vc) - dd_i)).astype(BF16)
                        parts.append((_nn(dsp, kp) + _nn(dsc, kc), _tn(dsp, qv), _tn(dsc, qv),
                                      _tn(pp.astype(BF16), dov), _tn(pc.astype(BF16), dov)))
                    done.append((rows_c, rows_p, [jnp.where(heads[0], a0, a1) for a0, a1 in zip(*parts)]))
                for rows_c, rows_p, (dq, dk_p, dk_c, dv_p, dv_c) in done:
                    dq_ref[rows_c, :] += dq
                    dk_ref[rows_p, :] += dk_p
                    dk_ref[rows_c, :] += dk_c
                    dv_ref[rows_p, :] += dv_p
                    dv_ref[rows_c, :] += dv_c
                return carry

            lax.fori_loop(0, n_blk // unroll, group, 0)

    col_blk = lambda k: pl.BlockSpec((s, LANES), lambda hp: (0, 4 * k + hp))
    nat = pl.BlockSpec((s, LANES), lambda hp: (0, hp))
    return pl.pallas_call(
        body, grid=(4,), in_specs=[col_blk(0), col_blk(1), col_blk(2), nat, nat, nat], out_specs=[nat, nat, nat],
        out_shape=[SDS((s, ATT_W), F32)] * 3, name="dil_bwd", compiler_params=_cp(("parallel",)))(qkv, qkv, qkv, dy, lse, dd)


def _sigmoid(v):
    return 1.0 / (1.0 + jnp.exp(-v))


def gate_mix(ya, yb, wa, wb, z, *, tm=512, tn=512):
    s = ya.shape[0]
    d = wa.shape[1]
    ga_blk = 3 * ATT_W * 2 // tn
    gb_blk = ga_blk + d // tn

    def body(ya_ref, yb_ref, wa_ref, wb_ref, ga_ref, gb_ref, pa_ref, pb_ref, mx_ref):
        pa = _nn(ya_ref[...], wa_ref[...])
        pb = _nn(yb_ref[...].astype(BF16), wb_ref[...])
        pa_ref[...] = pa.astype(BF16)
        pb_ref[...] = pb.astype(BF16)
        mx_ref[...] = (_sigmoid(ga_ref[...].astype(F32)) * pa + _sigmoid(gb_ref[...].astype(F32)) * pb).astype(BF16)

    out = pl.BlockSpec((tm, tn), lambda i, j: (i, j))
    return pl.pallas_call(
        body, grid=(s // tm, d // tn),
        in_specs=[pl.BlockSpec((tm, ATT_W), lambda i, j: (i, 0)), pl.BlockSpec((tm, ATT_W), lambda i, j: (i, 0)),
                  pl.BlockSpec((ATT_W, tn), lambda i, j: (0, j)), pl.BlockSpec((ATT_W, tn), lambda i, j: (0, j)),
                  pl.BlockSpec((tm, tn), lambda i, j: (i, ga_blk + j)), pl.BlockSpec((tm, tn), lambda i, j: (i, gb_blk + j))],
        out_specs=[out, out, out], out_shape=[SDS((s, d), BF16)] * 3, name="gate_mix",
        compiler_params=_cp(("parallel", "parallel")))(ya, yb, wa, wb, z, z)


def gate_bwd(dmx, z, pa, pb, *, tm=256):
    s, d = dmx.shape

    def body(dm_ref, ga_ref, gb_ref, pa_ref, pb_ref, dpa_ref, dpb_ref, dg_ref):
        dm = dm_ref[...].astype(F32)
        sa = _sigmoid(ga_ref[...].astype(F32))
        sb = _sigmoid(gb_ref[...].astype(F32))
        dpa_ref[...] = (dm * sa).astype(BF16)
        dpb_ref[...] = (dm * sb).astype(BF16)
        dg_ref[:, 0:d] = (dm * pa_ref[...].astype(F32) * sa * (1.0 - sa)).astype(BF16)
        dg_ref[:, d:2 * d] = (dm * pb_ref[...].astype(F32) * sb * (1.0 - sb)).astype(BF16)

    row = pl.BlockSpec((tm, d), lambda i: (i, 0))
    return pl.pallas_call(
        body, grid=(s // tm,),
        in_specs=[row, pl.BlockSpec((tm, d), lambda i: (i, 3)), pl.BlockSpec((tm, d), lambda i: (i, 4)), row, row],
        out_specs=[row, row, pl.BlockSpec((tm, 2 * d), lambda i: (i, 0))],
        out_shape=[SDS((s, d), BF16), SDS((s, d), BF16), SDS((s, 2 * d), BF16)], name="gate_bwd",
        compiler_params=_cp(("parallel",)))(dmx, z, z, pa, pb)


GELU_C = math.sqrt(2.0 / math.pi)


def _gelu_parts(a):
    inner = GELU_C * (a + 0.044715 * a * a * a)
    th = jnp.tanh(inner)
    gelu = 0.5 * a * (1.0 + th)
    dgelu = 0.5 * (1.0 + th) + 0.5 * a * (1.0 - th * th) * GELU_C * (1.0 + 3.0 * 0.044715 * a * a)
    return gelu, dgelu


def _causal_taps(u, before):
    row = _row(u.shape)
    r1 = jnp.where(row == 0, before[7:8, :], pltpu.roll(u, 1, axis=0))
    r2 = jnp.where(row == 0, before[6:7, :], jnp.where(row == 1, before[7:8, :], pltpu.roll(u, 2, axis=0)))
    return r1, r2


def ffn_up(h, wa, wb, cw, cb, *, tm=512, tn=256):
    s, d = h.shape
    f = wa.shape[1]
    nj = f // tn

    def body(h_ref, wa_ref, wb_ref, cwa_ref, cwb_ref, cba_ref, cbb_ref, ua_ref, ub_ref, m_ref, carry):
        @pl.when(pl.program_id(1) == 0)
        def _():
            carry[...] = jnp.zeros_like(carry)

        conv = []
        for k, (w_ref, cw_ref, cb_ref, u_ref) in enumerate(((wa_ref, cwa_ref, cba_ref, ua_ref), (wb_ref, cwb_ref, cbb_ref, ub_ref))):
            u16 = _nn(h_ref[...], w_ref[...]).astype(BF16)
            u_ref[...] = u16
            u = u16.astype(F32)
            r1, r2 = _causal_taps(u, carry[k])
            carry[k] = u[tm - 8:tm, :]
            conv.append(cw_ref[0:1, :] * r2 + cw_ref[1:2, :] * r1 + cw_ref[2:3, :] * u + cb_ref[...])
        m_ref[...] = (_gelu_parts(conv[0])[0] * conv[1]).astype(BF16)

    out = pl.BlockSpec((tm, tn), lambda j, i: (i, j))
    return pl.pallas_call(
        body, grid=(nj, s // tm),
        in_specs=[pl.BlockSpec((tm, d), lambda j, i: (i, 0)),
                  pl.BlockSpec((d, tn), lambda j, i: (0, j)), pl.BlockSpec((d, tn), lambda j, i: (0, j)),
                  pl.BlockSpec((3, tn), lambda j, i: (0, j)), pl.BlockSpec((3, tn), lambda j, i: (0, nj + j)),
                  pl.BlockSpec((1, tn), lambda j, i: (0, j)), pl.BlockSpec((1, tn), lambda j, i: (0, nj + j))],
        out_specs=[out, out, out], out_shape=[SDS((s, f), BF16)] * 3,
        scratch_shapes=[pltpu.VMEM((2, 8, tn), F32)], name="ffn_up",
        compiler_params=_cp(("parallel", "arbitrary")))(h, wa, wb, cw, cw, cb, cb)


def ffn_bwd(dm, ua, ub, cw, cb, *, tm=512, tn=256):
    s, f = dm.shape
    nj = f // tn
    ni = s // tm
    halo = 16

    def body(dm_ref, ua_ref, ub_ref, ha_ref, hb_ref, cwa_ref, cwb_ref, cba_ref, cbb_ref,
             dua_ref, dub_ref, ga_ref, gb_ref, carry):
        i = pl.program_id(1)

        @pl.when(i == 0)
        def _():
            carry[...] = jnp.zeros_like(carry)
            ga_ref[...] = jnp.zeros_like(ga_ref)
            gb_ref[...] = jnp.zeros_like(gb_ref)

        first_tile = i == ni - 1
        row = _row((tm, tn))
        dmv = dm_ref[...].astype(F32)
        us, taps, convs = [], [], []
        for u_ref, h_ref, cw_ref, cb_ref in ((ua_ref, ha_ref, cwa_ref, cba_ref), (ub_ref, hb_ref, cwb_ref, cbb_ref)):
            u = u_ref[...].astype(F32)
            before = jnp.where(first_tile, 0.0, h_ref[halo - 8:halo, :].astype(F32))
            r1, r2 = _causal_taps(u, before)
            us.append(u)
            taps.append((r1, r2))
            convs.append(cw_ref[0:1, :] * r2 + cw_ref[1:2, :] * r1 + cw_ref[2:3, :] * u + cb_ref[...])
        gelu, dgelu = _gelu_parts(convs[0])
        dcs = (dmv * convs[1] * dgelu, dmv * gelu)
        for k, (dc, cw_ref, du_ref, g_ref) in enumerate(((dcs[0], cwa_ref, dua_ref, ga_ref), (dcs[1], cwb_ref, dub_ref, gb_ref))):
            r1, r2 = taps[k]
            g_ref[0:1, :] += jnp.sum(dc * r2, axis=0, keepdims=True)
            g_ref[1:2, :] += jnp.sum(dc * r1, axis=0, keepdims=True)
            g_ref[2:3, :] += jnp.sum(dc * us[k], axis=0, keepdims=True)
            g_ref[3:4, :] += jnp.sum(dc, axis=0, keepdims=True)
            after = carry[k]
            n1 = jnp.where(row == tm - 1, after[0:1, :], pltpu.roll(dc, tm - 1, axis=0))
            n2 = jnp.where(row == tm - 2, after[0:1, :], jnp.where(row == tm - 1, after[1:2, :], pltpu.roll(dc, tm - 2, axis=0)))
            du_ref[...] = (cw_ref[2:3, :] * dc + cw_ref[1:2, :] * n1 + cw_ref[0:1, :] * n2).astype(BF16)
            carry[k] = dc[0:8, :]

    tile = pl.BlockSpec((tm, tn), lambda j, i: (ni - 1 - i, j))
    halo_spec = pl.BlockSpec((halo, tn), lambda j, i: (jnp.maximum((ni - 1 - i) * (tm // halo) - 1, 0), j))
    gspec = pl.BlockSpec((8, tn), lambda j, i: (0, j))
    return pl.pallas_call(
        body, grid=(nj, ni),
        in_specs=[tile, tile, tile, halo_spec, halo_spec,
                  pl.BlockSpec((3, tn), lambda j, i: (0, j)), pl.BlockSpec((3, tn), lambda j, i: (0, nj + j)),
                  pl.BlockSpec((1, tn), lambda j, i: (0, j)), pl.BlockSpec((1, tn), lambda j, i: (0, nj + j))],
        out_specs=[tile, tile, gspec, gspec],
        out_shape=[SDS((s, f), BF16), SDS((s, f), BF16), SDS((8, f), F32), SDS((8, f), F32)],
        scratch_shapes=[pltpu.VMEM((2, 8, tn), F32)], name="ffn_bwd",
        compiler_params=_cp(("parallel", "arbitrary")))(dm, ua, ub, ua, ub, cw, cw, cb, cb)


def adamw(w, g, m, v, *, name):
    r, c = w.shape
    tr = r
    for cand in (256, 128, 64, 32, 16, 8):
        if r % cand == 0:
            tr = cand
            break

    def body(w_ref, g_ref, m_ref, v_ref, d_ref, nm_ref, nv_ref):
        gv = g_ref[...]
        mn = ADAM_B1 * m_ref[...] + (1.0 - ADAM_B1) * gv
        vn = ADAM_B2 * v_ref[...] + (1.0 - ADAM_B2) * (gv * gv)
        m_hat = mn / (1.0 - ADAM_B1 ** ADAM_STEP)
        v_hat = vn / (1.0 - ADAM_B2 ** ADAM_STEP)
        d_ref[...] = -ADAM_LR * (m_hat / (jnp.sqrt(v_hat) + ADAM_EPS) + ADAM_WD * w_ref[...])
        nm_ref[...] = mn
        nv_ref[...] = vn

    blk = pl.BlockSpec((tr, c), lambda i: (i, 0))
    return pl.pallas_call(body, grid=(r // tr,), in_specs=[blk] * 4, out_specs=[blk] * 3, out_shape=[SDS((r, c), F32)] * 3,
                          name=name, compiler_params=_cp(("parallel",)))(w, g, m, v)


ANY = pl.BlockSpec(memory_space=pl.ANY)
ICI_KINDS = ("x", "y", "xy")


def _coords():
    return lax.axis_index("x"), lax.axis_index("y"), lax.axis_index("c")


def _peer(kind, x, y, c):
    if kind == "c":
        return (x, y, 1 - c)
    if kind == "x":
        return (1 - x, y, c)
    if kind == "y":
        return (x, 1 - y, c)
    return (1 - x, 1 - y, c)


def _chip_of(p):
    return 2 * p[0] + p[1]


def _half(rows, which):
    h = rows // 2
    return pl.ds(pl.multiple_of(which * h, 16), h)


def _remote(src, dst, send_sem, recv_sem, to):
    return pltpu.make_async_remote_copy(src_ref=src, dst_ref=dst, send_sem=send_sem, recv_sem=recv_sem,
                                        device_id=to, device_id_type=MESH)


def allgather_chips(shards, halved, *, name):
    n = len(shards)

    def body(*refs):
        ins, outs = refs[:n], refs[n:2 * n]
        send_sems, recv_sems = refs[2 * n:]
        x, y, c = _coords()
        me = (x, y, c)
        my_chip = 2 * x + y

        def rows(w, which):
            r = shards[w].shape[0]
            return _half(r, which) if halved[w] else pl.ds(0, r)

        first = []
        for w in range(n):
            for k, kind in enumerate(ICI_KINDS):
                cp = _remote(ins[w].at[rows(w, c)], outs[w].at[my_chip, rows(w, c)], send_sems.at[w, k], recv_sems.at[w, k],
                             _peer(kind, x, y, c))
                cp.start()
                first.append(cp)
        second = []
        for w in range(n):
            for k, kind in enumerate(ICI_KINDS):
                landed = outs[w].at[_chip_of(_peer(kind, x, y, c)), rows(w, c)]
                _remote(landed, landed, send_sems.at[w, k], recv_sems.at[w, k], me).wait_recv()
                if halved[w]:
                    cp = _remote(landed, landed, send_sems.at[w, 3 + k], recv_sems.at[w, 3 + k], _peer("c", x, y, c))
                    cp.start()
                    second.append(cp)
        for w in range(n):
            if halved[w]:
                for k, kind in enumerate(ICI_KINDS):
                    other = outs[w].at[_chip_of(_peer(kind, x, y, c)), rows(w, 1 - c)]
                    _remote(other, other, send_sems.at[w, 3 + k], recv_sems.at[w, 3 + k], me).wait_recv()
        for cp in first + second:
            cp.wait_send()

    return pl.pallas_call(
        body, in_specs=[ANY] * n, out_specs=[ANY] * n,
        out_shape=[SDS((4,) + a.shape, a.dtype) for a in shards],
        scratch_shapes=[pltpu.SemaphoreType.DMA((n, 6)), pltpu.SemaphoreType.DMA((n, 6))],
        name=name)(*shards)


def grads_to_sibling(gs, *, name):
    n = len(gs)

    def body(*refs):
        ins, outs = refs[:n], refs[n:2 * n]
        send_sems, recv_sems = refs[2 * n:]
        x, y, c = _coords()
        cps = []
        for w in range(n):
            cp = _remote(ins[w].at[:, _half(gs[w].shape[1], 1 - c)], outs[w], send_sems.at[w], recv_sems.at[w],
                         _peer("c", x, y, c))
            cp.start()
            cps.append(cp)
        for cp in cps:
            cp.wait()

    return pl.pallas_call(
        body, in_specs=[ANY] * n, out_specs=[ANY] * n,
        out_shape=[SDS((4, a.shape[1] // 2, a.shape[2]), a.dtype) for a in gs],
        scratch_shapes=[pltpu.SemaphoreType.DMA((n,)), pltpu.SemaphoreType.DMA((n,))], name=name)(*gs)


def grads_to_chips(ps, *, name):
    n = len(ps)

    def body(*refs):
        ins, outs = refs[:n], refs[n:2 * n]
        send_sems, recv_sems = refs[2 * n:]
        x, y, c = _coords()
        cps = []
        for w in range(n):
            for k, kind in enumerate(ICI_KINDS):
                to = _peer(kind, x, y, c)
                cp = _remote(ins[w].at[_chip_of(to)], outs[w].at[k], send_sems.at[w, k], recv_sems.at[w, k], to)
                cp.start()
                cps.append(cp)
        for cp in cps:
            cp.wait()

    return pl.pallas_call(
        body, in_specs=[ANY] * n, out_specs=[ANY] * n,
        out_shape=[SDS((3,) + a.shape[1:], a.dtype) for a in ps],
        scratch_shapes=[pltpu.SemaphoreType.DMA((n, 3)), pltpu.SemaphoreType.DMA((n, 3))], name=name)(*ps)


def halves_to_full(hs, *, name):
    n = len(hs)

    def body(*refs):
        ins, outs = refs[:n], refs[n:2 * n]
        send_sems, recv_sems = refs[2 * n:]
        x, y, c = _coords()
        cps = []
        for w in range(n):
            cp = _remote(ins[w], outs[w].at[_half(2 * hs[w].shape[0], c)], send_sems.at[w], recv_sems.at[w],
                         _peer("c", x, y, c))
            cp.start()
            cps.append(cp)
        for cp in cps:
            cp.wait()

    return pl.pallas_call(
        body, in_specs=[ANY] * n, out_specs=[ANY] * n,
        out_shape=[SDS((2 * a.shape[0], a.shape[1]), a.dtype) for a in hs],
        scratch_shapes=[pltpu.SemaphoreType.DMA((n,)), pltpu.SemaphoreType.DMA((n,))],
        name=name)(*hs)


def _row_tile(rows):
    for cand in (256, 192, 176, 128, 64, 32, 16):
        if rows % cand == 0:
            return cand
    return rows


def chip_sum(g, recv, c_arr, *, name):
    _, r, cols = g.shape
    h = r // 2
    tr = _row_tile(h)
    nblk = h // tr

    def body(c_ref, g_ref, r_ref, f_ref, b_ref):
        tot = g_ref[...] + r_ref[...]
        f_ref[...] = tot
        b_ref[...] = tot.astype(BF16)

    blk = pl.BlockSpec((None, tr, cols), lambda j, i, c_ref: (j, i, 0))
    grid_spec = pltpu.PrefetchScalarGridSpec(
        num_scalar_prefetch=1, grid=(4, nblk),
        in_specs=[pl.BlockSpec((None, tr, cols), lambda j, i, c_ref: (j, c_ref[0] * nblk + i, 0)), blk],
        out_specs=[blk, blk])
    return pl.pallas_call(body, grid_spec=grid_spec, out_shape=[SDS((4, h, cols), F32), SDS((4, h, cols), BF16)],
                          name=name, compiler_params=_cp(("parallel", "parallel")))(c_arr, g, recv)


def final_sum(pf, recv, chip_arr, *, name):
    _, h, cols = pf.shape
    tr = _row_tile(h)

    def body(chip_ref, p_ref, r_ref, o_ref):
        o_ref[...] = ((p_ref[...] + r_ref[0].astype(F32)) + r_ref[1].astype(F32)) + r_ref[2].astype(F32)

    grid_spec = pltpu.PrefetchScalarGridSpec(
        num_scalar_prefetch=1, grid=(h // tr,),
        in_specs=[pl.BlockSpec((None, tr, cols), lambda i, chip_ref: (chip_ref[0], i, 0)),
                  pl.BlockSpec((3, tr, cols), lambda i, chip_ref: (0, i, 0))],
        out_specs=pl.BlockSpec((tr, cols), lambda i, chip_ref: (i, 0)))
    return pl.pallas_call(body, grid_spec=grid_spec, out_shape=SDS((h, cols), F32), name=name,
                          compiler_params=_cp(("parallel",)))(chip_arr, pf, recv)


def allreduce_small(v, *, name):
    rws, cols = v.shape

    def body(v_ref, all_ref, sum_ref, send_sems, recv_sems, local_sem):
        x, y, c = _coords()
        me, sibling = (x, y, c), (x, y, 1 - c)
        chips = [(1 - x, y), (x, 1 - y), (1 - x, 1 - y)]

        def rows(px, py, pc):
            return all_ref.at[pl.ds(pl.multiple_of((4 * px + 2 * py + pc) * rws, 8), rws), :]

        def copy(k, block, to, src=None):
            return _remote(rows(*block) if src is None else src, rows(*block), send_sems.at[k], recv_sems.at[k], to)

        mine = pltpu.make_async_copy(v_ref, rows(*me), local_sem)
        mine.start()
        first = [copy(0, me, sibling, src=v_ref)]
        first += [copy(1 + j, me, (*chip, c), src=v_ref) for j, chip in enumerate(chips)]
        for cp in first:
            cp.start()
        passed = [copy(4 + j, (*chip, c), sibling) for j, chip in enumerate(chips)]
        for j, chip in enumerate(chips):
            copy(1 + j, (*chip, c), me).wait_recv()
            passed[j].start()
        copy(0, sibling, me).wait_recv()
        for j, chip in enumerate(chips):
            copy(4 + j, (*chip, 1 - c), me).wait_recv()
        for cp in first + passed:
            cp.wait_send()
        mine.wait()
        tot = all_ref[0:rws, :]
        for dev in range(1, 8):
            tot = tot + all_ref[dev * rws:(dev + 1) * rws, :]
        sum_ref[...] = tot

    vm = pl.BlockSpec(memory_space=pltpu.VMEM)
    return pl.pallas_call(
        body, in_specs=[vm], out_specs=[vm, vm],
        out_shape=[SDS((8 * rws, cols), v.dtype), SDS((rws, cols), v.dtype)],
        scratch_shapes=[pltpu.SemaphoreType.DMA((7,)), pltpu.SemaphoreType.DMA((7,)), pltpu.SemaphoreType.DMA],
        name=name)(v)[1]


def _pack_rows(parts, rows):
    out = []
    for a, r in zip(parts, rows):
        flat = a.reshape(-1)
        flat = jnp.pad(flat, (0, r * LANES - flat.shape[0]))
        out.append(flat.reshape(r, LANES))
    return jnp.concatenate(out, axis=0)


def _unpack_rows(packed, shapes, rows):
    out, at = [], 0
    for shp, r in zip(shapes, rows):
        size = int(np.prod(shp))
        out.append(packed[at:at + r].reshape(-1)[:size].reshape(shp))
        at += r
    return out


def kernel(x, g_pre_mix, w_in, b_forget, w_o_fox, w_o_dil, w_out, g_post_mix, g_pre_ffn, w_up, conv_w, conv_b, w_down, g_post_ffn, loss_target, m_g_pre_mix, m_w_in, m_b_forget, m_w_o_fox, m_w_o_dil, m_w_out, m_g_post_mix, m_g_pre_ffn, m_w_up, m_conv_w, m_conv_b, m_w_down, m_g_post_ffn, v_g_pre_mix, v_w_in, v_b_forget, v_w_o_fox, v_w_o_dil, v_w_out, v_g_post_mix, v_g_pre_ffn, v_w_up, v_conv_w, v_conv_b, v_w_down, v_g_post_ffn):
    xi, yi, ci = _coords()
    chip = 2 * xi + yi
    c_arr = jnp.reshape(ci, (1,)).astype(jnp.int32)
    chip_arr = jnp.reshape(chip, (1,)).astype(jnp.int32)
    xs = x[0]
    target = loss_target[0]
    s, d = xs.shape
    f_half = w_down.shape[1] * 4
    cols_in = w_in.shape[2]

    big = (w_in, w_o_fox, w_o_dil, w_out, w_up, w_down)
    shards = [w[0].astype(BF16) for w in big] + [conv_w[0]]
    gathered = allgather_chips(shards, [True] * 6 + [False], name="allgather_weights")
    a_in, a_of, a_od, a_out, a_up, a_down, a_cw = [
        lax.dynamic_update_index_in_dim(a4, own, chip, 0) for a4, own in zip(gathered, shards)]
    w_in_full = jnp.concatenate([a_in[j] for j in range(4)], axis=1)
    nf = N_HEADS
    e_a, e_b = 3 * ATT_W, 3 * ATT_W + nf
    wz = jnp.concatenate([w_in_full[:, :e_a], w_in_full[:, e_b:]], axis=1)
    wf = jnp.pad(w_in_full[:, e_a:e_b], ((0, 0), (0, LANES - nf)))
    wo_a = jnp.concatenate([a_of[j] for j in range(4)], axis=1)
    wo_b = jnp.concatenate([a_od[j] for j in range(4)], axis=1)
    w_o = a_out.reshape(d, d)
    w_dn = a_down.reshape(f_half, d)
    wu_a = jnp.concatenate([a_up[0], a_up[1]], axis=1)
    wu_b = jnp.concatenate([a_up[2], a_up[3]], axis=1)
    cw = jnp.concatenate([a_cw[j] for j in range(4)], axis=1)
    cb = conv_b
    bfo = jnp.pad(b_forget, ((0, 0), (0, LANES - nf)))

    h1 = rmsnorm_fwd(xs, g_pre_mix)
    z = mm([(h1, d, 0)], [(wz, d, 0)], nt=False, out_dtype=BF16, tm=1024, tn=512, name="in_proj")
    fa = mm([(h1, d, 0)], [(wf, d, 0)], nt=False, out_dtype=F32, tm=1024, tn=LANES, name="in_proj_forget")
    q_aug, k_aug = fox_prep(z, fa, bfo)
    ya, lse_a = fox_fwd(q_aug, k_aug, z)
    qkv_b = rope_apply([(z, Z_QB, QK_SCALE, True), (z, Z_KB, 1.0, True), (z, Z_VB, 1.0, False)], rope_tables(s, 1.0),
                       out_dtype=F32, name="rope_fwd")
    yb, lse_b = dil_fwd_all(qkv_b)
    pa, pb, mixed = gate_mix(ya, yb, wo_a, wo_b, z)
    y1, x1 = mm_rms_res(mixed, w_o, g_post_mix, xs, name="out_proj")
    h2 = rmsnorm_fwd(x1, g_pre_ffn)
    ua, ub, mid = ffn_up(h2, wu_a, wu_b, cw, cb)
    y2, dout, sq = mm_rms_res(mid, w_dn, g_post_ffn, x1, target, name="down_proj")
    loss = lax.psum(0.5 * sq[0, 0] / d, ("x", "y", "c"))

    dy2, gg_post_ffn = rmsnorm_bwd(dout, y2, g_post_ffn, None, out_dtype=BF16, name="norm_bwd_post_ffn")
    dmid = mm([(dy2, d, 0)], [(w_dn, d, 0)], nt=True, out_dtype=BF16, tm=512, tn=256, name="down_dgrad")
    dw_down = wgrad((mid, f_half, 0), dy2, tk=f_half // 2, tn=512, ts=512, name="down_wgrad")
    dua, dub, gc_a, gc_b = ffn_bwd(dmid, ua, ub, cw, cb)
    dh2 = mm([(dua, f_half, 0), (dub, f_half, 0)], [(wu_a, f_half, 0), (wu_b, f_half, 0)], nt=True, out_dtype=BF16,
             tm=512, tn=512, name="up_dgrad")
    dw_up = jnp.concatenate(
        [wgrad((h2, d, 0), du, tk=512, tn=f_half // 2, ts=512, name=f"up_wgrad_{k}", chip_major=True)
         for k, du in enumerate((dua, dub))], axis=0)
    dx1, gg_pre_ffn = rmsnorm_bwd(dh2, x1, g_pre_ffn, dout, out_dtype=F32, name="norm_bwd_pre_ffn")
    dy1, gg_post_mix = rmsnorm_bwd(dx1, y1, g_post_mix, None, out_dtype=BF16, name="norm_bwd_post_mix")
    dmixed = mm([(dy1, d, 0)], [(w_o, d, 0)], nt=True, out_dtype=BF16, tm=512, tn=512, name="out_dgrad")
    dw_out = wgrad((mixed, d, 0), dy1, tk=512, tn=512, ts=512, name="out_wgrad")
    dpa, dpb, dz_g = gate_bwd(dmixed, z, pa, pb)
    dya = mm([(dpa, d, 0)], [(wo_a, d, 0)], nt=True, out_dtype=BF16, tm=512, tn=ATT_W, name="fox_o_dgrad")
    dyb = mm([(dpb, d, 0)], [(wo_b, d, 0)], nt=True, out_dtype=F32, tm=512, tn=ATT_W, name="dil_o_dgrad")
    dw_of = wgrad((ya, ATT_W, 0), dpa, tk=ATT_W, tn=d // 4, ts=512, name="fox_o_wgrad", chip_major=True)
    dw_od = wgrad((yb, ATT_W, 0), dpb, tk=ATT_W, tn=d // 4, ts=512, name="dil_o_wgrad", chip_major=True)
    dd_a = head_rowsum(dya, ya, name="fox_delta")
    dq_aug, dk_aug, dv_a = fox_bwd(q_aug, k_aug, z, dya, lse_a, dd_a)
    dz_a, dfa, gg_bf = fox_post(dq_aug, dk_aug, dv_a, fa, bfo)
    dd_b = head_rowsum(dyb, yb, name="dil_delta")
    dq_b, dk_b, dv_b = dil_bwd_all(qkv_b, dyb, lse_b, dd_b)
    dz_b = rope_apply([(dq_b, 0, QK_SCALE, True), (dk_b, 0, 1.0, True), (dv_b, 0, 1.0, False)],
                      rope_tables(s, -1.0), out_dtype=BF16, name="rope_bwd")
    dh1 = mm([(dz_a, e_a, 0), (dz_b, e_a, 0), (dz_g, d, 0), (dz_g, d, 1), (dfa, LANES, 0)],
             [(wz, e_a, 0), (wz, e_a, 1), (wz, d, 3), (wz, d, 4), (wf, LANES, 0)], nt=True, out_dtype=BF16,
             tm=512, tn=512, name="in_dgrad")
    dw_a = wgrad((h1, d, 0), dz_a, tk=512, tn=512, ts=512, name="in_wgrad_a")
    dw_b = wgrad((h1, d, 0), dz_b, tk=512, tn=512, ts=512, name="in_wgrad_b")
    dw_g = wgrad((h1, d, 0), dz_g, tk=512, tn=512, ts=512, name="in_wgrad_g")
    dw_f = wgrad((h1, d, 0), dfa, tk=512, tn=LANES, ts=512, name="in_wgrad_f")
    grad_x, gg_pre_mix = rmsnorm_bwd(dh1, xs, g_pre_mix, dx1, out_dtype=F32, name="norm_bwd_pre_mix")
    dw_in_full = jnp.concatenate([dw_a, dw_f[:, :nf], dw_b, dw_g], axis=1)
    dw_in = jnp.stack([dw_in_full[:, j * cols_in:(j + 1) * cols_in] for j in range(4)], axis=0)

    gs = [dw_in, dw_of, dw_od, dw_out.reshape(4, d // 4, d), dw_up, dw_down.reshape(4, f_half // 4, d)]
    names = ("w_in", "w_o_fox", "w_o_dil", "w_out", "w_up", "w_down")
    from_sib = grads_to_sibling(gs, name="grads_to_sibling")
    sums = [chip_sum(g, r, c_arr, name=f"chip_sum_{nm}") for g, r, nm in zip(gs, from_sib, names)]
    from_chips = grads_to_chips([p[1] for p in sums], name="grads_to_chips")
    halves = [final_sum(p[0], r, chip_arr, name=f"final_sum_{nm}") for p, r, nm in zip(sums, from_chips, names)]
    from_half = halves_to_full(halves, name="halves_to_full")
    g_big = [lax.dynamic_update_slice_in_dim(full, mine, ci * mine.shape[0], axis=0) for full, mine in zip(from_half, halves)]
    upd_big = [adamw(w[0], g, m[0], v[0], name=f"adamw_{nm}") for w, g, m, v, nm in zip(
        big, g_big, (m_w_in, m_w_o_fox, m_w_o_dil, m_w_out, m_w_up, m_w_down),
        (v_w_in, v_w_o_fox, v_w_o_dil, v_w_out, v_w_up, v_w_down), names)]

    g_cw_loc = jnp.concatenate([gc_a[0:3], gc_b[0:3]], axis=1)
    g_cb_loc = jnp.concatenate([gc_a[3:4], gc_b[3:4]], axis=1)
    small_loc = [gg_pre_mix, gg_post_mix, gg_pre_ffn, gg_post_ffn, g_cb_loc, gg_bf[:, :nf], g_cw_loc]
    red_rows = (8, 8, 8, 8, 48, 8, 136)
    red = allreduce_small(_pack_rows(small_loc, red_rows), name="allreduce_small")
    g_pm, g_qm, g_pf, g_qf, g_cb, g_bf, g_cw_full = _unpack_rows(red, [a.shape for a in small_loc], red_rows)
    cols_cw = conv_w.shape[2]
    g_cw = lax.dynamic_slice_in_dim(g_cw_full, chip * cols_cw, cols_cw, axis=1)
    small_w = (g_pre_mix, g_post_mix, g_pre_ffn, g_post_ffn, conv_b, b_forget, conv_w[0])
    small_m = (m_g_pre_mix, m_g_post_mix, m_g_pre_ffn, m_g_post_ffn, m_conv_b, m_b_forget, m_conv_w[0])
    small_v = (v_g_pre_mix, v_g_post_mix, v_g_pre_ffn, v_g_post_ffn, v_conv_b, v_b_forget, v_conv_w[0])
    small_g = (g_pm, g_qm, g_pf, g_qf, g_cb, g_bf, g_cw)
    ad_rows = (8, 8, 8, 8, 48, 8, 40)
    packed = [_pack_rows(t, ad_rows) for t in (small_w, small_g, small_m, small_v)]
    upd_small = [_unpack_rows(o, [a.shape for a in small_w], ad_rows) for o in adamw(*packed, name="adamw_small")]

    order = ("g_pre_mix", "w_in", "b_forget", "w_o_fox", "w_o_dil", "w_out", "g_post_mix", "g_pre_ffn", "w_up", "conv_w",
             "conv_b", "w_down", "g_post_ffn")
    small_names = ("g_pre_mix", "g_post_mix", "g_pre_ffn", "g_post_ffn", "conv_b", "b_forget", "conv_w")
    grads, deltas, new_ms, new_vs = {}, {}, {}, {}
    for k, nm in enumerate(names):
        grads[nm] = g_big[k][None]
        deltas[nm], new_ms[nm], new_vs[nm] = (a[None] for a in upd_big[k])
    for k, nm in enumerate(small_names):
        lead = (lambda a: a[None]) if nm == "conv_w" else (lambda a: a)
        grads[nm] = lead(small_g[k])
        deltas[nm], new_ms[nm], new_vs[nm] = (lead(upd_small[j][k]) for j in range(3))
    return (loss, grad_x[None], *[grads[nm] for nm in order], *[deltas[nm] for nm in order],
            *[new_ms[nm] for nm in order], *[new_vs[nm] for nm in order])
```

```python
import functools
import math

import numpy as np
import jax
import jax.numpy as jnp
from jax import lax
from jax.experimental import pallas as pl
from jax.experimental.pallas import tpu as pltpu

F32 = jnp.float32
BF16 = jnp.bfloat16
SDS = jax.ShapeDtypeStruct
MESH = pl.DeviceIdType.MESH

HEAD_DIM = 64
N_HEADS = 8
LANES = 128
ATT_W = N_HEADS * HEAD_DIM
DIL_PATTERNS = ((128, 1), (512, 4), (2048, 16))
DIL_BLK = 128
ROPE_DIM = HEAD_DIM // 4
ROPE_THETA = 500000.0
RMS_EPS = 1e-6
NEG = -1e30
QK_SCALE = 1.0 / math.sqrt(HEAD_DIM)
ADAM_LR, ADAM_B1, ADAM_B2, ADAM_EPS, ADAM_WD, ADAM_STEP = 0.001, 0.9, 0.999, 1e-08, 0.01, 10
VMEM_LIMIT = 56 * 1024 * 1024

Z_QA, Z_KA, Z_VA, Z_QB, Z_KB, Z_VB = 0, 1, 2, 3, 4, 5
Z_W = 5120


def _cp(sem):
    return pltpu.CompilerParams(dimension_semantics=sem, vmem_limit_bytes=VMEM_LIMIT)


def _nt(a, b):
    return lax.dot_general(a, b, (((1,), (1,)), ((), ())), preferred_element_type=F32)


def _tn(a, b):
    return lax.dot_general(a, b, (((0,), (0,)), ((), ())), preferred_element_type=F32)


def _nn(a, b):
    return jnp.dot(a, b, preferred_element_type=F32)


def _lane(shape):
    return lax.broadcasted_iota(jnp.int32, shape, 1)


def _row(shape):
    return lax.broadcasted_iota(jnp.int32, shape, 0)


def rmsnorm_fwd(x, g, *, tm=512):
    s, d = x.shape

    def body(x_ref, g_ref, h_ref):
        xv = x_ref[...]
        inv = lax.rsqrt(jnp.mean(xv * xv, axis=-1, keepdims=True) + RMS_EPS)
        h_ref[...] = (xv * inv * g_ref[...]).astype(h_ref.dtype)

    return pl.pallas_call(
        body, grid=(s // tm,),
        in_specs=[pl.BlockSpec((tm, d), lambda i: (i, 0)), pl.BlockSpec((1, d), lambda i: (0, 0))],
        out_specs=pl.BlockSpec((tm, d), lambda i: (i, 0)),
        out_shape=SDS((s, d), BF16), name="rmsnorm_fwd", compiler_params=_cp(("parallel",)))(x, g)


def rmsnorm_bwd(dh, x, g, res, *, out_dtype, tm=256, name):
    s, d = x.shape
    n = s // tm
    has_res = res is not None

    def body(*refs):
        if has_res:
            dh_ref, x_ref, g_ref, res_ref, dx_ref, dg_ref, acc = refs
        else:
            dh_ref, x_ref, g_ref, dx_ref, dg_ref, acc = refs
        i = pl.program_id(0)

        @pl.when(i == 0)
        def _():
            acc[...] = jnp.zeros_like(acc)

        xv = x_ref[...]
        inv = lax.rsqrt(jnp.mean(xv * xv, axis=-1, keepdims=True) + RMS_EPS)
        xh = xv * inv
        dhv = dh_ref[...].astype(F32)
        dxh = dhv * g_ref[...]
        dot = jnp.mean(dxh * xh, axis=-1, keepdims=True)
        dx = inv * (dxh - xh * dot)
        if has_res:
            dx = dx + res_ref[...]
        dx_ref[...] = dx.astype(dx_ref.dtype)
        acc[...] += jnp.sum((dhv * xh).reshape(tm // 8, 8, d), axis=0)

        @pl.when(i == n - 1)
        def _():
            dg_ref[...] = jnp.sum(acc[...], axis=0, keepdims=True)

    row = pl.BlockSpec((tm, d), lambda i: (i, 0))
    in_specs = [row, row, pl.BlockSpec((1, d), lambda i: (0, 0))] + ([row] if has_res else [])
    args = [dh, x, g] + ([res] if has_res else [])
    return pl.pallas_call(
        body, grid=(n,), in_specs=in_specs,
        out_specs=[row, pl.BlockSpec((1, d), lambda i: (0, 0))],
        out_shape=[SDS((s, d), out_dtype), SDS((1, d), F32)],
        scratch_shapes=[pltpu.VMEM((8, d), F32)],
        name=name, compiler_params=_cp(("arbitrary",)))(*args)


def mm(a_views, b_views, *, nt, out_dtype, tm, tn, name):
    n_p = len(a_views)
    m = a_views[0][0].shape[0]
    n = b_views[0][0].shape[0] if nt else b_views[0][0].shape[1]

    def body(*refs):
        o_ref = refs[-1]
        acc = None
        for p in range(n_p):
            av = refs[p][...].astype(BF16)
            bv = refs[n_p + p][...].astype(BF16)
            dv = _nt(av, bv) if nt else _nn(av, bv)
            acc = dv if acc is None else acc + dv
        o_ref[...] = acc.astype(o_ref.dtype)

    in_specs = []
    for arr, w, blk in a_views:
        in_specs.append(pl.BlockSpec((tm, w), functools.partial(lambda i, j, blk: (i, blk), blk=blk)))
    for arr, w, blk in b_views:
        if nt:
            in_specs.append(pl.BlockSpec((tn, w), functools.partial(lambda i, j, blk: (j, blk), blk=blk)))
        else:
            in_specs.append(pl.BlockSpec((w, tn), lambda i, j: (0, j)))
    return pl.pallas_call(
        body, grid=(m // tm, n // tn), in_specs=in_specs,
        out_specs=pl.BlockSpec((tm, tn), lambda i, j: (i, j)),
        out_shape=SDS((m, n), out_dtype), name=name,
        compiler_params=_cp(("parallel", "parallel")))(*[a[0] for a in a_views], *[b[0] for b in b_views])


def wgrad(a_view, g, *, tk, tn, ts, name, chip_major=False):
    arr, ka, blk = a_view
    s, n = g.shape
    ns = s // ts

    def body(a_ref, g_ref, o_ref):
        @pl.when(pl.program_id(2) == 0)
        def _():
            o_ref[...] = jnp.zeros_like(o_ref)

        o_ref[...] += _tn(a_ref[...].astype(BF16), g_ref[...].astype(BF16))

    if chip_major:
        out_spec = pl.BlockSpec((None, tk, tn), lambda i, j, k: (j, i, 0))
        out_shape = SDS((n // tn, ka, tn), F32)
    else:
        out_spec = pl.BlockSpec((tk, tn), lambda i, j, k: (i, j))
        out_shape = SDS((ka, n), F32)
    return pl.pallas_call(
        body, grid=(ka // tk, n // tn, ns),
        in_specs=[pl.BlockSpec((ts, tk), lambda i, j, k: (k, blk * (ka // tk) + i)),
                  pl.BlockSpec((ts, tn), lambda i, j, k: (k, j))],
        out_specs=out_spec, out_shape=out_shape, name=name,
        compiler_params=_cp(("parallel", "parallel", "arbitrary")))(arr, g)


def mm_rms_res(a, w, g, xres, target=None, *, tm=256, name):
    s, k = a.shape
    d = w.shape[1]
    n = s // tm
    with_loss = target is not None

    def body(*refs):
        if with_loss:
            a_ref, w_ref, g_ref, x_ref, t_ref, y_ref, o_ref, l_ref = refs
        else:
            a_ref, w_ref, g_ref, x_ref, y_ref, o_ref = refs
        y = _nn(a_ref[...], w_ref[...])
        inv = lax.rsqrt(jnp.mean(y * y, axis=-1, keepdims=True) + RMS_EPS)
        xn = x_ref[...] + y * inv * g_ref[...]
        y_ref[...] = y
        if with_loss:
            err = xn - t_ref[...]
            o_ref[...] = err * (1.0 / d)

            @pl.when(pl.program_id(0) == 0)
            def _():
                l_ref[...] = jnp.zeros_like(l_ref)

            l_ref[...] += jnp.sum(jnp.sum(err * err, axis=1, keepdims=True), axis=0, keepdims=True)
        else:
            o_ref[...] = xn

    row = pl.BlockSpec((tm, d), lambda i: (i, 0))
    in_specs = [pl.BlockSpec((tm, k), lambda i: (i, 0)), pl.BlockSpec((k, d), lambda i: (0, 0)),
                pl.BlockSpec((1, d), lambda i: (0, 0)), row]
    out_specs = [row, row]
    out_shape = [SDS((s, d), F32), SDS((s, d), F32)]
    args = [a, w, g, xres]
    if with_loss:
        in_specs.append(row)
        out_specs.append(pl.BlockSpec((1, 1), lambda i: (0, 0)))
        out_shape.append(SDS((1, 1), F32))
        args.append(target)
    return pl.pallas_call(
        body, grid=(n,), in_specs=in_specs, out_specs=out_specs, out_shape=out_shape, name=name,
        compiler_params=_cp(("arbitrary",)))(*args)


def _split3(v):
    hi = v.astype(BF16).astype(F32)
    r = v - hi
    mid = r.astype(BF16).astype(F32)
    lo = (r - mid).astype(BF16).astype(F32)
    return hi, mid, lo


def _tri(n, upper):
    r = np.arange(n)
    m = (r[:, None] <= r[None, :]) if upper else (r[:, None] >= r[None, :])
    return jnp.asarray(m.astype(np.float32))


def fox_prep(z, fa, bfo, *, tb=512):
    s = z.shape[0]
    n = s // tb

    def body(q_ref, k_ref, fa_ref, b_ref, tri_ref, qa_ref, ka_ref, carry):
        @pl.when(pl.program_id(0) == 0)
        def _():
            carry[...] = jnp.zeros_like(carry)

        xv = fa_ref[...] + b_ref[...]
        logf = jnp.minimum(xv, 0.0) - jnp.log(1.0 + jnp.exp(-jnp.abs(xv)))
        csum = jnp.dot(tri_ref[...], logf, preferred_element_type=F32, precision=lax.Precision.HIGHEST) + carry[0:1, :]
        carry[0:1, :] = csum[tb - 1:tb, :]
        lane = _lane((tb, LANES))
        for h in range(N_HEADS):
            hi, mid, lo = _split3(csum[:, h:h + 1])
            pair = (h // 2) * LANES
            qv = q_ref[:, pair:pair + LANES].astype(F32)
            kv = k_ref[:, pair:pair + LANES].astype(F32)
            if h % 2:
                qv = pltpu.roll(qv, 64, axis=1)
                kv = pltpu.roll(kv, 64, axis=1)
            one = jnp.where((lane >= 67) & (lane < 70), 1.0, 0.0)
            q_x = jnp.where(lane == 64, hi, jnp.where(lane == 65, mid, jnp.where(lane == 66, lo, one)))
            one = jnp.where((lane >= 64) & (lane < 67), 1.0, 0.0)
            k_x = jnp.where(lane == 67, -hi, jnp.where(lane == 68, -mid, jnp.where(lane == 69, -lo, one)))
            qa_ref[:, h * LANES:(h + 1) * LANES] = jnp.where(lane < 64, qv * QK_SCALE, q_x).astype(BF16)
            ka_ref[:, h * LANES:(h + 1) * LANES] = jnp.where(lane < 64, kv, k_x).astype(BF16)

    return pl.pallas_call(
        body, grid=(n,),
        in_specs=[pl.BlockSpec((tb, ATT_W), lambda i: (i, Z_QA)), pl.BlockSpec((tb, ATT_W), lambda i: (i, Z_KA)),
                  pl.BlockSpec((tb, LANES), lambda i: (i, 0)), pl.BlockSpec((1, LANES), lambda i: (0, 0)),
                  pl.BlockSpec((tb, tb), lambda i: (0, 0))],
        out_specs=[pl.BlockSpec((tb, N_HEADS * LANES), lambda i: (i, 0))] * 2,
        out_shape=[SDS((s, N_HEADS * LANES), BF16)] * 2,
        scratch_shapes=[pltpu.VMEM((8, LANES), F32)],
        name="fox_prep", compiler_params=_cp(("arbitrary",)))(z, z, fa, bfo, _tri(tb, False))


def _causal_pairs(n, k_major):
    if k_major:
        pairs = [(qi, kj) for kj in range(n) for qi in range(kj, n)]
    else:
        pairs = [(qi, kj) for qi in range(n) for kj in range(qi + 1)]
    return (jnp.asarray([p[0] for p in pairs], jnp.int32), jnp.asarray([p[1] for p in pairs], jnp.int32), len(pairs))


def fox_fwd(q_aug, k_aug, z, *, t=512):
    s = z.shape[0]
    qi_arr, kj_arr, n_pairs = _causal_pairs(s // t, False)

    def body(qi_ref, kj_ref, q_ref, k_ref, v_ref, o_ref, lse_ref, m_scr, l_scr, acc_scr):
        step = pl.program_id(1)
        qi = qi_ref[step]
        kj = kj_ref[step]

        @pl.when(kj == 0)
        def _():
            m_scr[...] = jnp.full_like(m_scr, NEG)
            l_scr[...] = jnp.zeros_like(l_scr)
            acc_scr[...] = jnp.zeros_like(acc_scr)

        def update(masked):
            for i in range(2):
                sc = _nt(q_ref[:, i * LANES:(i + 1) * LANES], k_ref[:, i * LANES:(i + 1) * LANES])
                if masked:
                    sc = jnp.where(_row((t, t)) >= _lane((t, t)), sc, NEG)
                m_prev = m_scr[i]
                m_new = jnp.maximum(m_prev, jnp.max(sc, axis=-1, keepdims=True))
                alpha = jnp.exp(m_prev - m_new)
                p = jnp.exp(sc - jnp.tile(m_new, (1, t // LANES)))
                l_scr[i] = alpha * l_scr[i] + jnp.sum(p, axis=-1, keepdims=True)
                acc_scr[i] = alpha * acc_scr[i] + _nn(p.astype(BF16), v_ref[...])
                m_scr[i] = m_new

        @pl.when(kj < qi)
        def _():
            update(False)

        @pl.when(kj == qi)
        def _():
            update(True)
            lane = _lane((t, LANES))
            o_ref[...] = jnp.where(lane < 64, acc_scr[0] / l_scr[0], acc_scr[1] / l_scr[1]).astype(o_ref.dtype)
            lse_ref[...] = jnp.where(lane < 64, m_scr[0] + jnp.log(l_scr[0]), m_scr[1] + jnp.log(l_scr[1]))

    grid_spec = pltpu.PrefetchScalarGridSpec(
        num_scalar_prefetch=2, grid=(4, n_pairs),
        in_specs=[pl.BlockSpec((t, 2 * LANES), lambda hp, st, qi, kj: (qi[st], hp)),
                  pl.BlockSpec((t, 2 * LANES), lambda hp, st, qi, kj: (kj[st], hp)),
                  pl.BlockSpec((t, LANES), lambda hp, st, qi, kj: (kj[st], 4 * Z_VA + hp))],
        out_specs=[pl.BlockSpec((t, LANES), lambda hp, st, qi, kj: (qi[st], hp))] * 2,
        scratch_shapes=[pltpu.VMEM((2, t, LANES), F32)] * 3)
    return pl.pallas_call(
        body, grid_spec=grid_spec, out_shape=[SDS((s, ATT_W), BF16), SDS((s, ATT_W), F32)],
        name="fox_fwd", compiler_params=_cp(("parallel", "arbitrary")))(qi_arr, kj_arr, q_aug, k_aug, z)


def fox_bwd(q_aug, k_aug, z, dy, lse, dd, *, t=512):
    s = z.shape[0]
    qi_arr, kj_arr, n_pairs = _causal_pairs(s // t, True)

    def body(qi_ref, kj_ref, q_ref, k_ref, v_ref, do_ref, lse_ref, dd_ref, dq_ref, dk_ref, dv_ref):
        step = pl.program_id(1)
        qi = qi_ref[step]
        kj = kj_ref[step]

        @pl.when(step == 0)
        def _():
            dq_ref[...] = jnp.zeros_like(dq_ref)

        @pl.when(qi == kj)
        def _():
            dk_ref[...] = jnp.zeros_like(dk_ref)
            dv_ref[...] = jnp.zeros_like(dv_ref)

        def update(masked):
            lane = _lane((t, LANES))
            rows = pl.ds(pl.multiple_of(qi * t, t), t)
            dov = do_ref[...]
            dv_new = None
            for i in range(2):
                head = (lane < 64) if i == 0 else (lane >= 64)
                qv = q_ref[:, i * LANES:(i + 1) * LANES]
                kv = k_ref[:, i * LANES:(i + 1) * LANES]
                sc = _nt(qv, kv)
                if masked:
                    sc = jnp.where(_row((t, t)) >= _lane((t, t)), sc, NEG)
                p = jnp.exp(sc - lse_ref[:, i * 64:i * 64 + 1])
                dp = _nt(jnp.where(head, dov, jnp.zeros_like(dov)), v_ref[...])
                ds = (p * (dp - dd_ref[:, i * 64:i * 64 + 1])).astype(BF16)
                dq_ref[rows, i * LANES:(i + 1) * LANES] += _nn(ds, kv)
                dk_ref[:, i * LANES:(i + 1) * LANES] += _tn(ds, qv)
                dvi = _tn(p.astype(BF16), dov)
                dv_new = dvi if dv_new is None else jnp.where(head, dvi, dv_new)
            dv_ref[...] += dv_new

        @pl.when(kj < qi)
        def _():
            update(False)

        @pl.when(kj == qi)
        def _():
            update(True)

    grid_spec = pltpu.PrefetchScalarGridSpec(
        num_scalar_prefetch=2, grid=(4, n_pairs),
        in_specs=[pl.BlockSpec((t, 2 * LANES), lambda hp, st, qi, kj: (qi[st], hp)),
                  pl.BlockSpec((t, 2 * LANES), lambda hp, st, qi, kj: (kj[st], hp)),
                  pl.BlockSpec((t, LANES), lambda hp, st, qi, kj: (kj[st], 4 * Z_VA + hp)),
                  pl.BlockSpec((t, LANES), lambda hp, st, qi, kj: (qi[st], hp)),
                  pl.BlockSpec((t, LANES), lambda hp, st, qi, kj: (qi[st], hp)),
                  pl.BlockSpec((t, LANES), lambda hp, st, qi, kj: (qi[st], hp))],
        out_specs=[pl.BlockSpec((s, 2 * LANES), lambda hp, st, qi, kj: (0, hp)),
                   pl.BlockSpec((t, 2 * LANES), lambda hp, st, qi, kj: (kj[st], hp)),
                   pl.BlockSpec((t, LANES), lambda hp, st, qi, kj: (kj[st], hp))])
    return pl.pallas_call(
        body, grid_spec=grid_spec,
        out_shape=[SDS((s, N_HEADS * LANES), F32), SDS((s, N_HEADS * LANES), F32), SDS((s, ATT_W), F32)],
        name="fox_bwd", compiler_params=_cp(("parallel", "arbitrary")))(qi_arr, kj_arr, q_aug, k_aug, z, dy, lse, dd)


def head_rowsum(a, b, *, tm=512, name):
    s = a.shape[0]

    def body(a_ref, b_ref, o_ref):
        prod = a_ref[...].astype(F32) * b_ref[...].astype(F32)
        lane = _lane((tm, LANES))
        lo = jnp.sum(jnp.where(lane < 64, prod, 0.0), axis=-1, keepdims=True)
        hi = jnp.sum(jnp.where(lane >= 64, prod, 0.0), axis=-1, keepdims=True)
        o_ref[...] = jnp.where(lane < 64, lo, hi)

    blk = pl.BlockSpec((tm, LANES), lambda i, j: (i, j))
    return pl.pallas_call(body, grid=(s // tm, 4), in_specs=[blk, blk], out_specs=blk, out_shape=SDS((s, ATT_W), F32),
                          name=name, compiler_params=_cp(("parallel", "parallel")))(a, b)


def fox_post(dq_aug, dk_aug, dv, fa, bfo, *, tb=512):
    s = dv.shape[0]
    n = s // tb

    def body(dq_ref, dk_ref, dv_ref, fa_ref, b_ref, tri_ref, dz_ref, dfa_ref, gb_ref, carry, acc):
        i = pl.program_id(0)

        @pl.when(i == 0)
        def _():
            carry[...] = jnp.zeros_like(carry)
            acc[...] = jnp.zeros_like(acc)

        lane = _lane((tb, LANES))
        d_f = jnp.zeros((tb, LANES), F32)
        for h in range(N_HEADS):
            col = dq_ref[:, h * LANES + 64:h * LANES + 65] - dk_ref[:, h * LANES + 67:h * LANES + 68]
            d_f = jnp.where(lane == h, col, d_f)
        suffix = jnp.dot(tri_ref[...], d_f, preferred_element_type=F32, precision=lax.Precision.HIGHEST) + carry[0:1, :]
        carry[0:1, :] = suffix[0:1, :]
        xv = fa_ref[...] + b_ref[...]
        dx = suffix * (1.0 / (1.0 + jnp.exp(xv)))
        dfa_ref[...] = dx.astype(dfa_ref.dtype)
        acc[...] += jnp.sum(dx.reshape(tb // 8, 8, LANES), axis=0)
        for hp in range(4):
            for src, off, scale in ((dq_ref, 0, QK_SCALE), (dk_ref, ATT_W, 1.0)):
                even = src[:, (2 * hp) * LANES:(2 * hp + 1) * LANES]
                odd = pltpu.roll(src[:, (2 * hp + 1) * LANES:(2 * hp + 2) * LANES], 64, axis=1)
                dz_ref[:, off + hp * LANES:off + (hp + 1) * LANES] = (jnp.where(lane < 64, even, odd) * scale).astype(BF16)
        dz_ref[:, 2 * ATT_W:3 * ATT_W] = dv_ref[...].astype(BF16)

        @pl.when(i == n - 1)
        def _():
            gb_ref[...] = jnp.sum(acc[...], axis=0, keepdims=True)

    rev = lambda i: (n - 1 - i, 0)
    return pl.pallas_call(
        body, grid=(n,),
        in_specs=[pl.BlockSpec((tb, N_HEADS * LANES), rev), pl.BlockSpec((tb, N_HEADS * LANES), rev),
                  pl.BlockSpec((tb, ATT_W), rev), pl.BlockSpec((tb, LANES), rev),
                  pl.BlockSpec((1, LANES), lambda i: (0, 0)), pl.BlockSpec((tb, tb), lambda i: (0, 0))],
        out_specs=[pl.BlockSpec((tb, 3 * ATT_W), rev), pl.BlockSpec((tb, LANES), rev),
                   pl.BlockSpec((1, LANES), lambda i: (0, 0))],
        out_shape=[SDS((s, 3 * ATT_W), BF16), SDS((s, LANES), BF16), SDS((1, LANES), F32)],
        scratch_shapes=[pltpu.VMEM((8, LANES), F32), pltpu.VMEM((8, LANES), F32)],
        name="fox_post", compiler_params=_cp(("arbitrary",)))(dq_aug, dk_aug, dv, fa, bfo, _tri(tb, True))


def rope_tables(s, sign):
    half = ROPE_DIM // 2
    inv_freq = ROPE_THETA ** (-jnp.arange(half, dtype=F32) * 2.0 / ROPE_DIM)
    ang = jnp.arange(s, dtype=F32)[:, None] * inv_freq[None, :]
    l64 = np.arange(LANES) % HEAD_DIM
    cos = jnp.cos(ang)[:, l64 % half]
    sin = jnp.sin(ang)[:, l64 % half] * sign
    first = jnp.asarray(l64 < half)[None, :]
    second = jnp.asarray((l64 >= half) & (l64 < ROPE_DIM))[None, :]
    return (jnp.where(first | second, cos, 1.0), jnp.where(first, -sin, 0.0), jnp.where(second, sin, 0.0))


def rope_apply(items, tabs, *, out_dtype, tm=512, name):
    s = items[0][0].shape[0]
    n_i = len(items)

    def body(*refs):
        c_ref, sn_ref, sp_ref = refs[n_i:n_i + 3]
        o_ref = refs[-1]
        for j, (_, _, scale, rotate) in enumerate(items):
            for b in range(4):
                xv = refs[j][:, b * LANES:(b + 1) * LANES].astype(F32)
                if rotate:
                    xv = xv * c_ref[...] + pltpu.roll(xv, LANES - 8, axis=1) * sn_ref[...] + pltpu.roll(xv, 8, axis=1) * sp_ref[...]
                o_ref[:, j * ATT_W + b * LANES:j * ATT_W + (b + 1) * LANES] = (xv * scale).astype(o_ref.dtype)

    in_specs = [pl.BlockSpec((tm, ATT_W), functools.partial(lambda i, blk: (i, blk), blk=it[1])) for it in items]
    in_specs += [pl.BlockSpec((tm, LANES), lambda i: (i, 0))] * 3
    return pl.pallas_call(
        body, grid=(s // tm,), in_specs=in_specs, out_specs=pl.BlockSpec((tm, n_i * ATT_W), lambda i: (i, 0)),
        out_shape=SDS((s, n_i * ATT_W), out_dtype), name=name, compiler_params=_cp(("parallel",)))(*[it[0] for it in items], *tabs)


def _dil_views(qk, z, r):
    s = z.shape[0]
    return qk.reshape(s // r, r * 2 * ATT_W), z.reshape(s // r, r * Z_W)


def _dil_cols(r):
    q_col = lambda rho, hp: rho * 8 + hp
    k_col = lambda rho, hp: rho * 8 + 4 + hp
    v_col = lambda rho, hp: rho * (Z_W // LANES) + 4 * Z_VB + hp
    return q_col, k_col, v_col


def _dil_scores(qv, kp, kc, head, has_prev):
    b = DIL_BLK
    qm = jnp.where(head, qv, jnp.zeros_like(qv))
    row, col = _row((b, b)), _lane((b, b))
    sp = jnp.where((col >= row) & has_prev, _nt(qm, kp), NEG)
    sc = jnp.where(col <= row, _nt(qm, kc), NEG)
    return sp, sc


def dil_fwd(qk, z, prev, *, r):
    s = z.shape[0]
    b = DIL_BLK
    l_sub = s // r
    nb = l_sub // b
    qk_v, z_v = _dil_views(qk, z, r)
    q_col, k_col, v_col = _dil_cols(r)
    merge = prev is not None

    def body(*refs):
        if merge:
            q_ref, kp_ref, kc_ref, vp_ref, vc_ref, op_ref, lp_ref, o_ref, l_ref = refs
        else:
            q_ref, kp_ref, kc_ref, vp_ref, vc_ref, o_ref, l_ref = refs
        has_prev = pl.program_id(2) > 0
        lane = _lane((b, LANES))
        res = []
        for i in range(2):
            head = (lane < 64) if i == 0 else (lane >= 64)
            sp, sc = _dil_scores(q_ref[...], kp_ref[...], kc_ref[...], head, has_prev)
            m = jnp.maximum(jnp.max(sp, axis=-1, keepdims=True), jnp.max(sc, axis=-1, keepdims=True))
            pp = jnp.exp(sp - m)
            pc = jnp.exp(sc - m)
            den = jnp.sum(pp, axis=-1, keepdims=True) + jnp.sum(pc, axis=-1, keepdims=True)
            ov = (_nn(pp.astype(BF16), vp_ref[...]) + _nn(pc.astype(BF16), vc_ref[...])) / den
            res.append((ov, m + jnp.log(den)))
        ov = jnp.where(lane < 64, res[0][0], res[1][0])
        lse = jnp.where(lane < 64, res[0][1], res[1][1])
        if merge:
            lp = lp_ref[...]
            m2 = jnp.maximum(lp, lse)
            wp = jnp.exp(lp - m2)
            wn = jnp.exp(lse - m2)
            ov = (wp * op_ref[...] + wn * ov) / (wp + wn)
            lse = m2 + jnp.log(wp + wn)
        o_ref[...] = ov
        l_ref[...] = lse

    blk = lambda f: pl.BlockSpec((b, LANES), f)
    in_specs = [blk(lambda rho, hp, n: (n, q_col(rho, hp))), blk(lambda rho, hp, n: (jnp.maximum(n - 1, 0), k_col(rho, hp))),
                blk(lambda rho, hp, n: (n, k_col(rho, hp))), blk(lambda rho, hp, n: (jnp.maximum(n - 1, 0), v_col(rho, hp))),
                blk(lambda rho, hp, n: (n, v_col(rho, hp)))]
    args = [qk_v, qk_v, qk_v, z_v, z_v]
    nat = blk(lambda rho, hp, n: (n, rho * 4 + hp))
    if merge:
        in_specs += [nat, nat]
        args += [prev[0].reshape(l_sub, r * ATT_W), prev[1].reshape(l_sub, r * ATT_W)]
    o, lse = pl.pallas_call(
        body, grid=(r, 4, nb), in_specs=in_specs, out_specs=[nat, nat],
        out_shape=[SDS((l_sub, r * ATT_W), F32)] * 2, name=f"dil_fwd_r{r}",
        compiler_params=_cp(("parallel", "parallel", "arbitrary")))(*args)
    return o.reshape(s, ATT_W), lse.reshape(s, ATT_W)


def dil_bwd_dq(qk, z, dy, lse, dd, acc, *, r):
    s = z.shape[0]
    b = DIL_BLK
    l_sub = s // r
    nb = l_sub // b
    qk_v, z_v = _dil_views(qk, z, r)
    q_col, k_col, v_col = _dil_cols(r)
    add = acc is not None

    def body(*refs):
        q_ref, kp_ref, kc_ref, vp_ref, vc_ref, do_ref, l_ref, dd_ref = refs[:8]
        dq_ref = refs[-1]
        has_prev = pl.program_id(2) > 0
        lane = _lane((b, LANES))
        dov = do_ref[...]
        parts = []
        for i in range(2):
            head = (lane < 64) if i == 0 else (lane >= 64)
            sp, sc = _dil_scores(q_ref[...], kp_ref[...], kc_ref[...], head, has_prev)
            lse_i = l_ref[:, i * 64:i * 64 + 1]
            dd_i = dd_ref[:, i * 64:i * 64 + 1]
            dom = jnp.where(head, dov, jnp.zeros_like(dov))
            dsp = (jnp.exp(sp - lse_i) * (_nt(dom, vp_ref[...]) - dd_i)).astype(BF16)
            dsc = (jnp.exp(sc - lse_i) * (_nt(dom, vc_ref[...]) - dd_i)).astype(BF16)
            parts.append(_nn(dsp, kp_ref[...]) + _nn(dsc, kc_ref[...]))
        dq = jnp.where(lane < 64, parts[0], parts[1])
        if add:
            dq = dq + refs[8][...]
        dq_ref[...] = dq

    blk = lambda f: pl.BlockSpec((b, LANES), f)
    nat = blk(lambda rho, hp, n: (n, rho * 4 + hp))
    in_specs = [blk(lambda rho, hp, n: (n, q_col(rho, hp))), blk(lambda rho, hp, n: (jnp.maximum(n - 1, 0), k_col(rho, hp))),
                blk(lambda rho, hp, n: (n, k_col(rho, hp))), blk(lambda rho, hp, n: (jnp.maximum(n - 1, 0), v_col(rho, hp))),
                blk(lambda rho, hp, n: (n, v_col(rho, hp))), nat, nat, nat]
    nview = lambda a: a.reshape(l_sub, r * ATT_W)
    args = [qk_v, qk_v, qk_v, z_v, z_v, nview(dy), nview(lse), nview(dd)]
    if add:
        in_specs.append(nat)
        args.append(nview(acc))
    dq = pl.pallas_call(
        body, grid=(r, 4, nb), in_specs=in_specs, out_specs=nat, out_shape=SDS((l_sub, r * ATT_W), F32),
        name=f"dil_bwd_dq_r{r}", compiler_params=_cp(("parallel", "parallel", "arbitrary")))(*args)
    return dq.reshape(s, ATT_W)


def dil_bwd_dkv(qk, z, dy, lse, dd, acc, *, r):
    s = z.shape[0]
    b = DIL_BLK
    l_sub = s // r
    nb = l_sub // b
    qk_v, z_v = _dil_views(qk, z, r)
    q_col, k_col, v_col = _dil_cols(r)
    add = acc is not None

    def body(*refs):
        k_ref, v_ref, qc_ref, qn_ref, doc_ref, don_ref, lc_ref, ln_ref, ddc_ref, ddn_ref = refs[:10]
        dk_ref, dv_ref = refs[-2:]
        has_next = pl.program_id(2) < nb - 1
        lane = _lane((b, LANES))
        row, col = _row((b, b)), _lane((b, b))
        kv = k_ref[...]
        vv = v_ref[...]
        dk_parts, dv_parts = [], []
        for i in range(2):
            head = (lane < 64) if i == 0 else (lane >= 64)
            dk_i = jnp.zeros((b, LANES), F32)
            dv_i = jnp.zeros((b, LANES), F32)
            for q_ref, do_ref, l_ref, d_ref, mask in ((qc_ref, doc_ref, lc_ref, ddc_ref, col <= row),
                                                      (qn_ref, don_ref, ln_ref, ddn_ref, (col >= row) & has_next)):
                qv = q_ref[...]
                dov = do_ref[...]
                sc = jnp.where(mask, _nt(jnp.where(head, qv, jnp.zeros_like(qv)), kv), NEG)
                p = jnp.exp(sc - l_ref[:, i * 64:i * 64 + 1])
                dp = _nt(jnp.where(head, dov, jnp.zeros_like(dov)), vv)
                ds = (p * (dp - d_ref[:, i * 64:i * 64 + 1])).astype(BF16)
                dv_i = dv_i + _tn(p.astype(BF16), dov)
                dk_i = dk_i + _tn(ds, qv)
            dk_parts.append(dk_i)
            dv_parts.append(dv_i)
        dk = jnp.where(lane < 64, dk_parts[0], dk_parts[1])
        dv = jnp.where(lane < 64, dv_parts[0], dv_parts[1])
        if add:
            dk = dk + refs[10][...]
            dv = dv + refs[11][...]
        dk_ref[...] = dk
        dv_ref[...] = dv

    blk = lambda f: pl.BlockSpec((b, LANES), f)
    nat = blk(lambda rho, hp, n: (n, rho * 4 + hp))
    nxt = blk(lambda rho, hp, n: (jnp.minimum(n + 1, nb - 1), rho * 4 + hp))
    in_specs = [blk(lambda rho, hp, n: (n, k_col(rho, hp))), blk(lambda rho, hp, n: (n, v_col(rho, hp))),
                blk(lambda rho, hp, n: (n, q_col(rho, hp))), blk(lambda rho, hp, n: (jnp.minimum(n + 1, nb - 1), q_col(rho, hp))),
                nat, nxt, nat, nxt, nat, nxt]
    nview = lambda a: a.reshape(l_sub, r * ATT_W)
    args = [qk_v, z_v, qk_v, qk_v, nview(dy), nview(dy), nview(lse), nview(lse), nview(dd), nview(dd)]
    if add:
        in_specs += [nat, nat]
        args += [nview(acc[0]), nview(acc[1])]
    dk, dv = pl.pallas_call(
        body, grid=(r, 4, nb), in_specs=in_specs, out_specs=[nat, nat],
        out_shape=[SDS((l_sub, r * ATT_W), F32)] * 2, name=f"dil_bwd_dkv_r{r}",
        compiler_params=_cp(("parallel", "parallel", "arbitrary")))(*args)
    return dk.reshape(s, ATT_W), dv.reshape(s, ATT_W)


def _dil_rows(base, r):
    if r == 1:
        return pl.ds(pl.multiple_of(base, DIL_BLK), DIL_BLK)
    return pl.ds(base, DIL_BLK, stride=r)


def _dil_block(idx, r, nb):
    shift = nb.bit_length() - 1
    rho = idx >> shift
    n = idx & (nb - 1)
    base = rho + n * (r * DIL_BLK)
    return _dil_rows(base, r), _dil_rows(jnp.maximum(base - r * DIL_BLK, rho), r), n > 0


def _cat(a, b):
    return jnp.concatenate([a, b], axis=0)


def _two_heads(v, first_head):
    zero = jnp.zeros_like(v)
    return _cat(jnp.where(first_head, v, zero), jnp.where(first_head, zero, v))


def _dil_bands():
    b = DIL_BLK
    q = _row((2 * b, 2 * b)) & (b - 1)
    col = _lane((2 * b, 2 * b))
    return (col < b) & (col >= q), (col >= b) & (col - b <= q)


def dil_fwd_all(qkv, *, unroll=2):
    s = qkv.shape[0]
    b = DIL_BLK
    n_blk = s // b

    def body(q_ref, k_ref, v_ref, o_ref, l_ref):
        first_head = _lane((b, LANES)) < 64
        band_prev, band_cur = _dil_bands()
        for g, (_, r) in enumerate(DIL_PATTERNS):
            nb = n_blk // r

            def group(it, carry, g=g, r=r, nb=nb):
                loaded = []
                for u in range(unroll):
                    rows_c, rows_p, has_prev = _dil_block(it * unroll + u, r, nb)
                    vals = [q_ref[rows_c, :].astype(BF16), k_ref[rows_p, :].astype(BF16), k_ref[rows_c, :].astype(BF16),
                            v_ref[rows_p, :].astype(BF16), v_ref[rows_c, :].astype(BF16)]
                    state = (o_ref[rows_c, :], l_ref[rows_c, :]) if g else None
                    loaded.append((rows_c, has_prev, vals, state))
                done = []
                for rows_c, has_prev, (qv, kp, kc, vp, vc), state in loaded:
                    sc = jnp.where(band_cur | (band_prev & has_prev), _nt(_two_heads(qv, first_head), _cat(kp, kc)), NEG)
                    m = jnp.max(sc, axis=-1, keepdims=True)
                    p = jnp.exp(sc - m)
                    den = jnp.sum(p, axis=-1, keepdims=True)
                    both = _nn(p.astype(BF16), _cat(vp, vc)) / den
                    lse2 = m + jnp.log(den)
                    ov = jnp.where(first_head, both[:b], both[b:])
                    lse = jnp.where(first_head, lse2[:b], lse2[b:])
                    if state is not None:
                        m2 = jnp.maximum(state[1], lse)
                        wp = jnp.exp(state[1] - m2)
                        wn = jnp.exp(lse - m2)
                        ov = (wp * state[0] + wn * ov) / (wp + wn)
                        lse = m2 + jnp.log(wp + wn)
                    done.append((rows_c, ov, lse))
                for rows_c, ov, lse in done:
                    o_ref[rows_c, :] = ov
                    l_ref[rows_c, :] = lse
                return carry

            lax.fori_loop(0, n_blk // unroll, group, 0)

    col_blk = lambda k: pl.BlockSpec((s, LANES), lambda hp: (0, 4 * k + hp))
    out = pl.BlockSpec((s, LANES), lambda hp: (0, hp))
    return pl.pallas_call(
        body, grid=(4,), in_specs=[col_blk(0), col_blk(1), col_blk(2)], out_specs=[out, out],
        out_shape=[SDS((s, ATT_W), F32)] * 2, name="dil_fwd", compiler_params=_cp(("parallel",)))(qkv, qkv, qkv)


def dil_bwd_all(qkv, dy, lse, dd, *, unroll=2):
    s = qkv.shape[0]
    b = DIL_BLK
    n_blk = s // b

    def body(q_ref, k_ref, v_ref, do_ref, l_ref, dd_ref, dq_ref, dk_ref, dv_ref):
        dq_ref[...] = jnp.zeros_like(dq_ref)
        dk_ref[...] = jnp.zeros_like(dk_ref)
        dv_ref[...] = jnp.zeros_like(dv_ref)
        first_head = _lane((b, LANES)) < 64
        band_prev, band_cur = _dil_bands()
        for _, r in DIL_PATTERNS:
            nb = n_blk // r

            def group(it, carry, r=r, nb=nb):
                loaded = []
                for u in range(unroll):
                    rows_c, rows_p, has_prev = _dil_block(it * unroll + u, r, nb)
                    vals = [q_ref[rows_c, :].astype(BF16), k_ref[rows_p, :].astype(BF16), k_ref[rows_c, :].astype(BF16),
                            v_ref[rows_p, :].astype(BF16), v_ref[rows_c, :].astype(BF16), do_ref[rows_c, :].astype(BF16),
                            l_ref[rows_c, :], dd_ref[rows_c, :]]
                    loaded.append((rows_c, rows_p, has_prev, vals))
                done = []
                for rows_c, rows_p, has_prev, (qv, kp, kc, vp, vc, dov, lv, ddv) in loaded:
                    q2 = _two_heads(qv, first_head)
                    do2 = _two_heads(dov, first_head)
                    kcat, vcat = _cat(kp, kc), _cat(vp, vc)
                    lse2 = _cat(lv[:, 0:1], lv[:, 64:65])
                    dd2 = _cat(ddv[:, 0:1], ddv[:, 64:65])
                    p = jnp.exp(jnp.where(band_cur | (band_prev & has_prev), _nt(q2, kcat), NEG) - lse2)
                    ds = (p * (_nt(do2, vcat) - dd2)).astype(BF16)
                    dq2 = _nn(ds, kcat)
                    dkcat = _tn(ds, q2)
                    dvcat = _tn(p.astype(BF16), do2)
                    done.append((rows_c, rows_p, (jnp.where(first_head, dq2[:b], dq2[b:]), dkcat[:b], dkcat[b:],
                                                  dvcat[:b], dvcat[b:])))
                for rows_c, rows_p, (dq, dk_p, dk_c, dv_p, dv_c) in done:
                    dq_ref[rows_c, :] += dq
                    dk_ref[rows_p, :] += dk_p
                    dk_ref[rows_c, :] += dk_c
                    dv_ref[rows_p, :] += dv_p
                    dv_ref[rows_c, :] += dv_c
                return carry

            lax.fori_loop(0, n_blk // unroll, group, 0)

    col_blk = lambda k: pl.BlockSpec((s, LANES), lambda hp: (0, 4 * k + hp))
    nat = pl.BlockSpec((s, LANES), lambda hp: (0, hp))
    return pl.pallas_call(
        body, grid=(4,), in_specs=[col_blk(0), col_blk(1), col_blk(2), nat, nat, nat], out_specs=[nat, nat, nat],
        out_shape=[SDS((s, ATT_W), F32)] * 3, name="dil_bwd", compiler_params=_cp(("parallel",)))(qkv, qkv, qkv, dy, lse, dd)


def _sigmoid(v):
    return 1.0 / (1.0 + jnp.exp(-v))


def gate_mix(ya, yb, wa, wb, z, *, tm=512, tn=512):
    s = ya.shape[0]
    d = wa.shape[1]
    ga_blk = 3 * ATT_W * 2 // tn
    gb_blk = ga_blk + d // tn

    def body(ya_ref, yb_ref, wa_ref, wb_ref, ga_ref, gb_ref, pa_ref, pb_ref, mx_ref):
        pa = _nn(ya_ref[...], wa_ref[...])
        pb = _nn(yb_ref[...].astype(BF16), wb_ref[...])
        pa_ref[...] = pa.astype(BF16)
        pb_ref[...] = pb.astype(BF16)
        mx_ref[...] = (_sigmoid(ga_ref[...].astype(F32)) * pa + _sigmoid(gb_ref[...].astype(F32)) * pb).astype(BF16)

    out = pl.BlockSpec((tm, tn), lambda i, j: (i, j))
    return pl.pallas_call(
        body, grid=(s // tm, d // tn),
        in_specs=[pl.BlockSpec((tm, ATT_W), lambda i, j: (i, 0)), pl.BlockSpec((tm, ATT_W), lambda i, j: (i, 0)),
                  pl.BlockSpec((ATT_W, tn), lambda i, j: (0, j)), pl.BlockSpec((ATT_W, tn), lambda i, j: (0, j)),
                  pl.BlockSpec((tm, tn), lambda i, j: (i, ga_blk + j)), pl.BlockSpec((tm, tn), lambda i, j: (i, gb_blk + j))],
        out_specs=[out, out, out], out_shape=[SDS((s, d), BF16)] * 3, name="gate_mix",
        compiler_params=_cp(("parallel", "parallel")))(ya, yb, wa, wb, z, z)


def gate_bwd(dmx, z, pa, pb, *, tm=256):
    s, d = dmx.shape

    def body(dm_ref, ga_ref, gb_ref, pa_ref, pb_ref, dpa_ref, dpb_ref, dg_ref):
        dm = dm_ref[...].astype(F32)
        sa = _sigmoid(ga_ref[...].astype(F32))
        sb = _sigmoid(gb_ref[...].astype(F32))
        dpa_ref[...] = (dm * sa).astype(BF16)
        dpb_ref[...] = (dm * sb).astype(BF16)
        dg_ref[:, 0:d] = (dm * pa_ref[...].astype(F32) * sa * (1.0 - sa)).astype(BF16)
        dg_ref[:, d:2 * d] = (dm * pb_ref[...].astype(F32) * sb * (1.0 - sb)).astype(BF16)

    row = pl.BlockSpec((tm, d), lambda i: (i, 0))
    return pl.pallas_call(
        body, grid=(s // tm,),
        in_specs=[row, pl.BlockSpec((tm, d), lambda i: (i, 3)), pl.BlockSpec((tm, d), lambda i: (i, 4)), row, row],
        out_specs=[row, row, pl.BlockSpec((tm, 2 * d), lambda i: (i, 0))],
        out_shape=[SDS((s, d), BF16), SDS((s, d), BF16), SDS((s, 2 * d), BF16)], name="gate_bwd",
        compiler_params=_cp(("parallel",)))(dmx, z, z, pa, pb)


GELU_C = math.sqrt(2.0 / math.pi)


def _gelu_parts(a):
    inner = GELU_C * (a + 0.044715 * a * a * a)
    th = jnp.tanh(inner)
    gelu = 0.5 * a * (1.0 + th)
    dgelu = 0.5 * (1.0 + th) + 0.5 * a * (1.0 - th * th) * GELU_C * (1.0 + 3.0 * 0.044715 * a * a)
    return gelu, dgelu


def _causal_taps(u, before):
    row = _row(u.shape)
    r1 = jnp.where(row == 0, before[7:8, :], pltpu.roll(u, 1, axis=0))
    r2 = jnp.where(row == 0, before[6:7, :], jnp.where(row == 1, before[7:8, :], pltpu.roll(u, 2, axis=0)))
    return r1, r2


def ffn_up(h, wa, wb, cw, cb, *, tm=512, tn=256):
    s, d = h.shape
    f = wa.shape[1]
    nj = f // tn

    def body(h_ref, wa_ref, wb_ref, cwa_ref, cwb_ref, cba_ref, cbb_ref, ua_ref, ub_ref, m_ref, carry):
        @pl.when(pl.program_id(1) == 0)
        def _():
            carry[...] = jnp.zeros_like(carry)

        conv = []
        for k, (w_ref, cw_ref, cb_ref, u_ref) in enumerate(((wa_ref, cwa_ref, cba_ref, ua_ref), (wb_ref, cwb_ref, cbb_ref, ub_ref))):
            u16 = _nn(h_ref[...], w_ref[...]).astype(BF16)
            u_ref[...] = u16
            u = u16.astype(F32)
            r1, r2 = _causal_taps(u, carry[k])
            carry[k] = u[tm - 8:tm, :]
            conv.append(cw_ref[0:1, :] * r2 + cw_ref[1:2, :] * r1 + cw_ref[2:3, :] * u + cb_ref[...])
        m_ref[...] = (_gelu_parts(conv[0])[0] * conv[1]).astype(BF16)

    out = pl.BlockSpec((tm, tn), lambda j, i: (i, j))
    return pl.pallas_call(
        body, grid=(nj, s // tm),
        in_specs=[pl.BlockSpec((tm, d), lambda j, i: (i, 0)),
                  pl.BlockSpec((d, tn), lambda j, i: (0, j)), pl.BlockSpec((d, tn), lambda j, i: (0, j)),
                  pl.BlockSpec((3, tn), lambda j, i: (0, j)), pl.BlockSpec((3, tn), lambda j, i: (0, nj + j)),
                  pl.BlockSpec((1, tn), lambda j, i: (0, j)), pl.BlockSpec((1, tn), lambda j, i: (0, nj + j))],
        out_specs=[out, out, out], out_shape=[SDS((s, f), BF16)] * 3,
        scratch_shapes=[pltpu.VMEM((2, 8, tn), F32)], name="ffn_up",
        compiler_params=_cp(("parallel", "arbitrary")))(h, wa, wb, cw, cw, cb, cb)


def ffn_bwd(dm, ua, ub, cw, cb, *, tm=512, tn=256):
    s, f = dm.shape
    nj = f // tn
    ni = s // tm
    halo = 16

    def body(dm_ref, ua_ref, ub_ref, ha_ref, hb_ref, cwa_ref, cwb_ref, cba_ref, cbb_ref,
             dua_ref, dub_ref, ga_ref, gb_ref, carry):
        i = pl.program_id(1)

        @pl.when(i == 0)
        def _():
            carry[...] = jnp.zeros_like(carry)
            ga_ref[...] = jnp.zeros_like(ga_ref)
            gb_ref[...] = jnp.zeros_like(gb_ref)

        first_tile = i == ni - 1
        row = _row((tm, tn))
        dmv = dm_ref[...].astype(F32)
        us, taps, convs = [], [], []
        for u_ref, h_ref, cw_ref, cb_ref in ((ua_ref, ha_ref, cwa_ref, cba_ref), (ub_ref, hb_ref, cwb_ref, cbb_ref)):
            u = u_ref[...].astype(F32)
            before = jnp.where(first_tile, 0.0, h_ref[halo - 8:halo, :].astype(F32))
            r1, r2 = _causal_taps(u, before)
            us.append(u)
            taps.append((r1, r2))
            convs.append(cw_ref[0:1, :] * r2 + cw_ref[1:2, :] * r1 + cw_ref[2:3, :] * u + cb_ref[...])
        gelu, dgelu = _gelu_parts(convs[0])
        dcs = (dmv * convs[1] * dgelu, dmv * gelu)
        for k, (dc, cw_ref, du_ref, g_ref) in enumerate(((dcs[0], cwa_ref, dua_ref, ga_ref), (dcs[1], cwb_ref, dub_ref, gb_ref))):
            r1, r2 = taps[k]
            g_ref[0:1, :] += jnp.sum(dc * r2, axis=0, keepdims=True)
            g_ref[1:2, :] += jnp.sum(dc * r1, axis=0, keepdims=True)
            g_ref[2:3, :] += jnp.sum(dc * us[k], axis=0, keepdims=True)
            g_ref[3:4, :] += jnp.sum(dc, axis=0, keepdims=True)
            after = carry[k]
            n1 = jnp.where(row == tm - 1, after[0:1, :], pltpu.roll(dc, tm - 1, axis=0))
            n2 = jnp.where(row == tm - 2, after[0:1, :], jnp.where(row == tm - 1, after[1:2, :], pltpu.roll(dc, tm - 2, axis=0)))
            du_ref[...] = (cw_ref[2:3, :] * dc + cw_ref[1:2, :] * n1 + cw_ref[0:1, :] * n2).astype(BF16)
            carry[k] = dc[0:8, :]

    tile = pl.BlockSpec((tm, tn), lambda j, i: (ni - 1 - i, j))
    halo_spec = pl.BlockSpec((halo, tn), lambda j, i: (jnp.maximum((ni - 1 - i) * (tm // halo) - 1, 0), j))
    gspec = pl.BlockSpec((8, tn), lambda j, i: (0, j))
    return pl.pallas_call(
        body, grid=(nj, ni),
        in_specs=[tile, tile, tile, halo_spec, halo_spec,
                  pl.BlockSpec((3, tn), lambda j, i: (0, j)), pl.BlockSpec((3, tn), lambda j, i: (0, nj + j)),
                  pl.BlockSpec((1, tn), lambda j, i: (0, j)), pl.BlockSpec((1, tn), lambda j, i: (0, nj + j))],
        out_specs=[tile, tile, gspec, gspec],
        out_shape=[SDS((s, f), BF16), SDS((s, f), BF16), SDS((8, f), F32), SDS((8, f), F32)],
        scratch_shapes=[pltpu.VMEM((2, 8, tn), F32)], name="ffn_bwd",
        compiler_params=_cp(("parallel", "arbitrary")))(dm, ua, ub, ua, ub, cw, cw, cb, cb)


def adamw(w, g, m, v, *, name, tr=None):
    r = w.shape[0]
    rest = w.shape[1:]
    if tr is None:
        tr = r
        for cand in (256, 128, 64, 32, 16, 8):
            if r % cand == 0:
                tr = cand
                break

    def body(w_ref, g_ref, m_ref, v_ref, d_ref, nm_ref, nv_ref):
        gv = g_ref[...]
        mn = ADAM_B1 * m_ref[...] + (1.0 - ADAM_B1) * gv
        vn = ADAM_B2 * v_ref[...] + (1.0 - ADAM_B2) * (gv * gv)
        m_hat = mn / (1.0 - ADAM_B1 ** ADAM_STEP)
        v_hat = vn / (1.0 - ADAM_B2 ** ADAM_STEP)
        d_ref[...] = -ADAM_LR * (m_hat / (jnp.sqrt(v_hat) + ADAM_EPS) + ADAM_WD * w_ref[...])
        nm_ref[...] = mn
        nv_ref[...] = vn

    blk = pl.BlockSpec((tr,) + rest, lambda i: (i,) + (0,) * len(rest))
    return pl.pallas_call(body, grid=(r // tr,), in_specs=[blk] * 4, out_specs=[blk] * 3, out_shape=[SDS(w.shape, F32)] * 3,
                          name=name, compiler_params=_cp(("parallel",)))(w, g, m, v)


ANY = pl.BlockSpec(memory_space=pl.ANY)
ICI_KINDS = ("x", "y", "xy")


def _coords():
    return lax.axis_index("x"), lax.axis_index("y"), lax.axis_index("c")


def _peer(kind, x, y, c):
    if kind == "c":
        return (x, y, 1 - c)
    if kind == "x":
        return (1 - x, y, c)
    if kind == "y":
        return (x, 1 - y, c)
    return (1 - x, 1 - y, c)


def _chip_of(p):
    return 2 * p[0] + p[1]


def _half(rows, which):
    h = rows // 2
    return pl.ds(pl.multiple_of(which * h, 16), h)


def _remote(src, dst, send_sem, recv_sem, to):
    return pltpu.make_async_remote_copy(src_ref=src, dst_ref=dst, send_sem=send_sem, recv_sem=recv_sem,
                                        device_id=to, device_id_type=MESH)


def allgather_chips(shards, halved, *, name):
    n = len(shards)

    def body(*refs):
        ins, outs = refs[:n], refs[n:2 * n]
        send_sems, recv_sems = refs[2 * n:]
        x, y, c = _coords()
        me = (x, y, c)
        my_chip = 2 * x + y

        def rows(w, which):
            r = shards[w].shape[0]
            return _half(r, which) if halved[w] else pl.ds(0, r)

        first = []
        for w in range(n):
            for k, kind in enumerate(ICI_KINDS):
                cp = _remote(ins[w].at[rows(w, c)], outs[w].at[my_chip, rows(w, c)], send_sems.at[w, k], recv_sems.at[w, k],
                             _peer(kind, x, y, c))
                cp.start()
                first.append(cp)
        second = []
        for w in range(n):
            for k, kind in enumerate(ICI_KINDS):
                landed = outs[w].at[_chip_of(_peer(kind, x, y, c)), rows(w, c)]
                _remote(landed, landed, send_sems.at[w, k], recv_sems.at[w, k], me).wait_recv()
                if halved[w]:
                    cp = _remote(landed, landed, send_sems.at[w, 3 + k], recv_sems.at[w, 3 + k], _peer("c", x, y, c))
                    cp.start()
                    second.append(cp)
        for w in range(n):
            if halved[w]:
                for k, kind in enumerate(ICI_KINDS):
                    other = outs[w].at[_chip_of(_peer(kind, x, y, c)), rows(w, 1 - c)]
                    _remote(other, other, send_sems.at[w, 3 + k], recv_sems.at[w, 3 + k], me).wait_recv()
        for cp in first + second:
            cp.wait_send()

    return pl.pallas_call(
        body, in_specs=[ANY] * n, out_specs=[ANY] * n,
        out_shape=[SDS((4,) + a.shape, a.dtype) for a in shards],
        scratch_shapes=[pltpu.SemaphoreType.DMA((n, 6)), pltpu.SemaphoreType.DMA((n, 6))],
        name=name)(*shards)


def grads_to_sibling(gs, *, name):
    n = len(gs)

    def body(*refs):
        ins, outs = refs[:n], refs[n:2 * n]
        send_sems, recv_sems = refs[2 * n:]
        x, y, c = _coords()
        cps = []
        for w in range(n):
            cp = _remote(ins[w].at[:, _half(gs[w].shape[1], 1 - c)], outs[w], send_sems.at[w], recv_sems.at[w],
                         _peer("c", x, y, c))
            cp.start()
            cps.append(cp)
        for cp in cps:
            cp.wait()

    return pl.pallas_call(
        body, in_specs=[ANY] * n, out_specs=[ANY] * n,
        out_shape=[SDS((4, a.shape[1] // 2, a.shape[2]), a.dtype) for a in gs],
        scratch_shapes=[pltpu.SemaphoreType.DMA((n,)), pltpu.SemaphoreType.DMA((n,))], name=name)(*gs)


def grads_to_chips(ps, *, name):
    n = len(ps)

    def body(*refs):
        ins, outs = refs[:n], refs[n:2 * n]
        send_sems, recv_sems = refs[2 * n:]
        x, y, c = _coords()
        cps = []
        for w in range(n):
            for k, kind in enumerate(ICI_KINDS):
                to = _peer(kind, x, y, c)
                cp = _remote(ins[w].at[_chip_of(to)], outs[w].at[k], send_sems.at[w, k], recv_sems.at[w, k], to)
                cp.start()
                cps.append(cp)
        for cp in cps:
            cp.wait()

    return pl.pallas_call(
        body, in_specs=[ANY] * n, out_specs=[ANY] * n,
        out_shape=[SDS((3,) + a.shape[1:], a.dtype) for a in ps],
        scratch_shapes=[pltpu.SemaphoreType.DMA((n, 3)), pltpu.SemaphoreType.DMA((n, 3))], name=name)(*ps)


def halves_to_full(hs, *, name):
    n = len(hs)

    def body(*refs):
        ins, outs = refs[:n], refs[n:2 * n]
        send_sems, recv_sems = refs[2 * n:]
        x, y, c = _coords()
        cps = []
        for w in range(n):
            cp = _remote(ins[w], outs[w].at[_half(2 * hs[w].shape[0], c)], send_sems.at[w], recv_sems.at[w],
                         _peer("c", x, y, c))
            cp.start()
            cps.append(cp)
        for cp in cps:
            cp.wait()

    return pl.pallas_call(
        body, in_specs=[ANY] * n, out_specs=[ANY] * n,
        out_shape=[SDS((2 * a.shape[0], a.shape[1]), a.dtype) for a in hs],
        scratch_shapes=[pltpu.SemaphoreType.DMA((n,)), pltpu.SemaphoreType.DMA((n,))],
        name=name)(*hs)


def _row_tile(rows):
    for cand in (256, 192, 176, 128, 64, 32, 16):
        if rows % cand == 0:
            return cand
    return rows


def chip_sum(g, recv, c_arr, *, name):
    _, r, cols = g.shape
    h = r // 2
    tr = _row_tile(h)
    nblk = h // tr

    def body(c_ref, g_ref, r_ref, f_ref, b_ref):
        tot = g_ref[...] + r_ref[...]
        f_ref[...] = tot
        b_ref[...] = tot.astype(BF16)

    blk = pl.BlockSpec((None, tr, cols), lambda j, i, c_ref: (j, i, 0))
    grid_spec = pltpu.PrefetchScalarGridSpec(
        num_scalar_prefetch=1, grid=(4, nblk),
        in_specs=[pl.BlockSpec((None, tr, cols), lambda j, i, c_ref: (j, c_ref[0] * nblk + i, 0)), blk],
        out_specs=[blk, blk])
    return pl.pallas_call(body, grid_spec=grid_spec, out_shape=[SDS((4, h, cols), F32), SDS((4, h, cols), BF16)],
                          name=name, compiler_params=_cp(("parallel", "parallel")))(c_arr, g, recv)


def final_sum(pf, recv, chip_arr, *, name):
    _, h, cols = pf.shape
    tr = _row_tile(h)

    def body(chip_ref, p_ref, r_ref, o_ref):
        o_ref[...] = ((p_ref[...] + r_ref[0].astype(F32)) + r_ref[1].astype(F32)) + r_ref[2].astype(F32)

    grid_spec = pltpu.PrefetchScalarGridSpec(
        num_scalar_prefetch=1, grid=(h // tr,),
        in_specs=[pl.BlockSpec((None, tr, cols), lambda i, chip_ref: (chip_ref[0], i, 0)),
                  pl.BlockSpec((3, tr, cols), lambda i, chip_ref: (0, i, 0))],
        out_specs=pl.BlockSpec((tr, cols), lambda i, chip_ref: (i, 0)))
    return pl.pallas_call(body, grid_spec=grid_spec, out_shape=SDS((h, cols), F32), name=name,
                          compiler_params=_cp(("parallel",)))(chip_arr, pf, recv)


def allreduce_small(v, *, name):
    rws, cols = v.shape

    def body(v_ref, all_ref, sum_ref, send_sems, recv_sems, local_sem):
        x, y, c = _coords()
        me, sibling = (x, y, c), (x, y, 1 - c)
        chips = [(1 - x, y), (x, 1 - y), (1 - x, 1 - y)]

        def rows(px, py, pc):
            return all_ref.at[pl.ds(pl.multiple_of((4 * px + 2 * py + pc) * rws, 8), rws), :]

        def copy(k, block, to, src=None):
            return _remote(rows(*block) if src is None else src, rows(*block), send_sems.at[k], recv_sems.at[k], to)

        mine = pltpu.make_async_copy(v_ref, rows(*me), local_sem)
        mine.start()
        first = [copy(0, me, sibling, src=v_ref)]
        first += [copy(1 + j, me, (*chip, c), src=v_ref) for j, chip in enumerate(chips)]
        for cp in first:
            cp.start()
        passed = [copy(4 + j, (*chip, c), sibling) for j, chip in enumerate(chips)]
        for j, chip in enumerate(chips):
            copy(1 + j, (*chip, c), me).wait_recv()
            passed[j].start()
        copy(0, sibling, me).wait_recv()
        for j, chip in enumerate(chips):
            copy(4 + j, (*chip, 1 - c), me).wait_recv()
        for cp in first + passed:
            cp.wait_send()
        mine.wait()
        tot = all_ref[0:rws, :]
        for dev in range(1, 8):
            tot = tot + all_ref[dev * rws:(dev + 1) * rws, :]
        sum_ref[...] = tot

    vm = pl.BlockSpec(memory_space=pltpu.VMEM)
    return pl.pallas_call(
        body, in_specs=[vm], out_specs=[vm, vm],
        out_shape=[SDS((8 * rws, cols), v.dtype), SDS((rws, cols), v.dtype)],
        scratch_shapes=[pltpu.SemaphoreType.DMA((7,)), pltpu.SemaphoreType.DMA((7,)), pltpu.SemaphoreType.DMA],
        name=name)(v)[1]


def _pack_rows(parts, rows):
    out = []
    for a, r in zip(parts, rows):
        flat = a.reshape(-1)
        flat = jnp.pad(flat, (0, r * LANES - flat.shape[0]))
        out.append(flat.reshape(r, LANES))
    return jnp.concatenate(out, axis=0)


def _unpack_rows(packed, shapes, rows):
    out, at = [], 0
    for shp, r in zip(shapes, rows):
        size = int(np.prod(shp))
        out.append(packed[at:at + r].reshape(-1)[:size].reshape(shp))
        at += r
    return out


def kernel(x, g_pre_mix, w_in, b_forget, w_o_fox, w_o_dil, w_out, g_post_mix, g_pre_ffn, w_up, conv_w, conv_b, w_down, g_post_ffn, loss_target, m_g_pre_mix, m_w_in, m_b_forget, m_w_o_fox, m_w_o_dil, m_w_out, m_g_post_mix, m_g_pre_ffn, m_w_up, m_conv_w, m_conv_b, m_w_down, m_g_post_ffn, v_g_pre_mix, v_w_in, v_b_forget, v_w_o_fox, v_w_o_dil, v_w_out, v_g_post_mix, v_g_pre_ffn, v_w_up, v_conv_w, v_conv_b, v_w_down, v_g_post_ffn):
    xi, yi, ci = _coords()
    chip = 2 * xi + yi
    c_arr = jnp.reshape(ci, (1,)).astype(jnp.int32)
    chip_arr = jnp.reshape(chip, (1,)).astype(jnp.int32)
    xs = x[0]
    target = loss_target[0]
    s, d = xs.shape
    f_half = w_down.shape[1] * 4
    cols_in = w_in.shape[2]

    big = (w_in, w_o_fox, w_o_dil, w_out, w_up, w_down)
    shards = [w[0].astype(BF16) for w in big] + [conv_w[0]]
    gathered = allgather_chips(shards, [True] * 6 + [False], name="allgather_weights")
    a_in, a_of, a_od, a_out, a_up, a_down, a_cw = [
        lax.dynamic_update_index_in_dim(a4, own, chip, 0) for a4, own in zip(gathered, shards)]
    w_in_full = jnp.concatenate([a_in[j] for j in range(4)], axis=1)
    nf = N_HEADS
    e_a, e_b = 3 * ATT_W, 3 * ATT_W + nf
    wz = jnp.concatenate([w_in_full[:, :e_a], w_in_full[:, e_b:]], axis=1)
    wf = jnp.pad(w_in_full[:, e_a:e_b], ((0, 0), (0, LANES - nf)))
    wo_a = jnp.concatenate([a_of[j] for j in range(4)], axis=1)
    wo_b = jnp.concatenate([a_od[j] for j in range(4)], axis=1)
    w_o = a_out.reshape(d, d)
    w_dn = a_down.reshape(f_half, d)
    wu_a = jnp.concatenate([a_up[0], a_up[1]], axis=1)
    wu_b = jnp.concatenate([a_up[2], a_up[3]], axis=1)
    cw = jnp.concatenate([a_cw[j] for j in range(4)], axis=1)
    cb = conv_b
    bfo = jnp.pad(b_forget, ((0, 0), (0, LANES - nf)))

    h1 = rmsnorm_fwd(xs, g_pre_mix)
    z = mm([(h1, d, 0)], [(wz, d, 0)], nt=False, out_dtype=BF16, tm=1024, tn=512, name="in_proj")
    fa = mm([(h1, d, 0)], [(wf, d, 0)], nt=False, out_dtype=F32, tm=1024, tn=LANES, name="in_proj_forget")
    q_aug, k_aug = fox_prep(z, fa, bfo)
    ya, lse_a = fox_fwd(q_aug, k_aug, z)
    qkv_b = rope_apply([(z, Z_QB, QK_SCALE, True), (z, Z_KB, 1.0, True), (z, Z_VB, 1.0, False)], rope_tables(s, 1.0),
                       out_dtype=F32, name="rope_fwd")
    yb, lse_b = dil_fwd_all(qkv_b)
    pa, pb, mixed = gate_mix(ya, yb, wo_a, wo_b, z)
    y1, x1 = mm_rms_res(mixed, w_o, g_post_mix, xs, tm=512, name="out_proj")
    h2 = rmsnorm_fwd(x1, g_pre_ffn)
    ua, ub, mid = ffn_up(h2, wu_a, wu_b, cw, cb)
    y2, dout, sq = mm_rms_res(mid, w_dn, g_post_ffn, x1, target, tm=512, name="down_proj")
    loss = lax.psum(0.5 * sq[0, 0] / d, ("x", "y", "c"))

    dy2, gg_post_ffn = rmsnorm_bwd(dout, y2, g_post_ffn, None, out_dtype=BF16, name="norm_bwd_post_ffn")
    dmid = mm([(dy2, d, 0)], [(w_dn, d, 0)], nt=True, out_dtype=BF16, tm=512, tn=f_half // 2, name="down_dgrad")
    dw_down = wgrad((mid, f_half, 0), dy2, tk=f_half // 2, tn=1024, ts=1024, name="down_wgrad")
    dua, dub, gc_a, gc_b = ffn_bwd(dmid, ua, ub, cw, cb)
    dh2 = mm([(dua, f_half, 0), (dub, f_half, 0)], [(wu_a, f_half, 0), (wu_b, f_half, 0)], nt=True, out_dtype=BF16,
             tm=512, tn=512, name="up_dgrad")
    dw_up = jnp.concatenate(
        [wgrad((h2, d, 0), du, tk=1024, tn=f_half // 2, ts=1024, name=f"up_wgrad_{k}", chip_major=True)
         for k, du in enumerate((dua, dub))], axis=0)
    dx1, gg_pre_ffn = rmsnorm_bwd(dh2, x1, g_pre_ffn, dout, out_dtype=F32, name="norm_bwd_pre_ffn")
    dy1, gg_post_mix = rmsnorm_bwd(dx1, y1, g_post_mix, None, out_dtype=BF16, name="norm_bwd_post_mix")
    dmixed = mm([(dy1, d, 0)], [(w_o, d, 0)], nt=True, out_dtype=BF16, tm=512, tn=512, name="out_dgrad")
    dw_out = wgrad((mixed, d, 0), dy1, tk=1024, tn=1024, ts=1024, name="out_wgrad")
    dpa, dpb, dz_g = gate_bwd(dmixed, z, pa, pb)
    dya = mm([(dpa, d, 0)], [(wo_a, d, 0)], nt=True, out_dtype=BF16, tm=512, tn=ATT_W, name="fox_o_dgrad")
    dyb = mm([(dpb, d, 0)], [(wo_b, d, 0)], nt=True, out_dtype=F32, tm=512, tn=ATT_W, name="dil_o_dgrad")
    by_chip_cols = lambda a: jnp.stack([a[:, j * (d // 4):(j + 1) * (d // 4)] for j in range(4)], axis=0)
    dw_of = by_chip_cols(wgrad((ya, ATT_W, 0), dpa, tk=ATT_W, tn=d, ts=1024, name="fox_o_wgrad"))
    dw_od = by_chip_cols(wgrad((yb, ATT_W, 0), dpb, tk=ATT_W, tn=d, ts=1024, name="dil_o_wgrad"))
    dd_a = head_rowsum(dya, ya, name="fox_delta")
    dq_aug, dk_aug, dv_a = fox_bwd(q_aug, k_aug, z, dya, lse_a, dd_a)
    dz_a, dfa, gg_bf = fox_post(dq_aug, dk_aug, dv_a, fa, bfo)
    dd_b = head_rowsum(dyb, yb, name="dil_delta")
    dq_b, dk_b, dv_b = dil_bwd_all(qkv_b, dyb, lse_b, dd_b)
    dz_b = rope_apply([(dq_b, 0, QK_SCALE, True), (dk_b, 0, 1.0, True), (dv_b, 0, 1.0, False)],
                      rope_tables(s, -1.0), out_dtype=BF16, name="rope_bwd")
    dh1 = mm([(dz_a, e_a, 0), (dz_b, e_a, 0), (dz_g, d, 0), (dz_g, d, 1), (dfa, LANES, 0)],
             [(wz, e_a, 0), (wz, e_a, 1), (wz, d, 3), (wz, d, 4), (wf, LANES, 0)], nt=True, out_dtype=BF16,
             tm=512, tn=512, name="in_dgrad")
    dw_a = wgrad((h1, d, 0), dz_a, tk=1024, tn=e_a // 2, ts=1024, name="in_wgrad_a")
    dw_b = wgrad((h1, d, 0), dz_b, tk=1024, tn=e_a // 2, ts=1024, name="in_wgrad_b")
    dw_g = wgrad((h1, d, 0), dz_g, tk=1024, tn=1024, ts=1024, name="in_wgrad_g")
    dw_f = wgrad((h1, d, 0), dfa, tk=1024, tn=LANES, ts=1024, name="in_wgrad_f")
    grad_x, gg_pre_mix = rmsnorm_bwd(dh1, xs, g_pre_mix, dx1, out_dtype=F32, name="norm_bwd_pre_mix")
    dw_in_full = jnp.concatenate([dw_a, dw_f[:, :nf], dw_b, dw_g], axis=1)
    dw_in = jnp.stack([dw_in_full[:, j * cols_in:(j + 1) * cols_in] for j in range(4)], axis=0)

    gs = [dw_in, dw_of, dw_od, dw_out.reshape(4, d // 4, d), dw_up, dw_down.reshape(4, f_half // 4, d)]
    names = ("w_in", "w_o_fox", "w_o_dil", "w_out", "w_up", "w_down")
    from_sib = grads_to_sibling(gs, name="grads_to_sibling")
    sums = [chip_sum(g, r, c_arr, name=f"chip_sum_{nm}") for g, r, nm in zip(gs, from_sib, names)]
    from_chips = grads_to_chips([p[1] for p in sums], name="grads_to_chips")
    halves = [final_sum(p[0], r, chip_arr, name=f"final_sum_{nm}") for p, r, nm in zip(sums, from_chips, names)]
    from_half = halves_to_full(halves, name="halves_to_full")
    g_big = [lax.dynamic_update_slice_in_dim(full, mine, ci * mine.shape[0], axis=0) for full, mine in zip(from_half, halves)]
    upd_big = [adamw(w[0], g, m[0], v[0], name=f"adamw_{nm}") for w, g, m, v, nm in list(zip(
        big, g_big, (m_w_in, m_w_o_fox, m_w_o_dil, m_w_out, m_w_up, m_w_down),
        (v_w_in, v_w_o_fox, v_w_o_dil, v_w_out, v_w_up, v_w_down), names))[1:]]
    to_t = lambda a: jnp.transpose(a, (2, 0, 1))
    from_t = lambda a: jnp.transpose(a, (1, 2, 0))
    g_in_t = to_t(g_big[0][None])
    upd_in = adamw(to_t(w_in), g_in_t, to_t(m_w_in), to_t(v_w_in), name="adamw_w_in", tr=cols_in // 2)

    g_cw_loc = jnp.concatenate([gc_a[0:3], gc_b[0:3]], axis=1)
    g_cb_loc = jnp.concatenate([gc_a[3:4], gc_b[3:4]], axis=1)
    small_loc = [gg_pre_mix, gg_post_mix, gg_pre_ffn, gg_post_ffn, g_cb_loc, gg_bf[:, :nf], g_cw_loc]
    red_rows = (8, 8, 8, 8, 48, 8, 136)
    red = allreduce_small(_pack_rows(small_loc, red_rows), name="allreduce_small")
    g_pm, g_qm, g_pf, g_qf, g_cb, g_bf, g_cw_full = _unpack_rows(red, [a.shape for a in small_loc], red_rows)
    cols_cw = conv_w.shape[2]
    g_cw = lax.dynamic_slice_in_dim(g_cw_full, chip * cols_cw, cols_cw, axis=1)
    small_w = (g_pre_mix, g_post_mix, g_pre_ffn, g_post_ffn, conv_b, b_forget, conv_w[0])
    small_m = (m_g_pre_mix, m_g_post_mix, m_g_pre_ffn, m_g_post_ffn, m_conv_b, m_b_forget, m_conv_w[0])
    small_v = (v_g_pre_mix, v_g_post_mix, v_g_pre_ffn, v_g_post_ffn, v_conv_b, v_b_forget, v_conv_w[0])
    small_g = (g_pm, g_qm, g_pf, g_qf, g_cb, g_bf, g_cw)
    ad_rows = (8, 8, 8, 8, 48, 8, 40)
    packed = [_pack_rows(t, ad_rows) for t in (small_w, small_g, small_m, small_v)]
    upd_small = [_unpack_rows(o, [a.shape for a in small_w], ad_rows) for o in adamw(*packed, name="adamw_small")]

    order = ("g_pre_mix", "w_in", "b_forget", "w_o_fox", "w_o_dil", "w_out", "g_post_mix", "g_pre_ffn", "w_up", "conv_w",
             "conv_b", "w_down", "g_post_ffn")
    small_names = ("g_pre_mix", "g_post_mix", "g_pre_ffn", "g_post_ffn", "conv_b", "b_forget", "conv_w")
    grads, deltas, new_ms, new_vs = {}, {}, {}, {}
    grads["w_in"] = from_t(g_in_t)
    deltas["w_in"], new_ms["w_in"], new_vs["w_in"] = (from_t(a) for a in upd_in)
    for k, nm in enumerate(names[1:]):
        grads[nm] = g_big[k + 1][None]
        deltas[nm], new_ms[nm], new_vs[nm] = (a[None] for a in upd_big[k])
    for k, nm in enumerate(small_names):
        lead = (lambda a: a[None]) if nm == "conv_w" else (lambda a: a)
        grads[nm] = lead(small_g[k])
        deltas[nm], new_ms[nm], new_vs[nm] = (lead(upd_small[j][k]) for j in range(3))
    return (loss, grad_x[None], *[grads[nm] for nm in order], *[deltas[nm] for nm in order],
            *[new_ms[nm] for nm in order], *[new_vs[nm] for nm in order])
```

```python
import functools
import math

import numpy as np
import jax
import jax.numpy as jnp
from jax import lax
from jax.experimental import pallas as pl
from jax.experimental.pallas import tpu as pltpu

F32 = jnp.float32
BF16 = jnp.bfloat16
SDS = jax.ShapeDtypeStruct
MESH = pl.DeviceIdType.MESH

HEAD_DIM = 64
N_HEADS = 8
LANES = 128
ATT_W = N_HEADS * HEAD_DIM
DIL_PATTERNS = ((128, 1), (512, 4), (2048, 16))
DIL_BLK = 128
ROPE_DIM = HEAD_DIM // 4
ROPE_THETA = 500000.0
RMS_EPS = 1e-6
NEG = -1e30
QK_SCALE = 1.0 / math.sqrt(HEAD_DIM)
ADAM_LR, ADAM_B1, ADAM_B2, ADAM_EPS, ADAM_WD, ADAM_STEP = 0.001, 0.9, 0.999, 1e-08, 0.01, 10
VMEM_LIMIT = 56 * 1024 * 1024

Z_QA, Z_KA, Z_VA, Z_QB, Z_KB, Z_VB = 0, 1, 2, 3, 4, 5
Z_W = 5120


def _cp(sem):
    return pltpu.CompilerParams(dimension_semantics=sem, vmem_limit_bytes=VMEM_LIMIT)


def _nt(a, b):
    return lax.dot_general(a, b, (((1,), (1,)), ((), ())), preferred_element_type=F32)


def _tn(a, b):
    return lax.dot_general(a, b, (((0,), (0,)), ((), ())), preferred_element_type=F32)


def _nn(a, b):
    return jnp.dot(a, b, preferred_element_type=F32)


def _lane(shape):
    return lax.broadcasted_iota(jnp.int32, shape, 1)


def _row(shape):
    return lax.broadcasted_iota(jnp.int32, shape, 0)


def rmsnorm_fwd(x, g, *, tm=512):
    s, d = x.shape

    def body(x_ref, g_ref, h_ref):
        xv = x_ref[...]
        inv = lax.rsqrt(jnp.mean(xv * xv, axis=-1, keepdims=True) + RMS_EPS)
        h_ref[...] = (xv * inv * g_ref[...]).astype(h_ref.dtype)

    return pl.pallas_call(
        body, grid=(s // tm,),
        in_specs=[pl.BlockSpec((tm, d), lambda i: (i, 0)), pl.BlockSpec((1, d), lambda i: (0, 0))],
        out_specs=pl.BlockSpec((tm, d), lambda i: (i, 0)),
        out_shape=SDS((s, d), BF16), name="rmsnorm_fwd", compiler_params=_cp(("parallel",)))(x, g)


def rmsnorm_bwd(dh, x, g, res, *, out_dtype, tm=256, name):
    s, d = x.shape
    n = s // tm
    has_res = res is not None

    def body(*refs):
        if has_res:
            dh_ref, x_ref, g_ref, res_ref, dx_ref, dg_ref, acc = refs
        else:
            dh_ref, x_ref, g_ref, dx_ref, dg_ref, acc = refs
        i = pl.program_id(0)

        @pl.when(i == 0)
        def _():
            acc[...] = jnp.zeros_like(acc)

        xv = x_ref[...]
        inv = lax.rsqrt(jnp.mean(xv * xv, axis=-1, keepdims=True) + RMS_EPS)
        xh = xv * inv
        dhv = dh_ref[...].astype(F32)
        dxh = dhv * g_ref[...]
        dot = jnp.mean(dxh * xh, axis=-1, keepdims=True)
        dx = inv * (dxh - xh * dot)
        if has_res:
            dx = dx + res_ref[...]
        dx_ref[...] = dx.astype(dx_ref.dtype)
        acc[...] += jnp.sum((dhv * xh).reshape(tm // 8, 8, d), axis=0)

        @pl.when(i == n - 1)
        def _():
            dg_ref[...] = jnp.sum(acc[...], axis=0, keepdims=True)

    row = pl.BlockSpec((tm, d), lambda i: (i, 0))
    in_specs = [row, row, pl.BlockSpec((1, d), lambda i: (0, 0))] + ([row] if has_res else [])
    args = [dh, x, g] + ([res] if has_res else [])
    return pl.pallas_call(
        body, grid=(n,), in_specs=in_specs,
        out_specs=[row, pl.BlockSpec((1, d), lambda i: (0, 0))],
        out_shape=[SDS((s, d), out_dtype), SDS((1, d), F32)],
        scratch_shapes=[pltpu.VMEM((8, d), F32)],
        name=name, compiler_params=_cp(("arbitrary",)))(*args)


def mm(a_views, b_views, *, nt, out_dtype, tm, tn, name):
    n_p = len(a_views)
    m = a_views[0][0].shape[0]
    n = b_views[0][0].shape[0] if nt else b_views[0][0].shape[1]

    def body(*refs):
        o_ref = refs[-1]
        acc = None
        for p in range(n_p):
            av = refs[p][...].astype(BF16)
            bv = refs[n_p + p][...].astype(BF16)
            dv = _nt(av, bv) if nt else _nn(av, bv)
            acc = dv if acc is None else acc + dv
        o_ref[...] = acc.astype(o_ref.dtype)

    in_specs = []
    for arr, w, blk in a_views:
        in_specs.append(pl.BlockSpec((tm, w), functools.partial(lambda i, j, blk: (i, blk), blk=blk)))
    for arr, w, blk in b_views:
        if nt:
            in_specs.append(pl.BlockSpec((tn, w), functools.partial(lambda i, j, blk: (j, blk), blk=blk)))
        else:
            in_specs.append(pl.BlockSpec((w, tn), lambda i, j: (0, j)))
    return pl.pallas_call(
        body, grid=(m // tm, n // tn), in_specs=in_specs,
        out_specs=pl.BlockSpec((tm, tn), lambda i, j: (i, j)),
        out_shape=SDS((m, n), out_dtype), name=name,
        compiler_params=_cp(("parallel", "parallel")))(*[a[0] for a in a_views], *[b[0] for b in b_views])


def wgrad(a_view, g, *, tk, tn, ts, name, chip_major=False):
    arr, ka, blk = a_view
    s, n = g.shape
    ns = s // ts

    def body(a_ref, g_ref, o_ref):
        @pl.when(pl.program_id(2) == 0)
        def _():
            o_ref[...] = jnp.zeros_like(o_ref)

        o_ref[...] += _tn(a_ref[...].astype(BF16), g_ref[...].astype(BF16))

    if chip_major:
        out_spec = pl.BlockSpec((None, tk, tn), lambda i, j, k: (j, i, 0))
        out_shape = SDS((n // tn, ka, tn), F32)
    else:
        out_spec = pl.BlockSpec((tk, tn), lambda i, j, k: (i, j))
        out_shape = SDS((ka, n), F32)
    return pl.pallas_call(
        body, grid=(ka // tk, n // tn, ns),
        in_specs=[pl.BlockSpec((ts, tk), lambda i, j, k: (k, blk * (ka // tk) + i)),
                  pl.BlockSpec((ts, tn), lambda i, j, k: (k, j))],
        out_specs=out_spec, out_shape=out_shape, name=name,
        compiler_params=_cp(("parallel", "parallel", "arbitrary")))(arr, g)


def mm_rms_res(a, w, g, xres, target=None, *, tm=256, name):
    s, k = a.shape
    d = w.shape[1]
    n = s // tm
    with_loss = target is not None

    def body(*refs):
        if with_loss:
            a_ref, w_ref, g_ref, x_ref, t_ref, y_ref, o_ref, l_ref = refs
        else:
            a_ref, w_ref, g_ref, x_ref, y_ref, o_ref = refs
        y = _nn(a_ref[...], w_ref[...])
        inv = lax.rsqrt(jnp.mean(y * y, axis=-1, keepdims=True) + RMS_EPS)
        xn = x_ref[...] + y * inv * g_ref[...]
        y_ref[...] = y
        if with_loss:
            err = xn - t_ref[...]
            o_ref[...] = err * (1.0 / d)

            @pl.when(pl.program_id(0) == 0)
            def _():
                l_ref[...] = jnp.zeros_like(l_ref)

            l_ref[...] += jnp.sum(jnp.sum(err * err, axis=1, keepdims=True), axis=0, keepdims=True)
        else:
            o_ref[...] = xn

    row = pl.BlockSpec((tm, d), lambda i: (i, 0))
    in_specs = [pl.BlockSpec((tm, k), lambda i: (i, 0)), pl.BlockSpec((k, d), lambda i: (0, 0)),
                pl.BlockSpec((1, d), lambda i: (0, 0)), row]
    out_specs = [row, row]
    out_shape = [SDS((s, d), F32), SDS((s, d), F32)]
    args = [a, w, g, xres]
    if with_loss:
        in_specs.append(row)
        out_specs.append(pl.BlockSpec((1, 1), lambda i: (0, 0)))
        out_shape.append(SDS((1, 1), F32))
        args.append(target)
    return pl.pallas_call(
        body, grid=(n,), in_specs=in_specs, out_specs=out_specs, out_shape=out_shape, name=name,
        compiler_params=_cp(("arbitrary",)))(*args)


def _split3(v):
    hi = v.astype(BF16).astype(F32)
    r = v - hi
    mid = r.astype(BF16).astype(F32)
    lo = (r - mid).astype(BF16).astype(F32)
    return hi, mid, lo


def _tri(n, upper):
    r = np.arange(n)
    m = (r[:, None] <= r[None, :]) if upper else (r[:, None] >= r[None, :])
    return jnp.asarray(m.astype(np.float32))


def fox_prep(z, fa, bfo, *, tb=512):
    s = z.shape[0]
    n = s // tb

    def body(q_ref, k_ref, fa_ref, b_ref, tri_ref, qa_ref, ka_ref, carry):
        @pl.when(pl.program_id(0) == 0)
        def _():
            carry[...] = jnp.zeros_like(carry)

        xv = fa_ref[...] + b_ref[...]
        logf = jnp.minimum(xv, 0.0) - jnp.log(1.0 + jnp.exp(-jnp.abs(xv)))
        csum = jnp.dot(tri_ref[...], logf, preferred_element_type=F32, precision=lax.Precision.HIGHEST) + carry[0:1, :]
        carry[0:1, :] = csum[tb - 1:tb, :]
        lane = _lane((tb, LANES))
        for h in range(N_HEADS):
            hi, mid, lo = _split3(csum[:, h:h + 1])
            pair = (h // 2) * LANES
            qv = q_ref[:, pair:pair + LANES].astype(F32)
            kv = k_ref[:, pair:pair + LANES].astype(F32)
            if h % 2:
                qv = pltpu.roll(qv, 64, axis=1)
                kv = pltpu.roll(kv, 64, axis=1)
            one = jnp.where((lane >= 67) & (lane < 70), 1.0, 0.0)
            q_x = jnp.where(lane == 64, hi, jnp.where(lane == 65, mid, jnp.where(lane == 66, lo, one)))
            one = jnp.where((lane >= 64) & (lane < 67), 1.0, 0.0)
            k_x = jnp.where(lane == 67, -hi, jnp.where(lane == 68, -mid, jnp.where(lane == 69, -lo, one)))
            qa_ref[:, h * LANES:(h + 1) * LANES] = jnp.where(lane < 64, qv * QK_SCALE, q_x).astype(BF16)
            ka_ref[:, h * LANES:(h + 1) * LANES] = jnp.where(lane < 64, kv, k_x).astype(BF16)

    return pl.pallas_call(
        body, grid=(n,),
        in_specs=[pl.BlockSpec((tb, ATT_W), lambda i: (i, Z_QA)), pl.BlockSpec((tb, ATT_W), lambda i: (i, Z_KA)),
                  pl.BlockSpec((tb, LANES), lambda i: (i, 0)), pl.BlockSpec((1, LANES), lambda i: (0, 0)),
                  pl.BlockSpec((tb, tb), lambda i: (0, 0))],
        out_specs=[pl.BlockSpec((tb, N_HEADS * LANES), lambda i: (i, 0))] * 2,
        out_shape=[SDS((s, N_HEADS * LANES), BF16)] * 2,
        scratch_shapes=[pltpu.VMEM((8, LANES), F32)],
        name="fox_prep", compiler_params=_cp(("arbitrary",)))(z, z, fa, bfo, _tri(tb, False))


def _causal_pairs(n, k_major):
    if k_major:
        pairs = [(qi, kj) for kj in range(n) for qi in range(kj, n)]
    else:
        pairs = [(qi, kj) for qi in range(n) for kj in range(qi + 1)]
    return (jnp.asarray([p[0] for p in pairs], jnp.int32), jnp.asarray([p[1] for p in pairs], jnp.int32), len(pairs))


def fox_fwd(q_aug, k_aug, z, gather=(), *, t=512):
    s = z.shape[0]
    qi_arr, kj_arr, n_pairs = _causal_pairs(s // t, False)
    ng = len(gather)

    def body(qi_ref, kj_ref, q_ref, k_ref, v_ref, *rest):
        g_ins, (o_ref, lse_ref), g_outs = rest[:ng], rest[ng:ng + 2], rest[ng + 2:2 * ng + 2]
        m_scr, l_scr, acc_scr = rest[2 * ng + 2:2 * ng + 5]
        comm = (g_ins, g_outs) + tuple(rest[2 * ng + 5:]) + ([True] * ng,)
        step = pl.program_id(1)
        qi = qi_ref[step]
        kj = kj_ref[step]
        if ng:
            @pl.when((pl.program_id(0) == 0) & (step == 0))
            def _():
                _allgather_start(*comm)

        @pl.when(kj == 0)
        def _():
            m_scr[...] = jnp.full_like(m_scr, NEG)
            l_scr[...] = jnp.zeros_like(l_scr)
            acc_scr[...] = jnp.zeros_like(acc_scr)

        def update(masked):
            for i in range(2):
                sc = _nt(q_ref[:, i * LANES:(i + 1) * LANES], k_ref[:, i * LANES:(i + 1) * LANES])
                if masked:
                    sc = jnp.where(_row((t, t)) >= _lane((t, t)), sc, NEG)
                m_prev = m_scr[i]
                m_new = jnp.maximum(m_prev, jnp.max(sc, axis=-1, keepdims=True))
                alpha = jnp.exp(m_prev - m_new)
                p = jnp.exp(sc - jnp.tile(m_new, (1, t // LANES)))
                l_scr[i] = alpha * l_scr[i] + jnp.sum(p, axis=-1, keepdims=True)
                acc_scr[i] = alpha * acc_scr[i] + _nn(p.astype(BF16), v_ref[...])
                m_scr[i] = m_new

        @pl.when(kj < qi)
        def _():
            update(False)

        @pl.when(kj == qi)
        def _():
            update(True)
            lane = _lane((t, LANES))
            o_ref[...] = jnp.where(lane < 64, acc_scr[0] / l_scr[0], acc_scr[1] / l_scr[1]).astype(o_ref.dtype)
            lse_ref[...] = jnp.where(lane < 64, m_scr[0] + jnp.log(l_scr[0]), m_scr[1] + jnp.log(l_scr[1]))

        if ng:
            @pl.when((pl.program_id(0) == 3) & (step == n_pairs - 1))
            def _():
                _allgather_finish(*comm)

    grid_spec = pltpu.PrefetchScalarGridSpec(
        num_scalar_prefetch=2, grid=(4, n_pairs),
        in_specs=[pl.BlockSpec((t, 2 * LANES), lambda hp, st, qi, kj: (qi[st], hp)),
                  pl.BlockSpec((t, 2 * LANES), lambda hp, st, qi, kj: (kj[st], hp)),
                  pl.BlockSpec((t, LANES), lambda hp, st, qi, kj: (kj[st], 4 * Z_VA + hp))] + [ANY] * ng,
        out_specs=[pl.BlockSpec((t, LANES), lambda hp, st, qi, kj: (qi[st], hp))] * 2 + [ANY] * ng,
        scratch_shapes=[pltpu.VMEM((2, t, LANES), F32)] * 3 + (_allgather_sems(ng) if ng else []))
    return pl.pallas_call(
        body, grid_spec=grid_spec, out_shape=[SDS((s, ATT_W), BF16), SDS((s, ATT_W), F32)] + _allgather_shapes(gather),
        name="fox_fwd", compiler_params=_cp(("arbitrary", "arbitrary")))(qi_arr, kj_arr, q_aug, k_aug, z, *gather)


def fox_bwd(q_aug, k_aug, z, dy, lse, dd, exchange=(), *, t=512):
    s = z.shape[0]
    qi_arr, kj_arr, n_pairs = _causal_pairs(s // t, True)
    ne = len(exchange)

    def body(qi_ref, kj_ref, q_ref, k_ref, v_ref, do_ref, lse_ref, dd_ref, *rest):
        e_ins, (dq_ref, dk_ref, dv_ref), e_outs = rest[:ne], rest[ne:ne + 3], rest[ne + 3:2 * ne + 3]
        comm = (e_ins, e_outs) + tuple(rest[2 * ne + 3:])
        step = pl.program_id(1)
        qi = qi_ref[step]
        kj = kj_ref[step]
        if ne:
            @pl.when((pl.program_id(0) == 0) & (step == 0))
            def _():
                _to_chips_start(*comm)

        @pl.when(step == 0)
        def _():
            dq_ref[...] = jnp.zeros_like(dq_ref)

        @pl.when(qi == kj)
        def _():
            dk_ref[...] = jnp.zeros_like(dk_ref)
            dv_ref[...] = jnp.zeros_like(dv_ref)

        def update(masked):
            lane = _lane((t, LANES))
            rows = pl.ds(pl.multiple_of(qi * t, t), t)
            dov = do_ref[...]
            dv_new = None
            for i in range(2):
                head = (lane < 64) if i == 0 else (lane >= 64)
                qv = q_ref[:, i * LANES:(i + 1) * LANES]
                kv = k_ref[:, i * LANES:(i + 1) * LANES]
                sc = _nt(qv, kv)
                if masked:
                    sc = jnp.where(_row((t, t)) >= _lane((t, t)), sc, NEG)
                p = jnp.exp(sc - lse_ref[:, i * 64:i * 64 + 1])
                dp = _nt(jnp.where(head, dov, jnp.zeros_like(dov)), v_ref[...])
                ds = (p * (dp - dd_ref[:, i * 64:i * 64 + 1])).astype(BF16)
                dq_ref[rows, i * LANES:(i + 1) * LANES] += _nn(ds, kv)
                dk_ref[:, i * LANES:(i + 1) * LANES] += _tn(ds, qv)
                dvi = _tn(p.astype(BF16), dov)
                dv_new = dvi if dv_new is None else jnp.where(head, dvi, dv_new)
            dv_ref[...] += dv_new

        @pl.when(kj < qi)
        def _():
            update(False)

        @pl.when(kj == qi)
        def _():
            update(True)

        if ne:
            @pl.when((pl.program_id(0) == 3) & (step == n_pairs - 1))
            def _():
                _to_chips_finish(*comm)

    grid_spec = pltpu.PrefetchScalarGridSpec(
        num_scalar_prefetch=2, grid=(4, n_pairs),
        in_specs=[pl.BlockSpec((t, 2 * LANES), lambda hp, st, qi, kj: (qi[st], hp)),
                  pl.BlockSpec((t, 2 * LANES), lambda hp, st, qi, kj: (kj[st], hp)),
                  pl.BlockSpec((t, LANES), lambda hp, st, qi, kj: (kj[st], 4 * Z_VA + hp)),
                  pl.BlockSpec((t, LANES), lambda hp, st, qi, kj: (qi[st], hp)),
                  pl.BlockSpec((t, LANES), lambda hp, st, qi, kj: (qi[st], hp)),
                  pl.BlockSpec((t, LANES), lambda hp, st, qi, kj: (qi[st], hp))] + [ANY] * ne,
        out_specs=[pl.BlockSpec((s, 2 * LANES), lambda hp, st, qi, kj: (0, hp)),
                   pl.BlockSpec((t, 2 * LANES), lambda hp, st, qi, kj: (kj[st], hp)),
                   pl.BlockSpec((t, LANES), lambda hp, st, qi, kj: (kj[st], hp))] + [ANY] * ne,
        scratch_shapes=_to_chips_sems(ne) if ne else [])
    return pl.pallas_call(
        body, grid_spec=grid_spec,
        out_shape=[SDS((s, N_HEADS * LANES), F32), SDS((s, N_HEADS * LANES), F32), SDS((s, ATT_W), F32)]
        + _to_chips_shapes(exchange),
        name="fox_bwd", compiler_params=_cp(("arbitrary", "arbitrary")))(qi_arr, kj_arr, q_aug, k_aug, z, dy, lse, dd, *exchange)


def head_rowsum(a, b, *, tm=512, name):
    s = a.shape[0]

    def body(a_ref, b_ref, o_ref):
        prod = a_ref[...].astype(F32) * b_ref[...].astype(F32)
        lane = _lane((tm, LANES))
        lo = jnp.sum(jnp.where(lane < 64, prod, 0.0), axis=-1, keepdims=True)
        hi = jnp.sum(jnp.where(lane >= 64, prod, 0.0), axis=-1, keepdims=True)
        o_ref[...] = jnp.where(lane < 64, lo, hi)

    blk = pl.BlockSpec((tm, LANES), lambda i, j: (i, j))
    return pl.pallas_call(body, grid=(s // tm, 4), in_specs=[blk, blk], out_specs=blk, out_shape=SDS((s, ATT_W), F32),
                          name=name, compiler_params=_cp(("parallel", "parallel")))(a, b)


def fox_post(dq_aug, dk_aug, dv, fa, bfo, *, tb=512):
    s = dv.shape[0]
    n = s // tb

    def body(dq_ref, dk_ref, dv_ref, fa_ref, b_ref, tri_ref, dz_ref, dfa_ref, gb_ref, carry, acc):
        i = pl.program_id(0)

        @pl.when(i == 0)
        def _():
            carry[...] = jnp.zeros_like(carry)
            acc[...] = jnp.zeros_like(acc)

        lane = _lane((tb, LANES))
        d_f = jnp.zeros((tb, LANES), F32)
        for h in range(N_HEADS):
            col = dq_ref[:, h * LANES + 64:h * LANES + 65] - dk_ref[:, h * LANES + 67:h * LANES + 68]
            d_f = jnp.where(lane == h, col, d_f)
        suffix = jnp.dot(tri_ref[...], d_f, preferred_element_type=F32, precision=lax.Precision.HIGHEST) + carry[0:1, :]
        carry[0:1, :] = suffix[0:1, :]
        xv = fa_ref[...] + b_ref[...]
        dx = suffix * (1.0 / (1.0 + jnp.exp(xv)))
        dfa_ref[...] = dx.astype(dfa_ref.dtype)
        acc[...] += jnp.sum(dx.reshape(tb // 8, 8, LANES), axis=0)
        for hp in range(4):
            for src, off, scale in ((dq_ref, 0, QK_SCALE), (dk_ref, ATT_W, 1.0)):
                even = src[:, (2 * hp) * LANES:(2 * hp + 1) * LANES]
                odd = pltpu.roll(src[:, (2 * hp + 1) * LANES:(2 * hp + 2) * LANES], 64, axis=1)
                dz_ref[:, off + hp * LANES:off + (hp + 1) * LANES] = (jnp.where(lane < 64, even, odd) * scale).astype(BF16)
        dz_ref[:, 2 * ATT_W:3 * ATT_W] = dv_ref[...].astype(BF16)

        @pl.when(i == n - 1)
        def _():
            gb_ref[...] = jnp.sum(acc[...], axis=0, keepdims=True)

    rev = lambda i: (n - 1 - i, 0)
    return pl.pallas_call(
        body, grid=(n,),
        in_specs=[pl.BlockSpec((tb, N_HEADS * LANES), rev), pl.BlockSpec((tb, N_HEADS * LANES), rev),
                  pl.BlockSpec((tb, ATT_W), rev), pl.BlockSpec((tb, LANES), rev),
                  pl.BlockSpec((1, LANES), lambda i: (0, 0)), pl.BlockSpec((tb, tb), lambda i: (0, 0))],
        out_specs=[pl.BlockSpec((tb, 3 * ATT_W), rev), pl.BlockSpec((tb, LANES), rev),
                   pl.BlockSpec((1, LANES), lambda i: (0, 0))],
        out_shape=[SDS((s, 3 * ATT_W), BF16), SDS((s, LANES), BF16), SDS((1, LANES), F32)],
        scratch_shapes=[pltpu.VMEM((8, LANES), F32), pltpu.VMEM((8, LANES), F32)],
        name="fox_post", compiler_params=_cp(("arbitrary",)))(dq_aug, dk_aug, dv, fa, bfo, _tri(tb, True))


def rope_tables(s, sign):
    half = ROPE_DIM // 2
    inv_freq = ROPE_THETA ** (-jnp.arange(half, dtype=F32) * 2.0 / ROPE_DIM)
    ang = jnp.arange(s, dtype=F32)[:, None] * inv_freq[None, :]
    l64 = np.arange(LANES) % HEAD_DIM
    cos = jnp.cos(ang)[:, l64 % half]
    sin = jnp.sin(ang)[:, l64 % half] * sign
    first = jnp.asarray(l64 < half)[None, :]
    second = jnp.asarray((l64 >= half) & (l64 < ROPE_DIM))[None, :]
    return (jnp.where(first | second, cos, 1.0), jnp.where(first, -sin, 0.0), jnp.where(second, sin, 0.0))


def rope_apply(items, tabs, *, out_dtype, tm=512, name):
    s = items[0][0].shape[0]
    n_i = len(items)

    def body(*refs):
        c_ref, sn_ref, sp_ref = refs[n_i:n_i + 3]
        o_ref = refs[-1]
        for j, (_, _, scale, rotate) in enumerate(items):
            for b in range(4):
                xv = refs[j][:, b * LANES:(b + 1) * LANES].astype(F32)
                if rotate:
                    xv = xv * c_ref[...] + pltpu.roll(xv, LANES - 8, axis=1) * sn_ref[...] + pltpu.roll(xv, 8, axis=1) * sp_ref[...]
                o_ref[:, j * ATT_W + b * LANES:j * ATT_W + (b + 1) * LANES] = (xv * scale).astype(o_ref.dtype)

    in_specs = [pl.BlockSpec((tm, ATT_W), functools.partial(lambda i, blk: (i, blk), blk=it[1])) for it in items]
    in_specs += [pl.BlockSpec((tm, LANES), lambda i: (i, 0))] * 3
    return pl.pallas_call(
        body, grid=(s // tm,), in_specs=in_specs, out_specs=pl.BlockSpec((tm, n_i * ATT_W), lambda i: (i, 0)),
        out_shape=SDS((s, n_i * ATT_W), out_dtype), name=name, compiler_params=_cp(("parallel",)))(*[it[0] for it in items], *tabs)


def _dil_views(qk, z, r):
    s = z.shape[0]
    return qk.reshape(s // r, r * 2 * ATT_W), z.reshape(s // r, r * Z_W)


def _dil_cols(r):
    q_col = lambda rho, hp: rho * 8 + hp
    k_col = lambda rho, hp: rho * 8 + 4 + hp
    v_col = lambda rho, hp: rho * (Z_W // LANES) + 4 * Z_VB + hp
    return q_col, k_col, v_col


def _dil_scores(qv, kp, kc, head, has_prev):
    b = DIL_BLK
    qm = jnp.where(head, qv, jnp.zeros_like(qv))
    row, col = _row((b, b)), _lane((b, b))
    sp = jnp.where((col >= row) & has_prev, _nt(qm, kp), NEG)
    sc = jnp.where(col <= row, _nt(qm, kc), NEG)
    return sp, sc


def dil_fwd(qk, z, prev, *, r):
    s = z.shape[0]
    b = DIL_BLK
    l_sub = s // r
    nb = l_sub // b
    qk_v, z_v = _dil_views(qk, z, r)
    q_col, k_col, v_col = _dil_cols(r)
    merge = prev is not None

    def body(*refs):
        if merge:
            q_ref, kp_ref, kc_ref, vp_ref, vc_ref, op_ref, lp_ref, o_ref, l_ref = refs
        else:
            q_ref, kp_ref, kc_ref, vp_ref, vc_ref, o_ref, l_ref = refs
        has_prev = pl.program_id(2) > 0
        lane = _lane((b, LANES))
        res = []
        for i in range(2):
            head = (lane < 64) if i == 0 else (lane >= 64)
            sp, sc = _dil_scores(q_ref[...], kp_ref[...], kc_ref[...], head, has_prev)
            m = jnp.maximum(jnp.max(sp, axis=-1, keepdims=True), jnp.max(sc, axis=-1, keepdims=True))
            pp = jnp.exp(sp - m)
            pc = jnp.exp(sc - m)
            den = jnp.sum(pp, axis=-1, keepdims=True) + jnp.sum(pc, axis=-1, keepdims=True)
            ov = (_nn(pp.astype(BF16), vp_ref[...]) + _nn(pc.astype(BF16), vc_ref[...])) / den
            res.append((ov, m + jnp.log(den)))
        ov = jnp.where(lane < 64, res[0][0], res[1][0])
        lse = jnp.where(lane < 64, res[0][1], res[1][1])
        if merge:
            lp = lp_ref[...]
            m2 = jnp.maximum(lp, lse)
            wp = jnp.exp(lp - m2)
            wn = jnp.exp(lse - m2)
            ov = (wp * op_ref[...] + wn * ov) / (wp + wn)
            lse = m2 + jnp.log(wp + wn)
        o_ref[...] = ov
        l_ref[...] = lse

    blk = lambda f: pl.BlockSpec((b, LANES), f)
    in_specs = [blk(lambda rho, hp, n: (n, q_col(rho, hp))), blk(lambda rho, hp, n: (jnp.maximum(n - 1, 0), k_col(rho, hp))),
                blk(lambda rho, hp, n: (n, k_col(rho, hp))), blk(lambda rho, hp, n: (jnp.maximum(n - 1, 0), v_col(rho, hp))),
                blk(lambda rho, hp, n: (n, v_col(rho, hp)))]
    args = [qk_v, qk_v, qk_v, z_v, z_v]
    nat = blk(lambda rho, hp, n: (n, rho * 4 + hp))
    if merge:
        in_specs += [nat, nat]
        args += [prev[0].reshape(l_sub, r * ATT_W), prev[1].reshape(l_sub, r * ATT_W)]
    o, lse = pl.pallas_call(
        body, grid=(r, 4, nb), in_specs=in_specs, out_specs=[nat, nat],
        out_shape=[SDS((l_sub, r * ATT_W), F32)] * 2, name=f"dil_fwd_r{r}",
        compiler_params=_cp(("parallel", "parallel", "arbitrary")))(*args)
    return o.reshape(s, ATT_W), lse.reshape(s, ATT_W)


def dil_bwd_dq(qk, z, dy, lse, dd, acc, *, r):
    s = z.shape[0]
    b = DIL_BLK
    l_sub = s // r
    nb = l_sub // b
    qk_v, z_v = _dil_views(qk, z, r)
    q_col, k_col, v_col = _dil_cols(r)
    add = acc is not None

    def body(*refs):
        q_ref, kp_ref, kc_ref, vp_ref, vc_ref, do_ref, l_ref, dd_ref = refs[:8]
        dq_ref = refs[-1]
        has_prev = pl.program_id(2) > 0
        lane = _lane((b, LANES))
        dov = do_ref[...]
        parts = []
        for i in range(2):
            head = (lane < 64) if i == 0 else (lane >= 64)
            sp, sc = _dil_scores(q_ref[...], kp_ref[...], kc_ref[...], head, has_prev)
            lse_i = l_ref[:, i * 64:i * 64 + 1]
            dd_i = dd_ref[:, i * 64:i * 64 + 1]
            dom = jnp.where(head, dov, jnp.zeros_like(dov))
            dsp = (jnp.exp(sp - lse_i) * (_nt(dom, vp_ref[...]) - dd_i)).astype(BF16)
            dsc = (jnp.exp(sc - lse_i) * (_nt(dom, vc_ref[...]) - dd_i)).astype(BF16)
            parts.append(_nn(dsp, kp_ref[...]) + _nn(dsc, kc_ref[...]))
        dq = jnp.where(lane < 64, parts[0], parts[1])
        if add:
            dq = dq + refs[8][...]
        dq_ref[...] = dq

    blk = lambda f: pl.BlockSpec((b, LANES), f)
    nat = blk(lambda rho, hp, n: (n, rho * 4 + hp))
    in_specs = [blk(lambda rho, hp, n: (n, q_col(rho, hp))), blk(lambda rho, hp, n: (jnp.maximum(n - 1, 0), k_col(rho, hp))),
                blk(lambda rho, hp, n: (n, k_col(rho, hp))), blk(lambda rho, hp, n: (jnp.maximum(n - 1, 0), v_col(rho, hp))),
                blk(lambda rho, hp, n: (n, v_col(rho, hp))), nat, nat, nat]
    nview = lambda a: a.reshape(l_sub, r * ATT_W)
    args = [qk_v, qk_v, qk_v, z_v, z_v, nview(dy), nview(lse), nview(dd)]
    if add:
        in_specs.append(nat)
        args.append(nview(acc))
    dq = pl.pallas_call(
        body, grid=(r, 4, nb), in_specs=in_specs, out_specs=nat, out_shape=SDS((l_sub, r * ATT_W), F32),
        name=f"dil_bwd_dq_r{r}", compiler_params=_cp(("parallel", "parallel", "arbitrary")))(*args)
    return dq.reshape(s, ATT_W)


def dil_bwd_dkv(qk, z, dy, lse, dd, acc, *, r):
    s = z.shape[0]
    b = DIL_BLK
    l_sub = s // r
    nb = l_sub // b
    qk_v, z_v = _dil_views(qk, z, r)
    q_col, k_col, v_col = _dil_cols(r)
    add = acc is not None

    def body(*refs):
        k_ref, v_ref, qc_ref, qn_ref, doc_ref, don_ref, lc_ref, ln_ref, ddc_ref, ddn_ref = refs[:10]
        dk_ref, dv_ref = refs[-2:]
        has_next = pl.program_id(2) < nb - 1
        lane = _lane((b, LANES))
        row, col = _row((b, b)), _lane((b, b))
        kv = k_ref[...]
        vv = v_ref[...]
        dk_parts, dv_parts = [], []
        for i in range(2):
            head = (lane < 64) if i == 0 else (lane >= 64)
            dk_i = jnp.zeros((b, LANES), F32)
            dv_i = jnp.zeros((b, LANES), F32)
            for q_ref, do_ref, l_ref, d_ref, mask in ((qc_ref, doc_ref, lc_ref, ddc_ref, col <= row),
                                                      (qn_ref, don_ref, ln_ref, ddn_ref, (col >= row) & has_next)):
                qv = q_ref[...]
                dov = do_ref[...]
                sc = jnp.where(mask, _nt(jnp.where(head, qv, jnp.zeros_like(qv)), kv), NEG)
                p = jnp.exp(sc - l_ref[:, i * 64:i * 64 + 1])
                dp = _nt(jnp.where(head, dov, jnp.zeros_like(dov)), vv)
                ds = (p * (dp - d_ref[:, i * 64:i * 64 + 1])).astype(BF16)
                dv_i = dv_i + _tn(p.astype(BF16), dov)
                dk_i = dk_i + _tn(ds, qv)
            dk_parts.append(dk_i)
            dv_parts.append(dv_i)
        dk = jnp.where(lane < 64, dk_parts[0], dk_parts[1])
        dv = jnp.where(lane < 64, dv_parts[0], dv_parts[1])
        if add:
            dk = dk + refs[10][...]
            dv = dv + refs[11][...]
        dk_ref[...] = dk
        dv_ref[...] = dv

    blk = lambda f: pl.BlockSpec((b, LANES), f)
    nat = blk(lambda rho, hp, n: (n, rho * 4 + hp))
    nxt = blk(lambda rho, hp, n: (jnp.minimum(n + 1, nb - 1), rho * 4 + hp))
    in_specs = [blk(lambda rho, hp, n: (n, k_col(rho, hp))), blk(lambda rho, hp, n: (n, v_col(rho, hp))),
                blk(lambda rho, hp, n: (n, q_col(rho, hp))), blk(lambda rho, hp, n: (jnp.minimum(n + 1, nb - 1), q_col(rho, hp))),
                nat, nxt, nat, nxt, nat, nxt]
    nview = lambda a: a.reshape(l_sub, r * ATT_W)
    args = [qk_v, z_v, qk_v, qk_v, nview(dy), nview(dy), nview(lse), nview(lse), nview(dd), nview(dd)]
    if add:
        in_specs += [nat, nat]
        args += [nview(acc[0]), nview(acc[1])]
    dk, dv = pl.pallas_call(
        body, grid=(r, 4, nb), in_specs=in_specs, out_specs=[nat, nat],
        out_shape=[SDS((l_sub, r * ATT_W), F32)] * 2, name=f"dil_bwd_dkv_r{r}",
        compiler_params=_cp(("parallel", "parallel", "arbitrary")))(*args)
    return dk.reshape(s, ATT_W), dv.reshape(s, ATT_W)


def _dil_rows(base, r):
    if r == 1:
        return pl.ds(pl.multiple_of(base, DIL_BLK), DIL_BLK)
    return pl.ds(base, DIL_BLK, stride=r)


def _dil_block(idx, r, nb):
    shift = nb.bit_length() - 1
    rho = idx >> shift
    n = idx & (nb - 1)
    base = rho + n * (r * DIL_BLK)
    return _dil_rows(base, r), _dil_rows(jnp.maximum(base - r * DIL_BLK, rho), r), n > 0


def _cat(a, b):
    return jnp.concatenate([a, b], axis=0)


def _two_heads(v, first_head):
    zero = jnp.zeros_like(v)
    return _cat(jnp.where(first_head, v, zero), jnp.where(first_head, zero, v))


def _dil_bands():
    b = DIL_BLK
    q = _row((2 * b, 2 * b)) & (b - 1)
    col = _lane((2 * b, 2 * b))
    return (col < b) & (col >= q), (col >= b) & (col - b <= q)


def dil_fwd_all(qkv, *, unroll=2):
    s = qkv.shape[0]
    b = DIL_BLK
    n_blk = s // b

    def body(q_ref, k_ref, v_ref, o_ref, l_ref):
        first_head = _lane((b, LANES)) < 64
        band_prev, band_cur = _dil_bands()
        for g, (_, r) in enumerate(DIL_PATTERNS):
            nb = n_blk // r

            def group(it, carry, g=g, r=r, nb=nb):
                loaded = []
                for u in range(unroll):
                    rows_c, rows_p, has_prev = _dil_block(it * unroll + u, r, nb)
                    vals = [q_ref[rows_c, :].astype(BF16), k_ref[rows_p, :].astype(BF16), k_ref[rows_c, :].astype(BF16),
                            v_ref[rows_p, :].astype(BF16), v_ref[rows_c, :].astype(BF16)]
                    state = (o_ref[rows_c, :], l_ref[rows_c, :]) if g else None
                    loaded.append((rows_c, has_prev, vals, state))
                done = []
                for rows_c, has_prev, (qv, kp, kc, vp, vc), state in loaded:
                    sc = jnp.where(band_cur | (band_prev & has_prev), _nt(_two_heads(qv, first_head), _cat(kp, kc)), NEG)
                    m = jnp.max(sc, axis=-1, keepdims=True)
                    p = jnp.exp(sc - m)
                    den = jnp.sum(p, axis=-1, keepdims=True)
                    both = _nn(p.astype(BF16), _cat(vp, vc)) / den
                    lse2 = m + jnp.log(den)
                    ov = jnp.where(first_head, both[:b], both[b:])
                    lse = jnp.where(first_head, lse2[:b], lse2[b:])
                    if state is not None:
                        m2 = jnp.maximum(state[1], lse)
                        wp = jnp.exp(state[1] - m2)
                        wn = jnp.exp(lse - m2)
                        ov = (wp * state[0] + wn * ov) / (wp + wn)
                        lse = m2 + jnp.log(wp + wn)
                    done.append((rows_c, ov, lse))
                for rows_c, ov, lse in done:
                    o_ref[rows_c, :] = ov
                    l_ref[rows_c, :] = lse
                return carry

            lax.fori_loop(0, n_blk // unroll, group, 0)

    col_blk = lambda k: pl.BlockSpec((s, LANES), lambda hp: (0, 4 * k + hp))
    out = pl.BlockSpec((s, LANES), lambda hp: (0, hp))
    return pl.pallas_call(
        body, grid=(4,), in_specs=[col_blk(0), col_blk(1), col_blk(2)], out_specs=[out, out],
        out_shape=[SDS((s, ATT_W), F32)] * 2, name="dil_fwd", compiler_params=_cp(("parallel",)))(qkv, qkv, qkv)


def dil_bwd_all(qkv, dy, lse, y, exchange=(), *, unroll=2):
    s = qkv.shape[0]
    b = DIL_BLK
    n_blk = s // b
    ne = len(exchange)

    def body(q_ref, k_ref, v_ref, do_ref, l_ref, y_ref, *rest):
        e_ins, (dq_ref, dk_ref, dv_ref), e_outs = rest[:ne], rest[ne:ne + 3], rest[ne + 3:2 * ne + 3]
        comm = (e_ins, e_outs) + tuple(rest[2 * ne + 3:])
        if ne:
            @pl.when(pl.program_id(0) == 0)
            def _():
                _to_chips_start(*comm)

        dq_ref[...] = jnp.zeros_like(dq_ref)
        dk_ref[...] = jnp.zeros_like(dk_ref)
        dv_ref[...] = jnp.zeros_like(dv_ref)
        first_head = _lane((b, LANES)) < 64
        band_prev, band_cur = _dil_bands()
        for _, r in DIL_PATTERNS:
            nb = n_blk // r

            def group(it, carry, r=r, nb=nb):
                loaded = []
                for u in range(unroll):
                    rows_c, rows_p, has_prev = _dil_block(it * unroll + u, r, nb)
                    vals = [q_ref[rows_c, :].astype(BF16), k_ref[rows_p, :].astype(BF16), k_ref[rows_c, :].astype(BF16),
                            v_ref[rows_p, :].astype(BF16), v_ref[rows_c, :].astype(BF16), do_ref[rows_c, :],
                            l_ref[rows_c, :], y_ref[rows_c, :]]
                    loaded.append((rows_c, rows_p, has_prev, vals))
                done = []
                for rows_c, rows_p, has_prev, (qv, kp, kc, vp, vc, dof, lv, yv) in loaded:
                    q2 = _two_heads(qv, first_head)
                    do2 = _two_heads(dof.astype(BF16), first_head)
                    kcat, vcat = _cat(kp, kc), _cat(vp, vc)
                    lse2 = _cat(lv[:, 0:1], lv[:, 64:65])
                    dd2 = jnp.sum(_two_heads(dof * yv, first_head), axis=-1, keepdims=True)
                    p = jnp.exp(jnp.where(band_cur | (band_prev & has_prev), _nt(q2, kcat), NEG) - lse2)
                    ds = (p * (_nt(do2, vcat) - dd2)).astype(BF16)
                    dq2 = _nn(ds, kcat)
                    dkcat = _tn(ds, q2)
                    dvcat = _tn(p.astype(BF16), do2)
                    done.append((rows_c, rows_p, (jnp.where(first_head, dq2[:b], dq2[b:]), dkcat[:b], dkcat[b:],
                                                  dvcat[:b], dvcat[b:])))
                for rows_c, rows_p, (dq, dk_p, dk_c, dv_p, dv_c) in done:
                    dq_ref[rows_c, :] += dq
                    dk_ref[rows_p, :] += dk_p
                    dk_ref[rows_c, :] += dk_c
                    dv_ref[rows_p, :] += dv_p
                    dv_ref[rows_c, :] += dv_c
                return carry

            lax.fori_loop(0, n_blk // unroll, group, 0)

        if ne:
            @pl.when(pl.program_id(0) == 3)
            def _():
                _to_chips_finish(*comm)

    col_blk = lambda k: pl.BlockSpec((s, LANES), lambda hp: (0, 4 * k + hp))
    nat = pl.BlockSpec((s, LANES), lambda hp: (0, hp))
    return pl.pallas_call(
        body, grid=(4,), in_specs=[col_blk(0), col_blk(1), col_blk(2), nat, nat, nat] + [ANY] * ne,
        out_specs=[nat, nat, nat] + [ANY] * ne, out_shape=[SDS((s, ATT_W), F32)] * 3 + _to_chips_shapes(exchange),
        scratch_shapes=_to_chips_sems(ne) if ne else [], name="dil_bwd",
        compiler_params=_cp(("arbitrary",)))(qkv, qkv, qkv, dy, lse, y, *exchange)


def _sigmoid(v):
    return 1.0 / (1.0 + jnp.exp(-v))


def gate_mix(ya, yb, wa, wb, z, *, tm=512, tn=512):
    s = ya.shape[0]
    d = wa.shape[1]
    ga_blk = 3 * ATT_W * 2 // tn
    gb_blk = ga_blk + d // tn

    def body(ya_ref, yb_ref, wa_ref, wb_ref, ga_ref, gb_ref, pa_ref, pb_ref, mx_ref):
        pa = _nn(ya_ref[...], wa_ref[...])
        pb = _nn(yb_ref[...].astype(BF16), wb_ref[...])
        pa_ref[...] = pa.astype(BF16)
        pb_ref[...] = pb.astype(BF16)
        mx_ref[...] = (_sigmoid(ga_ref[...].astype(F32)) * pa + _sigmoid(gb_ref[...].astype(F32)) * pb).astype(BF16)

    out = pl.BlockSpec((tm, tn), lambda i, j: (i, j))
    return pl.pallas_call(
        body, grid=(s // tm, d // tn),
        in_specs=[pl.BlockSpec((tm, ATT_W), lambda i, j: (i, 0)), pl.BlockSpec((tm, ATT_W), lambda i, j: (i, 0)),
                  pl.BlockSpec((ATT_W, tn), lambda i, j: (0, j)), pl.BlockSpec((ATT_W, tn), lambda i, j: (0, j)),
                  pl.BlockSpec((tm, tn), lambda i, j: (i, ga_blk + j)), pl.BlockSpec((tm, tn), lambda i, j: (i, gb_blk + j))],
        out_specs=[out, out, out], out_shape=[SDS((s, d), BF16)] * 3, name="gate_mix",
        compiler_params=_cp(("parallel", "parallel")))(ya, yb, wa, wb, z, z)


def gate_bwd(dmx, z, pa, pb, *, tm=256):
    s, d = dmx.shape

    def body(dm_ref, ga_ref, gb_ref, pa_ref, pb_ref, dpa_ref, dpb_ref, dg_ref):
        dm = dm_ref[...].astype(F32)
        sa = _sigmoid(ga_ref[...].astype(F32))
        sb = _sigmoid(gb_ref[...].astype(F32))
        dpa_ref[...] = (dm * sa).astype(BF16)
        dpb_ref[...] = (dm * sb).astype(BF16)
        dg_ref[:, 0:d] = (dm * pa_ref[...].astype(F32) * sa * (1.0 - sa)).astype(BF16)
        dg_ref[:, d:2 * d] = (dm * pb_ref[...].astype(F32) * sb * (1.0 - sb)).astype(BF16)

    row = pl.BlockSpec((tm, d), lambda i: (i, 0))
    return pl.pallas_call(
        body, grid=(s // tm,),
        in_specs=[row, pl.BlockSpec((tm, d), lambda i: (i, 3)), pl.BlockSpec((tm, d), lambda i: (i, 4)), row, row],
        out_specs=[row, row, pl.BlockSpec((tm, 2 * d), lambda i: (i, 0))],
        out_shape=[SDS((s, d), BF16), SDS((s, d), BF16), SDS((s, 2 * d), BF16)], name="gate_bwd",
        compiler_params=_cp(("parallel",)))(dmx, z, z, pa, pb)


GELU_C = math.sqrt(2.0 / math.pi)


def _gelu_parts(a):
    inner = GELU_C * (a + 0.044715 * a * a * a)
    th = jnp.tanh(inner)
    gelu = 0.5 * a * (1.0 + th)
    dgelu = 0.5 * (1.0 + th) + 0.5 * a * (1.0 - th * th) * GELU_C * (1.0 + 3.0 * 0.044715 * a * a)
    return gelu, dgelu


def _causal_taps(u, before):
    row = _row(u.shape)
    r1 = jnp.where(row == 0, before[7:8, :], pltpu.roll(u, 1, axis=0))
    r2 = jnp.where(row == 0, before[6:7, :], jnp.where(row == 1, before[7:8, :], pltpu.roll(u, 2, axis=0)))
    return r1, r2


def ffn_up(h, wa, wb, cw, cb, *, tm=512, tn=256):
    s, d = h.shape
    f = wa.shape[1]
    nj = f // tn

    def body(h_ref, wa_ref, wb_ref, cwa_ref, cwb_ref, cba_ref, cbb_ref, ua_ref, ub_ref, m_ref, carry):
        @pl.when(pl.program_id(1) == 0)
        def _():
            carry[...] = jnp.zeros_like(carry)

        conv = []
        for k, (w_ref, cw_ref, cb_ref, u_ref) in enumerate(((wa_ref, cwa_ref, cba_ref, ua_ref), (wb_ref, cwb_ref, cbb_ref, ub_ref))):
            u16 = _nn(h_ref[...], w_ref[...]).astype(BF16)
            u_ref[...] = u16
            u = u16.astype(F32)
            r1, r2 = _causal_taps(u, carry[k])
            carry[k] = u[tm - 8:tm, :]
            conv.append(cw_ref[0:1, :] * r2 + cw_ref[1:2, :] * r1 + cw_ref[2:3, :] * u + cb_ref[...])
        m_ref[...] = (_gelu_parts(conv[0])[0] * conv[1]).astype(BF16)

    out = pl.BlockSpec((tm, tn), lambda j, i: (i, j))
    return pl.pallas_call(
        body, grid=(nj, s // tm),
        in_specs=[pl.BlockSpec((tm, d), lambda j, i: (i, 0)),
                  pl.BlockSpec((d, tn), lambda j, i: (0, j)), pl.BlockSpec((d, tn), lambda j, i: (0, j)),
                  pl.BlockSpec((3, tn), lambda j, i: (0, j)), pl.BlockSpec((3, tn), lambda j, i: (0, nj + j)),
                  pl.BlockSpec((1, tn), lambda j, i: (0, j)), pl.BlockSpec((1, tn), lambda j, i: (0, nj + j))],
        out_specs=[out, out, out], out_shape=[SDS((s, f), BF16)] * 3,
        scratch_shapes=[pltpu.VMEM((2, 8, tn), F32)], name="ffn_up",
        compiler_params=_cp(("parallel", "arbitrary")))(h, wa, wb, cw, cw, cb, cb)


def ffn_bwd(dm, ua, ub, cw, cb, *, tm=512, tn=256):
    s, f = dm.shape
    nj = f // tn
    ni = s // tm
    halo = 16

    def body(dm_ref, ua_ref, ub_ref, ha_ref, hb_ref, cwa_ref, cwb_ref, cba_ref, cbb_ref,
             dua_ref, dub_ref, ga_ref, gb_ref, carry):
        i = pl.program_id(1)

        @pl.when(i == 0)
        def _():
            carry[...] = jnp.zeros_like(carry)
            ga_ref[...] = jnp.zeros_like(ga_ref)
            gb_ref[...] = jnp.zeros_like(gb_ref)

        first_tile = i == ni - 1
        row = _row((tm, tn))
        dmv = dm_ref[...].astype(F32)
        us, taps, convs = [], [], []
        for u_ref, h_ref, cw_ref, cb_ref in ((ua_ref, ha_ref, cwa_ref, cba_ref), (ub_ref, hb_ref, cwb_ref, cbb_ref)):
            u = u_ref[...].astype(F32)
            before = jnp.where(first_tile, 0.0, h_ref[halo - 8:halo, :].astype(F32))
            r1, r2 = _causal_taps(u, before)
            us.append(u)
            taps.append((r1, r2))
            convs.append(cw_ref[0:1, :] * r2 + cw_ref[1:2, :] * r1 + cw_ref[2:3, :] * u + cb_ref[...])
        gelu, dgelu = _gelu_parts(convs[0])
        dcs = (dmv * convs[1] * dgelu, dmv * gelu)
        for k, (dc, cw_ref, du_ref, g_ref) in enumerate(((dcs[0], cwa_ref, dua_ref, ga_ref), (dcs[1], cwb_ref, dub_ref, gb_ref))):
            r1, r2 = taps[k]
            g_ref[0:1, :] += jnp.sum(dc * r2, axis=0, keepdims=True)
            g_ref[1:2, :] += jnp.sum(dc * r1, axis=0, keepdims=True)
            g_ref[2:3, :] += jnp.sum(dc * us[k], axis=0, keepdims=True)
            g_ref[3:4, :] += jnp.sum(dc, axis=0, keepdims=True)
            after = carry[k]
            n1 = jnp.where(row == tm - 1, after[0:1, :], pltpu.roll(dc, tm - 1, axis=0))
            n2 = jnp.where(row == tm - 2, after[0:1, :], jnp.where(row == tm - 1, after[1:2, :], pltpu.roll(dc, tm - 2, axis=0)))
            du_ref[...] = (cw_ref[2:3, :] * dc + cw_ref[1:2, :] * n1 + cw_ref[0:1, :] * n2).astype(BF16)
            carry[k] = dc[0:8, :]

    tile = pl.BlockSpec((tm, tn), lambda j, i: (ni - 1 - i, j))
    halo_spec = pl.BlockSpec((halo, tn), lambda j, i: (jnp.maximum((ni - 1 - i) * (tm // halo) - 1, 0), j))
    gspec = pl.BlockSpec((8, tn), lambda j, i: (0, j))
    return pl.pallas_call(
        body, grid=(nj, ni),
        in_specs=[tile, tile, tile, halo_spec, halo_spec,
                  pl.BlockSpec((3, tn), lambda j, i: (0, j)), pl.BlockSpec((3, tn), lambda j, i: (0, nj + j)),
                  pl.BlockSpec((1, tn), lambda j, i: (0, j)), pl.BlockSpec((1, tn), lambda j, i: (0, nj + j))],
        out_specs=[tile, tile, gspec, gspec],
        out_shape=[SDS((s, f), BF16), SDS((s, f), BF16), SDS((8, f), F32), SDS((8, f), F32)],
        scratch_shapes=[pltpu.VMEM((2, 8, tn), F32)], name="ffn_bwd",
        compiler_params=_cp(("parallel", "arbitrary")))(dm, ua, ub, ua, ub, cw, cw, cb, cb)


def adamw(w, g, m, v, *, name, tr=None):
    r = w.shape[0]
    rest = w.shape[1:]
    if tr is None:
        tr = r
        for cand in (256, 128, 64, 32, 16, 8):
            if r % cand == 0:
                tr = cand
                break

    def body(w_ref, g_ref, m_ref, v_ref, d_ref, nm_ref, nv_ref):
        gv = g_ref[...]
        mn = ADAM_B1 * m_ref[...] + (1.0 - ADAM_B1) * gv
        vn = ADAM_B2 * v_ref[...] + (1.0 - ADAM_B2) * (gv * gv)
        m_hat = mn / (1.0 - ADAM_B1 ** ADAM_STEP)
        v_hat = vn / (1.0 - ADAM_B2 ** ADAM_STEP)
        d_ref[...] = -ADAM_LR * (m_hat / (jnp.sqrt(v_hat) + ADAM_EPS) + ADAM_WD * w_ref[...])
        nm_ref[...] = mn
        nv_ref[...] = vn

    blk = pl.BlockSpec((tr,) + rest, lambda i: (i,) + (0,) * len(rest))
    return pl.pallas_call(body, grid=(r // tr,), in_specs=[blk] * 4, out_specs=[blk] * 3, out_shape=[SDS(w.shape, F32)] * 3,
                          name=name, compiler_params=_cp(("parallel",)))(w, g, m, v)


ANY = pl.BlockSpec(memory_space=pl.ANY)
ICI_KINDS = ("x", "y", "xy")


def _coords():
    return lax.axis_index("x"), lax.axis_index("y"), lax.axis_index("c")


def _peer(kind, x, y, c):
    if kind == "c":
        return (x, y, 1 - c)
    if kind == "x":
        return (1 - x, y, c)
    if kind == "y":
        return (x, 1 - y, c)
    return (1 - x, 1 - y, c)


def _chip_of(p):
    return 2 * p[0] + p[1]


def _half(rows, which):
    h = rows // 2
    return pl.ds(pl.multiple_of(which * h, 16), h)


def _remote(src, dst, send_sem, recv_sem, to):
    return pltpu.make_async_remote_copy(src_ref=src, dst_ref=dst, send_sem=send_sem, recv_sem=recv_sem,
                                        device_id=to, device_id_type=MESH)


def allgather_chips(shards, halved, *, name):
    n = len(shards)

    def body(*refs):
        parts = (refs[:n], refs[n:2 * n], refs[2 * n], refs[2 * n + 1], halved)
        _allgather_start(*parts)
        _allgather_finish(*parts)

    return pl.pallas_call(
        body, in_specs=[ANY] * n, out_specs=[ANY] * n,
        out_shape=_allgather_shapes(shards), scratch_shapes=_allgather_sems(n), name=name)(*shards)


def _allgather_shapes(shards):
    return [SDS((4,) + a.shape, a.dtype) for a in shards]


def _allgather_sems(n):
    return [pltpu.SemaphoreType.DMA((n, 6)), pltpu.SemaphoreType.DMA((n, 6))]


def _allgather_rows(ref, is_halved, which):
    r = ref.shape[0]
    return _half(r, which) if is_halved else pl.ds(0, r)


def _allgather_first(ins, outs, send_sems, recv_sems, halved):
    x, y, c = _coords()
    my_chip = 2 * x + y
    cps = []
    for w in range(len(ins)):
        rows = _allgather_rows(ins[w], halved[w], c)
        for k, kind in enumerate(ICI_KINDS):
            cps.append(_remote(ins[w].at[rows], outs[w].at[my_chip, rows], send_sems.at[w, k], recv_sems.at[w, k],
                               _peer(kind, x, y, c)))
    return cps


def _allgather_start(ins, outs, send_sems, recv_sems, halved):
    for cp in _allgather_first(ins, outs, send_sems, recv_sems, halved):
        cp.start()


def _allgather_finish(ins, outs, send_sems, recv_sems, halved):
    x, y, c = _coords()
    me = (x, y, c)
    second = []
    for w in range(len(ins)):
        for k, kind in enumerate(ICI_KINDS):
            landed = outs[w].at[_chip_of(_peer(kind, x, y, c)), _allgather_rows(ins[w], halved[w], c)]
            _remote(landed, landed, send_sems.at[w, k], recv_sems.at[w, k], me).wait_recv()
            if halved[w]:
                cp = _remote(landed, landed, send_sems.at[w, 3 + k], recv_sems.at[w, 3 + k], _peer("c", x, y, c))
                cp.start()
                second.append(cp)
    for w in range(len(ins)):
        if halved[w]:
            for k, kind in enumerate(ICI_KINDS):
                other = outs[w].at[_chip_of(_peer(kind, x, y, c)), _allgather_rows(ins[w], True, 1 - c)]
                _remote(other, other, send_sems.at[w, 3 + k], recv_sems.at[w, 3 + k], me).wait_recv()
    for cp in _allgather_first(ins, outs, send_sems, recv_sems, halved) + second:
        cp.wait_send()


def grads_to_sibling(gs, *, name):
    n = len(gs)

    def body(*refs):
        ins, outs = refs[:n], refs[n:2 * n]
        send_sems, recv_sems = refs[2 * n:]
        x, y, c = _coords()
        cps = []
        for w in range(n):
            cp = _remote(ins[w].at[:, _half(gs[w].shape[1], 1 - c)], outs[w], send_sems.at[w], recv_sems.at[w],
                         _peer("c", x, y, c))
            cp.start()
            cps.append(cp)
        for cp in cps:
            cp.wait()

    return pl.pallas_call(
        body, in_specs=[ANY] * n, out_specs=[ANY] * n,
        out_shape=[SDS((4, a.shape[1] // 2, a.shape[2]), a.dtype) for a in gs],
        scratch_shapes=[pltpu.SemaphoreType.DMA((n,)), pltpu.SemaphoreType.DMA((n,))], name=name)(*gs)


def grads_to_chips(ps, *, name):
    n = len(ps)

    def body(*refs):
        parts = (refs[:n], refs[n:2 * n], refs[2 * n], refs[2 * n + 1])
        _to_chips_start(*parts)
        _to_chips_finish(*parts)

    return pl.pallas_call(
        body, in_specs=[ANY] * n, out_specs=[ANY] * n,
        out_shape=_to_chips_shapes(ps), scratch_shapes=_to_chips_sems(n), name=name)(*ps)


def _to_chips_shapes(ps):
    return [SDS((3,) + a.shape[1:], a.dtype) for a in ps]


def _to_chips_sems(n):
    return [pltpu.SemaphoreType.DMA((n, 3)), pltpu.SemaphoreType.DMA((n, 3))]


def _to_chips_copies(ins, outs, send_sems, recv_sems):
    x, y, c = _coords()
    cps = []
    for w in range(len(ins)):
        for k, kind in enumerate(ICI_KINDS):
            to = _peer(kind, x, y, c)
            cps.append(_remote(ins[w].at[_chip_of(to)], outs[w].at[k], send_sems.at[w, k], recv_sems.at[w, k], to))
    return cps


def _to_chips_start(ins, outs, send_sems, recv_sems):
    for cp in _to_chips_copies(ins, outs, send_sems, recv_sems):
        cp.start()


def _to_chips_finish(ins, outs, send_sems, recv_sems):
    for cp in _to_chips_copies(ins, outs, send_sems, recv_sems):
        cp.wait()


def halves_to_full(hs, *, name):
    n = len(hs)

    def body(*refs):
        ins, outs = refs[:n], refs[n:2 * n]
        send_sems, recv_sems = refs[2 * n:]
        x, y, c = _coords()
        cps = []
        for w in range(n):
            cp = _remote(ins[w], outs[w].at[_half(2 * hs[w].shape[0], c)], send_sems.at[w], recv_sems.at[w],
                         _peer("c", x, y, c))
            cp.start()
            cps.append(cp)
        for cp in cps:
            cp.wait()

    return pl.pallas_call(
        body, in_specs=[ANY] * n, out_specs=[ANY] * n,
        out_shape=[SDS((2 * a.shape[0], a.shape[1]), a.dtype) for a in hs],
        scratch_shapes=[pltpu.SemaphoreType.DMA((n,)), pltpu.SemaphoreType.DMA((n,))],
        name=name)(*hs)


def _row_tile(rows):
    for cand in (256, 192, 176, 128, 64, 32, 16):
        if rows % cand == 0:
            return cand
    return rows


def chip_sum(g, recv, c_arr, *, name):
    _, r, cols = g.shape
    h = r // 2
    tr = _row_tile(h)
    nblk = h // tr

    def body(c_ref, g_ref, r_ref, f_ref, b_ref):
        tot = g_ref[...] + r_ref[...]
        f_ref[...] = tot
        b_ref[...] = tot.astype(BF16)

    blk = pl.BlockSpec((None, tr, cols), lambda j, i, c_ref: (j, i, 0))
    grid_spec = pltpu.PrefetchScalarGridSpec(
        num_scalar_prefetch=1, grid=(4, nblk),
        in_specs=[pl.BlockSpec((None, tr, cols), lambda j, i, c_ref: (j, c_ref[0] * nblk + i, 0)), blk],
        out_specs=[blk, blk])
    return pl.pallas_call(body, grid_spec=grid_spec, out_shape=[SDS((4, h, cols), F32), SDS((4, h, cols), BF16)],
                          name=name, compiler_params=_cp(("parallel", "parallel")))(c_arr, g, recv)


def final_sum(pf, recv, chip_arr, *, name):
    _, h, cols = pf.shape
    tr = _row_tile(h)

    def body(chip_ref, p_ref, r_ref, o_ref):
        o_ref[...] = ((p_ref[...] + r_ref[0].astype(F32)) + r_ref[1].astype(F32)) + r_ref[2].astype(F32)

    grid_spec = pltpu.PrefetchScalarGridSpec(
        num_scalar_prefetch=1, grid=(h // tr,),
        in_specs=[pl.BlockSpec((None, tr, cols), lambda i, chip_ref: (chip_ref[0], i, 0)),
                  pl.BlockSpec((3, tr, cols), lambda i, chip_ref: (0, i, 0))],
        out_specs=pl.BlockSpec((tr, cols), lambda i, chip_ref: (i, 0)))
    return pl.pallas_call(body, grid_spec=grid_spec, out_shape=SDS((h, cols), F32), name=name,
                          compiler_params=_cp(("parallel",)))(chip_arr, pf, recv)


def allreduce_small(v, *, name):
    rws, cols = v.shape

    def body(v_ref, all_ref, sum_ref, send_sems, recv_sems, local_sem):
        x, y, c = _coords()
        me, sibling = (x, y, c), (x, y, 1 - c)
        chips = [(1 - x, y), (x, 1 - y), (1 - x, 1 - y)]

        def rows(px, py, pc):
            return all_ref.at[pl.ds(pl.multiple_of((4 * px + 2 * py + pc) * rws, 8), rws), :]

        def copy(k, block, to, src=None):
            return _remote(rows(*block) if src is None else src, rows(*block), send_sems.at[k], recv_sems.at[k], to)

        mine = pltpu.make_async_copy(v_ref, rows(*me), local_sem)
        mine.start()
        first = [copy(0, me, sibling, src=v_ref)]
        first += [copy(1 + j, me, (*chip, c), src=v_ref) for j, chip in enumerate(chips)]
        for cp in first:
            cp.start()
        passed = [copy(4 + j, (*chip, c), sibling) for j, chip in enumerate(chips)]
        for j, chip in enumerate(chips):
            copy(1 + j, (*chip, c), me).wait_recv()
            passed[j].start()
        copy(0, sibling, me).wait_recv()
        for j, chip in enumerate(chips):
            copy(4 + j, (*chip, 1 - c), me).wait_recv()
        for cp in first + passed:
            cp.wait_send()
        mine.wait()
        tot = all_ref[0:rws, :]
        for dev in range(1, 8):
            tot = tot + all_ref[dev * rws:(dev + 1) * rws, :]
        sum_ref[...] = tot

    vm = pl.BlockSpec(memory_space=pltpu.VMEM)
    return pl.pallas_call(
        body, in_specs=[vm], out_specs=[vm, vm],
        out_shape=[SDS((8 * rws, cols), v.dtype), SDS((rws, cols), v.dtype)],
        scratch_shapes=[pltpu.SemaphoreType.DMA((7,)), pltpu.SemaphoreType.DMA((7,)), pltpu.SemaphoreType.DMA],
        name=name)(v)[1]


def _pack_rows(parts, rows):
    out = []
    for a, r in zip(parts, rows):
        flat = a.reshape(-1)
        flat = jnp.pad(flat, (0, r * LANES - flat.shape[0]))
        out.append(flat.reshape(r, LANES))
    return jnp.concatenate(out, axis=0)


def _unpack_rows(packed, shapes, rows):
    out, at = [], 0
    for shp, r in zip(shapes, rows):
        size = int(np.prod(shp))
        out.append(packed[at:at + r].reshape(-1)[:size].reshape(shp))
        at += r
    return out


def kernel(x, g_pre_mix, w_in, b_forget, w_o_fox, w_o_dil, w_out, g_post_mix, g_pre_ffn, w_up, conv_w, conv_b, w_down, g_post_ffn, loss_target, m_g_pre_mix, m_w_in, m_b_forget, m_w_o_fox, m_w_o_dil, m_w_out, m_g_post_mix, m_g_pre_ffn, m_w_up, m_conv_w, m_conv_b, m_w_down, m_g_post_ffn, v_g_pre_mix, v_w_in, v_b_forget, v_w_o_fox, v_w_o_dil, v_w_out, v_g_post_mix, v_g_pre_ffn, v_w_up, v_conv_w, v_conv_b, v_w_down, v_g_post_ffn):
    xi, yi, ci = _coords()
    chip = 2 * xi + yi
    c_arr = jnp.reshape(ci, (1,)).astype(jnp.int32)
    chip_arr = jnp.reshape(chip, (1,)).astype(jnp.int32)
    xs = x[0]
    target = loss_target[0]
    s, d = xs.shape
    f_half = w_down.shape[1] * 4
    cols_in = w_in.shape[2]

    big = (w_in, w_o_fox, w_o_dil, w_out, w_up, w_down)
    shards = [w[0].astype(BF16) for w in big]
    a_in, a_cw = allgather_chips([shards[0], conv_w[0]], [True, False], name="allgather_w_in")
    w_in_full = jnp.concatenate([jnp.where(chip == j, shards[0], a_in[j]) for j in range(4)], axis=1)
    cw = jnp.concatenate([jnp.where(chip == j, conv_w[0], a_cw[j]) for j in range(4)], axis=1)
    nf = N_HEADS
    e_a, e_b = 3 * ATT_W, 3 * ATT_W + nf
    wz = jnp.concatenate([w_in_full[:, :e_a], w_in_full[:, e_b:]], axis=1)
    wf = jnp.pad(w_in_full[:, e_a:e_b], ((0, 0), (0, LANES - nf)))
    cb = conv_b
    bfo = jnp.pad(b_forget, ((0, 0), (0, LANES - nf)))

    h1 = rmsnorm_fwd(xs, g_pre_mix)
    z = mm([(h1, d, 0)], [(wz, d, 0)], nt=False, out_dtype=BF16, tm=1024, tn=512, name="in_proj")
    fa = mm([(h1, d, 0)], [(wf, d, 0)], nt=False, out_dtype=F32, tm=1024, tn=LANES, name="in_proj_forget")
    q_aug, k_aug = fox_prep(z, fa, bfo)
    ya, lse_a, *late = fox_fwd(q_aug, k_aug, z, gather=shards[1:])
    a_of, a_od, a_out, a_up, a_down = [
        lax.dynamic_update_index_in_dim(a4, own, chip, 0) for a4, own in zip(late, shards[1:])]
    wo_a = jnp.concatenate([a_of[j] for j in range(4)], axis=1)
    wo_b = jnp.concatenate([a_od[j] for j in range(4)], axis=1)
    w_o = a_out.reshape(d, d)
    w_dn = a_down.reshape(f_half, d)
    wu_a = jnp.concatenate([a_up[0], a_up[1]], axis=1)
    wu_b = jnp.concatenate([a_up[2], a_up[3]], axis=1)
    qkv_b = rope_apply([(z, Z_QB, QK_SCALE, True), (z, Z_KB, 1.0, True), (z, Z_VB, 1.0, False)], rope_tables(s, 1.0),
                       out_dtype=F32, name="rope_fwd")
    yb, lse_b = dil_fwd_all(qkv_b)
    pa, pb, mixed = gate_mix(ya, yb, wo_a, wo_b, z)
    y1, x1 = mm_rms_res(mixed, w_o, g_post_mix, xs, tm=512, name="out_proj")
    h2 = rmsnorm_fwd(x1, g_pre_ffn)
    ua, ub, mid = ffn_up(h2, wu_a, wu_b, cw, cb)
    y2, dout, sq = mm_rms_res(mid, w_dn, g_post_ffn, x1, target, tm=512, name="down_proj")
    loss = lax.psum(0.5 * sq[0, 0] / d, ("x", "y", "c"))

    dy2, gg_post_ffn = rmsnorm_bwd(dout, y2, g_post_ffn, None, out_dtype=BF16, name="norm_bwd_post_ffn")
    dmid = mm([(dy2, d, 0)], [(w_dn, d, 0)], nt=True, out_dtype=BF16, tm=512, tn=f_half // 2, name="down_dgrad")
    dw_down = wgrad((mid, f_half, 0), dy2, tk=f_half // 2, tn=1024, ts=1024, name="down_wgrad")
    dua, dub, gc_a, gc_b = ffn_bwd(dmid, ua, ub, cw, cb)
    dh2 = mm([(dua, f_half, 0), (dub, f_half, 0)], [(wu_a, f_half, 0), (wu_b, f_half, 0)], nt=True, out_dtype=BF16,
             tm=512, tn=512, name="up_dgrad")
    dw_up = jnp.concatenate(
        [wgrad((h2, d, 0), du, tk=1024, tn=f_half // 2, ts=1024, name=f"up_wgrad_{k}", chip_major=True)
         for k, du in enumerate((dua, dub))], axis=0)
    def to_chip_sums(gs, nms, tag):
        from_sib = grads_to_sibling(gs, name=f"grads_to_sibling_{tag}")
        return [chip_sum(g, r, c_arr, name=f"chip_sum_{nm}") for g, r, nm in zip(gs, from_sib, nms)]

    sums_ffn = to_chip_sums([dw_up, dw_down.reshape(4, f_half // 4, d)], ("w_up", "w_down"), "ffn")
    dx1, gg_pre_ffn = rmsnorm_bwd(dh2, x1, g_pre_ffn, dout, out_dtype=F32, name="norm_bwd_pre_ffn")
    dy1, gg_post_mix = rmsnorm_bwd(dx1, y1, g_post_mix, None, out_dtype=BF16, name="norm_bwd_post_mix")
    dmixed = mm([(dy1, d, 0)], [(w_o, d, 0)], nt=True, out_dtype=BF16, tm=512, tn=512, name="out_dgrad")
    dw_out = wgrad((mixed, d, 0), dy1, tk=1024, tn=1024, ts=1024, name="out_wgrad")
    dpa, dpb, dz_g = gate_bwd(dmixed, z, pa, pb)
    dya = mm([(dpa, d, 0)], [(wo_a, d, 0)], nt=True, out_dtype=BF16, tm=512, tn=ATT_W, name="fox_o_dgrad")
    dyb = mm([(dpb, d, 0)], [(wo_b, d, 0)], nt=True, out_dtype=F32, tm=512, tn=ATT_W, name="dil_o_dgrad")
    by_chip_cols = lambda a: jnp.stack([a[:, j * (d // 4):(j + 1) * (d // 4)] for j in range(4)], axis=0)
    dw_of = by_chip_cols(wgrad((ya, ATT_W, 0), dpa, tk=ATT_W, tn=d, ts=1024, name="fox_o_wgrad"))
    dw_od = by_chip_cols(wgrad((yb, ATT_W, 0), dpb, tk=ATT_W, tn=d, ts=1024, name="dil_o_wgrad"))
    sums_mix = to_chip_sums([dw_of, dw_od, dw_out.reshape(4, d // 4, d)], ("w_o_fox", "w_o_dil", "w_out"), "mix")
    dd_a = head_rowsum(dya, ya, name="fox_delta")
    dq_aug, dk_aug, dv_a, *got_ffn = fox_bwd(q_aug, k_aug, z, dya, lse_a, dd_a, exchange=[p[1] for p in sums_ffn])
    dz_a, dfa, gg_bf = fox_post(dq_aug, dk_aug, dv_a, fa, bfo)
    dq_b, dk_b, dv_b, *got_mix = dil_bwd_all(qkv_b, dyb, lse_b, yb, exchange=[p[1] for p in sums_mix])
    dz_b = rope_apply([(dq_b, 0, QK_SCALE, True), (dk_b, 0, 1.0, True), (dv_b, 0, 1.0, False)],
                      rope_tables(s, -1.0), out_dtype=BF16, name="rope_bwd")
    dh1 = mm([(dz_a, e_a, 0), (dz_b, e_a, 0), (dz_g, d, 0), (dz_g, d, 1), (dfa, LANES, 0)],
             [(wz, e_a, 0), (wz, e_a, 1), (wz, d, 3), (wz, d, 4), (wf, LANES, 0)], nt=True, out_dtype=BF16,
             tm=512, tn=512, name="in_dgrad")
    dw_a = wgrad((h1, d, 0), dz_a, tk=1024, tn=e_a // 2, ts=1024, name="in_wgrad_a")
    dw_b = wgrad((h1, d, 0), dz_b, tk=1024, tn=e_a // 2, ts=1024, name="in_wgrad_b")
    dw_g = wgrad((h1, d, 0), dz_g, tk=1024, tn=1024, ts=1024, name="in_wgrad_g")
    dw_f = wgrad((h1, d, 0), dfa, tk=1024, tn=LANES, ts=1024, name="in_wgrad_f")
    grad_x, gg_pre_mix = rmsnorm_bwd(dh1, xs, g_pre_mix, dx1, out_dtype=F32, name="norm_bwd_pre_mix")
    dw_in_full = jnp.concatenate([dw_a, dw_f[:, :nf], dw_b, dw_g], axis=1)
    dw_in = jnp.stack([dw_in_full[:, j * cols_in:(j + 1) * cols_in] for j in range(4)], axis=0)

    names = ("w_in", "w_o_fox", "w_o_dil", "w_out", "w_up", "w_down")
    sums_in = to_chip_sums([dw_in], ("w_in",), "in")
    got_in = grads_to_chips([sums_in[0][1]], name="grads_to_chips_in")
    sums = sums_in + sums_mix + sums_ffn
    from_chips = list(got_in) + list(got_mix) + list(got_ffn)
    halves = [final_sum(p[0], r, chip_arr, name=f"final_sum_{nm}") for p, r, nm in zip(sums, from_chips, names)]
    from_half = halves_to_full(halves, name="halves_to_full")
    g_big = [lax.dynamic_update_slice_in_dim(full, mine, ci * mine.shape[0], axis=0) for full, mine in zip(from_half, halves)]
    upd_big = [adamw(w[0], g, m[0], v[0], name=f"adamw_{nm}") for w, g, m, v, nm in list(zip(
        big, g_big, (m_w_in, m_w_o_fox, m_w_o_dil, m_w_out, m_w_up, m_w_down),
        (v_w_in, v_w_o_fox, v_w_o_dil, v_w_out, v_w_up, v_w_down), names))[1:]]
    to_t = lambda a: jnp.transpose(a, (2, 0, 1))
    from_t = lambda a: jnp.transpose(a, (1, 2, 0))
    g_in_t = to_t(g_big[0][None])
    upd_in = adamw(to_t(w_in), g_in_t, to_t(m_w_in), to_t(v_w_in), name="adamw_w_in", tr=cols_in // 2)

    g_cw_loc = jnp.concatenate([gc_a[0:3], gc_b[0:3]], axis=1)
    g_cb_loc = jnp.concatenate([gc_a[3:4], gc_b[3:4]], axis=1)
    small_loc = [gg_pre_mix, gg_post_mix, gg_pre_ffn, gg_post_ffn, g_cb_loc, gg_bf[:, :nf], g_cw_loc]
    red_rows = (8, 8, 8, 8, 48, 8, 136)
    red = allreduce_small(_pack_rows(small_loc, red_rows), name="allreduce_small")
    g_pm, g_qm, g_pf, g_qf, g_cb, g_bf, g_cw_full = _unpack_rows(red, [a.shape for a in small_loc], red_rows)
    cols_cw = conv_w.shape[2]
    g_cw = lax.dynamic_slice_in_dim(g_cw_full, chip * cols_cw, cols_cw, axis=1)
    small_w = (g_pre_mix, g_post_mix, g_pre_ffn, g_post_ffn, conv_b, b_forget, conv_w[0])
    small_m = (m_g_pre_mix, m_g_post_mix, m_g_pre_ffn, m_g_post_ffn, m_conv_b, m_b_forget, m_conv_w[0])
    small_v = (v_g_pre_mix, v_g_post_mix, v_g_pre_ffn, v_g_post_ffn, v_conv_b, v_b_forget, v_conv_w[0])
    small_g = (g_pm, g_qm, g_pf, g_qf, g_cb, g_bf, g_cw)
    ad_rows = (8, 8, 8, 8, 48, 8, 40)
    packed = [_pack_rows(t, ad_rows) for t in (small_w, small_g, small_m, small_v)]
    upd_small = [_unpack_rows(o, [a.shape for a in small_w], ad_rows) for o in adamw(*packed, name="adamw_small")]

    order = ("g_pre_mix", "w_in", "b_forget", "w_o_fox", "w_o_dil", "w_out", "g_post_mix", "g_pre_ffn", "w_up", "conv_w",
             "conv_b", "w_down", "g_post_ffn")
    small_names = ("g_pre_mix", "g_post_mix", "g_pre_ffn", "g_post_ffn", "conv_b", "b_forget", "conv_w")
    grads, deltas, new_ms, new_vs = {}, {}, {}, {}
    grads["w_in"] = from_t(g_in_t)
    deltas["w_in"], new_ms["w_in"], new_vs["w_in"] = (from_t(a) for a in upd_in)
    for k, nm in enumerate(names[1:]):
        grads[nm] = g_big[k + 1][None]
        deltas[nm], new_ms[nm], new_vs[nm] = (a[None] for a in upd_big[k])
    for k, nm in enumerate(small_names):
        lead = (lambda a: a[None]) if nm == "conv_w" else (lambda a: a)
        grads[nm] = lead(small_g[k])
        deltas[nm], new_ms[nm], new_vs[nm] = (lead(upd_small[j][k]) for j in range(3))
    return (loss, grad_x[None], *[grads[nm] for nm in order], *[deltas[nm] for nm in order],
            *[new_ms[nm] for nm in order], *[new_vs[nm] for nm in order])
```

```python
import functools
import math

import numpy as np
import jax
import jax.numpy as jnp
from jax import lax
from jax.experimental import pallas as pl
from jax.experimental.pallas import tpu as pltpu

F32 = jnp.float32
BF16 = jnp.bfloat16
SDS = jax.ShapeDtypeStruct
MESH = pl.DeviceIdType.MESH

HEAD_DIM = 64
N_HEADS = 8
LANES = 128
ATT_W = N_HEADS * HEAD_DIM
DIL_PATTERNS = ((128, 1), (512, 4), (2048, 16))
DIL_BLK = 128
ROPE_DIM = HEAD_DIM // 4
ROPE_THETA = 500000.0
RMS_EPS = 1e-6
NEG = -1e30
QK_SCALE = 1.0 / math.sqrt(HEAD_DIM)
ADAM_LR, ADAM_B1, ADAM_B2, ADAM_EPS, ADAM_WD, ADAM_STEP = 0.001, 0.9, 0.999, 1e-08, 0.01, 10
VMEM_LIMIT = 56 * 1024 * 1024

Z_QA, Z_KA, Z_VA, Z_QB, Z_KB, Z_VB = 0, 1, 2, 3, 4, 5
Z_W = 5120


def _cp(sem):
    return pltpu.CompilerParams(dimension_semantics=sem, vmem_limit_bytes=VMEM_LIMIT)


def _nt(a, b):
    return lax.dot_general(a, b, (((1,), (1,)), ((), ())), preferred_element_type=F32)


def _tn(a, b):
    return lax.dot_general(a, b, (((0,), (0,)), ((), ())), preferred_element_type=F32)


def _nn(a, b):
    return jnp.dot(a, b, preferred_element_type=F32)


def _lane(shape):
    return lax.broadcasted_iota(jnp.int32, shape, 1)


def _row(shape):
    return lax.broadcasted_iota(jnp.int32, shape, 0)


def rmsnorm_fwd(x, g, *, tm=512):
    s, d = x.shape

    def body(x_ref, g_ref, h_ref):
        xv = x_ref[...]
        inv = lax.rsqrt(jnp.mean(xv * xv, axis=-1, keepdims=True) + RMS_EPS)
        h_ref[...] = (xv * inv * g_ref[...]).astype(h_ref.dtype)

    return pl.pallas_call(
        body, grid=(s // tm,),
        in_specs=[pl.BlockSpec((tm, d), lambda i: (i, 0)), pl.BlockSpec((1, d), lambda i: (0, 0))],
        out_specs=pl.BlockSpec((tm, d), lambda i: (i, 0)),
        out_shape=SDS((s, d), BF16), name="rmsnorm_fwd", compiler_params=_cp(("parallel",)))(x, g)


def rmsnorm_bwd(dh, x, g, res, *, out_dtype, tm=256, name):
    s, d = x.shape
    n = s // tm
    has_res = res is not None

    def body(*refs):
        if has_res:
            dh_ref, x_ref, g_ref, res_ref, dx_ref, dg_ref, acc = refs
        else:
            dh_ref, x_ref, g_ref, dx_ref, dg_ref, acc = refs
        i = pl.program_id(0)

        @pl.when(i == 0)
        def _():
            acc[...] = jnp.zeros_like(acc)

        xv = x_ref[...]
        inv = lax.rsqrt(jnp.mean(xv * xv, axis=-1, keepdims=True) + RMS_EPS)
        xh = xv * inv
        dhv = dh_ref[...].astype(F32)
        dxh = dhv * g_ref[...]
        dot = jnp.mean(dxh * xh, axis=-1, keepdims=True)
        dx = inv * (dxh - xh * dot)
        if has_res:
            dx = dx + res_ref[...]
        dx_ref[...] = dx.astype(dx_ref.dtype)
        acc[...] += jnp.sum((dhv * xh).reshape(tm // 8, 8, d), axis=0)

        @pl.when(i == n - 1)
        def _():
            dg_ref[...] = jnp.sum(acc[...], axis=0, keepdims=True)

    row = pl.BlockSpec((tm, d), lambda i: (i, 0))
    in_specs = [row, row, pl.BlockSpec((1, d), lambda i: (0, 0))] + ([row] if has_res else [])
    args = [dh, x, g] + ([res] if has_res else [])
    return pl.pallas_call(
        body, grid=(n,), in_specs=in_specs,
        out_specs=[row, pl.BlockSpec((1, d), lambda i: (0, 0))],
        out_shape=[SDS((s, d), out_dtype), SDS((1, d), F32)],
        scratch_shapes=[pltpu.VMEM((8, d), F32)],
        name=name, compiler_params=_cp(("arbitrary",)))(*args)


def mm(a_views, b_views, *, nt, out_dtype, tm, tn, name):
    n_p = len(a_views)
    m = a_views[0][0].shape[0]
    n = b_views[0][0].shape[0] if nt else b_views[0][0].shape[1]

    def body(*refs):
        o_ref = refs[-1]
        acc = None
        for p in range(n_p):
            av = refs[p][...].astype(BF16)
            bv = refs[n_p + p][...].astype(BF16)
            dv = _nt(av, bv) if nt else _nn(av, bv)
            acc = dv if acc is None else acc + dv
        o_ref[...] = acc.astype(o_ref.dtype)

    in_specs = []
    for arr, w, blk in a_views:
        in_specs.append(pl.BlockSpec((tm, w), functools.partial(lambda i, j, blk: (i, blk), blk=blk)))
    for arr, w, blk in b_views:
        if nt:
            in_specs.append(pl.BlockSpec((tn, w), functools.partial(lambda i, j, blk: (j, blk), blk=blk)))
        else:
            in_specs.append(pl.BlockSpec((w, tn), lambda i, j: (0, j)))
    return pl.pallas_call(
        body, grid=(m // tm, n // tn), in_specs=in_specs,
        out_specs=pl.BlockSpec((tm, tn), lambda i, j: (i, j)),
        out_shape=SDS((m, n), out_dtype), name=name,
        compiler_params=_cp(("parallel", "parallel")))(*[a[0] for a in a_views], *[b[0] for b in b_views])


def wgrad(a_view, g, *, tk, tn, ts, name, chip_major=False):
    arr, ka, blk = a_view
    s, n = g.shape
    ns = s // ts

    def body(a_ref, g_ref, o_ref):
        @pl.when(pl.program_id(2) == 0)
        def _():
            o_ref[...] = jnp.zeros_like(o_ref)

        o_ref[...] += _tn(a_ref[...].astype(BF16), g_ref[...].astype(BF16))

    if chip_major:
        out_spec = pl.BlockSpec((None, tk, tn), lambda i, j, k: (j, i, 0))
        out_shape = SDS((n // tn, ka, tn), F32)
    else:
        out_spec = pl.BlockSpec((tk, tn), lambda i, j, k: (i, j))
        out_shape = SDS((ka, n), F32)
    return pl.pallas_call(
        body, grid=(ka // tk, n // tn, ns),
        in_specs=[pl.BlockSpec((ts, tk), lambda i, j, k: (k, blk * (ka // tk) + i)),
                  pl.BlockSpec((ts, tn), lambda i, j, k: (k, j))],
        out_specs=out_spec, out_shape=out_shape, name=name,
        compiler_params=_cp(("parallel", "parallel", "arbitrary")))(arr, g)


def mm_rms_res(a, w, g, xres, target=None, *, tm=256, name):
    s, k = a.shape
    d = w.shape[1]
    n = s // tm
    with_loss = target is not None

    def body(*refs):
        if with_loss:
            a_ref, w_ref, g_ref, x_ref, t_ref, y_ref, o_ref, l_ref = refs
        else:
            a_ref, w_ref, g_ref, x_ref, y_ref, o_ref = refs
        y = _nn(a_ref[...], w_ref[...])
        inv = lax.rsqrt(jnp.mean(y * y, axis=-1, keepdims=True) + RMS_EPS)
        xn = x_ref[...] + y * inv * g_ref[...]
        y_ref[...] = y
        if with_loss:
            err = xn - t_ref[...]
            o_ref[...] = err * (1.0 / d)

            @pl.when(pl.program_id(0) == 0)
            def _():
                l_ref[...] = jnp.zeros_like(l_ref)

            l_ref[...] += jnp.sum(jnp.sum(err * err, axis=1, keepdims=True), axis=0, keepdims=True)
        else:
            o_ref[...] = xn

    row = pl.BlockSpec((tm, d), lambda i: (i, 0))
    in_specs = [pl.BlockSpec((tm, k), lambda i: (i, 0)), pl.BlockSpec((k, d), lambda i: (0, 0)),
                pl.BlockSpec((1, d), lambda i: (0, 0)), row]
    out_specs = [row, row]
    out_shape = [SDS((s, d), F32), SDS((s, d), F32)]
    args = [a, w, g, xres]
    if with_loss:
        in_specs.append(row)
        out_specs.append(pl.BlockSpec((1, 1), lambda i: (0, 0)))
        out_shape.append(SDS((1, 1), F32))
        args.append(target)
    return pl.pallas_call(
        body, grid=(n,), in_specs=in_specs, out_specs=out_specs, out_shape=out_shape, name=name,
        compiler_params=_cp(("arbitrary",)))(*args)


def _split3(v):
    hi = v.astype(BF16).astype(F32)
    r = v - hi
    mid = r.astype(BF16).astype(F32)
    lo = (r - mid).astype(BF16).astype(F32)
    return hi, mid, lo


def _tri(n, upper):
    r = np.arange(n)
    m = (r[:, None] <= r[None, :]) if upper else (r[:, None] >= r[None, :])
    return jnp.asarray(m.astype(np.float32))


def fox_prep(z, fa, bfo, *, tb=512):
    s = z.shape[0]
    n = s // tb

    def body(q_ref, k_ref, fa_ref, b_ref, tri_ref, qa_ref, ka_ref, carry):
        @pl.when(pl.program_id(0) == 0)
        def _():
            carry[...] = jnp.zeros_like(carry)

        xv = fa_ref[...] + b_ref[...]
        logf = jnp.minimum(xv, 0.0) - jnp.log(1.0 + jnp.exp(-jnp.abs(xv)))
        csum = jnp.dot(tri_ref[...], logf, preferred_element_type=F32, precision=lax.Precision.HIGHEST) + carry[0:1, :]
        carry[0:1, :] = csum[tb - 1:tb, :]
        lane = _lane((tb, LANES))
        for h in range(N_HEADS):
            hi, mid, lo = _split3(csum[:, h:h + 1])
            pair = (h // 2) * LANES
            qv = q_ref[:, pair:pair + LANES].astype(F32)
            kv = k_ref[:, pair:pair + LANES].astype(F32)
            if h % 2:
                qv = pltpu.roll(qv, 64, axis=1)
                kv = pltpu.roll(kv, 64, axis=1)
            one = jnp.where((lane >= 67) & (lane < 70), 1.0, 0.0)
            q_x = jnp.where(lane == 64, hi, jnp.where(lane == 65, mid, jnp.where(lane == 66, lo, one)))
            one = jnp.where((lane >= 64) & (lane < 67), 1.0, 0.0)
            k_x = jnp.where(lane == 67, -hi, jnp.where(lane == 68, -mid, jnp.where(lane == 69, -lo, one)))
            qa_ref[:, h * LANES:(h + 1) * LANES] = jnp.where(lane < 64, qv * QK_SCALE, q_x).astype(BF16)
            ka_ref[:, h * LANES:(h + 1) * LANES] = jnp.where(lane < 64, kv, k_x).astype(BF16)

    return pl.pallas_call(
        body, grid=(n,),
        in_specs=[pl.BlockSpec((tb, ATT_W), lambda i: (i, Z_QA)), pl.BlockSpec((tb, ATT_W), lambda i: (i, Z_KA)),
                  pl.BlockSpec((tb, LANES), lambda i: (i, 0)), pl.BlockSpec((1, LANES), lambda i: (0, 0)),
                  pl.BlockSpec((tb, tb), lambda i: (0, 0))],
        out_specs=[pl.BlockSpec((tb, N_HEADS * LANES), lambda i: (i, 0))] * 2,
        out_shape=[SDS((s, N_HEADS * LANES), BF16)] * 2,
        scratch_shapes=[pltpu.VMEM((8, LANES), F32)],
        name="fox_prep", compiler_params=_cp(("arbitrary",)))(z, z, fa, bfo, _tri(tb, False))


def _causal_pairs(n, k_major):
    if k_major:
        pairs = [(qi, kj) for kj in range(n) for qi in range(kj, n)]
    else:
        pairs = [(qi, kj) for qi in range(n) for kj in range(qi + 1)]
    return (jnp.asarray([p[0] for p in pairs], jnp.int32), jnp.asarray([p[1] for p in pairs], jnp.int32), len(pairs))


def fox_fwd(q_aug, k_aug, z, gather=(), *, t=512):
    s = z.shape[0]
    qi_arr, kj_arr, n_pairs = _causal_pairs(s // t, False)
    ng = len(gather)

    def body(qi_ref, kj_ref, q_ref, k_ref, v_ref, *rest):
        g_ins, (o_ref, lse_ref), g_outs = rest[:ng], rest[ng:ng + 2], rest[ng + 2:2 * ng + 2]
        m_scr, l_scr, acc_scr = rest[2 * ng + 2:2 * ng + 5]
        comm = (g_ins, g_outs) + tuple(rest[2 * ng + 5:]) + ([True] * ng,)
        step = pl.program_id(1)
        qi = qi_ref[step]
        kj = kj_ref[step]
        if ng:
            @pl.when((pl.program_id(0) == 0) & (step == 0))
            def _():
                _allgather_start(*comm)

        @pl.when(kj == 0)
        def _():
            m_scr[...] = jnp.full_like(m_scr, NEG)
            l_scr[...] = jnp.zeros_like(l_scr)
            acc_scr[...] = jnp.zeros_like(acc_scr)

        def update(masked):
            for i in range(2):
                sc = _nt(q_ref[:, i * LANES:(i + 1) * LANES], k_ref[:, i * LANES:(i + 1) * LANES])
                if masked:
                    sc = jnp.where(_row((t, t)) >= _lane((t, t)), sc, NEG)
                m_prev = m_scr[i]
                m_new = jnp.maximum(m_prev, jnp.max(sc, axis=-1, keepdims=True))
                alpha = jnp.exp(m_prev - m_new)
                p = jnp.exp(sc - jnp.tile(m_new, (1, t // LANES)))
                l_scr[i] = alpha * l_scr[i] + jnp.sum(p, axis=-1, keepdims=True)
                acc_scr[i] = alpha * acc_scr[i] + _nn(p.astype(BF16), v_ref[...])
                m_scr[i] = m_new

        @pl.when(kj < qi)
        def _():
            update(False)

        @pl.when(kj == qi)
        def _():
            update(True)
            lane = _lane((t, LANES))
            o_ref[...] = jnp.where(lane < 64, acc_scr[0] / l_scr[0], acc_scr[1] / l_scr[1]).astype(o_ref.dtype)
            lse_ref[...] = jnp.where(lane < 64, m_scr[0] + jnp.log(l_scr[0]), m_scr[1] + jnp.log(l_scr[1]))

        if ng:
            @pl.when((pl.program_id(0) == 3) & (step == n_pairs - 1))
            def _():
                _allgather_finish(*comm)

    grid_spec = pltpu.PrefetchScalarGridSpec(
        num_scalar_prefetch=2, grid=(4, n_pairs),
        in_specs=[pl.BlockSpec((t, 2 * LANES), lambda hp, st, qi, kj: (qi[st], hp)),
                  pl.BlockSpec((t, 2 * LANES), lambda hp, st, qi, kj: (kj[st], hp)),
                  pl.BlockSpec((t, LANES), lambda hp, st, qi, kj: (kj[st], 4 * Z_VA + hp))] + [ANY] * ng,
        out_specs=[pl.BlockSpec((t, LANES), lambda hp, st, qi, kj: (qi[st], hp))] * 2 + [ANY] * ng,
        scratch_shapes=[pltpu.VMEM((2, t, LANES), F32)] * 3 + (_allgather_sems(ng) if ng else []))
    return pl.pallas_call(
        body, grid_spec=grid_spec, out_shape=[SDS((s, ATT_W), BF16), SDS((s, ATT_W), F32)] + _allgather_shapes(gather),
        name="fox_fwd", compiler_params=_cp(("arbitrary", "arbitrary")))(qi_arr, kj_arr, q_aug, k_aug, z, *gather)


def fox_bwd(q_aug, k_aug, z, dy, lse, dd, exchange=(), *, t=512):
    s = z.shape[0]
    qi_arr, kj_arr, n_pairs = _causal_pairs(s // t, True)
    ne = len(exchange)

    def body(qi_ref, kj_ref, q_ref, k_ref, v_ref, do_ref, lse_ref, dd_ref, *rest):
        e_ins, (dq_ref, dk_ref, dv_ref), e_outs = rest[:ne], rest[ne:ne + 3], rest[ne + 3:2 * ne + 3]
        comm = (e_ins, e_outs) + tuple(rest[2 * ne + 3:])
        step = pl.program_id(1)
        qi = qi_ref[step]
        kj = kj_ref[step]
        if ne:
            @pl.when((pl.program_id(0) == 0) & (step == 0))
            def _():
                _to_chips_start(*comm)

        @pl.when(step == 0)
        def _():
            dq_ref[...] = jnp.zeros_like(dq_ref)

        @pl.when(qi == kj)
        def _():
            dk_ref[...] = jnp.zeros_like(dk_ref)
            dv_ref[...] = jnp.zeros_like(dv_ref)

        def update(masked):
            lane = _lane((t, LANES))
            rows = pl.ds(pl.multiple_of(qi * t, t), t)
            dov = do_ref[...]
            dv_new = None
            for i in range(2):
                head = (lane < 64) if i == 0 else (lane >= 64)
                qv = q_ref[:, i * LANES:(i + 1) * LANES]
                kv = k_ref[:, i * LANES:(i + 1) * LANES]
                sc = _nt(qv, kv)
                if masked:
                    sc = jnp.where(_row((t, t)) >= _lane((t, t)), sc, NEG)
                p = jnp.exp(sc - lse_ref[:, i * 64:i * 64 + 1])
                dp = _nt(jnp.where(head, dov, jnp.zeros_like(dov)), v_ref[...])
                ds = (p * (dp - dd_ref[:, i * 64:i * 64 + 1])).astype(BF16)
                dq_ref[rows, i * LANES:(i + 1) * LANES] += _nn(ds, kv)
                dk_ref[:, i * LANES:(i + 1) * LANES] += _tn(ds, qv)
                dvi = _tn(p.astype(BF16), dov)
                dv_new = dvi if dv_new is None else jnp.where(head, dvi, dv_new)
            dv_ref[...] += dv_new

        @pl.when(kj < qi)
        def _():
            update(False)

        @pl.when(kj == qi)
        def _():
            update(True)

        if ne:
            @pl.when((pl.program_id(0) == 3) & (step == n_pairs - 1))
            def _():
                _to_chips_finish(*comm)

    grid_spec = pltpu.PrefetchScalarGridSpec(
        num_scalar_prefetch=2, grid=(4, n_pairs),
        in_specs=[pl.BlockSpec((t, 2 * LANES), lambda hp, st, qi, kj: (qi[st], hp)),
                  pl.BlockSpec((t, 2 * LANES), lambda hp, st, qi, kj: (kj[st], hp)),
                  pl.BlockSpec((t, LANES), lambda hp, st, qi, kj: (kj[st], 4 * Z_VA + hp)),
                  pl.BlockSpec((t, LANES), lambda hp, st, qi, kj: (qi[st], hp)),
                  pl.BlockSpec((t, LANES), lambda hp, st, qi, kj: (qi[st], hp)),
                  pl.BlockSpec((t, LANES), lambda hp, st, qi, kj: (qi[st], hp))] + [ANY] * ne,
        out_specs=[pl.BlockSpec((s, 2 * LANES), lambda hp, st, qi, kj: (0, hp)),
                   pl.BlockSpec((t, 2 * LANES), lambda hp, st, qi, kj: (kj[st], hp)),
                   pl.BlockSpec((t, LANES), lambda hp, st, qi, kj: (kj[st], hp))] + [ANY] * ne,
        scratch_shapes=_to_chips_sems(ne) if ne else [])
    return pl.pallas_call(
        body, grid_spec=grid_spec,
        out_shape=[SDS((s, N_HEADS * LANES), F32), SDS((s, N_HEADS * LANES), F32), SDS((s, ATT_W), F32)]
        + _to_chips_shapes(exchange),
        name="fox_bwd", compiler_params=_cp(("arbitrary", "arbitrary")))(qi_arr, kj_arr, q_aug, k_aug, z, dy, lse, dd, *exchange)


def head_rowsum(a, b, *, tm=512, name):
    s = a.shape[0]

    def body(a_ref, b_ref, o_ref):
        prod = a_ref[...].astype(F32) * b_ref[...].astype(F32)
        lane = _lane((tm, LANES))
        lo = jnp.sum(jnp.where(lane < 64, prod, 0.0), axis=-1, keepdims=True)
        hi = jnp.sum(jnp.where(lane >= 64, prod, 0.0), axis=-1, keepdims=True)
        o_ref[...] = jnp.where(lane < 64, lo, hi)

    blk = pl.BlockSpec((tm, LANES), lambda i, j: (i, j))
    return pl.pallas_call(body, grid=(s // tm, 4), in_specs=[blk, blk], out_specs=blk, out_shape=SDS((s, ATT_W), F32),
                          name=name, compiler_params=_cp(("parallel", "parallel")))(a, b)


def fox_post(dq_aug, dk_aug, dv, fa, bfo, *, tb=512):
    s = dv.shape[0]
    n = s // tb

    def body(dq_ref, dk_ref, dv_ref, fa_ref, b_ref, tri_ref, dz_ref, dfa_ref, gb_ref, carry, acc):
        i = pl.program_id(0)

        @pl.when(i == 0)
        def _():
            carry[...] = jnp.zeros_like(carry)
            acc[...] = jnp.zeros_like(acc)

        lane = _lane((tb, LANES))
        d_f = jnp.zeros((tb, LANES), F32)
        for h in range(N_HEADS):
            col = dq_ref[:, h * LANES + 64:h * LANES + 65] - dk_ref[:, h * LANES + 67:h * LANES + 68]
            d_f = jnp.where(lane == h, col, d_f)
        suffix = jnp.dot(tri_ref[...], d_f, preferred_element_type=F32, precision=lax.Precision.HIGHEST) + carry[0:1, :]
        carry[0:1, :] = suffix[0:1, :]
        xv = fa_ref[...] + b_ref[...]
        dx = suffix * (1.0 / (1.0 + jnp.exp(xv)))
        dfa_ref[...] = dx.astype(dfa_ref.dtype)
        acc[...] += jnp.sum(dx.reshape(tb // 8, 8, LANES), axis=0)
        for hp in range(4):
            for src, off, scale in ((dq_ref, 0, QK_SCALE), (dk_ref, ATT_W, 1.0)):
                even = src[:, (2 * hp) * LANES:(2 * hp + 1) * LANES]
                odd = pltpu.roll(src[:, (2 * hp + 1) * LANES:(2 * hp + 2) * LANES], 64, axis=1)
                dz_ref[:, off + hp * LANES:off + (hp + 1) * LANES] = (jnp.where(lane < 64, even, odd) * scale).astype(BF16)
        dz_ref[:, 2 * ATT_W:3 * ATT_W] = dv_ref[...].astype(BF16)

        @pl.when(i == n - 1)
        def _():
            gb_ref[...] = jnp.sum(acc[...], axis=0, keepdims=True)

    rev = lambda i: (n - 1 - i, 0)
    return pl.pallas_call(
        body, grid=(n,),
        in_specs=[pl.BlockSpec((tb, N_HEADS * LANES), rev), pl.BlockSpec((tb, N_HEADS * LANES), rev),
                  pl.BlockSpec((tb, ATT_W), rev), pl.BlockSpec((tb, LANES), rev),
                  pl.BlockSpec((1, LANES), lambda i: (0, 0)), pl.BlockSpec((tb, tb), lambda i: (0, 0))],
        out_specs=[pl.BlockSpec((tb, 3 * ATT_W), rev), pl.BlockSpec((tb, LANES), rev),
                   pl.BlockSpec((1, LANES), lambda i: (0, 0))],
        out_shape=[SDS((s, 3 * ATT_W), BF16), SDS((s, LANES), BF16), SDS((1, LANES), F32)],
        scratch_shapes=[pltpu.VMEM((8, LANES), F32), pltpu.VMEM((8, LANES), F32)],
        name="fox_post", compiler_params=_cp(("arbitrary",)))(dq_aug, dk_aug, dv, fa, bfo, _tri(tb, True))


def rope_tables(s, sign):
    half = ROPE_DIM // 2
    inv_freq = ROPE_THETA ** (-jnp.arange(half, dtype=F32) * 2.0 / ROPE_DIM)
    ang = jnp.arange(s, dtype=F32)[:, None] * inv_freq[None, :]
    l64 = np.arange(LANES) % HEAD_DIM
    cos = jnp.cos(ang)[:, l64 % half]
    sin = jnp.sin(ang)[:, l64 % half] * sign
    first = jnp.asarray(l64 < half)[None, :]
    second = jnp.asarray((l64 >= half) & (l64 < ROPE_DIM))[None, :]
    return (jnp.where(first | second, cos, 1.0), jnp.where(first, -sin, 0.0), jnp.where(second, sin, 0.0))


def rope_apply(items, tabs, *, out_dtype, tm=512, name):
    s = items[0][0].shape[0]
    n_i = len(items)

    def body(*refs):
        c_ref, sn_ref, sp_ref = refs[n_i:n_i + 3]
        o_ref = refs[-1]
        for j, (_, _, scale, rotate) in enumerate(items):
            for b in range(4):
                xv = refs[j][:, b * LANES:(b + 1) * LANES].astype(F32)
                if rotate:
                    xv = xv * c_ref[...] + pltpu.roll(xv, LANES - 8, axis=1) * sn_ref[...] + pltpu.roll(xv, 8, axis=1) * sp_ref[...]
                o_ref[:, j * ATT_W + b * LANES:j * ATT_W + (b + 1) * LANES] = (xv * scale).astype(o_ref.dtype)

    in_specs = [pl.BlockSpec((tm, ATT_W), functools.partial(lambda i, blk: (i, blk), blk=it[1])) for it in items]
    in_specs += [pl.BlockSpec((tm, LANES), lambda i: (i, 0))] * 3
    return pl.pallas_call(
        body, grid=(s // tm,), in_specs=in_specs, out_specs=pl.BlockSpec((tm, n_i * ATT_W), lambda i: (i, 0)),
        out_shape=SDS((s, n_i * ATT_W), out_dtype), name=name, compiler_params=_cp(("parallel",)))(*[it[0] for it in items], *tabs)


def _dil_views(qk, z, r):
    s = z.shape[0]
    return qk.reshape(s // r, r * 2 * ATT_W), z.reshape(s // r, r * Z_W)


def _dil_cols(r):
    q_col = lambda rho, hp: rho * 8 + hp
    k_col = lambda rho, hp: rho * 8 + 4 + hp
    v_col = lambda rho, hp: rho * (Z_W // LANES) + 4 * Z_VB + hp
    return q_col, k_col, v_col


def _dil_scores(qv, kp, kc, head, has_prev):
    b = DIL_BLK
    qm = jnp.where(head, qv, jnp.zeros_like(qv))
    row, col = _row((b, b)), _lane((b, b))
    sp = jnp.where((col >= row) & has_prev, _nt(qm, kp), NEG)
    sc = jnp.where(col <= row, _nt(qm, kc), NEG)
    return sp, sc


def dil_fwd(qk, z, prev, *, r):
    s = z.shape[0]
    b = DIL_BLK
    l_sub = s // r
    nb = l_sub // b
    qk_v, z_v = _dil_views(qk, z, r)
    q_col, k_col, v_col = _dil_cols(r)
    merge = prev is not None

    def body(*refs):
        if merge:
            q_ref, kp_ref, kc_ref, vp_ref, vc_ref, op_ref, lp_ref, o_ref, l_ref = refs
        else:
            q_ref, kp_ref, kc_ref, vp_ref, vc_ref, o_ref, l_ref = refs
        has_prev = pl.program_id(2) > 0
        lane = _lane((b, LANES))
        res = []
        for i in range(2):
            head = (lane < 64) if i == 0 else (lane >= 64)
            sp, sc = _dil_scores(q_ref[...], kp_ref[...], kc_ref[...], head, has_prev)
            m = jnp.maximum(jnp.max(sp, axis=-1, keepdims=True), jnp.max(sc, axis=-1, keepdims=True))
            pp = jnp.exp(sp - m)
            pc = jnp.exp(sc - m)
            den = jnp.sum(pp, axis=-1, keepdims=True) + jnp.sum(pc, axis=-1, keepdims=True)
            ov = (_nn(pp.astype(BF16), vp_ref[...]) + _nn(pc.astype(BF16), vc_ref[...])) / den
            res.append((ov, m + jnp.log(den)))
        ov = jnp.where(lane < 64, res[0][0], res[1][0])
        lse = jnp.where(lane < 64, res[0][1], res[1][1])
        if merge:
            lp = lp_ref[...]
            m2 = jnp.maximum(lp, lse)
            wp = jnp.exp(lp - m2)
            wn = jnp.exp(lse - m2)
            ov = (wp * op_ref[...] + wn * ov) / (wp + wn)
            lse = m2 + jnp.log(wp + wn)
        o_ref[...] = ov
        l_ref[...] = lse

    blk = lambda f: pl.BlockSpec((b, LANES), f)
    in_specs = [blk(lambda rho, hp, n: (n, q_col(rho, hp))), blk(lambda rho, hp, n: (jnp.maximum(n - 1, 0), k_col(rho, hp))),
                blk(lambda rho, hp, n: (n, k_col(rho, hp))), blk(lambda rho, hp, n: (jnp.maximum(n - 1, 0), v_col(rho, hp))),
                blk(lambda rho, hp, n: (n, v_col(rho, hp)))]
    args = [qk_v, qk_v, qk_v, z_v, z_v]
    nat = blk(lambda rho, hp, n: (n, rho * 4 + hp))
    if merge:
        in_specs += [nat, nat]
        args += [prev[0].reshape(l_sub, r * ATT_W), prev[1].reshape(l_sub, r * ATT_W)]
    o, lse = pl.pallas_call(
        body, grid=(r, 4, nb), in_specs=in_specs, out_specs=[nat, nat],
        out_shape=[SDS((l_sub, r * ATT_W), F32)] * 2, name=f"dil_fwd_r{r}",
        compiler_params=_cp(("parallel", "parallel", "arbitrary")))(*args)
    return o.reshape(s, ATT_W), lse.reshape(s, ATT_W)


def dil_bwd_dq(qk, z, dy, lse, dd, acc, *, r):
    s = z.shape[0]
    b = DIL_BLK
    l_sub = s // r
    nb = l_sub // b
    qk_v, z_v = _dil_views(qk, z, r)
    q_col, k_col, v_col = _dil_cols(r)
    add = acc is not None

    def body(*refs):
        q_ref, kp_ref, kc_ref, vp_ref, vc_ref, do_ref, l_ref, dd_ref = refs[:8]
        dq_ref = refs[-1]
        has_prev = pl.program_id(2) > 0
        lane = _lane((b, LANES))
        dov = do_ref[...]
        parts = []
        for i in range(2):
            head = (lane < 64) if i == 0 else (lane >= 64)
            sp, sc = _dil_scores(q_ref[...], kp_ref[...], kc_ref[...], head, has_prev)
            lse_i = l_ref[:, i * 64:i * 64 + 1]
            dd_i = dd_ref[:, i * 64:i * 64 + 1]
            dom = jnp.where(head, dov, jnp.zeros_like(dov))
            dsp = (jnp.exp(sp - lse_i) * (_nt(dom, vp_ref[...]) - dd_i)).astype(BF16)
            dsc = (jnp.exp(sc - lse_i) * (_nt(dom, vc_ref[...]) - dd_i)).astype(BF16)
            parts.append(_nn(dsp, kp_ref[...]) + _nn(dsc, kc_ref[...]))
        dq = jnp.where(lane < 64, parts[0], parts[1])
        if add:
            dq = dq + refs[8][...]
        dq_ref[...] = dq

    blk = lambda f: pl.BlockSpec((b, LANES), f)
    nat = blk(lambda rho, hp, n: (n, rho * 4 + hp))
    in_specs = [blk(lambda rho, hp, n: (n, q_col(rho, hp))), blk(lambda rho, hp, n: (jnp.maximum(n - 1, 0), k_col(rho, hp))),
                blk(lambda rho, hp, n: (n, k_col(rho, hp))), blk(lambda rho, hp, n: (jnp.maximum(n - 1, 0), v_col(rho, hp))),
                blk(lambda rho, hp, n: (n, v_col(rho, hp))), nat, nat, nat]
    nview = lambda a: a.reshape(l_sub, r * ATT_W)
    args = [qk_v, qk_v, qk_v, z_v, z_v, nview(dy), nview(lse), nview(dd)]
    if add:
        in_specs.append(nat)
        args.append(nview(acc))
    dq = pl.pallas_call(
        body, grid=(r, 4, nb), in_specs=in_specs, out_specs=nat, out_shape=SDS((l_sub, r * ATT_W), F32),
        name=f"dil_bwd_dq_r{r}", compiler_params=_cp(("parallel", "parallel", "arbitrary")))(*args)
    return dq.reshape(s, ATT_W)


def dil_bwd_dkv(qk, z, dy, lse, dd, acc, *, r):
    s = z.shape[0]
    b = DIL_BLK
    l_sub = s // r
    nb = l_sub // b
    qk_v, z_v = _dil_views(qk, z, r)
    q_col, k_col, v_col = _dil_cols(r)
    add = acc is not None

    def body(*refs):
        k_ref, v_ref, qc_ref, qn_ref, doc_ref, don_ref, lc_ref, ln_ref, ddc_ref, ddn_ref = refs[:10]
        dk_ref, dv_ref = refs[-2:]
        has_next = pl.program_id(2) < nb - 1
        lane = _lane((b, LANES))
        row, col = _row((b, b)), _lane((b, b))
        kv = k_ref[...]
        vv = v_ref[...]
        dk_parts, dv_parts = [], []
        for i in range(2):
            head = (lane < 64) if i == 0 else (lane >= 64)
            dk_i = jnp.zeros((b, LANES), F32)
            dv_i = jnp.zeros((b, LANES), F32)
            for q_ref, do_ref, l_ref, d_ref, mask in ((qc_ref, doc_ref, lc_ref, ddc_ref, col <= row),
                                                      (qn_ref, don_ref, ln_ref, ddn_ref, (col >= row) & has_next)):
                qv = q_ref[...]
                dov = do_ref[...]
                sc = jnp.where(mask, _nt(jnp.where(head, qv, jnp.zeros_like(qv)), kv), NEG)
                p = jnp.exp(sc - l_ref[:, i * 64:i * 64 + 1])
                dp = _nt(jnp.where(head, dov, jnp.zeros_like(dov)), vv)
                ds = (p * (dp - d_ref[:, i * 64:i * 64 + 1])).astype(BF16)
                dv_i = dv_i + _tn(p.astype(BF16), dov)
                dk_i = dk_i + _tn(ds, qv)
            dk_parts.append(dk_i)
            dv_parts.append(dv_i)
        dk = jnp.where(lane < 64, dk_parts[0], dk_parts[1])
        dv = jnp.where(lane < 64, dv_parts[0], dv_parts[1])
        if add:
            dk = dk + refs[10][...]
            dv = dv + refs[11][...]
        dk_ref[...] = dk
        dv_ref[...] = dv

    blk = lambda f: pl.BlockSpec((b, LANES), f)
    nat = blk(lambda rho, hp, n: (n, rho * 4 + hp))
    nxt = blk(lambda rho, hp, n: (jnp.minimum(n + 1, nb - 1), rho * 4 + hp))
    in_specs = [blk(lambda rho, hp, n: (n, k_col(rho, hp))), blk(lambda rho, hp, n: (n, v_col(rho, hp))),
                blk(lambda rho, hp, n: (n, q_col(rho, hp))), blk(lambda rho, hp, n: (jnp.minimum(n + 1, nb - 1), q_col(rho, hp))),
                nat, nxt, nat, nxt, nat, nxt]
    nview = lambda a: a.reshape(l_sub, r * ATT_W)
    args = [qk_v, z_v, qk_v, qk_v, nview(dy), nview(dy), nview(lse), nview(lse), nview(dd), nview(dd)]
    if add:
        in_specs += [nat, nat]
        args += [nview(acc[0]), nview(acc[1])]
    dk, dv = pl.pallas_call(
        body, grid=(r, 4, nb), in_specs=in_specs, out_specs=[nat, nat],
        out_shape=[SDS((l_sub, r * ATT_W), F32)] * 2, name=f"dil_bwd_dkv_r{r}",
        compiler_params=_cp(("parallel", "parallel", "arbitrary")))(*args)
    return dk.reshape(s, ATT_W), dv.reshape(s, ATT_W)


def _dil_rows(base, r):
    if r == 1:
        return pl.ds(pl.multiple_of(base, DIL_BLK), DIL_BLK)
    return pl.ds(base, DIL_BLK, stride=r)


def _dil_block(idx, r, nb):
    shift = nb.bit_length() - 1
    rho = idx >> shift
    n = idx & (nb - 1)
    base = rho + n * (r * DIL_BLK)
    return _dil_rows(base, r), _dil_rows(jnp.maximum(base - r * DIL_BLK, rho), r), n > 0


def _cat(a, b):
    return jnp.concatenate([a, b], axis=0)


def _two_heads(v, first_head):
    zero = jnp.zeros_like(v)
    return _cat(jnp.where(first_head, v, zero), jnp.where(first_head, zero, v))


def _dil_bands():
    b = DIL_BLK
    q = _row((2 * b, 2 * b)) & (b - 1)
    col = _lane((2 * b, 2 * b))
    return (col < b) & (col >= q), (col >= b) & (col - b <= q)


def dil_fwd_all(qkv, *, unroll=2):
    s = qkv.shape[0]
    b = DIL_BLK
    n_blk = s // b

    def body(q_ref, k_ref, v_ref, o_ref, l_ref):
        first_head = _lane((b, LANES)) < 64
        band_prev, band_cur = _dil_bands()
        for g, (_, r) in enumerate(DIL_PATTERNS):
            nb = n_blk // r

            def group(it, carry, g=g, r=r, nb=nb):
                loaded = []
                for u in range(unroll):
                    rows_c, rows_p, has_prev = _dil_block(it * unroll + u, r, nb)
                    vals = [q_ref[rows_c, :].astype(BF16), k_ref[rows_p, :].astype(BF16), k_ref[rows_c, :].astype(BF16),
                            v_ref[rows_p, :].astype(BF16), v_ref[rows_c, :].astype(BF16)]
                    state = (o_ref[rows_c, :], l_ref[rows_c, :]) if g else None
                    loaded.append((rows_c, has_prev, vals, state))
                done = []
                for rows_c, has_prev, (qv, kp, kc, vp, vc), state in loaded:
                    sc = jnp.where(band_cur | (band_prev & has_prev), _nt(_two_heads(qv, first_head), _cat(kp, kc)), NEG)
                    m = jnp.max(sc, axis=-1, keepdims=True)
                    p = jnp.exp(sc - m)
                    den = jnp.sum(p, axis=-1, keepdims=True)
                    both = _nn(p.astype(BF16), _cat(vp, vc)) / den
                    lse2 = m + jnp.log(den)
                    ov = jnp.where(first_head, both[:b], both[b:])
                    lse = jnp.where(first_head, lse2[:b], lse2[b:])
                    if state is not None:
                        m2 = jnp.maximum(state[1], lse)
                        wp = jnp.exp(state[1] - m2)
                        wn = jnp.exp(lse - m2)
                        ov = (wp * state[0] + wn * ov) / (wp + wn)
                        lse = m2 + jnp.log(wp + wn)
                    done.append((rows_c, ov, lse))
                for rows_c, ov, lse in done:
                    o_ref[rows_c, :] = ov
                    l_ref[rows_c, :] = lse
                return carry

            lax.fori_loop(0, n_blk // unroll, group, 0)

    col_blk = lambda k: pl.BlockSpec((s, LANES), lambda hp: (0, 4 * k + hp))
    out = pl.BlockSpec((s, LANES), lambda hp: (0, hp))
    return pl.pallas_call(
        body, grid=(4,), in_specs=[col_blk(0), col_blk(1), col_blk(2)], out_specs=[out, out],
        out_shape=[SDS((s, ATT_W), F32)] * 2, name="dil_fwd", compiler_params=_cp(("parallel",)))(qkv, qkv, qkv)


def dil_bwd_all(qkv, dy, lse, y, exchange=(), *, unroll=2):
    s = qkv.shape[0]
    b = DIL_BLK
    n_blk = s // b
    ne = len(exchange)

    def body(q_ref, k_ref, v_ref, do_ref, l_ref, y_ref, *rest):
        e_ins, (dq_ref, dk_ref, dv_ref), e_outs = rest[:ne], rest[ne:ne + 3], rest[ne + 3:2 * ne + 3]
        comm = (e_ins, e_outs) + tuple(rest[2 * ne + 3:])
        if ne:
            @pl.when(pl.program_id(0) == 0)
            def _():
                _to_chips_start(*comm)

        dq_ref[...] = jnp.zeros_like(dq_ref)
        dk_ref[...] = jnp.zeros_like(dk_ref)
        dv_ref[...] = jnp.zeros_like(dv_ref)
        first_head = _lane((b, LANES)) < 64
        band_prev, band_cur = _dil_bands()
        for _, r in DIL_PATTERNS:
            nb = n_blk // r

            def group(it, carry, r=r, nb=nb):
                loaded = []
                for u in range(unroll):
                    rows_c, rows_p, has_prev = _dil_block(it * unroll + u, r, nb)
                    vals = [q_ref[rows_c, :].astype(BF16), k_ref[rows_p, :].astype(BF16), k_ref[rows_c, :].astype(BF16),
                            v_ref[rows_p, :].astype(BF16), v_ref[rows_c, :].astype(BF16), do_ref[rows_c, :],
                            l_ref[rows_c, :], y_ref[rows_c, :]]
                    loaded.append((rows_c, rows_p, has_prev, vals))
                done = []
                for rows_c, rows_p, has_prev, (qv, kp, kc, vp, vc, dof, lv, yv) in loaded:
                    q2 = _two_heads(qv, first_head)
                    do2 = _two_heads(dof.astype(BF16), first_head)
                    kcat, vcat = _cat(kp, kc), _cat(vp, vc)
                    lse2 = _cat(lv[:, 0:1], lv[:, 64:65])
                    dd2 = jnp.sum(_two_heads(dof * yv, first_head), axis=-1, keepdims=True)
                    p = jnp.exp(jnp.where(band_cur | (band_prev & has_prev), _nt(q2, kcat), NEG) - lse2)
                    ds = (p * (_nt(do2, vcat) - dd2)).astype(BF16)
                    dq2 = _nn(ds, kcat)
                    dkcat = _tn(ds, q2)
                    dvcat = _tn(p.astype(BF16), do2)
                    done.append((rows_c, rows_p, (jnp.where(first_head, dq2[:b], dq2[b:]), dkcat[:b], dkcat[b:],
                                                  dvcat[:b], dvcat[b:])))
                for rows_c, rows_p, (dq, dk_p, dk_c, dv_p, dv_c) in done:
                    dq_ref[rows_c, :] += dq
                    dk_ref[rows_p, :] += dk_p
                    dk_ref[rows_c, :] += dk_c
                    dv_ref[rows_p, :] += dv_p
                    dv_ref[rows_c, :] += dv_c
                return carry

            lax.fori_loop(0, n_blk // unroll, group, 0)

        if ne:
            @pl.when(pl.program_id(0) == 3)
            def _():
                _to_chips_finish(*comm)

    col_blk = lambda k: pl.BlockSpec((s, LANES), lambda hp: (0, 4 * k + hp))
    nat = pl.BlockSpec((s, LANES), lambda hp: (0, hp))
    return pl.pallas_call(
        body, grid=(4,), in_specs=[col_blk(0), col_blk(1), col_blk(2), nat, nat, nat] + [ANY] * ne,
        out_specs=[nat, nat, nat] + [ANY] * ne, out_shape=[SDS((s, ATT_W), F32)] * 3 + _to_chips_shapes(exchange),
        scratch_shapes=_to_chips_sems(ne) if ne else [], name="dil_bwd",
        compiler_params=_cp(("arbitrary",)))(qkv, qkv, qkv, dy, lse, y, *exchange)


def _sigmoid(v):
    return 1.0 / (1.0 + jnp.exp(-v))


def gate_mix(ya, yb, wa, wb, z, *, tm=512, tn=512):
    s = ya.shape[0]
    d = wa.shape[1]
    ga_blk = 3 * ATT_W * 2 // tn
    gb_blk = ga_blk + d // tn

    def body(ya_ref, yb_ref, wa_ref, wb_ref, ga_ref, gb_ref, pa_ref, pb_ref, mx_ref):
        pa = _nn(ya_ref[...], wa_ref[...])
        pb = _nn(yb_ref[...].astype(BF16), wb_ref[...])
        pa_ref[...] = pa.astype(BF16)
        pb_ref[...] = pb.astype(BF16)
        mx_ref[...] = (_sigmoid(ga_ref[...].astype(F32)) * pa + _sigmoid(gb_ref[...].astype(F32)) * pb).astype(BF16)

    out = pl.BlockSpec((tm, tn), lambda i, j: (i, j))
    return pl.pallas_call(
        body, grid=(s // tm, d // tn),
        in_specs=[pl.BlockSpec((tm, ATT_W), lambda i, j: (i, 0)), pl.BlockSpec((tm, ATT_W), lambda i, j: (i, 0)),
                  pl.BlockSpec((ATT_W, tn), lambda i, j: (0, j)), pl.BlockSpec((ATT_W, tn), lambda i, j: (0, j)),
                  pl.BlockSpec((tm, tn), lambda i, j: (i, ga_blk + j)), pl.BlockSpec((tm, tn), lambda i, j: (i, gb_blk + j))],
        out_specs=[out, out, out], out_shape=[SDS((s, d), BF16)] * 3, name="gate_mix",
        compiler_params=_cp(("parallel", "parallel")))(ya, yb, wa, wb, z, z)


def gate_bwd(dmx, z, pa, pb, *, tm=256):
    s, d = dmx.shape

    def body(dm_ref, ga_ref, gb_ref, pa_ref, pb_ref, dpa_ref, dpb_ref, dg_ref):
        dm = dm_ref[...].astype(F32)
        sa = _sigmoid(ga_ref[...].astype(F32))
        sb = _sigmoid(gb_ref[...].astype(F32))
        dpa_ref[...] = (dm * sa).astype(BF16)
        dpb_ref[...] = (dm * sb).astype(BF16)
        dg_ref[:, 0:d] = (dm * pa_ref[...].astype(F32) * sa * (1.0 - sa)).astype(BF16)
        dg_ref[:, d:2 * d] = (dm * pb_ref[...].astype(F32) * sb * (1.0 - sb)).astype(BF16)

    row = pl.BlockSpec((tm, d), lambda i: (i, 0))
    return pl.pallas_call(
        body, grid=(s // tm,),
        in_specs=[row, pl.BlockSpec((tm, d), lambda i: (i, 3)), pl.BlockSpec((tm, d), lambda i: (i, 4)), row, row],
        out_specs=[row, row, pl.BlockSpec((tm, 2 * d), lambda i: (i, 0))],
        out_shape=[SDS((s, d), BF16), SDS((s, d), BF16), SDS((s, 2 * d), BF16)], name="gate_bwd",
        compiler_params=_cp(("parallel",)))(dmx, z, z, pa, pb)


GELU_C = math.sqrt(2.0 / math.pi)


def _gelu_parts(a):
    inner = GELU_C * (a + 0.044715 * a * a * a)
    th = jnp.tanh(inner)
    gelu = 0.5 * a * (1.0 + th)
    dgelu = 0.5 * (1.0 + th) + 0.5 * a * (1.0 - th * th) * GELU_C * (1.0 + 3.0 * 0.044715 * a * a)
    return gelu, dgelu


def _causal_taps(u, before):
    row = _row(u.shape)
    r1 = jnp.where(row == 0, before[7:8, :], pltpu.roll(u, 1, axis=0))
    r2 = jnp.where(row == 0, before[6:7, :], jnp.where(row == 1, before[7:8, :], pltpu.roll(u, 2, axis=0)))
    return r1, r2


def ffn_up(h, wa, wb, cw, cb, *, tm=512, tn=256):
    s, d = h.shape
    f = wa.shape[1]
    nj = f // tn

    def body(h_ref, wa_ref, wb_ref, cwa_ref, cwb_ref, cba_ref, cbb_ref, ua_ref, ub_ref, m_ref, carry):
        @pl.when(pl.program_id(1) == 0)
        def _():
            carry[...] = jnp.zeros_like(carry)

        conv = []
        for k, (w_ref, cw_ref, cb_ref, u_ref) in enumerate(((wa_ref, cwa_ref, cba_ref, ua_ref), (wb_ref, cwb_ref, cbb_ref, ub_ref))):
            u16 = _nn(h_ref[...], w_ref[...]).astype(BF16)
            u_ref[...] = u16
            u = u16.astype(F32)
            r1, r2 = _causal_taps(u, carry[k])
            carry[k] = u[tm - 8:tm, :]
            conv.append(cw_ref[0:1, :] * r2 + cw_ref[1:2, :] * r1 + cw_ref[2:3, :] * u + cb_ref[...])
        m_ref[...] = (_gelu_parts(conv[0])[0] * conv[1]).astype(BF16)

    out = pl.BlockSpec((tm, tn), lambda j, i: (i, j))
    return pl.pallas_call(
        body, grid=(nj, s // tm),
        in_specs=[pl.BlockSpec((tm, d), lambda j, i: (i, 0)),
                  pl.BlockSpec((d, tn), lambda j, i: (0, j)), pl.BlockSpec((d, tn), lambda j, i: (0, j)),
                  pl.BlockSpec((3, tn), lambda j, i: (0, j)), pl.BlockSpec((3, tn), lambda j, i: (0, nj + j)),
                  pl.BlockSpec((1, tn), lambda j, i: (0, j)), pl.BlockSpec((1, tn), lambda j, i: (0, nj + j))],
        out_specs=[out, out, out], out_shape=[SDS((s, f), BF16)] * 3,
        scratch_shapes=[pltpu.VMEM((2, 8, tn), F32)], name="ffn_up",
        compiler_params=_cp(("parallel", "arbitrary")))(h, wa, wb, cw, cw, cb, cb)


def ffn_bwd(dm, ua, ub, cw, cb, *, tm=512, tn=256):
    s, f = dm.shape
    nj = f // tn
    ni = s // tm
    halo = 16

    def body(dm_ref, ua_ref, ub_ref, ha_ref, hb_ref, cwa_ref, cwb_ref, cba_ref, cbb_ref,
             dua_ref, dub_ref, ga_ref, gb_ref, carry):
        i = pl.program_id(1)

        @pl.when(i == 0)
        def _():
            carry[...] = jnp.zeros_like(carry)
            ga_ref[...] = jnp.zeros_like(ga_ref)
            gb_ref[...] = jnp.zeros_like(gb_ref)

        first_tile = i == ni - 1
        row = _row((tm, tn))
        dmv = dm_ref[...].astype(F32)
        us, taps, convs = [], [], []
        for u_ref, h_ref, cw_ref, cb_ref in ((ua_ref, ha_ref, cwa_ref, cba_ref), (ub_ref, hb_ref, cwb_ref, cbb_ref)):
            u = u_ref[...].astype(F32)
            before = jnp.where(first_tile, 0.0, h_ref[halo - 8:halo, :].astype(F32))
            r1, r2 = _causal_taps(u, before)
            us.append(u)
            taps.append((r1, r2))
            convs.append(cw_ref[0:1, :] * r2 + cw_ref[1:2, :] * r1 + cw_ref[2:3, :] * u + cb_ref[...])
        gelu, dgelu = _gelu_parts(convs[0])
        dcs = (dmv * convs[1] * dgelu, dmv * gelu)
        for k, (dc, cw_ref, du_ref, g_ref) in enumerate(((dcs[0], cwa_ref, dua_ref, ga_ref), (dcs[1], cwb_ref, dub_ref, gb_ref))):
            r1, r2 = taps[k]
            g_ref[0:1, :] += jnp.sum(dc * r2, axis=0, keepdims=True)
            g_ref[1:2, :] += jnp.sum(dc * r1, axis=0, keepdims=True)
            g_ref[2:3, :] += jnp.sum(dc * us[k], axis=0, keepdims=True)
            g_ref[3:4, :] += jnp.sum(dc, axis=0, keepdims=True)
            after = carry[k]
            n1 = jnp.where(row == tm - 1, after[0:1, :], pltpu.roll(dc, tm - 1, axis=0))
            n2 = jnp.where(row == tm - 2, after[0:1, :], jnp.where(row == tm - 1, after[1:2, :], pltpu.roll(dc, tm - 2, axis=0)))
            du_ref[...] = (cw_ref[2:3, :] * dc + cw_ref[1:2, :] * n1 + cw_ref[0:1, :] * n2).astype(BF16)
            carry[k] = dc[0:8, :]

    tile = pl.BlockSpec((tm, tn), lambda j, i: (ni - 1 - i, j))
    halo_spec = pl.BlockSpec((halo, tn), lambda j, i: (jnp.maximum((ni - 1 - i) * (tm // halo) - 1, 0), j))
    gspec = pl.BlockSpec((8, tn), lambda j, i: (0, j))
    return pl.pallas_call(
        body, grid=(nj, ni),
        in_specs=[tile, tile, tile, halo_spec, halo_spec,
                  pl.BlockSpec((3, tn), lambda j, i: (0, j)), pl.BlockSpec((3, tn), lambda j, i: (0, nj + j)),
                  pl.BlockSpec((1, tn), lambda j, i: (0, j)), pl.BlockSpec((1, tn), lambda j, i: (0, nj + j))],
        out_specs=[tile, tile, gspec, gspec],
        out_shape=[SDS((s, f), BF16), SDS((s, f), BF16), SDS((8, f), F32), SDS((8, f), F32)],
        scratch_shapes=[pltpu.VMEM((2, 8, tn), F32)], name="ffn_bwd",
        compiler_params=_cp(("parallel", "arbitrary")))(dm, ua, ub, ua, ub, cw, cw, cb, cb)


def adamw(w, g, m, v, *, name, tr=None):
    r = w.shape[0]
    rest = w.shape[1:]
    if tr is None:
        tr = r
        for cand in (256, 128, 64, 32, 16, 8):
            if r % cand == 0:
                tr = cand
                break

    def body(w_ref, g_ref, m_ref, v_ref, d_ref, nm_ref, nv_ref):
        gv = g_ref[...]
        mn = ADAM_B1 * m_ref[...] + (1.0 - ADAM_B1) * gv
        vn = ADAM_B2 * v_ref[...] + (1.0 - ADAM_B2) * (gv * gv)
        m_hat = mn / (1.0 - ADAM_B1 ** ADAM_STEP)
        v_hat = vn / (1.0 - ADAM_B2 ** ADAM_STEP)
        d_ref[...] = -ADAM_LR * (m_hat / (jnp.sqrt(v_hat) + ADAM_EPS) + ADAM_WD * w_ref[...])
        nm_ref[...] = mn
        nv_ref[...] = vn

    blk = pl.BlockSpec((tr,) + rest, lambda i: (i,) + (0,) * len(rest))
    return pl.pallas_call(body, grid=(r // tr,), in_specs=[blk] * 4, out_specs=[blk] * 3, out_shape=[SDS(w.shape, F32)] * 3,
                          name=name, compiler_params=_cp(("parallel",)))(w, g, m, v)


ANY = pl.BlockSpec(memory_space=pl.ANY)
ICI_KINDS = ("x", "y", "xy")


def _coords():
    return lax.axis_index("x"), lax.axis_index("y"), lax.axis_index("c")


def _peer(kind, x, y, c):
    if kind == "c":
        return (x, y, 1 - c)
    if kind == "x":
        return (1 - x, y, c)
    if kind == "y":
        return (x, 1 - y, c)
    return (1 - x, 1 - y, c)


def _chip_of(p):
    return 2 * p[0] + p[1]


def _half(rows, which):
    h = rows // 2
    return pl.ds(pl.multiple_of(which * h, 16), h)


def _remote(src, dst, send_sem, recv_sem, to):
    return pltpu.make_async_remote_copy(src_ref=src, dst_ref=dst, send_sem=send_sem, recv_sem=recv_sem,
                                        device_id=to, device_id_type=MESH)


def allgather_chips(shards, halved, *, name):
    n = len(shards)

    def body(*refs):
        parts = (refs[:n], refs[n:2 * n], refs[2 * n], refs[2 * n + 1], halved)
        _allgather_start(*parts)
        _allgather_finish(*parts)

    return pl.pallas_call(
        body, in_specs=[ANY] * n, out_specs=[ANY] * n,
        out_shape=_allgather_shapes(shards), scratch_shapes=_allgather_sems(n), name=name)(*shards)


def _allgather_shapes(shards):
    return [SDS((4,) + a.shape, a.dtype) for a in shards]


def _allgather_sems(n):
    return [pltpu.SemaphoreType.DMA((n, 6)), pltpu.SemaphoreType.DMA((n, 6))]


def _allgather_rows(ref, is_halved, which):
    r = ref.shape[0]
    return _half(r, which) if is_halved else pl.ds(0, r)


def _allgather_first(ins, outs, send_sems, recv_sems, halved):
    x, y, c = _coords()
    my_chip = 2 * x + y
    cps = []
    for w in range(len(ins)):
        rows = _allgather_rows(ins[w], halved[w], c)
        for k, kind in enumerate(ICI_KINDS):
            cps.append(_remote(ins[w].at[rows], outs[w].at[my_chip, rows], send_sems.at[w, k], recv_sems.at[w, k],
                               _peer(kind, x, y, c)))
    return cps


def _allgather_start(ins, outs, send_sems, recv_sems, halved):
    for cp in _allgather_first(ins, outs, send_sems, recv_sems, halved):
        cp.start()


def _allgather_finish(ins, outs, send_sems, recv_sems, halved):
    x, y, c = _coords()
    me = (x, y, c)
    second = []
    for w in range(len(ins)):
        for k, kind in enumerate(ICI_KINDS):
            landed = outs[w].at[_chip_of(_peer(kind, x, y, c)), _allgather_rows(ins[w], halved[w], c)]
            _remote(landed, landed, send_sems.at[w, k], recv_sems.at[w, k], me).wait_recv()
            if halved[w]:
                cp = _remote(landed, landed, send_sems.at[w, 3 + k], recv_sems.at[w, 3 + k], _peer("c", x, y, c))
                cp.start()
                second.append(cp)
    for w in range(len(ins)):
        if halved[w]:
            for k, kind in enumerate(ICI_KINDS):
                other = outs[w].at[_chip_of(_peer(kind, x, y, c)), _allgather_rows(ins[w], True, 1 - c)]
                _remote(other, other, send_sems.at[w, 3 + k], recv_sems.at[w, 3 + k], me).wait_recv()
    for cp in _allgather_first(ins, outs, send_sems, recv_sems, halved) + second:
        cp.wait_send()


def _half_of(ref, by_cols, which):
    lead = (slice(None),) * (len(ref.shape) - 2)
    if by_cols:
        h = ref.shape[-1] // 2
        return ref.at[lead + (slice(None), pl.ds(pl.multiple_of(which * h, LANES), h))]
    return ref.at[lead + (_half(ref.shape[-2], which),)]


def _half_shape(shape, by_cols):
    return shape[:-1] + (shape[-1] // 2,) if by_cols else shape[:-2] + (shape[-2] // 2, shape[-1])


def grads_to_sibling(gs, by_cols, *, name):
    n = len(gs)

    def body(*refs):
        ins, outs = refs[:n], refs[n:2 * n]
        send_sems, recv_sems = refs[2 * n:]
        x, y, c = _coords()
        cps = []
        for w in range(n):
            cp = _remote(_half_of(ins[w], by_cols[w], 1 - c), outs[w], send_sems.at[w], recv_sems.at[w], _peer("c", x, y, c))
            cp.start()
            cps.append(cp)
        for cp in cps:
            cp.wait()

    return pl.pallas_call(
        body, in_specs=[ANY] * n, out_specs=[ANY] * n,
        out_shape=[SDS(_half_shape(a.shape, bc), a.dtype) for a, bc in zip(gs, by_cols)],
        scratch_shapes=[pltpu.SemaphoreType.DMA((n,)), pltpu.SemaphoreType.DMA((n,))], name=name)(*gs)


def grads_to_chips(ps, *, name):
    n = len(ps)

    def body(*refs):
        parts = (refs[:n], refs[n:2 * n], refs[2 * n], refs[2 * n + 1])
        _to_chips_start(*parts)
        _to_chips_finish(*parts)

    return pl.pallas_call(
        body, in_specs=[ANY] * n, out_specs=[ANY] * n,
        out_shape=_to_chips_shapes(ps), scratch_shapes=_to_chips_sems(n), name=name)(*ps)


def _to_chips_shapes(ps):
    return [SDS((3,) + a.shape[1:], a.dtype) for a in ps]


def _to_chips_sems(n):
    return [pltpu.SemaphoreType.DMA((n, 3)), pltpu.SemaphoreType.DMA((n, 3))]


def _to_chips_copies(ins, outs, send_sems, recv_sems):
    x, y, c = _coords()
    cps = []
    for w in range(len(ins)):
        for k, kind in enumerate(ICI_KINDS):
            to = _peer(kind, x, y, c)
            cps.append(_remote(ins[w].at[_chip_of(to)], outs[w].at[k], send_sems.at[w, k], recv_sems.at[w, k], to))
    return cps


def _to_chips_start(ins, outs, send_sems, recv_sems):
    for cp in _to_chips_copies(ins, outs, send_sems, recv_sems):
        cp.start()


def _to_chips_finish(ins, outs, send_sems, recv_sems):
    for cp in _to_chips_copies(ins, outs, send_sems, recv_sems):
        cp.wait()


def halves_to_full(hs, by_cols, *, name):
    n = len(hs)

    def body(*refs):
        ins, outs = refs[:n], refs[n:2 * n]
        send_sems, recv_sems = refs[2 * n:]
        x, y, c = _coords()
        cps = []
        for w in range(n):
            cp = _remote(ins[w], _half_of(outs[w], by_cols[w], c), send_sems.at[w], recv_sems.at[w], _peer("c", x, y, c))
            cp.start()
            cps.append(cp)
        for cp in cps:
            cp.wait()

    return pl.pallas_call(
        body, in_specs=[ANY] * n, out_specs=[ANY] * n,
        out_shape=[SDS((a.shape[0], 2 * a.shape[1]) if bc else (2 * a.shape[0], a.shape[1]), a.dtype)
                   for a, bc in zip(hs, by_cols)],
        scratch_shapes=[pltpu.SemaphoreType.DMA((n,)), pltpu.SemaphoreType.DMA((n,))],
        name=name)(*hs)


def _row_tile(rows):
    for cand in (256, 192, 176, 128, 64, 32, 16):
        if rows % cand == 0:
            return cand
    return rows


def chip_sum(g, recv, c_arr, by_cols, *, name):
    _, r, cols = g.shape

    def body(c_ref, g_ref, r_ref, f_ref, b_ref):
        tot = g_ref[...] + r_ref[...]
        f_ref[...] = tot
        b_ref[...] = tot.astype(BF16)

    if by_cols:
        tc = 2 * LANES
        nblk = cols // 2 // tc
        shape = (4, r, cols // 2)
        blk = pl.BlockSpec((None, r, tc), lambda j, i, c_ref: (j, 0, i))
        mine = pl.BlockSpec((None, r, tc), lambda j, i, c_ref: (j, 0, c_ref[0] * nblk + i))
    else:
        tr = _row_tile(r // 2)
        nblk = r // 2 // tr
        shape = (4, r // 2, cols)
        blk = pl.BlockSpec((None, tr, cols), lambda j, i, c_ref: (j, i, 0))
        mine = pl.BlockSpec((None, tr, cols), lambda j, i, c_ref: (j, c_ref[0] * nblk + i, 0))
    grid_spec = pltpu.PrefetchScalarGridSpec(num_scalar_prefetch=1, grid=(4, nblk), in_specs=[mine, blk], out_specs=[blk, blk])
    return pl.pallas_call(body, grid_spec=grid_spec, out_shape=[SDS(shape, F32), SDS(shape, BF16)],
                          name=name, compiler_params=_cp(("parallel", "parallel")))(c_arr, g, recv)


def final_sum(pf, recv, chip_arr, *, name):
    _, h, cols = pf.shape
    tr = _row_tile(h)

    def body(chip_ref, p_ref, r_ref, o_ref):
        o_ref[...] = ((p_ref[...] + r_ref[0].astype(F32)) + r_ref[1].astype(F32)) + r_ref[2].astype(F32)

    grid_spec = pltpu.PrefetchScalarGridSpec(
        num_scalar_prefetch=1, grid=(h // tr,),
        in_specs=[pl.BlockSpec((None, tr, cols), lambda i, chip_ref: (chip_ref[0], i, 0)),
                  pl.BlockSpec((3, tr, cols), lambda i, chip_ref: (0, i, 0))],
        out_specs=pl.BlockSpec((tr, cols), lambda i, chip_ref: (i, 0)))
    return pl.pallas_call(body, grid_spec=grid_spec, out_shape=SDS((h, cols), F32), name=name,
                          compiler_params=_cp(("parallel",)))(chip_arr, pf, recv)


def allreduce_small(v, *, name):
    rws, cols = v.shape

    def body(v_ref, all_ref, sum_ref, send_sems, recv_sems, local_sem):
        x, y, c = _coords()
        me, sibling = (x, y, c), (x, y, 1 - c)
        chips = [(1 - x, y), (x, 1 - y), (1 - x, 1 - y)]

        def rows(px, py, pc):
            return all_ref.at[pl.ds(pl.multiple_of((4 * px + 2 * py + pc) * rws, 8), rws), :]

        def copy(k, block, to, src=None):
            return _remote(rows(*block) if src is None else src, rows(*block), send_sems.at[k], recv_sems.at[k], to)

        mine = pltpu.make_async_copy(v_ref, rows(*me), local_sem)
        mine.start()
        first = [copy(0, me, sibling, src=v_ref)]
        first += [copy(1 + j, me, (*chip, c), src=v_ref) for j, chip in enumerate(chips)]
        for cp in first:
            cp.start()
        passed = [copy(4 + j, (*chip, c), sibling) for j, chip in enumerate(chips)]
        for j, chip in enumerate(chips):
            copy(1 + j, (*chip, c), me).wait_recv()
            passed[j].start()
        copy(0, sibling, me).wait_recv()
        for j, chip in enumerate(chips):
            copy(4 + j, (*chip, 1 - c), me).wait_recv()
        for cp in first + passed:
            cp.wait_send()
        mine.wait()
        tot = all_ref[0:rws, :]
        for dev in range(1, 8):
            tot = tot + all_ref[dev * rws:(dev + 1) * rws, :]
        sum_ref[...] = tot

    vm = pl.BlockSpec(memory_space=pltpu.VMEM)
    return pl.pallas_call(
        body, in_specs=[vm], out_specs=[vm, vm],
        out_shape=[SDS((8 * rws, cols), v.dtype), SDS((rws, cols), v.dtype)],
        scratch_shapes=[pltpu.SemaphoreType.DMA((7,)), pltpu.SemaphoreType.DMA((7,)), pltpu.SemaphoreType.DMA],
        name=name)(v)[1]


def _pack_rows(parts, rows):
    out = []
    for a, r in zip(parts, rows):
        flat = a.reshape(-1)
        flat = jnp.pad(flat, (0, r * LANES - flat.shape[0]))
        out.append(flat.reshape(r, LANES))
    return jnp.concatenate(out, axis=0)


def _unpack_rows(packed, shapes, rows):
    out, at = [], 0
    for shp, r in zip(shapes, rows):
        size = int(np.prod(shp))
        out.append(packed[at:at + r].reshape(-1)[:size].reshape(shp))
        at += r
    return out


def kernel(x, g_pre_mix, w_in, b_forget, w_o_fox, w_o_dil, w_out, g_post_mix, g_pre_ffn, w_up, conv_w, conv_b, w_down, g_post_ffn, loss_target, m_g_pre_mix, m_w_in, m_b_forget, m_w_o_fox, m_w_o_dil, m_w_out, m_g_post_mix, m_g_pre_ffn, m_w_up, m_conv_w, m_conv_b, m_w_down, m_g_post_ffn, v_g_pre_mix, v_w_in, v_b_forget, v_w_o_fox, v_w_o_dil, v_w_out, v_g_post_mix, v_g_pre_ffn, v_w_up, v_conv_w, v_conv_b, v_w_down, v_g_post_ffn):
    xi, yi, ci = _coords()
    chip = 2 * xi + yi
    c_arr = jnp.reshape(ci, (1,)).astype(jnp.int32)
    chip_arr = jnp.reshape(chip, (1,)).astype(jnp.int32)
    xs = x[0]
    target = loss_target[0]
    s, d = xs.shape
    f_half = w_down.shape[1] * 4
    cols_in = w_in.shape[2]

    big = (w_in, w_o_fox, w_o_dil, w_out, w_up, w_down)
    shards = [w[0].astype(BF16) for w in big]
    a_in, a_cw = allgather_chips([shards[0], conv_w[0]], [True, False], name="allgather_w_in")
    w_in_full = jnp.concatenate([jnp.where(chip == j, shards[0], a_in[j]) for j in range(4)], axis=1)
    cw = jnp.concatenate([jnp.where(chip == j, conv_w[0], a_cw[j]) for j in range(4)], axis=1)
    nf = N_HEADS
    e_a, e_b = 3 * ATT_W, 3 * ATT_W + nf
    wz = jnp.concatenate([w_in_full[:, :e_a], w_in_full[:, e_b:]], axis=1)
    wf = jnp.pad(w_in_full[:, e_a:e_b], ((0, 0), (0, LANES - nf)))
    cb = conv_b
    bfo = jnp.pad(b_forget, ((0, 0), (0, LANES - nf)))

    h1 = rmsnorm_fwd(xs, g_pre_mix)
    z = mm([(h1, d, 0)], [(wz, d, 0)], nt=False, out_dtype=BF16, tm=1024, tn=512, name="in_proj")
    fa = mm([(h1, d, 0)], [(wf, d, 0)], nt=False, out_dtype=F32, tm=1024, tn=LANES, name="in_proj_forget")
    q_aug, k_aug = fox_prep(z, fa, bfo)
    ya, lse_a, *late = fox_fwd(q_aug, k_aug, z, gather=shards[1:])
    a_of, a_od, a_out, a_up, a_down = [
        lax.dynamic_update_index_in_dim(a4, own, chip, 0) for a4, own in zip(late, shards[1:])]
    wo_a = jnp.concatenate([a_of[j] for j in range(4)], axis=1)
    wo_b = jnp.concatenate([a_od[j] for j in range(4)], axis=1)
    w_o = a_out.reshape(d, d)
    w_dn = a_down.reshape(f_half, d)
    wu_a = jnp.concatenate([a_up[0], a_up[1]], axis=1)
    wu_b = jnp.concatenate([a_up[2], a_up[3]], axis=1)
    qkv_b = rope_apply([(z, Z_QB, QK_SCALE, True), (z, Z_KB, 1.0, True), (z, Z_VB, 1.0, False)], rope_tables(s, 1.0),
                       out_dtype=F32, name="rope_fwd")
    yb, lse_b = dil_fwd_all(qkv_b)
    pa, pb, mixed = gate_mix(ya, yb, wo_a, wo_b, z)
    y1, x1 = mm_rms_res(mixed, w_o, g_post_mix, xs, tm=512, name="out_proj")
    h2 = rmsnorm_fwd(x1, g_pre_ffn)
    ua, ub, mid = ffn_up(h2, wu_a, wu_b, cw, cb)
    y2, dout, sq = mm_rms_res(mid, w_dn, g_post_ffn, x1, target, tm=512, name="down_proj")
    loss = lax.psum(0.5 * sq[0, 0] / d, ("x", "y", "c"))

    dy2, gg_post_ffn = rmsnorm_bwd(dout, y2, g_post_ffn, None, out_dtype=BF16, name="norm_bwd_post_ffn")
    dmid = mm([(dy2, d, 0)], [(w_dn, d, 0)], nt=True, out_dtype=BF16, tm=512, tn=f_half // 2, name="down_dgrad")
    dw_down = wgrad((mid, f_half, 0), dy2, tk=f_half // 2, tn=1024, ts=1024, name="down_wgrad")
    dua, dub, gc_a, gc_b = ffn_bwd(dmid, ua, ub, cw, cb)
    dh2 = mm([(dua, f_half, 0), (dub, f_half, 0)], [(wu_a, f_half, 0), (wu_b, f_half, 0)], nt=True, out_dtype=BF16,
             tm=512, tn=512, name="up_dgrad")
    dw_up = jnp.concatenate(
        [wgrad((h2, d, 0), du, tk=1024, tn=f_half // 2, ts=1024, name=f"up_wgrad_{k}", chip_major=True)
         for k, du in enumerate((dua, dub))], axis=0)
    def to_chip_sums(gs, nms, tag, by_cols=False):
        from_sib = grads_to_sibling(gs, [by_cols] * len(gs), name=f"grads_to_sibling_{tag}")
        return [chip_sum(g, r, c_arr, by_cols, name=f"chip_sum_{nm}") for g, r, nm in zip(gs, from_sib, nms)]

    sums_ffn = to_chip_sums([dw_up, dw_down.reshape(4, f_half // 4, d)], ("w_up", "w_down"), "ffn")
    dx1, gg_pre_ffn = rmsnorm_bwd(dh2, x1, g_pre_ffn, dout, out_dtype=F32, name="norm_bwd_pre_ffn")
    dy1, gg_post_mix = rmsnorm_bwd(dx1, y1, g_post_mix, None, out_dtype=BF16, name="norm_bwd_post_mix")
    dmixed = mm([(dy1, d, 0)], [(w_o, d, 0)], nt=True, out_dtype=BF16, tm=512, tn=512, name="out_dgrad")
    dw_out = wgrad((mixed, d, 0), dy1, tk=1024, tn=1024, ts=1024, name="out_wgrad")
    dpa, dpb, dz_g = gate_bwd(dmixed, z, pa, pb)
    dya = mm([(dpa, d, 0)], [(wo_a, d, 0)], nt=True, out_dtype=BF16, tm=512, tn=ATT_W, name="fox_o_dgrad")
    dyb = mm([(dpb, d, 0)], [(wo_b, d, 0)], nt=True, out_dtype=F32, tm=512, tn=ATT_W, name="dil_o_dgrad")
    by_chip_cols = lambda a: jnp.stack([a[:, j * (d // 4):(j + 1) * (d // 4)] for j in range(4)], axis=0)
    dw_of = by_chip_cols(wgrad((ya, ATT_W, 0), dpa, tk=ATT_W, tn=d, ts=1024, name="fox_o_wgrad"))
    dw_od = by_chip_cols(wgrad((yb, ATT_W, 0), dpb, tk=ATT_W, tn=d, ts=1024, name="dil_o_wgrad"))
    sums_mix = to_chip_sums([dw_of, dw_od, dw_out.reshape(4, d // 4, d)], ("w_o_fox", "w_o_dil", "w_out"), "mix")
    dd_a = head_rowsum(dya, ya, name="fox_delta")
    dq_aug, dk_aug, dv_a, *got_ffn = fox_bwd(q_aug, k_aug, z, dya, lse_a, dd_a, exchange=[p[1] for p in sums_ffn])
    dz_a, dfa, gg_bf = fox_post(dq_aug, dk_aug, dv_a, fa, bfo)
    dq_b, dk_b, dv_b, *got_mix = dil_bwd_all(qkv_b, dyb, lse_b, yb, exchange=[p[1] for p in sums_mix])
    dz_b = rope_apply([(dq_b, 0, QK_SCALE, True), (dk_b, 0, 1.0, True), (dv_b, 0, 1.0, False)],
                      rope_tables(s, -1.0), out_dtype=BF16, name="rope_bwd")
    dh1 = mm([(dz_a, e_a, 0), (dz_b, e_a, 0), (dz_g, d, 0), (dz_g, d, 1), (dfa, LANES, 0)],
             [(wz, e_a, 0), (wz, e_a, 1), (wz, d, 3), (wz, d, 4), (wf, LANES, 0)], nt=True, out_dtype=BF16,
             tm=512, tn=512, name="in_dgrad")
    dwt_a = wgrad((dz_a, e_a, 0), h1, tk=e_a // 2, tn=d, ts=1024, name="in_wgrad_a")
    dwt_b = wgrad((dz_b, e_a, 0), h1, tk=e_a // 2, tn=d, ts=1024, name="in_wgrad_b")
    dwt_g = wgrad((dz_g, 2 * d, 0), h1, tk=d, tn=d, ts=1024, name="in_wgrad_g")
    dwt_f = wgrad((dfa, LANES, 0), h1, tk=LANES, tn=d, ts=1024, name="in_wgrad_f")
    grad_x, gg_pre_mix = rmsnorm_bwd(dh1, xs, g_pre_mix, dx1, out_dtype=F32, name="norm_bwd_pre_mix")
    dwt_full = jnp.concatenate([dwt_a, dwt_f[:nf], dwt_b, dwt_g], axis=0)
    dw_in = jnp.stack([dwt_full[j * cols_in:(j + 1) * cols_in] for j in range(4)], axis=0)

    names = ("w_in", "w_o_fox", "w_o_dil", "w_out", "w_up", "w_down")
    sums_in = to_chip_sums([dw_in], ("w_in",), "in", by_cols=True)
    got_in = grads_to_chips([sums_in[0][1]], name="grads_to_chips_in")
    sums = sums_in + sums_mix + sums_ffn
    from_chips = list(got_in) + list(got_mix) + list(got_ffn)
    halves = [final_sum(p[0], r, chip_arr, name=f"final_sum_{nm}") for p, r, nm in zip(sums, from_chips, names)]
    from_half = halves_to_full(halves, [True] + [False] * 5, name="halves_to_full")
    g_big = [lax.dynamic_update_slice_in_dim(full, mine, ci * mine.shape[k == 0], axis=int(k == 0))
             for k, (full, mine) in enumerate(zip(from_half, halves))]
    upd_big = [adamw(w[0], g, m[0], v[0], name=f"adamw_{nm}") for w, g, m, v, nm in list(zip(
        big, g_big, (m_w_in, m_w_o_fox, m_w_o_dil, m_w_out, m_w_up, m_w_down),
        (v_w_in, v_w_o_fox, v_w_o_dil, v_w_out, v_w_up, v_w_down), names))[1:]]
    to_t = lambda a: jnp.transpose(a, (2, 0, 1))
    from_t = lambda a: jnp.transpose(a, (1, 2, 0))
    g_in_t = g_big[0][:, None, :]
    upd_in = adamw(to_t(w_in), g_in_t, to_t(m_w_in), to_t(v_w_in), name="adamw_w_in", tr=cols_in // 2)

    g_cw_loc = jnp.concatenate([gc_a[0:3], gc_b[0:3]], axis=1)
    g_cb_loc = jnp.concatenate([gc_a[3:4], gc_b[3:4]], axis=1)
    small_loc = [gg_pre_mix, gg_post_mix, gg_pre_ffn, gg_post_ffn, g_cb_loc, gg_bf[:, :nf], g_cw_loc]
    red_rows = (8, 8, 8, 8, 48, 8, 136)
    red = allreduce_small(_pack_rows(small_loc, red_rows), name="allreduce_small")
    g_pm, g_qm, g_pf, g_qf, g_cb, g_bf, g_cw_full = _unpack_rows(red, [a.shape for a in small_loc], red_rows)
    cols_cw = conv_w.shape[2]
    g_cw = lax.dynamic_slice_in_dim(g_cw_full, chip * cols_cw, cols_cw, axis=1)
    small_w = (g_pre_mix, g_post_mix, g_pre_ffn, g_post_ffn, conv_b, b_forget, conv_w[0])
    small_m = (m_g_pre_mix, m_g_post_mix, m_g_pre_ffn, m_g_post_ffn, m_conv_b, m_b_forget, m_conv_w[0])
    small_v = (v_g_pre_mix, v_g_post_mix, v_g_pre_ffn, v_g_post_ffn, v_conv_b, v_b_forget, v_conv_w[0])
    small_g = (g_pm, g_qm, g_pf, g_qf, g_cb, g_bf, g_cw)
    ad_rows = (8, 8, 8, 8, 48, 8, 40)
    packed = [_pack_rows(t, ad_rows) for t in (small_w, small_g, small_m, small_v)]
    upd_small = [_unpack_rows(o, [a.shape for a in small_w], ad_rows) for o in adamw(*packed, name="adamw_small")]

    order = ("g_pre_mix", "w_in", "b_forget", "w_o_fox", "w_o_dil", "w_out", "g_post_mix", "g_pre_ffn", "w_up", "conv_w",
             "conv_b", "w_down", "g_post_ffn")
    small_names = ("g_pre_mix", "g_post_mix", "g_pre_ffn", "g_post_ffn", "conv_b", "b_forget", "conv_w")
    grads, deltas, new_ms, new_vs = {}, {}, {}, {}
    grads["w_in"] = from_t(g_in_t)
    deltas["w_in"], new_ms["w_in"], new_vs["w_in"] = (from_t(a) for a in upd_in)
    for k, nm in enumerate(names[1:]):
        grads[nm] = g_big[k + 1][None]
        deltas[nm], new_ms[nm], new_vs[nm] = (a[None] for a in upd_big[k])
    for k, nm in enumerate(small_names):
        lead = (lambda a: a[None]) if nm == "conv_w" else (lambda a: a)
        grads[nm] = lead(small_g[k])
        deltas[nm], new_ms[nm], new_vs[nm] = (lead(upd_small[j][k]) for j in range(3))
    return (loss, grad_x[None], *[grads[nm] for nm in order], *[deltas[nm] for nm in order],
            *[new_ms[nm] for nm in order], *[new_vs[nm] for nm in order])
```

```python
import functools
import math

import numpy as np
import jax
import jax.numpy as jnp
from jax import lax
from jax.experimental import pallas as pl
from jax.experimental.pallas import tpu as pltpu

F32 = jnp.float32
BF16 = jnp.bfloat16
SDS = jax.ShapeDtypeStruct
MESH = pl.DeviceIdType.MESH

HEAD_DIM = 64
N_HEADS = 8
LANES = 128
ATT_W = N_HEADS * HEAD_DIM
DIL_PATTERNS = ((128, 1), (512, 4), (2048, 16))
DIL_BLK = 128
ROPE_DIM = HEAD_DIM // 4
ROPE_THETA = 500000.0
RMS_EPS = 1e-6
NEG = -1e30
QK_SCALE = 1.0 / math.sqrt(HEAD_DIM)
ADAM_LR, ADAM_B1, ADAM_B2, ADAM_EPS, ADAM_WD, ADAM_STEP = 0.001, 0.9, 0.999, 1e-08, 0.01, 10
VMEM_LIMIT = 56 * 1024 * 1024

Z_QA, Z_KA, Z_VA, Z_QB, Z_KB, Z_VB = 0, 1, 2, 3, 4, 5
Z_W = 5120


def _cp(sem):
    return pltpu.CompilerParams(dimension_semantics=sem, vmem_limit_bytes=VMEM_LIMIT)


def _nt(a, b):
    return lax.dot_general(a, b, (((1,), (1,)), ((), ())), preferred_element_type=F32)


def _tn(a, b):
    return lax.dot_general(a, b, (((0,), (0,)), ((), ())), preferred_element_type=F32)


def _nn(a, b):
    return jnp.dot(a, b, preferred_element_type=F32)


def _lane(shape):
    return lax.broadcasted_iota(jnp.int32, shape, 1)


def _row(shape):
    return lax.broadcasted_iota(jnp.int32, shape, 0)


def rmsnorm_fwd(x, g, *, tm=512):
    s, d = x.shape

    def body(x_ref, g_ref, h_ref):
        xv = x_ref[...]
        inv = lax.rsqrt(jnp.mean(xv * xv, axis=-1, keepdims=True) + RMS_EPS)
        h_ref[...] = (xv * inv * g_ref[...]).astype(h_ref.dtype)

    return pl.pallas_call(
        body, grid=(s // tm,),
        in_specs=[pl.BlockSpec((tm, d), lambda i: (i, 0)), pl.BlockSpec((1, d), lambda i: (0, 0))],
        out_specs=pl.BlockSpec((tm, d), lambda i: (i, 0)),
        out_shape=SDS((s, d), BF16), name="rmsnorm_fwd", compiler_params=_cp(("parallel",)))(x, g)


def rmsnorm_bwd(dh, x, g, res, *, out_dtype, tm=256, name):
    s, d = x.shape
    n = s // tm
    has_res = res is not None

    def body(*refs):
        if has_res:
            dh_ref, x_ref, g_ref, res_ref, dx_ref, dg_ref, acc = refs
        else:
            dh_ref, x_ref, g_ref, dx_ref, dg_ref, acc = refs
        i = pl.program_id(0)

        @pl.when(i == 0)
        def _():
            acc[...] = jnp.zeros_like(acc)

        xv = x_ref[...]
        inv = lax.rsqrt(jnp.mean(xv * xv, axis=-1, keepdims=True) + RMS_EPS)
        xh = xv * inv
        dhv = dh_ref[...].astype(F32)
        dxh = dhv * g_ref[...]
        dot = jnp.mean(dxh * xh, axis=-1, keepdims=True)
        dx = inv * (dxh - xh * dot)
        if has_res:
            dx = dx + res_ref[...]
        dx_ref[...] = dx.astype(dx_ref.dtype)
        acc[...] += jnp.sum((dhv * xh).reshape(tm // 8, 8, d), axis=0)

        @pl.when(i == n - 1)
        def _():
            dg_ref[...] = jnp.sum(acc[...], axis=0, keepdims=True)

    row = pl.BlockSpec((tm, d), lambda i: (i, 0))
    in_specs = [row, row, pl.BlockSpec((1, d), lambda i: (0, 0))] + ([row] if has_res else [])
    args = [dh, x, g] + ([res] if has_res else [])
    return pl.pallas_call(
        body, grid=(n,), in_specs=in_specs,
        out_specs=[row, pl.BlockSpec((1, d), lambda i: (0, 0))],
        out_shape=[SDS((s, d), out_dtype), SDS((1, d), F32)],
        scratch_shapes=[pltpu.VMEM((8, d), F32)],
        name=name, compiler_params=_cp(("arbitrary",)))(*args)


def mm(a_views, b_views, *, nt, out_dtype, tm, tn, name):
    n_p = len(a_views)
    m = a_views[0][0].shape[0]
    n = b_views[0][0].shape[0] if nt else b_views[0][0].shape[1]

    def body(*refs):
        o_ref = refs[-1]
        acc = None
        for p in range(n_p):
            av = refs[p][...].astype(BF16)
            bv = refs[n_p + p][...].astype(BF16)
            dv = _nt(av, bv) if nt else _nn(av, bv)
            acc = dv if acc is None else acc + dv
        o_ref[...] = acc.astype(o_ref.dtype)

    in_specs = []
    for arr, w, blk in a_views:
        in_specs.append(pl.BlockSpec((tm, w), functools.partial(lambda i, j, blk: (i, blk), blk=blk)))
    for arr, w, blk in b_views:
        if nt:
            in_specs.append(pl.BlockSpec((tn, w), functools.partial(lambda i, j, blk: (j, blk), blk=blk)))
        else:
            in_specs.append(pl.BlockSpec((w, tn), lambda i, j: (0, j)))
    return pl.pallas_call(
        body, grid=(m // tm, n // tn), in_specs=in_specs,
        out_specs=pl.BlockSpec((tm, tn), lambda i, j: (i, j)),
        out_shape=SDS((m, n), out_dtype), name=name,
        compiler_params=_cp(("parallel", "parallel")))(*[a[0] for a in a_views], *[b[0] for b in b_views])


def wgrad(a_view, g, *, tk, tn, ts, name, chip_major=False):
    arr, ka, blk = a_view
    s, n = g.shape
    ns = s // ts

    def body(a_ref, g_ref, o_ref):
        @pl.when(pl.program_id(2) == 0)
        def _():
            o_ref[...] = jnp.zeros_like(o_ref)

        o_ref[...] += _tn(a_ref[...].astype(BF16), g_ref[...].astype(BF16))

    if chip_major:
        out_spec = pl.BlockSpec((None, tk, tn), lambda i, j, k: (j, i, 0))
        out_shape = SDS((n // tn, ka, tn), F32)
    else:
        out_spec = pl.BlockSpec((tk, tn), lambda i, j, k: (i, j))
        out_shape = SDS((ka, n), F32)
    return pl.pallas_call(
        body, grid=(ka // tk, n // tn, ns),
        in_specs=[pl.BlockSpec((ts, tk), lambda i, j, k: (k, blk * (ka // tk) + i)),
                  pl.BlockSpec((ts, tn), lambda i, j, k: (k, j))],
        out_specs=out_spec, out_shape=out_shape, name=name,
        compiler_params=_cp(("parallel", "parallel", "arbitrary")))(arr, g)


def mm_rms_res(a, w, g, xres, target=None, *, tm=256, name):
    s, k = a.shape
    d = w.shape[1]
    n = s // tm
    with_loss = target is not None

    def body(*refs):
        if with_loss:
            a_ref, w_ref, g_ref, x_ref, t_ref, y_ref, o_ref, l_ref = refs
        else:
            a_ref, w_ref, g_ref, x_ref, y_ref, o_ref = refs
        y = _nn(a_ref[...], w_ref[...])
        inv = lax.rsqrt(jnp.mean(y * y, axis=-1, keepdims=True) + RMS_EPS)
        xn = x_ref[...] + y * inv * g_ref[...]
        y_ref[...] = y
        if with_loss:
            err = xn - t_ref[...]
            o_ref[...] = err * (1.0 / d)

            @pl.when(pl.program_id(0) == 0)
            def _():
                l_ref[...] = jnp.zeros_like(l_ref)

            l_ref[...] += jnp.sum(jnp.sum(err * err, axis=1, keepdims=True), axis=0, keepdims=True)
        else:
            o_ref[...] = xn

    row = pl.BlockSpec((tm, d), lambda i: (i, 0))
    in_specs = [pl.BlockSpec((tm, k), lambda i: (i, 0)), pl.BlockSpec((k, d), lambda i: (0, 0)),
                pl.BlockSpec((1, d), lambda i: (0, 0)), row]
    out_specs = [row, row]
    out_shape = [SDS((s, d), F32), SDS((s, d), F32)]
    args = [a, w, g, xres]
    if with_loss:
        in_specs.append(row)
        out_specs.append(pl.BlockSpec((1, 1), lambda i: (0, 0)))
        out_shape.append(SDS((1, 1), F32))
        args.append(target)
    return pl.pallas_call(
        body, grid=(n,), in_specs=in_specs, out_specs=out_specs, out_shape=out_shape, name=name,
        compiler_params=_cp(("arbitrary",)))(*args)


def _split3(v):
    hi = v.astype(BF16).astype(F32)
    r = v - hi
    mid = r.astype(BF16).astype(F32)
    lo = (r - mid).astype(BF16).astype(F32)
    return hi, mid, lo


def _tri(n, upper):
    r = np.arange(n)
    m = (r[:, None] <= r[None, :]) if upper else (r[:, None] >= r[None, :])
    return jnp.asarray(m.astype(np.float32))


def fox_prep(z, fa, bfo, *, tb=512):
    s = z.shape[0]
    n = s // tb

    def body(q_ref, k_ref, fa_ref, b_ref, tri_ref, qa_ref, ka_ref, carry):
        @pl.when(pl.program_id(0) == 0)
        def _():
            carry[...] = jnp.zeros_like(carry)

        xv = fa_ref[...] + b_ref[...]
        logf = jnp.minimum(xv, 0.0) - jnp.log(1.0 + jnp.exp(-jnp.abs(xv)))
        csum = jnp.dot(tri_ref[...], logf, preferred_element_type=F32, precision=lax.Precision.HIGHEST) + carry[0:1, :]
        carry[0:1, :] = csum[tb - 1:tb, :]
        lane = _lane((tb, LANES))
        for h in range(N_HEADS):
            hi, mid, lo = _split3(csum[:, h:h + 1])
            pair = (h // 2) * LANES
            qv = q_ref[:, pair:pair + LANES].astype(F32)
            kv = k_ref[:, pair:pair + LANES].astype(F32)
            if h % 2:
                qv = pltpu.roll(qv, 64, axis=1)
                kv = pltpu.roll(kv, 64, axis=1)
            one = jnp.where((lane >= 67) & (lane < 70), 1.0, 0.0)
            q_x = jnp.where(lane == 64, hi, jnp.where(lane == 65, mid, jnp.where(lane == 66, lo, one)))
            one = jnp.where((lane >= 64) & (lane < 67), 1.0, 0.0)
            k_x = jnp.where(lane == 67, -hi, jnp.where(lane == 68, -mid, jnp.where(lane == 69, -lo, one)))
            qa_ref[:, h * LANES:(h + 1) * LANES] = jnp.where(lane < 64, qv * QK_SCALE, q_x).astype(BF16)
            ka_ref[:, h * LANES:(h + 1) * LANES] = jnp.where(lane < 64, kv, k_x).astype(BF16)

    return pl.pallas_call(
        body, grid=(n,),
        in_specs=[pl.BlockSpec((tb, ATT_W), lambda i: (i, Z_QA)), pl.BlockSpec((tb, ATT_W), lambda i: (i, Z_KA)),
                  pl.BlockSpec((tb, LANES), lambda i: (i, 0)), pl.BlockSpec((1, LANES), lambda i: (0, 0)),
                  pl.BlockSpec((tb, tb), lambda i: (0, 0))],
        out_specs=[pl.BlockSpec((tb, N_HEADS * LANES), lambda i: (i, 0))] * 2,
        out_shape=[SDS((s, N_HEADS * LANES), BF16)] * 2,
        scratch_shapes=[pltpu.VMEM((8, LANES), F32)],
        name="fox_prep", compiler_params=_cp(("arbitrary",)))(z, z, fa, bfo, _tri(tb, False))


def _causal_pairs(n, k_major):
    if k_major:
        pairs = [(qi, kj) for kj in range(n) for qi in range(kj, n)]
    else:
        pairs = [(qi, kj) for qi in range(n) for kj in range(qi + 1)]
    return (jnp.asarray([p[0] for p in pairs], jnp.int32), jnp.asarray([p[1] for p in pairs], jnp.int32), len(pairs))


def fox_fwd(q_aug, k_aug, z, gather=(), *, t=512):
    s = z.shape[0]
    qi_arr, kj_arr, n_pairs = _causal_pairs(s // t, False)
    ng = len(gather)

    def body(qi_ref, kj_ref, q_ref, k_ref, v_ref, *rest):
        g_ins, (o_ref, lse_ref), g_outs = rest[:ng], rest[ng:ng + 2], rest[ng + 2:2 * ng + 2]
        m_scr, l_scr, acc_scr = rest[2 * ng + 2:2 * ng + 5]
        comm = (g_ins, g_outs) + tuple(rest[2 * ng + 5:]) + ([True] * ng,)
        step = pl.program_id(1)
        qi = qi_ref[step]
        kj = kj_ref[step]
        if ng:
            @pl.when((pl.program_id(0) == 0) & (step == 0))
            def _():
                _allgather_start(*comm)

        @pl.when(kj == 0)
        def _():
            m_scr[...] = jnp.full_like(m_scr, NEG)
            l_scr[...] = jnp.zeros_like(l_scr)
            acc_scr[...] = jnp.zeros_like(acc_scr)

        def update(masked):
            for i in range(2):
                sc = _nt(q_ref[:, i * LANES:(i + 1) * LANES], k_ref[:, i * LANES:(i + 1) * LANES])
                if masked:
                    sc = jnp.where(_row((t, t)) >= _lane((t, t)), sc, NEG)
                m_prev = m_scr[i]
                m_new = jnp.maximum(m_prev, jnp.max(sc, axis=-1, keepdims=True))
                alpha = jnp.exp(m_prev - m_new)
                p = jnp.exp(sc - jnp.tile(m_new, (1, t // LANES)))
                l_scr[i] = alpha * l_scr[i] + jnp.sum(p, axis=-1, keepdims=True)
                acc_scr[i] = alpha * acc_scr[i] + _nn(p.astype(BF16), v_ref[...])
                m_scr[i] = m_new

        @pl.when(kj < qi)
        def _():
            update(False)

        @pl.when(kj == qi)
        def _():
            update(True)
            lane = _lane((t, LANES))
            o_ref[...] = jnp.where(lane < 64, acc_scr[0] / l_scr[0], acc_scr[1] / l_scr[1]).astype(o_ref.dtype)
            lse_ref[...] = jnp.where(lane < 64, m_scr[0] + jnp.log(l_scr[0]), m_scr[1] + jnp.log(l_scr[1]))

        if ng:
            @pl.when((pl.program_id(0) == 3) & (step == n_pairs - 1))
            def _():
                _allgather_finish(*comm)

    grid_spec = pltpu.PrefetchScalarGridSpec(
        num_scalar_prefetch=2, grid=(4, n_pairs),
        in_specs=[pl.BlockSpec((t, 2 * LANES), lambda hp, st, qi, kj: (qi[st], hp)),
                  pl.BlockSpec((t, 2 * LANES), lambda hp, st, qi, kj: (kj[st], hp)),
                  pl.BlockSpec((t, LANES), lambda hp, st, qi, kj: (kj[st], 4 * Z_VA + hp))] + [ANY] * ng,
        out_specs=[pl.BlockSpec((t, LANES), lambda hp, st, qi, kj: (qi[st], hp))] * 2 + [ANY] * ng,
        scratch_shapes=[pltpu.VMEM((2, t, LANES), F32)] * 3 + (_allgather_sems(ng) if ng else []))
    return pl.pallas_call(
        body, grid_spec=grid_spec, out_shape=[SDS((s, ATT_W), BF16), SDS((s, ATT_W), F32)] + _allgather_shapes(gather),
        name="fox_fwd", compiler_params=_cp(("arbitrary", "arbitrary")))(qi_arr, kj_arr, q_aug, k_aug, z, *gather)


def fox_bwd(q_aug, k_aug, z, dy, lse, dd, exchange=(), *, t=512):
    s = z.shape[0]
    qi_arr, kj_arr, n_pairs = _causal_pairs(s // t, True)
    ne = len(exchange)

    def body(qi_ref, kj_ref, q_ref, k_ref, v_ref, do_ref, lse_ref, dd_ref, *rest):
        e_ins, (dq_ref, dk_ref, dv_ref), e_outs = rest[:ne], rest[ne:ne + 3], rest[ne + 3:2 * ne + 3]
        comm = (e_ins, e_outs) + tuple(rest[2 * ne + 3:])
        step = pl.program_id(1)
        qi = qi_ref[step]
        kj = kj_ref[step]
        if ne:
            @pl.when((pl.program_id(0) == 0) & (step == 0))
            def _():
                _to_chips_start(*comm)

        @pl.when(step == 0)
        def _():
            dq_ref[...] = jnp.zeros_like(dq_ref)

        @pl.when(qi == kj)
        def _():
            dk_ref[...] = jnp.zeros_like(dk_ref)
            dv_ref[...] = jnp.zeros_like(dv_ref)

        def update(masked):
            lane = _lane((t, LANES))
            rows = pl.ds(pl.multiple_of(qi * t, t), t)
            dov = do_ref[...]
            dv_new = None
            for i in range(2):
                head = (lane < 64) if i == 0 else (lane >= 64)
                qv = q_ref[:, i * LANES:(i + 1) * LANES]
                kv = k_ref[:, i * LANES:(i + 1) * LANES]
                sc = _nt(qv, kv)
                if masked:
                    sc = jnp.where(_row((t, t)) >= _lane((t, t)), sc, NEG)
                p = jnp.exp(sc - lse_ref[:, i * 64:i * 64 + 1])
                dp = _nt(jnp.where(head, dov, jnp.zeros_like(dov)), v_ref[...])
                ds = (p * (dp - dd_ref[:, i * 64:i * 64 + 1])).astype(BF16)
                dq_ref[rows, i * LANES:(i + 1) * LANES] += _nn(ds, kv)
                dk_ref[:, i * LANES:(i + 1) * LANES] += _tn(ds, qv)
                dvi = _tn(p.astype(BF16), dov)
                dv_new = dvi if dv_new is None else jnp.where(head, dvi, dv_new)
            dv_ref[...] += dv_new

        @pl.when(kj < qi)
        def _():
            update(False)

        @pl.when(kj == qi)
        def _():
            update(True)

        if ne:
            @pl.when((pl.program_id(0) == 3) & (step == n_pairs - 1))
            def _():
                _to_chips_finish(*comm)

    grid_spec = pltpu.PrefetchScalarGridSpec(
        num_scalar_prefetch=2, grid=(4, n_pairs),
        in_specs=[pl.BlockSpec((t, 2 * LANES), lambda hp, st, qi, kj: (qi[st], hp)),
                  pl.BlockSpec((t, 2 * LANES), lambda hp, st, qi, kj: (kj[st], hp)),
                  pl.BlockSpec((t, LANES), lambda hp, st, qi, kj: (kj[st], 4 * Z_VA + hp)),
                  pl.BlockSpec((t, LANES), lambda hp, st, qi, kj: (qi[st], hp)),
                  pl.BlockSpec((t, LANES), lambda hp, st, qi, kj: (qi[st], hp)),
                  pl.BlockSpec((t, LANES), lambda hp, st, qi, kj: (qi[st], hp))] + [ANY] * ne,
        out_specs=[pl.BlockSpec((s, 2 * LANES), lambda hp, st, qi, kj: (0, hp)),
                   pl.BlockSpec((t, 2 * LANES), lambda hp, st, qi, kj: (kj[st], hp)),
                   pl.BlockSpec((t, LANES), lambda hp, st, qi, kj: (kj[st], hp))] + [ANY] * ne,
        scratch_shapes=_to_chips_sems(ne) if ne else [])
    return pl.pallas_call(
        body, grid_spec=grid_spec,
        out_shape=[SDS((s, N_HEADS * LANES), F32), SDS((s, N_HEADS * LANES), F32), SDS((s, ATT_W), F32)]
        + _to_chips_shapes(exchange),
        name="fox_bwd", compiler_params=_cp(("arbitrary", "arbitrary")))(qi_arr, kj_arr, q_aug, k_aug, z, dy, lse, dd, *exchange)


def head_rowsum(a, b, *, tm=512, name):
    s = a.shape[0]

    def body(a_ref, b_ref, o_ref):
        prod = a_ref[...].astype(F32) * b_ref[...].astype(F32)
        lane = _lane((tm, LANES))
        lo = jnp.sum(jnp.where(lane < 64, prod, 0.0), axis=-1, keepdims=True)
        hi = jnp.sum(jnp.where(lane >= 64, prod, 0.0), axis=-1, keepdims=True)
        o_ref[...] = jnp.where(lane < 64, lo, hi)

    blk = pl.BlockSpec((tm, LANES), lambda i, j: (i, j))
    return pl.pallas_call(body, grid=(s // tm, 4), in_specs=[blk, blk], out_specs=blk, out_shape=SDS((s, ATT_W), F32),
                          name=name, compiler_params=_cp(("parallel", "parallel")))(a, b)


def fox_post(dq_aug, dk_aug, dv, fa, bfo, *, tb=512):
    s = dv.shape[0]
    n = s // tb

    def body(dq_ref, dk_ref, dv_ref, fa_ref, b_ref, tri_ref, dz_ref, dfa_ref, gb_ref, carry, acc):
        i = pl.program_id(0)

        @pl.when(i == 0)
        def _():
            carry[...] = jnp.zeros_like(carry)
            acc[...] = jnp.zeros_like(acc)

        lane = _lane((tb, LANES))
        d_f = jnp.zeros((tb, LANES), F32)
        for h in range(N_HEADS):
            col = dq_ref[:, h * LANES + 64:h * LANES + 65] - dk_ref[:, h * LANES + 67:h * LANES + 68]
            d_f = jnp.where(lane == h, col, d_f)
        suffix = jnp.dot(tri_ref[...], d_f, preferred_element_type=F32, precision=lax.Precision.HIGHEST) + carry[0:1, :]
        carry[0:1, :] = suffix[0:1, :]
        xv = fa_ref[...] + b_ref[...]
        dx = suffix * (1.0 / (1.0 + jnp.exp(xv)))
        dfa_ref[...] = dx.astype(dfa_ref.dtype)
        acc[...] += jnp.sum(dx.reshape(tb // 8, 8, LANES), axis=0)
        for hp in range(4):
            for src, off, scale in ((dq_ref, 0, QK_SCALE), (dk_ref, ATT_W, 1.0)):
                even = src[:, (2 * hp) * LANES:(2 * hp + 1) * LANES]
                odd = pltpu.roll(src[:, (2 * hp + 1) * LANES:(2 * hp + 2) * LANES], 64, axis=1)
                dz_ref[:, off + hp * LANES:off + (hp + 1) * LANES] = (jnp.where(lane < 64, even, odd) * scale).astype(BF16)
        dz_ref[:, 2 * ATT_W:3 * ATT_W] = dv_ref[...].astype(BF16)

        @pl.when(i == n - 1)
        def _():
            gb_ref[...] = jnp.sum(acc[...], axis=0, keepdims=True)

    rev = lambda i: (n - 1 - i, 0)
    return pl.pallas_call(
        body, grid=(n,),
        in_specs=[pl.BlockSpec((tb, N_HEADS * LANES), rev), pl.BlockSpec((tb, N_HEADS * LANES), rev),
                  pl.BlockSpec((tb, ATT_W), rev), pl.BlockSpec((tb, LANES), rev),
                  pl.BlockSpec((1, LANES), lambda i: (0, 0)), pl.BlockSpec((tb, tb), lambda i: (0, 0))],
        out_specs=[pl.BlockSpec((tb, 3 * ATT_W), rev), pl.BlockSpec((tb, LANES), rev),
                   pl.BlockSpec((1, LANES), lambda i: (0, 0))],
        out_shape=[SDS((s, 3 * ATT_W), BF16), SDS((s, LANES), BF16), SDS((1, LANES), F32)],
        scratch_shapes=[pltpu.VMEM((8, LANES), F32), pltpu.VMEM((8, LANES), F32)],
        name="fox_post", compiler_params=_cp(("arbitrary",)))(dq_aug, dk_aug, dv, fa, bfo, _tri(tb, True))


def rope_tables(s, sign):
    half = ROPE_DIM // 2
    inv_freq = ROPE_THETA ** (-jnp.arange(half, dtype=F32) * 2.0 / ROPE_DIM)
    ang = jnp.arange(s, dtype=F32)[:, None] * inv_freq[None, :]
    l64 = np.arange(LANES) % HEAD_DIM
    cos = jnp.cos(ang)[:, l64 % half]
    sin = jnp.sin(ang)[:, l64 % half] * sign
    first = jnp.asarray(l64 < half)[None, :]
    second = jnp.asarray((l64 >= half) & (l64 < ROPE_DIM))[None, :]
    return (jnp.where(first | second, cos, 1.0), jnp.where(first, -sin, 0.0), jnp.where(second, sin, 0.0))


def rope_apply(items, tabs, *, out_dtype, tm=512, name):
    s = items[0][0].shape[0]
    n_i = len(items)

    def body(*refs):
        c_ref, sn_ref, sp_ref = refs[n_i:n_i + 3]
        o_ref = refs[-1]
        for j, (_, _, scale, rotate) in enumerate(items):
            for b in range(4):
                xv = refs[j][:, b * LANES:(b + 1) * LANES].astype(F32)
                if rotate:
                    xv = xv * c_ref[...] + pltpu.roll(xv, LANES - 8, axis=1) * sn_ref[...] + pltpu.roll(xv, 8, axis=1) * sp_ref[...]
                o_ref[:, j * ATT_W + b * LANES:j * ATT_W + (b + 1) * LANES] = (xv * scale).astype(o_ref.dtype)

    in_specs = [pl.BlockSpec((tm, ATT_W), functools.partial(lambda i, blk: (i, blk), blk=it[1])) for it in items]
    in_specs += [pl.BlockSpec((tm, LANES), lambda i: (i, 0))] * 3
    return pl.pallas_call(
        body, grid=(s // tm,), in_specs=in_specs, out_specs=pl.BlockSpec((tm, n_i * ATT_W), lambda i: (i, 0)),
        out_shape=SDS((s, n_i * ATT_W), out_dtype), name=name, compiler_params=_cp(("parallel",)))(*[it[0] for it in items], *tabs)


def _dil_views(qk, z, r):
    s = z.shape[0]
    return qk.reshape(s // r, r * 2 * ATT_W), z.reshape(s // r, r * Z_W)


def _dil_cols(r):
    q_col = lambda rho, hp: rho * 8 + hp
    k_col = lambda rho, hp: rho * 8 + 4 + hp
    v_col = lambda rho, hp: rho * (Z_W // LANES) + 4 * Z_VB + hp
    return q_col, k_col, v_col


def _dil_scores(qv, kp, kc, head, has_prev):
    b = DIL_BLK
    qm = jnp.where(head, qv, jnp.zeros_like(qv))
    row, col = _row((b, b)), _lane((b, b))
    sp = jnp.where((col >= row) & has_prev, _nt(qm, kp), NEG)
    sc = jnp.where(col <= row, _nt(qm, kc), NEG)
    return sp, sc


def dil_fwd(qk, z, prev, *, r):
    s = z.shape[0]
    b = DIL_BLK
    l_sub = s // r
    nb = l_sub // b
    qk_v, z_v = _dil_views(qk, z, r)
    q_col, k_col, v_col = _dil_cols(r)
    merge = prev is not None

    def body(*refs):
        if merge:
            q_ref, kp_ref, kc_ref, vp_ref, vc_ref, op_ref, lp_ref, o_ref, l_ref = refs
        else:
            q_ref, kp_ref, kc_ref, vp_ref, vc_ref, o_ref, l_ref = refs
        has_prev = pl.program_id(2) > 0
        lane = _lane((b, LANES))
        res = []
        for i in range(2):
            head = (lane < 64) if i == 0 else (lane >= 64)
            sp, sc = _dil_scores(q_ref[...], kp_ref[...], kc_ref[...], head, has_prev)
            m = jnp.maximum(jnp.max(sp, axis=-1, keepdims=True), jnp.max(sc, axis=-1, keepdims=True))
            pp = jnp.exp(sp - m)
            pc = jnp.exp(sc - m)
            den = jnp.sum(pp, axis=-1, keepdims=True) + jnp.sum(pc, axis=-1, keepdims=True)
            ov = (_nn(pp.astype(BF16), vp_ref[...]) + _nn(pc.astype(BF16), vc_ref[...])) / den
            res.append((ov, m + jnp.log(den)))
        ov = jnp.where(lane < 64, res[0][0], res[1][0])
        lse = jnp.where(lane < 64, res[0][1], res[1][1])
        if merge:
            lp = lp_ref[...]
            m2 = jnp.maximum(lp, lse)
            wp = jnp.exp(lp - m2)
            wn = jnp.exp(lse - m2)
            ov = (wp * op_ref[...] + wn * ov) / (wp + wn)
            lse = m2 + jnp.log(wp + wn)
        o_ref[...] = ov
        l_ref[...] = lse

    blk = lambda f: pl.BlockSpec((b, LANES), f)
    in_specs = [blk(lambda rho, hp, n: (n, q_col(rho, hp))), blk(lambda rho, hp, n: (jnp.maximum(n - 1, 0), k_col(rho, hp))),
                blk(lambda rho, hp, n: (n, k_col(rho, hp))), blk(lambda rho, hp, n: (jnp.maximum(n - 1, 0), v_col(rho, hp))),
                blk(lambda rho, hp, n: (n, v_col(rho, hp)))]
    args = [qk_v, qk_v, qk_v, z_v, z_v]
    nat = blk(lambda rho, hp, n: (n, rho * 4 + hp))
    if merge:
        in_specs += [nat, nat]
        args += [prev[0].reshape(l_sub, r * ATT_W), prev[1].reshape(l_sub, r * ATT_W)]
    o, lse = pl.pallas_call(
        body, grid=(r, 4, nb), in_specs=in_specs, out_specs=[nat, nat],
        out_shape=[SDS((l_sub, r * ATT_W), F32)] * 2, name=f"dil_fwd_r{r}",
        compiler_params=_cp(("parallel", "parallel", "arbitrary")))(*args)
    return o.reshape(s, ATT_W), lse.reshape(s, ATT_W)


def dil_bwd_dq(qk, z, dy, lse, dd, acc, *, r):
    s = z.shape[0]
    b = DIL_BLK
    l_sub = s // r
    nb = l_sub // b
    qk_v, z_v = _dil_views(qk, z, r)
    q_col, k_col, v_col = _dil_cols(r)
    add = acc is not None

    def body(*refs):
        q_ref, kp_ref, kc_ref, vp_ref, vc_ref, do_ref, l_ref, dd_ref = refs[:8]
        dq_ref = refs[-1]
        has_prev = pl.program_id(2) > 0
        lane = _lane((b, LANES))
        dov = do_ref[...]
        parts = []
        for i in range(2):
            head = (lane < 64) if i == 0 else (lane >= 64)
            sp, sc = _dil_scores(q_ref[...], kp_ref[...], kc_ref[...], head, has_prev)
            lse_i = l_ref[:, i * 64:i * 64 + 1]
            dd_i = dd_ref[:, i * 64:i * 64 + 1]
            dom = jnp.where(head, dov, jnp.zeros_like(dov))
            dsp = (jnp.exp(sp - lse_i) * (_nt(dom, vp_ref[...]) - dd_i)).astype(BF16)
            dsc = (jnp.exp(sc - lse_i) * (_nt(dom, vc_ref[...]) - dd_i)).astype(BF16)
            parts.append(_nn(dsp, kp_ref[...]) + _nn(dsc, kc_ref[...]))
        dq = jnp.where(lane < 64, parts[0], parts[1])
        if add:
            dq = dq + refs[8][...]
        dq_ref[...] = dq

    blk = lambda f: pl.BlockSpec((b, LANES), f)
    nat = blk(lambda rho, hp, n: (n, rho * 4 + hp))
    in_specs = [blk(lambda rho, hp, n: (n, q_col(rho, hp))), blk(lambda rho, hp, n: (jnp.maximum(n - 1, 0), k_col(rho, hp))),
                blk(lambda rho, hp, n: (n, k_col(rho, hp))), blk(lambda rho, hp, n: (jnp.maximum(n - 1, 0), v_col(rho, hp))),
                blk(lambda rho, hp, n: (n, v_col(rho, hp))), nat, nat, nat]
    nview = lambda a: a.reshape(l_sub, r * ATT_W)
    args = [qk_v, qk_v, qk_v, z_v, z_v, nview(dy), nview(lse), nview(dd)]
    if add:
        in_specs.append(nat)
        args.append(nview(acc))
    dq = pl.pallas_call(
        body, grid=(r, 4, nb), in_specs=in_specs, out_specs=nat, out_shape=SDS((l_sub, r * ATT_W), F32),
        name=f"dil_bwd_dq_r{r}", compiler_params=_cp(("parallel", "parallel", "arbitrary")))(*args)
    return dq.reshape(s, ATT_W)


def dil_bwd_dkv(qk, z, dy, lse, dd, acc, *, r):
    s = z.shape[0]
    b = DIL_BLK
    l_sub = s // r
    nb = l_sub // b
    qk_v, z_v = _dil_views(qk, z, r)
    q_col, k_col, v_col = _dil_cols(r)
    add = acc is not None

    def body(*refs):
        k_ref, v_ref, qc_ref, qn_ref, doc_ref, don_ref, lc_ref, ln_ref, ddc_ref, ddn_ref = refs[:10]
        dk_ref, dv_ref = refs[-2:]
        has_next = pl.program_id(2) < nb - 1
        lane = _lane((b, LANES))
        row, col = _row((b, b)), _lane((b, b))
        kv = k_ref[...]
        vv = v_ref[...]
        dk_parts, dv_parts = [], []
        for i in range(2):
            head = (lane < 64) if i == 0 else (lane >= 64)
            dk_i = jnp.zeros((b, LANES), F32)
            dv_i = jnp.zeros((b, LANES), F32)
            for q_ref, do_ref, l_ref, d_ref, mask in ((qc_ref, doc_ref, lc_ref, ddc_ref, col <= row),
                                                      (qn_ref, don_ref, ln_ref, ddn_ref, (col >= row) & has_next)):
                qv = q_ref[...]
                dov = do_ref[...]
                sc = jnp.where(mask, _nt(jnp.where(head, qv, jnp.zeros_like(qv)), kv), NEG)
                p = jnp.exp(sc - l_ref[:, i * 64:i * 64 + 1])
                dp = _nt(jnp.where(head, dov, jnp.zeros_like(dov)), vv)
                ds = (p * (dp - d_ref[:, i * 64:i * 64 + 1])).astype(BF16)
                dv_i = dv_i + _tn(p.astype(BF16), dov)
                dk_i = dk_i + _tn(ds, qv)
            dk_parts.append(dk_i)
            dv_parts.append(dv_i)
        dk = jnp.where(lane < 64, dk_parts[0], dk_parts[1])
        dv = jnp.where(lane < 64, dv_parts[0], dv_parts[1])
        if add:
            dk = dk + refs[10][...]
            dv = dv + refs[11][...]
        dk_ref[...] = dk
        dv_ref[...] = dv

    blk = lambda f: pl.BlockSpec((b, LANES), f)
    nat = blk(lambda rho, hp, n: (n, rho * 4 + hp))
    nxt = blk(lambda rho, hp, n: (jnp.minimum(n + 1, nb - 1), rho * 4 + hp))
    in_specs = [blk(lambda rho, hp, n: (n, k_col(rho, hp))), blk(lambda rho, hp, n: (n, v_col(rho, hp))),
                blk(lambda rho, hp, n: (n, q_col(rho, hp))), blk(lambda rho, hp, n: (jnp.minimum(n + 1, nb - 1), q_col(rho, hp))),
                nat, nxt, nat, nxt, nat, nxt]
    nview = lambda a: a.reshape(l_sub, r * ATT_W)
    args = [qk_v, z_v, qk_v, qk_v, nview(dy), nview(dy), nview(lse), nview(lse), nview(dd), nview(dd)]
    if add:
        in_specs += [nat, nat]
        args += [nview(acc[0]), nview(acc[1])]
    dk, dv = pl.pallas_call(
        body, grid=(r, 4, nb), in_specs=in_specs, out_specs=[nat, nat],
        out_shape=[SDS((l_sub, r * ATT_W), F32)] * 2, name=f"dil_bwd_dkv_r{r}",
        compiler_params=_cp(("parallel", "parallel", "arbitrary")))(*args)
    return dk.reshape(s, ATT_W), dv.reshape(s, ATT_W)


def _dil_rows(base, r):
    if r == 1:
        return pl.ds(pl.multiple_of(base, DIL_BLK), DIL_BLK)
    return pl.ds(base, DIL_BLK, stride=r)


def _dil_block(idx, r, nb):
    shift = nb.bit_length() - 1
    rho = idx >> shift
    n = idx & (nb - 1)
    base = rho + n * (r * DIL_BLK)
    return _dil_rows(base, r), _dil_rows(jnp.maximum(base - r * DIL_BLK, rho), r), n > 0


def _cat(a, b):
    return jnp.concatenate([a, b], axis=0)


def _two_heads(v, first_head):
    zero = jnp.zeros_like(v)
    return _cat(jnp.where(first_head, v, zero), jnp.where(first_head, zero, v))


def _dil_bands():
    b = DIL_BLK
    q = _row((2 * b, 2 * b)) & (b - 1)
    col = _lane((2 * b, 2 * b))
    return (col < b) & (col >= q), (col >= b) & (col - b <= q)


def dil_fwd_all(qkv, *, unroll=8):
    s = qkv.shape[0]
    b = DIL_BLK
    n_blk = s // b

    def body(q_ref, k_ref, v_ref, o_ref, l_ref):
        first_head = _lane((b, LANES)) < 64
        band_prev, band_cur = _dil_bands()
        for g, (_, r) in enumerate(DIL_PATTERNS):
            nb = n_blk // r

            def group(it, carry, g=g, r=r, nb=nb):
                loaded = []
                for u in range(unroll):
                    rows_c, rows_p, has_prev = _dil_block(it * unroll + u, r, nb)
                    vals = [q_ref[rows_c, :].astype(BF16), k_ref[rows_p, :].astype(BF16), k_ref[rows_c, :].astype(BF16),
                            v_ref[rows_p, :].astype(BF16), v_ref[rows_c, :].astype(BF16)]
                    state = (o_ref[rows_c, :], l_ref[rows_c, :]) if g else None
                    loaded.append((rows_c, has_prev, vals, state))
                done = []
                for rows_c, has_prev, (qv, kp, kc, vp, vc), state in loaded:
                    sc = jnp.where(band_cur | (band_prev & has_prev), _nt(_two_heads(qv, first_head), _cat(kp, kc)), NEG)
                    m = jnp.max(sc, axis=-1, keepdims=True)
                    p = jnp.exp(sc - m)
                    den = jnp.sum(p, axis=-1, keepdims=True)
                    both = _nn(p.astype(BF16), _cat(vp, vc)) / den
                    lse2 = m + jnp.log(den)
                    ov = jnp.where(first_head, both[:b], both[b:])
                    lse = jnp.where(first_head, lse2[:b], lse2[b:])
                    if state is not None:
                        m2 = jnp.maximum(state[1], lse)
                        wp = jnp.exp(state[1] - m2)
                        wn = jnp.exp(lse - m2)
                        ov = (wp * state[0] + wn * ov) / (wp + wn)
                        lse = m2 + jnp.log(wp + wn)
                    done.append((rows_c, ov, lse))
                for rows_c, ov, lse in done:
                    o_ref[rows_c, :] = ov
                    l_ref[rows_c, :] = lse
                return carry

            lax.fori_loop(0, n_blk // unroll, group, 0)

    col_blk = lambda k: pl.BlockSpec((s, LANES), lambda hp: (0, 4 * k + hp))
    out = pl.BlockSpec((s, LANES), lambda hp: (0, hp))
    return pl.pallas_call(
        body, grid=(4,), in_specs=[col_blk(0), col_blk(1), col_blk(2)], out_specs=[out, out],
        out_shape=[SDS((s, ATT_W), F32)] * 2, name="dil_fwd", compiler_params=_cp(("parallel",)))(qkv, qkv, qkv)


def dil_bwd_all(qkv, dy, lse, y, exchange=(), *, unroll=8):
    s = qkv.shape[0]
    b = DIL_BLK
    n_blk = s // b
    ne = len(exchange)

    def body(q_ref, k_ref, v_ref, do_ref, l_ref, y_ref, *rest):
        e_ins, (dq_ref, dk_ref, dv_ref), e_outs = rest[:ne], rest[ne:ne + 3], rest[ne + 3:2 * ne + 3]
        comm = (e_ins, e_outs) + tuple(rest[2 * ne + 3:])
        if ne:
            @pl.when(pl.program_id(0) == 0)
            def _():
                _to_chips_start(*comm)

        dq_ref[...] = jnp.zeros_like(dq_ref)
        dk_ref[...] = jnp.zeros_like(dk_ref)
        dv_ref[...] = jnp.zeros_like(dv_ref)
        first_head = _lane((b, LANES)) < 64
        band_prev, band_cur = _dil_bands()
        for _, r in DIL_PATTERNS:
            nb = n_blk // r

            def group(it, carry, r=r, nb=nb):
                loaded = []
                for u in range(unroll):
                    rows_c, rows_p, has_prev = _dil_block(it * unroll + u, r, nb)
                    vals = [q_ref[rows_c, :].astype(BF16), k_ref[rows_p, :].astype(BF16), k_ref[rows_c, :].astype(BF16),
                            v_ref[rows_p, :].astype(BF16), v_ref[rows_c, :].astype(BF16), do_ref[rows_c, :],
                            l_ref[rows_c, :], y_ref[rows_c, :]]
                    loaded.append((rows_c, rows_p, has_prev, vals))
                done = []
                for rows_c, rows_p, has_prev, (qv, kp, kc, vp, vc, dof, lv, yv) in loaded:
                    q2 = _two_heads(qv, first_head)
                    do2 = _two_heads(dof.astype(BF16), first_head)
                    kcat, vcat = _cat(kp, kc), _cat(vp, vc)
                    lse2 = _cat(lv[:, 0:1], lv[:, 64:65])
                    dd2 = jnp.sum(_two_heads(dof * yv, first_head), axis=-1, keepdims=True)
                    p = jnp.exp(jnp.where(band_cur | (band_prev & has_prev), _nt(q2, kcat), NEG) - lse2)
                    ds = (p * (_nt(do2, vcat) - dd2)).astype(BF16)
                    dq2 = _nn(ds, kcat)
                    dkcat = _tn(ds, q2)
                    dvcat = _tn(p.astype(BF16), do2)
                    done.append((rows_c, rows_p, (jnp.where(first_head, dq2[:b], dq2[b:]), dkcat[:b], dkcat[b:],
                                                  dvcat[:b], dvcat[b:])))
                for rows_c, rows_p, (dq, dk_p, dk_c, dv_p, dv_c) in done:
                    dq_ref[rows_c, :] += dq
                    dk_ref[rows_p, :] += dk_p
                    dk_ref[rows_c, :] += dk_c
                    dv_ref[rows_p, :] += dv_p
                    dv_ref[rows_c, :] += dv_c
                return carry

            lax.fori_loop(0, n_blk // unroll, group, 0)

        if ne:
            @pl.when(pl.program_id(0) == 3)
            def _():
                _to_chips_finish(*comm)

    col_blk = lambda k: pl.BlockSpec((s, LANES), lambda hp: (0, 4 * k + hp))
    nat = pl.BlockSpec((s, LANES), lambda hp: (0, hp))
    return pl.pallas_call(
        body, grid=(4,), in_specs=[col_blk(0), col_blk(1), col_blk(2), nat, nat, nat] + [ANY] * ne,
        out_specs=[nat, nat, nat] + [ANY] * ne, out_shape=[SDS((s, ATT_W), F32)] * 3 + _to_chips_shapes(exchange),
        scratch_shapes=_to_chips_sems(ne) if ne else [], name="dil_bwd",
        compiler_params=_cp(("arbitrary",)))(qkv, qkv, qkv, dy, lse, y, *exchange)


def _sigmoid(v):
    return 1.0 / (1.0 + jnp.exp(-v))


def gate_mix(ya, yb, wa, wb, z, *, tm=512, tn=512):
    s = ya.shape[0]
    d = wa.shape[1]
    ga_blk = 3 * ATT_W * 2 // tn
    gb_blk = ga_blk + d // tn

    def body(ya_ref, yb_ref, wa_ref, wb_ref, ga_ref, gb_ref, pa_ref, pb_ref, mx_ref):
        pa = _nn(ya_ref[...], wa_ref[...])
        pb = _nn(yb_ref[...].astype(BF16), wb_ref[...])
        pa_ref[...] = pa.astype(BF16)
        pb_ref[...] = pb.astype(BF16)
        mx_ref[...] = (_sigmoid(ga_ref[...].astype(F32)) * pa + _sigmoid(gb_ref[...].astype(F32)) * pb).astype(BF16)

    out = pl.BlockSpec((tm, tn), lambda i, j: (i, j))
    return pl.pallas_call(
        body, grid=(s // tm, d // tn),
        in_specs=[pl.BlockSpec((tm, ATT_W), lambda i, j: (i, 0)), pl.BlockSpec((tm, ATT_W), lambda i, j: (i, 0)),
                  pl.BlockSpec((ATT_W, tn), lambda i, j: (0, j)), pl.BlockSpec((ATT_W, tn), lambda i, j: (0, j)),
                  pl.BlockSpec((tm, tn), lambda i, j: (i, ga_blk + j)), pl.BlockSpec((tm, tn), lambda i, j: (i, gb_blk + j))],
        out_specs=[out, out, out], out_shape=[SDS((s, d), BF16)] * 3, name="gate_mix",
        compiler_params=_cp(("parallel", "parallel")))(ya, yb, wa, wb, z, z)


def gate_bwd(dmx, z, pa, pb, *, tm=256):
    s, d = dmx.shape

    def body(dm_ref, ga_ref, gb_ref, pa_ref, pb_ref, dpa_ref, dpb_ref, dg_ref):
        dm = dm_ref[...].astype(F32)
        sa = _sigmoid(ga_ref[...].astype(F32))
        sb = _sigmoid(gb_ref[...].astype(F32))
        dpa_ref[...] = (dm * sa).astype(BF16)
        dpb_ref[...] = (dm * sb).astype(BF16)
        dg_ref[:, 0:d] = (dm * pa_ref[...].astype(F32) * sa * (1.0 - sa)).astype(BF16)
        dg_ref[:, d:2 * d] = (dm * pb_ref[...].astype(F32) * sb * (1.0 - sb)).astype(BF16)

    row = pl.BlockSpec((tm, d), lambda i: (i, 0))
    return pl.pallas_call(
        body, grid=(s // tm,),
        in_specs=[row, pl.BlockSpec((tm, d), lambda i: (i, 3)), pl.BlockSpec((tm, d), lambda i: (i, 4)), row, row],
        out_specs=[row, row, pl.BlockSpec((tm, 2 * d), lambda i: (i, 0))],
        out_shape=[SDS((s, d), BF16), SDS((s, d), BF16), SDS((s, 2 * d), BF16)], name="gate_bwd",
        compiler_params=_cp(("parallel",)))(dmx, z, z, pa, pb)


GELU_C = math.sqrt(2.0 / math.pi)


def _gelu_parts(a):
    inner = GELU_C * (a + 0.044715 * a * a * a)
    th = jnp.tanh(inner)
    gelu = 0.5 * a * (1.0 + th)
    dgelu = 0.5 * (1.0 + th) + 0.5 * a * (1.0 - th * th) * GELU_C * (1.0 + 3.0 * 0.044715 * a * a)
    return gelu, dgelu


def _causal_taps(u, before):
    row = _row(u.shape)
    r1 = jnp.where(row == 0, before[7:8, :], pltpu.roll(u, 1, axis=0))
    r2 = jnp.where(row == 0, before[6:7, :], jnp.where(row == 1, before[7:8, :], pltpu.roll(u, 2, axis=0)))
    return r1, r2


def ffn_up(h, wa, wb, cw, cb, *, tm=512, tn=256):
    s, d = h.shape
    f = wa.shape[1]
    nj = f // tn

    def body(h_ref, wa_ref, wb_ref, cwa_ref, cwb_ref, cba_ref, cbb_ref, ua_ref, ub_ref, m_ref, carry):
        @pl.when(pl.program_id(1) == 0)
        def _():
            carry[...] = jnp.zeros_like(carry)

        conv = []
        for k, (w_ref, cw_ref, cb_ref, u_ref) in enumerate(((wa_ref, cwa_ref, cba_ref, ua_ref), (wb_ref, cwb_ref, cbb_ref, ub_ref))):
            u16 = _nn(h_ref[...], w_ref[...]).astype(BF16)
            u_ref[...] = u16
            u = u16.astype(F32)
            r1, r2 = _causal_taps(u, carry[k])
            carry[k] = u[tm - 8:tm, :]
            conv.append(cw_ref[0:1, :] * r2 + cw_ref[1:2, :] * r1 + cw_ref[2:3, :] * u + cb_ref[...])
        m_ref[...] = (_gelu_parts(conv[0])[0] * conv[1]).astype(BF16)

    out = pl.BlockSpec((tm, tn), lambda j, i: (i, j))
    return pl.pallas_call(
        body, grid=(nj, s // tm),
        in_specs=[pl.BlockSpec((tm, d), lambda j, i: (i, 0)),
                  pl.BlockSpec((d, tn), lambda j, i: (0, j)), pl.BlockSpec((d, tn), lambda j, i: (0, j)),
                  pl.BlockSpec((3, tn), lambda j, i: (0, j)), pl.BlockSpec((3, tn), lambda j, i: (0, nj + j)),
                  pl.BlockSpec((1, tn), lambda j, i: (0, j)), pl.BlockSpec((1, tn), lambda j, i: (0, nj + j))],
        out_specs=[out, out, out], out_shape=[SDS((s, f), BF16)] * 3,
        scratch_shapes=[pltpu.VMEM((2, 8, tn), F32)], name="ffn_up",
        compiler_params=_cp(("parallel", "arbitrary")))(h, wa, wb, cw, cw, cb, cb)


def ffn_bwd(dm, ua, ub, cw, cb, *, tm=512, tn=256):
    s, f = dm.shape
    nj = f // tn
    ni = s // tm
    halo = 16

    def body(dm_ref, ua_ref, ub_ref, ha_ref, hb_ref, cwa_ref, cwb_ref, cba_ref, cbb_ref,
             dua_ref, dub_ref, ga_ref, gb_ref, carry):
        i = pl.program_id(1)

        @pl.when(i == 0)
        def _():
            carry[...] = jnp.zeros_like(carry)
            ga_ref[...] = jnp.zeros_like(ga_ref)
            gb_ref[...] = jnp.zeros_like(gb_ref)

        first_tile = i == ni - 1
        row = _row((tm, tn))
        dmv = dm_ref[...].astype(F32)
        us, taps, convs = [], [], []
        for u_ref, h_ref, cw_ref, cb_ref in ((ua_ref, ha_ref, cwa_ref, cba_ref), (ub_ref, hb_ref, cwb_ref, cbb_ref)):
            u = u_ref[...].astype(F32)
            before = jnp.where(first_tile, 0.0, h_ref[halo - 8:halo, :].astype(F32))
            r1, r2 = _causal_taps(u, before)
            us.append(u)
            taps.append((r1, r2))
            convs.append(cw_ref[0:1, :] * r2 + cw_ref[1:2, :] * r1 + cw_ref[2:3, :] * u + cb_ref[...])
        gelu, dgelu = _gelu_parts(convs[0])
        dcs = (dmv * convs[1] * dgelu, dmv * gelu)
        for k, (dc, cw_ref, du_ref, g_ref) in enumerate(((dcs[0], cwa_ref, dua_ref, ga_ref), (dcs[1], cwb_ref, dub_ref, gb_ref))):
            r1, r2 = taps[k]
            g_ref[0:1, :] += jnp.sum(dc * r2, axis=0, keepdims=True)
            g_ref[1:2, :] += jnp.sum(dc * r1, axis=0, keepdims=True)
            g_ref[2:3, :] += jnp.sum(dc * us[k], axis=0, keepdims=True)
            g_ref[3:4, :] += jnp.sum(dc, axis=0, keepdims=True)
            after = carry[k]
            n1 = jnp.where(row == tm - 1, after[0:1, :], pltpu.roll(dc, tm - 1, axis=0))
            n2 = jnp.where(row == tm - 2, after[0:1, :], jnp.where(row == tm - 1, after[1:2, :], pltpu.roll(dc, tm - 2, axis=0)))
            du_ref[...] = (cw_ref[2:3, :] * dc + cw_ref[1:2, :] * n1 + cw_ref[0:1, :] * n2).astype(BF16)
            carry[k] = dc[0:8, :]

    tile = pl.BlockSpec((tm, tn), lambda j, i: (ni - 1 - i, j))
    halo_spec = pl.BlockSpec((halo, tn), lambda j, i: (jnp.maximum((ni - 1 - i) * (tm // halo) - 1, 0), j))
    gspec = pl.BlockSpec((8, tn), lambda j, i: (0, j))
    return pl.pallas_call(
        body, grid=(nj, ni),
        in_specs=[tile, tile, tile, halo_spec, halo_spec,
                  pl.BlockSpec((3, tn), lambda j, i: (0, j)), pl.BlockSpec((3, tn), lambda j, i: (0, nj + j)),
                  pl.BlockSpec((1, tn), lambda j, i: (0, j)), pl.BlockSpec((1, tn), lambda j, i: (0, nj + j))],
        out_specs=[tile, tile, gspec, gspec],
        out_shape=[SDS((s, f), BF16), SDS((s, f), BF16), SDS((8, f), F32), SDS((8, f), F32)],
        scratch_shapes=[pltpu.VMEM((2, 8, tn), F32)], name="ffn_bwd",
        compiler_params=_cp(("parallel", "arbitrary")))(dm, ua, ub, ua, ub, cw, cw, cb, cb)


def adamw(w, g, m, v, *, name, tr=None):
    r = w.shape[0]
    rest = w.shape[1:]
    if tr is None:
        tr = r
        for cand in (256, 128, 64, 32, 16, 8):
            if r % cand == 0:
                tr = cand
                break

    def body(w_ref, g_ref, m_ref, v_ref, d_ref, nm_ref, nv_ref):
        gv = g_ref[...]
        mn = ADAM_B1 * m_ref[...] + (1.0 - ADAM_B1) * gv
        vn = ADAM_B2 * v_ref[...] + (1.0 - ADAM_B2) * (gv * gv)
        m_hat = mn / (1.0 - ADAM_B1 ** ADAM_STEP)
        v_hat = vn / (1.0 - ADAM_B2 ** ADAM_STEP)
        d_ref[...] = -ADAM_LR * (m_hat / (jnp.sqrt(v_hat) + ADAM_EPS) + ADAM_WD * w_ref[...])
        nm_ref[...] = mn
        nv_ref[...] = vn

    blk = pl.BlockSpec((tr,) + rest, lambda i: (i,) + (0,) * len(rest))
    return pl.pallas_call(body, grid=(r // tr,), in_specs=[blk] * 4, out_specs=[blk] * 3, out_shape=[SDS(w.shape, F32)] * 3,
                          name=name, compiler_params=_cp(("parallel",)))(w, g, m, v)


def adamw_rows_view(w, g, m, v, *, name, tc=256):
    r, _, c = w.shape

    def body(w_ref, g_ref, m_ref, v_ref, d_ref, nm_ref, nv_ref, go_ref):
        gv = g_ref[...][:, None, :]
        mn = ADAM_B1 * m_ref[...] + (1.0 - ADAM_B1) * gv
        vn = ADAM_B2 * v_ref[...] + (1.0 - ADAM_B2) * (gv * gv)
        m_hat = mn / (1.0 - ADAM_B1 ** ADAM_STEP)
        v_hat = vn / (1.0 - ADAM_B2 ** ADAM_STEP)
        d_ref[...] = -ADAM_LR * (m_hat / (jnp.sqrt(v_hat) + ADAM_EPS) + ADAM_WD * w_ref[...])
        nm_ref[...] = mn
        nv_ref[...] = vn
        go_ref[...] = gv

    b3 = pl.BlockSpec((r, 1, tc), lambda i: (0, 0, i))
    b2 = pl.BlockSpec((r, tc), lambda i: (0, i))
    return pl.pallas_call(body, grid=(c // tc,), in_specs=[b3, b2, b3, b3], out_specs=[b3] * 4,
                          out_shape=[SDS(w.shape, F32)] * 4, name=name, compiler_params=_cp(("parallel",)))(w, g, m, v)


ANY = pl.BlockSpec(memory_space=pl.ANY)
ICI_KINDS = ("x", "y", "xy")


def _coords():
    return lax.axis_index("x"), lax.axis_index("y"), lax.axis_index("c")


def _peer(kind, x, y, c):
    if kind == "c":
        return (x, y, 1 - c)
    if kind == "x":
        return (1 - x, y, c)
    if kind == "y":
        return (x, 1 - y, c)
    return (1 - x, 1 - y, c)


def _chip_of(p):
    return 2 * p[0] + p[1]


def _half(rows, which):
    h = rows // 2
    return pl.ds(pl.multiple_of(which * h, 16), h)


def _remote(src, dst, send_sem, recv_sem, to):
    return pltpu.make_async_remote_copy(src_ref=src, dst_ref=dst, send_sem=send_sem, recv_sem=recv_sem,
                                        device_id=to, device_id_type=MESH)


def allgather_chips(shards, halved, *, name):
    n = len(shards)

    def body(*refs):
        parts = (refs[:n], refs[n:2 * n], refs[2 * n], refs[2 * n + 1], halved)
        _allgather_start(*parts)
        _allgather_finish(*parts)

    return pl.pallas_call(
        body, in_specs=[ANY] * n, out_specs=[ANY] * n,
        out_shape=_allgather_shapes(shards), scratch_shapes=_allgather_sems(n), name=name)(*shards)


def _allgather_shapes(shards):
    return [SDS((4,) + a.shape, a.dtype) for a in shards]


def _allgather_sems(n):
    return [pltpu.SemaphoreType.DMA((n, 6)), pltpu.SemaphoreType.DMA((n, 6))]


def _allgather_rows(ref, is_halved, which):
    r = ref.shape[0]
    return _half(r, which) if is_halved else pl.ds(0, r)


def _allgather_first(ins, outs, send_sems, recv_sems, halved):
    x, y, c = _coords()
    my_chip = 2 * x + y
    cps = []
    for w in range(len(ins)):
        rows = _allgather_rows(ins[w], halved[w], c)
        for k, kind in enumerate(ICI_KINDS):
            cps.append(_remote(ins[w].at[rows], outs[w].at[my_chip, rows], send_sems.at[w, k], recv_sems.at[w, k],
                               _peer(kind, x, y, c)))
    return cps


def _allgather_start(ins, outs, send_sems, recv_sems, halved):
    for cp in _allgather_first(ins, outs, send_sems, recv_sems, halved):
        cp.start()


def _allgather_finish(ins, outs, send_sems, recv_sems, halved):
    x, y, c = _coords()
    me = (x, y, c)
    second = []
    for w in range(len(ins)):
        for k, kind in enumerate(ICI_KINDS):
            landed = outs[w].at[_chip_of(_peer(kind, x, y, c)), _allgather_rows(ins[w], halved[w], c)]
            _remote(landed, landed, send_sems.at[w, k], recv_sems.at[w, k], me).wait_recv()
            if halved[w]:
                cp = _remote(landed, landed, send_sems.at[w, 3 + k], recv_sems.at[w, 3 + k], _peer("c", x, y, c))
                cp.start()
                second.append(cp)
    for w in range(len(ins)):
        if halved[w]:
            for k, kind in enumerate(ICI_KINDS):
                other = outs[w].at[_chip_of(_peer(kind, x, y, c)), _allgather_rows(ins[w], True, 1 - c)]
                _remote(other, other, send_sems.at[w, 3 + k], recv_sems.at[w, 3 + k], me).wait_recv()
    for cp in _allgather_first(ins, outs, send_sems, recv_sems, halved) + second:
        cp.wait_send()


def _half_of(ref, by_cols, which):
    lead = (slice(None),) * (len(ref.shape) - 2)
    if by_cols:
        h = ref.shape[-1] // 2
        return ref.at[lead + (slice(None), pl.ds(pl.multiple_of(which * h, LANES), h))]
    return ref.at[lead + (_half(ref.shape[-2], which),)]


def _half_shape(shape, by_cols):
    return shape[:-1] + (shape[-1] // 2,) if by_cols else shape[:-2] + (shape[-2] // 2, shape[-1])


def grads_to_sibling(gs, by_cols, *, name):
    n = len(gs)

    def body(*refs):
        ins, outs = refs[:n], refs[n:2 * n]
        send_sems, recv_sems = refs[2 * n:]
        x, y, c = _coords()
        cps = []
        for w in range(n):
            cp = _remote(_half_of(ins[w], by_cols[w], 1 - c), outs[w], send_sems.at[w], recv_sems.at[w], _peer("c", x, y, c))
            cp.start()
            cps.append(cp)
        for cp in cps:
            cp.wait()

    return pl.pallas_call(
        body, in_specs=[ANY] * n, out_specs=[ANY] * n,
        out_shape=[SDS(_half_shape(a.shape, bc), a.dtype) for a, bc in zip(gs, by_cols)],
        scratch_shapes=[pltpu.SemaphoreType.DMA((n,)), pltpu.SemaphoreType.DMA((n,))], name=name)(*gs)


def grads_to_chips(ps, *, name):
    n = len(ps)

    def body(*refs):
        parts = (refs[:n], refs[n:2 * n], refs[2 * n], refs[2 * n + 1])
        _to_chips_start(*parts)
        _to_chips_finish(*parts)

    return pl.pallas_call(
        body, in_specs=[ANY] * n, out_specs=[ANY] * n,
        out_shape=_to_chips_shapes(ps), scratch_shapes=_to_chips_sems(n), name=name)(*ps)


def _to_chips_shapes(ps):
    return [SDS((3,) + a.shape[1:], a.dtype) for a in ps]


def _to_chips_sems(n):
    return [pltpu.SemaphoreType.DMA((n, 3)), pltpu.SemaphoreType.DMA((n, 3))]


def _to_chips_copies(ins, outs, send_sems, recv_sems):
    x, y, c = _coords()
    cps = []
    for w in range(len(ins)):
        for k, kind in enumerate(ICI_KINDS):
            to = _peer(kind, x, y, c)
            cps.append(_remote(ins[w].at[_chip_of(to)], outs[w].at[k], send_sems.at[w, k], recv_sems.at[w, k], to))
    return cps


def _to_chips_start(ins, outs, send_sems, recv_sems):
    for cp in _to_chips_copies(ins, outs, send_sems, recv_sems):
        cp.start()


def _to_chips_finish(ins, outs, send_sems, recv_sems):
    for cp in _to_chips_copies(ins, outs, send_sems, recv_sems):
        cp.wait()


def halves_to_full(hs, by_cols, *, name):
    n = len(hs)

    def body(*refs):
        ins, outs = refs[:n], refs[n:2 * n]
        send_sems, recv_sems = refs[2 * n:]
        x, y, c = _coords()
        cps = []
        for w in range(n):
            cp = _remote(ins[w], _half_of(outs[w], by_cols[w], c), send_sems.at[w], recv_sems.at[w], _peer("c", x, y, c))
            cp.start()
            cps.append(cp)
        for cp in cps:
            cp.wait()

    return pl.pallas_call(
        body, in_specs=[ANY] * n, out_specs=[ANY] * n,
        out_shape=[SDS((a.shape[0], 2 * a.shape[1]) if bc else (2 * a.shape[0], a.shape[1]), a.dtype)
                   for a, bc in zip(hs, by_cols)],
        scratch_shapes=[pltpu.SemaphoreType.DMA((n,)), pltpu.SemaphoreType.DMA((n,))],
        name=name)(*hs)


def _row_tile(rows):
    for cand in (256, 192, 176, 128, 64, 32, 16):
        if rows % cand == 0:
            return cand
    return rows


def chip_sum(g, recv, c_arr, by_cols, *, name):
    _, r, cols = g.shape

    def body(c_ref, g_ref, r_ref, f_ref, b_ref):
        tot = g_ref[...] + r_ref[...]
        f_ref[...] = tot
        b_ref[...] = tot.astype(BF16)

    if by_cols:
        tc = 2 * LANES
        nblk = cols // 2 // tc
        shape = (4, r, cols // 2)
        blk = pl.BlockSpec((None, r, tc), lambda j, i, c_ref: (j, 0, i))
        mine = pl.BlockSpec((None, r, tc), lambda j, i, c_ref: (j, 0, c_ref[0] * nblk + i))
    else:
        tr = _row_tile(r // 2)
        nblk = r // 2 // tr
        shape = (4, r // 2, cols)
        blk = pl.BlockSpec((None, tr, cols), lambda j, i, c_ref: (j, i, 0))
        mine = pl.BlockSpec((None, tr, cols), lambda j, i, c_ref: (j, c_ref[0] * nblk + i, 0))
    grid_spec = pltpu.PrefetchScalarGridSpec(num_scalar_prefetch=1, grid=(4, nblk), in_specs=[mine, blk], out_specs=[blk, blk])
    return pl.pallas_call(body, grid_spec=grid_spec, out_shape=[SDS(shape, F32), SDS(shape, BF16)],
                          name=name, compiler_params=_cp(("parallel", "parallel")))(c_arr, g, recv)


def final_sum(pf, recv, chip_arr, *, name):
    _, h, cols = pf.shape
    tr = _row_tile(h)

    def body(chip_ref, p_ref, r_ref, o_ref):
        o_ref[...] = ((p_ref[...] + r_ref[0].astype(F32)) + r_ref[1].astype(F32)) + r_ref[2].astype(F32)

    grid_spec = pltpu.PrefetchScalarGridSpec(
        num_scalar_prefetch=1, grid=(h // tr,),
        in_specs=[pl.BlockSpec((None, tr, cols), lambda i, chip_ref: (chip_ref[0], i, 0)),
                  pl.BlockSpec((3, tr, cols), lambda i, chip_ref: (0, i, 0))],
        out_specs=pl.BlockSpec((tr, cols), lambda i, chip_ref: (i, 0)))
    return pl.pallas_call(body, grid_spec=grid_spec, out_shape=SDS((h, cols), F32), name=name,
                          compiler_params=_cp(("parallel",)))(chip_arr, pf, recv)


def allreduce_small(v, *, name):
    rws, cols = v.shape

    def body(v_ref, all_ref, sum_ref, send_sems, recv_sems, local_sem):
        x, y, c = _coords()
        me, sibling = (x, y, c), (x, y, 1 - c)
        chips = [(1 - x, y), (x, 1 - y), (1 - x, 1 - y)]

        def rows(px, py, pc):
            return all_ref.at[pl.ds(pl.multiple_of((4 * px + 2 * py + pc) * rws, 8), rws), :]

        def copy(k, block, to, src=None):
            return _remote(rows(*block) if src is None else src, rows(*block), send_sems.at[k], recv_sems.at[k], to)

        mine = pltpu.make_async_copy(v_ref, rows(*me), local_sem)
        mine.start()
        first = [copy(0, me, sibling, src=v_ref)]
        first += [copy(1 + j, me, (*chip, c), src=v_ref) for j, chip in enumerate(chips)]
        for cp in first:
            cp.start()
        passed = [copy(4 + j, (*chip, c), sibling) for j, chip in enumerate(chips)]
        for j, chip in enumerate(chips):
            copy(1 + j, (*chip, c), me).wait_recv()
            passed[j].start()
        copy(0, sibling, me).wait_recv()
        for j, chip in enumerate(chips):
            copy(4 + j, (*chip, 1 - c), me).wait_recv()
        for cp in first + passed:
            cp.wait_send()
        mine.wait()
        tot = all_ref[0:rws, :]
        for dev in range(1, 8):
            tot = tot + all_ref[dev * rws:(dev + 1) * rws, :]
        sum_ref[...] = tot

    vm = pl.BlockSpec(memory_space=pltpu.VMEM)
    return pl.pallas_call(
        body, in_specs=[vm], out_specs=[vm, vm],
        out_shape=[SDS((8 * rws, cols), v.dtype), SDS((rws, cols), v.dtype)],
        scratch_shapes=[pltpu.SemaphoreType.DMA((7,)), pltpu.SemaphoreType.DMA((7,)), pltpu.SemaphoreType.DMA],
        name=name)(v)[1]


def _pack_rows(parts, rows):
    out = []
    for a, r in zip(parts, rows):
        flat = a.reshape(-1)
        flat = jnp.pad(flat, (0, r * LANES - flat.shape[0]))
        out.append(flat.reshape(r, LANES))
    return jnp.concatenate(out, axis=0)


def _unpack_rows(packed, shapes, rows):
    out, at = [], 0
    for shp, r in zip(shapes, rows):
        size = int(np.prod(shp))
        out.append(packed[at:at + r].reshape(-1)[:size].reshape(shp))
        at += r
    return out


def kernel(x, g_pre_mix, w_in, b_forget, w_o_fox, w_o_dil, w_out, g_post_mix, g_pre_ffn, w_up, conv_w, conv_b, w_down, g_post_ffn, loss_target, m_g_pre_mix, m_w_in, m_b_forget, m_w_o_fox, m_w_o_dil, m_w_out, m_g_post_mix, m_g_pre_ffn, m_w_up, m_conv_w, m_conv_b, m_w_down, m_g_post_ffn, v_g_pre_mix, v_w_in, v_b_forget, v_w_o_fox, v_w_o_dil, v_w_out, v_g_post_mix, v_g_pre_ffn, v_w_up, v_conv_w, v_conv_b, v_w_down, v_g_post_ffn):
    xi, yi, ci = _coords()
    chip = 2 * xi + yi
    c_arr = jnp.reshape(ci, (1,)).astype(jnp.int32)
    chip_arr = jnp.reshape(chip, (1,)).astype(jnp.int32)
    xs = x[0]
    target = loss_target[0]
    s, d = xs.shape
    f_half = w_down.shape[1] * 4
    cols_in = w_in.shape[2]

    big = (w_in, w_o_fox, w_o_dil, w_out, w_up, w_down)
    shards = [w[0].astype(BF16) for w in big]
    a_in, a_cw = allgather_chips([shards[0], conv_w[0]], [True, False], name="allgather_w_in")
    w_in_full = jnp.concatenate([jnp.where(chip == j, shards[0], a_in[j]) for j in range(4)], axis=1)
    cw = jnp.concatenate([jnp.where(chip == j, conv_w[0], a_cw[j]) for j in range(4)], axis=1)
    nf = N_HEADS
    e_a, e_b = 3 * ATT_W, 3 * ATT_W + nf
    wz = jnp.concatenate([w_in_full[:, :e_a], w_in_full[:, e_b:]], axis=1)
    wf = jnp.pad(w_in_full[:, e_a:e_b], ((0, 0), (0, LANES - nf)))
    cb = conv_b
    bfo = jnp.pad(b_forget, ((0, 0), (0, LANES - nf)))

    h1 = rmsnorm_fwd(xs, g_pre_mix)
    z = mm([(h1, d, 0)], [(wz, d, 0)], nt=False, out_dtype=BF16, tm=1024, tn=512, name="in_proj")
    fa = mm([(h1, d, 0)], [(wf, d, 0)], nt=False, out_dtype=F32, tm=1024, tn=LANES, name="in_proj_forget")
    q_aug, k_aug = fox_prep(z, fa, bfo)
    ya, lse_a, *late = fox_fwd(q_aug, k_aug, z, gather=shards[1:])
    a_of, a_od, a_out, a_up, a_down = [
        lax.dynamic_update_index_in_dim(a4, own, chip, 0) for a4, own in zip(late, shards[1:])]
    wo_a = jnp.concatenate([a_of[j] for j in range(4)], axis=1)
    wo_b = jnp.concatenate([a_od[j] for j in range(4)], axis=1)
    w_o = a_out.reshape(d, d)
    w_dn = a_down.reshape(f_half, d)
    wu_a = jnp.concatenate([a_up[0], a_up[1]], axis=1)
    wu_b = jnp.concatenate([a_up[2], a_up[3]], axis=1)
    qkv_b = rope_apply([(z, Z_QB, QK_SCALE, True), (z, Z_KB, 1.0, True), (z, Z_VB, 1.0, False)], rope_tables(s, 1.0),
                       out_dtype=F32, name="rope_fwd")
    yb, lse_b = dil_fwd_all(qkv_b)
    pa, pb, mixed = gate_mix(ya, yb, wo_a, wo_b, z)
    y1, x1 = mm_rms_res(mixed, w_o, g_post_mix, xs, tm=512, name="out_proj")
    h2 = rmsnorm_fwd(x1, g_pre_ffn)
    ua, ub, mid = ffn_up(h2, wu_a, wu_b, cw, cb)
    y2, dout, sq = mm_rms_res(mid, w_dn, g_post_ffn, x1, target, tm=512, name="down_proj")
    loss = lax.psum(0.5 * sq[0, 0] / d, ("x", "y", "c"))

    dy2, gg_post_ffn = rmsnorm_bwd(dout, y2, g_post_ffn, None, out_dtype=BF16, name="norm_bwd_post_ffn")
    dmid = mm([(dy2, d, 0)], [(w_dn, d, 0)], nt=True, out_dtype=BF16, tm=512, tn=f_half // 2, name="down_dgrad")
    dw_down = wgrad((mid, f_half, 0), dy2, tk=f_half // 2, tn=1024, ts=1024, name="down_wgrad")
    dua, dub, gc_a, gc_b = ffn_bwd(dmid, ua, ub, cw, cb)
    dh2 = mm([(dua, f_half, 0), (dub, f_half, 0)], [(wu_a, f_half, 0), (wu_b, f_half, 0)], nt=True, out_dtype=BF16,
             tm=512, tn=512, name="up_dgrad")
    dw_up = jnp.concatenate(
        [wgrad((h2, d, 0), du, tk=1024, tn=f_half // 2, ts=1024, name=f"up_wgrad_{k}", chip_major=True)
         for k, du in enumerate((dua, dub))], axis=0)
    def to_chip_sums(gs, nms, tag, by_cols=False):
        from_sib = grads_to_sibling(gs, [by_cols] * len(gs), name=f"grads_to_sibling_{tag}")
        return [chip_sum(g, r, c_arr, by_cols, name=f"chip_sum_{nm}") for g, r, nm in zip(gs, from_sib, nms)]

    sums_ffn = to_chip_sums([dw_up, dw_down.reshape(4, f_half // 4, d)], ("w_up", "w_down"), "ffn")
    dx1, gg_pre_ffn = rmsnorm_bwd(dh2, x1, g_pre_ffn, dout, out_dtype=F32, name="norm_bwd_pre_ffn")
    dy1, gg_post_mix = rmsnorm_bwd(dx1, y1, g_post_mix, None, out_dtype=BF16, name="norm_bwd_post_mix")
    dmixed = mm([(dy1, d, 0)], [(w_o, d, 0)], nt=True, out_dtype=BF16, tm=512, tn=512, name="out_dgrad")
    dw_out = wgrad((mixed, d, 0), dy1, tk=1024, tn=1024, ts=1024, name="out_wgrad")
    dpa, dpb, dz_g = gate_bwd(dmixed, z, pa, pb)
    dya = mm([(dpa, d, 0)], [(wo_a, d, 0)], nt=True, out_dtype=BF16, tm=512, tn=ATT_W, name="fox_o_dgrad")
    dyb = mm([(dpb, d, 0)], [(wo_b, d, 0)], nt=True, out_dtype=F32, tm=512, tn=ATT_W, name="dil_o_dgrad")
    by_chip_cols = lambda a: jnp.stack([a[:, j * (d // 4):(j + 1) * (d // 4)] for j in range(4)], axis=0)
    dw_of = by_chip_cols(wgrad((ya, ATT_W, 0), dpa, tk=ATT_W, tn=d, ts=1024, name="fox_o_wgrad"))
    dw_od = by_chip_cols(wgrad((yb, ATT_W, 0), dpb, tk=ATT_W, tn=d, ts=1024, name="dil_o_wgrad"))
    sums_mix = to_chip_sums([dw_of, dw_od, dw_out.reshape(4, d // 4, d)], ("w_o_fox", "w_o_dil", "w_out"), "mix")
    dd_a = head_rowsum(dya, ya, name="fox_delta")
    dq_aug, dk_aug, dv_a, *got_ffn = fox_bwd(q_aug, k_aug, z, dya, lse_a, dd_a, exchange=[p[1] for p in sums_ffn])
    dz_a, dfa, gg_bf = fox_post(dq_aug, dk_aug, dv_a, fa, bfo)
    dq_b, dk_b, dv_b, *got_mix = dil_bwd_all(qkv_b, dyb, lse_b, yb, exchange=[p[1] for p in sums_mix])
    dz_b = rope_apply([(dq_b, 0, QK_SCALE, True), (dk_b, 0, 1.0, True), (dv_b, 0, 1.0, False)],
                      rope_tables(s, -1.0), out_dtype=BF16, name="rope_bwd")
    dh1 = mm([(dz_a, e_a, 0), (dz_b, e_a, 0), (dz_g, d, 0), (dz_g, d, 1), (dfa, LANES, 0)],
             [(wz, e_a, 0), (wz, e_a, 1), (wz, d, 3), (wz, d, 4), (wf, LANES, 0)], nt=True, out_dtype=BF16,
             tm=512, tn=512, name="in_dgrad")
    dwt_a = wgrad((dz_a, e_a, 0), h1, tk=e_a // 2, tn=d, ts=1024, name="in_wgrad_a")
    dwt_b = wgrad((dz_b, e_a, 0), h1, tk=e_a // 2, tn=d, ts=1024, name="in_wgrad_b")
    dwt_g = wgrad((dz_g, 2 * d, 0), h1, tk=d, tn=d, ts=1024, name="in_wgrad_g")
    dwt_f = wgrad((dfa, LANES, 0), h1, tk=LANES, tn=d, ts=1024, name="in_wgrad_f")
    grad_x, gg_pre_mix = rmsnorm_bwd(dh1, xs, g_pre_mix, dx1, out_dtype=F32, name="norm_bwd_pre_mix")
    dwt_full = jnp.concatenate([dwt_a, dwt_f[:nf], dwt_b, dwt_g], axis=0)
    dw_in = jnp.stack([dwt_full[j * cols_in:(j + 1) * cols_in] for j in range(4)], axis=0)

    names = ("w_in", "w_o_fox", "w_o_dil", "w_out", "w_up", "w_down")
    sums_in = to_chip_sums([dw_in], ("w_in",), "in", by_cols=True)
    got_in = grads_to_chips([sums_in[0][1]], name="grads_to_chips_in")
    sums = sums_in + sums_mix + sums_ffn
    from_chips = list(got_in) + list(got_mix) + list(got_ffn)
    halves = [final_sum(p[0], r, chip_arr, name=f"final_sum_{nm}") for p, r, nm in zip(sums, from_chips, names)]
    from_half = halves_to_full(halves, [True] + [False] * 5, name="halves_to_full")
    g_big = [lax.dynamic_update_slice_in_dim(full, mine, ci * mine.shape[k == 0], axis=int(k == 0))
             for k, (full, mine) in enumerate(zip(from_half, halves))]
    upd_big = [adamw(w[0], g, m[0], v[0], name=f"adamw_{nm}") for w, g, m, v, nm in list(zip(
        big, g_big, (m_w_in, m_w_o_fox, m_w_o_dil, m_w_out, m_w_up, m_w_down),
        (v_w_in, v_w_o_fox, v_w_o_dil, v_w_out, v_w_up, v_w_down), names))[1:]]
    to_t = lambda a: jnp.transpose(a, (2, 0, 1))
    from_t = lambda a: jnp.transpose(a, (1, 2, 0))
    *upd_in, g_in_t = adamw_rows_view(to_t(w_in), g_big[0], to_t(m_w_in), to_t(v_w_in), name="adamw_w_in")

    g_cw_loc = jnp.concatenate([gc_a[0:3], gc_b[0:3]], axis=1)
    g_cb_loc = jnp.concatenate([gc_a[3:4], gc_b[3:4]], axis=1)
    small_loc = [gg_pre_mix, gg_post_mix, gg_pre_ffn, gg_post_ffn, g_cb_loc, gg_bf[:, :nf], g_cw_loc]
    red_rows = (8, 8, 8, 8, 48, 8, 136)
    red = allreduce_small(_pack_rows(small_loc, red_rows), name="allreduce_small")
    g_pm, g_qm, g_pf, g_qf, g_cb, g_bf, g_cw_full = _unpack_rows(red, [a.shape for a in small_loc], red_rows)
    cols_cw = conv_w.shape[2]
    g_cw = lax.dynamic_slice_in_dim(g_cw_full, chip * cols_cw, cols_cw, axis=1)
    small_w = (g_pre_mix, g_post_mix, g_pre_ffn, g_post_ffn, conv_b, b_forget, conv_w[0])
    small_m = (m_g_pre_mix, m_g_post_mix, m_g_pre_ffn, m_g_post_ffn, m_conv_b, m_b_forget, m_conv_w[0])
    small_v = (v_g_pre_mix, v_g_post_mix, v_g_pre_ffn, v_g_post_ffn, v_conv_b, v_b_forget, v_conv_w[0])
    small_g = (g_pm, g_qm, g_pf, g_qf, g_cb, g_bf, g_cw)
    ad_rows = (8, 8, 8, 8, 48, 8, 40)
    packed = [_pack_rows(t, ad_rows) for t in (small_w, small_g, small_m, small_v)]
    upd_small = [_unpack_rows(o, [a.shape for a in small_w], ad_rows) for o in adamw(*packed, name="adamw_small")]

    order = ("g_pre_mix", "w_in", "b_forget", "w_o_fox", "w_o_dil", "w_out", "g_post_mix", "g_pre_ffn", "w_up", "conv_w",
             "conv_b", "w_down", "g_post_ffn")
    small_names = ("g_pre_mix", "g_post_mix", "g_pre_ffn", "g_post_ffn", "conv_b", "b_forget", "conv_w")
    grads, deltas, new_ms, new_vs = {}, {}, {}, {}
    grads["w_in"] = from_t(g_in_t)
    deltas["w_in"], new_ms["w_in"], new_vs["w_in"] = (from_t(a) for a in upd_in)
    for k, nm in enumerate(names[1:]):
        grads[nm] = g_big[k + 1][None]
        deltas[nm], new_ms[nm], new_vs[nm] = (a[None] for a in upd_big[k])
    for k, nm in enumerate(small_names):
        lead = (lambda a: a[None]) if nm == "conv_w" else (lambda a: a)
        grads[nm] = lead(small_g[k])
        deltas[nm], new_ms[nm], new_vs[nm] = (lead(upd_small[j][k]) for j in range(3))
    return (loss, grad_x[None], *[grads[nm] for nm in order], *[deltas[nm] for nm in order],
            *[new_ms[nm] for nm in order], *[new_vs[nm] for nm in order])
```

```python
import functools
import math

import numpy as np
import jax
import jax.numpy as jnp
from jax import lax
from jax.experimental import pallas as pl
from jax.experimental.pallas import tpu as pltpu

F32 = jnp.float32
BF16 = jnp.bfloat16
SDS = jax.ShapeDtypeStruct
MESH = pl.DeviceIdType.MESH

HEAD_DIM = 64
N_HEADS = 8
LANES = 128
ATT_W = N_HEADS * HEAD_DIM
DIL_PATTERNS = ((128, 1), (512, 4), (2048, 16))
DIL_BLK = 128
ROPE_DIM = HEAD_DIM // 4
ROPE_THETA = 500000.0
RMS_EPS = 1e-6
NEG = -1e30
QK_SCALE = 1.0 / math.sqrt(HEAD_DIM)
ADAM_LR, ADAM_B1, ADAM_B2, ADAM_EPS, ADAM_WD, ADAM_STEP = 0.001, 0.9, 0.999, 1e-08, 0.01, 10
VMEM_LIMIT = 56 * 1024 * 1024

Z_QA, Z_KA, Z_VA, Z_QB, Z_KB, Z_VB = 0, 1, 2, 3, 4, 5
Z_W = 5120


def _cp(sem):
    return pltpu.CompilerParams(dimension_semantics=sem, vmem_limit_bytes=VMEM_LIMIT)


def _nt(a, b):
    return lax.dot_general(a, b, (((1,), (1,)), ((), ())), preferred_element_type=F32)


def _tn(a, b):
    return lax.dot_general(a, b, (((0,), (0,)), ((), ())), preferred_element_type=F32)


def _nn(a, b):
    return jnp.dot(a, b, preferred_element_type=F32)


def _lane(shape):
    return lax.broadcasted_iota(jnp.int32, shape, 1)


def _row(shape):
    return lax.broadcasted_iota(jnp.int32, shape, 0)


def rmsnorm_fwd(x, g, *, tm=512):
    s, d = x.shape

    def body(x_ref, g_ref, h_ref):
        xv = x_ref[...]
        inv = lax.rsqrt(jnp.mean(xv * xv, axis=-1, keepdims=True) + RMS_EPS)
        h_ref[...] = (xv * inv * g_ref[...]).astype(h_ref.dtype)

    return pl.pallas_call(
        body, grid=(s // tm,),
        in_specs=[pl.BlockSpec((tm, d), lambda i: (i, 0)), pl.BlockSpec((1, d), lambda i: (0, 0))],
        out_specs=pl.BlockSpec((tm, d), lambda i: (i, 0)),
        out_shape=SDS((s, d), BF16), name="rmsnorm_fwd", compiler_params=_cp(("parallel",)))(x, g)


def rmsnorm_bwd(dh, x, g, res, *, out_dtype, tm=256, name):
    s, d = x.shape
    n = s // tm
    has_res = res is not None

    def body(*refs):
        if has_res:
            dh_ref, x_ref, g_ref, res_ref, dx_ref, dg_ref, acc = refs
        else:
            dh_ref, x_ref, g_ref, dx_ref, dg_ref, acc = refs
        i = pl.program_id(0)

        @pl.when(i == 0)
        def _():
            acc[...] = jnp.zeros_like(acc)

        xv = x_ref[...]
        inv = lax.rsqrt(jnp.mean(xv * xv, axis=-1, keepdims=True) + RMS_EPS)
        xh = xv * inv
        dhv = dh_ref[...].astype(F32)
        dxh = dhv * g_ref[...]
        dot = jnp.mean(dxh * xh, axis=-1, keepdims=True)
        dx = inv * (dxh - xh * dot)
        if has_res:
            dx = dx + res_ref[...]
        dx_ref[...] = dx.astype(dx_ref.dtype)
        acc[...] += jnp.sum((dhv * xh).reshape(tm // 8, 8, d), axis=0)

        @pl.when(i == n - 1)
        def _():
            dg_ref[...] = jnp.sum(acc[...], axis=0, keepdims=True)

    row = pl.BlockSpec((tm, d), lambda i: (i, 0))
    in_specs = [row, row, pl.BlockSpec((1, d), lambda i: (0, 0))] + ([row] if has_res else [])
    args = [dh, x, g] + ([res] if has_res else [])
    return pl.pallas_call(
        body, grid=(n,), in_specs=in_specs,
        out_specs=[row, pl.BlockSpec((1, d), lambda i: (0, 0))],
        out_shape=[SDS((s, d), out_dtype), SDS((1, d), F32)],
        scratch_shapes=[pltpu.VMEM((8, d), F32)],
        name=name, compiler_params=_cp(("arbitrary",)))(*args)


def mm(a_views, b_views, *, nt, out_dtype, tm, tn, name):
    n_p = len(a_views)
    m = a_views[0][0].shape[0]
    n = b_views[0][0].shape[0] if nt else b_views[0][0].shape[1]

    def body(*refs):
        o_ref = refs[-1]
        acc = None
        for p in range(n_p):
            av = refs[p][...].astype(BF16)
            bv = refs[n_p + p][...].astype(BF16)
            dv = _nt(av, bv) if nt else _nn(av, bv)
            acc = dv if acc is None else acc + dv
        o_ref[...] = acc.astype(o_ref.dtype)

    in_specs = []
    for arr, w, blk in a_views:
        in_specs.append(pl.BlockSpec((tm, w), functools.partial(lambda i, j, blk: (i, blk), blk=blk)))
    for arr, w, blk in b_views:
        if nt:
            in_specs.append(pl.BlockSpec((tn, w), functools.partial(lambda i, j, blk: (j, blk), blk=blk)))
        else:
            in_specs.append(pl.BlockSpec((w, tn), lambda i, j: (0, j)))
    return pl.pallas_call(
        body, grid=(m // tm, n // tn), in_specs=in_specs,
        out_specs=pl.BlockSpec((tm, tn), lambda i, j: (i, j)),
        out_shape=SDS((m, n), out_dtype), name=name,
        compiler_params=_cp(("parallel", "parallel")))(*[a[0] for a in a_views], *[b[0] for b in b_views])


def wgrad(a_view, g, *, tk, tn, ts, name, chip_major=False):
    arr, ka, blk = a_view
    s, n = g.shape
    ns = s // ts

    def body(a_ref, g_ref, o_ref):
        @pl.when(pl.program_id(2) == 0)
        def _():
            o_ref[...] = jnp.zeros_like(o_ref)

        o_ref[...] += _tn(a_ref[...].astype(BF16), g_ref[...].astype(BF16))

    if chip_major:
        out_spec = pl.BlockSpec((None, tk, tn), lambda i, j, k: (j, i, 0))
        out_shape = SDS((n // tn, ka, tn), F32)
    else:
        out_spec = pl.BlockSpec((tk, tn), lambda i, j, k: (i, j))
        out_shape = SDS((ka, n), F32)
    return pl.pallas_call(
        body, grid=(ka // tk, n // tn, ns),
        in_specs=[pl.BlockSpec((ts, tk), lambda i, j, k: (k, blk * (ka // tk) + i)),
                  pl.BlockSpec((ts, tn), lambda i, j, k: (k, j))],
        out_specs=out_spec, out_shape=out_shape, name=name,
        compiler_params=_cp(("parallel", "parallel", "arbitrary")))(arr, g)


def _norm_bwd_rows(dh, xh, inv, g):
    dxh = dh * g
    dx = inv * (dxh - xh * jnp.mean(dxh * xh, axis=-1, keepdims=True))
    return dx, jnp.sum((dh * xh).reshape(dh.shape[0] // 8, 8, dh.shape[1]), axis=0)


def proj_norm_res(a, w, g, xres, g_next, *, tm=512, name):
    s, k = a.shape
    d = w.shape[1]

    def body(a_ref, w_ref, g_ref, x_ref, gn_ref, y_ref, o_ref, h_ref):
        y = _nn(a_ref[...], w_ref[...])
        inv = lax.rsqrt(jnp.mean(y * y, axis=-1, keepdims=True) + RMS_EPS)
        xn = x_ref[...] + y * inv * g_ref[...]
        y_ref[...] = y
        o_ref[...] = xn
        inv_n = lax.rsqrt(jnp.mean(xn * xn, axis=-1, keepdims=True) + RMS_EPS)
        h_ref[...] = (xn * inv_n * gn_ref[...]).astype(h_ref.dtype)

    row = pl.BlockSpec((tm, d), lambda i: (i, 0))
    vec = pl.BlockSpec((1, d), lambda i: (0, 0))
    return pl.pallas_call(
        body, grid=(s // tm,),
        in_specs=[pl.BlockSpec((tm, k), lambda i: (i, 0)), pl.BlockSpec((k, d), lambda i: (0, 0)), vec, row, vec],
        out_specs=[row, row, row], out_shape=[SDS((s, d), F32), SDS((s, d), F32), SDS((s, d), BF16)], name=name,
        compiler_params=_cp(("parallel",)))(a, w, g, xres, g_next)


def proj_norm_loss(a, w, g, xres, target, *, tm=512, name):
    s, k = a.shape
    d = w.shape[1]
    n = s // tm

    def body(a_ref, w_ref, g_ref, x_ref, t_ref, do_ref, dy_ref, dg_ref, l_ref, acc):
        i = pl.program_id(0)

        @pl.when(i == 0)
        def _():
            acc[...] = jnp.zeros_like(acc)
            l_ref[...] = jnp.zeros_like(l_ref)

        y = _nn(a_ref[...], w_ref[...])
        inv = lax.rsqrt(jnp.mean(y * y, axis=-1, keepdims=True) + RMS_EPS)
        yh = y * inv
        err = x_ref[...] + yh * g_ref[...] - t_ref[...]
        dout = err * (1.0 / d)
        do_ref[...] = dout
        l_ref[...] += jnp.sum(jnp.sum(err * err, axis=1, keepdims=True), axis=0, keepdims=True)
        dy, part = _norm_bwd_rows(dout, yh, inv, g_ref[...])
        dy_ref[...] = dy.astype(dy_ref.dtype)
        acc[...] += part

        @pl.when(i == n - 1)
        def _():
            dg_ref[...] = jnp.sum(acc[...], axis=0, keepdims=True)

    row = pl.BlockSpec((tm, d), lambda i: (i, 0))
    vec = pl.BlockSpec((1, d), lambda i: (0, 0))
    return pl.pallas_call(
        body, grid=(n,),
        in_specs=[pl.BlockSpec((tm, k), lambda i: (i, 0)), pl.BlockSpec((k, d), lambda i: (0, 0)), vec, row, row],
        out_specs=[row, row, vec, pl.BlockSpec((1, 1), lambda i: (0, 0))],
        out_shape=[SDS((s, d), F32), SDS((s, d), BF16), SDS((1, d), F32), SDS((1, 1), F32)],
        scratch_shapes=[pltpu.VMEM((8, d), F32)], name=name, compiler_params=_cp(("arbitrary",)))(a, w, g, xres, target)


def mm_norm_bwd(a_views, b_views, stages, exchange=(), *, tm=256, name):
    n_p, n_s, ne = len(a_views), len(stages), len(exchange)
    s = a_views[0][0].shape[0]
    d = b_views[0][0].shape[0]
    n = s // tm
    has_res = [st[2] is not None for st in stages]

    def body(*refs):
        a_refs, b_refs = refs[:n_p], refs[n_p:2 * n_p]
        at = 2 * n_p
        st_refs = []
        for k in range(n_s):
            cnt = 3 if has_res[k] else 2
            st_refs.append(refs[at:at + cnt])
            at += cnt
        e_ins = refs[at:at + ne]
        at += ne
        dx_refs, dg_refs = refs[at:at + n_s], refs[at + n_s:at + 2 * n_s]
        at += 2 * n_s
        e_outs = refs[at:at + ne]
        at += ne
        accs = refs[at:at + n_s]
        comm = (e_ins, e_outs) + tuple(refs[at + n_s:])
        i = pl.program_id(0)

        @pl.when(i == 0)
        def _():
            for acc in accs:
                acc[...] = jnp.zeros_like(acc)
            if ne:
                _to_chips_start(*comm)

        dh = None
        for p in range(n_p):
            part = _nt(a_refs[p][...].astype(BF16), b_refs[p][...].astype(BF16))
            dh = part if dh is None else dh + part
        for k in range(n_s):
            xv = st_refs[k][0][...]
            inv = lax.rsqrt(jnp.mean(xv * xv, axis=-1, keepdims=True) + RMS_EPS)
            dx, part = _norm_bwd_rows(dh, xv * inv, inv, st_refs[k][1][...])
            if has_res[k]:
                dx = dx + st_refs[k][2][...]
            dx_refs[k][...] = dx.astype(dx_refs[k].dtype)
            accs[k][...] += part
            dh = dx

        @pl.when(i == n - 1)
        def _():
            for k in range(n_s):
                dg_refs[k][...] = jnp.sum(accs[k][...], axis=0, keepdims=True)
            if ne:
                _to_chips_finish(*comm)

    row = pl.BlockSpec((tm, d), lambda i: (i, 0))
    vec = pl.BlockSpec((1, d), lambda i: (0, 0))
    in_specs, args = [], []
    for arr, w, blk in a_views:
        in_specs.append(pl.BlockSpec((tm, w), functools.partial(lambda i, blk: (i, blk), blk=blk)))
        args.append(arr)
    for arr, w, blk in b_views:
        in_specs.append(pl.BlockSpec((d, w), functools.partial(lambda i, blk: (0, blk), blk=blk)))
        args.append(arr)
    for x, g, res, _ in stages:
        in_specs += [row, vec] + ([row] if res is not None else [])
        args += [x, g] + ([res] if res is not None else [])
    return pl.pallas_call(
        body, grid=(n,), in_specs=in_specs + [ANY] * ne,
        out_specs=[row] * n_s + [vec] * n_s + [ANY] * ne,
        out_shape=[SDS((s, d), st[3]) for st in stages] + [SDS((1, d), F32)] * n_s + _to_chips_shapes(exchange),
        scratch_shapes=[pltpu.VMEM((8, d), F32)] * n_s + (_to_chips_sems(ne) if ne else []), name=name,
        compiler_params=_cp(("arbitrary",)))(*args, *exchange)


def _split3(v):
    hi = v.astype(BF16).astype(F32)
    r = v - hi
    mid = r.astype(BF16).astype(F32)
    lo = (r - mid).astype(BF16).astype(F32)
    return hi, mid, lo


def _tri(n, upper):
    r = np.arange(n)
    m = (r[:, None] <= r[None, :]) if upper else (r[:, None] >= r[None, :])
    return jnp.asarray(m.astype(np.float32))


def fox_prep(z, fa, bfo, *, tb=512):
    s = z.shape[0]
    n = s // tb

    def body(q_ref, k_ref, fa_ref, b_ref, tri_ref, qa_ref, ka_ref, carry):
        @pl.when(pl.program_id(0) == 0)
        def _():
            carry[...] = jnp.zeros_like(carry)

        xv = fa_ref[...] + b_ref[...]
        logf = jnp.minimum(xv, 0.0) - jnp.log(1.0 + jnp.exp(-jnp.abs(xv)))
        csum = jnp.dot(tri_ref[...], logf, preferred_element_type=F32, precision=lax.Precision.HIGHEST) + carry[0:1, :]
        carry[0:1, :] = csum[tb - 1:tb, :]
        lane = _lane((tb, LANES))
        for h in range(N_HEADS):
            hi, mid, lo = _split3(csum[:, h:h + 1])
            pair = (h // 2) * LANES
            qv = q_ref[:, pair:pair + LANES].astype(F32)
            kv = k_ref[:, pair:pair + LANES].astype(F32)
            if h % 2:
                qv = pltpu.roll(qv, 64, axis=1)
                kv = pltpu.roll(kv, 64, axis=1)
            one = jnp.where((lane >= 67) & (lane < 70), 1.0, 0.0)
            q_x = jnp.where(lane == 64, hi, jnp.where(lane == 65, mid, jnp.where(lane == 66, lo, one)))
            one = jnp.where((lane >= 64) & (lane < 67), 1.0, 0.0)
            k_x = jnp.where(lane == 67, -hi, jnp.where(lane == 68, -mid, jnp.where(lane == 69, -lo, one)))
            qa_ref[:, h * LANES:(h + 1) * LANES] = jnp.where(lane < 64, qv * QK_SCALE, q_x).astype(BF16)
            ka_ref[:, h * LANES:(h + 1) * LANES] = jnp.where(lane < 64, kv, k_x).astype(BF16)

    return pl.pallas_call(
        body, grid=(n,),
        in_specs=[pl.BlockSpec((tb, ATT_W), lambda i: (i, Z_QA)), pl.BlockSpec((tb, ATT_W), lambda i: (i, Z_KA)),
                  pl.BlockSpec((tb, LANES), lambda i: (i, 0)), pl.BlockSpec((1, LANES), lambda i: (0, 0)),
                  pl.BlockSpec((tb, tb), lambda i: (0, 0))],
        out_specs=[pl.BlockSpec((tb, N_HEADS * LANES), lambda i: (i, 0))] * 2,
        out_shape=[SDS((s, N_HEADS * LANES), BF16)] * 2,
        scratch_shapes=[pltpu.VMEM((8, LANES), F32)],
        name="fox_prep", compiler_params=_cp(("arbitrary",)))(z, z, fa, bfo, _tri(tb, False))


def _causal_pairs(n, k_major):
    if k_major:
        pairs = [(qi, kj) for kj in range(n) for qi in range(kj, n)]
    else:
        pairs = [(qi, kj) for qi in range(n) for kj in range(qi + 1)]
    return (jnp.asarray([p[0] for p in pairs], jnp.int32), jnp.asarray([p[1] for p in pairs], jnp.int32), len(pairs))


def fox_fwd(q_aug, k_aug, z, gather=(), *, t=512):
    s = z.shape[0]
    qi_arr, kj_arr, n_pairs = _causal_pairs(s // t, False)
    ng = len(gather)

    def body(qi_ref, kj_ref, q_ref, k_ref, v_ref, *rest):
        g_ins, (o_ref, lse_ref), g_outs = rest[:ng], rest[ng:ng + 2], rest[ng + 2:2 * ng + 2]
        m_scr, l_scr, acc_scr = rest[2 * ng + 2:2 * ng + 5]
        comm = (g_ins, g_outs) + tuple(rest[2 * ng + 5:]) + ([True] * ng,)
        step = pl.program_id(1)
        qi = qi_ref[step]
        kj = kj_ref[step]
        if ng:
            @pl.when((pl.program_id(0) == 0) & (step == 0))
            def _():
                _allgather_start(*comm)

        @pl.when(kj == 0)
        def _():
            m_scr[...] = jnp.full_like(m_scr, NEG)
            l_scr[...] = jnp.zeros_like(l_scr)
            acc_scr[...] = jnp.zeros_like(acc_scr)

        def update(masked):
            for i in range(2):
                sc = _nt(q_ref[:, i * LANES:(i + 1) * LANES], k_ref[:, i * LANES:(i + 1) * LANES])
                if masked:
                    sc = jnp.where(_row((t, t)) >= _lane((t, t)), sc, NEG)
                m_prev = m_scr[i]
                m_new = jnp.maximum(m_prev, jnp.max(sc, axis=-1, keepdims=True))
                alpha = jnp.exp(m_prev - m_new)
                p = jnp.exp(sc - jnp.tile(m_new, (1, t // LANES)))
                l_scr[i] = alpha * l_scr[i] + jnp.sum(p, axis=-1, keepdims=True)
                acc_scr[i] = alpha * acc_scr[i] + _nn(p.astype(BF16), v_ref[...])
                m_scr[i] = m_new

        @pl.when(kj < qi)
        def _():
            update(False)

        @pl.when(kj == qi)
        def _():
            update(True)
            lane = _lane((t, LANES))
            o_ref[...] = jnp.where(lane < 64, acc_scr[0] / l_scr[0], acc_scr[1] / l_scr[1]).astype(o_ref.dtype)
            lse_ref[...] = jnp.where(lane < 64, m_scr[0] + jnp.log(l_scr[0]), m_scr[1] + jnp.log(l_scr[1]))

        if ng:
            @pl.when((pl.program_id(0) == 3) & (step == n_pairs - 1))
            def _():
                _allgather_finish(*comm)

    grid_spec = pltpu.PrefetchScalarGridSpec(
        num_scalar_prefetch=2, grid=(4, n_pairs),
        in_specs=[pl.BlockSpec((t, 2 * LANES), lambda hp, st, qi, kj: (qi[st], hp)),
                  pl.BlockSpec((t, 2 * LANES), lambda hp, st, qi, kj: (kj[st], hp)),
                  pl.BlockSpec((t, LANES), lambda hp, st, qi, kj: (kj[st], 4 * Z_VA + hp))] + [ANY] * ng,
        out_specs=[pl.BlockSpec((t, LANES), lambda hp, st, qi, kj: (qi[st], hp))] * 2 + [ANY] * ng,
        scratch_shapes=[pltpu.VMEM((2, t, LANES), F32)] * 3 + (_allgather_sems(ng) if ng else []))
    return pl.pallas_call(
        body, grid_spec=grid_spec, out_shape=[SDS((s, ATT_W), BF16), SDS((s, ATT_W), F32)] + _allgather_shapes(gather),
        name="fox_fwd", compiler_params=_cp(("arbitrary", "arbitrary")))(qi_arr, kj_arr, q_aug, k_aug, z, *gather)


def fox_bwd(q_aug, k_aug, z, dy, lse, dd, exchange=(), *, t=512):
    s = z.shape[0]
    qi_arr, kj_arr, n_pairs = _causal_pairs(s // t, True)
    ne = len(exchange)

    def body(qi_ref, kj_ref, q_ref, k_ref, v_ref, do_ref, lse_ref, dd_ref, *rest):
        e_ins, (dq_ref, dk_ref, dv_ref), e_outs = rest[:ne], rest[ne:ne + 3], rest[ne + 3:2 * ne + 3]
        comm = (e_ins, e_outs) + tuple(rest[2 * ne + 3:])
        step = pl.program_id(1)
        qi = qi_ref[step]
        kj = kj_ref[step]
        if ne:
            @pl.when((pl.program_id(0) == 0) & (step == 0))
            def _():
                _to_chips_start(*comm)

        @pl.when(step == 0)
        def _():
            dq_ref[...] = jnp.zeros_like(dq_ref)

        @pl.when(qi == kj)
        def _():
            dk_ref[...] = jnp.zeros_like(dk_ref)
            dv_ref[...] = jnp.zeros_like(dv_ref)

        def update(masked):
            lane = _lane((t, LANES))
            rows = pl.ds(pl.multiple_of(qi * t, t), t)
            dov = do_ref[...]
            dv_new = None
            for i in range(2):
                head = (lane < 64) if i == 0 else (lane >= 64)
                qv = q_ref[:, i * LANES:(i + 1) * LANES]
                kv = k_ref[:, i * LANES:(i + 1) * LANES]
                sc = _nt(qv, kv)
                if masked:
                    sc = jnp.where(_row((t, t)) >= _lane((t, t)), sc, NEG)
                p = jnp.exp(sc - lse_ref[:, i * 64:i * 64 + 1])
                dp = _nt(jnp.where(head, dov, jnp.zeros_like(dov)), v_ref[...])
                ds = (p * (dp - dd_ref[:, i * 64:i * 64 + 1])).astype(BF16)
                dq_ref[rows, i * LANES:(i + 1) * LANES] += _nn(ds, kv)
                dk_ref[:, i * LANES:(i + 1) * LANES] += _tn(ds, qv)
                dvi = _tn(p.astype(BF16), dov)
                dv_new = dvi if dv_new is None else jnp.where(head, dvi, dv_new)
            dv_ref[...] += dv_new

        @pl.when(kj < qi)
        def _():
            update(False)

        @pl.when(kj == qi)
        def _():
            update(True)

        if ne:
            @pl.when((pl.program_id(0) == 3) & (step == n_pairs - 1))
            def _():
                _to_chips_finish(*comm)

    grid_spec = pltpu.PrefetchScalarGridSpec(
        num_scalar_prefetch=2, grid=(4, n_pairs),
        in_specs=[pl.BlockSpec((t, 2 * LANES), lambda hp, st, qi, kj: (qi[st], hp)),
                  pl.BlockSpec((t, 2 * LANES), lambda hp, st, qi, kj: (kj[st], hp)),
                  pl.BlockSpec((t, LANES), lambda hp, st, qi, kj: (kj[st], 4 * Z_VA + hp)),
                  pl.BlockSpec((t, LANES), lambda hp, st, qi, kj: (qi[st], hp)),
                  pl.BlockSpec((t, LANES), lambda hp, st, qi, kj: (qi[st], hp)),
                  pl.BlockSpec((t, LANES), lambda hp, st, qi, kj: (qi[st], hp))] + [ANY] * ne,
        out_specs=[pl.BlockSpec((s, 2 * LANES), lambda hp, st, qi, kj: (0, hp)),
                   pl.BlockSpec((t, 2 * LANES), lambda hp, st, qi, kj: (kj[st], hp)),
                   pl.BlockSpec((t, LANES), lambda hp, st, qi, kj: (kj[st], hp))] + [ANY] * ne,
        scratch_shapes=_to_chips_sems(ne) if ne else [])
    return pl.pallas_call(
        body, grid_spec=grid_spec,
        out_shape=[SDS((s, N_HEADS * LANES), F32), SDS((s, N_HEADS * LANES), F32), SDS((s, ATT_W), F32)]
        + _to_chips_shapes(exchange),
        name="fox_bwd", compiler_params=_cp(("arbitrary", "arbitrary")))(qi_arr, kj_arr, q_aug, k_aug, z, dy, lse, dd, *exchange)


def head_rowsum(a, b, *, tm=512, name):
    s = a.shape[0]

    def body(a_ref, b_ref, o_ref):
        prod = a_ref[...].astype(F32) * b_ref[...].astype(F32)
        lane = _lane((tm, LANES))
        lo = jnp.sum(jnp.where(lane < 64, prod, 0.0), axis=-1, keepdims=True)
        hi = jnp.sum(jnp.where(lane >= 64, prod, 0.0), axis=-1, keepdims=True)
        o_ref[...] = jnp.where(lane < 64, lo, hi)

    blk = pl.BlockSpec((tm, LANES), lambda i, j: (i, j))
    return pl.pallas_call(body, grid=(s // tm, 4), in_specs=[blk, blk], out_specs=blk, out_shape=SDS((s, ATT_W), F32),
                          name=name, compiler_params=_cp(("parallel", "parallel")))(a, b)


def fox_post(dq_aug, dk_aug, dv, fa, bfo, *, tb=512):
    s = dv.shape[0]
    n = s // tb

    def body(dq_ref, dk_ref, dv_ref, fa_ref, b_ref, tri_ref, dz_ref, dfa_ref, gb_ref, carry, acc):
        i = pl.program_id(0)

        @pl.when(i == 0)
        def _():
            carry[...] = jnp.zeros_like(carry)
            acc[...] = jnp.zeros_like(acc)

        lane = _lane((tb, LANES))
        d_f = jnp.zeros((tb, LANES), F32)
        for h in range(N_HEADS):
            col = dq_ref[:, h * LANES + 64:h * LANES + 65] - dk_ref[:, h * LANES + 67:h * LANES + 68]
            d_f = jnp.where(lane == h, col, d_f)
        suffix = jnp.dot(tri_ref[...], d_f, preferred_element_type=F32, precision=lax.Precision.HIGHEST) + carry[0:1, :]
        carry[0:1, :] = suffix[0:1, :]
        xv = fa_ref[...] + b_ref[...]
        dx = suffix * (1.0 / (1.0 + jnp.exp(xv)))
        dfa_ref[...] = dx.astype(dfa_ref.dtype)
        acc[...] += jnp.sum(dx.reshape(tb // 8, 8, LANES), axis=0)
        for hp in range(4):
            for src, off, scale in ((dq_ref, 0, QK_SCALE), (dk_ref, ATT_W, 1.0)):
                even = src[:, (2 * hp) * LANES:(2 * hp + 1) * LANES]
                odd = pltpu.roll(src[:, (2 * hp + 1) * LANES:(2 * hp + 2) * LANES], 64, axis=1)
                dz_ref[:, off + hp * LANES:off + (hp + 1) * LANES] = (jnp.where(lane < 64, even, odd) * scale).astype(BF16)
        dz_ref[:, 2 * ATT_W:3 * ATT_W] = dv_ref[...].astype(BF16)

        @pl.when(i == n - 1)
        def _():
            gb_ref[...] = jnp.sum(acc[...], axis=0, keepdims=True)

    rev = lambda i: (n - 1 - i, 0)
    return pl.pallas_call(
        body, grid=(n,),
        in_specs=[pl.BlockSpec((tb, N_HEADS * LANES), rev), pl.BlockSpec((tb, N_HEADS * LANES), rev),
                  pl.BlockSpec((tb, ATT_W), rev), pl.BlockSpec((tb, LANES), rev),
                  pl.BlockSpec((1, LANES), lambda i: (0, 0)), pl.BlockSpec((tb, tb), lambda i: (0, 0))],
        out_specs=[pl.BlockSpec((tb, 3 * ATT_W), rev), pl.BlockSpec((tb, LANES), rev),
                   pl.BlockSpec((1, LANES), lambda i: (0, 0))],
        out_shape=[SDS((s, 3 * ATT_W), BF16), SDS((s, LANES), BF16), SDS((1, LANES), F32)],
        scratch_shapes=[pltpu.VMEM((8, LANES), F32), pltpu.VMEM((8, LANES), F32)],
        name="fox_post", compiler_params=_cp(("arbitrary",)))(dq_aug, dk_aug, dv, fa, bfo, _tri(tb, True))


def rope_tables(s, sign):
    half = ROPE_DIM // 2
    inv_freq = ROPE_THETA ** (-jnp.arange(half, dtype=F32) * 2.0 / ROPE_DIM)
    ang = jnp.arange(s, dtype=F32)[:, None] * inv_freq[None, :]
    l64 = np.arange(LANES) % HEAD_DIM
    cos = jnp.cos(ang)[:, l64 % half]
    sin = jnp.sin(ang)[:, l64 % half] * sign
    first = jnp.asarray(l64 < half)[None, :]
    second = jnp.asarray((l64 >= half) & (l64 < ROPE_DIM))[None, :]
    return (jnp.where(first | second, cos, 1.0), jnp.where(first, -sin, 0.0), jnp.where(second, sin, 0.0))


def rope_apply(items, tabs, *, out_dtype, tm=512, name):
    s = items[0][0].shape[0]
    n_i = len(items)

    def body(*refs):
        c_ref, sn_ref, sp_ref = refs[n_i:n_i + 3]
        o_ref = refs[-1]
        for j, (_, _, scale, rotate) in enumerate(items):
            for b in range(4):
                xv = refs[j][:, b * LANES:(b + 1) * LANES].astype(F32)
                if rotate:
                    xv = xv * c_ref[...] + pltpu.roll(xv, LANES - 8, axis=1) * sn_ref[...] + pltpu.roll(xv, 8, axis=1) * sp_ref[...]
                o_ref[:, j * ATT_W + b * LANES:j * ATT_W + (b + 1) * LANES] = (xv * scale).astype(o_ref.dtype)

    in_specs = [pl.BlockSpec((tm, ATT_W), functools.partial(lambda i, blk: (i, blk), blk=it[1])) for it in items]
    in_specs += [pl.BlockSpec((tm, LANES), lambda i: (i, 0))] * 3
    return pl.pallas_call(
        body, grid=(s // tm,), in_specs=in_specs, out_specs=pl.BlockSpec((tm, n_i * ATT_W), lambda i: (i, 0)),
        out_shape=SDS((s, n_i * ATT_W), out_dtype), name=name, compiler_params=_cp(("parallel",)))(*[it[0] for it in items], *tabs)


def _dil_views(qk, z, r):
    s = z.shape[0]
    return qk.reshape(s // r, r * 2 * ATT_W), z.reshape(s // r, r * Z_W)


def _dil_cols(r):
    q_col = lambda rho, hp: rho * 8 + hp
    k_col = lambda rho, hp: rho * 8 + 4 + hp
    v_col = lambda rho, hp: rho * (Z_W // LANES) + 4 * Z_VB + hp
    return q_col, k_col, v_col


def _dil_scores(qv, kp, kc, head, has_prev):
    b = DIL_BLK
    qm = jnp.where(head, qv, jnp.zeros_like(qv))
    row, col = _row((b, b)), _lane((b, b))
    sp = jnp.where((col >= row) & has_prev, _nt(qm, kp), NEG)
    sc = jnp.where(col <= row, _nt(qm, kc), NEG)
    return sp, sc


def dil_fwd(qk, z, prev, *, r):
    s = z.shape[0]
    b = DIL_BLK
    l_sub = s // r
    nb = l_sub // b
    qk_v, z_v = _dil_views(qk, z, r)
    q_col, k_col, v_col = _dil_cols(r)
    merge = prev is not None

    def body(*refs):
        if merge:
            q_ref, kp_ref, kc_ref, vp_ref, vc_ref, op_ref, lp_ref, o_ref, l_ref = refs
        else:
            q_ref, kp_ref, kc_ref, vp_ref, vc_ref, o_ref, l_ref = refs
        has_prev = pl.program_id(2) > 0
        lane = _lane((b, LANES))
        res = []
        for i in range(2):
            head = (lane < 64) if i == 0 else (lane >= 64)
            sp, sc = _dil_scores(q_ref[...], kp_ref[...], kc_ref[...], head, has_prev)
            m = jnp.maximum(jnp.max(sp, axis=-1, keepdims=True), jnp.max(sc, axis=-1, keepdims=True))
            pp = jnp.exp(sp - m)
            pc = jnp.exp(sc - m)
            den = jnp.sum(pp, axis=-1, keepdims=True) + jnp.sum(pc, axis=-1, keepdims=True)
            ov = (_nn(pp.astype(BF16), vp_ref[...]) + _nn(pc.astype(BF16), vc_ref[...])) / den
            res.append((ov, m + jnp.log(den)))
        ov = jnp.where(lane < 64, res[0][0], res[1][0])
        lse = jnp.where(lane < 64, res[0][1], res[1][1])
        if merge:
            lp = lp_ref[...]
            m2 = jnp.maximum(lp, lse)
            wp = jnp.exp(lp - m2)
            wn = jnp.exp(lse - m2)
            ov = (wp * op_ref[...] + wn * ov) / (wp + wn)
            lse = m2 + jnp.log(wp + wn)
        o_ref[...] = ov
        l_ref[...] = lse

    blk = lambda f: pl.BlockSpec((b, LANES), f)
    in_specs = [blk(lambda rho, hp, n: (n, q_col(rho, hp))), blk(lambda rho, hp, n: (jnp.maximum(n - 1, 0), k_col(rho, hp))),
                blk(lambda rho, hp, n: (n, k_col(rho, hp))), blk(lambda rho, hp, n: (jnp.maximum(n - 1, 0), v_col(rho, hp))),
                blk(lambda rho, hp, n: (n, v_col(rho, hp)))]
    args = [qk_v, qk_v, qk_v, z_v, z_v]
    nat = blk(lambda rho, hp, n: (n, rho * 4 + hp))
    if merge:
        in_specs += [nat, nat]
        args += [prev[0].reshape(l_sub, r * ATT_W), prev[1].reshape(l_sub, r * ATT_W)]
    o, lse = pl.pallas_call(
        body, grid=(r, 4, nb), in_specs=in_specs, out_specs=[nat, nat],
        out_shape=[SDS((l_sub, r * ATT_W), F32)] * 2, name=f"dil_fwd_r{r}",
        compiler_params=_cp(("parallel", "parallel", "arbitrary")))(*args)
    return o.reshape(s, ATT_W), lse.reshape(s, ATT_W)


def dil_bwd_dq(qk, z, dy, lse, dd, acc, *, r):
    s = z.shape[0]
    b = DIL_BLK
    l_sub = s // r
    nb = l_sub // b
    qk_v, z_v = _dil_views(qk, z, r)
    q_col, k_col, v_col = _dil_cols(r)
    add = acc is not None

    def body(*refs):
        q_ref, kp_ref, kc_ref, vp_ref, vc_ref, do_ref, l_ref, dd_ref = refs[:8]
        dq_ref = refs[-1]
        has_prev = pl.program_id(2) > 0
        lane = _lane((b, LANES))
        dov = do_ref[...]
        parts = []
        for i in range(2):
            head = (lane < 64) if i == 0 else (lane >= 64)
            sp, sc = _dil_scores(q_ref[...], kp_ref[...], kc_ref[...], head, has_prev)
            lse_i = l_ref[:, i * 64:i * 64 + 1]
            dd_i = dd_ref[:, i * 64:i * 64 + 1]
            dom = jnp.where(head, dov, jnp.zeros_like(dov))
            dsp = (jnp.exp(sp - lse_i) * (_nt(dom, vp_ref[...]) - dd_i)).astype(BF16)
            dsc = (jnp.exp(sc - lse_i) * (_nt(dom, vc_ref[...]) - dd_i)).astype(BF16)
            parts.append(_nn(dsp, kp_ref[...]) + _nn(dsc, kc_ref[...]))
        dq = jnp.where(lane < 64, parts[0], parts[1])
        if add:
            dq = dq + refs[8][...]
        dq_ref[...] = dq

    blk = lambda f: pl.BlockSpec((b, LANES), f)
    nat = blk(lambda rho, hp, n: (n, rho * 4 + hp))
    in_specs = [blk(lambda rho, hp, n: (n, q_col(rho, hp))), blk(lambda rho, hp, n: (jnp.maximum(n - 1, 0), k_col(rho, hp))),
                blk(lambda rho, hp, n: (n, k_col(rho, hp))), blk(lambda rho, hp, n: (jnp.maximum(n - 1, 0), v_col(rho, hp))),
                blk(lambda rho, hp, n: (n, v_col(rho, hp))), nat, nat, nat]
    nview = lambda a: a.reshape(l_sub, r * ATT_W)
    args = [qk_v, qk_v, qk_v, z_v, z_v, nview(dy), nview(lse), nview(dd)]
    if add:
        in_specs.append(nat)
        args.append(nview(acc))
    dq = pl.pallas_call(
        body, grid=(r, 4, nb), in_specs=in_specs, out_specs=nat, out_shape=SDS((l_sub, r * ATT_W), F32),
        name=f"dil_bwd_dq_r{r}", compiler_params=_cp(("parallel", "parallel", "arbitrary")))(*args)
    return dq.reshape(s, ATT_W)


def dil_bwd_dkv(qk, z, dy, lse, dd, acc, *, r):
    s = z.shape[0]
    b = DIL_BLK
    l_sub = s // r
    nb = l_sub // b
    qk_v, z_v = _dil_views(qk, z, r)
    q_col, k_col, v_col = _dil_cols(r)
    add = acc is not None

    def body(*refs):
        k_ref, v_ref, qc_ref, qn_ref, doc_ref, don_ref, lc_ref, ln_ref, ddc_ref, ddn_ref = refs[:10]
        dk_ref, dv_ref = refs[-2:]
        has_next = pl.program_id(2) < nb - 1
        lane = _lane((b, LANES))
        row, col = _row((b, b)), _lane((b, b))
        kv = k_ref[...]
        vv = v_ref[...]
        dk_parts, dv_parts = [], []
        for i in range(2):
            head = (lane < 64) if i == 0 else (lane >= 64)
            dk_i = jnp.zeros((b, LANES), F32)
            dv_i = jnp.zeros((b, LANES), F32)
            for q_ref, do_ref, l_ref, d_ref, mask in ((qc_ref, doc_ref, lc_ref, ddc_ref, col <= row),
                                                      (qn_ref, don_ref, ln_ref, ddn_ref, (col >= row) & has_next)):
                qv = q_ref[...]
                dov = do_ref[...]
                sc = jnp.where(mask, _nt(jnp.where(head, qv, jnp.zeros_like(qv)), kv), NEG)
                p = jnp.exp(sc - l_ref[:, i * 64:i * 64 + 1])
                dp = _nt(jnp.where(head, dov, jnp.zeros_like(dov)), vv)
                ds = (p * (dp - d_ref[:, i * 64:i * 64 + 1])).astype(BF16)
                dv_i = dv_i + _tn(p.astype(BF16), dov)
                dk_i = dk_i + _tn(ds, qv)
            dk_parts.append(dk_i)
            dv_parts.append(dv_i)
        dk = jnp.where(lane < 64, dk_parts[0], dk_parts[1])
        dv = jnp.where(lane < 64, dv_parts[0], dv_parts[1])
        if add:
            dk = dk + refs[10][...]
            dv = dv + refs[11][...]
        dk_ref[...] = dk
        dv_ref[...] = dv

    blk = lambda f: pl.BlockSpec((b, LANES), f)
    nat = blk(lambda rho, hp, n: (n, rho * 4 + hp))
    nxt = blk(lambda rho, hp, n: (jnp.minimum(n + 1, nb - 1), rho * 4 + hp))
    in_specs = [blk(lambda rho, hp, n: (n, k_col(rho, hp))), blk(lambda rho, hp, n: (n, v_col(rho, hp))),
                blk(lambda rho, hp, n: (n, q_col(rho, hp))), blk(lambda rho, hp, n: (jnp.minimum(n + 1, nb - 1), q_col(rho, hp))),
                nat, nxt, nat, nxt, nat, nxt]
    nview = lambda a: a.reshape(l_sub, r * ATT_W)
    args = [qk_v, z_v, qk_v, qk_v, nview(dy), nview(dy), nview(lse), nview(lse), nview(dd), nview(dd)]
    if add:
        in_specs += [nat, nat]
        args += [nview(acc[0]), nview(acc[1])]
    dk, dv = pl.pallas_call(
        body, grid=(r, 4, nb), in_specs=in_specs, out_specs=[nat, nat],
        out_shape=[SDS((l_sub, r * ATT_W), F32)] * 2, name=f"dil_bwd_dkv_r{r}",
        compiler_params=_cp(("parallel", "parallel", "arbitrary")))(*args)
    return dk.reshape(s, ATT_W), dv.reshape(s, ATT_W)


def _dil_rows(base, r):
    if r == 1:
        return pl.ds(pl.multiple_of(base, DIL_BLK), DIL_BLK)
    return pl.ds(base, DIL_BLK, stride=r)


def _dil_block(idx, r, nb):
    shift = nb.bit_length() - 1
    rho = idx >> shift
    n = idx & (nb - 1)
    base = rho + n * (r * DIL_BLK)
    return _dil_rows(base, r), _dil_rows(jnp.maximum(base - r * DIL_BLK, rho), r), n > 0


def _cat(a, b):
    return jnp.concatenate([a, b], axis=0)


def _two_heads(v, first_head):
    zero = jnp.zeros_like(v)
    return _cat(jnp.where(first_head, v, zero), jnp.where(first_head, zero, v))


def _dil_bands():
    b = DIL_BLK
    q = _row((2 * b, 2 * b)) & (b - 1)
    col = _lane((2 * b, 2 * b))
    return (col < b) & (col >= q), (col >= b) & (col - b <= q)


def dil_fwd_all(qkv, *, unroll=8):
    s = qkv.shape[0]
    b = DIL_BLK
    n_blk = s // b

    def body(q_ref, k_ref, v_ref, o_ref, l_ref):
        first_head = _lane((b, LANES)) < 64
        band_prev, band_cur = _dil_bands()
        for g, (_, r) in enumerate(DIL_PATTERNS):
            nb = n_blk // r

            def group(it, carry, g=g, r=r, nb=nb):
                loaded = []
                for u in range(unroll):
                    rows_c, rows_p, has_prev = _dil_block(it * unroll + u, r, nb)
                    vals = [q_ref[rows_c, :].astype(BF16), k_ref[rows_p, :].astype(BF16), k_ref[rows_c, :].astype(BF16),
                            v_ref[rows_p, :].astype(BF16), v_ref[rows_c, :].astype(BF16)]
                    state = (o_ref[rows_c, :], l_ref[rows_c, :]) if g else None
                    loaded.append((rows_c, has_prev, vals, state))
                done = []
                for rows_c, has_prev, (qv, kp, kc, vp, vc), state in loaded:
                    sc = jnp.where(band_cur | (band_prev & has_prev), _nt(_two_heads(qv, first_head), _cat(kp, kc)), NEG)
                    m = jnp.max(sc, axis=-1, keepdims=True)
                    p = jnp.exp(sc - m)
                    den = jnp.sum(p, axis=-1, keepdims=True)
                    both = _nn(p.astype(BF16), _cat(vp, vc)) / den
                    lse2 = m + jnp.log(den)
                    ov = jnp.where(first_head, both[:b], both[b:])
                    lse = jnp.where(first_head, lse2[:b], lse2[b:])
                    if state is not None:
                        m2 = jnp.maximum(state[1], lse)
                        wp = jnp.exp(state[1] - m2)
                        wn = jnp.exp(lse - m2)
                        ov = (wp * state[0] + wn * ov) / (wp + wn)
                        lse = m2 + jnp.log(wp + wn)
                    done.append((rows_c, ov, lse))
                for rows_c, ov, lse in done:
                    o_ref[rows_c, :] = ov
                    l_ref[rows_c, :] = lse
                return carry

            lax.fori_loop(0, n_blk // unroll, group, 0)

    col_blk = lambda k: pl.BlockSpec((s, LANES), lambda hp: (0, 4 * k + hp))
    out = pl.BlockSpec((s, LANES), lambda hp: (0, hp))
    return pl.pallas_call(
        body, grid=(4,), in_specs=[col_blk(0), col_blk(1), col_blk(2)], out_specs=[out, out],
        out_shape=[SDS((s, ATT_W), F32)] * 2, name="dil_fwd", compiler_params=_cp(("parallel",)))(qkv, qkv, qkv)


def dil_bwd_all(qkv, dy, lse, y, exchange=(), *, unroll=8):
    s = qkv.shape[0]
    b = DIL_BLK
    n_blk = s // b
    ne = len(exchange)

    def body(q_ref, k_ref, v_ref, do_ref, l_ref, y_ref, *rest):
        e_ins, (dq_ref, dk_ref, dv_ref), e_outs = rest[:ne], rest[ne:ne + 3], rest[ne + 3:2 * ne + 3]
        comm = (e_ins, e_outs) + tuple(rest[2 * ne + 3:])
        if ne:
            @pl.when(pl.program_id(0) == 0)
            def _():
                _to_chips_start(*comm)

        dq_ref[...] = jnp.zeros_like(dq_ref)
        dk_ref[...] = jnp.zeros_like(dk_ref)
        dv_ref[...] = jnp.zeros_like(dv_ref)
        first_head = _lane((b, LANES)) < 64
        band_prev, band_cur = _dil_bands()
        for _, r in DIL_PATTERNS:
            nb = n_blk // r

            def group(it, carry, r=r, nb=nb):
                loaded = []
                for u in range(unroll):
                    rows_c, rows_p, has_prev = _dil_block(it * unroll + u, r, nb)
                    vals = [q_ref[rows_c, :].astype(BF16), k_ref[rows_p, :].astype(BF16), k_ref[rows_c, :].astype(BF16),
                            v_ref[rows_p, :].astype(BF16), v_ref[rows_c, :].astype(BF16), do_ref[rows_c, :],
                            l_ref[rows_c, :], y_ref[rows_c, :]]
                    loaded.append((rows_c, rows_p, has_prev, vals))
                done = []
                for rows_c, rows_p, has_prev, (qv, kp, kc, vp, vc, dof, lv, yv) in loaded:
                    q2 = _two_heads(qv, first_head)
                    do2 = _two_heads(dof.astype(BF16), first_head)
                    kcat, vcat = _cat(kp, kc), _cat(vp, vc)
                    lse2 = _cat(lv[:, 0:1], lv[:, 64:65])
                    dd2 = jnp.sum(_two_heads(dof * yv, first_head), axis=-1, keepdims=True)
                    p = jnp.exp(jnp.where(band_cur | (band_prev & has_prev), _nt(q2, kcat), NEG) - lse2)
                    ds = (p * (_nt(do2, vcat) - dd2)).astype(BF16)
                    dq2 = _nn(ds, kcat)
                    dkcat = _tn(ds, q2)
                    dvcat = _tn(p.astype(BF16), do2)
                    done.append((rows_c, rows_p, (jnp.where(first_head, dq2[:b], dq2[b:]), dkcat[:b], dkcat[b:],
                                                  dvcat[:b], dvcat[b:])))
                for rows_c, rows_p, (dq, dk_p, dk_c, dv_p, dv_c) in done:
                    dq_ref[rows_c, :] += dq
                    dk_ref[rows_p, :] += dk_p
                    dk_ref[rows_c, :] += dk_c
                    dv_ref[rows_p, :] += dv_p
                    dv_ref[rows_c, :] += dv_c
                return carry

            lax.fori_loop(0, n_blk // unroll, group, 0)

        if ne:
            @pl.when(pl.program_id(0) == 3)
            def _():
                _to_chips_finish(*comm)

    col_blk = lambda k: pl.BlockSpec((s, LANES), lambda hp: (0, 4 * k + hp))
    nat = pl.BlockSpec((s, LANES), lambda hp: (0, hp))
    return pl.pallas_call(
        body, grid=(4,), in_specs=[col_blk(0), col_blk(1), col_blk(2), nat, nat, nat] + [ANY] * ne,
        out_specs=[nat, nat, nat] + [ANY] * ne, out_shape=[SDS((s, ATT_W), F32)] * 3 + _to_chips_shapes(exchange),
        scratch_shapes=_to_chips_sems(ne) if ne else [], name="dil_bwd",
        compiler_params=_cp(("arbitrary",)))(qkv, qkv, qkv, dy, lse, y, *exchange)


def _sigmoid(v):
    return 1.0 / (1.0 + jnp.exp(-v))


def gate_mix(ya, yb, wa, wb, z, *, tm=512, tn=512):
    s = ya.shape[0]
    d = wa.shape[1]
    ga_blk = 3 * ATT_W * 2 // tn
    gb_blk = ga_blk + d // tn

    def body(ya_ref, yb_ref, wa_ref, wb_ref, ga_ref, gb_ref, pa_ref, pb_ref, mx_ref):
        pa = _nn(ya_ref[...], wa_ref[...])
        pb = _nn(yb_ref[...].astype(BF16), wb_ref[...])
        pa_ref[...] = pa.astype(BF16)
        pb_ref[...] = pb.astype(BF16)
        mx_ref[...] = (_sigmoid(ga_ref[...].astype(F32)) * pa + _sigmoid(gb_ref[...].astype(F32)) * pb).astype(BF16)

    out = pl.BlockSpec((tm, tn), lambda i, j: (i, j))
    return pl.pallas_call(
        body, grid=(s // tm, d // tn),
        in_specs=[pl.BlockSpec((tm, ATT_W), lambda i, j: (i, 0)), pl.BlockSpec((tm, ATT_W), lambda i, j: (i, 0)),
                  pl.BlockSpec((ATT_W, tn), lambda i, j: (0, j)), pl.BlockSpec((ATT_W, tn), lambda i, j: (0, j)),
                  pl.BlockSpec((tm, tn), lambda i, j: (i, ga_blk + j)), pl.BlockSpec((tm, tn), lambda i, j: (i, gb_blk + j))],
        out_specs=[out, out, out], out_shape=[SDS((s, d), BF16)] * 3, name="gate_mix",
        compiler_params=_cp(("parallel", "parallel")))(ya, yb, wa, wb, z, z)


def gate_bwd(dmx, z, pa, pb, *, tm=256):
    s, d = dmx.shape

    def body(dm_ref, ga_ref, gb_ref, pa_ref, pb_ref, dpa_ref, dpb_ref, dg_ref):
        dm = dm_ref[...].astype(F32)
        sa = _sigmoid(ga_ref[...].astype(F32))
        sb = _sigmoid(gb_ref[...].astype(F32))
        dpa_ref[...] = (dm * sa).astype(BF16)
        dpb_ref[...] = (dm * sb).astype(BF16)
        dg_ref[:, 0:d] = (dm * pa_ref[...].astype(F32) * sa * (1.0 - sa)).astype(BF16)
        dg_ref[:, d:2 * d] = (dm * pb_ref[...].astype(F32) * sb * (1.0 - sb)).astype(BF16)

    row = pl.BlockSpec((tm, d), lambda i: (i, 0))
    return pl.pallas_call(
        body, grid=(s // tm,),
        in_specs=[row, pl.BlockSpec((tm, d), lambda i: (i, 3)), pl.BlockSpec((tm, d), lambda i: (i, 4)), row, row],
        out_specs=[row, row, pl.BlockSpec((tm, 2 * d), lambda i: (i, 0))],
        out_shape=[SDS((s, d), BF16), SDS((s, d), BF16), SDS((s, 2 * d), BF16)], name="gate_bwd",
        compiler_params=_cp(("parallel",)))(dmx, z, z, pa, pb)


GELU_C = math.sqrt(2.0 / math.pi)


def _gelu_parts(a):
    inner = GELU_C * (a + 0.044715 * a * a * a)
    th = jnp.tanh(inner)
    gelu = 0.5 * a * (1.0 + th)
    dgelu = 0.5 * (1.0 + th) + 0.5 * a * (1.0 - th * th) * GELU_C * (1.0 + 3.0 * 0.044715 * a * a)
    return gelu, dgelu


def _causal_taps(u, before):
    row = _row(u.shape)
    r1 = jnp.where(row == 0, before[7:8, :], pltpu.roll(u, 1, axis=0))
    r2 = jnp.where(row == 0, before[6:7, :], jnp.where(row == 1, before[7:8, :], pltpu.roll(u, 2, axis=0)))
    return r1, r2


def ffn_up(h, wa, wb, cw, cb, *, tm=512, tn=256):
    s, d = h.shape
    f = wa.shape[1]
    nj = f // tn

    def body(h_ref, wa_ref, wb_ref, cwa_ref, cwb_ref, cba_ref, cbb_ref, ua_ref, ub_ref, m_ref, carry):
        @pl.when(pl.program_id(1) == 0)
        def _():
            carry[...] = jnp.zeros_like(carry)

        conv = []
        for k, (w_ref, cw_ref, cb_ref, u_ref) in enumerate(((wa_ref, cwa_ref, cba_ref, ua_ref), (wb_ref, cwb_ref, cbb_ref, ub_ref))):
            u16 = _nn(h_ref[...], w_ref[...]).astype(BF16)
            u_ref[...] = u16
            u = u16.astype(F32)
            r1, r2 = _causal_taps(u, carry[k])
            carry[k] = u[tm - 8:tm, :]
            conv.append(cw_ref[0:1, :] * r2 + cw_ref[1:2, :] * r1 + cw_ref[2:3, :] * u + cb_ref[...])
        m_ref[...] = (_gelu_parts(conv[0])[0] * conv[1]).astype(BF16)

    out = pl.BlockSpec((tm, tn), lambda j, i: (i, j))
    return pl.pallas_call(
        body, grid=(nj, s // tm),
        in_specs=[pl.BlockSpec((tm, d), lambda j, i: (i, 0)),
                  pl.BlockSpec((d, tn), lambda j, i: (0, j)), pl.BlockSpec((d, tn), lambda j, i: (0, j)),
                  pl.BlockSpec((3, tn), lambda j, i: (0, j)), pl.BlockSpec((3, tn), lambda j, i: (0, nj + j)),
                  pl.BlockSpec((1, tn), lambda j, i: (0, j)), pl.BlockSpec((1, tn), lambda j, i: (0, nj + j))],
        out_specs=[out, out, out], out_shape=[SDS((s, f), BF16)] * 3,
        scratch_shapes=[pltpu.VMEM((2, 8, tn), F32)], name="ffn_up",
        compiler_params=_cp(("parallel", "arbitrary")))(h, wa, wb, cw, cw, cb, cb)


def ffn_bwd(dm, ua, ub, cw, cb, *, tm=512, tn=256):
    s, f = dm.shape
    nj = f // tn
    ni = s // tm
    halo = 16

    def body(dm_ref, ua_ref, ub_ref, ha_ref, hb_ref, cwa_ref, cwb_ref, cba_ref, cbb_ref,
             dua_ref, dub_ref, ga_ref, gb_ref, carry):
        i = pl.program_id(1)

        @pl.when(i == 0)
        def _():
            carry[...] = jnp.zeros_like(carry)
            ga_ref[...] = jnp.zeros_like(ga_ref)
            gb_ref[...] = jnp.zeros_like(gb_ref)

        first_tile = i == ni - 1
        row = _row((tm, tn))
        dmv = dm_ref[...].astype(F32)
        us, taps, convs = [], [], []
        for u_ref, h_ref, cw_ref, cb_ref in ((ua_ref, ha_ref, cwa_ref, cba_ref), (ub_ref, hb_ref, cwb_ref, cbb_ref)):
            u = u_ref[...].astype(F32)
            before = jnp.where(first_tile, 0.0, h_ref[halo - 8:halo, :].astype(F32))
            r1, r2 = _causal_taps(u, before)
            us.append(u)
            taps.append((r1, r2))
            convs.append(cw_ref[0:1, :] * r2 + cw_ref[1:2, :] * r1 + cw_ref[2:3, :] * u + cb_ref[...])
        gelu, dgelu = _gelu_parts(convs[0])
        dcs = (dmv * convs[1] * dgelu, dmv * gelu)
        for k, (dc, cw_ref, du_ref, g_ref) in enumerate(((dcs[0], cwa_ref, dua_ref, ga_ref), (dcs[1], cwb_ref, dub_ref, gb_ref))):
            r1, r2 = taps[k]
            g_ref[0:1, :] += jnp.sum(dc * r2, axis=0, keepdims=True)
            g_ref[1:2, :] += jnp.sum(dc * r1, axis=0, keepdims=True)
            g_ref[2:3, :] += jnp.sum(dc * us[k], axis=0, keepdims=True)
            g_ref[3:4, :] += jnp.sum(dc, axis=0, keepdims=True)
            after = carry[k]
            n1 = jnp.where(row == tm - 1, after[0:1, :], pltpu.roll(dc, tm - 1, axis=0))
            n2 = jnp.where(row == tm - 2, after[0:1, :], jnp.where(row == tm - 1, after[1:2, :], pltpu.roll(dc, tm - 2, axis=0)))
            du_ref[...] = (cw_ref[2:3, :] * dc + cw_ref[1:2, :] * n1 + cw_ref[0:1, :] * n2).astype(BF16)
            carry[k] = dc[0:8, :]

    tile = pl.BlockSpec((tm, tn), lambda j, i: (ni - 1 - i, j))
    halo_spec = pl.BlockSpec((halo, tn), lambda j, i: (jnp.maximum((ni - 1 - i) * (tm // halo) - 1, 0), j))
    gspec = pl.BlockSpec((8, tn), lambda j, i: (0, j))
    return pl.pallas_call(
        body, grid=(nj, ni),
        in_specs=[tile, tile, tile, halo_spec, halo_spec,
                  pl.BlockSpec((3, tn), lambda j, i: (0, j)), pl.BlockSpec((3, tn), lambda j, i: (0, nj + j)),
                  pl.BlockSpec((1, tn), lambda j, i: (0, j)), pl.BlockSpec((1, tn), lambda j, i: (0, nj + j))],
        out_specs=[tile, tile, gspec, gspec],
        out_shape=[SDS((s, f), BF16), SDS((s, f), BF16), SDS((8, f), F32), SDS((8, f), F32)],
        scratch_shapes=[pltpu.VMEM((2, 8, tn), F32)], name="ffn_bwd",
        compiler_params=_cp(("parallel", "arbitrary")))(dm, ua, ub, ua, ub, cw, cw, cb, cb)


def adamw(w, g, m, v, *, name, tr=None):
    r = w.shape[0]
    rest = w.shape[1:]
    if tr is None:
        tr = r
        for cand in (256, 128, 64, 32, 16, 8):
            if r % cand == 0:
                tr = cand
                break

    def body(w_ref, g_ref, m_ref, v_ref, d_ref, nm_ref, nv_ref):
        gv = g_ref[...]
        mn = ADAM_B1 * m_ref[...] + (1.0 - ADAM_B1) * gv
        vn = ADAM_B2 * v_ref[...] + (1.0 - ADAM_B2) * (gv * gv)
        m_hat = mn / (1.0 - ADAM_B1 ** ADAM_STEP)
        v_hat = vn / (1.0 - ADAM_B2 ** ADAM_STEP)
        d_ref[...] = -ADAM_LR * (m_hat / (jnp.sqrt(v_hat) + ADAM_EPS) + ADAM_WD * w_ref[...])
        nm_ref[...] = mn
        nv_ref[...] = vn

    blk = pl.BlockSpec((tr,) + rest, lambda i: (i,) + (0,) * len(rest))
    return pl.pallas_call(body, grid=(r // tr,), in_specs=[blk] * 4, out_specs=[blk] * 3, out_shape=[SDS(w.shape, F32)] * 3,
                          name=name, compiler_params=_cp(("parallel",)))(w, g, m, v)


def adamw_rows_view(w, g, m, v, *, name, tc=256):
    r, _, c = w.shape

    def body(w_ref, g_ref, m_ref, v_ref, d_ref, nm_ref, nv_ref, go_ref):
        gv = g_ref[...][:, None, :]
        mn = ADAM_B1 * m_ref[...] + (1.0 - ADAM_B1) * gv
        vn = ADAM_B2 * v_ref[...] + (1.0 - ADAM_B2) * (gv * gv)
        m_hat = mn / (1.0 - ADAM_B1 ** ADAM_STEP)
        v_hat = vn / (1.0 - ADAM_B2 ** ADAM_STEP)
        d_ref[...] = -ADAM_LR * (m_hat / (jnp.sqrt(v_hat) + ADAM_EPS) + ADAM_WD * w_ref[...])
        nm_ref[...] = mn
        nv_ref[...] = vn
        go_ref[...] = gv

    b3 = pl.BlockSpec((r, 1, tc), lambda i: (0, 0, i))
    b2 = pl.BlockSpec((r, tc), lambda i: (0, i))
    return pl.pallas_call(body, grid=(c // tc,), in_specs=[b3, b2, b3, b3], out_specs=[b3] * 4,
                          out_shape=[SDS(w.shape, F32)] * 4, name=name, compiler_params=_cp(("parallel",)))(w, g, m, v)


ANY = pl.BlockSpec(memory_space=pl.ANY)
ICI_KINDS = ("x", "y", "xy")


def _coords():
    return lax.axis_index("x"), lax.axis_index("y"), lax.axis_index("c")


def _peer(kind, x, y, c):
    if kind == "c":
        return (x, y, 1 - c)
    if kind == "x":
        return (1 - x, y, c)
    if kind == "y":
        return (x, 1 - y, c)
    return (1 - x, 1 - y, c)


def _chip_of(p):
    return 2 * p[0] + p[1]


def _half(rows, which):
    h = rows // 2
    return pl.ds(pl.multiple_of(which * h, 16), h)


def _remote(src, dst, send_sem, recv_sem, to):
    return pltpu.make_async_remote_copy(src_ref=src, dst_ref=dst, send_sem=send_sem, recv_sem=recv_sem,
                                        device_id=to, device_id_type=MESH)


def allgather_chips(shards, halved, *, name):
    n = len(shards)

    def body(*refs):
        parts = (refs[:n], refs[n:2 * n], refs[2 * n], refs[2 * n + 1], halved)
        _allgather_start(*parts)
        _allgather_finish(*parts)

    return pl.pallas_call(
        body, in_specs=[ANY] * n, out_specs=[ANY] * n,
        out_shape=_allgather_shapes(shards), scratch_shapes=_allgather_sems(n), name=name)(*shards)


def _allgather_shapes(shards):
    return [SDS((4,) + a.shape, a.dtype) for a in shards]


def _allgather_sems(n):
    return [pltpu.SemaphoreType.DMA((n, 6)), pltpu.SemaphoreType.DMA((n, 6))]


def _allgather_rows(ref, is_halved, which):
    r = ref.shape[0]
    return _half(r, which) if is_halved else pl.ds(0, r)


def _allgather_first(ins, outs, send_sems, recv_sems, halved):
    x, y, c = _coords()
    my_chip = 2 * x + y
    cps = []
    for w in range(len(ins)):
        rows = _allgather_rows(ins[w], halved[w], c)
        for k, kind in enumerate(ICI_KINDS):
            cps.append(_remote(ins[w].at[rows], outs[w].at[my_chip, rows], send_sems.at[w, k], recv_sems.at[w, k],
                               _peer(kind, x, y, c)))
    return cps


def _allgather_start(ins, outs, send_sems, recv_sems, halved):
    for cp in _allgather_first(ins, outs, send_sems, recv_sems, halved):
        cp.start()


def _allgather_finish(ins, outs, send_sems, recv_sems, halved):
    x, y, c = _coords()
    me = (x, y, c)
    second = []
    for w in range(len(ins)):
        for k, kind in enumerate(ICI_KINDS):
            landed = outs[w].at[_chip_of(_peer(kind, x, y, c)), _allgather_rows(ins[w], halved[w], c)]
            _remote(landed, landed, send_sems.at[w, k], recv_sems.at[w, k], me).wait_recv()
            if halved[w]:
                cp = _remote(landed, landed, send_sems.at[w, 3 + k], recv_sems.at[w, 3 + k], _peer("c", x, y, c))
                cp.start()
                second.append(cp)
    for w in range(len(ins)):
        if halved[w]:
            for k, kind in enumerate(ICI_KINDS):
                other = outs[w].at[_chip_of(_peer(kind, x, y, c)), _allgather_rows(ins[w], True, 1 - c)]
                _remote(other, other, send_sems.at[w, 3 + k], recv_sems.at[w, 3 + k], me).wait_recv()
    for cp in _allgather_first(ins, outs, send_sems, recv_sems, halved) + second:
        cp.wait_send()


def _half_of(ref, by_cols, which):
    lead = (slice(None),) * (len(ref.shape) - 2)
    if by_cols:
        h = ref.shape[-1] // 2
        return ref.at[lead + (slice(None), pl.ds(pl.multiple_of(which * h, LANES), h))]
    return ref.at[lead + (_half(ref.shape[-2], which),)]


def _half_shape(shape, by_cols):
    return shape[:-1] + (shape[-1] // 2,) if by_cols else shape[:-2] + (shape[-2] // 2, shape[-1])


def grads_to_sibling(gs, by_cols, *, name):
    n = len(gs)

    def body(*refs):
        ins, outs = refs[:n], refs[n:2 * n]
        send_sems, recv_sems = refs[2 * n:]
        x, y, c = _coords()
        cps = []
        for w in range(n):
            cp = _remote(_half_of(ins[w], by_cols[w], 1 - c), outs[w], send_sems.at[w], recv_sems.at[w], _peer("c", x, y, c))
            cp.start()
            cps.append(cp)
        for cp in cps:
            cp.wait()

    return pl.pallas_call(
        body, in_specs=[ANY] * n, out_specs=[ANY] * n,
        out_shape=[SDS(_half_shape(a.shape, bc), a.dtype) for a, bc in zip(gs, by_cols)],
        scratch_shapes=[pltpu.SemaphoreType.DMA((n,)), pltpu.SemaphoreType.DMA((n,))], name=name)(*gs)


def grads_to_chips(ps, *, name):
    n = len(ps)

    def body(*refs):
        parts = (refs[:n], refs[n:2 * n], refs[2 * n], refs[2 * n + 1])
        _to_chips_start(*parts)
        _to_chips_finish(*parts)

    return pl.pallas_call(
        body, in_specs=[ANY] * n, out_specs=[ANY] * n,
        out_shape=_to_chips_shapes(ps), scratch_shapes=_to_chips_sems(n), name=name)(*ps)


def _to_chips_shapes(ps):
    return [SDS((3,) + a.shape[1:], a.dtype) for a in ps]


def _to_chips_sems(n):
    return [pltpu.SemaphoreType.DMA((n, 3)), pltpu.SemaphoreType.DMA((n, 3))]


def _to_chips_copies(ins, outs, send_sems, recv_sems):
    x, y, c = _coords()
    cps = []
    for w in range(len(ins)):
        for k, kind in enumerate(ICI_KINDS):
            to = _peer(kind, x, y, c)
            cps.append(_remote(ins[w].at[_chip_of(to)], outs[w].at[k], send_sems.at[w, k], recv_sems.at[w, k], to))
    return cps


def _to_chips_start(ins, outs, send_sems, recv_sems):
    for cp in _to_chips_copies(ins, outs, send_sems, recv_sems):
        cp.start()


def _to_chips_finish(ins, outs, send_sems, recv_sems):
    for cp in _to_chips_copies(ins, outs, send_sems, recv_sems):
        cp.wait()


def halves_to_full(hs, by_cols, *, name):
    n = len(hs)

    def body(*refs):
        ins, outs = refs[:n], refs[n:2 * n]
        send_sems, recv_sems = refs[2 * n:]
        x, y, c = _coords()
        cps = []
        for w in range(n):
            cp = _remote(ins[w], _half_of(outs[w], by_cols[w], c), send_sems.at[w], recv_sems.at[w], _peer("c", x, y, c))
            cp.start()
            cps.append(cp)
        for cp in cps:
            cp.wait()

    return pl.pallas_call(
        body, in_specs=[ANY] * n, out_specs=[ANY] * n,
        out_shape=[SDS((a.shape[0], 2 * a.shape[1]) if bc else (2 * a.shape[0], a.shape[1]), a.dtype)
                   for a, bc in zip(hs, by_cols)],
        scratch_shapes=[pltpu.SemaphoreType.DMA((n,)), pltpu.SemaphoreType.DMA((n,))],
        name=name)(*hs)


def _row_tile(rows):
    for cand in (256, 192, 176, 128, 64, 32, 16):
        if rows % cand == 0:
            return cand
    return rows


def chip_sum(g, recv, c_arr, by_cols, *, name):
    _, r, cols = g.shape

    def body(c_ref, g_ref, r_ref, f_ref, b_ref):
        tot = g_ref[...] + r_ref[...]
        f_ref[...] = tot
        b_ref[...] = tot.astype(BF16)

    if by_cols:
        tc = 2 * LANES
        nblk = cols // 2 // tc
        shape = (4, r, cols // 2)
        blk = pl.BlockSpec((None, r, tc), lambda j, i, c_ref: (j, 0, i))
        mine = pl.BlockSpec((None, r, tc), lambda j, i, c_ref: (j, 0, c_ref[0] * nblk + i))
    else:
        tr = _row_tile(r // 2)
        nblk = r // 2 // tr
        shape = (4, r // 2, cols)
        blk = pl.BlockSpec((None, tr, cols), lambda j, i, c_ref: (j, i, 0))
        mine = pl.BlockSpec((None, tr, cols), lambda j, i, c_ref: (j, c_ref[0] * nblk + i, 0))
    grid_spec = pltpu.PrefetchScalarGridSpec(num_scalar_prefetch=1, grid=(4, nblk), in_specs=[mine, blk], out_specs=[blk, blk])
    return pl.pallas_call(body, grid_spec=grid_spec, out_shape=[SDS(shape, F32), SDS(shape, BF16)],
                          name=name, compiler_params=_cp(("parallel", "parallel")))(c_arr, g, recv)


def final_sum(pf, recv, chip_arr, *, name):
    _, h, cols = pf.shape
    tr = _row_tile(h)

    def body(chip_ref, p_ref, r_ref, o_ref):
        o_ref[...] = ((p_ref[...] + r_ref[0].astype(F32)) + r_ref[1].astype(F32)) + r_ref[2].astype(F32)

    grid_spec = pltpu.PrefetchScalarGridSpec(
        num_scalar_prefetch=1, grid=(h // tr,),
        in_specs=[pl.BlockSpec((None, tr, cols), lambda i, chip_ref: (chip_ref[0], i, 0)),
                  pl.BlockSpec((3, tr, cols), lambda i, chip_ref: (0, i, 0))],
        out_specs=pl.BlockSpec((tr, cols), lambda i, chip_ref: (i, 0)))
    return pl.pallas_call(body, grid_spec=grid_spec, out_shape=SDS((h, cols), F32), name=name,
                          compiler_params=_cp(("parallel",)))(chip_arr, pf, recv)


def allreduce_small(v, *, name):
    rws, cols = v.shape

    def body(v_ref, all_ref, sum_ref, send_sems, recv_sems, local_sem):
        x, y, c = _coords()
        me, sibling = (x, y, c), (x, y, 1 - c)
        chips = [(1 - x, y), (x, 1 - y), (1 - x, 1 - y)]

        def rows(px, py, pc):
            return all_ref.at[pl.ds(pl.multiple_of((4 * px + 2 * py + pc) * rws, 8), rws), :]

        def copy(k, block, to, src=None):
            return _remote(rows(*block) if src is None else src, rows(*block), send_sems.at[k], recv_sems.at[k], to)

        mine = pltpu.make_async_copy(v_ref, rows(*me), local_sem)
        mine.start()
        first = [copy(0, me, sibling, src=v_ref)]
        first += [copy(1 + j, me, (*chip, c), src=v_ref) for j, chip in enumerate(chips)]
        for cp in first:
            cp.start()
        passed = [copy(4 + j, (*chip, c), sibling) for j, chip in enumerate(chips)]
        for j, chip in enumerate(chips):
            copy(1 + j, (*chip, c), me).wait_recv()
            passed[j].start()
        copy(0, sibling, me).wait_recv()
        for j, chip in enumerate(chips):
            copy(4 + j, (*chip, 1 - c), me).wait_recv()
        for cp in first + passed:
            cp.wait_send()
        mine.wait()
        tot = all_ref[0:rws, :]
        for dev in range(1, 8):
            tot = tot + all_ref[dev * rws:(dev + 1) * rws, :]
        sum_ref[...] = tot

    vm = pl.BlockSpec(memory_space=pltpu.VMEM)
    return pl.pallas_call(
        body, in_specs=[vm], out_specs=[vm, vm],
        out_shape=[SDS((8 * rws, cols), v.dtype), SDS((rws, cols), v.dtype)],
        scratch_shapes=[pltpu.SemaphoreType.DMA((7,)), pltpu.SemaphoreType.DMA((7,)), pltpu.SemaphoreType.DMA],
        name=name)(v)[1]


def _pack_rows(parts, rows):
    out = []
    for a, r in zip(parts, rows):
        flat = a.reshape(-1)
        flat = jnp.pad(flat, (0, r * LANES - flat.shape[0]))
        out.append(flat.reshape(r, LANES))
    return jnp.concatenate(out, axis=0)


def _unpack_rows(packed, shapes, rows):
    out, at = [], 0
    for shp, r in zip(shapes, rows):
        size = int(np.prod(shp))
        out.append(packed[at:at + r].reshape(-1)[:size].reshape(shp))
        at += r
    return out


def kernel(x, g_pre_mix, w_in, b_forget, w_o_fox, w_o_dil, w_out, g_post_mix, g_pre_ffn, w_up, conv_w, conv_b, w_down, g_post_ffn, loss_target, m_g_pre_mix, m_w_in, m_b_forget, m_w_o_fox, m_w_o_dil, m_w_out, m_g_post_mix, m_g_pre_ffn, m_w_up, m_conv_w, m_conv_b, m_w_down, m_g_post_ffn, v_g_pre_mix, v_w_in, v_b_forget, v_w_o_fox, v_w_o_dil, v_w_out, v_g_post_mix, v_g_pre_ffn, v_w_up, v_conv_w, v_conv_b, v_w_down, v_g_post_ffn):
    xi, yi, ci = _coords()
    chip = 2 * xi + yi
    c_arr = jnp.reshape(ci, (1,)).astype(jnp.int32)
    chip_arr = jnp.reshape(chip, (1,)).astype(jnp.int32)
    xs = x[0]
    target = loss_target[0]
    s, d = xs.shape
    f_half = w_down.shape[1] * 4
    cols_in = w_in.shape[2]

    big = (w_in, w_o_fox, w_o_dil, w_out, w_up, w_down)
    shards = [w[0].astype(BF16) for w in big]
    a_in, a_cw = allgather_chips([shards[0], conv_w[0]], [True, False], name="allgather_w_in")
    w_in_full = jnp.concatenate([jnp.where(chip == j, shards[0], a_in[j]) for j in range(4)], axis=1)
    cw = jnp.concatenate([jnp.where(chip == j, conv_w[0], a_cw[j]) for j in range(4)], axis=1)
    nf = N_HEADS
    e_a, e_b = 3 * ATT_W, 3 * ATT_W + nf
    wz = jnp.concatenate([w_in_full[:, :e_a], w_in_full[:, e_b:]], axis=1)
    wf = jnp.pad(w_in_full[:, e_a:e_b], ((0, 0), (0, LANES - nf)))
    cb = conv_b
    bfo = jnp.pad(b_forget, ((0, 0), (0, LANES - nf)))

    h1 = rmsnorm_fwd(xs, g_pre_mix)
    z = mm([(h1, d, 0)], [(wz, d, 0)], nt=False, out_dtype=BF16, tm=1024, tn=512, name="in_proj")
    fa = mm([(h1, d, 0)], [(wf, d, 0)], nt=False, out_dtype=F32, tm=1024, tn=LANES, name="in_proj_forget")
    q_aug, k_aug = fox_prep(z, fa, bfo)
    ya, lse_a, *late = fox_fwd(q_aug, k_aug, z, gather=shards[1:])
    a_of, a_od, a_out, a_up, a_down = [
        lax.dynamic_update_index_in_dim(a4, own, chip, 0) for a4, own in zip(late, shards[1:])]
    wo_a = jnp.concatenate([a_of[j] for j in range(4)], axis=1)
    wo_b = jnp.concatenate([a_od[j] for j in range(4)], axis=1)
    w_o = a_out.reshape(d, d)
    w_dn = a_down.reshape(f_half, d)
    wu_a = jnp.concatenate([a_up[0], a_up[1]], axis=1)
    wu_b = jnp.concatenate([a_up[2], a_up[3]], axis=1)
    qkv_b = rope_apply([(z, Z_QB, QK_SCALE, True), (z, Z_KB, 1.0, True), (z, Z_VB, 1.0, False)], rope_tables(s, 1.0),
                       out_dtype=F32, name="rope_fwd")
    yb, lse_b = dil_fwd_all(qkv_b)
    pa, pb, mixed = gate_mix(ya, yb, wo_a, wo_b, z)
    y1, x1, h2 = proj_norm_res(mixed, w_o, g_post_mix, xs, g_pre_ffn, name="out_proj")
    ua, ub, mid = ffn_up(h2, wu_a, wu_b, cw, cb)
    dout, dy2, gg_post_ffn, sq = proj_norm_loss(mid, w_dn, g_post_ffn, x1, target, name="down_proj")
    loss = lax.psum(0.5 * sq[0, 0] / d, ("x", "y", "c"))

    dmid = mm([(dy2, d, 0)], [(w_dn, d, 0)], nt=True, out_dtype=BF16, tm=512, tn=f_half // 2, name="down_dgrad")
    dw_down = wgrad((mid, f_half, 0), dy2, tk=f_half // 2, tn=1024, ts=1024, name="down_wgrad")
    dua, dub, gc_a, gc_b = ffn_bwd(dmid, ua, ub, cw, cb)
    dx1, dy1, gg_pre_ffn, gg_post_mix = mm_norm_bwd(
        [(dua, f_half, 0), (dub, f_half, 0)], [(wu_a, f_half, 0), (wu_b, f_half, 0)],
        [(x1, g_pre_ffn, dout, F32), (y1, g_post_mix, None, BF16)], name="up_dgrad")
    dw_up = jnp.concatenate(
        [wgrad((h2, d, 0), du, tk=1024, tn=f_half // 2, ts=1024, name=f"up_wgrad_{k}", chip_major=True)
         for k, du in enumerate((dua, dub))], axis=0)
    def to_chip_sums(gs, nms, tag, by_cols=False):
        from_sib = grads_to_sibling(gs, [by_cols] * len(gs), name=f"grads_to_sibling_{tag}")
        return [chip_sum(g, r, c_arr, by_cols, name=f"chip_sum_{nm}") for g, r, nm in zip(gs, from_sib, nms)]

    sums_ffn = to_chip_sums([dw_up, dw_down.reshape(4, f_half // 4, d)], ("w_up", "w_down"), "ffn")
    dmixed = mm([(dy1, d, 0)], [(w_o, d, 0)], nt=True, out_dtype=BF16, tm=512, tn=512, name="out_dgrad")
    dw_out = wgrad((mixed, d, 0), dy1, tk=1024, tn=1024, ts=1024, name="out_wgrad")
    dpa, dpb, dz_g = gate_bwd(dmixed, z, pa, pb)
    dya = mm([(dpa, d, 0)], [(wo_a, d, 0)], nt=True, out_dtype=BF16, tm=512, tn=ATT_W, name="fox_o_dgrad")
    dyb = mm([(dpb, d, 0)], [(wo_b, d, 0)], nt=True, out_dtype=F32, tm=512, tn=ATT_W, name="dil_o_dgrad")
    by_chip_cols = lambda a: jnp.stack([a[:, j * (d // 4):(j + 1) * (d // 4)] for j in range(4)], axis=0)
    dw_of = by_chip_cols(wgrad((ya, ATT_W, 0), dpa, tk=ATT_W, tn=d, ts=1024, name="fox_o_wgrad"))
    dw_od = by_chip_cols(wgrad((yb, ATT_W, 0), dpb, tk=ATT_W, tn=d, ts=1024, name="dil_o_wgrad"))
    sums_mix = to_chip_sums([dw_of, dw_od, dw_out.reshape(4, d // 4, d)], ("w_o_fox", "w_o_dil", "w_out"), "mix")
    dd_a = head_rowsum(dya, ya, name="fox_delta")
    dq_aug, dk_aug, dv_a, *got_ffn = fox_bwd(q_aug, k_aug, z, dya, lse_a, dd_a, exchange=[p[1] for p in sums_ffn])
    dz_a, dfa, gg_bf = fox_post(dq_aug, dk_aug, dv_a, fa, bfo)
    dq_b, dk_b, dv_b, *got_mix = dil_bwd_all(qkv_b, dyb, lse_b, yb, exchange=[p[1] for p in sums_mix])
    dz_b = rope_apply([(dq_b, 0, QK_SCALE, True), (dk_b, 0, 1.0, True), (dv_b, 0, 1.0, False)],
                      rope_tables(s, -1.0), out_dtype=BF16, name="rope_bwd")
    dwt_a = wgrad((dz_a, e_a, 0), h1, tk=e_a // 2, tn=d, ts=1024, name="in_wgrad_a")
    dwt_b = wgrad((dz_b, e_a, 0), h1, tk=e_a // 2, tn=d, ts=1024, name="in_wgrad_b")
    dwt_g = wgrad((dz_g, 2 * d, 0), h1, tk=d, tn=d, ts=1024, name="in_wgrad_g")
    dwt_f = wgrad((dfa, LANES, 0), h1, tk=LANES, tn=d, ts=1024, name="in_wgrad_f")
    dwt_full = jnp.concatenate([dwt_a, dwt_f[:nf], dwt_b, dwt_g], axis=0)
    dw_in = jnp.stack([dwt_full[j * cols_in:(j + 1) * cols_in] for j in range(4)], axis=0)
    sums_in = to_chip_sums([dw_in], ("w_in",), "in", by_cols=True)
    grad_x, gg_pre_mix, *got_in = mm_norm_bwd(
        [(dz_a, e_a, 0), (dz_b, e_a, 0), (dz_g, d, 0), (dz_g, d, 1), (dfa, LANES, 0)],
        [(wz, e_a, 0), (wz, e_a, 1), (wz, d, 3), (wz, d, 4), (wf, LANES, 0)],
        [(xs, g_pre_mix, dx1, F32)], exchange=[sums_in[0][1]], name="in_dgrad")

    names = ("w_in", "w_o_fox", "w_o_dil", "w_out", "w_up", "w_down")
    sums = sums_in + sums_mix + sums_ffn
    from_chips = list(got_in) + list(got_mix) + list(got_ffn)
    halves = [final_sum(p[0], r, chip_arr, name=f"final_sum_{nm}") for p, r, nm in zip(sums, from_chips, names)]
    from_half = halves_to_full(halves, [True] + [False] * 5, name="halves_to_full")
    g_big = [lax.dynamic_update_slice_in_dim(full, mine, ci * mine.shape[k == 0], axis=int(k == 0))
             for k, (full, mine) in enumerate(zip(from_half, halves))]
    upd_big = [adamw(w[0], g, m[0], v[0], name=f"adamw_{nm}") for w, g, m, v, nm in list(zip(
        big, g_big, (m_w_in, m_w_o_fox, m_w_o_dil, m_w_out, m_w_up, m_w_down),
        (v_w_in, v_w_o_fox, v_w_o_dil, v_w_out, v_w_up, v_w_down), names))[1:]]
    to_t = lambda a: jnp.transpose(a, (2, 0, 1))
    from_t = lambda a: jnp.transpose(a, (1, 2, 0))
    *upd_in, g_in_t = adamw_rows_view(to_t(w_in), g_big[0], to_t(m_w_in), to_t(v_w_in), name="adamw_w_in")

    g_cw_loc = jnp.concatenate([gc_a[0:3], gc_b[0:3]], axis=1)
    g_cb_loc = jnp.concatenate([gc_a[3:4], gc_b[3:4]], axis=1)
    small_loc = [gg_pre_mix, gg_post_mix, gg_pre_ffn, gg_post_ffn, g_cb_loc, gg_bf[:, :nf], g_cw_loc]
    red_rows = (8, 8, 8, 8, 48, 8, 136)
    red = allreduce_small(_pack_rows(small_loc, red_rows), name="allreduce_small")
    g_pm, g_qm, g_pf, g_qf, g_cb, g_bf, g_cw_full = _unpack_rows(red, [a.shape for a in small_loc], red_rows)
    cols_cw = conv_w.shape[2]
    g_cw = lax.dynamic_slice_in_dim(g_cw_full, chip * cols_cw, cols_cw, axis=1)
    small_w = (g_pre_mix, g_post_mix, g_pre_ffn, g_post_ffn, conv_b, b_forget, conv_w[0])
    small_m = (m_g_pre_mix, m_g_post_mix, m_g_pre_ffn, m_g_post_ffn, m_conv_b, m_b_forget, m_conv_w[0])
    small_v = (v_g_pre_mix, v_g_post_mix, v_g_pre_ffn, v_g_post_ffn, v_conv_b, v_b_forget, v_conv_w[0])
    small_g = (g_pm, g_qm, g_pf, g_qf, g_cb, g_bf, g_cw)
    ad_rows = (8, 8, 8, 8, 48, 8, 40)
    packed = [_pack_rows(t, ad_rows) for t in (small_w, small_g, small_m, small_v)]
    upd_small = [_unpack_rows(o, [a.shape for a in small_w], ad_rows) for o in adamw(*packed, name="adamw_small")]

    order = ("g_pre_mix", "w_in", "b_forget", "w_o_fox", "w_o_dil", "w_out", "g_post_mix", "g_pre_ffn", "w_up", "conv_w",
             "conv_b", "w_down", "g_post_ffn")
    small_names = ("g_pre_mix", "g_post_mix", "g_pre_ffn", "g_post_ffn", "conv_b", "b_forget", "conv_w")
    grads, deltas, new_ms, new_vs = {}, {}, {}, {}
    grads["w_in"] = from_t(g_in_t)
    deltas["w_in"], new_ms["w_in"], new_vs["w_in"] = (from_t(a) for a in upd_in)
    for k, nm in enumerate(names[1:]):
        grads[nm] = g_big[k + 1][None]
        deltas[nm], new_ms[nm], new_vs[nm] = (a[None] for a in upd_big[k])
    for k, nm in enumerate(small_names):
        lead = (lambda a: a[None]) if nm == "conv_w" else (lambda a: a)
        grads[nm] = lead(small_g[k])
        deltas[nm], new_ms[nm], new_vs[nm] = (lead(upd_small[j][k]) for j in range(3))
    return (loss, grad_x[None], *[grads[nm] for nm in order], *[deltas[nm] for nm in order],
            *[new_ms[nm] for nm in order], *[new_vs[nm] for nm in order])
```

```python
import functools
import math

import numpy as np
import jax
import jax.numpy as jnp
from jax import lax
from jax.experimental import pallas as pl
from jax.experimental.pallas import tpu as pltpu

F32 = jnp.float32
BF16 = jnp.bfloat16
SDS = jax.ShapeDtypeStruct
MESH = pl.DeviceIdType.MESH

HEAD_DIM = 64
N_HEADS = 8
LANES = 128
ATT_W = N_HEADS * HEAD_DIM
DIL_PATTERNS = ((128, 1), (512, 4), (2048, 16))
DIL_BLK = 128
ROPE_DIM = HEAD_DIM // 4
ROPE_THETA = 500000.0
RMS_EPS = 1e-6
NEG = -1e30
QK_SCALE = 1.0 / math.sqrt(HEAD_DIM)
ADAM_LR, ADAM_B1, ADAM_B2, ADAM_EPS, ADAM_WD, ADAM_STEP = 0.001, 0.9, 0.999, 1e-08, 0.01, 10
VMEM_LIMIT = 56 * 1024 * 1024

Z_QA, Z_KA, Z_VA, Z_QB, Z_KB, Z_VB = 0, 1, 2, 3, 4, 5
Z_W = 5120


def _cp(sem):
    return pltpu.CompilerParams(dimension_semantics=sem, vmem_limit_bytes=VMEM_LIMIT)


def _nt(a, b):
    return lax.dot_general(a, b, (((1,), (1,)), ((), ())), preferred_element_type=F32)


def _tn(a, b):
    return lax.dot_general(a, b, (((0,), (0,)), ((), ())), preferred_element_type=F32)


def _nn(a, b):
    return jnp.dot(a, b, preferred_element_type=F32)


def _lane(shape):
    return lax.broadcasted_iota(jnp.int32, shape, 1)


def _row(shape):
    return lax.broadcasted_iota(jnp.int32, shape, 0)


def rmsnorm_fwd(x, g, *, tm=512):
    s, d = x.shape

    def body(x_ref, g_ref, h_ref):
        xv = x_ref[...]
        inv = lax.rsqrt(jnp.mean(xv * xv, axis=-1, keepdims=True) + RMS_EPS)
        h_ref[...] = (xv * inv * g_ref[...]).astype(h_ref.dtype)

    return pl.pallas_call(
        body, grid=(s // tm,),
        in_specs=[pl.BlockSpec((tm, d), lambda i: (i, 0)), pl.BlockSpec((1, d), lambda i: (0, 0))],
        out_specs=pl.BlockSpec((tm, d), lambda i: (i, 0)),
        out_shape=SDS((s, d), BF16), name="rmsnorm_fwd", compiler_params=_cp(("parallel",)))(x, g)


def rmsnorm_bwd(dh, x, g, res, *, out_dtype, tm=256, name):
    s, d = x.shape
    n = s // tm
    has_res = res is not None

    def body(*refs):
        if has_res:
            dh_ref, x_ref, g_ref, res_ref, dx_ref, dg_ref, acc = refs
        else:
            dh_ref, x_ref, g_ref, dx_ref, dg_ref, acc = refs
        i = pl.program_id(0)

        @pl.when(i == 0)
        def _():
            acc[...] = jnp.zeros_like(acc)

        xv = x_ref[...]
        inv = lax.rsqrt(jnp.mean(xv * xv, axis=-1, keepdims=True) + RMS_EPS)
        xh = xv * inv
        dhv = dh_ref[...].astype(F32)
        dxh = dhv * g_ref[...]
        dot = jnp.mean(dxh * xh, axis=-1, keepdims=True)
        dx = inv * (dxh - xh * dot)
        if has_res:
            dx = dx + res_ref[...]
        dx_ref[...] = dx.astype(dx_ref.dtype)
        acc[...] += jnp.sum((dhv * xh).reshape(tm // 8, 8, d), axis=0)

        @pl.when(i == n - 1)
        def _():
            dg_ref[...] = jnp.sum(acc[...], axis=0, keepdims=True)

    row = pl.BlockSpec((tm, d), lambda i: (i, 0))
    in_specs = [row, row, pl.BlockSpec((1, d), lambda i: (0, 0))] + ([row] if has_res else [])
    args = [dh, x, g] + ([res] if has_res else [])
    return pl.pallas_call(
        body, grid=(n,), in_specs=in_specs,
        out_specs=[row, pl.BlockSpec((1, d), lambda i: (0, 0))],
        out_shape=[SDS((s, d), out_dtype), SDS((1, d), F32)],
        scratch_shapes=[pltpu.VMEM((8, d), F32)],
        name=name, compiler_params=_cp(("arbitrary",)))(*args)


def mm(a_views, b_views, *, nt, out_dtype, tm, tn, name):
    n_p = len(a_views)
    m = a_views[0][0].shape[0]
    n = b_views[0][0].shape[0] if nt else b_views[0][0].shape[1]

    def body(*refs):
        o_ref = refs[-1]
        acc = None
        for p in range(n_p):
            av = refs[p][...].astype(BF16)
            bv = refs[n_p + p][...].astype(BF16)
            dv = _nt(av, bv) if nt else _nn(av, bv)
            acc = dv if acc is None else acc + dv
        o_ref[...] = acc.astype(o_ref.dtype)

    in_specs = []
    for arr, w, blk in a_views:
        in_specs.append(pl.BlockSpec((tm, w), functools.partial(lambda i, j, blk: (i, blk), blk=blk)))
    for arr, w, blk in b_views:
        if nt:
            in_specs.append(pl.BlockSpec((tn, w), functools.partial(lambda i, j, blk: (j, blk), blk=blk)))
        else:
            in_specs.append(pl.BlockSpec((w, tn), lambda i, j: (0, j)))
    return pl.pallas_call(
        body, grid=(m // tm, n // tn), in_specs=in_specs,
        out_specs=pl.BlockSpec((tm, tn), lambda i, j: (i, j)),
        out_shape=SDS((m, n), out_dtype), name=name,
        compiler_params=_cp(("parallel", "parallel")))(*[a[0] for a in a_views], *[b[0] for b in b_views])


def wgrad(a_view, g, *, tk, tn, ts, name, chip_major=False):
    arr, ka, blk = a_view
    s, n = g.shape
    ns = s // ts

    def body(a_ref, g_ref, o_ref):
        @pl.when(pl.program_id(2) == 0)
        def _():
            o_ref[...] = jnp.zeros_like(o_ref)

        o_ref[...] += _tn(a_ref[...].astype(BF16), g_ref[...].astype(BF16))

    if chip_major:
        out_spec = pl.BlockSpec((None, tk, tn), lambda i, j, k: (j, i, 0))
        out_shape = SDS((n // tn, ka, tn), F32)
    else:
        out_spec = pl.BlockSpec((tk, tn), lambda i, j, k: (i, j))
        out_shape = SDS((ka, n), F32)
    return pl.pallas_call(
        body, grid=(ka // tk, n // tn, ns),
        in_specs=[pl.BlockSpec((ts, tk), lambda i, j, k: (k, blk * (ka // tk) + i)),
                  pl.BlockSpec((ts, tn), lambda i, j, k: (k, j))],
        out_specs=out_spec, out_shape=out_shape, name=name,
        compiler_params=_cp(("parallel", "parallel", "arbitrary")))(arr, g)


def _norm_bwd_rows(dh, xh, inv, g):
    dxh = dh * g
    dx = inv * (dxh - xh * jnp.mean(dxh * xh, axis=-1, keepdims=True))
    return dx, jnp.sum((dh * xh).reshape(dh.shape[0] // 8, 8, dh.shape[1]), axis=0)


def proj_norm_res(a, w, g, xres, g_next, *, tm=512, name):
    s, k = a.shape
    d = w.shape[1]

    def body(a_ref, w_ref, g_ref, x_ref, gn_ref, y_ref, o_ref, h_ref):
        y = _nn(a_ref[...], w_ref[...])
        inv = lax.rsqrt(jnp.mean(y * y, axis=-1, keepdims=True) + RMS_EPS)
        xn = x_ref[...] + y * inv * g_ref[...]
        y_ref[...] = y
        o_ref[...] = xn
        inv_n = lax.rsqrt(jnp.mean(xn * xn, axis=-1, keepdims=True) + RMS_EPS)
        h_ref[...] = (xn * inv_n * gn_ref[...]).astype(h_ref.dtype)

    row = pl.BlockSpec((tm, d), lambda i: (i, 0))
    vec = pl.BlockSpec((1, d), lambda i: (0, 0))
    return pl.pallas_call(
        body, grid=(s // tm,),
        in_specs=[pl.BlockSpec((tm, k), lambda i: (i, 0)), pl.BlockSpec((k, d), lambda i: (0, 0)), vec, row, vec],
        out_specs=[row, row, row], out_shape=[SDS((s, d), F32), SDS((s, d), F32), SDS((s, d), BF16)], name=name,
        compiler_params=_cp(("parallel",)))(a, w, g, xres, g_next)


def proj_norm_loss(a, w, g, xres, target, *, tm=512, name):
    s, k = a.shape
    d = w.shape[1]
    n = s // tm

    def body(a_ref, w_ref, g_ref, x_ref, t_ref, do_ref, dy_ref, dg_ref, l_ref, acc):
        i = pl.program_id(0)

        @pl.when(i == 0)
        def _():
            acc[...] = jnp.zeros_like(acc)
            l_ref[...] = jnp.zeros_like(l_ref)

        y = _nn(a_ref[...], w_ref[...])
        inv = lax.rsqrt(jnp.mean(y * y, axis=-1, keepdims=True) + RMS_EPS)
        yh = y * inv
        err = x_ref[...] + yh * g_ref[...] - t_ref[...]
        dout = err * (1.0 / d)
        do_ref[...] = dout
        l_ref[...] += jnp.sum(jnp.sum(err * err, axis=1, keepdims=True), axis=0, keepdims=True)
        dy, part = _norm_bwd_rows(dout, yh, inv, g_ref[...])
        dy_ref[...] = dy.astype(dy_ref.dtype)
        acc[...] += part

        @pl.when(i == n - 1)
        def _():
            dg_ref[...] = jnp.sum(acc[...], axis=0, keepdims=True)

    row = pl.BlockSpec((tm, d), lambda i: (i, 0))
    vec = pl.BlockSpec((1, d), lambda i: (0, 0))
    return pl.pallas_call(
        body, grid=(n,),
        in_specs=[pl.BlockSpec((tm, k), lambda i: (i, 0)), pl.BlockSpec((k, d), lambda i: (0, 0)), vec, row, row],
        out_specs=[row, row, vec, pl.BlockSpec((1, 1), lambda i: (0, 0))],
        out_shape=[SDS((s, d), F32), SDS((s, d), BF16), SDS((1, d), F32), SDS((1, 1), F32)],
        scratch_shapes=[pltpu.VMEM((8, d), F32)], name=name, compiler_params=_cp(("arbitrary",)))(a, w, g, xres, target)


def mm_norm_bwd(a_views, b_views, stages, exchange=(), *, tm=256, name):
    n_p, n_s, ne = len(a_views), len(stages), len(exchange)
    s = a_views[0][0].shape[0]
    d = b_views[0][0].shape[0]
    n = s // tm
    has_res = [st[2] is not None for st in stages]

    def body(*refs):
        a_refs, b_refs = refs[:n_p], refs[n_p:2 * n_p]
        at = 2 * n_p
        st_refs = []
        for k in range(n_s):
            cnt = 3 if has_res[k] else 2
            st_refs.append(refs[at:at + cnt])
            at += cnt
        e_ins = refs[at:at + ne]
        at += ne
        dx_refs, dg_refs = refs[at:at + n_s], refs[at + n_s:at + 2 * n_s]
        at += 2 * n_s
        e_outs = refs[at:at + ne]
        at += ne
        accs = refs[at:at + n_s]
        comm = (e_ins, e_outs) + tuple(refs[at + n_s:])
        i = pl.program_id(0)

        @pl.when(i == 0)
        def _():
            for acc in accs:
                acc[...] = jnp.zeros_like(acc)
            if ne:
                _to_chips_start(*comm)

        dh = None
        for p in range(n_p):
            part = _nt(a_refs[p][...].astype(BF16), b_refs[p][...].astype(BF16))
            dh = part if dh is None else dh + part
        for k in range(n_s):
            xv = st_refs[k][0][...]
            inv = lax.rsqrt(jnp.mean(xv * xv, axis=-1, keepdims=True) + RMS_EPS)
            dx, part = _norm_bwd_rows(dh, xv * inv, inv, st_refs[k][1][...])
            if has_res[k]:
                dx = dx + st_refs[k][2][...]
            dx_refs[k][...] = dx.astype(dx_refs[k].dtype)
            accs[k][...] += part
            dh = dx

        @pl.when(i == n - 1)
        def _():
            for k in range(n_s):
                dg_refs[k][...] = jnp.sum(accs[k][...], axis=0, keepdims=True)
            if ne:
                _to_chips_finish(*comm)

    row = pl.BlockSpec((tm, d), lambda i: (i, 0))
    vec = pl.BlockSpec((1, d), lambda i: (0, 0))
    in_specs, args = [], []
    for arr, w, blk in a_views:
        in_specs.append(pl.BlockSpec((tm, w), functools.partial(lambda i, blk: (i, blk), blk=blk)))
        args.append(arr)
    for arr, w, blk in b_views:
        in_specs.append(pl.BlockSpec((d, w), functools.partial(lambda i, blk: (0, blk), blk=blk)))
        args.append(arr)
    for x, g, res, _ in stages:
        in_specs += [row, vec] + ([row] if res is not None else [])
        args += [x, g] + ([res] if res is not None else [])
    return pl.pallas_call(
        body, grid=(n,), in_specs=in_specs + [ANY] * ne,
        out_specs=[row] * n_s + [vec] * n_s + [ANY] * ne,
        out_shape=[SDS((s, d), st[3]) for st in stages] + [SDS((1, d), F32)] * n_s + _to_chips_shapes(exchange),
        scratch_shapes=[pltpu.VMEM((8, d), F32)] * n_s + (_to_chips_sems(ne) if ne else []), name=name,
        compiler_params=_cp(("arbitrary",)))(*args, *exchange)


def _split3(v):
    hi = v.astype(BF16).astype(F32)
    r = v - hi
    mid = r.astype(BF16).astype(F32)
    lo = (r - mid).astype(BF16).astype(F32)
    return hi, mid, lo


def _tri(n, upper):
    r = np.arange(n)
    m = (r[:, None] <= r[None, :]) if upper else (r[:, None] >= r[None, :])
    return jnp.asarray(m.astype(np.float32))


def fox_prep(z, fa, bfo, *, tb=512):
    s = z.shape[0]
    n = s // tb

    def body(q_ref, k_ref, v_ref, fa_ref, b_ref, tri_ref, qa_ref, ka_ref, va_ref, carry):
        @pl.when(pl.program_id(0) == 0)
        def _():
            carry[...] = jnp.zeros_like(carry)

        xv = fa_ref[...] + b_ref[...]
        logf = jnp.minimum(xv, 0.0) - jnp.log(1.0 + jnp.exp(-jnp.abs(xv)))
        csum = jnp.dot(tri_ref[...], logf, preferred_element_type=F32, precision=lax.Precision.HIGHEST) + carry[0:1, :]
        carry[0:1, :] = csum[tb - 1:tb, :]
        lane = _lane((tb, LANES))
        for h in range(N_HEADS):
            hi, mid, lo = _split3(csum[:, h:h + 1])
            pair = (h // 2) * LANES
            qv = q_ref[:, pair:pair + LANES].astype(F32)
            kv = k_ref[:, pair:pair + LANES].astype(F32)
            vv = v_ref[:, pair:pair + LANES].astype(F32)
            if h % 2:
                qv = pltpu.roll(qv, 64, axis=1)
                kv = pltpu.roll(kv, 64, axis=1)
                vv = pltpu.roll(vv, 64, axis=1)
            va_ref[:, h * LANES:(h + 1) * LANES] = jnp.where(lane < 64, vv, jnp.where(lane == 64, 1.0, 0.0)).astype(BF16)
            one = jnp.where((lane >= 67) & (lane < 70), 1.0, 0.0)
            q_x = jnp.where(lane == 64, hi, jnp.where(lane == 65, mid, jnp.where(lane == 66, lo, one)))
            one = jnp.where((lane >= 64) & (lane < 67), 1.0, 0.0)
            k_x = jnp.where(lane == 67, -hi, jnp.where(lane == 68, -mid, jnp.where(lane == 69, -lo, one)))
            qa_ref[:, h * LANES:(h + 1) * LANES] = jnp.where(lane < 64, qv * QK_SCALE, q_x).astype(BF16)
            ka_ref[:, h * LANES:(h + 1) * LANES] = jnp.where(lane < 64, kv, k_x).astype(BF16)

    return pl.pallas_call(
        body, grid=(n,),
        in_specs=[pl.BlockSpec((tb, ATT_W), lambda i: (i, Z_QA)), pl.BlockSpec((tb, ATT_W), lambda i: (i, Z_KA)),
                  pl.BlockSpec((tb, ATT_W), lambda i: (i, Z_VA)),
                  pl.BlockSpec((tb, LANES), lambda i: (i, 0)), pl.BlockSpec((1, LANES), lambda i: (0, 0)),
                  pl.BlockSpec((tb, tb), lambda i: (0, 0))],
        out_specs=[pl.BlockSpec((tb, N_HEADS * LANES), lambda i: (i, 0))] * 3,
        out_shape=[SDS((s, N_HEADS * LANES), BF16)] * 3,
        scratch_shapes=[pltpu.VMEM((8, LANES), F32)],
        name="fox_prep", compiler_params=_cp(("arbitrary",)))(z, z, z, fa, bfo, _tri(tb, False))


def _causal_pairs(n, k_major):
    if k_major:
        pairs = [(qi, kj) for kj in range(n) for qi in range(kj, n)]
    else:
        pairs = [(qi, kj) for qi in range(n) for kj in range(qi + 1)]
    return (jnp.asarray([p[0] for p in pairs], jnp.int32), jnp.asarray([p[1] for p in pairs], jnp.int32), len(pairs))


def fox_fwd(q_aug, k_aug, v_aug, gather=(), *, t=512, hps=4):
    s = v_aug.shape[0]
    qi_arr, kj_arr, n_pairs = _causal_pairs(s // t, False)
    ng = len(gather)
    n_groups = N_HEADS // hps

    def body(qi_ref, kj_ref, q_ref, k_ref, v_ref, *rest):
        g_ins, (o_ref, lse_ref), g_outs = rest[:ng], rest[ng:ng + 2], rest[ng + 2:2 * ng + 2]
        m_scr, acc_scr = rest[2 * ng + 2:2 * ng + 4]
        comm = (g_ins, g_outs) + tuple(rest[2 * ng + 4:]) + ([True] * ng,)
        step = pl.program_id(1)
        qi = qi_ref[step]
        kj = kj_ref[step]
        if ng:
            @pl.when((pl.program_id(0) == 0) & (step == 0))
            def _():
                _allgather_start(*comm)

        @pl.when(kj == 0)
        def _():
            m_scr[...] = jnp.full_like(m_scr, NEG)
            acc_scr[...] = jnp.zeros_like(acc_scr)

        def update(masked):
            for i in range(hps):
                sc = _nt(q_ref[:, i * LANES:(i + 1) * LANES], k_ref[:, i * LANES:(i + 1) * LANES])
                if masked:
                    sc = jnp.where(_row((t, t)) >= _lane((t, t)), sc, NEG)
                m_prev = m_scr[i]
                m_new = jnp.maximum(m_prev, jnp.max(sc, axis=-1, keepdims=True))
                p = jnp.exp((sc - jnp.tile(m_new, (1, t // LANES))).astype(BF16))
                acc_scr[i] = jnp.exp(m_prev - m_new) * acc_scr[i] + _nn(p, v_ref[:, i * LANES:(i + 1) * LANES])
                m_scr[i] = m_new

        @pl.when(kj < qi)
        def _():
            update(False)

        @pl.when(kj == qi)
        def _():
            update(True)
            lane = _lane((t, LANES))
            for pr in range(hps // 2):
                den = [acc_scr[2 * pr + i][:, 64:65] for i in range(2)]
                o_ref[:, pr * LANES:(pr + 1) * LANES] = jnp.where(
                    lane < 64, acc_scr[2 * pr] / den[0], pltpu.roll(acc_scr[2 * pr + 1] / den[1], 64, axis=1)).astype(o_ref.dtype)
                lse_ref[:, pr * LANES:(pr + 1) * LANES] = jnp.where(
                    lane < 64, m_scr[2 * pr] + jnp.log(den[0]), m_scr[2 * pr + 1] + jnp.log(den[1]))

        if ng:
            @pl.when((pl.program_id(0) == n_groups - 1) & (step == n_pairs - 1))
            def _():
                _allgather_finish(*comm)

    wide = hps * LANES
    grid_spec = pltpu.PrefetchScalarGridSpec(
        num_scalar_prefetch=2, grid=(n_groups, n_pairs),
        in_specs=[pl.BlockSpec((t, wide), lambda hg, st, qi, kj: (qi[st], hg)),
                  pl.BlockSpec((t, wide), lambda hg, st, qi, kj: (kj[st], hg)),
                  pl.BlockSpec((t, wide), lambda hg, st, qi, kj: (kj[st], hg))] + [ANY] * ng,
        out_specs=[pl.BlockSpec((t, wide // 2), lambda hg, st, qi, kj: (qi[st], hg))] * 2 + [ANY] * ng,
        scratch_shapes=[pltpu.VMEM((hps, t, LANES), F32)] * 2 + (_allgather_sems(ng) if ng else []))
    return pl.pallas_call(
        body, grid_spec=grid_spec, out_shape=[SDS((s, ATT_W), BF16), SDS((s, ATT_W), F32)] + _allgather_shapes(gather),
        name="fox_fwd", compiler_params=_cp(("arbitrary", "arbitrary")))(qi_arr, kj_arr, q_aug, k_aug, v_aug, *gather)


def fox_bwd(q_aug, k_aug, z, dy, lse, dd, exchange=(), *, t=512, hps=4):
    s = z.shape[0]
    qi_arr, kj_arr, n_pairs = _causal_pairs(s // t, True)
    ne = len(exchange)
    n_groups = N_HEADS // hps

    def body(qi_ref, kj_ref, q_ref, k_ref, v_ref, do_ref, lse_ref, dd_ref, *rest):
        e_ins, (dq_ref, dk_ref, dv_ref), e_outs = rest[:ne], rest[ne:ne + 3], rest[ne + 3:2 * ne + 3]
        comm = (e_ins, e_outs) + tuple(rest[2 * ne + 3:])
        step = pl.program_id(1)
        qi = qi_ref[step]
        kj = kj_ref[step]
        if ne:
            @pl.when((pl.program_id(0) == 0) & (step == 0))
            def _():
                _to_chips_start(*comm)

        @pl.when(step == 0)
        def _():
            dq_ref[...] = jnp.zeros_like(dq_ref)

        @pl.when(qi == kj)
        def _():
            dk_ref[...] = jnp.zeros_like(dk_ref)
            dv_ref[...] = jnp.zeros_like(dv_ref)

        def update(masked):
            lane = _lane((t, LANES))
            rows = pl.ds(pl.multiple_of(qi * t, t), t)
            for pr in range(hps // 2):
                pair = slice(pr * LANES, (pr + 1) * LANES)
                dov = do_ref[:, pair]
                dv_new = None
                for i in range(2):
                    head = (lane < 64) if i == 0 else (lane >= 64)
                    own = slice((2 * pr + i) * LANES, (2 * pr + i + 1) * LANES)
                    col = slice(pr * LANES + i * 64, pr * LANES + i * 64 + 1)
                    qv = q_ref[:, own]
                    kv = k_ref[:, own]
                    sc = _nt(qv, kv)
                    if masked:
                        sc = jnp.where(_row((t, t)) >= _lane((t, t)), sc, NEG)
                    p = jnp.exp(sc - lse_ref[:, col])
                    dp = _nt(jnp.where(head, dov, jnp.zeros_like(dov)), v_ref[:, pair])
                    ds = (p * (dp - dd_ref[:, col])).astype(BF16)
                    dq_ref[rows, own] += _nn(ds, kv)
                    dk_ref[:, own] += _tn(ds, qv)
                    dvi = _tn(p.astype(BF16), dov)
                    dv_new = dvi if dv_new is None else jnp.where(head, dvi, dv_new)
                dv_ref[:, pair] += dv_new

        @pl.when(kj < qi)
        def _():
            update(False)

        @pl.when(kj == qi)
        def _():
            update(True)

        if ne:
            @pl.when((pl.program_id(0) == n_groups - 1) & (step == n_pairs - 1))
            def _():
                _to_chips_finish(*comm)

    wide, half = hps * LANES, hps // 2 * LANES
    v_blk = Z_VA * ATT_W // half
    grid_spec = pltpu.PrefetchScalarGridSpec(
        num_scalar_prefetch=2, grid=(n_groups, n_pairs),
        in_specs=[pl.BlockSpec((t, wide), lambda hg, st, qi, kj: (qi[st], hg)),
                  pl.BlockSpec((t, wide), lambda hg, st, qi, kj: (kj[st], hg)),
                  pl.BlockSpec((t, half), lambda hg, st, qi, kj: (kj[st], v_blk + hg)),
                  pl.BlockSpec((t, half), lambda hg, st, qi, kj: (qi[st], hg)),
                  pl.BlockSpec((t, half), lambda hg, st, qi, kj: (qi[st], hg)),
                  pl.BlockSpec((t, half), lambda hg, st, qi, kj: (qi[st], hg))] + [ANY] * ne,
        out_specs=[pl.BlockSpec((s, wide), lambda hg, st, qi, kj: (0, hg)),
                   pl.BlockSpec((t, wide), lambda hg, st, qi, kj: (kj[st], hg)),
                   pl.BlockSpec((t, half), lambda hg, st, qi, kj: (kj[st], hg))] + [ANY] * ne,
        scratch_shapes=_to_chips_sems(ne) if ne else [])
    return pl.pallas_call(
        body, grid_spec=grid_spec,
        out_shape=[SDS((s, N_HEADS * LANES), F32), SDS((s, N_HEADS * LANES), F32), SDS((s, ATT_W), F32)]
        + _to_chips_shapes(exchange),
        name="fox_bwd", compiler_params=_cp(("arbitrary", "arbitrary")))(qi_arr, kj_arr, q_aug, k_aug, z, dy, lse, dd, *exchange)


def head_rowsum(a, b, *, tm=512, name):
    s = a.shape[0]

    def body(a_ref, b_ref, o_ref):
        prod = a_ref[...].astype(F32) * b_ref[...].astype(F32)
        lane = _lane((tm, LANES))
        lo = jnp.sum(jnp.where(lane < 64, prod, 0.0), axis=-1, keepdims=True)
        hi = jnp.sum(jnp.where(lane >= 64, prod, 0.0), axis=-1, keepdims=True)
        o_ref[...] = jnp.where(lane < 64, lo, hi)

    blk = pl.BlockSpec((tm, LANES), lambda i, j: (i, j))
    return pl.pallas_call(body, grid=(s // tm, 4), in_specs=[blk, blk], out_specs=blk, out_shape=SDS((s, ATT_W), F32),
                          name=name, compiler_params=_cp(("parallel", "parallel")))(a, b)


def fox_post(dq_aug, dk_aug, dv, fa, bfo, *, tb=512):
    s = dv.shape[0]
    n = s // tb

    def body(dq_ref, dk_ref, dv_ref, fa_ref, b_ref, tri_ref, dz_ref, dfa_ref, gb_ref, carry, acc):
        i = pl.program_id(0)

        @pl.when(i == 0)
        def _():
            carry[...] = jnp.zeros_like(carry)
            acc[...] = jnp.zeros_like(acc)

        lane = _lane((tb, LANES))
        d_f = jnp.zeros((tb, LANES), F32)
        for h in range(N_HEADS):
            col = dq_ref[:, h * LANES + 64:h * LANES + 65] - dk_ref[:, h * LANES + 67:h * LANES + 68]
            d_f = jnp.where(lane == h, col, d_f)
        suffix = jnp.dot(tri_ref[...], d_f, preferred_element_type=F32, precision=lax.Precision.HIGHEST) + carry[0:1, :]
        carry[0:1, :] = suffix[0:1, :]
        xv = fa_ref[...] + b_ref[...]
        dx = suffix * (1.0 / (1.0 + jnp.exp(xv)))
        dfa_ref[...] = dx.astype(dfa_ref.dtype)
        acc[...] += jnp.sum(dx.reshape(tb // 8, 8, LANES), axis=0)
        for hp in range(4):
            for src, off, scale in ((dq_ref, 0, QK_SCALE), (dk_ref, ATT_W, 1.0)):
                even = src[:, (2 * hp) * LANES:(2 * hp + 1) * LANES]
                odd = pltpu.roll(src[:, (2 * hp + 1) * LANES:(2 * hp + 2) * LANES], 64, axis=1)
                dz_ref[:, off + hp * LANES:off + (hp + 1) * LANES] = (jnp.where(lane < 64, even, odd) * scale).astype(BF16)
        dz_ref[:, 2 * ATT_W:3 * ATT_W] = dv_ref[...].astype(BF16)

        @pl.when(i == n - 1)
        def _():
            gb_ref[...] = jnp.sum(acc[...], axis=0, keepdims=True)

    rev = lambda i: (n - 1 - i, 0)
    return pl.pallas_call(
        body, grid=(n,),
        in_specs=[pl.BlockSpec((tb, N_HEADS * LANES), rev), pl.BlockSpec((tb, N_HEADS * LANES), rev),
                  pl.BlockSpec((tb, ATT_W), rev), pl.BlockSpec((tb, LANES), rev),
                  pl.BlockSpec((1, LANES), lambda i: (0, 0)), pl.BlockSpec((tb, tb), lambda i: (0, 0))],
        out_specs=[pl.BlockSpec((tb, 3 * ATT_W), rev), pl.BlockSpec((tb, LANES), rev),
                   pl.BlockSpec((1, LANES), lambda i: (0, 0))],
        out_shape=[SDS((s, 3 * ATT_W), BF16), SDS((s, LANES), BF16), SDS((1, LANES), F32)],
        scratch_shapes=[pltpu.VMEM((8, LANES), F32), pltpu.VMEM((8, LANES), F32)],
        name="fox_post", compiler_params=_cp(("arbitrary",)))(dq_aug, dk_aug, dv, fa, bfo, _tri(tb, True))


def rope_tables(s, sign):
    half = ROPE_DIM // 2
    inv_freq = ROPE_THETA ** (-jnp.arange(half, dtype=F32) * 2.0 / ROPE_DIM)
    ang = jnp.arange(s, dtype=F32)[:, None] * inv_freq[None, :]
    l64 = np.arange(LANES) % HEAD_DIM
    cos = jnp.cos(ang)[:, l64 % half]
    sin = jnp.sin(ang)[:, l64 % half] * sign
    first = jnp.asarray(l64 < half)[None, :]
    second = jnp.asarray((l64 >= half) & (l64 < ROPE_DIM))[None, :]
    return (jnp.where(first | second, cos, 1.0), jnp.where(first, -sin, 0.0), jnp.where(second, sin, 0.0))


def rope_apply(items, tabs, *, out_dtype, tm=512, name):
    s = items[0][0].shape[0]
    n_i = len(items)

    def body(*refs):
        c_ref, sn_ref, sp_ref = refs[n_i:n_i + 3]
        o_ref = refs[-1]
        for j, (_, _, scale, rotate) in enumerate(items):
            for b in range(4):
                xv = refs[j][:, b * LANES:(b + 1) * LANES].astype(F32)
                if rotate:
                    xv = xv * c_ref[...] + pltpu.roll(xv, LANES - 8, axis=1) * sn_ref[...] + pltpu.roll(xv, 8, axis=1) * sp_ref[...]
                o_ref[:, j * ATT_W + b * LANES:j * ATT_W + (b + 1) * LANES] = (xv * scale).astype(o_ref.dtype)

    in_specs = [pl.BlockSpec((tm, ATT_W), functools.partial(lambda i, blk: (i, blk), blk=it[1])) for it in items]
    in_specs += [pl.BlockSpec((tm, LANES), lambda i: (i, 0))] * 3
    return pl.pallas_call(
        body, grid=(s // tm,), in_specs=in_specs, out_specs=pl.BlockSpec((tm, n_i * ATT_W), lambda i: (i, 0)),
        out_shape=SDS((s, n_i * ATT_W), out_dtype), name=name, compiler_params=_cp(("parallel",)))(*[it[0] for it in items], *tabs)


def _dil_views(qk, z, r):
    s = z.shape[0]
    return qk.reshape(s // r, r * 2 * ATT_W), z.reshape(s // r, r * Z_W)


def _dil_cols(r):
    q_col = lambda rho, hp: rho * 8 + hp
    k_col = lambda rho, hp: rho * 8 + 4 + hp
    v_col = lambda rho, hp: rho * (Z_W // LANES) + 4 * Z_VB + hp
    return q_col, k_col, v_col


def _dil_scores(qv, kp, kc, head, has_prev):
    b = DIL_BLK
    qm = jnp.where(head, qv, jnp.zeros_like(qv))
    row, col = _row((b, b)), _lane((b, b))
    sp = jnp.where((col >= row) & has_prev, _nt(qm, kp), NEG)
    sc = jnp.where(col <= row, _nt(qm, kc), NEG)
    return sp, sc


def dil_fwd(qk, z, prev, *, r):
    s = z.shape[0]
    b = DIL_BLK
    l_sub = s // r
    nb = l_sub // b
    qk_v, z_v = _dil_views(qk, z, r)
    q_col, k_col, v_col = _dil_cols(r)
    merge = prev is not None

    def body(*refs):
        if merge:
            q_ref, kp_ref, kc_ref, vp_ref, vc_ref, op_ref, lp_ref, o_ref, l_ref = refs
        else:
            q_ref, kp_ref, kc_ref, vp_ref, vc_ref, o_ref, l_ref = refs
        has_prev = pl.program_id(2) > 0
        lane = _lane((b, LANES))
        res = []
        for i in range(2):
            head = (lane < 64) if i == 0 else (lane >= 64)
            sp, sc = _dil_scores(q_ref[...], kp_ref[...], kc_ref[...], head, has_prev)
            m = jnp.maximum(jnp.max(sp, axis=-1, keepdims=True), jnp.max(sc, axis=-1, keepdims=True))
            pp = jnp.exp(sp - m)
            pc = jnp.exp(sc - m)
            den = jnp.sum(pp, axis=-1, keepdims=True) + jnp.sum(pc, axis=-1, keepdims=True)
            ov = (_nn(pp.astype(BF16), vp_ref[...]) + _nn(pc.astype(BF16), vc_ref[...])) / den
            res.append((ov, m + jnp.log(den)))
        ov = jnp.where(lane < 64, res[0][0], res[1][0])
        lse = jnp.where(lane < 64, res[0][1], res[1][1])
        if merge:
            lp = lp_ref[...]
            m2 = jnp.maximum(lp, lse)
            wp = jnp.exp(lp - m2)
            wn = jnp.exp(lse - m2)
            ov = (wp * op_ref[...] + wn * ov) / (wp + wn)
            lse = m2 + jnp.log(wp + wn)
        o_ref[...] = ov
        l_ref[...] = lse

    blk = lambda f: pl.BlockSpec((b, LANES), f)
    in_specs = [blk(lambda rho, hp, n: (n, q_col(rho, hp))), blk(lambda rho, hp, n: (jnp.maximum(n - 1, 0), k_col(rho, hp))),
                blk(lambda rho, hp, n: (n, k_col(rho, hp))), blk(lambda rho, hp, n: (jnp.maximum(n - 1, 0), v_col(rho, hp))),
                blk(lambda rho, hp, n: (n, v_col(rho, hp)))]
    args = [qk_v, qk_v, qk_v, z_v, z_v]
    nat = blk(lambda rho, hp, n: (n, rho * 4 + hp))
    if merge:
        in_specs += [nat, nat]
        args += [prev[0].reshape(l_sub, r * ATT_W), prev[1].reshape(l_sub, r * ATT_W)]
    o, lse = pl.pallas_call(
        body, grid=(r, 4, nb), in_specs=in_specs, out_specs=[nat, nat],
        out_shape=[SDS((l_sub, r * ATT_W), F32)] * 2, name=f"dil_fwd_r{r}",
        compiler_params=_cp(("parallel", "parallel", "arbitrary")))(*args)
    return o.reshape(s, ATT_W), lse.reshape(s, ATT_W)


def dil_bwd_dq(qk, z, dy, lse, dd, acc, *, r):
    s = z.shape[0]
    b = DIL_BLK
    l_sub = s // r
    nb = l_sub // b
    qk_v, z_v = _dil_views(qk, z, r)
    q_col, k_col, v_col = _dil_cols(r)
    add = acc is not None

    def body(*refs):
        q_ref, kp_ref, kc_ref, vp_ref, vc_ref, do_ref, l_ref, dd_ref = refs[:8]
        dq_ref = refs[-1]
        has_prev = pl.program_id(2) > 0
        lane = _lane((b, LANES))
        dov = do_ref[...]
        parts = []
        for i in range(2):
            head = (lane < 64) if i == 0 else (lane >= 64)
            sp, sc = _dil_scores(q_ref[...], kp_ref[...], kc_ref[...], head, has_prev)
            lse_i = l_ref[:, i * 64:i * 64 + 1]
            dd_i = dd_ref[:, i * 64:i * 64 + 1]
            dom = jnp.where(head, dov, jnp.zeros_like(dov))
            dsp = (jnp.exp(sp - lse_i) * (_nt(dom, vp_ref[...]) - dd_i)).astype(BF16)
            dsc = (jnp.exp(sc - lse_i) * (_nt(dom, vc_ref[...]) - dd_i)).astype(BF16)
            parts.append(_nn(dsp, kp_ref[...]) + _nn(dsc, kc_ref[...]))
        dq = jnp.where(lane < 64, parts[0], parts[1])
        if add:
            dq = dq + refs[8][...]
        dq_ref[...] = dq

    blk = lambda f: pl.BlockSpec((b, LANES), f)
    nat = blk(lambda rho, hp, n: (n, rho * 4 + hp))
    in_specs = [blk(lambda rho, hp, n: (n, q_col(rho, hp))), blk(lambda rho, hp, n: (jnp.maximum(n - 1, 0), k_col(rho, hp))),
                blk(lambda rho, hp, n: (n, k_col(rho, hp))), blk(lambda rho, hp, n: (jnp.maximum(n - 1, 0), v_col(rho, hp))),
                blk(lambda rho, hp, n: (n, v_col(rho, hp))), nat, nat, nat]
    nview = lambda a: a.reshape(l_sub, r * ATT_W)
    args = [qk_v, qk_v, qk_v, z_v, z_v, nview(dy), nview(lse), nview(dd)]
    if add:
        in_specs.append(nat)
        args.append(nview(acc))
    dq = pl.pallas_call(
        body, grid=(r, 4, nb), in_specs=in_specs, out_specs=nat, out_shape=SDS((l_sub, r * ATT_W), F32),
        name=f"dil_bwd_dq_r{r}", compiler_params=_cp(("parallel", "parallel", "arbitrary")))(*args)
    return dq.reshape(s, ATT_W)


def dil_bwd_dkv(qk, z, dy, lse, dd, acc, *, r):
    s = z.shape[0]
    b = DIL_BLK
    l_sub = s // r
    nb = l_sub // b
    qk_v, z_v = _dil_views(qk, z, r)
    q_col, k_col, v_col = _dil_cols(r)
    add = acc is not None

    def body(*refs):
        k_ref, v_ref, qc_ref, qn_ref, doc_ref, don_ref, lc_ref, ln_ref, ddc_ref, ddn_ref = refs[:10]
        dk_ref, dv_ref = refs[-2:]
        has_next = pl.program_id(2) < nb - 1
        lane = _lane((b, LANES))
        row, col = _row((b, b)), _lane((b, b))
        kv = k_ref[...]
        vv = v_ref[...]
        dk_parts, dv_parts = [], []
        for i in range(2):
            head = (lane < 64) if i == 0 else (lane >= 64)
            dk_i = jnp.zeros((b, LANES), F32)
            dv_i = jnp.zeros((b, LANES), F32)
            for q_ref, do_ref, l_ref, d_ref, mask in ((qc_ref, doc_ref, lc_ref, ddc_ref, col <= row),
                                                      (qn_ref, don_ref, ln_ref, ddn_ref, (col >= row) & has_next)):
                qv = q_ref[...]
                dov = do_ref[...]
                sc = jnp.where(mask, _nt(jnp.where(head, qv, jnp.zeros_like(qv)), kv), NEG)
                p = jnp.exp(sc - l_ref[:, i * 64:i * 64 + 1])
                dp = _nt(jnp.where(head, dov, jnp.zeros_like(dov)), vv)
                ds = (p * (dp - d_ref[:, i * 64:i * 64 + 1])).astype(BF16)
                dv_i = dv_i + _tn(p.astype(BF16), dov)
                dk_i = dk_i + _tn(ds, qv)
            dk_parts.append(dk_i)
            dv_parts.append(dv_i)
        dk = jnp.where(lane < 64, dk_parts[0], dk_parts[1])
        dv = jnp.where(lane < 64, dv_parts[0], dv_parts[1])
        if add:
            dk = dk + refs[10][...]
            dv = dv + refs[11][...]
        dk_ref[...] = dk
        dv_ref[...] = dv

    blk = lambda f: pl.BlockSpec((b, LANES), f)
    nat = blk(lambda rho, hp, n: (n, rho * 4 + hp))
    nxt = blk(lambda rho, hp, n: (jnp.minimum(n + 1, nb - 1), rho * 4 + hp))
    in_specs = [blk(lambda rho, hp, n: (n, k_col(rho, hp))), blk(lambda rho, hp, n: (n, v_col(rho, hp))),
                blk(lambda rho, hp, n: (n, q_col(rho, hp))), blk(lambda rho, hp, n: (jnp.minimum(n + 1, nb - 1), q_col(rho, hp))),
                nat, nxt, nat, nxt, nat, nxt]
    nview = lambda a: a.reshape(l_sub, r * ATT_W)
    args = [qk_v, z_v, qk_v, qk_v, nview(dy), nview(dy), nview(lse), nview(lse), nview(dd), nview(dd)]
    if add:
        in_specs += [nat, nat]
        args += [nview(acc[0]), nview(acc[1])]
    dk, dv = pl.pallas_call(
        body, grid=(r, 4, nb), in_specs=in_specs, out_specs=[nat, nat],
        out_shape=[SDS((l_sub, r * ATT_W), F32)] * 2, name=f"dil_bwd_dkv_r{r}",
        compiler_params=_cp(("parallel", "parallel", "arbitrary")))(*args)
    return dk.reshape(s, ATT_W), dv.reshape(s, ATT_W)


def _dil_rows(base, r):
    if r == 1:
        return pl.ds(pl.multiple_of(base, DIL_BLK), DIL_BLK)
    return pl.ds(base, DIL_BLK, stride=r)


def _dil_block(idx, r, nb):
    shift = nb.bit_length() - 1
    rho = idx >> shift
    n = idx & (nb - 1)
    base = rho + n * (r * DIL_BLK)
    return _dil_rows(base, r), _dil_rows(jnp.maximum(base - r * DIL_BLK, rho), r), n > 0


def _cat(a, b):
    return jnp.concatenate([a, b], axis=0)


def _two_heads(v, first_head):
    zero = jnp.zeros_like(v)
    return _cat(jnp.where(first_head, v, zero), jnp.where(first_head, zero, v))


def _dil_bands():
    b = DIL_BLK
    q = _row((2 * b, 2 * b)) & (b - 1)
    col = _lane((2 * b, 2 * b))
    return (col < b) & (col >= q), (col >= b) & (col - b <= q)


def dil_fwd_all(qkv, *, unroll=8):
    s = qkv.shape[0]
    b = DIL_BLK
    n_blk = s // b

    def body(q_ref, k_ref, v_ref, o_ref, l_ref):
        first_head = _lane((b, LANES)) < 64
        band_prev, band_cur = _dil_bands()
        for g, (_, r) in enumerate(DIL_PATTERNS):
            nb = n_blk // r

            def group(it, carry, g=g, r=r, nb=nb):
                loaded = []
                for u in range(unroll):
                    rows_c, rows_p, has_prev = _dil_block(it * unroll + u, r, nb)
                    vals = [q_ref[rows_c, :].astype(BF16), k_ref[rows_p, :].astype(BF16), k_ref[rows_c, :].astype(BF16),
                            v_ref[rows_p, :].astype(BF16), v_ref[rows_c, :].astype(BF16)]
                    state = (o_ref[rows_c, :], l_ref[rows_c, :]) if g else None
                    loaded.append((rows_c, has_prev, vals, state))
                done = []
                for rows_c, has_prev, (qv, kp, kc, vp, vc), state in loaded:
                    sc = jnp.where(band_cur | (band_prev & has_prev), _nt(_two_heads(qv, first_head), _cat(kp, kc)), NEG)
                    m = jnp.max(sc, axis=-1, keepdims=True)
                    p = jnp.exp(sc - m)
                    den = jnp.sum(p, axis=-1, keepdims=True)
                    both = _nn(p.astype(BF16), _cat(vp, vc)) / den
                    lse2 = m + jnp.log(den)
                    ov = jnp.where(first_head, both[:b], both[b:])
                    lse = jnp.where(first_head, lse2[:b], lse2[b:])
                    if state is not None:
                        m2 = jnp.maximum(state[1], lse)
                        wp = jnp.exp(state[1] - m2)
                        wn = jnp.exp(lse - m2)
                        ov = (wp * state[0] + wn * ov) / (wp + wn)
                        lse = m2 + jnp.log(wp + wn)
                    done.append((rows_c, ov, lse))
                for rows_c, ov, lse in done:
                    o_ref[rows_c, :] = ov
                    l_ref[rows_c, :] = lse
                return carry

            lax.fori_loop(0, n_blk // unroll, group, 0)

    col_blk = lambda k: pl.BlockSpec((s, LANES), lambda hp: (0, 4 * k + hp))
    out = pl.BlockSpec((s, LANES), lambda hp: (0, hp))
    return pl.pallas_call(
        body, grid=(4,), in_specs=[col_blk(0), col_blk(1), col_blk(2)], out_specs=[out, out],
        out_shape=[SDS((s, ATT_W), F32)] * 2, name="dil_fwd", compiler_params=_cp(("parallel",)))(qkv, qkv, qkv)


def dil_bwd_all(qkv, dy, lse, y, exchange=(), *, unroll=8):
    s = qkv.shape[0]
    b = DIL_BLK
    n_blk = s // b
    ne = len(exchange)

    def body(q_ref, k_ref, v_ref, do_ref, l_ref, y_ref, *rest):
        e_ins, (dq_ref, dk_ref, dv_ref), e_outs = rest[:ne], rest[ne:ne + 3], rest[ne + 3:2 * ne + 3]
        comm = (e_ins, e_outs) + tuple(rest[2 * ne + 3:])
        if ne:
            @pl.when(pl.program_id(0) == 0)
            def _():
                _to_chips_start(*comm)

        dq_ref[...] = jnp.zeros_like(dq_ref)
        dk_ref[...] = jnp.zeros_like(dk_ref)
        dv_ref[...] = jnp.zeros_like(dv_ref)
        first_head = _lane((b, LANES)) < 64
        band_prev, band_cur = _dil_bands()
        for _, r in DIL_PATTERNS:
            nb = n_blk // r

            def group(it, carry, r=r, nb=nb):
                loaded = []
                for u in range(unroll):
                    rows_c, rows_p, has_prev = _dil_block(it * unroll + u, r, nb)
                    vals = [q_ref[rows_c, :].astype(BF16), k_ref[rows_p, :].astype(BF16), k_ref[rows_c, :].astype(BF16),
                            v_ref[rows_p, :].astype(BF16), v_ref[rows_c, :].astype(BF16), do_ref[rows_c, :],
                            l_ref[rows_c, :], y_ref[rows_c, :]]
                    loaded.append((rows_c, rows_p, has_prev, vals))
                done = []
                for rows_c, rows_p, has_prev, (qv, kp, kc, vp, vc, dof, lv, yv) in loaded:
                    q2 = _two_heads(qv, first_head)
                    do2 = _two_heads(dof.astype(BF16), first_head)
                    kcat, vcat = _cat(kp, kc), _cat(vp, vc)
                    lse2 = _cat(lv[:, 0:1], lv[:, 64:65])
                    dd2 = jnp.sum(_two_heads(dof * yv, first_head), axis=-1, keepdims=True)
                    p = jnp.exp(jnp.where(band_cur | (band_prev & has_prev), _nt(q2, kcat), NEG) - lse2)
                    ds = (p * (_nt(do2, vcat) - dd2)).astype(BF16)
                    dq2 = _nn(ds, kcat)
                    dkcat = _tn(ds, q2)
                    dvcat = _tn(p.astype(BF16), do2)
                    done.append((rows_c, rows_p, (jnp.where(first_head, dq2[:b], dq2[b:]), dkcat[:b], dkcat[b:],
                                                  dvcat[:b], dvcat[b:])))
                for rows_c, rows_p, (dq, dk_p, dk_c, dv_p, dv_c) in done:
                    dq_ref[rows_c, :] += dq
                    dk_ref[rows_p, :] += dk_p
                    dk_ref[rows_c, :] += dk_c
                    dv_ref[rows_p, :] += dv_p
                    dv_ref[rows_c, :] += dv_c
                return carry

            lax.fori_loop(0, n_blk // unroll, group, 0)

        if ne:
            @pl.when(pl.program_id(0) == 3)
            def _():
                _to_chips_finish(*comm)

    col_blk = lambda k: pl.BlockSpec((s, LANES), lambda hp: (0, 4 * k + hp))
    nat = pl.BlockSpec((s, LANES), lambda hp: (0, hp))
    return pl.pallas_call(
        body, grid=(4,), in_specs=[col_blk(0), col_blk(1), col_blk(2), nat, nat, nat] + [ANY] * ne,
        out_specs=[nat, nat, nat] + [ANY] * ne, out_shape=[SDS((s, ATT_W), F32)] * 3 + _to_chips_shapes(exchange),
        scratch_shapes=_to_chips_sems(ne) if ne else [], name="dil_bwd",
        compiler_params=_cp(("arbitrary",)))(qkv, qkv, qkv, dy, lse, y, *exchange)


def _sigmoid(v):
    return 1.0 / (1.0 + jnp.exp(-v))


def gate_mix(ya, yb, wa, wb, z, *, tm=512, tn=512):
    s = ya.shape[0]
    d = wa.shape[1]
    ga_blk = 3 * ATT_W * 2 // tn
    gb_blk = ga_blk + d // tn

    def body(ya_ref, yb_ref, wa_ref, wb_ref, ga_ref, gb_ref, pa_ref, pb_ref, mx_ref):
        pa = _nn(ya_ref[...], wa_ref[...])
        pb = _nn(yb_ref[...].astype(BF16), wb_ref[...])
        pa_ref[...] = pa.astype(BF16)
        pb_ref[...] = pb.astype(BF16)
        mx_ref[...] = (_sigmoid(ga_ref[...].astype(F32)) * pa + _sigmoid(gb_ref[...].astype(F32)) * pb).astype(BF16)

    out = pl.BlockSpec((tm, tn), lambda i, j: (i, j))
    return pl.pallas_call(
        body, grid=(s // tm, d // tn),
        in_specs=[pl.BlockSpec((tm, ATT_W), lambda i, j: (i, 0)), pl.BlockSpec((tm, ATT_W), lambda i, j: (i, 0)),
                  pl.BlockSpec((ATT_W, tn), lambda i, j: (0, j)), pl.BlockSpec((ATT_W, tn), lambda i, j: (0, j)),
                  pl.BlockSpec((tm, tn), lambda i, j: (i, ga_blk + j)), pl.BlockSpec((tm, tn), lambda i, j: (i, gb_blk + j))],
        out_specs=[out, out, out], out_shape=[SDS((s, d), BF16)] * 3, name="gate_mix",
        compiler_params=_cp(("parallel", "parallel")))(ya, yb, wa, wb, z, z)


def gate_bwd(dmx, z, pa, pb, *, tm=256):
    s, d = dmx.shape

    def body(dm_ref, ga_ref, gb_ref, pa_ref, pb_ref, dpa_ref, dpb_ref, dg_ref):
        dm = dm_ref[...].astype(F32)
        sa = _sigmoid(ga_ref[...].astype(F32))
        sb = _sigmoid(gb_ref[...].astype(F32))
        dpa_ref[...] = (dm * sa).astype(BF16)
        dpb_ref[...] = (dm * sb).astype(BF16)
        dg_ref[:, 0:d] = (dm * pa_ref[...].astype(F32) * sa * (1.0 - sa)).astype(BF16)
        dg_ref[:, d:2 * d] = (dm * pb_ref[...].astype(F32) * sb * (1.0 - sb)).astype(BF16)

    row = pl.BlockSpec((tm, d), lambda i: (i, 0))
    return pl.pallas_call(
        body, grid=(s // tm,),
        in_specs=[row, pl.BlockSpec((tm, d), lambda i: (i, 3)), pl.BlockSpec((tm, d), lambda i: (i, 4)), row, row],
        out_specs=[row, row, pl.BlockSpec((tm, 2 * d), lambda i: (i, 0))],
        out_shape=[SDS((s, d), BF16), SDS((s, d), BF16), SDS((s, 2 * d), BF16)], name="gate_bwd",
        compiler_params=_cp(("parallel",)))(dmx, z, z, pa, pb)


GELU_C = math.sqrt(2.0 / math.pi)


def _gelu_parts(a):
    inner = GELU_C * (a + 0.044715 * a * a * a)
    th = jnp.tanh(inner)
    gelu = 0.5 * a * (1.0 + th)
    dgelu = 0.5 * (1.0 + th) + 0.5 * a * (1.0 - th * th) * GELU_C * (1.0 + 3.0 * 0.044715 * a * a)
    return gelu, dgelu


def _causal_taps(u, before):
    row = _row(u.shape)
    r1 = jnp.where(row == 0, before[7:8, :], pltpu.roll(u, 1, axis=0))
    r2 = jnp.where(row == 0, before[6:7, :], jnp.where(row == 1, before[7:8, :], pltpu.roll(u, 2, axis=0)))
    return r1, r2


def ffn_up(h, wa, wb, cw, cb, *, tm=512, tn=256):
    s, d = h.shape
    f = wa.shape[1]
    nj = f // tn

    def body(h_ref, wa_ref, wb_ref, cwa_ref, cwb_ref, cba_ref, cbb_ref, ua_ref, ub_ref, m_ref, carry):
        @pl.when(pl.program_id(1) == 0)
        def _():
            carry[...] = jnp.zeros_like(carry)

        conv = []
        for k, (w_ref, cw_ref, cb_ref, u_ref) in enumerate(((wa_ref, cwa_ref, cba_ref, ua_ref), (wb_ref, cwb_ref, cbb_ref, ub_ref))):
            u16 = _nn(h_ref[...], w_ref[...]).astype(BF16)
            u_ref[...] = u16
            u = u16.astype(F32)
            r1, r2 = _causal_taps(u, carry[k])
            carry[k] = u[tm - 8:tm, :]
            conv.append(cw_ref[0:1, :] * r2 + cw_ref[1:2, :] * r1 + cw_ref[2:3, :] * u + cb_ref[...])
        m_ref[...] = (_gelu_parts(conv[0])[0] * conv[1]).astype(BF16)

    out = pl.BlockSpec((tm, tn), lambda j, i: (i, j))
    return pl.pallas_call(
        body, grid=(nj, s // tm),
        in_specs=[pl.BlockSpec((tm, d), lambda j, i: (i, 0)),
                  pl.BlockSpec((d, tn), lambda j, i: (0, j)), pl.BlockSpec((d, tn), lambda j, i: (0, j)),
                  pl.BlockSpec((3, tn), lambda j, i: (0, j)), pl.BlockSpec((3, tn), lambda j, i: (0, nj + j)),
                  pl.BlockSpec((1, tn), lambda j, i: (0, j)), pl.BlockSpec((1, tn), lambda j, i: (0, nj + j))],
        out_specs=[out, out, out], out_shape=[SDS((s, f), BF16)] * 3,
        scratch_shapes=[pltpu.VMEM((2, 8, tn), F32)], name="ffn_up",
        compiler_params=_cp(("parallel", "arbitrary")))(h, wa, wb, cw, cw, cb, cb)


def ffn_bwd(dm, ua, ub, cw, cb, *, tm=512, tn=256):
    s, f = dm.shape
    nj = f // tn
    ni = s // tm
    halo = 16

    def body(dm_ref, ua_ref, ub_ref, ha_ref, hb_ref, cwa_ref, cwb_ref, cba_ref, cbb_ref,
             dua_ref, dub_ref, ga_ref, gb_ref, carry):
        i = pl.program_id(1)

        @pl.when(i == 0)
        def _():
            carry[...] = jnp.zeros_like(carry)
            ga_ref[...] = jnp.zeros_like(ga_ref)
            gb_ref[...] = jnp.zeros_like(gb_ref)

        first_tile = i == ni - 1
        row = _row((tm, tn))
        dmv = dm_ref[...].astype(F32)
        us, taps, convs = [], [], []
        for u_ref, h_ref, cw_ref, cb_ref in ((ua_ref, ha_ref, cwa_ref, cba_ref), (ub_ref, hb_ref, cwb_ref, cbb_ref)):
            u = u_ref[...].astype(F32)
            before = jnp.where(first_tile, 0.0, h_ref[halo - 8:halo, :].astype(F32))
            r1, r2 = _causal_taps(u, before)
            us.append(u)
            taps.append((r1, r2))
            convs.append(cw_ref[0:1, :] * r2 + cw_ref[1:2, :] * r1 + cw_ref[2:3, :] * u + cb_ref[...])
        gelu, dgelu = _gelu_parts(convs[0])
        dcs = (dmv * convs[1] * dgelu, dmv * gelu)
        for k, (dc, cw_ref, du_ref, g_ref) in enumerate(((dcs[0], cwa_ref, dua_ref, ga_ref), (dcs[1], cwb_ref, dub_ref, gb_ref))):
            r1, r2 = taps[k]
            g_ref[0:1, :] += jnp.sum(dc * r2, axis=0, keepdims=True)
            g_ref[1:2, :] += jnp.sum(dc * r1, axis=0, keepdims=True)
            g_ref[2:3, :] += jnp.sum(dc * us[k], axis=0, keepdims=True)
            g_ref[3:4, :] += jnp.sum(dc, axis=0, keepdims=True)
            after = carry[k]
            n1 = jnp.where(row == tm - 1, after[0:1, :], pltpu.roll(dc, tm - 1, axis=0))
            n2 = jnp.where(row == tm - 2, after[0:1, :], jnp.where(row == tm - 1, after[1:2, :], pltpu.roll(dc, tm - 2, axis=0)))
            du_ref[...] = (cw_ref[2:3, :] * dc + cw_ref[1:2, :] * n1 + cw_ref[0:1, :] * n2).astype(BF16)
            carry[k] = dc[0:8, :]

    tile = pl.BlockSpec((tm, tn), lambda j, i: (ni - 1 - i, j))
    halo_spec = pl.BlockSpec((halo, tn), lambda j, i: (jnp.maximum((ni - 1 - i) * (tm // halo) - 1, 0), j))
    gspec = pl.BlockSpec((8, tn), lambda j, i: (0, j))
    return pl.pallas_call(
        body, grid=(nj, ni),
        in_specs=[tile, tile, tile, halo_spec, halo_spec,
                  pl.BlockSpec((3, tn), lambda j, i: (0, j)), pl.BlockSpec((3, tn), lambda j, i: (0, nj + j)),
                  pl.BlockSpec((1, tn), lambda j, i: (0, j)), pl.BlockSpec((1, tn), lambda j, i: (0, nj + j))],
        out_specs=[tile, tile, gspec, gspec],
        out_shape=[SDS((s, f), BF16), SDS((s, f), BF16), SDS((8, f), F32), SDS((8, f), F32)],
        scratch_shapes=[pltpu.VMEM((2, 8, tn), F32)], name="ffn_bwd",
        compiler_params=_cp(("parallel", "arbitrary")))(dm, ua, ub, ua, ub, cw, cw, cb, cb)


def adamw(w, g, m, v, *, name, tr=None):
    r = w.shape[0]
    rest = w.shape[1:]
    if tr is None:
        tr = r
        for cand in (256, 128, 64, 32, 16, 8):
            if r % cand == 0:
                tr = cand
                break

    def body(w_ref, g_ref, m_ref, v_ref, d_ref, nm_ref, nv_ref):
        gv = g_ref[...]
        mn = ADAM_B1 * m_ref[...] + (1.0 - ADAM_B1) * gv
        vn = ADAM_B2 * v_ref[...] + (1.0 - ADAM_B2) * (gv * gv)
        m_hat = mn / (1.0 - ADAM_B1 ** ADAM_STEP)
        v_hat = vn / (1.0 - ADAM_B2 ** ADAM_STEP)
        d_ref[...] = -ADAM_LR * (m_hat / (jnp.sqrt(v_hat) + ADAM_EPS) + ADAM_WD * w_ref[...])
        nm_ref[...] = mn
        nv_ref[...] = vn

    blk = pl.BlockSpec((tr,) + rest, lambda i: (i,) + (0,) * len(rest))
    return pl.pallas_call(body, grid=(r // tr,), in_specs=[blk] * 4, out_specs=[blk] * 3, out_shape=[SDS(w.shape, F32)] * 3,
                          name=name, compiler_params=_cp(("parallel",)))(w, g, m, v)


def adamw_rows_view(w, g, m, v, *, name, tc=256):
    r, _, c = w.shape

    def body(w_ref, g_ref, m_ref, v_ref, d_ref, nm_ref, nv_ref, go_ref):
        gv = g_ref[...][:, None, :]
        mn = ADAM_B1 * m_ref[...] + (1.0 - ADAM_B1) * gv
        vn = ADAM_B2 * v_ref[...] + (1.0 - ADAM_B2) * (gv * gv)
        m_hat = mn / (1.0 - ADAM_B1 ** ADAM_STEP)
        v_hat = vn / (1.0 - ADAM_B2 ** ADAM_STEP)
        d_ref[...] = -ADAM_LR * (m_hat / (jnp.sqrt(v_hat) + ADAM_EPS) + ADAM_WD * w_ref[...])
        nm_ref[...] = mn
        nv_ref[...] = vn
        go_ref[...] = gv

    b3 = pl.BlockSpec((r, 1, tc), lambda i: (0, 0, i))
    b2 = pl.BlockSpec((r, tc), lambda i: (0, i))
    return pl.pallas_call(body, grid=(c // tc,), in_specs=[b3, b2, b3, b3], out_specs=[b3] * 4,
                          out_shape=[SDS(w.shape, F32)] * 4, name=name, compiler_params=_cp(("parallel",)))(w, g, m, v)


ANY = pl.BlockSpec(memory_space=pl.ANY)
ICI_KINDS = ("x", "y", "xy")


def _coords():
    return lax.axis_index("x"), lax.axis_index("y"), lax.axis_index("c")


def _peer(kind, x, y, c):
    if kind == "c":
        return (x, y, 1 - c)
    if kind == "x":
        return (1 - x, y, c)
    if kind == "y":
        return (x, 1 - y, c)
    return (1 - x, 1 - y, c)


def _chip_of(p):
    return 2 * p[0] + p[1]


def _half(rows, which):
    h = rows // 2
    return pl.ds(pl.multiple_of(which * h, 16), h)


def _remote(src, dst, send_sem, recv_sem, to):
    return pltpu.make_async_remote_copy(src_ref=src, dst_ref=dst, send_sem=send_sem, recv_sem=recv_sem,
                                        device_id=to, device_id_type=MESH)


def allgather_chips(shards, halved, *, name):
    n = len(shards)

    def body(*refs):
        parts = (refs[:n], refs[n:2 * n], refs[2 * n], refs[2 * n + 1], halved)
        _allgather_start(*parts)
        _allgather_finish(*parts)

    return pl.pallas_call(
        body, in_specs=[ANY] * n, out_specs=[ANY] * n,
        out_shape=_allgather_shapes(shards), scratch_shapes=_allgather_sems(n), name=name)(*shards)


def _allgather_shapes(shards):
    return [SDS((4,) + a.shape, a.dtype) for a in shards]


def _allgather_sems(n):
    return [pltpu.SemaphoreType.DMA((n, 6)), pltpu.SemaphoreType.DMA((n, 6))]


def _allgather_rows(ref, is_halved, which):
    r = ref.shape[0]
    return _half(r, which) if is_halved else pl.ds(0, r)


def _allgather_first(ins, outs, send_sems, recv_sems, halved):
    x, y, c = _coords()
    my_chip = 2 * x + y
    cps = []
    for w in range(len(ins)):
        rows = _allgather_rows(ins[w], halved[w], c)
        for k, kind in enumerate(ICI_KINDS):
            cps.append(_remote(ins[w].at[rows], outs[w].at[my_chip, rows], send_sems.at[w, k], recv_sems.at[w, k],
                               _peer(kind, x, y, c)))
    return cps


def _allgather_start(ins, outs, send_sems, recv_sems, halved):
    for cp in _allgather_first(ins, outs, send_sems, recv_sems, halved):
        cp.start()


def _allgather_finish(ins, outs, send_sems, recv_sems, halved):
    x, y, c = _coords()
    me = (x, y, c)
    second = []
    for w in range(len(ins)):
        for k, kind in enumerate(ICI_KINDS):
            landed = outs[w].at[_chip_of(_peer(kind, x, y, c)), _allgather_rows(ins[w], halved[w], c)]
            _remote(landed, landed, send_sems.at[w, k], recv_sems.at[w, k], me).wait_recv()
            if halved[w]:
                cp = _remote(landed, landed, send_sems.at[w, 3 + k], recv_sems.at[w, 3 + k], _peer("c", x, y, c))
                cp.start()
                second.append(cp)
    for w in range(len(ins)):
        if halved[w]:
            for k, kind in enumerate(ICI_KINDS):
                other = outs[w].at[_chip_of(_peer(kind, x, y, c)), _allgather_rows(ins[w], True, 1 - c)]
                _remote(other, other, send_sems.at[w, 3 + k], recv_sems.at[w, 3 + k], me).wait_recv()
    for cp in _allgather_first(ins, outs, send_sems, recv_sems, halved) + second:
        cp.wait_send()


def _half_of(ref, by_cols, which):
    lead = (slice(None),) * (len(ref.shape) - 2)
    if by_cols:
        h = ref.shape[-1] // 2
        return ref.at[lead + (slice(None), pl.ds(pl.multiple_of(which * h, LANES), h))]
    return ref.at[lead + (_half(ref.shape[-2], which),)]


def _half_shape(shape, by_cols):
    return shape[:-1] + (shape[-1] // 2,) if by_cols else shape[:-2] + (shape[-2] // 2, shape[-1])


def grads_to_sibling(gs, by_cols, *, name):
    n = len(gs)

    def body(*refs):
        ins, outs = refs[:n], refs[n:2 * n]
        send_sems, recv_sems = refs[2 * n:]
        x, y, c = _coords()
        cps = []
        for w in range(n):
            cp = _remote(_half_of(ins[w], by_cols[w], 1 - c), outs[w], send_sems.at[w], recv_sems.at[w], _peer("c", x, y, c))
            cp.start()
            cps.append(cp)
        for cp in cps:
            cp.wait()

    return pl.pallas_call(
        body, in_specs=[ANY] * n, out_specs=[ANY] * n,
        out_shape=[SDS(_half_shape(a.shape, bc), a.dtype) for a, bc in zip(gs, by_cols)],
        scratch_shapes=[pltpu.SemaphoreType.DMA((n,)), pltpu.SemaphoreType.DMA((n,))], name=name)(*gs)


def grads_to_chips(ps, *, name):
    n = len(ps)

    def body(*refs):
        parts = (refs[:n], refs[n:2 * n], refs[2 * n], refs[2 * n + 1])
        _to_chips_start(*parts)
        _to_chips_finish(*parts)

    return pl.pallas_call(
        body, in_specs=[ANY] * n, out_specs=[ANY] * n,
        out_shape=_to_chips_shapes(ps), scratch_shapes=_to_chips_sems(n), name=name)(*ps)


def _to_chips_shapes(ps):
    return [SDS((3,) + a.shape[1:], a.dtype) for a in ps]


def _to_chips_sems(n):
    return [pltpu.SemaphoreType.DMA((n, 3)), pltpu.SemaphoreType.DMA((n, 3))]


def _to_chips_copies(ins, outs, send_sems, recv_sems):
    x, y, c = _coords()
    cps = []
    for w in range(len(ins)):
        for k, kind in enumerate(ICI_KINDS):
            to = _peer(kind, x, y, c)
            cps.append(_remote(ins[w].at[_chip_of(to)], outs[w].at[k], send_sems.at[w, k], recv_sems.at[w, k], to))
    return cps


def _to_chips_start(ins, outs, send_sems, recv_sems):
    for cp in _to_chips_copies(ins, outs, send_sems, recv_sems):
        cp.start()


def _to_chips_finish(ins, outs, send_sems, recv_sems):
    for cp in _to_chips_copies(ins, outs, send_sems, recv_sems):
        cp.wait()


def halves_to_full(hs, by_cols, *, name):
    n = len(hs)

    def body(*refs):
        ins, outs = refs[:n], refs[n:2 * n]
        send_sems, recv_sems = refs[2 * n:]
        x, y, c = _coords()
        cps = []
        for w in range(n):
            cp = _remote(ins[w], _half_of(outs[w], by_cols[w], c), send_sems.at[w], recv_sems.at[w], _peer("c", x, y, c))
            cp.start()
            cps.append(cp)
        for cp in cps:
            cp.wait()

    return pl.pallas_call(
        body, in_specs=[ANY] * n, out_specs=[ANY] * n,
        out_shape=[SDS((a.shape[0], 2 * a.shape[1]) if bc else (2 * a.shape[0], a.shape[1]), a.dtype)
                   for a, bc in zip(hs, by_cols)],
        scratch_shapes=[pltpu.SemaphoreType.DMA((n,)), pltpu.SemaphoreType.DMA((n,))],
        name=name)(*hs)


def _row_tile(rows):
    for cand in (256, 192, 176, 128, 64, 32, 16):
        if rows % cand == 0:
            return cand
    return rows


def chip_sum(g, recv, c_arr, by_cols, *, name):
    _, r, cols = g.shape

    def body(c_ref, g_ref, r_ref, f_ref, b_ref):
        tot = g_ref[...] + r_ref[...]
        f_ref[...] = tot
        b_ref[...] = tot.astype(BF16)

    if by_cols:
        tc = 2 * LANES
        nblk = cols // 2 // tc
        shape = (4, r, cols // 2)
        blk = pl.BlockSpec((None, r, tc), lambda j, i, c_ref: (j, 0, i))
        mine = pl.BlockSpec((None, r, tc), lambda j, i, c_ref: (j, 0, c_ref[0] * nblk + i))
    else:
        tr = _row_tile(r // 2)
        nblk = r // 2 // tr
        shape = (4, r // 2, cols)
        blk = pl.BlockSpec((None, tr, cols), lambda j, i, c_ref: (j, i, 0))
        mine = pl.BlockSpec((None, tr, cols), lambda j, i, c_ref: (j, c_ref[0] * nblk + i, 0))
    grid_spec = pltpu.PrefetchScalarGridSpec(num_scalar_prefetch=1, grid=(4, nblk), in_specs=[mine, blk], out_specs=[blk, blk])
    return pl.pallas_call(body, grid_spec=grid_spec, out_shape=[SDS(shape, F32), SDS(shape, BF16)],
                          name=name, compiler_params=_cp(("parallel", "parallel")))(c_arr, g, recv)


def final_sum(pf, recv, chip_arr, *, name):
    _, h, cols = pf.shape
    tr = _row_tile(h)

    def body(chip_ref, p_ref, r_ref, o_ref):
        o_ref[...] = ((p_ref[...] + r_ref[0].astype(F32)) + r_ref[1].astype(F32)) + r_ref[2].astype(F32)

    grid_spec = pltpu.PrefetchScalarGridSpec(
        num_scalar_prefetch=1, grid=(h // tr,),
        in_specs=[pl.BlockSpec((None, tr, cols), lambda i, chip_ref: (chip_ref[0], i, 0)),
                  pl.BlockSpec((3, tr, cols), lambda i, chip_ref: (0, i, 0))],
        out_specs=pl.BlockSpec((tr, cols), lambda i, chip_ref: (i, 0)))
    return pl.pallas_call(body, grid_spec=grid_spec, out_shape=SDS((h, cols), F32), name=name,
                          compiler_params=_cp(("parallel",)))(chip_arr, pf, recv)


def allreduce_small(v, *, name):
    rws, cols = v.shape

    def body(v_ref, all_ref, sum_ref, send_sems, recv_sems, local_sem):
        x, y, c = _coords()
        me, sibling = (x, y, c), (x, y, 1 - c)
        chips = [(1 - x, y), (x, 1 - y), (1 - x, 1 - y)]

        def rows(px, py, pc):
            return all_ref.at[pl.ds(pl.multiple_of((4 * px + 2 * py + pc) * rws, 8), rws), :]

        def copy(k, block, to, src=None):
            return _remote(rows(*block) if src is None else src, rows(*block), send_sems.at[k], recv_sems.at[k], to)

        mine = pltpu.make_async_copy(v_ref, rows(*me), local_sem)
        mine.start()
        first = [copy(0, me, sibling, src=v_ref)]
        first += [copy(1 + j, me, (*chip, c), src=v_ref) for j, chip in enumerate(chips)]
        for cp in first:
            cp.start()
        passed = [copy(4 + j, (*chip, c), sibling) for j, chip in enumerate(chips)]
        for j, chip in enumerate(chips):
            copy(1 + j, (*chip, c), me).wait_recv()
            passed[j].start()
        copy(0, sibling, me).wait_recv()
        for j, chip in enumerate(chips):
            copy(4 + j, (*chip, 1 - c), me).wait_recv()
        for cp in first + passed:
            cp.wait_send()
        mine.wait()
        tot = all_ref[0:rws, :]
        for dev in range(1, 8):
            tot = tot + all_ref[dev * rws:(dev + 1) * rws, :]
        sum_ref[...] = tot

    vm = pl.BlockSpec(memory_space=pltpu.VMEM)
    return pl.pallas_call(
        body, in_specs=[vm], out_specs=[vm, vm],
        out_shape=[SDS((8 * rws, cols), v.dtype), SDS((rws, cols), v.dtype)],
        scratch_shapes=[pltpu.SemaphoreType.DMA((7,)), pltpu.SemaphoreType.DMA((7,)), pltpu.SemaphoreType.DMA],
        name=name)(v)[1]


def _pack_rows(parts, rows):
    out = []
    for a, r in zip(parts, rows):
        flat = a.reshape(-1)
        flat = jnp.pad(flat, (0, r * LANES - flat.shape[0]))
        out.append(flat.reshape(r, LANES))
    return jnp.concatenate(out, axis=0)


def _unpack_rows(packed, shapes, rows):
    out, at = [], 0
    for shp, r in zip(shapes, rows):
        size = int(np.prod(shp))
        out.append(packed[at:at + r].reshape(-1)[:size].reshape(shp))
        at += r
    return out


def kernel(x, g_pre_mix, w_in, b_forget, w_o_fox, w_o_dil, w_out, g_post_mix, g_pre_ffn, w_up, conv_w, conv_b, w_down, g_post_ffn, loss_target, m_g_pre_mix, m_w_in, m_b_forget, m_w_o_fox, m_w_o_dil, m_w_out, m_g_post_mix, m_g_pre_ffn, m_w_up, m_conv_w, m_conv_b, m_w_down, m_g_post_ffn, v_g_pre_mix, v_w_in, v_b_forget, v_w_o_fox, v_w_o_dil, v_w_out, v_g_post_mix, v_g_pre_ffn, v_w_up, v_conv_w, v_conv_b, v_w_down, v_g_post_ffn):
    xi, yi, ci = _coords()
    chip = 2 * xi + yi
    c_arr = jnp.reshape(ci, (1,)).astype(jnp.int32)
    chip_arr = jnp.reshape(chip, (1,)).astype(jnp.int32)
    xs = x[0]
    target = loss_target[0]
    s, d = xs.shape
    f_half = w_down.shape[1] * 4
    cols_in = w_in.shape[2]

    big = (w_in, w_o_fox, w_o_dil, w_out, w_up, w_down)
    shards = [w[0].astype(BF16) for w in big]
    a_in, a_cw = allgather_chips([shards[0], conv_w[0]], [True, False], name="allgather_w_in")
    w_in_full = jnp.concatenate([jnp.where(chip == j, shards[0], a_in[j]) for j in range(4)], axis=1)
    cw = jnp.concatenate([jnp.where(chip == j, conv_w[0], a_cw[j]) for j in range(4)], axis=1)
    nf = N_HEADS
    e_a, e_b = 3 * ATT_W, 3 * ATT_W + nf
    wz = jnp.concatenate([w_in_full[:, :e_a], w_in_full[:, e_b:]], axis=1)
    wf = jnp.pad(w_in_full[:, e_a:e_b], ((0, 0), (0, LANES - nf)))
    cb = conv_b
    bfo = jnp.pad(b_forget, ((0, 0), (0, LANES - nf)))

    h1 = rmsnorm_fwd(xs, g_pre_mix)
    z = mm([(h1, d, 0)], [(wz, d, 0)], nt=False, out_dtype=BF16, tm=1024, tn=512, name="in_proj")
    fa = mm([(h1, d, 0)], [(wf, d, 0)], nt=False, out_dtype=F32, tm=1024, tn=LANES, name="in_proj_forget")
    q_aug, k_aug, v_aug = fox_prep(z, fa, bfo)
    ya, lse_a, *late = fox_fwd(q_aug, k_aug, v_aug, gather=shards[1:])
    a_of, a_od, a_out, a_up, a_down = [
        lax.dynamic_update_index_in_dim(a4, own, chip, 0) for a4, own in zip(late, shards[1:])]
    wo_a = jnp.concatenate([a_of[j] for j in range(4)], axis=1)
    wo_b = jnp.concatenate([a_od[j] for j in range(4)], axis=1)
    w_o = a_out.reshape(d, d)
    w_dn = a_down.reshape(f_half, d)
    wu_a = jnp.concatenate([a_up[0], a_up[1]], axis=1)
    wu_b = jnp.concatenate([a_up[2], a_up[3]], axis=1)
    qkv_b = rope_apply([(z, Z_QB, QK_SCALE, True), (z, Z_KB, 1.0, True), (z, Z_VB, 1.0, False)], rope_tables(s, 1.0),
                       out_dtype=F32, name="rope_fwd")
    yb, lse_b = dil_fwd_all(qkv_b)
    pa, pb, mixed = gate_mix(ya, yb, wo_a, wo_b, z)
    y1, x1, h2 = proj_norm_res(mixed, w_o, g_post_mix, xs, g_pre_ffn, name="out_proj")
    ua, ub, mid = ffn_up(h2, wu_a, wu_b, cw, cb)
    dout, dy2, gg_post_ffn, sq = proj_norm_loss(mid, w_dn, g_post_ffn, x1, target, name="down_proj")
    loss = lax.psum(0.5 * sq[0, 0] / d, ("x", "y", "c"))

    dmid = mm([(dy2, d, 0)], [(w_dn, d, 0)], nt=True, out_dtype=BF16, tm=512, tn=f_half // 2, name="down_dgrad")
    dw_down = wgrad((mid, f_half, 0), dy2, tk=f_half // 2, tn=1024, ts=1024, name="down_wgrad")
    dua, dub, gc_a, gc_b = ffn_bwd(dmid, ua, ub, cw, cb)
    dx1, dy1, gg_pre_ffn, gg_post_mix = mm_norm_bwd(
        [(dua, f_half, 0), (dub, f_half, 0)], [(wu_a, f_half, 0), (wu_b, f_half, 0)],
        [(x1, g_pre_ffn, dout, F32), (y1, g_post_mix, None, BF16)], name="up_dgrad")
    dw_up = jnp.concatenate(
        [wgrad((h2, d, 0), du, tk=1024, tn=f_half // 2, ts=1024, name=f"up_wgrad_{k}", chip_major=True)
         for k, du in enumerate((dua, dub))], axis=0)
    def to_chip_sums(gs, nms, tag, by_cols=False):
        from_sib = grads_to_sibling(gs, [by_cols] * len(gs), name=f"grads_to_sibling_{tag}")
        return [chip_sum(g, r, c_arr, by_cols, name=f"chip_sum_{nm}") for g, r, nm in zip(gs, from_sib, nms)]

    sums_ffn = to_chip_sums([dw_up, dw_down.reshape(4, f_half // 4, d)], ("w_up", "w_down"), "ffn")
    dmixed = mm([(dy1, d, 0)], [(w_o, d, 0)], nt=True, out_dtype=BF16, tm=512, tn=512, name="out_dgrad")
    dw_out = wgrad((mixed, d, 0), dy1, tk=1024, tn=1024, ts=1024, name="out_wgrad")
    dpa, dpb, dz_g = gate_bwd(dmixed, z, pa, pb)
    dya = mm([(dpa, d, 0)], [(wo_a, d, 0)], nt=True, out_dtype=BF16, tm=512, tn=ATT_W, name="fox_o_dgrad")
    dyb = mm([(dpb, d, 0)], [(wo_b, d, 0)], nt=True, out_dtype=F32, tm=512, tn=ATT_W, name="dil_o_dgrad")
    by_chip_cols = lambda a: jnp.stack([a[:, j * (d // 4):(j + 1) * (d // 4)] for j in range(4)], axis=0)
    dw_of = by_chip_cols(wgrad((ya, ATT_W, 0), dpa, tk=ATT_W, tn=d, ts=1024, name="fox_o_wgrad"))
    dw_od = by_chip_cols(wgrad((yb, ATT_W, 0), dpb, tk=ATT_W, tn=d, ts=1024, name="dil_o_wgrad"))
    sums_mix = to_chip_sums([dw_of, dw_od, dw_out.reshape(4, d // 4, d)], ("w_o_fox", "w_o_dil", "w_out"), "mix")
    dd_a = head_rowsum(dya, ya, name="fox_delta")
    dq_aug, dk_aug, dv_a, *got_ffn = fox_bwd(q_aug, k_aug, z, dya, lse_a, dd_a, exchange=[p[1] for p in sums_ffn])
    dz_a, dfa, gg_bf = fox_post(dq_aug, dk_aug, dv_a, fa, bfo)
    dq_b, dk_b, dv_b, *got_mix = dil_bwd_all(qkv_b, dyb, lse_b, yb, exchange=[p[1] for p in sums_mix])
    dz_b = rope_apply([(dq_b, 0, QK_SCALE, True), (dk_b, 0, 1.0, True), (dv_b, 0, 1.0, False)],
                      rope_tables(s, -1.0), out_dtype=BF16, name="rope_bwd")
    dwt_a = wgrad((dz_a, e_a, 0), h1, tk=e_a // 2, tn=d, ts=1024, name="in_wgrad_a")
    dwt_b = wgrad((dz_b, e_a, 0), h1, tk=e_a // 2, tn=d, ts=1024, name="in_wgrad_b")
    dwt_g = wgrad((dz_g, 2 * d, 0), h1, tk=d, tn=d, ts=1024, name="in_wgrad_g")
    dwt_f = wgrad((dfa, LANES, 0), h1, tk=LANES, tn=d, ts=1024, name="in_wgrad_f")
    dwt_full = jnp.concatenate([dwt_a, dwt_f[:nf], dwt_b, dwt_g], axis=0)
    dw_in = jnp.stack([dwt_full[j * cols_in:(j + 1) * cols_in] for j in range(4)], axis=0)
    sums_in = to_chip_sums([dw_in], ("w_in",), "in", by_cols=True)
    grad_x, gg_pre_mix, *got_in = mm_norm_bwd(
        [(dz_a, e_a, 0), (dz_b, e_a, 0), (dz_g, d, 0), (dz_g, d, 1), (dfa, LANES, 0)],
        [(wz, e_a, 0), (wz, e_a, 1), (wz, d, 3), (wz, d, 4), (wf, LANES, 0)],
        [(xs, g_pre_mix, dx1, F32)], exchange=[sums_in[0][1]], name="in_dgrad")

    names = ("w_in", "w_o_fox", "w_o_dil", "w_out", "w_up", "w_down")
    sums = sums_in + sums_mix + sums_ffn
    from_chips = list(got_in) + list(got_mix) + list(got_ffn)
    halves = [final_sum(p[0], r, chip_arr, name=f"final_sum_{nm}") for p, r, nm in zip(sums, from_chips, names)]
    from_half = halves_to_full(halves, [True] + [False] * 5, name="halves_to_full")
    g_big = [lax.dynamic_update_slice_in_dim(full, mine, ci * mine.shape[k == 0], axis=int(k == 0))
             for k, (full, mine) in enumerate(zip(from_half, halves))]
    upd_big = [adamw(w[0], g, m[0], v[0], name=f"adamw_{nm}") for w, g, m, v, nm in list(zip(
        big, g_big, (m_w_in, m_w_o_fox, m_w_o_dil, m_w_out, m_w_up, m_w_down),
        (v_w_in, v_w_o_fox, v_w_o_dil, v_w_out, v_w_up, v_w_down), names))[1:]]
    to_t = lambda a: jnp.transpose(a, (2, 0, 1))
    from_t = lambda a: jnp.transpose(a, (1, 2, 0))
    *upd_in, g_in_t = adamw_rows_view(to_t(w_in), g_big[0], to_t(m_w_in), to_t(v_w_in), name="adamw_w_in")

    g_cw_loc = jnp.concatenate([gc_a[0:3], gc_b[0:3]], axis=1)
    g_cb_loc = jnp.concatenate([gc_a[3:4], gc_b[3:4]], axis=1)
    small_loc = [gg_pre_mix, gg_post_mix, gg_pre_ffn, gg_post_ffn, g_cb_loc, gg_bf[:, :nf], g_cw_loc]
    red_rows = (8, 8, 8, 8, 48, 8, 136)
    red = allreduce_small(_pack_rows(small_loc, red_rows), name="allreduce_small")
    g_pm, g_qm, g_pf, g_qf, g_cb, g_bf, g_cw_full = _unpack_rows(red, [a.shape for a in small_loc], red_rows)
    cols_cw = conv_w.shape[2]
    g_cw = lax.dynamic_slice_in_dim(g_cw_full, chip * cols_cw, cols_cw, axis=1)
    small_w = (g_pre_mix, g_post_mix, g_pre_ffn, g_post_ffn, conv_b, b_forget, conv_w[0])
    small_m = (m_g_pre_mix, m_g_post_mix, m_g_pre_ffn, m_g_post_ffn, m_conv_b, m_b_forget, m_conv_w[0])
    small_v = (v_g_pre_mix, v_g_post_mix, v_g_pre_ffn, v_g_post_ffn, v_conv_b, v_b_forget, v_conv_w[0])
    small_g = (g_pm, g_qm, g_pf, g_qf, g_cb, g_bf, g_cw)
    ad_rows = (8, 8, 8, 8, 48, 8, 40)
    packed = [_pack_rows(t, ad_rows) for t in (small_w, small_g, small_m, small_v)]
    upd_small = [_unpack_rows(o, [a.shape for a in small_w], ad_rows) for o in adamw(*packed, name="adamw_small")]

    order = ("g_pre_mix", "w_in", "b_forget", "w_o_fox", "w_o_dil", "w_out", "g_post_mix", "g_pre_ffn", "w_up", "conv_w",
             "conv_b", "w_down", "g_post_ffn")
    small_names = ("g_pre_mix", "g_post_mix", "g_pre_ffn", "g_post_ffn", "conv_b", "b_forget", "conv_w")
    grads, deltas, new_ms, new_vs = {}, {}, {}, {}
    grads["w_in"] = from_t(g_in_t)
    deltas["w_in"], new_ms["w_in"], new_vs["w_in"] = (from_t(a) for a in upd_in)
    for k, nm in enumerate(names[1:]):
        grads[nm] = g_big[k + 1][None]
        deltas[nm], new_ms[nm], new_vs[nm] = (a[None] for a in upd_big[k])
    for k, nm in enumerate(small_names):
        lead = (lambda a: a[None]) if nm == "conv_w" else (lambda a: a)
        grads[nm] = lead(small_g[k])
        deltas[nm], new_ms[nm], new_vs[nm] = (lead(upd_small[j][k]) for j in range(3))
    return (loss, grad_x[None], *[grads[nm] for nm in order], *[deltas[nm] for nm in order],
            *[new_ms[nm] for nm in order], *[new_vs[nm] for nm in order])
```

```python
import functools
import math

import numpy as np
import jax
import jax.numpy as jnp
from jax import lax
from jax.experimental import pallas as pl
from jax.experimental.pallas import tpu as pltpu

F32 = jnp.float32
BF16 = jnp.bfloat16
SDS = jax.ShapeDtypeStruct
MESH = pl.DeviceIdType.MESH

HEAD_DIM = 64
N_HEADS = 8
LANES = 128
ATT_W = N_HEADS * HEAD_DIM
DIL_PATTERNS = ((128, 1), (512, 4), (2048, 16))
DIL_BLK = 128
ROPE_DIM = HEAD_DIM // 4
ROPE_THETA = 500000.0
RMS_EPS = 1e-6
NEG = -1e30
QK_SCALE = 1.0 / math.sqrt(HEAD_DIM)
ADAM_LR, ADAM_B1, ADAM_B2, ADAM_EPS, ADAM_WD, ADAM_STEP = 0.001, 0.9, 0.999, 1e-08, 0.01, 10
VMEM_LIMIT = 56 * 1024 * 1024

Z_QA, Z_KA, Z_VA, Z_QB, Z_KB, Z_VB = 0, 1, 2, 3, 4, 5
Z_W = 5120


def _cp(sem):
    return pltpu.CompilerParams(dimension_semantics=sem, vmem_limit_bytes=VMEM_LIMIT)


def _nt(a, b):
    return lax.dot_general(a, b, (((1,), (1,)), ((), ())), preferred_element_type=F32)


def _tn(a, b):
    return lax.dot_general(a, b, (((0,), (0,)), ((), ())), preferred_element_type=F32)


def _nn(a, b):
    return jnp.dot(a, b, preferred_element_type=F32)


def _lane(shape):
    return lax.broadcasted_iota(jnp.int32, shape, 1)


def _row(shape):
    return lax.broadcasted_iota(jnp.int32, shape, 0)


def rmsnorm_fwd(x, g, *, tm=512):
    s, d = x.shape

    def body(x_ref, g_ref, h_ref):
        xv = x_ref[...]
        inv = lax.rsqrt(jnp.mean(xv * xv, axis=-1, keepdims=True) + RMS_EPS)
        h_ref[...] = (xv * inv * g_ref[...]).astype(h_ref.dtype)

    return pl.pallas_call(
        body, grid=(s // tm,),
        in_specs=[pl.BlockSpec((tm, d), lambda i: (i, 0)), pl.BlockSpec((1, d), lambda i: (0, 0))],
        out_specs=pl.BlockSpec((tm, d), lambda i: (i, 0)),
        out_shape=SDS((s, d), BF16), name="rmsnorm_fwd", compiler_params=_cp(("parallel",)))(x, g)


def rmsnorm_bwd(dh, x, g, res, *, out_dtype, tm=256, name):
    s, d = x.shape
    n = s // tm
    has_res = res is not None

    def body(*refs):
        if has_res:
            dh_ref, x_ref, g_ref, res_ref, dx_ref, dg_ref, acc = refs
        else:
            dh_ref, x_ref, g_ref, dx_ref, dg_ref, acc = refs
        i = pl.program_id(0)

        @pl.when(i == 0)
        def _():
            acc[...] = jnp.zeros_like(acc)

        xv = x_ref[...]
        inv = lax.rsqrt(jnp.mean(xv * xv, axis=-1, keepdims=True) + RMS_EPS)
        xh = xv * inv
        dhv = dh_ref[...].astype(F32)
        dxh = dhv * g_ref[...]
        dot = jnp.mean(dxh * xh, axis=-1, keepdims=True)
        dx = inv * (dxh - xh * dot)
        if has_res:
            dx = dx + res_ref[...]
        dx_ref[...] = dx.astype(dx_ref.dtype)
        acc[...] += jnp.sum((dhv * xh).reshape(tm // 8, 8, d), axis=0)

        @pl.when(i == n - 1)
        def _():
            dg_ref[...] = jnp.sum(acc[...], axis=0, keepdims=True)

    row = pl.BlockSpec((tm, d), lambda i: (i, 0))
    in_specs = [row, row, pl.BlockSpec((1, d), lambda i: (0, 0))] + ([row] if has_res else [])
    args = [dh, x, g] + ([res] if has_res else [])
    return pl.pallas_call(
        body, grid=(n,), in_specs=in_specs,
        out_specs=[row, pl.BlockSpec((1, d), lambda i: (0, 0))],
        out_shape=[SDS((s, d), out_dtype), SDS((1, d), F32)],
        scratch_shapes=[pltpu.VMEM((8, d), F32)],
        name=name, compiler_params=_cp(("arbitrary",)))(*args)


def mm(a_views, b_views, *, nt, out_dtype, tm, tn, name):
    n_p = len(a_views)
    m = a_views[0][0].shape[0]
    n = b_views[0][0].shape[0] if nt else b_views[0][0].shape[1]

    def body(*refs):
        o_ref = refs[-1]
        acc = None
        for p in range(n_p):
            av = refs[p][...].astype(BF16)
            bv = refs[n_p + p][...].astype(BF16)
            dv = _nt(av, bv) if nt else _nn(av, bv)
            acc = dv if acc is None else acc + dv
        o_ref[...] = acc.astype(o_ref.dtype)

    in_specs = []
    for arr, w, blk in a_views:
        in_specs.append(pl.BlockSpec((tm, w), functools.partial(lambda i, j, blk: (i, blk), blk=blk)))
    for arr, w, blk in b_views:
        if nt:
            in_specs.append(pl.BlockSpec((tn, w), functools.partial(lambda i, j, blk: (j, blk), blk=blk)))
        else:
            in_specs.append(pl.BlockSpec((w, tn), lambda i, j: (0, j)))
    return pl.pallas_call(
        body, grid=(m // tm, n // tn), in_specs=in_specs,
        out_specs=pl.BlockSpec((tm, tn), lambda i, j: (i, j)),
        out_shape=SDS((m, n), out_dtype), name=name,
        compiler_params=_cp(("parallel", "parallel")))(*[a[0] for a in a_views], *[b[0] for b in b_views])


def wgrad(a_view, g, *, tk, tn, ts, name, chip_major=False):
    arr, ka, blk = a_view
    s, n = g.shape
    ns = s // ts

    def body(a_ref, g_ref, o_ref):
        @pl.when(pl.program_id(2) == 0)
        def _():
            o_ref[...] = jnp.zeros_like(o_ref)

        o_ref[...] += _tn(a_ref[...].astype(BF16), g_ref[...].astype(BF16))

    if chip_major:
        out_spec = pl.BlockSpec((None, tk, tn), lambda i, j, k: (j, i, 0))
        out_shape = SDS((n // tn, ka, tn), F32)
    else:
        out_spec = pl.BlockSpec((tk, tn), lambda i, j, k: (i, j))
        out_shape = SDS((ka, n), F32)
    return pl.pallas_call(
        body, grid=(ka // tk, n // tn, ns),
        in_specs=[pl.BlockSpec((ts, tk), lambda i, j, k: (k, blk * (ka // tk) + i)),
                  pl.BlockSpec((ts, tn), lambda i, j, k: (k, j))],
        out_specs=out_spec, out_shape=out_shape, name=name,
        compiler_params=_cp(("parallel", "parallel", "arbitrary")))(arr, g)


def _norm_bwd_rows(dh, xh, inv, g):
    dxh = dh * g
    dx = inv * (dxh - xh * jnp.mean(dxh * xh, axis=-1, keepdims=True))
    return dx, jnp.sum((dh * xh).reshape(dh.shape[0] // 8, 8, dh.shape[1]), axis=0)


def proj_norm_res(a, w, g, xres, g_next, *, tm=512, name):
    s, k = a.shape
    d = w.shape[1]

    def body(a_ref, w_ref, g_ref, x_ref, gn_ref, y_ref, o_ref, h_ref):
        y = _nn(a_ref[...], w_ref[...])
        inv = lax.rsqrt(jnp.mean(y * y, axis=-1, keepdims=True) + RMS_EPS)
        xn = x_ref[...] + y * inv * g_ref[...]
        y_ref[...] = y
        o_ref[...] = xn
        inv_n = lax.rsqrt(jnp.mean(xn * xn, axis=-1, keepdims=True) + RMS_EPS)
        h_ref[...] = (xn * inv_n * gn_ref[...]).astype(h_ref.dtype)

    row = pl.BlockSpec((tm, d), lambda i: (i, 0))
    vec = pl.BlockSpec((1, d), lambda i: (0, 0))
    return pl.pallas_call(
        body, grid=(s // tm,),
        in_specs=[pl.BlockSpec((tm, k), lambda i: (i, 0)), pl.BlockSpec((k, d), lambda i: (0, 0)), vec, row, vec],
        out_specs=[row, row, row], out_shape=[SDS((s, d), F32), SDS((s, d), F32), SDS((s, d), BF16)], name=name,
        compiler_params=_cp(("parallel",)))(a, w, g, xres, g_next)


def proj_norm_loss(a, w, g, xres, target, *, tm=512, name):
    s, k = a.shape
    d = w.shape[1]
    n = s // tm

    def body(a_ref, w_ref, g_ref, x_ref, t_ref, do_ref, dy_ref, dg_ref, l_ref, acc):
        i = pl.program_id(0)

        @pl.when(i == 0)
        def _():
            acc[...] = jnp.zeros_like(acc)
            l_ref[...] = jnp.zeros_like(l_ref)

        y = _nn(a_ref[...], w_ref[...])
        inv = lax.rsqrt(jnp.mean(y * y, axis=-1, keepdims=True) + RMS_EPS)
        yh = y * inv
        err = x_ref[...] + yh * g_ref[...] - t_ref[...]
        dout = err * (1.0 / d)
        do_ref[...] = dout
        l_ref[...] += jnp.sum(jnp.sum(err * err, axis=1, keepdims=True), axis=0, keepdims=True)
        dy, part = _norm_bwd_rows(dout, yh, inv, g_ref[...])
        dy_ref[...] = dy.astype(dy_ref.dtype)
        acc[...] += part

        @pl.when(i == n - 1)
        def _():
            dg_ref[...] = jnp.sum(acc[...], axis=0, keepdims=True)

    row = pl.BlockSpec((tm, d), lambda i: (i, 0))
    vec = pl.BlockSpec((1, d), lambda i: (0, 0))
    return pl.pallas_call(
        body, grid=(n,),
        in_specs=[pl.BlockSpec((tm, k), lambda i: (i, 0)), pl.BlockSpec((k, d), lambda i: (0, 0)), vec, row, row],
        out_specs=[row, row, vec, pl.BlockSpec((1, 1), lambda i: (0, 0))],
        out_shape=[SDS((s, d), F32), SDS((s, d), BF16), SDS((1, d), F32), SDS((1, 1), F32)],
        scratch_shapes=[pltpu.VMEM((8, d), F32)], name=name, compiler_params=_cp(("arbitrary",)))(a, w, g, xres, target)


def mm_norm_bwd(a_views, b_views, stages, exchange=(), *, tm=256, name):
    n_p, n_s, ne = len(a_views), len(stages), len(exchange)
    s = a_views[0][0].shape[0]
    d = b_views[0][0].shape[0]
    n = s // tm
    has_res = [st[2] is not None for st in stages]

    def body(*refs):
        a_refs, b_refs = refs[:n_p], refs[n_p:2 * n_p]
        at = 2 * n_p
        st_refs = []
        for k in range(n_s):
            cnt = 3 if has_res[k] else 2
            st_refs.append(refs[at:at + cnt])
            at += cnt
        e_ins = refs[at:at + ne]
        at += ne
        dx_refs, dg_refs = refs[at:at + n_s], refs[at + n_s:at + 2 * n_s]
        at += 2 * n_s
        e_outs = refs[at:at + ne]
        at += ne
        accs = refs[at:at + n_s]
        comm = (e_ins, e_outs) + tuple(refs[at + n_s:])
        i = pl.program_id(0)

        @pl.when(i == 0)
        def _():
            for acc in accs:
                acc[...] = jnp.zeros_like(acc)
            if ne:
                _to_chips_start(*comm)

        dh = None
        for p in range(n_p):
            part = _nt(a_refs[p][...].astype(BF16), b_refs[p][...].astype(BF16))
            dh = part if dh is None else dh + part
        for k in range(n_s):
            xv = st_refs[k][0][...]
            inv = lax.rsqrt(jnp.mean(xv * xv, axis=-1, keepdims=True) + RMS_EPS)
            dx, part = _norm_bwd_rows(dh, xv * inv, inv, st_refs[k][1][...])
            if has_res[k]:
                dx = dx + st_refs[k][2][...]
            dx_refs[k][...] = dx.astype(dx_refs[k].dtype)
            accs[k][...] += part
            dh = dx

        @pl.when(i == n - 1)
        def _():
            for k in range(n_s):
                dg_refs[k][...] = jnp.sum(accs[k][...], axis=0, keepdims=True)
            if ne:
                _to_chips_finish(*comm)

    row = pl.BlockSpec((tm, d), lambda i: (i, 0))
    vec = pl.BlockSpec((1, d), lambda i: (0, 0))
    in_specs, args = [], []
    for arr, w, blk in a_views:
        in_specs.append(pl.BlockSpec((tm, w), functools.partial(lambda i, blk: (i, blk), blk=blk)))
        args.append(arr)
    for arr, w, blk in b_views:
        in_specs.append(pl.BlockSpec((d, w), functools.partial(lambda i, blk: (0, blk), blk=blk)))
        args.append(arr)
    for x, g, res, _ in stages:
        in_specs += [row, vec] + ([row] if res is not None else [])
        args += [x, g] + ([res] if res is not None else [])
    return pl.pallas_call(
        body, grid=(n,), in_specs=in_specs + [ANY] * ne,
        out_specs=[row] * n_s + [vec] * n_s + [ANY] * ne,
        out_shape=[SDS((s, d), st[3]) for st in stages] + [SDS((1, d), F32)] * n_s + _to_chips_shapes(exchange),
        scratch_shapes=[pltpu.VMEM((8, d), F32)] * n_s + (_to_chips_sems(ne) if ne else []), name=name,
        compiler_params=_cp(("arbitrary",)))(*args, *exchange)


def _split3(v):
    hi = v.astype(BF16).astype(F32)
    r = v - hi
    mid = r.astype(BF16).astype(F32)
    lo = (r - mid).astype(BF16).astype(F32)
    return hi, mid, lo


def _tri(n, upper):
    r = np.arange(n)
    m = (r[:, None] <= r[None, :]) if upper else (r[:, None] >= r[None, :])
    return jnp.asarray(m.astype(np.float32))


def fox_prep(z, fa, bfo, *, tb=512):
    s = z.shape[0]
    n = s // tb

    def body(q_ref, k_ref, v_ref, fa_ref, b_ref, tri_ref, qa_ref, ka_ref, va_ref, carry):
        @pl.when(pl.program_id(0) == 0)
        def _():
            carry[...] = jnp.zeros_like(carry)

        xv = fa_ref[...] + b_ref[...]
        logf = jnp.minimum(xv, 0.0) - jnp.log(1.0 + jnp.exp(-jnp.abs(xv)))
        csum = jnp.dot(tri_ref[...], logf, preferred_element_type=F32, precision=lax.Precision.HIGHEST) + carry[0:1, :]
        carry[0:1, :] = csum[tb - 1:tb, :]
        lane = _lane((tb, LANES))
        for h in range(N_HEADS):
            hi, mid, lo = _split3(csum[:, h:h + 1])
            pair = (h // 2) * LANES
            qv = q_ref[:, pair:pair + LANES].astype(F32)
            kv = k_ref[:, pair:pair + LANES].astype(F32)
            vv = v_ref[:, pair:pair + LANES].astype(F32)
            if h % 2:
                qv = pltpu.roll(qv, 64, axis=1)
                kv = pltpu.roll(kv, 64, axis=1)
                vv = pltpu.roll(vv, 64, axis=1)
            va_ref[:, h * LANES:(h + 1) * LANES] = jnp.where(lane < 64, vv, jnp.where(lane == 64, 1.0, 0.0)).astype(BF16)
            one = jnp.where((lane >= 67) & (lane < 70), 1.0, 0.0)
            q_x = jnp.where(lane == 64, hi, jnp.where(lane == 65, mid, jnp.where(lane == 66, lo, one)))
            one = jnp.where((lane >= 64) & (lane < 67), 1.0, 0.0)
            k_x = jnp.where(lane == 67, -hi, jnp.where(lane == 68, -mid, jnp.where(lane == 69, -lo, one)))
            qa_ref[:, h * LANES:(h + 1) * LANES] = jnp.where(lane < 64, qv * QK_SCALE, q_x).astype(BF16)
            ka_ref[:, h * LANES:(h + 1) * LANES] = jnp.where(lane < 64, kv, k_x).astype(BF16)

    return pl.pallas_call(
        body, grid=(n,),
        in_specs=[pl.BlockSpec((tb, ATT_W), lambda i: (i, Z_QA)), pl.BlockSpec((tb, ATT_W), lambda i: (i, Z_KA)),
                  pl.BlockSpec((tb, ATT_W), lambda i: (i, Z_VA)),
                  pl.BlockSpec((tb, LANES), lambda i: (i, 0)), pl.BlockSpec((1, LANES), lambda i: (0, 0)),
                  pl.BlockSpec((tb, tb), lambda i: (0, 0))],
        out_specs=[pl.BlockSpec((tb, N_HEADS * LANES), lambda i: (i, 0))] * 3,
        out_shape=[SDS((s, N_HEADS * LANES), BF16)] * 3,
        scratch_shapes=[pltpu.VMEM((8, LANES), F32)],
        name="fox_prep", compiler_params=_cp(("arbitrary",)))(z, z, z, fa, bfo, _tri(tb, False))


def _causal_pairs(n, k_major):
    if k_major:
        pairs = [(qi, kj) for kj in range(n) for qi in range(kj, n)]
    else:
        pairs = [(qi, kj) for qi in range(n) for kj in range(qi + 1)]
    return (jnp.asarray([p[0] for p in pairs], jnp.int32), jnp.asarray([p[1] for p in pairs], jnp.int32), len(pairs))


def fox_fwd(q_aug, k_aug, v_aug, gather=(), *, t=512, hps=4):
    s = v_aug.shape[0]
    qi_arr, kj_arr, n_pairs = _causal_pairs(s // t, False)
    ng = len(gather)
    n_groups = N_HEADS // hps

    def body(qi_ref, kj_ref, q_ref, k_ref, v_ref, *rest):
        g_ins, (o_ref, lse_ref), g_outs = rest[:ng], rest[ng:ng + 2], rest[ng + 2:2 * ng + 2]
        m_scr, acc_scr = rest[2 * ng + 2:2 * ng + 4]
        comm = (g_ins, g_outs) + tuple(rest[2 * ng + 4:]) + ([True] * ng,)
        step = pl.program_id(1)
        qi = qi_ref[step]
        kj = kj_ref[step]
        if ng:
            @pl.when((pl.program_id(0) == 0) & (step == 0))
            def _():
                _allgather_start(*comm)

        @pl.when(kj == 0)
        def _():
            m_scr[...] = jnp.full_like(m_scr, NEG)
            acc_scr[...] = jnp.zeros_like(acc_scr)

        def update(masked):
            for i in range(hps):
                sc = _nt(q_ref[:, i * LANES:(i + 1) * LANES], k_ref[:, i * LANES:(i + 1) * LANES])
                if masked:
                    sc = jnp.where(_row((t, t)) >= _lane((t, t)), sc, NEG)
                m_prev = m_scr[i]
                m_new = jnp.maximum(m_prev, jnp.max(sc, axis=-1, keepdims=True))
                p = jnp.exp((sc - jnp.tile(m_new, (1, t // LANES))).astype(BF16))
                acc_scr[i] = jnp.exp(m_prev - m_new) * acc_scr[i] + _nn(p, v_ref[:, i * LANES:(i + 1) * LANES])
                m_scr[i] = m_new

        @pl.when(kj < qi)
        def _():
            update(False)

        @pl.when(kj == qi)
        def _():
            update(True)
            lane = _lane((t, LANES))
            for pr in range(hps // 2):
                den = [acc_scr[2 * pr + i][:, 64:65] for i in range(2)]
                o_ref[:, pr * LANES:(pr + 1) * LANES] = jnp.where(
                    lane < 64, acc_scr[2 * pr] / den[0], pltpu.roll(acc_scr[2 * pr + 1] / den[1], 64, axis=1)).astype(o_ref.dtype)
                lse_ref[:, pr * LANES:(pr + 1) * LANES] = jnp.where(
                    lane < 64, m_scr[2 * pr] + jnp.log(den[0]), m_scr[2 * pr + 1] + jnp.log(den[1]))

        if ng:
            @pl.when((pl.program_id(0) == n_groups - 1) & (step == n_pairs - 1))
            def _():
                _allgather_finish(*comm)

    wide = hps * LANES
    grid_spec = pltpu.PrefetchScalarGridSpec(
        num_scalar_prefetch=2, grid=(n_groups, n_pairs),
        in_specs=[pl.BlockSpec((t, wide), lambda hg, st, qi, kj: (qi[st], hg)),
                  pl.BlockSpec((t, wide), lambda hg, st, qi, kj: (kj[st], hg)),
                  pl.BlockSpec((t, wide), lambda hg, st, qi, kj: (kj[st], hg))] + [ANY] * ng,
        out_specs=[pl.BlockSpec((t, wide // 2), lambda hg, st, qi, kj: (qi[st], hg))] * 2 + [ANY] * ng,
        scratch_shapes=[pltpu.VMEM((hps, t, LANES), F32)] * 2 + (_allgather_sems(ng) if ng else []))
    return pl.pallas_call(
        body, grid_spec=grid_spec, out_shape=[SDS((s, ATT_W), BF16), SDS((s, ATT_W), F32)] + _allgather_shapes(gather),
        name="fox_fwd", compiler_params=_cp(("arbitrary", "arbitrary")))(qi_arr, kj_arr, q_aug, k_aug, v_aug, *gather)


def fox_bwd(q_aug, k_aug, z, dy, lse, dd, exchange=(), *, t=512, hps=4):
    s = z.shape[0]
    qi_arr, kj_arr, n_pairs = _causal_pairs(s // t, True)
    ne = len(exchange)
    n_groups = N_HEADS // hps

    def body(qi_ref, kj_ref, q_ref, k_ref, v_ref, do_ref, lse_ref, dd_ref, *rest):
        e_ins, (dq_ref, dk_ref, dv_ref), e_outs = rest[:ne], rest[ne:ne + 3], rest[ne + 3:2 * ne + 3]
        comm = (e_ins, e_outs) + tuple(rest[2 * ne + 3:])
        step = pl.program_id(1)
        qi = qi_ref[step]
        kj = kj_ref[step]
        if ne:
            @pl.when((pl.program_id(0) == 0) & (step == 0))
            def _():
                _to_chips_start(*comm)

        @pl.when(step == 0)
        def _():
            dq_ref[...] = jnp.zeros_like(dq_ref)

        @pl.when(qi == kj)
        def _():
            dk_ref[...] = jnp.zeros_like(dk_ref)
            dv_ref[...] = jnp.zeros_like(dv_ref)

        def update(masked):
            lane = _lane((t, LANES))
            rows = pl.ds(pl.multiple_of(qi * t, t), t)
            for pr in range(hps // 2):
                pair = slice(pr * LANES, (pr + 1) * LANES)
                dov = do_ref[:, pair]
                dv_new = None
                for i in range(2):
                    head = (lane < 64) if i == 0 else (lane >= 64)
                    own = slice((2 * pr + i) * LANES, (2 * pr + i + 1) * LANES)
                    col = slice(pr * LANES + i * 64, pr * LANES + i * 64 + 1)
                    qv = q_ref[:, own]
                    kv = k_ref[:, own]
                    sc = _nt(qv, kv)
                    if masked:
                        sc = jnp.where(_row((t, t)) >= _lane((t, t)), sc, NEG)
                    p = jnp.exp(sc - lse_ref[:, col])
                    dp = _nt(jnp.where(head, dov, jnp.zeros_like(dov)), v_ref[:, pair])
                    ds = (p * (dp - dd_ref[:, col])).astype(BF16)
                    dq_ref[rows, own] += _nn(ds, kv)
                    dk_ref[:, own] += _tn(ds, qv)
                    dvi = _tn(p.astype(BF16), dov)
                    dv_new = dvi if dv_new is None else jnp.where(head, dvi, dv_new)
                dv_ref[:, pair] += dv_new

        @pl.when(kj < qi)
        def _():
            update(False)

        @pl.when(kj == qi)
        def _():
            update(True)

        if ne:
            @pl.when((pl.program_id(0) == n_groups - 1) & (step == n_pairs - 1))
            def _():
                _to_chips_finish(*comm)

    wide, half = hps * LANES, hps // 2 * LANES
    v_blk = Z_VA * ATT_W // half
    grid_spec = pltpu.PrefetchScalarGridSpec(
        num_scalar_prefetch=2, grid=(n_groups, n_pairs),
        in_specs=[pl.BlockSpec((t, wide), lambda hg, st, qi, kj: (qi[st], hg)),
                  pl.BlockSpec((t, wide), lambda hg, st, qi, kj: (kj[st], hg)),
                  pl.BlockSpec((t, half), lambda hg, st, qi, kj: (kj[st], v_blk + hg)),
                  pl.BlockSpec((t, half), lambda hg, st, qi, kj: (qi[st], hg)),
                  pl.BlockSpec((t, half), lambda hg, st, qi, kj: (qi[st], hg)),
                  pl.BlockSpec((t, half), lambda hg, st, qi, kj: (qi[st], hg))] + [ANY] * ne,
        out_specs=[pl.BlockSpec((s, wide), lambda hg, st, qi, kj: (0, hg)),
                   pl.BlockSpec((t, wide), lambda hg, st, qi, kj: (kj[st], hg)),
                   pl.BlockSpec((t, half), lambda hg, st, qi, kj: (kj[st], hg))] + [ANY] * ne,
        scratch_shapes=_to_chips_sems(ne) if ne else [])
    return pl.pallas_call(
        body, grid_spec=grid_spec,
        out_shape=[SDS((s, N_HEADS * LANES), F32), SDS((s, N_HEADS * LANES), F32), SDS((s, ATT_W), F32)]
        + _to_chips_shapes(exchange),
        name="fox_bwd", compiler_params=_cp(("arbitrary", "arbitrary")))(qi_arr, kj_arr, q_aug, k_aug, z, dy, lse, dd, *exchange)


def head_rowsum(a, b, *, tm=512, name):
    s = a.shape[0]

    def body(a_ref, b_ref, o_ref):
        prod = a_ref[...].astype(F32) * b_ref[...].astype(F32)
        lane = _lane((tm, LANES))
        lo = jnp.sum(jnp.where(lane < 64, prod, 0.0), axis=-1, keepdims=True)
        hi = jnp.sum(jnp.where(lane >= 64, prod, 0.0), axis=-1, keepdims=True)
        o_ref[...] = jnp.where(lane < 64, lo, hi)

    blk = pl.BlockSpec((tm, LANES), lambda i, j: (i, j))
    return pl.pallas_call(body, grid=(s // tm, 4), in_specs=[blk, blk], out_specs=blk, out_shape=SDS((s, ATT_W), F32),
                          name=name, compiler_params=_cp(("parallel", "parallel")))(a, b)


def fox_post(dq_aug, dk_aug, dv, fa, bfo, *, tb=512):
    s = dv.shape[0]
    n = s // tb

    def body(dq_ref, dk_ref, dv_ref, fa_ref, b_ref, tri_ref, dz_ref, dfa_ref, gb_ref, carry, acc):
        i = pl.program_id(0)

        @pl.when(i == 0)
        def _():
            carry[...] = jnp.zeros_like(carry)
            acc[...] = jnp.zeros_like(acc)

        lane = _lane((tb, LANES))
        d_f = jnp.zeros((tb, LANES), F32)
        for h in range(N_HEADS):
            col = dq_ref[:, h * LANES + 64:h * LANES + 65] - dk_ref[:, h * LANES + 67:h * LANES + 68]
            d_f = jnp.where(lane == h, col, d_f)
        suffix = jnp.dot(tri_ref[...], d_f, preferred_element_type=F32, precision=lax.Precision.HIGHEST) + carry[0:1, :]
        carry[0:1, :] = suffix[0:1, :]
        xv = fa_ref[...] + b_ref[...]
        dx = suffix * (1.0 / (1.0 + jnp.exp(xv)))
        dfa_ref[...] = dx.astype(dfa_ref.dtype)
        acc[...] += jnp.sum(dx.reshape(tb // 8, 8, LANES), axis=0)
        for hp in range(4):
            for src, off, scale in ((dq_ref, 0, QK_SCALE), (dk_ref, ATT_W, 1.0)):
                even = src[:, (2 * hp) * LANES:(2 * hp + 1) * LANES]
                odd = pltpu.roll(src[:, (2 * hp + 1) * LANES:(2 * hp + 2) * LANES], 64, axis=1)
                dz_ref[:, off + hp * LANES:off + (hp + 1) * LANES] = (jnp.where(lane < 64, even, odd) * scale).astype(BF16)
        dz_ref[:, 2 * ATT_W:3 * ATT_W] = dv_ref[...].astype(BF16)

        @pl.when(i == n - 1)
        def _():
            gb_ref[...] = jnp.sum(acc[...], axis=0, keepdims=True)

    rev = lambda i: (n - 1 - i, 0)
    return pl.pallas_call(
        body, grid=(n,),
        in_specs=[pl.BlockSpec((tb, N_HEADS * LANES), rev), pl.BlockSpec((tb, N_HEADS * LANES), rev),
                  pl.BlockSpec((tb, ATT_W), rev), pl.BlockSpec((tb, LANES), rev),
                  pl.BlockSpec((1, LANES), lambda i: (0, 0)), pl.BlockSpec((tb, tb), lambda i: (0, 0))],
        out_specs=[pl.BlockSpec((tb, 3 * ATT_W), rev), pl.BlockSpec((tb, LANES), rev),
                   pl.BlockSpec((1, LANES), lambda i: (0, 0))],
        out_shape=[SDS((s, 3 * ATT_W), BF16), SDS((s, LANES), BF16), SDS((1, LANES), F32)],
        scratch_shapes=[pltpu.VMEM((8, LANES), F32), pltpu.VMEM((8, LANES), F32)],
        name="fox_post", compiler_params=_cp(("arbitrary",)))(dq_aug, dk_aug, dv, fa, bfo, _tri(tb, True))


def rope_tables(s, sign):
    half = ROPE_DIM // 2
    inv_freq = ROPE_THETA ** (-jnp.arange(half, dtype=F32) * 2.0 / ROPE_DIM)
    ang = jnp.arange(s, dtype=F32)[:, None] * inv_freq[None, :]
    l64 = np.arange(LANES) % HEAD_DIM
    cos = jnp.tile(jnp.cos(ang), (1, LANES // half))
    sin = jnp.tile(jnp.sin(ang), (1, LANES // half)) * sign
    first = jnp.asarray(l64 < half)[None, :]
    second = jnp.asarray((l64 >= half) & (l64 < ROPE_DIM))[None, :]
    return (jnp.where(first | second, cos, 1.0), jnp.where(first, -sin, 0.0), jnp.where(second, sin, 0.0))


def rope_apply(items, tabs, *, out_dtype, tm=512, name):
    s = items[0][0].shape[0]
    n_i = len(items)

    def body(*refs):
        c_ref, sn_ref, sp_ref = refs[n_i:n_i + 3]
        o_ref = refs[-1]
        for j, (_, _, scale, rotate) in enumerate(items):
            for b in range(4):
                xv = refs[j][:, b * LANES:(b + 1) * LANES].astype(F32)
                if rotate:
                    xv = xv * c_ref[...] + pltpu.roll(xv, LANES - 8, axis=1) * sn_ref[...] + pltpu.roll(xv, 8, axis=1) * sp_ref[...]
                o_ref[:, j * ATT_W + b * LANES:j * ATT_W + (b + 1) * LANES] = (xv * scale).astype(o_ref.dtype)

    in_specs = [pl.BlockSpec((tm, ATT_W), functools.partial(lambda i, blk: (i, blk), blk=it[1])) for it in items]
    in_specs += [pl.BlockSpec((tm, LANES), lambda i: (i, 0))] * 3
    return pl.pallas_call(
        body, grid=(s // tm,), in_specs=in_specs, out_specs=pl.BlockSpec((tm, n_i * ATT_W), lambda i: (i, 0)),
        out_shape=SDS((s, n_i * ATT_W), out_dtype), name=name, compiler_params=_cp(("parallel",)))(*[it[0] for it in items], *tabs)


def _dil_views(qk, z, r):
    s = z.shape[0]
    return qk.reshape(s // r, r * 2 * ATT_W), z.reshape(s // r, r * Z_W)


def _dil_cols(r):
    q_col = lambda rho, hp: rho * 8 + hp
    k_col = lambda rho, hp: rho * 8 + 4 + hp
    v_col = lambda rho, hp: rho * (Z_W // LANES) + 4 * Z_VB + hp
    return q_col, k_col, v_col


def _dil_scores(qv, kp, kc, head, has_prev):
    b = DIL_BLK
    qm = jnp.where(head, qv, jnp.zeros_like(qv))
    row, col = _row((b, b)), _lane((b, b))
    sp = jnp.where((col >= row) & has_prev, _nt(qm, kp), NEG)
    sc = jnp.where(col <= row, _nt(qm, kc), NEG)
    return sp, sc


def dil_fwd(qk, z, prev, *, r):
    s = z.shape[0]
    b = DIL_BLK
    l_sub = s // r
    nb = l_sub // b
    qk_v, z_v = _dil_views(qk, z, r)
    q_col, k_col, v_col = _dil_cols(r)
    merge = prev is not None

    def body(*refs):
        if merge:
            q_ref, kp_ref, kc_ref, vp_ref, vc_ref, op_ref, lp_ref, o_ref, l_ref = refs
        else:
            q_ref, kp_ref, kc_ref, vp_ref, vc_ref, o_ref, l_ref = refs
        has_prev = pl.program_id(2) > 0
        lane = _lane((b, LANES))
        res = []
        for i in range(2):
            head = (lane < 64) if i == 0 else (lane >= 64)
            sp, sc = _dil_scores(q_ref[...], kp_ref[...], kc_ref[...], head, has_prev)
            m = jnp.maximum(jnp.max(sp, axis=-1, keepdims=True), jnp.max(sc, axis=-1, keepdims=True))
            pp = jnp.exp(sp - m)
            pc = jnp.exp(sc - m)
            den = jnp.sum(pp, axis=-1, keepdims=True) + jnp.sum(pc, axis=-1, keepdims=True)
            ov = (_nn(pp.astype(BF16), vp_ref[...]) + _nn(pc.astype(BF16), vc_ref[...])) / den
            res.append((ov, m + jnp.log(den)))
        ov = jnp.where(lane < 64, res[0][0], res[1][0])
        lse = jnp.where(lane < 64, res[0][1], res[1][1])
        if merge:
            lp = lp_ref[...]
            m2 = jnp.maximum(lp, lse)
            wp = jnp.exp(lp - m2)
            wn = jnp.exp(lse - m2)
            ov = (wp * op_ref[...] + wn * ov) / (wp + wn)
            lse = m2 + jnp.log(wp + wn)
        o_ref[...] = ov
        l_ref[...] = lse

    blk = lambda f: pl.BlockSpec((b, LANES), f)
    in_specs = [blk(lambda rho, hp, n: (n, q_col(rho, hp))), blk(lambda rho, hp, n: (jnp.maximum(n - 1, 0), k_col(rho, hp))),
                blk(lambda rho, hp, n: (n, k_col(rho, hp))), blk(lambda rho, hp, n: (jnp.maximum(n - 1, 0), v_col(rho, hp))),
                blk(lambda rho, hp, n: (n, v_col(rho, hp)))]
    args = [qk_v, qk_v, qk_v, z_v, z_v]
    nat = blk(lambda rho, hp, n: (n, rho * 4 + hp))
    if merge:
        in_specs += [nat, nat]
        args += [prev[0].reshape(l_sub, r * ATT_W), prev[1].reshape(l_sub, r * ATT_W)]
    o, lse = pl.pallas_call(
        body, grid=(r, 4, nb), in_specs=in_specs, out_specs=[nat, nat],
        out_shape=[SDS((l_sub, r * ATT_W), F32)] * 2, name=f"dil_fwd_r{r}",
        compiler_params=_cp(("parallel", "parallel", "arbitrary")))(*args)
    return o.reshape(s, ATT_W), lse.reshape(s, ATT_W)


def dil_bwd_dq(qk, z, dy, lse, dd, acc, *, r):
    s = z.shape[0]
    b = DIL_BLK
    l_sub = s // r
    nb = l_sub // b
    qk_v, z_v = _dil_views(qk, z, r)
    q_col, k_col, v_col = _dil_cols(r)
    add = acc is not None

    def body(*refs):
        q_ref, kp_ref, kc_ref, vp_ref, vc_ref, do_ref, l_ref, dd_ref = refs[:8]
        dq_ref = refs[-1]
        has_prev = pl.program_id(2) > 0
        lane = _lane((b, LANES))
        dov = do_ref[...]
        parts = []
        for i in range(2):
            head = (lane < 64) if i == 0 else (lane >= 64)
            sp, sc = _dil_scores(q_ref[...], kp_ref[...], kc_ref[...], head, has_prev)
            lse_i = l_ref[:, i * 64:i * 64 + 1]
            dd_i = dd_ref[:, i * 64:i * 64 + 1]
            dom = jnp.where(head, dov, jnp.zeros_like(dov))
            dsp = (jnp.exp(sp - lse_i) * (_nt(dom, vp_ref[...]) - dd_i)).astype(BF16)
            dsc = (jnp.exp(sc - lse_i) * (_nt(dom, vc_ref[...]) - dd_i)).astype(BF16)
            parts.append(_nn(dsp, kp_ref[...]) + _nn(dsc, kc_ref[...]))
        dq = jnp.where(lane < 64, parts[0], parts[1])
        if add:
            dq = dq + refs[8][...]
        dq_ref[...] = dq

    blk = lambda f: pl.BlockSpec((b, LANES), f)
    nat = blk(lambda rho, hp, n: (n, rho * 4 + hp))
    in_specs = [blk(lambda rho, hp, n: (n, q_col(rho, hp))), blk(lambda rho, hp, n: (jnp.maximum(n - 1, 0), k_col(rho, hp))),
                blk(lambda rho, hp, n: (n, k_col(rho, hp))), blk(lambda rho, hp, n: (jnp.maximum(n - 1, 0), v_col(rho, hp))),
                blk(lambda rho, hp, n: (n, v_col(rho, hp))), nat, nat, nat]
    nview = lambda a: a.reshape(l_sub, r * ATT_W)
    args = [qk_v, qk_v, qk_v, z_v, z_v, nview(dy), nview(lse), nview(dd)]
    if add:
        in_specs.append(nat)
        args.append(nview(acc))
    dq = pl.pallas_call(
        body, grid=(r, 4, nb), in_specs=in_specs, out_specs=nat, out_shape=SDS((l_sub, r * ATT_W), F32),
        name=f"dil_bwd_dq_r{r}", compiler_params=_cp(("parallel", "parallel", "arbitrary")))(*args)
    return dq.reshape(s, ATT_W)


def dil_bwd_dkv(qk, z, dy, lse, dd, acc, *, r):
    s = z.shape[0]
    b = DIL_BLK
    l_sub = s // r
    nb = l_sub // b
    qk_v, z_v = _dil_views(qk, z, r)
    q_col, k_col, v_col = _dil_cols(r)
    add = acc is not None

    def body(*refs):
        k_ref, v_ref, qc_ref, qn_ref, doc_ref, don_ref, lc_ref, ln_ref, ddc_ref, ddn_ref = refs[:10]
        dk_ref, dv_ref = refs[-2:]
        has_next = pl.program_id(2) < nb - 1
        lane = _lane((b, LANES))
        row, col = _row((b, b)), _lane((b, b))
        kv = k_ref[...]
        vv = v_ref[...]
        dk_parts, dv_parts = [], []
        for i in range(2):
            head = (lane < 64) if i == 0 else (lane >= 64)
            dk_i = jnp.zeros((b, LANES), F32)
            dv_i = jnp.zeros((b, LANES), F32)
            for q_ref, do_ref, l_ref, d_ref, mask in ((qc_ref, doc_ref, lc_ref, ddc_ref, col <= row),
                                                      (qn_ref, don_ref, ln_ref, ddn_ref, (col >= row) & has_next)):
                qv = q_ref[...]
                dov = do_ref[...]
                sc = jnp.where(mask, _nt(jnp.where(head, qv, jnp.zeros_like(qv)), kv), NEG)
                p = jnp.exp(sc - l_ref[:, i * 64:i * 64 + 1])
                dp = _nt(jnp.where(head, dov, jnp.zeros_like(dov)), vv)
                ds = (p * (dp - d_ref[:, i * 64:i * 64 + 1])).astype(BF16)
                dv_i = dv_i + _tn(p.astype(BF16), dov)
                dk_i = dk_i + _tn(ds, qv)
            dk_parts.append(dk_i)
            dv_parts.append(dv_i)
        dk = jnp.where(lane < 64, dk_parts[0], dk_parts[1])
        dv = jnp.where(lane < 64, dv_parts[0], dv_parts[1])
        if add:
            dk = dk + refs[10][...]
            dv = dv + refs[11][...]
        dk_ref[...] = dk
        dv_ref[...] = dv

    blk = lambda f: pl.BlockSpec((b, LANES), f)
    nat = blk(lambda rho, hp, n: (n, rho * 4 + hp))
    nxt = blk(lambda rho, hp, n: (jnp.minimum(n + 1, nb - 1), rho * 4 + hp))
    in_specs = [blk(lambda rho, hp, n: (n, k_col(rho, hp))), blk(lambda rho, hp, n: (n, v_col(rho, hp))),
                blk(lambda rho, hp, n: (n, q_col(rho, hp))), blk(lambda rho, hp, n: (jnp.minimum(n + 1, nb - 1), q_col(rho, hp))),
                nat, nxt, nat, nxt, nat, nxt]
    nview = lambda a: a.reshape(l_sub, r * ATT_W)
    args = [qk_v, z_v, qk_v, qk_v, nview(dy), nview(dy), nview(lse), nview(lse), nview(dd), nview(dd)]
    if add:
        in_specs += [nat, nat]
        args += [nview(acc[0]), nview(acc[1])]
    dk, dv = pl.pallas_call(
        body, grid=(r, 4, nb), in_specs=in_specs, out_specs=[nat, nat],
        out_shape=[SDS((l_sub, r * ATT_W), F32)] * 2, name=f"dil_bwd_dkv_r{r}",
        compiler_params=_cp(("parallel", "parallel", "arbitrary")))(*args)
    return dk.reshape(s, ATT_W), dv.reshape(s, ATT_W)


def _dil_rows(base, r):
    if r == 1:
        return pl.ds(pl.multiple_of(base, DIL_BLK), DIL_BLK)
    return pl.ds(base, DIL_BLK, stride=r)


def _dil_block(idx, r, nb):
    shift = nb.bit_length() - 1
    rho = idx >> shift
    n = idx & (nb - 1)
    base = rho + n * (r * DIL_BLK)
    return _dil_rows(base, r), _dil_rows(jnp.maximum(base - r * DIL_BLK, rho), r), n > 0


def _cat(a, b):
    return jnp.concatenate([a, b], axis=0)


def _two_heads(v, first_head):
    zero = jnp.zeros_like(v)
    return _cat(jnp.where(first_head, v, zero), jnp.where(first_head, zero, v))


def _dil_bands():
    b = DIL_BLK
    q = _row((2 * b, 2 * b)) & (b - 1)
    col = _lane((2 * b, 2 * b))
    return (col < b) & (col >= q), (col >= b) & (col - b <= q)


def dil_fwd_all(qkv, *, unroll=8):
    s = qkv.shape[0]
    b = DIL_BLK
    n_blk = s // b

    def body(q_ref, k_ref, v_ref, o_ref, l_ref):
        first_head = _lane((b, LANES)) < 64
        band_prev, band_cur = _dil_bands()
        for g, (_, r) in enumerate(DIL_PATTERNS):
            nb = n_blk // r

            def group(it, carry, g=g, r=r, nb=nb):
                loaded = []
                for u in range(unroll):
                    rows_c, rows_p, has_prev = _dil_block(it * unroll + u, r, nb)
                    vals = [q_ref[rows_c, :].astype(BF16), k_ref[rows_p, :].astype(BF16), k_ref[rows_c, :].astype(BF16),
                            v_ref[rows_p, :].astype(BF16), v_ref[rows_c, :].astype(BF16)]
                    state = (o_ref[rows_c, :], l_ref[rows_c, :]) if g else None
                    loaded.append((rows_c, has_prev, vals, state))
                done = []
                for rows_c, has_prev, (qv, kp, kc, vp, vc), state in loaded:
                    sc = jnp.where(band_cur | (band_prev & has_prev), _nt(_two_heads(qv, first_head), _cat(kp, kc)), NEG)
                    m = jnp.max(sc, axis=-1, keepdims=True)
                    p = jnp.exp(sc - m)
                    den = jnp.sum(p, axis=-1, keepdims=True)
                    both = _nn(p.astype(BF16), _cat(vp, vc)) / den
                    lse2 = m + jnp.log(den)
                    ov = jnp.where(first_head, both[:b], both[b:])
                    lse = jnp.where(first_head, lse2[:b], lse2[b:])
                    if state is not None:
                        m2 = jnp.maximum(state[1], lse)
                        wp = jnp.exp(state[1] - m2)
                        wn = jnp.exp(lse - m2)
                        ov = (wp * state[0] + wn * ov) / (wp + wn)
                        lse = m2 + jnp.log(wp + wn)
                    done.append((rows_c, ov, lse))
                for rows_c, ov, lse in done:
                    o_ref[rows_c, :] = ov
                    l_ref[rows_c, :] = lse
                return carry

            lax.fori_loop(0, n_blk // unroll, group, 0)

    col_blk = lambda k: pl.BlockSpec((s, LANES), lambda hp: (0, 4 * k + hp))
    out = pl.BlockSpec((s, LANES), lambda hp: (0, hp))
    return pl.pallas_call(
        body, grid=(4,), in_specs=[col_blk(0), col_blk(1), col_blk(2)], out_specs=[out, out],
        out_shape=[SDS((s, ATT_W), F32)] * 2, name="dil_fwd", compiler_params=_cp(("parallel",)))(qkv, qkv, qkv)


def dil_bwd_all(qkv, dy, lse, y, exchange=(), *, unroll=8):
    s = qkv.shape[0]
    b = DIL_BLK
    n_blk = s // b
    ne = len(exchange)

    def body(q_ref, k_ref, v_ref, do_ref, l_ref, y_ref, *rest):
        e_ins, (dq_ref, dk_ref, dv_ref), e_outs = rest[:ne], rest[ne:ne + 3], rest[ne + 3:2 * ne + 3]
        comm = (e_ins, e_outs) + tuple(rest[2 * ne + 3:])
        if ne:
            @pl.when(pl.program_id(0) == 0)
            def _():
                _to_chips_start(*comm)

        dq_ref[...] = jnp.zeros_like(dq_ref)
        dk_ref[...] = jnp.zeros_like(dk_ref)
        dv_ref[...] = jnp.zeros_like(dv_ref)
        first_head = _lane((b, LANES)) < 64
        band_prev, band_cur = _dil_bands()
        for _, r in DIL_PATTERNS:
            nb = n_blk // r

            def group(it, carry, r=r, nb=nb):
                loaded = []
                for u in range(unroll):
                    rows_c, rows_p, has_prev = _dil_block(it * unroll + u, r, nb)
                    vals = [q_ref[rows_c, :].astype(BF16), k_ref[rows_p, :].astype(BF16), k_ref[rows_c, :].astype(BF16),
                            v_ref[rows_p, :].astype(BF16), v_ref[rows_c, :].astype(BF16), do_ref[rows_c, :],
                            l_ref[rows_c, :], y_ref[rows_c, :]]
                    loaded.append((rows_c, rows_p, has_prev, vals))
                done = []
                for rows_c, rows_p, has_prev, (qv, kp, kc, vp, vc, dof, lv, yv) in loaded:
                    q2 = _two_heads(qv, first_head)
                    do2 = _two_heads(dof.astype(BF16), first_head)
                    kcat, vcat = _cat(kp, kc), _cat(vp, vc)
                    lse2 = _cat(lv[:, 0:1], lv[:, 64:65])
                    dd2 = jnp.sum(_two_heads(dof * yv, first_head), axis=-1, keepdims=True)
                    p = jnp.exp(jnp.where(band_cur | (band_prev & has_prev), _nt(q2, kcat), NEG) - lse2)
                    ds = (p * (_nt(do2, vcat) - dd2)).astype(BF16)
                    dq2 = _nn(ds, kcat)
                    dkcat = _tn(ds, q2)
                    dvcat = _tn(p.astype(BF16), do2)
                    done.append((rows_c, rows_p, (jnp.where(first_head, dq2[:b], dq2[b:]), dkcat[:b], dkcat[b:],
                                                  dvcat[:b], dvcat[b:])))
                for rows_c, rows_p, (dq, dk_p, dk_c, dv_p, dv_c) in done:
                    dq_ref[rows_c, :] += dq
                    dk_ref[rows_p, :] += dk_p
                    dk_ref[rows_c, :] += dk_c
                    dv_ref[rows_p, :] += dv_p
                    dv_ref[rows_c, :] += dv_c
                return carry

            lax.fori_loop(0, n_blk // unroll, group, 0)

        if ne:
            @pl.when(pl.program_id(0) == 3)
            def _():
                _to_chips_finish(*comm)

    col_blk = lambda k: pl.BlockSpec((s, LANES), lambda hp: (0, 4 * k + hp))
    nat = pl.BlockSpec((s, LANES), lambda hp: (0, hp))
    return pl.pallas_call(
        body, grid=(4,), in_specs=[col_blk(0), col_blk(1), col_blk(2), nat, nat, nat] + [ANY] * ne,
        out_specs=[nat, nat, nat] + [ANY] * ne, out_shape=[SDS((s, ATT_W), F32)] * 3 + _to_chips_shapes(exchange),
        scratch_shapes=_to_chips_sems(ne) if ne else [], name="dil_bwd",
        compiler_params=_cp(("arbitrary",)))(qkv, qkv, qkv, dy, lse, y, *exchange)


def _sigmoid(v):
    return 1.0 / (1.0 + jnp.exp(-v))


def gate_mix(ya, yb, wa, wb, z, *, tm=512, tn=512):
    s = ya.shape[0]
    d = wa.shape[1]
    ga_blk = 3 * ATT_W * 2 // tn
    gb_blk = ga_blk + d // tn

    def body(ya_ref, yb_ref, wa_ref, wb_ref, ga_ref, gb_ref, pa_ref, pb_ref, mx_ref):
        pa = _nn(ya_ref[...], wa_ref[...])
        pb = _nn(yb_ref[...].astype(BF16), wb_ref[...])
        pa_ref[...] = pa.astype(BF16)
        pb_ref[...] = pb.astype(BF16)
        mx_ref[...] = (_sigmoid(ga_ref[...].astype(F32)) * pa + _sigmoid(gb_ref[...].astype(F32)) * pb).astype(BF16)

    out = pl.BlockSpec((tm, tn), lambda i, j: (i, j))
    return pl.pallas_call(
        body, grid=(s // tm, d // tn),
        in_specs=[pl.BlockSpec((tm, ATT_W), lambda i, j: (i, 0)), pl.BlockSpec((tm, ATT_W), lambda i, j: (i, 0)),
                  pl.BlockSpec((ATT_W, tn), lambda i, j: (0, j)), pl.BlockSpec((ATT_W, tn), lambda i, j: (0, j)),
                  pl.BlockSpec((tm, tn), lambda i, j: (i, ga_blk + j)), pl.BlockSpec((tm, tn), lambda i, j: (i, gb_blk + j))],
        out_specs=[out, out, out], out_shape=[SDS((s, d), BF16)] * 3, name="gate_mix",
        compiler_params=_cp(("parallel", "parallel")))(ya, yb, wa, wb, z, z)


def gate_bwd(dmx, z, pa, pb, *, tm=256):
    s, d = dmx.shape

    def body(dm_ref, ga_ref, gb_ref, pa_ref, pb_ref, dpa_ref, dpb_ref, dg_ref):
        dm = dm_ref[...].astype(F32)
        sa = _sigmoid(ga_ref[...].astype(F32))
        sb = _sigmoid(gb_ref[...].astype(F32))
        dpa_ref[...] = (dm * sa).astype(BF16)
        dpb_ref[...] = (dm * sb).astype(BF16)
        dg_ref[:, 0:d] = (dm * pa_ref[...].astype(F32) * sa * (1.0 - sa)).astype(BF16)
        dg_ref[:, d:2 * d] = (dm * pb_ref[...].astype(F32) * sb * (1.0 - sb)).astype(BF16)

    row = pl.BlockSpec((tm, d), lambda i: (i, 0))
    return pl.pallas_call(
        body, grid=(s // tm,),
        in_specs=[row, pl.BlockSpec((tm, d), lambda i: (i, 3)), pl.BlockSpec((tm, d), lambda i: (i, 4)), row, row],
        out_specs=[row, row, pl.BlockSpec((tm, 2 * d), lambda i: (i, 0))],
        out_shape=[SDS((s, d), BF16), SDS((s, d), BF16), SDS((s, 2 * d), BF16)], name="gate_bwd",
        compiler_params=_cp(("parallel",)))(dmx, z, z, pa, pb)


GELU_C = math.sqrt(2.0 / math.pi)


def _gelu_parts(a):
    a2 = a * a
    th = jnp.tanh(a * (GELU_C + (GELU_C * 0.044715) * a2))
    half = 0.5 * a
    gelu = half + half * th
    dgelu = (0.5 + 0.5 * th) + half * (1.0 - th * th) * (GELU_C + (3.0 * GELU_C * 0.044715) * a2)
    return gelu, dgelu


def _causal_taps(u, before):
    row = _row(u.shape)
    r1 = jnp.where(row == 0, before[7:8, :], pltpu.roll(u, 1, axis=0))
    r2 = jnp.where(row == 0, before[6:7, :], jnp.where(row == 1, before[7:8, :], pltpu.roll(u, 2, axis=0)))
    return r1, r2


def ffn_up(h, wa, wb, cw, cb, *, tm=512, tn=256):
    s, d = h.shape
    f = wa.shape[1]
    nj = f // tn

    def body(h_ref, wa_ref, wb_ref, cwa_ref, cwb_ref, cba_ref, cbb_ref, ua_ref, ub_ref, ca_ref, cbo_ref, m_ref, carry):
        @pl.when(pl.program_id(1) == 0)
        def _():
            carry[...] = jnp.zeros_like(carry)

        conv = []
        for k, (w_ref, cw_ref, cb_ref, u_ref, c_ref) in enumerate(((wa_ref, cwa_ref, cba_ref, ua_ref, ca_ref),
                                                                   (wb_ref, cwb_ref, cbb_ref, ub_ref, cbo_ref))):
            u16 = _nn(h_ref[...], w_ref[...]).astype(BF16)
            u_ref[...] = u16
            u = u16.astype(F32)
            r1, r2 = _causal_taps(u, carry[k])
            carry[k] = u[tm - 8:tm, :]
            c16 = (cw_ref[0:1, :] * r2 + cw_ref[1:2, :] * r1 + cw_ref[2:3, :] * u + cb_ref[...]).astype(BF16)
            c_ref[...] = c16
            conv.append(c16.astype(F32))
        m_ref[...] = (_gelu_parts(conv[0])[0] * conv[1]).astype(BF16)

    out = pl.BlockSpec((tm, tn), lambda j, i: (i, j))
    return pl.pallas_call(
        body, grid=(nj, s // tm),
        in_specs=[pl.BlockSpec((tm, d), lambda j, i: (i, 0)),
                  pl.BlockSpec((d, tn), lambda j, i: (0, j)), pl.BlockSpec((d, tn), lambda j, i: (0, j)),
                  pl.BlockSpec((3, tn), lambda j, i: (0, j)), pl.BlockSpec((3, tn), lambda j, i: (0, nj + j)),
                  pl.BlockSpec((1, tn), lambda j, i: (0, j)), pl.BlockSpec((1, tn), lambda j, i: (0, nj + j))],
        out_specs=[out] * 5, out_shape=[SDS((s, f), BF16)] * 5,
        scratch_shapes=[pltpu.VMEM((2, 8, tn), F32)], name="ffn_up",
        compiler_params=_cp(("parallel", "arbitrary")))(h, wa, wb, cw, cw, cb, cb)


def ffn_bwd(dm, ua, ub, ca, cbo, cw, *, tm=512, tn=256):
    s, f = dm.shape
    nj = f // tn
    ni = s // tm

    def body(dm_ref, ua_ref, ub_ref, ca_ref, cbo_ref, cwa_ref, cwb_ref, dua_ref, dub_ref, ga_ref, gb_ref, carry):
        @pl.when(pl.program_id(1) == 0)
        def _():
            carry[...] = jnp.zeros_like(carry)
            ga_ref[...] = jnp.zeros_like(ga_ref)
            gb_ref[...] = jnp.zeros_like(gb_ref)

        row = _row((tm, tn))
        dmv = dm_ref[...].astype(F32)
        gelu, dgelu = _gelu_parts(ca_ref[...].astype(F32))
        dcs = (dmv * cbo_ref[...].astype(F32) * dgelu, dmv * gelu)
        for k, (dc, u_ref, cw_ref, du_ref, g_ref) in enumerate(((dcs[0], ua_ref, cwa_ref, dua_ref, ga_ref),
                                                                (dcs[1], ub_ref, cwb_ref, dub_ref, gb_ref))):
            u = u_ref[...].astype(F32)
            after = carry[k]
            n1 = jnp.where(row == tm - 1, after[0:1, :], pltpu.roll(dc, tm - 1, axis=0))
            n2 = jnp.where(row == tm - 2, after[0:1, :], jnp.where(row == tm - 1, after[1:2, :], pltpu.roll(dc, tm - 2, axis=0)))
            g_ref[0:1, :] += jnp.sum(n2 * u, axis=0, keepdims=True)
            g_ref[1:2, :] += jnp.sum(n1 * u, axis=0, keepdims=True)
            g_ref[2:3, :] += jnp.sum(dc * u, axis=0, keepdims=True)
            g_ref[3:4, :] += jnp.sum(dc, axis=0, keepdims=True)
            du_ref[...] = (cw_ref[2:3, :] * dc + cw_ref[1:2, :] * n1 + cw_ref[0:1, :] * n2).astype(BF16)
            carry[k] = dc[0:8, :]

    tile = pl.BlockSpec((tm, tn), lambda j, i: (ni - 1 - i, j))
    gspec = pl.BlockSpec((8, tn), lambda j, i: (0, j))
    return pl.pallas_call(
        body, grid=(nj, ni),
        in_specs=[tile] * 5 + [pl.BlockSpec((3, tn), lambda j, i: (0, j)), pl.BlockSpec((3, tn), lambda j, i: (0, nj + j))],
        out_specs=[tile, tile, gspec, gspec],
        out_shape=[SDS((s, f), BF16), SDS((s, f), BF16), SDS((8, f), F32), SDS((8, f), F32)],
        scratch_shapes=[pltpu.VMEM((2, 8, tn), F32)], name="ffn_bwd",
        compiler_params=_cp(("parallel", "arbitrary")))(dm, ua, ub, ca, cbo, cw, cw)


def adamw(w, g, m, v, *, name, tr=None):
    r = w.shape[0]
    rest = w.shape[1:]
    if tr is None:
        tr = r
        for cand in (256, 128, 64, 32, 16, 8):
            if r % cand == 0:
                tr = cand
                break

    def body(w_ref, g_ref, m_ref, v_ref, d_ref, nm_ref, nv_ref):
        gv = g_ref[...]
        mn = ADAM_B1 * m_ref[...] + (1.0 - ADAM_B1) * gv
        vn = ADAM_B2 * v_ref[...] + (1.0 - ADAM_B2) * (gv * gv)
        m_hat = mn / (1.0 - ADAM_B1 ** ADAM_STEP)
        v_hat = vn / (1.0 - ADAM_B2 ** ADAM_STEP)
        d_ref[...] = -ADAM_LR * (m_hat / (jnp.sqrt(v_hat) + ADAM_EPS) + ADAM_WD * w_ref[...])
        nm_ref[...] = mn
        nv_ref[...] = vn

    blk = pl.BlockSpec((tr,) + rest, lambda i: (i,) + (0,) * len(rest))
    return pl.pallas_call(body, grid=(r // tr,), in_specs=[blk] * 4, out_specs=[blk] * 3, out_shape=[SDS(w.shape, F32)] * 3,
                          name=name, compiler_params=_cp(("parallel",)))(w, g, m, v)


def adamw_rows_view(w, g, m, v, *, name, tc=256):
    r, _, c = w.shape

    def body(w_ref, g_ref, m_ref, v_ref, d_ref, nm_ref, nv_ref, go_ref):
        gv = g_ref[...][:, None, :]
        mn = ADAM_B1 * m_ref[...] + (1.0 - ADAM_B1) * gv
        vn = ADAM_B2 * v_ref[...] + (1.0 - ADAM_B2) * (gv * gv)
        m_hat = mn / (1.0 - ADAM_B1 ** ADAM_STEP)
        v_hat = vn / (1.0 - ADAM_B2 ** ADAM_STEP)
        d_ref[...] = -ADAM_LR * (m_hat / (jnp.sqrt(v_hat) + ADAM_EPS) + ADAM_WD * w_ref[...])
        nm_ref[...] = mn
        nv_ref[...] = vn
        go_ref[...] = gv

    b3 = pl.BlockSpec((r, 1, tc), lambda i: (0, 0, i))
    b2 = pl.BlockSpec((r, tc), lambda i: (0, i))
    return pl.pallas_call(body, grid=(c // tc,), in_specs=[b3, b2, b3, b3], out_specs=[b3] * 4,
                          out_shape=[SDS(w.shape, F32)] * 4, name=name, compiler_params=_cp(("parallel",)))(w, g, m, v)


ANY = pl.BlockSpec(memory_space=pl.ANY)
ICI_KINDS = ("x", "y", "xy")


def _coords():
    return lax.axis_index("x"), lax.axis_index("y"), lax.axis_index("c")


def _peer(kind, x, y, c):
    if kind == "c":
        return (x, y, 1 - c)
    if kind == "x":
        return (1 - x, y, c)
    if kind == "y":
        return (x, 1 - y, c)
    return (1 - x, 1 - y, c)


def _chip_of(p):
    return 2 * p[0] + p[1]


def _half(rows, which):
    h = rows // 2
    return pl.ds(pl.multiple_of(which * h, 16), h)


def _remote(src, dst, send_sem, recv_sem, to):
    return pltpu.make_async_remote_copy(src_ref=src, dst_ref=dst, send_sem=send_sem, recv_sem=recv_sem,
                                        device_id=to, device_id_type=MESH)


def allgather_chips(shards, halved, *, name):
    n = len(shards)

    def body(*refs):
        parts = (refs[:n], refs[n:2 * n], refs[2 * n], refs[2 * n + 1], halved)
        _allgather_start(*parts)
        _allgather_finish(*parts)

    return pl.pallas_call(
        body, in_specs=[ANY] * n, out_specs=[ANY] * n,
        out_shape=_allgather_shapes(shards), scratch_shapes=_allgather_sems(n), name=name)(*shards)


def _allgather_shapes(shards):
    return [SDS((4,) + a.shape, a.dtype) for a in shards]


def _allgather_sems(n):
    return [pltpu.SemaphoreType.DMA((n, 6)), pltpu.SemaphoreType.DMA((n, 6))]


def _allgather_rows(ref, is_halved, which):
    r = ref.shape[0]
    return _half(r, which) if is_halved else pl.ds(0, r)


def _allgather_first(ins, outs, send_sems, recv_sems, halved):
    x, y, c = _coords()
    my_chip = 2 * x + y
    cps = []
    for w in range(len(ins)):
        rows = _allgather_rows(ins[w], halved[w], c)
        for k, kind in enumerate(ICI_KINDS):
            cps.append(_remote(ins[w].at[rows], outs[w].at[my_chip, rows], send_sems.at[w, k], recv_sems.at[w, k],
                               _peer(kind, x, y, c)))
    return cps


def _allgather_start(ins, outs, send_sems, recv_sems, halved):
    for cp in _allgather_first(ins, outs, send_sems, recv_sems, halved):
        cp.start()


def _allgather_finish(ins, outs, send_sems, recv_sems, halved):
    x, y, c = _coords()
    me = (x, y, c)
    second = []
    for w in range(len(ins)):
        for k, kind in enumerate(ICI_KINDS):
            landed = outs[w].at[_chip_of(_peer(kind, x, y, c)), _allgather_rows(ins[w], halved[w], c)]
            _remote(landed, landed, send_sems.at[w, k], recv_sems.at[w, k], me).wait_recv()
            if halved[w]:
                cp = _remote(landed, landed, send_sems.at[w, 3 + k], recv_sems.at[w, 3 + k], _peer("c", x, y, c))
                cp.start()
                second.append(cp)
    for w in range(len(ins)):
        if halved[w]:
            for k, kind in enumerate(ICI_KINDS):
                other = outs[w].at[_chip_of(_peer(kind, x, y, c)), _allgather_rows(ins[w], True, 1 - c)]
                _remote(other, other, send_sems.at[w, 3 + k], recv_sems.at[w, 3 + k], me).wait_recv()
    for cp in _allgather_first(ins, outs, send_sems, recv_sems, halved) + second:
        cp.wait_send()


def _half_of(ref, by_cols, which):
    lead = (slice(None),) * (len(ref.shape) - 2)
    if by_cols:
        h = ref.shape[-1] // 2
        return ref.at[lead + (slice(None), pl.ds(pl.multiple_of(which * h, LANES), h))]
    return ref.at[lead + (_half(ref.shape[-2], which),)]


def _half_shape(shape, by_cols):
    return shape[:-1] + (shape[-1] // 2,) if by_cols else shape[:-2] + (shape[-2] // 2, shape[-1])


def grads_to_sibling(gs, by_cols, *, name):
    n = len(gs)

    def body(*refs):
        ins, outs = refs[:n], refs[n:2 * n]
        send_sems, recv_sems = refs[2 * n:]
        x, y, c = _coords()
        cps = []
        for w in range(n):
            cp = _remote(_half_of(ins[w], by_cols[w], 1 - c), outs[w], send_sems.at[w], recv_sems.at[w], _peer("c", x, y, c))
            cp.start()
            cps.append(cp)
        for cp in cps:
            cp.wait()

    return pl.pallas_call(
        body, in_specs=[ANY] * n, out_specs=[ANY] * n,
        out_shape=[SDS(_half_shape(a.shape, bc), a.dtype) for a, bc in zip(gs, by_cols)],
        scratch_shapes=[pltpu.SemaphoreType.DMA((n,)), pltpu.SemaphoreType.DMA((n,))], name=name)(*gs)


def grads_to_chips(ps, *, name):
    n = len(ps)

    def body(*refs):
        parts = (refs[:n], refs[n:2 * n], refs[2 * n], refs[2 * n + 1])
        _to_chips_start(*parts)
        _to_chips_finish(*parts)

    return pl.pallas_call(
        body, in_specs=[ANY] * n, out_specs=[ANY] * n,
        out_shape=_to_chips_shapes(ps), scratch_shapes=_to_chips_sems(n), name=name)(*ps)


def _to_chips_shapes(ps):
    return [SDS((3,) + a.shape[1:], a.dtype) for a in ps]


def _to_chips_sems(n):
    return [pltpu.SemaphoreType.DMA((n, 3)), pltpu.SemaphoreType.DMA((n, 3))]


def _to_chips_copies(ins, outs, send_sems, recv_sems):
    x, y, c = _coords()
    cps = []
    for w in range(len(ins)):
        for k, kind in enumerate(ICI_KINDS):
            to = _peer(kind, x, y, c)
            cps.append(_remote(ins[w].at[_chip_of(to)], outs[w].at[k], send_sems.at[w, k], recv_sems.at[w, k], to))
    return cps


def _to_chips_start(ins, outs, send_sems, recv_sems):
    for cp in _to_chips_copies(ins, outs, send_sems, recv_sems):
        cp.start()


def _to_chips_finish(ins, outs, send_sems, recv_sems):
    for cp in _to_chips_copies(ins, outs, send_sems, recv_sems):
        cp.wait()


def halves_to_full(hs, by_cols, *, name):
    n = len(hs)

    def body(*refs):
        ins, outs = refs[:n], refs[n:2 * n]
        send_sems, recv_sems = refs[2 * n:]
        x, y, c = _coords()
        cps = []
        for w in range(n):
            cp = _remote(ins[w], _half_of(outs[w], by_cols[w], c), send_sems.at[w], recv_sems.at[w], _peer("c", x, y, c))
            cp.start()
            cps.append(cp)
        for cp in cps:
            cp.wait()

    return pl.pallas_call(
        body, in_specs=[ANY] * n, out_specs=[ANY] * n,
        out_shape=[SDS((a.shape[0], 2 * a.shape[1]) if bc else (2 * a.shape[0], a.shape[1]), a.dtype)
                   for a, bc in zip(hs, by_cols)],
        scratch_shapes=[pltpu.SemaphoreType.DMA((n,)), pltpu.SemaphoreType.DMA((n,))],
        name=name)(*hs)


def _row_tile(rows):
    for cand in (256, 192, 176, 128, 64, 32, 16):
        if rows % cand == 0:
            return cand
    return rows


def chip_sum(g, recv, c_arr, by_cols, *, name):
    _, r, cols = g.shape

    def body(c_ref, g_ref, r_ref, f_ref, b_ref):
        tot = g_ref[...] + r_ref[...]
        f_ref[...] = tot
        b_ref[...] = tot.astype(BF16)

    if by_cols:
        tc = 2 * LANES
        nblk = cols // 2 // tc
        shape = (4, r, cols // 2)
        blk = pl.BlockSpec((None, r, tc), lambda j, i, c_ref: (j, 0, i))
        mine = pl.BlockSpec((None, r, tc), lambda j, i, c_ref: (j, 0, c_ref[0] * nblk + i))
    else:
        tr = _row_tile(r // 2)
        nblk = r // 2 // tr
        shape = (4, r // 2, cols)
        blk = pl.BlockSpec((None, tr, cols), lambda j, i, c_ref: (j, i, 0))
        mine = pl.BlockSpec((None, tr, cols), lambda j, i, c_ref: (j, c_ref[0] * nblk + i, 0))
    grid_spec = pltpu.PrefetchScalarGridSpec(num_scalar_prefetch=1, grid=(4, nblk), in_specs=[mine, blk], out_specs=[blk, blk])
    return pl.pallas_call(body, grid_spec=grid_spec, out_shape=[SDS(shape, F32), SDS(shape, BF16)],
                          name=name, compiler_params=_cp(("parallel", "parallel")))(c_arr, g, recv)


def final_sum(pf, recv, chip_arr, *, name):
    _, h, cols = pf.shape
    tr = _row_tile(h)

    def body(chip_ref, p_ref, r_ref, o_ref):
        o_ref[...] = ((p_ref[...] + r_ref[0].astype(F32)) + r_ref[1].astype(F32)) + r_ref[2].astype(F32)

    grid_spec = pltpu.PrefetchScalarGridSpec(
        num_scalar_prefetch=1, grid=(h // tr,),
        in_specs=[pl.BlockSpec((None, tr, cols), lambda i, chip_ref: (chip_ref[0], i, 0)),
                  pl.BlockSpec((3, tr, cols), lambda i, chip_ref: (0, i, 0))],
        out_specs=pl.BlockSpec((tr, cols), lambda i, chip_ref: (i, 0)))
    return pl.pallas_call(body, grid_spec=grid_spec, out_shape=SDS((h, cols), F32), name=name,
                          compiler_params=_cp(("parallel",)))(chip_arr, pf, recv)


def allreduce_small(v, *, name):
    rws, cols = v.shape

    def body(v_ref, all_ref, sum_ref, send_sems, recv_sems, local_sem):
        x, y, c = _coords()
        me, sibling = (x, y, c), (x, y, 1 - c)
        chips = [(1 - x, y), (x, 1 - y), (1 - x, 1 - y)]

        def rows(px, py, pc):
            return all_ref.at[pl.ds(pl.multiple_of((4 * px + 2 * py + pc) * rws, 8), rws), :]

        def copy(k, block, to, src=None):
            return _remote(rows(*block) if src is None else src, rows(*block), send_sems.at[k], recv_sems.at[k], to)

        mine = pltpu.make_async_copy(v_ref, rows(*me), local_sem)
        mine.start()
        first = [copy(0, me, sibling, src=v_ref)]
        first += [copy(1 + j, me, (*chip, c), src=v_ref) for j, chip in enumerate(chips)]
        for cp in first:
            cp.start()
        passed = [copy(4 + j, (*chip, c), sibling) for j, chip in enumerate(chips)]
        for j, chip in enumerate(chips):
            copy(1 + j, (*chip, c), me).wait_recv()
            passed[j].start()
        copy(0, sibling, me).wait_recv()
        for j, chip in enumerate(chips):
            copy(4 + j, (*chip, 1 - c), me).wait_recv()
        for cp in first + passed:
            cp.wait_send()
        mine.wait()
        tot = all_ref[0:rws, :]
        for dev in range(1, 8):
            tot = tot + all_ref[dev * rws:(dev + 1) * rws, :]
        sum_ref[...] = tot

    vm = pl.BlockSpec(memory_space=pltpu.VMEM)
    return pl.pallas_call(
        body, in_specs=[vm], out_specs=[vm, vm],
        out_shape=[SDS((8 * rws, cols), v.dtype), SDS((rws, cols), v.dtype)],
        scratch_shapes=[pltpu.SemaphoreType.DMA((7,)), pltpu.SemaphoreType.DMA((7,)), pltpu.SemaphoreType.DMA],
        name=name)(v)[1]


def _pack_rows(parts, rows):
    out = []
    for a, r in zip(parts, rows):
        flat = a.reshape(-1)
        flat = jnp.pad(flat, (0, r * LANES - flat.shape[0]))
        out.append(flat.reshape(r, LANES))
    return jnp.concatenate(out, axis=0)


def _unpack_rows(packed, shapes, rows):
    out, at = [], 0
    for shp, r in zip(shapes, rows):
        size = int(np.prod(shp))
        out.append(packed[at:at + r].reshape(-1)[:size].reshape(shp))
        at += r
    return out


def kernel(x, g_pre_mix, w_in, b_forget, w_o_fox, w_o_dil, w_out, g_post_mix, g_pre_ffn, w_up, conv_w, conv_b, w_down, g_post_ffn, loss_target, m_g_pre_mix, m_w_in, m_b_forget, m_w_o_fox, m_w_o_dil, m_w_out, m_g_post_mix, m_g_pre_ffn, m_w_up, m_conv_w, m_conv_b, m_w_down, m_g_post_ffn, v_g_pre_mix, v_w_in, v_b_forget, v_w_o_fox, v_w_o_dil, v_w_out, v_g_post_mix, v_g_pre_ffn, v_w_up, v_conv_w, v_conv_b, v_w_down, v_g_post_ffn):
    xi, yi, ci = _coords()
    chip = 2 * xi + yi
    c_arr = jnp.reshape(ci, (1,)).astype(jnp.int32)
    chip_arr = jnp.reshape(chip, (1,)).astype(jnp.int32)
    xs = x[0]
    target = loss_target[0]
    s, d = xs.shape
    f_half = w_down.shape[1] * 4
    cols_in = w_in.shape[2]

    big = (w_in, w_o_fox, w_o_dil, w_out, w_up, w_down)
    shards = [w[0].astype(BF16) for w in big]
    a_in, a_cw = allgather_chips([shards[0], conv_w[0]], [True, False], name="allgather_w_in")
    w_in_full = jnp.concatenate([jnp.where(chip == j, shards[0], a_in[j]) for j in range(4)], axis=1)
    cw = jnp.concatenate([jnp.where(chip == j, conv_w[0], a_cw[j]) for j in range(4)], axis=1)
    nf = N_HEADS
    e_a, e_b = 3 * ATT_W, 3 * ATT_W + nf
    wz = jnp.concatenate([w_in_full[:, :e_a], w_in_full[:, e_b:]], axis=1)
    wf = jnp.pad(w_in_full[:, e_a:e_b], ((0, 0), (0, LANES - nf)))
    cb = conv_b
    bfo = jnp.pad(b_forget, ((0, 0), (0, LANES - nf)))

    h1 = rmsnorm_fwd(xs, g_pre_mix)
    z = mm([(h1, d, 0)], [(wz, d, 0)], nt=False, out_dtype=BF16, tm=1024, tn=512, name="in_proj")
    fa = mm([(h1, d, 0)], [(wf, d, 0)], nt=False, out_dtype=F32, tm=1024, tn=LANES, name="in_proj_forget")
    q_aug, k_aug, v_aug = fox_prep(z, fa, bfo)
    ya, lse_a, *late = fox_fwd(q_aug, k_aug, v_aug, gather=shards[1:])
    a_of, a_od, a_out, a_up, a_down = [
        lax.dynamic_update_index_in_dim(a4, own, chip, 0) for a4, own in zip(late, shards[1:])]
    wo_a = jnp.concatenate([a_of[j] for j in range(4)], axis=1)
    wo_b = jnp.concatenate([a_od[j] for j in range(4)], axis=1)
    w_o = a_out.reshape(d, d)
    w_dn = a_down.reshape(f_half, d)
    wu_a = jnp.concatenate([a_up[0], a_up[1]], axis=1)
    wu_b = jnp.concatenate([a_up[2], a_up[3]], axis=1)
    qkv_b = rope_apply([(z, Z_QB, QK_SCALE, True), (z, Z_KB, 1.0, True), (z, Z_VB, 1.0, False)], rope_tables(s, 1.0),
                       out_dtype=F32, name="rope_fwd")
    yb, lse_b = dil_fwd_all(qkv_b)
    pa, pb, mixed = gate_mix(ya, yb, wo_a, wo_b, z)
    y1, x1, h2 = proj_norm_res(mixed, w_o, g_post_mix, xs, g_pre_ffn, name="out_proj")
    ua, ub, conv_a, conv_bh, mid = ffn_up(h2, wu_a, wu_b, cw, cb)
    dout, dy2, gg_post_ffn, sq = proj_norm_loss(mid, w_dn, g_post_ffn, x1, target, name="down_proj")
    loss = lax.psum(0.5 * sq[0, 0] / d, ("x", "y", "c"))

    dmid = mm([(dy2, d, 0)], [(w_dn, d, 0)], nt=True, out_dtype=BF16, tm=512, tn=f_half // 2, name="down_dgrad")
    dw_down = wgrad((mid, f_half, 0), dy2, tk=f_half // 2, tn=1024, ts=1024, name="down_wgrad")
    dua, dub, gc_a, gc_b = ffn_bwd(dmid, ua, ub, conv_a, conv_bh, cw)
    dx1, dy1, gg_pre_ffn, gg_post_mix = mm_norm_bwd(
        [(dua, f_half, 0), (dub, f_half, 0)], [(wu_a, f_half, 0), (wu_b, f_half, 0)],
        [(x1, g_pre_ffn, dout, F32), (y1, g_post_mix, None, BF16)], name="up_dgrad")
    dw_up = jnp.concatenate(
        [wgrad((h2, d, 0), du, tk=1024, tn=f_half // 2, ts=1024, name=f"up_wgrad_{k}", chip_major=True)
         for k, du in enumerate((dua, dub))], axis=0)
    def to_chip_sums(gs, nms, tag, by_cols=False):
        from_sib = grads_to_sibling(gs, [by_cols] * len(gs), name=f"grads_to_sibling_{tag}")
        return [chip_sum(g, r, c_arr, by_cols, name=f"chip_sum_{nm}") for g, r, nm in zip(gs, from_sib, nms)]

    sums_ffn = to_chip_sums([dw_up, dw_down.reshape(4, f_half // 4, d)], ("w_up", "w_down"), "ffn")
    dmixed = mm([(dy1, d, 0)], [(w_o, d, 0)], nt=True, out_dtype=BF16, tm=512, tn=512, name="out_dgrad")
    dw_out = wgrad((mixed, d, 0), dy1, tk=1024, tn=1024, ts=1024, name="out_wgrad")
    dpa, dpb, dz_g = gate_bwd(dmixed, z, pa, pb)
    dya = mm([(dpa, d, 0)], [(wo_a, d, 0)], nt=True, out_dtype=BF16, tm=512, tn=ATT_W, name="fox_o_dgrad")
    dyb = mm([(dpb, d, 0)], [(wo_b, d, 0)], nt=True, out_dtype=F32, tm=512, tn=ATT_W, name="dil_o_dgrad")
    by_chip_cols = lambda a: jnp.stack([a[:, j * (d // 4):(j + 1) * (d // 4)] for j in range(4)], axis=0)
    dw_of = by_chip_cols(wgrad((ya, ATT_W, 0), dpa, tk=ATT_W, tn=d, ts=1024, name="fox_o_wgrad"))
    dw_od = by_chip_cols(wgrad((yb, ATT_W, 0), dpb, tk=ATT_W, tn=d, ts=1024, name="dil_o_wgrad"))
    sums_mix = to_chip_sums([dw_of, dw_od, dw_out.reshape(4, d // 4, d)], ("w_o_fox", "w_o_dil", "w_out"), "mix")
    dd_a = head_rowsum(dya, ya, name="fox_delta")
    dq_aug, dk_aug, dv_a, *got_ffn = fox_bwd(q_aug, k_aug, z, dya, lse_a, dd_a, exchange=[p[1] for p in sums_ffn])
    dz_a, dfa, gg_bf = fox_post(dq_aug, dk_aug, dv_a, fa, bfo)
    dq_b, dk_b, dv_b, *got_mix = dil_bwd_all(qkv_b, dyb, lse_b, yb, exchange=[p[1] for p in sums_mix])
    dz_b = rope_apply([(dq_b, 0, QK_SCALE, True), (dk_b, 0, 1.0, True), (dv_b, 0, 1.0, False)],
                      rope_tables(s, -1.0), out_dtype=BF16, name="rope_bwd")
    dwt_a = wgrad((dz_a, e_a, 0), h1, tk=e_a // 2, tn=d, ts=1024, name="in_wgrad_a")
    dwt_b = wgrad((dz_b, e_a, 0), h1, tk=e_a // 2, tn=d, ts=1024, name="in_wgrad_b")
    dwt_g = wgrad((dz_g, 2 * d, 0), h1, tk=d, tn=d, ts=1024, name="in_wgrad_g")
    dwt_f = wgrad((dfa, LANES, 0), h1, tk=LANES, tn=d, ts=1024, name="in_wgrad_f")
    dwt_full = jnp.concatenate([dwt_a, dwt_f[:nf], dwt_b, dwt_g], axis=0)
    dw_in = jnp.stack([dwt_full[j * cols_in:(j + 1) * cols_in] for j in range(4)], axis=0)
    sums_in = to_chip_sums([dw_in], ("w_in",), "in", by_cols=True)
    grad_x, gg_pre_mix, *got_in = mm_norm_bwd(
        [(dz_a, e_a, 0), (dz_b, e_a, 0), (dz_g, d, 0), (dz_g, d, 1), (dfa, LANES, 0)],
        [(wz, e_a, 0), (wz, e_a, 1), (wz, d, 3), (wz, d, 4), (wf, LANES, 0)],
        [(xs, g_pre_mix, dx1, F32)], exchange=[sums_in[0][1]], name="in_dgrad")

    names = ("w_in", "w_o_fox", "w_o_dil", "w_out", "w_up", "w_down")
    sums = sums_in + sums_mix + sums_ffn
    from_chips = list(got_in) + list(got_mix) + list(got_ffn)
    halves = [final_sum(p[0], r, chip_arr, name=f"final_sum_{nm}") for p, r, nm in zip(sums, from_chips, names)]
    from_half = halves_to_full(halves, [True] + [False] * 5, name="halves_to_full")
    g_big = [lax.dynamic_update_slice_in_dim(full, mine, ci * mine.shape[k == 0], axis=int(k == 0))
             for k, (full, mine) in enumerate(zip(from_half, halves))]
    upd_big = [adamw(w[0], g, m[0], v[0], name=f"adamw_{nm}") for w, g, m, v, nm in list(zip(
        big, g_big, (m_w_in, m_w_o_fox, m_w_o_dil, m_w_out, m_w_up, m_w_down),
        (v_w_in, v_w_o_fox, v_w_o_dil, v_w_out, v_w_up, v_w_down), names))[1:]]
    to_t = lambda a: jnp.transpose(a, (2, 0, 1))
    from_t = lambda a: jnp.transpose(a, (1, 2, 0))
    *upd_in, g_in_t = adamw_rows_view(to_t(w_in), g_big[0], to_t(m_w_in), to_t(v_w_in), name="adamw_w_in")

    g_cw_loc = jnp.concatenate([gc_a[0:3], gc_b[0:3]], axis=1)
    g_cb_loc = jnp.concatenate([gc_a[3:4], gc_b[3:4]], axis=1)
    small_loc = [gg_pre_mix, gg_post_mix, gg_pre_ffn, gg_post_ffn, g_cb_loc, gg_bf[:, :nf], g_cw_loc]
    red_rows = (8, 8, 8, 8, 48, 8, 136)
    red = allreduce_small(_pack_rows(small_loc, red_rows), name="allreduce_small")
    g_pm, g_qm, g_pf, g_qf, g_cb, g_bf, g_cw_full = _unpack_rows(red, [a.shape for a in small_loc], red_rows)
    cols_cw = conv_w.shape[2]
    g_cw = lax.dynamic_slice_in_dim(g_cw_full, chip * cols_cw, cols_cw, axis=1)
    small_w = (g_pre_mix, g_post_mix, g_pre_ffn, g_post_ffn, conv_b, b_forget, conv_w[0])
    small_m = (m_g_pre_mix, m_g_post_mix, m_g_pre_ffn, m_g_post_ffn, m_conv_b, m_b_forget, m_conv_w[0])
    small_v = (v_g_pre_mix, v_g_post_mix, v_g_pre_ffn, v_g_post_ffn, v_conv_b, v_b_forget, v_conv_w[0])
    small_g = (g_pm, g_qm, g_pf, g_qf, g_cb, g_bf, g_cw)
    ad_rows = (8, 8, 8, 8, 48, 8, 40)
    packed = [_pack_rows(t, ad_rows) for t in (small_w, small_g, small_m, small_v)]
    upd_small = [_unpack_rows(o, [a.shape for a in small_w], ad_rows) for o in adamw(*packed, name="adamw_small")]

    order = ("g_pre_mix", "w_in", "b_forget", "w_o_fox", "w_o_dil", "w_out", "g_post_mix", "g_pre_ffn", "w_up", "conv_w",
             "conv_b", "w_down", "g_post_ffn")
    small_names = ("g_pre_mix", "g_post_mix", "g_pre_ffn", "g_post_ffn", "conv_b", "b_forget", "conv_w")
    grads, deltas, new_ms, new_vs = {}, {}, {}, {}
    grads["w_in"] = from_t(g_in_t)
    deltas["w_in"], new_ms["w_in"], new_vs["w_in"] = (from_t(a) for a in upd_in)
    for k, nm in enumerate(names[1:]):
        grads[nm] = g_big[k + 1][None]
        deltas[nm], new_ms[nm], new_vs[nm] = (a[None] for a in upd_big[k])
    for k, nm in enumerate(small_names):
        lead = (lambda a: a[None]) if nm == "conv_w" else (lambda a: a)
        grads[nm] = lead(small_g[k])
        deltas[nm], new_ms[nm], new_vs[nm] = (lead(upd_small[j][k]) for j in range(3))
    return (loss, grad_x[None], *[grads[nm] for nm in order], *[deltas[nm] for nm in order],
            *[new_ms[nm] for nm in order], *[new_vs[nm] for nm in order])
```

```python
import functools
import math

import numpy as np
import jax
import jax.numpy as jnp
from jax import lax
from jax.experimental import pallas as pl
from jax.experimental.pallas import tpu as pltpu

F32 = jnp.float32
BF16 = jnp.bfloat16
SDS = jax.ShapeDtypeStruct
MESH = pl.DeviceIdType.MESH

HEAD_DIM = 64
N_HEADS = 8
LANES = 128
ATT_W = N_HEADS * HEAD_DIM
DIL_PATTERNS = ((128, 1), (512, 4), (2048, 16))
DIL_BLK = 128
ROPE_DIM = HEAD_DIM // 4
ROPE_THETA = 500000.0
RMS_EPS = 1e-6
NEG = -1e30
QK_SCALE = 1.0 / math.sqrt(HEAD_DIM)
ADAM_LR, ADAM_B1, ADAM_B2, ADAM_EPS, ADAM_WD, ADAM_STEP = 0.001, 0.9, 0.999, 1e-08, 0.01, 10
VMEM_LIMIT = 56 * 1024 * 1024

Z_QA, Z_KA, Z_VA, Z_QB, Z_KB, Z_VB = 0, 1, 2, 3, 4, 5
Z_W = 5120


def _cp(sem):
    return pltpu.CompilerParams(dimension_semantics=sem, vmem_limit_bytes=VMEM_LIMIT)


def _nt(a, b):
    return lax.dot_general(a, b, (((1,), (1,)), ((), ())), preferred_element_type=F32)


def _tn(a, b):
    return lax.dot_general(a, b, (((0,), (0,)), ((), ())), preferred_element_type=F32)


def _nn(a, b):
    return jnp.dot(a, b, preferred_element_type=F32)


def _lane(shape):
    return lax.broadcasted_iota(jnp.int32, shape, 1)


def _row(shape):
    return lax.broadcasted_iota(jnp.int32, shape, 0)


def rmsnorm_fwd(x, g, *, tm=512):
    s, d = x.shape

    def body(x_ref, g_ref, h_ref):
        xv = x_ref[...]
        inv = lax.rsqrt(jnp.mean(xv * xv, axis=-1, keepdims=True) + RMS_EPS)
        h_ref[...] = (xv * inv * g_ref[...]).astype(h_ref.dtype)

    return pl.pallas_call(
        body, grid=(s // tm,),
        in_specs=[pl.BlockSpec((tm, d), lambda i: (i, 0)), pl.BlockSpec((1, d), lambda i: (0, 0))],
        out_specs=pl.BlockSpec((tm, d), lambda i: (i, 0)),
        out_shape=SDS((s, d), BF16), name="rmsnorm_fwd", compiler_params=_cp(("parallel",)))(x, g)


def rmsnorm_bwd(dh, x, g, res, *, out_dtype, tm=256, name):
    s, d = x.shape
    n = s // tm
    has_res = res is not None

    def body(*refs):
        if has_res:
            dh_ref, x_ref, g_ref, res_ref, dx_ref, dg_ref, acc = refs
        else:
            dh_ref, x_ref, g_ref, dx_ref, dg_ref, acc = refs
        i = pl.program_id(0)

        @pl.when(i == 0)
        def _():
            acc[...] = jnp.zeros_like(acc)

        xv = x_ref[...]
        inv = lax.rsqrt(jnp.mean(xv * xv, axis=-1, keepdims=True) + RMS_EPS)
        xh = xv * inv
        dhv = dh_ref[...].astype(F32)
        dxh = dhv * g_ref[...]
        dot = jnp.mean(dxh * xh, axis=-1, keepdims=True)
        dx = inv * (dxh - xh * dot)
        if has_res:
            dx = dx + res_ref[...]
        dx_ref[...] = dx.astype(dx_ref.dtype)
        acc[...] += jnp.sum((dhv * xh).reshape(tm // 8, 8, d), axis=0)

        @pl.when(i == n - 1)
        def _():
            dg_ref[...] = jnp.sum(acc[...], axis=0, keepdims=True)

    row = pl.BlockSpec((tm, d), lambda i: (i, 0))
    in_specs = [row, row, pl.BlockSpec((1, d), lambda i: (0, 0))] + ([row] if has_res else [])
    args = [dh, x, g] + ([res] if has_res else [])
    return pl.pallas_call(
        body, grid=(n,), in_specs=in_specs,
        out_specs=[row, pl.BlockSpec((1, d), lambda i: (0, 0))],
        out_shape=[SDS((s, d), out_dtype), SDS((1, d), F32)],
        scratch_shapes=[pltpu.VMEM((8, d), F32)],
        name=name, compiler_params=_cp(("arbitrary",)))(*args)


def mm(a_views, b_views, *, nt, out_dtype, tm, tn, name):
    n_p = len(a_views)
    m = a_views[0][0].shape[0]
    n = b_views[0][0].shape[0] if nt else b_views[0][0].shape[1]

    def body(*refs):
        o_ref = refs[-1]
        acc = None
        for p in range(n_p):
            av = refs[p][...].astype(BF16)
            bv = refs[n_p + p][...].astype(BF16)
            dv = _nt(av, bv) if nt else _nn(av, bv)
            acc = dv if acc is None else acc + dv
        o_ref[...] = acc.astype(o_ref.dtype)

    in_specs = []
    for arr, w, blk in a_views:
        in_specs.append(pl.BlockSpec((tm, w), functools.partial(lambda i, j, blk: (i, blk), blk=blk)))
    for arr, w, blk in b_views:
        if nt:
            in_specs.append(pl.BlockSpec((tn, w), functools.partial(lambda i, j, blk: (j, blk), blk=blk)))
        else:
            in_specs.append(pl.BlockSpec((w, tn), lambda i, j: (0, j)))
    return pl.pallas_call(
        body, grid=(m // tm, n // tn), in_specs=in_specs,
        out_specs=pl.BlockSpec((tm, tn), lambda i, j: (i, j)),
        out_shape=SDS((m, n), out_dtype), name=name,
        compiler_params=_cp(("parallel", "parallel")))(*[a[0] for a in a_views], *[b[0] for b in b_views])


def wgrad(a_view, g, *, tk, tn, ts, name, chip_major=False, slabs=None, into=None):
    arr, ka, blk = a_view
    s, n = g.shape
    ns = s // ts
    total, first = slabs if slabs else (n // tn, 0)

    def body(a_ref, g_ref, *rest):
        o_ref = rest[-1]

        @pl.when(pl.program_id(2) == 0)
        def _():
            o_ref[...] = jnp.zeros_like(o_ref)

        o_ref[...] += _tn(a_ref[...].astype(BF16), g_ref[...].astype(BF16))

    if chip_major:
        out_spec = pl.BlockSpec((None, tk, tn), lambda i, j, k: (first + j, i, 0))
        out_shape = SDS((total, ka, tn), F32)
    else:
        out_spec = pl.BlockSpec((tk, tn), lambda i, j, k: (i, j))
        out_shape = SDS((ka, n), F32)
    in_specs = [pl.BlockSpec((ts, tk), lambda i, j, k: (k, blk * (ka // tk) + i)),
                pl.BlockSpec((ts, tn), lambda i, j, k: (k, j))]
    args = [arr, g]
    if into is not None:
        in_specs.append(pl.BlockSpec(memory_space=pl.ANY))
        args.append(into)
    return pl.pallas_call(
        body, grid=(ka // tk, n // tn, ns), in_specs=in_specs,
        out_specs=out_spec, out_shape=out_shape, name=name,
        input_output_aliases={2: 0} if into is not None else {},
        compiler_params=_cp(("parallel", "parallel", "arbitrary")))(*args)


def _norm_bwd_rows(dh, xh, inv, g):
    dxh = dh * g
    dx = inv * (dxh - xh * jnp.mean(dxh * xh, axis=-1, keepdims=True))
    return dx, jnp.sum((dh * xh).reshape(dh.shape[0] // 8, 8, dh.shape[1]), axis=0)


def proj_norm_res(a, w, g, xres, g_next, *, tm=512, name):
    s, k = a.shape
    d = w.shape[1]

    def body(a_ref, w_ref, g_ref, x_ref, gn_ref, y_ref, o_ref, h_ref):
        y = _nn(a_ref[...], w_ref[...])
        inv = lax.rsqrt(jnp.mean(y * y, axis=-1, keepdims=True) + RMS_EPS)
        xn = x_ref[...] + y * inv * g_ref[...]
        y_ref[...] = y
        o_ref[...] = xn
        inv_n = lax.rsqrt(jnp.mean(xn * xn, axis=-1, keepdims=True) + RMS_EPS)
        h_ref[...] = (xn * inv_n * gn_ref[...]).astype(h_ref.dtype)

    row = pl.BlockSpec((tm, d), lambda i: (i, 0))
    vec = pl.BlockSpec((1, d), lambda i: (0, 0))
    return pl.pallas_call(
        body, grid=(s // tm,),
        in_specs=[pl.BlockSpec((tm, k), lambda i: (i, 0)), pl.BlockSpec((k, d), lambda i: (0, 0)), vec, row, vec],
        out_specs=[row, row, row], out_shape=[SDS((s, d), F32), SDS((s, d), F32), SDS((s, d), BF16)], name=name,
        compiler_params=_cp(("parallel",)))(a, w, g, xres, g_next)


def proj_norm_loss(a, w, g, xres, target, *, tm=512, name):
    s, k = a.shape
    d = w.shape[1]
    n = s // tm

    def body(a_ref, w_ref, g_ref, x_ref, t_ref, do_ref, dy_ref, dg_ref, l_ref, acc):
        i = pl.program_id(0)

        @pl.when(i == 0)
        def _():
            acc[...] = jnp.zeros_like(acc)
            l_ref[...] = jnp.zeros_like(l_ref)

        y = _nn(a_ref[...], w_ref[...])
        inv = lax.rsqrt(jnp.mean(y * y, axis=-1, keepdims=True) + RMS_EPS)
        yh = y * inv
        err = x_ref[...] + yh * g_ref[...] - t_ref[...]
        dout = err * (1.0 / d)
        do_ref[...] = dout
        l_ref[...] += jnp.sum(jnp.sum(err * err, axis=1, keepdims=True), axis=0, keepdims=True)
        dy, part = _norm_bwd_rows(dout, yh, inv, g_ref[...])
        dy_ref[...] = dy.astype(dy_ref.dtype)
        acc[...] += part

        @pl.when(i == n - 1)
        def _():
            dg_ref[...] = jnp.sum(acc[...], axis=0, keepdims=True)

    row = pl.BlockSpec((tm, d), lambda i: (i, 0))
    vec = pl.BlockSpec((1, d), lambda i: (0, 0))
    return pl.pallas_call(
        body, grid=(n,),
        in_specs=[pl.BlockSpec((tm, k), lambda i: (i, 0)), pl.BlockSpec((k, d), lambda i: (0, 0)), vec, row, row],
        out_specs=[row, row, vec, pl.BlockSpec((1, 1), lambda i: (0, 0))],
        out_shape=[SDS((s, d), F32), SDS((s, d), BF16), SDS((1, d), F32), SDS((1, 1), F32)],
        scratch_shapes=[pltpu.VMEM((8, d), F32)], name=name, compiler_params=_cp(("arbitrary",)))(a, w, g, xres, target)


def mm_norm_bwd(a_views, b_views, stages, exchange=(), *, tm=256, name):
    n_p, n_s, ne = len(a_views), len(stages), len(exchange)
    s = a_views[0][0].shape[0]
    d = b_views[0][0].shape[0]
    n = s // tm
    has_res = [st[2] is not None for st in stages]

    def body(*refs):
        a_refs, b_refs = refs[:n_p], refs[n_p:2 * n_p]
        at = 2 * n_p
        st_refs = []
        for k in range(n_s):
            cnt = 3 if has_res[k] else 2
            st_refs.append(refs[at:at + cnt])
            at += cnt
        e_ins = refs[at:at + ne]
        at += ne
        dx_refs, dg_refs = refs[at:at + n_s], refs[at + n_s:at + 2 * n_s]
        at += 2 * n_s
        e_outs = refs[at:at + ne]
        at += ne
        accs = refs[at:at + n_s]
        comm = (e_ins, e_outs) + tuple(refs[at + n_s:])
        i = pl.program_id(0)

        @pl.when(i == 0)
        def _():
            for acc in accs:
                acc[...] = jnp.zeros_like(acc)
            if ne:
                _to_chips_start(*comm)

        dh = None
        for p in range(n_p):
            part = _nt(a_refs[p][...].astype(BF16), b_refs[p][...].astype(BF16))
            dh = part if dh is None else dh + part
        for k in range(n_s):
            xv = st_refs[k][0][...]
            inv = lax.rsqrt(jnp.mean(xv * xv, axis=-1, keepdims=True) + RMS_EPS)
            dx, part = _norm_bwd_rows(dh, xv * inv, inv, st_refs[k][1][...])
            if has_res[k]:
                dx = dx + st_refs[k][2][...]
            dx_refs[k][...] = dx.astype(dx_refs[k].dtype)
            accs[k][...] += part
            dh = dx

        @pl.when(i == n - 1)
        def _():
            for k in range(n_s):
                dg_refs[k][...] = jnp.sum(accs[k][...], axis=0, keepdims=True)
            if ne:
                _to_chips_finish(*comm)

    row = pl.BlockSpec((tm, d), lambda i: (i, 0))
    vec = pl.BlockSpec((1, d), lambda i: (0, 0))
    in_specs, args = [], []
    for arr, w, blk in a_views:
        in_specs.append(pl.BlockSpec((tm, w), functools.partial(lambda i, blk: (i, blk), blk=blk)))
        args.append(arr)
    for arr, w, blk in b_views:
        in_specs.append(pl.BlockSpec((d, w), functools.partial(lambda i, blk: (0, blk), blk=blk)))
        args.append(arr)
    for x, g, res, _ in stages:
        in_specs += [row, vec] + ([row] if res is not None else [])
        args += [x, g] + ([res] if res is not None else [])
    return pl.pallas_call(
        body, grid=(n,), in_specs=in_specs + [ANY] * ne,
        out_specs=[row] * n_s + [vec] * n_s + [ANY] * ne,
        out_shape=[SDS((s, d), st[3]) for st in stages] + [SDS((1, d), F32)] * n_s + _to_chips_shapes(exchange),
        scratch_shapes=[pltpu.VMEM((8, d), F32)] * n_s + (_to_chips_sems(ne) if ne else []), name=name,
        compiler_params=_cp(("arbitrary",)))(*args, *exchange)


def _split3(v):
    hi = v.astype(BF16).astype(F32)
    r = v - hi
    mid = r.astype(BF16).astype(F32)
    lo = (r - mid).astype(BF16).astype(F32)
    return hi, mid, lo


def _tri(n, upper):
    r = np.arange(n)
    m = (r[:, None] <= r[None, :]) if upper else (r[:, None] >= r[None, :])
    return jnp.asarray(m.astype(np.float32))


def fox_prep(z, fa, bfo, *, tb=512):
    s = z.shape[0]
    n = s // tb

    def body(q_ref, k_ref, v_ref, fa_ref, b_ref, tri_ref, qa_ref, ka_ref, va_ref, carry):
        @pl.when(pl.program_id(0) == 0)
        def _():
            carry[...] = jnp.zeros_like(carry)

        xv = fa_ref[...] + b_ref[...]
        logf = jnp.minimum(xv, 0.0) - jnp.log(1.0 + jnp.exp(-jnp.abs(xv)))
        csum = jnp.dot(tri_ref[...], logf, preferred_element_type=F32, precision=lax.Precision.HIGHEST) + carry[0:1, :]
        carry[0:1, :] = csum[tb - 1:tb, :]
        lane = _lane((tb, LANES))
        for h in range(N_HEADS):
            hi, mid, lo = _split3(csum[:, h:h + 1])
            pair = (h // 2) * LANES
            qv = q_ref[:, pair:pair + LANES].astype(F32)
            kv = k_ref[:, pair:pair + LANES].astype(F32)
            vv = v_ref[:, pair:pair + LANES].astype(F32)
            if h % 2:
                qv = pltpu.roll(qv, 64, axis=1)
                kv = pltpu.roll(kv, 64, axis=1)
                vv = pltpu.roll(vv, 64, axis=1)
            va_ref[:, h * LANES:(h + 1) * LANES] = jnp.where(lane < 64, vv, jnp.where(lane == 64, 1.0, 0.0)).astype(BF16)
            one = jnp.where((lane >= 67) & (lane < 70), 1.0, 0.0)
            q_x = jnp.where(lane == 64, hi, jnp.where(lane == 65, mid, jnp.where(lane == 66, lo, one)))
            one = jnp.where((lane >= 64) & (lane < 67), 1.0, 0.0)
            k_x = jnp.where(lane == 67, -hi, jnp.where(lane == 68, -mid, jnp.where(lane == 69, -lo, one)))
            qa_ref[:, h * LANES:(h + 1) * LANES] = jnp.where(lane < 64, qv * QK_SCALE, q_x).astype(BF16)
            ka_ref[:, h * LANES:(h + 1) * LANES] = jnp.where(lane < 64, kv, k_x).astype(BF16)

    return pl.pallas_call(
        body, grid=(n,),
        in_specs=[pl.BlockSpec((tb, ATT_W), lambda i: (i, Z_QA)), pl.BlockSpec((tb, ATT_W), lambda i: (i, Z_KA)),
                  pl.BlockSpec((tb, ATT_W), lambda i: (i, Z_VA)),
                  pl.BlockSpec((tb, LANES), lambda i: (i, 0)), pl.BlockSpec((1, LANES), lambda i: (0, 0)),
                  pl.BlockSpec((tb, tb), lambda i: (0, 0))],
        out_specs=[pl.BlockSpec((tb, N_HEADS * LANES), lambda i: (i, 0))] * 3,
        out_shape=[SDS((s, N_HEADS * LANES), BF16)] * 3,
        scratch_shapes=[pltpu.VMEM((8, LANES), F32)],
        name="fox_prep", compiler_params=_cp(("arbitrary",)))(z, z, z, fa, bfo, _tri(tb, False))


def _causal_pairs(n, k_major):
    if k_major:
        pairs = [(qi, kj) for kj in range(n) for qi in range(kj, n)]
    else:
        pairs = [(qi, kj) for qi in range(n) for kj in range(qi + 1)]
    return (jnp.asarray([p[0] for p in pairs], jnp.int32), jnp.asarray([p[1] for p in pairs], jnp.int32), len(pairs))


def fox_fwd(q_aug, k_aug, v_aug, gather=(), *, t=512, hps=4):
    s = v_aug.shape[0]
    qi_arr, kj_arr, n_pairs = _causal_pairs(s // t, False)
    ng = len(gather)
    n_groups = N_HEADS // hps

    def body(qi_ref, kj_ref, q_ref, k_ref, v_ref, *rest):
        g_ins, (o_ref, lse_ref), g_outs = rest[:ng], rest[ng:ng + 2], rest[ng + 2:2 * ng + 2]
        m_scr, acc_scr = rest[2 * ng + 2:2 * ng + 4]
        comm = (g_ins, g_outs) + tuple(rest[2 * ng + 4:]) + ([True] * ng,)
        step = pl.program_id(1)
        qi = qi_ref[step]
        kj = kj_ref[step]
        if ng:
            @pl.when((pl.program_id(0) == 0) & (step == 0))
            def _():
                _allgather_start(*comm)

        @pl.when(kj == 0)
        def _():
            m_scr[...] = jnp.full_like(m_scr, NEG)
            acc_scr[...] = jnp.zeros_like(acc_scr)

        def update(masked):
            for i in range(hps):
                sc = _nt(q_ref[:, i * LANES:(i + 1) * LANES], k_ref[:, i * LANES:(i + 1) * LANES])
                if masked:
                    sc = jnp.where(_row((t, t)) >= _lane((t, t)), sc, NEG)
                m_prev = m_scr[i]
                m_new = jnp.maximum(m_prev, jnp.max(sc, axis=-1, keepdims=True))
                p = jnp.exp((sc - jnp.tile(m_new, (1, t // LANES))).astype(BF16))
                acc_scr[i] = jnp.exp(m_prev - m_new) * acc_scr[i] + _nn(p, v_ref[:, i * LANES:(i + 1) * LANES])
                m_scr[i] = m_new

        @pl.when(kj < qi)
        def _():
            update(False)

        @pl.when(kj == qi)
        def _():
            update(True)
            lane = _lane((t, LANES))
            for pr in range(hps // 2):
                den = [acc_scr[2 * pr + i][:, 64:65] for i in range(2)]
                o_ref[:, pr * LANES:(pr + 1) * LANES] = jnp.where(
                    lane < 64, acc_scr[2 * pr] / den[0], pltpu.roll(acc_scr[2 * pr + 1] / den[1], 64, axis=1)).astype(o_ref.dtype)
                lse_ref[:, pr * LANES:(pr + 1) * LANES] = jnp.where(
                    lane < 64, m_scr[2 * pr] + jnp.log(den[0]), m_scr[2 * pr + 1] + jnp.log(den[1]))

        if ng:
            @pl.when((pl.program_id(0) == n_groups - 1) & (step == n_pairs - 1))
            def _():
                _allgather_finish(*comm)

    wide = hps * LANES
    grid_spec = pltpu.PrefetchScalarGridSpec(
        num_scalar_prefetch=2, grid=(n_groups, n_pairs),
        in_specs=[pl.BlockSpec((t, wide), lambda hg, st, qi, kj: (qi[st], hg)),
                  pl.BlockSpec((t, wide), lambda hg, st, qi, kj: (kj[st], hg)),
                  pl.BlockSpec((t, wide), lambda hg, st, qi, kj: (kj[st], hg))] + [ANY] * ng,
        out_specs=[pl.BlockSpec((t, wide // 2), lambda hg, st, qi, kj: (qi[st], hg))] * 2 + [ANY] * ng,
        scratch_shapes=[pltpu.VMEM((hps, t, LANES), F32)] * 2 + (_allgather_sems(ng) if ng else []))
    return pl.pallas_call(
        body, grid_spec=grid_spec, out_shape=[SDS((s, ATT_W), BF16), SDS((s, ATT_W), F32)] + _allgather_shapes(gather),
        name="fox_fwd", compiler_params=_cp(("arbitrary", "arbitrary")))(qi_arr, kj_arr, q_aug, k_aug, v_aug, *gather)


def fox_bwd(q_aug, k_aug, z, dy, lse, dd, exchange=(), *, t=512, hps=4):
    s = z.shape[0]
    qi_arr, kj_arr, n_pairs = _causal_pairs(s // t, True)
    ne = len(exchange)
    n_groups = N_HEADS // hps

    def body(qi_ref, kj_ref, q_ref, k_ref, v_ref, do_ref, lse_ref, dd_ref, *rest):
        e_ins, (dq_ref, dk_ref, dv_ref), e_outs = rest[:ne], rest[ne:ne + 3], rest[ne + 3:2 * ne + 3]
        comm = (e_ins, e_outs) + tuple(rest[2 * ne + 3:])
        step = pl.program_id(1)
        qi = qi_ref[step]
        kj = kj_ref[step]
        if ne:
            @pl.when((pl.program_id(0) == 0) & (step == 0))
            def _():
                _to_chips_start(*comm)

        @pl.when(step == 0)
        def _():
            dq_ref[...] = jnp.zeros_like(dq_ref)

        @pl.when(qi == kj)
        def _():
            dk_ref[...] = jnp.zeros_like(dk_ref)
            dv_ref[...] = jnp.zeros_like(dv_ref)

        def update(masked):
            lane = _lane((t, LANES))
            rows = pl.ds(pl.multiple_of(qi * t, t), t)
            for pr in range(hps // 2):
                pair = slice(pr * LANES, (pr + 1) * LANES)
                dov = do_ref[:, pair]
                dv_new = None
                for i in range(2):
                    head = (lane < 64) if i == 0 else (lane >= 64)
                    own = slice((2 * pr + i) * LANES, (2 * pr + i + 1) * LANES)
                    col = slice(pr * LANES + i * 64, pr * LANES + i * 64 + 1)
                    qv = q_ref[:, own]
                    kv = k_ref[:, own]
                    sc = _nt(qv, kv)
                    if masked:
                        sc = jnp.where(_row((t, t)) >= _lane((t, t)), sc, NEG)
                    p = jnp.exp(sc - lse_ref[:, col])
                    dp = _nt(jnp.where(head, dov, jnp.zeros_like(dov)), v_ref[:, pair])
                    ds = (p * (dp - dd_ref[:, col])).astype(BF16)
                    dq_ref[rows, own] += _nn(ds, kv)
                    dk_ref[:, own] += _tn(ds, qv)
                    dvi = _tn(p.astype(BF16), dov)
                    dv_new = dvi if dv_new is None else jnp.where(head, dvi, dv_new)
                dv_ref[:, pair] += dv_new

        @pl.when(kj < qi)
        def _():
            update(False)

        @pl.when(kj == qi)
        def _():
            update(True)

        if ne:
            @pl.when((pl.program_id(0) == n_groups - 1) & (step == n_pairs - 1))
            def _():
                _to_chips_finish(*comm)

    wide, half = hps * LANES, hps // 2 * LANES
    v_blk = Z_VA * ATT_W // half
    grid_spec = pltpu.PrefetchScalarGridSpec(
        num_scalar_prefetch=2, grid=(n_groups, n_pairs),
        in_specs=[pl.BlockSpec((t, wide), lambda hg, st, qi, kj: (qi[st], hg)),
                  pl.BlockSpec((t, wide), lambda hg, st, qi, kj: (kj[st], hg)),
                  pl.BlockSpec((t, half), lambda hg, st, qi, kj: (kj[st], v_blk + hg)),
                  pl.BlockSpec((t, half), lambda hg, st, qi, kj: (qi[st], hg)),
                  pl.BlockSpec((t, half), lambda hg, st, qi, kj: (qi[st], hg)),
                  pl.BlockSpec((t, half), lambda hg, st, qi, kj: (qi[st], hg))] + [ANY] * ne,
        out_specs=[pl.BlockSpec((s, wide), lambda hg, st, qi, kj: (0, hg)),
                   pl.BlockSpec((t, wide), lambda hg, st, qi, kj: (kj[st], hg)),
                   pl.BlockSpec((t, half), lambda hg, st, qi, kj: (kj[st], hg))] + [ANY] * ne,
        scratch_shapes=_to_chips_sems(ne) if ne else [])
    return pl.pallas_call(
        body, grid_spec=grid_spec,
        out_shape=[SDS((s, N_HEADS * LANES), F32), SDS((s, N_HEADS * LANES), F32), SDS((s, ATT_W), F32)]
        + _to_chips_shapes(exchange),
        name="fox_bwd", compiler_params=_cp(("arbitrary", "arbitrary")))(qi_arr, kj_arr, q_aug, k_aug, z, dy, lse, dd, *exchange)


def head_rowsum(a, b, *, tm=512, name):
    s = a.shape[0]

    def body(a_ref, b_ref, o_ref):
        prod = a_ref[...].astype(F32) * b_ref[...].astype(F32)
        lane = _lane((tm, LANES))
        lo = jnp.sum(jnp.where(lane < 64, prod, 0.0), axis=-1, keepdims=True)
        hi = jnp.sum(jnp.where(lane >= 64, prod, 0.0), axis=-1, keepdims=True)
        o_ref[...] = jnp.where(lane < 64, lo, hi)

    blk = pl.BlockSpec((tm, LANES), lambda i, j: (i, j))
    return pl.pallas_call(body, grid=(s // tm, 4), in_specs=[blk, blk], out_specs=blk, out_shape=SDS((s, ATT_W), F32),
                          name=name, compiler_params=_cp(("parallel", "parallel")))(a, b)


def fox_post(dq_aug, dk_aug, dv, fa, bfo, *, tb=512):
    s = dv.shape[0]
    n = s // tb

    def body(dq_ref, dk_ref, dv_ref, fa_ref, b_ref, tri_ref, dz_ref, dfa_ref, gb_ref, carry, acc):
        i = pl.program_id(0)

        @pl.when(i == 0)
        def _():
            carry[...] = jnp.zeros_like(carry)
            acc[...] = jnp.zeros_like(acc)

        lane = _lane((tb, LANES))
        d_f = jnp.zeros((tb, LANES), F32)
        for h in range(N_HEADS):
            col = dq_ref[:, h * LANES + 64:h * LANES + 65] - dk_ref[:, h * LANES + 67:h * LANES + 68]
            d_f = jnp.where(lane == h, col, d_f)
        suffix = jnp.dot(tri_ref[...], d_f, preferred_element_type=F32, precision=lax.Precision.HIGHEST) + carry[0:1, :]
        carry[0:1, :] = suffix[0:1, :]
        xv = fa_ref[...] + b_ref[...]
        dx = suffix * (1.0 / (1.0 + jnp.exp(xv)))
        dfa_ref[...] = dx.astype(dfa_ref.dtype)
        acc[...] += jnp.sum(dx.reshape(tb // 8, 8, LANES), axis=0)
        for hp in range(4):
            for src, off, scale in ((dq_ref, 0, QK_SCALE), (dk_ref, ATT_W, 1.0)):
                even = src[:, (2 * hp) * LANES:(2 * hp + 1) * LANES]
                odd = pltpu.roll(src[:, (2 * hp + 1) * LANES:(2 * hp + 2) * LANES], 64, axis=1)
                dz_ref[:, off + hp * LANES:off + (hp + 1) * LANES] = (jnp.where(lane < 64, even, odd) * scale).astype(BF16)
        dz_ref[:, 2 * ATT_W:3 * ATT_W] = dv_ref[...].astype(BF16)

        @pl.when(i == n - 1)
        def _():
            gb_ref[...] = jnp.sum(acc[...], axis=0, keepdims=True)

    rev = lambda i: (n - 1 - i, 0)
    return pl.pallas_call(
        body, grid=(n,),
        in_specs=[pl.BlockSpec((tb, N_HEADS * LANES), rev), pl.BlockSpec((tb, N_HEADS * LANES), rev),
                  pl.BlockSpec((tb, ATT_W), rev), pl.BlockSpec((tb, LANES), rev),
                  pl.BlockSpec((1, LANES), lambda i: (0, 0)), pl.BlockSpec((tb, tb), lambda i: (0, 0))],
        out_specs=[pl.BlockSpec((tb, 3 * ATT_W), rev), pl.BlockSpec((tb, LANES), rev),
                   pl.BlockSpec((1, LANES), lambda i: (0, 0))],
        out_shape=[SDS((s, 3 * ATT_W), BF16), SDS((s, LANES), BF16), SDS((1, LANES), F32)],
        scratch_shapes=[pltpu.VMEM((8, LANES), F32), pltpu.VMEM((8, LANES), F32)],
        name="fox_post", compiler_params=_cp(("arbitrary",)))(dq_aug, dk_aug, dv, fa, bfo, _tri(tb, True))


def rope_tables(s, sign):
    half = ROPE_DIM // 2
    inv_freq = ROPE_THETA ** (-jnp.arange(half, dtype=F32) * 2.0 / ROPE_DIM)
    ang = jnp.arange(s, dtype=F32)[:, None] * inv_freq[None, :]
    l64 = np.arange(LANES) % HEAD_DIM
    cos = jnp.tile(jnp.cos(ang), (1, LANES // half))
    sin = jnp.tile(jnp.sin(ang), (1, LANES // half)) * sign
    first = jnp.asarray(l64 < half)[None, :]
    second = jnp.asarray((l64 >= half) & (l64 < ROPE_DIM))[None, :]
    return (jnp.where(first | second, cos, 1.0), jnp.where(first, -sin, 0.0), jnp.where(second, sin, 0.0))


def rope_apply(items, tabs, *, out_dtype, tm=512, name):
    s = items[0][0].shape[0]
    n_i = len(items)

    def body(*refs):
        c_ref, sn_ref, sp_ref = refs[n_i:n_i + 3]
        o_ref = refs[-1]
        for j, (_, _, scale, rotate) in enumerate(items):
            for b in range(4):
                xv = refs[j][:, b * LANES:(b + 1) * LANES].astype(F32)
                if rotate:
                    xv = xv * c_ref[...] + pltpu.roll(xv, LANES - 8, axis=1) * sn_ref[...] + pltpu.roll(xv, 8, axis=1) * sp_ref[...]
                o_ref[:, j * ATT_W + b * LANES:j * ATT_W + (b + 1) * LANES] = (xv * scale).astype(o_ref.dtype)

    in_specs = [pl.BlockSpec((tm, ATT_W), functools.partial(lambda i, blk: (i, blk), blk=it[1])) for it in items]
    in_specs += [pl.BlockSpec((tm, LANES), lambda i: (i, 0))] * 3
    return pl.pallas_call(
        body, grid=(s // tm,), in_specs=in_specs, out_specs=pl.BlockSpec((tm, n_i * ATT_W), lambda i: (i, 0)),
        out_shape=SDS((s, n_i * ATT_W), out_dtype), name=name, compiler_params=_cp(("parallel",)))(*[it[0] for it in items], *tabs)


def _dil_views(qk, z, r):
    s = z.shape[0]
    return qk.reshape(s // r, r * 2 * ATT_W), z.reshape(s // r, r * Z_W)


def _dil_cols(r):
    q_col = lambda rho, hp: rho * 8 + hp
    k_col = lambda rho, hp: rho * 8 + 4 + hp
    v_col = lambda rho, hp: rho * (Z_W // LANES) + 4 * Z_VB + hp
    return q_col, k_col, v_col


def _dil_scores(qv, kp, kc, head, has_prev):
    b = DIL_BLK
    qm = jnp.where(head, qv, jnp.zeros_like(qv))
    row, col = _row((b, b)), _lane((b, b))
    sp = jnp.where((col >= row) & has_prev, _nt(qm, kp), NEG)
    sc = jnp.where(col <= row, _nt(qm, kc), NEG)
    return sp, sc


def dil_fwd(qk, z, prev, *, r):
    s = z.shape[0]
    b = DIL_BLK
    l_sub = s // r
    nb = l_sub // b
    qk_v, z_v = _dil_views(qk, z, r)
    q_col, k_col, v_col = _dil_cols(r)
    merge = prev is not None

    def body(*refs):
        if merge:
            q_ref, kp_ref, kc_ref, vp_ref, vc_ref, op_ref, lp_ref, o_ref, l_ref = refs
        else:
            q_ref, kp_ref, kc_ref, vp_ref, vc_ref, o_ref, l_ref = refs
        has_prev = pl.program_id(2) > 0
        lane = _lane((b, LANES))
        res = []
        for i in range(2):
            head = (lane < 64) if i == 0 else (lane >= 64)
            sp, sc = _dil_scores(q_ref[...], kp_ref[...], kc_ref[...], head, has_prev)
            m = jnp.maximum(jnp.max(sp, axis=-1, keepdims=True), jnp.max(sc, axis=-1, keepdims=True))
            pp = jnp.exp(sp - m)
            pc = jnp.exp(sc - m)
            den = jnp.sum(pp, axis=-1, keepdims=True) + jnp.sum(pc, axis=-1, keepdims=True)
            ov = (_nn(pp.astype(BF16), vp_ref[...]) + _nn(pc.astype(BF16), vc_ref[...])) / den
            res.append((ov, m + jnp.log(den)))
        ov = jnp.where(lane < 64, res[0][0], res[1][0])
        lse = jnp.where(lane < 64, res[0][1], res[1][1])
        if merge:
            lp = lp_ref[...]
            m2 = jnp.maximum(lp, lse)
            wp = jnp.exp(lp - m2)
            wn = jnp.exp(lse - m2)
            ov = (wp * op_ref[...] + wn * ov) / (wp + wn)
            lse = m2 + jnp.log(wp + wn)
        o_ref[...] = ov
        l_ref[...] = lse

    blk = lambda f: pl.BlockSpec((b, LANES), f)
    in_specs = [blk(lambda rho, hp, n: (n, q_col(rho, hp))), blk(lambda rho, hp, n: (jnp.maximum(n - 1, 0), k_col(rho, hp))),
                blk(lambda rho, hp, n: (n, k_col(rho, hp))), blk(lambda rho, hp, n: (jnp.maximum(n - 1, 0), v_col(rho, hp))),
                blk(lambda rho, hp, n: (n, v_col(rho, hp)))]
    args = [qk_v, qk_v, qk_v, z_v, z_v]
    nat = blk(lambda rho, hp, n: (n, rho * 4 + hp))
    if merge:
        in_specs += [nat, nat]
        args += [prev[0].reshape(l_sub, r * ATT_W), prev[1].reshape(l_sub, r * ATT_W)]
    o, lse = pl.pallas_call(
        body, grid=(r, 4, nb), in_specs=in_specs, out_specs=[nat, nat],
        out_shape=[SDS((l_sub, r * ATT_W), F32)] * 2, name=f"dil_fwd_r{r}",
        compiler_params=_cp(("parallel", "parallel", "arbitrary")))(*args)
    return o.reshape(s, ATT_W), lse.reshape(s, ATT_W)


def dil_bwd_dq(qk, z, dy, lse, dd, acc, *, r):
    s = z.shape[0]
    b = DIL_BLK
    l_sub = s // r
    nb = l_sub // b
    qk_v, z_v = _dil_views(qk, z, r)
    q_col, k_col, v_col = _dil_cols(r)
    add = acc is not None

    def body(*refs):
        q_ref, kp_ref, kc_ref, vp_ref, vc_ref, do_ref, l_ref, dd_ref = refs[:8]
        dq_ref = refs[-1]
        has_prev = pl.program_id(2) > 0
        lane = _lane((b, LANES))
        dov = do_ref[...]
        parts = []
        for i in range(2):
            head = (lane < 64) if i == 0 else (lane >= 64)
            sp, sc = _dil_scores(q_ref[...], kp_ref[...], kc_ref[...], head, has_prev)
            lse_i = l_ref[:, i * 64:i * 64 + 1]
            dd_i = dd_ref[:, i * 64:i * 64 + 1]
            dom = jnp.where(head, dov, jnp.zeros_like(dov))
            dsp = (jnp.exp(sp - lse_i) * (_nt(dom, vp_ref[...]) - dd_i)).astype(BF16)
            dsc = (jnp.exp(sc - lse_i) * (_nt(dom, vc_ref[...]) - dd_i)).astype(BF16)
            parts.append(_nn(dsp, kp_ref[...]) + _nn(dsc, kc_ref[...]))
        dq = jnp.where(lane < 64, parts[0], parts[1])
        if add:
            dq = dq + refs[8][...]
        dq_ref[...] = dq

    blk = lambda f: pl.BlockSpec((b, LANES), f)
    nat = blk(lambda rho, hp, n: (n, rho * 4 + hp))
    in_specs = [blk(lambda rho, hp, n: (n, q_col(rho, hp))), blk(lambda rho, hp, n: (jnp.maximum(n - 1, 0), k_col(rho, hp))),
                blk(lambda rho, hp, n: (n, k_col(rho, hp))), blk(lambda rho, hp, n: (jnp.maximum(n - 1, 0), v_col(rho, hp))),
                blk(lambda rho, hp, n: (n, v_col(rho, hp))), nat, nat, nat]
    nview = lambda a: a.reshape(l_sub, r * ATT_W)
    args = [qk_v, qk_v, qk_v, z_v, z_v, nview(dy), nview(lse), nview(dd)]
    if add:
        in_specs.append(nat)
        args.append(nview(acc))
    dq = pl.pallas_call(
        body, grid=(r, 4, nb), in_specs=in_specs, out_specs=nat, out_shape=SDS((l_sub, r * ATT_W), F32),
        name=f"dil_bwd_dq_r{r}", compiler_params=_cp(("parallel", "parallel", "arbitrary")))(*args)
    return dq.reshape(s, ATT_W)


def dil_bwd_dkv(qk, z, dy, lse, dd, acc, *, r):
    s = z.shape[0]
    b = DIL_BLK
    l_sub = s // r
    nb = l_sub // b
    qk_v, z_v = _dil_views(qk, z, r)
    q_col, k_col, v_col = _dil_cols(r)
    add = acc is not None

    def body(*refs):
        k_ref, v_ref, qc_ref, qn_ref, doc_ref, don_ref, lc_ref, ln_ref, ddc_ref, ddn_ref = refs[:10]
        dk_ref, dv_ref = refs[-2:]
        has_next = pl.program_id(2) < nb - 1
        lane = _lane((b, LANES))
        row, col = _row((b, b)), _lane((b, b))
        kv = k_ref[...]
        vv = v_ref[...]
        dk_parts, dv_parts = [], []
        for i in range(2):
            head = (lane < 64) if i == 0 else (lane >= 64)
            dk_i = jnp.zeros((b, LANES), F32)
            dv_i = jnp.zeros((b, LANES), F32)
            for q_ref, do_ref, l_ref, d_ref, mask in ((qc_ref, doc_ref, lc_ref, ddc_ref, col <= row),
                                                      (qn_ref, don_ref, ln_ref, ddn_ref, (col >= row) & has_next)):
                qv = q_ref[...]
                dov = do_ref[...]
                sc = jnp.where(mask, _nt(jnp.where(head, qv, jnp.zeros_like(qv)), kv), NEG)
                p = jnp.exp(sc - l_ref[:, i * 64:i * 64 + 1])
                dp = _nt(jnp.where(head, dov, jnp.zeros_like(dov)), vv)
                ds = (p * (dp - d_ref[:, i * 64:i * 64 + 1])).astype(BF16)
                dv_i = dv_i + _tn(p.astype(BF16), dov)
                dk_i = dk_i + _tn(ds, qv)
            dk_parts.append(dk_i)
            dv_parts.append(dv_i)
        dk = jnp.where(lane < 64, dk_parts[0], dk_parts[1])
        dv = jnp.where(lane < 64, dv_parts[0], dv_parts[1])
        if add:
            dk = dk + refs[10][...]
            dv = dv + refs[11][...]
        dk_ref[...] = dk
        dv_ref[...] = dv

    blk = lambda f: pl.BlockSpec((b, LANES), f)
    nat = blk(lambda rho, hp, n: (n, rho * 4 + hp))
    nxt = blk(lambda rho, hp, n: (jnp.minimum(n + 1, nb - 1), rho * 4 + hp))
    in_specs = [blk(lambda rho, hp, n: (n, k_col(rho, hp))), blk(lambda rho, hp, n: (n, v_col(rho, hp))),
                blk(lambda rho, hp, n: (n, q_col(rho, hp))), blk(lambda rho, hp, n: (jnp.minimum(n + 1, nb - 1), q_col(rho, hp))),
                nat, nxt, nat, nxt, nat, nxt]
    nview = lambda a: a.reshape(l_sub, r * ATT_W)
    args = [qk_v, z_v, qk_v, qk_v, nview(dy), nview(dy), nview(lse), nview(lse), nview(dd), nview(dd)]
    if add:
        in_specs += [nat, nat]
        args += [nview(acc[0]), nview(acc[1])]
    dk, dv = pl.pallas_call(
        body, grid=(r, 4, nb), in_specs=in_specs, out_specs=[nat, nat],
        out_shape=[SDS((l_sub, r * ATT_W), F32)] * 2, name=f"dil_bwd_dkv_r{r}",
        compiler_params=_cp(("parallel", "parallel", "arbitrary")))(*args)
    return dk.reshape(s, ATT_W), dv.reshape(s, ATT_W)


def _dil_rows(base, r):
    if r == 1:
        return pl.ds(pl.multiple_of(base, DIL_BLK), DIL_BLK)
    return pl.ds(base, DIL_BLK, stride=r)


def _dil_block(idx, r, nb):
    shift = nb.bit_length() - 1
    rho = idx >> shift
    n = idx & (nb - 1)
    base = rho + n * (r * DIL_BLK)
    return _dil_rows(base, r), _dil_rows(jnp.maximum(base - r * DIL_BLK, rho), r), n > 0


def _cat(a, b):
    return jnp.concatenate([a, b], axis=0)


def _two_heads(v, first_head):
    zero = jnp.zeros_like(v)
    return _cat(jnp.where(first_head, v, zero), jnp.where(first_head, zero, v))


def _dil_bands():
    b = DIL_BLK
    q = _row((2 * b, 2 * b)) & (b - 1)
    col = _lane((2 * b, 2 * b))
    return (col < b) & (col >= q), (col >= b) & (col - b <= q)


def dil_fwd_all(qkv, *, unroll=8):
    s = qkv.shape[0]
    b = DIL_BLK
    n_blk = s // b

    def body(q_ref, k_ref, v_ref, o_ref, l_ref):
        first_head = _lane((b, LANES)) < 64
        band_prev, band_cur = _dil_bands()
        for g, (_, r) in enumerate(DIL_PATTERNS):
            nb = n_blk // r

            def group(it, carry, g=g, r=r, nb=nb):
                loaded = []
                for u in range(unroll):
                    rows_c, rows_p, has_prev = _dil_block(it * unroll + u, r, nb)
                    vals = [q_ref[rows_c, :].astype(BF16), k_ref[rows_p, :].astype(BF16), k_ref[rows_c, :].astype(BF16),
                            v_ref[rows_p, :].astype(BF16), v_ref[rows_c, :].astype(BF16)]
                    state = (o_ref[rows_c, :], l_ref[rows_c, :]) if g else None
                    loaded.append((rows_c, has_prev, vals, state))
                done = []
                for rows_c, has_prev, (qv, kp, kc, vp, vc), state in loaded:
                    sc = jnp.where(band_cur | (band_prev & has_prev), _nt(_two_heads(qv, first_head), _cat(kp, kc)), NEG)
                    m = jnp.max(sc, axis=-1, keepdims=True)
                    p = jnp.exp(sc - m)
                    den = jnp.sum(p, axis=-1, keepdims=True)
                    both = _nn(p.astype(BF16), _cat(vp, vc)) / den
                    lse2 = m + jnp.log(den)
                    ov = jnp.where(first_head, both[:b], both[b:])
                    lse = jnp.where(first_head, lse2[:b], lse2[b:])
                    if state is not None:
                        m2 = jnp.maximum(state[1], lse)
                        wp = jnp.exp(state[1] - m2)
                        wn = jnp.exp(lse - m2)
                        ov = (wp * state[0] + wn * ov) / (wp + wn)
                        lse = m2 + jnp.log(wp + wn)
                    done.append((rows_c, ov, lse))
                for rows_c, ov, lse in done:
                    o_ref[rows_c, :] = ov
                    l_ref[rows_c, :] = lse
                return carry

            lax.fori_loop(0, n_blk // unroll, group, 0)

    col_blk = lambda k: pl.BlockSpec((s, LANES), lambda hp: (0, 4 * k + hp))
    out = pl.BlockSpec((s, LANES), lambda hp: (0, hp))
    return pl.pallas_call(
        body, grid=(4,), in_specs=[col_blk(0), col_blk(1), col_blk(2)], out_specs=[out, out],
        out_shape=[SDS((s, ATT_W), F32)] * 2, name="dil_fwd", compiler_params=_cp(("parallel",)))(qkv, qkv, qkv)


def dil_bwd_all(qkv, dy, lse, y, exchange=(), *, unroll=8):
    s = qkv.shape[0]
    b = DIL_BLK
    n_blk = s // b
    ne = len(exchange)

    def body(q_ref, k_ref, v_ref, do_ref, l_ref, y_ref, *rest):
        e_ins, (dq_ref, dk_ref, dv_ref), e_outs = rest[:ne], rest[ne:ne + 3], rest[ne + 3:2 * ne + 3]
        comm = (e_ins, e_outs) + tuple(rest[2 * ne + 3:])
        if ne:
            @pl.when(pl.program_id(0) == 0)
            def _():
                _to_chips_start(*comm)

        dq_ref[...] = jnp.zeros_like(dq_ref)
        dk_ref[...] = jnp.zeros_like(dk_ref)
        dv_ref[...] = jnp.zeros_like(dv_ref)
        first_head = _lane((b, LANES)) < 64
        band_prev, band_cur = _dil_bands()
        for _, r in DIL_PATTERNS:
            nb = n_blk // r

            def group(it, carry, r=r, nb=nb):
                loaded = []
                for u in range(unroll):
                    rows_c, rows_p, has_prev = _dil_block(it * unroll + u, r, nb)
                    vals = [q_ref[rows_c, :].astype(BF16), k_ref[rows_p, :].astype(BF16), k_ref[rows_c, :].astype(BF16),
                            v_ref[rows_p, :].astype(BF16), v_ref[rows_c, :].astype(BF16), do_ref[rows_c, :],
                            l_ref[rows_c, :], y_ref[rows_c, :]]
                    loaded.append((rows_c, rows_p, has_prev, vals))
                done = []
                for rows_c, rows_p, has_prev, (qv, kp, kc, vp, vc, dof, lv, yv) in loaded:
                    q2 = _two_heads(qv, first_head)
                    do2 = _two_heads(dof.astype(BF16), first_head)
                    kcat, vcat = _cat(kp, kc), _cat(vp, vc)
                    lse2 = _cat(lv[:, 0:1], lv[:, 64:65])
                    dd2 = jnp.sum(_two_heads(dof * yv, first_head), axis=-1, keepdims=True)
                    p = jnp.exp(jnp.where(band_cur | (band_prev & has_prev), _nt(q2, kcat), NEG) - lse2)
                    ds = (p * (_nt(do2, vcat) - dd2)).astype(BF16)
                    dq2 = _nn(ds, kcat)
                    dkcat = _tn(ds, q2)
                    dvcat = _tn(p.astype(BF16), do2)
                    done.append((rows_c, rows_p, (jnp.where(first_head, dq2[:b], dq2[b:]), dkcat[:b], dkcat[b:],
                                                  dvcat[:b], dvcat[b:])))
                for rows_c, rows_p, (dq, dk_p, dk_c, dv_p, dv_c) in done:
                    dq_ref[rows_c, :] += dq
                    dk_ref[rows_p, :] += dk_p
                    dk_ref[rows_c, :] += dk_c
                    dv_ref[rows_p, :] += dv_p
                    dv_ref[rows_c, :] += dv_c
                return carry

            lax.fori_loop(0, n_blk // unroll, group, 0)

        if ne:
            @pl.when(pl.program_id(0) == 3)
            def _():
                _to_chips_finish(*comm)

    col_blk = lambda k: pl.BlockSpec((s, LANES), lambda hp: (0, 4 * k + hp))
    nat = pl.BlockSpec((s, LANES), lambda hp: (0, hp))
    return pl.pallas_call(
        body, grid=(4,), in_specs=[col_blk(0), col_blk(1), col_blk(2), nat, nat, nat] + [ANY] * ne,
        out_specs=[nat, nat, nat] + [ANY] * ne, out_shape=[SDS((s, ATT_W), F32)] * 3 + _to_chips_shapes(exchange),
        scratch_shapes=_to_chips_sems(ne) if ne else [], name="dil_bwd",
        compiler_params=_cp(("arbitrary",)))(qkv, qkv, qkv, dy, lse, y, *exchange)


def _sigmoid(v):
    return 1.0 / (1.0 + jnp.exp(-v))


def gate_mix(ya, yb, wa, wb, z, *, tm=512, tn=512):
    s = ya.shape[0]
    d = wa.shape[1]
    ga_blk = 3 * ATT_W * 2 // tn
    gb_blk = ga_blk + d // tn

    def body(ya_ref, yb_ref, wa_ref, wb_ref, ga_ref, gb_ref, pa_ref, pb_ref, mx_ref):
        pa = _nn(ya_ref[...], wa_ref[...])
        pb = _nn(yb_ref[...].astype(BF16), wb_ref[...])
        pa_ref[...] = pa.astype(BF16)
        pb_ref[...] = pb.astype(BF16)
        mx_ref[...] = (_sigmoid(ga_ref[...].astype(F32)) * pa + _sigmoid(gb_ref[...].astype(F32)) * pb).astype(BF16)

    out = pl.BlockSpec((tm, tn), lambda i, j: (i, j))
    return pl.pallas_call(
        body, grid=(s // tm, d // tn),
        in_specs=[pl.BlockSpec((tm, ATT_W), lambda i, j: (i, 0)), pl.BlockSpec((tm, ATT_W), lambda i, j: (i, 0)),
                  pl.BlockSpec((ATT_W, tn), lambda i, j: (0, j)), pl.BlockSpec((ATT_W, tn), lambda i, j: (0, j)),
                  pl.BlockSpec((tm, tn), lambda i, j: (i, ga_blk + j)), pl.BlockSpec((tm, tn), lambda i, j: (i, gb_blk + j))],
        out_specs=[out, out, out], out_shape=[SDS((s, d), BF16)] * 3, name="gate_mix",
        compiler_params=_cp(("parallel", "parallel")))(ya, yb, wa, wb, z, z)


def mix_bwd(dy, w_o, z, pa, pb, wo_a, wo_b, ya, *, tm=256):
    s, d = dy.shape

    def body(dy_ref, wo_ref, ga_ref, gb_ref, pa_ref, pb_ref, wa_ref, wb_ref, ya_ref,
             dpa_ref, dpb_ref, dg_ref, dya_ref, dyb_ref, dd_ref):
        dm = _nt(dy_ref[...], wo_ref[...])
        sa = _sigmoid(ga_ref[...].astype(F32))
        sb = _sigmoid(gb_ref[...].astype(F32))
        dpa = (dm * sa).astype(BF16)
        dpb = (dm * sb).astype(BF16)
        dpa_ref[...] = dpa
        dpb_ref[...] = dpb
        dg_ref[:, 0:d] = (dm * pa_ref[...].astype(F32) * sa * (1.0 - sa)).astype(BF16)
        dg_ref[:, d:2 * d] = (dm * pb_ref[...].astype(F32) * sb * (1.0 - sb)).astype(BF16)
        dya = _nt(dpa, wa_ref[...]).astype(BF16)
        dya_ref[...] = dya
        dyb_ref[...] = _nt(dpb, wb_ref[...])
        lane = _lane((tm, LANES))
        for pr in range(ATT_W // LANES):
            pair = slice(pr * LANES, (pr + 1) * LANES)
            prod = dya[:, pair].astype(F32) * ya_ref[:, pair].astype(F32)
            lo = jnp.sum(jnp.where(lane < 64, prod, 0.0), axis=-1, keepdims=True)
            hi = jnp.sum(jnp.where(lane >= 64, prod, 0.0), axis=-1, keepdims=True)
            dd_ref[:, pair] = jnp.where(lane < 64, lo, hi)

    row = pl.BlockSpec((tm, d), lambda i: (i, 0))
    att = pl.BlockSpec((tm, ATT_W), lambda i: (i, 0))
    whole = lambda a: pl.BlockSpec(a.shape, lambda i: (0, 0))
    return pl.pallas_call(
        body, grid=(s // tm,),
        in_specs=[row, whole(w_o), pl.BlockSpec((tm, d), lambda i: (i, 3)), pl.BlockSpec((tm, d), lambda i: (i, 4)), row, row,
                  whole(wo_a), whole(wo_b), att],
        out_specs=[row, row, pl.BlockSpec((tm, 2 * d), lambda i: (i, 0)), att, att, att],
        out_shape=[SDS((s, d), BF16), SDS((s, d), BF16), SDS((s, 2 * d), BF16), SDS((s, ATT_W), BF16),
                   SDS((s, ATT_W), F32), SDS((s, ATT_W), F32)], name="mix_bwd",
        compiler_params=_cp(("parallel",)))(dy, w_o, z, z, pa, pb, wo_a, wo_b, ya)


GELU_C = math.sqrt(2.0 / math.pi)


def _gelu_parts(a):
    a2 = a * a
    th = jnp.tanh(a * (GELU_C + (GELU_C * 0.044715) * a2))
    half = 0.5 * a
    gelu = half + half * th
    dgelu = (0.5 + 0.5 * th) + half * (1.0 - th * th) * (GELU_C + (3.0 * GELU_C * 0.044715) * a2)
    return gelu, dgelu


def _causal_taps(u, before):
    row = _row(u.shape)
    r1 = jnp.where(row == 0, before[7:8, :], pltpu.roll(u, 1, axis=0))
    r2 = jnp.where(row == 0, before[6:7, :], jnp.where(row == 1, before[7:8, :], pltpu.roll(u, 2, axis=0)))
    return r1, r2


def ffn_up(h, wa, wb, cw, cb, *, tm=512, tn=256):
    s, d = h.shape
    f = wa.shape[1]
    nj = f // tn

    def body(h_ref, wa_ref, wb_ref, cwa_ref, cwb_ref, cba_ref, cbb_ref, ua_ref, ub_ref, ca_ref, cbo_ref, m_ref, carry):
        @pl.when(pl.program_id(1) == 0)
        def _():
            carry[...] = jnp.zeros_like(carry)

        conv = []
        for k, (w_ref, cw_ref, cb_ref, u_ref, c_ref) in enumerate(((wa_ref, cwa_ref, cba_ref, ua_ref, ca_ref),
                                                                   (wb_ref, cwb_ref, cbb_ref, ub_ref, cbo_ref))):
            u16 = _nn(h_ref[...], w_ref[...]).astype(BF16)
            u_ref[...] = u16
            u = u16.astype(F32)
            r1, r2 = _causal_taps(u, carry[k])
            carry[k] = u[tm - 8:tm, :]
            c16 = (cw_ref[0:1, :] * r2 + cw_ref[1:2, :] * r1 + cw_ref[2:3, :] * u + cb_ref[...]).astype(BF16)
            c_ref[...] = c16
            conv.append(c16.astype(F32))
        m_ref[...] = (_gelu_parts(conv[0])[0] * conv[1]).astype(BF16)

    out = pl.BlockSpec((tm, tn), lambda j, i: (i, j))
    return pl.pallas_call(
        body, grid=(nj, s // tm),
        in_specs=[pl.BlockSpec((tm, d), lambda j, i: (i, 0)),
                  pl.BlockSpec((d, tn), lambda j, i: (0, j)), pl.BlockSpec((d, tn), lambda j, i: (0, j)),
                  pl.BlockSpec((3, tn), lambda j, i: (0, j)), pl.BlockSpec((3, tn), lambda j, i: (0, nj + j)),
                  pl.BlockSpec((1, tn), lambda j, i: (0, j)), pl.BlockSpec((1, tn), lambda j, i: (0, nj + j))],
        out_specs=[out] * 5, out_shape=[SDS((s, f), BF16)] * 5,
        scratch_shapes=[pltpu.VMEM((2, 8, tn), F32)], name="ffn_up",
        compiler_params=_cp(("parallel", "arbitrary")))(h, wa, wb, cw, cw, cb, cb)


def ffn_bwd(dm, ua, ub, ca, cbo, cw, *, tm=512, tn=256):
    s, f = dm.shape
    nj = f // tn
    ni = s // tm

    def body(dm_ref, ua_ref, ub_ref, ca_ref, cbo_ref, cwa_ref, cwb_ref, dua_ref, dub_ref, ga_ref, gb_ref, carry):
        @pl.when(pl.program_id(1) == 0)
        def _():
            carry[...] = jnp.zeros_like(carry)
            ga_ref[...] = jnp.zeros_like(ga_ref)
            gb_ref[...] = jnp.zeros_like(gb_ref)

        row = _row((tm, tn))
        dmv = dm_ref[...].astype(F32)
        gelu, dgelu = _gelu_parts(ca_ref[...].astype(F32))
        dcs = (dmv * cbo_ref[...].astype(F32) * dgelu, dmv * gelu)
        for k, (dc, u_ref, cw_ref, du_ref, g_ref) in enumerate(((dcs[0], ua_ref, cwa_ref, dua_ref, ga_ref),
                                                                (dcs[1], ub_ref, cwb_ref, dub_ref, gb_ref))):
            u = u_ref[...].astype(F32)
            after = carry[k]
            n1 = jnp.where(row == tm - 1, after[0:1, :], pltpu.roll(dc, tm - 1, axis=0))
            n2 = jnp.where(row == tm - 2, after[0:1, :], jnp.where(row == tm - 1, after[1:2, :], pltpu.roll(dc, tm - 2, axis=0)))
            g_ref[0:1, :] += jnp.sum(n2 * u, axis=0, keepdims=True)
            g_ref[1:2, :] += jnp.sum(n1 * u, axis=0, keepdims=True)
            g_ref[2:3, :] += jnp.sum(dc * u, axis=0, keepdims=True)
            g_ref[3:4, :] += jnp.sum(dc, axis=0, keepdims=True)
            du_ref[...] = (cw_ref[2:3, :] * dc + cw_ref[1:2, :] * n1 + cw_ref[0:1, :] * n2).astype(BF16)
            carry[k] = dc[0:8, :]

    tile = pl.BlockSpec((tm, tn), lambda j, i: (ni - 1 - i, j))
    gspec = pl.BlockSpec((8, tn), lambda j, i: (0, j))
    return pl.pallas_call(
        body, grid=(nj, ni),
        in_specs=[tile] * 5 + [pl.BlockSpec((3, tn), lambda j, i: (0, j)), pl.BlockSpec((3, tn), lambda j, i: (0, nj + j))],
        out_specs=[tile, tile, gspec, gspec],
        out_shape=[SDS((s, f), BF16), SDS((s, f), BF16), SDS((8, f), F32), SDS((8, f), F32)],
        scratch_shapes=[pltpu.VMEM((2, 8, tn), F32)], name="ffn_bwd",
        compiler_params=_cp(("parallel", "arbitrary")))(dm, ua, ub, ca, cbo, cw, cw)


def adamw(w, g, m, v, *, name, tr=None):
    r = w.shape[0]
    rest = w.shape[1:]
    if tr is None:
        tr = r
        for cand in (256, 128, 64, 32, 16, 8):
            if r % cand == 0:
                tr = cand
                break

    def body(w_ref, g_ref, m_ref, v_ref, d_ref, nm_ref, nv_ref):
        gv = g_ref[...]
        mn = ADAM_B1 * m_ref[...] + (1.0 - ADAM_B1) * gv
        vn = ADAM_B2 * v_ref[...] + (1.0 - ADAM_B2) * (gv * gv)
        m_hat = mn / (1.0 - ADAM_B1 ** ADAM_STEP)
        v_hat = vn / (1.0 - ADAM_B2 ** ADAM_STEP)
        d_ref[...] = -ADAM_LR * (m_hat / (jnp.sqrt(v_hat) + ADAM_EPS) + ADAM_WD * w_ref[...])
        nm_ref[...] = mn
        nv_ref[...] = vn

    blk = pl.BlockSpec((tr,) + rest, lambda i: (i,) + (0,) * len(rest))
    return pl.pallas_call(body, grid=(r // tr,), in_specs=[blk] * 4, out_specs=[blk] * 3, out_shape=[SDS(w.shape, F32)] * 3,
                          name=name, compiler_params=_cp(("parallel",)))(w, g, m, v)


def adamw_rows_view(w, g_mine, g_full, m, v, c_arr, *, name, tc=256):
    r, _, c = w.shape
    per_half = c // 2 // tc

    def body(c_ref, w_ref, gm_ref, gf_ref, m_ref, v_ref, d_ref, nm_ref, nv_ref, go_ref):
        mine = (pl.program_id(0) >> (per_half.bit_length() - 1)) == c_ref[0]
        gv = jnp.where(mine, gm_ref[...], gf_ref[...])[:, None, :]
        mn = ADAM_B1 * m_ref[...] + (1.0 - ADAM_B1) * gv
        vn = ADAM_B2 * v_ref[...] + (1.0 - ADAM_B2) * (gv * gv)
        m_hat = mn / (1.0 - ADAM_B1 ** ADAM_STEP)
        v_hat = vn / (1.0 - ADAM_B2 ** ADAM_STEP)
        d_ref[...] = -ADAM_LR * (m_hat / (jnp.sqrt(v_hat) + ADAM_EPS) + ADAM_WD * w_ref[...])
        nm_ref[...] = mn
        nv_ref[...] = vn
        go_ref[...] = gv

    b3 = pl.BlockSpec((r, 1, tc), lambda i, c_ref: (0, 0, i))
    own = pl.BlockSpec((r, tc), lambda i, c_ref: (0, jnp.clip(i - c_ref[0] * per_half, 0, per_half - 1)))
    full = pl.BlockSpec((r, tc), lambda i, c_ref: (0, i))
    grid_spec = pltpu.PrefetchScalarGridSpec(num_scalar_prefetch=1, grid=(c // tc,), in_specs=[b3, own, full, b3, b3],
                                             out_specs=[b3] * 4)
    return pl.pallas_call(body, grid_spec=grid_spec, out_shape=[SDS(w.shape, F32)] * 4, name=name,
                          compiler_params=_cp(("parallel",)))(c_arr, w, g_mine, g_full, m, v)


ANY = pl.BlockSpec(memory_space=pl.ANY)
ICI_KINDS = ("x", "y", "xy")


def _coords():
    return lax.axis_index("x"), lax.axis_index("y"), lax.axis_index("c")


def _peer(kind, x, y, c):
    if kind == "c":
        return (x, y, 1 - c)
    if kind == "x":
        return (1 - x, y, c)
    if kind == "y":
        return (x, 1 - y, c)
    return (1 - x, 1 - y, c)


def _chip_of(p):
    return 2 * p[0] + p[1]


def _half(rows, which):
    h = rows // 2
    return pl.ds(pl.multiple_of(which * h, 16), h)


def _remote(src, dst, send_sem, recv_sem, to):
    return pltpu.make_async_remote_copy(src_ref=src, dst_ref=dst, send_sem=send_sem, recv_sem=recv_sem,
                                        device_id=to, device_id_type=MESH)


def allgather_chips(shards, halved, *, name):
    n = len(shards)

    def body(*refs):
        parts = (refs[:n], refs[n:2 * n], refs[2 * n], refs[2 * n + 1], halved)
        _allgather_start(*parts)
        _allgather_finish(*parts)

    return pl.pallas_call(
        body, in_specs=[ANY] * n, out_specs=[ANY] * n,
        out_shape=_allgather_shapes(shards), scratch_shapes=_allgather_sems(n), name=name)(*shards)


def _allgather_shapes(shards):
    return [SDS((4,) + a.shape, a.dtype) for a in shards]


def _allgather_sems(n):
    return [pltpu.SemaphoreType.DMA((n, 6)), pltpu.SemaphoreType.DMA((n, 6))]


def _allgather_rows(ref, is_halved, which):
    r = ref.shape[0]
    return _half(r, which) if is_halved else pl.ds(0, r)


def _allgather_first(ins, outs, send_sems, recv_sems, halved):
    x, y, c = _coords()
    my_chip = 2 * x + y
    cps = []
    for w in range(len(ins)):
        rows = _allgather_rows(ins[w], halved[w], c)
        for k, kind in enumerate(ICI_KINDS):
            cps.append(_remote(ins[w].at[rows], outs[w].at[my_chip, rows], send_sems.at[w, k], recv_sems.at[w, k],
                               _peer(kind, x, y, c)))
    return cps


def _allgather_start(ins, outs, send_sems, recv_sems, halved):
    for cp in _allgather_first(ins, outs, send_sems, recv_sems, halved):
        cp.start()


def _allgather_finish(ins, outs, send_sems, recv_sems, halved):
    x, y, c = _coords()
    me = (x, y, c)
    second = []
    for w in range(len(ins)):
        for k, kind in enumerate(ICI_KINDS):
            landed = outs[w].at[_chip_of(_peer(kind, x, y, c)), _allgather_rows(ins[w], halved[w], c)]
            _remote(landed, landed, send_sems.at[w, k], recv_sems.at[w, k], me).wait_recv()
            if halved[w]:
                cp = _remote(landed, landed, send_sems.at[w, 3 + k], recv_sems.at[w, 3 + k], _peer("c", x, y, c))
                cp.start()
                second.append(cp)
    for w in range(len(ins)):
        if halved[w]:
            for k, kind in enumerate(ICI_KINDS):
                other = outs[w].at[_chip_of(_peer(kind, x, y, c)), _allgather_rows(ins[w], True, 1 - c)]
                _remote(other, other, send_sems.at[w, 3 + k], recv_sems.at[w, 3 + k], me).wait_recv()
    for cp in _allgather_first(ins, outs, send_sems, recv_sems, halved) + second:
        cp.wait_send()


def _half_of(ref, by_cols, which):
    lead = (slice(None),) * (len(ref.shape) - 2)
    if by_cols:
        h = ref.shape[-1] // 2
        return ref.at[lead + (slice(None), pl.ds(pl.multiple_of(which * h, LANES), h))]
    return ref.at[lead + (_half(ref.shape[-2], which),)]


def _half_shape(shape, by_cols):
    return shape[:-1] + (shape[-1] // 2,) if by_cols else shape[:-2] + (shape[-2] // 2, shape[-1])


def grads_to_sibling(gs, by_cols, *, name):
    n = len(gs)

    def body(*refs):
        ins, outs = refs[:n], refs[n:2 * n]
        send_sems, recv_sems = refs[2 * n:]
        x, y, c = _coords()
        cps = []
        for w in range(n):
            cp = _remote(_half_of(ins[w], by_cols[w], 1 - c), outs[w], send_sems.at[w], recv_sems.at[w], _peer("c", x, y, c))
            cp.start()
            cps.append(cp)
        for cp in cps:
            cp.wait()

    return pl.pallas_call(
        body, in_specs=[ANY] * n, out_specs=[ANY] * n,
        out_shape=[SDS(_half_shape(a.shape, bc), a.dtype) for a, bc in zip(gs, by_cols)],
        scratch_shapes=[pltpu.SemaphoreType.DMA((n,)), pltpu.SemaphoreType.DMA((n,))], name=name)(*gs)


def grads_to_chips(ps, *, name):
    n = len(ps)

    def body(*refs):
        parts = (refs[:n], refs[n:2 * n], refs[2 * n], refs[2 * n + 1])
        _to_chips_start(*parts)
        _to_chips_finish(*parts)

    return pl.pallas_call(
        body, in_specs=[ANY] * n, out_specs=[ANY] * n,
        out_shape=_to_chips_shapes(ps), scratch_shapes=_to_chips_sems(n), name=name)(*ps)


def _to_chips_shapes(ps):
    return [SDS((3,) + a.shape[1:], a.dtype) for a in ps]


def _to_chips_sems(n):
    return [pltpu.SemaphoreType.DMA((n, 3)), pltpu.SemaphoreType.DMA((n, 3))]


def _to_chips_copies(ins, outs, send_sems, recv_sems):
    x, y, c = _coords()
    cps = []
    for w in range(len(ins)):
        for k, kind in enumerate(ICI_KINDS):
            to = _peer(kind, x, y, c)
            cps.append(_remote(ins[w].at[_chip_of(to)], outs[w].at[k], send_sems.at[w, k], recv_sems.at[w, k], to))
    return cps


def _to_chips_start(ins, outs, send_sems, recv_sems):
    for cp in _to_chips_copies(ins, outs, send_sems, recv_sems):
        cp.start()


def _to_chips_finish(ins, outs, send_sems, recv_sems):
    for cp in _to_chips_copies(ins, outs, send_sems, recv_sems):
        cp.wait()


def halves_to_full(hs, by_cols, *, name):
    n = len(hs)

    def body(*refs):
        ins, outs = refs[:n], refs[n:2 * n]
        send_sems, recv_sems = refs[2 * n:]
        x, y, c = _coords()
        cps = []
        for w in range(n):
            cp = _remote(ins[w], _half_of(outs[w], by_cols[w], c), send_sems.at[w], recv_sems.at[w], _peer("c", x, y, c))
            cp.start()
            cps.append(cp)
        for cp in cps:
            cp.wait()

    return pl.pallas_call(
        body, in_specs=[ANY] * n, out_specs=[ANY] * n,
        out_shape=[SDS((a.shape[0], 2 * a.shape[1]) if bc else (2 * a.shape[0], a.shape[1]), a.dtype)
                   for a, bc in zip(hs, by_cols)],
        scratch_shapes=[pltpu.SemaphoreType.DMA((n,)), pltpu.SemaphoreType.DMA((n,))],
        name=name)(*hs)


def _row_tile(rows):
    for cand in (256, 192, 176, 128, 64, 32, 16):
        if rows % cand == 0:
            return cand
    return rows


def chip_sum(g, recv, c_arr, by_cols, *, name):
    _, r, cols = g.shape

    def body(c_ref, g_ref, r_ref, f_ref, b_ref):
        tot = g_ref[...] + r_ref[...]
        f_ref[...] = tot
        b_ref[...] = tot.astype(BF16)

    if by_cols:
        tc = 2 * LANES
        nblk = cols // 2 // tc
        shape = (4, r, cols // 2)
        blk = pl.BlockSpec((None, r, tc), lambda j, i, c_ref: (j, 0, i))
        mine = pl.BlockSpec((None, r, tc), lambda j, i, c_ref: (j, 0, c_ref[0] * nblk + i))
    else:
        tr = _row_tile(r // 2)
        nblk = r // 2 // tr
        shape = (4, r // 2, cols)
        blk = pl.BlockSpec((None, tr, cols), lambda j, i, c_ref: (j, i, 0))
        mine = pl.BlockSpec((None, tr, cols), lambda j, i, c_ref: (j, c_ref[0] * nblk + i, 0))
    grid_spec = pltpu.PrefetchScalarGridSpec(num_scalar_prefetch=1, grid=(4, nblk), in_specs=[mine, blk], out_specs=[blk, blk])
    return pl.pallas_call(body, grid_spec=grid_spec, out_shape=[SDS(shape, F32), SDS(shape, BF16)],
                          name=name, compiler_params=_cp(("parallel", "parallel")))(c_arr, g, recv)


def final_sum(pf, recv, chip_arr, *, name):
    _, h, cols = pf.shape
    tr = _row_tile(h)

    def body(chip_ref, p_ref, r_ref, o_ref):
        o_ref[...] = ((p_ref[...] + r_ref[0].astype(F32)) + r_ref[1].astype(F32)) + r_ref[2].astype(F32)

    grid_spec = pltpu.PrefetchScalarGridSpec(
        num_scalar_prefetch=1, grid=(h // tr,),
        in_specs=[pl.BlockSpec((None, tr, cols), lambda i, chip_ref: (chip_ref[0], i, 0)),
                  pl.BlockSpec((3, tr, cols), lambda i, chip_ref: (0, i, 0))],
        out_specs=pl.BlockSpec((tr, cols), lambda i, chip_ref: (i, 0)))
    return pl.pallas_call(body, grid_spec=grid_spec, out_shape=SDS((h, cols), F32), name=name,
                          compiler_params=_cp(("parallel",)))(chip_arr, pf, recv)


def allreduce_small(v, *, name):
    rws, cols = v.shape

    def body(v_ref, all_ref, sum_ref, send_sems, recv_sems, local_sem):
        x, y, c = _coords()
        me, sibling = (x, y, c), (x, y, 1 - c)
        chips = [(1 - x, y), (x, 1 - y), (1 - x, 1 - y)]

        def rows(px, py, pc):
            return all_ref.at[pl.ds(pl.multiple_of((4 * px + 2 * py + pc) * rws, 8), rws), :]

        def copy(k, block, to, src=None):
            return _remote(rows(*block) if src is None else src, rows(*block), send_sems.at[k], recv_sems.at[k], to)

        mine = pltpu.make_async_copy(v_ref, rows(*me), local_sem)
        mine.start()
        first = [copy(0, me, sibling, src=v_ref)]
        first += [copy(1 + j, me, (*chip, c), src=v_ref) for j, chip in enumerate(chips)]
        for cp in first:
            cp.start()
        passed = [copy(4 + j, (*chip, c), sibling) for j, chip in enumerate(chips)]
        for j, chip in enumerate(chips):
            copy(1 + j, (*chip, c), me).wait_recv()
            passed[j].start()
        copy(0, sibling, me).wait_recv()
        for j, chip in enumerate(chips):
            copy(4 + j, (*chip, 1 - c), me).wait_recv()
        for cp in first + passed:
            cp.wait_send()
        mine.wait()
        tot = all_ref[0:rws, :]
        for dev in range(1, 8):
            tot = tot + all_ref[dev * rws:(dev + 1) * rws, :]
        sum_ref[...] = tot

    vm = pl.BlockSpec(memory_space=pltpu.VMEM)
    return pl.pallas_call(
        body, in_specs=[vm], out_specs=[vm, vm],
        out_shape=[SDS((8 * rws, cols), v.dtype), SDS((rws, cols), v.dtype)],
        scratch_shapes=[pltpu.SemaphoreType.DMA((7,)), pltpu.SemaphoreType.DMA((7,)), pltpu.SemaphoreType.DMA],
        name=name)(v)[1]


def _pack_rows(parts, rows):
    out = []
    for a, r in zip(parts, rows):
        flat = a.reshape(-1)
        flat = jnp.pad(flat, (0, r * LANES - flat.shape[0]))
        out.append(flat.reshape(r, LANES))
    return jnp.concatenate(out, axis=0)


def _unpack_rows(packed, shapes, rows):
    out, at = [], 0
    for shp, r in zip(shapes, rows):
        size = int(np.prod(shp))
        out.append(packed[at:at + r].reshape(-1)[:size].reshape(shp))
        at += r
    return out


def kernel(x, g_pre_mix, w_in, b_forget, w_o_fox, w_o_dil, w_out, g_post_mix, g_pre_ffn, w_up, conv_w, conv_b, w_down, g_post_ffn, loss_target, m_g_pre_mix, m_w_in, m_b_forget, m_w_o_fox, m_w_o_dil, m_w_out, m_g_post_mix, m_g_pre_ffn, m_w_up, m_conv_w, m_conv_b, m_w_down, m_g_post_ffn, v_g_pre_mix, v_w_in, v_b_forget, v_w_o_fox, v_w_o_dil, v_w_out, v_g_post_mix, v_g_pre_ffn, v_w_up, v_conv_w, v_conv_b, v_w_down, v_g_post_ffn):
    xi, yi, ci = _coords()
    chip = 2 * xi + yi
    c_arr = jnp.reshape(ci, (1,)).astype(jnp.int32)
    chip_arr = jnp.reshape(chip, (1,)).astype(jnp.int32)
    xs = x[0]
    target = loss_target[0]
    s, d = xs.shape
    f_half = w_down.shape[1] * 4
    cols_in = w_in.shape[2]

    big = (w_in, w_o_fox, w_o_dil, w_out, w_up, w_down)
    shards = [w[0].astype(BF16) for w in big]
    a_in, a_cw = allgather_chips([shards[0], conv_w[0]], [True, False], name="allgather_w_in")
    w_in_full = jnp.concatenate([jnp.where(chip == j, shards[0], a_in[j]) for j in range(4)], axis=1)
    cw = jnp.concatenate([jnp.where(chip == j, conv_w[0], a_cw[j]) for j in range(4)], axis=1)
    nf = N_HEADS
    e_a, e_b = 3 * ATT_W, 3 * ATT_W + nf
    wz = jnp.concatenate([w_in_full[:, :e_a], w_in_full[:, e_b:]], axis=1)
    wf = jnp.pad(w_in_full[:, e_a:e_b], ((0, 0), (0, LANES - nf)))
    cb = conv_b
    bfo = jnp.pad(b_forget, ((0, 0), (0, LANES - nf)))

    h1 = rmsnorm_fwd(xs, g_pre_mix)
    z = mm([(h1, d, 0)], [(wz, d, 0)], nt=False, out_dtype=BF16, tm=1024, tn=512, name="in_proj")
    fa = mm([(h1, d, 0)], [(wf, d, 0)], nt=False, out_dtype=F32, tm=1024, tn=LANES, name="in_proj_forget")
    q_aug, k_aug, v_aug = fox_prep(z, fa, bfo)
    ya, lse_a, *late = fox_fwd(q_aug, k_aug, v_aug, gather=shards[1:], hps=N_HEADS)
    a_of, a_od, a_out, a_up, a_down = [
        lax.dynamic_update_index_in_dim(a4, own, chip, 0) for a4, own in zip(late, shards[1:])]
    wo_a = jnp.concatenate([a_of[j] for j in range(4)], axis=1)
    wo_b = jnp.concatenate([a_od[j] for j in range(4)], axis=1)
    w_o = a_out.reshape(d, d)
    w_dn = a_down.reshape(f_half, d)
    wu_a = jnp.concatenate([a_up[0], a_up[1]], axis=1)
    wu_b = jnp.concatenate([a_up[2], a_up[3]], axis=1)
    qkv_b = rope_apply([(z, Z_QB, QK_SCALE, True), (z, Z_KB, 1.0, True), (z, Z_VB, 1.0, False)], rope_tables(s, 1.0),
                       out_dtype=F32, name="rope_fwd")
    yb, lse_b = dil_fwd_all(qkv_b)
    pa, pb, mixed = gate_mix(ya, yb, wo_a, wo_b, z)
    y1, x1, h2 = proj_norm_res(mixed, w_o, g_post_mix, xs, g_pre_ffn, name="out_proj")
    ua, ub, conv_a, conv_bh, mid = ffn_up(h2, wu_a, wu_b, cw, cb)
    dout, dy2, gg_post_ffn, sq = proj_norm_loss(mid, w_dn, g_post_ffn, x1, target, name="down_proj")
    loss = lax.psum(0.5 * sq[0, 0] / d, ("x", "y", "c"))

    dmid = mm([(dy2, d, 0)], [(w_dn, d, 0)], nt=True, out_dtype=BF16, tm=512, tn=f_half // 2, name="down_dgrad")
    dw_down = wgrad((mid, f_half, 0), dy2, tk=f_half // 2, tn=1024, ts=1024, name="down_wgrad")
    dua, dub, gc_a, gc_b = ffn_bwd(dmid, ua, ub, conv_a, conv_bh, cw)
    dx1, dy1, gg_pre_ffn, gg_post_mix = mm_norm_bwd(
        [(dua, f_half, 0), (dub, f_half, 0)], [(wu_a, f_half, 0), (wu_b, f_half, 0)],
        [(x1, g_pre_ffn, dout, F32), (y1, g_post_mix, None, BF16)], name="up_dgrad")
    dw_up = None
    for k, du in enumerate((dua, dub)):
        dw_up = wgrad((h2, d, 0), du, tk=1024, tn=f_half // 2, ts=1024, name=f"up_wgrad_{k}", chip_major=True,
                      slabs=(4, 2 * k), into=dw_up)
    def to_chip_sums(gs, nms, tag, by_cols=False):
        from_sib = grads_to_sibling(gs, [by_cols] * len(gs), name=f"grads_to_sibling_{tag}")
        return [chip_sum(g, r, c_arr, by_cols, name=f"chip_sum_{nm}") for g, r, nm in zip(gs, from_sib, nms)]

    sums_ffn = to_chip_sums([dw_up, dw_down.reshape(4, f_half // 4, d)], ("w_up", "w_down"), "ffn")
    dw_out = wgrad((mixed, d, 0), dy1, tk=1024, tn=1024, ts=1024, name="out_wgrad")
    dpa, dpb, dz_g, dya, dyb, dd_a = mix_bwd(dy1, w_o, z, pa, pb, wo_a, wo_b, ya)
    by_chip_cols = lambda a: jnp.stack([a[:, j * (d // 4):(j + 1) * (d // 4)] for j in range(4)], axis=0)
    dw_of = by_chip_cols(wgrad((ya, ATT_W, 0), dpa, tk=ATT_W, tn=d, ts=1024, name="fox_o_wgrad"))
    dw_od = by_chip_cols(wgrad((yb, ATT_W, 0), dpb, tk=ATT_W, tn=d, ts=1024, name="dil_o_wgrad"))
    sums_mix = to_chip_sums([dw_of, dw_od, dw_out.reshape(4, d // 4, d)], ("w_o_fox", "w_o_dil", "w_out"), "mix")
    dq_aug, dk_aug, dv_a, *got_ffn = fox_bwd(q_aug, k_aug, z, dya, lse_a, dd_a, exchange=[p[1] for p in sums_ffn])
    dz_a, dfa, gg_bf = fox_post(dq_aug, dk_aug, dv_a, fa, bfo)
    dq_b, dk_b, dv_b, *got_mix = dil_bwd_all(qkv_b, dyb, lse_b, yb, exchange=[p[1] for p in sums_mix])
    dz_b = rope_apply([(dq_b, 0, QK_SCALE, True), (dk_b, 0, 1.0, True), (dv_b, 0, 1.0, False)],
                      rope_tables(s, -1.0), out_dtype=BF16, name="rope_bwd")
    dwt_a = wgrad((dz_a, e_a, 0), h1, tk=e_a // 2, tn=d, ts=1024, name="in_wgrad_a")
    dwt_b = wgrad((dz_b, e_a, 0), h1, tk=e_a // 2, tn=d, ts=1024, name="in_wgrad_b")
    dwt_g = wgrad((dz_g, 2 * d, 0), h1, tk=d, tn=d, ts=1024, name="in_wgrad_g")
    dwt_f = wgrad((dfa, LANES, 0), h1, tk=LANES, tn=d, ts=1024, name="in_wgrad_f")
    dwt_full = jnp.concatenate([dwt_a, dwt_f[:nf], dwt_b, dwt_g], axis=0)
    dw_in = jnp.stack([dwt_full[j * cols_in:(j + 1) * cols_in] for j in range(4)], axis=0)
    sums_in = to_chip_sums([dw_in], ("w_in",), "in", by_cols=True)
    grad_x, gg_pre_mix, *got_in = mm_norm_bwd(
        [(dz_a, e_a, 0), (dz_b, e_a, 0), (dz_g, d, 0), (dz_g, d, 1), (dfa, LANES, 0)],
        [(wz, e_a, 0), (wz, e_a, 1), (wz, d, 3), (wz, d, 4), (wf, LANES, 0)],
        [(xs, g_pre_mix, dx1, F32)], exchange=[sums_in[0][1]], name="in_dgrad")

    names = ("w_in", "w_o_fox", "w_o_dil", "w_out", "w_up", "w_down")
    sums = sums_in + sums_mix + sums_ffn
    from_chips = list(got_in) + list(got_mix) + list(got_ffn)
    halves = [final_sum(p[0], r, chip_arr, name=f"final_sum_{nm}") for p, r, nm in zip(sums, from_chips, names)]
    from_half = halves_to_full(halves, [True] + [False] * 5, name="halves_to_full")
    g_big = [None] + [lax.dynamic_update_slice_in_dim(full, mine, ci * mine.shape[0], axis=0)
                      for full, mine in zip(from_half[1:], halves[1:])]
    upd_big = [adamw(w[0], g, m[0], v[0], name=f"adamw_{nm}") for w, g, m, v, nm in list(zip(
        big, g_big, (m_w_in, m_w_o_fox, m_w_o_dil, m_w_out, m_w_up, m_w_down),
        (v_w_in, v_w_o_fox, v_w_o_dil, v_w_out, v_w_up, v_w_down), names))[1:]]
    to_t = lambda a: jnp.transpose(a, (2, 0, 1))
    from_t = lambda a: jnp.transpose(a, (1, 2, 0))
    *upd_in, g_in_t = adamw_rows_view(to_t(w_in), halves[0], from_half[0], to_t(m_w_in), to_t(v_w_in), c_arr,
                                      name="adamw_w_in")

    g_cw_loc = jnp.concatenate([gc_a[0:3], gc_b[0:3]], axis=1)
    g_cb_loc = jnp.concatenate([gc_a[3:4], gc_b[3:4]], axis=1)
    small_loc = [gg_pre_mix, gg_post_mix, gg_pre_ffn, gg_post_ffn, g_cb_loc, gg_bf[:, :nf], g_cw_loc]
    red_rows = (8, 8, 8, 8, 48, 8, 136)
    red = allreduce_small(_pack_rows(small_loc, red_rows), name="allreduce_small")
    g_pm, g_qm, g_pf, g_qf, g_cb, g_bf, g_cw_full = _unpack_rows(red, [a.shape for a in small_loc], red_rows)
    cols_cw = conv_w.shape[2]
    g_cw = lax.dynamic_slice_in_dim(g_cw_full, chip * cols_cw, cols_cw, axis=1)
    small_w = (g_pre_mix, g_post_mix, g_pre_ffn, g_post_ffn, conv_b, b_forget, conv_w[0])
    small_m = (m_g_pre_mix, m_g_post_mix, m_g_pre_ffn, m_g_post_ffn, m_conv_b, m_b_forget, m_conv_w[0])
    small_v = (v_g_pre_mix, v_g_post_mix, v_g_pre_ffn, v_g_post_ffn, v_conv_b, v_b_forget, v_conv_w[0])
    small_g = (g_pm, g_qm, g_pf, g_qf, g_cb, g_bf, g_cw)
    ad_rows = (8, 8, 8, 8, 48, 8, 40)
    packed = [_pack_rows(t, ad_rows) for t in (small_w, small_g, small_m, small_v)]
    upd_small = [_unpack_rows(o, [a.shape for a in small_w], ad_rows) for o in adamw(*packed, name="adamw_small")]

    order = ("g_pre_mix", "w_in", "b_forget", "w_o_fox", "w_o_dil", "w_out", "g_post_mix", "g_pre_ffn", "w_up", "conv_w",
             "conv_b", "w_down", "g_post_ffn")
    small_names = ("g_pre_mix", "g_post_mix", "g_pre_ffn", "g_post_ffn", "conv_b", "b_forget", "conv_w")
    grads, deltas, new_ms, new_vs = {}, {}, {}, {}
    grads["w_in"] = from_t(g_in_t)
    deltas["w_in"], new_ms["w_in"], new_vs["w_in"] = (from_t(a) for a in upd_in)
    for k, nm in enumerate(names[1:]):
        grads[nm] = g_big[k + 1][None]
        deltas[nm], new_ms[nm], new_vs[nm] = (a[None] for a in upd_big[k])
    for k, nm in enumerate(small_names):
        lead = (lambda a: a[None]) if nm == "conv_w" else (lambda a: a)
        grads[nm] = lead(small_g[k])
        deltas[nm], new_ms[nm], new_vs[nm] = (lead(upd_small[j][k]) for j in range(3))
    return (loss, grad_x[None], *[grads[nm] for nm in order], *[deltas[nm] for nm in order],
            *[new_ms[nm] for nm in order], *[new_vs[nm] for nm in order])
```

```python
import functools
import math

import numpy as np
import jax
import jax.numpy as jnp
from jax import lax
from jax.experimental import pallas as pl
from jax.experimental.pallas import tpu as pltpu

F32 = jnp.float32
BF16 = jnp.bfloat16
SDS = jax.ShapeDtypeStruct
MESH = pl.DeviceIdType.MESH

HEAD_DIM = 64
N_HEADS = 8
LANES = 128
ATT_W = N_HEADS * HEAD_DIM
DIL_PATTERNS = ((128, 1), (512, 4), (2048, 16))
DIL_BLK = 128
ROPE_DIM = HEAD_DIM // 4
ROPE_THETA = 500000.0
RMS_EPS = 1e-6
NEG = -1e30
QK_SCALE = 1.0 / math.sqrt(HEAD_DIM)
ADAM_LR, ADAM_B1, ADAM_B2, ADAM_EPS, ADAM_WD, ADAM_STEP = 0.001, 0.9, 0.999, 1e-08, 0.01, 10
VMEM_LIMIT = 56 * 1024 * 1024

Z_QA, Z_KA, Z_VA, Z_QB, Z_KB, Z_VB = 0, 1, 2, 3, 4, 5
Z_W = 5120


def _cp(sem):
    return pltpu.CompilerParams(dimension_semantics=sem, vmem_limit_bytes=VMEM_LIMIT)


def _nt(a, b):
    return lax.dot_general(a, b, (((1,), (1,)), ((), ())), preferred_element_type=F32)


def _tn(a, b):
    return lax.dot_general(a, b, (((0,), (0,)), ((), ())), preferred_element_type=F32)


def _nn(a, b):
    return jnp.dot(a, b, preferred_element_type=F32)


def _lane(shape):
    return lax.broadcasted_iota(jnp.int32, shape, 1)


def _row(shape):
    return lax.broadcasted_iota(jnp.int32, shape, 0)


def rmsnorm_fwd(x, g, *, tm=512):
    s, d = x.shape

    def body(x_ref, g_ref, h_ref):
        xv = x_ref[...]
        inv = lax.rsqrt(jnp.mean(xv * xv, axis=-1, keepdims=True) + RMS_EPS)
        h_ref[...] = (xv * inv * g_ref[...]).astype(h_ref.dtype)

    return pl.pallas_call(
        body, grid=(s // tm,),
        in_specs=[pl.BlockSpec((tm, d), lambda i: (i, 0)), pl.BlockSpec((1, d), lambda i: (0, 0))],
        out_specs=pl.BlockSpec((tm, d), lambda i: (i, 0)),
        out_shape=SDS((s, d), BF16), name="rmsnorm_fwd", compiler_params=_cp(("parallel",)))(x, g)


def rmsnorm_bwd(dh, x, g, res, *, out_dtype, tm=256, name):
    s, d = x.shape
    n = s // tm
    has_res = res is not None

    def body(*refs):
        if has_res:
            dh_ref, x_ref, g_ref, res_ref, dx_ref, dg_ref, acc = refs
        else:
            dh_ref, x_ref, g_ref, dx_ref, dg_ref, acc = refs
        i = pl.program_id(0)

        @pl.when(i == 0)
        def _():
            acc[...] = jnp.zeros_like(acc)

        xv = x_ref[...]
        inv = lax.rsqrt(jnp.mean(xv * xv, axis=-1, keepdims=True) + RMS_EPS)
        xh = xv * inv
        dhv = dh_ref[...].astype(F32)
        dxh = dhv * g_ref[...]
        dot = jnp.mean(dxh * xh, axis=-1, keepdims=True)
        dx = inv * (dxh - xh * dot)
        if has_res:
            dx = dx + res_ref[...]
        dx_ref[...] = dx.astype(dx_ref.dtype)
        acc[...] += jnp.sum((dhv * xh).reshape(tm // 8, 8, d), axis=0)

        @pl.when(i == n - 1)
        def _():
            dg_ref[...] = jnp.sum(acc[...], axis=0, keepdims=True)

    row = pl.BlockSpec((tm, d), lambda i: (i, 0))
    in_specs = [row, row, pl.BlockSpec((1, d), lambda i: (0, 0))] + ([row] if has_res else [])
    args = [dh, x, g] + ([res] if has_res else [])
    return pl.pallas_call(
        body, grid=(n,), in_specs=in_specs,
        out_specs=[row, pl.BlockSpec((1, d), lambda i: (0, 0))],
        out_shape=[SDS((s, d), out_dtype), SDS((1, d), F32)],
        scratch_shapes=[pltpu.VMEM((8, d), F32)],
        name=name, compiler_params=_cp(("arbitrary",)))(*args)


def mm(a_views, b_views, *, nt, out_dtype, tm, tn, name):
    n_p = len(a_views)
    m = a_views[0][0].shape[0]
    n = b_views[0][0].shape[0] if nt else b_views[0][0].shape[1]

    def body(*refs):
        o_ref = refs[-1]
        acc = None
        for p in range(n_p):
            av = refs[p][...].astype(BF16)
            bv = refs[n_p + p][...].astype(BF16)
            dv = _nt(av, bv) if nt else _nn(av, bv)
            acc = dv if acc is None else acc + dv
        o_ref[...] = acc.astype(o_ref.dtype)

    in_specs = []
    for arr, w, blk in a_views:
        in_specs.append(pl.BlockSpec((tm, w), functools.partial(lambda i, j, blk: (i, blk), blk=blk)))
    for arr, w, blk in b_views:
        if nt:
            in_specs.append(pl.BlockSpec((tn, w), functools.partial(lambda i, j, blk: (j, blk), blk=blk)))
        else:
            in_specs.append(pl.BlockSpec((w, tn), lambda i, j: (0, j)))
    return pl.pallas_call(
        body, grid=(m // tm, n // tn), in_specs=in_specs,
        out_specs=pl.BlockSpec((tm, tn), lambda i, j: (i, j)),
        out_shape=SDS((m, n), out_dtype), name=name,
        compiler_params=_cp(("parallel", "parallel")))(*[a[0] for a in a_views], *[b[0] for b in b_views])


def wgrad(a_view, g, *, tk, tn, ts, name, chip_major=False, slabs=None, into=None, bf16_copy=False):
    arr, ka, blk = a_view
    s, n = g.shape
    ns = s // ts
    total, first = slabs if slabs else (n // tn, 0)
    n_into = 0 if into is None else (2 if bf16_copy else 1)

    def body(a_ref, g_ref, *rest):
        o_ref = rest[n_into]

        @pl.when(pl.program_id(2) == 0)
        def _():
            o_ref[...] = jnp.zeros_like(o_ref)

        o_ref[...] += _tn(a_ref[...].astype(BF16), g_ref[...].astype(BF16))
        if bf16_copy:
            @pl.when(pl.program_id(2) == ns - 1)
            def _():
                rest[n_into + 1][...] = o_ref[...].astype(BF16)

    if chip_major:
        out_spec = pl.BlockSpec((None, tk, tn), lambda i, j, k: (first + j, i, 0))
        shape = (total, ka, tn)
    else:
        out_spec = pl.BlockSpec((tk, tn), lambda i, j, k: (i, j))
        shape = (ka, n)
    in_specs = [pl.BlockSpec((ts, tk), lambda i, j, k: (k, blk * (ka // tk) + i)),
                pl.BlockSpec((ts, tn), lambda i, j, k: (k, j))]
    args = [arr, g]
    if into is not None:
        earlier = list(into) if bf16_copy else [into]
        in_specs += [pl.BlockSpec(memory_space=pl.ANY)] * len(earlier)
        args += earlier
    out = pl.pallas_call(
        body, grid=(ka // tk, n // tn, ns), in_specs=in_specs,
        out_specs=[out_spec, out_spec] if bf16_copy else out_spec,
        out_shape=[SDS(shape, F32), SDS(shape, BF16)] if bf16_copy else SDS(shape, F32), name=name,
        input_output_aliases={2 + k: k for k in range(n_into)},
        compiler_params=_cp(("parallel", "parallel", "arbitrary")))(*args)
    return out


def _norm_bwd_rows(dh, xh, inv, g):
    dxh = dh * g
    dx = inv * (dxh - xh * jnp.mean(dxh * xh, axis=-1, keepdims=True))
    return dx, jnp.sum((dh * xh).reshape(dh.shape[0] // 8, 8, dh.shape[1]), axis=0)


def proj_norm_res(a, w, g, xres, g_next, *, tm=512, name):
    s, k = a.shape
    d = w.shape[1]

    def body(a_ref, w_ref, g_ref, x_ref, gn_ref, y_ref, o_ref, h_ref):
        y = _nn(a_ref[...], w_ref[...])
        inv = lax.rsqrt(jnp.mean(y * y, axis=-1, keepdims=True) + RMS_EPS)
        xn = x_ref[...] + y * inv * g_ref[...]
        y_ref[...] = y
        o_ref[...] = xn
        inv_n = lax.rsqrt(jnp.mean(xn * xn, axis=-1, keepdims=True) + RMS_EPS)
        h_ref[...] = (xn * inv_n * gn_ref[...]).astype(h_ref.dtype)

    row = pl.BlockSpec((tm, d), lambda i: (i, 0))
    vec = pl.BlockSpec((1, d), lambda i: (0, 0))
    return pl.pallas_call(
        body, grid=(s // tm,),
        in_specs=[pl.BlockSpec((tm, k), lambda i: (i, 0)), pl.BlockSpec((k, d), lambda i: (0, 0)), vec, row, vec],
        out_specs=[row, row, row], out_shape=[SDS((s, d), F32), SDS((s, d), F32), SDS((s, d), BF16)], name=name,
        compiler_params=_cp(("parallel",)))(a, w, g, xres, g_next)


def proj_norm_loss(a, w, g, xres, target, *, tm=512, name):
    s, k = a.shape
    d = w.shape[1]
    n = s // tm

    def body(a_ref, w_ref, g_ref, x_ref, t_ref, do_ref, dy_ref, dg_ref, l_ref, acc):
        i = pl.program_id(0)

        @pl.when(i == 0)
        def _():
            acc[...] = jnp.zeros_like(acc)
            l_ref[...] = jnp.zeros_like(l_ref)

        y = _nn(a_ref[...], w_ref[...])
        inv = lax.rsqrt(jnp.mean(y * y, axis=-1, keepdims=True) + RMS_EPS)
        yh = y * inv
        err = x_ref[...] + yh * g_ref[...] - t_ref[...]
        dout = err * (1.0 / d)
        do_ref[...] = dout
        l_ref[...] += jnp.sum(jnp.sum(err * err, axis=1, keepdims=True), axis=0, keepdims=True)
        dy, part = _norm_bwd_rows(dout, yh, inv, g_ref[...])
        dy_ref[...] = dy.astype(dy_ref.dtype)
        acc[...] += part

        @pl.when(i == n - 1)
        def _():
            dg_ref[...] = jnp.sum(acc[...], axis=0, keepdims=True)

    row = pl.BlockSpec((tm, d), lambda i: (i, 0))
    vec = pl.BlockSpec((1, d), lambda i: (0, 0))
    return pl.pallas_call(
        body, grid=(n,),
        in_specs=[pl.BlockSpec((tm, k), lambda i: (i, 0)), pl.BlockSpec((k, d), lambda i: (0, 0)), vec, row, row],
        out_specs=[row, row, vec, pl.BlockSpec((1, 1), lambda i: (0, 0))],
        out_shape=[SDS((s, d), F32), SDS((s, d), BF16), SDS((1, d), F32), SDS((1, 1), F32)],
        scratch_shapes=[pltpu.VMEM((8, d), F32)], name=name, compiler_params=_cp(("arbitrary",)))(a, w, g, xres, target)


def mm_norm_bwd(a_views, b_views, stages, exchange=(), *, tm=256, name):
    n_p, n_s, ne = len(a_views), len(stages), len(exchange)
    s = a_views[0][0].shape[0]
    d = b_views[0][0].shape[0]
    n = s // tm
    has_res = [st[2] is not None for st in stages]

    def body(*refs):
        a_refs, b_refs = refs[:n_p], refs[n_p:2 * n_p]
        at = 2 * n_p
        st_refs = []
        for k in range(n_s):
            cnt = 3 if has_res[k] else 2
            st_refs.append(refs[at:at + cnt])
            at += cnt
        e_ins = refs[at:at + ne]
        at += ne
        dx_refs, dg_refs = refs[at:at + n_s], refs[at + n_s:at + 2 * n_s]
        at += 2 * n_s
        e_outs = refs[at:at + ne]
        at += ne
        accs = refs[at:at + n_s]
        comm = (e_ins, e_outs) + tuple(refs[at + n_s:])
        i = pl.program_id(0)

        @pl.when(i == 0)
        def _():
            for acc in accs:
                acc[...] = jnp.zeros_like(acc)
            if ne:
                _to_chips_start(*comm)

        dh = None
        for p in range(n_p):
            part = _nt(a_refs[p][...].astype(BF16), b_refs[p][...].astype(BF16))
            dh = part if dh is None else dh + part
        for k in range(n_s):
            xv = st_refs[k][0][...]
            inv = lax.rsqrt(jnp.mean(xv * xv, axis=-1, keepdims=True) + RMS_EPS)
            dx, part = _norm_bwd_rows(dh, xv * inv, inv, st_refs[k][1][...])
            if has_res[k]:
                dx = dx + st_refs[k][2][...]
            dx_refs[k][...] = dx.astype(dx_refs[k].dtype)
            accs[k][...] += part
            dh = dx

        @pl.when(i == n - 1)
        def _():
            for k in range(n_s):
                dg_refs[k][...] = jnp.sum(accs[k][...], axis=0, keepdims=True)
            if ne:
                _to_chips_finish(*comm)

    row = pl.BlockSpec((tm, d), lambda i: (i, 0))
    vec = pl.BlockSpec((1, d), lambda i: (0, 0))
    in_specs, args = [], []
    for arr, w, blk in a_views:
        in_specs.append(pl.BlockSpec((tm, w), functools.partial(lambda i, blk: (i, blk), blk=blk)))
        args.append(arr)
    for arr, w, blk in b_views:
        in_specs.append(pl.BlockSpec((d, w), functools.partial(lambda i, blk: (0, blk), blk=blk)))
        args.append(arr)
    for x, g, res, _ in stages:
        in_specs += [row, vec] + ([row] if res is not None else [])
        args += [x, g] + ([res] if res is not None else [])
    return pl.pallas_call(
        body, grid=(n,), in_specs=in_specs + [ANY] * ne,
        out_specs=[row] * n_s + [vec] * n_s + [ANY] * ne,
        out_shape=[SDS((s, d), st[3]) for st in stages] + [SDS((1, d), F32)] * n_s + _to_chips_shapes(exchange),
        scratch_shapes=[pltpu.VMEM((8, d), F32)] * n_s + (_to_chips_sems(ne) if ne else []), name=name,
        compiler_params=_cp(("arbitrary",)))(*args, *exchange)


def _split3(v):
    hi = v.astype(BF16).astype(F32)
    r = v - hi
    mid = r.astype(BF16).astype(F32)
    lo = (r - mid).astype(BF16).astype(F32)
    return hi, mid, lo


def _tri(n, upper):
    r = np.arange(n)
    m = (r[:, None] <= r[None, :]) if upper else (r[:, None] >= r[None, :])
    return jnp.asarray(m.astype(np.float32))


def fox_prep(z, fa, bfo, *, tb=512):
    s = z.shape[0]
    n = s // tb

    def body(q_ref, k_ref, v_ref, fa_ref, b_ref, tri_ref, qa_ref, ka_ref, va_ref, carry):
        @pl.when(pl.program_id(0) == 0)
        def _():
            carry[...] = jnp.zeros_like(carry)

        xv = fa_ref[...] + b_ref[...]
        logf = jnp.minimum(xv, 0.0) - jnp.log(1.0 + jnp.exp(-jnp.abs(xv)))
        csum = jnp.dot(tri_ref[...], logf, preferred_element_type=F32, precision=lax.Precision.HIGHEST) + carry[0:1, :]
        carry[0:1, :] = csum[tb - 1:tb, :]
        lane = _lane((tb, LANES))
        for h in range(N_HEADS):
            hi, mid, lo = _split3(csum[:, h:h + 1])
            pair = (h // 2) * LANES
            qv = q_ref[:, pair:pair + LANES].astype(F32)
            kv = k_ref[:, pair:pair + LANES].astype(F32)
            vv = v_ref[:, pair:pair + LANES].astype(F32)
            if h % 2:
                qv = pltpu.roll(qv, 64, axis=1)
                kv = pltpu.roll(kv, 64, axis=1)
                vv = pltpu.roll(vv, 64, axis=1)
            va_ref[:, h * LANES:(h + 1) * LANES] = jnp.where(lane < 64, vv, jnp.where(lane == 64, 1.0, 0.0)).astype(BF16)
            one = jnp.where((lane >= 67) & (lane < 70), 1.0, 0.0)
            q_x = jnp.where(lane == 64, hi, jnp.where(lane == 65, mid, jnp.where(lane == 66, lo, one)))
            one = jnp.where((lane >= 64) & (lane < 67), 1.0, 0.0)
            k_x = jnp.where(lane == 67, -hi, jnp.where(lane == 68, -mid, jnp.where(lane == 69, -lo, one)))
            qa_ref[:, h * LANES:(h + 1) * LANES] = jnp.where(lane < 64, qv * QK_SCALE, q_x).astype(BF16)
            ka_ref[:, h * LANES:(h + 1) * LANES] = jnp.where(lane < 64, kv, k_x).astype(BF16)

    return pl.pallas_call(
        body, grid=(n,),
        in_specs=[pl.BlockSpec((tb, ATT_W), lambda i: (i, Z_QA)), pl.BlockSpec((tb, ATT_W), lambda i: (i, Z_KA)),
                  pl.BlockSpec((tb, ATT_W), lambda i: (i, Z_VA)),
                  pl.BlockSpec((tb, LANES), lambda i: (i, 0)), pl.BlockSpec((1, LANES), lambda i: (0, 0)),
                  pl.BlockSpec((tb, tb), lambda i: (0, 0))],
        out_specs=[pl.BlockSpec((tb, N_HEADS * LANES), lambda i: (i, 0))] * 3,
        out_shape=[SDS((s, N_HEADS * LANES), BF16)] * 3,
        scratch_shapes=[pltpu.VMEM((8, LANES), F32)],
        name="fox_prep", compiler_params=_cp(("arbitrary",)))(z, z, z, fa, bfo, _tri(tb, False))


def _causal_pairs(n, k_major):
    if k_major:
        pairs = [(qi, kj) for kj in range(n) for qi in range(kj, n)]
    else:
        pairs = [(qi, kj) for qi in range(n) for kj in range(qi + 1)]
    return (jnp.asarray([p[0] for p in pairs], jnp.int32), jnp.asarray([p[1] for p in pairs], jnp.int32), len(pairs))


def fox_fwd(q_aug, k_aug, v_aug, gather=(), *, t=512, hps=4):
    s = v_aug.shape[0]
    qi_arr, kj_arr, n_pairs = _causal_pairs(s // t, False)
    ng = len(gather)
    n_groups = N_HEADS // hps

    def body(qi_ref, kj_ref, q_ref, k_ref, v_ref, *rest):
        g_ins, (o_ref, lse_ref), g_outs = rest[:ng], rest[ng:ng + 2], rest[ng + 2:2 * ng + 2]
        m_scr, acc_scr = rest[2 * ng + 2:2 * ng + 4]
        comm = (g_ins, g_outs) + tuple(rest[2 * ng + 4:]) + ([True] * ng,)
        step = pl.program_id(1)
        qi = qi_ref[step]
        kj = kj_ref[step]
        if ng:
            @pl.when((pl.program_id(0) == 0) & (step == 0))
            def _():
                _allgather_start(*comm)

        @pl.when(kj == 0)
        def _():
            m_scr[...] = jnp.full_like(m_scr, NEG)
            acc_scr[...] = jnp.zeros_like(acc_scr)

        def update(masked):
            for i in range(hps):
                sc = _nt(q_ref[:, i * LANES:(i + 1) * LANES], k_ref[:, i * LANES:(i + 1) * LANES])
                if masked:
                    sc = jnp.where(_row((t, t)) >= _lane((t, t)), sc, NEG)
                m_prev = m_scr[i]
                m_new = jnp.maximum(m_prev, jnp.max(sc, axis=-1, keepdims=True))
                p = jnp.exp((sc - jnp.tile(m_new, (1, t // LANES))).astype(BF16))
                acc_scr[i] = jnp.exp(m_prev - m_new) * acc_scr[i] + _nn(p, v_ref[:, i * LANES:(i + 1) * LANES])
                m_scr[i] = m_new

        @pl.when(kj < qi)
        def _():
            update(False)

        @pl.when(kj == qi)
        def _():
            update(True)
            lane = _lane((t, LANES))
            for pr in range(hps // 2):
                den = [acc_scr[2 * pr + i][:, 64:65] for i in range(2)]
                o_ref[:, pr * LANES:(pr + 1) * LANES] = jnp.where(
                    lane < 64, acc_scr[2 * pr] / den[0], pltpu.roll(acc_scr[2 * pr + 1] / den[1], 64, axis=1)).astype(o_ref.dtype)
                lse_ref[:, pr * LANES:(pr + 1) * LANES] = jnp.where(
                    lane < 64, m_scr[2 * pr] + jnp.log(den[0]), m_scr[2 * pr + 1] + jnp.log(den[1]))

        if ng:
            @pl.when((pl.program_id(0) == n_groups - 1) & (step == n_pairs - 1))
            def _():
                _allgather_finish(*comm)

    wide = hps * LANES
    grid_spec = pltpu.PrefetchScalarGridSpec(
        num_scalar_prefetch=2, grid=(n_groups, n_pairs),
        in_specs=[pl.BlockSpec((t, wide), lambda hg, st, qi, kj: (qi[st], hg)),
                  pl.BlockSpec((t, wide), lambda hg, st, qi, kj: (kj[st], hg)),
                  pl.BlockSpec((t, wide), lambda hg, st, qi, kj: (kj[st], hg))] + [ANY] * ng,
        out_specs=[pl.BlockSpec((t, wide // 2), lambda hg, st, qi, kj: (qi[st], hg))] * 2 + [ANY] * ng,
        scratch_shapes=[pltpu.VMEM((hps, t, LANES), F32)] * 2 + (_allgather_sems(ng) if ng else []))
    return pl.pallas_call(
        body, grid_spec=grid_spec, out_shape=[SDS((s, ATT_W), BF16), SDS((s, ATT_W), F32)] + _allgather_shapes(gather),
        name="fox_fwd", compiler_params=_cp(("arbitrary", "arbitrary")))(qi_arr, kj_arr, q_aug, k_aug, v_aug, *gather)


def fox_bwd(q_aug, k_aug, z, dy, lse, dd, exchange=(), kind="to_chips", *, t=512, hps=4):
    s = z.shape[0]
    qi_arr, kj_arr, n_pairs = _causal_pairs(s // t, True)
    ne = len(exchange)
    n_groups = N_HEADS // hps
    x_shapes, x_sems, x_start, x_finish = EXCHANGES[kind]

    def body(qi_ref, kj_ref, q_ref, k_ref, v_ref, do_ref, lse_ref, dd_ref, *rest):
        e_ins, (dq_ref, dk_ref, dv_ref), e_outs = rest[:ne], rest[ne:ne + 3], rest[ne + 3:2 * ne + 3]
        comm = (e_ins, e_outs) + tuple(rest[2 * ne + 3:])
        step = pl.program_id(1)
        qi = qi_ref[step]
        kj = kj_ref[step]
        if ne:
            @pl.when((pl.program_id(0) == 0) & (step == 0))
            def _():
                x_start(*comm)

        @pl.when(step == 0)
        def _():
            dq_ref[...] = jnp.zeros_like(dq_ref)

        @pl.when(qi == kj)
        def _():
            dk_ref[...] = jnp.zeros_like(dk_ref)
            dv_ref[...] = jnp.zeros_like(dv_ref)

        def update(masked):
            lane = _lane((t, LANES))
            rows = pl.ds(pl.multiple_of(qi * t, t), t)
            for pr in range(hps // 2):
                pair = slice(pr * LANES, (pr + 1) * LANES)
                dov = do_ref[:, pair]
                dv_new = None
                for i in range(2):
                    head = (lane < 64) if i == 0 else (lane >= 64)
                    own = slice((2 * pr + i) * LANES, (2 * pr + i + 1) * LANES)
                    col = slice(pr * LANES + i * 64, pr * LANES + i * 64 + 1)
                    qv = q_ref[:, own]
                    kv = k_ref[:, own]
                    sc = _nt(qv, kv)
                    if masked:
                        sc = jnp.where(_row((t, t)) >= _lane((t, t)), sc, NEG)
                    p = jnp.exp(sc - lse_ref[:, col])
                    dp = _nt(jnp.where(head, dov, jnp.zeros_like(dov)), v_ref[:, pair])
                    ds = (p * (dp - dd_ref[:, col])).astype(BF16)
                    dq_ref[rows, own] += _nn(ds, kv)
                    dk_ref[:, own] += _tn(ds, qv)
                    dvi = _tn(p.astype(BF16), dov)
                    dv_new = dvi if dv_new is None else jnp.where(head, dvi, dv_new)
                dv_ref[:, pair] += dv_new

        @pl.when(kj < qi)
        def _():
            update(False)

        @pl.when(kj == qi)
        def _():
            update(True)

        if ne:
            @pl.when((pl.program_id(0) == n_groups - 1) & (step == n_pairs - 1))
            def _():
                x_finish(*comm)

    wide, half = hps * LANES, hps // 2 * LANES
    v_blk = Z_VA * ATT_W // half
    grid_spec = pltpu.PrefetchScalarGridSpec(
        num_scalar_prefetch=2, grid=(n_groups, n_pairs),
        in_specs=[pl.BlockSpec((t, wide), lambda hg, st, qi, kj: (qi[st], hg)),
                  pl.BlockSpec((t, wide), lambda hg, st, qi, kj: (kj[st], hg)),
                  pl.BlockSpec((t, half), lambda hg, st, qi, kj: (kj[st], v_blk + hg)),
                  pl.BlockSpec((t, half), lambda hg, st, qi, kj: (qi[st], hg)),
                  pl.BlockSpec((t, half), lambda hg, st, qi, kj: (qi[st], hg)),
                  pl.BlockSpec((t, half), lambda hg, st, qi, kj: (qi[st], hg))] + [ANY] * ne,
        out_specs=[pl.BlockSpec((s, wide), lambda hg, st, qi, kj: (0, hg)),
                   pl.BlockSpec((t, wide), lambda hg, st, qi, kj: (kj[st], hg)),
                   pl.BlockSpec((t, half), lambda hg, st, qi, kj: (kj[st], hg))] + [ANY] * ne,
        scratch_shapes=x_sems(ne) if ne else [])
    return pl.pallas_call(
        body, grid_spec=grid_spec,
        out_shape=[SDS((s, N_HEADS * LANES), F32), SDS((s, N_HEADS * LANES), F32), SDS((s, ATT_W), F32)]
        + x_shapes(exchange),
        name="fox_bwd", compiler_params=_cp(("arbitrary", "arbitrary")))(qi_arr, kj_arr, q_aug, k_aug, z, dy, lse, dd, *exchange)


def head_rowsum(a, b, *, tm=512, name):
    s = a.shape[0]

    def body(a_ref, b_ref, o_ref):
        prod = a_ref[...].astype(F32) * b_ref[...].astype(F32)
        lane = _lane((tm, LANES))
        lo = jnp.sum(jnp.where(lane < 64, prod, 0.0), axis=-1, keepdims=True)
        hi = jnp.sum(jnp.where(lane >= 64, prod, 0.0), axis=-1, keepdims=True)
        o_ref[...] = jnp.where(lane < 64, lo, hi)

    blk = pl.BlockSpec((tm, LANES), lambda i, j: (i, j))
    return pl.pallas_call(body, grid=(s // tm, 4), in_specs=[blk, blk], out_specs=blk, out_shape=SDS((s, ATT_W), F32),
                          name=name, compiler_params=_cp(("parallel", "parallel")))(a, b)


def fox_post(dq_aug, dk_aug, dv, fa, bfo, *, tb=512):
    s = dv.shape[0]
    n = s // tb

    def body(dq_ref, dk_ref, dv_ref, fa_ref, b_ref, tri_ref, dz_ref, dfa_ref, gb_ref, carry, acc):
        i = pl.program_id(0)

        @pl.when(i == 0)
        def _():
            carry[...] = jnp.zeros_like(carry)
            acc[...] = jnp.zeros_like(acc)

        lane = _lane((tb, LANES))
        d_f = jnp.zeros((tb, LANES), F32)
        for h in range(N_HEADS):
            col = dq_ref[:, h * LANES + 64:h * LANES + 65] - dk_ref[:, h * LANES + 67:h * LANES + 68]
            d_f = jnp.where(lane == h, col, d_f)
        suffix = jnp.dot(tri_ref[...], d_f, preferred_element_type=F32, precision=lax.Precision.HIGHEST) + carry[0:1, :]
        carry[0:1, :] = suffix[0:1, :]
        xv = fa_ref[...] + b_ref[...]
        dx = suffix * (1.0 / (1.0 + jnp.exp(xv)))
        dfa_ref[...] = dx.astype(dfa_ref.dtype)
        acc[...] += jnp.sum(dx.reshape(tb // 8, 8, LANES), axis=0)
        for hp in range(4):
            for src, off, scale in ((dq_ref, 0, QK_SCALE), (dk_ref, ATT_W, 1.0)):
                even = src[:, (2 * hp) * LANES:(2 * hp + 1) * LANES]
                odd = pltpu.roll(src[:, (2 * hp + 1) * LANES:(2 * hp + 2) * LANES], 64, axis=1)
                dz_ref[:, off + hp * LANES:off + (hp + 1) * LANES] = (jnp.where(lane < 64, even, odd) * scale).astype(BF16)
        dz_ref[:, 2 * ATT_W:3 * ATT_W] = dv_ref[...].astype(BF16)

        @pl.when(i == n - 1)
        def _():
            gb_ref[...] = jnp.sum(acc[...], axis=0, keepdims=True)

    rev = lambda i: (n - 1 - i, 0)
    return pl.pallas_call(
        body, grid=(n,),
        in_specs=[pl.BlockSpec((tb, N_HEADS * LANES), rev), pl.BlockSpec((tb, N_HEADS * LANES), rev),
                  pl.BlockSpec((tb, ATT_W), rev), pl.BlockSpec((tb, LANES), rev),
                  pl.BlockSpec((1, LANES), lambda i: (0, 0)), pl.BlockSpec((tb, tb), lambda i: (0, 0))],
        out_specs=[pl.BlockSpec((tb, 3 * ATT_W), rev), pl.BlockSpec((tb, LANES), rev),
                   pl.BlockSpec((1, LANES), lambda i: (0, 0))],
        out_shape=[SDS((s, 3 * ATT_W), BF16), SDS((s, LANES), BF16), SDS((1, LANES), F32)],
        scratch_shapes=[pltpu.VMEM((8, LANES), F32), pltpu.VMEM((8, LANES), F32)],
        name="fox_post", compiler_params=_cp(("arbitrary",)))(dq_aug, dk_aug, dv, fa, bfo, _tri(tb, True))


def rope_tables(s, sign):
    half = ROPE_DIM // 2
    inv_freq = ROPE_THETA ** (-jnp.arange(half, dtype=F32) * 2.0 / ROPE_DIM)
    ang = jnp.arange(s, dtype=F32)[:, None] * inv_freq[None, :]
    l64 = np.arange(LANES) % HEAD_DIM
    cos = jnp.tile(jnp.cos(ang), (1, LANES // half))
    sin = jnp.tile(jnp.sin(ang), (1, LANES // half)) * sign
    first = jnp.asarray(l64 < half)[None, :]
    second = jnp.asarray((l64 >= half) & (l64 < ROPE_DIM))[None, :]
    return (jnp.where(first | second, cos, 1.0), jnp.where(first, -sin, 0.0), jnp.where(second, sin, 0.0))


def rope_apply(items, tabs, *, out_dtype, tm=512, name):
    s = items[0][0].shape[0]
    n_i = len(items)

    def body(*refs):
        c_ref, sn_ref, sp_ref = refs[n_i:n_i + 3]
        o_ref = refs[-1]
        for j, (_, _, scale, rotate) in enumerate(items):
            for b in range(4):
                xv = refs[j][:, b * LANES:(b + 1) * LANES].astype(F32)
                if rotate:
                    xv = xv * c_ref[...] + pltpu.roll(xv, LANES - 8, axis=1) * sn_ref[...] + pltpu.roll(xv, 8, axis=1) * sp_ref[...]
                o_ref[:, j * ATT_W + b * LANES:j * ATT_W + (b + 1) * LANES] = (xv * scale).astype(o_ref.dtype)

    in_specs = [pl.BlockSpec((tm, ATT_W), functools.partial(lambda i, blk: (i, blk), blk=it[1])) for it in items]
    in_specs += [pl.BlockSpec((tm, LANES), lambda i: (i, 0))] * 3
    return pl.pallas_call(
        body, grid=(s // tm,), in_specs=in_specs, out_specs=pl.BlockSpec((tm, n_i * ATT_W), lambda i: (i, 0)),
        out_shape=SDS((s, n_i * ATT_W), out_dtype), name=name, compiler_params=_cp(("parallel",)))(*[it[0] for it in items], *tabs)


def _dil_views(qk, z, r):
    s = z.shape[0]
    return qk.reshape(s // r, r * 2 * ATT_W), z.reshape(s // r, r * Z_W)


def _dil_cols(r):
    q_col = lambda rho, hp: rho * 8 + hp
    k_col = lambda rho, hp: rho * 8 + 4 + hp
    v_col = lambda rho, hp: rho * (Z_W // LANES) + 4 * Z_VB + hp
    return q_col, k_col, v_col


def _dil_scores(qv, kp, kc, head, has_prev):
    b = DIL_BLK
    qm = jnp.where(head, qv, jnp.zeros_like(qv))
    row, col = _row((b, b)), _lane((b, b))
    sp = jnp.where((col >= row) & has_prev, _nt(qm, kp), NEG)
    sc = jnp.where(col <= row, _nt(qm, kc), NEG)
    return sp, sc


def dil_fwd(qk, z, prev, *, r):
    s = z.shape[0]
    b = DIL_BLK
    l_sub = s // r
    nb = l_sub // b
    qk_v, z_v = _dil_views(qk, z, r)
    q_col, k_col, v_col = _dil_cols(r)
    merge = prev is not None

    def body(*refs):
        if merge:
            q_ref, kp_ref, kc_ref, vp_ref, vc_ref, op_ref, lp_ref, o_ref, l_ref = refs
        else:
            q_ref, kp_ref, kc_ref, vp_ref, vc_ref, o_ref, l_ref = refs
        has_prev = pl.program_id(2) > 0
        lane = _lane((b, LANES))
        res = []
        for i in range(2):
            head = (lane < 64) if i == 0 else (lane >= 64)
            sp, sc = _dil_scores(q_ref[...], kp_ref[...], kc_ref[...], head, has_prev)
            m = jnp.maximum(jnp.max(sp, axis=-1, keepdims=True), jnp.max(sc, axis=-1, keepdims=True))
            pp = jnp.exp(sp - m)
            pc = jnp.exp(sc - m)
            den = jnp.sum(pp, axis=-1, keepdims=True) + jnp.sum(pc, axis=-1, keepdims=True)
            ov = (_nn(pp.astype(BF16), vp_ref[...]) + _nn(pc.astype(BF16), vc_ref[...])) / den
            res.append((ov, m + jnp.log(den)))
        ov = jnp.where(lane < 64, res[0][0], res[1][0])
        lse = jnp.where(lane < 64, res[0][1], res[1][1])
        if merge:
            lp = lp_ref[...]
            m2 = jnp.maximum(lp, lse)
            wp = jnp.exp(lp - m2)
            wn = jnp.exp(lse - m2)
            ov = (wp * op_ref[...] + wn * ov) / (wp + wn)
            lse = m2 + jnp.log(wp + wn)
        o_ref[...] = ov
        l_ref[...] = lse

    blk = lambda f: pl.BlockSpec((b, LANES), f)
    in_specs = [blk(lambda rho, hp, n: (n, q_col(rho, hp))), blk(lambda rho, hp, n: (jnp.maximum(n - 1, 0), k_col(rho, hp))),
                blk(lambda rho, hp, n: (n, k_col(rho, hp))), blk(lambda rho, hp, n: (jnp.maximum(n - 1, 0), v_col(rho, hp))),
                blk(lambda rho, hp, n: (n, v_col(rho, hp)))]
    args = [qk_v, qk_v, qk_v, z_v, z_v]
    nat = blk(lambda rho, hp, n: (n, rho * 4 + hp))
    if merge:
        in_specs += [nat, nat]
        args += [prev[0].reshape(l_sub, r * ATT_W), prev[1].reshape(l_sub, r * ATT_W)]
    o, lse = pl.pallas_call(
        body, grid=(r, 4, nb), in_specs=in_specs, out_specs=[nat, nat],
        out_shape=[SDS((l_sub, r * ATT_W), F32)] * 2, name=f"dil_fwd_r{r}",
        compiler_params=_cp(("parallel", "parallel", "arbitrary")))(*args)
    return o.reshape(s, ATT_W), lse.reshape(s, ATT_W)


def dil_bwd_dq(qk, z, dy, lse, dd, acc, *, r):
    s = z.shape[0]
    b = DIL_BLK
    l_sub = s // r
    nb = l_sub // b
    qk_v, z_v = _dil_views(qk, z, r)
    q_col, k_col, v_col = _dil_cols(r)
    add = acc is not None

    def body(*refs):
        q_ref, kp_ref, kc_ref, vp_ref, vc_ref, do_ref, l_ref, dd_ref = refs[:8]
        dq_ref = refs[-1]
        has_prev = pl.program_id(2) > 0
        lane = _lane((b, LANES))
        dov = do_ref[...]
        parts = []
        for i in range(2):
            head = (lane < 64) if i == 0 else (lane >= 64)
            sp, sc = _dil_scores(q_ref[...], kp_ref[...], kc_ref[...], head, has_prev)
            lse_i = l_ref[:, i * 64:i * 64 + 1]
            dd_i = dd_ref[:, i * 64:i * 64 + 1]
            dom = jnp.where(head, dov, jnp.zeros_like(dov))
            dsp = (jnp.exp(sp - lse_i) * (_nt(dom, vp_ref[...]) - dd_i)).astype(BF16)
            dsc = (jnp.exp(sc - lse_i) * (_nt(dom, vc_ref[...]) - dd_i)).astype(BF16)
            parts.append(_nn(dsp, kp_ref[...]) + _nn(dsc, kc_ref[...]))
        dq = jnp.where(lane < 64, parts[0], parts[1])
        if add:
            dq = dq + refs[8][...]
        dq_ref[...] = dq

    blk = lambda f: pl.BlockSpec((b, LANES), f)
    nat = blk(lambda rho, hp, n: (n, rho * 4 + hp))
    in_specs = [blk(lambda rho, hp, n: (n, q_col(rho, hp))), blk(lambda rho, hp, n: (jnp.maximum(n - 1, 0), k_col(rho, hp))),
                blk(lambda rho, hp, n: (n, k_col(rho, hp))), blk(lambda rho, hp, n: (jnp.maximum(n - 1, 0), v_col(rho, hp))),
                blk(lambda rho, hp, n: (n, v_col(rho, hp))), nat, nat, nat]
    nview = lambda a: a.reshape(l_sub, r * ATT_W)
    args = [qk_v, qk_v, qk_v, z_v, z_v, nview(dy), nview(lse), nview(dd)]
    if add:
        in_specs.append(nat)
        args.append(nview(acc))
    dq = pl.pallas_call(
        body, grid=(r, 4, nb), in_specs=in_specs, out_specs=nat, out_shape=SDS((l_sub, r * ATT_W), F32),
        name=f"dil_bwd_dq_r{r}", compiler_params=_cp(("parallel", "parallel", "arbitrary")))(*args)
    return dq.reshape(s, ATT_W)


def dil_bwd_dkv(qk, z, dy, lse, dd, acc, *, r):
    s = z.shape[0]
    b = DIL_BLK
    l_sub = s // r
    nb = l_sub // b
    qk_v, z_v = _dil_views(qk, z, r)
    q_col, k_col, v_col = _dil_cols(r)
    add = acc is not None

    def body(*refs):
        k_ref, v_ref, qc_ref, qn_ref, doc_ref, don_ref, lc_ref, ln_ref, ddc_ref, ddn_ref = refs[:10]
        dk_ref, dv_ref = refs[-2:]
        has_next = pl.program_id(2) < nb - 1
        lane = _lane((b, LANES))
        row, col = _row((b, b)), _lane((b, b))
        kv = k_ref[...]
        vv = v_ref[...]
        dk_parts, dv_parts = [], []
        for i in range(2):
            head = (lane < 64) if i == 0 else (lane >= 64)
            dk_i = jnp.zeros((b, LANES), F32)
            dv_i = jnp.zeros((b, LANES), F32)
            for q_ref, do_ref, l_ref, d_ref, mask in ((qc_ref, doc_ref, lc_ref, ddc_ref, col <= row),
                                                      (qn_ref, don_ref, ln_ref, ddn_ref, (col >= row) & has_next)):
                qv = q_ref[...]
                dov = do_ref[...]
                sc = jnp.where(mask, _nt(jnp.where(head, qv, jnp.zeros_like(qv)), kv), NEG)
                p = jnp.exp(sc - l_ref[:, i * 64:i * 64 + 1])
                dp = _nt(jnp.where(head, dov, jnp.zeros_like(dov)), vv)
                ds = (p * (dp - d_ref[:, i * 64:i * 64 + 1])).astype(BF16)
                dv_i = dv_i + _tn(p.astype(BF16), dov)
                dk_i = dk_i + _tn(ds, qv)
            dk_parts.append(dk_i)
            dv_parts.append(dv_i)
        dk = jnp.where(lane < 64, dk_parts[0], dk_parts[1])
        dv = jnp.where(lane < 64, dv_parts[0], dv_parts[1])
        if add:
            dk = dk + refs[10][...]
            dv = dv + refs[11][...]
        dk_ref[...] = dk
        dv_ref[...] = dv

    blk = lambda f: pl.BlockSpec((b, LANES), f)
    nat = blk(lambda rho, hp, n: (n, rho * 4 + hp))
    nxt = blk(lambda rho, hp, n: (jnp.minimum(n + 1, nb - 1), rho * 4 + hp))
    in_specs = [blk(lambda rho, hp, n: (n, k_col(rho, hp))), blk(lambda rho, hp, n: (n, v_col(rho, hp))),
                blk(lambda rho, hp, n: (n, q_col(rho, hp))), blk(lambda rho, hp, n: (jnp.minimum(n + 1, nb - 1), q_col(rho, hp))),
                nat, nxt, nat, nxt, nat, nxt]
    nview = lambda a: a.reshape(l_sub, r * ATT_W)
    args = [qk_v, z_v, qk_v, qk_v, nview(dy), nview(dy), nview(lse), nview(lse), nview(dd), nview(dd)]
    if add:
        in_specs += [nat, nat]
        args += [nview(acc[0]), nview(acc[1])]
    dk, dv = pl.pallas_call(
        body, grid=(r, 4, nb), in_specs=in_specs, out_specs=[nat, nat],
        out_shape=[SDS((l_sub, r * ATT_W), F32)] * 2, name=f"dil_bwd_dkv_r{r}",
        compiler_params=_cp(("parallel", "parallel", "arbitrary")))(*args)
    return dk.reshape(s, ATT_W), dv.reshape(s, ATT_W)


def _dil_rows(base, r):
    if r == 1:
        return pl.ds(pl.multiple_of(base, DIL_BLK), DIL_BLK)
    return pl.ds(base, DIL_BLK, stride=r)


def _dil_block(idx, r, nb):
    shift = nb.bit_length() - 1
    rho = idx >> shift
    n = idx & (nb - 1)
    base = rho + n * (r * DIL_BLK)
    return _dil_rows(base, r), _dil_rows(jnp.maximum(base - r * DIL_BLK, rho), r), n > 0


def _cat(a, b):
    return jnp.concatenate([a, b], axis=0)


def _two_heads(v, first_head):
    zero = jnp.zeros_like(v)
    return _cat(jnp.where(first_head, v, zero), jnp.where(first_head, zero, v))


def _dil_bands():
    b = DIL_BLK
    q = _row((2 * b, 2 * b)) & (b - 1)
    col = _lane((2 * b, 2 * b))
    return (col < b) & (col >= q), (col >= b) & (col - b <= q)


def dil_fwd_all(qkv, *, unroll=8):
    s = qkv.shape[0]
    b = DIL_BLK
    n_blk = s // b

    def body(q_ref, k_ref, v_ref, o_ref, l_ref):
        first_head = _lane((b, LANES)) < 64
        band_prev, band_cur = _dil_bands()
        for g, (_, r) in enumerate(DIL_PATTERNS):
            nb = n_blk // r

            def group(it, carry, g=g, r=r, nb=nb):
                loaded = []
                for u in range(unroll):
                    rows_c, rows_p, has_prev = _dil_block(it * unroll + u, r, nb)
                    vals = [q_ref[rows_c, :].astype(BF16), k_ref[rows_p, :].astype(BF16), k_ref[rows_c, :].astype(BF16),
                            v_ref[rows_p, :].astype(BF16), v_ref[rows_c, :].astype(BF16)]
                    state = (o_ref[rows_c, :], l_ref[rows_c, :]) if g else None
                    loaded.append((rows_c, has_prev, vals, state))
                done = []
                for rows_c, has_prev, (qv, kp, kc, vp, vc), state in loaded:
                    sc = jnp.where(band_cur | (band_prev & has_prev), _nt(_two_heads(qv, first_head), _cat(kp, kc)), NEG)
                    m = jnp.max(sc, axis=-1, keepdims=True)
                    p = jnp.exp(sc - m)
                    den = jnp.sum(p, axis=-1, keepdims=True)
                    both = _nn(p.astype(BF16), _cat(vp, vc)) / den
                    lse2 = m + jnp.log(den)
                    ov = jnp.where(first_head, both[:b], both[b:])
                    lse = jnp.where(first_head, lse2[:b], lse2[b:])
                    if state is not None:
                        m2 = jnp.maximum(state[1], lse)
                        wp = jnp.exp(state[1] - m2)
                        wn = jnp.exp(lse - m2)
                        ov = (wp * state[0] + wn * ov) / (wp + wn)
                        lse = m2 + jnp.log(wp + wn)
                    done.append((rows_c, ov, lse))
                for rows_c, ov, lse in done:
                    o_ref[rows_c, :] = ov
                    l_ref[rows_c, :] = lse
                return carry

            lax.fori_loop(0, n_blk // unroll, group, 0)

    col_blk = lambda k: pl.BlockSpec((s, LANES), lambda hp: (0, 4 * k + hp))
    out = pl.BlockSpec((s, LANES), lambda hp: (0, hp))
    return pl.pallas_call(
        body, grid=(4,), in_specs=[col_blk(0), col_blk(1), col_blk(2)], out_specs=[out, out],
        out_shape=[SDS((s, ATT_W), F32)] * 2, name="dil_fwd", compiler_params=_cp(("parallel",)))(qkv, qkv, qkv)


def dil_bwd_all(qkv, dy, lse, y, exchange=(), kind="to_chips", *, unroll=8):
    s = qkv.shape[0]
    b = DIL_BLK
    n_blk = s // b
    ne = len(exchange)
    x_shapes, x_sems, x_start, x_finish = EXCHANGES[kind]

    def body(q_ref, k_ref, v_ref, do_ref, l_ref, y_ref, *rest):
        e_ins, (dq_ref, dk_ref, dv_ref), e_outs = rest[:ne], rest[ne:ne + 3], rest[ne + 3:2 * ne + 3]
        comm = (e_ins, e_outs) + tuple(rest[2 * ne + 3:])
        if ne:
            @pl.when(pl.program_id(0) == 0)
            def _():
                x_start(*comm)

        dq_ref[...] = jnp.zeros_like(dq_ref)
        dk_ref[...] = jnp.zeros_like(dk_ref)
        dv_ref[...] = jnp.zeros_like(dv_ref)
        first_head = _lane((b, LANES)) < 64
        band_prev, band_cur = _dil_bands()
        for _, r in DIL_PATTERNS:
            nb = n_blk // r

            def group(it, carry, r=r, nb=nb):
                loaded = []
                for u in range(unroll):
                    rows_c, rows_p, has_prev = _dil_block(it * unroll + u, r, nb)
                    vals = [q_ref[rows_c, :].astype(BF16), k_ref[rows_p, :].astype(BF16), k_ref[rows_c, :].astype(BF16),
                            v_ref[rows_p, :].astype(BF16), v_ref[rows_c, :].astype(BF16), do_ref[rows_c, :],
                            l_ref[rows_c, :], y_ref[rows_c, :]]
                    loaded.append((rows_c, rows_p, has_prev, vals))
                done = []
                for rows_c, rows_p, has_prev, (qv, kp, kc, vp, vc, dof, lv, yv) in loaded:
                    q2 = _two_heads(qv, first_head)
                    do2 = _two_heads(dof.astype(BF16), first_head)
                    kcat, vcat = _cat(kp, kc), _cat(vp, vc)
                    lse2 = _cat(lv[:, 0:1], lv[:, 64:65])
                    dd2 = jnp.sum(_two_heads(dof * yv, first_head), axis=-1, keepdims=True)
                    p = jnp.exp(jnp.where(band_cur | (band_prev & has_prev), _nt(q2, kcat), NEG) - lse2)
                    ds = (p * (_nt(do2, vcat) - dd2)).astype(BF16)
                    dq2 = _nn(ds, kcat)
                    dkcat = _tn(ds, q2)
                    dvcat = _tn(p.astype(BF16), do2)
                    done.append((rows_c, rows_p, (jnp.where(first_head, dq2[:b], dq2[b:]), dkcat[:b], dkcat[b:],
                                                  dvcat[:b], dvcat[b:])))
                for rows_c, rows_p, (dq, dk_p, dk_c, dv_p, dv_c) in done:
                    dq_ref[rows_c, :] += dq
                    dk_ref[rows_p, :] += dk_p
                    dk_ref[rows_c, :] += dk_c
                    dv_ref[rows_p, :] += dv_p
                    dv_ref[rows_c, :] += dv_c
                return carry

            lax.fori_loop(0, n_blk // unroll, group, 0)

        if ne:
            @pl.when(pl.program_id(0) == 3)
            def _():
                x_finish(*comm)

    col_blk = lambda k: pl.BlockSpec((s, LANES), lambda hp: (0, 4 * k + hp))
    nat = pl.BlockSpec((s, LANES), lambda hp: (0, hp))
    return pl.pallas_call(
        body, grid=(4,), in_specs=[col_blk(0), col_blk(1), col_blk(2), nat, nat, nat] + [ANY] * ne,
        out_specs=[nat, nat, nat] + [ANY] * ne, out_shape=[SDS((s, ATT_W), F32)] * 3 + x_shapes(exchange),
        scratch_shapes=x_sems(ne) if ne else [], name="dil_bwd",
        compiler_params=_cp(("arbitrary",)))(qkv, qkv, qkv, dy, lse, y, *exchange)


def _sigmoid(v):
    return 1.0 / (1.0 + jnp.exp(-v))


def gate_mix(ya, yb, wa, wb, z, *, tm=512, tn=512):
    s = ya.shape[0]
    d = wa.shape[1]
    ga_blk = 3 * ATT_W * 2 // tn
    gb_blk = ga_blk + d // tn

    def body(ya_ref, yb_ref, wa_ref, wb_ref, ga_ref, gb_ref, pa_ref, pb_ref, mx_ref):
        pa = _nn(ya_ref[...], wa_ref[...])
        pb = _nn(yb_ref[...].astype(BF16), wb_ref[...])
        pa_ref[...] = pa.astype(BF16)
        pb_ref[...] = pb.astype(BF16)
        mx_ref[...] = (_sigmoid(ga_ref[...].astype(F32)) * pa + _sigmoid(gb_ref[...].astype(F32)) * pb).astype(BF16)

    out = pl.BlockSpec((tm, tn), lambda i, j: (i, j))
    return pl.pallas_call(
        body, grid=(s // tm, d // tn),
        in_specs=[pl.BlockSpec((tm, ATT_W), lambda i, j: (i, 0)), pl.BlockSpec((tm, ATT_W), lambda i, j: (i, 0)),
                  pl.BlockSpec((ATT_W, tn), lambda i, j: (0, j)), pl.BlockSpec((ATT_W, tn), lambda i, j: (0, j)),
                  pl.BlockSpec((tm, tn), lambda i, j: (i, ga_blk + j)), pl.BlockSpec((tm, tn), lambda i, j: (i, gb_blk + j))],
        out_specs=[out, out, out], out_shape=[SDS((s, d), BF16)] * 3, name="gate_mix",
        compiler_params=_cp(("parallel", "parallel")))(ya, yb, wa, wb, z, z)


def mix_bwd(dy, w_o, z, pa, pb, wo_a, wo_b, ya, *, tm=256):
    s, d = dy.shape

    def body(dy_ref, wo_ref, ga_ref, gb_ref, pa_ref, pb_ref, wa_ref, wb_ref, ya_ref,
             dpa_ref, dpb_ref, dg_ref, dya_ref, dyb_ref, dd_ref):
        dm = _nt(dy_ref[...], wo_ref[...])
        sa = _sigmoid(ga_ref[...].astype(F32))
        sb = _sigmoid(gb_ref[...].astype(F32))
        dpa = (dm * sa).astype(BF16)
        dpb = (dm * sb).astype(BF16)
        dpa_ref[...] = dpa
        dpb_ref[...] = dpb
        dg_ref[:, 0:d] = (dm * pa_ref[...].astype(F32) * sa * (1.0 - sa)).astype(BF16)
        dg_ref[:, d:2 * d] = (dm * pb_ref[...].astype(F32) * sb * (1.0 - sb)).astype(BF16)
        dya = _nt(dpa, wa_ref[...]).astype(BF16)
        dya_ref[...] = dya
        dyb_ref[...] = _nt(dpb, wb_ref[...])
        lane = _lane((tm, LANES))
        for pr in range(ATT_W // LANES):
            pair = slice(pr * LANES, (pr + 1) * LANES)
            prod = dya[:, pair].astype(F32) * ya_ref[:, pair].astype(F32)
            lo = jnp.sum(jnp.where(lane < 64, prod, 0.0), axis=-1, keepdims=True)
            hi = jnp.sum(jnp.where(lane >= 64, prod, 0.0), axis=-1, keepdims=True)
            dd_ref[:, pair] = jnp.where(lane < 64, lo, hi)

    row = pl.BlockSpec((tm, d), lambda i: (i, 0))
    att = pl.BlockSpec((tm, ATT_W), lambda i: (i, 0))
    whole = lambda a: pl.BlockSpec(a.shape, lambda i: (0, 0))
    return pl.pallas_call(
        body, grid=(s // tm,),
        in_specs=[row, whole(w_o), pl.BlockSpec((tm, d), lambda i: (i, 3)), pl.BlockSpec((tm, d), lambda i: (i, 4)), row, row,
                  whole(wo_a), whole(wo_b), att],
        out_specs=[row, row, pl.BlockSpec((tm, 2 * d), lambda i: (i, 0)), att, att, att],
        out_shape=[SDS((s, d), BF16), SDS((s, d), BF16), SDS((s, 2 * d), BF16), SDS((s, ATT_W), BF16),
                   SDS((s, ATT_W), F32), SDS((s, ATT_W), F32)], name="mix_bwd",
        compiler_params=_cp(("parallel",)))(dy, w_o, z, z, pa, pb, wo_a, wo_b, ya)


GELU_C = math.sqrt(2.0 / math.pi)


def _gelu_parts(a):
    a2 = a * a
    th = jnp.tanh(a * (GELU_C + (GELU_C * 0.044715) * a2))
    half = 0.5 * a
    gelu = half + half * th
    dgelu = (0.5 + 0.5 * th) + half * (1.0 - th * th) * (GELU_C + (3.0 * GELU_C * 0.044715) * a2)
    return gelu, dgelu


def _causal_taps(u, before):
    row = _row(u.shape)
    r1 = jnp.where(row == 0, before[7:8, :], pltpu.roll(u, 1, axis=0))
    r2 = jnp.where(row == 0, before[6:7, :], jnp.where(row == 1, before[7:8, :], pltpu.roll(u, 2, axis=0)))
    return r1, r2


def ffn_up(h, wa, wb, cw, cb, *, tm=1024, tn=256):
    s, d = h.shape
    f = wa.shape[1]
    nj = f // tn

    def body(h_ref, wa_ref, wb_ref, cwa_ref, cwb_ref, cba_ref, cbb_ref, ua_ref, ub_ref, ca_ref, cbo_ref, m_ref, carry):
        @pl.when(pl.program_id(1) == 0)
        def _():
            carry[...] = jnp.zeros_like(carry)

        conv = []
        for k, (w_ref, cw_ref, cb_ref, u_ref, c_ref) in enumerate(((wa_ref, cwa_ref, cba_ref, ua_ref, ca_ref),
                                                                   (wb_ref, cwb_ref, cbb_ref, ub_ref, cbo_ref))):
            u16 = _nn(h_ref[...], w_ref[...]).astype(BF16)
            u_ref[...] = u16
            u = u16.astype(F32)
            r1, r2 = _causal_taps(u, carry[k])
            carry[k] = u[tm - 8:tm, :]
            c16 = (cw_ref[0:1, :] * r2 + cw_ref[1:2, :] * r1 + cw_ref[2:3, :] * u + cb_ref[...]).astype(BF16)
            c_ref[...] = c16
            conv.append(c16.astype(F32))
        m_ref[...] = (_gelu_parts(conv[0])[0] * conv[1]).astype(BF16)

    out = pl.BlockSpec((tm, tn), lambda j, i: (i, j))
    return pl.pallas_call(
        body, grid=(nj, s // tm),
        in_specs=[pl.BlockSpec((tm, d), lambda j, i: (i, 0)),
                  pl.BlockSpec((d, tn), lambda j, i: (0, j)), pl.BlockSpec((d, tn), lambda j, i: (0, j)),
                  pl.BlockSpec((3, tn), lambda j, i: (0, j)), pl.BlockSpec((3, tn), lambda j, i: (0, nj + j)),
                  pl.BlockSpec((1, tn), lambda j, i: (0, j)), pl.BlockSpec((1, tn), lambda j, i: (0, nj + j))],
        out_specs=[out] * 5, out_shape=[SDS((s, f), BF16)] * 5,
        scratch_shapes=[pltpu.VMEM((2, 8, tn), F32)], name="ffn_up",
        compiler_params=_cp(("parallel", "arbitrary")))(h, wa, wb, cw, cw, cb, cb)


def ffn_bwd(dm, ua, ub, ca, cbo, cw, *, tm=1024, tn=256):
    s, f = dm.shape
    nj = f // tn
    ni = s // tm

    def body(dm_ref, ua_ref, ub_ref, ca_ref, cbo_ref, cwa_ref, cwb_ref, dua_ref, dub_ref, ga_ref, gb_ref, carry):
        @pl.when(pl.program_id(1) == 0)
        def _():
            carry[...] = jnp.zeros_like(carry)
            ga_ref[...] = jnp.zeros_like(ga_ref)
            gb_ref[...] = jnp.zeros_like(gb_ref)

        row = _row((tm, tn))
        dmv = dm_ref[...].astype(F32)
        gelu, dgelu = _gelu_parts(ca_ref[...].astype(F32))
        dcs = (dmv * cbo_ref[...].astype(F32) * dgelu, dmv * gelu)
        for k, (dc, u_ref, cw_ref, du_ref, g_ref) in enumerate(((dcs[0], ua_ref, cwa_ref, dua_ref, ga_ref),
                                                                (dcs[1], ub_ref, cwb_ref, dub_ref, gb_ref))):
            u = u_ref[...].astype(F32)
            after = carry[k]
            n1 = jnp.where(row == tm - 1, after[0:1, :], pltpu.roll(dc, tm - 1, axis=0))
            n2 = jnp.where(row == tm - 2, after[0:1, :], jnp.where(row == tm - 1, after[1:2, :], pltpu.roll(dc, tm - 2, axis=0)))
            g_ref[0:1, :] += jnp.sum(n2 * u, axis=0, keepdims=True)
            g_ref[1:2, :] += jnp.sum(n1 * u, axis=0, keepdims=True)
            g_ref[2:3, :] += jnp.sum(dc * u, axis=0, keepdims=True)
            g_ref[3:4, :] += jnp.sum(dc, axis=0, keepdims=True)
            du_ref[...] = (cw_ref[2:3, :] * dc + cw_ref[1:2, :] * n1 + cw_ref[0:1, :] * n2).astype(BF16)
            carry[k] = dc[0:8, :]

    tile = pl.BlockSpec((tm, tn), lambda j, i: (ni - 1 - i, j))
    gspec = pl.BlockSpec((8, tn), lambda j, i: (0, j))
    return pl.pallas_call(
        body, grid=(nj, ni),
        in_specs=[tile] * 5 + [pl.BlockSpec((3, tn), lambda j, i: (0, j)), pl.BlockSpec((3, tn), lambda j, i: (0, nj + j))],
        out_specs=[tile, tile, gspec, gspec],
        out_shape=[SDS((s, f), BF16), SDS((s, f), BF16), SDS((8, f), F32), SDS((8, f), F32)],
        scratch_shapes=[pltpu.VMEM((2, 8, tn), F32)], name="ffn_bwd",
        compiler_params=_cp(("parallel", "arbitrary")))(dm, ua, ub, ca, cbo, cw, cw)


def adamw(w, g, m, v, *, name, tr=None):
    r = w.shape[0]
    rest = w.shape[1:]
    if tr is None:
        tr = r
        for cand in (256, 128, 64, 32, 16, 8):
            if r % cand == 0:
                tr = cand
                break

    def body(w_ref, g_ref, m_ref, v_ref, d_ref, nm_ref, nv_ref):
        gv = g_ref[...]
        mn = ADAM_B1 * m_ref[...] + (1.0 - ADAM_B1) * gv
        vn = ADAM_B2 * v_ref[...] + (1.0 - ADAM_B2) * (gv * gv)
        m_hat = mn / (1.0 - ADAM_B1 ** ADAM_STEP)
        v_hat = vn / (1.0 - ADAM_B2 ** ADAM_STEP)
        d_ref[...] = -ADAM_LR * (m_hat / (jnp.sqrt(v_hat) + ADAM_EPS) + ADAM_WD * w_ref[...])
        nm_ref[...] = mn
        nv_ref[...] = vn

    blk = pl.BlockSpec((tr,) + rest, lambda i: (i,) + (0,) * len(rest))
    return pl.pallas_call(body, grid=(r // tr,), in_specs=[blk] * 4, out_specs=[blk] * 3, out_shape=[SDS(w.shape, F32)] * 3,
                          name=name, compiler_params=_cp(("parallel",)))(w, g, m, v)


def adamw_rows_view(w, g_mine, g_full, m, v, c_arr, *, name, tc=256):
    r, _, c = w.shape
    per_half = c // 2 // tc

    def body(c_ref, w_ref, gm_ref, gf_ref, m_ref, v_ref, d_ref, nm_ref, nv_ref, go_ref):
        mine = (pl.program_id(0) >> (per_half.bit_length() - 1)) == c_ref[0]
        gv = jnp.where(mine, gm_ref[...], gf_ref[...])[:, None, :]
        mn = ADAM_B1 * m_ref[...] + (1.0 - ADAM_B1) * gv
        vn = ADAM_B2 * v_ref[...] + (1.0 - ADAM_B2) * (gv * gv)
        m_hat = mn / (1.0 - ADAM_B1 ** ADAM_STEP)
        v_hat = vn / (1.0 - ADAM_B2 ** ADAM_STEP)
        d_ref[...] = -ADAM_LR * (m_hat / (jnp.sqrt(v_hat) + ADAM_EPS) + ADAM_WD * w_ref[...])
        nm_ref[...] = mn
        nv_ref[...] = vn
        go_ref[...] = gv

    b3 = pl.BlockSpec((r, 1, tc), lambda i, c_ref: (0, 0, i))
    own = pl.BlockSpec((r, tc), lambda i, c_ref: (0, jnp.clip(i - c_ref[0] * per_half, 0, per_half - 1)))
    full = pl.BlockSpec((r, tc), lambda i, c_ref: (0, i))
    grid_spec = pltpu.PrefetchScalarGridSpec(num_scalar_prefetch=1, grid=(c // tc,), in_specs=[b3, own, full, b3, b3],
                                             out_specs=[b3] * 4)
    return pl.pallas_call(body, grid_spec=grid_spec, out_shape=[SDS(w.shape, F32)] * 4, name=name,
                          compiler_params=_cp(("parallel",)))(c_arr, w, g_mine, g_full, m, v)


ANY = pl.BlockSpec(memory_space=pl.ANY)
ICI_KINDS = ("x", "y", "xy")


def _coords():
    return lax.axis_index("x"), lax.axis_index("y"), lax.axis_index("c")


def _peer(kind, x, y, c):
    if kind == "c":
        return (x, y, 1 - c)
    if kind == "x":
        return (1 - x, y, c)
    if kind == "y":
        return (x, 1 - y, c)
    return (1 - x, 1 - y, c)


def _chip_of(p):
    return 2 * p[0] + p[1]


def _half(rows, which):
    h = rows // 2
    return pl.ds(pl.multiple_of(which * h, 16), h)


def _remote(src, dst, send_sem, recv_sem, to):
    return pltpu.make_async_remote_copy(src_ref=src, dst_ref=dst, send_sem=send_sem, recv_sem=recv_sem,
                                        device_id=to, device_id_type=MESH)


def allgather_chips(shards, halved, *, name):
    n = len(shards)

    def body(*refs):
        parts = (refs[:n], refs[n:2 * n], refs[2 * n], refs[2 * n + 1], halved)
        _allgather_start(*parts)
        _allgather_finish(*parts)

    return pl.pallas_call(
        body, in_specs=[ANY] * n, out_specs=[ANY] * n,
        out_shape=_allgather_shapes(shards), scratch_shapes=_allgather_sems(n), name=name)(*shards)


def _allgather_shapes(shards):
    return [SDS((4,) + a.shape, a.dtype) for a in shards]


def _allgather_sems(n):
    return [pltpu.SemaphoreType.DMA((n, 6)), pltpu.SemaphoreType.DMA((n, 6))]


def _allgather_rows(ref, is_halved, which):
    r = ref.shape[0]
    return _half(r, which) if is_halved else pl.ds(0, r)


def _allgather_first(ins, outs, send_sems, recv_sems, halved):
    x, y, c = _coords()
    my_chip = 2 * x + y
    cps = []
    for w in range(len(ins)):
        rows = _allgather_rows(ins[w], halved[w], c)
        for k, kind in enumerate(ICI_KINDS):
            cps.append(_remote(ins[w].at[rows], outs[w].at[my_chip, rows], send_sems.at[w, k], recv_sems.at[w, k],
                               _peer(kind, x, y, c)))
    return cps


def _allgather_start(ins, outs, send_sems, recv_sems, halved):
    for cp in _allgather_first(ins, outs, send_sems, recv_sems, halved):
        cp.start()


def _allgather_finish(ins, outs, send_sems, recv_sems, halved):
    x, y, c = _coords()
    me = (x, y, c)
    second = []
    for w in range(len(ins)):
        for k, kind in enumerate(ICI_KINDS):
            landed = outs[w].at[_chip_of(_peer(kind, x, y, c)), _allgather_rows(ins[w], halved[w], c)]
            _remote(landed, landed, send_sems.at[w, k], recv_sems.at[w, k], me).wait_recv()
            if halved[w]:
                cp = _remote(landed, landed, send_sems.at[w, 3 + k], recv_sems.at[w, 3 + k], _peer("c", x, y, c))
                cp.start()
                second.append(cp)
    for w in range(len(ins)):
        if halved[w]:
            for k, kind in enumerate(ICI_KINDS):
                other = outs[w].at[_chip_of(_peer(kind, x, y, c)), _allgather_rows(ins[w], True, 1 - c)]
                _remote(other, other, send_sems.at[w, 3 + k], recv_sems.at[w, 3 + k], me).wait_recv()
    for cp in _allgather_first(ins, outs, send_sems, recv_sems, halved) + second:
        cp.wait_send()


def _half_of(ref, by_cols, which):
    lead = (slice(None),) * (len(ref.shape) - 2)
    if by_cols:
        h = ref.shape[-1] // 2
        return ref.at[lead + (slice(None), pl.ds(pl.multiple_of(which * h, LANES), h))]
    return ref.at[lead + (_half(ref.shape[-2], which),)]


def _half_shape(shape, by_cols):
    return shape[:-1] + (shape[-1] // 2,) if by_cols else shape[:-2] + (shape[-2] // 2, shape[-1])


def grads_to_sibling(gs, by_cols, *, name):
    n = len(gs)

    def body(*refs):
        ins, outs = refs[:n], refs[n:2 * n]
        send_sems, recv_sems = refs[2 * n:]
        x, y, c = _coords()
        cps = []
        for w in range(n):
            cp = _remote(_half_of(ins[w], by_cols[w], 1 - c), outs[w], send_sems.at[w], recv_sems.at[w], _peer("c", x, y, c))
            cp.start()
            cps.append(cp)
        for cp in cps:
            cp.wait()

    return pl.pallas_call(
        body, in_specs=[ANY] * n, out_specs=[ANY] * n,
        out_shape=[SDS(_half_shape(a.shape, bc), a.dtype) for a, bc in zip(gs, by_cols)],
        scratch_shapes=[pltpu.SemaphoreType.DMA((n,)), pltpu.SemaphoreType.DMA((n,))], name=name)(*gs)


def grads_to_chips(ps, *, name):
    n = len(ps)

    def body(*refs):
        parts = (refs[:n], refs[n:2 * n], refs[2 * n], refs[2 * n + 1])
        _to_chips_start(*parts)
        _to_chips_finish(*parts)

    return pl.pallas_call(
        body, in_specs=[ANY] * n, out_specs=[ANY] * n,
        out_shape=_to_chips_shapes(ps), scratch_shapes=_to_chips_sems(n), name=name)(*ps)


def _to_chips_shapes(ps):
    return [SDS((3,) + a.shape[1:], a.dtype) for a in ps]


def _to_chips_sems(n):
    return [pltpu.SemaphoreType.DMA((n, 3)), pltpu.SemaphoreType.DMA((n, 3))]


def _to_chips_copies(ins, outs, send_sems, recv_sems):
    x, y, c = _coords()
    cps = []
    for w in range(len(ins)):
        for k, kind in enumerate(ICI_KINDS):
            to = _peer(kind, x, y, c)
            cps.append(_remote(ins[w].at[_chip_of(to)], outs[w].at[k], send_sems.at[w, k], recv_sems.at[w, k], to))
    return cps


def _to_chips_start(ins, outs, send_sems, recv_sems):
    for cp in _to_chips_copies(ins, outs, send_sems, recv_sems):
        cp.start()


def _to_chips_finish(ins, outs, send_sems, recv_sems):
    for cp in _to_chips_copies(ins, outs, send_sems, recv_sems):
        cp.wait()


def _to_owners_shapes(ps):
    return [SDS((7, a.shape[1] // 2, a.shape[2]), a.dtype) for a in ps]


def _to_owners_sems(n):
    return [pltpu.SemaphoreType.DMA((n, 7)), pltpu.SemaphoreType.DMA((n, 7))]


def _to_owners_copies(ins, outs, send_sems, recv_sems):
    x, y, c = _coords()
    cps = []
    for w in range(len(ins)):
        rows = ins[w].shape[1]
        for k, kind in enumerate(ICI_KINDS):
            px, py, _ = _peer(kind, x, y, c)
            for h in range(2):
                cps.append(_remote(ins[w].at[2 * px + py, _half(rows, h)], outs[w].at[2 * k + c],
                                   send_sems.at[w, 2 * k + h], recv_sems.at[w, 2 * k + c], (px, py, h)))
        cps.append(_remote(ins[w].at[2 * x + y, _half(rows, 1 - c)], outs[w].at[6], send_sems.at[w, 6], recv_sems.at[w, 6],
                           _peer("c", x, y, c)))
    return cps


def _to_owners_start(ins, outs, send_sems, recv_sems):
    for cp in _to_owners_copies(ins, outs, send_sems, recv_sems):
        cp.start()


def _to_owners_finish(ins, outs, send_sems, recv_sems):
    for cp in _to_owners_copies(ins, outs, send_sems, recv_sems):
        cp.wait_send()
    for w in range(len(ins)):
        for slot in range(7):
            got = outs[w].at[slot]
            _remote(got, got, send_sems.at[w, slot], recv_sems.at[w, slot], _coords()).wait_recv()


EXCHANGES = {"to_chips": (_to_chips_shapes, _to_chips_sems, _to_chips_start, _to_chips_finish),
             "to_owners": (_to_owners_shapes, _to_owners_sems, _to_owners_start, _to_owners_finish)}


def halves_to_full(hs, by_cols, *, name):
    n = len(hs)

    def body(*refs):
        ins, outs = refs[:n], refs[n:2 * n]
        send_sems, recv_sems = refs[2 * n:]
        x, y, c = _coords()
        cps = []
        for w in range(n):
            cp = _remote(ins[w], _half_of(outs[w], by_cols[w], c), send_sems.at[w], recv_sems.at[w], _peer("c", x, y, c))
            cp.start()
            cps.append(cp)
        for cp in cps:
            cp.wait()

    return pl.pallas_call(
        body, in_specs=[ANY] * n, out_specs=[ANY] * n,
        out_shape=[SDS((a.shape[0], 2 * a.shape[1]) if bc else (2 * a.shape[0], a.shape[1]), a.dtype)
                   for a, bc in zip(hs, by_cols)],
        scratch_shapes=[pltpu.SemaphoreType.DMA((n,)), pltpu.SemaphoreType.DMA((n,))],
        name=name)(*hs)


def _row_tile(rows):
    for cand in (256, 192, 176, 128, 64, 32, 16):
        if rows % cand == 0:
            return cand
    return rows


def chip_sum(g, recv, c_arr, by_cols, *, name):
    _, r, cols = g.shape

    def body(c_ref, g_ref, r_ref, f_ref, b_ref):
        tot = g_ref[...] + r_ref[...]
        f_ref[...] = tot
        b_ref[...] = tot.astype(BF16)

    if by_cols:
        tc = 2 * LANES
        nblk = cols // 2 // tc
        shape = (4, r, cols // 2)
        blk = pl.BlockSpec((None, r, tc), lambda j, i, c_ref: (j, 0, i))
        mine = pl.BlockSpec((None, r, tc), lambda j, i, c_ref: (j, 0, c_ref[0] * nblk + i))
    else:
        tr = _row_tile(r // 2)
        nblk = r // 2 // tr
        shape = (4, r // 2, cols)
        blk = pl.BlockSpec((None, tr, cols), lambda j, i, c_ref: (j, i, 0))
        mine = pl.BlockSpec((None, tr, cols), lambda j, i, c_ref: (j, c_ref[0] * nblk + i, 0))
    grid_spec = pltpu.PrefetchScalarGridSpec(num_scalar_prefetch=1, grid=(4, nblk), in_specs=[mine, blk], out_specs=[blk, blk])
    return pl.pallas_call(body, grid_spec=grid_spec, out_shape=[SDS(shape, F32), SDS(shape, BF16)],
                          name=name, compiler_params=_cp(("parallel", "parallel")))(c_arr, g, recv)


def final_sum(pf, recv, chip_arr, *, name):
    _, h, cols = pf.shape
    tr = _row_tile(h)

    def body(chip_ref, p_ref, r_ref, o_ref):
        o_ref[...] = ((p_ref[...] + r_ref[0].astype(F32)) + r_ref[1].astype(F32)) + r_ref[2].astype(F32)

    grid_spec = pltpu.PrefetchScalarGridSpec(
        num_scalar_prefetch=1, grid=(h // tr,),
        in_specs=[pl.BlockSpec((None, tr, cols), lambda i, chip_ref: (chip_ref[0], i, 0)),
                  pl.BlockSpec((3, tr, cols), lambda i, chip_ref: (0, i, 0))],
        out_specs=pl.BlockSpec((tr, cols), lambda i, chip_ref: (i, 0)))
    return pl.pallas_call(body, grid_spec=grid_spec, out_shape=SDS((h, cols), F32), name=name,
                          compiler_params=_cp(("parallel",)))(chip_arr, pf, recv)


def owner_sum(g, recv, pos_arr, *, name):
    _, r, cols = g.shape
    h = r // 2
    tr = _row_tile(h)
    nblk = h // tr

    def body(pos_ref, g_ref, r_ref, o_ref):
        tot = g_ref[...]
        for slot in range(7):
            tot = tot + r_ref[slot].astype(F32)
        o_ref[...] = tot

    grid_spec = pltpu.PrefetchScalarGridSpec(
        num_scalar_prefetch=1, grid=(nblk,),
        in_specs=[pl.BlockSpec((None, tr, cols), lambda i, pos: (pos[0], pos[1] * nblk + i, 0)),
                  pl.BlockSpec((7, tr, cols), lambda i, pos: (0, i, 0))],
        out_specs=pl.BlockSpec((tr, cols), lambda i, pos: (i, 0)))
    return pl.pallas_call(body, grid_spec=grid_spec, out_shape=SDS((h, cols), F32), name=name,
                          compiler_params=_cp(("parallel",)))(pos_arr, g, recv)


def allreduce_small(v, *, name):
    rws, cols = v.shape

    def body(v_ref, all_ref, sum_ref, send_sems, recv_sems, local_sem):
        x, y, c = _coords()
        me, sibling = (x, y, c), (x, y, 1 - c)
        chips = [(1 - x, y), (x, 1 - y), (1 - x, 1 - y)]

        def rows(px, py, pc):
            return all_ref.at[pl.ds(pl.multiple_of((4 * px + 2 * py + pc) * rws, 8), rws), :]

        def copy(k, block, to, src=None):
            return _remote(rows(*block) if src is None else src, rows(*block), send_sems.at[k], recv_sems.at[k], to)

        mine = pltpu.make_async_copy(v_ref, rows(*me), local_sem)
        mine.start()
        first = [copy(0, me, sibling, src=v_ref)]
        first += [copy(1 + j, me, (*chip, c), src=v_ref) for j, chip in enumerate(chips)]
        for cp in first:
            cp.start()
        passed = [copy(4 + j, (*chip, c), sibling) for j, chip in enumerate(chips)]
        for j, chip in enumerate(chips):
            copy(1 + j, (*chip, c), me).wait_recv()
            passed[j].start()
        copy(0, sibling, me).wait_recv()
        for j, chip in enumerate(chips):
            copy(4 + j, (*chip, 1 - c), me).wait_recv()
        for cp in first + passed:
            cp.wait_send()
        mine.wait()
        tot = all_ref[0:rws, :]
        for dev in range(1, 8):
            tot = tot + all_ref[dev * rws:(dev + 1) * rws, :]
        sum_ref[...] = tot

    vm = pl.BlockSpec(memory_space=pltpu.VMEM)
    return pl.pallas_call(
        body, in_specs=[vm], out_specs=[vm, vm],
        out_shape=[SDS((8 * rws, cols), v.dtype), SDS((rws, cols), v.dtype)],
        scratch_shapes=[pltpu.SemaphoreType.DMA((7,)), pltpu.SemaphoreType.DMA((7,)), pltpu.SemaphoreType.DMA],
        name=name)(v)[1]


def _pack_rows(parts, rows):
    out = []
    for a, r in zip(parts, rows):
        flat = a.reshape(-1)
        flat = jnp.pad(flat, (0, r * LANES - flat.shape[0]))
        out.append(flat.reshape(r, LANES))
    return jnp.concatenate(out, axis=0)


def _unpack_rows(packed, shapes, rows):
    out, at = [], 0
    for shp, r in zip(shapes, rows):
        size = int(np.prod(shp))
        out.append(packed[at:at + r].reshape(-1)[:size].reshape(shp))
        at += r
    return out


def kernel(x, g_pre_mix, w_in, b_forget, w_o_fox, w_o_dil, w_out, g_post_mix, g_pre_ffn, w_up, conv_w, conv_b, w_down, g_post_ffn, loss_target, m_g_pre_mix, m_w_in, m_b_forget, m_w_o_fox, m_w_o_dil, m_w_out, m_g_post_mix, m_g_pre_ffn, m_w_up, m_conv_w, m_conv_b, m_w_down, m_g_post_ffn, v_g_pre_mix, v_w_in, v_b_forget, v_w_o_fox, v_w_o_dil, v_w_out, v_g_post_mix, v_g_pre_ffn, v_w_up, v_conv_w, v_conv_b, v_w_down, v_g_post_ffn):
    xi, yi, ci = _coords()
    chip = 2 * xi + yi
    c_arr = jnp.reshape(ci, (1,)).astype(jnp.int32)
    chip_arr = jnp.reshape(chip, (1,)).astype(jnp.int32)
    xs = x[0]
    target = loss_target[0]
    s, d = xs.shape
    f_half = w_down.shape[1] * 4
    cols_in = w_in.shape[2]

    big = (w_in, w_o_fox, w_o_dil, w_out, w_up, w_down)
    shards = [w[0].astype(BF16) for w in big]
    a_in, a_cw = allgather_chips([shards[0], conv_w[0]], [True, False], name="allgather_w_in")
    w_in_full = jnp.concatenate([jnp.where(chip == j, shards[0], a_in[j]) for j in range(4)], axis=1)
    cw = jnp.concatenate([jnp.where(chip == j, conv_w[0], a_cw[j]) for j in range(4)], axis=1)
    nf = N_HEADS
    e_a, e_b = 3 * ATT_W, 3 * ATT_W + nf
    wz = jnp.concatenate([w_in_full[:, :e_a], w_in_full[:, e_b:]], axis=1)
    wf = jnp.pad(w_in_full[:, e_a:e_b], ((0, 0), (0, LANES - nf)))
    cb = conv_b
    bfo = jnp.pad(b_forget, ((0, 0), (0, LANES - nf)))

    h1 = rmsnorm_fwd(xs, g_pre_mix)
    z = mm([(h1, d, 0)], [(wz, d, 0)], nt=False, out_dtype=BF16, tm=1024, tn=512, name="in_proj")
    fa = mm([(h1, d, 0)], [(wf, d, 0)], nt=False, out_dtype=F32, tm=1024, tn=LANES, name="in_proj_forget")
    q_aug, k_aug, v_aug = fox_prep(z, fa, bfo)
    ya, lse_a, *late = fox_fwd(q_aug, k_aug, v_aug, gather=shards[1:], hps=N_HEADS)
    a_of, a_od, a_out, a_up, a_down = [
        lax.dynamic_update_index_in_dim(a4, own, chip, 0) for a4, own in zip(late, shards[1:])]
    wo_a = jnp.concatenate([a_of[j] for j in range(4)], axis=1)
    wo_b = jnp.concatenate([a_od[j] for j in range(4)], axis=1)
    w_o = a_out.reshape(d, d)
    w_dn = a_down.reshape(f_half, d)
    wu_a = jnp.concatenate([a_up[0], a_up[1]], axis=1)
    wu_b = jnp.concatenate([a_up[2], a_up[3]], axis=1)
    qkv_b = rope_apply([(z, Z_QB, QK_SCALE, True), (z, Z_KB, 1.0, True), (z, Z_VB, 1.0, False)], rope_tables(s, 1.0),
                       out_dtype=F32, name="rope_fwd")
    yb, lse_b = dil_fwd_all(qkv_b)
    pa, pb, mixed = gate_mix(ya, yb, wo_a, wo_b, z)
    y1, x1, h2 = proj_norm_res(mixed, w_o, g_post_mix, xs, g_pre_ffn, name="out_proj")
    ua, ub, conv_a, conv_bh, mid = ffn_up(h2, wu_a, wu_b, cw, cb)
    dout, dy2, gg_post_ffn, sq = proj_norm_loss(mid, w_dn, g_post_ffn, x1, target, name="down_proj")
    loss = lax.psum(0.5 * sq[0, 0] / d, ("x", "y", "c"))

    dmid = mm([(dy2, d, 0)], [(w_dn, d, 0)], nt=True, out_dtype=BF16, tm=512, tn=f_half // 2, name="down_dgrad")
    dw_down, dw_down16 = wgrad((mid, f_half, 0), dy2, tk=f_half // 2, tn=1024, ts=1024, name="down_wgrad", bf16_copy=True)
    dua, dub, gc_a, gc_b = ffn_bwd(dmid, ua, ub, conv_a, conv_bh, cw)
    dx1, dy1, gg_pre_ffn, gg_post_mix = mm_norm_bwd(
        [(dua, f_half, 0), (dub, f_half, 0)], [(wu_a, f_half, 0), (wu_b, f_half, 0)],
        [(x1, g_pre_ffn, dout, F32), (y1, g_post_mix, None, BF16)], name="up_dgrad")
    dw_up = None
    for k, du in enumerate((dua, dub)):
        dw_up = wgrad((h2, d, 0), du, tk=1024, tn=f_half // 2, ts=1024, name=f"up_wgrad_{k}", chip_major=True,
                      slabs=(4, 2 * k), into=dw_up, bf16_copy=True)
    g_ffn = [(dw_up[0], dw_up[1]), (dw_down.reshape(4, f_half // 4, d), dw_down16.reshape(4, f_half // 4, d))]
    dw_out, dw_out16 = wgrad((mixed, d, 0), dy1, tk=1024, tn=1024, ts=1024, name="out_wgrad", bf16_copy=True)
    dpa, dpb, dz_g, dya, dyb, dd_a = mix_bwd(dy1, w_o, z, pa, pb, wo_a, wo_b, ya)
    by_chip_cols = lambda a: jnp.stack([a[:, j * (d // 4):(j + 1) * (d // 4)] for j in range(4)], axis=0)
    dw_of = [by_chip_cols(a) for a in wgrad((ya, ATT_W, 0), dpa, tk=ATT_W, tn=d, ts=1024, name="fox_o_wgrad", bf16_copy=True)]
    dw_od = [by_chip_cols(a) for a in wgrad((yb, ATT_W, 0), dpb, tk=ATT_W, tn=d, ts=1024, name="dil_o_wgrad", bf16_copy=True)]
    g_mix = [dw_of, dw_od, (dw_out.reshape(4, d // 4, d), dw_out16.reshape(4, d // 4, d))]
    dq_aug, dk_aug, dv_a, *got_ffn = fox_bwd(q_aug, k_aug, z, dya, lse_a, dd_a, exchange=[g[1] for g in g_ffn], kind="to_owners")
    dz_a, dfa, gg_bf = fox_post(dq_aug, dk_aug, dv_a, fa, bfo)
    dq_b, dk_b, dv_b, *got_mix = dil_bwd_all(qkv_b, dyb, lse_b, yb, exchange=[g[1] for g in g_mix], kind="to_owners")
    dz_b = rope_apply([(dq_b, 0, QK_SCALE, True), (dk_b, 0, 1.0, True), (dv_b, 0, 1.0, False)],
                      rope_tables(s, -1.0), out_dtype=BF16, name="rope_bwd")
    dwt_a = wgrad((dz_a, e_a, 0), h1, tk=e_a // 2, tn=d, ts=1024, name="in_wgrad_a")
    dwt_b = wgrad((dz_b, e_a, 0), h1, tk=e_a // 2, tn=d, ts=1024, name="in_wgrad_b")
    dwt_g = wgrad((dz_g, 2 * d, 0), h1, tk=d, tn=d, ts=1024, name="in_wgrad_g")
    dwt_f = wgrad((dfa, LANES, 0), h1, tk=LANES, tn=d, ts=1024, name="in_wgrad_f")
    dwt_full = jnp.concatenate([dwt_a, dwt_f[:nf], dwt_b, dwt_g], axis=0)
    dw_in = jnp.stack([dwt_full[j * cols_in:(j + 1) * cols_in] for j in range(4)], axis=0)
    from_sib = grads_to_sibling([dw_in], [True], name="grads_to_sibling_in")
    sum_in = chip_sum(dw_in, from_sib[0], c_arr, True, name="chip_sum_w_in")
    grad_x, gg_pre_mix, got_in = mm_norm_bwd(
        [(dz_a, e_a, 0), (dz_b, e_a, 0), (dz_g, d, 0), (dz_g, d, 1), (dfa, LANES, 0)],
        [(wz, e_a, 0), (wz, e_a, 1), (wz, d, 3), (wz, d, 4), (wf, LANES, 0)],
        [(xs, g_pre_mix, dx1, F32)], exchange=[sum_in[1]], name="in_dgrad")

    names = ("w_in", "w_o_fox", "w_o_dil", "w_out", "w_up", "w_down")
    pos_arr = jnp.concatenate([chip_arr, c_arr])
    halves = [final_sum(sum_in[0], got_in, chip_arr, name="final_sum_w_in")] + [
        owner_sum(g[0], got, pos_arr, name=f"owner_sum_{nm}") for g, got, nm in zip(g_mix + g_ffn, got_mix + got_ffn, names[1:])]
    from_half = halves_to_full(halves, [True] + [False] * 5, name="halves_to_full")
    g_big = [None] + [lax.dynamic_update_slice_in_dim(full, mine, ci * mine.shape[0], axis=0)
                      for full, mine in zip(from_half[1:], halves[1:])]
    upd_big = [adamw(w[0], g, m[0], v[0], name=f"adamw_{nm}") for w, g, m, v, nm in list(zip(
        big, g_big, (m_w_in, m_w_o_fox, m_w_o_dil, m_w_out, m_w_up, m_w_down),
        (v_w_in, v_w_o_fox, v_w_o_dil, v_w_out, v_w_up, v_w_down), names))[1:]]
    to_t = lambda a: jnp.transpose(a, (2, 0, 1))
    from_t = lambda a: jnp.transpose(a, (1, 2, 0))
    *upd_in, g_in_t = adamw_rows_view(to_t(w_in), halves[0], from_half[0], to_t(m_w_in), to_t(v_w_in), c_arr,
                                      name="adamw_w_in")

    g_cw_loc = jnp.concatenate([gc_a[0:3], gc_b[0:3]], axis=1)
    g_cb_loc = jnp.concatenate([gc_a[3:4], gc_b[3:4]], axis=1)
    small_loc = [gg_pre_mix, gg_post_mix, gg_pre_ffn, gg_post_ffn, g_cb_loc, gg_bf[:, :nf], g_cw_loc]
    red_rows = (8, 8, 8, 8, 48, 8, 136)
    red = allreduce_small(_pack_rows(small_loc, red_rows), name="allreduce_small")
    g_pm, g_qm, g_pf, g_qf, g_cb, g_bf, g_cw_full = _unpack_rows(red, [a.shape for a in small_loc], red_rows)
    cols_cw = conv_w.shape[2]
    g_cw = lax.dynamic_slice_in_dim(g_cw_full, chip * cols_cw, cols_cw, axis=1)
    small_w = (g_pre_mix, g_post_mix, g_pre_ffn, g_post_ffn, conv_b, b_forget, conv_w[0])
    small_m = (m_g_pre_mix, m_g_post_mix, m_g_pre_ffn, m_g_post_ffn, m_conv_b, m_b_forget, m_conv_w[0])
    small_v = (v_g_pre_mix, v_g_post_mix, v_g_pre_ffn, v_g_post_ffn, v_conv_b, v_b_forget, v_conv_w[0])
    small_g = (g_pm, g_qm, g_pf, g_qf, g_cb, g_bf, g_cw)
    ad_rows = (8, 8, 8, 8, 48, 8, 40)
    packed = [_pack_rows(t, ad_rows) for t in (small_w, small_g, small_m, small_v)]
    upd_small = [_unpack_rows(o, [a.shape for a in small_w], ad_rows) for o in adamw(*packed, name="adamw_small")]

    order = ("g_pre_mix", "w_in", "b_forget", "w_o_fox", "w_o_dil", "w_out", "g_post_mix", "g_pre_ffn", "w_up", "conv_w",
             "conv_b", "w_down", "g_post_ffn")
    small_names = ("g_pre_mix", "g_post_mix", "g_pre_ffn", "g_post_ffn", "conv_b", "b_forget", "conv_w")
    grads, deltas, new_ms, new_vs = {}, {}, {}, {}
    grads["w_in"] = from_t(g_in_t)
    deltas["w_in"], new_ms["w_in"], new_vs["w_in"] = (from_t(a) for a in upd_in)
    for k, nm in enumerate(names[1:]):
        grads[nm] = g_big[k + 1][None]
        deltas[nm], new_ms[nm], new_vs[nm] = (a[None] for a in upd_big[k])
    for k, nm in enumerate(small_names):
        lead = (lambda a: a[None]) if nm == "conv_w" else (lambda a: a)
        grads[nm] = lead(small_g[k])
        deltas[nm], new_ms[nm], new_vs[nm] = (lead(upd_small[j][k]) for j in range(3))
    return (loss, grad_x[None], *[grads[nm] for nm in order], *[deltas[nm] for nm in order],
            *[new_ms[nm] for nm in order], *[new_vs[nm] for nm in order])
```

```python
import functools
import math

import numpy as np
import jax
import jax.numpy as jnp
from jax import lax
from jax.experimental import pallas as pl
from jax.experimental.pallas import tpu as pltpu

F32 = jnp.float32
BF16 = jnp.bfloat16
SDS = jax.ShapeDtypeStruct
MESH = pl.DeviceIdType.MESH

HEAD_DIM = 64
N_HEADS = 8
LANES = 128
ATT_W = N_HEADS * HEAD_DIM
DIL_PATTERNS = ((128, 1), (512, 4), (2048, 16))
DIL_BLK = 128
ROPE_DIM = HEAD_DIM // 4
ROPE_THETA = 500000.0
RMS_EPS = 1e-6
NEG = -1e30
QK_SCALE = 1.0 / math.sqrt(HEAD_DIM)
ADAM_LR, ADAM_B1, ADAM_B2, ADAM_EPS, ADAM_WD, ADAM_STEP = 0.001, 0.9, 0.999, 1e-08, 0.01, 10
VMEM_LIMIT = 56 * 1024 * 1024

Z_QA, Z_KA, Z_VA, Z_QB, Z_KB, Z_VB = 0, 1, 2, 3, 4, 5
Z_W = 5120


def _cp(sem):
    return pltpu.CompilerParams(dimension_semantics=sem, vmem_limit_bytes=VMEM_LIMIT)


def _nt(a, b):
    return lax.dot_general(a, b, (((1,), (1,)), ((), ())), preferred_element_type=F32)


def _tn(a, b):
    return lax.dot_general(a, b, (((0,), (0,)), ((), ())), preferred_element_type=F32)


def _nn(a, b):
    return jnp.dot(a, b, preferred_element_type=F32)


def _lane(shape):
    return lax.broadcasted_iota(jnp.int32, shape, 1)


def _row(shape):
    return lax.broadcasted_iota(jnp.int32, shape, 0)


def rmsnorm_fwd(x, g, *, tm=512):
    s, d = x.shape

    def body(x_ref, g_ref, h_ref):
        xv = x_ref[...]
        inv = lax.rsqrt(jnp.mean(xv * xv, axis=-1, keepdims=True) + RMS_EPS)
        h_ref[...] = (xv * inv * g_ref[...]).astype(h_ref.dtype)

    return pl.pallas_call(
        body, grid=(s // tm,),
        in_specs=[pl.BlockSpec((tm, d), lambda i: (i, 0)), pl.BlockSpec((1, d), lambda i: (0, 0))],
        out_specs=pl.BlockSpec((tm, d), lambda i: (i, 0)),
        out_shape=SDS((s, d), BF16), name="rmsnorm_fwd", compiler_params=_cp(("parallel",)))(x, g)


def mm(a_views, b_views, *, nt, out_dtype, tm, tn, name):
    n_p = len(a_views)
    m = a_views[0][0].shape[0]
    n = b_views[0][0].shape[0] if nt else b_views[0][0].shape[1]

    def body(*refs):
        o_ref = refs[-1]
        acc = None
        for p in range(n_p):
            av = refs[p][...].astype(BF16)
            bv = refs[n_p + p][...].astype(BF16)
            dv = _nt(av, bv) if nt else _nn(av, bv)
            acc = dv if acc is None else acc + dv
        o_ref[...] = acc.astype(o_ref.dtype)

    in_specs = []
    for arr, w, blk in a_views:
        in_specs.append(pl.BlockSpec((tm, w), functools.partial(lambda i, j, blk: (i, blk), blk=blk)))
    for arr, w, blk in b_views:
        if nt:
            in_specs.append(pl.BlockSpec((tn, w), functools.partial(lambda i, j, blk: (j, blk), blk=blk)))
        else:
            in_specs.append(pl.BlockSpec((w, tn), lambda i, j: (0, j)))
    return pl.pallas_call(
        body, grid=(m // tm, n // tn), in_specs=in_specs,
        out_specs=pl.BlockSpec((tm, tn), lambda i, j: (i, j)),
        out_shape=SDS((m, n), out_dtype), name=name,
        compiler_params=_cp(("parallel", "parallel")))(*[a[0] for a in a_views], *[b[0] for b in b_views])


def wgrad(a_view, g, *, tk, tn, ts, name, chip_major=False, slabs=None, into=None, bf16_copy=False):
    arr, ka, blk = a_view
    s, n = g.shape
    ns = s // ts
    total, first = slabs if slabs else (n // tn, 0)
    n_into = 0 if into is None else (2 if bf16_copy else 1)

    def body(a_ref, g_ref, *rest):
        o_ref = rest[n_into]

        @pl.when(pl.program_id(2) == 0)
        def _():
            o_ref[...] = jnp.zeros_like(o_ref)

        o_ref[...] += _tn(a_ref[...].astype(BF16), g_ref[...].astype(BF16))
        if bf16_copy:
            @pl.when(pl.program_id(2) == ns - 1)
            def _():
                rest[n_into + 1][...] = o_ref[...].astype(BF16)

    if chip_major:
        out_spec = pl.BlockSpec((None, tk, tn), lambda i, j, k: (first + j, i, 0))
        shape = (total, ka, tn)
    else:
        out_spec = pl.BlockSpec((tk, tn), lambda i, j, k: (i, j))
        shape = (ka, n)
    in_specs = [pl.BlockSpec((ts, tk), lambda i, j, k: (k, blk * (ka // tk) + i)),
                pl.BlockSpec((ts, tn), lambda i, j, k: (k, j))]
    args = [arr, g]
    if into is not None:
        earlier = list(into) if bf16_copy else [into]
        in_specs += [pl.BlockSpec(memory_space=pl.ANY)] * len(earlier)
        args += earlier
    out = pl.pallas_call(
        body, grid=(ka // tk, n // tn, ns), in_specs=in_specs,
        out_specs=[out_spec, out_spec] if bf16_copy else out_spec,
        out_shape=[SDS(shape, F32), SDS(shape, BF16)] if bf16_copy else SDS(shape, F32), name=name,
        input_output_aliases={2 + k: k for k in range(n_into)},
        compiler_params=_cp(("parallel", "parallel", "arbitrary")))(*args)
    return out


def _norm_bwd_rows(dh, xh, inv, g):
    dxh = dh * g
    dx = inv * (dxh - xh * jnp.mean(dxh * xh, axis=-1, keepdims=True))
    return dx, jnp.sum((dh * xh).reshape(dh.shape[0] // 8, 8, dh.shape[1]), axis=0)


def proj_norm_res(a, w, g, xres, g_next, *, tm=512, name):
    s, k = a.shape
    d = w.shape[1]

    def body(a_ref, w_ref, g_ref, x_ref, gn_ref, y_ref, o_ref, h_ref):
        y = _nn(a_ref[...], w_ref[...])
        inv = lax.rsqrt(jnp.mean(y * y, axis=-1, keepdims=True) + RMS_EPS)
        xn = x_ref[...] + y * inv * g_ref[...]
        y_ref[...] = y
        o_ref[...] = xn
        inv_n = lax.rsqrt(jnp.mean(xn * xn, axis=-1, keepdims=True) + RMS_EPS)
        h_ref[...] = (xn * inv_n * gn_ref[...]).astype(h_ref.dtype)

    row = pl.BlockSpec((tm, d), lambda i: (i, 0))
    vec = pl.BlockSpec((1, d), lambda i: (0, 0))
    return pl.pallas_call(
        body, grid=(s // tm,),
        in_specs=[pl.BlockSpec((tm, k), lambda i: (i, 0)), pl.BlockSpec((k, d), lambda i: (0, 0)), vec, row, vec],
        out_specs=[row, row, row], out_shape=[SDS((s, d), F32), SDS((s, d), F32), SDS((s, d), BF16)], name=name,
        compiler_params=_cp(("parallel",)))(a, w, g, xres, g_next)


def proj_norm_loss(a, w, g, xres, target, *, tm=512, name):
    s, k = a.shape
    d = w.shape[1]
    n = s // tm

    def body(a_ref, w_ref, g_ref, x_ref, t_ref, do_ref, dy_ref, dg_ref, l_ref, acc):
        i = pl.program_id(0)

        @pl.when(i == 0)
        def _():
            acc[...] = jnp.zeros_like(acc)
            l_ref[...] = jnp.zeros_like(l_ref)

        y = _nn(a_ref[...], w_ref[...])
        inv = lax.rsqrt(jnp.mean(y * y, axis=-1, keepdims=True) + RMS_EPS)
        yh = y * inv
        err = x_ref[...] + yh * g_ref[...] - t_ref[...]
        dout = err * (1.0 / d)
        do_ref[...] = dout
        l_ref[...] += jnp.sum(jnp.sum(err * err, axis=1, keepdims=True), axis=0, keepdims=True)
        dy, part = _norm_bwd_rows(dout, yh, inv, g_ref[...])
        dy_ref[...] = dy.astype(dy_ref.dtype)
        acc[...] += part

        @pl.when(i == n - 1)
        def _():
            dg_ref[...] = jnp.sum(acc[...], axis=0, keepdims=True)

    row = pl.BlockSpec((tm, d), lambda i: (i, 0))
    vec = pl.BlockSpec((1, d), lambda i: (0, 0))
    return pl.pallas_call(
        body, grid=(n,),
        in_specs=[pl.BlockSpec((tm, k), lambda i: (i, 0)), pl.BlockSpec((k, d), lambda i: (0, 0)), vec, row, row],
        out_specs=[row, row, vec, pl.BlockSpec((1, 1), lambda i: (0, 0))],
        out_shape=[SDS((s, d), F32), SDS((s, d), BF16), SDS((1, d), F32), SDS((1, 1), F32)],
        scratch_shapes=[pltpu.VMEM((8, d), F32)], name=name, compiler_params=_cp(("arbitrary",)))(a, w, g, xres, target)


def mm_norm_bwd(a_views, b_views, stages, exchange=(), *, tm=256, name):
    n_p, n_s, ne = len(a_views), len(stages), len(exchange)
    s = a_views[0][0].shape[0]
    d = b_views[0][0].shape[0]
    n = s // tm
    has_res = [st[2] is not None for st in stages]

    def body(*refs):
        a_refs, b_refs = refs[:n_p], refs[n_p:2 * n_p]
        at = 2 * n_p
        st_refs = []
        for k in range(n_s):
            cnt = 3 if has_res[k] else 2
            st_refs.append(refs[at:at + cnt])
            at += cnt
        e_ins = refs[at:at + ne]
        at += ne
        dx_refs, dg_refs = refs[at:at + n_s], refs[at + n_s:at + 2 * n_s]
        at += 2 * n_s
        e_outs = refs[at:at + ne]
        at += ne
        accs = refs[at:at + n_s]
        comm = (e_ins, e_outs) + tuple(refs[at + n_s:])
        i = pl.program_id(0)

        @pl.when(i == 0)
        def _():
            for acc in accs:
                acc[...] = jnp.zeros_like(acc)
            if ne:
                _to_chips_start(*comm)

        dh = None
        for p in range(n_p):
            part = _nt(a_refs[p][...].astype(BF16), b_refs[p][...].astype(BF16))
            dh = part if dh is None else dh + part
        for k in range(n_s):
            xv = st_refs[k][0][...]
            inv = lax.rsqrt(jnp.mean(xv * xv, axis=-1, keepdims=True) + RMS_EPS)
            dx, part = _norm_bwd_rows(dh, xv * inv, inv, st_refs[k][1][...])
            if has_res[k]:
                dx = dx + st_refs[k][2][...]
            dx_refs[k][...] = dx.astype(dx_refs[k].dtype)
            accs[k][...] += part
            dh = dx

        @pl.when(i == n - 1)
        def _():
            for k in range(n_s):
                dg_refs[k][...] = jnp.sum(accs[k][...], axis=0, keepdims=True)
            if ne:
                _to_chips_finish(*comm)

    row = pl.BlockSpec((tm, d), lambda i: (i, 0))
    vec = pl.BlockSpec((1, d), lambda i: (0, 0))
    in_specs, args = [], []
    for arr, w, blk in a_views:
        in_specs.append(pl.BlockSpec((tm, w), functools.partial(lambda i, blk: (i, blk), blk=blk)))
        args.append(arr)
    for arr, w, blk in b_views:
        in_specs.append(pl.BlockSpec((d, w), functools.partial(lambda i, blk: (0, blk), blk=blk)))
        args.append(arr)
    for x, g, res, _ in stages:
        in_specs += [row, vec] + ([row] if res is not None else [])
        args += [x, g] + ([res] if res is not None else [])
    return pl.pallas_call(
        body, grid=(n,), in_specs=in_specs + [ANY] * ne,
        out_specs=[row] * n_s + [vec] * n_s + [ANY] * ne,
        out_shape=[SDS((s, d), st[3]) for st in stages] + [SDS((1, d), F32)] * n_s + _to_chips_shapes(exchange),
        scratch_shapes=[pltpu.VMEM((8, d), F32)] * n_s + (_to_chips_sems(ne) if ne else []), name=name,
        compiler_params=_cp(("arbitrary",)))(*args, *exchange)


def _split3(v):
    hi = v.astype(BF16).astype(F32)
    r = v - hi
    mid = r.astype(BF16).astype(F32)
    lo = (r - mid).astype(BF16).astype(F32)
    return hi, mid, lo


def _tri(n, upper):
    r = np.arange(n)
    m = (r[:, None] <= r[None, :]) if upper else (r[:, None] >= r[None, :])
    return jnp.asarray(m.astype(np.float32))


def fox_prep(z, fa, bfo, *, tb=512):
    s = z.shape[0]
    n = s // tb

    def body(q_ref, k_ref, v_ref, fa_ref, b_ref, tri_ref, qa_ref, ka_ref, va_ref, carry):
        @pl.when(pl.program_id(0) == 0)
        def _():
            carry[...] = jnp.zeros_like(carry)

        xv = fa_ref[...] + b_ref[...]
        logf = jnp.minimum(xv, 0.0) - jnp.log(1.0 + jnp.exp(-jnp.abs(xv)))
        csum = jnp.dot(tri_ref[...], logf, preferred_element_type=F32, precision=lax.Precision.HIGHEST) + carry[0:1, :]
        carry[0:1, :] = csum[tb - 1:tb, :]
        lane = _lane((tb, LANES))
        for h in range(N_HEADS):
            hi, mid, lo = _split3(csum[:, h:h + 1])
            pair = (h // 2) * LANES
            qv = q_ref[:, pair:pair + LANES].astype(F32)
            kv = k_ref[:, pair:pair + LANES].astype(F32)
            vv = v_ref[:, pair:pair + LANES].astype(F32)
            if h % 2:
                qv = pltpu.roll(qv, 64, axis=1)
                kv = pltpu.roll(kv, 64, axis=1)
                vv = pltpu.roll(vv, 64, axis=1)
            va_ref[:, h * LANES:(h + 1) * LANES] = jnp.where(lane < 64, vv, jnp.where(lane == 64, 1.0, 0.0)).astype(BF16)
            one = jnp.where((lane >= 67) & (lane < 70), 1.0, 0.0)
            q_x = jnp.where(lane == 64, hi, jnp.where(lane == 65, mid, jnp.where(lane == 66, lo, one)))
            one = jnp.where((lane >= 64) & (lane < 67), 1.0, 0.0)
            k_x = jnp.where(lane == 67, -hi, jnp.where(lane == 68, -mid, jnp.where(lane == 69, -lo, one)))
            qa_ref[:, h * LANES:(h + 1) * LANES] = jnp.where(lane < 64, qv * QK_SCALE, q_x).astype(BF16)
            ka_ref[:, h * LANES:(h + 1) * LANES] = jnp.where(lane < 64, kv, k_x).astype(BF16)

    return pl.pallas_call(
        body, grid=(n,),
        in_specs=[pl.BlockSpec((tb, ATT_W), lambda i: (i, Z_QA)), pl.BlockSpec((tb, ATT_W), lambda i: (i, Z_KA)),
                  pl.BlockSpec((tb, ATT_W), lambda i: (i, Z_VA)),
                  pl.BlockSpec((tb, LANES), lambda i: (i, 0)), pl.BlockSpec((1, LANES), lambda i: (0, 0)),
                  pl.BlockSpec((tb, tb), lambda i: (0, 0))],
        out_specs=[pl.BlockSpec((tb, N_HEADS * LANES), lambda i: (i, 0))] * 3,
        out_shape=[SDS((s, N_HEADS * LANES), BF16)] * 3,
        scratch_shapes=[pltpu.VMEM((8, LANES), F32)],
        name="fox_prep", compiler_params=_cp(("arbitrary",)))(z, z, z, fa, bfo, _tri(tb, False))


def _causal_pairs(n, k_major):
    if k_major:
        pairs = [(qi, kj) for kj in range(n) for qi in range(kj, n)]
    else:
        pairs = [(qi, kj) for qi in range(n) for kj in range(qi + 1)]
    return (jnp.asarray([p[0] for p in pairs], jnp.int32), jnp.asarray([p[1] for p in pairs], jnp.int32), len(pairs))


def fox_fwd(q_aug, k_aug, v_aug, gather=(), *, t=512, hps=4):
    s = v_aug.shape[0]
    qi_arr, kj_arr, n_pairs = _causal_pairs(s // t, False)
    ng = len(gather)
    n_groups = N_HEADS // hps

    def body(qi_ref, kj_ref, q_ref, k_ref, v_ref, *rest):
        g_ins, (o_ref, lse_ref), g_outs = rest[:ng], rest[ng:ng + 2], rest[ng + 2:2 * ng + 2]
        m_scr, acc_scr = rest[2 * ng + 2:2 * ng + 4]
        comm = (g_ins, g_outs) + tuple(rest[2 * ng + 4:]) + ([True] * ng,)
        step = pl.program_id(1)
        qi = qi_ref[step]
        kj = kj_ref[step]
        if ng:
            @pl.when((pl.program_id(0) == 0) & (step == 0))
            def _():
                _allgather_start(*comm)

        @pl.when(kj == 0)
        def _():
            m_scr[...] = jnp.full_like(m_scr, NEG)
            acc_scr[...] = jnp.zeros_like(acc_scr)

        def update(masked):
            for i in range(hps):
                sc = _nt(q_ref[:, i * LANES:(i + 1) * LANES], k_ref[:, i * LANES:(i + 1) * LANES])
                if masked:
                    sc = jnp.where(_row((t, t)) >= _lane((t, t)), sc, NEG)
                m_prev = m_scr[i]
                m_new = jnp.maximum(m_prev, jnp.max(sc, axis=-1, keepdims=True))
                p = jnp.exp((sc - jnp.tile(m_new, (1, t // LANES))).astype(BF16))
                acc_scr[i] = jnp.exp(m_prev - m_new) * acc_scr[i] + _nn(p, v_ref[:, i * LANES:(i + 1) * LANES])
                m_scr[i] = m_new

        @pl.when(kj < qi)
        def _():
            update(False)

        @pl.when(kj == qi)
        def _():
            update(True)
            lane = _lane((t, LANES))
            for pr in range(hps // 2):
                den = [acc_scr[2 * pr + i][:, 64:65] for i in range(2)]
                o_ref[:, pr * LANES:(pr + 1) * LANES] = jnp.where(
                    lane < 64, acc_scr[2 * pr] / den[0], pltpu.roll(acc_scr[2 * pr + 1] / den[1], 64, axis=1)).astype(o_ref.dtype)
                lse_ref[:, pr * LANES:(pr + 1) * LANES] = jnp.where(
                    lane < 64, m_scr[2 * pr] + jnp.log(den[0]), m_scr[2 * pr + 1] + jnp.log(den[1]))

        if ng:
            @pl.when((pl.program_id(0) == n_groups - 1) & (step == n_pairs - 1))
            def _():
                _allgather_finish(*comm)

    wide = hps * LANES
    grid_spec = pltpu.PrefetchScalarGridSpec(
        num_scalar_prefetch=2, grid=(n_groups, n_pairs),
        in_specs=[pl.BlockSpec((t, wide), lambda hg, st, qi, kj: (qi[st], hg)),
                  pl.BlockSpec((t, wide), lambda hg, st, qi, kj: (kj[st], hg)),
                  pl.BlockSpec((t, wide), lambda hg, st, qi, kj: (kj[st], hg))] + [ANY] * ng,
        out_specs=[pl.BlockSpec((t, wide // 2), lambda hg, st, qi, kj: (qi[st], hg))] * 2 + [ANY] * ng,
        scratch_shapes=[pltpu.VMEM((hps, t, LANES), F32)] * 2 + (_allgather_sems(ng) if ng else []))
    return pl.pallas_call(
        body, grid_spec=grid_spec, out_shape=[SDS((s, ATT_W), BF16), SDS((s, ATT_W), F32)] + _allgather_shapes(gather),
        name="fox_fwd", compiler_params=_cp(("arbitrary", "arbitrary")))(qi_arr, kj_arr, q_aug, k_aug, v_aug, *gather)


def fox_bwd(q_aug, k_aug, z, dy, lse, dd, exchange=(), kind="to_chips", *, t=512, hps=4):
    s = z.shape[0]
    qi_arr, kj_arr, n_pairs = _causal_pairs(s // t, True)
    ne = len(exchange)
    n_groups = N_HEADS // hps
    x_shapes, x_sems, x_start, x_finish = EXCHANGES[kind]

    def body(qi_ref, kj_ref, q_ref, k_ref, v_ref, do_ref, lse_ref, dd_ref, *rest):
        e_ins, (dq_ref, dk_ref, dv_ref), e_outs = rest[:ne], rest[ne:ne + 3], rest[ne + 3:2 * ne + 3]
        comm = (e_ins, e_outs) + tuple(rest[2 * ne + 3:])
        step = pl.program_id(1)
        qi = qi_ref[step]
        kj = kj_ref[step]
        if ne:
            @pl.when((pl.program_id(0) == 0) & (step == 0))
            def _():
                x_start(*comm)

        @pl.when(step == 0)
        def _():
            dq_ref[...] = jnp.zeros_like(dq_ref)

        @pl.when(qi == kj)
        def _():
            dk_ref[...] = jnp.zeros_like(dk_ref)
            dv_ref[...] = jnp.zeros_like(dv_ref)

        def update(masked):
            lane = _lane((t, LANES))
            rows = pl.ds(pl.multiple_of(qi * t, t), t)
            for pr in range(hps // 2):
                pair = slice(pr * LANES, (pr + 1) * LANES)
                dov = do_ref[:, pair]
                dv_new = None
                for i in range(2):
                    head = (lane < 64) if i == 0 else (lane >= 64)
                    own = slice((2 * pr + i) * LANES, (2 * pr + i + 1) * LANES)
                    col = slice(pr * LANES + i * 64, pr * LANES + i * 64 + 1)
                    qv = q_ref[:, own]
                    kv = k_ref[:, own]
                    sc = _nt(qv, kv)
                    if masked:
                        sc = jnp.where(_row((t, t)) >= _lane((t, t)), sc, NEG)
                    p = jnp.exp(sc - lse_ref[:, col])
                    dp = _nt(jnp.where(head, dov, jnp.zeros_like(dov)), v_ref[:, pair])
                    ds = (p * (dp - dd_ref[:, col])).astype(BF16)
                    dq_ref[rows, own] += _nn(ds, kv)
                    dk_ref[:, own] += _tn(ds, qv)
                    dvi = _tn(p.astype(BF16), dov)
                    dv_new = dvi if dv_new is None else jnp.where(head, dvi, dv_new)
                dv_ref[:, pair] += dv_new

        @pl.when(kj < qi)
        def _():
            update(False)

        @pl.when(kj == qi)
        def _():
            update(True)

        if ne:
            @pl.when((pl.program_id(0) == n_groups - 1) & (step == n_pairs - 1))
            def _():
                x_finish(*comm)

    wide, half = hps * LANES, hps // 2 * LANES
    v_blk = Z_VA * ATT_W // half
    grid_spec = pltpu.PrefetchScalarGridSpec(
        num_scalar_prefetch=2, grid=(n_groups, n_pairs),
        in_specs=[pl.BlockSpec((t, wide), lambda hg, st, qi, kj: (qi[st], hg)),
                  pl.BlockSpec((t, wide), lambda hg, st, qi, kj: (kj[st], hg)),
                  pl.BlockSpec((t, half), lambda hg, st, qi, kj: (kj[st], v_blk + hg)),
                  pl.BlockSpec((t, half), lambda hg, st, qi, kj: (qi[st], hg)),
                  pl.BlockSpec((t, half), lambda hg, st, qi, kj: (qi[st], hg)),
                  pl.BlockSpec((t, half), lambda hg, st, qi, kj: (qi[st], hg))] + [ANY] * ne,
        out_specs=[pl.BlockSpec((s, wide), lambda hg, st, qi, kj: (0, hg)),
                   pl.BlockSpec((t, wide), lambda hg, st, qi, kj: (kj[st], hg)),
                   pl.BlockSpec((t, half), lambda hg, st, qi, kj: (kj[st], hg))] + [ANY] * ne,
        scratch_shapes=x_sems(ne) if ne else [])
    return pl.pallas_call(
        body, grid_spec=grid_spec,
        out_shape=[SDS((s, N_HEADS * LANES), F32), SDS((s, N_HEADS * LANES), F32), SDS((s, ATT_W), F32)]
        + x_shapes(exchange),
        name="fox_bwd", compiler_params=_cp(("arbitrary", "arbitrary")))(qi_arr, kj_arr, q_aug, k_aug, z, dy, lse, dd, *exchange)


def fox_post(dq_aug, dk_aug, dv, fa, bfo, *, tb=512):
    s = dv.shape[0]
    n = s // tb

    def body(dq_ref, dk_ref, dv_ref, fa_ref, b_ref, tri_ref, dz_ref, dfa_ref, gb_ref, carry, acc):
        i = pl.program_id(0)

        @pl.when(i == 0)
        def _():
            carry[...] = jnp.zeros_like(carry)
            acc[...] = jnp.zeros_like(acc)

        lane = _lane((tb, LANES))
        d_f = jnp.zeros((tb, LANES), F32)
        for h in range(N_HEADS):
            col = dq_ref[:, h * LANES + 64:h * LANES + 65] - dk_ref[:, h * LANES + 67:h * LANES + 68]
            d_f = jnp.where(lane == h, col, d_f)
        suffix = jnp.dot(tri_ref[...], d_f, preferred_element_type=F32, precision=lax.Precision.HIGHEST) + carry[0:1, :]
        carry[0:1, :] = suffix[0:1, :]
        xv = fa_ref[...] + b_ref[...]
        dx = suffix * (1.0 / (1.0 + jnp.exp(xv)))
        dfa_ref[...] = dx.astype(dfa_ref.dtype)
        acc[...] += jnp.sum(dx.reshape(tb // 8, 8, LANES), axis=0)
        for hp in range(4):
            for src, off, scale in ((dq_ref, 0, QK_SCALE), (dk_ref, ATT_W, 1.0)):
                even = src[:, (2 * hp) * LANES:(2 * hp + 1) * LANES]
                odd = pltpu.roll(src[:, (2 * hp + 1) * LANES:(2 * hp + 2) * LANES], 64, axis=1)
                dz_ref[:, off + hp * LANES:off + (hp + 1) * LANES] = (jnp.where(lane < 64, even, odd) * scale).astype(BF16)
        dz_ref[:, 2 * ATT_W:3 * ATT_W] = dv_ref[...].astype(BF16)

        @pl.when(i == n - 1)
        def _():
            gb_ref[...] = jnp.sum(acc[...], axis=0, keepdims=True)

    rev = lambda i: (n - 1 - i, 0)
    return pl.pallas_call(
        body, grid=(n,),
        in_specs=[pl.BlockSpec((tb, N_HEADS * LANES), rev), pl.BlockSpec((tb, N_HEADS * LANES), rev),
                  pl.BlockSpec((tb, ATT_W), rev), pl.BlockSpec((tb, LANES), rev),
                  pl.BlockSpec((1, LANES), lambda i: (0, 0)), pl.BlockSpec((tb, tb), lambda i: (0, 0))],
        out_specs=[pl.BlockSpec((tb, 3 * ATT_W), rev), pl.BlockSpec((tb, LANES), rev),
                   pl.BlockSpec((1, LANES), lambda i: (0, 0))],
        out_shape=[SDS((s, 3 * ATT_W), BF16), SDS((s, LANES), BF16), SDS((1, LANES), F32)],
        scratch_shapes=[pltpu.VMEM((8, LANES), F32), pltpu.VMEM((8, LANES), F32)],
        name="fox_post", compiler_params=_cp(("arbitrary",)))(dq_aug, dk_aug, dv, fa, bfo, _tri(tb, True))


def rope_tables(s, sign):
    half = ROPE_DIM // 2
    inv_freq = ROPE_THETA ** (-jnp.arange(half, dtype=F32) * 2.0 / ROPE_DIM)
    ang = jnp.arange(s, dtype=F32)[:, None] * inv_freq[None, :]
    l64 = np.arange(LANES) % HEAD_DIM
    cos = jnp.tile(jnp.cos(ang), (1, LANES // half))
    sin = jnp.tile(jnp.sin(ang), (1, LANES // half)) * sign
    first = jnp.asarray(l64 < half)[None, :]
    second = jnp.asarray((l64 >= half) & (l64 < ROPE_DIM))[None, :]
    return (jnp.where(first | second, cos, 1.0), jnp.where(first, -sin, 0.0), jnp.where(second, sin, 0.0))


def rope_apply(items, tabs, *, out_dtype, tm=512, name):
    s = items[0][0].shape[0]
    n_i = len(items)

    def body(*refs):
        c_ref, sn_ref, sp_ref = refs[n_i:n_i + 3]
        o_ref = refs[-1]
        for j, (_, _, scale, rotate) in enumerate(items):
            for b in range(4):
                xv = refs[j][:, b * LANES:(b + 1) * LANES].astype(F32)
                if rotate:
                    xv = xv * c_ref[...] + pltpu.roll(xv, LANES - 8, axis=1) * sn_ref[...] + pltpu.roll(xv, 8, axis=1) * sp_ref[...]
                o_ref[:, j * ATT_W + b * LANES:j * ATT_W + (b + 1) * LANES] = (xv * scale).astype(o_ref.dtype)

    in_specs = [pl.BlockSpec((tm, ATT_W), functools.partial(lambda i, blk: (i, blk), blk=it[1])) for it in items]
    in_specs += [pl.BlockSpec((tm, LANES), lambda i: (i, 0))] * 3
    return pl.pallas_call(
        body, grid=(s // tm,), in_specs=in_specs, out_specs=pl.BlockSpec((tm, n_i * ATT_W), lambda i: (i, 0)),
        out_shape=SDS((s, n_i * ATT_W), out_dtype), name=name, compiler_params=_cp(("parallel",)))(*[it[0] for it in items], *tabs)


def _dil_rows(base, r):
    if r == 1:
        return pl.ds(pl.multiple_of(base, DIL_BLK), DIL_BLK)
    return pl.ds(base, DIL_BLK, stride=r)


def _dil_block(idx, r, nb):
    shift = nb.bit_length() - 1
    rho = idx >> shift
    n = idx & (nb - 1)
    base = rho + n * (r * DIL_BLK)
    return _dil_rows(base, r), _dil_rows(jnp.maximum(base - r * DIL_BLK, rho), r), n > 0


def _cat(a, b):
    return jnp.concatenate([a, b], axis=0)


def _two_heads(v, first_head):
    zero = jnp.zeros_like(v)
    return _cat(jnp.where(first_head, v, zero), jnp.where(first_head, zero, v))


def _dil_bands():
    b = DIL_BLK
    q = _row((2 * b, 2 * b)) & (b - 1)
    col = _lane((2 * b, 2 * b))
    return (col < b) & (col >= q), (col >= b) & (col - b <= q)


def dil_fwd_all(qkv, *, unroll=8):
    s = qkv.shape[0]
    b = DIL_BLK
    n_blk = s // b

    def body(q_ref, k_ref, v_ref, o_ref, l_ref):
        first_head = _lane((b, LANES)) < 64
        band_prev, band_cur = _dil_bands()
        for g, (_, r) in enumerate(DIL_PATTERNS):
            nb = n_blk // r

            def group(it, carry, g=g, r=r, nb=nb):
                loaded = []
                kc = vc = None
                for u in range(unroll):
                    rows_c, rows_p, has_prev = _dil_block(it * unroll + u, r, nb)
                    if u % min(nb, unroll):
                        kp, vp = kc, vc
                    else:
                        kp, vp = k_ref[rows_p, :].astype(BF16), v_ref[rows_p, :].astype(BF16)
                    kc, vc = k_ref[rows_c, :].astype(BF16), v_ref[rows_c, :].astype(BF16)
                    state = (o_ref[rows_c, :], l_ref[rows_c, :]) if g else None
                    loaded.append((rows_c, has_prev, [q_ref[rows_c, :].astype(BF16), kp, kc, vp, vc], state))
                done = []
                for rows_c, has_prev, (qv, kp, kc, vp, vc), state in loaded:
                    sc = jnp.where(band_cur | (band_prev & has_prev), _nt(_two_heads(qv, first_head), _cat(kp, kc)), NEG)
                    m = jnp.max(sc, axis=-1, keepdims=True)
                    p = jnp.exp(sc - m)
                    den = jnp.sum(p, axis=-1, keepdims=True)
                    both = _nn(p.astype(BF16), _cat(vp, vc)) / den
                    lse2 = m + jnp.log(den)
                    ov = jnp.where(first_head, both[:b], both[b:])
                    lse = jnp.where(first_head, lse2[:b], lse2[b:])
                    if state is not None:
                        m2 = jnp.maximum(state[1], lse)
                        wp = jnp.exp(state[1] - m2)
                        wn = jnp.exp(lse - m2)
                        ov = (wp * state[0] + wn * ov) / (wp + wn)
                        lse = m2 + jnp.log(wp + wn)
                    done.append((rows_c, ov, lse))
                for rows_c, ov, lse in done:
                    o_ref[rows_c, :] = ov
                    l_ref[rows_c, :] = lse
                return carry

            lax.fori_loop(0, n_blk // unroll, group, 0)

    col_blk = lambda k: pl.BlockSpec((s, LANES), lambda hp: (0, 4 * k + hp))
    out = pl.BlockSpec((s, LANES), lambda hp: (0, hp))
    return pl.pallas_call(
        body, grid=(4,), in_specs=[col_blk(0), col_blk(1), col_blk(2)], out_specs=[out, out],
        out_shape=[SDS((s, ATT_W), F32)] * 2, name="dil_fwd", compiler_params=_cp(("parallel",)))(qkv, qkv, qkv)


def dil_bwd_all(qkv, dy, lse, y, exchange=(), kind="to_chips", *, unroll=8):
    s = qkv.shape[0]
    b = DIL_BLK
    n_blk = s // b
    ne = len(exchange)
    x_shapes, x_sems, x_start, x_finish = EXCHANGES[kind]

    def body(q_ref, k_ref, v_ref, do_ref, l_ref, y_ref, *rest):
        e_ins, (dq_ref, dk_ref, dv_ref), e_outs = rest[:ne], rest[ne:ne + 3], rest[ne + 3:2 * ne + 3]
        comm = (e_ins, e_outs) + tuple(rest[2 * ne + 3:])
        if ne:
            @pl.when(pl.program_id(0) == 0)
            def _():
                x_start(*comm)

        dq_ref[...] = jnp.zeros_like(dq_ref)
        dk_ref[...] = jnp.zeros_like(dk_ref)
        dv_ref[...] = jnp.zeros_like(dv_ref)
        first_head = _lane((b, LANES)) < 64
        band_prev, band_cur = _dil_bands()
        for _, r in DIL_PATTERNS:
            nb = n_blk // r

            def group(it, carry, r=r, nb=nb):
                loaded = []
                kc = vc = None
                for u in range(unroll):
                    rows_c, rows_p, has_prev = _dil_block(it * unroll + u, r, nb)
                    if u % min(nb, unroll):
                        kp, vp = kc, vc
                    else:
                        kp, vp = k_ref[rows_p, :].astype(BF16), v_ref[rows_p, :].astype(BF16)
                    kc, vc = k_ref[rows_c, :].astype(BF16), v_ref[rows_c, :].astype(BF16)
                    vals = [q_ref[rows_c, :].astype(BF16), kp, kc, vp, vc, do_ref[rows_c, :], l_ref[rows_c, :], y_ref[rows_c, :]]
                    loaded.append((rows_c, rows_p, has_prev, vals))
                done = []
                for rows_c, rows_p, has_prev, (qv, kp, kc, vp, vc, dof, lv, yv) in loaded:
                    q2 = _two_heads(qv, first_head)
                    do2 = _two_heads(dof.astype(BF16), first_head)
                    kcat, vcat = _cat(kp, kc), _cat(vp, vc)
                    lse2 = _cat(lv[:, 0:1], lv[:, 64:65])
                    dd2 = jnp.sum(_two_heads(dof * yv, first_head), axis=-1, keepdims=True)
                    p = jnp.exp(jnp.where(band_cur | (band_prev & has_prev), _nt(q2, kcat), NEG) - lse2)
                    ds = (p * (_nt(do2, vcat) - dd2)).astype(BF16)
                    dq2 = _nn(ds, kcat)
                    dkcat = _tn(ds, q2)
                    dvcat = _tn(p.astype(BF16), do2)
                    done.append((rows_c, rows_p, (jnp.where(first_head, dq2[:b], dq2[b:]), dkcat[:b], dkcat[b:],
                                                  dvcat[:b], dvcat[b:])))
                for rows_c, rows_p, (dq, dk_p, dk_c, dv_p, dv_c) in done:
                    dq_ref[rows_c, :] += dq
                    dk_ref[rows_p, :] += dk_p
                    dk_ref[rows_c, :] += dk_c
                    dv_ref[rows_p, :] += dv_p
                    dv_ref[rows_c, :] += dv_c
                return carry

            lax.fori_loop(0, n_blk // unroll, group, 0)

        if ne:
            @pl.when(pl.program_id(0) == 3)
            def _():
                x_finish(*comm)

    col_blk = lambda k: pl.BlockSpec((s, LANES), lambda hp: (0, 4 * k + hp))
    nat = pl.BlockSpec((s, LANES), lambda hp: (0, hp))
    return pl.pallas_call(
        body, grid=(4,), in_specs=[col_blk(0), col_blk(1), col_blk(2), nat, nat, nat] + [ANY] * ne,
        out_specs=[nat, nat, nat] + [ANY] * ne, out_shape=[SDS((s, ATT_W), F32)] * 3 + x_shapes(exchange),
        scratch_shapes=x_sems(ne) if ne else [], name="dil_bwd",
        compiler_params=_cp(("arbitrary",)))(qkv, qkv, qkv, dy, lse, y, *exchange)


def _sigmoid(v):
    return 1.0 / (1.0 + jnp.exp(-v))


def gate_mix(ya, yb, wa, wb, z, *, tm=512, tn=512):
    s = ya.shape[0]
    d = wa.shape[1]
    ga_blk = 3 * ATT_W * 2 // tn
    gb_blk = ga_blk + d // tn

    def body(ya_ref, yb_ref, wa_ref, wb_ref, ga_ref, gb_ref, pa_ref, pb_ref, mx_ref):
        pa = _nn(ya_ref[...], wa_ref[...])
        pb = _nn(yb_ref[...].astype(BF16), wb_ref[...])
        pa_ref[...] = pa.astype(BF16)
        pb_ref[...] = pb.astype(BF16)
        mx_ref[...] = (_sigmoid(ga_ref[...].astype(F32)) * pa + _sigmoid(gb_ref[...].astype(F32)) * pb).astype(BF16)

    out = pl.BlockSpec((tm, tn), lambda i, j: (i, j))
    return pl.pallas_call(
        body, grid=(s // tm, d // tn),
        in_specs=[pl.BlockSpec((tm, ATT_W), lambda i, j: (i, 0)), pl.BlockSpec((tm, ATT_W), lambda i, j: (i, 0)),
                  pl.BlockSpec((ATT_W, tn), lambda i, j: (0, j)), pl.BlockSpec((ATT_W, tn), lambda i, j: (0, j)),
                  pl.BlockSpec((tm, tn), lambda i, j: (i, ga_blk + j)), pl.BlockSpec((tm, tn), lambda i, j: (i, gb_blk + j))],
        out_specs=[out, out, out], out_shape=[SDS((s, d), BF16)] * 3, name="gate_mix",
        compiler_params=_cp(("parallel", "parallel")))(ya, yb, wa, wb, z, z)


def mix_bwd(dy, w_o, z, pa, pb, wo_a, wo_b, ya, *, tm=256):
    s, d = dy.shape

    def body(dy_ref, wo_ref, ga_ref, gb_ref, pa_ref, pb_ref, wa_ref, wb_ref, ya_ref,
             dpa_ref, dpb_ref, dg_ref, dya_ref, dyb_ref, dd_ref):
        dm = _nt(dy_ref[...], wo_ref[...])
        sa = _sigmoid(ga_ref[...].astype(F32))
        sb = _sigmoid(gb_ref[...].astype(F32))
        dpa = (dm * sa).astype(BF16)
        dpb = (dm * sb).astype(BF16)
        dpa_ref[...] = dpa
        dpb_ref[...] = dpb
        dg_ref[:, 0:d] = (dm * pa_ref[...].astype(F32) * sa * (1.0 - sa)).astype(BF16)
        dg_ref[:, d:2 * d] = (dm * pb_ref[...].astype(F32) * sb * (1.0 - sb)).astype(BF16)
        dya = _nt(dpa, wa_ref[...]).astype(BF16)
        dya_ref[...] = dya
        dyb_ref[...] = _nt(dpb, wb_ref[...])
        lane = _lane((tm, LANES))
        for pr in range(ATT_W // LANES):
            pair = slice(pr * LANES, (pr + 1) * LANES)
            prod = dya[:, pair].astype(F32) * ya_ref[:, pair].astype(F32)
            lo = jnp.sum(jnp.where(lane < 64, prod, 0.0), axis=-1, keepdims=True)
            hi = jnp.sum(jnp.where(lane >= 64, prod, 0.0), axis=-1, keepdims=True)
            dd_ref[:, pair] = jnp.where(lane < 64, lo, hi)

    row = pl.BlockSpec((tm, d), lambda i: (i, 0))
    att = pl.BlockSpec((tm, ATT_W), lambda i: (i, 0))
    whole = lambda a: pl.BlockSpec(a.shape, lambda i: (0, 0))
    return pl.pallas_call(
        body, grid=(s // tm,),
        in_specs=[row, whole(w_o), pl.BlockSpec((tm, d), lambda i: (i, 3)), pl.BlockSpec((tm, d), lambda i: (i, 4)), row, row,
                  whole(wo_a), whole(wo_b), att],
        out_specs=[row, row, pl.BlockSpec((tm, 2 * d), lambda i: (i, 0)), att, att, att],
        out_shape=[SDS((s, d), BF16), SDS((s, d), BF16), SDS((s, 2 * d), BF16), SDS((s, ATT_W), BF16),
                   SDS((s, ATT_W), F32), SDS((s, ATT_W), F32)], name="mix_bwd",
        compiler_params=_cp(("parallel",)))(dy, w_o, z, z, pa, pb, wo_a, wo_b, ya)


GELU_C = math.sqrt(2.0 / math.pi)


def _gelu_parts(a):
    a2 = a * a
    th = jnp.tanh(a * (GELU_C + (GELU_C * 0.044715) * a2))
    half = 0.5 * a
    gelu = half + half * th
    dgelu = (0.5 + 0.5 * th) + half * (1.0 - th * th) * (GELU_C + (3.0 * GELU_C * 0.044715) * a2)
    return gelu, dgelu


def _causal_taps(u, before):
    row = _row(u.shape)
    r1 = jnp.where(row == 0, before[7:8, :], pltpu.roll(u, 1, axis=0))
    r2 = jnp.where(row == 0, before[6:7, :], jnp.where(row == 1, before[7:8, :], pltpu.roll(u, 2, axis=0)))
    return r1, r2


def ffn_up(h, wa, wb, cw, cb, *, tm=1024, tn=256):
    s, d = h.shape
    f = wa.shape[1]
    nj = f // tn

    def body(h_ref, wa_ref, wb_ref, cwa_ref, cwb_ref, cba_ref, cbb_ref, ua_ref, ub_ref, ca_ref, cbo_ref, m_ref, carry):
        @pl.when(pl.program_id(1) == 0)
        def _():
            carry[...] = jnp.zeros_like(carry)

        conv = []
        for k, (w_ref, cw_ref, cb_ref, u_ref, c_ref) in enumerate(((wa_ref, cwa_ref, cba_ref, ua_ref, ca_ref),
                                                                   (wb_ref, cwb_ref, cbb_ref, ub_ref, cbo_ref))):
            u16 = _nn(h_ref[...], w_ref[...]).astype(BF16)
            u_ref[...] = u16
            u = u16.astype(F32)
            r1, r2 = _causal_taps(u, carry[k])
            carry[k] = u[tm - 8:tm, :]
            c16 = (cw_ref[0:1, :] * r2 + cw_ref[1:2, :] * r1 + cw_ref[2:3, :] * u + cb_ref[...]).astype(BF16)
            c_ref[...] = c16
            conv.append(c16.astype(F32))
        m_ref[...] = (_gelu_parts(conv[0])[0] * conv[1]).astype(BF16)

    out = pl.BlockSpec((tm, tn), lambda j, i: (i, j))
    return pl.pallas_call(
        body, grid=(nj, s // tm),
        in_specs=[pl.BlockSpec((tm, d), lambda j, i: (i, 0)),
                  pl.BlockSpec((d, tn), lambda j, i: (0, j)), pl.BlockSpec((d, tn), lambda j, i: (0, j)),
                  pl.BlockSpec((3, tn), lambda j, i: (0, j)), pl.BlockSpec((3, tn), lambda j, i: (0, nj + j)),
                  pl.BlockSpec((1, tn), lambda j, i: (0, j)), pl.BlockSpec((1, tn), lambda j, i: (0, nj + j))],
        out_specs=[out] * 5, out_shape=[SDS((s, f), BF16)] * 5,
        scratch_shapes=[pltpu.VMEM((2, 8, tn), F32)], name="ffn_up",
        compiler_params=_cp(("parallel", "arbitrary")))(h, wa, wb, cw, cw, cb, cb)


def ffn_bwd(dm, ua, ub, ca, cbo, cw, *, tm=1024, tn=256):
    s, f = dm.shape
    nj = f // tn
    ni = s // tm

    def body(dm_ref, ua_ref, ub_ref, ca_ref, cbo_ref, cwa_ref, cwb_ref, dua_ref, dub_ref, ga_ref, gb_ref, carry):
        @pl.when(pl.program_id(1) == 0)
        def _():
            carry[...] = jnp.zeros_like(carry)
            ga_ref[...] = jnp.zeros_like(ga_ref)
            gb_ref[...] = jnp.zeros_like(gb_ref)

        row = _row((tm, tn))
        dmv = dm_ref[...].astype(F32)
        gelu, dgelu = _gelu_parts(ca_ref[...].astype(F32))
        dcs = (dmv * cbo_ref[...].astype(F32) * dgelu, dmv * gelu)
        for k, (dc, u_ref, cw_ref, du_ref, g_ref) in enumerate(((dcs[0], ua_ref, cwa_ref, dua_ref, ga_ref),
                                                                (dcs[1], ub_ref, cwb_ref, dub_ref, gb_ref))):
            u = u_ref[...].astype(F32)
            after = carry[k]
            n1 = jnp.where(row == tm - 1, after[0:1, :], pltpu.roll(dc, tm - 1, axis=0))
            n2 = jnp.where(row == tm - 2, after[0:1, :], jnp.where(row == tm - 1, after[1:2, :], pltpu.roll(dc, tm - 2, axis=0)))
            g_ref[0:1, :] += jnp.sum(n2 * u, axis=0, keepdims=True)
            g_ref[1:2, :] += jnp.sum(n1 * u, axis=0, keepdims=True)
            g_ref[2:3, :] += jnp.sum(dc * u, axis=0, keepdims=True)
            g_ref[3:4, :] += jnp.sum(dc, axis=0, keepdims=True)
            du_ref[...] = (cw_ref[2:3, :] * dc + cw_ref[1:2, :] * n1 + cw_ref[0:1, :] * n2).astype(BF16)
            carry[k] = dc[0:8, :]

    tile = pl.BlockSpec((tm, tn), lambda j, i: (ni - 1 - i, j))
    gspec = pl.BlockSpec((8, tn), lambda j, i: (0, j))
    return pl.pallas_call(
        body, grid=(nj, ni),
        in_specs=[tile] * 5 + [pl.BlockSpec((3, tn), lambda j, i: (0, j)), pl.BlockSpec((3, tn), lambda j, i: (0, nj + j))],
        out_specs=[tile, tile, gspec, gspec],
        out_shape=[SDS((s, f), BF16), SDS((s, f), BF16), SDS((8, f), F32), SDS((8, f), F32)],
        scratch_shapes=[pltpu.VMEM((2, 8, tn), F32)], name="ffn_bwd",
        compiler_params=_cp(("parallel", "arbitrary")))(dm, ua, ub, ca, cbo, cw, cw)


def adamw(w, g, m, v, *, name, tr=None):
    r = w.shape[0]
    rest = w.shape[1:]
    if tr is None:
        tr = r
        for cand in (256, 128, 64, 32, 16, 8):
            if r % cand == 0:
                tr = cand
                break

    def body(w_ref, g_ref, m_ref, v_ref, d_ref, nm_ref, nv_ref):
        gv = g_ref[...]
        mn = ADAM_B1 * m_ref[...] + (1.0 - ADAM_B1) * gv
        vn = ADAM_B2 * v_ref[...] + (1.0 - ADAM_B2) * (gv * gv)
        m_hat = mn / (1.0 - ADAM_B1 ** ADAM_STEP)
        v_hat = vn / (1.0 - ADAM_B2 ** ADAM_STEP)
        d_ref[...] = -ADAM_LR * (m_hat / (jnp.sqrt(v_hat) + ADAM_EPS) + ADAM_WD * w_ref[...])
        nm_ref[...] = mn
        nv_ref[...] = vn

    blk = pl.BlockSpec((tr,) + rest, lambda i: (i,) + (0,) * len(rest))
    return pl.pallas_call(body, grid=(r // tr,), in_specs=[blk] * 4, out_specs=[blk] * 3, out_shape=[SDS(w.shape, F32)] * 3,
                          name=name, compiler_params=_cp(("parallel",)))(w, g, m, v)


def adamw_rows_view(w, g_mine, g_full, m, v, c_arr, *, name, tc=256):
    r, _, c = w.shape
    per_half = c // 2 // tc

    def body(c_ref, w_ref, gm_ref, gf_ref, m_ref, v_ref, d_ref, nm_ref, nv_ref, go_ref):
        mine = (pl.program_id(0) >> (per_half.bit_length() - 1)) == c_ref[0]
        gv = jnp.where(mine, gm_ref[...], gf_ref[...])[:, None, :]
        mn = ADAM_B1 * m_ref[...] + (1.0 - ADAM_B1) * gv
        vn = ADAM_B2 * v_ref[...] + (1.0 - ADAM_B2) * (gv * gv)
        m_hat = mn / (1.0 - ADAM_B1 ** ADAM_STEP)
        v_hat = vn / (1.0 - ADAM_B2 ** ADAM_STEP)
        d_ref[...] = -ADAM_LR * (m_hat / (jnp.sqrt(v_hat) + ADAM_EPS) + ADAM_WD * w_ref[...])
        nm_ref[...] = mn
        nv_ref[...] = vn
        go_ref[...] = gv

    b3 = pl.BlockSpec((r, 1, tc), lambda i, c_ref: (0, 0, i))
    own = pl.BlockSpec((r, tc), lambda i, c_ref: (0, jnp.clip(i - c_ref[0] * per_half, 0, per_half - 1)))
    full = pl.BlockSpec((r, tc), lambda i, c_ref: (0, i))
    grid_spec = pltpu.PrefetchScalarGridSpec(num_scalar_prefetch=1, grid=(c // tc,), in_specs=[b3, own, full, b3, b3],
                                             out_specs=[b3] * 4)
    return pl.pallas_call(body, grid_spec=grid_spec, out_shape=[SDS(w.shape, F32)] * 4, name=name,
                          compiler_params=_cp(("parallel",)))(c_arr, w, g_mine, g_full, m, v)


ANY = pl.BlockSpec(memory_space=pl.ANY)
ICI_KINDS = ("x", "y", "xy")


def _coords():
    return lax.axis_index("x"), lax.axis_index("y"), lax.axis_index("c")


def _peer(kind, x, y, c):
    if kind == "c":
        return (x, y, 1 - c)
    if kind == "x":
        return (1 - x, y, c)
    if kind == "y":
        return (x, 1 - y, c)
    return (1 - x, 1 - y, c)


def _chip_of(p):
    return 2 * p[0] + p[1]


def _half(rows, which):
    h = rows // 2
    return pl.ds(pl.multiple_of(which * h, 16), h)


def _remote(src, dst, send_sem, recv_sem, to):
    return pltpu.make_async_remote_copy(src_ref=src, dst_ref=dst, send_sem=send_sem, recv_sem=recv_sem,
                                        device_id=to, device_id_type=MESH)


def allgather_chips(shards, halved, *, name):
    n = len(shards)

    def body(*refs):
        parts = (refs[:n], refs[n:2 * n], refs[2 * n], refs[2 * n + 1], halved)
        _allgather_start(*parts)
        _allgather_finish(*parts)

    return pl.pallas_call(
        body, in_specs=[ANY] * n, out_specs=[ANY] * n,
        out_shape=_allgather_shapes(shards), scratch_shapes=_allgather_sems(n), name=name)(*shards)


def _allgather_shapes(shards):
    return [SDS((4,) + a.shape, a.dtype) for a in shards]


def _allgather_sems(n):
    return [pltpu.SemaphoreType.DMA((n, 6)), pltpu.SemaphoreType.DMA((n, 6))]


def _allgather_rows(ref, is_halved, which):
    r = ref.shape[0]
    return _half(r, which) if is_halved else pl.ds(0, r)


def _allgather_first(ins, outs, send_sems, recv_sems, halved):
    x, y, c = _coords()
    my_chip = 2 * x + y
    cps = []
    for w in range(len(ins)):
        rows = _allgather_rows(ins[w], halved[w], c)
        for k, kind in enumerate(ICI_KINDS):
            cps.append(_remote(ins[w].at[rows], outs[w].at[my_chip, rows], send_sems.at[w, k], recv_sems.at[w, k],
                               _peer(kind, x, y, c)))
    return cps


def _allgather_start(ins, outs, send_sems, recv_sems, halved):
    for cp in _allgather_first(ins, outs, send_sems, recv_sems, halved):
        cp.start()


def _allgather_finish(ins, outs, send_sems, recv_sems, halved):
    x, y, c = _coords()
    me = (x, y, c)
    second = []
    for w in range(len(ins)):
        for k, kind in enumerate(ICI_KINDS):
            landed = outs[w].at[_chip_of(_peer(kind, x, y, c)), _allgather_rows(ins[w], halved[w], c)]
            _remote(landed, landed, send_sems.at[w, k], recv_sems.at[w, k], me).wait_recv()
            if halved[w]:
                cp = _remote(landed, landed, send_sems.at[w, 3 + k], recv_sems.at[w, 3 + k], _peer("c", x, y, c))
                cp.start()
                second.append(cp)
    for w in range(len(ins)):
        if halved[w]:
            for k, kind in enumerate(ICI_KINDS):
                other = outs[w].at[_chip_of(_peer(kind, x, y, c)), _allgather_rows(ins[w], True, 1 - c)]
                _remote(other, other, send_sems.at[w, 3 + k], recv_sems.at[w, 3 + k], me).wait_recv()
    for cp in _allgather_first(ins, outs, send_sems, recv_sems, halved) + second:
        cp.wait_send()


def _half_of(ref, by_cols, which):
    lead = (slice(None),) * (len(ref.shape) - 2)
    if by_cols:
        h = ref.shape[-1] // 2
        return ref.at[lead + (slice(None), pl.ds(pl.multiple_of(which * h, LANES), h))]
    return ref.at[lead + (_half(ref.shape[-2], which),)]


def _half_shape(shape, by_cols):
    return shape[:-1] + (shape[-1] // 2,) if by_cols else shape[:-2] + (shape[-2] // 2, shape[-1])


def grads_to_sibling(gs, by_cols, *, name):
    n = len(gs)

    def body(*refs):
        ins, outs = refs[:n], refs[n:2 * n]
        send_sems, recv_sems = refs[2 * n:]
        x, y, c = _coords()
        cps = []
        for w in range(n):
            cp = _remote(_half_of(ins[w], by_cols[w], 1 - c), outs[w], send_sems.at[w], recv_sems.at[w], _peer("c", x, y, c))
            cp.start()
            cps.append(cp)
        for cp in cps:
            cp.wait()

    return pl.pallas_call(
        body, in_specs=[ANY] * n, out_specs=[ANY] * n,
        out_shape=[SDS(_half_shape(a.shape, bc), a.dtype) for a, bc in zip(gs, by_cols)],
        scratch_shapes=[pltpu.SemaphoreType.DMA((n,)), pltpu.SemaphoreType.DMA((n,))], name=name)(*gs)


def _to_chips_shapes(ps):
    return [SDS((3,) + a.shape[1:], a.dtype) for a in ps]


def _to_chips_sems(n):
    return [pltpu.SemaphoreType.DMA((n, 3)), pltpu.SemaphoreType.DMA((n, 3))]


def _to_chips_copies(ins, outs, send_sems, recv_sems):
    x, y, c = _coords()
    cps = []
    for w in range(len(ins)):
        for k, kind in enumerate(ICI_KINDS):
            to = _peer(kind, x, y, c)
            cps.append(_remote(ins[w].at[_chip_of(to)], outs[w].at[k], send_sems.at[w, k], recv_sems.at[w, k], to))
    return cps


def _to_chips_start(ins, outs, send_sems, recv_sems):
    for cp in _to_chips_copies(ins, outs, send_sems, recv_sems):
        cp.start()


def _to_chips_finish(ins, outs, send_sems, recv_sems):
    for cp in _to_chips_copies(ins, outs, send_sems, recv_sems):
        cp.wait()


def _to_owners_shapes(ps):
    return [SDS((7, a.shape[1] // 2, a.shape[2]), a.dtype) for a in ps]


def _to_owners_sems(n):
    return [pltpu.SemaphoreType.DMA((n, 7)), pltpu.SemaphoreType.DMA((n, 7))]


def _to_owners_copies(ins, outs, send_sems, recv_sems):
    x, y, c = _coords()
    cps = []
    for w in range(len(ins)):
        rows = ins[w].shape[1]
        for k, kind in enumerate(ICI_KINDS):
            px, py, _ = _peer(kind, x, y, c)
            for h in range(2):
                cps.append(_remote(ins[w].at[2 * px + py, _half(rows, h)], outs[w].at[2 * k + c],
                                   send_sems.at[w, 2 * k + h], recv_sems.at[w, 2 * k + c], (px, py, h)))
        cps.append(_remote(ins[w].at[2 * x + y, _half(rows, 1 - c)], outs[w].at[6], send_sems.at[w, 6], recv_sems.at[w, 6],
                           _peer("c", x, y, c)))
    return cps


def _to_owners_start(ins, outs, send_sems, recv_sems):
    for cp in _to_owners_copies(ins, outs, send_sems, recv_sems):
        cp.start()


def _to_owners_finish(ins, outs, send_sems, recv_sems):
    for cp in _to_owners_copies(ins, outs, send_sems, recv_sems):
        cp.wait_send()
    for w in range(len(ins)):
        for slot in range(7):
            got = outs[w].at[slot]
            _remote(got, got, send_sems.at[w, slot], recv_sems.at[w, slot], _coords()).wait_recv()


EXCHANGES = {"to_chips": (_to_chips_shapes, _to_chips_sems, _to_chips_start, _to_chips_finish),
             "to_owners": (_to_owners_shapes, _to_owners_sems, _to_owners_start, _to_owners_finish)}


def halves_to_full(hs, by_cols, *, name):
    n = len(hs)

    def body(*refs):
        ins, outs = refs[:n], refs[n:2 * n]
        send_sems, recv_sems = refs[2 * n:]
        x, y, c = _coords()
        cps = []
        for w in range(n):
            cp = _remote(ins[w], _half_of(outs[w], by_cols[w], c), send_sems.at[w], recv_sems.at[w], _peer("c", x, y, c))
            cp.start()
            cps.append(cp)
        for cp in cps:
            cp.wait()

    return pl.pallas_call(
        body, in_specs=[ANY] * n, out_specs=[ANY] * n,
        out_shape=[SDS((a.shape[0], 2 * a.shape[1]) if bc else (2 * a.shape[0], a.shape[1]), a.dtype)
                   for a, bc in zip(hs, by_cols)],
        scratch_shapes=[pltpu.SemaphoreType.DMA((n,)), pltpu.SemaphoreType.DMA((n,))],
        name=name)(*hs)


def _row_tile(rows):
    for cand in (256, 192, 176, 128, 64, 32, 16):
        if rows % cand == 0:
            return cand
    return rows


def chip_sum(g, recv, c_arr, by_cols, *, name):
    _, r, cols = g.shape

    def body(c_ref, g_ref, r_ref, f_ref, b_ref):
        tot = g_ref[...] + r_ref[...]
        f_ref[...] = tot
        b_ref[...] = tot.astype(BF16)

    if by_cols:
        tc = 2 * LANES
        nblk = cols // 2 // tc
        shape = (4, r, cols // 2)
        blk = pl.BlockSpec((None, r, tc), lambda j, i, c_ref: (j, 0, i))
        mine = pl.BlockSpec((None, r, tc), lambda j, i, c_ref: (j, 0, c_ref[0] * nblk + i))
    else:
        tr = _row_tile(r // 2)
        nblk = r // 2 // tr
        shape = (4, r // 2, cols)
        blk = pl.BlockSpec((None, tr, cols), lambda j, i, c_ref: (j, i, 0))
        mine = pl.BlockSpec((None, tr, cols), lambda j, i, c_ref: (j, c_ref[0] * nblk + i, 0))
    grid_spec = pltpu.PrefetchScalarGridSpec(num_scalar_prefetch=1, grid=(4, nblk), in_specs=[mine, blk], out_specs=[blk, blk])
    return pl.pallas_call(body, grid_spec=grid_spec, out_shape=[SDS(shape, F32), SDS(shape, BF16)],
                          name=name, compiler_params=_cp(("parallel", "parallel")))(c_arr, g, recv)


def final_sum(pf, recv, chip_arr, *, name):
    _, h, cols = pf.shape
    tr = _row_tile(h)

    def body(chip_ref, p_ref, r_ref, o_ref):
        o_ref[...] = ((p_ref[...] + r_ref[0].astype(F32)) + r_ref[1].astype(F32)) + r_ref[2].astype(F32)

    grid_spec = pltpu.PrefetchScalarGridSpec(
        num_scalar_prefetch=1, grid=(h // tr,),
        in_specs=[pl.BlockSpec((None, tr, cols), lambda i, chip_ref: (chip_ref[0], i, 0)),
                  pl.BlockSpec((3, tr, cols), lambda i, chip_ref: (0, i, 0))],
        out_specs=pl.BlockSpec((tr, cols), lambda i, chip_ref: (i, 0)))
    return pl.pallas_call(body, grid_spec=grid_spec, out_shape=SDS((h, cols), F32), name=name,
                          compiler_params=_cp(("parallel",)))(chip_arr, pf, recv)


def owner_sum(g, recv, pos_arr, *, name):
    _, r, cols = g.shape
    h = r // 2
    tr = _row_tile(h)
    nblk = h // tr

    def body(pos_ref, g_ref, r_ref, o_ref):
        tot = g_ref[...]
        for slot in range(7):
            tot = tot + r_ref[slot].astype(F32)
        o_ref[...] = tot

    grid_spec = pltpu.PrefetchScalarGridSpec(
        num_scalar_prefetch=1, grid=(nblk,),
        in_specs=[pl.BlockSpec((None, tr, cols), lambda i, pos: (pos[0], pos[1] * nblk + i, 0)),
                  pl.BlockSpec((7, tr, cols), lambda i, pos: (0, i, 0))],
        out_specs=pl.BlockSpec((tr, cols), lambda i, pos: (i, 0)))
    return pl.pallas_call(body, grid_spec=grid_spec, out_shape=SDS((h, cols), F32), name=name,
                          compiler_params=_cp(("parallel",)))(pos_arr, g, recv)


def allreduce_small(v, *, name):
    rws, cols = v.shape

    def body(v_ref, all_ref, sum_ref, send_sems, recv_sems, local_sem):
        x, y, c = _coords()
        me, sibling = (x, y, c), (x, y, 1 - c)
        chips = [(1 - x, y), (x, 1 - y), (1 - x, 1 - y)]

        def rows(px, py, pc):
            return all_ref.at[pl.ds(pl.multiple_of((4 * px + 2 * py + pc) * rws, 8), rws), :]

        def copy(k, block, to, src=None):
            return _remote(rows(*block) if src is None else src, rows(*block), send_sems.at[k], recv_sems.at[k], to)

        mine = pltpu.make_async_copy(v_ref, rows(*me), local_sem)
        mine.start()
        first = [copy(0, me, sibling, src=v_ref)]
        first += [copy(1 + j, me, (*chip, c), src=v_ref) for j, chip in enumerate(chips)]
        for cp in first:
            cp.start()
        passed = [copy(4 + j, (*chip, c), sibling) for j, chip in enumerate(chips)]
        for j, chip in enumerate(chips):
            copy(1 + j, (*chip, c), me).wait_recv()
            passed[j].start()
        copy(0, sibling, me).wait_recv()
        for j, chip in enumerate(chips):
            copy(4 + j, (*chip, 1 - c), me).wait_recv()
        for cp in first + passed:
            cp.wait_send()
        mine.wait()
        tot = all_ref[0:rws, :]
        for dev in range(1, 8):
            tot = tot + all_ref[dev * rws:(dev + 1) * rws, :]
        sum_ref[...] = tot

    vm = pl.BlockSpec(memory_space=pltpu.VMEM)
    return pl.pallas_call(
        body, in_specs=[vm], out_specs=[vm, vm],
        out_shape=[SDS((8 * rws, cols), v.dtype), SDS((rws, cols), v.dtype)],
        scratch_shapes=[pltpu.SemaphoreType.DMA((7,)), pltpu.SemaphoreType.DMA((7,)), pltpu.SemaphoreType.DMA],
        name=name)(v)[1]


def _pack_rows(parts, rows):
    out = []
    for a, r in zip(parts, rows):
        flat = a.reshape(-1)
        flat = jnp.pad(flat, (0, r * LANES - flat.shape[0]))
        out.append(flat.reshape(r, LANES))
    return jnp.concatenate(out, axis=0)


def _unpack_rows(packed, shapes, rows):
    out, at = [], 0
    for shp, r in zip(shapes, rows):
        size = int(np.prod(shp))
        out.append(packed[at:at + r].reshape(-1)[:size].reshape(shp))
        at += r
    return out


def kernel(x, g_pre_mix, w_in, b_forget, w_o_fox, w_o_dil, w_out, g_post_mix, g_pre_ffn, w_up, conv_w, conv_b, w_down, g_post_ffn, loss_target, m_g_pre_mix, m_w_in, m_b_forget, m_w_o_fox, m_w_o_dil, m_w_out, m_g_post_mix, m_g_pre_ffn, m_w_up, m_conv_w, m_conv_b, m_w_down, m_g_post_ffn, v_g_pre_mix, v_w_in, v_b_forget, v_w_o_fox, v_w_o_dil, v_w_out, v_g_post_mix, v_g_pre_ffn, v_w_up, v_conv_w, v_conv_b, v_w_down, v_g_post_ffn):
    xi, yi, ci = _coords()
    chip = 2 * xi + yi
    c_arr = jnp.reshape(ci, (1,)).astype(jnp.int32)
    chip_arr = jnp.reshape(chip, (1,)).astype(jnp.int32)
    xs = x[0]
    target = loss_target[0]
    s, d = xs.shape
    f_half = w_down.shape[1] * 4
    cols_in = w_in.shape[2]

    big = (w_in, w_o_fox, w_o_dil, w_out, w_up, w_down)
    shards = [w[0].astype(BF16) for w in big]
    a_in, a_cw = allgather_chips([shards[0], conv_w[0]], [True, False], name="allgather_w_in")
    w_in_full = jnp.concatenate([jnp.where(chip == j, shards[0], a_in[j]) for j in range(4)], axis=1)
    cw = jnp.concatenate([jnp.where(chip == j, conv_w[0], a_cw[j]) for j in range(4)], axis=1)
    nf = N_HEADS
    e_a, e_b = 3 * ATT_W, 3 * ATT_W + nf
    wz = jnp.concatenate([w_in_full[:, :e_a], w_in_full[:, e_b:]], axis=1)
    wf = jnp.pad(w_in_full[:, e_a:e_b], ((0, 0), (0, LANES - nf)))
    cb = conv_b
    bfo = jnp.pad(b_forget, ((0, 0), (0, LANES - nf)))

    h1 = rmsnorm_fwd(xs, g_pre_mix)
    z = mm([(h1, d, 0)], [(wz, d, 0)], nt=False, out_dtype=BF16, tm=1024, tn=512, name="in_proj")
    fa = mm([(h1, d, 0)], [(wf, d, 0)], nt=False, out_dtype=F32, tm=1024, tn=LANES, name="in_proj_forget")
    q_aug, k_aug, v_aug = fox_prep(z, fa, bfo)
    ya, lse_a, *late = fox_fwd(q_aug, k_aug, v_aug, gather=shards[1:], hps=N_HEADS)
    a_of, a_od, a_out, a_up, a_down = [
        lax.dynamic_update_index_in_dim(a4, own, chip, 0) for a4, own in zip(late, shards[1:])]
    wo_a = jnp.concatenate([a_of[j] for j in range(4)], axis=1)
    wo_b = jnp.concatenate([a_od[j] for j in range(4)], axis=1)
    w_o = a_out.reshape(d, d)
    w_dn = a_down.reshape(f_half, d)
    wu_a = jnp.concatenate([a_up[0], a_up[1]], axis=1)
    wu_b = jnp.concatenate([a_up[2], a_up[3]], axis=1)
    qkv_b = rope_apply([(z, Z_QB, QK_SCALE, True), (z, Z_KB, 1.0, True), (z, Z_VB, 1.0, False)], rope_tables(s, 1.0),
                       out_dtype=F32, name="rope_fwd")
    yb, lse_b = dil_fwd_all(qkv_b)
    pa, pb, mixed = gate_mix(ya, yb, wo_a, wo_b, z)
    y1, x1, h2 = proj_norm_res(mixed, w_o, g_post_mix, xs, g_pre_ffn, name="out_proj")
    ua, ub, conv_a, conv_bh, mid = ffn_up(h2, wu_a, wu_b, cw, cb)
    dout, dy2, gg_post_ffn, sq = proj_norm_loss(mid, w_dn, g_post_ffn, x1, target, name="down_proj")
    loss = lax.psum(0.5 * sq[0, 0] / d, ("x", "y", "c"))

    dmid = mm([(dy2, d, 0)], [(w_dn, d, 0)], nt=True, out_dtype=BF16, tm=512, tn=f_half // 2, name="down_dgrad")
    dw_down, dw_down16 = wgrad((mid, f_half, 0), dy2, tk=f_half // 2, tn=1024, ts=1024, name="down_wgrad", bf16_copy=True)
    dua, dub, gc_a, gc_b = ffn_bwd(dmid, ua, ub, conv_a, conv_bh, cw)
    dx1, dy1, gg_pre_ffn, gg_post_mix = mm_norm_bwd(
        [(dua, f_half, 0), (dub, f_half, 0)], [(wu_a, f_half, 0), (wu_b, f_half, 0)],
        [(x1, g_pre_ffn, dout, F32), (y1, g_post_mix, None, BF16)], name="up_dgrad")
    dw_up = None
    for k, du in enumerate((dua, dub)):
        dw_up = wgrad((h2, d, 0), du, tk=1024, tn=f_half // 2, ts=1024, name=f"up_wgrad_{k}", chip_major=True,
                      slabs=(4, 2 * k), into=dw_up, bf16_copy=True)
    g_ffn = [(dw_up[0], dw_up[1]), (dw_down.reshape(4, f_half // 4, d), dw_down16.reshape(4, f_half // 4, d))]
    dw_out, dw_out16 = wgrad((mixed, d, 0), dy1, tk=1024, tn=1024, ts=1024, name="out_wgrad", bf16_copy=True)
    dpa, dpb, dz_g, dya, dyb, dd_a = mix_bwd(dy1, w_o, z, pa, pb, wo_a, wo_b, ya)
    by_chip_cols = lambda a: jnp.stack([a[:, j * (d // 4):(j + 1) * (d // 4)] for j in range(4)], axis=0)
    dw_of = [by_chip_cols(a) for a in wgrad((ya, ATT_W, 0), dpa, tk=ATT_W, tn=d, ts=1024, name="fox_o_wgrad", bf16_copy=True)]
    dw_od = [by_chip_cols(a) for a in wgrad((yb, ATT_W, 0), dpb, tk=ATT_W, tn=d, ts=1024, name="dil_o_wgrad", bf16_copy=True)]
    g_mix = [dw_of, dw_od, (dw_out.reshape(4, d // 4, d), dw_out16.reshape(4, d // 4, d))]
    dq_aug, dk_aug, dv_a, *got_ffn = fox_bwd(q_aug, k_aug, z, dya, lse_a, dd_a, exchange=[g[1] for g in g_ffn], kind="to_owners")
    dz_a, dfa, gg_bf = fox_post(dq_aug, dk_aug, dv_a, fa, bfo)
    dq_b, dk_b, dv_b, *got_mix = dil_bwd_all(qkv_b, dyb, lse_b, yb, exchange=[g[1] for g in g_mix], kind="to_owners")
    dz_b = rope_apply([(dq_b, 0, QK_SCALE, True), (dk_b, 0, 1.0, True), (dv_b, 0, 1.0, False)],
                      rope_tables(s, -1.0), out_dtype=BF16, name="rope_bwd")
    dwt_a = wgrad((dz_a, e_a, 0), h1, tk=e_a // 2, tn=d, ts=1024, name="in_wgrad_a")
    dwt_b = wgrad((dz_b, e_a, 0), h1, tk=e_a // 2, tn=d, ts=1024, name="in_wgrad_b")
    dwt_g = wgrad((dz_g, 2 * d, 0), h1, tk=d, tn=d, ts=1024, name="in_wgrad_g")
    dwt_f = wgrad((dfa, LANES, 0), h1, tk=LANES, tn=d, ts=1024, name="in_wgrad_f")
    dwt_full = jnp.concatenate([dwt_a, dwt_f[:nf], dwt_b, dwt_g], axis=0)
    dw_in = jnp.stack([dwt_full[j * cols_in:(j + 1) * cols_in] for j in range(4)], axis=0)
    from_sib = grads_to_sibling([dw_in], [True], name="grads_to_sibling_in")
    sum_in = chip_sum(dw_in, from_sib[0], c_arr, True, name="chip_sum_w_in")
    grad_x, gg_pre_mix, got_in = mm_norm_bwd(
        [(dz_a, e_a, 0), (dz_b, e_a, 0), (dz_g, d, 0), (dz_g, d, 1), (dfa, LANES, 0)],
        [(wz, e_a, 0), (wz, e_a, 1), (wz, d, 3), (wz, d, 4), (wf, LANES, 0)],
        [(xs, g_pre_mix, dx1, F32)], exchange=[sum_in[1]], name="in_dgrad")

    names = ("w_in", "w_o_fox", "w_o_dil", "w_out", "w_up", "w_down")
    pos_arr = jnp.concatenate([chip_arr, c_arr])
    halves = [final_sum(sum_in[0], got_in, chip_arr, name="final_sum_w_in")] + [
        owner_sum(g[0], got, pos_arr, name=f"owner_sum_{nm}") for g, got, nm in zip(g_mix + g_ffn, got_mix + got_ffn, names[1:])]
    from_half = halves_to_full(halves, [True] + [False] * 5, name="halves_to_full")
    g_big = [None] + [lax.dynamic_update_slice_in_dim(full, mine, ci * mine.shape[0], axis=0)
                      for full, mine in zip(from_half[1:], halves[1:])]
    upd_big = [adamw(w[0], g, m[0], v[0], name=f"adamw_{nm}") for w, g, m, v, nm in list(zip(
        big, g_big, (m_w_in, m_w_o_fox, m_w_o_dil, m_w_out, m_w_up, m_w_down),
        (v_w_in, v_w_o_fox, v_w_o_dil, v_w_out, v_w_up, v_w_down), names))[1:]]
    to_t = lambda a: jnp.transpose(a, (2, 0, 1))
    from_t = lambda a: jnp.transpose(a, (1, 2, 0))
    *upd_in, g_in_t = adamw_rows_view(to_t(w_in), halves[0], from_half[0], to_t(m_w_in), to_t(v_w_in), c_arr,
                                      name="adamw_w_in")

    g_cw_loc = jnp.concatenate([gc_a[0:3], gc_b[0:3]], axis=1)
    g_cb_loc = jnp.concatenate([gc_a[3:4], gc_b[3:4]], axis=1)
    small_loc = [gg_pre_mix, gg_post_mix, gg_pre_ffn, gg_post_ffn, g_cb_loc, gg_bf[:, :nf], g_cw_loc]
    red_rows = (8, 8, 8, 8, 48, 8, 136)
    red = allreduce_small(_pack_rows(small_loc, red_rows), name="allreduce_small")
    g_pm, g_qm, g_pf, g_qf, g_cb, g_bf, g_cw_full = _unpack_rows(red, [a.shape for a in small_loc], red_rows)
    cols_cw = conv_w.shape[2]
    g_cw = lax.dynamic_slice_in_dim(g_cw_full, chip * cols_cw, cols_cw, axis=1)
    small_w = (g_pre_mix, g_post_mix, g_pre_ffn, g_post_ffn, conv_b, b_forget, conv_w[0])
    small_m = (m_g_pre_mix, m_g_post_mix, m_g_pre_ffn, m_g_post_ffn, m_conv_b, m_b_forget, m_conv_w[0])
    small_v = (v_g_pre_mix, v_g_post_mix, v_g_pre_ffn, v_g_post_ffn, v_conv_b, v_b_forget, v_conv_w[0])
    small_g = (g_pm, g_qm, g_pf, g_qf, g_cb, g_bf, g_cw)
    ad_rows = (8, 8, 8, 8, 48, 8, 40)
    packed = [_pack_rows(t, ad_rows) for t in (small_w, small_g, small_m, small_v)]
    upd_small = [_unpack_rows(o, [a.shape for a in small_w], ad_rows) for o in adamw(*packed, name="adamw_small")]

    order = ("g_pre_mix", "w_in", "b_forget", "w_o_fox", "w_o_dil", "w_out", "g_post_mix", "g_pre_ffn", "w_up", "conv_w",
             "conv_b", "w_down", "g_post_ffn")
    small_names = ("g_pre_mix", "g_post_mix", "g_pre_ffn", "g_post_ffn", "conv_b", "b_forget", "conv_w")
    grads, deltas, new_ms, new_vs = {}, {}, {}, {}
    grads["w_in"] = from_t(g_in_t)
    deltas["w_in"], new_ms["w_in"], new_vs["w_in"] = (from_t(a) for a in upd_in)
    for k, nm in enumerate(names[1:]):
        grads[nm] = g_big[k + 1][None]
        deltas[nm], new_ms[nm], new_vs[nm] = (a[None] for a in upd_big[k])
    for k, nm in enumerate(small_names):
        lead = (lambda a: a[None]) if nm == "conv_w" else (lambda a: a)
        grads[nm] = lead(small_g[k])
        deltas[nm], new_ms[nm], new_vs[nm] = (lead(upd_small[j][k]) for j in range(3))
    return (loss, grad_x[None], *[grads[nm] for nm in order], *[deltas[nm] for nm in order],
            *[new_ms[nm] for nm in order], *[new_vs[nm] for nm in order])
```

```python
import functools
import math

import numpy as np
import jax
import jax.numpy as jnp
from jax import lax
from jax.experimental import pallas as pl
from jax.experimental.pallas import tpu as pltpu

F32 = jnp.float32
BF16 = jnp.bfloat16
SDS = jax.ShapeDtypeStruct
MESH = pl.DeviceIdType.MESH

HEAD_DIM = 64
N_HEADS = 8
LANES = 128
ATT_W = N_HEADS * HEAD_DIM
DIL_PATTERNS = ((128, 1), (512, 4), (2048, 16))
DIL_BLK = 128
ROPE_DIM = HEAD_DIM // 4
ROPE_THETA = 500000.0
RMS_EPS = 1e-6
NEG = -1e30
QK_SCALE = 1.0 / math.sqrt(HEAD_DIM)
ADAM_LR, ADAM_B1, ADAM_B2, ADAM_EPS, ADAM_WD, ADAM_STEP = 0.001, 0.9, 0.999, 1e-08, 0.01, 10
VMEM_LIMIT = 56 * 1024 * 1024

Z_QA, Z_KA, Z_VA, Z_QB, Z_KB, Z_VB = 0, 1, 2, 3, 4, 5
Z_W = 5120


def _cp(sem):
    return pltpu.CompilerParams(dimension_semantics=sem, vmem_limit_bytes=VMEM_LIMIT)


def _nt(a, b):
    return lax.dot_general(a, b, (((1,), (1,)), ((), ())), preferred_element_type=F32)


def _tn(a, b):
    return lax.dot_general(a, b, (((0,), (0,)), ((), ())), preferred_element_type=F32)


def _nn(a, b):
    return jnp.dot(a, b, preferred_element_type=F32)


def _lane(shape):
    return lax.broadcasted_iota(jnp.int32, shape, 1)


def _row(shape):
    return lax.broadcasted_iota(jnp.int32, shape, 0)


def rmsnorm_fwd(x, g, *, tm=512):
    s, d = x.shape

    def body(x_ref, g_ref, h_ref):
        xv = x_ref[...]
        inv = lax.rsqrt(jnp.mean(xv * xv, axis=-1, keepdims=True) + RMS_EPS)
        h_ref[...] = (xv * inv * g_ref[...]).astype(h_ref.dtype)

    return pl.pallas_call(
        body, grid=(s // tm,),
        in_specs=[pl.BlockSpec((tm, d), lambda i: (i, 0)), pl.BlockSpec((1, d), lambda i: (0, 0))],
        out_specs=pl.BlockSpec((tm, d), lambda i: (i, 0)),
        out_shape=SDS((s, d), BF16), name="rmsnorm_fwd", compiler_params=_cp(("parallel",)))(x, g)


def mm(a_views, b_views, *, nt, out_dtype, tm, tn, name):
    n_p = len(a_views)
    m = a_views[0][0].shape[0]
    n = b_views[0][0].shape[0] if nt else b_views[0][0].shape[1]

    def body(*refs):
        o_ref = refs[-1]
        acc = None
        for p in range(n_p):
            av = refs[p][...].astype(BF16)
            bv = refs[n_p + p][...].astype(BF16)
            dv = _nt(av, bv) if nt else _nn(av, bv)
            acc = dv if acc is None else acc + dv
        o_ref[...] = acc.astype(o_ref.dtype)

    in_specs = []
    for arr, w, blk in a_views:
        in_specs.append(pl.BlockSpec((tm, w), functools.partial(lambda i, j, blk: (i, blk), blk=blk)))
    for arr, w, blk in b_views:
        if nt:
            in_specs.append(pl.BlockSpec((tn, w), functools.partial(lambda i, j, blk: (j, blk), blk=blk)))
        else:
            in_specs.append(pl.BlockSpec((w, tn), lambda i, j: (0, j)))
    return pl.pallas_call(
        body, grid=(m // tm, n // tn), in_specs=in_specs,
        out_specs=pl.BlockSpec((tm, tn), lambda i, j: (i, j)),
        out_shape=SDS((m, n), out_dtype), name=name,
        compiler_params=_cp(("parallel", "parallel")))(*[a[0] for a in a_views], *[b[0] for b in b_views])


def wgrad(a_view, g, *, tk, tn, ts, name, chip_major=False, slabs=None, into=None, bf16_copy=False):
    arr, ka, blk = a_view
    s, n = g.shape
    ns = s // ts
    total, first = slabs if slabs else (n // tn, 0)
    n_into = 0 if into is None else (2 if bf16_copy else 1)

    def body(a_ref, g_ref, *rest):
        o_ref = rest[n_into]

        @pl.when(pl.program_id(2) == 0)
        def _():
            o_ref[...] = jnp.zeros_like(o_ref)

        o_ref[...] += _tn(a_ref[...].astype(BF16), g_ref[...].astype(BF16))
        if bf16_copy:
            @pl.when(pl.program_id(2) == ns - 1)
            def _():
                rest[n_into + 1][...] = o_ref[...].astype(BF16)

    if chip_major:
        out_spec = pl.BlockSpec((None, tk, tn), lambda i, j, k: (first + j, i, 0))
        shape = (total, ka, tn)
    else:
        out_spec = pl.BlockSpec((tk, tn), lambda i, j, k: (i, j))
        shape = (ka, n)
    in_specs = [pl.BlockSpec((ts, tk), lambda i, j, k: (k, blk * (ka // tk) + i)),
                pl.BlockSpec((ts, tn), lambda i, j, k: (k, j))]
    args = [arr, g]
    if into is not None:
        earlier = list(into) if bf16_copy else [into]
        in_specs += [pl.BlockSpec(memory_space=pl.ANY)] * len(earlier)
        args += earlier
    out = pl.pallas_call(
        body, grid=(ka // tk, n // tn, ns), in_specs=in_specs,
        out_specs=[out_spec, out_spec] if bf16_copy else out_spec,
        out_shape=[SDS(shape, F32), SDS(shape, BF16)] if bf16_copy else SDS(shape, F32), name=name,
        input_output_aliases={2 + k: k for k in range(n_into)},
        compiler_params=_cp(("parallel", "parallel", "arbitrary")))(*args)
    return out


def _norm_bwd_rows(dh, xh, inv, g):
    dxh = dh * g
    dx = inv * (dxh - xh * jnp.mean(dxh * xh, axis=-1, keepdims=True))
    return dx, jnp.sum((dh * xh).reshape(dh.shape[0] // 8, 8, dh.shape[1]), axis=0)


def proj_norm_res(a, w, g, xres, g_next, *, tm=512, name):
    s, k = a.shape
    d = w.shape[1]

    def body(a_ref, w_ref, g_ref, x_ref, gn_ref, y_ref, o_ref, h_ref):
        y = _nn(a_ref[...], w_ref[...])
        inv = lax.rsqrt(jnp.mean(y * y, axis=-1, keepdims=True) + RMS_EPS)
        xn = x_ref[...] + y * inv * g_ref[...]
        y_ref[...] = y
        o_ref[...] = xn
        inv_n = lax.rsqrt(jnp.mean(xn * xn, axis=-1, keepdims=True) + RMS_EPS)
        h_ref[...] = (xn * inv_n * gn_ref[...]).astype(h_ref.dtype)

    row = pl.BlockSpec((tm, d), lambda i: (i, 0))
    vec = pl.BlockSpec((1, d), lambda i: (0, 0))
    return pl.pallas_call(
        body, grid=(s // tm,),
        in_specs=[pl.BlockSpec((tm, k), lambda i: (i, 0)), pl.BlockSpec((k, d), lambda i: (0, 0)), vec, row, vec],
        out_specs=[row, row, row], out_shape=[SDS((s, d), F32), SDS((s, d), F32), SDS((s, d), BF16)], name=name,
        compiler_params=_cp(("parallel",)))(a, w, g, xres, g_next)


def proj_norm_loss(a, w, g, xres, target, *, tm=512, name):
    s, k = a.shape
    d = w.shape[1]
    n = s // tm

    def body(a_ref, w_ref, g_ref, x_ref, t_ref, do_ref, dy_ref, dg_ref, l_ref, acc):
        i = pl.program_id(0)

        @pl.when(i == 0)
        def _():
            acc[...] = jnp.zeros_like(acc)
            l_ref[...] = jnp.zeros_like(l_ref)

        y = _nn(a_ref[...], w_ref[...])
        inv = lax.rsqrt(jnp.mean(y * y, axis=-1, keepdims=True) + RMS_EPS)
        yh = y * inv
        err = x_ref[...] + yh * g_ref[...] - t_ref[...]
        dout = err * (1.0 / d)
        do_ref[...] = dout
        l_ref[...] += jnp.sum(jnp.sum(err * err, axis=1, keepdims=True), axis=0, keepdims=True)
        dy, part = _norm_bwd_rows(dout, yh, inv, g_ref[...])
        dy_ref[...] = dy.astype(dy_ref.dtype)
        acc[...] += part

        @pl.when(i == n - 1)
        def _():
            dg_ref[...] = jnp.sum(acc[...], axis=0, keepdims=True)

    row = pl.BlockSpec((tm, d), lambda i: (i, 0))
    vec = pl.BlockSpec((1, d), lambda i: (0, 0))
    return pl.pallas_call(
        body, grid=(n,),
        in_specs=[pl.BlockSpec((tm, k), lambda i: (i, 0)), pl.BlockSpec((k, d), lambda i: (0, 0)), vec, row, row],
        out_specs=[row, row, vec, pl.BlockSpec((1, 1), lambda i: (0, 0))],
        out_shape=[SDS((s, d), F32), SDS((s, d), BF16), SDS((1, d), F32), SDS((1, 1), F32)],
        scratch_shapes=[pltpu.VMEM((8, d), F32)], name=name, compiler_params=_cp(("arbitrary",)))(a, w, g, xres, target)


def mm_norm_bwd(a_views, b_views, stages, exchange=(), *, tm=256, name):
    n_p, n_s, ne = len(a_views), len(stages), len(exchange)
    s = a_views[0][0].shape[0]
    d = b_views[0][0].shape[0]
    n = s // tm
    has_res = [st[2] is not None for st in stages]

    def body(*refs):
        a_refs, b_refs = refs[:n_p], refs[n_p:2 * n_p]
        at = 2 * n_p
        st_refs = []
        for k in range(n_s):
            cnt = 3 if has_res[k] else 2
            st_refs.append(refs[at:at + cnt])
            at += cnt
        e_ins = refs[at:at + ne]
        at += ne
        dx_refs, dg_refs = refs[at:at + n_s], refs[at + n_s:at + 2 * n_s]
        at += 2 * n_s
        e_outs = refs[at:at + ne]
        at += ne
        accs = refs[at:at + n_s]
        comm = (e_ins, e_outs) + tuple(refs[at + n_s:])
        i = pl.program_id(0)

        @pl.when(i == 0)
        def _():
            for acc in accs:
                acc[...] = jnp.zeros_like(acc)
            if ne:
                _to_chips_start(*comm)

        dh = None
        for p in range(n_p):
            part = _nt(a_refs[p][...].astype(BF16), b_refs[p][...].astype(BF16))
            dh = part if dh is None else dh + part
        for k in range(n_s):
            xv = st_refs[k][0][...]
            inv = lax.rsqrt(jnp.mean(xv * xv, axis=-1, keepdims=True) + RMS_EPS)
            dx, part = _norm_bwd_rows(dh, xv * inv, inv, st_refs[k][1][...])
            if has_res[k]:
                dx = dx + st_refs[k][2][...]
            dx_refs[k][...] = dx.astype(dx_refs[k].dtype)
            accs[k][...] += part
            dh = dx

        @pl.when(i == n - 1)
        def _():
            for k in range(n_s):
                dg_refs[k][...] = jnp.sum(accs[k][...], axis=0, keepdims=True)
            if ne:
                _to_chips_finish(*comm)

    row = pl.BlockSpec((tm, d), lambda i: (i, 0))
    vec = pl.BlockSpec((1, d), lambda i: (0, 0))
    in_specs, args = [], []
    for arr, w, blk in a_views:
        in_specs.append(pl.BlockSpec((tm, w), functools.partial(lambda i, blk: (i, blk), blk=blk)))
        args.append(arr)
    for arr, w, blk in b_views:
        in_specs.append(pl.BlockSpec((d, w), functools.partial(lambda i, blk: (0, blk), blk=blk)))
        args.append(arr)
    for x, g, res, _ in stages:
        in_specs += [row, vec] + ([row] if res is not None else [])
        args += [x, g] + ([res] if res is not None else [])
    return pl.pallas_call(
        body, grid=(n,), in_specs=in_specs + [ANY] * ne,
        out_specs=[row] * n_s + [vec] * n_s + [ANY] * ne,
        out_shape=[SDS((s, d), st[3]) for st in stages] + [SDS((1, d), F32)] * n_s + _to_chips_shapes(exchange),
        scratch_shapes=[pltpu.VMEM((8, d), F32)] * n_s + (_to_chips_sems(ne) if ne else []), name=name,
        compiler_params=_cp(("arbitrary",)))(*args, *exchange)


def _split3(v):
    hi = v.astype(BF16).astype(F32)
    r = v - hi
    mid = r.astype(BF16).astype(F32)
    lo = (r - mid).astype(BF16).astype(F32)
    return hi, mid, lo


def _tri(n, upper):
    r = np.arange(n)
    m = (r[:, None] <= r[None, :]) if upper else (r[:, None] >= r[None, :])
    return jnp.asarray(m.astype(np.float32))


def fox_prep(z, fa, bfo, *, tb=512):
    s = z.shape[0]
    n = s // tb

    def body(q_ref, k_ref, v_ref, fa_ref, b_ref, tri_ref, qa_ref, ka_ref, va_ref, carry):
        @pl.when(pl.program_id(0) == 0)
        def _():
            carry[...] = jnp.zeros_like(carry)

        xv = fa_ref[...] + b_ref[...]
        logf = jnp.minimum(xv, 0.0) - jnp.log(1.0 + jnp.exp(-jnp.abs(xv)))
        csum = jnp.dot(tri_ref[...], logf, preferred_element_type=F32, precision=lax.Precision.HIGHEST) + carry[0:1, :]
        carry[0:1, :] = csum[tb - 1:tb, :]
        lane = _lane((tb, LANES))
        for h in range(N_HEADS):
            hi, mid, lo = _split3(csum[:, h:h + 1])
            pair = (h // 2) * LANES
            qv = q_ref[:, pair:pair + LANES].astype(F32)
            kv = k_ref[:, pair:pair + LANES].astype(F32)
            vv = v_ref[:, pair:pair + LANES].astype(F32)
            if h % 2:
                qv = pltpu.roll(qv, 64, axis=1)
                kv = pltpu.roll(kv, 64, axis=1)
                vv = pltpu.roll(vv, 64, axis=1)
            va_ref[:, h * LANES:(h + 1) * LANES] = jnp.where(lane < 64, vv, jnp.where(lane == 64, 1.0, 0.0)).astype(BF16)
            one = jnp.where((lane >= 67) & (lane < 70), 1.0, 0.0)
            q_x = jnp.where(lane == 64, hi, jnp.where(lane == 65, mid, jnp.where(lane == 66, lo, one)))
            one = jnp.where((lane >= 64) & (lane < 67), 1.0, 0.0)
            k_x = jnp.where(lane == 67, -hi, jnp.where(lane == 68, -mid, jnp.where(lane == 69, -lo, one)))
            qa_ref[:, h * LANES:(h + 1) * LANES] = jnp.where(lane < 64, qv * QK_SCALE, q_x).astype(BF16)
            ka_ref[:, h * LANES:(h + 1) * LANES] = jnp.where(lane < 64, kv, k_x).astype(BF16)

    return pl.pallas_call(
        body, grid=(n,),
        in_specs=[pl.BlockSpec((tb, ATT_W), lambda i: (i, Z_QA)), pl.BlockSpec((tb, ATT_W), lambda i: (i, Z_KA)),
                  pl.BlockSpec((tb, ATT_W), lambda i: (i, Z_VA)),
                  pl.BlockSpec((tb, LANES), lambda i: (i, 0)), pl.BlockSpec((1, LANES), lambda i: (0, 0)),
                  pl.BlockSpec((tb, tb), lambda i: (0, 0))],
        out_specs=[pl.BlockSpec((tb, N_HEADS * LANES), lambda i: (i, 0))] * 3,
        out_shape=[SDS((s, N_HEADS * LANES), BF16)] * 3,
        scratch_shapes=[pltpu.VMEM((8, LANES), F32)],
        name="fox_prep", compiler_params=_cp(("arbitrary",)))(z, z, z, fa, bfo, _tri(tb, False))


def _causal_pairs(n, k_major):
    if k_major:
        pairs = [(qi, kj) for kj in range(n) for qi in range(kj, n)]
    else:
        pairs = [(qi, kj) for qi in range(n) for kj in range(qi + 1)]
    return (jnp.asarray([p[0] for p in pairs], jnp.int32), jnp.asarray([p[1] for p in pairs], jnp.int32), len(pairs))


def fox_fwd(q_aug, k_aug, v_aug, gather=(), *, t=512, hps=4):
    s = v_aug.shape[0]
    qi_arr, kj_arr, n_pairs = _causal_pairs(s // t, False)
    ng = len(gather)
    n_groups = N_HEADS // hps

    def body(qi_ref, kj_ref, q_ref, k_ref, v_ref, *rest):
        g_ins, (o_ref, lse_ref), g_outs = rest[:ng], rest[ng:ng + 2], rest[ng + 2:2 * ng + 2]
        m_scr, acc_scr = rest[2 * ng + 2:2 * ng + 4]
        comm = (g_ins, g_outs) + tuple(rest[2 * ng + 4:]) + ([True] * ng,)
        step = pl.program_id(1)
        qi = qi_ref[step]
        kj = kj_ref[step]
        if ng:
            @pl.when((pl.program_id(0) == 0) & (step == 0))
            def _():
                _allgather_start(*comm)

        @pl.when(kj == 0)
        def _():
            m_scr[...] = jnp.full_like(m_scr, NEG)
            acc_scr[...] = jnp.zeros_like(acc_scr)

        def update(masked):
            for i in range(hps):
                sc = _nt(q_ref[:, i * LANES:(i + 1) * LANES], k_ref[:, i * LANES:(i + 1) * LANES])
                if masked:
                    sc = jnp.where(_row((t, t)) >= _lane((t, t)), sc, NEG)
                m_prev = m_scr[i]
                m_new = jnp.maximum(m_prev, jnp.max(sc, axis=-1, keepdims=True))
                p = jnp.exp((sc - jnp.tile(m_new, (1, t // LANES))).astype(BF16))
                acc_scr[i] = jnp.exp(m_prev - m_new) * acc_scr[i] + _nn(p, v_ref[:, i * LANES:(i + 1) * LANES])
                m_scr[i] = m_new

        @pl.when(kj < qi)
        def _():
            update(False)

        @pl.when(kj == qi)
        def _():
            update(True)
            lane = _lane((t, LANES))
            for pr in range(hps // 2):
                den = [acc_scr[2 * pr + i][:, 64:65] for i in range(2)]
                o_ref[:, pr * LANES:(pr + 1) * LANES] = jnp.where(
                    lane < 64, acc_scr[2 * pr] / den[0], pltpu.roll(acc_scr[2 * pr + 1] / den[1], 64, axis=1)).astype(o_ref.dtype)
                lse_ref[:, pr * LANES:(pr + 1) * LANES] = jnp.where(
                    lane < 64, m_scr[2 * pr] + jnp.log(den[0]), m_scr[2 * pr + 1] + jnp.log(den[1]))

        if ng:
            @pl.when((pl.program_id(0) == n_groups - 1) & (step == n_pairs - 1))
            def _():
                _allgather_finish(*comm)

    wide = hps * LANES
    grid_spec = pltpu.PrefetchScalarGridSpec(
        num_scalar_prefetch=2, grid=(n_groups, n_pairs),
        in_specs=[pl.BlockSpec((t, wide), lambda hg, st, qi, kj: (qi[st], hg)),
                  pl.BlockSpec((t, wide), lambda hg, st, qi, kj: (kj[st], hg)),
                  pl.BlockSpec((t, wide), lambda hg, st, qi, kj: (kj[st], hg))] + [ANY] * ng,
        out_specs=[pl.BlockSpec((t, wide // 2), lambda hg, st, qi, kj: (qi[st], hg))] * 2 + [ANY] * ng,
        scratch_shapes=[pltpu.VMEM((hps, t, LANES), F32)] * 2 + (_allgather_sems(ng) if ng else []))
    return pl.pallas_call(
        body, grid_spec=grid_spec, out_shape=[SDS((s, ATT_W), BF16), SDS((s, ATT_W), F32)] + _allgather_shapes(gather),
        name="fox_fwd", compiler_params=_cp(("arbitrary", "arbitrary")))(qi_arr, kj_arr, q_aug, k_aug, v_aug, *gather)


def fox_bwd(q_aug, k_aug, z, dy, lse, dd, exchange=(), kind="to_chips", *, t=512, hps=4):
    s = z.shape[0]
    qi_arr, kj_arr, n_pairs = _causal_pairs(s // t, True)
    ne = len(exchange)
    n_groups = N_HEADS // hps
    x_shapes, x_sems, x_start, x_finish = EXCHANGES[kind]

    def body(qi_ref, kj_ref, q_ref, k_ref, v_ref, do_ref, lse_ref, dd_ref, *rest):
        e_ins, (dq_ref, dk_ref, dv_ref), e_outs = rest[:ne], rest[ne:ne + 3], rest[ne + 3:2 * ne + 3]
        comm = (e_ins, e_outs) + tuple(rest[2 * ne + 3:])
        step = pl.program_id(1)
        qi = qi_ref[step]
        kj = kj_ref[step]
        if ne:
            @pl.when((pl.program_id(0) == 0) & (step == 0))
            def _():
                x_start(*comm)

        @pl.when(step == 0)
        def _():
            dq_ref[...] = jnp.zeros_like(dq_ref)

        @pl.when(qi == kj)
        def _():
            dk_ref[...] = jnp.zeros_like(dk_ref)
            dv_ref[...] = jnp.zeros_like(dv_ref)

        def update(masked):
            lane = _lane((t, LANES))
            rows = pl.ds(pl.multiple_of(qi * t, t), t)
            for pr in range(hps // 2):
                pair = slice(pr * LANES, (pr + 1) * LANES)
                dov = do_ref[:, pair]
                dv_new = None
                for i in range(2):
                    head = (lane < 64) if i == 0 else (lane >= 64)
                    own = slice((2 * pr + i) * LANES, (2 * pr + i + 1) * LANES)
                    col = slice(pr * LANES + i * 64, pr * LANES + i * 64 + 1)
                    qv = q_ref[:, own]
                    kv = k_ref[:, own]
                    sc = _nt(qv, kv)
                    if masked:
                        sc = jnp.where(_row((t, t)) >= _lane((t, t)), sc, NEG)
                    p = jnp.exp(sc - lse_ref[:, col])
                    dp = _nt(jnp.where(head, dov, jnp.zeros_like(dov)), v_ref[:, pair])
                    ds = (p * (dp - dd_ref[:, col])).astype(BF16)
                    dq_ref[rows, own] += _nn(ds, kv)
                    dk_ref[:, own] += _tn(ds, qv)
                    dvi = _tn(p.astype(BF16), dov)
                    dv_new = dvi if dv_new is None else jnp.where(head, dvi, dv_new)
                dv_ref[:, pair] += dv_new

        @pl.when(kj < qi)
        def _():
            update(False)

        @pl.when(kj == qi)
        def _():
            update(True)

        if ne:
            @pl.when((pl.program_id(0) == n_groups - 1) & (step == n_pairs - 1))
            def _():
                x_finish(*comm)

    wide, half = hps * LANES, hps // 2 * LANES
    v_blk = Z_VA * ATT_W // half
    grid_spec = pltpu.PrefetchScalarGridSpec(
        num_scalar_prefetch=2, grid=(n_groups, n_pairs),
        in_specs=[pl.BlockSpec((t, wide), lambda hg, st, qi, kj: (qi[st], hg)),
                  pl.BlockSpec((t, wide), lambda hg, st, qi, kj: (kj[st], hg)),
                  pl.BlockSpec((t, half), lambda hg, st, qi, kj: (kj[st], v_blk + hg)),
                  pl.BlockSpec((t, half), lambda hg, st, qi, kj: (qi[st], hg)),
                  pl.BlockSpec((t, half), lambda hg, st, qi, kj: (qi[st], hg)),
                  pl.BlockSpec((t, half), lambda hg, st, qi, kj: (qi[st], hg))] + [ANY] * ne,
        out_specs=[pl.BlockSpec((s, wide), lambda hg, st, qi, kj: (0, hg)),
                   pl.BlockSpec((t, wide), lambda hg, st, qi, kj: (kj[st], hg)),
                   pl.BlockSpec((t, half), lambda hg, st, qi, kj: (kj[st], hg))] + [ANY] * ne,
        scratch_shapes=x_sems(ne) if ne else [])
    return pl.pallas_call(
        body, grid_spec=grid_spec,
        out_shape=[SDS((s, N_HEADS * LANES), F32), SDS((s, N_HEADS * LANES), F32), SDS((s, ATT_W), F32)]
        + x_shapes(exchange),
        name="fox_bwd", compiler_params=_cp(("arbitrary", "arbitrary")))(qi_arr, kj_arr, q_aug, k_aug, z, dy, lse, dd, *exchange)


def fox_post(dq_aug, dk_aug, dv, fa, bfo, *, tb=512):
    s = dv.shape[0]
    n = s // tb

    def body(dq_ref, dk_ref, dv_ref, fa_ref, b_ref, tri_ref, dz_ref, dfa_ref, gb_ref, carry, acc):
        i = pl.program_id(0)

        @pl.when(i == 0)
        def _():
            carry[...] = jnp.zeros_like(carry)
            acc[...] = jnp.zeros_like(acc)

        lane = _lane((tb, LANES))
        d_f = jnp.zeros((tb, LANES), F32)
        for h in range(N_HEADS):
            col = dq_ref[:, h * LANES + 64:h * LANES + 65] - dk_ref[:, h * LANES + 67:h * LANES + 68]
            d_f = jnp.where(lane == h, col, d_f)
        suffix = jnp.dot(tri_ref[...], d_f, preferred_element_type=F32, precision=lax.Precision.HIGHEST) + carry[0:1, :]
        carry[0:1, :] = suffix[0:1, :]
        xv = fa_ref[...] + b_ref[...]
        dx = suffix * (1.0 / (1.0 + jnp.exp(xv)))
        dfa_ref[...] = dx.astype(dfa_ref.dtype)
        acc[...] += jnp.sum(dx.reshape(tb // 8, 8, LANES), axis=0)
        for hp in range(4):
            for src, off, scale in ((dq_ref, 0, QK_SCALE), (dk_ref, ATT_W, 1.0)):
                even = src[:, (2 * hp) * LANES:(2 * hp + 1) * LANES]
                odd = pltpu.roll(src[:, (2 * hp + 1) * LANES:(2 * hp + 2) * LANES], 64, axis=1)
                dz_ref[:, off + hp * LANES:off + (hp + 1) * LANES] = (jnp.where(lane < 64, even, odd) * scale).astype(BF16)
        dz_ref[:, 2 * ATT_W:3 * ATT_W] = dv_ref[...].astype(BF16)

        @pl.when(i == n - 1)
        def _():
            gb_ref[...] = jnp.sum(acc[...], axis=0, keepdims=True)

    rev = lambda i: (n - 1 - i, 0)
    return pl.pallas_call(
        body, grid=(n,),
        in_specs=[pl.BlockSpec((tb, N_HEADS * LANES), rev), pl.BlockSpec((tb, N_HEADS * LANES), rev),
                  pl.BlockSpec((tb, ATT_W), rev), pl.BlockSpec((tb, LANES), rev),
                  pl.BlockSpec((1, LANES), lambda i: (0, 0)), pl.BlockSpec((tb, tb), lambda i: (0, 0))],
        out_specs=[pl.BlockSpec((tb, 3 * ATT_W), rev), pl.BlockSpec((tb, LANES), rev),
                   pl.BlockSpec((1, LANES), lambda i: (0, 0))],
        out_shape=[SDS((s, 3 * ATT_W), BF16), SDS((s, LANES), BF16), SDS((1, LANES), F32)],
        scratch_shapes=[pltpu.VMEM((8, LANES), F32), pltpu.VMEM((8, LANES), F32)],
        name="fox_post", compiler_params=_cp(("arbitrary",)))(dq_aug, dk_aug, dv, fa, bfo, _tri(tb, True))


def rope_cos_sin(s):
    half = ROPE_DIM // 2
    inv_freq = ROPE_THETA ** (-jnp.arange(half, dtype=F32) * 2.0 / ROPE_DIM)
    ang = jnp.arange(s, dtype=F32)[:, None] * inv_freq[None, :]
    return jnp.tile(jnp.cos(ang), (1, LANES // half)), jnp.tile(jnp.sin(ang), (1, LANES // half))


def _rotate(x, cos, sin, sign):
    l64 = _lane(x.shape) & (HEAD_DIM - 1)
    first = l64 < ROPE_DIM // 2
    second = (l64 >= ROPE_DIM // 2) & (l64 < ROPE_DIM)
    from_next = jnp.where(first, -sign * sin, 0.0)
    from_prev = jnp.where(second, sign * sin, 0.0)
    return (x * jnp.where(first | second, cos, 1.0) + pltpu.roll(x, LANES - 8, axis=1) * from_next
            + pltpu.roll(x, 8, axis=1) * from_prev)


def _dil_rows(base, r):
    if r == 1:
        return pl.ds(pl.multiple_of(base, DIL_BLK), DIL_BLK)
    return pl.ds(base, DIL_BLK, stride=r)


def _dil_block(idx, r, nb):
    shift = nb.bit_length() - 1
    rho = idx >> shift
    n = idx & (nb - 1)
    base = rho + n * (r * DIL_BLK)
    return _dil_rows(base, r), _dil_rows(jnp.maximum(base - r * DIL_BLK, rho), r), n > 0


def _cat(a, b):
    return jnp.concatenate([a, b], axis=0)


def _two_heads(v, first_head):
    zero = jnp.zeros_like(v)
    return _cat(jnp.where(first_head, v, zero), jnp.where(first_head, zero, v))


def _dil_bands():
    b = DIL_BLK
    q = _row((2 * b, 2 * b)) & (b - 1)
    col = _lane((2 * b, 2 * b))
    return (col < b) & (col >= q), (col >= b) & (col - b <= q)


def _dil_load_qkv(zq_ref, zk_ref, zv_ref, cos_ref, sin_ref, q_ref, k_ref, v_ref, *, chunk=512):
    def step(i, carry):
        rows = pl.ds(pl.multiple_of(i * chunk, chunk), chunk)
        cos, sin = cos_ref[rows, :], sin_ref[rows, :]
        q_ref[rows, :] = _rotate(zq_ref[rows, :].astype(F32), cos, sin, 1.0) * QK_SCALE
        k_ref[rows, :] = _rotate(zk_ref[rows, :].astype(F32), cos, sin, 1.0)
        v_ref[rows, :] = zv_ref[rows, :].astype(F32)
        return carry

    lax.fori_loop(0, q_ref.shape[0] // chunk, step, 0)


def dil_fwd_all(z, cos_t, sin_t, *, unroll=8):
    s = z.shape[0]
    b = DIL_BLK
    n_blk = s // b

    def body(zq_ref, zk_ref, zv_ref, cos_ref, sin_ref, o_ref, l_ref, q_ref, k_ref, v_ref):
        _dil_load_qkv(zq_ref, zk_ref, zv_ref, cos_ref, sin_ref, q_ref, k_ref, v_ref)
        first_head = _lane((b, LANES)) < 64
        band_prev, band_cur = _dil_bands()
        for g, (_, r) in enumerate(DIL_PATTERNS):
            nb = n_blk // r

            def group(it, carry, g=g, r=r, nb=nb):
                loaded = []
                kc = vc = None
                for u in range(unroll):
                    rows_c, rows_p, has_prev = _dil_block(it * unroll + u, r, nb)
                    if u % min(nb, unroll):
                        kp, vp = kc, vc
                    else:
                        kp, vp = k_ref[rows_p, :].astype(BF16), v_ref[rows_p, :].astype(BF16)
                    kc, vc = k_ref[rows_c, :].astype(BF16), v_ref[rows_c, :].astype(BF16)
                    state = (o_ref[rows_c, :], l_ref[rows_c, :]) if g else None
                    loaded.append((rows_c, has_prev, [q_ref[rows_c, :].astype(BF16), kp, kc, vp, vc], state))
                done = []
                for rows_c, has_prev, (qv, kp, kc, vp, vc), state in loaded:
                    sc = jnp.where(band_cur | (band_prev & has_prev), _nt(_two_heads(qv, first_head), _cat(kp, kc)), NEG)
                    m = jnp.max(sc, axis=-1, keepdims=True)
                    p = jnp.exp(sc - m)
                    den = jnp.sum(p, axis=-1, keepdims=True)
                    both = _nn(p.astype(BF16), _cat(vp, vc)) / den
                    lse2 = m + jnp.log(den)
                    ov = jnp.where(first_head, both[:b], both[b:])
                    lse = jnp.where(first_head, lse2[:b], lse2[b:])
                    if state is not None:
                        m2 = jnp.maximum(state[1], lse)
                        wp = jnp.exp(state[1] - m2)
                        wn = jnp.exp(lse - m2)
                        ov = (wp * state[0] + wn * ov) / (wp + wn)
                        lse = m2 + jnp.log(wp + wn)
                    done.append((rows_c, ov, lse))
                for rows_c, ov, lse in done:
                    o_ref[rows_c, :] = ov
                    l_ref[rows_c, :] = lse
                return carry

            lax.fori_loop(0, n_blk // unroll, group, 0)

    col_blk = lambda k: pl.BlockSpec((s, LANES), lambda hp: (0, 4 * k + hp))
    table = pl.BlockSpec((s, LANES), lambda hp: (0, 0))
    out = pl.BlockSpec((s, LANES), lambda hp: (0, hp))
    return pl.pallas_call(
        body, grid=(4,), in_specs=[col_blk(Z_QB), col_blk(Z_KB), col_blk(Z_VB), table, table], out_specs=[out, out],
        out_shape=[SDS((s, ATT_W), F32)] * 2, scratch_shapes=[pltpu.VMEM((s, LANES), F32)] * 3, name="dil_fwd",
        compiler_params=_cp(("parallel",)))(z, z, z, cos_t, sin_t)


def dil_bwd_all(z, cos_t, sin_t, dy, lse, y, exchange=(), kind="to_chips", *, unroll=8):
    s = z.shape[0]
    b = DIL_BLK
    n_blk = s // b
    ne = len(exchange)
    x_shapes, x_sems, x_start, x_finish = EXCHANGES[kind]

    def body(zq_ref, zk_ref, zv_ref, cos_ref, sin_ref, do_ref, l_ref, y_ref, *rest):
        e_ins, (gq_ref, gk_ref, gv_ref), e_outs = rest[:ne], rest[ne:ne + 3], rest[ne + 3:2 * ne + 3]
        q_ref, k_ref, v_ref, dq_ref, dk_ref, dv_ref = rest[2 * ne + 3:2 * ne + 9]
        comm = (e_ins, e_outs) + tuple(rest[2 * ne + 9:])
        if ne:
            @pl.when(pl.program_id(0) == 0)
            def _():
                x_start(*comm)

        _dil_load_qkv(zq_ref, zk_ref, zv_ref, cos_ref, sin_ref, q_ref, k_ref, v_ref)
        dq_ref[...] = jnp.zeros_like(dq_ref)
        dk_ref[...] = jnp.zeros_like(dk_ref)
        dv_ref[...] = jnp.zeros_like(dv_ref)
        first_head = _lane((b, LANES)) < 64
        band_prev, band_cur = _dil_bands()
        for _, r in DIL_PATTERNS:
            nb = n_blk // r

            def group(it, carry, r=r, nb=nb):
                loaded = []
                kc = vc = None
                for u in range(unroll):
                    rows_c, rows_p, has_prev = _dil_block(it * unroll + u, r, nb)
                    if u % min(nb, unroll):
                        kp, vp = kc, vc
                    else:
                        kp, vp = k_ref[rows_p, :].astype(BF16), v_ref[rows_p, :].astype(BF16)
                    kc, vc = k_ref[rows_c, :].astype(BF16), v_ref[rows_c, :].astype(BF16)
                    vals = [q_ref[rows_c, :].astype(BF16), kp, kc, vp, vc, do_ref[rows_c, :], l_ref[rows_c, :], y_ref[rows_c, :]]
                    loaded.append((rows_c, rows_p, has_prev, vals))
                done = []
                for rows_c, rows_p, has_prev, (qv, kp, kc, vp, vc, dof, lv, yv) in loaded:
                    q2 = _two_heads(qv, first_head)
                    do2 = _two_heads(dof.astype(BF16), first_head)
                    kcat, vcat = _cat(kp, kc), _cat(vp, vc)
                    lse2 = _cat(lv[:, 0:1], lv[:, 64:65])
                    dd2 = jnp.sum(_two_heads(dof * yv, first_head), axis=-1, keepdims=True)
                    p = jnp.exp(jnp.where(band_cur | (band_prev & has_prev), _nt(q2, kcat), NEG) - lse2)
                    ds = (p * (_nt(do2, vcat) - dd2)).astype(BF16)
                    dq2 = _nn(ds, kcat)
                    dkcat = _tn(ds, q2)
                    dvcat = _tn(p.astype(BF16), do2)
                    done.append((rows_c, rows_p, (jnp.where(first_head, dq2[:b], dq2[b:]), dkcat[:b], dkcat[b:],
                                                  dvcat[:b], dvcat[b:])))
                for rows_c, rows_p, (dq, dk_p, dk_c, dv_p, dv_c) in done:
                    dq_ref[rows_c, :] += dq
                    dk_ref[rows_p, :] += dk_p
                    dk_ref[rows_c, :] += dk_c
                    dv_ref[rows_p, :] += dv_p
                    dv_ref[rows_c, :] += dv_c
                return carry

            lax.fori_loop(0, n_blk // unroll, group, 0)

        def finish(i, carry, chunk=512):
            rows = pl.ds(pl.multiple_of(i * chunk, chunk), chunk)
            cos, sin = cos_ref[rows, :], sin_ref[rows, :]
            gq_ref[rows, :] = (_rotate(dq_ref[rows, :], cos, sin, -1.0) * QK_SCALE).astype(BF16)
            gk_ref[rows, :] = _rotate(dk_ref[rows, :], cos, sin, -1.0).astype(BF16)
            gv_ref[rows, :] = dv_ref[rows, :].astype(BF16)
            return carry

        lax.fori_loop(0, s // 512, finish, 0)
        if ne:
            @pl.when(pl.program_id(0) == 3)
            def _():
                x_finish(*comm)

    col_blk = lambda k: pl.BlockSpec((s, LANES), lambda hp: (0, 4 * k + hp))
    table = pl.BlockSpec((s, LANES), lambda hp: (0, 0))
    nat = pl.BlockSpec((s, LANES), lambda hp: (0, hp))
    return pl.pallas_call(
        body, grid=(4,), in_specs=[col_blk(Z_QB), col_blk(Z_KB), col_blk(Z_VB), table, table, nat, nat, nat] + [ANY] * ne,
        out_specs=[nat, nat, nat] + [ANY] * ne, out_shape=[SDS((s, ATT_W), BF16)] * 3 + x_shapes(exchange),
        scratch_shapes=[pltpu.VMEM((s, LANES), F32)] * 6 + (x_sems(ne) if ne else []), name="dil_bwd",
        compiler_params=_cp(("arbitrary",)))(z, z, z, cos_t, sin_t, dy, lse, y, *exchange)


def _sigmoid(v):
    return 1.0 / (1.0 + jnp.exp(-v))


def gate_mix(ya, yb, wa, wb, z, *, tm=512, tn=512):
    s = ya.shape[0]
    d = wa.shape[1]
    ga_blk = 3 * ATT_W * 2 // tn
    gb_blk = ga_blk + d // tn

    def body(ya_ref, yb_ref, wa_ref, wb_ref, ga_ref, gb_ref, pa_ref, pb_ref, mx_ref):
        pa = _nn(ya_ref[...], wa_ref[...])
        pb = _nn(yb_ref[...].astype(BF16), wb_ref[...])
        pa_ref[...] = pa.astype(BF16)
        pb_ref[...] = pb.astype(BF16)
        mx_ref[...] = (_sigmoid(ga_ref[...].astype(F32)) * pa + _sigmoid(gb_ref[...].astype(F32)) * pb).astype(BF16)

    out = pl.BlockSpec((tm, tn), lambda i, j: (i, j))
    return pl.pallas_call(
        body, grid=(s // tm, d // tn),
        in_specs=[pl.BlockSpec((tm, ATT_W), lambda i, j: (i, 0)), pl.BlockSpec((tm, ATT_W), lambda i, j: (i, 0)),
                  pl.BlockSpec((ATT_W, tn), lambda i, j: (0, j)), pl.BlockSpec((ATT_W, tn), lambda i, j: (0, j)),
                  pl.BlockSpec((tm, tn), lambda i, j: (i, ga_blk + j)), pl.BlockSpec((tm, tn), lambda i, j: (i, gb_blk + j))],
        out_specs=[out, out, out], out_shape=[SDS((s, d), BF16)] * 3, name="gate_mix",
        compiler_params=_cp(("parallel", "parallel")))(ya, yb, wa, wb, z, z)


def mix_bwd(dy, w_o, z, pa, pb, wo_a, wo_b, ya, *, tm=256):
    s, d = dy.shape

    def body(dy_ref, wo_ref, ga_ref, gb_ref, pa_ref, pb_ref, wa_ref, wb_ref, ya_ref,
             dpa_ref, dpb_ref, dg_ref, dya_ref, dyb_ref, dd_ref):
        dm = _nt(dy_ref[...], wo_ref[...])
        sa = _sigmoid(ga_ref[...].astype(F32))
        sb = _sigmoid(gb_ref[...].astype(F32))
        dpa = (dm * sa).astype(BF16)
        dpb = (dm * sb).astype(BF16)
        dpa_ref[...] = dpa
        dpb_ref[...] = dpb
        dg_ref[:, 0:d] = (dm * pa_ref[...].astype(F32) * sa * (1.0 - sa)).astype(BF16)
        dg_ref[:, d:2 * d] = (dm * pb_ref[...].astype(F32) * sb * (1.0 - sb)).astype(BF16)
        dya = _nt(dpa, wa_ref[...]).astype(BF16)
        dya_ref[...] = dya
        dyb_ref[...] = _nt(dpb, wb_ref[...])
        lane = _lane((tm, LANES))
        for pr in range(ATT_W // LANES):
            pair = slice(pr * LANES, (pr + 1) * LANES)
            prod = dya[:, pair].astype(F32) * ya_ref[:, pair].astype(F32)
            lo = jnp.sum(jnp.where(lane < 64, prod, 0.0), axis=-1, keepdims=True)
            hi = jnp.sum(jnp.where(lane >= 64, prod, 0.0), axis=-1, keepdims=True)
            dd_ref[:, pair] = jnp.where(lane < 64, lo, hi)

    row = pl.BlockSpec((tm, d), lambda i: (i, 0))
    att = pl.BlockSpec((tm, ATT_W), lambda i: (i, 0))
    whole = lambda a: pl.BlockSpec(a.shape, lambda i: (0, 0))
    return pl.pallas_call(
        body, grid=(s // tm,),
        in_specs=[row, whole(w_o), pl.BlockSpec((tm, d), lambda i: (i, 3)), pl.BlockSpec((tm, d), lambda i: (i, 4)), row, row,
                  whole(wo_a), whole(wo_b), att],
        out_specs=[row, row, pl.BlockSpec((tm, 2 * d), lambda i: (i, 0)), att, att, att],
        out_shape=[SDS((s, d), BF16), SDS((s, d), BF16), SDS((s, 2 * d), BF16), SDS((s, ATT_W), BF16),
                   SDS((s, ATT_W), F32), SDS((s, ATT_W), F32)], name="mix_bwd",
        compiler_params=_cp(("parallel",)))(dy, w_o, z, z, pa, pb, wo_a, wo_b, ya)


GELU_C = math.sqrt(2.0 / math.pi)


def _gelu_parts(a):
    a2 = a * a
    th = jnp.tanh(a * (GELU_C + (GELU_C * 0.044715) * a2))
    half = 0.5 * a
    gelu = half + half * th
    dgelu = (0.5 + 0.5 * th) + half * (1.0 - th * th) * (GELU_C + (3.0 * GELU_C * 0.044715) * a2)
    return gelu, dgelu


def _causal_taps(u, before):
    row = _row(u.shape)
    r1 = jnp.where(row == 0, before[7:8, :], pltpu.roll(u, 1, axis=0))
    r2 = jnp.where(row == 0, before[6:7, :], jnp.where(row == 1, before[7:8, :], pltpu.roll(u, 2, axis=0)))
    return r1, r2


def ffn_up(h, wa, wb, cw, cb, *, tm=1024, tn=256):
    s, d = h.shape
    f = wa.shape[1]
    nj = f // tn

    def body(h_ref, wa_ref, wb_ref, cwa_ref, cwb_ref, cba_ref, cbb_ref, ua_ref, ub_ref, ca_ref, cbo_ref, m_ref, carry):
        @pl.when(pl.program_id(1) == 0)
        def _():
            carry[...] = jnp.zeros_like(carry)

        conv = []
        for k, (w_ref, cw_ref, cb_ref, u_ref, c_ref) in enumerate(((wa_ref, cwa_ref, cba_ref, ua_ref, ca_ref),
                                                                   (wb_ref, cwb_ref, cbb_ref, ub_ref, cbo_ref))):
            u16 = _nn(h_ref[...], w_ref[...]).astype(BF16)
            u_ref[...] = u16
            u = u16.astype(F32)
            r1, r2 = _causal_taps(u, carry[k])
            carry[k] = u[tm - 8:tm, :]
            c16 = (cw_ref[0:1, :] * r2 + cw_ref[1:2, :] * r1 + cw_ref[2:3, :] * u + cb_ref[...]).astype(BF16)
            c_ref[...] = c16
            conv.append(c16.astype(F32))
        m_ref[...] = (_gelu_parts(conv[0])[0] * conv[1]).astype(BF16)

    out = pl.BlockSpec((tm, tn), lambda j, i: (i, j))
    return pl.pallas_call(
        body, grid=(nj, s // tm),
        in_specs=[pl.BlockSpec((tm, d), lambda j, i: (i, 0)),
                  pl.BlockSpec((d, tn), lambda j, i: (0, j)), pl.BlockSpec((d, tn), lambda j, i: (0, j)),
                  pl.BlockSpec((3, tn), lambda j, i: (0, j)), pl.BlockSpec((3, tn), lambda j, i: (0, nj + j)),
                  pl.BlockSpec((1, tn), lambda j, i: (0, j)), pl.BlockSpec((1, tn), lambda j, i: (0, nj + j))],
        out_specs=[out] * 5, out_shape=[SDS((s, f), BF16)] * 5,
        scratch_shapes=[pltpu.VMEM((2, 8, tn), F32)], name="ffn_up",
        compiler_params=_cp(("parallel", "arbitrary")))(h, wa, wb, cw, cw, cb, cb)


def ffn_bwd(dm, ua, ub, ca, cbo, cw, *, tm=1024, tn=256):
    s, f = dm.shape
    nj = f // tn
    ni = s // tm

    def body(dm_ref, ua_ref, ub_ref, ca_ref, cbo_ref, cwa_ref, cwb_ref, dua_ref, dub_ref, ga_ref, gb_ref, carry):
        @pl.when(pl.program_id(1) == 0)
        def _():
            carry[...] = jnp.zeros_like(carry)
            ga_ref[...] = jnp.zeros_like(ga_ref)
            gb_ref[...] = jnp.zeros_like(gb_ref)

        row = _row((tm, tn))
        dmv = dm_ref[...].astype(F32)
        gelu, dgelu = _gelu_parts(ca_ref[...].astype(F32))
        dcs = (dmv * cbo_ref[...].astype(F32) * dgelu, dmv * gelu)
        for k, (dc, u_ref, cw_ref, du_ref, g_ref) in enumerate(((dcs[0], ua_ref, cwa_ref, dua_ref, ga_ref),
                                                                (dcs[1], ub_ref, cwb_ref, dub_ref, gb_ref))):
            u = u_ref[...].astype(F32)
            after = carry[k]
            n1 = jnp.where(row == tm - 1, after[0:1, :], pltpu.roll(dc, tm - 1, axis=0))
            n2 = jnp.where(row == tm - 2, after[0:1, :], jnp.where(row == tm - 1, after[1:2, :], pltpu.roll(dc, tm - 2, axis=0)))
            g_ref[0:1, :] += jnp.sum(n2 * u, axis=0, keepdims=True)
            g_ref[1:2, :] += jnp.sum(n1 * u, axis=0, keepdims=True)
            g_ref[2:3, :] += jnp.sum(dc * u, axis=0, keepdims=True)
            g_ref[3:4, :] += jnp.sum(dc, axis=0, keepdims=True)
            du_ref[...] = (cw_ref[2:3, :] * dc + cw_ref[1:2, :] * n1 + cw_ref[0:1, :] * n2).astype(BF16)
            carry[k] = dc[0:8, :]

    tile = pl.BlockSpec((tm, tn), lambda j, i: (ni - 1 - i, j))
    gspec = pl.BlockSpec((8, tn), lambda j, i: (0, j))
    return pl.pallas_call(
        body, grid=(nj, ni),
        in_specs=[tile] * 5 + [pl.BlockSpec((3, tn), lambda j, i: (0, j)), pl.BlockSpec((3, tn), lambda j, i: (0, nj + j))],
        out_specs=[tile, tile, gspec, gspec],
        out_shape=[SDS((s, f), BF16), SDS((s, f), BF16), SDS((8, f), F32), SDS((8, f), F32)],
        scratch_shapes=[pltpu.VMEM((2, 8, tn), F32)], name="ffn_bwd",
        compiler_params=_cp(("parallel", "arbitrary")))(dm, ua, ub, ca, cbo, cw, cw)


def adamw(w, g, m, v, *, name, tr=None):
    r = w.shape[0]
    rest = w.shape[1:]
    if tr is None:
        tr = r
        for cand in (256, 128, 64, 32, 16, 8):
            if r % cand == 0:
                tr = cand
                break

    def body(w_ref, g_ref, m_ref, v_ref, d_ref, nm_ref, nv_ref):
        gv = g_ref[...]
        mn = ADAM_B1 * m_ref[...] + (1.0 - ADAM_B1) * gv
        vn = ADAM_B2 * v_ref[...] + (1.0 - ADAM_B2) * (gv * gv)
        m_hat = mn / (1.0 - ADAM_B1 ** ADAM_STEP)
        v_hat = vn / (1.0 - ADAM_B2 ** ADAM_STEP)
        d_ref[...] = -ADAM_LR * (m_hat / (jnp.sqrt(v_hat) + ADAM_EPS) + ADAM_WD * w_ref[...])
        nm_ref[...] = mn
        nv_ref[...] = vn

    blk = pl.BlockSpec((tr,) + rest, lambda i: (i,) + (0,) * len(rest))
    return pl.pallas_call(body, grid=(r // tr,), in_specs=[blk] * 4, out_specs=[blk] * 3, out_shape=[SDS(w.shape, F32)] * 3,
                          name=name, compiler_params=_cp(("parallel",)))(w, g, m, v)


def adamw_rows_view(w, g_mine, g_full, m, v, c_arr, *, name, tc=256):
    r, _, c = w.shape
    per_half = c // 2 // tc

    def body(c_ref, w_ref, gm_ref, gf_ref, m_ref, v_ref, d_ref, nm_ref, nv_ref, go_ref):
        mine = (pl.program_id(0) >> (per_half.bit_length() - 1)) == c_ref[0]
        gv = jnp.where(mine, gm_ref[...], gf_ref[...])[:, None, :]
        mn = ADAM_B1 * m_ref[...] + (1.0 - ADAM_B1) * gv
        vn = ADAM_B2 * v_ref[...] + (1.0 - ADAM_B2) * (gv * gv)
        m_hat = mn / (1.0 - ADAM_B1 ** ADAM_STEP)
        v_hat = vn / (1.0 - ADAM_B2 ** ADAM_STEP)
        d_ref[...] = -ADAM_LR * (m_hat / (jnp.sqrt(v_hat) + ADAM_EPS) + ADAM_WD * w_ref[...])
        nm_ref[...] = mn
        nv_ref[...] = vn
        go_ref[...] = gv

    b3 = pl.BlockSpec((r, 1, tc), lambda i, c_ref: (0, 0, i))
    own = pl.BlockSpec((r, tc), lambda i, c_ref: (0, jnp.clip(i - c_ref[0] * per_half, 0, per_half - 1)))
    full = pl.BlockSpec((r, tc), lambda i, c_ref: (0, i))
    grid_spec = pltpu.PrefetchScalarGridSpec(num_scalar_prefetch=1, grid=(c // tc,), in_specs=[b3, own, full, b3, b3],
                                             out_specs=[b3] * 4)
    return pl.pallas_call(body, grid_spec=grid_spec, out_shape=[SDS(w.shape, F32)] * 4, name=name,
                          compiler_params=_cp(("parallel",)))(c_arr, w, g_mine, g_full, m, v)


ANY = pl.BlockSpec(memory_space=pl.ANY)
ICI_KINDS = ("x", "y", "xy")


def _coords():
    return lax.axis_index("x"), lax.axis_index("y"), lax.axis_index("c")


def _peer(kind, x, y, c):
    if kind == "c":
        return (x, y, 1 - c)
    if kind == "x":
        return (1 - x, y, c)
    if kind == "y":
        return (x, 1 - y, c)
    return (1 - x, 1 - y, c)


def _chip_of(p):
    return 2 * p[0] + p[1]


def _half(rows, which):
    h = rows // 2
    return pl.ds(pl.multiple_of(which * h, 16), h)


def _remote(src, dst, send_sem, recv_sem, to):
    return pltpu.make_async_remote_copy(src_ref=src, dst_ref=dst, send_sem=send_sem, recv_sem=recv_sem,
                                        device_id=to, device_id_type=MESH)


def allgather_chips(shards, halved, *, name):
    n = len(shards)

    def body(*refs):
        parts = (refs[:n], refs[n:2 * n], refs[2 * n], refs[2 * n + 1], halved)
        _allgather_start(*parts)
        _allgather_finish(*parts)

    return pl.pallas_call(
        body, in_specs=[ANY] * n, out_specs=[ANY] * n,
        out_shape=_allgather_shapes(shards), scratch_shapes=_allgather_sems(n), name=name)(*shards)


def _allgather_shapes(shards):
    return [SDS((4,) + a.shape, a.dtype) for a in shards]


def _allgather_sems(n):
    return [pltpu.SemaphoreType.DMA((n, 6)), pltpu.SemaphoreType.DMA((n, 6))]


def _allgather_rows(ref, is_halved, which):
    r = ref.shape[0]
    return _half(r, which) if is_halved else pl.ds(0, r)


def _allgather_first(ins, outs, send_sems, recv_sems, halved):
    x, y, c = _coords()
    my_chip = 2 * x + y
    cps = []
    for w in range(len(ins)):
        rows = _allgather_rows(ins[w], halved[w], c)
        for k, kind in enumerate(ICI_KINDS):
            cps.append(_remote(ins[w].at[rows], outs[w].at[my_chip, rows], send_sems.at[w, k], recv_sems.at[w, k],
                               _peer(kind, x, y, c)))
    return cps


def _allgather_start(ins, outs, send_sems, recv_sems, halved):
    for cp in _allgather_first(ins, outs, send_sems, recv_sems, halved):
        cp.start()


def _allgather_finish(ins, outs, send_sems, recv_sems, halved):
    x, y, c = _coords()
    me = (x, y, c)
    second = []
    for w in range(len(ins)):
        for k, kind in enumerate(ICI_KINDS):
            landed = outs[w].at[_chip_of(_peer(kind, x, y, c)), _allgather_rows(ins[w], halved[w], c)]
            _remote(landed, landed, send_sems.at[w, k], recv_sems.at[w, k], me).wait_recv()
            if halved[w]:
                cp = _remote(landed, landed, send_sems.at[w, 3 + k], recv_sems.at[w, 3 + k], _peer("c", x, y, c))
                cp.start()
                second.append(cp)
    for w in range(len(ins)):
        if halved[w]:
            for k, kind in enumerate(ICI_KINDS):
                other = outs[w].at[_chip_of(_peer(kind, x, y, c)), _allgather_rows(ins[w], True, 1 - c)]
                _remote(other, other, send_sems.at[w, 3 + k], recv_sems.at[w, 3 + k], me).wait_recv()
    for cp in _allgather_first(ins, outs, send_sems, recv_sems, halved) + second:
        cp.wait_send()


def _half_of(ref, by_cols, which):
    lead = (slice(None),) * (len(ref.shape) - 2)
    if by_cols:
        h = ref.shape[-1] // 2
        return ref.at[lead + (slice(None), pl.ds(pl.multiple_of(which * h, LANES), h))]
    return ref.at[lead + (_half(ref.shape[-2], which),)]


def _half_shape(shape, by_cols):
    return shape[:-1] + (shape[-1] // 2,) if by_cols else shape[:-2] + (shape[-2] // 2, shape[-1])


def grads_to_sibling(gs, by_cols, *, name):
    n = len(gs)

    def body(*refs):
        ins, outs = refs[:n], refs[n:2 * n]
        send_sems, recv_sems = refs[2 * n:]
        x, y, c = _coords()
        cps = []
        for w in range(n):
            cp = _remote(_half_of(ins[w], by_cols[w], 1 - c), outs[w], send_sems.at[w], recv_sems.at[w], _peer("c", x, y, c))
            cp.start()
            cps.append(cp)
        for cp in cps:
            cp.wait()

    return pl.pallas_call(
        body, in_specs=[ANY] * n, out_specs=[ANY] * n,
        out_shape=[SDS(_half_shape(a.shape, bc), a.dtype) for a, bc in zip(gs, by_cols)],
        scratch_shapes=[pltpu.SemaphoreType.DMA((n,)), pltpu.SemaphoreType.DMA((n,))], name=name)(*gs)


def _to_chips_shapes(ps):
    return [SDS((3,) + a.shape[1:], a.dtype) for a in ps]


def _to_chips_sems(n):
    return [pltpu.SemaphoreType.DMA((n, 3)), pltpu.SemaphoreType.DMA((n, 3))]


def _to_chips_copies(ins, outs, send_sems, recv_sems):
    x, y, c = _coords()
    cps = []
    for w in range(len(ins)):
        for k, kind in enumerate(ICI_KINDS):
            to = _peer(kind, x, y, c)
            cps.append(_remote(ins[w].at[_chip_of(to)], outs[w].at[k], send_sems.at[w, k], recv_sems.at[w, k], to))
    return cps


def _to_chips_start(ins, outs, send_sems, recv_sems):
    for cp in _to_chips_copies(ins, outs, send_sems, recv_sems):
        cp.start()


def _to_chips_finish(ins, outs, send_sems, recv_sems):
    for cp in _to_chips_copies(ins, outs, send_sems, recv_sems):
        cp.wait()


def _to_owners_shapes(ps):
    return [SDS((7, a.shape[1] // 2, a.shape[2]), a.dtype) for a in ps]


def _to_owners_sems(n):
    return [pltpu.SemaphoreType.DMA((n, 7)), pltpu.SemaphoreType.DMA((n, 7))]


def _to_owners_copies(ins, outs, send_sems, recv_sems):
    x, y, c = _coords()
    cps = []
    for w in range(len(ins)):
        rows = ins[w].shape[1]
        for k, kind in enumerate(ICI_KINDS):
            px, py, _ = _peer(kind, x, y, c)
            for h in range(2):
                cps.append(_remote(ins[w].at[2 * px + py, _half(rows, h)], outs[w].at[2 * k + c],
                                   send_sems.at[w, 2 * k + h], recv_sems.at[w, 2 * k + c], (px, py, h)))
        cps.append(_remote(ins[w].at[2 * x + y, _half(rows, 1 - c)], outs[w].at[6], send_sems.at[w, 6], recv_sems.at[w, 6],
                           _peer("c", x, y, c)))
    return cps


def _to_owners_start(ins, outs, send_sems, recv_sems):
    for cp in _to_owners_copies(ins, outs, send_sems, recv_sems):
        cp.start()


def _to_owners_finish(ins, outs, send_sems, recv_sems):
    for cp in _to_owners_copies(ins, outs, send_sems, recv_sems):
        cp.wait_send()
    for w in range(len(ins)):
        for slot in range(7):
            got = outs[w].at[slot]
            _remote(got, got, send_sems.at[w, slot], recv_sems.at[w, slot], _coords()).wait_recv()


EXCHANGES = {"to_chips": (_to_chips_shapes, _to_chips_sems, _to_chips_start, _to_chips_finish),
             "to_owners": (_to_owners_shapes, _to_owners_sems, _to_owners_start, _to_owners_finish)}


def halves_to_full(hs, by_cols, *, name):
    n = len(hs)

    def body(*refs):
        ins, outs = refs[:n], refs[n:2 * n]
        send_sems, recv_sems = refs[2 * n:]
        x, y, c = _coords()
        cps = []
        for w in range(n):
            cp = _remote(ins[w], _half_of(outs[w], by_cols[w], c), send_sems.at[w], recv_sems.at[w], _peer("c", x, y, c))
            cp.start()
            cps.append(cp)
        for cp in cps:
            cp.wait()

    return pl.pallas_call(
        body, in_specs=[ANY] * n, out_specs=[ANY] * n,
        out_shape=[SDS((a.shape[0], 2 * a.shape[1]) if bc else (2 * a.shape[0], a.shape[1]), a.dtype)
                   for a, bc in zip(hs, by_cols)],
        scratch_shapes=[pltpu.SemaphoreType.DMA((n,)), pltpu.SemaphoreType.DMA((n,))],
        name=name)(*hs)


def _row_tile(rows):
    for cand in (256, 192, 176, 128, 64, 32, 16):
        if rows % cand == 0:
            return cand
    return rows


def chip_sum(g, recv, c_arr, by_cols, *, name):
    _, r, cols = g.shape

    def body(c_ref, g_ref, r_ref, f_ref, b_ref):
        tot = g_ref[...] + r_ref[...]
        f_ref[...] = tot
        b_ref[...] = tot.astype(BF16)

    if by_cols:
        tc = 2 * LANES
        nblk = cols // 2 // tc
        shape = (4, r, cols // 2)
        blk = pl.BlockSpec((None, r, tc), lambda j, i, c_ref: (j, 0, i))
        mine = pl.BlockSpec((None, r, tc), lambda j, i, c_ref: (j, 0, c_ref[0] * nblk + i))
    else:
        tr = _row_tile(r // 2)
        nblk = r // 2 // tr
        shape = (4, r // 2, cols)
        blk = pl.BlockSpec((None, tr, cols), lambda j, i, c_ref: (j, i, 0))
        mine = pl.BlockSpec((None, tr, cols), lambda j, i, c_ref: (j, c_ref[0] * nblk + i, 0))
    grid_spec = pltpu.PrefetchScalarGridSpec(num_scalar_prefetch=1, grid=(4, nblk), in_specs=[mine, blk], out_specs=[blk, blk])
    return pl.pallas_call(body, grid_spec=grid_spec, out_shape=[SDS(shape, F32), SDS(shape, BF16)],
                          name=name, compiler_params=_cp(("parallel", "parallel")))(c_arr, g, recv)


def final_sum(pf, recv, chip_arr, *, name):
    _, h, cols = pf.shape
    tr = _row_tile(h)

    def body(chip_ref, p_ref, r_ref, o_ref):
        o_ref[...] = ((p_ref[...] + r_ref[0].astype(F32)) + r_ref[1].astype(F32)) + r_ref[2].astype(F32)

    grid_spec = pltpu.PrefetchScalarGridSpec(
        num_scalar_prefetch=1, grid=(h // tr,),
        in_specs=[pl.BlockSpec((None, tr, cols), lambda i, chip_ref: (chip_ref[0], i, 0)),
                  pl.BlockSpec((3, tr, cols), lambda i, chip_ref: (0, i, 0))],
        out_specs=pl.BlockSpec((tr, cols), lambda i, chip_ref: (i, 0)))
    return pl.pallas_call(body, grid_spec=grid_spec, out_shape=SDS((h, cols), F32), name=name,
                          compiler_params=_cp(("parallel",)))(chip_arr, pf, recv)


def owner_sum(g, recv, pos_arr, *, name):
    _, r, cols = g.shape
    h = r // 2
    tr = _row_tile(h)
    nblk = h // tr

    def body(pos_ref, g_ref, r_ref, o_ref):
        tot = g_ref[...]
        for slot in range(7):
            tot = tot + r_ref[slot].astype(F32)
        o_ref[...] = tot

    grid_spec = pltpu.PrefetchScalarGridSpec(
        num_scalar_prefetch=1, grid=(nblk,),
        in_specs=[pl.BlockSpec((None, tr, cols), lambda i, pos: (pos[0], pos[1] * nblk + i, 0)),
                  pl.BlockSpec((7, tr, cols), lambda i, pos: (0, i, 0))],
        out_specs=pl.BlockSpec((tr, cols), lambda i, pos: (i, 0)))
    return pl.pallas_call(body, grid_spec=grid_spec, out_shape=SDS((h, cols), F32), name=name,
                          compiler_params=_cp(("parallel",)))(pos_arr, g, recv)


def allreduce_small(v, *, name):
    rws, cols = v.shape

    def body(v_ref, all_ref, sum_ref, send_sems, recv_sems, local_sem):
        x, y, c = _coords()
        me, sibling = (x, y, c), (x, y, 1 - c)
        chips = [(1 - x, y), (x, 1 - y), (1 - x, 1 - y)]

        def rows(px, py, pc):
            return all_ref.at[pl.ds(pl.multiple_of((4 * px + 2 * py + pc) * rws, 8), rws), :]

        def copy(k, block, to, src=None):
            return _remote(rows(*block) if src is None else src, rows(*block), send_sems.at[k], recv_sems.at[k], to)

        mine = pltpu.make_async_copy(v_ref, rows(*me), local_sem)
        mine.start()
        first = [copy(0, me, sibling, src=v_ref)]
        first += [copy(1 + j, me, (*chip, c), src=v_ref) for j, chip in enumerate(chips)]
        for cp in first:
            cp.start()
        passed = [copy(4 + j, (*chip, c), sibling) for j, chip in enumerate(chips)]
        for j, chip in enumerate(chips):
            copy(1 + j, (*chip, c), me).wait_recv()
            passed[j].start()
        copy(0, sibling, me).wait_recv()
        for j, chip in enumerate(chips):
            copy(4 + j, (*chip, 1 - c), me).wait_recv()
        for cp in first + passed:
            cp.wait_send()
        mine.wait()
        tot = all_ref[0:rws, :]
        for dev in range(1, 8):
            tot = tot + all_ref[dev * rws:(dev + 1) * rws, :]
        sum_ref[...] = tot

    vm = pl.BlockSpec(memory_space=pltpu.VMEM)
    return pl.pallas_call(
        body, in_specs=[vm], out_specs=[vm, vm],
        out_shape=[SDS((8 * rws, cols), v.dtype), SDS((rws, cols), v.dtype)],
        scratch_shapes=[pltpu.SemaphoreType.DMA((7,)), pltpu.SemaphoreType.DMA((7,)), pltpu.SemaphoreType.DMA],
        name=name)(v)[1]


def _pack_rows(parts, rows):
    out = []
    for a, r in zip(parts, rows):
        flat = a.reshape(-1)
        flat = jnp.pad(flat, (0, r * LANES - flat.shape[0]))
        out.append(flat.reshape(r, LANES))
    return jnp.concatenate(out, axis=0)


def _unpack_rows(packed, shapes, rows):
    out, at = [], 0
    for shp, r in zip(shapes, rows):
        size = int(np.prod(shp))
        out.append(packed[at:at + r].reshape(-1)[:size].reshape(shp))
        at += r
    return out


def kernel(x, g_pre_mix, w_in, b_forget, w_o_fox, w_o_dil, w_out, g_post_mix, g_pre_ffn, w_up, conv_w, conv_b, w_down, g_post_ffn, loss_target, m_g_pre_mix, m_w_in, m_b_forget, m_w_o_fox, m_w_o_dil, m_w_out, m_g_post_mix, m_g_pre_ffn, m_w_up, m_conv_w, m_conv_b, m_w_down, m_g_post_ffn, v_g_pre_mix, v_w_in, v_b_forget, v_w_o_fox, v_w_o_dil, v_w_out, v_g_post_mix, v_g_pre_ffn, v_w_up, v_conv_w, v_conv_b, v_w_down, v_g_post_ffn):
    xi, yi, ci = _coords()
    chip = 2 * xi + yi
    c_arr = jnp.reshape(ci, (1,)).astype(jnp.int32)
    chip_arr = jnp.reshape(chip, (1,)).astype(jnp.int32)
    xs = x[0]
    target = loss_target[0]
    s, d = xs.shape
    f_half = w_down.shape[1] * 4
    cols_in = w_in.shape[2]

    big = (w_in, w_o_fox, w_o_dil, w_out, w_up, w_down)
    shards = [w[0].astype(BF16) for w in big]
    a_in, a_cw = allgather_chips([shards[0], conv_w[0]], [True, False], name="allgather_w_in")
    w_in_full = jnp.concatenate([jnp.where(chip == j, shards[0], a_in[j]) for j in range(4)], axis=1)
    cw = jnp.concatenate([jnp.where(chip == j, conv_w[0], a_cw[j]) for j in range(4)], axis=1)
    nf = N_HEADS
    e_a, e_b = 3 * ATT_W, 3 * ATT_W + nf
    wz = jnp.concatenate([w_in_full[:, :e_a], w_in_full[:, e_b:]], axis=1)
    wf = jnp.pad(w_in_full[:, e_a:e_b], ((0, 0), (0, LANES - nf)))
    cb = conv_b
    bfo = jnp.pad(b_forget, ((0, 0), (0, LANES - nf)))

    h1 = rmsnorm_fwd(xs, g_pre_mix)
    z = mm([(h1, d, 0)], [(wz, d, 0)], nt=False, out_dtype=BF16, tm=1024, tn=512, name="in_proj")
    fa = mm([(h1, d, 0)], [(wf, d, 0)], nt=False, out_dtype=F32, tm=1024, tn=LANES, name="in_proj_forget")
    q_aug, k_aug, v_aug = fox_prep(z, fa, bfo)
    ya, lse_a, *late = fox_fwd(q_aug, k_aug, v_aug, gather=shards[1:], hps=N_HEADS)
    a_of, a_od, a_out, a_up, a_down = [
        lax.dynamic_update_index_in_dim(a4, own, chip, 0) for a4, own in zip(late, shards[1:])]
    wo_a = jnp.concatenate([a_of[j] for j in range(4)], axis=1)
    wo_b = jnp.concatenate([a_od[j] for j in range(4)], axis=1)
    w_o = a_out.reshape(d, d)
    w_dn = a_down.reshape(f_half, d)
    wu_a = jnp.concatenate([a_up[0], a_up[1]], axis=1)
    wu_b = jnp.concatenate([a_up[2], a_up[3]], axis=1)
    cos_t, sin_t = rope_cos_sin(s)
    yb, lse_b = dil_fwd_all(z, cos_t, sin_t)
    pa, pb, mixed = gate_mix(ya, yb, wo_a, wo_b, z)
    y1, x1, h2 = proj_norm_res(mixed, w_o, g_post_mix, xs, g_pre_ffn, name="out_proj")
    ua, ub, conv_a, conv_bh, mid = ffn_up(h2, wu_a, wu_b, cw, cb)
    dout, dy2, gg_post_ffn, sq = proj_norm_loss(mid, w_dn, g_post_ffn, x1, target, name="down_proj")
    loss = lax.psum(0.5 * sq[0, 0] / d, ("x", "y", "c"))

    dmid = mm([(dy2, d, 0)], [(w_dn, d, 0)], nt=True, out_dtype=BF16, tm=512, tn=f_half // 2, name="down_dgrad")
    dw_down, dw_down16 = wgrad((mid, f_half, 0), dy2, tk=f_half // 2, tn=1024, ts=1024, name="down_wgrad", bf16_copy=True)
    dua, dub, gc_a, gc_b = ffn_bwd(dmid, ua, ub, conv_a, conv_bh, cw)
    dx1, dy1, gg_pre_ffn, gg_post_mix = mm_norm_bwd(
        [(dua, f_half, 0), (dub, f_half, 0)], [(wu_a, f_half, 0), (wu_b, f_half, 0)],
        [(x1, g_pre_ffn, dout, F32), (y1, g_post_mix, None, BF16)], name="up_dgrad")
    dw_up = None
    for k, du in enumerate((dua, dub)):
        dw_up = wgrad((h2, d, 0), du, tk=1024, tn=f_half // 2, ts=1024, name=f"up_wgrad_{k}", chip_major=True,
                      slabs=(4, 2 * k), into=dw_up, bf16_copy=True)
    g_ffn = [(dw_up[0], dw_up[1]), (dw_down.reshape(4, f_half // 4, d), dw_down16.reshape(4, f_half // 4, d))]
    dw_out, dw_out16 = wgrad((mixed, d, 0), dy1, tk=1024, tn=1024, ts=1024, name="out_wgrad", bf16_copy=True)
    dpa, dpb, dz_g, dya, dyb, dd_a = mix_bwd(dy1, w_o, z, pa, pb, wo_a, wo_b, ya)
    by_chip_cols = lambda a: jnp.stack([a[:, j * (d // 4):(j + 1) * (d // 4)] for j in range(4)], axis=0)
    dw_of = [by_chip_cols(a) for a in wgrad((ya, ATT_W, 0), dpa, tk=ATT_W, tn=d, ts=1024, name="fox_o_wgrad", bf16_copy=True)]
    dw_od = [by_chip_cols(a) for a in wgrad((yb, ATT_W, 0), dpb, tk=ATT_W, tn=d, ts=1024, name="dil_o_wgrad", bf16_copy=True)]
    g_mix = [dw_of, dw_od, (dw_out.reshape(4, d // 4, d), dw_out16.reshape(4, d // 4, d))]
    dq_aug, dk_aug, dv_a, *got_ffn = fox_bwd(q_aug, k_aug, z, dya, lse_a, dd_a, exchange=[g[1] for g in g_ffn], kind="to_owners")
    dz_a, dfa, gg_bf = fox_post(dq_aug, dk_aug, dv_a, fa, bfo)
    *dz_b, got_of, got_od, got_out = dil_bwd_all(z, cos_t, sin_t, dyb, lse_b, yb, exchange=[g[1] for g in g_mix],
                                                 kind="to_owners")
    got_mix = [got_of, got_od, got_out]
    dwt_a = wgrad((dz_a, e_a, 0), h1, tk=e_a // 2, tn=d, ts=1024, name="in_wgrad_a")
    dwt_b = [wgrad((part, ATT_W, 0), h1, tk=ATT_W, tn=d, ts=1024, name=f"in_wgrad_b{k}") for k, part in enumerate(dz_b)]
    dwt_g = wgrad((dz_g, 2 * d, 0), h1, tk=d, tn=d, ts=1024, name="in_wgrad_g")
    dwt_f = wgrad((dfa, LANES, 0), h1, tk=LANES, tn=d, ts=1024, name="in_wgrad_f")
    dwt_full = jnp.concatenate([dwt_a, dwt_f[:nf], *dwt_b, dwt_g], axis=0)
    dw_in = jnp.stack([dwt_full[j * cols_in:(j + 1) * cols_in] for j in range(4)], axis=0)
    from_sib = grads_to_sibling([dw_in], [True], name="grads_to_sibling_in")
    sum_in = chip_sum(dw_in, from_sib[0], c_arr, True, name="chip_sum_w_in")
    grad_x, gg_pre_mix, got_in = mm_norm_bwd(
        [(dz_a, e_a, 0), *[(part, ATT_W, 0) for part in dz_b], (dz_g, d, 0), (dz_g, d, 1), (dfa, LANES, 0)],
        [(wz, e_a, 0), *[(wz, ATT_W, Z_QB + k) for k in range(3)], (wz, d, 3), (wz, d, 4), (wf, LANES, 0)],
        [(xs, g_pre_mix, dx1, F32)], exchange=[sum_in[1]], name="in_dgrad")

    names = ("w_in", "w_o_fox", "w_o_dil", "w_out", "w_up", "w_down")
    pos_arr = jnp.concatenate([chip_arr, c_arr])
    halves = [final_sum(sum_in[0], got_in, chip_arr, name="final_sum_w_in")] + [
        owner_sum(g[0], got, pos_arr, name=f"owner_sum_{nm}") for g, got, nm in zip(g_mix + g_ffn, got_mix + got_ffn, names[1:])]
    from_half = halves_to_full(halves, [True] + [False] * 5, name="halves_to_full")
    g_big = [None] + [lax.dynamic_update_slice_in_dim(full, mine, ci * mine.shape[0], axis=0)
                      for full, mine in zip(from_half[1:], halves[1:])]
    upd_big = [adamw(w[0], g, m[0], v[0], name=f"adamw_{nm}") for w, g, m, v, nm in list(zip(
        big, g_big, (m_w_in, m_w_o_fox, m_w_o_dil, m_w_out, m_w_up, m_w_down),
        (v_w_in, v_w_o_fox, v_w_o_dil, v_w_out, v_w_up, v_w_down), names))[1:]]
    to_t = lambda a: jnp.transpose(a, (2, 0, 1))
    from_t = lambda a: jnp.transpose(a, (1, 2, 0))
    *upd_in, g_in_t = adamw_rows_view(to_t(w_in), halves[0], from_half[0], to_t(m_w_in), to_t(v_w_in), c_arr,
                                      name="adamw_w_in")

    g_cw_loc = jnp.concatenate([gc_a[0:3], gc_b[0:3]], axis=1)
    g_cb_loc = jnp.concatenate([gc_a[3:4], gc_b[3:4]], axis=1)
    small_loc = [gg_pre_mix, gg_post_mix, gg_pre_ffn, gg_post_ffn, g_cb_loc, gg_bf[:, :nf], g_cw_loc]
    red_rows = (8, 8, 8, 8, 48, 8, 136)
    red = allreduce_small(_pack_rows(small_loc, red_rows), name="allreduce_small")
    g_pm, g_qm, g_pf, g_qf, g_cb, g_bf, g_cw_full = _unpack_rows(red, [a.shape for a in small_loc], red_rows)
    cols_cw = conv_w.shape[2]
    g_cw = lax.dynamic_slice_in_dim(g_cw_full, chip * cols_cw, cols_cw, axis=1)
    small_w = (g_pre_mix, g_post_mix, g_pre_ffn, g_post_ffn, conv_b, b_forget, conv_w[0])
    small_m = (m_g_pre_mix, m_g_post_mix, m_g_pre_ffn, m_g_post_ffn, m_conv_b, m_b_forget, m_conv_w[0])
    small_v = (v_g_pre_mix, v_g_post_mix, v_g_pre_ffn, v_g_post_ffn, v_conv_b, v_b_forget, v_conv_w[0])
    small_g = (g_pm, g_qm, g_pf, g_qf, g_cb, g_bf, g_cw)
    ad_rows = (8, 8, 8, 8, 48, 8, 40)
    packed = [_pack_rows(t, ad_rows) for t in (small_w, small_g, small_m, small_v)]
    upd_small = [_unpack_rows(o, [a.shape for a in small_w], ad_rows) for o in adamw(*packed, name="adamw_small")]

    order = ("g_pre_mix", "w_in", "b_forget", "w_o_fox", "w_o_dil", "w_out", "g_post_mix", "g_pre_ffn", "w_up", "conv_w",
             "conv_b", "w_down", "g_post_ffn")
    small_names = ("g_pre_mix", "g_post_mix", "g_pre_ffn", "g_post_ffn", "conv_b", "b_forget", "conv_w")
    grads, deltas, new_ms, new_vs = {}, {}, {}, {}
    grads["w_in"] = from_t(g_in_t)
    deltas["w_in"], new_ms["w_in"], new_vs["w_in"] = (from_t(a) for a in upd_in)
    for k, nm in enumerate(names[1:]):
        grads[nm] = g_big[k + 1][None]
        deltas[nm], new_ms[nm], new_vs[nm] = (a[None] for a in upd_big[k])
    for k, nm in enumerate(small_names):
        lead = (lambda a: a[None]) if nm == "conv_w" else (lambda a: a)
        grads[nm] = lead(small_g[k])
        deltas[nm], new_ms[nm], new_vs[nm] = (lead(upd_small[j][k]) for j in range(3))
    return (loss, grad_x[None], *[grads[nm] for nm in order], *[deltas[nm] for nm in order],
            *[new_ms[nm] for nm in order], *[new_vs[nm] for nm in order])
```

```python
import functools
import math

import numpy as np
import jax
import jax.numpy as jnp
from jax import lax
from jax.experimental import pallas as pl
from jax.experimental.pallas import tpu as pltpu

F32 = jnp.float32
BF16 = jnp.bfloat16
SDS = jax.ShapeDtypeStruct
MESH = pl.DeviceIdType.MESH

HEAD_DIM = 64
N_HEADS = 8
LANES = 128
ATT_W = N_HEADS * HEAD_DIM
DIL_PATTERNS = ((128, 1), (512, 4), (2048, 16))
DIL_BLK = 128
ROPE_DIM = HEAD_DIM // 4
ROPE_THETA = 500000.0
RMS_EPS = 1e-6
NEG = -1e30
QK_SCALE = 1.0 / math.sqrt(HEAD_DIM)
ADAM_LR, ADAM_B1, ADAM_B2, ADAM_EPS, ADAM_WD, ADAM_STEP = 0.001, 0.9, 0.999, 1e-08, 0.01, 10
VMEM_LIMIT = 56 * 1024 * 1024

Z_QA, Z_KA, Z_VA, Z_QB, Z_KB, Z_VB = 0, 1, 2, 3, 4, 5
Z_W = 5120


def _cp(sem):
    return pltpu.CompilerParams(dimension_semantics=sem, vmem_limit_bytes=VMEM_LIMIT)


def _nt(a, b):
    return lax.dot_general(a, b, (((1,), (1,)), ((), ())), preferred_element_type=F32)


def _tn(a, b):
    return lax.dot_general(a, b, (((0,), (0,)), ((), ())), preferred_element_type=F32)


def _nn(a, b):
    return jnp.dot(a, b, preferred_element_type=F32)


def _lane(shape):
    return lax.broadcasted_iota(jnp.int32, shape, 1)


def _row(shape):
    return lax.broadcasted_iota(jnp.int32, shape, 0)


def rmsnorm_fwd(x, g, *, tm=512):
    s, d = x.shape

    def body(x_ref, g_ref, h_ref):
        xv = x_ref[...]
        inv = lax.rsqrt(jnp.mean(xv * xv, axis=-1, keepdims=True) + RMS_EPS)
        h_ref[...] = (xv * inv * g_ref[...]).astype(h_ref.dtype)

    return pl.pallas_call(
        body, grid=(s // tm,),
        in_specs=[pl.BlockSpec((tm, d), lambda i: (i, 0)), pl.BlockSpec((1, d), lambda i: (0, 0))],
        out_specs=pl.BlockSpec((tm, d), lambda i: (i, 0)),
        out_shape=SDS((s, d), BF16), name="rmsnorm_fwd", compiler_params=_cp(("parallel",)))(x, g)


def mm(a_views, b_views, *, nt, out_dtype, tm, tn, name):
    n_p = len(a_views)
    m = a_views[0][0].shape[0]
    n = b_views[0][0].shape[0] if nt else b_views[0][0].shape[1]

    def body(*refs):
        o_ref = refs[-1]
        acc = None
        for p in range(n_p):
            av = refs[p][...].astype(BF16)
            bv = refs[n_p + p][...].astype(BF16)
            dv = _nt(av, bv) if nt else _nn(av, bv)
            acc = dv if acc is None else acc + dv
        o_ref[...] = acc.astype(o_ref.dtype)

    in_specs = []
    for arr, w, blk in a_views:
        in_specs.append(pl.BlockSpec((tm, w), functools.partial(lambda i, j, blk: (i, blk), blk=blk)))
    for arr, w, blk in b_views:
        if nt:
            in_specs.append(pl.BlockSpec((tn, w), functools.partial(lambda i, j, blk: (j, blk), blk=blk)))
        else:
            in_specs.append(pl.BlockSpec((w, tn), lambda i, j: (0, j)))
    return pl.pallas_call(
        body, grid=(m // tm, n // tn), in_specs=in_specs,
        out_specs=pl.BlockSpec((tm, tn), lambda i, j: (i, j)),
        out_shape=SDS((m, n), out_dtype), name=name,
        compiler_params=_cp(("parallel", "parallel")))(*[a[0] for a in a_views], *[b[0] for b in b_views])


def wgrad(a_view, g, *, tk, tn, ts, name, chip_major=False, slabs=None, into=None, bf16_copy=False):
    arr, ka, blk = a_view
    s, n = g.shape
    ns = s // ts
    total, first = slabs if slabs else (n // tn, 0)
    n_into = 0 if into is None else (2 if bf16_copy else 1)

    def body(a_ref, g_ref, *rest):
        o_ref = rest[n_into]

        @pl.when(pl.program_id(2) == 0)
        def _():
            o_ref[...] = jnp.zeros_like(o_ref)

        o_ref[...] += _tn(a_ref[...].astype(BF16), g_ref[...].astype(BF16))
        if bf16_copy:
            @pl.when(pl.program_id(2) == ns - 1)
            def _():
                rest[n_into + 1][...] = o_ref[...].astype(BF16)

    if chip_major:
        out_spec = pl.BlockSpec((None, tk, tn), lambda i, j, k: (first + j, i, 0))
        shape = (total, ka, tn)
    else:
        out_spec = pl.BlockSpec((tk, tn), lambda i, j, k: (i, j))
        shape = (ka, n)
    in_specs = [pl.BlockSpec((ts, tk), lambda i, j, k: (k, blk * (ka // tk) + i)),
                pl.BlockSpec((ts, tn), lambda i, j, k: (k, j))]
    args = [arr, g]
    if into is not None:
        earlier = list(into) if bf16_copy else [into]
        in_specs += [pl.BlockSpec(memory_space=pl.ANY)] * len(earlier)
        args += earlier
    out = pl.pallas_call(
        body, grid=(ka // tk, n // tn, ns), in_specs=in_specs,
        out_specs=[out_spec, out_spec] if bf16_copy else out_spec,
        out_shape=[SDS(shape, F32), SDS(shape, BF16)] if bf16_copy else SDS(shape, F32), name=name,
        input_output_aliases={2 + k: k for k in range(n_into)},
        compiler_params=_cp(("parallel", "parallel", "arbitrary")))(*args)
    return out


def _norm_bwd_rows(dh, xh, inv, g):
    dxh = dh * g
    dx = inv * (dxh - xh * jnp.mean(dxh * xh, axis=-1, keepdims=True))
    return dx, jnp.sum((dh * xh).reshape(dh.shape[0] // 8, 8, dh.shape[1]), axis=0)


def proj_norm_res(a, w, g, xres, g_next, *, tm=512, name):
    s, k = a.shape
    d = w.shape[1]

    def body(a_ref, w_ref, g_ref, x_ref, gn_ref, y_ref, o_ref, h_ref):
        y = _nn(a_ref[...], w_ref[...])
        inv = lax.rsqrt(jnp.mean(y * y, axis=-1, keepdims=True) + RMS_EPS)
        xn = x_ref[...] + y * inv * g_ref[...]
        y_ref[...] = y
        o_ref[...] = xn
        inv_n = lax.rsqrt(jnp.mean(xn * xn, axis=-1, keepdims=True) + RMS_EPS)
        h_ref[...] = (xn * inv_n * gn_ref[...]).astype(h_ref.dtype)

    row = pl.BlockSpec((tm, d), lambda i: (i, 0))
    vec = pl.BlockSpec((1, d), lambda i: (0, 0))
    return pl.pallas_call(
        body, grid=(s // tm,),
        in_specs=[pl.BlockSpec((tm, k), lambda i: (i, 0)), pl.BlockSpec((k, d), lambda i: (0, 0)), vec, row, vec],
        out_specs=[row, row, row], out_shape=[SDS((s, d), F32), SDS((s, d), F32), SDS((s, d), BF16)], name=name,
        compiler_params=_cp(("parallel",)))(a, w, g, xres, g_next)


def proj_norm_loss(a, w, g, xres, target, *, tm=512, name):
    s, k = a.shape
    d = w.shape[1]
    n = s // tm

    def body(a_ref, w_ref, g_ref, x_ref, t_ref, do_ref, dy_ref, dg_ref, l_ref, acc):
        i = pl.program_id(0)

        @pl.when(i == 0)
        def _():
            acc[...] = jnp.zeros_like(acc)
            l_ref[...] = jnp.zeros_like(l_ref)

        y = _nn(a_ref[...], w_ref[...])
        inv = lax.rsqrt(jnp.mean(y * y, axis=-1, keepdims=True) + RMS_EPS)
        yh = y * inv
        err = x_ref[...] + yh * g_ref[...] - t_ref[...]
        dout = err * (1.0 / d)
        do_ref[...] = dout
        l_ref[...] += jnp.sum(jnp.sum(err * err, axis=1, keepdims=True), axis=0, keepdims=True)
        dy, part = _norm_bwd_rows(dout, yh, inv, g_ref[...])
        dy_ref[...] = dy.astype(dy_ref.dtype)
        acc[...] += part

        @pl.when(i == n - 1)
        def _():
            dg_ref[...] = jnp.sum(acc[...], axis=0, keepdims=True)

    row = pl.BlockSpec((tm, d), lambda i: (i, 0))
    vec = pl.BlockSpec((1, d), lambda i: (0, 0))
    return pl.pallas_call(
        body, grid=(n,),
        in_specs=[pl.BlockSpec((tm, k), lambda i: (i, 0)), pl.BlockSpec((k, d), lambda i: (0, 0)), vec, row, row],
        out_specs=[row, row, vec, pl.BlockSpec((1, 1), lambda i: (0, 0))],
        out_shape=[SDS((s, d), F32), SDS((s, d), BF16), SDS((1, d), F32), SDS((1, 1), F32)],
        scratch_shapes=[pltpu.VMEM((8, d), F32)], name=name, compiler_params=_cp(("arbitrary",)))(a, w, g, xres, target)


def mm_norm_bwd(a_views, b_views, stages, exchange=(), *, tm=256, name):
    n_p, n_s, ne = len(a_views), len(stages), len(exchange)
    s = a_views[0][0].shape[0]
    d = b_views[0][0].shape[0]
    n = s // tm
    has_res = [st[2] is not None for st in stages]

    def body(*refs):
        a_refs, b_refs = refs[:n_p], refs[n_p:2 * n_p]
        at = 2 * n_p
        st_refs = []
        for k in range(n_s):
            cnt = 3 if has_res[k] else 2
            st_refs.append(refs[at:at + cnt])
            at += cnt
        e_ins = refs[at:at + ne]
        at += ne
        dx_refs, dg_refs = refs[at:at + n_s], refs[at + n_s:at + 2 * n_s]
        at += 2 * n_s
        e_outs = refs[at:at + ne]
        at += ne
        accs = refs[at:at + n_s]
        comm = (e_ins, e_outs) + tuple(refs[at + n_s:])
        i = pl.program_id(0)

        @pl.when(i == 0)
        def _():
            for acc in accs:
                acc[...] = jnp.zeros_like(acc)
            if ne:
                _to_chips_start(*comm)

        dh = None
        for p in range(n_p):
            part = _nt(a_refs[p][...].astype(BF16), b_refs[p][...].astype(BF16))
            dh = part if dh is None else dh + part
        for k in range(n_s):
            xv = st_refs[k][0][...]
            inv = lax.rsqrt(jnp.mean(xv * xv, axis=-1, keepdims=True) + RMS_EPS)
            dx, part = _norm_bwd_rows(dh, xv * inv, inv, st_refs[k][1][...])
            if has_res[k]:
                dx = dx + st_refs[k][2][...]
            dx_refs[k][...] = dx.astype(dx_refs[k].dtype)
            accs[k][...] += part
            dh = dx

        @pl.when(i == n - 1)
        def _():
            for k in range(n_s):
                dg_refs[k][...] = jnp.sum(accs[k][...], axis=0, keepdims=True)
            if ne:
                _to_chips_finish(*comm)

    row = pl.BlockSpec((tm, d), lambda i: (i, 0))
    vec = pl.BlockSpec((1, d), lambda i: (0, 0))
    in_specs, args = [], []
    for arr, w, blk in a_views:
        in_specs.append(pl.BlockSpec((tm, w), functools.partial(lambda i, blk: (i, blk), blk=blk)))
        args.append(arr)
    for arr, w, blk in b_views:
        in_specs.append(pl.BlockSpec((d, w), functools.partial(lambda i, blk: (0, blk), blk=blk)))
        args.append(arr)
    for x, g, res, _ in stages:
        in_specs += [row, vec] + ([row] if res is not None else [])
        args += [x, g] + ([res] if res is not None else [])
    return pl.pallas_call(
        body, grid=(n,), in_specs=in_specs + [ANY] * ne,
        out_specs=[row] * n_s + [vec] * n_s + [ANY] * ne,
        out_shape=[SDS((s, d), st[3]) for st in stages] + [SDS((1, d), F32)] * n_s + _to_chips_shapes(exchange),
        scratch_shapes=[pltpu.VMEM((8, d), F32)] * n_s + (_to_chips_sems(ne) if ne else []), name=name,
        compiler_params=_cp(("arbitrary",)))(*args, *exchange)


def _split3(v):
    hi = v.astype(BF16).astype(F32)
    r = v - hi
    mid = r.astype(BF16).astype(F32)
    lo = (r - mid).astype(BF16).astype(F32)
    return hi, mid, lo


def _tri(n, upper):
    r = np.arange(n)
    m = (r[:, None] <= r[None, :]) if upper else (r[:, None] >= r[None, :])
    return jnp.asarray(m.astype(np.float32))


def fox_prep(z, fa, bfo, *, tb=512):
    s = z.shape[0]
    n = s // tb

    def body(q_ref, k_ref, v_ref, fa_ref, b_ref, tri_ref, qa_ref, ka_ref, va_ref, carry):
        @pl.when(pl.program_id(0) == 0)
        def _():
            carry[...] = jnp.zeros_like(carry)

        xv = fa_ref[...] + b_ref[...]
        logf = jnp.minimum(xv, 0.0) - jnp.log(1.0 + jnp.exp(-jnp.abs(xv)))
        csum = jnp.dot(tri_ref[...], logf, preferred_element_type=F32, precision=lax.Precision.HIGHEST) + carry[0:1, :]
        carry[0:1, :] = csum[tb - 1:tb, :]
        lane = _lane((tb, LANES))
        for h in range(N_HEADS):
            hi, mid, lo = _split3(csum[:, h:h + 1])
            pair = (h // 2) * LANES
            qv = q_ref[:, pair:pair + LANES].astype(F32)
            kv = k_ref[:, pair:pair + LANES].astype(F32)
            vv = v_ref[:, pair:pair + LANES].astype(F32)
            if h % 2:
                qv = pltpu.roll(qv, 64, axis=1)
                kv = pltpu.roll(kv, 64, axis=1)
                vv = pltpu.roll(vv, 64, axis=1)
            va_ref[:, h * LANES:(h + 1) * LANES] = jnp.where(lane < 64, vv, jnp.where(lane == 64, 1.0, 0.0)).astype(BF16)
            one = jnp.where((lane >= 67) & (lane < 70), 1.0, 0.0)
            q_x = jnp.where(lane == 64, hi, jnp.where(lane == 65, mid, jnp.where(lane == 66, lo, one)))
            one = jnp.where((lane >= 64) & (lane < 67), 1.0, 0.0)
            k_x = jnp.where(lane == 67, -hi, jnp.where(lane == 68, -mid, jnp.where(lane == 69, -lo, one)))
            qa_ref[:, h * LANES:(h + 1) * LANES] = jnp.where(lane < 64, qv * QK_SCALE, q_x).astype(BF16)
            ka_ref[:, h * LANES:(h + 1) * LANES] = jnp.where(lane < 64, kv, k_x).astype(BF16)

    return pl.pallas_call(
        body, grid=(n,),
        in_specs=[pl.BlockSpec((tb, ATT_W), lambda i: (i, Z_QA)), pl.BlockSpec((tb, ATT_W), lambda i: (i, Z_KA)),
                  pl.BlockSpec((tb, ATT_W), lambda i: (i, Z_VA)),
                  pl.BlockSpec((tb, LANES), lambda i: (i, 0)), pl.BlockSpec((1, LANES), lambda i: (0, 0)),
                  pl.BlockSpec((tb, tb), lambda i: (0, 0))],
        out_specs=[pl.BlockSpec((tb, N_HEADS * LANES), lambda i: (i, 0))] * 3,
        out_shape=[SDS((s, N_HEADS * LANES), BF16)] * 3,
        scratch_shapes=[pltpu.VMEM((8, LANES), F32)],
        name="fox_prep", compiler_params=_cp(("arbitrary",)))(z, z, z, fa, bfo, _tri(tb, False))


def _causal_pairs(n, k_major):
    if k_major:
        pairs = [(qi, kj) for kj in range(n) for qi in range(kj, n)]
    else:
        pairs = [(qi, kj) for qi in range(n) for kj in range(qi + 1)]
    return (jnp.asarray([p[0] for p in pairs], jnp.int32), jnp.asarray([p[1] for p in pairs], jnp.int32), len(pairs))


def fox_fwd(q_aug, k_aug, v_aug, gather=(), *, t=512, hps=4):
    s = v_aug.shape[0]
    qi_arr, kj_arr, n_pairs = _causal_pairs(s // t, False)
    ng = len(gather)
    n_groups = N_HEADS // hps

    def body(qi_ref, kj_ref, q_ref, k_ref, v_ref, *rest):
        g_ins, (o_ref, lse_ref), g_outs = rest[:ng], rest[ng:ng + 2], rest[ng + 2:2 * ng + 2]
        m_scr, acc_scr = rest[2 * ng + 2:2 * ng + 4]
        comm = (g_ins, g_outs) + tuple(rest[2 * ng + 4:]) + ([True] * ng,)
        step = pl.program_id(1)
        qi = qi_ref[step]
        kj = kj_ref[step]
        if ng:
            @pl.when((pl.program_id(0) == 0) & (step == 0))
            def _():
                _allgather_start(*comm)

        @pl.when(kj == 0)
        def _():
            m_scr[...] = jnp.full_like(m_scr, NEG)
            acc_scr[...] = jnp.zeros_like(acc_scr)

        def update(masked):
            for i in range(hps):
                sc = _nt(q_ref[:, i * LANES:(i + 1) * LANES], k_ref[:, i * LANES:(i + 1) * LANES])
                if masked:
                    sc = jnp.where(_row((t, t)) >= _lane((t, t)), sc, NEG)
                m_prev = m_scr[i]
                m_new = jnp.maximum(m_prev, jnp.max(sc, axis=-1, keepdims=True))
                p = jnp.exp((sc - jnp.tile(m_new, (1, t // LANES))).astype(BF16))
                acc_scr[i] = jnp.exp(m_prev - m_new) * acc_scr[i] + _nn(p, v_ref[:, i * LANES:(i + 1) * LANES])
                m_scr[i] = m_new

        @pl.when(kj < qi)
        def _():
            update(False)

        @pl.when(kj == qi)
        def _():
            update(True)
            lane = _lane((t, LANES))
            for pr in range(hps // 2):
                den = [acc_scr[2 * pr + i][:, 64:65] for i in range(2)]
                o_ref[:, pr * LANES:(pr + 1) * LANES] = jnp.where(
                    lane < 64, acc_scr[2 * pr] / den[0], pltpu.roll(acc_scr[2 * pr + 1] / den[1], 64, axis=1)).astype(o_ref.dtype)
                lse_ref[:, pr * LANES:(pr + 1) * LANES] = jnp.where(
                    lane < 64, m_scr[2 * pr] + jnp.log(den[0]), m_scr[2 * pr + 1] + jnp.log(den[1]))

        if ng:
            @pl.when((pl.program_id(0) == n_groups - 1) & (step == n_pairs - 1))
            def _():
                _allgather_finish(*comm)

    wide = hps * LANES
    grid_spec = pltpu.PrefetchScalarGridSpec(
        num_scalar_prefetch=2, grid=(n_groups, n_pairs),
        in_specs=[pl.BlockSpec((t, wide), lambda hg, st, qi, kj: (qi[st], hg)),
                  pl.BlockSpec((t, wide), lambda hg, st, qi, kj: (kj[st], hg)),
                  pl.BlockSpec((t, wide), lambda hg, st, qi, kj: (kj[st], hg))] + [ANY] * ng,
        out_specs=[pl.BlockSpec((t, wide // 2), lambda hg, st, qi, kj: (qi[st], hg))] * 2 + [ANY] * ng,
        scratch_shapes=[pltpu.VMEM((hps, t, LANES), F32)] * 2 + (_allgather_sems(ng) if ng else []))
    return pl.pallas_call(
        body, grid_spec=grid_spec, out_shape=[SDS((s, ATT_W), BF16), SDS((s, ATT_W), F32)] + _allgather_shapes(gather),
        name="fox_fwd", compiler_params=_cp(("arbitrary", "arbitrary")))(qi_arr, kj_arr, q_aug, k_aug, v_aug, *gather)


def fox_bwd(q_aug, k_aug, z, dy, lse, dd, exchange=(), kind="to_chips", *, t=512, hps=4):
    s = z.shape[0]
    qi_arr, kj_arr, n_pairs = _causal_pairs(s // t, True)
    ne = len(exchange)
    n_groups = N_HEADS // hps
    x_shapes, x_sems, x_start, x_finish = EXCHANGES[kind]

    def body(qi_ref, kj_ref, q_ref, k_ref, v_ref, do_ref, lse_ref, dd_ref, *rest):
        e_ins, (dq_ref, dk_ref, dv_ref), e_outs = rest[:ne], rest[ne:ne + 3], rest[ne + 3:2 * ne + 3]
        comm = (e_ins, e_outs) + tuple(rest[2 * ne + 3:])
        step = pl.program_id(1)
        qi = qi_ref[step]
        kj = kj_ref[step]
        if ne:
            @pl.when((pl.program_id(0) == 0) & (step == 0))
            def _():
                x_start(*comm)

        @pl.when(step == 0)
        def _():
            dq_ref[...] = jnp.zeros_like(dq_ref)

        @pl.when(qi == kj)
        def _():
            dk_ref[...] = jnp.zeros_like(dk_ref)
            dv_ref[...] = jnp.zeros_like(dv_ref)

        def update(masked):
            lane = _lane((t, LANES))
            rows = pl.ds(pl.multiple_of(qi * t, t), t)
            for pr in range(hps // 2):
                pair = slice(pr * LANES, (pr + 1) * LANES)
                dov = do_ref[:, pair]
                dv_new = None
                for i in range(2):
                    head = (lane < 64) if i == 0 else (lane >= 64)
                    own = slice((2 * pr + i) * LANES, (2 * pr + i + 1) * LANES)
                    col = slice(pr * LANES + i * 64, pr * LANES + i * 64 + 1)
                    qv = q_ref[:, own]
                    kv = k_ref[:, own]
                    sc = _nt(qv, kv)
                    if masked:
                        sc = jnp.where(_row((t, t)) >= _lane((t, t)), sc, NEG)
                    p = jnp.exp(sc - lse_ref[:, col])
                    dp = _nt(jnp.where(head, dov, jnp.zeros_like(dov)), v_ref[:, pair])
                    ds = (p * (dp - dd_ref[:, col])).astype(BF16)
                    dq_ref[rows, own] += _nn(ds, kv)
                    dk_ref[:, own] += _tn(ds, qv)
                    dvi = _tn(p.astype(BF16), dov)
                    dv_new = dvi if dv_new is None else jnp.where(head, dvi, dv_new)
                dv_ref[:, pair] += dv_new

        @pl.when(kj < qi)
        def _():
            update(False)

        @pl.when(kj == qi)
        def _():
            update(True)

        if ne:
            @pl.when((pl.program_id(0) == n_groups - 1) & (step == n_pairs - 1))
            def _():
                x_finish(*comm)

    wide, half = hps * LANES, hps // 2 * LANES
    v_blk = Z_VA * ATT_W // half
    grid_spec = pltpu.PrefetchScalarGridSpec(
        num_scalar_prefetch=2, grid=(n_groups, n_pairs),
        in_specs=[pl.BlockSpec((t, wide), lambda hg, st, qi, kj: (qi[st], hg)),
                  pl.BlockSpec((t, wide), lambda hg, st, qi, kj: (kj[st], hg)),
                  pl.BlockSpec((t, half), lambda hg, st, qi, kj: (kj[st], v_blk + hg)),
                  pl.BlockSpec((t, half), lambda hg, st, qi, kj: (qi[st], hg)),
                  pl.BlockSpec((t, half), lambda hg, st, qi, kj: (qi[st], hg)),
                  pl.BlockSpec((t, half), lambda hg, st, qi, kj: (qi[st], hg))] + [ANY] * ne,
        out_specs=[pl.BlockSpec((s, wide), lambda hg, st, qi, kj: (0, hg)),
                   pl.BlockSpec((t, wide), lambda hg, st, qi, kj: (kj[st], hg)),
                   pl.BlockSpec((t, half), lambda hg, st, qi, kj: (kj[st], hg))] + [ANY] * ne,
        scratch_shapes=x_sems(ne) if ne else [])
    return pl.pallas_call(
        body, grid_spec=grid_spec,
        out_shape=[SDS((s, N_HEADS * LANES), F32), SDS((s, N_HEADS * LANES), F32), SDS((s, ATT_W), F32)]
        + x_shapes(exchange),
        name="fox_bwd", compiler_params=_cp(("arbitrary", "arbitrary")))(qi_arr, kj_arr, q_aug, k_aug, z, dy, lse, dd, *exchange)


def fox_post(dq_aug, dk_aug, dv, fa, bfo, *, tb=512):
    s = dv.shape[0]
    n = s // tb

    def body(dq_ref, dk_ref, dv_ref, fa_ref, b_ref, tri_ref, dz_ref, dfa_ref, gb_ref, carry, acc):
        i = pl.program_id(0)

        @pl.when(i == 0)
        def _():
            carry[...] = jnp.zeros_like(carry)
            acc[...] = jnp.zeros_like(acc)

        lane = _lane((tb, LANES))
        d_f = jnp.zeros((tb, LANES), F32)
        for h in range(N_HEADS):
            col = dq_ref[:, h * LANES + 64:h * LANES + 65] - dk_ref[:, h * LANES + 67:h * LANES + 68]
            d_f = jnp.where(lane == h, col, d_f)
        suffix = jnp.dot(tri_ref[...], d_f, preferred_element_type=F32, precision=lax.Precision.HIGHEST) + carry[0:1, :]
        carry[0:1, :] = suffix[0:1, :]
        xv = fa_ref[...] + b_ref[...]
        dx = suffix * (1.0 / (1.0 + jnp.exp(xv)))
        dfa_ref[...] = dx.astype(dfa_ref.dtype)
        acc[...] += jnp.sum(dx.reshape(tb // 8, 8, LANES), axis=0)
        for hp in range(4):
            for src, off, scale in ((dq_ref, 0, QK_SCALE), (dk_ref, ATT_W, 1.0)):
                even = src[:, (2 * hp) * LANES:(2 * hp + 1) * LANES]
                odd = pltpu.roll(src[:, (2 * hp + 1) * LANES:(2 * hp + 2) * LANES], 64, axis=1)
                dz_ref[:, off + hp * LANES:off + (hp + 1) * LANES] = (jnp.where(lane < 64, even, odd) * scale).astype(BF16)
        dz_ref[:, 2 * ATT_W:3 * ATT_W] = dv_ref[...].astype(BF16)

        @pl.when(i == n - 1)
        def _():
            gb_ref[...] = jnp.sum(acc[...], axis=0, keepdims=True)

    rev = lambda i: (n - 1 - i, 0)
    return pl.pallas_call(
        body, grid=(n,),
        in_specs=[pl.BlockSpec((tb, N_HEADS * LANES), rev), pl.BlockSpec((tb, N_HEADS * LANES), rev),
                  pl.BlockSpec((tb, ATT_W), rev), pl.BlockSpec((tb, LANES), rev),
                  pl.BlockSpec((1, LANES), lambda i: (0, 0)), pl.BlockSpec((tb, tb), lambda i: (0, 0))],
        out_specs=[pl.BlockSpec((tb, 3 * ATT_W), rev), pl.BlockSpec((tb, LANES), rev),
                   pl.BlockSpec((1, LANES), lambda i: (0, 0))],
        out_shape=[SDS((s, 3 * ATT_W), BF16), SDS((s, LANES), BF16), SDS((1, LANES), F32)],
        scratch_shapes=[pltpu.VMEM((8, LANES), F32), pltpu.VMEM((8, LANES), F32)],
        name="fox_post", compiler_params=_cp(("arbitrary",)))(dq_aug, dk_aug, dv, fa, bfo, _tri(tb, True))


def rope_cos_sin(s):
    half = ROPE_DIM // 2
    inv_freq = ROPE_THETA ** (-jnp.arange(half, dtype=F32) * 2.0 / ROPE_DIM)
    ang = jnp.arange(s, dtype=F32)[:, None] * inv_freq[None, :]
    return jnp.tile(jnp.cos(ang), (1, LANES // half)), jnp.tile(jnp.sin(ang), (1, LANES // half))


def _rotate(x, cos, sin, sign):
    l64 = _lane(x.shape) & (HEAD_DIM - 1)
    first = l64 < ROPE_DIM // 2
    second = (l64 >= ROPE_DIM // 2) & (l64 < ROPE_DIM)
    from_next = jnp.where(first, -sign * sin, 0.0)
    from_prev = jnp.where(second, sign * sin, 0.0)
    return (x * jnp.where(first | second, cos, 1.0) + pltpu.roll(x, LANES - 8, axis=1) * from_next
            + pltpu.roll(x, 8, axis=1) * from_prev)


def _dil_rows(base, r):
    if r == 1:
        return pl.ds(pl.multiple_of(base, DIL_BLK), DIL_BLK)
    return pl.ds(base, DIL_BLK, stride=r)


def _dil_block(idx, r, nb):
    shift = nb.bit_length() - 1
    rho = idx >> shift
    n = idx & (nb - 1)
    base = rho + n * (r * DIL_BLK)
    return _dil_rows(base, r), _dil_rows(jnp.maximum(base - r * DIL_BLK, rho), r), n > 0


def _cat(a, b):
    return jnp.concatenate([a, b], axis=0)


def _two_heads(v, first_head):
    zero = jnp.zeros_like(v)
    return _cat(jnp.where(first_head, v, zero), jnp.where(first_head, zero, v))


def _dil_bands():
    b = DIL_BLK
    q = _row((2 * b, 2 * b)) & (b - 1)
    col = _lane((2 * b, 2 * b))
    return (col < b) & (col >= q), (col >= b) & (col - b <= q)


def _dil_load_qkv(zq_ref, zk_ref, zv_ref, cos_ref, sin_ref, q_ref, k_ref, v_ref, *, chunk=512):
    def step(i, carry):
        rows = pl.ds(pl.multiple_of(i * chunk, chunk), chunk)
        cos, sin = cos_ref[rows, :], sin_ref[rows, :]
        q_ref[rows, :] = _rotate(zq_ref[rows, :].astype(F32), cos, sin, 1.0) * QK_SCALE
        k_ref[rows, :] = _rotate(zk_ref[rows, :].astype(F32), cos, sin, 1.0)
        v_ref[rows, :] = zv_ref[rows, :].astype(F32)
        return carry

    lax.fori_loop(0, q_ref.shape[0] // chunk, step, 0)


def dil_fwd_all(z, cos_t, sin_t, *, unroll=8):
    s = z.shape[0]
    b = DIL_BLK
    n_blk = s // b

    def body(zq_ref, zk_ref, zv_ref, cos_ref, sin_ref, o_ref, l_ref, q_ref, k_ref, v_ref):
        _dil_load_qkv(zq_ref, zk_ref, zv_ref, cos_ref, sin_ref, q_ref, k_ref, v_ref)
        first_head = _lane((b, LANES)) < 64
        band_prev, band_cur = _dil_bands()
        for g, (_, r) in enumerate(DIL_PATTERNS):
            nb = n_blk // r

            def group(it, carry, g=g, r=r, nb=nb):
                loaded = []
                kc = vc = None
                for u in range(unroll):
                    rows_c, rows_p, has_prev = _dil_block(it * unroll + u, r, nb)
                    if u % min(nb, unroll):
                        kp, vp = kc, vc
                    else:
                        kp, vp = k_ref[rows_p, :].astype(BF16), v_ref[rows_p, :].astype(BF16)
                    kc, vc = k_ref[rows_c, :].astype(BF16), v_ref[rows_c, :].astype(BF16)
                    state = (o_ref[rows_c, :], l_ref[rows_c, :]) if g else None
                    loaded.append((rows_c, has_prev, [q_ref[rows_c, :].astype(BF16), kp, kc, vp, vc], state))
                done = []
                for rows_c, has_prev, (qv, kp, kc, vp, vc), state in loaded:
                    sc = jnp.where(band_cur | (band_prev & has_prev), _nt(_two_heads(qv, first_head), _cat(kp, kc)), NEG)
                    m = jnp.max(sc, axis=-1, keepdims=True)
                    p = jnp.exp(sc - m)
                    den = jnp.sum(p, axis=-1, keepdims=True)
                    both = _nn(p.astype(BF16), _cat(vp, vc)) / den
                    lse2 = m + jnp.log(den)
                    ov = jnp.where(first_head, both[:b], both[b:])
                    lse = jnp.where(first_head, lse2[:b], lse2[b:])
                    if state is not None:
                        m2 = jnp.maximum(state[1], lse)
                        wp = jnp.exp(state[1] - m2)
                        wn = jnp.exp(lse - m2)
                        ov = (wp * state[0] + wn * ov) / (wp + wn)
                        lse = m2 + jnp.log(wp + wn)
                    done.append((rows_c, ov, lse))
                for rows_c, ov, lse in done:
                    o_ref[rows_c, :] = ov
                    l_ref[rows_c, :] = lse
                return carry

            lax.fori_loop(0, n_blk // unroll, group, 0)

    col_blk = lambda k: pl.BlockSpec((s, LANES), lambda hp: (0, 4 * k + hp))
    table = pl.BlockSpec((s, LANES), lambda hp: (0, 0))
    out = pl.BlockSpec((s, LANES), lambda hp: (0, hp))
    return pl.pallas_call(
        body, grid=(4,), in_specs=[col_blk(Z_QB), col_blk(Z_KB), col_blk(Z_VB), table, table], out_specs=[out, out],
        out_shape=[SDS((s, ATT_W), F32)] * 2, scratch_shapes=[pltpu.VMEM((s, LANES), F32)] * 3, name="dil_fwd",
        compiler_params=_cp(("parallel",)))(z, z, z, cos_t, sin_t)


def dil_bwd_all(z, cos_t, sin_t, dy, lse, y, exchange=(), kind="to_chips", *, unroll=8):
    s = z.shape[0]
    b = DIL_BLK
    n_blk = s // b
    ne = len(exchange)
    x_shapes, x_sems, x_start, x_finish = EXCHANGES[kind]

    def body(zq_ref, zk_ref, zv_ref, cos_ref, sin_ref, do_ref, l_ref, y_ref, *rest):
        e_ins, (gq_ref, gk_ref, gv_ref), e_outs = rest[:ne], rest[ne:ne + 3], rest[ne + 3:2 * ne + 3]
        q_ref, k_ref, v_ref, dq_ref, dk_ref, dv_ref = rest[2 * ne + 3:2 * ne + 9]
        comm = (e_ins, e_outs) + tuple(rest[2 * ne + 9:])
        if ne:
            @pl.when(pl.program_id(0) == 0)
            def _():
                x_start(*comm)

        _dil_load_qkv(zq_ref, zk_ref, zv_ref, cos_ref, sin_ref, q_ref, k_ref, v_ref)
        dq_ref[...] = jnp.zeros_like(dq_ref)
        dk_ref[...] = jnp.zeros_like(dk_ref)
        dv_ref[...] = jnp.zeros_like(dv_ref)
        first_head = _lane((b, LANES)) < 64
        band_prev, band_cur = _dil_bands()
        for _, r in DIL_PATTERNS:
            nb = n_blk // r

            def group(it, carry, r=r, nb=nb):
                loaded = []
                kc = vc = None
                for u in range(unroll):
                    rows_c, rows_p, has_prev = _dil_block(it * unroll + u, r, nb)
                    if u % min(nb, unroll):
                        kp, vp = kc, vc
                    else:
                        kp, vp = k_ref[rows_p, :].astype(BF16), v_ref[rows_p, :].astype(BF16)
                    kc, vc = k_ref[rows_c, :].astype(BF16), v_ref[rows_c, :].astype(BF16)
                    vals = [q_ref[rows_c, :].astype(BF16), kp, kc, vp, vc, do_ref[rows_c, :], l_ref[rows_c, :], y_ref[rows_c, :]]
                    loaded.append((rows_c, rows_p, has_prev, vals))
                done = []
                for rows_c, rows_p, has_prev, (qv, kp, kc, vp, vc, dof, lv, yv) in loaded:
                    q2 = _two_heads(qv, first_head)
                    do2 = _two_heads(dof.astype(BF16), first_head)
                    kcat, vcat = _cat(kp, kc), _cat(vp, vc)
                    lse2 = _cat(lv[:, 0:1], lv[:, 64:65])
                    dd2 = jnp.sum(_two_heads(dof * yv, first_head), axis=-1, keepdims=True)
                    p = jnp.exp(jnp.where(band_cur | (band_prev & has_prev), _nt(q2, kcat), NEG) - lse2)
                    ds = (p * (_nt(do2, vcat) - dd2)).astype(BF16)
                    dq2 = _nn(ds, kcat)
                    dkcat = _tn(ds, q2)
                    dvcat = _tn(p.astype(BF16), do2)
                    done.append((rows_c, rows_p, (jnp.where(first_head, dq2[:b], dq2[b:]), dkcat[:b], dkcat[b:],
                                                  dvcat[:b], dvcat[b:])))
                for rows_c, rows_p, (dq, dk_p, dk_c, dv_p, dv_c) in done:
                    dq_ref[rows_c, :] += dq
                    dk_ref[rows_p, :] += dk_p
                    dk_ref[rows_c, :] += dk_c
                    dv_ref[rows_p, :] += dv_p
                    dv_ref[rows_c, :] += dv_c
                return carry

            lax.fori_loop(0, n_blk // unroll, group, 0)

        def finish(i, carry, chunk=512):
            rows = pl.ds(pl.multiple_of(i * chunk, chunk), chunk)
            cos, sin = cos_ref[rows, :], sin_ref[rows, :]
            gq_ref[rows, :] = (_rotate(dq_ref[rows, :], cos, sin, -1.0) * QK_SCALE).astype(BF16)
            gk_ref[rows, :] = _rotate(dk_ref[rows, :], cos, sin, -1.0).astype(BF16)
            gv_ref[rows, :] = dv_ref[rows, :].astype(BF16)
            return carry

        lax.fori_loop(0, s // 512, finish, 0)
        if ne:
            @pl.when(pl.program_id(0) == 3)
            def _():
                x_finish(*comm)

    col_blk = lambda k: pl.BlockSpec((s, LANES), lambda hp: (0, 4 * k + hp))
    table = pl.BlockSpec((s, LANES), lambda hp: (0, 0))
    nat = pl.BlockSpec((s, LANES), lambda hp: (0, hp))
    return pl.pallas_call(
        body, grid=(4,), in_specs=[col_blk(Z_QB), col_blk(Z_KB), col_blk(Z_VB), table, table, nat, nat, nat] + [ANY] * ne,
        out_specs=[nat, nat, nat] + [ANY] * ne, out_shape=[SDS((s, ATT_W), BF16)] * 3 + x_shapes(exchange),
        scratch_shapes=[pltpu.VMEM((s, LANES), F32)] * 6 + (x_sems(ne) if ne else []), name="dil_bwd",
        compiler_params=_cp(("arbitrary",)))(z, z, z, cos_t, sin_t, dy, lse, y, *exchange)


def _sigmoid(v):
    return 1.0 / (1.0 + jnp.exp(-v))


def gate_mix(ya, yb, wa, wb, z, *, tm=512, tn=512):
    s = ya.shape[0]
    d = wa.shape[1]
    ga_blk = 3 * ATT_W * 2 // tn
    gb_blk = ga_blk + d // tn

    def body(ya_ref, yb_ref, wa_ref, wb_ref, ga_ref, gb_ref, pa_ref, pb_ref, mx_ref):
        pa = _nn(ya_ref[...], wa_ref[...])
        pb = _nn(yb_ref[...].astype(BF16), wb_ref[...])
        pa_ref[...] = pa.astype(BF16)
        pb_ref[...] = pb.astype(BF16)
        mx_ref[...] = (_sigmoid(ga_ref[...].astype(F32)) * pa + _sigmoid(gb_ref[...].astype(F32)) * pb).astype(BF16)

    out = pl.BlockSpec((tm, tn), lambda i, j: (i, j))
    return pl.pallas_call(
        body, grid=(s // tm, d // tn),
        in_specs=[pl.BlockSpec((tm, ATT_W), lambda i, j: (i, 0)), pl.BlockSpec((tm, ATT_W), lambda i, j: (i, 0)),
                  pl.BlockSpec((ATT_W, tn), lambda i, j: (0, j)), pl.BlockSpec((ATT_W, tn), lambda i, j: (0, j)),
                  pl.BlockSpec((tm, tn), lambda i, j: (i, ga_blk + j)), pl.BlockSpec((tm, tn), lambda i, j: (i, gb_blk + j))],
        out_specs=[out, out, out], out_shape=[SDS((s, d), BF16)] * 3, name="gate_mix",
        compiler_params=_cp(("parallel", "parallel")))(ya, yb, wa, wb, z, z)


def mix_bwd(dy, w_o, z, pa, pb, wo_a, wo_b, ya, *, tm=256):
    s, d = dy.shape

    def body(dy_ref, wo_ref, ga_ref, gb_ref, pa_ref, pb_ref, wa_ref, wb_ref, ya_ref,
             dpa_ref, dpb_ref, dg_ref, dya_ref, dyb_ref, dd_ref):
        dm = _nt(dy_ref[...], wo_ref[...])
        sa = _sigmoid(ga_ref[...].astype(F32))
        sb = _sigmoid(gb_ref[...].astype(F32))
        dpa = (dm * sa).astype(BF16)
        dpb = (dm * sb).astype(BF16)
        dpa_ref[...] = dpa
        dpb_ref[...] = dpb
        dg_ref[:, 0:d] = (dm * pa_ref[...].astype(F32) * sa * (1.0 - sa)).astype(BF16)
        dg_ref[:, d:2 * d] = (dm * pb_ref[...].astype(F32) * sb * (1.0 - sb)).astype(BF16)
        dya = _nt(dpa, wa_ref[...]).astype(BF16)
        dya_ref[...] = dya
        dyb_ref[...] = _nt(dpb, wb_ref[...])
        lane = _lane((tm, LANES))
        for pr in range(ATT_W // LANES):
            pair = slice(pr * LANES, (pr + 1) * LANES)
            prod = dya[:, pair].astype(F32) * ya_ref[:, pair].astype(F32)
            lo = jnp.sum(jnp.where(lane < 64, prod, 0.0), axis=-1, keepdims=True)
            hi = jnp.sum(jnp.where(lane >= 64, prod, 0.0), axis=-1, keepdims=True)
            dd_ref[:, pair] = jnp.where(lane < 64, lo, hi)

    row = pl.BlockSpec((tm, d), lambda i: (i, 0))
    att = pl.BlockSpec((tm, ATT_W), lambda i: (i, 0))
    whole = lambda a: pl.BlockSpec(a.shape, lambda i: (0, 0))
    return pl.pallas_call(
        body, grid=(s // tm,),
        in_specs=[row, whole(w_o), pl.BlockSpec((tm, d), lambda i: (i, 3)), pl.BlockSpec((tm, d), lambda i: (i, 4)), row, row,
                  whole(wo_a), whole(wo_b), att],
        out_specs=[row, row, pl.BlockSpec((tm, 2 * d), lambda i: (i, 0)), att, att, att],
        out_shape=[SDS((s, d), BF16), SDS((s, d), BF16), SDS((s, 2 * d), BF16), SDS((s, ATT_W), BF16),
                   SDS((s, ATT_W), F32), SDS((s, ATT_W), F32)], name="mix_bwd",
        compiler_params=_cp(("parallel",)))(dy, w_o, z, z, pa, pb, wo_a, wo_b, ya)


GELU_C = math.sqrt(2.0 / math.pi)


def _gelu_parts(a):
    a2 = a * a
    th = jnp.tanh(a * (GELU_C + (GELU_C * 0.044715) * a2))
    half = 0.5 * a
    gelu = half + half * th
    dgelu = (0.5 + 0.5 * th) + half * (1.0 - th * th) * (GELU_C + (3.0 * GELU_C * 0.044715) * a2)
    return gelu, dgelu


def _causal_taps(u, before):
    row = _row(u.shape)
    r1 = jnp.where(row == 0, before[7:8, :], pltpu.roll(u, 1, axis=0))
    r2 = jnp.where(row == 0, before[6:7, :], jnp.where(row == 1, before[7:8, :], pltpu.roll(u, 2, axis=0)))
    return r1, r2


def ffn_up(h, wa, wb, cw, cb, *, tm=1024, tn=256):
    s, d = h.shape
    f = wa.shape[1]
    nj = f // tn

    def body(h_ref, wa_ref, wb_ref, cwa_ref, cwb_ref, cba_ref, cbb_ref, ua_ref, ub_ref, ca_ref, cbo_ref, m_ref, carry):
        @pl.when(pl.program_id(1) == 0)
        def _():
            carry[...] = jnp.zeros_like(carry)

        conv = []
        for k, (w_ref, cw_ref, cb_ref, u_ref, c_ref) in enumerate(((wa_ref, cwa_ref, cba_ref, ua_ref, ca_ref),
                                                                   (wb_ref, cwb_ref, cbb_ref, ub_ref, cbo_ref))):
            u16 = _nn(h_ref[...], w_ref[...]).astype(BF16)
            u_ref[...] = u16
            u = u16.astype(F32)
            r1, r2 = _causal_taps(u, carry[k])
            carry[k] = u[tm - 8:tm, :]
            c16 = (cw_ref[0:1, :] * r2 + cw_ref[1:2, :] * r1 + cw_ref[2:3, :] * u + cb_ref[...]).astype(BF16)
            c_ref[...] = c16
            conv.append(c16.astype(F32))
        m_ref[...] = (_gelu_parts(conv[0])[0] * conv[1]).astype(BF16)

    out = pl.BlockSpec((tm, tn), lambda j, i: (i, j))
    return pl.pallas_call(
        body, grid=(nj, s // tm),
        in_specs=[pl.BlockSpec((tm, d), lambda j, i: (i, 0)),
                  pl.BlockSpec((d, tn), lambda j, i: (0, j)), pl.BlockSpec((d, tn), lambda j, i: (0, j)),
                  pl.BlockSpec((3, tn), lambda j, i: (0, j)), pl.BlockSpec((3, tn), lambda j, i: (0, nj + j)),
                  pl.BlockSpec((1, tn), lambda j, i: (0, j)), pl.BlockSpec((1, tn), lambda j, i: (0, nj + j))],
        out_specs=[out] * 5, out_shape=[SDS((s, f), BF16)] * 5,
        scratch_shapes=[pltpu.VMEM((2, 8, tn), F32)], name="ffn_up",
        compiler_params=_cp(("parallel", "arbitrary")))(h, wa, wb, cw, cw, cb, cb)


def ffn_bwd(dm, ua, ub, ca, cbo, cw, *, tm=1024, tn=256):
    s, f = dm.shape
    nj = f // tn
    ni = s // tm

    def body(dm_ref, ua_ref, ub_ref, ca_ref, cbo_ref, cwa_ref, cwb_ref, dua_ref, dub_ref, ga_ref, gb_ref, carry):
        @pl.when(pl.program_id(1) == 0)
        def _():
            carry[...] = jnp.zeros_like(carry)
            ga_ref[...] = jnp.zeros_like(ga_ref)
            gb_ref[...] = jnp.zeros_like(gb_ref)

        row = _row((tm, tn))
        dmv = dm_ref[...].astype(F32)
        gelu, dgelu = _gelu_parts(ca_ref[...].astype(F32))
        dcs = (dmv * cbo_ref[...].astype(F32) * dgelu, dmv * gelu)
        for k, (dc, u_ref, cw_ref, du_ref, g_ref) in enumerate(((dcs[0], ua_ref, cwa_ref, dua_ref, ga_ref),
                                                                (dcs[1], ub_ref, cwb_ref, dub_ref, gb_ref))):
            u = u_ref[...].astype(F32)
            after = carry[k]
            n1 = jnp.where(row == tm - 1, after[0:1, :], pltpu.roll(dc, tm - 1, axis=0))
            n2 = jnp.where(row == tm - 2, after[0:1, :], jnp.where(row == tm - 1, after[1:2, :], pltpu.roll(dc, tm - 2, axis=0)))
            g_ref[0:1, :] += jnp.sum(n2 * u, axis=0, keepdims=True)
            g_ref[1:2, :] += jnp.sum(n1 * u, axis=0, keepdims=True)
            g_ref[2:3, :] += jnp.sum(dc * u, axis=0, keepdims=True)
            g_ref[3:4, :] += jnp.sum(dc, axis=0, keepdims=True)
            du_ref[...] = (cw_ref[2:3, :] * dc + cw_ref[1:2, :] * n1 + cw_ref[0:1, :] * n2).astype(BF16)
            carry[k] = dc[0:8, :]

    tile = pl.BlockSpec((tm, tn), lambda j, i: (ni - 1 - i, j))
    gspec = pl.BlockSpec((8, tn), lambda j, i: (0, j))
    return pl.pallas_call(
        body, grid=(nj, ni),
        in_specs=[tile] * 5 + [pl.BlockSpec((3, tn), lambda j, i: (0, j)), pl.BlockSpec((3, tn), lambda j, i: (0, nj + j))],
        out_specs=[tile, tile, gspec, gspec],
        out_shape=[SDS((s, f), BF16), SDS((s, f), BF16), SDS((8, f), F32), SDS((8, f), F32)],
        scratch_shapes=[pltpu.VMEM((2, 8, tn), F32)], name="ffn_bwd",
        compiler_params=_cp(("parallel", "arbitrary")))(dm, ua, ub, ca, cbo, cw, cw)


def adamw(w, g, m, v, *, name, tr=None):
    r = w.shape[0]
    rest = w.shape[1:]
    if tr is None:
        tr = r
        for cand in (256, 128, 64, 32, 16, 8):
            if r % cand == 0:
                tr = cand
                break

    def body(w_ref, g_ref, m_ref, v_ref, d_ref, nm_ref, nv_ref):
        gv = g_ref[...]
        mn = ADAM_B1 * m_ref[...] + (1.0 - ADAM_B1) * gv
        vn = ADAM_B2 * v_ref[...] + (1.0 - ADAM_B2) * (gv * gv)
        m_hat = mn / (1.0 - ADAM_B1 ** ADAM_STEP)
        v_hat = vn / (1.0 - ADAM_B2 ** ADAM_STEP)
        d_ref[...] = -ADAM_LR * (m_hat / (jnp.sqrt(v_hat) + ADAM_EPS) + ADAM_WD * w_ref[...])
        nm_ref[...] = mn
        nv_ref[...] = vn

    blk = pl.BlockSpec((tr,) + rest, lambda i: (i,) + (0,) * len(rest))
    return pl.pallas_call(body, grid=(r // tr,), in_specs=[blk] * 4, out_specs=[blk] * 3, out_shape=[SDS(w.shape, F32)] * 3,
                          name=name, compiler_params=_cp(("parallel",)))(w, g, m, v)


def adamw_rows_view(w, g_mine, g_full, m, v, c_arr, *, name, tc=256):
    r, _, c = w.shape
    per_half = c // 2 // tc

    def body(c_ref, w_ref, gm_ref, gf_ref, m_ref, v_ref, d_ref, nm_ref, nv_ref, go_ref):
        mine = (pl.program_id(0) >> (per_half.bit_length() - 1)) == c_ref[0]
        gv = jnp.where(mine, gm_ref[...], gf_ref[...])
        mn = ADAM_B1 * m_ref[:, 0, :] + (1.0 - ADAM_B1) * gv
        vn = ADAM_B2 * v_ref[:, 0, :] + (1.0 - ADAM_B2) * (gv * gv)
        m_hat = mn / (1.0 - ADAM_B1 ** ADAM_STEP)
        v_hat = vn / (1.0 - ADAM_B2 ** ADAM_STEP)
        d_ref[:, 0, :] = -ADAM_LR * (m_hat / (jnp.sqrt(v_hat) + ADAM_EPS) + ADAM_WD * w_ref[:, 0, :])
        nm_ref[:, 0, :] = mn
        nv_ref[:, 0, :] = vn
        go_ref[:, 0, :] = gv

    b3 = pl.BlockSpec((r, 1, tc), lambda i, c_ref: (0, 0, i))
    own = pl.BlockSpec((r, tc), lambda i, c_ref: (0, jnp.clip(i - c_ref[0] * per_half, 0, per_half - 1)))
    full = pl.BlockSpec((r, tc), lambda i, c_ref: (0, i))
    grid_spec = pltpu.PrefetchScalarGridSpec(num_scalar_prefetch=1, grid=(c // tc,), in_specs=[b3, own, full, b3, b3],
                                             out_specs=[b3] * 4)
    return pl.pallas_call(body, grid_spec=grid_spec, out_shape=[SDS(w.shape, F32)] * 4, name=name,
                          compiler_params=_cp(("parallel",)))(c_arr, w, g_mine, g_full, m, v)


ANY = pl.BlockSpec(memory_space=pl.ANY)
ICI_KINDS = ("x", "y", "xy")


def _coords():
    return lax.axis_index("x"), lax.axis_index("y"), lax.axis_index("c")


def _peer(kind, x, y, c):
    if kind == "c":
        return (x, y, 1 - c)
    if kind == "x":
        return (1 - x, y, c)
    if kind == "y":
        return (x, 1 - y, c)
    return (1 - x, 1 - y, c)


def _chip_of(p):
    return 2 * p[0] + p[1]


def _half(rows, which):
    h = rows // 2
    return pl.ds(pl.multiple_of(which * h, 16), h)


def _remote(src, dst, send_sem, recv_sem, to):
    return pltpu.make_async_remote_copy(src_ref=src, dst_ref=dst, send_sem=send_sem, recv_sem=recv_sem,
                                        device_id=to, device_id_type=MESH)


def allgather_chips(shards, halved, *, name):
    n = len(shards)

    def body(*refs):
        parts = (refs[:n], refs[n:2 * n], refs[2 * n], refs[2 * n + 1], halved)
        _allgather_start(*parts)
        _allgather_finish(*parts)

    return pl.pallas_call(
        body, in_specs=[ANY] * n, out_specs=[ANY] * n,
        out_shape=_allgather_shapes(shards), scratch_shapes=_allgather_sems(n), name=name)(*shards)


def _allgather_shapes(shards):
    return [SDS((4,) + a.shape, a.dtype) for a in shards]


def _allgather_sems(n):
    return [pltpu.SemaphoreType.DMA((n, 6)), pltpu.SemaphoreType.DMA((n, 6))]


def _allgather_rows(ref, is_halved, which):
    r = ref.shape[0]
    return _half(r, which) if is_halved else pl.ds(0, r)


def _allgather_first(ins, outs, send_sems, recv_sems, halved):
    x, y, c = _coords()
    my_chip = 2 * x + y
    cps = []
    for w in range(len(ins)):
        rows = _allgather_rows(ins[w], halved[w], c)
        for k, kind in enumerate(ICI_KINDS):
            cps.append(_remote(ins[w].at[rows], outs[w].at[my_chip, rows], send_sems.at[w, k], recv_sems.at[w, k],
                               _peer(kind, x, y, c)))
    return cps


def _allgather_start(ins, outs, send_sems, recv_sems, halved):
    for cp in _allgather_first(ins, outs, send_sems, recv_sems, halved):
        cp.start()


def _allgather_finish(ins, outs, send_sems, recv_sems, halved):
    x, y, c = _coords()
    me = (x, y, c)
    second = []
    for w in range(len(ins)):
        for k, kind in enumerate(ICI_KINDS):
            landed = outs[w].at[_chip_of(_peer(kind, x, y, c)), _allgather_rows(ins[w], halved[w], c)]
            _remote(landed, landed, send_sems.at[w, k], recv_sems.at[w, k], me).wait_recv()
            if halved[w]:
                cp = _remote(landed, landed, send_sems.at[w, 3 + k], recv_sems.at[w, 3 + k], _peer("c", x, y, c))
                cp.start()
                second.append(cp)
    for w in range(len(ins)):
        if halved[w]:
            for k, kind in enumerate(ICI_KINDS):
                other = outs[w].at[_chip_of(_peer(kind, x, y, c)), _allgather_rows(ins[w], True, 1 - c)]
                _remote(other, other, send_sems.at[w, 3 + k], recv_sems.at[w, 3 + k], me).wait_recv()
    for cp in _allgather_first(ins, outs, send_sems, recv_sems, halved) + second:
        cp.wait_send()


def _half_of(ref, by_cols, which):
    lead = (slice(None),) * (len(ref.shape) - 2)
    if by_cols:
        h = ref.shape[-1] // 2
        return ref.at[lead + (slice(None), pl.ds(pl.multiple_of(which * h, LANES), h))]
    return ref.at[lead + (_half(ref.shape[-2], which),)]


def _half_shape(shape, by_cols):
    return shape[:-1] + (shape[-1] // 2,) if by_cols else shape[:-2] + (shape[-2] // 2, shape[-1])


def grads_to_sibling(gs, by_cols, *, name):
    n = len(gs)

    def body(*refs):
        ins, outs = refs[:n], refs[n:2 * n]
        send_sems, recv_sems = refs[2 * n:]
        x, y, c = _coords()
        cps = []
        for w in range(n):
            cp = _remote(_half_of(ins[w], by_cols[w], 1 - c), outs[w], send_sems.at[w], recv_sems.at[w], _peer("c", x, y, c))
            cp.start()
            cps.append(cp)
        for cp in cps:
            cp.wait()

    return pl.pallas_call(
        body, in_specs=[ANY] * n, out_specs=[ANY] * n,
        out_shape=[SDS(_half_shape(a.shape, bc), a.dtype) for a, bc in zip(gs, by_cols)],
        scratch_shapes=[pltpu.SemaphoreType.DMA((n,)), pltpu.SemaphoreType.DMA((n,))], name=name)(*gs)


def _to_chips_shapes(ps):
    return [SDS((3,) + a.shape[1:], a.dtype) for a in ps]


def _to_chips_sems(n):
    return [pltpu.SemaphoreType.DMA((n, 3)), pltpu.SemaphoreType.DMA((n, 3))]


def _to_chips_copies(ins, outs, send_sems, recv_sems):
    x, y, c = _coords()
    cps = []
    for w in range(len(ins)):
        for k, kind in enumerate(ICI_KINDS):
            to = _peer(kind, x, y, c)
            cps.append(_remote(ins[w].at[_chip_of(to)], outs[w].at[k], send_sems.at[w, k], recv_sems.at[w, k], to))
    return cps


def _to_chips_start(ins, outs, send_sems, recv_sems):
    for cp in _to_chips_copies(ins, outs, send_sems, recv_sems):
        cp.start()


def _to_chips_finish(ins, outs, send_sems, recv_sems):
    for cp in _to_chips_copies(ins, outs, send_sems, recv_sems):
        cp.wait()


def _to_owners_shapes(ps):
    return [SDS((7, a.shape[1] // 2, a.shape[2]), a.dtype) for a in ps]


def _to_owners_sems(n):
    return [pltpu.SemaphoreType.DMA((n, 7)), pltpu.SemaphoreType.DMA((n, 7))]


def _to_owners_copies(ins, outs, send_sems, recv_sems):
    x, y, c = _coords()
    cps = []
    for w in range(len(ins)):
        rows = ins[w].shape[1]
        for k, kind in enumerate(ICI_KINDS):
            px, py, _ = _peer(kind, x, y, c)
            for h in range(2):
                cps.append(_remote(ins[w].at[2 * px + py, _half(rows, h)], outs[w].at[2 * k + c],
                                   send_sems.at[w, 2 * k + h], recv_sems.at[w, 2 * k + c], (px, py, h)))
        cps.append(_remote(ins[w].at[2 * x + y, _half(rows, 1 - c)], outs[w].at[6], send_sems.at[w, 6], recv_sems.at[w, 6],
                           _peer("c", x, y, c)))
    return cps


def _to_owners_start(ins, outs, send_sems, recv_sems):
    for cp in _to_owners_copies(ins, outs, send_sems, recv_sems):
        cp.start()


def _to_owners_finish(ins, outs, send_sems, recv_sems):
    for cp in _to_owners_copies(ins, outs, send_sems, recv_sems):
        cp.wait_send()
    for w in range(len(ins)):
        for slot in range(7):
            got = outs[w].at[slot]
            _remote(got, got, send_sems.at[w, slot], recv_sems.at[w, slot], _coords()).wait_recv()


EXCHANGES = {"to_chips": (_to_chips_shapes, _to_chips_sems, _to_chips_start, _to_chips_finish),
             "to_owners": (_to_owners_shapes, _to_owners_sems, _to_owners_start, _to_owners_finish)}


def halves_to_full(hs, by_cols, *, name):
    n = len(hs)

    def body(*refs):
        ins, outs = refs[:n], refs[n:2 * n]
        send_sems, recv_sems = refs[2 * n:]
        x, y, c = _coords()
        cps = []
        for w in range(n):
            cp = _remote(ins[w], _half_of(outs[w], by_cols[w], c), send_sems.at[w], recv_sems.at[w], _peer("c", x, y, c))
            cp.start()
            cps.append(cp)
        for cp in cps:
            cp.wait()

    return pl.pallas_call(
        body, in_specs=[ANY] * n, out_specs=[ANY] * n,
        out_shape=[SDS((a.shape[0], 2 * a.shape[1]) if bc else (2 * a.shape[0], a.shape[1]), a.dtype)
                   for a, bc in zip(hs, by_cols)],
        scratch_shapes=[pltpu.SemaphoreType.DMA((n,)), pltpu.SemaphoreType.DMA((n,))],
        name=name)(*hs)


def _row_tile(rows):
    for cand in (256, 192, 176, 128, 64, 32, 16):
        if rows % cand == 0:
            return cand
    return rows


def chip_sum(g, recv, c_arr, by_cols, *, name):
    _, r, cols = g.shape

    def body(c_ref, g_ref, r_ref, f_ref, b_ref):
        tot = g_ref[...] + r_ref[...]
        f_ref[...] = tot
        b_ref[...] = tot.astype(BF16)

    if by_cols:
        tc = 2 * LANES
        nblk = cols // 2 // tc
        shape = (4, r, cols // 2)
        blk = pl.BlockSpec((None, r, tc), lambda j, i, c_ref: (j, 0, i))
        mine = pl.BlockSpec((None, r, tc), lambda j, i, c_ref: (j, 0, c_ref[0] * nblk + i))
    else:
        tr = _row_tile(r // 2)
        nblk = r // 2 // tr
        shape = (4, r // 2, cols)
        blk = pl.BlockSpec((None, tr, cols), lambda j, i, c_ref: (j, i, 0))
        mine = pl.BlockSpec((None, tr, cols), lambda j, i, c_ref: (j, c_ref[0] * nblk + i, 0))
    grid_spec = pltpu.PrefetchScalarGridSpec(num_scalar_prefetch=1, grid=(4, nblk), in_specs=[mine, blk], out_specs=[blk, blk])
    return pl.pallas_call(body, grid_spec=grid_spec, out_shape=[SDS(shape, F32), SDS(shape, BF16)],
                          name=name, compiler_params=_cp(("parallel", "parallel")))(c_arr, g, recv)


def final_sum(pf, recv, chip_arr, *, name):
    _, h, cols = pf.shape
    tr = _row_tile(h)

    def body(chip_ref, p_ref, r_ref, o_ref):
        o_ref[...] = ((p_ref[...] + r_ref[0].astype(F32)) + r_ref[1].astype(F32)) + r_ref[2].astype(F32)

    grid_spec = pltpu.PrefetchScalarGridSpec(
        num_scalar_prefetch=1, grid=(h // tr,),
        in_specs=[pl.BlockSpec((None, tr, cols), lambda i, chip_ref: (chip_ref[0], i, 0)),
                  pl.BlockSpec((3, tr, cols), lambda i, chip_ref: (0, i, 0))],
        out_specs=pl.BlockSpec((tr, cols), lambda i, chip_ref: (i, 0)))
    return pl.pallas_call(body, grid_spec=grid_spec, out_shape=SDS((h, cols), F32), name=name,
                          compiler_params=_cp(("parallel",)))(chip_arr, pf, recv)


def owner_sum(g, recv, pos_arr, *, name):
    _, r, cols = g.shape
    h = r // 2
    tr = _row_tile(h)
    nblk = h // tr

    def body(pos_ref, g_ref, r_ref, o_ref):
        tot = g_ref[...]
        for slot in range(7):
            tot = tot + r_ref[slot].astype(F32)
        o_ref[...] = tot

    grid_spec = pltpu.PrefetchScalarGridSpec(
        num_scalar_prefetch=1, grid=(nblk,),
        in_specs=[pl.BlockSpec((None, tr, cols), lambda i, pos: (pos[0], pos[1] * nblk + i, 0)),
                  pl.BlockSpec((7, tr, cols), lambda i, pos: (0, i, 0))],
        out_specs=pl.BlockSpec((tr, cols), lambda i, pos: (i, 0)))
    return pl.pallas_call(body, grid_spec=grid_spec, out_shape=SDS((h, cols), F32), name=name,
                          compiler_params=_cp(("parallel",)))(pos_arr, g, recv)


def allreduce_small(v, *, name):
    rws, cols = v.shape

    def body(v_ref, all_ref, sum_ref, send_sems, recv_sems, local_sem):
        x, y, c = _coords()
        me, sibling = (x, y, c), (x, y, 1 - c)
        chips = [(1 - x, y), (x, 1 - y), (1 - x, 1 - y)]

        def rows(px, py, pc):
            return all_ref.at[pl.ds(pl.multiple_of((4 * px + 2 * py + pc) * rws, 8), rws), :]

        def copy(k, block, to, src=None):
            return _remote(rows(*block) if src is None else src, rows(*block), send_sems.at[k], recv_sems.at[k], to)

        mine = pltpu.make_async_copy(v_ref, rows(*me), local_sem)
        mine.start()
        first = [copy(0, me, sibling, src=v_ref)]
        first += [copy(1 + j, me, (*chip, c), src=v_ref) for j, chip in enumerate(chips)]
        for cp in first:
            cp.start()
        passed = [copy(4 + j, (*chip, c), sibling) for j, chip in enumerate(chips)]
        for j, chip in enumerate(chips):
            copy(1 + j, (*chip, c), me).wait_recv()
            passed[j].start()
        copy(0, sibling, me).wait_recv()
        for j, chip in enumerate(chips):
            copy(4 + j, (*chip, 1 - c), me).wait_recv()
        for cp in first + passed:
            cp.wait_send()
        mine.wait()
        tot = all_ref[0:rws, :]
        for dev in range(1, 8):
            tot = tot + all_ref[dev * rws:(dev + 1) * rws, :]
        sum_ref[...] = tot

    vm = pl.BlockSpec(memory_space=pltpu.VMEM)
    return pl.pallas_call(
        body, in_specs=[vm], out_specs=[vm, vm],
        out_shape=[SDS((8 * rws, cols), v.dtype), SDS((rws, cols), v.dtype)],
        scratch_shapes=[pltpu.SemaphoreType.DMA((7,)), pltpu.SemaphoreType.DMA((7,)), pltpu.SemaphoreType.DMA],
        name=name)(v)[1]


def _pack_rows(parts, rows):
    out = []
    for a, r in zip(parts, rows):
        flat = a.reshape(-1)
        flat = jnp.pad(flat, (0, r * LANES - flat.shape[0]))
        out.append(flat.reshape(r, LANES))
    return jnp.concatenate(out, axis=0)


def _unpack_rows(packed, shapes, rows):
    out, at = [], 0
    for shp, r in zip(shapes, rows):
        size = int(np.prod(shp))
        out.append(packed[at:at + r].reshape(-1)[:size].reshape(shp))
        at += r
    return out


def kernel(x, g_pre_mix, w_in, b_forget, w_o_fox, w_o_dil, w_out, g_post_mix, g_pre_ffn, w_up, conv_w, conv_b, w_down, g_post_ffn, loss_target, m_g_pre_mix, m_w_in, m_b_forget, m_w_o_fox, m_w_o_dil, m_w_out, m_g_post_mix, m_g_pre_ffn, m_w_up, m_conv_w, m_conv_b, m_w_down, m_g_post_ffn, v_g_pre_mix, v_w_in, v_b_forget, v_w_o_fox, v_w_o_dil, v_w_out, v_g_post_mix, v_g_pre_ffn, v_w_up, v_conv_w, v_conv_b, v_w_down, v_g_post_ffn):
    xi, yi, ci = _coords()
    chip = 2 * xi + yi
    c_arr = jnp.reshape(ci, (1,)).astype(jnp.int32)
    chip_arr = jnp.reshape(chip, (1,)).astype(jnp.int32)
    xs = x[0]
    target = loss_target[0]
    s, d = xs.shape
    f_half = w_down.shape[1] * 4
    cols_in = w_in.shape[2]

    big = (w_in, w_o_fox, w_o_dil, w_out, w_up, w_down)
    shards = [w[0].astype(BF16) for w in big]
    a_in, a_cw = allgather_chips([shards[0], conv_w[0]], [True, False], name="allgather_w_in")
    w_in_full = jnp.concatenate([jnp.where(chip == j, shards[0], a_in[j]) for j in range(4)], axis=1)
    cw = jnp.concatenate([jnp.where(chip == j, conv_w[0], a_cw[j]) for j in range(4)], axis=1)
    nf = N_HEADS
    e_a, e_b = 3 * ATT_W, 3 * ATT_W + nf
    wz = jnp.concatenate([w_in_full[:, :e_a], w_in_full[:, e_b:]], axis=1)
    wf = jnp.pad(w_in_full[:, e_a:e_b], ((0, 0), (0, LANES - nf)))
    cb = conv_b
    bfo = jnp.pad(b_forget, ((0, 0), (0, LANES - nf)))

    h1 = rmsnorm_fwd(xs, g_pre_mix)
    z = mm([(h1, d, 0)], [(wz, d, 0)], nt=False, out_dtype=BF16, tm=1024, tn=512, name="in_proj")
    fa = mm([(h1, d, 0)], [(wf, d, 0)], nt=False, out_dtype=F32, tm=1024, tn=LANES, name="in_proj_forget")
    q_aug, k_aug, v_aug = fox_prep(z, fa, bfo)
    ya, lse_a, *late = fox_fwd(q_aug, k_aug, v_aug, gather=shards[1:], hps=N_HEADS)
    a_of, a_od, a_out, a_up, a_down = [
        lax.dynamic_update_index_in_dim(a4, own, chip, 0) for a4, own in zip(late, shards[1:])]
    wo_a = jnp.concatenate([a_of[j] for j in range(4)], axis=1)
    wo_b = jnp.concatenate([a_od[j] for j in range(4)], axis=1)
    w_o = a_out.reshape(d, d)
    w_dn = a_down.reshape(f_half, d)
    wu_a = jnp.concatenate([a_up[0], a_up[1]], axis=1)
    wu_b = jnp.concatenate([a_up[2], a_up[3]], axis=1)
    cos_t, sin_t = rope_cos_sin(s)
    yb, lse_b = dil_fwd_all(z, cos_t, sin_t)
    pa, pb, mixed = gate_mix(ya, yb, wo_a, wo_b, z)
    y1, x1, h2 = proj_norm_res(mixed, w_o, g_post_mix, xs, g_pre_ffn, name="out_proj")
    ua, ub, conv_a, conv_bh, mid = ffn_up(h2, wu_a, wu_b, cw, cb)
    dout, dy2, gg_post_ffn, sq = proj_norm_loss(mid, w_dn, g_post_ffn, x1, target, name="down_proj")
    loss = lax.psum(0.5 * sq[0, 0] / d, ("x", "y", "c"))

    dmid = mm([(dy2, d, 0)], [(w_dn, d, 0)], nt=True, out_dtype=BF16, tm=512, tn=f_half // 2, name="down_dgrad")
    dw_down, dw_down16 = wgrad((mid, f_half, 0), dy2, tk=f_half // 2, tn=1024, ts=1024, name="down_wgrad", bf16_copy=True)
    dua, dub, gc_a, gc_b = ffn_bwd(dmid, ua, ub, conv_a, conv_bh, cw)
    dx1, dy1, gg_pre_ffn, gg_post_mix = mm_norm_bwd(
        [(dua, f_half, 0), (dub, f_half, 0)], [(wu_a, f_half, 0), (wu_b, f_half, 0)],
        [(x1, g_pre_ffn, dout, F32), (y1, g_post_mix, None, BF16)], name="up_dgrad")
    dw_up = None
    for k, du in enumerate((dua, dub)):
        dw_up = wgrad((h2, d, 0), du, tk=1024, tn=f_half // 2, ts=1024, name=f"up_wgrad_{k}", chip_major=True,
                      slabs=(4, 2 * k), into=dw_up, bf16_copy=True)
    g_ffn = [(dw_up[0], dw_up[1]), (dw_down.reshape(4, f_half // 4, d), dw_down16.reshape(4, f_half // 4, d))]
    dw_out, dw_out16 = wgrad((mixed, d, 0), dy1, tk=1024, tn=1024, ts=1024, name="out_wgrad", bf16_copy=True)
    dpa, dpb, dz_g, dya, dyb, dd_a = mix_bwd(dy1, w_o, z, pa, pb, wo_a, wo_b, ya)
    by_chip_cols = lambda a: jnp.stack([a[:, j * (d // 4):(j + 1) * (d // 4)] for j in range(4)], axis=0)
    dw_of = [by_chip_cols(a) for a in wgrad((ya, ATT_W, 0), dpa, tk=ATT_W, tn=d, ts=1024, name="fox_o_wgrad", bf16_copy=True)]
    dw_od = [by_chip_cols(a) for a in wgrad((yb, ATT_W, 0), dpb, tk=ATT_W, tn=d, ts=1024, name="dil_o_wgrad", bf16_copy=True)]
    g_mix = [dw_of, dw_od, (dw_out.reshape(4, d // 4, d), dw_out16.reshape(4, d // 4, d))]
    dq_aug, dk_aug, dv_a, *got_ffn = fox_bwd(q_aug, k_aug, z, dya, lse_a, dd_a, exchange=[g[1] for g in g_ffn], kind="to_owners")
    dz_a, dfa, gg_bf = fox_post(dq_aug, dk_aug, dv_a, fa, bfo)
    *dz_b, got_of, got_od, got_out = dil_bwd_all(z, cos_t, sin_t, dyb, lse_b, yb, exchange=[g[1] for g in g_mix],
                                                 kind="to_owners")
    got_mix = [got_of, got_od, got_out]
    dwt_a = wgrad((dz_a, e_a, 0), h1, tk=e_a // 2, tn=d, ts=1024, name="in_wgrad_a")
    dwt_b = [wgrad((part, ATT_W, 0), h1, tk=ATT_W, tn=d, ts=1024, name=f"in_wgrad_b{k}") for k, part in enumerate(dz_b)]
    dwt_g = wgrad((dz_g, 2 * d, 0), h1, tk=d, tn=d, ts=1024, name="in_wgrad_g")
    dwt_f = wgrad((dfa, LANES, 0), h1, tk=LANES, tn=d, ts=1024, name="in_wgrad_f")
    dwt_full = jnp.concatenate([dwt_a, dwt_f[:nf], *dwt_b, dwt_g], axis=0)
    dw_in = jnp.stack([dwt_full[j * cols_in:(j + 1) * cols_in] for j in range(4)], axis=0)
    from_sib = grads_to_sibling([dw_in], [True], name="grads_to_sibling_in")
    sum_in = chip_sum(dw_in, from_sib[0], c_arr, True, name="chip_sum_w_in")
    grad_x, gg_pre_mix, got_in = mm_norm_bwd(
        [(dz_a, e_a, 0), *[(part, ATT_W, 0) for part in dz_b], (dz_g, d, 0), (dz_g, d, 1), (dfa, LANES, 0)],
        [(wz, e_a, 0), *[(wz, ATT_W, Z_QB + k) for k in range(3)], (wz, d, 3), (wz, d, 4), (wf, LANES, 0)],
        [(xs, g_pre_mix, dx1, F32)], exchange=[sum_in[1]], name="in_dgrad")

    names = ("w_in", "w_o_fox", "w_o_dil", "w_out", "w_up", "w_down")
    pos_arr = jnp.concatenate([chip_arr, c_arr])
    halves = [final_sum(sum_in[0], got_in, chip_arr, name="final_sum_w_in")] + [
        owner_sum(g[0], got, pos_arr, name=f"owner_sum_{nm}") for g, got, nm in zip(g_mix + g_ffn, got_mix + got_ffn, names[1:])]
    from_half = halves_to_full(halves, [True] + [False] * 5, name="halves_to_full")
    g_big = [None] + [lax.dynamic_update_slice_in_dim(full, mine, ci * mine.shape[0], axis=0)
                      for full, mine in zip(from_half[1:], halves[1:])]
    upd_big = [adamw(w[0], g, m[0], v[0], name=f"adamw_{nm}") for w, g, m, v, nm in list(zip(
        big, g_big, (m_w_in, m_w_o_fox, m_w_o_dil, m_w_out, m_w_up, m_w_down),
        (v_w_in, v_w_o_fox, v_w_o_dil, v_w_out, v_w_up, v_w_down), names))[1:]]
    to_t = lambda a: jnp.transpose(a, (2, 0, 1))
    from_t = lambda a: jnp.transpose(a, (1, 2, 0))
    *upd_in, g_in_t = adamw_rows_view(to_t(w_in), halves[0], from_half[0], to_t(m_w_in), to_t(v_w_in), c_arr,
                                      name="adamw_w_in")

    g_cw_loc = jnp.concatenate([gc_a[0:3], gc_b[0:3]], axis=1)
    g_cb_loc = jnp.concatenate([gc_a[3:4], gc_b[3:4]], axis=1)
    small_loc = [gg_pre_mix, gg_post_mix, gg_pre_ffn, gg_post_ffn, g_cb_loc, gg_bf[:, :nf], g_cw_loc]
    red_rows = (8, 8, 8, 8, 48, 8, 136)
    red = allreduce_small(_pack_rows(small_loc, red_rows), name="allreduce_small")
    g_pm, g_qm, g_pf, g_qf, g_cb, g_bf, g_cw_full = _unpack_rows(red, [a.shape for a in small_loc], red_rows)
    cols_cw = conv_w.shape[2]
    g_cw = lax.dynamic_slice_in_dim(g_cw_full, chip * cols_cw, cols_cw, axis=1)
    small_w = (g_pre_mix, g_post_mix, g_pre_ffn, g_post_ffn, conv_b, b_forget, conv_w[0])
    small_m = (m_g_pre_mix, m_g_post_mix, m_g_pre_ffn, m_g_post_ffn, m_conv_b, m_b_forget, m_conv_w[0])
    small_v = (v_g_pre_mix, v_g_post_mix, v_g_pre_ffn, v_g_post_ffn, v_conv_b, v_b_forget, v_conv_w[0])
    small_g = (g_pm, g_qm, g_pf, g_qf, g_cb, g_bf, g_cw)
    small_names = ("g_pre_mix", "g_post_mix", "g_pre_ffn", "g_post_ffn", "conv_b", "b_forget", "conv_w")
    per_param = [adamw(w, g, m, v, name=f"adamw_{nm}") for w, g, m, v, nm in zip(small_w, small_g, small_m, small_v, small_names)]
    upd_small = [[u[j] for u in per_param] for j in range(3)]

    order = ("g_pre_mix", "w_in", "b_forget", "w_o_fox", "w_o_dil", "w_out", "g_post_mix", "g_pre_ffn", "w_up", "conv_w",
             "conv_b", "w_down", "g_post_ffn")
    grads, deltas, new_ms, new_vs = {}, {}, {}, {}
    grads["w_in"] = from_t(g_in_t)
    deltas["w_in"], new_ms["w_in"], new_vs["w_in"] = (from_t(a) for a in upd_in)
    for k, nm in enumerate(names[1:]):
        grads[nm] = g_big[k + 1][None]
        deltas[nm], new_ms[nm], new_vs[nm] = (a[None] for a in upd_big[k])
    for k, nm in enumerate(small_names):
        lead = (lambda a: a[None]) if nm == "conv_w" else (lambda a: a)
        grads[nm] = lead(small_g[k])
        deltas[nm], new_ms[nm], new_vs[nm] = (lead(upd_small[j][k]) for j in range(3))
    return (loss, grad_x[None], *[grads[nm] for nm in order], *[deltas[nm] for nm in order],
            *[new_ms[nm] for nm in order], *[new_vs[nm] for nm in order])
```

```python
import functools
import math

import numpy as np
import jax
import jax.numpy as jnp
from jax import lax
from jax.experimental import pallas as pl
from jax.experimental.pallas import tpu as pltpu

F32 = jnp.float32
BF16 = jnp.bfloat16
SDS = jax.ShapeDtypeStruct
MESH = pl.DeviceIdType.MESH

HEAD_DIM = 64
N_HEADS = 8
LANES = 128
ATT_W = N_HEADS * HEAD_DIM
DIL_PATTERNS = ((128, 1), (512, 4), (2048, 16))
DIL_BLK = 128
ROPE_DIM = HEAD_DIM // 4
ROPE_THETA = 500000.0
RMS_EPS = 1e-6
NEG = -1e30
QK_SCALE = 1.0 / math.sqrt(HEAD_DIM)
ADAM_LR, ADAM_B1, ADAM_B2, ADAM_EPS, ADAM_WD, ADAM_STEP = 0.001, 0.9, 0.999, 1e-08, 0.01, 10
VMEM_LIMIT = 56 * 1024 * 1024

Z_QA, Z_KA, Z_VA, Z_QB, Z_KB, Z_VB = 0, 1, 2, 3, 4, 5
Z_W = 5120


def _cp(sem):
    return pltpu.CompilerParams(dimension_semantics=sem, vmem_limit_bytes=VMEM_LIMIT)


def _nt(a, b):
    return lax.dot_general(a, b, (((1,), (1,)), ((), ())), preferred_element_type=F32)


def _tn(a, b):
    return lax.dot_general(a, b, (((0,), (0,)), ((), ())), preferred_element_type=F32)


def _nn(a, b):
    return jnp.dot(a, b, preferred_element_type=F32)


def _lane(shape):
    return lax.broadcasted_iota(jnp.int32, shape, 1)


def _row(shape):
    return lax.broadcasted_iota(jnp.int32, shape, 0)


def rmsnorm_fwd(x, g, *, tm=512):
    s, d = x.shape

    def body(x_ref, g_ref, h_ref):
        xv = x_ref[...]
        inv = lax.rsqrt(jnp.mean(xv * xv, axis=-1, keepdims=True) + RMS_EPS)
        h_ref[...] = (xv * inv * g_ref[...]).astype(h_ref.dtype)

    return pl.pallas_call(
        body, grid=(s // tm,),
        in_specs=[pl.BlockSpec((tm, d), lambda i: (i, 0)), pl.BlockSpec((1, d), lambda i: (0, 0))],
        out_specs=pl.BlockSpec((tm, d), lambda i: (i, 0)),
        out_shape=SDS((s, d), BF16), name="rmsnorm_fwd", compiler_params=_cp(("parallel",)))(x, g)


def mm(a_views, b_views, *, nt, out_dtype, tm, tn, name):
    n_p = len(a_views)
    m = a_views[0][0].shape[0]
    n = b_views[0][0].shape[0] if nt else b_views[0][0].shape[1]

    def body(*refs):
        o_ref = refs[-1]
        acc = None
        for p in range(n_p):
            av = refs[p][...].astype(BF16)
            bv = refs[n_p + p][...].astype(BF16)
            dv = _nt(av, bv) if nt else _nn(av, bv)
            acc = dv if acc is None else acc + dv
        o_ref[...] = acc.astype(o_ref.dtype)

    in_specs = []
    for arr, w, blk in a_views:
        in_specs.append(pl.BlockSpec((tm, w), functools.partial(lambda i, j, blk: (i, blk), blk=blk)))
    for arr, w, blk in b_views:
        if nt:
            in_specs.append(pl.BlockSpec((tn, w), functools.partial(lambda i, j, blk: (j, blk), blk=blk)))
        else:
            in_specs.append(pl.BlockSpec((w, tn), lambda i, j: (0, j)))
    return pl.pallas_call(
        body, grid=(m // tm, n // tn), in_specs=in_specs,
        out_specs=pl.BlockSpec((tm, tn), lambda i, j: (i, j)),
        out_shape=SDS((m, n), out_dtype), name=name,
        compiler_params=_cp(("parallel", "parallel")))(*[a[0] for a in a_views], *[b[0] for b in b_views])


def wgrad(a_view, g, *, tk, tn, ts, name, chip_major=False, slabs=None, into=None, bf16_copy=False):
    arr, ka, blk = a_view
    s, n = g.shape
    ns = s // ts
    total, first = slabs if slabs else (n // tn, 0)
    n_into = 0 if into is None else (2 if bf16_copy else 1)

    def body(a_ref, g_ref, *rest):
        o_ref = rest[n_into]

        @pl.when(pl.program_id(2) == 0)
        def _():
            o_ref[...] = jnp.zeros_like(o_ref)

        o_ref[...] += _tn(a_ref[...].astype(BF16), g_ref[...].astype(BF16))
        if bf16_copy:
            @pl.when(pl.program_id(2) == ns - 1)
            def _():
                rest[n_into + 1][...] = o_ref[...].astype(BF16)

    if chip_major:
        out_spec = pl.BlockSpec((None, tk, tn), lambda i, j, k: (first + j, i, 0))
        shape = (total, ka, tn)
    else:
        out_spec = pl.BlockSpec((tk, tn), lambda i, j, k: (i, j))
        shape = (ka, n)
    in_specs = [pl.BlockSpec((ts, tk), lambda i, j, k: (k, blk * (ka // tk) + i)),
                pl.BlockSpec((ts, tn), lambda i, j, k: (k, j))]
    args = [arr, g]
    if into is not None:
        earlier = list(into) if bf16_copy else [into]
        in_specs += [pl.BlockSpec(memory_space=pl.ANY)] * len(earlier)
        args += earlier
    out = pl.pallas_call(
        body, grid=(ka // tk, n // tn, ns), in_specs=in_specs,
        out_specs=[out_spec, out_spec] if bf16_copy else out_spec,
        out_shape=[SDS(shape, F32), SDS(shape, BF16)] if bf16_copy else SDS(shape, F32), name=name,
        input_output_aliases={2 + k: k for k in range(n_into)},
        compiler_params=_cp(("parallel", "parallel", "arbitrary")))(*args)
    return out


def _norm_bwd_rows(dh, xh, inv, g):
    dxh = dh * g
    dx = inv * (dxh - xh * jnp.mean(dxh * xh, axis=-1, keepdims=True))
    return dx, jnp.sum((dh * xh).reshape(dh.shape[0] // 8, 8, dh.shape[1]), axis=0)


def proj_norm_res(a, w, g, xres, g_next, *, tm=512, name):
    s, k = a.shape
    d = w.shape[1]

    def body(a_ref, w_ref, g_ref, x_ref, gn_ref, y_ref, o_ref, h_ref):
        y = _nn(a_ref[...], w_ref[...])
        inv = lax.rsqrt(jnp.mean(y * y, axis=-1, keepdims=True) + RMS_EPS)
        xn = x_ref[...] + y * inv * g_ref[...]
        y_ref[...] = y
        o_ref[...] = xn
        inv_n = lax.rsqrt(jnp.mean(xn * xn, axis=-1, keepdims=True) + RMS_EPS)
        h_ref[...] = (xn * inv_n * gn_ref[...]).astype(h_ref.dtype)

    row = pl.BlockSpec((tm, d), lambda i: (i, 0))
    vec = pl.BlockSpec((1, d), lambda i: (0, 0))
    return pl.pallas_call(
        body, grid=(s // tm,),
        in_specs=[pl.BlockSpec((tm, k), lambda i: (i, 0)), pl.BlockSpec((k, d), lambda i: (0, 0)), vec, row, vec],
        out_specs=[row, row, row], out_shape=[SDS((s, d), F32), SDS((s, d), F32), SDS((s, d), BF16)], name=name,
        compiler_params=_cp(("parallel",)))(a, w, g, xres, g_next)


def proj_norm_loss(a, w, g, xres, target, *, tm=512, name):
    s, k = a.shape
    d = w.shape[1]
    n = s // tm

    def body(a_ref, w_ref, g_ref, x_ref, t_ref, do_ref, dy_ref, dg_ref, l_ref, acc):
        i = pl.program_id(0)

        @pl.when(i == 0)
        def _():
            acc[...] = jnp.zeros_like(acc)
            l_ref[...] = jnp.zeros_like(l_ref)

        y = _nn(a_ref[...], w_ref[...])
        inv = lax.rsqrt(jnp.mean(y * y, axis=-1, keepdims=True) + RMS_EPS)
        yh = y * inv
        err = x_ref[...] + yh * g_ref[...] - t_ref[...]
        dout = err * (1.0 / d)
        do_ref[...] = dout
        l_ref[...] += jnp.sum(jnp.sum(err * err, axis=1, keepdims=True), axis=0, keepdims=True)
        dy, part = _norm_bwd_rows(dout, yh, inv, g_ref[...])
        dy_ref[...] = dy.astype(dy_ref.dtype)
        acc[...] += part

        @pl.when(i == n - 1)
        def _():
            dg_ref[...] = jnp.sum(acc[...], axis=0, keepdims=True)

    row = pl.BlockSpec((tm, d), lambda i: (i, 0))
    vec = pl.BlockSpec((1, d), lambda i: (0, 0))
    return pl.pallas_call(
        body, grid=(n,),
        in_specs=[pl.BlockSpec((tm, k), lambda i: (i, 0)), pl.BlockSpec((k, d), lambda i: (0, 0)), vec, row, row],
        out_specs=[row, row, vec, pl.BlockSpec((1, 1), lambda i: (0, 0))],
        out_shape=[SDS((s, d), F32), SDS((s, d), BF16), SDS((1, d), F32), SDS((1, 1), F32)],
        scratch_shapes=[pltpu.VMEM((8, d), F32)], name=name, compiler_params=_cp(("arbitrary",)))(a, w, g, xres, target)


def mm_norm_bwd(a_views, b_views, stages, exchange=(), *, tm=256, name):
    n_p, n_s, ne = len(a_views), len(stages), len(exchange)
    s = a_views[0][0].shape[0]
    d = b_views[0][0].shape[0]
    n = s // tm
    has_res = [st[2] is not None for st in stages]

    def body(*refs):
        a_refs, b_refs = refs[:n_p], refs[n_p:2 * n_p]
        at = 2 * n_p
        st_refs = []
        for k in range(n_s):
            cnt = 3 if has_res[k] else 2
            st_refs.append(refs[at:at + cnt])
            at += cnt
        e_ins = refs[at:at + ne]
        at += ne
        dx_refs, dg_refs = refs[at:at + n_s], refs[at + n_s:at + 2 * n_s]
        at += 2 * n_s
        e_outs = refs[at:at + ne]
        at += ne
        accs = refs[at:at + n_s]
        comm = (e_ins, e_outs) + tuple(refs[at + n_s:])
        i = pl.program_id(0)

        @pl.when(i == 0)
        def _():
            for acc in accs:
                acc[...] = jnp.zeros_like(acc)
            if ne:
                _to_chips_start(*comm)

        dh = None
        for p in range(n_p):
            part = _nt(a_refs[p][...].astype(BF16), b_refs[p][...].astype(BF16))
            dh = part if dh is None else dh + part
        for k in range(n_s):
            xv = st_refs[k][0][...]
            inv = lax.rsqrt(jnp.mean(xv * xv, axis=-1, keepdims=True) + RMS_EPS)
            dx, part = _norm_bwd_rows(dh, xv * inv, inv, st_refs[k][1][...])
            if has_res[k]:
                dx = dx + st_refs[k][2][...]
            dx_refs[k][...] = dx.astype(dx_refs[k].dtype)
            accs[k][...] += part
            dh = dx

        @pl.when(i == n - 1)
        def _():
            for k in range(n_s):
                dg_refs[k][...] = jnp.sum(accs[k][...], axis=0, keepdims=True)
            if ne:
                _to_chips_finish(*comm)

    row = pl.BlockSpec((tm, d), lambda i: (i, 0))
    vec = pl.BlockSpec((1, d), lambda i: (0, 0))
    in_specs, args = [], []
    for arr, w, blk in a_views:
        in_specs.append(pl.BlockSpec((tm, w), functools.partial(lambda i, blk: (i, blk), blk=blk)))
        args.append(arr)
    for arr, w, blk in b_views:
        in_specs.append(pl.BlockSpec((d, w), functools.partial(lambda i, blk: (0, blk), blk=blk)))
        args.append(arr)
    for x, g, res, _ in stages:
        in_specs += [row, vec] + ([row] if res is not None else [])
        args += [x, g] + ([res] if res is not None else [])
    return pl.pallas_call(
        body, grid=(n,), in_specs=in_specs + [ANY] * ne,
        out_specs=[row] * n_s + [vec] * n_s + [ANY] * ne,
        out_shape=[SDS((s, d), st[3]) for st in stages] + [SDS((1, d), F32)] * n_s + _to_chips_shapes(exchange),
        scratch_shapes=[pltpu.VMEM((8, d), F32)] * n_s + (_to_chips_sems(ne) if ne else []), name=name,
        compiler_params=_cp(("arbitrary",)))(*args, *exchange)


def _split3(v):
    hi = v.astype(BF16).astype(F32)
    r = v - hi
    mid = r.astype(BF16).astype(F32)
    lo = (r - mid).astype(BF16).astype(F32)
    return hi, mid, lo


def _tri(n, upper):
    r = np.arange(n)
    m = (r[:, None] <= r[None, :]) if upper else (r[:, None] >= r[None, :])
    return jnp.asarray(m.astype(np.float32))


def fox_prep(z, fa, bfo, *, tb=512):
    s = z.shape[0]
    n = s // tb

    def body(q_ref, k_ref, v_ref, fa_ref, b_ref, tri_ref, qa_ref, ka_ref, va_ref, carry):
        @pl.when(pl.program_id(0) == 0)
        def _():
            carry[...] = jnp.zeros_like(carry)

        xv = fa_ref[...] + b_ref[...]
        logf = jnp.minimum(xv, 0.0) - jnp.log(1.0 + jnp.exp(-jnp.abs(xv)))
        csum = jnp.dot(tri_ref[...], logf, preferred_element_type=F32, precision=lax.Precision.HIGHEST) + carry[0:1, :]
        carry[0:1, :] = csum[tb - 1:tb, :]
        lane = _lane((tb, LANES))
        for h in range(N_HEADS):
            hi, mid, lo = _split3(csum[:, h:h + 1])
            pair = (h // 2) * LANES
            qv = q_ref[:, pair:pair + LANES].astype(F32)
            kv = k_ref[:, pair:pair + LANES].astype(F32)
            vv = v_ref[:, pair:pair + LANES].astype(F32)
            if h % 2:
                qv = pltpu.roll(qv, 64, axis=1)
                kv = pltpu.roll(kv, 64, axis=1)
                vv = pltpu.roll(vv, 64, axis=1)
            va_ref[:, h * LANES:(h + 1) * LANES] = jnp.where(lane < 64, vv, jnp.where(lane == 64, 1.0, 0.0)).astype(BF16)
            one = jnp.where((lane >= 67) & (lane < 70), 1.0, 0.0)
            q_x = jnp.where(lane == 64, hi, jnp.where(lane == 65, mid, jnp.where(lane == 66, lo, one)))
            one = jnp.where((lane >= 64) & (lane < 67), 1.0, 0.0)
            k_x = jnp.where(lane == 67, -hi, jnp.where(lane == 68, -mid, jnp.where(lane == 69, -lo, one)))
            qa_ref[:, h * LANES:(h + 1) * LANES] = jnp.where(lane < 64, qv * QK_SCALE, q_x).astype(BF16)
            ka_ref[:, h * LANES:(h + 1) * LANES] = jnp.where(lane < 64, kv, k_x).astype(BF16)

    return pl.pallas_call(
        body, grid=(n,),
        in_specs=[pl.BlockSpec((tb, ATT_W), lambda i: (i, Z_QA)), pl.BlockSpec((tb, ATT_W), lambda i: (i, Z_KA)),
                  pl.BlockSpec((tb, ATT_W), lambda i: (i, Z_VA)),
                  pl.BlockSpec((tb, LANES), lambda i: (i, 0)), pl.BlockSpec((1, LANES), lambda i: (0, 0)),
                  pl.BlockSpec((tb, tb), lambda i: (0, 0))],
        out_specs=[pl.BlockSpec((tb, N_HEADS * LANES), lambda i: (i, 0))] * 3,
        out_shape=[SDS((s, N_HEADS * LANES), BF16)] * 3,
        scratch_shapes=[pltpu.VMEM((8, LANES), F32)],
        name="fox_prep", compiler_params=_cp(("arbitrary",)))(z, z, z, fa, bfo, _tri(tb, False))


def _causal_pairs(n, k_major):
    if k_major:
        pairs = [(qi, kj) for kj in range(n) for qi in range(kj, n)]
    else:
        pairs = [(qi, kj) for qi in range(n) for kj in range(qi + 1)]
    return (jnp.asarray([p[0] for p in pairs], jnp.int32), jnp.asarray([p[1] for p in pairs], jnp.int32), len(pairs))


def fox_fwd(q_aug, k_aug, v_aug, gather=(), halved=(), *, t=512, hps=4):
    s = v_aug.shape[0]
    qi_arr, kj_arr, n_pairs = _causal_pairs(s // t, False)
    ng = len(gather)
    n_groups = N_HEADS // hps

    def body(qi_ref, kj_ref, q_ref, k_ref, v_ref, *rest):
        g_ins, (o_ref, lse_ref), g_outs = rest[:ng], rest[ng:ng + 2], rest[ng + 2:2 * ng + 2]
        m_scr, acc_scr = rest[2 * ng + 2:2 * ng + 4]
        comm = (g_ins, g_outs) + tuple(rest[2 * ng + 4:]) + (list(halved),)
        step = pl.program_id(1)
        qi = qi_ref[step]
        kj = kj_ref[step]
        if ng:
            @pl.when((pl.program_id(0) == 0) & (step == 0))
            def _():
                _allgather_start(*comm)

        @pl.when(kj == 0)
        def _():
            m_scr[...] = jnp.full_like(m_scr, NEG)
            acc_scr[...] = jnp.zeros_like(acc_scr)

        def update(masked):
            for i in range(hps):
                sc = _nt(q_ref[:, i * LANES:(i + 1) * LANES], k_ref[:, i * LANES:(i + 1) * LANES])
                if masked:
                    sc = jnp.where(_row((t, t)) >= _lane((t, t)), sc, NEG)
                m_prev = m_scr[i]
                m_new = jnp.maximum(m_prev, jnp.max(sc, axis=-1, keepdims=True))
                p = jnp.exp((sc - jnp.tile(m_new, (1, t // LANES))).astype(BF16))
                acc_scr[i] = jnp.exp(m_prev - m_new) * acc_scr[i] + _nn(p, v_ref[:, i * LANES:(i + 1) * LANES])
                m_scr[i] = m_new

        @pl.when(kj < qi)
        def _():
            update(False)

        @pl.when(kj == qi)
        def _():
            update(True)
            lane = _lane((t, LANES))
            for pr in range(hps // 2):
                den = [acc_scr[2 * pr + i][:, 64:65] for i in range(2)]
                o_ref[:, pr * LANES:(pr + 1) * LANES] = jnp.where(
                    lane < 64, acc_scr[2 * pr] / den[0], pltpu.roll(acc_scr[2 * pr + 1] / den[1], 64, axis=1)).astype(o_ref.dtype)
                lse_ref[:, pr * LANES:(pr + 1) * LANES] = jnp.where(
                    lane < 64, m_scr[2 * pr] + jnp.log(den[0]), m_scr[2 * pr + 1] + jnp.log(den[1]))

        if ng:
            @pl.when((pl.program_id(0) == n_groups - 1) & (step == n_pairs - 1))
            def _():
                _allgather_finish(*comm)

    wide = hps * LANES
    grid_spec = pltpu.PrefetchScalarGridSpec(
        num_scalar_prefetch=2, grid=(n_groups, n_pairs),
        in_specs=[pl.BlockSpec((t, wide), lambda hg, st, qi, kj: (qi[st], hg)),
                  pl.BlockSpec((t, wide), lambda hg, st, qi, kj: (kj[st], hg)),
                  pl.BlockSpec((t, wide), lambda hg, st, qi, kj: (kj[st], hg))] + [ANY] * ng,
        out_specs=[pl.BlockSpec((t, wide // 2), lambda hg, st, qi, kj: (qi[st], hg))] * 2 + [ANY] * ng,
        scratch_shapes=[pltpu.VMEM((hps, t, LANES), F32)] * 2 + (_allgather_sems(ng) if ng else []))
    return pl.pallas_call(
        body, grid_spec=grid_spec, out_shape=[SDS((s, ATT_W), BF16), SDS((s, ATT_W), F32)] + _allgather_shapes(gather),
        name="fox_fwd", compiler_params=_cp(("arbitrary", "arbitrary")))(qi_arr, kj_arr, q_aug, k_aug, v_aug, *gather)


def fox_bwd(q_aug, k_aug, z, dy, lse, dd, exchange=(), kind="to_chips", *, t=512, hps=4):
    s = z.shape[0]
    qi_arr, kj_arr, n_pairs = _causal_pairs(s // t, True)
    ne = len(exchange)
    n_groups = N_HEADS // hps
    x_shapes, x_sems, x_start, x_finish = EXCHANGES[kind]

    def body(qi_ref, kj_ref, q_ref, k_ref, v_ref, do_ref, lse_ref, dd_ref, *rest):
        e_ins, (dq_ref, dk_ref, dv_ref), e_outs = rest[:ne], rest[ne:ne + 3], rest[ne + 3:2 * ne + 3]
        comm = (e_ins, e_outs) + tuple(rest[2 * ne + 3:])
        step = pl.program_id(1)
        qi = qi_ref[step]
        kj = kj_ref[step]
        if ne:
            @pl.when((pl.program_id(0) == 0) & (step == 0))
            def _():
                x_start(*comm)

        @pl.when(step == 0)
        def _():
            dq_ref[...] = jnp.zeros_like(dq_ref)

        @pl.when(qi == kj)
        def _():
            dk_ref[...] = jnp.zeros_like(dk_ref)
            dv_ref[...] = jnp.zeros_like(dv_ref)

        def update(masked):
            lane = _lane((t, LANES))
            rows = pl.ds(pl.multiple_of(qi * t, t), t)
            for pr in range(hps // 2):
                pair = slice(pr * LANES, (pr + 1) * LANES)
                dov = do_ref[:, pair]
                dv_new = None
                for i in range(2):
                    head = (lane < 64) if i == 0 else (lane >= 64)
                    own = slice((2 * pr + i) * LANES, (2 * pr + i + 1) * LANES)
                    col = slice(pr * LANES + i * 64, pr * LANES + i * 64 + 1)
                    qv = q_ref[:, own]
                    kv = k_ref[:, own]
                    sc = _nt(qv, kv)
                    if masked:
                        sc = jnp.where(_row((t, t)) >= _lane((t, t)), sc, NEG)
                    p = jnp.exp(sc - lse_ref[:, col])
                    dp = _nt(jnp.where(head, dov, jnp.zeros_like(dov)), v_ref[:, pair])
                    ds = (p * (dp - dd_ref[:, col])).astype(BF16)
                    dq_ref[rows, own] += _nn(ds, kv)
                    dk_ref[:, own] += _tn(ds, qv)
                    dvi = _tn(p.astype(BF16), dov)
                    dv_new = dvi if dv_new is None else jnp.where(head, dvi, dv_new)
                dv_ref[:, pair] += dv_new

        @pl.when(kj < qi)
        def _():
            update(False)

        @pl.when(kj == qi)
        def _():
            update(True)

        if ne:
            @pl.when((pl.program_id(0) == n_groups - 1) & (step == n_pairs - 1))
            def _():
                x_finish(*comm)

    wide, half = hps * LANES, hps // 2 * LANES
    v_blk = Z_VA * ATT_W // half
    grid_spec = pltpu.PrefetchScalarGridSpec(
        num_scalar_prefetch=2, grid=(n_groups, n_pairs),
        in_specs=[pl.BlockSpec((t, wide), lambda hg, st, qi, kj: (qi[st], hg)),
                  pl.BlockSpec((t, wide), lambda hg, st, qi, kj: (kj[st], hg)),
                  pl.BlockSpec((t, half), lambda hg, st, qi, kj: (kj[st], v_blk + hg)),
                  pl.BlockSpec((t, half), lambda hg, st, qi, kj: (qi[st], hg)),
                  pl.BlockSpec((t, half), lambda hg, st, qi, kj: (qi[st], hg)),
                  pl.BlockSpec((t, half), lambda hg, st, qi, kj: (qi[st], hg))] + [ANY] * ne,
        out_specs=[pl.BlockSpec((s, wide), lambda hg, st, qi, kj: (0, hg)),
                   pl.BlockSpec((t, wide), lambda hg, st, qi, kj: (kj[st], hg)),
                   pl.BlockSpec((t, half), lambda hg, st, qi, kj: (kj[st], hg))] + [ANY] * ne,
        scratch_shapes=x_sems(ne) if ne else [])
    return pl.pallas_call(
        body, grid_spec=grid_spec,
        out_shape=[SDS((s, N_HEADS * LANES), F32), SDS((s, N_HEADS * LANES), F32), SDS((s, ATT_W), F32)]
        + x_shapes(exchange),
        name="fox_bwd", compiler_params=_cp(("arbitrary", "arbitrary")))(qi_arr, kj_arr, q_aug, k_aug, z, dy, lse, dd, *exchange)


def fox_post(dq_aug, dk_aug, dv, fa, bfo, *, tb=512):
    s = dv.shape[0]
    n = s // tb

    def body(dq_ref, dk_ref, dv_ref, fa_ref, b_ref, tri_ref, dz_ref, dfa_ref, gb_ref, carry, acc):
        i = pl.program_id(0)

        @pl.when(i == 0)
        def _():
            carry[...] = jnp.zeros_like(carry)
            acc[...] = jnp.zeros_like(acc)

        lane = _lane((tb, LANES))
        d_f = jnp.zeros((tb, LANES), F32)
        for h in range(N_HEADS):
            col = dq_ref[:, h * LANES + 64:h * LANES + 65] - dk_ref[:, h * LANES + 67:h * LANES + 68]
            d_f = jnp.where(lane == h, col, d_f)
        suffix = jnp.dot(tri_ref[...], d_f, preferred_element_type=F32, precision=lax.Precision.HIGHEST) + carry[0:1, :]
        carry[0:1, :] = suffix[0:1, :]
        xv = fa_ref[...] + b_ref[...]
        dx = suffix * (1.0 / (1.0 + jnp.exp(xv)))
        dfa_ref[...] = dx.astype(dfa_ref.dtype)
        acc[...] += jnp.sum(dx.reshape(tb // 8, 8, LANES), axis=0)
        for hp in range(4):
            for src, off, scale in ((dq_ref, 0, QK_SCALE), (dk_ref, ATT_W, 1.0)):
                even = src[:, (2 * hp) * LANES:(2 * hp + 1) * LANES]
                odd = pltpu.roll(src[:, (2 * hp + 1) * LANES:(2 * hp + 2) * LANES], 64, axis=1)
                dz_ref[:, off + hp * LANES:off + (hp + 1) * LANES] = (jnp.where(lane < 64, even, odd) * scale).astype(BF16)
        dz_ref[:, 2 * ATT_W:3 * ATT_W] = dv_ref[...].astype(BF16)

        @pl.when(i == n - 1)
        def _():
            gb_ref[...] = jnp.sum(acc[...], axis=0, keepdims=True)

    rev = lambda i: (n - 1 - i, 0)
    return pl.pallas_call(
        body, grid=(n,),
        in_specs=[pl.BlockSpec((tb, N_HEADS * LANES), rev), pl.BlockSpec((tb, N_HEADS * LANES), rev),
                  pl.BlockSpec((tb, ATT_W), rev), pl.BlockSpec((tb, LANES), rev),
                  pl.BlockSpec((1, LANES), lambda i: (0, 0)), pl.BlockSpec((tb, tb), lambda i: (0, 0))],
        out_specs=[pl.BlockSpec((tb, 3 * ATT_W), rev), pl.BlockSpec((tb, LANES), rev),
                   pl.BlockSpec((1, LANES), lambda i: (0, 0))],
        out_shape=[SDS((s, 3 * ATT_W), BF16), SDS((s, LANES), BF16), SDS((1, LANES), F32)],
        scratch_shapes=[pltpu.VMEM((8, LANES), F32), pltpu.VMEM((8, LANES), F32)],
        name="fox_post", compiler_params=_cp(("arbitrary",)))(dq_aug, dk_aug, dv, fa, bfo, _tri(tb, True))


def rope_cos_sin(s):
    half = ROPE_DIM // 2
    inv_freq = ROPE_THETA ** (-jnp.arange(half, dtype=F32) * 2.0 / ROPE_DIM)
    ang = jnp.arange(s, dtype=F32)[:, None] * inv_freq[None, :]
    return jnp.tile(jnp.cos(ang), (1, LANES // half)), jnp.tile(jnp.sin(ang), (1, LANES // half))


def _rotate(x, cos, sin, sign):
    l64 = _lane(x.shape) & (HEAD_DIM - 1)
    first = l64 < ROPE_DIM // 2
    second = (l64 >= ROPE_DIM // 2) & (l64 < ROPE_DIM)
    from_next = jnp.where(first, -sign * sin, 0.0)
    from_prev = jnp.where(second, sign * sin, 0.0)
    return (x * jnp.where(first | second, cos, 1.0) + pltpu.roll(x, LANES - 8, axis=1) * from_next
            + pltpu.roll(x, 8, axis=1) * from_prev)


def _dil_rows(base, r):
    if r == 1:
        return pl.ds(pl.multiple_of(base, DIL_BLK), DIL_BLK)
    return pl.ds(base, DIL_BLK, stride=r)


def _dil_block(idx, r, nb):
    shift = nb.bit_length() - 1
    rho = idx >> shift
    n = idx & (nb - 1)
    base = rho + n * (r * DIL_BLK)
    return _dil_rows(base, r), _dil_rows(jnp.maximum(base - r * DIL_BLK, rho), r), n > 0


def _cat(a, b):
    return jnp.concatenate([a, b], axis=0)


def _two_heads(v, first_head):
    zero = jnp.zeros_like(v)
    return _cat(jnp.where(first_head, v, zero), jnp.where(first_head, zero, v))


def _dil_bands():
    b = DIL_BLK
    q = _row((2 * b, 2 * b)) & (b - 1)
    col = _lane((2 * b, 2 * b))
    return (col < b) & (col >= q), (col >= b) & (col - b <= q)


def _dil_load_qkv(zq_ref, zk_ref, zv_ref, cos_ref, sin_ref, q_ref, k_ref, v_ref, *, chunk=512):
    def step(i, carry):
        rows = pl.ds(pl.multiple_of(i * chunk, chunk), chunk)
        cos, sin = cos_ref[rows, :], sin_ref[rows, :]
        q_ref[rows, :] = _rotate(zq_ref[rows, :].astype(F32), cos, sin, 1.0) * QK_SCALE
        k_ref[rows, :] = _rotate(zk_ref[rows, :].astype(F32), cos, sin, 1.0)
        v_ref[rows, :] = zv_ref[rows, :].astype(F32)
        return carry

    lax.fori_loop(0, q_ref.shape[0] // chunk, step, 0)


def dil_fwd_all(z, cos_t, sin_t, *, unroll=8):
    s = z.shape[0]
    b = DIL_BLK
    n_blk = s // b

    def body(zq_ref, zk_ref, zv_ref, cos_ref, sin_ref, o_ref, l_ref, q_ref, k_ref, v_ref):
        _dil_load_qkv(zq_ref, zk_ref, zv_ref, cos_ref, sin_ref, q_ref, k_ref, v_ref)
        first_head = _lane((b, LANES)) < 64
        band_prev, band_cur = _dil_bands()
        for g, (_, r) in enumerate(DIL_PATTERNS):
            nb = n_blk // r

            def group(it, carry, g=g, r=r, nb=nb):
                loaded = []
                kc = vc = None
                for u in range(unroll):
                    rows_c, rows_p, has_prev = _dil_block(it * unroll + u, r, nb)
                    if u % min(nb, unroll):
                        kp, vp = kc, vc
                    else:
                        kp, vp = k_ref[rows_p, :].astype(BF16), v_ref[rows_p, :].astype(BF16)
                    kc, vc = k_ref[rows_c, :].astype(BF16), v_ref[rows_c, :].astype(BF16)
                    state = (o_ref[rows_c, :], l_ref[rows_c, :]) if g else None
                    loaded.append((rows_c, has_prev, [q_ref[rows_c, :].astype(BF16), kp, kc, vp, vc], state))
                done = []
                for rows_c, has_prev, (qv, kp, kc, vp, vc), state in loaded:
                    sc = jnp.where(band_cur | (band_prev & has_prev), _nt(_two_heads(qv, first_head), _cat(kp, kc)), NEG)
                    m = jnp.max(sc, axis=-1, keepdims=True)
                    p = jnp.exp(sc - m)
                    den = jnp.sum(p, axis=-1, keepdims=True)
                    both = _nn(p.astype(BF16), _cat(vp, vc)) / den
                    lse2 = m + jnp.log(den)
                    ov = jnp.where(first_head, both[:b], both[b:])
                    lse = jnp.where(first_head, lse2[:b], lse2[b:])
                    if state is not None:
                        m2 = jnp.maximum(state[1], lse)
                        wp = jnp.exp(state[1] - m2)
                        wn = jnp.exp(lse - m2)
                        ov = (wp * state[0] + wn * ov) / (wp + wn)
                        lse = m2 + jnp.log(wp + wn)
                    done.append((rows_c, ov, lse))
                for rows_c, ov, lse in done:
                    o_ref[rows_c, :] = ov
                    l_ref[rows_c, :] = lse
                return carry

            lax.fori_loop(0, n_blk // unroll, group, 0)

    col_blk = lambda k: pl.BlockSpec((s, LANES), lambda hp: (0, 4 * k + hp))
    table = pl.BlockSpec((s, LANES), lambda hp: (0, 0))
    out = pl.BlockSpec((s, LANES), lambda hp: (0, hp))
    return pl.pallas_call(
        body, grid=(4,), in_specs=[col_blk(Z_QB), col_blk(Z_KB), col_blk(Z_VB), table, table], out_specs=[out, out],
        out_shape=[SDS((s, ATT_W), F32)] * 2, scratch_shapes=[pltpu.VMEM((s, LANES), F32)] * 3, name="dil_fwd",
        compiler_params=_cp(("parallel",)))(z, z, z, cos_t, sin_t)


def dil_bwd_all(z, cos_t, sin_t, dy, lse, y, exchange=(), kind="to_chips", *, unroll=8):
    s = z.shape[0]
    b = DIL_BLK
    n_blk = s // b
    ne = len(exchange)
    x_shapes, x_sems, x_start, x_finish = EXCHANGES[kind]

    def body(zq_ref, zk_ref, zv_ref, cos_ref, sin_ref, do_ref, l_ref, y_ref, *rest):
        e_ins, (gq_ref, gk_ref, gv_ref), e_outs = rest[:ne], rest[ne:ne + 3], rest[ne + 3:2 * ne + 3]
        q_ref, k_ref, v_ref, dq_ref, dk_ref, dv_ref = rest[2 * ne + 3:2 * ne + 9]
        comm = (e_ins, e_outs) + tuple(rest[2 * ne + 9:])
        if ne:
            @pl.when(pl.program_id(0) == 0)
            def _():
                x_start(*comm)

        _dil_load_qkv(zq_ref, zk_ref, zv_ref, cos_ref, sin_ref, q_ref, k_ref, v_ref)
        dq_ref[...] = jnp.zeros_like(dq_ref)
        dk_ref[...] = jnp.zeros_like(dk_ref)
        dv_ref[...] = jnp.zeros_like(dv_ref)
        first_head = _lane((b, LANES)) < 64
        band_prev, band_cur = _dil_bands()
        for _, r in DIL_PATTERNS:
            nb = n_blk // r

            def group(it, carry, r=r, nb=nb):
                loaded = []
                kc = vc = None
                for u in range(unroll):
                    rows_c, rows_p, has_prev = _dil_block(it * unroll + u, r, nb)
                    if u % min(nb, unroll):
                        kp, vp = kc, vc
                    else:
                        kp, vp = k_ref[rows_p, :].astype(BF16), v_ref[rows_p, :].astype(BF16)
                    kc, vc = k_ref[rows_c, :].astype(BF16), v_ref[rows_c, :].astype(BF16)
                    vals = [q_ref[rows_c, :].astype(BF16), kp, kc, vp, vc, do_ref[rows_c, :], l_ref[rows_c, :], y_ref[rows_c, :]]
                    loaded.append((rows_c, rows_p, has_prev, vals))
                done = []
                for rows_c, rows_p, has_prev, (qv, kp, kc, vp, vc, dof, lv, yv) in loaded:
                    q2 = _two_heads(qv, first_head)
                    do2 = _two_heads(dof.astype(BF16), first_head)
                    kcat, vcat = _cat(kp, kc), _cat(vp, vc)
                    lse2 = _cat(lv[:, 0:1], lv[:, 64:65])
                    dd2 = jnp.sum(_two_heads(dof * yv, first_head), axis=-1, keepdims=True)
                    p = jnp.exp(jnp.where(band_cur | (band_prev & has_prev), _nt(q2, kcat), NEG) - lse2)
                    ds = (p * (_nt(do2, vcat) - dd2)).astype(BF16)
                    dq2 = _nn(ds, kcat)
                    dkcat = _tn(ds, q2)
                    dvcat = _tn(p.astype(BF16), do2)
                    done.append((rows_c, rows_p, (jnp.where(first_head, dq2[:b], dq2[b:]), dkcat[:b], dkcat[b:],
                                                  dvcat[:b], dvcat[b:])))
                for rows_c, rows_p, (dq, dk_p, dk_c, dv_p, dv_c) in done:
                    dq_ref[rows_c, :] += dq
                    dk_ref[rows_p, :] += dk_p
                    dk_ref[rows_c, :] += dk_c
                    dv_ref[rows_p, :] += dv_p
                    dv_ref[rows_c, :] += dv_c
                return carry

            lax.fori_loop(0, n_blk // unroll, group, 0)

        def finish(i, carry, chunk=512):
            rows = pl.ds(pl.multiple_of(i * chunk, chunk), chunk)
            cos, sin = cos_ref[rows, :], sin_ref[rows, :]
            gq_ref[rows, :] = (_rotate(dq_ref[rows, :], cos, sin, -1.0) * QK_SCALE).astype(BF16)
            gk_ref[rows, :] = _rotate(dk_ref[rows, :], cos, sin, -1.0).astype(BF16)
            gv_ref[rows, :] = dv_ref[rows, :].astype(BF16)
            return carry

        lax.fori_loop(0, s // 512, finish, 0)
        if ne:
            @pl.when(pl.program_id(0) == 3)
            def _():
                x_finish(*comm)

    col_blk = lambda k: pl.BlockSpec((s, LANES), lambda hp: (0, 4 * k + hp))
    table = pl.BlockSpec((s, LANES), lambda hp: (0, 0))
    nat = pl.BlockSpec((s, LANES), lambda hp: (0, hp))
    return pl.pallas_call(
        body, grid=(4,), in_specs=[col_blk(Z_QB), col_blk(Z_KB), col_blk(Z_VB), table, table, nat, nat, nat] + [ANY] * ne,
        out_specs=[nat, nat, nat] + [ANY] * ne, out_shape=[SDS((s, ATT_W), BF16)] * 3 + x_shapes(exchange),
        scratch_shapes=[pltpu.VMEM((s, LANES), F32)] * 6 + (x_sems(ne) if ne else []), name="dil_bwd",
        compiler_params=_cp(("arbitrary",)))(z, z, z, cos_t, sin_t, dy, lse, y, *exchange)


def _sigmoid(v):
    return 1.0 / (1.0 + jnp.exp(-v))


def gate_mix(ya, yb, wa, wb, z, *, tm=512, tn=512):
    s = ya.shape[0]
    d = wa.shape[1]
    ga_blk = 3 * ATT_W * 2 // tn
    gb_blk = ga_blk + d // tn

    def body(ya_ref, yb_ref, wa_ref, wb_ref, ga_ref, gb_ref, pa_ref, pb_ref, mx_ref):
        pa = _nn(ya_ref[...], wa_ref[...])
        pb = _nn(yb_ref[...].astype(BF16), wb_ref[...])
        pa_ref[...] = pa.astype(BF16)
        pb_ref[...] = pb.astype(BF16)
        mx_ref[...] = (_sigmoid(ga_ref[...].astype(F32)) * pa + _sigmoid(gb_ref[...].astype(F32)) * pb).astype(BF16)

    out = pl.BlockSpec((tm, tn), lambda i, j: (i, j))
    return pl.pallas_call(
        body, grid=(s // tm, d // tn),
        in_specs=[pl.BlockSpec((tm, ATT_W), lambda i, j: (i, 0)), pl.BlockSpec((tm, ATT_W), lambda i, j: (i, 0)),
                  pl.BlockSpec((ATT_W, tn), lambda i, j: (0, j)), pl.BlockSpec((ATT_W, tn), lambda i, j: (0, j)),
                  pl.BlockSpec((tm, tn), lambda i, j: (i, ga_blk + j)), pl.BlockSpec((tm, tn), lambda i, j: (i, gb_blk + j))],
        out_specs=[out, out, out], out_shape=[SDS((s, d), BF16)] * 3, name="gate_mix",
        compiler_params=_cp(("parallel", "parallel")))(ya, yb, wa, wb, z, z)


def mix_bwd(dy, w_o, z, pa, pb, wo_a, wo_b, ya, *, tm=256):
    s, d = dy.shape

    def body(dy_ref, wo_ref, ga_ref, gb_ref, pa_ref, pb_ref, wa_ref, wb_ref, ya_ref,
             dpa_ref, dpb_ref, dg_ref, dya_ref, dyb_ref, dd_ref):
        dm = _nt(dy_ref[...], wo_ref[...])
        sa = _sigmoid(ga_ref[...].astype(F32))
        sb = _sigmoid(gb_ref[...].astype(F32))
        dpa = (dm * sa).astype(BF16)
        dpb = (dm * sb).astype(BF16)
        dpa_ref[...] = dpa
        dpb_ref[...] = dpb
        dg_ref[:, 0:d] = (dm * pa_ref[...].astype(F32) * sa * (1.0 - sa)).astype(BF16)
        dg_ref[:, d:2 * d] = (dm * pb_ref[...].astype(F32) * sb * (1.0 - sb)).astype(BF16)
        dya = _nt(dpa, wa_ref[...]).astype(BF16)
        dya_ref[...] = dya
        dyb_ref[...] = _nt(dpb, wb_ref[...])
        lane = _lane((tm, LANES))
        for pr in range(ATT_W // LANES):
            pair = slice(pr * LANES, (pr + 1) * LANES)
            prod = dya[:, pair].astype(F32) * ya_ref[:, pair].astype(F32)
            lo = jnp.sum(jnp.where(lane < 64, prod, 0.0), axis=-1, keepdims=True)
            hi = jnp.sum(jnp.where(lane >= 64, prod, 0.0), axis=-1, keepdims=True)
            dd_ref[:, pair] = jnp.where(lane < 64, lo, hi)

    row = pl.BlockSpec((tm, d), lambda i: (i, 0))
    att = pl.BlockSpec((tm, ATT_W), lambda i: (i, 0))
    whole = lambda a: pl.BlockSpec(a.shape, lambda i: (0, 0))
    return pl.pallas_call(
        body, grid=(s // tm,),
        in_specs=[row, whole(w_o), pl.BlockSpec((tm, d), lambda i: (i, 3)), pl.BlockSpec((tm, d), lambda i: (i, 4)), row, row,
                  whole(wo_a), whole(wo_b), att],
        out_specs=[row, row, pl.BlockSpec((tm, 2 * d), lambda i: (i, 0)), att, att, att],
        out_shape=[SDS((s, d), BF16), SDS((s, d), BF16), SDS((s, 2 * d), BF16), SDS((s, ATT_W), BF16),
                   SDS((s, ATT_W), F32), SDS((s, ATT_W), F32)], name="mix_bwd",
        compiler_params=_cp(("parallel",)))(dy, w_o, z, z, pa, pb, wo_a, wo_b, ya)


GELU_C = math.sqrt(2.0 / math.pi)


def _gelu_parts(a):
    a2 = a * a
    th = jnp.tanh(a * (GELU_C + (GELU_C * 0.044715) * a2))
    half = 0.5 * a
    gelu = half + half * th
    dgelu = (0.5 + 0.5 * th) + half * (1.0 - th * th) * (GELU_C + (3.0 * GELU_C * 0.044715) * a2)
    return gelu, dgelu


def _causal_taps(u, before):
    row = _row(u.shape)
    r1 = jnp.where(row == 0, before[7:8, :], pltpu.roll(u, 1, axis=0))
    r2 = jnp.where(row == 0, before[6:7, :], jnp.where(row == 1, before[7:8, :], pltpu.roll(u, 2, axis=0)))
    return r1, r2


def ffn_up(h, wa, wb, cw, cb, *, tm=1024, tn=256):
    s, d = h.shape
    f = wa.shape[1]
    nj = f // tn

    def body(h_ref, wa_ref, wb_ref, cwa_ref, cwb_ref, cba_ref, cbb_ref, ua_ref, ub_ref, ca_ref, cbo_ref, m_ref, carry):
        @pl.when(pl.program_id(1) == 0)
        def _():
            carry[...] = jnp.zeros_like(carry)

        conv = []
        for k, (w_ref, cw_ref, cb_ref, u_ref, c_ref) in enumerate(((wa_ref, cwa_ref, cba_ref, ua_ref, ca_ref),
                                                                   (wb_ref, cwb_ref, cbb_ref, ub_ref, cbo_ref))):
            u16 = _nn(h_ref[...], w_ref[...]).astype(BF16)
            u_ref[...] = u16
            u = u16.astype(F32)
            r1, r2 = _causal_taps(u, carry[k])
            carry[k] = u[tm - 8:tm, :]
            c16 = (cw_ref[0:1, :] * r2 + cw_ref[1:2, :] * r1 + cw_ref[2:3, :] * u + cb_ref[...]).astype(BF16)
            c_ref[...] = c16
            conv.append(c16.astype(F32))
        m_ref[...] = (_gelu_parts(conv[0])[0] * conv[1]).astype(BF16)

    out = pl.BlockSpec((tm, tn), lambda j, i: (i, j))
    return pl.pallas_call(
        body, grid=(nj, s // tm),
        in_specs=[pl.BlockSpec((tm, d), lambda j, i: (i, 0)),
                  pl.BlockSpec((d, tn), lambda j, i: (0, j)), pl.BlockSpec((d, tn), lambda j, i: (0, j)),
                  pl.BlockSpec((3, tn), lambda j, i: (0, j)), pl.BlockSpec((3, tn), lambda j, i: (0, nj + j)),
                  pl.BlockSpec((1, tn), lambda j, i: (0, j)), pl.BlockSpec((1, tn), lambda j, i: (0, nj + j))],
        out_specs=[out] * 5, out_shape=[SDS((s, f), BF16)] * 5,
        scratch_shapes=[pltpu.VMEM((2, 8, tn), F32)], name="ffn_up",
        compiler_params=_cp(("parallel", "arbitrary")))(h, wa, wb, cw, cw, cb, cb)


def ffn_bwd(dm, ua, ub, ca, cbo, cw, *, tm=1024, tn=256):
    s, f = dm.shape
    nj = f // tn
    ni = s // tm

    def body(dm_ref, ua_ref, ub_ref, ca_ref, cbo_ref, cwa_ref, cwb_ref, dua_ref, dub_ref, ga_ref, gb_ref, carry):
        @pl.when(pl.program_id(1) == 0)
        def _():
            carry[...] = jnp.zeros_like(carry)
            ga_ref[...] = jnp.zeros_like(ga_ref)
            gb_ref[...] = jnp.zeros_like(gb_ref)

        row = _row((tm, tn))
        dmv = dm_ref[...].astype(F32)
        gelu, dgelu = _gelu_parts(ca_ref[...].astype(F32))
        dcs = (dmv * cbo_ref[...].astype(F32) * dgelu, dmv * gelu)
        for k, (dc, u_ref, cw_ref, du_ref, g_ref) in enumerate(((dcs[0], ua_ref, cwa_ref, dua_ref, ga_ref),
                                                                (dcs[1], ub_ref, cwb_ref, dub_ref, gb_ref))):
            u = u_ref[...].astype(F32)
            after = carry[k]
            n1 = jnp.where(row == tm - 1, after[0:1, :], pltpu.roll(dc, tm - 1, axis=0))
            n2 = jnp.where(row == tm - 2, after[0:1, :], jnp.where(row == tm - 1, after[1:2, :], pltpu.roll(dc, tm - 2, axis=0)))
            g_ref[0:1, :] += jnp.sum(n2 * u, axis=0, keepdims=True)
            g_ref[1:2, :] += jnp.sum(n1 * u, axis=0, keepdims=True)
            g_ref[2:3, :] += jnp.sum(dc * u, axis=0, keepdims=True)
            g_ref[3:4, :] += jnp.sum(dc, axis=0, keepdims=True)
            du_ref[...] = (cw_ref[2:3, :] * dc + cw_ref[1:2, :] * n1 + cw_ref[0:1, :] * n2).astype(BF16)
            carry[k] = dc[0:8, :]

    tile = pl.BlockSpec((tm, tn), lambda j, i: (ni - 1 - i, j))
    gspec = pl.BlockSpec((8, tn), lambda j, i: (0, j))
    return pl.pallas_call(
        body, grid=(nj, ni),
        in_specs=[tile] * 5 + [pl.BlockSpec((3, tn), lambda j, i: (0, j)), pl.BlockSpec((3, tn), lambda j, i: (0, nj + j))],
        out_specs=[tile, tile, gspec, gspec],
        out_shape=[SDS((s, f), BF16), SDS((s, f), BF16), SDS((8, f), F32), SDS((8, f), F32)],
        scratch_shapes=[pltpu.VMEM((2, 8, tn), F32)], name="ffn_bwd",
        compiler_params=_cp(("parallel", "arbitrary")))(dm, ua, ub, ca, cbo, cw, cw)


def adamw(w, g, m, v, *, name, tr=None):
    r = w.shape[0]
    rest = w.shape[1:]
    if tr is None:
        tr = r
        for cand in (256, 128, 64, 32, 16, 8):
            if r % cand == 0:
                tr = cand
                break

    def body(w_ref, g_ref, m_ref, v_ref, d_ref, nm_ref, nv_ref):
        gv = g_ref[...]
        mn = ADAM_B1 * m_ref[...] + (1.0 - ADAM_B1) * gv
        vn = ADAM_B2 * v_ref[...] + (1.0 - ADAM_B2) * (gv * gv)
        m_hat = mn / (1.0 - ADAM_B1 ** ADAM_STEP)
        v_hat = vn / (1.0 - ADAM_B2 ** ADAM_STEP)
        d_ref[...] = -ADAM_LR * (m_hat / (jnp.sqrt(v_hat) + ADAM_EPS) + ADAM_WD * w_ref[...])
        nm_ref[...] = mn
        nv_ref[...] = vn

    blk = pl.BlockSpec((tr,) + rest, lambda i: (i,) + (0,) * len(rest))
    return pl.pallas_call(body, grid=(r // tr,), in_specs=[blk] * 4, out_specs=[blk] * 3, out_shape=[SDS(w.shape, F32)] * 3,
                          name=name, compiler_params=_cp(("parallel",)))(w, g, m, v)


def adamw_rows_view(w, g_mine, g_full, m, v, c_arr, *, name, tc=256):
    r, _, c = w.shape
    per_half = c // 2 // tc

    def body(c_ref, w_ref, gm_ref, gf_ref, m_ref, v_ref, d_ref, nm_ref, nv_ref, go_ref):
        mine = (pl.program_id(0) >> (per_half.bit_length() - 1)) == c_ref[0]
        gv = jnp.where(mine, gm_ref[...], gf_ref[...])
        mn = ADAM_B1 * m_ref[:, 0, :] + (1.0 - ADAM_B1) * gv
        vn = ADAM_B2 * v_ref[:, 0, :] + (1.0 - ADAM_B2) * (gv * gv)
        m_hat = mn / (1.0 - ADAM_B1 ** ADAM_STEP)
        v_hat = vn / (1.0 - ADAM_B2 ** ADAM_STEP)
        d_ref[:, 0, :] = -ADAM_LR * (m_hat / (jnp.sqrt(v_hat) + ADAM_EPS) + ADAM_WD * w_ref[:, 0, :])
        nm_ref[:, 0, :] = mn
        nv_ref[:, 0, :] = vn
        go_ref[:, 0, :] = gv

    b3 = pl.BlockSpec((r, 1, tc), lambda i, c_ref: (0, 0, i))
    own = pl.BlockSpec((r, tc), lambda i, c_ref: (0, jnp.clip(i - c_ref[0] * per_half, 0, per_half - 1)))
    full = pl.BlockSpec((r, tc), lambda i, c_ref: (0, i))
    grid_spec = pltpu.PrefetchScalarGridSpec(num_scalar_prefetch=1, grid=(c // tc,), in_specs=[b3, own, full, b3, b3],
                                             out_specs=[b3] * 4)
    return pl.pallas_call(body, grid_spec=grid_spec, out_shape=[SDS(w.shape, F32)] * 4, name=name,
                          compiler_params=_cp(("parallel",)))(c_arr, w, g_mine, g_full, m, v)


ANY = pl.BlockSpec(memory_space=pl.ANY)
ICI_KINDS = ("x", "y", "xy")


def _coords():
    return lax.axis_index("x"), lax.axis_index("y"), lax.axis_index("c")


def _peer(kind, x, y, c):
    if kind == "c":
        return (x, y, 1 - c)
    if kind == "x":
        return (1 - x, y, c)
    if kind == "y":
        return (x, 1 - y, c)
    return (1 - x, 1 - y, c)


def _chip_of(p):
    return 2 * p[0] + p[1]


def _half(rows, which):
    h = rows // 2
    return pl.ds(pl.multiple_of(which * h, 16), h)


def _remote(src, dst, send_sem, recv_sem, to):
    return pltpu.make_async_remote_copy(src_ref=src, dst_ref=dst, send_sem=send_sem, recv_sem=recv_sem,
                                        device_id=to, device_id_type=MESH)


def allgather_balanced(shard, *, name):
    r, cols = shard.shape
    h, q = r // 2, r // 4

    def body(in_ref, out_ref, send_sems, recv_sems):
        x, y, c = _coords()
        me, sibling = (x, y, c), (x, y, 1 - c)
        nbr = ((1 - x, y, c), (x, 1 - y, c))
        chip = (2 * (1 - x) + y, 2 * x + (1 - y), 2 * (1 - x) + (1 - y))
        half = lambda core: pl.ds(pl.multiple_of(core * h, 16), h)
        quarter = lambda core, i: pl.ds(pl.multiple_of(core * h + i * q, 16), q)
        sent = []
        for k in range(2):
            cp = _remote(in_ref.at[half(c)], out_ref.at[2 * x + y, half(c)], send_sems.at[k], recv_sems.at[k], nbr[k])
            cp.start()
            sent.append(cp)
        for k in range(2):
            landed = out_ref.at[chip[k], half(c)]
            _remote(landed, landed, send_sems.at[k], recv_sems.at[k], me).wait_recv()
            piece = out_ref.at[chip[k], quarter(c, k)]
            for cp in (_remote(piece, piece, send_sems.at[2 + k], recv_sems.at[2 + k], nbr[1 - k]),
                       _remote(landed, landed, send_sems.at[4 + k], recv_sems.at[4 + k], sibling)):
                cp.start()
                sent.append(cp)
        for k in range(2):
            got = out_ref.at[chip[2], quarter(c, k)]
            _remote(got, got, send_sems.at[2 + k], recv_sems.at[2 + k], me).wait_recv()
            cp = _remote(got, got, send_sems.at[6 + k], recv_sems.at[6 + k], sibling)
            cp.start()
            sent.append(cp)
        for k in range(2):
            for slot, region in ((4 + k, out_ref.at[chip[k], half(1 - c)]), (6 + k, out_ref.at[chip[2], quarter(1 - c, k)])):
                _remote(region, region, send_sems.at[slot], recv_sems.at[slot], me).wait_recv()
        for cp in sent:
            cp.wait_send()

    return pl.pallas_call(
        body, in_specs=[ANY], out_specs=ANY, out_shape=SDS((4,) + shard.shape, shard.dtype),
        scratch_shapes=[pltpu.SemaphoreType.DMA((8,)), pltpu.SemaphoreType.DMA((8,))], name=name)(shard)


def _allgather_shapes(shards):
    return [SDS((4,) + a.shape, a.dtype) for a in shards]


def _allgather_sems(n):
    return [pltpu.SemaphoreType.DMA((n, 6)), pltpu.SemaphoreType.DMA((n, 6))]


def _allgather_rows(ref, is_halved, which):
    r = ref.shape[0]
    return _half(r, which) if is_halved else pl.ds(0, r)


def _allgather_first(ins, outs, send_sems, recv_sems, halved):
    x, y, c = _coords()
    my_chip = 2 * x + y
    cps = []
    for w in range(len(ins)):
        rows = _allgather_rows(ins[w], halved[w], c)
        for k, kind in enumerate(ICI_KINDS):
            cps.append(_remote(ins[w].at[rows], outs[w].at[my_chip, rows], send_sems.at[w, k], recv_sems.at[w, k],
                               _peer(kind, x, y, c)))
    return cps


def _allgather_start(ins, outs, send_sems, recv_sems, halved):
    for cp in _allgather_first(ins, outs, send_sems, recv_sems, halved):
        cp.start()


def _allgather_finish(ins, outs, send_sems, recv_sems, halved):
    x, y, c = _coords()
    me = (x, y, c)
    second = []
    for w in range(len(ins)):
        for k, kind in enumerate(ICI_KINDS):
            landed = outs[w].at[_chip_of(_peer(kind, x, y, c)), _allgather_rows(ins[w], halved[w], c)]
            _remote(landed, landed, send_sems.at[w, k], recv_sems.at[w, k], me).wait_recv()
            if halved[w]:
                cp = _remote(landed, landed, send_sems.at[w, 3 + k], recv_sems.at[w, 3 + k], _peer("c", x, y, c))
                cp.start()
                second.append(cp)
    for w in range(len(ins)):
        if halved[w]:
            for k, kind in enumerate(ICI_KINDS):
                other = outs[w].at[_chip_of(_peer(kind, x, y, c)), _allgather_rows(ins[w], True, 1 - c)]
                _remote(other, other, send_sems.at[w, 3 + k], recv_sems.at[w, 3 + k], me).wait_recv()
    for cp in _allgather_first(ins, outs, send_sems, recv_sems, halved) + second:
        cp.wait_send()


def _half_of(ref, by_cols, which):
    lead = (slice(None),) * (len(ref.shape) - 2)
    if by_cols:
        h = ref.shape[-1] // 2
        return ref.at[lead + (slice(None), pl.ds(pl.multiple_of(which * h, LANES), h))]
    return ref.at[lead + (_half(ref.shape[-2], which),)]


def _half_shape(shape, by_cols):
    return shape[:-1] + (shape[-1] // 2,) if by_cols else shape[:-2] + (shape[-2] // 2, shape[-1])


def grads_to_sibling(gs, by_cols, *, name):
    n = len(gs)

    def body(*refs):
        ins, outs = refs[:n], refs[n:2 * n]
        send_sems, recv_sems = refs[2 * n:]
        x, y, c = _coords()
        cps = []
        for w in range(n):
            cp = _remote(_half_of(ins[w], by_cols[w], 1 - c), outs[w], send_sems.at[w], recv_sems.at[w], _peer("c", x, y, c))
            cp.start()
            cps.append(cp)
        for cp in cps:
            cp.wait()

    return pl.pallas_call(
        body, in_specs=[ANY] * n, out_specs=[ANY] * n,
        out_shape=[SDS(_half_shape(a.shape, bc), a.dtype) for a, bc in zip(gs, by_cols)],
        scratch_shapes=[pltpu.SemaphoreType.DMA((n,)), pltpu.SemaphoreType.DMA((n,))], name=name)(*gs)


def _to_chips_shapes(ps):
    return [SDS((3,) + a.shape[1:], a.dtype) for a in ps]


def _to_chips_sems(n):
    return [pltpu.SemaphoreType.DMA((n, 3)), pltpu.SemaphoreType.DMA((n, 3))]


def _to_chips_copies(ins, outs, send_sems, recv_sems):
    x, y, c = _coords()
    cps = []
    for w in range(len(ins)):
        for k, kind in enumerate(ICI_KINDS):
            to = _peer(kind, x, y, c)
            cps.append(_remote(ins[w].at[_chip_of(to)], outs[w].at[k], send_sems.at[w, k], recv_sems.at[w, k], to))
    return cps


def _to_chips_start(ins, outs, send_sems, recv_sems):
    for cp in _to_chips_copies(ins, outs, send_sems, recv_sems):
        cp.start()


def _to_chips_finish(ins, outs, send_sems, recv_sems):
    for cp in _to_chips_copies(ins, outs, send_sems, recv_sems):
        cp.wait()


def _to_owners_shapes(ps):
    return [SDS((7, a.shape[1] // 2, a.shape[2]), a.dtype) for a in ps]


def _to_owners_sems(n):
    return [pltpu.SemaphoreType.DMA((n, 7)), pltpu.SemaphoreType.DMA((n, 7))]


def _to_owners_copies(ins, outs, send_sems, recv_sems):
    x, y, c = _coords()
    cps = []
    for w in range(len(ins)):
        rows = ins[w].shape[1]
        for k, kind in enumerate(ICI_KINDS):
            px, py, _ = _peer(kind, x, y, c)
            for h in range(2):
                cps.append(_remote(ins[w].at[2 * px + py, _half(rows, h)], outs[w].at[2 * k + c],
                                   send_sems.at[w, 2 * k + h], recv_sems.at[w, 2 * k + c], (px, py, h)))
        cps.append(_remote(ins[w].at[2 * x + y, _half(rows, 1 - c)], outs[w].at[6], send_sems.at[w, 6], recv_sems.at[w, 6],
                           _peer("c", x, y, c)))
    return cps


def _to_owners_start(ins, outs, send_sems, recv_sems):
    for cp in _to_owners_copies(ins, outs, send_sems, recv_sems):
        cp.start()


def _to_owners_finish(ins, outs, send_sems, recv_sems):
    for cp in _to_owners_copies(ins, outs, send_sems, recv_sems):
        cp.wait_send()
    for w in range(len(ins)):
        for slot in range(7):
            got = outs[w].at[slot]
            _remote(got, got, send_sems.at[w, slot], recv_sems.at[w, slot], _coords()).wait_recv()


EXCHANGES = {"to_chips": (_to_chips_shapes, _to_chips_sems, _to_chips_start, _to_chips_finish),
             "to_owners": (_to_owners_shapes, _to_owners_sems, _to_owners_start, _to_owners_finish)}


def halves_to_full(hs, by_cols, *, name):
    n = len(hs)

    def body(*refs):
        ins, outs = refs[:n], refs[n:2 * n]
        send_sems, recv_sems = refs[2 * n:]
        x, y, c = _coords()
        cps = []
        for w in range(n):
            cp = _remote(ins[w], _half_of(outs[w], by_cols[w], c), send_sems.at[w], recv_sems.at[w], _peer("c", x, y, c))
            cp.start()
            cps.append(cp)
        for cp in cps:
            cp.wait()

    return pl.pallas_call(
        body, in_specs=[ANY] * n, out_specs=[ANY] * n,
        out_shape=[SDS((a.shape[0], 2 * a.shape[1]) if bc else (2 * a.shape[0], a.shape[1]), a.dtype)
                   for a, bc in zip(hs, by_cols)],
        scratch_shapes=[pltpu.SemaphoreType.DMA((n,)), pltpu.SemaphoreType.DMA((n,))],
        name=name)(*hs)


def _row_tile(rows):
    for cand in (256, 192, 176, 128, 64, 32, 16):
        if rows % cand == 0:
            return cand
    return rows


def chip_sum(g, recv, c_arr, by_cols, *, name):
    _, r, cols = g.shape

    def body(c_ref, g_ref, r_ref, f_ref, b_ref):
        tot = g_ref[...] + r_ref[...]
        f_ref[...] = tot
        b_ref[...] = tot.astype(BF16)

    if by_cols:
        tc = 2 * LANES
        nblk = cols // 2 // tc
        shape = (4, r, cols // 2)
        blk = pl.BlockSpec((None, r, tc), lambda j, i, c_ref: (j, 0, i))
        mine = pl.BlockSpec((None, r, tc), lambda j, i, c_ref: (j, 0, c_ref[0] * nblk + i))
    else:
        tr = _row_tile(r // 2)
        nblk = r // 2 // tr
        shape = (4, r // 2, cols)
        blk = pl.BlockSpec((None, tr, cols), lambda j, i, c_ref: (j, i, 0))
        mine = pl.BlockSpec((None, tr, cols), lambda j, i, c_ref: (j, c_ref[0] * nblk + i, 0))
    grid_spec = pltpu.PrefetchScalarGridSpec(num_scalar_prefetch=1, grid=(4, nblk), in_specs=[mine, blk], out_specs=[blk, blk])
    return pl.pallas_call(body, grid_spec=grid_spec, out_shape=[SDS(shape, F32), SDS(shape, BF16)],
                          name=name, compiler_params=_cp(("parallel", "parallel")))(c_arr, g, recv)


def final_sum(pf, recv, chip_arr, *, name):
    _, h, cols = pf.shape
    tr = _row_tile(h)

    def body(chip_ref, p_ref, r_ref, o_ref):
        o_ref[...] = ((p_ref[...] + r_ref[0].astype(F32)) + r_ref[1].astype(F32)) + r_ref[2].astype(F32)

    grid_spec = pltpu.PrefetchScalarGridSpec(
        num_scalar_prefetch=1, grid=(h // tr,),
        in_specs=[pl.BlockSpec((None, tr, cols), lambda i, chip_ref: (chip_ref[0], i, 0)),
                  pl.BlockSpec((3, tr, cols), lambda i, chip_ref: (0, i, 0))],
        out_specs=pl.BlockSpec((tr, cols), lambda i, chip_ref: (i, 0)))
    return pl.pallas_call(body, grid_spec=grid_spec, out_shape=SDS((h, cols), F32), name=name,
                          compiler_params=_cp(("parallel",)))(chip_arr, pf, recv)


def owner_sum(g, recv, pos_arr, *, name):
    _, r, cols = g.shape
    h = r // 2
    tr = _row_tile(h)
    nblk = h // tr

    def body(pos_ref, g_ref, r_ref, o_ref):
        tot = g_ref[...]
        for slot in range(7):
            tot = tot + r_ref[slot].astype(F32)
        o_ref[...] = tot

    grid_spec = pltpu.PrefetchScalarGridSpec(
        num_scalar_prefetch=1, grid=(nblk,),
        in_specs=[pl.BlockSpec((None, tr, cols), lambda i, pos: (pos[0], pos[1] * nblk + i, 0)),
                  pl.BlockSpec((7, tr, cols), lambda i, pos: (0, i, 0))],
        out_specs=pl.BlockSpec((tr, cols), lambda i, pos: (i, 0)))
    return pl.pallas_call(body, grid_spec=grid_spec, out_shape=SDS((h, cols), F32), name=name,
                          compiler_params=_cp(("parallel",)))(pos_arr, g, recv)


def allreduce_small(v, *, name):
    rws, cols = v.shape

    def body(v_ref, all_ref, sum_ref, send_sems, recv_sems, local_sem):
        x, y, c = _coords()
        me, sibling = (x, y, c), (x, y, 1 - c)
        chips = [(1 - x, y), (x, 1 - y), (1 - x, 1 - y)]

        def rows(px, py, pc):
            return all_ref.at[pl.ds(pl.multiple_of((4 * px + 2 * py + pc) * rws, 8), rws), :]

        def copy(k, block, to, src=None):
            return _remote(rows(*block) if src is None else src, rows(*block), send_sems.at[k], recv_sems.at[k], to)

        mine = pltpu.make_async_copy(v_ref, rows(*me), local_sem)
        mine.start()
        first = [copy(0, me, sibling, src=v_ref)]
        first += [copy(1 + j, me, (*chip, c), src=v_ref) for j, chip in enumerate(chips)]
        for cp in first:
            cp.start()
        passed = [copy(4 + j, (*chip, c), sibling) for j, chip in enumerate(chips)]
        for j, chip in enumerate(chips):
            copy(1 + j, (*chip, c), me).wait_recv()
            passed[j].start()
        copy(0, sibling, me).wait_recv()
        for j, chip in enumerate(chips):
            copy(4 + j, (*chip, 1 - c), me).wait_recv()
        for cp in first + passed:
            cp.wait_send()
        mine.wait()
        tot = all_ref[0:rws, :]
        for dev in range(1, 8):
            tot = tot + all_ref[dev * rws:(dev + 1) * rws, :]
        sum_ref[...] = tot

    vm = pl.BlockSpec(memory_space=pltpu.VMEM)
    return pl.pallas_call(
        body, in_specs=[vm], out_specs=[vm, vm],
        out_shape=[SDS((8 * rws, cols), v.dtype), SDS((rws, cols), v.dtype)],
        scratch_shapes=[pltpu.SemaphoreType.DMA((7,)), pltpu.SemaphoreType.DMA((7,)), pltpu.SemaphoreType.DMA],
        name=name)(v)[1]


def _pack_rows(parts, rows):
    out = []
    for a, r in zip(parts, rows):
        flat = a.reshape(-1)
        flat = jnp.pad(flat, (0, r * LANES - flat.shape[0]))
        out.append(flat.reshape(r, LANES))
    return jnp.concatenate(out, axis=0)


def _unpack_rows(packed, shapes, rows):
    out, at = [], 0
    for shp, r in zip(shapes, rows):
        size = int(np.prod(shp))
        out.append(packed[at:at + r].reshape(-1)[:size].reshape(shp))
        at += r
    return out


def kernel(x, g_pre_mix, w_in, b_forget, w_o_fox, w_o_dil, w_out, g_post_mix, g_pre_ffn, w_up, conv_w, conv_b, w_down, g_post_ffn, loss_target, m_g_pre_mix, m_w_in, m_b_forget, m_w_o_fox, m_w_o_dil, m_w_out, m_g_post_mix, m_g_pre_ffn, m_w_up, m_conv_w, m_conv_b, m_w_down, m_g_post_ffn, v_g_pre_mix, v_w_in, v_b_forget, v_w_o_fox, v_w_o_dil, v_w_out, v_g_post_mix, v_g_pre_ffn, v_w_up, v_conv_w, v_conv_b, v_w_down, v_g_post_ffn):
    xi, yi, ci = _coords()
    chip = 2 * xi + yi
    c_arr = jnp.reshape(ci, (1,)).astype(jnp.int32)
    chip_arr = jnp.reshape(chip, (1,)).astype(jnp.int32)
    xs = x[0]
    target = loss_target[0]
    s, d = xs.shape
    f_half = w_down.shape[1] * 4
    cols_in = w_in.shape[2]

    big = (w_in, w_o_fox, w_o_dil, w_out, w_up, w_down)
    shards = [w[0].astype(BF16) for w in big]
    a_in = allgather_balanced(shards[0], name="allgather_w_in")
    w_in_full = jnp.concatenate([jnp.where(chip == j, shards[0], a_in[j]) for j in range(4)], axis=1)
    nf = N_HEADS
    e_a, e_b = 3 * ATT_W, 3 * ATT_W + nf
    wz = jnp.concatenate([w_in_full[:, :e_a], w_in_full[:, e_b:]], axis=1)
    wf = jnp.pad(w_in_full[:, e_a:e_b], ((0, 0), (0, LANES - nf)))
    cb = conv_b
    bfo = jnp.pad(b_forget, ((0, 0), (0, LANES - nf)))

    h1 = rmsnorm_fwd(xs, g_pre_mix)
    z = mm([(h1, d, 0)], [(wz, d, 0)], nt=False, out_dtype=BF16, tm=1024, tn=512, name="in_proj")
    fa = mm([(h1, d, 0)], [(wf, d, 0)], nt=False, out_dtype=F32, tm=1024, tn=LANES, name="in_proj_forget")
    q_aug, k_aug, v_aug = fox_prep(z, fa, bfo)
    later = shards[1:] + [conv_w[0]]
    ya, lse_a, *late = fox_fwd(q_aug, k_aug, v_aug, gather=later, halved=[True] * 5 + [False], hps=N_HEADS)
    a_of, a_od, a_out, a_up, a_down, a_cw = [
        lax.dynamic_update_index_in_dim(a4, own, chip, 0) for a4, own in zip(late, later)]
    cw = jnp.concatenate([a_cw[j] for j in range(4)], axis=1)
    wo_a = jnp.concatenate([a_of[j] for j in range(4)], axis=1)
    wo_b = jnp.concatenate([a_od[j] for j in range(4)], axis=1)
    w_o = a_out.reshape(d, d)
    w_dn = a_down.reshape(f_half, d)
    wu_a = jnp.concatenate([a_up[0], a_up[1]], axis=1)
    wu_b = jnp.concatenate([a_up[2], a_up[3]], axis=1)
    cos_t, sin_t = rope_cos_sin(s)
    yb, lse_b = dil_fwd_all(z, cos_t, sin_t)
    pa, pb, mixed = gate_mix(ya, yb, wo_a, wo_b, z)
    y1, x1, h2 = proj_norm_res(mixed, w_o, g_post_mix, xs, g_pre_ffn, name="out_proj")
    ua, ub, conv_a, conv_bh, mid = ffn_up(h2, wu_a, wu_b, cw, cb)
    dout, dy2, gg_post_ffn, sq = proj_norm_loss(mid, w_dn, g_post_ffn, x1, target, name="down_proj")
    loss = lax.psum(0.5 * sq[0, 0] / d, ("x", "y", "c"))

    dmid = mm([(dy2, d, 0)], [(w_dn, d, 0)], nt=True, out_dtype=BF16, tm=512, tn=f_half // 2, name="down_dgrad")
    dw_down, dw_down16 = wgrad((mid, f_half, 0), dy2, tk=f_half // 2, tn=1024, ts=1024, name="down_wgrad", bf16_copy=True)
    dua, dub, gc_a, gc_b = ffn_bwd(dmid, ua, ub, conv_a, conv_bh, cw)
    dx1, dy1, gg_pre_ffn, gg_post_mix = mm_norm_bwd(
        [(dua, f_half, 0), (dub, f_half, 0)], [(wu_a, f_half, 0), (wu_b, f_half, 0)],
        [(x1, g_pre_ffn, dout, F32), (y1, g_post_mix, None, BF16)], name="up_dgrad")
    dw_up = None
    for k, du in enumerate((dua, dub)):
        dw_up = wgrad((h2, d, 0), du, tk=1024, tn=f_half // 2, ts=1024, name=f"up_wgrad_{k}", chip_major=True,
                      slabs=(4, 2 * k), into=dw_up, bf16_copy=True)
    g_ffn = [(dw_up[0], dw_up[1]), (dw_down.reshape(4, f_half // 4, d), dw_down16.reshape(4, f_half // 4, d))]
    dw_out, dw_out16 = wgrad((mixed, d, 0), dy1, tk=1024, tn=1024, ts=1024, name="out_wgrad", bf16_copy=True)
    dpa, dpb, dz_g, dya, dyb, dd_a = mix_bwd(dy1, w_o, z, pa, pb, wo_a, wo_b, ya)
    by_chip_cols = lambda a: jnp.stack([a[:, j * (d // 4):(j + 1) * (d // 4)] for j in range(4)], axis=0)
    dw_of = [by_chip_cols(a) for a in wgrad((ya, ATT_W, 0), dpa, tk=ATT_W, tn=d, ts=1024, name="fox_o_wgrad", bf16_copy=True)]
    dw_od = [by_chip_cols(a) for a in wgrad((yb, ATT_W, 0), dpb, tk=ATT_W, tn=d, ts=1024, name="dil_o_wgrad", bf16_copy=True)]
    g_mix = [dw_of, dw_od, (dw_out.reshape(4, d // 4, d), dw_out16.reshape(4, d // 4, d))]
    dq_aug, dk_aug, dv_a, *got_ffn = fox_bwd(q_aug, k_aug, z, dya, lse_a, dd_a, exchange=[g[1] for g in g_ffn], kind="to_owners")
    dz_a, dfa, gg_bf = fox_post(dq_aug, dk_aug, dv_a, fa, bfo)
    *dz_b, got_of, got_od, got_out = dil_bwd_all(z, cos_t, sin_t, dyb, lse_b, yb, exchange=[g[1] for g in g_mix],
                                                 kind="to_owners")
    got_mix = [got_of, got_od, got_out]
    dwt_a = wgrad((dz_a, e_a, 0), h1, tk=e_a // 2, tn=d, ts=1024, name="in_wgrad_a")
    dwt_b = [wgrad((part, ATT_W, 0), h1, tk=ATT_W, tn=d, ts=1024, name=f"in_wgrad_b{k}") for k, part in enumerate(dz_b)]
    dwt_g = wgrad((dz_g, 2 * d, 0), h1, tk=d, tn=d, ts=1024, name="in_wgrad_g")
    dwt_f = wgrad((dfa, LANES, 0), h1, tk=LANES, tn=d, ts=1024, name="in_wgrad_f")
    dwt_full = jnp.concatenate([dwt_a, dwt_f[:nf], *dwt_b, dwt_g], axis=0)
    dw_in = jnp.stack([dwt_full[j * cols_in:(j + 1) * cols_in] for j in range(4)], axis=0)
    from_sib = grads_to_sibling([dw_in], [True], name="grads_to_sibling_in")
    sum_in = chip_sum(dw_in, from_sib[0], c_arr, True, name="chip_sum_w_in")
    grad_x, gg_pre_mix, got_in = mm_norm_bwd(
        [(dz_a, e_a, 0), *[(part, ATT_W, 0) for part in dz_b], (dz_g, d, 0), (dz_g, d, 1), (dfa, LANES, 0)],
        [(wz, e_a, 0), *[(wz, ATT_W, Z_QB + k) for k in range(3)], (wz, d, 3), (wz, d, 4), (wf, LANES, 0)],
        [(xs, g_pre_mix, dx1, F32)], exchange=[sum_in[1]], name="in_dgrad")

    names = ("w_in", "w_o_fox", "w_o_dil", "w_out", "w_up", "w_down")
    pos_arr = jnp.concatenate([chip_arr, c_arr])
    halves = [final_sum(sum_in[0], got_in, chip_arr, name="final_sum_w_in")] + [
        owner_sum(g[0], got, pos_arr, name=f"owner_sum_{nm}") for g, got, nm in zip(g_mix + g_ffn, got_mix + got_ffn, names[1:])]
    from_half = halves_to_full(halves, [True] + [False] * 5, name="halves_to_full")
    g_big = [None] + [lax.dynamic_update_slice_in_dim(full, mine, ci * mine.shape[0], axis=0)
                      for full, mine in zip(from_half[1:], halves[1:])]
    upd_big = [adamw(w[0], g, m[0], v[0], name=f"adamw_{nm}") for w, g, m, v, nm in list(zip(
        big, g_big, (m_w_in, m_w_o_fox, m_w_o_dil, m_w_out, m_w_up, m_w_down),
        (v_w_in, v_w_o_fox, v_w_o_dil, v_w_out, v_w_up, v_w_down), names))[1:]]
    to_t = lambda a: jnp.transpose(a, (2, 0, 1))
    from_t = lambda a: jnp.transpose(a, (1, 2, 0))
    *upd_in, g_in_t = adamw_rows_view(to_t(w_in), halves[0], from_half[0], to_t(m_w_in), to_t(v_w_in), c_arr,
                                      name="adamw_w_in")

    g_cw_loc = jnp.concatenate([gc_a[0:3], gc_b[0:3]], axis=1)
    g_cb_loc = jnp.concatenate([gc_a[3:4], gc_b[3:4]], axis=1)
    small_loc = [gg_pre_mix, gg_post_mix, gg_pre_ffn, gg_post_ffn, g_cb_loc, gg_bf[:, :nf], g_cw_loc]
    red_rows = (8, 8, 8, 8, 48, 8, 136)
    red = allreduce_small(_pack_rows(small_loc, red_rows), name="allreduce_small")
    g_pm, g_qm, g_pf, g_qf, g_cb, g_bf, g_cw_full = _unpack_rows(red, [a.shape for a in small_loc], red_rows)
    cols_cw = conv_w.shape[2]
    g_cw = lax.dynamic_slice_in_dim(g_cw_full, chip * cols_cw, cols_cw, axis=1)
    small_w = (g_pre_mix, g_post_mix, g_pre_ffn, g_post_ffn, conv_b, b_forget, conv_w[0])
    small_m = (m_g_pre_mix, m_g_post_mix, m_g_pre_ffn, m_g_post_ffn, m_conv_b, m_b_forget, m_conv_w[0])
    small_v = (v_g_pre_mix, v_g_post_mix, v_g_pre_ffn, v_g_post_ffn, v_conv_b, v_b_forget, v_conv_w[0])
    small_g = (g_pm, g_qm, g_pf, g_qf, g_cb, g_bf, g_cw)
    small_names = ("g_pre_mix", "g_post_mix", "g_pre_ffn", "g_post_ffn", "conv_b", "b_forget", "conv_w")
    per_param = [adamw(w, g, m, v, name=f"adamw_{nm}") for w, g, m, v, nm in zip(small_w, small_g, small_m, small_v, small_names)]
    upd_small = [[u[j] for u in per_param] for j in range(3)]

    order = ("g_pre_mix", "w_in", "b_forget", "w_o_fox", "w_o_dil", "w_out", "g_post_mix", "g_pre_ffn", "w_up", "conv_w",
             "conv_b", "w_down", "g_post_ffn")
    grads, deltas, new_ms, new_vs = {}, {}, {}, {}
    grads["w_in"] = from_t(g_in_t)
    deltas["w_in"], new_ms["w_in"], new_vs["w_in"] = (from_t(a) for a in upd_in)
    for k, nm in enumerate(names[1:]):
        grads[nm] = g_big[k + 1][None]
        deltas[nm], new_ms[nm], new_vs[nm] = (a[None] for a in upd_big[k])
    for k, nm in enumerate(small_names):
        lead = (lambda a: a[None]) if nm == "conv_w" else (lambda a: a)
        grads[nm] = lead(small_g[k])
        deltas[nm], new_ms[nm], new_vs[nm] = (lead(upd_small[j][k]) for j in range(3))
    return (loss, grad_x[None], *[grads[nm] for nm in order], *[deltas[nm] for nm in order],
            *[new_ms[nm] for nm in order], *[new_vs[nm] for nm in order])
```

```python
import functools
import math

import numpy as np
import jax
import jax.numpy as jnp
from jax import lax
from jax.experimental import pallas as pl
from jax.experimental.pallas import tpu as pltpu

F32 = jnp.float32
BF16 = jnp.bfloat16
SDS = jax.ShapeDtypeStruct
MESH = pl.DeviceIdType.MESH

HEAD_DIM = 64
N_HEADS = 8
LANES = 128
ATT_W = N_HEADS * HEAD_DIM
DIL_PATTERNS = ((128, 1), (512, 4), (2048, 16))
DIL_BLK = 128
ROPE_DIM = HEAD_DIM // 4
ROPE_THETA = 500000.0
RMS_EPS = 1e-6
NEG = -1e30
QK_SCALE = 1.0 / math.sqrt(HEAD_DIM)
ADAM_LR, ADAM_B1, ADAM_B2, ADAM_EPS, ADAM_WD, ADAM_STEP = 0.001, 0.9, 0.999, 1e-08, 0.01, 10
VMEM_LIMIT = 56 * 1024 * 1024

Z_QA, Z_KA, Z_VA, Z_QB, Z_KB, Z_VB = 0, 1, 2, 3, 4, 5
Z_W = 5120


def _cp(sem):
    return pltpu.CompilerParams(dimension_semantics=sem, vmem_limit_bytes=VMEM_LIMIT)


def _nt(a, b):
    return lax.dot_general(a, b, (((1,), (1,)), ((), ())), preferred_element_type=F32)


def _tn(a, b):
    return lax.dot_general(a, b, (((0,), (0,)), ((), ())), preferred_element_type=F32)


def _nn(a, b):
    return jnp.dot(a, b, preferred_element_type=F32)


def _lane(shape):
    return lax.broadcasted_iota(jnp.int32, shape, 1)


def _row(shape):
    return lax.broadcasted_iota(jnp.int32, shape, 0)


def rmsnorm_fwd(x, g, *, tm=512):
    s, d = x.shape

    def body(x_ref, g_ref, h_ref):
        xv = x_ref[...]
        inv = lax.rsqrt(jnp.mean(xv * xv, axis=-1, keepdims=True) + RMS_EPS)
        h_ref[...] = (xv * inv * g_ref[...]).astype(h_ref.dtype)

    return pl.pallas_call(
        body, grid=(s // tm,),
        in_specs=[pl.BlockSpec((tm, d), lambda i: (i, 0)), pl.BlockSpec((1, d), lambda i: (0, 0))],
        out_specs=pl.BlockSpec((tm, d), lambda i: (i, 0)),
        out_shape=SDS((s, d), BF16), name="rmsnorm_fwd", compiler_params=_cp(("parallel",)))(x, g)


def mm(a_views, b_views, *, nt, out_dtype, tm, tn, name):
    n_p = len(a_views)
    m = a_views[0][0].shape[0]
    n = b_views[0][0].shape[0] if nt else b_views[0][0].shape[1]

    def body(*refs):
        o_ref = refs[-1]
        acc = None
        for p in range(n_p):
            av = refs[p][...].astype(BF16)
            bv = refs[n_p + p][...].astype(BF16)
            dv = _nt(av, bv) if nt else _nn(av, bv)
            acc = dv if acc is None else acc + dv
        o_ref[...] = acc.astype(o_ref.dtype)

    in_specs = []
    for arr, w, blk in a_views:
        in_specs.append(pl.BlockSpec((tm, w), functools.partial(lambda i, j, blk: (i, blk), blk=blk)))
    for arr, w, blk in b_views:
        if nt:
            in_specs.append(pl.BlockSpec((tn, w), functools.partial(lambda i, j, blk: (j, blk), blk=blk)))
        else:
            in_specs.append(pl.BlockSpec((w, tn), lambda i, j: (0, j)))
    return pl.pallas_call(
        body, grid=(m // tm, n // tn), in_specs=in_specs,
        out_specs=pl.BlockSpec((tm, tn), lambda i, j: (i, j)),
        out_shape=SDS((m, n), out_dtype), name=name,
        compiler_params=_cp(("parallel", "parallel")))(*[a[0] for a in a_views], *[b[0] for b in b_views])


def wgrad(a_view, g, *, tk, tn, ts, name, chip_major=False, slabs=None, into=None, bf16_copy=False):
    arr, ka, blk = a_view
    s, n = g.shape
    ns = s // ts
    total, first = slabs if slabs else (n // tn, 0)
    n_into = 0 if into is None else (2 if bf16_copy else 1)

    def body(a_ref, g_ref, *rest):
        o_ref = rest[n_into]

        @pl.when(pl.program_id(2) == 0)
        def _():
            o_ref[...] = jnp.zeros_like(o_ref)

        o_ref[...] += _tn(a_ref[...].astype(BF16), g_ref[...].astype(BF16))
        if bf16_copy:
            @pl.when(pl.program_id(2) == ns - 1)
            def _():
                rest[n_into + 1][...] = o_ref[...].astype(BF16)

    if chip_major:
        out_spec = pl.BlockSpec((None, tk, tn), lambda i, j, k: (first + j, i, 0))
        shape = (total, ka, tn)
    else:
        out_spec = pl.BlockSpec((tk, tn), lambda i, j, k: (i, j))
        shape = (ka, n)
    in_specs = [pl.BlockSpec((ts, tk), lambda i, j, k: (k, blk * (ka // tk) + i)),
                pl.BlockSpec((ts, tn), lambda i, j, k: (k, j))]
    args = [arr, g]
    if into is not None:
        earlier = list(into) if bf16_copy else [into]
        in_specs += [pl.BlockSpec(memory_space=pl.ANY)] * len(earlier)
        args += earlier
    out = pl.pallas_call(
        body, grid=(ka // tk, n // tn, ns), in_specs=in_specs,
        out_specs=[out_spec, out_spec] if bf16_copy else out_spec,
        out_shape=[SDS(shape, F32), SDS(shape, BF16)] if bf16_copy else SDS(shape, F32), name=name,
        input_output_aliases={2 + k: k for k in range(n_into)},
        compiler_params=_cp(("parallel", "parallel", "arbitrary")))(*args)
    return out


def _norm_bwd_rows(dh, xh, inv, g):
    dxh = dh * g
    dx = inv * (dxh - xh * jnp.mean(dxh * xh, axis=-1, keepdims=True))
    return dx, jnp.sum((dh * xh).reshape(dh.shape[0] // 8, 8, dh.shape[1]), axis=0)


def proj_norm_res(a, w, g, xres, g_next, *, tm=512, name):
    s, k = a.shape
    d = w.shape[1]

    def body(a_ref, w_ref, g_ref, x_ref, gn_ref, y_ref, o_ref, h_ref):
        y = _nn(a_ref[...], w_ref[...])
        inv = lax.rsqrt(jnp.mean(y * y, axis=-1, keepdims=True) + RMS_EPS)
        xn = x_ref[...] + y * inv * g_ref[...]
        y_ref[...] = y
        o_ref[...] = xn
        inv_n = lax.rsqrt(jnp.mean(xn * xn, axis=-1, keepdims=True) + RMS_EPS)
        h_ref[...] = (xn * inv_n * gn_ref[...]).astype(h_ref.dtype)

    row = pl.BlockSpec((tm, d), lambda i: (i, 0))
    vec = pl.BlockSpec((1, d), lambda i: (0, 0))
    return pl.pallas_call(
        body, grid=(s // tm,),
        in_specs=[pl.BlockSpec((tm, k), lambda i: (i, 0)), pl.BlockSpec((k, d), lambda i: (0, 0)), vec, row, vec],
        out_specs=[row, row, row], out_shape=[SDS((s, d), F32), SDS((s, d), F32), SDS((s, d), BF16)], name=name,
        compiler_params=_cp(("parallel",)))(a, w, g, xres, g_next)


def proj_norm_loss(a, w, g, xres, target, *, tm=512, name):
    s, k = a.shape
    d = w.shape[1]
    n = s // tm

    def body(a_ref, w_ref, g_ref, x_ref, t_ref, do_ref, dy_ref, dg_ref, l_ref, acc):
        i = pl.program_id(0)

        @pl.when(i == 0)
        def _():
            acc[...] = jnp.zeros_like(acc)
            l_ref[...] = jnp.zeros_like(l_ref)

        y = _nn(a_ref[...], w_ref[...])
        inv = lax.rsqrt(jnp.mean(y * y, axis=-1, keepdims=True) + RMS_EPS)
        yh = y * inv
        err = x_ref[...] + yh * g_ref[...] - t_ref[...]
        dout = err * (1.0 / d)
        do_ref[...] = dout
        l_ref[...] += jnp.sum(jnp.sum(err * err, axis=1, keepdims=True), axis=0, keepdims=True)
        dy, part = _norm_bwd_rows(dout, yh, inv, g_ref[...])
        dy_ref[...] = dy.astype(dy_ref.dtype)
        acc[...] += part

        @pl.when(i == n - 1)
        def _():
            dg_ref[...] = jnp.sum(acc[...], axis=0, keepdims=True)

    row = pl.BlockSpec((tm, d), lambda i: (i, 0))
    vec = pl.BlockSpec((1, d), lambda i: (0, 0))
    return pl.pallas_call(
        body, grid=(n,),
        in_specs=[pl.BlockSpec((tm, k), lambda i: (i, 0)), pl.BlockSpec((k, d), lambda i: (0, 0)), vec, row, row],
        out_specs=[row, row, vec, pl.BlockSpec((1, 1), lambda i: (0, 0))],
        out_shape=[SDS((s, d), F32), SDS((s, d), BF16), SDS((1, d), F32), SDS((1, 1), F32)],
        scratch_shapes=[pltpu.VMEM((8, d), F32)], name=name, compiler_params=_cp(("arbitrary",)))(a, w, g, xres, target)


def mm_norm_bwd(a_views, b_views, stages, exchange=(), *, tm=256, name):
    n_p, n_s, ne = len(a_views), len(stages), len(exchange)
    s = a_views[0][0].shape[0]
    d = b_views[0][0].shape[0]
    n = s // tm
    has_res = [st[2] is not None for st in stages]

    def body(*refs):
        a_refs, b_refs = refs[:n_p], refs[n_p:2 * n_p]
        at = 2 * n_p
        st_refs = []
        for k in range(n_s):
            cnt = 3 if has_res[k] else 2
            st_refs.append(refs[at:at + cnt])
            at += cnt
        e_ins = refs[at:at + ne]
        at += ne
        dx_refs, dg_refs = refs[at:at + n_s], refs[at + n_s:at + 2 * n_s]
        at += 2 * n_s
        e_outs = refs[at:at + ne]
        at += ne
        accs = refs[at:at + n_s]
        comm = (e_ins, e_outs) + tuple(refs[at + n_s:])
        i = pl.program_id(0)

        @pl.when(i == 0)
        def _():
            for acc in accs:
                acc[...] = jnp.zeros_like(acc)
            if ne:
                _to_chips_start(*comm)

        dh = None
        for p in range(n_p):
            part = _nt(a_refs[p][...].astype(BF16), b_refs[p][...].astype(BF16))
            dh = part if dh is None else dh + part
        for k in range(n_s):
            xv = st_refs[k][0][...]
            inv = lax.rsqrt(jnp.mean(xv * xv, axis=-1, keepdims=True) + RMS_EPS)
            dx, part = _norm_bwd_rows(dh, xv * inv, inv, st_refs[k][1][...])
            if has_res[k]:
                dx = dx + st_refs[k][2][...]
            dx_refs[k][...] = dx.astype(dx_refs[k].dtype)
            accs[k][...] += part
            dh = dx

        @pl.when(i == n - 1)
        def _():
            for k in range(n_s):
                dg_refs[k][...] = jnp.sum(accs[k][...], axis=0, keepdims=True)
            if ne:
                _to_chips_finish(*comm)

    row = pl.BlockSpec((tm, d), lambda i: (i, 0))
    vec = pl.BlockSpec((1, d), lambda i: (0, 0))
    in_specs, args = [], []
    for arr, w, blk in a_views:
        in_specs.append(pl.BlockSpec((tm, w), functools.partial(lambda i, blk: (i, blk), blk=blk)))
        args.append(arr)
    for arr, w, blk in b_views:
        in_specs.append(pl.BlockSpec((d, w), functools.partial(lambda i, blk: (0, blk), blk=blk)))
        args.append(arr)
    for x, g, res, _ in stages:
        in_specs += [row, vec] + ([row] if res is not None else [])
        args += [x, g] + ([res] if res is not None else [])
    return pl.pallas_call(
        body, grid=(n,), in_specs=in_specs + [ANY] * ne,
        out_specs=[row] * n_s + [vec] * n_s + [ANY] * ne,
        out_shape=[SDS((s, d), st[3]) for st in stages] + [SDS((1, d), F32)] * n_s + _to_chips_shapes(exchange),
        scratch_shapes=[pltpu.VMEM((8, d), F32)] * n_s + (_to_chips_sems(ne) if ne else []), name=name,
        compiler_params=_cp(("arbitrary",)))(*args, *exchange)


def _split3(v):
    hi = v.astype(BF16).astype(F32)
    r = v - hi
    mid = r.astype(BF16).astype(F32)
    lo = (r - mid).astype(BF16).astype(F32)
    return hi, mid, lo


def _tri(n, upper):
    r = np.arange(n)
    m = (r[:, None] <= r[None, :]) if upper else (r[:, None] >= r[None, :])
    return jnp.asarray(m.astype(np.float32))


def fox_prep(z, fa, bfo, *, tb=512):
    s = z.shape[0]
    n = s // tb

    def body(q_ref, k_ref, v_ref, fa_ref, b_ref, tri_ref, qa_ref, ka_ref, va_ref, carry):
        @pl.when(pl.program_id(0) == 0)
        def _():
            carry[...] = jnp.zeros_like(carry)

        xv = fa_ref[...] + b_ref[...]
        logf = jnp.minimum(xv, 0.0) - jnp.log(1.0 + jnp.exp(-jnp.abs(xv)))
        csum = jnp.dot(tri_ref[...], logf, preferred_element_type=F32, precision=lax.Precision.HIGHEST) + carry[0:1, :]
        carry[0:1, :] = csum[tb - 1:tb, :]
        lane = _lane((tb, LANES))
        for h in range(N_HEADS):
            hi, mid, lo = _split3(csum[:, h:h + 1])
            pair = (h // 2) * LANES
            qv = q_ref[:, pair:pair + LANES].astype(F32)
            kv = k_ref[:, pair:pair + LANES].astype(F32)
            vv = v_ref[:, pair:pair + LANES].astype(F32)
            if h % 2:
                qv = pltpu.roll(qv, 64, axis=1)
                kv = pltpu.roll(kv, 64, axis=1)
                vv = pltpu.roll(vv, 64, axis=1)
            va_ref[:, h * LANES:(h + 1) * LANES] = jnp.where(lane < 64, vv, jnp.where(lane == 64, 1.0, 0.0)).astype(BF16)
            one = jnp.where((lane >= 67) & (lane < 70), 1.0, 0.0)
            q_x = jnp.where(lane == 64, hi, jnp.where(lane == 65, mid, jnp.where(lane == 66, lo, one)))
            one = jnp.where((lane >= 64) & (lane < 67), 1.0, 0.0)
            k_x = jnp.where(lane == 67, -hi, jnp.where(lane == 68, -mid, jnp.where(lane == 69, -lo, one)))
            qa_ref[:, h * LANES:(h + 1) * LANES] = jnp.where(lane < 64, qv * QK_SCALE, q_x).astype(BF16)
            ka_ref[:, h * LANES:(h + 1) * LANES] = jnp.where(lane < 64, kv, k_x).astype(BF16)

    return pl.pallas_call(
        body, grid=(n,),
        in_specs=[pl.BlockSpec((tb, ATT_W), lambda i: (i, Z_QA)), pl.BlockSpec((tb, ATT_W), lambda i: (i, Z_KA)),
                  pl.BlockSpec((tb, ATT_W), lambda i: (i, Z_VA)),
                  pl.BlockSpec((tb, LANES), lambda i: (i, 0)), pl.BlockSpec((1, LANES), lambda i: (0, 0)),
                  pl.BlockSpec((tb, tb), lambda i: (0, 0))],
        out_specs=[pl.BlockSpec((tb, N_HEADS * LANES), lambda i: (i, 0))] * 3,
        out_shape=[SDS((s, N_HEADS * LANES), BF16)] * 3,
        scratch_shapes=[pltpu.VMEM((8, LANES), F32)],
        name="fox_prep", compiler_params=_cp(("arbitrary",)))(z, z, z, fa, bfo, _tri(tb, False))


def _causal_pairs(n, k_major):
    if k_major:
        pairs = [(qi, kj) for kj in range(n) for qi in range(kj, n)]
    else:
        pairs = [(qi, kj) for qi in range(n) for kj in range(qi + 1)]
    return (jnp.asarray([p[0] for p in pairs], jnp.int32), jnp.asarray([p[1] for p in pairs], jnp.int32), len(pairs))


def fox_fwd(q_aug, k_aug, v_aug, gather=(), halved=(), *, t=512, hps=4):
    s = v_aug.shape[0]
    qi_arr, kj_arr, n_pairs = _causal_pairs(s // t, False)
    ng = len(gather)
    n_groups = N_HEADS // hps

    def body(qi_ref, kj_ref, q_ref, k_ref, v_ref, *rest):
        g_ins, (o_ref, lse_ref), g_outs = rest[:ng], rest[ng:ng + 2], rest[ng + 2:2 * ng + 2]
        m_scr, acc_scr = rest[2 * ng + 2:2 * ng + 4]
        comm = (g_ins, g_outs) + tuple(rest[2 * ng + 4:]) + (list(halved),)
        step = pl.program_id(1)
        qi = qi_ref[step]
        kj = kj_ref[step]
        if ng:
            @pl.when((pl.program_id(0) == 0) & (step == 0))
            def _():
                _allgather_start(*comm)

        @pl.when(kj == 0)
        def _():
            m_scr[...] = jnp.full_like(m_scr, NEG)
            acc_scr[...] = jnp.zeros_like(acc_scr)

        def update(masked):
            for i in range(hps):
                sc = _nt(q_ref[:, i * LANES:(i + 1) * LANES], k_ref[:, i * LANES:(i + 1) * LANES])
                if masked:
                    sc = jnp.where(_row((t, t)) >= _lane((t, t)), sc, NEG)
                m_prev = m_scr[i]
                m_new = jnp.maximum(m_prev, jnp.max(sc, axis=-1, keepdims=True))
                p = jnp.exp((sc - jnp.tile(m_new, (1, t // LANES))).astype(BF16))
                acc_scr[i] = jnp.exp(m_prev - m_new) * acc_scr[i] + _nn(p, v_ref[:, i * LANES:(i + 1) * LANES])
                m_scr[i] = m_new

        @pl.when(kj < qi)
        def _():
            update(False)

        @pl.when(kj == qi)
        def _():
            update(True)
            lane = _lane((t, LANES))
            for pr in range(hps // 2):
                den = [acc_scr[2 * pr + i][:, 64:65] for i in range(2)]
                o_ref[:, pr * LANES:(pr + 1) * LANES] = jnp.where(
                    lane < 64, acc_scr[2 * pr] / den[0], pltpu.roll(acc_scr[2 * pr + 1] / den[1], 64, axis=1)).astype(o_ref.dtype)
                lse_ref[:, pr * LANES:(pr + 1) * LANES] = jnp.where(
                    lane < 64, m_scr[2 * pr] + jnp.log(den[0]), m_scr[2 * pr + 1] + jnp.log(den[1]))

        if ng:
            @pl.when((pl.program_id(0) == n_groups - 1) & (step == n_pairs - 1))
            def _():
                _allgather_finish(*comm)

    wide = hps * LANES
    grid_spec = pltpu.PrefetchScalarGridSpec(
        num_scalar_prefetch=2, grid=(n_groups, n_pairs),
        in_specs=[pl.BlockSpec((t, wide), lambda hg, st, qi, kj: (qi[st], hg)),
                  pl.BlockSpec((t, wide), lambda hg, st, qi, kj: (kj[st], hg)),
                  pl.BlockSpec((t, wide), lambda hg, st, qi, kj: (kj[st], hg))] + [ANY] * ng,
        out_specs=[pl.BlockSpec((t, wide // 2), lambda hg, st, qi, kj: (qi[st], hg))] * 2 + [ANY] * ng,
        scratch_shapes=[pltpu.VMEM((hps, t, LANES), F32)] * 2 + (_allgather_sems(ng) if ng else []))
    return pl.pallas_call(
        body, grid_spec=grid_spec, out_shape=[SDS((s, ATT_W), BF16), SDS((s, ATT_W), F32)] + _allgather_shapes(gather),
        name="fox_fwd", compiler_params=_cp(("arbitrary", "arbitrary")))(qi_arr, kj_arr, q_aug, k_aug, v_aug, *gather)


def fox_bwd(q_aug, k_aug, z, dy, lse, dd, exchange=(), kind="to_chips", *, t=512, hps=4):
    s = z.shape[0]
    qi_arr, kj_arr, n_pairs = _causal_pairs(s // t, True)
    ne = len(exchange)
    n_groups = N_HEADS // hps
    x_shapes, x_sems, x_start, x_finish = EXCHANGES[kind]

    def body(qi_ref, kj_ref, q_ref, k_ref, v_ref, do_ref, lse_ref, dd_ref, *rest):
        e_ins, (dq_ref, dk_ref, dv_ref), e_outs = rest[:ne], rest[ne:ne + 3], rest[ne + 3:2 * ne + 3]
        comm = (e_ins, e_outs) + tuple(rest[2 * ne + 3:])
        step = pl.program_id(1)
        qi = qi_ref[step]
        kj = kj_ref[step]
        if ne:
            @pl.when((pl.program_id(0) == 0) & (step == 0))
            def _():
                x_start(*comm)

        @pl.when(step == 0)
        def _():
            dq_ref[...] = jnp.zeros_like(dq_ref)

        @pl.when(qi == kj)
        def _():
            dk_ref[...] = jnp.zeros_like(dk_ref)
            dv_ref[...] = jnp.zeros_like(dv_ref)

        def update(masked):
            lane = _lane((t, LANES))
            rows = pl.ds(pl.multiple_of(qi * t, t), t)
            for pr in range(hps // 2):
                pair = slice(pr * LANES, (pr + 1) * LANES)
                dov = do_ref[:, pair]
                dv_new = None
                for i in range(2):
                    head = (lane < 64) if i == 0 else (lane >= 64)
                    own = slice((2 * pr + i) * LANES, (2 * pr + i + 1) * LANES)
                    col = slice(pr * LANES + i * 64, pr * LANES + i * 64 + 1)
                    qv = q_ref[:, own]
                    kv = k_ref[:, own]
                    sc = _nt(qv, kv)
                    if masked:
                        sc = jnp.where(_row((t, t)) >= _lane((t, t)), sc, NEG)
                    p = jnp.exp(sc - lse_ref[:, col])
                    dp = _nt(jnp.where(head, dov, jnp.zeros_like(dov)), v_ref[:, pair])
                    ds = (p * (dp - dd_ref[:, col])).astype(BF16)
                    dq_ref[rows, own] += _nn(ds, kv)
                    dk_ref[:, own] += _tn(ds, qv)
                    dvi = _tn(p.astype(BF16), dov)
                    dv_new = dvi if dv_new is None else jnp.where(head, dvi, dv_new)
                dv_ref[:, pair] += dv_new

        @pl.when(kj < qi)
        def _():
            update(False)

        @pl.when(kj == qi)
        def _():
            update(True)

        if ne:
            @pl.when((pl.program_id(0) == n_groups - 1) & (step == n_pairs - 1))
            def _():
                x_finish(*comm)

    wide, half = hps * LANES, hps // 2 * LANES
    v_blk = Z_VA * ATT_W // half
    grid_spec = pltpu.PrefetchScalarGridSpec(
        num_scalar_prefetch=2, grid=(n_groups, n_pairs),
        in_specs=[pl.BlockSpec((t, wide), lambda hg, st, qi, kj: (qi[st], hg)),
                  pl.BlockSpec((t, wide), lambda hg, st, qi, kj: (kj[st], hg)),
                  pl.BlockSpec((t, half), lambda hg, st, qi, kj: (kj[st], v_blk + hg)),
                  pl.BlockSpec((t, half), lambda hg, st, qi, kj: (qi[st], hg)),
                  pl.BlockSpec((t, half), lambda hg, st, qi, kj: (qi[st], hg)),
                  pl.BlockSpec((t, half), lambda hg, st, qi, kj: (qi[st], hg))] + [ANY] * ne,
        out_specs=[pl.BlockSpec((s, wide), lambda hg, st, qi, kj: (0, hg)),
                   pl.BlockSpec((t, wide), lambda hg, st, qi, kj: (kj[st], hg)),
                   pl.BlockSpec((t, half), lambda hg, st, qi, kj: (kj[st], hg))] + [ANY] * ne,
        scratch_shapes=x_sems(ne) if ne else [])
    return pl.pallas_call(
        body, grid_spec=grid_spec,
        out_shape=[SDS((s, N_HEADS * LANES), F32), SDS((s, N_HEADS * LANES), F32), SDS((s, ATT_W), F32)]
        + x_shapes(exchange),
        name="fox_bwd", compiler_params=_cp(("arbitrary", "arbitrary")))(qi_arr, kj_arr, q_aug, k_aug, z, dy, lse, dd, *exchange)


def fox_post(dq_aug, dk_aug, dv, fa, bfo, *, tb=512):
    s = dv.shape[0]
    n = s // tb

    def body(dq_ref, dk_ref, dv_ref, fa_ref, b_ref, tri_ref, dz_ref, dfa_ref, gb_ref, carry, acc):
        i = pl.program_id(0)

        @pl.when(i == 0)
        def _():
            carry[...] = jnp.zeros_like(carry)
            acc[...] = jnp.zeros_like(acc)

        lane = _lane((tb, LANES))
        d_f = jnp.zeros((tb, LANES), F32)
        for h in range(N_HEADS):
            col = dq_ref[:, h * LANES + 64:h * LANES + 65] - dk_ref[:, h * LANES + 67:h * LANES + 68]
            d_f = jnp.where(lane == h, col, d_f)
        suffix = jnp.dot(tri_ref[...], d_f, preferred_element_type=F32, precision=lax.Precision.HIGHEST) + carry[0:1, :]
        carry[0:1, :] = suffix[0:1, :]
        xv = fa_ref[...] + b_ref[...]
        dx = suffix * (1.0 / (1.0 + jnp.exp(xv)))
        dfa_ref[...] = dx.astype(dfa_ref.dtype)
        acc[...] += jnp.sum(dx.reshape(tb // 8, 8, LANES), axis=0)
        for hp in range(4):
            for src, off, scale in ((dq_ref, 0, QK_SCALE), (dk_ref, ATT_W, 1.0)):
                even = src[:, (2 * hp) * LANES:(2 * hp + 1) * LANES]
                odd = pltpu.roll(src[:, (2 * hp + 1) * LANES:(2 * hp + 2) * LANES], 64, axis=1)
                dz_ref[:, off + hp * LANES:off + (hp + 1) * LANES] = (jnp.where(lane < 64, even, odd) * scale).astype(BF16)
        dz_ref[:, 2 * ATT_W:3 * ATT_W] = dv_ref[...].astype(BF16)

        @pl.when(i == n - 1)
        def _():
            gb_ref[...] = jnp.sum(acc[...], axis=0, keepdims=True)

    rev = lambda i: (n - 1 - i, 0)
    return pl.pallas_call(
        body, grid=(n,),
        in_specs=[pl.BlockSpec((tb, N_HEADS * LANES), rev), pl.BlockSpec((tb, N_HEADS * LANES), rev),
                  pl.BlockSpec((tb, ATT_W), rev), pl.BlockSpec((tb, LANES), rev),
                  pl.BlockSpec((1, LANES), lambda i: (0, 0)), pl.BlockSpec((tb, tb), lambda i: (0, 0))],
        out_specs=[pl.BlockSpec((tb, 3 * ATT_W), rev), pl.BlockSpec((tb, LANES), rev),
                   pl.BlockSpec((1, LANES), lambda i: (0, 0))],
        out_shape=[SDS((s, 3 * ATT_W), BF16), SDS((s, LANES), BF16), SDS((1, LANES), F32)],
        scratch_shapes=[pltpu.VMEM((8, LANES), F32), pltpu.VMEM((8, LANES), F32)],
        name="fox_post", compiler_params=_cp(("arbitrary",)))(dq_aug, dk_aug, dv, fa, bfo, _tri(tb, True))


def rope_cos_sin(s):
    half = ROPE_DIM // 2
    inv_freq = ROPE_THETA ** (-jnp.arange(half, dtype=F32) * 2.0 / ROPE_DIM)
    ang = jnp.arange(s, dtype=F32)[:, None] * inv_freq[None, :]
    return jnp.tile(jnp.cos(ang), (1, LANES // half)), jnp.tile(jnp.sin(ang), (1, LANES // half))


def _rotate(x, cos, sin, sign):
    l64 = _lane(x.shape) & (HEAD_DIM - 1)
    first = l64 < ROPE_DIM // 2
    second = (l64 >= ROPE_DIM // 2) & (l64 < ROPE_DIM)
    from_next = jnp.where(first, -sign * sin, 0.0)
    from_prev = jnp.where(second, sign * sin, 0.0)
    return (x * jnp.where(first | second, cos, 1.0) + pltpu.roll(x, LANES - 8, axis=1) * from_next
            + pltpu.roll(x, 8, axis=1) * from_prev)


def _dil_rows(base, r):
    if r == 1:
        return pl.ds(pl.multiple_of(base, DIL_BLK), DIL_BLK)
    return pl.ds(base, DIL_BLK, stride=r)


def _dil_block(idx, r, nb):
    shift = nb.bit_length() - 1
    rho = idx >> shift
    n = idx & (nb - 1)
    base = rho + n * (r * DIL_BLK)
    return _dil_rows(base, r), _dil_rows(jnp.maximum(base - r * DIL_BLK, rho), r), n > 0


def _cat(a, b):
    return jnp.concatenate([a, b], axis=0)


def _two_heads(v, first_head):
    zero = jnp.zeros_like(v)
    return _cat(jnp.where(first_head, v, zero), jnp.where(first_head, zero, v))


def _dil_bands():
    b = DIL_BLK
    q = _row((2 * b, 2 * b)) & (b - 1)
    col = _lane((2 * b, 2 * b))
    return (col < b) & (col >= q), (col >= b) & (col - b <= q)


def _dil_load_qkv(zq_ref, zk_ref, zv_ref, cos_ref, sin_ref, q_ref, k_ref, v_ref, *, chunk=512):
    def step(i, carry):
        rows = pl.ds(pl.multiple_of(i * chunk, chunk), chunk)
        cos, sin = cos_ref[rows, :], sin_ref[rows, :]
        q_ref[rows, :] = _rotate(zq_ref[rows, :].astype(F32), cos, sin, 1.0) * QK_SCALE
        k_ref[rows, :] = _rotate(zk_ref[rows, :].astype(F32), cos, sin, 1.0)
        v_ref[rows, :] = zv_ref[rows, :].astype(F32)
        return carry

    lax.fori_loop(0, q_ref.shape[0] // chunk, step, 0)


def dil_fwd_all(z, cos_t, sin_t, *, unroll=8):
    s = z.shape[0]
    b = DIL_BLK
    n_blk = s // b

    def body(zq_ref, zk_ref, zv_ref, cos_ref, sin_ref, o_ref, l_ref, q_ref, k_ref, v_ref):
        _dil_load_qkv(zq_ref, zk_ref, zv_ref, cos_ref, sin_ref, q_ref, k_ref, v_ref)
        first_head = _lane((b, LANES)) < 64
        band_prev, band_cur = _dil_bands()
        for g, (_, r) in enumerate(DIL_PATTERNS):
            nb = n_blk // r

            def group(it, carry, g=g, r=r, nb=nb):
                loaded = []
                kc = vc = None
                for u in range(unroll):
                    rows_c, rows_p, has_prev = _dil_block(it * unroll + u, r, nb)
                    if u % min(nb, unroll):
                        kp, vp = kc, vc
                    else:
                        kp, vp = k_ref[rows_p, :].astype(BF16), v_ref[rows_p, :].astype(BF16)
                    kc, vc = k_ref[rows_c, :].astype(BF16), v_ref[rows_c, :].astype(BF16)
                    state = (o_ref[rows_c, :], l_ref[rows_c, :]) if g else None
                    loaded.append((rows_c, has_prev, [q_ref[rows_c, :].astype(BF16), kp, kc, vp, vc], state))
                done = []
                for rows_c, has_prev, (qv, kp, kc, vp, vc), state in loaded:
                    sc = jnp.where(band_cur | (band_prev & has_prev), _nt(_two_heads(qv, first_head), _cat(kp, kc)), NEG)
                    m = jnp.max(sc, axis=-1, keepdims=True)
                    p = jnp.exp(sc - m)
                    den = jnp.sum(p, axis=-1, keepdims=True)
                    both = _nn(p.astype(BF16), _cat(vp, vc)) / den
                    lse2 = m + jnp.log(den)
                    ov = jnp.where(first_head, both[:b], both[b:])
                    lse = jnp.where(first_head, lse2[:b], lse2[b:])
                    if state is not None:
                        m2 = jnp.maximum(state[1], lse)
                        wp = jnp.exp(state[1] - m2)
                        wn = jnp.exp(lse - m2)
                        ov = (wp * state[0] + wn * ov) / (wp + wn)
                        lse = m2 + jnp.log(wp + wn)
                    done.append((rows_c, ov, lse))
                for rows_c, ov, lse in done:
                    o_ref[rows_c, :] = ov
                    l_ref[rows_c, :] = lse
                return carry

            lax.fori_loop(0, n_blk // unroll, group, 0)

    col_blk = lambda k: pl.BlockSpec((s, LANES), lambda hp: (0, 4 * k + hp))
    table = pl.BlockSpec((s, LANES), lambda hp: (0, 0))
    out = pl.BlockSpec((s, LANES), lambda hp: (0, hp))
    return pl.pallas_call(
        body, grid=(4,), in_specs=[col_blk(Z_QB), col_blk(Z_KB), col_blk(Z_VB), table, table], out_specs=[out, out],
        out_shape=[SDS((s, ATT_W), F32)] * 2, scratch_shapes=[pltpu.VMEM((s, LANES), F32)] * 3, name="dil_fwd",
        compiler_params=_cp(("parallel",)))(z, z, z, cos_t, sin_t)


def dil_bwd_all(z, cos_t, sin_t, dy, lse, y, exchange=(), kind="to_chips", *, unroll=8):
    s = z.shape[0]
    b = DIL_BLK
    n_blk = s // b
    ne = len(exchange)
    x_shapes, x_sems, x_start, x_finish = EXCHANGES[kind]

    def body(zq_ref, zk_ref, zv_ref, cos_ref, sin_ref, do_ref, l_ref, y_ref, *rest):
        e_ins, (gq_ref, gk_ref, gv_ref), e_outs = rest[:ne], rest[ne:ne + 3], rest[ne + 3:2 * ne + 3]
        q_ref, k_ref, v_ref, dq_ref, dk_ref, dv_ref = rest[2 * ne + 3:2 * ne + 9]
        comm = (e_ins, e_outs) + tuple(rest[2 * ne + 9:])
        if ne:
            @pl.when(pl.program_id(0) == 0)
            def _():
                x_start(*comm)

        _dil_load_qkv(zq_ref, zk_ref, zv_ref, cos_ref, sin_ref, q_ref, k_ref, v_ref)
        dq_ref[...] = jnp.zeros_like(dq_ref)
        dk_ref[...] = jnp.zeros_like(dk_ref)
        dv_ref[...] = jnp.zeros_like(dv_ref)
        first_head = _lane((b, LANES)) < 64
        band_prev, band_cur = _dil_bands()
        for _, r in DIL_PATTERNS:
            nb = n_blk // r

            def group(it, carry, r=r, nb=nb):
                loaded = []
                kc = vc = None
                for u in range(unroll):
                    rows_c, rows_p, has_prev = _dil_block(it * unroll + u, r, nb)
                    if u % min(nb, unroll):
                        kp, vp = kc, vc
                    else:
                        kp, vp = k_ref[rows_p, :].astype(BF16), v_ref[rows_p, :].astype(BF16)
                    kc, vc = k_ref[rows_c, :].astype(BF16), v_ref[rows_c, :].astype(BF16)
                    vals = [q_ref[rows_c, :].astype(BF16), kp, kc, vp, vc, do_ref[rows_c, :], l_ref[rows_c, :], y_ref[rows_c, :]]
                    loaded.append((rows_c, rows_p, has_prev, vals))
                done = []
                for rows_c, rows_p, has_prev, (qv, kp, kc, vp, vc, dof, lv, yv) in loaded:
                    q2 = _two_heads(qv, first_head)
                    do2 = _two_heads(dof.astype(BF16), first_head)
                    kcat, vcat = _cat(kp, kc), _cat(vp, vc)
                    lse2 = _cat(lv[:, 0:1], lv[:, 64:65])
                    dd2 = jnp.sum(_two_heads(dof * yv, first_head), axis=-1, keepdims=True)
                    p = jnp.exp(jnp.where(band_cur | (band_prev & has_prev), _nt(q2, kcat), NEG) - lse2)
                    ds = (p * (_nt(do2, vcat) - dd2)).astype(BF16)
                    dq2 = _nn(ds, kcat)
                    dkcat = _tn(ds, q2)
                    dvcat = _tn(p.astype(BF16), do2)
                    done.append((rows_c, rows_p, (jnp.where(first_head, dq2[:b], dq2[b:]), dkcat[:b], dkcat[b:],
                                                  dvcat[:b], dvcat[b:])))
                for rows_c, rows_p, (dq, dk_p, dk_c, dv_p, dv_c) in done:
                    dq_ref[rows_c, :] += dq
                    dk_ref[rows_p, :] += dk_p
                    dk_ref[rows_c, :] += dk_c
                    dv_ref[rows_p, :] += dv_p
                    dv_ref[rows_c, :] += dv_c
                return carry

            lax.fori_loop(0, n_blk // unroll, group, 0)

        def finish(i, carry, chunk=512):
            rows = pl.ds(pl.multiple_of(i * chunk, chunk), chunk)
            cos, sin = cos_ref[rows, :], sin_ref[rows, :]
            gq_ref[rows, :] = (_rotate(dq_ref[rows, :], cos, sin, -1.0) * QK_SCALE).astype(BF16)
            gk_ref[rows, :] = _rotate(dk_ref[rows, :], cos, sin, -1.0).astype(BF16)
            gv_ref[rows, :] = dv_ref[rows, :].astype(BF16)
            return carry

        lax.fori_loop(0, s // 512, finish, 0)
        if ne:
            @pl.when(pl.program_id(0) == 3)
            def _():
                x_finish(*comm)

    col_blk = lambda k: pl.BlockSpec((s, LANES), lambda hp: (0, 4 * k + hp))
    table = pl.BlockSpec((s, LANES), lambda hp: (0, 0))
    nat = pl.BlockSpec((s, LANES), lambda hp: (0, hp))
    return pl.pallas_call(
        body, grid=(4,), in_specs=[col_blk(Z_QB), col_blk(Z_KB), col_blk(Z_VB), table, table, nat, nat, nat] + [ANY] * ne,
        out_specs=[nat, nat, nat] + [ANY] * ne, out_shape=[SDS((s, ATT_W), BF16)] * 3 + x_shapes(exchange),
        scratch_shapes=[pltpu.VMEM((s, LANES), F32)] * 6 + (x_sems(ne) if ne else []), name="dil_bwd",
        compiler_params=_cp(("arbitrary",)))(z, z, z, cos_t, sin_t, dy, lse, y, *exchange)


def _sigmoid(v):
    return 1.0 / (1.0 + jnp.exp(-v))


def gate_mix(ya, yb, wa, wb, z, *, tm=512, tn=512):
    s = ya.shape[0]
    d = wa.shape[1]
    ga_blk = 3 * ATT_W * 2 // tn
    gb_blk = ga_blk + d // tn

    def body(ya_ref, yb_ref, wa_ref, wb_ref, ga_ref, gb_ref, pa_ref, pb_ref, mx_ref):
        pa = _nn(ya_ref[...], wa_ref[...])
        pb = _nn(yb_ref[...].astype(BF16), wb_ref[...])
        pa_ref[...] = pa.astype(BF16)
        pb_ref[...] = pb.astype(BF16)
        mx_ref[...] = (_sigmoid(ga_ref[...].astype(F32)) * pa + _sigmoid(gb_ref[...].astype(F32)) * pb).astype(BF16)

    out = pl.BlockSpec((tm, tn), lambda i, j: (i, j))
    return pl.pallas_call(
        body, grid=(s // tm, d // tn),
        in_specs=[pl.BlockSpec((tm, ATT_W), lambda i, j: (i, 0)), pl.BlockSpec((tm, ATT_W), lambda i, j: (i, 0)),
                  pl.BlockSpec((ATT_W, tn), lambda i, j: (0, j)), pl.BlockSpec((ATT_W, tn), lambda i, j: (0, j)),
                  pl.BlockSpec((tm, tn), lambda i, j: (i, ga_blk + j)), pl.BlockSpec((tm, tn), lambda i, j: (i, gb_blk + j))],
        out_specs=[out, out, out], out_shape=[SDS((s, d), BF16)] * 3, name="gate_mix",
        compiler_params=_cp(("parallel", "parallel")))(ya, yb, wa, wb, z, z)


def mix_bwd(dy, w_o, z, pa, pb, wo_a, wo_b, ya, *, tm=256):
    s, d = dy.shape

    def body(dy_ref, wo_ref, ga_ref, gb_ref, pa_ref, pb_ref, wa_ref, wb_ref, ya_ref,
             dpa_ref, dpb_ref, dg_ref, dya_ref, dyb_ref, dd_ref):
        dm = _nt(dy_ref[...], wo_ref[...])
        sa = _sigmoid(ga_ref[...].astype(F32))
        sb = _sigmoid(gb_ref[...].astype(F32))
        dpa = (dm * sa).astype(BF16)
        dpb = (dm * sb).astype(BF16)
        dpa_ref[...] = dpa
        dpb_ref[...] = dpb
        dg_ref[:, 0:d] = (dm * pa_ref[...].astype(F32) * sa * (1.0 - sa)).astype(BF16)
        dg_ref[:, d:2 * d] = (dm * pb_ref[...].astype(F32) * sb * (1.0 - sb)).astype(BF16)
        dya = _nt(dpa, wa_ref[...]).astype(BF16)
        dya_ref[...] = dya
        dyb_ref[...] = _nt(dpb, wb_ref[...])
        lane = _lane((tm, LANES))
        for pr in range(ATT_W // LANES):
            pair = slice(pr * LANES, (pr + 1) * LANES)
            prod = dya[:, pair].astype(F32) * ya_ref[:, pair].astype(F32)
            lo = jnp.sum(jnp.where(lane < 64, prod, 0.0), axis=-1, keepdims=True)
            hi = jnp.sum(jnp.where(lane >= 64, prod, 0.0), axis=-1, keepdims=True)
            dd_ref[:, pair] = jnp.where(lane < 64, lo, hi)

    row = pl.BlockSpec((tm, d), lambda i: (i, 0))
    att = pl.BlockSpec((tm, ATT_W), lambda i: (i, 0))
    whole = lambda a: pl.BlockSpec(a.shape, lambda i: (0, 0))
    return pl.pallas_call(
        body, grid=(s // tm,),
        in_specs=[row, whole(w_o), pl.BlockSpec((tm, d), lambda i: (i, 3)), pl.BlockSpec((tm, d), lambda i: (i, 4)), row, row,
                  whole(wo_a), whole(wo_b), att],
        out_specs=[row, row, pl.BlockSpec((tm, 2 * d), lambda i: (i, 0)), att, att, att],
        out_shape=[SDS((s, d), BF16), SDS((s, d), BF16), SDS((s, 2 * d), BF16), SDS((s, ATT_W), BF16),
                   SDS((s, ATT_W), F32), SDS((s, ATT_W), F32)], name="mix_bwd",
        compiler_params=_cp(("parallel",)))(dy, w_o, z, z, pa, pb, wo_a, wo_b, ya)


GELU_C = math.sqrt(2.0 / math.pi)


def _gelu_parts(a):
    a2 = a * a
    th = jnp.tanh(a * (GELU_C + (GELU_C * 0.044715) * a2))
    half = 0.5 * a
    gelu = half + half * th
    dgelu = (0.5 + 0.5 * th) + half * (1.0 - th * th) * (GELU_C + (3.0 * GELU_C * 0.044715) * a2)
    return gelu, dgelu


def _causal_taps(u, before):
    row = _row(u.shape)
    r1 = jnp.where(row == 0, before[7:8, :], pltpu.roll(u, 1, axis=0))
    r2 = jnp.where(row == 0, before[6:7, :], jnp.where(row == 1, before[7:8, :], pltpu.roll(u, 2, axis=0)))
    return r1, r2


def ffn_up(h, wa, wb, cw, cb, *, tm=1024, tn=256):
    s, d = h.shape
    f = wa.shape[1]
    nj = f // tn

    def body(h_ref, wa_ref, wb_ref, cwa_ref, cwb_ref, cba_ref, cbb_ref, ua_ref, ub_ref, ca_ref, cbo_ref, m_ref, carry):
        @pl.when(pl.program_id(1) == 0)
        def _():
            carry[...] = jnp.zeros_like(carry)

        conv = []
        for k, (w_ref, cw_ref, cb_ref, u_ref, c_ref) in enumerate(((wa_ref, cwa_ref, cba_ref, ua_ref, ca_ref),
                                                                   (wb_ref, cwb_ref, cbb_ref, ub_ref, cbo_ref))):
            u16 = _nn(h_ref[...], w_ref[...]).astype(BF16)
            u_ref[...] = u16
            u = u16.astype(F32)
            r1, r2 = _causal_taps(u, carry[k])
            carry[k] = u[tm - 8:tm, :]
            c16 = (cw_ref[0:1, :] * r2 + cw_ref[1:2, :] * r1 + cw_ref[2:3, :] * u + cb_ref[...]).astype(BF16)
            c_ref[...] = c16
            conv.append(c16.astype(F32))
        m_ref[...] = (_gelu_parts(conv[0])[0] * conv[1]).astype(BF16)

    out = pl.BlockSpec((tm, tn), lambda j, i: (i, j))
    return pl.pallas_call(
        body, grid=(nj, s // tm),
        in_specs=[pl.BlockSpec((tm, d), lambda j, i: (i, 0)),
                  pl.BlockSpec((d, tn), lambda j, i: (0, j)), pl.BlockSpec((d, tn), lambda j, i: (0, j)),
                  pl.BlockSpec((3, tn), lambda j, i: (0, j)), pl.BlockSpec((3, tn), lambda j, i: (0, nj + j)),
                  pl.BlockSpec((1, tn), lambda j, i: (0, j)), pl.BlockSpec((1, tn), lambda j, i: (0, nj + j))],
        out_specs=[out] * 5, out_shape=[SDS((s, f), BF16)] * 5,
        scratch_shapes=[pltpu.VMEM((2, 8, tn), F32)], name="ffn_up",
        compiler_params=_cp(("parallel", "arbitrary")))(h, wa, wb, cw, cw, cb, cb)


def ffn_bwd(dm, ua, ub, ca, cbo, cw, *, tm=1024, tn=256):
    s, f = dm.shape
    nj = f // tn
    ni = s // tm

    def body(dm_ref, ua_ref, ub_ref, ca_ref, cbo_ref, cwa_ref, cwb_ref, dua_ref, dub_ref, ga_ref, gb_ref, carry):
        @pl.when(pl.program_id(1) == 0)
        def _():
            carry[...] = jnp.zeros_like(carry)
            ga_ref[...] = jnp.zeros_like(ga_ref)
            gb_ref[...] = jnp.zeros_like(gb_ref)

        row = _row((tm, tn))
        dmv = dm_ref[...].astype(F32)
        gelu, dgelu = _gelu_parts(ca_ref[...].astype(F32))
        dcs = (dmv * cbo_ref[...].astype(F32) * dgelu, dmv * gelu)
        for k, (dc, u_ref, cw_ref, du_ref, g_ref) in enumerate(((dcs[0], ua_ref, cwa_ref, dua_ref, ga_ref),
                                                                (dcs[1], ub_ref, cwb_ref, dub_ref, gb_ref))):
            u = u_ref[...].astype(F32)
            after = carry[k]
            n1 = jnp.where(row == tm - 1, after[0:1, :], pltpu.roll(dc, tm - 1, axis=0))
            n2 = jnp.where(row == tm - 2, after[0:1, :], jnp.where(row == tm - 1, after[1:2, :], pltpu.roll(dc, tm - 2, axis=0)))
            g_ref[0:1, :] += jnp.sum(n2 * u, axis=0, keepdims=True)
            g_ref[1:2, :] += jnp.sum(n1 * u, axis=0, keepdims=True)
            g_ref[2:3, :] += jnp.sum(dc * u, axis=0, keepdims=True)
            g_ref[3:4, :] += jnp.sum(dc, axis=0, keepdims=True)
            du_ref[...] = (cw_ref[2:3, :] * dc + cw_ref[1:2, :] * n1 + cw_ref[0:1, :] * n2).astype(BF16)
            carry[k] = dc[0:8, :]

    tile = pl.BlockSpec((tm, tn), lambda j, i: (ni - 1 - i, j))
    gspec = pl.BlockSpec((8, tn), lambda j, i: (0, j))
    return pl.pallas_call(
        body, grid=(nj, ni),
        in_specs=[tile] * 5 + [pl.BlockSpec((3, tn), lambda j, i: (0, j)), pl.BlockSpec((3, tn), lambda j, i: (0, nj + j))],
        out_specs=[tile, tile, gspec, gspec],
        out_shape=[SDS((s, f), BF16), SDS((s, f), BF16), SDS((8, f), F32), SDS((8, f), F32)],
        scratch_shapes=[pltpu.VMEM((2, 8, tn), F32)], name="ffn_bwd",
        compiler_params=_cp(("parallel", "arbitrary")))(dm, ua, ub, ca, cbo, cw, cw)


def adamw(w, g, m, v, *, name, tr=None):
    r = w.shape[0]
    rest = w.shape[1:]
    if tr is None:
        tr = r
        for cand in (256, 128, 64, 32, 16, 8):
            if r % cand == 0:
                tr = cand
                break

    def body(w_ref, g_ref, m_ref, v_ref, d_ref, nm_ref, nv_ref):
        gv = g_ref[...]
        mn = ADAM_B1 * m_ref[...] + (1.0 - ADAM_B1) * gv
        vn = ADAM_B2 * v_ref[...] + (1.0 - ADAM_B2) * (gv * gv)
        m_hat = mn / (1.0 - ADAM_B1 ** ADAM_STEP)
        v_hat = vn / (1.0 - ADAM_B2 ** ADAM_STEP)
        d_ref[...] = -ADAM_LR * (m_hat / (jnp.sqrt(v_hat) + ADAM_EPS) + ADAM_WD * w_ref[...])
        nm_ref[...] = mn
        nv_ref[...] = vn

    blk = pl.BlockSpec((tr,) + rest, lambda i: (i,) + (0,) * len(rest))
    return pl.pallas_call(body, grid=(r // tr,), in_specs=[blk] * 4, out_specs=[blk] * 3, out_shape=[SDS(w.shape, F32)] * 3,
                          name=name, compiler_params=_cp(("parallel",)))(w, g, m, v)


def adamw_rows_view(w, g_mine, g_full, m, v, c_arr, *, name, tc=256):
    r, _, c = w.shape
    per_half = c // 2 // tc

    def body(c_ref, w_ref, gm_ref, gf_ref, m_ref, v_ref, d_ref, nm_ref, nv_ref, go_ref):
        mine = (pl.program_id(0) >> (per_half.bit_length() - 1)) == c_ref[0]
        gv = jnp.where(mine, gm_ref[...], gf_ref[...])
        mn = ADAM_B1 * m_ref[:, 0, :] + (1.0 - ADAM_B1) * gv
        vn = ADAM_B2 * v_ref[:, 0, :] + (1.0 - ADAM_B2) * (gv * gv)
        m_hat = mn / (1.0 - ADAM_B1 ** ADAM_STEP)
        v_hat = vn / (1.0 - ADAM_B2 ** ADAM_STEP)
        d_ref[:, 0, :] = -ADAM_LR * (m_hat / (jnp.sqrt(v_hat) + ADAM_EPS) + ADAM_WD * w_ref[:, 0, :])
        nm_ref[:, 0, :] = mn
        nv_ref[:, 0, :] = vn
        go_ref[:, 0, :] = gv

    b3 = pl.BlockSpec((r, 1, tc), lambda i, c_ref: (0, 0, i))
    own = pl.BlockSpec((r, tc), lambda i, c_ref: (0, jnp.clip(i - c_ref[0] * per_half, 0, per_half - 1)))
    full = pl.BlockSpec((r, tc), lambda i, c_ref: (0, i))
    grid_spec = pltpu.PrefetchScalarGridSpec(num_scalar_prefetch=1, grid=(c // tc,), in_specs=[b3, own, full, b3, b3],
                                             out_specs=[b3] * 4)
    return pl.pallas_call(body, grid_spec=grid_spec, out_shape=[SDS(w.shape, F32)] * 4, name=name,
                          compiler_params=_cp(("parallel",)))(c_arr, w, g_mine, g_full, m, v)


ANY = pl.BlockSpec(memory_space=pl.ANY)
ICI_KINDS = ("x", "y", "xy")


def _coords():
    return lax.axis_index("x"), lax.axis_index("y"), lax.axis_index("c")


def _peer(kind, x, y, c):
    if kind == "c":
        return (x, y, 1 - c)
    if kind == "x":
        return (1 - x, y, c)
    if kind == "y":
        return (x, 1 - y, c)
    return (1 - x, 1 - y, c)


def _chip_of(p):
    return 2 * p[0] + p[1]


def _half(rows, which):
    h = rows // 2
    return pl.ds(pl.multiple_of(which * h, 16), h)


def _remote(src, dst, send_sem, recv_sem, to):
    return pltpu.make_async_remote_copy(src_ref=src, dst_ref=dst, send_sem=send_sem, recv_sem=recv_sem,
                                        device_id=to, device_id_type=MESH)


def allgather_balanced(shard, *, name):
    r, cols = shard.shape
    h, q = r // 2, r // 4

    def body(in_ref, out_ref, send_sems, recv_sems):
        x, y, c = _coords()
        me, sibling = (x, y, c), (x, y, 1 - c)
        nbr = ((1 - x, y, c), (x, 1 - y, c))
        chip = (2 * (1 - x) + y, 2 * x + (1 - y), 2 * (1 - x) + (1 - y))
        quarter = lambda core, i: pl.ds(pl.multiple_of(core * h + i * q, 16), q)
        sent = []

        def go(src, dst, slot, to):
            cp = _remote(src, dst, send_sems.at[slot], recv_sems.at[slot], to)
            cp.start()
            sent.append(cp)

        def landed(region, slot):
            _remote(region, region, send_sems.at[slot], recv_sems.at[slot], me).wait_recv()

        for i in range(2):
            for k in range(2):
                qi = k if i == 0 else 1 - k
                go(in_ref.at[quarter(c, qi)], out_ref.at[2 * x + y, quarter(c, qi)], 2 * k + qi, nbr[k])
        for k in range(2):
            piece = out_ref.at[chip[k], quarter(c, k)]
            landed(piece, 2 * k + k)
            go(piece, piece, 4 + k, nbr[1 - k])
            go(piece, piece, 6 + 2 * k + k, sibling)
        for k in range(2):
            piece = out_ref.at[chip[k], quarter(c, 1 - k)]
            landed(piece, 2 * k + 1 - k)
            go(piece, piece, 6 + 2 * k + 1 - k, sibling)
        for k in range(2):
            piece = out_ref.at[chip[2], quarter(c, k)]
            landed(piece, 4 + k)
            go(piece, piece, 10 + k, sibling)
        for k in range(2):
            for i in range(2):
                landed(out_ref.at[chip[k], quarter(1 - c, i)], 6 + 2 * k + i)
            landed(out_ref.at[chip[2], quarter(1 - c, k)], 10 + k)
        for cp in sent:
            cp.wait_send()

    return pl.pallas_call(
        body, in_specs=[ANY], out_specs=ANY, out_shape=SDS((4,) + shard.shape, shard.dtype),
        scratch_shapes=[pltpu.SemaphoreType.DMA((12,)), pltpu.SemaphoreType.DMA((12,))], name=name)(shard)


def _allgather_shapes(shards):
    return [SDS((4,) + a.shape, a.dtype) for a in shards]


def _allgather_sems(n):
    return [pltpu.SemaphoreType.DMA((n, 6)), pltpu.SemaphoreType.DMA((n, 6))]


def _allgather_rows(ref, is_halved, which):
    r = ref.shape[0]
    return _half(r, which) if is_halved else pl.ds(0, r)


def _allgather_first(ins, outs, send_sems, recv_sems, halved):
    x, y, c = _coords()
    my_chip = 2 * x + y
    cps = []
    for w in range(len(ins)):
        rows = _allgather_rows(ins[w], halved[w], c)
        for k, kind in enumerate(ICI_KINDS):
            cps.append(_remote(ins[w].at[rows], outs[w].at[my_chip, rows], send_sems.at[w, k], recv_sems.at[w, k],
                               _peer(kind, x, y, c)))
    return cps


def _allgather_start(ins, outs, send_sems, recv_sems, halved):
    for cp in _allgather_first(ins, outs, send_sems, recv_sems, halved):
        cp.start()


def _allgather_finish(ins, outs, send_sems, recv_sems, halved):
    x, y, c = _coords()
    me = (x, y, c)
    second = []
    for w in range(len(ins)):
        for k, kind in enumerate(ICI_KINDS):
            landed = outs[w].at[_chip_of(_peer(kind, x, y, c)), _allgather_rows(ins[w], halved[w], c)]
            _remote(landed, landed, send_sems.at[w, k], recv_sems.at[w, k], me).wait_recv()
            if halved[w]:
                cp = _remote(landed, landed, send_sems.at[w, 3 + k], recv_sems.at[w, 3 + k], _peer("c", x, y, c))
                cp.start()
                second.append(cp)
    for w in range(len(ins)):
        if halved[w]:
            for k, kind in enumerate(ICI_KINDS):
                other = outs[w].at[_chip_of(_peer(kind, x, y, c)), _allgather_rows(ins[w], True, 1 - c)]
                _remote(other, other, send_sems.at[w, 3 + k], recv_sems.at[w, 3 + k], me).wait_recv()
    for cp in _allgather_first(ins, outs, send_sems, recv_sems, halved) + second:
        cp.wait_send()


def _half_of(ref, by_cols, which):
    lead = (slice(None),) * (len(ref.shape) - 2)
    if by_cols:
        h = ref.shape[-1] // 2
        return ref.at[lead + (slice(None), pl.ds(pl.multiple_of(which * h, LANES), h))]
    return ref.at[lead + (_half(ref.shape[-2], which),)]


def _half_shape(shape, by_cols):
    return shape[:-1] + (shape[-1] // 2,) if by_cols else shape[:-2] + (shape[-2] // 2, shape[-1])


def grads_to_sibling(gs, by_cols, *, name):
    n = len(gs)

    def body(*refs):
        ins, outs = refs[:n], refs[n:2 * n]
        send_sems, recv_sems = refs[2 * n:]
        x, y, c = _coords()
        cps = []
        for w in range(n):
            cp = _remote(_half_of(ins[w], by_cols[w], 1 - c), outs[w], send_sems.at[w], recv_sems.at[w], _peer("c", x, y, c))
            cp.start()
            cps.append(cp)
        for cp in cps:
            cp.wait()

    return pl.pallas_call(
        body, in_specs=[ANY] * n, out_specs=[ANY] * n,
        out_shape=[SDS(_half_shape(a.shape, bc), a.dtype) for a, bc in zip(gs, by_cols)],
        scratch_shapes=[pltpu.SemaphoreType.DMA((n,)), pltpu.SemaphoreType.DMA((n,))], name=name)(*gs)


def _to_chips_shapes(ps):
    return [SDS((3,) + a.shape[1:], a.dtype) for a in ps]


def _to_chips_sems(n):
    return [pltpu.SemaphoreType.DMA((n, 3)), pltpu.SemaphoreType.DMA((n, 3))]


def _to_chips_copies(ins, outs, send_sems, recv_sems):
    x, y, c = _coords()
    cps = []
    for w in range(len(ins)):
        for k, kind in enumerate(ICI_KINDS):
            to = _peer(kind, x, y, c)
            cps.append(_remote(ins[w].at[_chip_of(to)], outs[w].at[k], send_sems.at[w, k], recv_sems.at[w, k], to))
    return cps


def _to_chips_start(ins, outs, send_sems, recv_sems):
    for cp in _to_chips_copies(ins, outs, send_sems, recv_sems):
        cp.start()


def _to_chips_finish(ins, outs, send_sems, recv_sems):
    for cp in _to_chips_copies(ins, outs, send_sems, recv_sems):
        cp.wait()


def _to_owners_shapes(ps):
    return [SDS((7, a.shape[1] // 2, a.shape[2]), a.dtype) for a in ps]


def _to_owners_sems(n):
    return [pltpu.SemaphoreType.DMA((n, 7)), pltpu.SemaphoreType.DMA((n, 7))]


def _to_owners_copies(ins, outs, send_sems, recv_sems):
    x, y, c = _coords()
    cps = []
    for w in range(len(ins)):
        rows = ins[w].shape[1]
        for k, kind in enumerate(ICI_KINDS):
            px, py, _ = _peer(kind, x, y, c)
            for h in range(2):
                cps.append(_remote(ins[w].at[2 * px + py, _half(rows, h)], outs[w].at[2 * k + c],
                                   send_sems.at[w, 2 * k + h], recv_sems.at[w, 2 * k + c], (px, py, h)))
        cps.append(_remote(ins[w].at[2 * x + y, _half(rows, 1 - c)], outs[w].at[6], send_sems.at[w, 6], recv_sems.at[w, 6],
                           _peer("c", x, y, c)))
    return cps


def _to_owners_start(ins, outs, send_sems, recv_sems):
    for cp in _to_owners_copies(ins, outs, send_sems, recv_sems):
        cp.start()


def _to_owners_finish(ins, outs, send_sems, recv_sems):
    for cp in _to_owners_copies(ins, outs, send_sems, recv_sems):
        cp.wait_send()
    for w in range(len(ins)):
        for slot in range(7):
            got = outs[w].at[slot]
            _remote(got, got, send_sems.at[w, slot], recv_sems.at[w, slot], _coords()).wait_recv()


EXCHANGES = {"to_chips": (_to_chips_shapes, _to_chips_sems, _to_chips_start, _to_chips_finish),
             "to_owners": (_to_owners_shapes, _to_owners_sems, _to_owners_start, _to_owners_finish)}


def halves_to_full(hs, by_cols, *, name):
    n = len(hs)

    def body(*refs):
        ins, outs = refs[:n], refs[n:2 * n]
        send_sems, recv_sems = refs[2 * n:]
        x, y, c = _coords()
        cps = []
        for w in range(n):
            cp = _remote(ins[w], _half_of(outs[w], by_cols[w], c), send_sems.at[w], recv_sems.at[w], _peer("c", x, y, c))
            cp.start()
            cps.append(cp)
        for cp in cps:
            cp.wait()

    return pl.pallas_call(
        body, in_specs=[ANY] * n, out_specs=[ANY] * n,
        out_shape=[SDS((a.shape[0], 2 * a.shape[1]) if bc else (2 * a.shape[0], a.shape[1]), a.dtype)
                   for a, bc in zip(hs, by_cols)],
        scratch_shapes=[pltpu.SemaphoreType.DMA((n,)), pltpu.SemaphoreType.DMA((n,))],
        name=name)(*hs)


def _row_tile(rows):
    for cand in (256, 192, 176, 128, 64, 32, 16):
        if rows % cand == 0:
            return cand
    return rows


def chip_sum(g, recv, c_arr, by_cols, *, name):
    _, r, cols = g.shape

    def body(c_ref, g_ref, r_ref, f_ref, b_ref):
        tot = g_ref[...] + r_ref[...]
        f_ref[...] = tot
        b_ref[...] = tot.astype(BF16)

    if by_cols:
        tc = 2 * LANES
        nblk = cols // 2 // tc
        shape = (4, r, cols // 2)
        blk = pl.BlockSpec((None, r, tc), lambda j, i, c_ref: (j, 0, i))
        mine = pl.BlockSpec((None, r, tc), lambda j, i, c_ref: (j, 0, c_ref[0] * nblk + i))
    else:
        tr = _row_tile(r // 2)
        nblk = r // 2 // tr
        shape = (4, r // 2, cols)
        blk = pl.BlockSpec((None, tr, cols), lambda j, i, c_ref: (j, i, 0))
        mine = pl.BlockSpec((None, tr, cols), lambda j, i, c_ref: (j, c_ref[0] * nblk + i, 0))
    grid_spec = pltpu.PrefetchScalarGridSpec(num_scalar_prefetch=1, grid=(4, nblk), in_specs=[mine, blk], out_specs=[blk, blk])
    return pl.pallas_call(body, grid_spec=grid_spec, out_shape=[SDS(shape, F32), SDS(shape, BF16)],
                          name=name, compiler_params=_cp(("parallel", "parallel")))(c_arr, g, recv)


def final_sum(pf, recv, chip_arr, *, name):
    _, h, cols = pf.shape
    tr = _row_tile(h)

    def body(chip_ref, p_ref, r_ref, o_ref):
        o_ref[...] = ((p_ref[...] + r_ref[0].astype(F32)) + r_ref[1].astype(F32)) + r_ref[2].astype(F32)

    grid_spec = pltpu.PrefetchScalarGridSpec(
        num_scalar_prefetch=1, grid=(h // tr,),
        in_specs=[pl.BlockSpec((None, tr, cols), lambda i, chip_ref: (chip_ref[0], i, 0)),
                  pl.BlockSpec((3, tr, cols), lambda i, chip_ref: (0, i, 0))],
        out_specs=pl.BlockSpec((tr, cols), lambda i, chip_ref: (i, 0)))
    return pl.pallas_call(body, grid_spec=grid_spec, out_shape=SDS((h, cols), F32), name=name,
                          compiler_params=_cp(("parallel",)))(chip_arr, pf, recv)


def owner_sum(g, recv, pos_arr, *, name):
    _, r, cols = g.shape
    h = r // 2
    tr = _row_tile(h)
    nblk = h // tr

    def body(pos_ref, g_ref, r_ref, o_ref):
        tot = g_ref[...]
        for slot in range(7):
            tot = tot + r_ref[slot].astype(F32)
        o_ref[...] = tot

    grid_spec = pltpu.PrefetchScalarGridSpec(
        num_scalar_prefetch=1, grid=(nblk,),
        in_specs=[pl.BlockSpec((None, tr, cols), lambda i, pos: (pos[0], pos[1] * nblk + i, 0)),
                  pl.BlockSpec((7, tr, cols), lambda i, pos: (0, i, 0))],
        out_specs=pl.BlockSpec((tr, cols), lambda i, pos: (i, 0)))
    return pl.pallas_call(body, grid_spec=grid_spec, out_shape=SDS((h, cols), F32), name=name,
                          compiler_params=_cp(("parallel",)))(pos_arr, g, recv)


def allreduce_small(v, *, name):
    rws, cols = v.shape

    def body(v_ref, all_ref, sum_ref, send_sems, recv_sems, local_sem):
        x, y, c = _coords()
        me, sibling = (x, y, c), (x, y, 1 - c)
        chips = [(1 - x, y), (x, 1 - y), (1 - x, 1 - y)]

        def rows(px, py, pc):
            return all_ref.at[pl.ds(pl.multiple_of((4 * px + 2 * py + pc) * rws, 8), rws), :]

        def copy(k, block, to, src=None):
            return _remote(rows(*block) if src is None else src, rows(*block), send_sems.at[k], recv_sems.at[k], to)

        mine = pltpu.make_async_copy(v_ref, rows(*me), local_sem)
        mine.start()
        first = [copy(0, me, sibling, src=v_ref)]
        first += [copy(1 + j, me, (*chip, c), src=v_ref) for j, chip in enumerate(chips)]
        for cp in first:
            cp.start()
        passed = [copy(4 + j, (*chip, c), sibling) for j, chip in enumerate(chips)]
        for j, chip in enumerate(chips):
            copy(1 + j, (*chip, c), me).wait_recv()
            passed[j].start()
        copy(0, sibling, me).wait_recv()
        for j, chip in enumerate(chips):
            copy(4 + j, (*chip, 1 - c), me).wait_recv()
        for cp in first + passed:
            cp.wait_send()
        mine.wait()
        tot = all_ref[0:rws, :]
        for dev in range(1, 8):
            tot = tot + all_ref[dev * rws:(dev + 1) * rws, :]
        sum_ref[...] = tot

    vm = pl.BlockSpec(memory_space=pltpu.VMEM)
    return pl.pallas_call(
        body, in_specs=[vm], out_specs=[vm, vm],
        out_shape=[SDS((8 * rws, cols), v.dtype), SDS((rws, cols), v.dtype)],
        scratch_shapes=[pltpu.SemaphoreType.DMA((7,)), pltpu.SemaphoreType.DMA((7,)), pltpu.SemaphoreType.DMA],
        name=name)(v)[1]


def _pack_rows(parts, rows):
    out = []
    for a, r in zip(parts, rows):
        flat = a.reshape(-1)
        flat = jnp.pad(flat, (0, r * LANES - flat.shape[0]))
        out.append(flat.reshape(r, LANES))
    return jnp.concatenate(out, axis=0)


def _unpack_rows(packed, shapes, rows):
    out, at = [], 0
    for shp, r in zip(shapes, rows):
        size = int(np.prod(shp))
        out.append(packed[at:at + r].reshape(-1)[:size].reshape(shp))
        at += r
    return out


def kernel(x, g_pre_mix, w_in, b_forget, w_o_fox, w_o_dil, w_out, g_post_mix, g_pre_ffn, w_up, conv_w, conv_b, w_down, g_post_ffn, loss_target, m_g_pre_mix, m_w_in, m_b_forget, m_w_o_fox, m_w_o_dil, m_w_out, m_g_post_mix, m_g_pre_ffn, m_w_up, m_conv_w, m_conv_b, m_w_down, m_g_post_ffn, v_g_pre_mix, v_w_in, v_b_forget, v_w_o_fox, v_w_o_dil, v_w_out, v_g_post_mix, v_g_pre_ffn, v_w_up, v_conv_w, v_conv_b, v_w_down, v_g_post_ffn):
    xi, yi, ci = _coords()
    chip = 2 * xi + yi
    c_arr = jnp.reshape(ci, (1,)).astype(jnp.int32)
    chip_arr = jnp.reshape(chip, (1,)).astype(jnp.int32)
    xs = x[0]
    target = loss_target[0]
    s, d = xs.shape
    f_half = w_down.shape[1] * 4
    cols_in = w_in.shape[2]

    big = (w_in, w_o_fox, w_o_dil, w_out, w_up, w_down)
    shards = [w[0].astype(BF16) for w in big]
    a_in = allgather_balanced(shards[0], name="allgather_w_in")
    w_in_full = jnp.concatenate([jnp.where(chip == j, shards[0], a_in[j]) for j in range(4)], axis=1)
    nf = N_HEADS
    e_a, e_b = 3 * ATT_W, 3 * ATT_W + nf
    wz = jnp.concatenate([w_in_full[:, :e_a], w_in_full[:, e_b:]], axis=1)
    wf = jnp.pad(w_in_full[:, e_a:e_b], ((0, 0), (0, LANES - nf)))
    cb = conv_b
    bfo = jnp.pad(b_forget, ((0, 0), (0, LANES - nf)))

    h1 = rmsnorm_fwd(xs, g_pre_mix)
    z = mm([(h1, d, 0)], [(wz, d, 0)], nt=False, out_dtype=BF16, tm=1024, tn=512, name="in_proj")
    fa = mm([(h1, d, 0)], [(wf, d, 0)], nt=False, out_dtype=F32, tm=1024, tn=LANES, name="in_proj_forget")
    q_aug, k_aug, v_aug = fox_prep(z, fa, bfo)
    later = shards[1:] + [conv_w[0]]
    ya, lse_a, *late = fox_fwd(q_aug, k_aug, v_aug, gather=later, halved=[True] * 5 + [False], hps=N_HEADS)
    a_of, a_od, a_out, a_up, a_down, a_cw = [
        lax.dynamic_update_index_in_dim(a4, own, chip, 0) for a4, own in zip(late, later)]
    cw = jnp.concatenate([a_cw[j] for j in range(4)], axis=1)
    wo_a = jnp.concatenate([a_of[j] for j in range(4)], axis=1)
    wo_b = jnp.concatenate([a_od[j] for j in range(4)], axis=1)
    w_o = a_out.reshape(d, d)
    w_dn = a_down.reshape(f_half, d)
    wu_a = jnp.concatenate([a_up[0], a_up[1]], axis=1)
    wu_b = jnp.concatenate([a_up[2], a_up[3]], axis=1)
    cos_t, sin_t = rope_cos_sin(s)
    yb, lse_b = dil_fwd_all(z, cos_t, sin_t)
    pa, pb, mixed = gate_mix(ya, yb, wo_a, wo_b, z)
    y1, x1, h2 = proj_norm_res(mixed, w_o, g_post_mix, xs, g_pre_ffn, name="out_proj")
    ua, ub, conv_a, conv_bh, mid = ffn_up(h2, wu_a, wu_b, cw, cb)
    dout, dy2, gg_post_ffn, sq = proj_norm_loss(mid, w_dn, g_post_ffn, x1, target, name="down_proj")
    loss = lax.psum(0.5 * sq[0, 0] / d, ("x", "y", "c"))

    dmid = mm([(dy2, d, 0)], [(w_dn, d, 0)], nt=True, out_dtype=BF16, tm=512, tn=f_half // 2, name="down_dgrad")
    dw_down, dw_down16 = wgrad((mid, f_half, 0), dy2, tk=f_half // 2, tn=1024, ts=1024, name="down_wgrad", bf16_copy=True)
    dua, dub, gc_a, gc_b = ffn_bwd(dmid, ua, ub, conv_a, conv_bh, cw)
    dx1, dy1, gg_pre_ffn, gg_post_mix = mm_norm_bwd(
        [(dua, f_half, 0), (dub, f_half, 0)], [(wu_a, f_half, 0), (wu_b, f_half, 0)],
        [(x1, g_pre_ffn, dout, F32), (y1, g_post_mix, None, BF16)], name="up_dgrad")
    dw_up = None
    for k, du in enumerate((dua, dub)):
        dw_up = wgrad((h2, d, 0), du, tk=1024, tn=f_half // 2, ts=1024, name=f"up_wgrad_{k}", chip_major=True,
                      slabs=(4, 2 * k), into=dw_up, bf16_copy=True)
    g_ffn = [(dw_up[0], dw_up[1]), (dw_down.reshape(4, f_half // 4, d), dw_down16.reshape(4, f_half // 4, d))]
    dw_out, dw_out16 = wgrad((mixed, d, 0), dy1, tk=1024, tn=1024, ts=1024, name="out_wgrad", bf16_copy=True)
    dpa, dpb, dz_g, dya, dyb, dd_a = mix_bwd(dy1, w_o, z, pa, pb, wo_a, wo_b, ya)
    by_chip_cols = lambda a: jnp.stack([a[:, j * (d // 4):(j + 1) * (d // 4)] for j in range(4)], axis=0)
    dw_of = [by_chip_cols(a) for a in wgrad((ya, ATT_W, 0), dpa, tk=ATT_W, tn=d, ts=1024, name="fox_o_wgrad", bf16_copy=True)]
    dw_od = [by_chip_cols(a) for a in wgrad((yb, ATT_W, 0), dpb, tk=ATT_W, tn=d, ts=1024, name="dil_o_wgrad", bf16_copy=True)]
    g_mix = [dw_of, dw_od, (dw_out.reshape(4, d // 4, d), dw_out16.reshape(4, d // 4, d))]
    dq_aug, dk_aug, dv_a, *got_ffn = fox_bwd(q_aug, k_aug, z, dya, lse_a, dd_a, exchange=[g[1] for g in g_ffn], kind="to_owners")
    dz_a, dfa, gg_bf = fox_post(dq_aug, dk_aug, dv_a, fa, bfo)
    *dz_b, got_of, got_od, got_out = dil_bwd_all(z, cos_t, sin_t, dyb, lse_b, yb, exchange=[g[1] for g in g_mix],
                                                 kind="to_owners")
    got_mix = [got_of, got_od, got_out]
    dwt_a = wgrad((dz_a, e_a, 0), h1, tk=e_a // 2, tn=d, ts=1024, name="in_wgrad_a")
    dwt_b = [wgrad((part, ATT_W, 0), h1, tk=ATT_W, tn=d, ts=1024, name=f"in_wgrad_b{k}") for k, part in enumerate(dz_b)]
    dwt_g = wgrad((dz_g, 2 * d, 0), h1, tk=d, tn=d, ts=1024, name="in_wgrad_g")
    dwt_f = wgrad((dfa, LANES, 0), h1, tk=LANES, tn=d, ts=1024, name="in_wgrad_f")
    dwt_full = jnp.concatenate([dwt_a, dwt_f[:nf], *dwt_b, dwt_g], axis=0)
    dw_in = jnp.stack([dwt_full[j * cols_in:(j + 1) * cols_in] for j in range(4)], axis=0)
    from_sib = grads_to_sibling([dw_in], [True], name="grads_to_sibling_in")
    sum_in = chip_sum(dw_in, from_sib[0], c_arr, True, name="chip_sum_w_in")
    grad_x, gg_pre_mix, got_in = mm_norm_bwd(
        [(dz_a, e_a, 0), *[(part, ATT_W, 0) for part in dz_b], (dz_g, d, 0), (dz_g, d, 1), (dfa, LANES, 0)],
        [(wz, e_a, 0), *[(wz, ATT_W, Z_QB + k) for k in range(3)], (wz, d, 3), (wz, d, 4), (wf, LANES, 0)],
        [(xs, g_pre_mix, dx1, F32)], exchange=[sum_in[1]], name="in_dgrad")

    names = ("w_in", "w_o_fox", "w_o_dil", "w_out", "w_up", "w_down")
    pos_arr = jnp.concatenate([chip_arr, c_arr])
    halves = [final_sum(sum_in[0], got_in, chip_arr, name="final_sum_w_in")] + [
        owner_sum(g[0], got, pos_arr, name=f"owner_sum_{nm}") for g, got, nm in zip(g_mix + g_ffn, got_mix + got_ffn, names[1:])]
    from_half = halves_to_full(halves, [True] + [False] * 5, name="halves_to_full")
    g_big = [None] + [lax.dynamic_update_slice_in_dim(full, mine, ci * mine.shape[0], axis=0)
                      for full, mine in zip(from_half[1:], halves[1:])]
    upd_big = [adamw(w[0], g, m[0], v[0], name=f"adamw_{nm}") for w, g, m, v, nm in list(zip(
        big, g_big, (m_w_in, m_w_o_fox, m_w_o_dil, m_w_out, m_w_up, m_w_down),
        (v_w_in, v_w_o_fox, v_w_o_dil, v_w_out, v_w_up, v_w_down), names))[1:]]
    to_t = lambda a: jnp.transpose(a, (2, 0, 1))
    from_t = lambda a: jnp.transpose(a, (1, 2, 0))
    *upd_in, g_in_t = adamw_rows_view(to_t(w_in), halves[0], from_half[0], to_t(m_w_in), to_t(v_w_in), c_arr,
                                      name="adamw_w_in")

    g_cw_loc = jnp.concatenate([gc_a[0:3], gc_b[0:3]], axis=1)
    g_cb_loc = jnp.concatenate([gc_a[3:4], gc_b[3:4]], axis=1)
    small_loc = [gg_pre_mix, gg_post_mix, gg_pre_ffn, gg_post_ffn, g_cb_loc, gg_bf[:, :nf], g_cw_loc]
    red_rows = (8, 8, 8, 8, 48, 8, 136)
    red = allreduce_small(_pack_rows(small_loc, red_rows), name="allreduce_small")
    g_pm, g_qm, g_pf, g_qf, g_cb, g_bf, g_cw_full = _unpack_rows(red, [a.shape for a in small_loc], red_rows)
    cols_cw = conv_w.shape[2]
    g_cw = lax.dynamic_slice_in_dim(g_cw_full, chip * cols_cw, cols_cw, axis=1)
    small_w = (g_pre_mix, g_post_mix, g_pre_ffn, g_post_ffn, conv_b, b_forget, conv_w[0])
    small_m = (m_g_pre_mix, m_g_post_mix, m_g_pre_ffn, m_g_post_ffn, m_conv_b, m_b_forget, m_conv_w[0])
    small_v = (v_g_pre_mix, v_g_post_mix, v_g_pre_ffn, v_g_post_ffn, v_conv_b, v_b_forget, v_conv_w[0])
    small_g = (g_pm, g_qm, g_pf, g_qf, g_cb, g_bf, g_cw)
    small_names = ("g_pre_mix", "g_post_mix", "g_pre_ffn", "g_post_ffn", "conv_b", "b_forget", "conv_w")
    per_param = [adamw(w, g, m, v, name=f"adamw_{nm}") for w, g, m, v, nm in zip(small_w, small_g, small_m, small_v, small_names)]
    upd_small = [[u[j] for u in per_param] for j in range(3)]

    order = ("g_pre_mix", "w_in", "b_forget", "w_o_fox", "w_o_dil", "w_out", "g_post_mix", "g_pre_ffn", "w_up", "conv_w",
             "conv_b", "w_down", "g_post_ffn")
    grads, deltas, new_ms, new_vs = {}, {}, {}, {}
    grads["w_in"] = from_t(g_in_t)
    deltas["w_in"], new_ms["w_in"], new_vs["w_in"] = (from_t(a) for a in upd_in)
    for k, nm in enumerate(names[1:]):
        grads[nm] = g_big[k + 1][None]
        deltas[nm], new_ms[nm], new_vs[nm] = (a[None] for a in upd_big[k])
    for k, nm in enumerate(small_names):
        lead = (lambda a: a[None]) if nm == "conv_w" else (lambda a: a)
        grads[nm] = lead(small_g[k])
        deltas[nm], new_ms[nm], new_vs[nm] = (lead(upd_small[j][k]) for j in range(3))
    return (loss, grad_x[None], *[grads[nm] for nm in order], *[deltas[nm] for nm in order],
            *[new_ms[nm] for nm in order], *[new_vs[nm] for nm in order])
```

```python
import functools
import math

import numpy as np
import jax
import jax.numpy as jnp
from jax import lax
from jax.experimental import pallas as pl
from jax.experimental.pallas import tpu as pltpu

F32 = jnp.float32
BF16 = jnp.bfloat16
SDS = jax.ShapeDtypeStruct
MESH = pl.DeviceIdType.MESH

HEAD_DIM = 64
N_HEADS = 8
LANES = 128
ATT_W = N_HEADS * HEAD_DIM
DIL_PATTERNS = ((128, 1), (512, 4), (2048, 16))
DIL_BLK = 128
ROPE_DIM = HEAD_DIM // 4
ROPE_THETA = 500000.0
RMS_EPS = 1e-6
NEG = -1e30
QK_SCALE = 1.0 / math.sqrt(HEAD_DIM)
ADAM_LR, ADAM_B1, ADAM_B2, ADAM_EPS, ADAM_WD, ADAM_STEP = 0.001, 0.9, 0.999, 1e-08, 0.01, 10
VMEM_LIMIT = 56 * 1024 * 1024

Z_QA, Z_KA, Z_VA, Z_QB, Z_KB, Z_VB = 0, 1, 2, 3, 4, 5
Z_W = 5120


def _cp(sem):
    return pltpu.CompilerParams(dimension_semantics=sem, vmem_limit_bytes=VMEM_LIMIT)


def _nt(a, b):
    return lax.dot_general(a, b, (((1,), (1,)), ((), ())), preferred_element_type=F32)


def _tn(a, b):
    return lax.dot_general(a, b, (((0,), (0,)), ((), ())), preferred_element_type=F32)


def _nn(a, b):
    return jnp.dot(a, b, preferred_element_type=F32)


def _lane(shape):
    return lax.broadcasted_iota(jnp.int32, shape, 1)


def _row(shape):
    return lax.broadcasted_iota(jnp.int32, shape, 0)


def rmsnorm_fwd(x, g, *, tm=512):
    s, d = x.shape

    def body(x_ref, g_ref, h_ref):
        xv = x_ref[...]
        inv = lax.rsqrt(jnp.mean(xv * xv, axis=-1, keepdims=True) + RMS_EPS)
        h_ref[...] = (xv * inv * g_ref[...]).astype(h_ref.dtype)

    return pl.pallas_call(
        body, grid=(s // tm,),
        in_specs=[pl.BlockSpec((tm, d), lambda i: (i, 0)), pl.BlockSpec((1, d), lambda i: (0, 0))],
        out_specs=pl.BlockSpec((tm, d), lambda i: (i, 0)),
        out_shape=SDS((s, d), BF16), name="rmsnorm_fwd", compiler_params=_cp(("parallel",)))(x, g)


def mm(a_views, b_views, *, nt, out_dtype, tm, tn, name):
    n_p = len(a_views)
    m = a_views[0][0].shape[0]
    n = b_views[0][0].shape[0] if nt else b_views[0][0].shape[1]

    def body(*refs):
        o_ref = refs[-1]
        acc = None
        for p in range(n_p):
            av = refs[p][...].astype(BF16)
            bv = refs[n_p + p][...].astype(BF16)
            dv = _nt(av, bv) if nt else _nn(av, bv)
            acc = dv if acc is None else acc + dv
        o_ref[...] = acc.astype(o_ref.dtype)

    in_specs = []
    for arr, w, blk in a_views:
        in_specs.append(pl.BlockSpec((tm, w), functools.partial(lambda i, j, blk: (i, blk), blk=blk)))
    for arr, w, blk in b_views:
        if nt:
            in_specs.append(pl.BlockSpec((tn, w), functools.partial(lambda i, j, blk: (j, blk), blk=blk)))
        else:
            in_specs.append(pl.BlockSpec((w, tn), lambda i, j: (0, j)))
    return pl.pallas_call(
        body, grid=(m // tm, n // tn), in_specs=in_specs,
        out_specs=pl.BlockSpec((tm, tn), lambda i, j: (i, j)),
        out_shape=SDS((m, n), out_dtype), name=name,
        compiler_params=_cp(("parallel", "parallel")))(*[a[0] for a in a_views], *[b[0] for b in b_views])


def wgrad(a_view, g, *, tk, tn, ts, name, chip_major=False, slabs=None, into=None, bf16_copy=False):
    arr, ka, blk = a_view
    s, n = g.shape
    ns = s // ts
    total, first = slabs if slabs else (n // tn, 0)
    n_into = 0 if into is None else (2 if bf16_copy else 1)

    def body(a_ref, g_ref, *rest):
        o_ref = rest[n_into]

        @pl.when(pl.program_id(2) == 0)
        def _():
            o_ref[...] = jnp.zeros_like(o_ref)

        o_ref[...] += _tn(a_ref[...].astype(BF16), g_ref[...].astype(BF16))
        if bf16_copy:
            @pl.when(pl.program_id(2) == ns - 1)
            def _():
                rest[n_into + 1][...] = o_ref[...].astype(BF16)

    if chip_major:
        out_spec = pl.BlockSpec((None, tk, tn), lambda i, j, k: (first + j, i, 0))
        shape = (total, ka, tn)
    else:
        out_spec = pl.BlockSpec((tk, tn), lambda i, j, k: (i, j))
        shape = (ka, n)
    in_specs = [pl.BlockSpec((ts, tk), lambda i, j, k: (k, blk * (ka // tk) + i)),
                pl.BlockSpec((ts, tn), lambda i, j, k: (k, j))]
    args = [arr, g]
    if into is not None:
        earlier = list(into) if bf16_copy else [into]
        in_specs += [pl.BlockSpec(memory_space=pl.ANY)] * len(earlier)
        args += earlier
    out = pl.pallas_call(
        body, grid=(ka // tk, n // tn, ns), in_specs=in_specs,
        out_specs=[out_spec, out_spec] if bf16_copy else out_spec,
        out_shape=[SDS(shape, F32), SDS(shape, BF16)] if bf16_copy else SDS(shape, F32), name=name,
        input_output_aliases={2 + k: k for k in range(n_into)},
        compiler_params=_cp(("parallel", "parallel", "arbitrary")))(*args)
    return out


def _norm_bwd_rows(dh, xh, inv, g):
    dxh = dh * g
    dx = inv * (dxh - xh * jnp.mean(dxh * xh, axis=-1, keepdims=True))
    return dx, jnp.sum((dh * xh).reshape(dh.shape[0] // 8, 8, dh.shape[1]), axis=0)


def proj_norm_res(a, w, g, xres, g_next, *, tm=512, name):
    s, k = a.shape
    d = w.shape[1]

    def body(a_ref, w_ref, g_ref, x_ref, gn_ref, y_ref, o_ref, h_ref):
        y = _nn(a_ref[...], w_ref[...])
        inv = lax.rsqrt(jnp.mean(y * y, axis=-1, keepdims=True) + RMS_EPS)
        xn = x_ref[...] + y * inv * g_ref[...]
        y_ref[...] = y
        o_ref[...] = xn
        inv_n = lax.rsqrt(jnp.mean(xn * xn, axis=-1, keepdims=True) + RMS_EPS)
        h_ref[...] = (xn * inv_n * gn_ref[...]).astype(h_ref.dtype)

    row = pl.BlockSpec((tm, d), lambda i: (i, 0))
    vec = pl.BlockSpec((1, d), lambda i: (0, 0))
    return pl.pallas_call(
        body, grid=(s // tm,),
        in_specs=[pl.BlockSpec((tm, k), lambda i: (i, 0)), pl.BlockSpec((k, d), lambda i: (0, 0)), vec, row, vec],
        out_specs=[row, row, row], out_shape=[SDS((s, d), F32), SDS((s, d), F32), SDS((s, d), BF16)], name=name,
        compiler_params=_cp(("parallel",)))(a, w, g, xres, g_next)


def proj_norm_loss(a, w, g, xres, target, *, tm=512, name):
    s, k = a.shape
    d = w.shape[1]
    n = s // tm

    def body(a_ref, w_ref, g_ref, x_ref, t_ref, do_ref, dy_ref, dg_ref, l_ref, acc):
        i = pl.program_id(0)

        @pl.when(i == 0)
        def _():
            acc[...] = jnp.zeros_like(acc)
            l_ref[...] = jnp.zeros_like(l_ref)

        y = _nn(a_ref[...], w_ref[...])
        inv = lax.rsqrt(jnp.mean(y * y, axis=-1, keepdims=True) + RMS_EPS)
        yh = y * inv
        err = x_ref[...] + yh * g_ref[...] - t_ref[...]
        dout = err * (1.0 / d)
        do_ref[...] = dout
        l_ref[...] += jnp.sum(jnp.sum(err * err, axis=1, keepdims=True), axis=0, keepdims=True)
        dy, part = _norm_bwd_rows(dout, yh, inv, g_ref[...])
        dy_ref[...] = dy.astype(dy_ref.dtype)
        acc[...] += part

        @pl.when(i == n - 1)
        def _():
            dg_ref[...] = jnp.sum(acc[...], axis=0, keepdims=True)

    row = pl.BlockSpec((tm, d), lambda i: (i, 0))
    vec = pl.BlockSpec((1, d), lambda i: (0, 0))
    return pl.pallas_call(
        body, grid=(n,),
        in_specs=[pl.BlockSpec((tm, k), lambda i: (i, 0)), pl.BlockSpec((k, d), lambda i: (0, 0)), vec, row, row],
        out_specs=[row, row, vec, pl.BlockSpec((1, 1), lambda i: (0, 0))],
        out_shape=[SDS((s, d), F32), SDS((s, d), BF16), SDS((1, d), F32), SDS((1, 1), F32)],
        scratch_shapes=[pltpu.VMEM((8, d), F32)], name=name, compiler_params=_cp(("arbitrary",)))(a, w, g, xres, target)


def mm_norm_bwd(a_views, b_views, stages, exchange=(), *, tm=256, name):
    n_p, n_s, ne = len(a_views), len(stages), len(exchange)
    s = a_views[0][0].shape[0]
    d = b_views[0][0].shape[0]
    n = s // tm
    has_res = [st[2] is not None for st in stages]

    def body(*refs):
        a_refs, b_refs = refs[:n_p], refs[n_p:2 * n_p]
        at = 2 * n_p
        st_refs = []
        for k in range(n_s):
            cnt = 3 if has_res[k] else 2
            st_refs.append(refs[at:at + cnt])
            at += cnt
        e_ins = refs[at:at + ne]
        at += ne
        dx_refs, dg_refs = refs[at:at + n_s], refs[at + n_s:at + 2 * n_s]
        at += 2 * n_s
        e_outs = refs[at:at + ne]
        at += ne
        accs = refs[at:at + n_s]
        comm = (e_ins, e_outs) + tuple(refs[at + n_s:])
        i = pl.program_id(0)

        @pl.when(i == 0)
        def _():
            for acc in accs:
                acc[...] = jnp.zeros_like(acc)
            if ne:
                _to_chips_start(*comm)

        dh = None
        for p in range(n_p):
            part = _nt(a_refs[p][...].astype(BF16), b_refs[p][...].astype(BF16))
            dh = part if dh is None else dh + part
        for k in range(n_s):
            xv = st_refs[k][0][...]
            inv = lax.rsqrt(jnp.mean(xv * xv, axis=-1, keepdims=True) + RMS_EPS)
            dx, part = _norm_bwd_rows(dh, xv * inv, inv, st_refs[k][1][...])
            if has_res[k]:
                dx = dx + st_refs[k][2][...]
            dx_refs[k][...] = dx.astype(dx_refs[k].dtype)
            accs[k][...] += part
            dh = dx

        @pl.when(i == n - 1)
        def _():
            for k in range(n_s):
                dg_refs[k][...] = jnp.sum(accs[k][...], axis=0, keepdims=True)
            if ne:
                _to_chips_finish(*comm)

    row = pl.BlockSpec((tm, d), lambda i: (i, 0))
    vec = pl.BlockSpec((1, d), lambda i: (0, 0))
    in_specs, args = [], []
    for arr, w, blk in a_views:
        in_specs.append(pl.BlockSpec((tm, w), functools.partial(lambda i, blk: (i, blk), blk=blk)))
        args.append(arr)
    for arr, w, blk in b_views:
        in_specs.append(pl.BlockSpec((d, w), functools.partial(lambda i, blk: (0, blk), blk=blk)))
        args.append(arr)
    for x, g, res, _ in stages:
        in_specs += [row, vec] + ([row] if res is not None else [])
        args += [x, g] + ([res] if res is not None else [])
    return pl.pallas_call(
        body, grid=(n,), in_specs=in_specs + [ANY] * ne,
        out_specs=[row] * n_s + [vec] * n_s + [ANY] * ne,
        out_shape=[SDS((s, d), st[3]) for st in stages] + [SDS((1, d), F32)] * n_s + _to_chips_shapes(exchange),
        scratch_shapes=[pltpu.VMEM((8, d), F32)] * n_s + (_to_chips_sems(ne) if ne else []), name=name,
        compiler_params=_cp(("arbitrary",)))(*args, *exchange)


def _split3(v):
    hi = v.astype(BF16).astype(F32)
    r = v - hi
    mid = r.astype(BF16).astype(F32)
    lo = (r - mid).astype(BF16).astype(F32)
    return hi, mid, lo


def _tri(n, upper):
    r = np.arange(n)
    m = (r[:, None] <= r[None, :]) if upper else (r[:, None] >= r[None, :])
    return jnp.asarray(m.astype(np.float32))


def fox_prep(z, fa, bfo, *, tb=1024):
    s = z.shape[0]
    n = s // tb

    def body(q_ref, k_ref, v_ref, fa_ref, b_ref, tri_ref, qa_ref, ka_ref, va_ref, carry):
        @pl.when(pl.program_id(0) == 0)
        def _():
            carry[...] = jnp.zeros_like(carry)

        xv = fa_ref[...] + b_ref[...]
        logf = jnp.minimum(xv, 0.0) - jnp.log(1.0 + jnp.exp(-jnp.abs(xv)))
        csum = jnp.dot(tri_ref[...], logf, preferred_element_type=F32, precision=lax.Precision.HIGHEST) + carry[0:1, :]
        carry[0:1, :] = csum[tb - 1:tb, :]
        lane = _lane((tb, LANES))
        for h in range(N_HEADS):
            hi, mid, lo = _split3(csum[:, h:h + 1])
            pair = (h // 2) * LANES
            qv = q_ref[:, pair:pair + LANES].astype(F32)
            kv = k_ref[:, pair:pair + LANES].astype(F32)
            vv = v_ref[:, pair:pair + LANES].astype(F32)
            if h % 2:
                qv = pltpu.roll(qv, 64, axis=1)
                kv = pltpu.roll(kv, 64, axis=1)
                vv = pltpu.roll(vv, 64, axis=1)
            va_ref[:, h * LANES:(h + 1) * LANES] = jnp.where(lane < 64, vv, jnp.where(lane == 64, 1.0, 0.0)).astype(BF16)
            one = jnp.where((lane >= 67) & (lane < 70), 1.0, 0.0)
            q_x = jnp.where(lane == 64, hi, jnp.where(lane == 65, mid, jnp.where(lane == 66, lo, one)))
            one = jnp.where((lane >= 64) & (lane < 67), 1.0, 0.0)
            k_x = jnp.where(lane == 67, -hi, jnp.where(lane == 68, -mid, jnp.where(lane == 69, -lo, one)))
            qa_ref[:, h * LANES:(h + 1) * LANES] = jnp.where(lane < 64, qv * QK_SCALE, q_x).astype(BF16)
            ka_ref[:, h * LANES:(h + 1) * LANES] = jnp.where(lane < 64, kv, k_x).astype(BF16)

    return pl.pallas_call(
        body, grid=(n,),
        in_specs=[pl.BlockSpec((tb, ATT_W), lambda i: (i, Z_QA)), pl.BlockSpec((tb, ATT_W), lambda i: (i, Z_KA)),
                  pl.BlockSpec((tb, ATT_W), lambda i: (i, Z_VA)),
                  pl.BlockSpec((tb, LANES), lambda i: (i, 0)), pl.BlockSpec((1, LANES), lambda i: (0, 0)),
                  pl.BlockSpec((tb, tb), lambda i: (0, 0))],
        out_specs=[pl.BlockSpec((tb, N_HEADS * LANES), lambda i: (i, 0))] * 3,
        out_shape=[SDS((s, N_HEADS * LANES), BF16)] * 3,
        scratch_shapes=[pltpu.VMEM((8, LANES), F32)],
        name="fox_prep", compiler_params=_cp(("arbitrary",)))(z, z, z, fa, bfo, _tri(tb, False))


def _causal_pairs(n, k_major):
    if k_major:
        pairs = [(qi, kj) for kj in range(n) for qi in range(kj, n)]
    else:
        pairs = [(qi, kj) for qi in range(n) for kj in range(qi + 1)]
    return (jnp.asarray([p[0] for p in pairs], jnp.int32), jnp.asarray([p[1] for p in pairs], jnp.int32), len(pairs))


def fox_fwd(q_aug, k_aug, v_aug, gather=(), halved=(), *, t=512, hps=4):
    s = v_aug.shape[0]
    qi_arr, kj_arr, n_pairs = _causal_pairs(s // t, False)
    ng = len(gather)
    n_groups = N_HEADS // hps

    def body(qi_ref, kj_ref, q_ref, k_ref, v_ref, *rest):
        g_ins, (o_ref, lse_ref), g_outs = rest[:ng], rest[ng:ng + 2], rest[ng + 2:2 * ng + 2]
        m_scr, acc_scr = rest[2 * ng + 2:2 * ng + 4]
        comm = (g_ins, g_outs) + tuple(rest[2 * ng + 4:]) + (list(halved),)
        step = pl.program_id(1)
        qi = qi_ref[step]
        kj = kj_ref[step]
        if ng:
            @pl.when((pl.program_id(0) == 0) & (step == 0))
            def _():
                _allgather_start(*comm)

        @pl.when(kj == 0)
        def _():
            m_scr[...] = jnp.full_like(m_scr, NEG)
            acc_scr[...] = jnp.zeros_like(acc_scr)

        def update(masked):
            for i in range(hps):
                sc = _nt(q_ref[:, i * LANES:(i + 1) * LANES], k_ref[:, i * LANES:(i + 1) * LANES])
                if masked:
                    sc = jnp.where(_row((t, t)) >= _lane((t, t)), sc, NEG)
                m_prev = m_scr[i]
                m_new = jnp.maximum(m_prev, jnp.max(sc, axis=-1, keepdims=True))
                p = jnp.exp((sc - jnp.tile(m_new, (1, t // LANES))).astype(BF16))
                acc_scr[i] = jnp.exp(m_prev - m_new) * acc_scr[i] + _nn(p, v_ref[:, i * LANES:(i + 1) * LANES])
                m_scr[i] = m_new

        @pl.when(kj < qi)
        def _():
            update(False)

        @pl.when(kj == qi)
        def _():
            update(True)
            lane = _lane((t, LANES))
            for pr in range(hps // 2):
                den = [acc_scr[2 * pr + i][:, 64:65] for i in range(2)]
                o_ref[:, pr * LANES:(pr + 1) * LANES] = jnp.where(
                    lane < 64, acc_scr[2 * pr] / den[0], pltpu.roll(acc_scr[2 * pr + 1] / den[1], 64, axis=1)).astype(o_ref.dtype)
                lse_ref[:, pr * LANES:(pr + 1) * LANES] = jnp.where(
                    lane < 64, m_scr[2 * pr] + jnp.log(den[0]), m_scr[2 * pr + 1] + jnp.log(den[1]))

        if ng:
            @pl.when((pl.program_id(0) == n_groups - 1) & (step == n_pairs - 1))
            def _():
                _allgather_finish(*comm)

    wide = hps * LANES
    grid_spec = pltpu.PrefetchScalarGridSpec(
        num_scalar_prefetch=2, grid=(n_groups, n_pairs),
        in_specs=[pl.BlockSpec((t, wide), lambda hg, st, qi, kj: (qi[st], hg)),
                  pl.BlockSpec((t, wide), lambda hg, st, qi, kj: (kj[st], hg)),
                  pl.BlockSpec((t, wide), lambda hg, st, qi, kj: (kj[st], hg))] + [ANY] * ng,
        out_specs=[pl.BlockSpec((t, wide // 2), lambda hg, st, qi, kj: (qi[st], hg))] * 2 + [ANY] * ng,
        scratch_shapes=[pltpu.VMEM((hps, t, LANES), F32)] * 2 + (_allgather_sems(ng) if ng else []))
    return pl.pallas_call(
        body, grid_spec=grid_spec, out_shape=[SDS((s, ATT_W), BF16), SDS((s, ATT_W), F32)] + _allgather_shapes(gather),
        name="fox_fwd", compiler_params=_cp(("arbitrary", "arbitrary")))(qi_arr, kj_arr, q_aug, k_aug, v_aug, *gather)


def fox_bwd(q_aug, k_aug, z, dy, lse, dd, exchange=(), kind="to_chips", *, t=512, hps=4):
    s = z.shape[0]
    qi_arr, kj_arr, n_pairs = _causal_pairs(s // t, True)
    ne = len(exchange)
    n_groups = N_HEADS // hps
    x_shapes, x_sems, x_start, x_finish = EXCHANGES[kind]

    def body(qi_ref, kj_ref, q_ref, k_ref, v_ref, do_ref, lse_ref, dd_ref, *rest):
        e_ins, (dq_ref, dk_ref, dv_ref), e_outs = rest[:ne], rest[ne:ne + 3], rest[ne + 3:2 * ne + 3]
        comm = (e_ins, e_outs) + tuple(rest[2 * ne + 3:])
        step = pl.program_id(1)
        qi = qi_ref[step]
        kj = kj_ref[step]
        if ne:
            @pl.when((pl.program_id(0) == 0) & (step == 0))
            def _():
                x_start(*comm)

        @pl.when(step == 0)
        def _():
            dq_ref[...] = jnp.zeros_like(dq_ref)

        @pl.when(qi == kj)
        def _():
            dk_ref[...] = jnp.zeros_like(dk_ref)
            dv_ref[...] = jnp.zeros_like(dv_ref)

        def update(masked):
            lane = _lane((t, LANES))
            rows = pl.ds(pl.multiple_of(qi * t, t), t)
            for pr in range(hps // 2):
                pair = slice(pr * LANES, (pr + 1) * LANES)
                dov = do_ref[:, pair]
                dv_new = None
                for i in range(2):
                    head = (lane < 64) if i == 0 else (lane >= 64)
                    own = slice((2 * pr + i) * LANES, (2 * pr + i + 1) * LANES)
                    col = slice(pr * LANES + i * 64, pr * LANES + i * 64 + 1)
                    qv = q_ref[:, own]
                    kv = k_ref[:, own]
                    sc = _nt(qv, kv)
                    if masked:
                        sc = jnp.where(_row((t, t)) >= _lane((t, t)), sc, NEG)
                    p = jnp.exp(sc - lse_ref[:, col])
                    dp = _nt(jnp.where(head, dov, jnp.zeros_like(dov)), v_ref[:, pair])
                    ds = (p * (dp - dd_ref[:, col])).astype(BF16)
                    dq_ref[rows, own] += _nn(ds, kv)
                    dk_ref[:, own] += _tn(ds, qv)
                    dvi = _tn(p.astype(BF16), dov)
                    dv_new = dvi if dv_new is None else jnp.where(head, dvi, dv_new)
                dv_ref[:, pair] += dv_new

        @pl.when(kj < qi)
        def _():
            update(False)

        @pl.when(kj == qi)
        def _():
            update(True)

        if ne:
            @pl.when((pl.program_id(0) == n_groups - 1) & (step == n_pairs - 1))
            def _():
                x_finish(*comm)

    wide, half = hps * LANES, hps // 2 * LANES
    v_blk = Z_VA * ATT_W // half
    grid_spec = pltpu.PrefetchScalarGridSpec(
        num_scalar_prefetch=2, grid=(n_groups, n_pairs),
        in_specs=[pl.BlockSpec((t, wide), lambda hg, st, qi, kj: (qi[st], hg)),
                  pl.BlockSpec((t, wide), lambda hg, st, qi, kj: (kj[st], hg)),
                  pl.BlockSpec((t, half), lambda hg, st, qi, kj: (kj[st], v_blk + hg)),
                  pl.BlockSpec((t, half), lambda hg, st, qi, kj: (qi[st], hg)),
                  pl.BlockSpec((t, half), lambda hg, st, qi, kj: (qi[st], hg)),
                  pl.BlockSpec((t, half), lambda hg, st, qi, kj: (qi[st], hg))] + [ANY] * ne,
        out_specs=[pl.BlockSpec((s, wide), lambda hg, st, qi, kj: (0, hg)),
                   pl.BlockSpec((t, wide), lambda hg, st, qi, kj: (kj[st], hg)),
                   pl.BlockSpec((t, half), lambda hg, st, qi, kj: (kj[st], hg))] + [ANY] * ne,
        scratch_shapes=x_sems(ne) if ne else [])
    return pl.pallas_call(
        body, grid_spec=grid_spec,
        out_shape=[SDS((s, N_HEADS * LANES), F32), SDS((s, N_HEADS * LANES), F32), SDS((s, ATT_W), F32)]
        + x_shapes(exchange),
        name="fox_bwd", compiler_params=_cp(("arbitrary", "arbitrary")))(qi_arr, kj_arr, q_aug, k_aug, z, dy, lse, dd, *exchange)


def fox_post(dq_aug, dk_aug, dv, fa, bfo, *, tb=1024):
    s = dv.shape[0]
    n = s // tb

    def body(dq_ref, dk_ref, dv_ref, fa_ref, b_ref, tri_ref, dz_ref, dfa_ref, gb_ref, carry, acc):
        i = pl.program_id(0)

        @pl.when(i == 0)
        def _():
            carry[...] = jnp.zeros_like(carry)
            acc[...] = jnp.zeros_like(acc)

        lane = _lane((tb, LANES))
        d_f = jnp.zeros((tb, LANES), F32)
        for h in range(N_HEADS):
            col = dq_ref[:, h * LANES + 64:h * LANES + 65] - dk_ref[:, h * LANES + 67:h * LANES + 68]
            d_f = jnp.where(lane == h, col, d_f)
        suffix = jnp.dot(tri_ref[...], d_f, preferred_element_type=F32, precision=lax.Precision.HIGHEST) + carry[0:1, :]
        carry[0:1, :] = suffix[0:1, :]
        xv = fa_ref[...] + b_ref[...]
        dx = suffix * (1.0 / (1.0 + jnp.exp(xv)))
        dfa_ref[...] = dx.astype(dfa_ref.dtype)
        acc[...] += jnp.sum(dx.reshape(tb // 8, 8, LANES), axis=0)
        for hp in range(4):
            for src, off, scale in ((dq_ref, 0, QK_SCALE), (dk_ref, ATT_W, 1.0)):
                even = src[:, (2 * hp) * LANES:(2 * hp + 1) * LANES]
                odd = pltpu.roll(src[:, (2 * hp + 1) * LANES:(2 * hp + 2) * LANES], 64, axis=1)
                dz_ref[:, off + hp * LANES:off + (hp + 1) * LANES] = (jnp.where(lane < 64, even, odd) * scale).astype(BF16)
        dz_ref[:, 2 * ATT_W:3 * ATT_W] = dv_ref[...].astype(BF16)

        @pl.when(i == n - 1)
        def _():
            gb_ref[...] = jnp.sum(acc[...], axis=0, keepdims=True)

    rev = lambda i: (n - 1 - i, 0)
    return pl.pallas_call(
        body, grid=(n,),
        in_specs=[pl.BlockSpec((tb, N_HEADS * LANES), rev), pl.BlockSpec((tb, N_HEADS * LANES), rev),
                  pl.BlockSpec((tb, ATT_W), rev), pl.BlockSpec((tb, LANES), rev),
                  pl.BlockSpec((1, LANES), lambda i: (0, 0)), pl.BlockSpec((tb, tb), lambda i: (0, 0))],
        out_specs=[pl.BlockSpec((tb, 3 * ATT_W), rev), pl.BlockSpec((tb, LANES), rev),
                   pl.BlockSpec((1, LANES), lambda i: (0, 0))],
        out_shape=[SDS((s, 3 * ATT_W), BF16), SDS((s, LANES), BF16), SDS((1, LANES), F32)],
        scratch_shapes=[pltpu.VMEM((8, LANES), F32), pltpu.VMEM((8, LANES), F32)],
        name="fox_post", compiler_params=_cp(("arbitrary",)))(dq_aug, dk_aug, dv, fa, bfo, _tri(tb, True))


def rope_cos_sin(s):
    half = ROPE_DIM // 2
    inv_freq = ROPE_THETA ** (-jnp.arange(half, dtype=F32) * 2.0 / ROPE_DIM)
    ang = jnp.arange(s, dtype=F32)[:, None] * inv_freq[None, :]
    return jnp.tile(jnp.cos(ang), (1, LANES // half)), jnp.tile(jnp.sin(ang), (1, LANES // half))


def _rotate(x, cos, sin, sign):
    l64 = _lane(x.shape) & (HEAD_DIM - 1)
    first = l64 < ROPE_DIM // 2
    second = (l64 >= ROPE_DIM // 2) & (l64 < ROPE_DIM)
    from_next = jnp.where(first, -sign * sin, 0.0)
    from_prev = jnp.where(second, sign * sin, 0.0)
    return (x * jnp.where(first | second, cos, 1.0) + pltpu.roll(x, LANES - 8, axis=1) * from_next
            + pltpu.roll(x, 8, axis=1) * from_prev)


def _dil_rows(base, r):
    if r == 1:
        return pl.ds(pl.multiple_of(base, DIL_BLK), DIL_BLK)
    return pl.ds(base, DIL_BLK, stride=r)


def _dil_block(idx, r, nb):
    shift = nb.bit_length() - 1
    rho = idx >> shift
    n = idx & (nb - 1)
    base = rho + n * (r * DIL_BLK)
    return _dil_rows(base, r), _dil_rows(jnp.maximum(base - r * DIL_BLK, rho), r), n > 0


def _cat(a, b):
    return jnp.concatenate([a, b], axis=0)


def _two_heads(v, first_head):
    zero = jnp.zeros_like(v)
    return _cat(jnp.where(first_head, v, zero), jnp.where(first_head, zero, v))


def _dil_bands():
    b = DIL_BLK
    q = _row((2 * b, 2 * b)) & (b - 1)
    col = _lane((2 * b, 2 * b))
    return (col < b) & (col >= q), (col >= b) & (col - b <= q)


def _dil_load_qkv(zq_ref, zk_ref, zv_ref, cos_ref, sin_ref, q_ref, k_ref, v_ref, *, chunk=512):
    def step(i, carry):
        rows = pl.ds(pl.multiple_of(i * chunk, chunk), chunk)
        cos, sin = cos_ref[rows, :], sin_ref[rows, :]
        q_ref[rows, :] = _rotate(zq_ref[rows, :].astype(F32), cos, sin, 1.0) * QK_SCALE
        k_ref[rows, :] = _rotate(zk_ref[rows, :].astype(F32), cos, sin, 1.0)
        v_ref[rows, :] = zv_ref[rows, :].astype(F32)
        return carry

    lax.fori_loop(0, q_ref.shape[0] // chunk, step, 0)


def dil_fwd_all(z, cos_t, sin_t, *, unroll=8):
    s = z.shape[0]
    b = DIL_BLK
    n_blk = s // b

    def body(zq_ref, zk_ref, zv_ref, cos_ref, sin_ref, o_ref, l_ref, q_ref, k_ref, v_ref):
        _dil_load_qkv(zq_ref, zk_ref, zv_ref, cos_ref, sin_ref, q_ref, k_ref, v_ref)
        first_head = _lane((b, LANES)) < 64
        band_prev, band_cur = _dil_bands()
        for g, (_, r) in enumerate(DIL_PATTERNS):
            nb = n_blk // r

            def group(it, carry, g=g, r=r, nb=nb):
                loaded = []
                kc = vc = None
                for u in range(unroll):
                    rows_c, rows_p, has_prev = _dil_block(it * unroll + u, r, nb)
                    if u % min(nb, unroll):
                        kp, vp = kc, vc
                    else:
                        kp, vp = k_ref[rows_p, :].astype(BF16), v_ref[rows_p, :].astype(BF16)
                    kc, vc = k_ref[rows_c, :].astype(BF16), v_ref[rows_c, :].astype(BF16)
                    state = (o_ref[rows_c, :], l_ref[rows_c, :]) if g else None
                    loaded.append((rows_c, has_prev, [q_ref[rows_c, :].astype(BF16), kp, kc, vp, vc], state))
                done = []
                for rows_c, has_prev, (qv, kp, kc, vp, vc), state in loaded:
                    sc = jnp.where(band_cur | (band_prev & has_prev), _nt(_two_heads(qv, first_head), _cat(kp, kc)), NEG)
                    m = jnp.max(sc, axis=-1, keepdims=True)
                    p = jnp.exp(sc - m)
                    den = jnp.sum(p, axis=-1, keepdims=True)
                    both = _nn(p.astype(BF16), _cat(vp, vc)) / den
                    lse2 = m + jnp.log(den)
                    ov = jnp.where(first_head, both[:b], both[b:])
                    lse = jnp.where(first_head, lse2[:b], lse2[b:])
                    if state is not None:
                        m2 = jnp.maximum(state[1], lse)
                        wp = jnp.exp(state[1] - m2)
                        wn = jnp.exp(lse - m2)
                        ov = (wp * state[0] + wn * ov) / (wp + wn)
                        lse = m2 + jnp.log(wp + wn)
                    done.append((rows_c, ov, lse))
                for rows_c, ov, lse in done:
                    o_ref[rows_c, :] = ov
                    l_ref[rows_c, :] = lse
                return carry

            lax.fori_loop(0, n_blk // unroll, group, 0)

    col_blk = lambda k: pl.BlockSpec((s, LANES), lambda hp: (0, 4 * k + hp))
    table = pl.BlockSpec((s, LANES), lambda hp: (0, 0))
    out = pl.BlockSpec((s, LANES), lambda hp: (0, hp))
    return pl.pallas_call(
        body, grid=(4,), in_specs=[col_blk(Z_QB), col_blk(Z_KB), col_blk(Z_VB), table, table], out_specs=[out, out],
        out_shape=[SDS((s, ATT_W), F32)] * 2, scratch_shapes=[pltpu.VMEM((s, LANES), F32)] * 3, name="dil_fwd",
        compiler_params=_cp(("parallel",)))(z, z, z, cos_t, sin_t)


def dil_bwd_all(z, cos_t, sin_t, dy, lse, y, exchange=(), kind="to_chips", *, unroll=8):
    s = z.shape[0]
    b = DIL_BLK
    n_blk = s // b
    ne = len(exchange)
    x_shapes, x_sems, x_start, x_finish = EXCHANGES[kind]

    def body(zq_ref, zk_ref, zv_ref, cos_ref, sin_ref, do_ref, l_ref, y_ref, *rest):
        e_ins, (gq_ref, gk_ref, gv_ref), e_outs = rest[:ne], rest[ne:ne + 3], rest[ne + 3:2 * ne + 3]
        q_ref, k_ref, v_ref, dq_ref, dk_ref, dv_ref = rest[2 * ne + 3:2 * ne + 9]
        comm = (e_ins, e_outs) + tuple(rest[2 * ne + 9:])
        if ne:
            @pl.when(pl.program_id(0) == 0)
            def _():
                x_start(*comm)

        _dil_load_qkv(zq_ref, zk_ref, zv_ref, cos_ref, sin_ref, q_ref, k_ref, v_ref)
        dq_ref[...] = jnp.zeros_like(dq_ref)
        dk_ref[...] = jnp.zeros_like(dk_ref)
        dv_ref[...] = jnp.zeros_like(dv_ref)
        first_head = _lane((b, LANES)) < 64
        band_prev, band_cur = _dil_bands()
        for _, r in DIL_PATTERNS:
            nb = n_blk // r

            def group(it, carry, r=r, nb=nb):
                loaded = []
                kc = vc = None
                for u in range(unroll):
                    rows_c, rows_p, has_prev = _dil_block(it * unroll + u, r, nb)
                    if u % min(nb, unroll):
                        kp, vp = kc, vc
                    else:
                        kp, vp = k_ref[rows_p, :].astype(BF16), v_ref[rows_p, :].astype(BF16)
                    kc, vc = k_ref[rows_c, :].astype(BF16), v_ref[rows_c, :].astype(BF16)
                    vals = [q_ref[rows_c, :].astype(BF16), kp, kc, vp, vc, do_ref[rows_c, :], l_ref[rows_c, :], y_ref[rows_c, :]]
                    loaded.append((rows_c, rows_p, has_prev, vals))
                done = []
                for rows_c, rows_p, has_prev, (qv, kp, kc, vp, vc, dof, lv, yv) in loaded:
                    q2 = _two_heads(qv, first_head)
                    do2 = _two_heads(dof.astype(BF16), first_head)
                    kcat, vcat = _cat(kp, kc), _cat(vp, vc)
                    lse2 = _cat(lv[:, 0:1], lv[:, 64:65])
                    dd2 = jnp.sum(_two_heads(dof * yv, first_head), axis=-1, keepdims=True)
                    p = jnp.exp(jnp.where(band_cur | (band_prev & has_prev), _nt(q2, kcat), NEG) - lse2)
                    ds = (p * (_nt(do2, vcat) - dd2)).astype(BF16)
                    dq2 = _nn(ds, kcat)
                    dkcat = _tn(ds, q2)
                    dvcat = _tn(p.astype(BF16), do2)
                    done.append((rows_c, rows_p, (jnp.where(first_head, dq2[:b], dq2[b:]), dkcat[:b], dkcat[b:],
                                                  dvcat[:b], dvcat[b:])))
                for rows_c, rows_p, (dq, dk_p, dk_c, dv_p, dv_c) in done:
                    dq_ref[rows_c, :] += dq
                    dk_ref[rows_p, :] += dk_p
                    dk_ref[rows_c, :] += dk_c
                    dv_ref[rows_p, :] += dv_p
                    dv_ref[rows_c, :] += dv_c
                return carry

            lax.fori_loop(0, n_blk // unroll, group, 0)

        def finish(i, carry, chunk=512):
            rows = pl.ds(pl.multiple_of(i * chunk, chunk), chunk)
            cos, sin = cos_ref[rows, :], sin_ref[rows, :]
            gq_ref[rows, :] = (_rotate(dq_ref[rows, :], cos, sin, -1.0) * QK_SCALE).astype(BF16)
            gk_ref[rows, :] = _rotate(dk_ref[rows, :], cos, sin, -1.0).astype(BF16)
            gv_ref[rows, :] = dv_ref[rows, :].astype(BF16)
            return carry

        lax.fori_loop(0, s // 512, finish, 0)
        if ne:
            @pl.when(pl.program_id(0) == 3)
            def _():
                x_finish(*comm)

    col_blk = lambda k: pl.BlockSpec((s, LANES), lambda hp: (0, 4 * k + hp))
    table = pl.BlockSpec((s, LANES), lambda hp: (0, 0))
    nat = pl.BlockSpec((s, LANES), lambda hp: (0, hp))
    return pl.pallas_call(
        body, grid=(4,), in_specs=[col_blk(Z_QB), col_blk(Z_KB), col_blk(Z_VB), table, table, nat, nat, nat] + [ANY] * ne,
        out_specs=[nat, nat, nat] + [ANY] * ne, out_shape=[SDS((s, ATT_W), BF16)] * 3 + x_shapes(exchange),
        scratch_shapes=[pltpu.VMEM((s, LANES), F32)] * 6 + (x_sems(ne) if ne else []), name="dil_bwd",
        compiler_params=_cp(("arbitrary",)))(z, z, z, cos_t, sin_t, dy, lse, y, *exchange)


def _sigmoid(v):
    return 1.0 / (1.0 + jnp.exp(-v))


def gate_mix(ya, yb, wa, wb, z, *, tm=1024, tn=512):
    s = ya.shape[0]
    d = wa.shape[1]
    ga_blk = 3 * ATT_W * 2 // tn
    gb_blk = ga_blk + d // tn

    def body(ya_ref, yb_ref, wa_ref, wb_ref, ga_ref, gb_ref, pa_ref, pb_ref, mx_ref):
        pa = _nn(ya_ref[...], wa_ref[...])
        pb = _nn(yb_ref[...].astype(BF16), wb_ref[...])
        pa_ref[...] = pa.astype(BF16)
        pb_ref[...] = pb.astype(BF16)
        mx_ref[...] = (_sigmoid(ga_ref[...].astype(F32)) * pa + _sigmoid(gb_ref[...].astype(F32)) * pb).astype(BF16)

    out = pl.BlockSpec((tm, tn), lambda i, j: (i, j))
    return pl.pallas_call(
        body, grid=(s // tm, d // tn),
        in_specs=[pl.BlockSpec((tm, ATT_W), lambda i, j: (i, 0)), pl.BlockSpec((tm, ATT_W), lambda i, j: (i, 0)),
                  pl.BlockSpec((ATT_W, tn), lambda i, j: (0, j)), pl.BlockSpec((ATT_W, tn), lambda i, j: (0, j)),
                  pl.BlockSpec((tm, tn), lambda i, j: (i, ga_blk + j)), pl.BlockSpec((tm, tn), lambda i, j: (i, gb_blk + j))],
        out_specs=[out, out, out], out_shape=[SDS((s, d), BF16)] * 3, name="gate_mix",
        compiler_params=_cp(("parallel", "parallel")))(ya, yb, wa, wb, z, z)


def mix_bwd(dy, w_o, z, pa, pb, wo_a, wo_b, ya, *, tm=512):
    s, d = dy.shape

    def body(dy_ref, wo_ref, ga_ref, gb_ref, pa_ref, pb_ref, wa_ref, wb_ref, ya_ref,
             dpa_ref, dpb_ref, dg_ref, dya_ref, dyb_ref, dd_ref):
        dm = _nt(dy_ref[...], wo_ref[...])
        sa = _sigmoid(ga_ref[...].astype(F32))
        sb = _sigmoid(gb_ref[...].astype(F32))
        dpa = (dm * sa).astype(BF16)
        dpb = (dm * sb).astype(BF16)
        dpa_ref[...] = dpa
        dpb_ref[...] = dpb
        dg_ref[:, 0:d] = (dm * pa_ref[...].astype(F32) * sa * (1.0 - sa)).astype(BF16)
        dg_ref[:, d:2 * d] = (dm * pb_ref[...].astype(F32) * sb * (1.0 - sb)).astype(BF16)
        dya = _nt(dpa, wa_ref[...]).astype(BF16)
        dya_ref[...] = dya
        dyb_ref[...] = _nt(dpb, wb_ref[...])
        lane = _lane((tm, LANES))
        for pr in range(ATT_W // LANES):
            pair = slice(pr * LANES, (pr + 1) * LANES)
            prod = dya[:, pair].astype(F32) * ya_ref[:, pair].astype(F32)
            lo = jnp.sum(jnp.where(lane < 64, prod, 0.0), axis=-1, keepdims=True)
            hi = jnp.sum(jnp.where(lane >= 64, prod, 0.0), axis=-1, keepdims=True)
            dd_ref[:, pair] = jnp.where(lane < 64, lo, hi)

    row = pl.BlockSpec((tm, d), lambda i: (i, 0))
    att = pl.BlockSpec((tm, ATT_W), lambda i: (i, 0))
    whole = lambda a: pl.BlockSpec(a.shape, lambda i: (0, 0))
    return pl.pallas_call(
        body, grid=(s // tm,),
        in_specs=[row, whole(w_o), pl.BlockSpec((tm, d), lambda i: (i, 3)), pl.BlockSpec((tm, d), lambda i: (i, 4)), row, row,
                  whole(wo_a), whole(wo_b), att],
        out_specs=[row, row, pl.BlockSpec((tm, 2 * d), lambda i: (i, 0)), att, att, att],
        out_shape=[SDS((s, d), BF16), SDS((s, d), BF16), SDS((s, 2 * d), BF16), SDS((s, ATT_W), BF16),
                   SDS((s, ATT_W), F32), SDS((s, ATT_W), F32)], name="mix_bwd",
        compiler_params=_cp(("parallel",)))(dy, w_o, z, z, pa, pb, wo_a, wo_b, ya)


GELU_C = math.sqrt(2.0 / math.pi)


def _gelu_parts(a):
    a2 = a * a
    th = jnp.tanh(a * (GELU_C + (GELU_C * 0.044715) * a2))
    half = 0.5 * a
    gelu = half + half * th
    dgelu = (0.5 + 0.5 * th) + half * (1.0 - th * th) * (GELU_C + (3.0 * GELU_C * 0.044715) * a2)
    return gelu, dgelu


def _causal_taps(u, before):
    row = _row(u.shape)
    r1 = jnp.where(row == 0, before[7:8, :], pltpu.roll(u, 1, axis=0))
    r2 = jnp.where(row == 0, before[6:7, :], jnp.where(row == 1, before[7:8, :], pltpu.roll(u, 2, axis=0)))
    return r1, r2


def ffn_up(h, wa, wb, cw, cb, *, tm=1024, tn=256):
    s, d = h.shape
    f = wa.shape[1]
    nj = f // tn

    def body(h_ref, wa_ref, wb_ref, cwa_ref, cwb_ref, cba_ref, cbb_ref, ua_ref, ub_ref, ca_ref, cbo_ref, m_ref, carry):
        @pl.when(pl.program_id(1) == 0)
        def _():
            carry[...] = jnp.zeros_like(carry)

        conv = []
        for k, (w_ref, cw_ref, cb_ref, u_ref, c_ref) in enumerate(((wa_ref, cwa_ref, cba_ref, ua_ref, ca_ref),
                                                                   (wb_ref, cwb_ref, cbb_ref, ub_ref, cbo_ref))):
            u16 = _nn(h_ref[...], w_ref[...]).astype(BF16)
            u_ref[...] = u16
            u = u16.astype(F32)
            r1, r2 = _causal_taps(u, carry[k])
            carry[k] = u[tm - 8:tm, :]
            c16 = (cw_ref[0:1, :] * r2 + cw_ref[1:2, :] * r1 + cw_ref[2:3, :] * u + cb_ref[...]).astype(BF16)
            c_ref[...] = c16
            conv.append(c16.astype(F32))
        m_ref[...] = (_gelu_parts(conv[0])[0] * conv[1]).astype(BF16)

    out = pl.BlockSpec((tm, tn), lambda j, i: (i, j))
    return pl.pallas_call(
        body, grid=(nj, s // tm),
        in_specs=[pl.BlockSpec((tm, d), lambda j, i: (i, 0)),
                  pl.BlockSpec((d, tn), lambda j, i: (0, j)), pl.BlockSpec((d, tn), lambda j, i: (0, j)),
                  pl.BlockSpec((3, tn), lambda j, i: (0, j)), pl.BlockSpec((3, tn), lambda j, i: (0, nj + j)),
                  pl.BlockSpec((1, tn), lambda j, i: (0, j)), pl.BlockSpec((1, tn), lambda j, i: (0, nj + j))],
        out_specs=[out] * 5, out_shape=[SDS((s, f), BF16)] * 5,
        scratch_shapes=[pltpu.VMEM((2, 8, tn), F32)], name="ffn_up",
        compiler_params=_cp(("parallel", "arbitrary")))(h, wa, wb, cw, cw, cb, cb)


def ffn_bwd(dm, ua, ub, ca, cbo, cw, *, tm=1024, tn=256):
    s, f = dm.shape
    nj = f // tn
    ni = s // tm

    def body(dm_ref, ua_ref, ub_ref, ca_ref, cbo_ref, cwa_ref, cwb_ref, dua_ref, dub_ref, ga_ref, gb_ref, carry):
        @pl.when(pl.program_id(1) == 0)
        def _():
            carry[...] = jnp.zeros_like(carry)
            ga_ref[...] = jnp.zeros_like(ga_ref)
            gb_ref[...] = jnp.zeros_like(gb_ref)

        row = _row((tm, tn))
        dmv = dm_ref[...].astype(F32)
        gelu, dgelu = _gelu_parts(ca_ref[...].astype(F32))
        dcs = (dmv * cbo_ref[...].astype(F32) * dgelu, dmv * gelu)
        for k, (dc, u_ref, cw_ref, du_ref, g_ref) in enumerate(((dcs[0], ua_ref, cwa_ref, dua_ref, ga_ref),
                                                                (dcs[1], ub_ref, cwb_ref, dub_ref, gb_ref))):
            u = u_ref[...].astype(F32)
            after = carry[k]
            n1 = jnp.where(row == tm - 1, after[0:1, :], pltpu.roll(dc, tm - 1, axis=0))
            n2 = jnp.where(row == tm - 2, after[0:1, :], jnp.where(row == tm - 1, after[1:2, :], pltpu.roll(dc, tm - 2, axis=0)))
            g_ref[0:1, :] += jnp.sum(n2 * u, axis=0, keepdims=True)
            g_ref[1:2, :] += jnp.sum(n1 * u, axis=0, keepdims=True)
            g_ref[2:3, :] += jnp.sum(dc * u, axis=0, keepdims=True)
            g_ref[3:4, :] += jnp.sum(dc, axis=0, keepdims=True)
            du_ref[...] = (cw_ref[2:3, :] * dc + cw_ref[1:2, :] * n1 + cw_ref[0:1, :] * n2).astype(BF16)
            carry[k] = dc[0:8, :]

    tile = pl.BlockSpec((tm, tn), lambda j, i: (ni - 1 - i, j))
    gspec = pl.BlockSpec((8, tn), lambda j, i: (0, j))
    return pl.pallas_call(
        body, grid=(nj, ni),
        in_specs=[tile] * 5 + [pl.BlockSpec((3, tn), lambda j, i: (0, j)), pl.BlockSpec((3, tn), lambda j, i: (0, nj + j))],
        out_specs=[tile, tile, gspec, gspec],
        out_shape=[SDS((s, f), BF16), SDS((s, f), BF16), SDS((8, f), F32), SDS((8, f), F32)],
        scratch_shapes=[pltpu.VMEM((2, 8, tn), F32)], name="ffn_bwd",
        compiler_params=_cp(("parallel", "arbitrary")))(dm, ua, ub, ca, cbo, cw, cw)


def adamw(w, g, m, v, *, name, tr=None):
    r = w.shape[0]
    rest = w.shape[1:]
    if tr is None:
        tr = r
        for cand in (256, 128, 64, 32, 16, 8):
            if r % cand == 0:
                tr = cand
                break

    def body(w_ref, g_ref, m_ref, v_ref, d_ref, nm_ref, nv_ref):
        gv = g_ref[...]
        mn = ADAM_B1 * m_ref[...] + (1.0 - ADAM_B1) * gv
        vn = ADAM_B2 * v_ref[...] + (1.0 - ADAM_B2) * (gv * gv)
        m_hat = mn / (1.0 - ADAM_B1 ** ADAM_STEP)
        v_hat = vn / (1.0 - ADAM_B2 ** ADAM_STEP)
        d_ref[...] = -ADAM_LR * (m_hat / (jnp.sqrt(v_hat) + ADAM_EPS) + ADAM_WD * w_ref[...])
        nm_ref[...] = mn
        nv_ref[...] = vn

    blk = pl.BlockSpec((tr,) + rest, lambda i: (i,) + (0,) * len(rest))
    return pl.pallas_call(body, grid=(r // tr,), in_specs=[blk] * 4, out_specs=[blk] * 3, out_shape=[SDS(w.shape, F32)] * 3,
                          name=name, compiler_params=_cp(("parallel",)))(w, g, m, v)


def adamw_rows_view(w, g_mine, g_full, m, v, c_arr, *, name, tc=256):
    r, _, c = w.shape
    per_half = c // 2 // tc

    def body(c_ref, w_ref, gm_ref, gf_ref, m_ref, v_ref, d_ref, nm_ref, nv_ref, go_ref):
        mine = (pl.program_id(0) >> (per_half.bit_length() - 1)) == c_ref[0]
        gv = jnp.where(mine, gm_ref[...], gf_ref[...])
        mn = ADAM_B1 * m_ref[:, 0, :] + (1.0 - ADAM_B1) * gv
        vn = ADAM_B2 * v_ref[:, 0, :] + (1.0 - ADAM_B2) * (gv * gv)
        m_hat = mn / (1.0 - ADAM_B1 ** ADAM_STEP)
        v_hat = vn / (1.0 - ADAM_B2 ** ADAM_STEP)
        d_ref[:, 0, :] = -ADAM_LR * (m_hat / (jnp.sqrt(v_hat) + ADAM_EPS) + ADAM_WD * w_ref[:, 0, :])
        nm_ref[:, 0, :] = mn
        nv_ref[:, 0, :] = vn
        go_ref[:, 0, :] = gv

    b3 = pl.BlockSpec((r, 1, tc), lambda i, c_ref: (0, 0, i))
    own = pl.BlockSpec((r, tc), lambda i, c_ref: (0, jnp.clip(i - c_ref[0] * per_half, 0, per_half - 1)))
    full = pl.BlockSpec((r, tc), lambda i, c_ref: (0, i))
    grid_spec = pltpu.PrefetchScalarGridSpec(num_scalar_prefetch=1, grid=(c // tc,), in_specs=[b3, own, full, b3, b3],
                                             out_specs=[b3] * 4)
    return pl.pallas_call(body, grid_spec=grid_spec, out_shape=[SDS(w.shape, F32)] * 4, name=name,
                          compiler_params=_cp(("parallel",)))(c_arr, w, g_mine, g_full, m, v)


ANY = pl.BlockSpec(memory_space=pl.ANY)
ICI_KINDS = ("x", "y", "xy")


def _coords():
    return lax.axis_index("x"), lax.axis_index("y"), lax.axis_index("c")


def _peer(kind, x, y, c):
    if kind == "c":
        return (x, y, 1 - c)
    if kind == "x":
        return (1 - x, y, c)
    if kind == "y":
        return (x, 1 - y, c)
    return (1 - x, 1 - y, c)


def _chip_of(p):
    return 2 * p[0] + p[1]


def _half(rows, which):
    h = rows // 2
    return pl.ds(pl.multiple_of(which * h, 16), h)


def _remote(src, dst, send_sem, recv_sem, to):
    return pltpu.make_async_remote_copy(src_ref=src, dst_ref=dst, send_sem=send_sem, recv_sem=recv_sem,
                                        device_id=to, device_id_type=MESH)


def allgather_balanced(shard, *, name):
    r, cols = shard.shape
    h, q = r // 2, r // 4

    def body(in_ref, out_ref, send_sems, recv_sems):
        x, y, c = _coords()
        me, sibling = (x, y, c), (x, y, 1 - c)
        nbr = ((1 - x, y, c), (x, 1 - y, c))
        chip = (2 * (1 - x) + y, 2 * x + (1 - y), 2 * (1 - x) + (1 - y))
        quarter = lambda core, i: pl.ds(pl.multiple_of(core * h + i * q, 16), q)
        sent = []

        def go(src, dst, slot, to):
            cp = _remote(src, dst, send_sems.at[slot], recv_sems.at[slot], to)
            cp.start()
            sent.append(cp)

        def landed(region, slot):
            _remote(region, region, send_sems.at[slot], recv_sems.at[slot], me).wait_recv()

        for i in range(2):
            for k in range(2):
                qi = k if i == 0 else 1 - k
                go(in_ref.at[quarter(c, qi)], out_ref.at[2 * x + y, quarter(c, qi)], 2 * k + qi, nbr[k])
        for k in range(2):
            piece = out_ref.at[chip[k], quarter(c, k)]
            landed(piece, 2 * k + k)
            go(piece, piece, 4 + k, nbr[1 - k])
            go(piece, piece, 6 + 2 * k + k, sibling)
        for k in range(2):
            piece = out_ref.at[chip[k], quarter(c, 1 - k)]
            landed(piece, 2 * k + 1 - k)
            go(piece, piece, 6 + 2 * k + 1 - k, sibling)
        for k in range(2):
            piece = out_ref.at[chip[2], quarter(c, k)]
            landed(piece, 4 + k)
            go(piece, piece, 10 + k, sibling)
        for k in range(2):
            for i in range(2):
                landed(out_ref.at[chip[k], quarter(1 - c, i)], 6 + 2 * k + i)
            landed(out_ref.at[chip[2], quarter(1 - c, k)], 10 + k)
        for cp in sent:
            cp.wait_send()

    return pl.pallas_call(
        body, in_specs=[ANY], out_specs=ANY, out_shape=SDS((4,) + shard.shape, shard.dtype),
        scratch_shapes=[pltpu.SemaphoreType.DMA((12,)), pltpu.SemaphoreType.DMA((12,))], name=name)(shard)


def _allgather_shapes(shards):
    return [SDS((4,) + a.shape, a.dtype) for a in shards]


def _allgather_sems(n):
    return [pltpu.SemaphoreType.DMA((n, 6)), pltpu.SemaphoreType.DMA((n, 6))]


def _allgather_rows(ref, is_halved, which):
    r = ref.shape[0]
    return _half(r, which) if is_halved else pl.ds(0, r)


def _allgather_first(ins, outs, send_sems, recv_sems, halved):
    x, y, c = _coords()
    my_chip = 2 * x + y
    cps = []
    for w in range(len(ins)):
        rows = _allgather_rows(ins[w], halved[w], c)
        for k, kind in enumerate(ICI_KINDS):
            cps.append(_remote(ins[w].at[rows], outs[w].at[my_chip, rows], send_sems.at[w, k], recv_sems.at[w, k],
                               _peer(kind, x, y, c)))
    return cps


def _allgather_start(ins, outs, send_sems, recv_sems, halved):
    for cp in _allgather_first(ins, outs, send_sems, recv_sems, halved):
        cp.start()


def _allgather_finish(ins, outs, send_sems, recv_sems, halved):
    x, y, c = _coords()
    me = (x, y, c)
    second = []
    for w in range(len(ins)):
        for k, kind in enumerate(ICI_KINDS):
            landed = outs[w].at[_chip_of(_peer(kind, x, y, c)), _allgather_rows(ins[w], halved[w], c)]
            _remote(landed, landed, send_sems.at[w, k], recv_sems.at[w, k], me).wait_recv()
            if halved[w]:
                cp = _remote(landed, landed, send_sems.at[w, 3 + k], recv_sems.at[w, 3 + k], _peer("c", x, y, c))
                cp.start()
                second.append(cp)
    for w in range(len(ins)):
        if halved[w]:
            for k, kind in enumerate(ICI_KINDS):
                other = outs[w].at[_chip_of(_peer(kind, x, y, c)), _allgather_rows(ins[w], True, 1 - c)]
                _remote(other, other, send_sems.at[w, 3 + k], recv_sems.at[w, 3 + k], me).wait_recv()
    for cp in _allgather_first(ins, outs, send_sems, recv_sems, halved) + second:
        cp.wait_send()


def _half_of(ref, by_cols, which):
    lead = (slice(None),) * (len(ref.shape) - 2)
    if by_cols:
        h = ref.shape[-1] // 2
        return ref.at[lead + (slice(None), pl.ds(pl.multiple_of(which * h, LANES), h))]
    return ref.at[lead + (_half(ref.shape[-2], which),)]


def _half_shape(shape, by_cols):
    return shape[:-1] + (shape[-1] // 2,) if by_cols else shape[:-2] + (shape[-2] // 2, shape[-1])


def grads_to_sibling(gs, by_cols, *, name):
    n = len(gs)

    def body(*refs):
        ins, outs = refs[:n], refs[n:2 * n]
        send_sems, recv_sems = refs[2 * n:]
        x, y, c = _coords()
        cps = []
        for w in range(n):
            cp = _remote(_half_of(ins[w], by_cols[w], 1 - c), outs[w], send_sems.at[w], recv_sems.at[w], _peer("c", x, y, c))
            cp.start()
            cps.append(cp)
        for cp in cps:
            cp.wait()

    return pl.pallas_call(
        body, in_specs=[ANY] * n, out_specs=[ANY] * n,
        out_shape=[SDS(_half_shape(a.shape, bc), a.dtype) for a, bc in zip(gs, by_cols)],
        scratch_shapes=[pltpu.SemaphoreType.DMA((n,)), pltpu.SemaphoreType.DMA((n,))], name=name)(*gs)


def _to_chips_shapes(ps):
    return [SDS((3,) + a.shape[1:], a.dtype) for a in ps]


def _to_chips_sems(n):
    return [pltpu.SemaphoreType.DMA((n, 3)), pltpu.SemaphoreType.DMA((n, 3))]


def _to_chips_copies(ins, outs, send_sems, recv_sems):
    x, y, c = _coords()
    cps = []
    for w in range(len(ins)):
        for k, kind in enumerate(ICI_KINDS):
            to = _peer(kind, x, y, c)
            cps.append(_remote(ins[w].at[_chip_of(to)], outs[w].at[k], send_sems.at[w, k], recv_sems.at[w, k], to))
    return cps


def _to_chips_start(ins, outs, send_sems, recv_sems):
    for cp in _to_chips_copies(ins, outs, send_sems, recv_sems):
        cp.start()


def _to_chips_finish(ins, outs, send_sems, recv_sems):
    for cp in _to_chips_copies(ins, outs, send_sems, recv_sems):
        cp.wait()


def _to_owners_shapes(ps):
    return [SDS((7, a.shape[1] // 2, a.shape[2]), a.dtype) for a in ps]


def _to_owners_sems(n):
    return [pltpu.SemaphoreType.DMA((n, 7)), pltpu.SemaphoreType.DMA((n, 7))]


def _to_owners_copies(ins, outs, send_sems, recv_sems):
    x, y, c = _coords()
    cps = []
    for w in range(len(ins)):
        rows = ins[w].shape[1]
        for k, kind in enumerate(ICI_KINDS):
            px, py, _ = _peer(kind, x, y, c)
            for h in range(2):
                cps.append(_remote(ins[w].at[2 * px + py, _half(rows, h)], outs[w].at[2 * k + c],
                                   send_sems.at[w, 2 * k + h], recv_sems.at[w, 2 * k + c], (px, py, h)))
        cps.append(_remote(ins[w].at[2 * x + y, _half(rows, 1 - c)], outs[w].at[6], send_sems.at[w, 6], recv_sems.at[w, 6],
                           _peer("c", x, y, c)))
    return cps


def _to_owners_start(ins, outs, send_sems, recv_sems):
    for cp in _to_owners_copies(ins, outs, send_sems, recv_sems):
        cp.start()


def _to_owners_finish(ins, outs, send_sems, recv_sems):
    for cp in _to_owners_copies(ins, outs, send_sems, recv_sems):
        cp.wait_send()
    for w in range(len(ins)):
        for slot in range(7):
            got = outs[w].at[slot]
            _remote(got, got, send_sems.at[w, slot], recv_sems.at[w, slot], _coords()).wait_recv()


EXCHANGES = {"to_chips": (_to_chips_shapes, _to_chips_sems, _to_chips_start, _to_chips_finish),
             "to_owners": (_to_owners_shapes, _to_owners_sems, _to_owners_start, _to_owners_finish)}


def halves_to_full(hs, by_cols, *, name):
    n = len(hs)

    def body(*refs):
        ins, outs = refs[:n], refs[n:2 * n]
        send_sems, recv_sems = refs[2 * n:]
        x, y, c = _coords()
        cps = []
        for w in range(n):
            cp = _remote(ins[w], _half_of(outs[w], by_cols[w], c), send_sems.at[w], recv_sems.at[w], _peer("c", x, y, c))
            cp.start()
            cps.append(cp)
        for cp in cps:
            cp.wait()

    return pl.pallas_call(
        body, in_specs=[ANY] * n, out_specs=[ANY] * n,
        out_shape=[SDS((a.shape[0], 2 * a.shape[1]) if bc else (2 * a.shape[0], a.shape[1]), a.dtype)
                   for a, bc in zip(hs, by_cols)],
        scratch_shapes=[pltpu.SemaphoreType.DMA((n,)), pltpu.SemaphoreType.DMA((n,))],
        name=name)(*hs)


def _row_tile(rows):
    for cand in (256, 192, 176, 128, 64, 32, 16):
        if rows % cand == 0:
            return cand
    return rows


def chip_sum(g, recv, c_arr, by_cols, *, name):
    _, r, cols = g.shape

    def body(c_ref, g_ref, r_ref, f_ref, b_ref):
        tot = g_ref[...] + r_ref[...]
        f_ref[...] = tot
        b_ref[...] = tot.astype(BF16)

    if by_cols:
        tc = 2 * LANES
        nblk = cols // 2 // tc
        shape = (4, r, cols // 2)
        blk = pl.BlockSpec((None, r, tc), lambda j, i, c_ref: (j, 0, i))
        mine = pl.BlockSpec((None, r, tc), lambda j, i, c_ref: (j, 0, c_ref[0] * nblk + i))
    else:
        tr = _row_tile(r // 2)
        nblk = r // 2 // tr
        shape = (4, r // 2, cols)
        blk = pl.BlockSpec((None, tr, cols), lambda j, i, c_ref: (j, i, 0))
        mine = pl.BlockSpec((None, tr, cols), lambda j, i, c_ref: (j, c_ref[0] * nblk + i, 0))
    grid_spec = pltpu.PrefetchScalarGridSpec(num_scalar_prefetch=1, grid=(4, nblk), in_specs=[mine, blk], out_specs=[blk, blk])
    return pl.pallas_call(body, grid_spec=grid_spec, out_shape=[SDS(shape, F32), SDS(shape, BF16)],
                          name=name, compiler_params=_cp(("parallel", "parallel")))(c_arr, g, recv)


def final_sum(pf, recv, chip_arr, *, name):
    _, h, cols = pf.shape
    tr = _row_tile(h)

    def body(chip_ref, p_ref, r_ref, o_ref):
        o_ref[...] = ((p_ref[...] + r_ref[0].astype(F32)) + r_ref[1].astype(F32)) + r_ref[2].astype(F32)

    grid_spec = pltpu.PrefetchScalarGridSpec(
        num_scalar_prefetch=1, grid=(h // tr,),
        in_specs=[pl.BlockSpec((None, tr, cols), lambda i, chip_ref: (chip_ref[0], i, 0)),
                  pl.BlockSpec((3, tr, cols), lambda i, chip_ref: (0, i, 0))],
        out_specs=pl.BlockSpec((tr, cols), lambda i, chip_ref: (i, 0)))
    return pl.pallas_call(body, grid_spec=grid_spec, out_shape=SDS((h, cols), F32), name=name,
                          compiler_params=_cp(("parallel",)))(chip_arr, pf, recv)


def owner_sum(g, recv, pos_arr, *, name):
    _, r, cols = g.shape
    h = r // 2
    tr = _row_tile(h)
    nblk = h // tr

    def body(pos_ref, g_ref, r_ref, o_ref):
        tot = g_ref[...]
        for slot in range(7):
            tot = tot + r_ref[slot].astype(F32)
        o_ref[...] = tot

    grid_spec = pltpu.PrefetchScalarGridSpec(
        num_scalar_prefetch=1, grid=(nblk,),
        in_specs=[pl.BlockSpec((None, tr, cols), lambda i, pos: (pos[0], pos[1] * nblk + i, 0)),
                  pl.BlockSpec((7, tr, cols), lambda i, pos: (0, i, 0))],
        out_specs=pl.BlockSpec((tr, cols), lambda i, pos: (i, 0)))
    return pl.pallas_call(body, grid_spec=grid_spec, out_shape=SDS((h, cols), F32), name=name,
                          compiler_params=_cp(("parallel",)))(pos_arr, g, recv)


def allreduce_small(v, *, name):
    rws, cols = v.shape

    def body(v_ref, all_ref, sum_ref, send_sems, recv_sems, local_sem):
        x, y, c = _coords()
        me, sibling = (x, y, c), (x, y, 1 - c)
        chips = [(1 - x, y), (x, 1 - y), (1 - x, 1 - y)]

        def rows(px, py, pc):
            return all_ref.at[pl.ds(pl.multiple_of((4 * px + 2 * py + pc) * rws, 8), rws), :]

        def copy(k, block, to, src=None):
            return _remote(rows(*block) if src is None else src, rows(*block), send_sems.at[k], recv_sems.at[k], to)

        mine = pltpu.make_async_copy(v_ref, rows(*me), local_sem)
        mine.start()
        first = [copy(0, me, sibling, src=v_ref)]
        first += [copy(1 + j, me, (*chip, c), src=v_ref) for j, chip in enumerate(chips)]
        for cp in first:
            cp.start()
        passed = [copy(4 + j, (*chip, c), sibling) for j, chip in enumerate(chips)]
        for j, chip in enumerate(chips):
            copy(1 + j, (*chip, c), me).wait_recv()
            passed[j].start()
        copy(0, sibling, me).wait_recv()
        for j, chip in enumerate(chips):
            copy(4 + j, (*chip, 1 - c), me).wait_recv()
        for cp in first + passed:
            cp.wait_send()
        mine.wait()
        tot = all_ref[0:rws, :]
        for dev in range(1, 8):
            tot = tot + all_ref[dev * rws:(dev + 1) * rws, :]
        sum_ref[...] = tot

    vm = pl.BlockSpec(memory_space=pltpu.VMEM)
    return pl.pallas_call(
        body, in_specs=[vm], out_specs=[vm, vm],
        out_shape=[SDS((8 * rws, cols), v.dtype), SDS((rws, cols), v.dtype)],
        scratch_shapes=[pltpu.SemaphoreType.DMA((7,)), pltpu.SemaphoreType.DMA((7,)), pltpu.SemaphoreType.DMA],
        name=name)(v)[1]


def _pack_rows(parts, rows):
    out = []
    for a, r in zip(parts, rows):
        flat = a.reshape(-1)
        flat = jnp.pad(flat, (0, r * LANES - flat.shape[0]))
        out.append(flat.reshape(r, LANES))
    return jnp.concatenate(out, axis=0)


def _unpack_rows(packed, shapes, rows):
    out, at = [], 0
    for shp, r in zip(shapes, rows):
        size = int(np.prod(shp))
        out.append(packed[at:at + r].reshape(-1)[:size].reshape(shp))
        at += r
    return out


def kernel(x, g_pre_mix, w_in, b_forget, w_o_fox, w_o_dil, w_out, g_post_mix, g_pre_ffn, w_up, conv_w, conv_b, w_down, g_post_ffn, loss_target, m_g_pre_mix, m_w_in, m_b_forget, m_w_o_fox, m_w_o_dil, m_w_out, m_g_post_mix, m_g_pre_ffn, m_w_up, m_conv_w, m_conv_b, m_w_down, m_g_post_ffn, v_g_pre_mix, v_w_in, v_b_forget, v_w_o_fox, v_w_o_dil, v_w_out, v_g_post_mix, v_g_pre_ffn, v_w_up, v_conv_w, v_conv_b, v_w_down, v_g_post_ffn):
    xi, yi, ci = _coords()
    chip = 2 * xi + yi
    c_arr = jnp.reshape(ci, (1,)).astype(jnp.int32)
    chip_arr = jnp.reshape(chip, (1,)).astype(jnp.int32)
    xs = x[0]
    target = loss_target[0]
    s, d = xs.shape
    f_half = w_down.shape[1] * 4
    cols_in = w_in.shape[2]

    big = (w_in, w_o_fox, w_o_dil, w_out, w_up, w_down)
    shards = [w[0].astype(BF16) for w in big]
    a_in = allgather_balanced(shards[0], name="allgather_w_in")
    w_in_full = jnp.concatenate([jnp.where(chip == j, shards[0], a_in[j]) for j in range(4)], axis=1)
    nf = N_HEADS
    e_a, e_b = 3 * ATT_W, 3 * ATT_W + nf
    wz = jnp.concatenate([w_in_full[:, :e_a], w_in_full[:, e_b:]], axis=1)
    wf = jnp.pad(w_in_full[:, e_a:e_b], ((0, 0), (0, LANES - nf)))
    cb = conv_b
    bfo = jnp.pad(b_forget, ((0, 0), (0, LANES - nf)))

    h1 = rmsnorm_fwd(xs, g_pre_mix)
    z = mm([(h1, d, 0)], [(wz, d, 0)], nt=False, out_dtype=BF16, tm=2048, tn=512, name="in_proj")
    fa = mm([(h1, d, 0)], [(wf, d, 0)], nt=False, out_dtype=F32, tm=1024, tn=LANES, name="in_proj_forget")
    q_aug, k_aug, v_aug = fox_prep(z, fa, bfo)
    later = shards[1:] + [conv_w[0]]
    ya, lse_a, *late = fox_fwd(q_aug, k_aug, v_aug, gather=later, halved=[True] * 5 + [False], hps=N_HEADS)
    a_of, a_od, a_out, a_up, a_down, a_cw = [
        lax.dynamic_update_index_in_dim(a4, own, chip, 0) for a4, own in zip(late, later)]
    cw = jnp.concatenate([a_cw[j] for j in range(4)], axis=1)
    wo_a = jnp.concatenate([a_of[j] for j in range(4)], axis=1)
    wo_b = jnp.concatenate([a_od[j] for j in range(4)], axis=1)
    w_o = a_out.reshape(d, d)
    w_dn = a_down.reshape(f_half, d)
    wu_a = jnp.concatenate([a_up[0], a_up[1]], axis=1)
    wu_b = jnp.concatenate([a_up[2], a_up[3]], axis=1)
    cos_t, sin_t = rope_cos_sin(s)
    yb, lse_b = dil_fwd_all(z, cos_t, sin_t)
    pa, pb, mixed = gate_mix(ya, yb, wo_a, wo_b, z)
    y1, x1, h2 = proj_norm_res(mixed, w_o, g_post_mix, xs, g_pre_ffn, name="out_proj")
    ua, ub, conv_a, conv_bh, mid = ffn_up(h2, wu_a, wu_b, cw, cb)
    dout, dy2, gg_post_ffn, sq = proj_norm_loss(mid, w_dn, g_post_ffn, x1, target, name="down_proj")
    loss = lax.psum(0.5 * sq[0, 0] / d, ("x", "y", "c"))

    dmid = mm([(dy2, d, 0)], [(w_dn, d, 0)], nt=True, out_dtype=BF16, tm=1024, tn=f_half // 2, name="down_dgrad")
    dw_down, dw_down16 = wgrad((mid, f_half, 0), dy2, tk=f_half // 2, tn=1024, ts=2048, name="down_wgrad", bf16_copy=True)
    dua, dub, gc_a, gc_b = ffn_bwd(dmid, ua, ub, conv_a, conv_bh, cw)
    dx1, dy1, gg_pre_ffn, gg_post_mix = mm_norm_bwd(
        [(dua, f_half, 0), (dub, f_half, 0)], [(wu_a, f_half, 0), (wu_b, f_half, 0)],
        [(x1, g_pre_ffn, dout, F32), (y1, g_post_mix, None, BF16)], name="up_dgrad")
    dw_up = None
    for k, du in enumerate((dua, dub)):
        dw_up = wgrad((h2, d, 0), du, tk=1024, tn=f_half // 2, ts=2048, name=f"up_wgrad_{k}", chip_major=True,
                      slabs=(4, 2 * k), into=dw_up, bf16_copy=True)
    g_ffn = [(dw_up[0], dw_up[1]), (dw_down.reshape(4, f_half // 4, d), dw_down16.reshape(4, f_half // 4, d))]
    dw_out, dw_out16 = wgrad((mixed, d, 0), dy1, tk=1024, tn=1024, ts=2048, name="out_wgrad", bf16_copy=True)
    dpa, dpb, dz_g, dya, dyb, dd_a = mix_bwd(dy1, w_o, z, pa, pb, wo_a, wo_b, ya)
    by_chip_cols = lambda a: jnp.stack([a[:, j * (d // 4):(j + 1) * (d // 4)] for j in range(4)], axis=0)
    dw_of = [by_chip_cols(a) for a in wgrad((ya, ATT_W, 0), dpa, tk=ATT_W, tn=d, ts=1024, name="fox_o_wgrad", bf16_copy=True)]
    dw_od = [by_chip_cols(a) for a in wgrad((yb, ATT_W, 0), dpb, tk=ATT_W, tn=d, ts=1024, name="dil_o_wgrad", bf16_copy=True)]
    g_mix = [dw_of, dw_od, (dw_out.reshape(4, d // 4, d), dw_out16.reshape(4, d // 4, d))]
    dq_aug, dk_aug, dv_a, *got_ffn = fox_bwd(q_aug, k_aug, z, dya, lse_a, dd_a, exchange=[g[1] for g in g_ffn], kind="to_owners")
    dz_a, dfa, gg_bf = fox_post(dq_aug, dk_aug, dv_a, fa, bfo)
    *dz_b, got_of, got_od, got_out = dil_bwd_all(z, cos_t, sin_t, dyb, lse_b, yb, exchange=[g[1] for g in g_mix],
                                                 kind="to_owners")
    got_mix = [got_of, got_od, got_out]
    dwt_a = wgrad((dz_a, e_a, 0), h1, tk=e_a // 2, tn=d, ts=1024, name="in_wgrad_a")
    dwt_b = [wgrad((part, ATT_W, 0), h1, tk=ATT_W, tn=d, ts=1024, name=f"in_wgrad_b{k}") for k, part in enumerate(dz_b)]
    dwt_g = wgrad((dz_g, 2 * d, 0), h1, tk=d, tn=d, ts=1024, name="in_wgrad_g")
    dwt_f = wgrad((dfa, LANES, 0), h1, tk=LANES, tn=d, ts=1024, name="in_wgrad_f")
    dwt_full = jnp.concatenate([dwt_a, dwt_f[:nf], *dwt_b, dwt_g], axis=0)
    dw_in = jnp.stack([dwt_full[j * cols_in:(j + 1) * cols_in] for j in range(4)], axis=0)
    from_sib = grads_to_sibling([dw_in], [True], name="grads_to_sibling_in")
    sum_in = chip_sum(dw_in, from_sib[0], c_arr, True, name="chip_sum_w_in")
    grad_x, gg_pre_mix, got_in = mm_norm_bwd(
        [(dz_a, e_a, 0), *[(part, ATT_W, 0) for part in dz_b], (dz_g, d, 0), (dz_g, d, 1), (dfa, LANES, 0)],
        [(wz, e_a, 0), *[(wz, ATT_W, Z_QB + k) for k in range(3)], (wz, d, 3), (wz, d, 4), (wf, LANES, 0)],
        [(xs, g_pre_mix, dx1, F32)], exchange=[sum_in[1]], tm=512, name="in_dgrad")

    names = ("w_in", "w_o_fox", "w_o_dil", "w_out", "w_up", "w_down")
    pos_arr = jnp.concatenate([chip_arr, c_arr])
    halves = [final_sum(sum_in[0], got_in, chip_arr, name="final_sum_w_in")] + [
        owner_sum(g[0], got, pos_arr, name=f"owner_sum_{nm}") for g, got, nm in zip(g_mix + g_ffn, got_mix + got_ffn, names[1:])]
    from_half = halves_to_full(halves, [True] + [False] * 5, name="halves_to_full")
    g_big = [None] + [lax.dynamic_update_slice_in_dim(full, mine, ci * mine.shape[0], axis=0)
                      for full, mine in zip(from_half[1:], halves[1:])]
    upd_big = [adamw(w[0], g, m[0], v[0], name=f"adamw_{nm}") for w, g, m, v, nm in list(zip(
        big, g_big, (m_w_in, m_w_o_fox, m_w_o_dil, m_w_out, m_w_up, m_w_down),
        (v_w_in, v_w_o_fox, v_w_o_dil, v_w_out, v_w_up, v_w_down), names))[1:]]
    to_t = lambda a: jnp.transpose(a, (2, 0, 1))
    from_t = lambda a: jnp.transpose(a, (1, 2, 0))
    *upd_in, g_in_t = adamw_rows_view(to_t(w_in), halves[0], from_half[0], to_t(m_w_in), to_t(v_w_in), c_arr,
                                      name="adamw_w_in")

    g_cw_loc = jnp.concatenate([gc_a[0:3], gc_b[0:3]], axis=1)
    g_cb_loc = jnp.concatenate([gc_a[3:4], gc_b[3:4]], axis=1)
    small_loc = [gg_pre_mix, gg_post_mix, gg_pre_ffn, gg_post_ffn, g_cb_loc, gg_bf[:, :nf], g_cw_loc]
    red_rows = (8, 8, 8, 8, 48, 8, 136)
    red = allreduce_small(_pack_rows(small_loc, red_rows), name="allreduce_small")
    g_pm, g_qm, g_pf, g_qf, g_cb, g_bf, g_cw_full = _unpack_rows(red, [a.shape for a in small_loc], red_rows)
    cols_cw = conv_w.shape[2]
    g_cw = lax.dynamic_slice_in_dim(g_cw_full, chip * cols_cw, cols_cw, axis=1)
    small_w = (g_pre_mix, g_post_mix, g_pre_ffn, g_post_ffn, conv_b, b_forget, conv_w[0])
    small_m = (m_g_pre_mix, m_g_post_mix, m_g_pre_ffn, m_g_post_ffn, m_conv_b, m_b_forget, m_conv_w[0])
    small_v = (v_g_pre_mix, v_g_post_mix, v_g_pre_ffn, v_g_post_ffn, v_conv_b, v_b_forget, v_conv_w[0])
    small_g = (g_pm, g_qm, g_pf, g_qf, g_cb, g_bf, g_cw)
    small_names = ("g_pre_mix", "g_post_mix", "g_pre_ffn", "g_post_ffn", "conv_b", "b_forget", "conv_w")
    per_param = [adamw(w, g, m, v, name=f"adamw_{nm}") for w, g, m, v, nm in zip(small_w, small_g, small_m, small_v, small_names)]
    upd_small = [[u[j] for u in per_param] for j in range(3)]

    order = ("g_pre_mix", "w_in", "b_forget", "w_o_fox", "w_o_dil", "w_out", "g_post_mix", "g_pre_ffn", "w_up", "conv_w",
             "conv_b", "w_down", "g_post_ffn")
    grads, deltas, new_ms, new_vs = {}, {}, {}, {}
    grads["w_in"] = from_t(g_in_t)
    deltas["w_in"], new_ms["w_in"], new_vs["w_in"] = (from_t(a) for a in upd_in)
    for k, nm in enumerate(names[1:]):
        grads[nm] = g_big[k + 1][None]
        deltas[nm], new_ms[nm], new_vs[nm] = (a[None] for a in upd_big[k])
    for k, nm in enumerate(small_names):
        lead = (lambda a: a[None]) if nm == "conv_w" else (lambda a: a)
        grads[nm] = lead(small_g[k])
        deltas[nm], new_ms[nm], new_vs[nm] = (lead(upd_small[j][k]) for j in range(3))
    return (loss, grad_x[None], *[grads[nm] for nm in order], *[deltas[nm] for nm in order],
            *[new_ms[nm] for nm in order], *[new_vs[nm] for nm in order])
```

```python
import functools
import math

import numpy as np
import jax
import jax.numpy as jnp
from jax import lax
from jax.experimental import pallas as pl
from jax.experimental.pallas import tpu as pltpu

F32 = jnp.float32
BF16 = jnp.bfloat16
SDS = jax.ShapeDtypeStruct
MESH = pl.DeviceIdType.MESH

HEAD_DIM = 64
N_HEADS = 8
LANES = 128
ATT_W = N_HEADS * HEAD_DIM
DIL_PATTERNS = ((128, 1), (512, 4), (2048, 16))
DIL_BLK = 128
ROPE_DIM = HEAD_DIM // 4
ROPE_THETA = 500000.0
RMS_EPS = 1e-6
NEG = -1e30
QK_SCALE = 1.0 / math.sqrt(HEAD_DIM)
ADAM_LR, ADAM_B1, ADAM_B2, ADAM_EPS, ADAM_WD, ADAM_STEP = 0.001, 0.9, 0.999, 1e-08, 0.01, 10
VMEM_LIMIT = 56 * 1024 * 1024

Z_QA, Z_KA, Z_VA, Z_QB, Z_KB, Z_VB = 0, 1, 2, 3, 4, 5
Z_W = 5120


def _cp(sem):
    return pltpu.CompilerParams(dimension_semantics=sem, vmem_limit_bytes=VMEM_LIMIT)


def _nt(a, b):
    return lax.dot_general(a, b, (((1,), (1,)), ((), ())), preferred_element_type=F32)


def _tn(a, b):
    return lax.dot_general(a, b, (((0,), (0,)), ((), ())), preferred_element_type=F32)


def _nn(a, b):
    return jnp.dot(a, b, preferred_element_type=F32)


def _lane(shape):
    return lax.broadcasted_iota(jnp.int32, shape, 1)


def _row(shape):
    return lax.broadcasted_iota(jnp.int32, shape, 0)


def rmsnorm_fwd(x, g, *, tm=512):
    s, d = x.shape

    def body(x_ref, g_ref, h_ref):
        xv = x_ref[...]
        inv = lax.rsqrt(jnp.mean(xv * xv, axis=-1, keepdims=True) + RMS_EPS)
        h_ref[...] = (xv * inv * g_ref[...]).astype(h_ref.dtype)

    return pl.pallas_call(
        body, grid=(s // tm,),
        in_specs=[pl.BlockSpec((tm, d), lambda i: (i, 0)), pl.BlockSpec((1, d), lambda i: (0, 0))],
        out_specs=pl.BlockSpec((tm, d), lambda i: (i, 0)),
        out_shape=SDS((s, d), BF16), name="rmsnorm_fwd", compiler_params=_cp(("parallel",)))(x, g)


def mm(a_views, b_views, *, nt, out_dtype, tm, tn, name):
    n_p = len(a_views)
    m = a_views[0][0].shape[0]
    n = b_views[0][0].shape[0] if nt else b_views[0][0].shape[1]

    def body(*refs):
        o_ref = refs[-1]
        acc = None
        for p in range(n_p):
            av = refs[p][...].astype(BF16)
            bv = refs[n_p + p][...].astype(BF16)
            dv = _nt(av, bv) if nt else _nn(av, bv)
            acc = dv if acc is None else acc + dv
        o_ref[...] = acc.astype(o_ref.dtype)

    in_specs = []
    for arr, w, blk in a_views:
        in_specs.append(pl.BlockSpec((tm, w), functools.partial(lambda i, j, blk: (i, blk), blk=blk)))
    for arr, w, blk in b_views:
        if nt:
            in_specs.append(pl.BlockSpec((tn, w), functools.partial(lambda i, j, blk: (j, blk), blk=blk)))
        else:
            in_specs.append(pl.BlockSpec((w, tn), lambda i, j: (0, j)))
    return pl.pallas_call(
        body, grid=(m // tm, n // tn), in_specs=in_specs,
        out_specs=pl.BlockSpec((tm, tn), lambda i, j: (i, j)),
        out_shape=SDS((m, n), out_dtype), name=name,
        compiler_params=_cp(("parallel", "parallel")))(*[a[0] for a in a_views], *[b[0] for b in b_views])


def wgrad(a_view, g, *, tk, tn, ts, name, chip_major=False, slabs=None, into=None, bf16_copy=False):
    arr, ka, blk = a_view
    s, n = g.shape
    ns = s // ts
    total, first = slabs if slabs else (n // tn, 0)
    n_into = 0 if into is None else (2 if bf16_copy else 1)

    def body(a_ref, g_ref, *rest):
        o_ref = rest[n_into]

        @pl.when(pl.program_id(2) == 0)
        def _():
            o_ref[...] = jnp.zeros_like(o_ref)

        o_ref[...] += _tn(a_ref[...].astype(BF16), g_ref[...].astype(BF16))
        if bf16_copy:
            @pl.when(pl.program_id(2) == ns - 1)
            def _():
                rest[n_into + 1][...] = o_ref[...].astype(BF16)

    if chip_major:
        out_spec = pl.BlockSpec((None, tk, tn), lambda i, j, k: (first + j, i, 0))
        shape = (total, ka, tn)
    else:
        out_spec = pl.BlockSpec((tk, tn), lambda i, j, k: (i, j))
        shape = (ka, n)
    in_specs = [pl.BlockSpec((ts, tk), lambda i, j, k: (k, blk * (ka // tk) + i)),
                pl.BlockSpec((ts, tn), lambda i, j, k: (k, j))]
    args = [arr, g]
    if into is not None:
        earlier = list(into) if bf16_copy else [into]
        in_specs += [pl.BlockSpec(memory_space=pl.ANY)] * len(earlier)
        args += earlier
    out = pl.pallas_call(
        body, grid=(ka // tk, n // tn, ns), in_specs=in_specs,
        out_specs=[out_spec, out_spec] if bf16_copy else out_spec,
        out_shape=[SDS(shape, F32), SDS(shape, BF16)] if bf16_copy else SDS(shape, F32), name=name,
        input_output_aliases={2 + k: k for k in range(n_into)},
        compiler_params=_cp(("parallel", "parallel", "arbitrary")))(*args)
    return out


def _norm_bwd_rows(dh, xh, inv, g):
    dxh = dh * g
    dx = inv * (dxh - xh * jnp.mean(dxh * xh, axis=-1, keepdims=True))
    return dx, jnp.sum((dh * xh).reshape(dh.shape[0] // 8, 8, dh.shape[1]), axis=0)


def proj_norm_res(a, w, g, xres, g_next, *, tm=512, name):
    s, k = a.shape
    d = w.shape[1]

    def body(a_ref, w_ref, g_ref, x_ref, gn_ref, y_ref, o_ref, h_ref):
        y = _nn(a_ref[...], w_ref[...])
        inv = lax.rsqrt(jnp.mean(y * y, axis=-1, keepdims=True) + RMS_EPS)
        xn = x_ref[...] + y * inv * g_ref[...]
        y_ref[...] = y
        o_ref[...] = xn
        inv_n = lax.rsqrt(jnp.mean(xn * xn, axis=-1, keepdims=True) + RMS_EPS)
        h_ref[...] = (xn * inv_n * gn_ref[...]).astype(h_ref.dtype)

    row = pl.BlockSpec((tm, d), lambda i: (i, 0))
    vec = pl.BlockSpec((1, d), lambda i: (0, 0))
    return pl.pallas_call(
        body, grid=(s // tm,),
        in_specs=[pl.BlockSpec((tm, k), lambda i: (i, 0)), pl.BlockSpec((k, d), lambda i: (0, 0)), vec, row, vec],
        out_specs=[row, row, row], out_shape=[SDS((s, d), F32), SDS((s, d), F32), SDS((s, d), BF16)], name=name,
        compiler_params=_cp(("parallel",)))(a, w, g, xres, g_next)


def proj_norm_loss(a, w, g, xres, target, *, tm=512, name):
    s, k = a.shape
    d = w.shape[1]
    n = s // tm

    def body(a_ref, w_ref, g_ref, x_ref, t_ref, do_ref, dy_ref, dg_ref, l_ref, acc):
        i = pl.program_id(0)

        @pl.when(i == 0)
        def _():
            acc[...] = jnp.zeros_like(acc)
            l_ref[...] = jnp.zeros_like(l_ref)

        y = _nn(a_ref[...], w_ref[...])
        inv = lax.rsqrt(jnp.mean(y * y, axis=-1, keepdims=True) + RMS_EPS)
        yh = y * inv
        err = x_ref[...] + yh * g_ref[...] - t_ref[...]
        dout = err * (1.0 / d)
        do_ref[...] = dout
        l_ref[...] += jnp.sum(jnp.sum(err * err, axis=1, keepdims=True), axis=0, keepdims=True)
        dy, part = _norm_bwd_rows(dout, yh, inv, g_ref[...])
        dy_ref[...] = dy.astype(dy_ref.dtype)
        acc[...] += part

        @pl.when(i == n - 1)
        def _():
            dg_ref[...] = jnp.sum(acc[...], axis=0, keepdims=True)

    row = pl.BlockSpec((tm, d), lambda i: (i, 0))
    vec = pl.BlockSpec((1, d), lambda i: (0, 0))
    return pl.pallas_call(
        body, grid=(n,),
        in_specs=[pl.BlockSpec((tm, k), lambda i: (i, 0)), pl.BlockSpec((k, d), lambda i: (0, 0)), vec, row, row],
        out_specs=[row, row, vec, pl.BlockSpec((1, 1), lambda i: (0, 0))],
        out_shape=[SDS((s, d), F32), SDS((s, d), BF16), SDS((1, d), F32), SDS((1, 1), F32)],
        scratch_shapes=[pltpu.VMEM((8, d), F32)], name=name, compiler_params=_cp(("arbitrary",)))(a, w, g, xres, target)


def mm_norm_bwd(a_views, b_views, stages, exchange=(), *, tm=256, name):
    n_p, n_s, ne = len(a_views), len(stages), len(exchange)
    s = a_views[0][0].shape[0]
    d = b_views[0][0].shape[0]
    n = s // tm
    has_res = [st[2] is not None for st in stages]

    def body(*refs):
        a_refs, b_refs = refs[:n_p], refs[n_p:2 * n_p]
        at = 2 * n_p
        st_refs = []
        for k in range(n_s):
            cnt = 3 if has_res[k] else 2
            st_refs.append(refs[at:at + cnt])
            at += cnt
        e_ins = refs[at:at + ne]
        at += ne
        dx_refs, dg_refs = refs[at:at + n_s], refs[at + n_s:at + 2 * n_s]
        at += 2 * n_s
        e_outs = refs[at:at + ne]
        at += ne
        accs = refs[at:at + n_s]
        comm = (e_ins, e_outs) + tuple(refs[at + n_s:])
        i = pl.program_id(0)

        @pl.when(i == 0)
        def _():
            for acc in accs:
                acc[...] = jnp.zeros_like(acc)
            if ne:
                _to_chips_start(*comm)

        dh = None
        for p in range(n_p):
            part = _nt(a_refs[p][...].astype(BF16), b_refs[p][...].astype(BF16))
            dh = part if dh is None else dh + part
        for k in range(n_s):
            xv = st_refs[k][0][...]
            inv = lax.rsqrt(jnp.mean(xv * xv, axis=-1, keepdims=True) + RMS_EPS)
            dx, part = _norm_bwd_rows(dh, xv * inv, inv, st_refs[k][1][...])
            if has_res[k]:
                dx = dx + st_refs[k][2][...]
            dx_refs[k][...] = dx.astype(dx_refs[k].dtype)
            accs[k][...] += part
            dh = dx

        @pl.when(i == n - 1)
        def _():
            for k in range(n_s):
                dg_refs[k][...] = jnp.sum(accs[k][...], axis=0, keepdims=True)
            if ne:
                _to_chips_finish(*comm)

    row = pl.BlockSpec((tm, d), lambda i: (i, 0))
    vec = pl.BlockSpec((1, d), lambda i: (0, 0))
    in_specs, args = [], []
    for arr, w, blk in a_views:
        in_specs.append(pl.BlockSpec((tm, w), functools.partial(lambda i, blk: (i, blk), blk=blk)))
        args.append(arr)
    for arr, w, blk in b_views:
        in_specs.append(pl.BlockSpec((d, w), functools.partial(lambda i, blk: (0, blk), blk=blk)))
        args.append(arr)
    for x, g, res, _ in stages:
        in_specs += [row, vec] + ([row] if res is not None else [])
        args += [x, g] + ([res] if res is not None else [])
    return pl.pallas_call(
        body, grid=(n,), in_specs=in_specs + [ANY] * ne,
        out_specs=[row] * n_s + [vec] * n_s + [ANY] * ne,
        out_shape=[SDS((s, d), st[3]) for st in stages] + [SDS((1, d), F32)] * n_s + _to_chips_shapes(exchange),
        scratch_shapes=[pltpu.VMEM((8, d), F32)] * n_s + (_to_chips_sems(ne) if ne else []), name=name,
        compiler_params=_cp(("arbitrary",)))(*args, *exchange)


def _split3(v):
    hi = v.astype(BF16).astype(F32)
    r = v - hi
    mid = r.astype(BF16).astype(F32)
    lo = (r - mid).astype(BF16).astype(F32)
    return hi, mid, lo


def _tri(n, upper):
    r = np.arange(n)
    m = (r[:, None] <= r[None, :]) if upper else (r[:, None] >= r[None, :])
    return jnp.asarray(m.astype(np.float32))


def fox_prep(z, fa, bfo, *, tb=512):
    s = z.shape[0]
    n = s // tb

    def body(q_ref, k_ref, v_ref, fa_ref, b_ref, tri_ref, qa_ref, ka_ref, va_ref, carry):
        @pl.when(pl.program_id(0) == 0)
        def _():
            carry[...] = jnp.zeros_like(carry)

        xv = fa_ref[...] + b_ref[...]
        logf = jnp.minimum(xv, 0.0) - jnp.log(1.0 + jnp.exp(-jnp.abs(xv)))
        csum = jnp.dot(tri_ref[...], logf, preferred_element_type=F32, precision=lax.Precision.HIGHEST) + carry[0:1, :]
        carry[0:1, :] = csum[tb - 1:tb, :]
        lane = _lane((tb, LANES))
        for h in range(N_HEADS):
            hi, mid, lo = _split3(csum[:, h:h + 1])
            pair = (h // 2) * LANES
            qv = q_ref[:, pair:pair + LANES].astype(F32)
            kv = k_ref[:, pair:pair + LANES].astype(F32)
            vv = v_ref[:, pair:pair + LANES].astype(F32)
            if h % 2:
                qv = pltpu.roll(qv, 64, axis=1)
                kv = pltpu.roll(kv, 64, axis=1)
                vv = pltpu.roll(vv, 64, axis=1)
            va_ref[:, h * LANES:(h + 1) * LANES] = jnp.where(lane < 64, vv, jnp.where(lane == 64, 1.0, 0.0)).astype(BF16)
            one = jnp.where((lane >= 67) & (lane < 70), 1.0, 0.0)
            q_x = jnp.where(lane == 64, hi, jnp.where(lane == 65, mid, jnp.where(lane == 66, lo, one)))
            one = jnp.where((lane >= 64) & (lane < 67), 1.0, 0.0)
            k_x = jnp.where(lane == 67, -hi, jnp.where(lane == 68, -mid, jnp.where(lane == 69, -lo, one)))
            qa_ref[:, h * LANES:(h + 1) * LANES] = jnp.where(lane < 64, qv * QK_SCALE, q_x).astype(BF16)
            ka_ref[:, h * LANES:(h + 1) * LANES] = jnp.where(lane < 64, kv, k_x).astype(BF16)

    return pl.pallas_call(
        body, grid=(n,),
        in_specs=[pl.BlockSpec((tb, ATT_W), lambda i: (i, Z_QA)), pl.BlockSpec((tb, ATT_W), lambda i: (i, Z_KA)),
                  pl.BlockSpec((tb, ATT_W), lambda i: (i, Z_VA)),
                  pl.BlockSpec((tb, LANES), lambda i: (i, 0)), pl.BlockSpec((1, LANES), lambda i: (0, 0)),
                  pl.BlockSpec((tb, tb), lambda i: (0, 0))],
        out_specs=[pl.BlockSpec((tb, N_HEADS * LANES), lambda i: (i, 0))] * 3,
        out_shape=[SDS((s, N_HEADS * LANES), BF16)] * 3,
        scratch_shapes=[pltpu.VMEM((8, LANES), F32)],
        name="fox_prep", compiler_params=_cp(("arbitrary",)))(z, z, z, fa, bfo, _tri(tb, False))


def _causal_pairs(n, k_major):
    if k_major:
        pairs = [(qi, kj) for kj in range(n) for qi in range(kj, n)]
    else:
        pairs = [(qi, kj) for qi in range(n) for kj in range(qi + 1)]
    return (jnp.asarray([p[0] for p in pairs], jnp.int32), jnp.asarray([p[1] for p in pairs], jnp.int32), len(pairs))


def fox_fwd(q_aug, k_aug, v_aug, gather=(), halved=(), *, t=512, hps=4):
    s = v_aug.shape[0]
    qi_arr, kj_arr, n_pairs = _causal_pairs(s // t, False)
    ng = len(gather)
    n_groups = N_HEADS // hps

    def body(qi_ref, kj_ref, q_ref, k_ref, v_ref, *rest):
        g_ins, (o_ref, lse_ref), g_outs = rest[:ng], rest[ng:ng + 2], rest[ng + 2:2 * ng + 2]
        m_scr, acc_scr = rest[2 * ng + 2:2 * ng + 4]
        comm = (g_ins, g_outs) + tuple(rest[2 * ng + 4:]) + (list(halved),)
        step = pl.program_id(1)
        qi = qi_ref[step]
        kj = kj_ref[step]
        if ng:
            @pl.when((pl.program_id(0) == 0) & (step == 0))
            def _():
                _allgather_start(*comm)

        @pl.when(kj == 0)
        def _():
            m_scr[...] = jnp.full_like(m_scr, NEG)
            acc_scr[...] = jnp.zeros_like(acc_scr)

        def update(masked):
            for i in range(hps):
                sc = _nt(q_ref[:, i * LANES:(i + 1) * LANES], k_ref[:, i * LANES:(i + 1) * LANES])
                if masked:
                    sc = jnp.where(_row((t, t)) >= _lane((t, t)), sc, NEG)
                m_prev = m_scr[i]
                m_new = jnp.maximum(m_prev, jnp.max(sc, axis=-1, keepdims=True))
                p = jnp.exp((sc - jnp.tile(m_new, (1, t // LANES))).astype(BF16))
                acc_scr[i] = jnp.exp(m_prev - m_new) * acc_scr[i] + _nn(p, v_ref[:, i * LANES:(i + 1) * LANES])
                m_scr[i] = m_new

        @pl.when(kj < qi)
        def _():
            update(False)

        @pl.when(kj == qi)
        def _():
            update(True)
            lane = _lane((t, LANES))
            for pr in range(hps // 2):
                den = [acc_scr[2 * pr + i][:, 64:65] for i in range(2)]
                o_ref[:, pr * LANES:(pr + 1) * LANES] = jnp.where(
                    lane < 64, acc_scr[2 * pr] / den[0], pltpu.roll(acc_scr[2 * pr + 1] / den[1], 64, axis=1)).astype(o_ref.dtype)
                lse_ref[:, pr * LANES:(pr + 1) * LANES] = jnp.where(
                    lane < 64, m_scr[2 * pr] + jnp.log(den[0]), m_scr[2 * pr + 1] + jnp.log(den[1]))

        if ng:
            @pl.when((pl.program_id(0) == n_groups - 1) & (step == n_pairs - 1))
            def _():
                _allgather_finish(*comm)

    wide = hps * LANES
    grid_spec = pltpu.PrefetchScalarGridSpec(
        num_scalar_prefetch=2, grid=(n_groups, n_pairs),
        in_specs=[pl.BlockSpec((t, wide), lambda hg, st, qi, kj: (qi[st], hg)),
                  pl.BlockSpec((t, wide), lambda hg, st, qi, kj: (kj[st], hg)),
                  pl.BlockSpec((t, wide), lambda hg, st, qi, kj: (kj[st], hg))] + [ANY] * ng,
        out_specs=[pl.BlockSpec((t, wide // 2), lambda hg, st, qi, kj: (qi[st], hg))] * 2 + [ANY] * ng,
        scratch_shapes=[pltpu.VMEM((hps, t, LANES), F32)] * 2 + (_allgather_sems(ng) if ng else []))
    return pl.pallas_call(
        body, grid_spec=grid_spec, out_shape=[SDS((s, ATT_W), BF16), SDS((s, ATT_W), F32)] + _allgather_shapes(gather),
        name="fox_fwd", compiler_params=_cp(("arbitrary", "arbitrary")))(qi_arr, kj_arr, q_aug, k_aug, v_aug, *gather)


def fox_bwd(q_aug, k_aug, z, dy, lse, dd, exchange=(), kind="to_chips", *, t=512, hps=4):
    s = z.shape[0]
    qi_arr, kj_arr, n_pairs = _causal_pairs(s // t, True)
    ne = len(exchange)
    n_groups = N_HEADS // hps
    x_shapes, x_sems, x_start, x_finish = EXCHANGES[kind]

    def body(qi_ref, kj_ref, q_ref, k_ref, v_ref, do_ref, lse_ref, dd_ref, *rest):
        e_ins, (dq_ref, dk_ref, dv_ref), e_outs = rest[:ne], rest[ne:ne + 3], rest[ne + 3:2 * ne + 3]
        comm = (e_ins, e_outs) + tuple(rest[2 * ne + 3:])
        step = pl.program_id(1)
        qi = qi_ref[step]
        kj = kj_ref[step]
        if ne:
            @pl.when((pl.program_id(0) == 0) & (step == 0))
            def _():
                x_start(*comm)

        @pl.when(step == 0)
        def _():
            dq_ref[...] = jnp.zeros_like(dq_ref)

        @pl.when(qi == kj)
        def _():
            dk_ref[...] = jnp.zeros_like(dk_ref)
            dv_ref[...] = jnp.zeros_like(dv_ref)

        def update(masked):
            lane = _lane((t, LANES))
            rows = pl.ds(pl.multiple_of(qi * t, t), t)
            for pr in range(hps // 2):
                pair = slice(pr * LANES, (pr + 1) * LANES)
                dov = do_ref[:, pair]
                dv_new = None
                for i in range(2):
                    head = (lane < 64) if i == 0 else (lane >= 64)
                    own = slice((2 * pr + i) * LANES, (2 * pr + i + 1) * LANES)
                    col = slice(pr * LANES + i * 64, pr * LANES + i * 64 + 1)
                    qv = q_ref[:, own]
                    kv = k_ref[:, own]
                    sc = _nt(qv, kv)
                    if masked:
                        sc = jnp.where(_row((t, t)) >= _lane((t, t)), sc, NEG)
                    p = jnp.exp(sc - lse_ref[:, col])
                    dp = _nt(jnp.where(head, dov, jnp.zeros_like(dov)), v_ref[:, pair])
                    ds = (p * (dp - dd_ref[:, col])).astype(BF16)
                    dq_ref[rows, own] += _nn(ds, kv)
                    dk_ref[:, own] += _tn(ds, qv)
                    dvi = _tn(p.astype(BF16), dov)
                    dv_new = dvi if dv_new is None else jnp.where(head, dvi, dv_new)
                dv_ref[:, pair] += dv_new

        @pl.when(kj < qi)
        def _():
            update(False)

        @pl.when(kj == qi)
        def _():
            update(True)

        if ne:
            @pl.when((pl.program_id(0) == n_groups - 1) & (step == n_pairs - 1))
            def _():
                x_finish(*comm)

    wide, half = hps * LANES, hps // 2 * LANES
    v_blk = Z_VA * ATT_W // half
    grid_spec = pltpu.PrefetchScalarGridSpec(
        num_scalar_prefetch=2, grid=(n_groups, n_pairs),
        in_specs=[pl.BlockSpec((t, wide), lambda hg, st, qi, kj: (qi[st], hg)),
                  pl.BlockSpec((t, wide), lambda hg, st, qi, kj: (kj[st], hg)),
                  pl.BlockSpec((t, half), lambda hg, st, qi, kj: (kj[st], v_blk + hg)),
                  pl.BlockSpec((t, half), lambda hg, st, qi, kj: (qi[st], hg)),
                  pl.BlockSpec((t, half), lambda hg, st, qi, kj: (qi[st], hg)),
                  pl.BlockSpec((t, half), lambda hg, st, qi, kj: (qi[st], hg))] + [ANY] * ne,
        out_specs=[pl.BlockSpec((s, wide), lambda hg, st, qi, kj: (0, hg)),
                   pl.BlockSpec((t, wide), lambda hg, st, qi, kj: (kj[st], hg)),
                   pl.BlockSpec((t, half), lambda hg, st, qi, kj: (kj[st], hg))] + [ANY] * ne,
        scratch_shapes=x_sems(ne) if ne else [])
    return pl.pallas_call(
        body, grid_spec=grid_spec,
        out_shape=[SDS((s, N_HEADS * LANES), F32), SDS((s, N_HEADS * LANES), F32), SDS((s, ATT_W), F32)]
        + x_shapes(exchange),
        name="fox_bwd", compiler_params=_cp(("arbitrary", "arbitrary")))(qi_arr, kj_arr, q_aug, k_aug, z, dy, lse, dd, *exchange)


def fox_post(dq_aug, dk_aug, dv, fa, bfo, *, tb=512):
    s = dv.shape[0]
    n = s // tb

    def body(dq_ref, dk_ref, dv_ref, fa_ref, b_ref, tri_ref, dz_ref, dfa_ref, gb_ref, carry, acc):
        i = pl.program_id(0)

        @pl.when(i == 0)
        def _():
            carry[...] = jnp.zeros_like(carry)
            acc[...] = jnp.zeros_like(acc)

        lane = _lane((tb, LANES))
        d_f = jnp.zeros((tb, LANES), F32)
        for h in range(N_HEADS):
            col = dq_ref[:, h * LANES + 64:h * LANES + 65] - dk_ref[:, h * LANES + 67:h * LANES + 68]
            d_f = jnp.where(lane == h, col, d_f)
        suffix = jnp.dot(tri_ref[...], d_f, preferred_element_type=F32, precision=lax.Precision.HIGHEST) + carry[0:1, :]
        carry[0:1, :] = suffix[0:1, :]
        xv = fa_ref[...] + b_ref[...]
        dx = suffix * (1.0 / (1.0 + jnp.exp(xv)))
        dfa_ref[...] = dx.astype(dfa_ref.dtype)
        acc[...] += jnp.sum(dx.reshape(tb // 8, 8, LANES), axis=0)
        for hp in range(4):
            for src, off, scale in ((dq_ref, 0, QK_SCALE), (dk_ref, ATT_W, 1.0)):
                even = src[:, (2 * hp) * LANES:(2 * hp + 1) * LANES]
                odd = pltpu.roll(src[:, (2 * hp + 1) * LANES:(2 * hp + 2) * LANES], 64, axis=1)
                dz_ref[:, off + hp * LANES:off + (hp + 1) * LANES] = (jnp.where(lane < 64, even, odd) * scale).astype(BF16)
        dz_ref[:, 2 * ATT_W:3 * ATT_W] = dv_ref[...].astype(BF16)

        @pl.when(i == n - 1)
        def _():
            gb_ref[...] = jnp.sum(acc[...], axis=0, keepdims=True)

    rev = lambda i: (n - 1 - i, 0)
    return pl.pallas_call(
        body, grid=(n,),
        in_specs=[pl.BlockSpec((tb, N_HEADS * LANES), rev), pl.BlockSpec((tb, N_HEADS * LANES), rev),
                  pl.BlockSpec((tb, ATT_W), rev), pl.BlockSpec((tb, LANES), rev),
                  pl.BlockSpec((1, LANES), lambda i: (0, 0)), pl.BlockSpec((tb, tb), lambda i: (0, 0))],
        out_specs=[pl.BlockSpec((tb, 3 * ATT_W), rev), pl.BlockSpec((tb, LANES), rev),
                   pl.BlockSpec((1, LANES), lambda i: (0, 0))],
        out_shape=[SDS((s, 3 * ATT_W), BF16), SDS((s, LANES), BF16), SDS((1, LANES), F32)],
        scratch_shapes=[pltpu.VMEM((8, LANES), F32), pltpu.VMEM((8, LANES), F32)],
        name="fox_post", compiler_params=_cp(("arbitrary",)))(dq_aug, dk_aug, dv, fa, bfo, _tri(tb, True))


def rope_cos_sin(s):
    half = ROPE_DIM // 2
    inv_freq = ROPE_THETA ** (-jnp.arange(half, dtype=F32) * 2.0 / ROPE_DIM)
    ang = jnp.arange(s, dtype=F32)[:, None] * inv_freq[None, :]
    return jnp.tile(jnp.cos(ang), (1, LANES // half)), jnp.tile(jnp.sin(ang), (1, LANES // half))


def _rotate(x, cos, sin, sign):
    l64 = _lane(x.shape) & (HEAD_DIM - 1)
    first = l64 < ROPE_DIM // 2
    second = (l64 >= ROPE_DIM // 2) & (l64 < ROPE_DIM)
    from_next = jnp.where(first, -sign * sin, 0.0)
    from_prev = jnp.where(second, sign * sin, 0.0)
    return (x * jnp.where(first | second, cos, 1.0) + pltpu.roll(x, LANES - 8, axis=1) * from_next
            + pltpu.roll(x, 8, axis=1) * from_prev)


def _dil_rows(base, r):
    if r == 1:
        return pl.ds(pl.multiple_of(base, DIL_BLK), DIL_BLK)
    return pl.ds(base, DIL_BLK, stride=r)


def _dil_block(idx, r, nb):
    shift = nb.bit_length() - 1
    rho = idx >> shift
    n = idx & (nb - 1)
    base = rho + n * (r * DIL_BLK)
    return _dil_rows(base, r), _dil_rows(jnp.maximum(base - r * DIL_BLK, rho), r), n > 0


def _cat(a, b):
    return jnp.concatenate([a, b], axis=0)


def _two_heads(v, first_head):
    zero = jnp.zeros_like(v)
    return _cat(jnp.where(first_head, v, zero), jnp.where(first_head, zero, v))


def _dil_bands():
    b = DIL_BLK
    q = _row((2 * b, 2 * b)) & (b - 1)
    col = _lane((2 * b, 2 * b))
    return (col < b) & (col >= q), (col >= b) & (col - b <= q)


def _dil_load_qkv(zq_ref, zk_ref, zv_ref, cos_ref, sin_ref, q_ref, k_ref, v_ref, *, chunk=512):
    def step(i, carry):
        rows = pl.ds(pl.multiple_of(i * chunk, chunk), chunk)
        cos, sin = cos_ref[rows, :], sin_ref[rows, :]
        q_ref[rows, :] = _rotate(zq_ref[rows, :].astype(F32), cos, sin, 1.0) * QK_SCALE
        k_ref[rows, :] = _rotate(zk_ref[rows, :].astype(F32), cos, sin, 1.0)
        v_ref[rows, :] = zv_ref[rows, :].astype(F32)
        return carry

    lax.fori_loop(0, q_ref.shape[0] // chunk, step, 0)


def dil_fwd_all(z, cos_t, sin_t, *, unroll=8):
    s = z.shape[0]
    b = DIL_BLK
    n_blk = s // b

    def body(zq_ref, zk_ref, zv_ref, cos_ref, sin_ref, o_ref, l_ref, q_ref, k_ref, v_ref):
        _dil_load_qkv(zq_ref, zk_ref, zv_ref, cos_ref, sin_ref, q_ref, k_ref, v_ref)
        first_head = _lane((b, LANES)) < 64
        band_prev, band_cur = _dil_bands()
        for g, (_, r) in enumerate(DIL_PATTERNS):
            nb = n_blk // r

            def group(it, carry, g=g, r=r, nb=nb):
                loaded = []
                kc = vc = None
                for u in range(unroll):
                    rows_c, rows_p, has_prev = _dil_block(it * unroll + u, r, nb)
                    if u % min(nb, unroll):
                        kp, vp = kc, vc
                    else:
                        kp, vp = k_ref[rows_p, :].astype(BF16), v_ref[rows_p, :].astype(BF16)
                    kc, vc = k_ref[rows_c, :].astype(BF16), v_ref[rows_c, :].astype(BF16)
                    state = (o_ref[rows_c, :], l_ref[rows_c, :]) if g else None
                    loaded.append((rows_c, has_prev, [q_ref[rows_c, :].astype(BF16), kp, kc, vp, vc], state))
                done = []
                for rows_c, has_prev, (qv, kp, kc, vp, vc), state in loaded:
                    sc = jnp.where(band_cur | (band_prev & has_prev), _nt(_two_heads(qv, first_head), _cat(kp, kc)), NEG)
                    m = jnp.max(sc, axis=-1, keepdims=True)
                    p = jnp.exp(sc - m)
                    den = jnp.sum(p, axis=-1, keepdims=True)
                    both = _nn(p.astype(BF16), _cat(vp, vc)) / den
                    lse2 = m + jnp.log(den)
                    ov = jnp.where(first_head, both[:b], both[b:])
                    lse = jnp.where(first_head, lse2[:b], lse2[b:])
                    if state is not None:
                        m2 = jnp.maximum(state[1], lse)
                        wp = jnp.exp(state[1] - m2)
                        wn = jnp.exp(lse - m2)
                        ov = (wp * state[0] + wn * ov) / (wp + wn)
                        lse = m2 + jnp.log(wp + wn)
                    done.append((rows_c, ov, lse))
                for rows_c, ov, lse in done:
                    o_ref[rows_c, :] = ov
                    l_ref[rows_c, :] = lse
                return carry

            lax.fori_loop(0, n_blk // unroll, group, 0)

    col_blk = lambda k: pl.BlockSpec((s, LANES), lambda hp: (0, 4 * k + hp))
    table = pl.BlockSpec((s, LANES), lambda hp: (0, 0))
    out = pl.BlockSpec((s, LANES), lambda hp: (0, hp))
    return pl.pallas_call(
        body, grid=(4,), in_specs=[col_blk(Z_QB), col_blk(Z_KB), col_blk(Z_VB), table, table], out_specs=[out, out],
        out_shape=[SDS((s, ATT_W), F32)] * 2, scratch_shapes=[pltpu.VMEM((s, LANES), F32)] * 3, name="dil_fwd",
        compiler_params=_cp(("parallel",)))(z, z, z, cos_t, sin_t)


def dil_bwd_all(z, cos_t, sin_t, dy, lse, y, exchange=(), kind="to_chips", *, unroll=8):
    s = z.shape[0]
    b = DIL_BLK
    n_blk = s // b
    ne = len(exchange)
    x_shapes, x_sems, x_start, x_finish = EXCHANGES[kind]

    def body(zq_ref, zk_ref, zv_ref, cos_ref, sin_ref, do_ref, l_ref, y_ref, *rest):
        e_ins, (gq_ref, gk_ref, gv_ref), e_outs = rest[:ne], rest[ne:ne + 3], rest[ne + 3:2 * ne + 3]
        q_ref, k_ref, v_ref, dq_ref, dk_ref, dv_ref = rest[2 * ne + 3:2 * ne + 9]
        comm = (e_ins, e_outs) + tuple(rest[2 * ne + 9:])
        if ne:
            @pl.when(pl.program_id(0) == 0)
            def _():
                x_start(*comm)

        _dil_load_qkv(zq_ref, zk_ref, zv_ref, cos_ref, sin_ref, q_ref, k_ref, v_ref)
        dq_ref[...] = jnp.zeros_like(dq_ref)
        dk_ref[...] = jnp.zeros_like(dk_ref)
        dv_ref[...] = jnp.zeros_like(dv_ref)
        first_head = _lane((b, LANES)) < 64
        band_prev, band_cur = _dil_bands()
        for _, r in DIL_PATTERNS:
            nb = n_blk // r

            def group(it, carry, r=r, nb=nb):
                loaded = []
                kc = vc = None
                for u in range(unroll):
                    rows_c, rows_p, has_prev = _dil_block(it * unroll + u, r, nb)
                    if u % min(nb, unroll):
                        kp, vp = kc, vc
                    else:
                        kp, vp = k_ref[rows_p, :].astype(BF16), v_ref[rows_p, :].astype(BF16)
                    kc, vc = k_ref[rows_c, :].astype(BF16), v_ref[rows_c, :].astype(BF16)
                    vals = [q_ref[rows_c, :].astype(BF16), kp, kc, vp, vc, do_ref[rows_c, :], l_ref[rows_c, :], y_ref[rows_c, :]]
                    loaded.append((rows_c, rows_p, has_prev, vals))
                done = []
                for rows_c, rows_p, has_prev, (qv, kp, kc, vp, vc, dof, lv, yv) in loaded:
                    q2 = _two_heads(qv, first_head)
                    do2 = _two_heads(dof.astype(BF16), first_head)
                    kcat, vcat = _cat(kp, kc), _cat(vp, vc)
                    lse2 = _cat(lv[:, 0:1], lv[:, 64:65])
                    dd2 = jnp.sum(_two_heads(dof * yv, first_head), axis=-1, keepdims=True)
                    p = jnp.exp(jnp.where(band_cur | (band_prev & has_prev), _nt(q2, kcat), NEG) - lse2)
                    ds = (p * (_nt(do2, vcat) - dd2)).astype(BF16)
                    dq2 = _nn(ds, kcat)
                    dkcat = _tn(ds, q2)
                    dvcat = _tn(p.astype(BF16), do2)
                    done.append((rows_c, rows_p, (jnp.where(first_head, dq2[:b], dq2[b:]), dkcat[:b], dkcat[b:],
                                                  dvcat[:b], dvcat[b:])))
                for rows_c, rows_p, (dq, dk_p, dk_c, dv_p, dv_c) in done:
                    dq_ref[rows_c, :] += dq
                    dk_ref[rows_p, :] += dk_p
                    dk_ref[rows_c, :] += dk_c
                    dv_ref[rows_p, :] += dv_p
                    dv_ref[rows_c, :] += dv_c
                return carry

            lax.fori_loop(0, n_blk // unroll, group, 0)

        def finish(i, carry, chunk=512):
            rows = pl.ds(pl.multiple_of(i * chunk, chunk), chunk)
            cos, sin = cos_ref[rows, :], sin_ref[rows, :]
            gq_ref[rows, :] = (_rotate(dq_ref[rows, :], cos, sin, -1.0) * QK_SCALE).astype(BF16)
            gk_ref[rows, :] = _rotate(dk_ref[rows, :], cos, sin, -1.0).astype(BF16)
            gv_ref[rows, :] = dv_ref[rows, :].astype(BF16)
            return carry

        lax.fori_loop(0, s // 512, finish, 0)
        if ne:
            @pl.when(pl.program_id(0) == 3)
            def _():
                x_finish(*comm)

    col_blk = lambda k: pl.BlockSpec((s, LANES), lambda hp: (0, 4 * k + hp))
    table = pl.BlockSpec((s, LANES), lambda hp: (0, 0))
    nat = pl.BlockSpec((s, LANES), lambda hp: (0, hp))
    return pl.pallas_call(
        body, grid=(4,), in_specs=[col_blk(Z_QB), col_blk(Z_KB), col_blk(Z_VB), table, table, nat, nat, nat] + [ANY] * ne,
        out_specs=[nat, nat, nat] + [ANY] * ne, out_shape=[SDS((s, ATT_W), BF16)] * 3 + x_shapes(exchange),
        scratch_shapes=[pltpu.VMEM((s, LANES), F32)] * 6 + (x_sems(ne) if ne else []), name="dil_bwd",
        compiler_params=_cp(("arbitrary",)))(z, z, z, cos_t, sin_t, dy, lse, y, *exchange)


def _sigmoid(v):
    return 1.0 / (1.0 + jnp.exp(-v))


def gate_mix(ya, yb, wa, wb, z, *, tm=2048, tn=512):
    s = ya.shape[0]
    d = wa.shape[1]
    ga_blk = 3 * ATT_W * 2 // tn
    gb_blk = ga_blk + d // tn

    def body(ya_ref, yb_ref, wa_ref, wb_ref, ga_ref, gb_ref, pa_ref, pb_ref, mx_ref):
        pa = _nn(ya_ref[...], wa_ref[...])
        pb = _nn(yb_ref[...].astype(BF16), wb_ref[...])
        pa_ref[...] = pa.astype(BF16)
        pb_ref[...] = pb.astype(BF16)
        mx_ref[...] = (_sigmoid(ga_ref[...].astype(F32)) * pa + _sigmoid(gb_ref[...].astype(F32)) * pb).astype(BF16)

    out = pl.BlockSpec((tm, tn), lambda i, j: (i, j))
    return pl.pallas_call(
        body, grid=(s // tm, d // tn),
        in_specs=[pl.BlockSpec((tm, ATT_W), lambda i, j: (i, 0)), pl.BlockSpec((tm, ATT_W), lambda i, j: (i, 0)),
                  pl.BlockSpec((ATT_W, tn), lambda i, j: (0, j)), pl.BlockSpec((ATT_W, tn), lambda i, j: (0, j)),
                  pl.BlockSpec((tm, tn), lambda i, j: (i, ga_blk + j)), pl.BlockSpec((tm, tn), lambda i, j: (i, gb_blk + j))],
        out_specs=[out, out, out], out_shape=[SDS((s, d), BF16)] * 3, name="gate_mix",
        compiler_params=_cp(("parallel", "parallel")))(ya, yb, wa, wb, z, z)


def mix_bwd(dy, w_o, z, pa, pb, wo_a, wo_b, ya, *, tm=512):
    s, d = dy.shape

    def body(dy_ref, wo_ref, ga_ref, gb_ref, pa_ref, pb_ref, wa_ref, wb_ref, ya_ref,
             dpa_ref, dpb_ref, dg_ref, dya_ref, dyb_ref, dd_ref):
        dm = _nt(dy_ref[...], wo_ref[...])
        sa = _sigmoid(ga_ref[...].astype(F32))
        sb = _sigmoid(gb_ref[...].astype(F32))
        dpa = (dm * sa).astype(BF16)
        dpb = (dm * sb).astype(BF16)
        dpa_ref[...] = dpa
        dpb_ref[...] = dpb
        dg_ref[:, 0:d] = (dm * pa_ref[...].astype(F32) * sa * (1.0 - sa)).astype(BF16)
        dg_ref[:, d:2 * d] = (dm * pb_ref[...].astype(F32) * sb * (1.0 - sb)).astype(BF16)
        dya = _nt(dpa, wa_ref[...]).astype(BF16)
        dya_ref[...] = dya
        dyb_ref[...] = _nt(dpb, wb_ref[...])
        lane = _lane((tm, LANES))
        for pr in range(ATT_W // LANES):
            pair = slice(pr * LANES, (pr + 1) * LANES)
            prod = dya[:, pair].astype(F32) * ya_ref[:, pair].astype(F32)
            lo = jnp.sum(jnp.where(lane < 64, prod, 0.0), axis=-1, keepdims=True)
            hi = jnp.sum(jnp.where(lane >= 64, prod, 0.0), axis=-1, keepdims=True)
            dd_ref[:, pair] = jnp.where(lane < 64, lo, hi)

    row = pl.BlockSpec((tm, d), lambda i: (i, 0))
    att = pl.BlockSpec((tm, ATT_W), lambda i: (i, 0))
    whole = lambda a: pl.BlockSpec(a.shape, lambda i: (0, 0))
    return pl.pallas_call(
        body, grid=(s // tm,),
        in_specs=[row, whole(w_o), pl.BlockSpec((tm, d), lambda i: (i, 3)), pl.BlockSpec((tm, d), lambda i: (i, 4)), row, row,
                  whole(wo_a), whole(wo_b), att],
        out_specs=[row, row, pl.BlockSpec((tm, 2 * d), lambda i: (i, 0)), att, att, att],
        out_shape=[SDS((s, d), BF16), SDS((s, d), BF16), SDS((s, 2 * d), BF16), SDS((s, ATT_W), BF16),
                   SDS((s, ATT_W), F32), SDS((s, ATT_W), F32)], name="mix_bwd",
        compiler_params=_cp(("parallel",)))(dy, w_o, z, z, pa, pb, wo_a, wo_b, ya)


GELU_C = math.sqrt(2.0 / math.pi)


def _gelu_parts(a):
    a2 = a * a
    th = jnp.tanh(a * (GELU_C + (GELU_C * 0.044715) * a2))
    half = 0.5 * a
    gelu = half + half * th
    dgelu = (0.5 + 0.5 * th) + half * (1.0 - th * th) * (GELU_C + (3.0 * GELU_C * 0.044715) * a2)
    return gelu, dgelu


def _causal_taps(u, before):
    row = _row(u.shape)
    r1 = jnp.where(row == 0, before[7:8, :], pltpu.roll(u, 1, axis=0))
    r2 = jnp.where(row == 0, before[6:7, :], jnp.where(row == 1, before[7:8, :], pltpu.roll(u, 2, axis=0)))
    return r1, r2


def ffn_up(h, wa, wb, cw, cb, *, tm=2048, tn=256):
    s, d = h.shape
    f = wa.shape[1]
    nj = f // tn

    def body(h_ref, wa_ref, wb_ref, cwa_ref, cwb_ref, cba_ref, cbb_ref, ua_ref, ub_ref, ca_ref, cbo_ref, m_ref, carry):
        @pl.when(pl.program_id(1) == 0)
        def _():
            carry[...] = jnp.zeros_like(carry)

        conv = []
        for k, (w_ref, cw_ref, cb_ref, u_ref, c_ref) in enumerate(((wa_ref, cwa_ref, cba_ref, ua_ref, ca_ref),
                                                                   (wb_ref, cwb_ref, cbb_ref, ub_ref, cbo_ref))):
            u16 = _nn(h_ref[...], w_ref[...]).astype(BF16)
            u_ref[...] = u16
            u = u16.astype(F32)
            r1, r2 = _causal_taps(u, carry[k])
            carry[k] = u[tm - 8:tm, :]
            c16 = (cw_ref[0:1, :] * r2 + cw_ref[1:2, :] * r1 + cw_ref[2:3, :] * u + cb_ref[...]).astype(BF16)
            c_ref[...] = c16
            conv.append(c16.astype(F32))
        m_ref[...] = (_gelu_parts(conv[0])[0] * conv[1]).astype(BF16)

    out = pl.BlockSpec((tm, tn), lambda j, i: (i, j))
    return pl.pallas_call(
        body, grid=(nj, s // tm),
        in_specs=[pl.BlockSpec((tm, d), lambda j, i: (i, 0)),
                  pl.BlockSpec((d, tn), lambda j, i: (0, j)), pl.BlockSpec((d, tn), lambda j, i: (0, j)),
                  pl.BlockSpec((3, tn), lambda j, i: (0, j)), pl.BlockSpec((3, tn), lambda j, i: (0, nj + j)),
                  pl.BlockSpec((1, tn), lambda j, i: (0, j)), pl.BlockSpec((1, tn), lambda j, i: (0, nj + j))],
        out_specs=[out] * 5, out_shape=[SDS((s, f), BF16)] * 5,
        scratch_shapes=[pltpu.VMEM((2, 8, tn), F32)], name="ffn_up",
        compiler_params=_cp(("parallel", "arbitrary")))(h, wa, wb, cw, cw, cb, cb)


def ffn_bwd(dm, ua, ub, ca, cbo, cw, *, tm=2048, tn=256):
    s, f = dm.shape
    nj = f // tn
    ni = s // tm

    def body(dm_ref, ua_ref, ub_ref, ca_ref, cbo_ref, cwa_ref, cwb_ref, dua_ref, dub_ref, ga_ref, gb_ref, carry):
        @pl.when(pl.program_id(1) == 0)
        def _():
            carry[...] = jnp.zeros_like(carry)
            ga_ref[...] = jnp.zeros_like(ga_ref)
            gb_ref[...] = jnp.zeros_like(gb_ref)

        row = _row((tm, tn))
        dmv = dm_ref[...].astype(F32)
        gelu, dgelu = _gelu_parts(ca_ref[...].astype(F32))
        dcs = (dmv * cbo_ref[...].astype(F32) * dgelu, dmv * gelu)
        for k, (dc, u_ref, cw_ref, du_ref, g_ref) in enumerate(((dcs[0], ua_ref, cwa_ref, dua_ref, ga_ref),
                                                                (dcs[1], ub_ref, cwb_ref, dub_ref, gb_ref))):
            u = u_ref[...].astype(F32)
            after = carry[k]
            n1 = jnp.where(row == tm - 1, after[0:1, :], pltpu.roll(dc, tm - 1, axis=0))
            n2 = jnp.where(row == tm - 2, after[0:1, :], jnp.where(row == tm - 1, after[1:2, :], pltpu.roll(dc, tm - 2, axis=0)))
            g_ref[0:1, :] += jnp.sum(n2 * u, axis=0, keepdims=True)
            g_ref[1:2, :] += jnp.sum(n1 * u, axis=0, keepdims=True)
            g_ref[2:3, :] += jnp.sum(dc * u, axis=0, keepdims=True)
            g_ref[3:4, :] += jnp.sum(dc, axis=0, keepdims=True)
            du_ref[...] = (cw_ref[2:3, :] * dc + cw_ref[1:2, :] * n1 + cw_ref[0:1, :] * n2).astype(BF16)
            carry[k] = dc[0:8, :]

    tile = pl.BlockSpec((tm, tn), lambda j, i: (ni - 1 - i, j))
    gspec = pl.BlockSpec((8, tn), lambda j, i: (0, j))
    return pl.pallas_call(
        body, grid=(nj, ni),
        in_specs=[tile] * 5 + [pl.BlockSpec((3, tn), lambda j, i: (0, j)), pl.BlockSpec((3, tn), lambda j, i: (0, nj + j))],
        out_specs=[tile, tile, gspec, gspec],
        out_shape=[SDS((s, f), BF16), SDS((s, f), BF16), SDS((8, f), F32), SDS((8, f), F32)],
        scratch_shapes=[pltpu.VMEM((2, 8, tn), F32)], name="ffn_bwd",
        compiler_params=_cp(("parallel", "arbitrary")))(dm, ua, ub, ca, cbo, cw, cw)


def adamw(w, g, m, v, *, name, tr=None):
    r = w.shape[0]
    rest = w.shape[1:]
    if tr is None:
        tr = r
        for cand in (256, 128, 64, 32, 16, 8):
            if r % cand == 0:
                tr = cand
                break

    def body(w_ref, g_ref, m_ref, v_ref, d_ref, nm_ref, nv_ref):
        gv = g_ref[...]
        mn = ADAM_B1 * m_ref[...] + (1.0 - ADAM_B1) * gv
        vn = ADAM_B2 * v_ref[...] + (1.0 - ADAM_B2) * (gv * gv)
        m_hat = mn / (1.0 - ADAM_B1 ** ADAM_STEP)
        v_hat = vn / (1.0 - ADAM_B2 ** ADAM_STEP)
        d_ref[...] = -ADAM_LR * (m_hat / (jnp.sqrt(v_hat) + ADAM_EPS) + ADAM_WD * w_ref[...])
        nm_ref[...] = mn
        nv_ref[...] = vn

    blk = pl.BlockSpec((tr,) + rest, lambda i: (i,) + (0,) * len(rest))
    return pl.pallas_call(body, grid=(r // tr,), in_specs=[blk] * 4, out_specs=[blk] * 3, out_shape=[SDS(w.shape, F32)] * 3,
                          name=name, compiler_params=_cp(("parallel",)))(w, g, m, v)


def adamw_rows_view(w, g_mine, g_full, m, v, c_arr, *, name, tc=256):
    r, _, c = w.shape
    per_half = c // 2 // tc

    def body(c_ref, w_ref, gm_ref, gf_ref, m_ref, v_ref, d_ref, nm_ref, nv_ref, go_ref):
        mine = (pl.program_id(0) >> (per_half.bit_length() - 1)) == c_ref[0]
        gv = jnp.where(mine, gm_ref[...], gf_ref[...])
        mn = ADAM_B1 * m_ref[:, 0, :] + (1.0 - ADAM_B1) * gv
        vn = ADAM_B2 * v_ref[:, 0, :] + (1.0 - ADAM_B2) * (gv * gv)
        m_hat = mn / (1.0 - ADAM_B1 ** ADAM_STEP)
        v_hat = vn / (1.0 - ADAM_B2 ** ADAM_STEP)
        d_ref[:, 0, :] = -ADAM_LR * (m_hat / (jnp.sqrt(v_hat) + ADAM_EPS) + ADAM_WD * w_ref[:, 0, :])
        nm_ref[:, 0, :] = mn
        nv_ref[:, 0, :] = vn
        go_ref[:, 0, :] = gv

    b3 = pl.BlockSpec((r, 1, tc), lambda i, c_ref: (0, 0, i))
    own = pl.BlockSpec((r, tc), lambda i, c_ref: (0, jnp.clip(i - c_ref[0] * per_half, 0, per_half - 1)))
    full = pl.BlockSpec((r, tc), lambda i, c_ref: (0, i))
    grid_spec = pltpu.PrefetchScalarGridSpec(num_scalar_prefetch=1, grid=(c // tc,), in_specs=[b3, own, full, b3, b3],
                                             out_specs=[b3] * 4)
    return pl.pallas_call(body, grid_spec=grid_spec, out_shape=[SDS(w.shape, F32)] * 4, name=name,
                          compiler_params=_cp(("parallel",)))(c_arr, w, g_mine, g_full, m, v)


ANY = pl.BlockSpec(memory_space=pl.ANY)
ICI_KINDS = ("x", "y", "xy")


def _coords():
    return lax.axis_index("x"), lax.axis_index("y"), lax.axis_index("c")


def _peer(kind, x, y, c):
    if kind == "c":
        return (x, y, 1 - c)
    if kind == "x":
        return (1 - x, y, c)
    if kind == "y":
        return (x, 1 - y, c)
    return (1 - x, 1 - y, c)


def _chip_of(p):
    return 2 * p[0] + p[1]


def _half(rows, which):
    h = rows // 2
    return pl.ds(pl.multiple_of(which * h, 16), h)


def _remote(src, dst, send_sem, recv_sem, to):
    return pltpu.make_async_remote_copy(src_ref=src, dst_ref=dst, send_sem=send_sem, recv_sem=recv_sem,
                                        device_id=to, device_id_type=MESH)


def allgather_balanced(shard, *, name):
    r, cols = shard.shape
    h, q = r // 2, r // 4

    def body(in_ref, out_ref, send_sems, recv_sems):
        x, y, c = _coords()
        me, sibling = (x, y, c), (x, y, 1 - c)
        nbr = ((1 - x, y, c), (x, 1 - y, c))
        chip = (2 * (1 - x) + y, 2 * x + (1 - y), 2 * (1 - x) + (1 - y))
        quarter = lambda core, i: pl.ds(pl.multiple_of(core * h + i * q, 16), q)
        sent = []

        def go(src, dst, slot, to):
            cp = _remote(src, dst, send_sems.at[slot], recv_sems.at[slot], to)
            cp.start()
            sent.append(cp)

        def landed(region, slot):
            _remote(region, region, send_sems.at[slot], recv_sems.at[slot], me).wait_recv()

        for i in range(2):
            for k in range(2):
                qi = k if i == 0 else 1 - k
                go(in_ref.at[quarter(c, qi)], out_ref.at[2 * x + y, quarter(c, qi)], 2 * k + qi, nbr[k])
        for k in range(2):
            piece = out_ref.at[chip[k], quarter(c, k)]
            landed(piece, 2 * k + k)
            go(piece, piece, 4 + k, nbr[1 - k])
            go(piece, piece, 6 + 2 * k + k, sibling)
        for k in range(2):
            piece = out_ref.at[chip[k], quarter(c, 1 - k)]
            landed(piece, 2 * k + 1 - k)
            go(piece, piece, 6 + 2 * k + 1 - k, sibling)
        for k in range(2):
            piece = out_ref.at[chip[2], quarter(c, k)]
            landed(piece, 4 + k)
            go(piece, piece, 10 + k, sibling)
        for k in range(2):
            for i in range(2):
                landed(out_ref.at[chip[k], quarter(1 - c, i)], 6 + 2 * k + i)
            landed(out_ref.at[chip[2], quarter(1 - c, k)], 10 + k)
        for cp in sent:
            cp.wait_send()

    return pl.pallas_call(
        body, in_specs=[ANY], out_specs=ANY, out_shape=SDS((4,) + shard.shape, shard.dtype),
        scratch_shapes=[pltpu.SemaphoreType.DMA((12,)), pltpu.SemaphoreType.DMA((12,))], name=name)(shard)


def _allgather_shapes(shards):
    return [SDS((4,) + a.shape, a.dtype) for a in shards]


def _allgather_sems(n):
    return [pltpu.SemaphoreType.DMA((n, 6)), pltpu.SemaphoreType.DMA((n, 6))]


def _allgather_rows(ref, is_halved, which):
    r = ref.shape[0]
    return _half(r, which) if is_halved else pl.ds(0, r)


def _allgather_first(ins, outs, send_sems, recv_sems, halved):
    x, y, c = _coords()
    my_chip = 2 * x + y
    cps = []
    for w in range(len(ins)):
        rows = _allgather_rows(ins[w], halved[w], c)
        for k, kind in enumerate(ICI_KINDS):
            cps.append(_remote(ins[w].at[rows], outs[w].at[my_chip, rows], send_sems.at[w, k], recv_sems.at[w, k],
                               _peer(kind, x, y, c)))
    return cps


def _allgather_start(ins, outs, send_sems, recv_sems, halved):
    for cp in _allgather_first(ins, outs, send_sems, recv_sems, halved):
        cp.start()


def _allgather_finish(ins, outs, send_sems, recv_sems, halved):
    x, y, c = _coords()
    me = (x, y, c)
    second = []
    for w in range(len(ins)):
        for k, kind in enumerate(ICI_KINDS):
            landed = outs[w].at[_chip_of(_peer(kind, x, y, c)), _allgather_rows(ins[w], halved[w], c)]
            _remote(landed, landed, send_sems.at[w, k], recv_sems.at[w, k], me).wait_recv()
            if halved[w]:
                cp = _remote(landed, landed, send_sems.at[w, 3 + k], recv_sems.at[w, 3 + k], _peer("c", x, y, c))
                cp.start()
                second.append(cp)
    for w in range(len(ins)):
        if halved[w]:
            for k, kind in enumerate(ICI_KINDS):
                other = outs[w].at[_chip_of(_peer(kind, x, y, c)), _allgather_rows(ins[w], True, 1 - c)]
                _remote(other, other, send_sems.at[w, 3 + k], recv_sems.at[w, 3 + k], me).wait_recv()
    for cp in _allgather_first(ins, outs, send_sems, recv_sems, halved) + second:
        cp.wait_send()


def _half_of(ref, by_cols, which):
    lead = (slice(None),) * (len(ref.shape) - 2)
    if by_cols:
        h = ref.shape[-1] // 2
        return ref.at[lead + (slice(None), pl.ds(pl.multiple_of(which * h, LANES), h))]
    return ref.at[lead + (_half(ref.shape[-2], which),)]


def _half_shape(shape, by_cols):
    return shape[:-1] + (shape[-1] // 2,) if by_cols else shape[:-2] + (shape[-2] // 2, shape[-1])


def grads_to_sibling(gs, by_cols, *, name):
    n = len(gs)

    def body(*refs):
        ins, outs = refs[:n], refs[n:2 * n]
        send_sems, recv_sems = refs[2 * n:]
        x, y, c = _coords()
        cps = []
        for w in range(n):
            cp = _remote(_half_of(ins[w], by_cols[w], 1 - c), outs[w], send_sems.at[w], recv_sems.at[w], _peer("c", x, y, c))
            cp.start()
            cps.append(cp)
        for cp in cps:
            cp.wait()

    return pl.pallas_call(
        body, in_specs=[ANY] * n, out_specs=[ANY] * n,
        out_shape=[SDS(_half_shape(a.shape, bc), a.dtype) for a, bc in zip(gs, by_cols)],
        scratch_shapes=[pltpu.SemaphoreType.DMA((n,)), pltpu.SemaphoreType.DMA((n,))], name=name)(*gs)


def _to_chips_shapes(ps):
    return [SDS((3,) + a.shape[1:], a.dtype) for a in ps]


def _to_chips_sems(n):
    return [pltpu.SemaphoreType.DMA((n, 3)), pltpu.SemaphoreType.DMA((n, 3))]


def _to_chips_copies(ins, outs, send_sems, recv_sems):
    x, y, c = _coords()
    cps = []
    for w in range(len(ins)):
        for k, kind in enumerate(ICI_KINDS):
            to = _peer(kind, x, y, c)
            cps.append(_remote(ins[w].at[_chip_of(to)], outs[w].at[k], send_sems.at[w, k], recv_sems.at[w, k], to))
    return cps


def _to_chips_start(ins, outs, send_sems, recv_sems):
    for cp in _to_chips_copies(ins, outs, send_sems, recv_sems):
        cp.start()


def _to_chips_finish(ins, outs, send_sems, recv_sems):
    for cp in _to_chips_copies(ins, outs, send_sems, recv_sems):
        cp.wait()


def _to_owners_shapes(ps):
    return [SDS((7, a.shape[1] // 2, a.shape[2]), a.dtype) for a in ps]


def _to_owners_sems(n):
    return [pltpu.SemaphoreType.DMA((n, 7)), pltpu.SemaphoreType.DMA((n, 7))]


def _to_owners_copies(ins, outs, send_sems, recv_sems):
    x, y, c = _coords()
    cps = []
    for w in range(len(ins)):
        rows = ins[w].shape[1]
        for k, kind in enumerate(ICI_KINDS):
            px, py, _ = _peer(kind, x, y, c)
            for h in range(2):
                cps.append(_remote(ins[w].at[2 * px + py, _half(rows, h)], outs[w].at[2 * k + c],
                                   send_sems.at[w, 2 * k + h], recv_sems.at[w, 2 * k + c], (px, py, h)))
        cps.append(_remote(ins[w].at[2 * x + y, _half(rows, 1 - c)], outs[w].at[6], send_sems.at[w, 6], recv_sems.at[w, 6],
                           _peer("c", x, y, c)))
    return cps


def _to_owners_start(ins, outs, send_sems, recv_sems):
    for cp in _to_owners_copies(ins, outs, send_sems, recv_sems):
        cp.start()


def _to_owners_finish(ins, outs, send_sems, recv_sems):
    for cp in _to_owners_copies(ins, outs, send_sems, recv_sems):
        cp.wait_send()
    for w in range(len(ins)):
        for slot in range(7):
            got = outs[w].at[slot]
            _remote(got, got, send_sems.at[w, slot], recv_sems.at[w, slot], _coords()).wait_recv()


EXCHANGES = {"to_chips": (_to_chips_shapes, _to_chips_sems, _to_chips_start, _to_chips_finish),
             "to_owners": (_to_owners_shapes, _to_owners_sems, _to_owners_start, _to_owners_finish)}


def halves_to_full(hs, by_cols, *, name):
    n = len(hs)

    def body(*refs):
        ins, outs = refs[:n], refs[n:2 * n]
        send_sems, recv_sems = refs[2 * n:]
        x, y, c = _coords()
        cps = []
        for w in range(n):
            cp = _remote(ins[w], _half_of(outs[w], by_cols[w], c), send_sems.at[w], recv_sems.at[w], _peer("c", x, y, c))
            cp.start()
            cps.append(cp)
        for cp in cps:
            cp.wait()

    return pl.pallas_call(
        body, in_specs=[ANY] * n, out_specs=[ANY] * n,
        out_shape=[SDS((a.shape[0], 2 * a.shape[1]) if bc else (2 * a.shape[0], a.shape[1]), a.dtype)
                   for a, bc in zip(hs, by_cols)],
        scratch_shapes=[pltpu.SemaphoreType.DMA((n,)), pltpu.SemaphoreType.DMA((n,))],
        name=name)(*hs)


def _row_tile(rows):
    for cand in (256, 192, 176, 128, 64, 32, 16):
        if rows % cand == 0:
            return cand
    return rows


def chip_sum(g, recv, c_arr, by_cols, *, name):
    _, r, cols = g.shape

    def body(c_ref, g_ref, r_ref, f_ref, b_ref):
        tot = g_ref[...] + r_ref[...]
        f_ref[...] = tot
        b_ref[...] = tot.astype(BF16)

    if by_cols:
        tc = 2 * LANES
        nblk = cols // 2 // tc
        shape = (4, r, cols // 2)
        blk = pl.BlockSpec((None, r, tc), lambda j, i, c_ref: (j, 0, i))
        mine = pl.BlockSpec((None, r, tc), lambda j, i, c_ref: (j, 0, c_ref[0] * nblk + i))
    else:
        tr = _row_tile(r // 2)
        nblk = r // 2 // tr
        shape = (4, r // 2, cols)
        blk = pl.BlockSpec((None, tr, cols), lambda j, i, c_ref: (j, i, 0))
        mine = pl.BlockSpec((None, tr, cols), lambda j, i, c_ref: (j, c_ref[0] * nblk + i, 0))
    grid_spec = pltpu.PrefetchScalarGridSpec(num_scalar_prefetch=1, grid=(4, nblk), in_specs=[mine, blk], out_specs=[blk, blk])
    return pl.pallas_call(body, grid_spec=grid_spec, out_shape=[SDS(shape, F32), SDS(shape, BF16)],
                          name=name, compiler_params=_cp(("parallel", "parallel")))(c_arr, g, recv)


def final_sum(pf, recv, chip_arr, *, name):
    _, h, cols = pf.shape
    tr = _row_tile(h)

    def body(chip_ref, p_ref, r_ref, o_ref):
        o_ref[...] = ((p_ref[...] + r_ref[0].astype(F32)) + r_ref[1].astype(F32)) + r_ref[2].astype(F32)

    grid_spec = pltpu.PrefetchScalarGridSpec(
        num_scalar_prefetch=1, grid=(h // tr,),
        in_specs=[pl.BlockSpec((None, tr, cols), lambda i, chip_ref: (chip_ref[0], i, 0)),
                  pl.BlockSpec((3, tr, cols), lambda i, chip_ref: (0, i, 0))],
        out_specs=pl.BlockSpec((tr, cols), lambda i, chip_ref: (i, 0)))
    return pl.pallas_call(body, grid_spec=grid_spec, out_shape=SDS((h, cols), F32), name=name,
                          compiler_params=_cp(("parallel",)))(chip_arr, pf, recv)


def owner_sum(g, recv, pos_arr, *, name):
    _, r, cols = g.shape
    h = r // 2
    tr = _row_tile(h)
    nblk = h // tr

    def body(pos_ref, g_ref, r_ref, o_ref):
        tot = g_ref[...]
        for slot in range(7):
            tot = tot + r_ref[slot].astype(F32)
        o_ref[...] = tot

    grid_spec = pltpu.PrefetchScalarGridSpec(
        num_scalar_prefetch=1, grid=(nblk,),
        in_specs=[pl.BlockSpec((None, tr, cols), lambda i, pos: (pos[0], pos[1] * nblk + i, 0)),
                  pl.BlockSpec((7, tr, cols), lambda i, pos: (0, i, 0))],
        out_specs=pl.BlockSpec((tr, cols), lambda i, pos: (i, 0)))
    return pl.pallas_call(body, grid_spec=grid_spec, out_shape=SDS((h, cols), F32), name=name,
                          compiler_params=_cp(("parallel",)))(pos_arr, g, recv)


def allreduce_small(v, *, name):
    rws, cols = v.shape

    def body(v_ref, all_ref, sum_ref, send_sems, recv_sems, local_sem):
        x, y, c = _coords()
        me, sibling = (x, y, c), (x, y, 1 - c)
        chips = [(1 - x, y), (x, 1 - y), (1 - x, 1 - y)]

        def rows(px, py, pc):
            return all_ref.at[pl.ds(pl.multiple_of((4 * px + 2 * py + pc) * rws, 8), rws), :]

        def copy(k, block, to, src=None):
            return _remote(rows(*block) if src is None else src, rows(*block), send_sems.at[k], recv_sems.at[k], to)

        mine = pltpu.make_async_copy(v_ref, rows(*me), local_sem)
        mine.start()
        first = [copy(0, me, sibling, src=v_ref)]
        first += [copy(1 + j, me, (*chip, c), src=v_ref) for j, chip in enumerate(chips)]
        for cp in first:
            cp.start()
        passed = [copy(4 + j, (*chip, c), sibling) for j, chip in enumerate(chips)]
        for j, chip in enumerate(chips):
            copy(1 + j, (*chip, c), me).wait_recv()
            passed[j].start()
        copy(0, sibling, me).wait_recv()
        for j, chip in enumerate(chips):
            copy(4 + j, (*chip, 1 - c), me).wait_recv()
        for cp in first + passed:
            cp.wait_send()
        mine.wait()
        tot = all_ref[0:rws, :]
        for dev in range(1, 8):
            tot = tot + all_ref[dev * rws:(dev + 1) * rws, :]
        sum_ref[...] = tot

    vm = pl.BlockSpec(memory_space=pltpu.VMEM)
    return pl.pallas_call(
        body, in_specs=[vm], out_specs=[vm, vm],
        out_shape=[SDS((8 * rws, cols), v.dtype), SDS((rws, cols), v.dtype)],
        scratch_shapes=[pltpu.SemaphoreType.DMA((7,)), pltpu.SemaphoreType.DMA((7,)), pltpu.SemaphoreType.DMA],
        name=name)(v)[1]


def _pack_rows(parts, rows):
    out = []
    for a, r in zip(parts, rows):
        flat = a.reshape(-1)
        flat = jnp.pad(flat, (0, r * LANES - flat.shape[0]))
        out.append(flat.reshape(r, LANES))
    return jnp.concatenate(out, axis=0)


def _unpack_rows(packed, shapes, rows):
    out, at = [], 0
    for shp, r in zip(shapes, rows):
        size = int(np.prod(shp))
        out.append(packed[at:at + r].reshape(-1)[:size].reshape(shp))
        at += r
    return out


def kernel(x, g_pre_mix, w_in, b_forget, w_o_fox, w_o_dil, w_out, g_post_mix, g_pre_ffn, w_up, conv_w, conv_b, w_down, g_post_ffn, loss_target, m_g_pre_mix, m_w_in, m_b_forget, m_w_o_fox, m_w_o_dil, m_w_out, m_g_post_mix, m_g_pre_ffn, m_w_up, m_conv_w, m_conv_b, m_w_down, m_g_post_ffn, v_g_pre_mix, v_w_in, v_b_forget, v_w_o_fox, v_w_o_dil, v_w_out, v_g_post_mix, v_g_pre_ffn, v_w_up, v_conv_w, v_conv_b, v_w_down, v_g_post_ffn):
    xi, yi, ci = _coords()
    chip = 2 * xi + yi
    c_arr = jnp.reshape(ci, (1,)).astype(jnp.int32)
    chip_arr = jnp.reshape(chip, (1,)).astype(jnp.int32)
    xs = x[0]
    target = loss_target[0]
    s, d = xs.shape
    f_half = w_down.shape[1] * 4
    cols_in = w_in.shape[2]

    big = (w_in, w_o_fox, w_o_dil, w_out, w_up, w_down)
    shards = [w[0].astype(BF16) for w in big]
    a_in = allgather_balanced(shards[0], name="allgather_w_in")
    w_in_full = jnp.concatenate([jnp.where(chip == j, shards[0], a_in[j]) for j in range(4)], axis=1)
    nf = N_HEADS
    e_a, e_b = 3 * ATT_W, 3 * ATT_W + nf
    wz = jnp.concatenate([w_in_full[:, :e_a], w_in_full[:, e_b:]], axis=1)
    wf = jnp.pad(w_in_full[:, e_a:e_b], ((0, 0), (0, LANES - nf)))
    cb = conv_b
    bfo = jnp.pad(b_forget, ((0, 0), (0, LANES - nf)))

    h1 = rmsnorm_fwd(xs, g_pre_mix)
    z = mm([(h1, d, 0)], [(wz, d, 0)], nt=False, out_dtype=BF16, tm=s, tn=512, name="in_proj")
    fa = mm([(h1, d, 0)], [(wf, d, 0)], nt=False, out_dtype=F32, tm=1024, tn=LANES, name="in_proj_forget")
    q_aug, k_aug, v_aug = fox_prep(z, fa, bfo)
    later = shards[1:] + [conv_w[0]]
    ya, lse_a, *late = fox_fwd(q_aug, k_aug, v_aug, gather=later, halved=[True] * 5 + [False], hps=N_HEADS)
    a_of, a_od, a_out, a_up, a_down, a_cw = [
        lax.dynamic_update_index_in_dim(a4, own, chip, 0) for a4, own in zip(late, later)]
    cw = jnp.concatenate([a_cw[j] for j in range(4)], axis=1)
    wo_a = jnp.concatenate([a_of[j] for j in range(4)], axis=1)
    wo_b = jnp.concatenate([a_od[j] for j in range(4)], axis=1)
    w_o = a_out.reshape(d, d)
    w_dn = a_down.reshape(f_half, d)
    wu_a = jnp.concatenate([a_up[0], a_up[1]], axis=1)
    wu_b = jnp.concatenate([a_up[2], a_up[3]], axis=1)
    cos_t, sin_t = rope_cos_sin(s)
    yb, lse_b = dil_fwd_all(z, cos_t, sin_t)
    pa, pb, mixed = gate_mix(ya, yb, wo_a, wo_b, z)
    y1, x1, h2 = proj_norm_res(mixed, w_o, g_post_mix, xs, g_pre_ffn, tm=1024, name="out_proj")
    ua, ub, conv_a, conv_bh, mid = ffn_up(h2, wu_a, wu_b, cw, cb)
    dout, dy2, gg_post_ffn, sq = proj_norm_loss(mid, w_dn, g_post_ffn, x1, target, name="down_proj")
    loss = lax.psum(0.5 * sq[0, 0] / d, ("x", "y", "c"))

    dmid = mm([(dy2, d, 0)], [(w_dn, d, 0)], nt=True, out_dtype=BF16, tm=2048, tn=f_half // 2, name="down_dgrad")
    dw_down, dw_down16 = wgrad((mid, f_half, 0), dy2, tk=f_half // 2, tn=1024, ts=2048, name="down_wgrad", bf16_copy=True)
    dua, dub, gc_a, gc_b = ffn_bwd(dmid, ua, ub, conv_a, conv_bh, cw)
    dx1, dy1, gg_pre_ffn, gg_post_mix = mm_norm_bwd(
        [(dua, f_half, 0), (dub, f_half, 0)], [(wu_a, f_half, 0), (wu_b, f_half, 0)],
        [(x1, g_pre_ffn, dout, F32), (y1, g_post_mix, None, BF16)], name="up_dgrad")
    dw_up = None
    for k, du in enumerate((dua, dub)):
        dw_up = wgrad((h2, d, 0), du, tk=1024, tn=f_half // 2, ts=2048, name=f"up_wgrad_{k}", chip_major=True,
                      slabs=(4, 2 * k), into=dw_up, bf16_copy=True)
    g_ffn = [(dw_up[0], dw_up[1]), (dw_down.reshape(4, f_half // 4, d), dw_down16.reshape(4, f_half // 4, d))]
    dw_out, dw_out16 = wgrad((mixed, d, 0), dy1, tk=1024, tn=1024, ts=2048, name="out_wgrad", bf16_copy=True)
    dpa, dpb, dz_g, dya, dyb, dd_a = mix_bwd(dy1, w_o, z, pa, pb, wo_a, wo_b, ya)
    by_chip_cols = lambda a: jnp.stack([a[:, j * (d // 4):(j + 1) * (d // 4)] for j in range(4)], axis=0)
    dw_of = [by_chip_cols(a) for a in wgrad((ya, ATT_W, 0), dpa, tk=ATT_W, tn=d, ts=1024, name="fox_o_wgrad", bf16_copy=True)]
    dw_od = [by_chip_cols(a) for a in wgrad((yb, ATT_W, 0), dpb, tk=ATT_W, tn=d, ts=1024, name="dil_o_wgrad", bf16_copy=True)]
    g_mix = [dw_of, dw_od, (dw_out.reshape(4, d // 4, d), dw_out16.reshape(4, d // 4, d))]
    dq_aug, dk_aug, dv_a, *got_ffn = fox_bwd(q_aug, k_aug, z, dya, lse_a, dd_a, exchange=[g[1] for g in g_ffn], kind="to_owners")
    dz_a, dfa, gg_bf = fox_post(dq_aug, dk_aug, dv_a, fa, bfo)
    *dz_b, got_of, got_od, got_out = dil_bwd_all(z, cos_t, sin_t, dyb, lse_b, yb, exchange=[g[1] for g in g_mix],
                                                 kind="to_owners")
    got_mix = [got_of, got_od, got_out]
    dwt_a = wgrad((dz_a, e_a, 0), h1, tk=e_a // 2, tn=d, ts=1024, name="in_wgrad_a")
    dwt_b = [wgrad((part, ATT_W, 0), h1, tk=ATT_W, tn=d, ts=1024, name=f"in_wgrad_b{k}") for k, part in enumerate(dz_b)]
    dwt_g = wgrad((dz_g, 2 * d, 0), h1, tk=d, tn=d, ts=1024, name="in_wgrad_g")
    dwt_f = wgrad((dfa, LANES, 0), h1, tk=LANES, tn=d, ts=1024, name="in_wgrad_f")
    dwt_full = jnp.concatenate([dwt_a, dwt_f[:nf], *dwt_b, dwt_g], axis=0)
    dw_in = jnp.stack([dwt_full[j * cols_in:(j + 1) * cols_in] for j in range(4)], axis=0)
    from_sib = grads_to_sibling([dw_in], [True], name="grads_to_sibling_in")
    sum_in = chip_sum(dw_in, from_sib[0], c_arr, True, name="chip_sum_w_in")
    grad_x, gg_pre_mix, got_in = mm_norm_bwd(
        [(dz_a, e_a, 0), *[(part, ATT_W, 0) for part in dz_b], (dz_g, d, 0), (dz_g, d, 1), (dfa, LANES, 0)],
        [(wz, e_a, 0), *[(wz, ATT_W, Z_QB + k) for k in range(3)], (wz, d, 3), (wz, d, 4), (wf, LANES, 0)],
        [(xs, g_pre_mix, dx1, F32)], exchange=[sum_in[1]], name="in_dgrad")

    names = ("w_in", "w_o_fox", "w_o_dil", "w_out", "w_up", "w_down")
    pos_arr = jnp.concatenate([chip_arr, c_arr])
    halves = [final_sum(sum_in[0], got_in, chip_arr, name="final_sum_w_in")] + [
        owner_sum(g[0], got, pos_arr, name=f"owner_sum_{nm}") for g, got, nm in zip(g_mix + g_ffn, got_mix + got_ffn, names[1:])]
    from_half = halves_to_full(halves, [True] + [False] * 5, name="halves_to_full")
    g_big = [None] + [lax.dynamic_update_slice_in_dim(full, mine, ci * mine.shape[0], axis=0)
                      for full, mine in zip(from_half[1:], halves[1:])]
    upd_big = [adamw(w[0], g, m[0], v[0], name=f"adamw_{nm}") for w, g, m, v, nm in list(zip(
        big, g_big, (m_w_in, m_w_o_fox, m_w_o_dil, m_w_out, m_w_up, m_w_down),
        (v_w_in, v_w_o_fox, v_w_o_dil, v_w_out, v_w_up, v_w_down), names))[1:]]
    to_t = lambda a: jnp.transpose(a, (2, 0, 1))
    from_t = lambda a: jnp.transpose(a, (1, 2, 0))
    *upd_in, g_in_t = adamw_rows_view(to_t(w_in), halves[0], from_half[0], to_t(m_w_in), to_t(v_w_in), c_arr,
                                      name="adamw_w_in")

    g_cw_loc = jnp.concatenate([gc_a[0:3], gc_b[0:3]], axis=1)
    g_cb_loc = jnp.concatenate([gc_a[3:4], gc_b[3:4]], axis=1)
    small_loc = [gg_pre_mix, gg_post_mix, gg_pre_ffn, gg_post_ffn, g_cb_loc, gg_bf[:, :nf], g_cw_loc]
    red_rows = (8, 8, 8, 8, 48, 8, 136)
    red = allreduce_small(_pack_rows(small_loc, red_rows), name="allreduce_small")
    g_pm, g_qm, g_pf, g_qf, g_cb, g_bf, g_cw_full = _unpack_rows(red, [a.shape for a in small_loc], red_rows)
    cols_cw = conv_w.shape[2]
    g_cw = lax.dynamic_slice_in_dim(g_cw_full, chip * cols_cw, cols_cw, axis=1)
    small_w = (g_pre_mix, g_post_mix, g_pre_ffn, g_post_ffn, conv_b, b_forget, conv_w[0])
    small_m = (m_g_pre_mix, m_g_post_mix, m_g_pre_ffn, m_g_post_ffn, m_conv_b, m_b_forget, m_conv_w[0])
    small_v = (v_g_pre_mix, v_g_post_mix, v_g_pre_ffn, v_g_post_ffn, v_conv_b, v_b_forget, v_conv_w[0])
    small_g = (g_pm, g_qm, g_pf, g_qf, g_cb, g_bf, g_cw)
    small_names = ("g_pre_mix", "g_post_mix", "g_pre_ffn", "g_post_ffn", "conv_b", "b_forget", "conv_w")
    per_param = [adamw(w, g, m, v, name=f"adamw_{nm}") for w, g, m, v, nm in zip(small_w, small_g, small_m, small_v, small_names)]
    upd_small = [[u[j] for u in per_param] for j in range(3)]

    order = ("g_pre_mix", "w_in", "b_forget", "w_o_fox", "w_o_dil", "w_out", "g_post_mix", "g_pre_ffn", "w_up", "conv_w",
             "conv_b", "w_down", "g_post_ffn")
    grads, deltas, new_ms, new_vs = {}, {}, {}, {}
    grads["w_in"] = from_t(g_in_t)
    deltas["w_in"], new_ms["w_in"], new_vs["w_in"] = (from_t(a) for a in upd_in)
    for k, nm in enumerate(names[1:]):
        grads[nm] = g_big[k + 1][None]
        deltas[nm], new_ms[nm], new_vs[nm] = (a[None] for a in upd_big[k])
    for k, nm in enumerate(small_names):
        lead = (lambda a: a[None]) if nm == "conv_w" else (lambda a: a)
        grads[nm] = lead(small_g[k])
        deltas[nm], new_ms[nm], new_vs[nm] = (lead(upd_small[j][k]) for j in range(3))
    return (loss, grad_x[None], *[grads[nm] for nm in order], *[deltas[nm] for nm in order],
            *[new_ms[nm] for nm in order], *[new_vs[nm] for nm in order])
```

```python
import functools
import math

import numpy as np
import jax
import jax.numpy as jnp
from jax import lax
from jax.experimental import pallas as pl
from jax.experimental.pallas import tpu as pltpu

F32 = jnp.float32
BF16 = jnp.bfloat16
SDS = jax.ShapeDtypeStruct
MESH = pl.DeviceIdType.MESH

HEAD_DIM = 64
N_HEADS = 8
LANES = 128
ATT_W = N_HEADS * HEAD_DIM
DIL_PATTERNS = ((128, 1), (512, 4), (2048, 16))
DIL_BLK = 128
ROPE_DIM = HEAD_DIM // 4
ROPE_THETA = 500000.0
RMS_EPS = 1e-6
NEG = -1e30
QK_SCALE = 1.0 / math.sqrt(HEAD_DIM)
ADAM_LR, ADAM_B1, ADAM_B2, ADAM_EPS, ADAM_WD, ADAM_STEP = 0.001, 0.9, 0.999, 1e-08, 0.01, 10
VMEM_LIMIT = 56 * 1024 * 1024

Z_QA, Z_KA, Z_VA, Z_QB, Z_KB, Z_VB = 0, 1, 2, 3, 4, 5
Z_W = 5120


def _cp(sem):
    return pltpu.CompilerParams(dimension_semantics=sem, vmem_limit_bytes=VMEM_LIMIT)


def _nt(a, b):
    return lax.dot_general(a, b, (((1,), (1,)), ((), ())), preferred_element_type=F32)


def _tn(a, b):
    return lax.dot_general(a, b, (((0,), (0,)), ((), ())), preferred_element_type=F32)


def _nn(a, b):
    return jnp.dot(a, b, preferred_element_type=F32)


def _lane(shape):
    return lax.broadcasted_iota(jnp.int32, shape, 1)


def _row(shape):
    return lax.broadcasted_iota(jnp.int32, shape, 0)


def rmsnorm_fwd(x, g, *, tm=1024):
    s, d = x.shape

    def body(x_ref, g_ref, h_ref):
        xv = x_ref[...]
        inv = lax.rsqrt(jnp.mean(xv * xv, axis=-1, keepdims=True) + RMS_EPS)
        h_ref[...] = (xv * inv * g_ref[...]).astype(h_ref.dtype)

    return pl.pallas_call(
        body, grid=(s // tm,),
        in_specs=[pl.BlockSpec((tm, d), lambda i: (i, 0)), pl.BlockSpec((1, d), lambda i: (0, 0))],
        out_specs=pl.BlockSpec((tm, d), lambda i: (i, 0)),
        out_shape=SDS((s, d), BF16), name="rmsnorm_fwd", compiler_params=_cp(("parallel",)))(x, g)


def mm(a_views, b_views, *, nt, out_dtype, tm, tn, name):
    n_p = len(a_views)
    m = a_views[0][0].shape[0]
    n = b_views[0][0].shape[0] if nt else b_views[0][0].shape[1]

    def body(*refs):
        o_ref = refs[-1]
        acc = None
        for p in range(n_p):
            av = refs[p][...].astype(BF16)
            bv = refs[n_p + p][...].astype(BF16)
            dv = _nt(av, bv) if nt else _nn(av, bv)
            acc = dv if acc is None else acc + dv
        o_ref[...] = acc.astype(o_ref.dtype)

    in_specs = []
    for arr, w, blk in a_views:
        in_specs.append(pl.BlockSpec((tm, w), functools.partial(lambda i, j, blk: (i, blk), blk=blk)))
    for arr, w, blk in b_views:
        if nt:
            in_specs.append(pl.BlockSpec((tn, w), functools.partial(lambda i, j, blk: (j, blk), blk=blk)))
        else:
            in_specs.append(pl.BlockSpec((w, tn), lambda i, j: (0, j)))
    return pl.pallas_call(
        body, grid=(m // tm, n // tn), in_specs=in_specs,
        out_specs=pl.BlockSpec((tm, tn), lambda i, j: (i, j)),
        out_shape=SDS((m, n), out_dtype), name=name,
        compiler_params=_cp(("parallel", "parallel")))(*[a[0] for a in a_views], *[b[0] for b in b_views])


def wgrad(a_view, g, *, tk, tn, ts, name, chip_major=False, slabs=None, into=None, bf16_copy=False):
    arr, ka, blk = a_view
    s, n = g.shape
    ns = s // ts
    total, first = slabs if slabs else (n // tn, 0)
    n_into = 0 if into is None else (2 if bf16_copy else 1)

    def body(a_ref, g_ref, *rest):
        o_ref = rest[n_into]

        @pl.when(pl.program_id(2) == 0)
        def _():
            o_ref[...] = jnp.zeros_like(o_ref)

        o_ref[...] += _tn(a_ref[...].astype(BF16), g_ref[...].astype(BF16))
        if bf16_copy:
            @pl.when(pl.program_id(2) == ns - 1)
            def _():
                rest[n_into + 1][...] = o_ref[...].astype(BF16)

    if chip_major:
        out_spec = pl.BlockSpec((None, tk, tn), lambda i, j, k: (first + j, i, 0))
        shape = (total, ka, tn)
    else:
        out_spec = pl.BlockSpec((tk, tn), lambda i, j, k: (i, j))
        shape = (ka, n)
    in_specs = [pl.BlockSpec((ts, tk), lambda i, j, k: (k, blk * (ka // tk) + i)),
                pl.BlockSpec((ts, tn), lambda i, j, k: (k, j))]
    args = [arr, g]
    if into is not None:
        earlier = list(into) if bf16_copy else [into]
        in_specs += [pl.BlockSpec(memory_space=pl.ANY)] * len(earlier)
        args += earlier
    out = pl.pallas_call(
        body, grid=(ka // tk, n // tn, ns), in_specs=in_specs,
        out_specs=[out_spec, out_spec] if bf16_copy else out_spec,
        out_shape=[SDS(shape, F32), SDS(shape, BF16)] if bf16_copy else SDS(shape, F32), name=name,
        input_output_aliases={2 + k: k for k in range(n_into)},
        compiler_params=_cp(("parallel", "parallel", "arbitrary")))(*args)
    return out


def _norm_bwd_rows(dh, xh, inv, g):
    dxh = dh * g
    dx = inv * (dxh - xh * jnp.mean(dxh * xh, axis=-1, keepdims=True))
    return dx, jnp.sum((dh * xh).reshape(dh.shape[0] // 8, 8, dh.shape[1]), axis=0)


def proj_norm_res(a, w, g, xres, g_next, *, tm=512, name):
    s, k = a.shape
    d = w.shape[1]

    def body(a_ref, w_ref, g_ref, x_ref, gn_ref, y_ref, o_ref, h_ref):
        y = _nn(a_ref[...], w_ref[...])
        inv = lax.rsqrt(jnp.mean(y * y, axis=-1, keepdims=True) + RMS_EPS)
        xn = x_ref[...] + y * inv * g_ref[...]
        y_ref[...] = y
        o_ref[...] = xn
        inv_n = lax.rsqrt(jnp.mean(xn * xn, axis=-1, keepdims=True) + RMS_EPS)
        h_ref[...] = (xn * inv_n * gn_ref[...]).astype(h_ref.dtype)

    row = pl.BlockSpec((tm, d), lambda i: (i, 0))
    vec = pl.BlockSpec((1, d), lambda i: (0, 0))
    return pl.pallas_call(
        body, grid=(s // tm,),
        in_specs=[pl.BlockSpec((tm, k), lambda i: (i, 0)), pl.BlockSpec((k, d), lambda i: (0, 0)), vec, row, vec],
        out_specs=[row, row, row], out_shape=[SDS((s, d), F32), SDS((s, d), F32), SDS((s, d), BF16)], name=name,
        compiler_params=_cp(("parallel",)))(a, w, g, xres, g_next)


def proj_norm_loss(a, w, g, xres, target, *, tm=512, name):
    s, k = a.shape
    d = w.shape[1]
    n = s // tm

    def body(a_ref, w_ref, g_ref, x_ref, t_ref, do_ref, dy_ref, dg_ref, l_ref, acc):
        i = pl.program_id(0)

        @pl.when(i == 0)
        def _():
            acc[...] = jnp.zeros_like(acc)
            l_ref[...] = jnp.zeros_like(l_ref)

        y = _nn(a_ref[...], w_ref[...])
        inv = lax.rsqrt(jnp.mean(y * y, axis=-1, keepdims=True) + RMS_EPS)
        yh = y * inv
        err = x_ref[...] + yh * g_ref[...] - t_ref[...]
        dout = err * (1.0 / d)
        do_ref[...] = dout
        l_ref[...] += jnp.sum(jnp.sum(err * err, axis=1, keepdims=True), axis=0, keepdims=True)
        dy, part = _norm_bwd_rows(dout, yh, inv, g_ref[...])
        dy_ref[...] = dy.astype(dy_ref.dtype)
        acc[...] += part

        @pl.when(i == n - 1)
        def _():
            dg_ref[...] = jnp.sum(acc[...], axis=0, keepdims=True)

    row = pl.BlockSpec((tm, d), lambda i: (i, 0))
    vec = pl.BlockSpec((1, d), lambda i: (0, 0))
    return pl.pallas_call(
        body, grid=(n,),
        in_specs=[pl.BlockSpec((tm, k), lambda i: (i, 0)), pl.BlockSpec((k, d), lambda i: (0, 0)), vec, row, row],
        out_specs=[row, row, vec, pl.BlockSpec((1, 1), lambda i: (0, 0))],
        out_shape=[SDS((s, d), F32), SDS((s, d), BF16), SDS((1, d), F32), SDS((1, 1), F32)],
        scratch_shapes=[pltpu.VMEM((8, d), F32)], name=name, compiler_params=_cp(("arbitrary",)))(a, w, g, xres, target)


def mm_norm_bwd(a_views, b_views, stages, exchange=(), *, tm=256, name):
    n_p, n_s, ne = len(a_views), len(stages), len(exchange)
    s = a_views[0][0].shape[0]
    d = b_views[0][0].shape[0]
    n = s // tm
    has_res = [st[2] is not None for st in stages]

    def body(*refs):
        a_refs, b_refs = refs[:n_p], refs[n_p:2 * n_p]
        at = 2 * n_p
        st_refs = []
        for k in range(n_s):
            cnt = 3 if has_res[k] else 2
            st_refs.append(refs[at:at + cnt])
            at += cnt
        e_ins = refs[at:at + ne]
        at += ne
        dx_refs, dg_refs = refs[at:at + n_s], refs[at + n_s:at + 2 * n_s]
        at += 2 * n_s
        e_outs = refs[at:at + ne]
        at += ne
        accs = refs[at:at + n_s]
        comm = (e_ins, e_outs) + tuple(refs[at + n_s:])
        i = pl.program_id(0)

        @pl.when(i == 0)
        def _():
            for acc in accs:
                acc[...] = jnp.zeros_like(acc)
            if ne:
                _to_chips_start(*comm)

        dh = None
        for p in range(n_p):
            part = _nt(a_refs[p][...].astype(BF16), b_refs[p][...].astype(BF16))
            dh = part if dh is None else dh + part
        for k in range(n_s):
            xv = st_refs[k][0][...]
            inv = lax.rsqrt(jnp.mean(xv * xv, axis=-1, keepdims=True) + RMS_EPS)
            dx, part = _norm_bwd_rows(dh, xv * inv, inv, st_refs[k][1][...])
            if has_res[k]:
                dx = dx + st_refs[k][2][...]
            dx_refs[k][...] = dx.astype(dx_refs[k].dtype)
            accs[k][...] += part
            dh = dx

        @pl.when(i == n - 1)
        def _():
            for k in range(n_s):
                dg_refs[k][...] = jnp.sum(accs[k][...], axis=0, keepdims=True)
            if ne:
                _to_chips_finish(*comm)

    row = pl.BlockSpec((tm, d), lambda i: (i, 0))
    vec = pl.BlockSpec((1, d), lambda i: (0, 0))
    in_specs, args = [], []
    for arr, w, blk in a_views:
        in_specs.append(pl.BlockSpec((tm, w), functools.partial(lambda i, blk: (i, blk), blk=blk)))
        args.append(arr)
    for arr, w, blk in b_views:
        in_specs.append(pl.BlockSpec((d, w), functools.partial(lambda i, blk: (0, blk), blk=blk)))
        args.append(arr)
    for x, g, res, _ in stages:
        in_specs += [row, vec] + ([row] if res is not None else [])
        args += [x, g] + ([res] if res is not None else [])
    return pl.pallas_call(
        body, grid=(n,), in_specs=in_specs + [ANY] * ne,
        out_specs=[row] * n_s + [vec] * n_s + [ANY] * ne,
        out_shape=[SDS((s, d), st[3]) for st in stages] + [SDS((1, d), F32)] * n_s + _to_chips_shapes(exchange),
        scratch_shapes=[pltpu.VMEM((8, d), F32)] * n_s + (_to_chips_sems(ne) if ne else []), name=name,
        compiler_params=_cp(("arbitrary",)))(*args, *exchange)


def _split3(v):
    hi = v.astype(BF16).astype(F32)
    r = v - hi
    mid = r.astype(BF16).astype(F32)
    lo = (r - mid).astype(BF16).astype(F32)
    return hi, mid, lo


def _tri(n, upper):
    r = np.arange(n)
    m = (r[:, None] <= r[None, :]) if upper else (r[:, None] >= r[None, :])
    return jnp.asarray(m.astype(np.float32))


def fox_prep(z, fa, bfo, *, tb=512):
    s = z.shape[0]
    n = s // tb

    def body(q_ref, k_ref, v_ref, fa_ref, b_ref, tri_ref, qa_ref, ka_ref, va_ref, carry):
        @pl.when(pl.program_id(0) == 0)
        def _():
            carry[...] = jnp.zeros_like(carry)

        xv = fa_ref[...] + b_ref[...]
        logf = jnp.minimum(xv, 0.0) - jnp.log(1.0 + jnp.exp(-jnp.abs(xv)))
        csum = jnp.dot(tri_ref[...], logf, preferred_element_type=F32, precision=lax.Precision.HIGHEST) + carry[0:1, :]
        carry[0:1, :] = csum[tb - 1:tb, :]
        lane = _lane((tb, LANES))
        for h in range(N_HEADS):
            hi, mid, lo = _split3(csum[:, h:h + 1])
            pair = (h // 2) * LANES
            qv = q_ref[:, pair:pair + LANES].astype(F32)
            kv = k_ref[:, pair:pair + LANES].astype(F32)
            vv = v_ref[:, pair:pair + LANES].astype(F32)
            if h % 2:
                qv = pltpu.roll(qv, 64, axis=1)
                kv = pltpu.roll(kv, 64, axis=1)
                vv = pltpu.roll(vv, 64, axis=1)
            va_ref[:, h * LANES:(h + 1) * LANES] = jnp.where(lane < 64, vv, jnp.where(lane == 64, 1.0, 0.0)).astype(BF16)
            one = jnp.where((lane >= 67) & (lane < 70), 1.0, 0.0)
            q_x = jnp.where(lane == 64, hi, jnp.where(lane == 65, mid, jnp.where(lane == 66, lo, one)))
            one = jnp.where((lane >= 64) & (lane < 67), 1.0, 0.0)
            k_x = jnp.where(lane == 67, -hi, jnp.where(lane == 68, -mid, jnp.where(lane == 69, -lo, one)))
            qa_ref[:, h * LANES:(h + 1) * LANES] = jnp.where(lane < 64, qv * QK_SCALE, q_x).astype(BF16)
            ka_ref[:, h * LANES:(h + 1) * LANES] = jnp.where(lane < 64, kv, k_x).astype(BF16)

    return pl.pallas_call(
        body, grid=(n,),
        in_specs=[pl.BlockSpec((tb, ATT_W), lambda i: (i, Z_QA)), pl.BlockSpec((tb, ATT_W), lambda i: (i, Z_KA)),
                  pl.BlockSpec((tb, ATT_W), lambda i: (i, Z_VA)),
                  pl.BlockSpec((tb, LANES), lambda i: (i, 0)), pl.BlockSpec((1, LANES), lambda i: (0, 0)),
                  pl.BlockSpec((tb, tb), lambda i: (0, 0))],
        out_specs=[pl.BlockSpec((tb, N_HEADS * LANES), lambda i: (i, 0))] * 3,
        out_shape=[SDS((s, N_HEADS * LANES), BF16)] * 3,
        scratch_shapes=[pltpu.VMEM((8, LANES), F32)],
        name="fox_prep", compiler_params=_cp(("arbitrary",)))(z, z, z, fa, bfo, _tri(tb, False))


def _causal_pairs(n, k_major):
    if k_major:
        pairs = [(qi, kj) for kj in range(n) for qi in range(kj, n)]
    else:
        pairs = [(qi, kj) for qi in range(n) for kj in range(qi + 1)]
    return (jnp.asarray([p[0] for p in pairs], jnp.int32), jnp.asarray([p[1] for p in pairs], jnp.int32), len(pairs))


def fox_fwd(q_aug, k_aug, v_aug, gather=(), halved=(), *, t=512, hps=4):
    s = v_aug.shape[0]
    qi_arr, kj_arr, n_pairs = _causal_pairs(s // t, False)
    ng = len(gather)
    n_groups = N_HEADS // hps

    def body(qi_ref, kj_ref, q_ref, k_ref, v_ref, *rest):
        g_ins, (o_ref, lse_ref), g_outs = rest[:ng], rest[ng:ng + 2], rest[ng + 2:2 * ng + 2]
        m_scr, acc_scr = rest[2 * ng + 2:2 * ng + 4]
        comm = (g_ins, g_outs) + tuple(rest[2 * ng + 4:]) + (list(halved),)
        step = pl.program_id(1)
        qi = qi_ref[step]
        kj = kj_ref[step]
        if ng:
            @pl.when((pl.program_id(0) == 0) & (step == 0))
            def _():
                _allgather_start(*comm)

        @pl.when(kj == 0)
        def _():
            m_scr[...] = jnp.full_like(m_scr, NEG)
            acc_scr[...] = jnp.zeros_like(acc_scr)

        def update(masked):
            for i in range(hps):
                sc = _nt(q_ref[:, i * LANES:(i + 1) * LANES], k_ref[:, i * LANES:(i + 1) * LANES])
                if masked:
                    sc = jnp.where(_row((t, t)) >= _lane((t, t)), sc, NEG)
                m_prev = m_scr[i]
                m_new = jnp.maximum(m_prev, jnp.max(sc, axis=-1, keepdims=True))
                p = jnp.exp((sc - jnp.tile(m_new, (1, t // LANES))).astype(BF16))
                acc_scr[i] = jnp.exp(m_prev - m_new) * acc_scr[i] + _nn(p, v_ref[:, i * LANES:(i + 1) * LANES])
                m_scr[i] = m_new

        @pl.when(kj < qi)
        def _():
            update(False)

        @pl.when(kj == qi)
        def _():
            update(True)
            lane = _lane((t, LANES))
            for pr in range(hps // 2):
                den = [acc_scr[2 * pr + i][:, 64:65] for i in range(2)]
                o_ref[:, pr * LANES:(pr + 1) * LANES] = jnp.where(
                    lane < 64, acc_scr[2 * pr] / den[0], pltpu.roll(acc_scr[2 * pr + 1] / den[1], 64, axis=1)).astype(o_ref.dtype)
                lse_ref[:, pr * LANES:(pr + 1) * LANES] = jnp.where(
                    lane < 64, m_scr[2 * pr] + jnp.log(den[0]), m_scr[2 * pr + 1] + jnp.log(den[1]))

        if ng:
            @pl.when((pl.program_id(0) == n_groups - 1) & (step == n_pairs - 1))
            def _():
                _allgather_finish(*comm)

    wide = hps * LANES
    grid_spec = pltpu.PrefetchScalarGridSpec(
        num_scalar_prefetch=2, grid=(n_groups, n_pairs),
        in_specs=[pl.BlockSpec((t, wide), lambda hg, st, qi, kj: (qi[st], hg)),
                  pl.BlockSpec((t, wide), lambda hg, st, qi, kj: (kj[st], hg)),
                  pl.BlockSpec((t, wide), lambda hg, st, qi, kj: (kj[st], hg))] + [ANY] * ng,
        out_specs=[pl.BlockSpec((t, wide // 2), lambda hg, st, qi, kj: (qi[st], hg))] * 2 + [ANY] * ng,
        scratch_shapes=[pltpu.VMEM((hps, t, LANES), F32)] * 2 + (_allgather_sems(ng) if ng else []))
    return pl.pallas_call(
        body, grid_spec=grid_spec, out_shape=[SDS((s, ATT_W), BF16), SDS((s, ATT_W), F32)] + _allgather_shapes(gather),
        name="fox_fwd", compiler_params=_cp(("arbitrary", "arbitrary")))(qi_arr, kj_arr, q_aug, k_aug, v_aug, *gather)


def fox_bwd(q_aug, k_aug, z, dy, lse, dd, exchange=(), kind="to_chips", *, t=512, hps=4):
    s = z.shape[0]
    qi_arr, kj_arr, n_pairs = _causal_pairs(s // t, True)
    ne = len(exchange)
    n_groups = N_HEADS // hps
    x_shapes, x_sems, x_start, x_finish = EXCHANGES[kind]

    def body(qi_ref, kj_ref, q_ref, k_ref, v_ref, do_ref, lse_ref, dd_ref, *rest):
        e_ins, (dq_ref, dk_ref, dv_ref), e_outs = rest[:ne], rest[ne:ne + 3], rest[ne + 3:2 * ne + 3]
        comm = (e_ins, e_outs) + tuple(rest[2 * ne + 3:])
        step = pl.program_id(1)
        qi = qi_ref[step]
        kj = kj_ref[step]
        if ne:
            @pl.when((pl.program_id(0) == 0) & (step == 0))
            def _():
                x_start(*comm)

        @pl.when(step == 0)
        def _():
            dq_ref[...] = jnp.zeros_like(dq_ref)

        @pl.when(qi == kj)
        def _():
            dk_ref[...] = jnp.zeros_like(dk_ref)
            dv_ref[...] = jnp.zeros_like(dv_ref)

        def update(masked):
            lane = _lane((t, LANES))
            rows = pl.ds(pl.multiple_of(qi * t, t), t)
            for pr in range(hps // 2):
                pair = slice(pr * LANES, (pr + 1) * LANES)
                dov = do_ref[:, pair]
                dv_new = None
                for i in range(2):
                    head = (lane < 64) if i == 0 else (lane >= 64)
                    own = slice((2 * pr + i) * LANES, (2 * pr + i + 1) * LANES)
                    col = slice(pr * LANES + i * 64, pr * LANES + i * 64 + 1)
                    qv = q_ref[:, own]
                    kv = k_ref[:, own]
                    sc = _nt(qv, kv)
                    if masked:
                        sc = jnp.where(_row((t, t)) >= _lane((t, t)), sc, NEG)
                    p = jnp.exp(sc - lse_ref[:, col])
                    dp = _nt(jnp.where(head, dov, jnp.zeros_like(dov)), v_ref[:, pair])
                    ds = (p * (dp - dd_ref[:, col])).astype(BF16)
                    dq_ref[rows, own] += _nn(ds, kv)
                    dk_ref[:, own] += _tn(ds, qv)
                    dvi = _tn(p.astype(BF16), dov)
                    dv_new = dvi if dv_new is None else jnp.where(head, dvi, dv_new)
                dv_ref[:, pair] += dv_new

        @pl.when(kj < qi)
        def _():
            update(False)

        @pl.when(kj == qi)
        def _():
            update(True)

        if ne:
            @pl.when((pl.program_id(0) == n_groups - 1) & (step == n_pairs - 1))
            def _():
                x_finish(*comm)

    wide, half = hps * LANES, hps // 2 * LANES
    v_blk = Z_VA * ATT_W // half
    grid_spec = pltpu.PrefetchScalarGridSpec(
        num_scalar_prefetch=2, grid=(n_groups, n_pairs),
        in_specs=[pl.BlockSpec((t, wide), lambda hg, st, qi, kj: (qi[st], hg)),
                  pl.BlockSpec((t, wide), lambda hg, st, qi, kj: (kj[st], hg)),
                  pl.BlockSpec((t, half), lambda hg, st, qi, kj: (kj[st], v_blk + hg)),
                  pl.BlockSpec((t, half), lambda hg, st, qi, kj: (qi[st], hg)),
                  pl.BlockSpec((t, half), lambda hg, st, qi, kj: (qi[st], hg)),
                  pl.BlockSpec((t, half), lambda hg, st, qi, kj: (qi[st], hg))] + [ANY] * ne,
        out_specs=[pl.BlockSpec((s, wide), lambda hg, st, qi, kj: (0, hg)),
                   pl.BlockSpec((t, wide), lambda hg, st, qi, kj: (kj[st], hg)),
                   pl.BlockSpec((t, half), lambda hg, st, qi, kj: (kj[st], hg))] + [ANY] * ne,
        scratch_shapes=x_sems(ne) if ne else [])
    return pl.pallas_call(
        body, grid_spec=grid_spec,
        out_shape=[SDS((s, N_HEADS * LANES), F32), SDS((s, N_HEADS * LANES), F32), SDS((s, ATT_W), F32)]
        + x_shapes(exchange),
        name="fox_bwd", compiler_params=_cp(("arbitrary", "arbitrary")))(qi_arr, kj_arr, q_aug, k_aug, z, dy, lse, dd, *exchange)


def fox_post(dq_aug, dk_aug, dv, fa, bfo, *, tb=512):
    s = dv.shape[0]
    n = s // tb

    def body(dq_ref, dk_ref, dv_ref, fa_ref, b_ref, tri_ref, dz_ref, dfa_ref, gb_ref, carry, acc):
        i = pl.program_id(0)

        @pl.when(i == 0)
        def _():
            carry[...] = jnp.zeros_like(carry)
            acc[...] = jnp.zeros_like(acc)

        lane = _lane((tb, LANES))
        d_f = jnp.zeros((tb, LANES), F32)
        for h in range(N_HEADS):
            col = dq_ref[:, h * LANES + 64:h * LANES + 65] - dk_ref[:, h * LANES + 67:h * LANES + 68]
            d_f = jnp.where(lane == h, col, d_f)
        suffix = jnp.dot(tri_ref[...], d_f, preferred_element_type=F32, precision=lax.Precision.HIGHEST) + carry[0:1, :]
        carry[0:1, :] = suffix[0:1, :]
        xv = fa_ref[...] + b_ref[...]
        dx = suffix * (1.0 / (1.0 + jnp.exp(xv)))
        dfa_ref[...] = dx.astype(dfa_ref.dtype)
        acc[...] += jnp.sum(dx.reshape(tb // 8, 8, LANES), axis=0)
        for hp in range(4):
            for src, off, scale in ((dq_ref, 0, QK_SCALE), (dk_ref, ATT_W, 1.0)):
                even = src[:, (2 * hp) * LANES:(2 * hp + 1) * LANES]
                odd = pltpu.roll(src[:, (2 * hp + 1) * LANES:(2 * hp + 2) * LANES], 64, axis=1)
                dz_ref[:, off + hp * LANES:off + (hp + 1) * LANES] = (jnp.where(lane < 64, even, odd) * scale).astype(BF16)
        dz_ref[:, 2 * ATT_W:3 * ATT_W] = dv_ref[...].astype(BF16)

        @pl.when(i == n - 1)
        def _():
            gb_ref[...] = jnp.sum(acc[...], axis=0, keepdims=True)

    rev = lambda i: (n - 1 - i, 0)
    return pl.pallas_call(
        body, grid=(n,),
        in_specs=[pl.BlockSpec((tb, N_HEADS * LANES), rev), pl.BlockSpec((tb, N_HEADS * LANES), rev),
                  pl.BlockSpec((tb, ATT_W), rev), pl.BlockSpec((tb, LANES), rev),
                  pl.BlockSpec((1, LANES), lambda i: (0, 0)), pl.BlockSpec((tb, tb), lambda i: (0, 0))],
        out_specs=[pl.BlockSpec((tb, 3 * ATT_W), rev), pl.BlockSpec((tb, LANES), rev),
                   pl.BlockSpec((1, LANES), lambda i: (0, 0))],
        out_shape=[SDS((s, 3 * ATT_W), BF16), SDS((s, LANES), BF16), SDS((1, LANES), F32)],
        scratch_shapes=[pltpu.VMEM((8, LANES), F32), pltpu.VMEM((8, LANES), F32)],
        name="fox_post", compiler_params=_cp(("arbitrary",)))(dq_aug, dk_aug, dv, fa, bfo, _tri(tb, True))


def rope_cos_sin(s):
    half = ROPE_DIM // 2
    inv_freq = ROPE_THETA ** (-jnp.arange(half, dtype=F32) * 2.0 / ROPE_DIM)
    ang = jnp.arange(s, dtype=F32)[:, None] * inv_freq[None, :]
    return jnp.tile(jnp.cos(ang), (1, LANES // half)), jnp.tile(jnp.sin(ang), (1, LANES // half))


def _rotate(x, cos, sin, sign):
    l64 = _lane(x.shape) & (HEAD_DIM - 1)
    first = l64 < ROPE_DIM // 2
    second = (l64 >= ROPE_DIM // 2) & (l64 < ROPE_DIM)
    from_next = jnp.where(first, -sign * sin, 0.0)
    from_prev = jnp.where(second, sign * sin, 0.0)
    return (x * jnp.where(first | second, cos, 1.0) + pltpu.roll(x, LANES - 8, axis=1) * from_next
            + pltpu.roll(x, 8, axis=1) * from_prev)


def _dil_rows(base, r):
    if r == 1:
        return pl.ds(pl.multiple_of(base, DIL_BLK), DIL_BLK)
    return pl.ds(base, DIL_BLK, stride=r)


def _dil_block(idx, r, nb):
    shift = nb.bit_length() - 1
    rho = idx >> shift
    n = idx & (nb - 1)
    base = rho + n * (r * DIL_BLK)
    return _dil_rows(base, r), _dil_rows(jnp.maximum(base - r * DIL_BLK, rho), r), n > 0


def _cat(a, b):
    return jnp.concatenate([a, b], axis=0)


def _two_heads(v, first_head):
    zero = jnp.zeros_like(v)
    return _cat(jnp.where(first_head, v, zero), jnp.where(first_head, zero, v))


def _dil_bands():
    b = DIL_BLK
    q = _row((2 * b, 2 * b)) & (b - 1)
    col = _lane((2 * b, 2 * b))
    return (col < b) & (col >= q), (col >= b) & (col - b <= q)


def _dil_load_qkv(zq_ref, zk_ref, zv_ref, cos_ref, sin_ref, q_ref, k_ref, v_ref, *, chunk=512):
    def step(i, carry):
        rows = pl.ds(pl.multiple_of(i * chunk, chunk), chunk)
        cos, sin = cos_ref[rows, :], sin_ref[rows, :]
        q_ref[rows, :] = _rotate(zq_ref[rows, :].astype(F32), cos, sin, 1.0) * QK_SCALE
        k_ref[rows, :] = _rotate(zk_ref[rows, :].astype(F32), cos, sin, 1.0)
        v_ref[rows, :] = zv_ref[rows, :].astype(F32)
        return carry

    lax.fori_loop(0, q_ref.shape[0] // chunk, step, 0)


def dil_fwd_all(z, cos_t, sin_t, *, unroll=16):
    s = z.shape[0]
    b = DIL_BLK
    n_blk = s // b

    def body(zq_ref, zk_ref, zv_ref, cos_ref, sin_ref, o_ref, l_ref, q_ref, k_ref, v_ref):
        _dil_load_qkv(zq_ref, zk_ref, zv_ref, cos_ref, sin_ref, q_ref, k_ref, v_ref)
        first_head = _lane((b, LANES)) < 64
        band_prev, band_cur = _dil_bands()
        for g, (_, r) in enumerate(DIL_PATTERNS):
            nb = n_blk // r

            def group(it, carry, g=g, r=r, nb=nb):
                loaded = []
                kc = vc = None
                for u in range(unroll):
                    rows_c, rows_p, has_prev = _dil_block(it * unroll + u, r, nb)
                    if u % min(nb, unroll):
                        kp, vp = kc, vc
                    else:
                        kp, vp = k_ref[rows_p, :].astype(BF16), v_ref[rows_p, :].astype(BF16)
                    kc, vc = k_ref[rows_c, :].astype(BF16), v_ref[rows_c, :].astype(BF16)
                    state = (o_ref[rows_c, :], l_ref[rows_c, :]) if g else None
                    loaded.append((rows_c, has_prev, [q_ref[rows_c, :].astype(BF16), kp, kc, vp, vc], state))
                done = []
                for rows_c, has_prev, (qv, kp, kc, vp, vc), state in loaded:
                    sc = jnp.where(band_cur | (band_prev & has_prev), _nt(_two_heads(qv, first_head), _cat(kp, kc)), NEG)
                    m = jnp.max(sc, axis=-1, keepdims=True)
                    p = jnp.exp(sc - m)
                    den = jnp.sum(p, axis=-1, keepdims=True)
                    both = _nn(p.astype(BF16), _cat(vp, vc)) / den
                    lse2 = m + jnp.log(den)
                    ov = jnp.where(first_head, both[:b], both[b:])
                    lse = jnp.where(first_head, lse2[:b], lse2[b:])
                    if state is not None:
                        m2 = jnp.maximum(state[1], lse)
                        wp = jnp.exp(state[1] - m2)
                        wn = jnp.exp(lse - m2)
                        ov = (wp * state[0] + wn * ov) / (wp + wn)
                        lse = m2 + jnp.log(wp + wn)
                    done.append((rows_c, ov, lse))
                for rows_c, ov, lse in done:
                    o_ref[rows_c, :] = ov
                    l_ref[rows_c, :] = lse
                return carry

            lax.fori_loop(0, n_blk // unroll, group, 0)

    col_blk = lambda k: pl.BlockSpec((s, LANES), lambda hp: (0, 4 * k + hp))
    table = pl.BlockSpec((s, LANES), lambda hp: (0, 0))
    out = pl.BlockSpec((s, LANES), lambda hp: (0, hp))
    return pl.pallas_call(
        body, grid=(4,), in_specs=[col_blk(Z_QB), col_blk(Z_KB), col_blk(Z_VB), table, table], out_specs=[out, out],
        out_shape=[SDS((s, ATT_W), F32)] * 2, scratch_shapes=[pltpu.VMEM((s, LANES), F32)] * 3, name="dil_fwd",
        compiler_params=_cp(("parallel",)))(z, z, z, cos_t, sin_t)


def dil_bwd_all(z, cos_t, sin_t, dy, lse, y, exchange=(), kind="to_chips", *, unroll=8):
    s = z.shape[0]
    b = DIL_BLK
    n_blk = s // b
    ne = len(exchange)
    x_shapes, x_sems, x_start, x_finish = EXCHANGES[kind]

    def body(zq_ref, zk_ref, zv_ref, cos_ref, sin_ref, do_ref, l_ref, y_ref, *rest):
        e_ins, (gq_ref, gk_ref, gv_ref), e_outs = rest[:ne], rest[ne:ne + 3], rest[ne + 3:2 * ne + 3]
        q_ref, k_ref, v_ref, dq_ref, dk_ref, dv_ref = rest[2 * ne + 3:2 * ne + 9]
        comm = (e_ins, e_outs) + tuple(rest[2 * ne + 9:])
        if ne:
            @pl.when(pl.program_id(0) == 0)
            def _():
                x_start(*comm)

        _dil_load_qkv(zq_ref, zk_ref, zv_ref, cos_ref, sin_ref, q_ref, k_ref, v_ref)
        dq_ref[...] = jnp.zeros_like(dq_ref)
        dk_ref[...] = jnp.zeros_like(dk_ref)
        dv_ref[...] = jnp.zeros_like(dv_ref)
        first_head = _lane((b, LANES)) < 64
        band_prev, band_cur = _dil_bands()
        for _, r in DIL_PATTERNS:
            nb = n_blk // r

            def group(it, carry, r=r, nb=nb):
                loaded = []
                kc = vc = None
                for u in range(unroll):
                    rows_c, rows_p, has_prev = _dil_block(it * unroll + u, r, nb)
                    if u % min(nb, unroll):
                        kp, vp = kc, vc
                    else:
                        kp, vp = k_ref[rows_p, :].astype(BF16), v_ref[rows_p, :].astype(BF16)
                    kc, vc = k_ref[rows_c, :].astype(BF16), v_ref[rows_c, :].astype(BF16)
                    vals = [q_ref[rows_c, :].astype(BF16), kp, kc, vp, vc, do_ref[rows_c, :], l_ref[rows_c, :], y_ref[rows_c, :]]
                    loaded.append((rows_c, rows_p, has_prev, vals))
                done = []
                for rows_c, rows_p, has_prev, (qv, kp, kc, vp, vc, dof, lv, yv) in loaded:
                    q2 = _two_heads(qv, first_head)
                    do2 = _two_heads(dof.astype(BF16), first_head)
                    kcat, vcat = _cat(kp, kc), _cat(vp, vc)
                    lse2 = _cat(lv[:, 0:1], lv[:, 64:65])
                    dd2 = jnp.sum(_two_heads(dof * yv, first_head), axis=-1, keepdims=True)
                    p = jnp.exp(jnp.where(band_cur | (band_prev & has_prev), _nt(q2, kcat), NEG) - lse2)
                    ds = (p * (_nt(do2, vcat) - dd2)).astype(BF16)
                    dq2 = _nn(ds, kcat)
                    dkcat = _tn(ds, q2)
                    dvcat = _tn(p.astype(BF16), do2)
                    done.append((rows_c, rows_p, (jnp.where(first_head, dq2[:b], dq2[b:]), dkcat[:b], dkcat[b:],
                                                  dvcat[:b], dvcat[b:])))
                for rows_c, rows_p, (dq, dk_p, dk_c, dv_p, dv_c) in done:
                    dq_ref[rows_c, :] += dq
                    dk_ref[rows_p, :] += dk_p
                    dk_ref[rows_c, :] += dk_c
                    dv_ref[rows_p, :] += dv_p
                    dv_ref[rows_c, :] += dv_c
                return carry

            lax.fori_loop(0, n_blk // unroll, group, 0)

        def finish(i, carry, chunk=512):
            rows = pl.ds(pl.multiple_of(i * chunk, chunk), chunk)
            cos, sin = cos_ref[rows, :], sin_ref[rows, :]
            gq_ref[rows, :] = (_rotate(dq_ref[rows, :], cos, sin, -1.0) * QK_SCALE).astype(BF16)
            gk_ref[rows, :] = _rotate(dk_ref[rows, :], cos, sin, -1.0).astype(BF16)
            gv_ref[rows, :] = dv_ref[rows, :].astype(BF16)
            return carry

        lax.fori_loop(0, s // 512, finish, 0)
        if ne:
            @pl.when(pl.program_id(0) == 3)
            def _():
                x_finish(*comm)

    col_blk = lambda k: pl.BlockSpec((s, LANES), lambda hp: (0, 4 * k + hp))
    table = pl.BlockSpec((s, LANES), lambda hp: (0, 0))
    nat = pl.BlockSpec((s, LANES), lambda hp: (0, hp))
    return pl.pallas_call(
        body, grid=(4,), in_specs=[col_blk(Z_QB), col_blk(Z_KB), col_blk(Z_VB), table, table, nat, nat, nat] + [ANY] * ne,
        out_specs=[nat, nat, nat] + [ANY] * ne, out_shape=[SDS((s, ATT_W), BF16)] * 3 + x_shapes(exchange),
        scratch_shapes=[pltpu.VMEM((s, LANES), F32)] * 6 + (x_sems(ne) if ne else []), name="dil_bwd",
        compiler_params=_cp(("arbitrary",)))(z, z, z, cos_t, sin_t, dy, lse, y, *exchange)


def _sigmoid(v):
    return 1.0 / (1.0 + jnp.exp(-v))


def gate_mix(ya, yb, wa, wb, z, *, tm=2048, tn=512):
    s = ya.shape[0]
    d = wa.shape[1]
    ga_blk = 3 * ATT_W * 2 // tn
    gb_blk = ga_blk + d // tn

    def body(ya_ref, yb_ref, wa_ref, wb_ref, ga_ref, gb_ref, pa_ref, pb_ref, mx_ref):
        pa = _nn(ya_ref[...], wa_ref[...])
        pb = _nn(yb_ref[...].astype(BF16), wb_ref[...])
        pa_ref[...] = pa.astype(BF16)
        pb_ref[...] = pb.astype(BF16)
        mx_ref[...] = (_sigmoid(ga_ref[...].astype(F32)) * pa + _sigmoid(gb_ref[...].astype(F32)) * pb).astype(BF16)

    out = pl.BlockSpec((tm, tn), lambda i, j: (i, j))
    return pl.pallas_call(
        body, grid=(s // tm, d // tn),
        in_specs=[pl.BlockSpec((tm, ATT_W), lambda i, j: (i, 0)), pl.BlockSpec((tm, ATT_W), lambda i, j: (i, 0)),
                  pl.BlockSpec((ATT_W, tn), lambda i, j: (0, j)), pl.BlockSpec((ATT_W, tn), lambda i, j: (0, j)),
                  pl.BlockSpec((tm, tn), lambda i, j: (i, ga_blk + j)), pl.BlockSpec((tm, tn), lambda i, j: (i, gb_blk + j))],
        out_specs=[out, out, out], out_shape=[SDS((s, d), BF16)] * 3, name="gate_mix",
        compiler_params=_cp(("parallel", "parallel")))(ya, yb, wa, wb, z, z)


def mix_bwd(dy, w_o, z, pa, pb, wo_a, wo_b, ya, *, tm=512):
    s, d = dy.shape

    def body(dy_ref, wo_ref, ga_ref, gb_ref, pa_ref, pb_ref, wa_ref, wb_ref, ya_ref,
             dpa_ref, dpb_ref, dg_ref, dya_ref, dyb_ref, dd_ref):
        dm = _nt(dy_ref[...], wo_ref[...])
        sa = _sigmoid(ga_ref[...].astype(F32))
        sb = _sigmoid(gb_ref[...].astype(F32))
        dpa = (dm * sa).astype(BF16)
        dpb = (dm * sb).astype(BF16)
        dpa_ref[...] = dpa
        dpb_ref[...] = dpb
        dg_ref[:, 0:d] = (dm * pa_ref[...].astype(F32) * sa * (1.0 - sa)).astype(BF16)
        dg_ref[:, d:2 * d] = (dm * pb_ref[...].astype(F32) * sb * (1.0 - sb)).astype(BF16)
        dya = _nt(dpa, wa_ref[...]).astype(BF16)
        dya_ref[...] = dya
        dyb_ref[...] = _nt(dpb, wb_ref[...])
        lane = _lane((tm, LANES))
        for pr in range(ATT_W // LANES):
            pair = slice(pr * LANES, (pr + 1) * LANES)
            prod = dya[:, pair].astype(F32) * ya_ref[:, pair].astype(F32)
            lo = jnp.sum(jnp.where(lane < 64, prod, 0.0), axis=-1, keepdims=True)
            hi = jnp.sum(jnp.where(lane >= 64, prod, 0.0), axis=-1, keepdims=True)
            dd_ref[:, pair] = jnp.where(lane < 64, lo, hi)

    row = pl.BlockSpec((tm, d), lambda i: (i, 0))
    att = pl.BlockSpec((tm, ATT_W), lambda i: (i, 0))
    whole = lambda a: pl.BlockSpec(a.shape, lambda i: (0, 0))
    return pl.pallas_call(
        body, grid=(s // tm,),
        in_specs=[row, whole(w_o), pl.BlockSpec((tm, d), lambda i: (i, 3)), pl.BlockSpec((tm, d), lambda i: (i, 4)), row, row,
                  whole(wo_a), whole(wo_b), att],
        out_specs=[row, row, pl.BlockSpec((tm, 2 * d), lambda i: (i, 0)), att, att, att],
        out_shape=[SDS((s, d), BF16), SDS((s, d), BF16), SDS((s, 2 * d), BF16), SDS((s, ATT_W), BF16),
                   SDS((s, ATT_W), F32), SDS((s, ATT_W), F32)], name="mix_bwd",
        compiler_params=_cp(("parallel",)))(dy, w_o, z, z, pa, pb, wo_a, wo_b, ya)


GELU_C = math.sqrt(2.0 / math.pi)


def _gelu_parts(a):
    a2 = a * a
    th = jnp.tanh(a * (GELU_C + (GELU_C * 0.044715) * a2))
    half = 0.5 * a
    gelu = half + half * th
    dgelu = (0.5 + 0.5 * th) + half * (1.0 - th * th) * (GELU_C + (3.0 * GELU_C * 0.044715) * a2)
    return gelu, dgelu


def _causal_taps(u, before):
    row = _row(u.shape)
    r1 = jnp.where(row == 0, before[7:8, :], pltpu.roll(u, 1, axis=0))
    r2 = jnp.where(row == 0, before[6:7, :], jnp.where(row == 1, before[7:8, :], pltpu.roll(u, 2, axis=0)))
    return r1, r2


def ffn_up(h, wa, wb, cw, cb, *, tm=2048, tn=256):
    s, d = h.shape
    f = wa.shape[1]
    nj = f // tn

    def body(h_ref, wa_ref, wb_ref, cwa_ref, cwb_ref, cba_ref, cbb_ref, ua_ref, ub_ref, ca_ref, cbo_ref, m_ref, carry):
        @pl.when(pl.program_id(1) == 0)
        def _():
            carry[...] = jnp.zeros_like(carry)

        conv = []
        for k, (w_ref, cw_ref, cb_ref, u_ref, c_ref) in enumerate(((wa_ref, cwa_ref, cba_ref, ua_ref, ca_ref),
                                                                   (wb_ref, cwb_ref, cbb_ref, ub_ref, cbo_ref))):
            u16 = _nn(h_ref[...], w_ref[...]).astype(BF16)
            u_ref[...] = u16
            u = u16.astype(F32)
            r1, r2 = _causal_taps(u, carry[k])
            carry[k] = u[tm - 8:tm, :]
            c16 = (cw_ref[0:1, :] * r2 + cw_ref[1:2, :] * r1 + cw_ref[2:3, :] * u + cb_ref[...]).astype(BF16)
            c_ref[...] = c16
            conv.append(c16.astype(F32))
        m_ref[...] = (_gelu_parts(conv[0])[0] * conv[1]).astype(BF16)

    out = pl.BlockSpec((tm, tn), lambda j, i: (i, j))
    return pl.pallas_call(
        body, grid=(nj, s // tm),
        in_specs=[pl.BlockSpec((tm, d), lambda j, i: (i, 0)),
                  pl.BlockSpec((d, tn), lambda j, i: (0, j)), pl.BlockSpec((d, tn), lambda j, i: (0, j)),
                  pl.BlockSpec((3, tn), lambda j, i: (0, j)), pl.BlockSpec((3, tn), lambda j, i: (0, nj + j)),
                  pl.BlockSpec((1, tn), lambda j, i: (0, j)), pl.BlockSpec((1, tn), lambda j, i: (0, nj + j))],
        out_specs=[out] * 5, out_shape=[SDS((s, f), BF16)] * 5,
        scratch_shapes=[pltpu.VMEM((2, 8, tn), F32)], name="ffn_up",
        compiler_params=_cp(("parallel", "arbitrary")))(h, wa, wb, cw, cw, cb, cb)


def ffn_bwd(dm, ua, ub, ca, cbo, cw, *, tm=2048, tn=256):
    s, f = dm.shape
    nj = f // tn
    ni = s // tm

    def body(dm_ref, ua_ref, ub_ref, ca_ref, cbo_ref, cwa_ref, cwb_ref, dua_ref, dub_ref, ga_ref, gb_ref, carry):
        @pl.when(pl.program_id(1) == 0)
        def _():
            carry[...] = jnp.zeros_like(carry)
            ga_ref[...] = jnp.zeros_like(ga_ref)
            gb_ref[...] = jnp.zeros_like(gb_ref)

        row = _row((tm, tn))
        dmv = dm_ref[...].astype(F32)
        gelu, dgelu = _gelu_parts(ca_ref[...].astype(F32))
        dcs = (dmv * cbo_ref[...].astype(F32) * dgelu, dmv * gelu)
        for k, (dc, u_ref, cw_ref, du_ref, g_ref) in enumerate(((dcs[0], ua_ref, cwa_ref, dua_ref, ga_ref),
                                                                (dcs[1], ub_ref, cwb_ref, dub_ref, gb_ref))):
            u = u_ref[...].astype(F32)
            after = carry[k]
            n1 = jnp.where(row == tm - 1, after[0:1, :], pltpu.roll(dc, tm - 1, axis=0))
            n2 = jnp.where(row == tm - 2, after[0:1, :], jnp.where(row == tm - 1, after[1:2, :], pltpu.roll(dc, tm - 2, axis=0)))
            g_ref[0:1, :] += jnp.sum(n2 * u, axis=0, keepdims=True)
            g_ref[1:2, :] += jnp.sum(n1 * u, axis=0, keepdims=True)
            g_ref[2:3, :] += jnp.sum(dc * u, axis=0, keepdims=True)
            g_ref[3:4, :] += jnp.sum(dc, axis=0, keepdims=True)
            du_ref[...] = (cw_ref[2:3, :] * dc + cw_ref[1:2, :] * n1 + cw_ref[0:1, :] * n2).astype(BF16)
            carry[k] = dc[0:8, :]

    tile = pl.BlockSpec((tm, tn), lambda j, i: (ni - 1 - i, j))
    gspec = pl.BlockSpec((8, tn), lambda j, i: (0, j))
    return pl.pallas_call(
        body, grid=(nj, ni),
        in_specs=[tile] * 5 + [pl.BlockSpec((3, tn), lambda j, i: (0, j)), pl.BlockSpec((3, tn), lambda j, i: (0, nj + j))],
        out_specs=[tile, tile, gspec, gspec],
        out_shape=[SDS((s, f), BF16), SDS((s, f), BF16), SDS((8, f), F32), SDS((8, f), F32)],
        scratch_shapes=[pltpu.VMEM((2, 8, tn), F32)], name="ffn_bwd",
        compiler_params=_cp(("parallel", "arbitrary")))(dm, ua, ub, ca, cbo, cw, cw)


def adamw(w, g, m, v, *, name, tr=None):
    r = w.shape[0]
    rest = w.shape[1:]
    if tr is None:
        tr = r
        for cand in (256, 128, 64, 32, 16, 8):
            if r % cand == 0:
                tr = cand
                break

    def body(w_ref, g_ref, m_ref, v_ref, d_ref, nm_ref, nv_ref):
        gv = g_ref[...]
        mn = ADAM_B1 * m_ref[...] + (1.0 - ADAM_B1) * gv
        vn = ADAM_B2 * v_ref[...] + (1.0 - ADAM_B2) * (gv * gv)
        m_hat = mn / (1.0 - ADAM_B1 ** ADAM_STEP)
        v_hat = vn / (1.0 - ADAM_B2 ** ADAM_STEP)
        d_ref[...] = -ADAM_LR * (m_hat / (jnp.sqrt(v_hat) + ADAM_EPS) + ADAM_WD * w_ref[...])
        nm_ref[...] = mn
        nv_ref[...] = vn

    blk = pl.BlockSpec((tr,) + rest, lambda i: (i,) + (0,) * len(rest))
    return pl.pallas_call(body, grid=(r // tr,), in_specs=[blk] * 4, out_specs=[blk] * 3, out_shape=[SDS(w.shape, F32)] * 3,
                          name=name, compiler_params=_cp(("parallel",)))(w, g, m, v)


def adamw_rows_view(w, g_mine, g_full, m, v, c_arr, *, name, tc=256):
    r, _, c = w.shape
    per_half = c // 2 // tc

    def body(c_ref, w_ref, gm_ref, gf_ref, m_ref, v_ref, d_ref, nm_ref, nv_ref, go_ref):
        mine = (pl.program_id(0) >> (per_half.bit_length() - 1)) == c_ref[0]
        gv = jnp.where(mine, gm_ref[...], gf_ref[...])
        mn = ADAM_B1 * m_ref[:, 0, :] + (1.0 - ADAM_B1) * gv
        vn = ADAM_B2 * v_ref[:, 0, :] + (1.0 - ADAM_B2) * (gv * gv)
        m_hat = mn / (1.0 - ADAM_B1 ** ADAM_STEP)
        v_hat = vn / (1.0 - ADAM_B2 ** ADAM_STEP)
        d_ref[:, 0, :] = -ADAM_LR * (m_hat / (jnp.sqrt(v_hat) + ADAM_EPS) + ADAM_WD * w_ref[:, 0, :])
        nm_ref[:, 0, :] = mn
        nv_ref[:, 0, :] = vn
        go_ref[:, 0, :] = gv

    b3 = pl.BlockSpec((r, 1, tc), lambda i, c_ref: (0, 0, i))
    own = pl.BlockSpec((r, tc), lambda i, c_ref: (0, jnp.clip(i - c_ref[0] * per_half, 0, per_half - 1)))
    full = pl.BlockSpec((r, tc), lambda i, c_ref: (0, i))
    grid_spec = pltpu.PrefetchScalarGridSpec(num_scalar_prefetch=1, grid=(c // tc,), in_specs=[b3, own, full, b3, b3],
                                             out_specs=[b3] * 4)
    return pl.pallas_call(body, grid_spec=grid_spec, out_shape=[SDS(w.shape, F32)] * 4, name=name,
                          compiler_params=_cp(("parallel",)))(c_arr, w, g_mine, g_full, m, v)


ANY = pl.BlockSpec(memory_space=pl.ANY)
ICI_KINDS = ("x", "y", "xy")


def _coords():
    return lax.axis_index("x"), lax.axis_index("y"), lax.axis_index("c")


def _peer(kind, x, y, c):
    if kind == "c":
        return (x, y, 1 - c)
    if kind == "x":
        return (1 - x, y, c)
    if kind == "y":
        return (x, 1 - y, c)
    return (1 - x, 1 - y, c)


def _chip_of(p):
    return 2 * p[0] + p[1]


def _half(rows, which):
    h = rows // 2
    return pl.ds(pl.multiple_of(which * h, 16), h)


def _remote(src, dst, send_sem, recv_sem, to):
    return pltpu.make_async_remote_copy(src_ref=src, dst_ref=dst, send_sem=send_sem, recv_sem=recv_sem,
                                        device_id=to, device_id_type=MESH)


def allgather_balanced(shard, *, name):
    r, cols = shard.shape
    h, q = r // 2, r // 4

    def body(in_ref, out_ref, send_sems, recv_sems):
        x, y, c = _coords()
        me, sibling = (x, y, c), (x, y, 1 - c)
        nbr = ((1 - x, y, c), (x, 1 - y, c))
        chip = (2 * (1 - x) + y, 2 * x + (1 - y), 2 * (1 - x) + (1 - y))
        quarter = lambda core, i: pl.ds(pl.multiple_of(core * h + i * q, 16), q)
        sent = []

        def go(src, dst, slot, to):
            cp = _remote(src, dst, send_sems.at[slot], recv_sems.at[slot], to)
            cp.start()
            sent.append(cp)

        def landed(region, slot):
            _remote(region, region, send_sems.at[slot], recv_sems.at[slot], me).wait_recv()

        for i in range(2):
            for k in range(2):
                qi = k if i == 0 else 1 - k
                go(in_ref.at[quarter(c, qi)], out_ref.at[2 * x + y, quarter(c, qi)], 2 * k + qi, nbr[k])
        for k in range(2):
            piece = out_ref.at[chip[k], quarter(c, k)]
            landed(piece, 2 * k + k)
            go(piece, piece, 4 + k, nbr[1 - k])
            go(piece, piece, 6 + 2 * k + k, sibling)
        for k in range(2):
            piece = out_ref.at[chip[k], quarter(c, 1 - k)]
            landed(piece, 2 * k + 1 - k)
            go(piece, piece, 6 + 2 * k + 1 - k, sibling)
        for k in range(2):
            piece = out_ref.at[chip[2], quarter(c, k)]
            landed(piece, 4 + k)
            go(piece, piece, 10 + k, sibling)
        for k in range(2):
            for i in range(2):
                landed(out_ref.at[chip[k], quarter(1 - c, i)], 6 + 2 * k + i)
            landed(out_ref.at[chip[2], quarter(1 - c, k)], 10 + k)
        for cp in sent:
            cp.wait_send()

    return pl.pallas_call(
        body, in_specs=[ANY], out_specs=ANY, out_shape=SDS((4,) + shard.shape, shard.dtype),
        scratch_shapes=[pltpu.SemaphoreType.DMA((12,)), pltpu.SemaphoreType.DMA((12,))], name=name)(shard)


def _allgather_shapes(shards):
    return [SDS((4,) + a.shape, a.dtype) for a in shards]


def _allgather_sems(n):
    return [pltpu.SemaphoreType.DMA((n, 6)), pltpu.SemaphoreType.DMA((n, 6))]


def _allgather_rows(ref, is_halved, which):
    r = ref.shape[0]
    return _half(r, which) if is_halved else pl.ds(0, r)


def _allgather_first(ins, outs, send_sems, recv_sems, halved):
    x, y, c = _coords()
    my_chip = 2 * x + y
    cps = []
    for w in range(len(ins)):
        rows = _allgather_rows(ins[w], halved[w], c)
        for k, kind in enumerate(ICI_KINDS):
            cps.append(_remote(ins[w].at[rows], outs[w].at[my_chip, rows], send_sems.at[w, k], recv_sems.at[w, k],
                               _peer(kind, x, y, c)))
    return cps


def _allgather_start(ins, outs, send_sems, recv_sems, halved):
    for cp in _allgather_first(ins, outs, send_sems, recv_sems, halved):
        cp.start()


def _allgather_finish(ins, outs, send_sems, recv_sems, halved):
    x, y, c = _coords()
    me = (x, y, c)
    second = []
    for w in range(len(ins)):
        for k, kind in enumerate(ICI_KINDS):
            landed = outs[w].at[_chip_of(_peer(kind, x, y, c)), _allgather_rows(ins[w], halved[w], c)]
            _remote(landed, landed, send_sems.at[w, k], recv_sems.at[w, k], me).wait_recv()
            if halved[w]:
                cp = _remote(landed, landed, send_sems.at[w, 3 + k], recv_sems.at[w, 3 + k], _peer("c", x, y, c))
                cp.start()
                second.append(cp)
    for w in range(len(ins)):
        if halved[w]:
            for k, kind in enumerate(ICI_KINDS):
                other = outs[w].at[_chip_of(_peer(kind, x, y, c)), _allgather_rows(ins[w], True, 1 - c)]
                _remote(other, other, send_sems.at[w, 3 + k], recv_sems.at[w, 3 + k], me).wait_recv()
    for cp in _allgather_first(ins, outs, send_sems, recv_sems, halved) + second:
        cp.wait_send()


def _half_of(ref, by_cols, which):
    lead = (slice(None),) * (len(ref.shape) - 2)
    if by_cols:
        h = ref.shape[-1] // 2
        return ref.at[lead + (slice(None), pl.ds(pl.multiple_of(which * h, LANES), h))]
    return ref.at[lead + (_half(ref.shape[-2], which),)]


def _half_shape(shape, by_cols):
    return shape[:-1] + (shape[-1] // 2,) if by_cols else shape[:-2] + (shape[-2] // 2, shape[-1])


def grads_to_sibling(gs, by_cols, *, name):
    n = len(gs)

    def body(*refs):
        ins, outs = refs[:n], refs[n:2 * n]
        send_sems, recv_sems = refs[2 * n:]
        x, y, c = _coords()
        cps = []
        for w in range(n):
            cp = _remote(_half_of(ins[w], by_cols[w], 1 - c), outs[w], send_sems.at[w], recv_sems.at[w], _peer("c", x, y, c))
            cp.start()
            cps.append(cp)
        for cp in cps:
            cp.wait()

    return pl.pallas_call(
        body, in_specs=[ANY] * n, out_specs=[ANY] * n,
        out_shape=[SDS(_half_shape(a.shape, bc), a.dtype) for a, bc in zip(gs, by_cols)],
        scratch_shapes=[pltpu.SemaphoreType.DMA((n,)), pltpu.SemaphoreType.DMA((n,))], name=name)(*gs)


def _to_chips_shapes(ps):
    return [SDS((3,) + a.shape[1:], a.dtype) for a in ps]


def _to_chips_sems(n):
    return [pltpu.SemaphoreType.DMA((n, 3)), pltpu.SemaphoreType.DMA((n, 3))]


def _to_chips_copies(ins, outs, send_sems, recv_sems):
    x, y, c = _coords()
    cps = []
    for w in range(len(ins)):
        for k, kind in enumerate(ICI_KINDS):
            to = _peer(kind, x, y, c)
            cps.append(_remote(ins[w].at[_chip_of(to)], outs[w].at[k], send_sems.at[w, k], recv_sems.at[w, k], to))
    return cps


def _to_chips_start(ins, outs, send_sems, recv_sems):
    for cp in _to_chips_copies(ins, outs, send_sems, recv_sems):
        cp.start()


def _to_chips_finish(ins, outs, send_sems, recv_sems):
    for cp in _to_chips_copies(ins, outs, send_sems, recv_sems):
        cp.wait()


def _to_owners_shapes(ps):
    return [SDS((7, a.shape[1] // 2, a.shape[2]), a.dtype) for a in ps]


def _to_owners_sems(n):
    return [pltpu.SemaphoreType.DMA((n, 7)), pltpu.SemaphoreType.DMA((n, 7))]


def _to_owners_copies(ins, outs, send_sems, recv_sems):
    x, y, c = _coords()
    cps = []
    for w in range(len(ins)):
        rows = ins[w].shape[1]
        for k, kind in enumerate(ICI_KINDS):
            px, py, _ = _peer(kind, x, y, c)
            for h in range(2):
                cps.append(_remote(ins[w].at[2 * px + py, _half(rows, h)], outs[w].at[2 * k + c],
                                   send_sems.at[w, 2 * k + h], recv_sems.at[w, 2 * k + c], (px, py, h)))
        cps.append(_remote(ins[w].at[2 * x + y, _half(rows, 1 - c)], outs[w].at[6], send_sems.at[w, 6], recv_sems.at[w, 6],
                           _peer("c", x, y, c)))
    return cps


def _to_owners_start(ins, outs, send_sems, recv_sems):
    for cp in _to_owners_copies(ins, outs, send_sems, recv_sems):
        cp.start()


def _to_owners_finish(ins, outs, send_sems, recv_sems):
    for cp in _to_owners_copies(ins, outs, send_sems, recv_sems):
        cp.wait_send()
    for w in range(len(ins)):
        for slot in range(7):
            got = outs[w].at[slot]
            _remote(got, got, send_sems.at[w, slot], recv_sems.at[w, slot], _coords()).wait_recv()


EXCHANGES = {"to_chips": (_to_chips_shapes, _to_chips_sems, _to_chips_start, _to_chips_finish),
             "to_owners": (_to_owners_shapes, _to_owners_sems, _to_owners_start, _to_owners_finish)}


def halves_to_full(hs, by_cols, *, name):
    n = len(hs)

    def body(*refs):
        ins, outs = refs[:n], refs[n:2 * n]
        send_sems, recv_sems = refs[2 * n:]
        x, y, c = _coords()
        cps = []
        for w in range(n):
            cp = _remote(ins[w], _half_of(outs[w], by_cols[w], c), send_sems.at[w], recv_sems.at[w], _peer("c", x, y, c))
            cp.start()
            cps.append(cp)
        for cp in cps:
            cp.wait()

    return pl.pallas_call(
        body, in_specs=[ANY] * n, out_specs=[ANY] * n,
        out_shape=[SDS((a.shape[0], 2 * a.shape[1]) if bc else (2 * a.shape[0], a.shape[1]), a.dtype)
                   for a, bc in zip(hs, by_cols)],
        scratch_shapes=[pltpu.SemaphoreType.DMA((n,)), pltpu.SemaphoreType.DMA((n,))],
        name=name)(*hs)


def _row_tile(rows):
    for cand in (256, 192, 176, 128, 64, 32, 16):
        if rows % cand == 0:
            return cand
    return rows


def chip_sum(g, recv, c_arr, by_cols, *, name):
    _, r, cols = g.shape

    def body(c_ref, g_ref, r_ref, f_ref, b_ref):
        tot = g_ref[...] + r_ref[...]
        f_ref[...] = tot
        b_ref[...] = tot.astype(BF16)

    if by_cols:
        tc = 4 * LANES
        nblk = cols // 2 // tc
        shape = (4, r, cols // 2)
        blk = pl.BlockSpec((None, r, tc), lambda j, i, c_ref: (j, 0, i))
        mine = pl.BlockSpec((None, r, tc), lambda j, i, c_ref: (j, 0, c_ref[0] * nblk + i))
    else:
        tr = _row_tile(r // 2)
        nblk = r // 2 // tr
        shape = (4, r // 2, cols)
        blk = pl.BlockSpec((None, tr, cols), lambda j, i, c_ref: (j, i, 0))
        mine = pl.BlockSpec((None, tr, cols), lambda j, i, c_ref: (j, c_ref[0] * nblk + i, 0))
    grid_spec = pltpu.PrefetchScalarGridSpec(num_scalar_prefetch=1, grid=(4, nblk), in_specs=[mine, blk], out_specs=[blk, blk])
    return pl.pallas_call(body, grid_spec=grid_spec, out_shape=[SDS(shape, F32), SDS(shape, BF16)],
                          name=name, compiler_params=_cp(("parallel", "parallel")))(c_arr, g, recv)


def final_sum(pf, recv, chip_arr, *, name):
    _, h, cols = pf.shape
    tr = _row_tile(h)

    def body(chip_ref, p_ref, r_ref, o_ref):
        o_ref[...] = ((p_ref[...] + r_ref[0].astype(F32)) + r_ref[1].astype(F32)) + r_ref[2].astype(F32)

    grid_spec = pltpu.PrefetchScalarGridSpec(
        num_scalar_prefetch=1, grid=(h // tr,),
        in_specs=[pl.BlockSpec((None, tr, cols), lambda i, chip_ref: (chip_ref[0], i, 0)),
                  pl.BlockSpec((3, tr, cols), lambda i, chip_ref: (0, i, 0))],
        out_specs=pl.BlockSpec((tr, cols), lambda i, chip_ref: (i, 0)))
    return pl.pallas_call(body, grid_spec=grid_spec, out_shape=SDS((h, cols), F32), name=name,
                          compiler_params=_cp(("parallel",)))(chip_arr, pf, recv)


def owner_sum(g, recv, pos_arr, *, name):
    _, r, cols = g.shape
    h = r // 2
    tr = _row_tile(h)
    nblk = h // tr

    def body(pos_ref, g_ref, r_ref, o_ref):
        tot = g_ref[...]
        for slot in range(7):
            tot = tot + r_ref[slot].astype(F32)
        o_ref[...] = tot

    grid_spec = pltpu.PrefetchScalarGridSpec(
        num_scalar_prefetch=1, grid=(nblk,),
        in_specs=[pl.BlockSpec((None, tr, cols), lambda i, pos: (pos[0], pos[1] * nblk + i, 0)),
                  pl.BlockSpec((7, tr, cols), lambda i, pos: (0, i, 0))],
        out_specs=pl.BlockSpec((tr, cols), lambda i, pos: (i, 0)))
    return pl.pallas_call(body, grid_spec=grid_spec, out_shape=SDS((h, cols), F32), name=name,
                          compiler_params=_cp(("parallel",)))(pos_arr, g, recv)


def allreduce_small(v, *, name):
    rws, cols = v.shape

    def body(v_ref, all_ref, sum_ref, send_sems, recv_sems, local_sem):
        x, y, c = _coords()
        me, sibling = (x, y, c), (x, y, 1 - c)
        chips = [(1 - x, y), (x, 1 - y), (1 - x, 1 - y)]

        def rows(px, py, pc):
            return all_ref.at[pl.ds(pl.multiple_of((4 * px + 2 * py + pc) * rws, 8), rws), :]

        def copy(k, block, to, src=None):
            return _remote(rows(*block) if src is None else src, rows(*block), send_sems.at[k], recv_sems.at[k], to)

        mine = pltpu.make_async_copy(v_ref, rows(*me), local_sem)
        mine.start()
        first = [copy(0, me, sibling, src=v_ref)]
        first += [copy(1 + j, me, (*chip, c), src=v_ref) for j, chip in enumerate(chips)]
        for cp in first:
            cp.start()
        passed = [copy(4 + j, (*chip, c), sibling) for j, chip in enumerate(chips)]
        for j, chip in enumerate(chips):
            copy(1 + j, (*chip, c), me).wait_recv()
            passed[j].start()
        copy(0, sibling, me).wait_recv()
        for j, chip in enumerate(chips):
            copy(4 + j, (*chip, 1 - c), me).wait_recv()
        for cp in first + passed:
            cp.wait_send()
        mine.wait()
        tot = all_ref[0:rws, :]
        for dev in range(1, 8):
            tot = tot + all_ref[dev * rws:(dev + 1) * rws, :]
        sum_ref[...] = tot

    vm = pl.BlockSpec(memory_space=pltpu.VMEM)
    return pl.pallas_call(
        body, in_specs=[vm], out_specs=[vm, vm],
        out_shape=[SDS((8 * rws, cols), v.dtype), SDS((rws, cols), v.dtype)],
        scratch_shapes=[pltpu.SemaphoreType.DMA((7,)), pltpu.SemaphoreType.DMA((7,)), pltpu.SemaphoreType.DMA],
        name=name)(v)[1]


def _pack_rows(parts, rows):
    out = []
    for a, r in zip(parts, rows):
        flat = a.reshape(-1)
        flat = jnp.pad(flat, (0, r * LANES - flat.shape[0]))
        out.append(flat.reshape(r, LANES))
    return jnp.concatenate(out, axis=0)


def _unpack_rows(packed, shapes, rows):
    out, at = [], 0
    for shp, r in zip(shapes, rows):
        size = int(np.prod(shp))
        out.append(packed[at:at + r].reshape(-1)[:size].reshape(shp))
        at += r
    return out


def kernel(x, g_pre_mix, w_in, b_forget, w_o_fox, w_o_dil, w_out, g_post_mix, g_pre_ffn, w_up, conv_w, conv_b, w_down, g_post_ffn, loss_target, m_g_pre_mix, m_w_in, m_b_forget, m_w_o_fox, m_w_o_dil, m_w_out, m_g_post_mix, m_g_pre_ffn, m_w_up, m_conv_w, m_conv_b, m_w_down, m_g_post_ffn, v_g_pre_mix, v_w_in, v_b_forget, v_w_o_fox, v_w_o_dil, v_w_out, v_g_post_mix, v_g_pre_ffn, v_w_up, v_conv_w, v_conv_b, v_w_down, v_g_post_ffn):
    xi, yi, ci = _coords()
    chip = 2 * xi + yi
    c_arr = jnp.reshape(ci, (1,)).astype(jnp.int32)
    chip_arr = jnp.reshape(chip, (1,)).astype(jnp.int32)
    xs = x[0]
    target = loss_target[0]
    s, d = xs.shape
    f_half = w_down.shape[1] * 4
    cols_in = w_in.shape[2]

    big = (w_in, w_o_fox, w_o_dil, w_out, w_up, w_down)
    shards = [w[0].astype(BF16) for w in big]
    a_in = allgather_balanced(shards[0], name="allgather_w_in")
    w_in_full = jnp.concatenate([jnp.where(chip == j, shards[0], a_in[j]) for j in range(4)], axis=1)
    nf = N_HEADS
    e_a, e_b = 3 * ATT_W, 3 * ATT_W + nf
    wz = jnp.concatenate([w_in_full[:, :e_a], w_in_full[:, e_b:]], axis=1)
    wf = jnp.pad(w_in_full[:, e_a:e_b], ((0, 0), (0, LANES - nf)))
    cb = conv_b
    bfo = jnp.pad(b_forget, ((0, 0), (0, LANES - nf)))

    h1 = rmsnorm_fwd(xs, g_pre_mix)
    z = mm([(h1, d, 0)], [(wz, d, 0)], nt=False, out_dtype=BF16, tm=s, tn=512, name="in_proj")
    fa = mm([(h1, d, 0)], [(wf, d, 0)], nt=False, out_dtype=F32, tm=s, tn=LANES, name="in_proj_forget")
    q_aug, k_aug, v_aug = fox_prep(z, fa, bfo)
    later = shards[1:] + [conv_w[0]]
    ya, lse_a, *late = fox_fwd(q_aug, k_aug, v_aug, gather=later, halved=[True] * 5 + [False], hps=N_HEADS)
    a_of, a_od, a_out, a_up, a_down, a_cw = [
        lax.dynamic_update_index_in_dim(a4, own, chip, 0) for a4, own in zip(late, later)]
    cw = jnp.concatenate([a_cw[j] for j in range(4)], axis=1)
    wo_a = jnp.concatenate([a_of[j] for j in range(4)], axis=1)
    wo_b = jnp.concatenate([a_od[j] for j in range(4)], axis=1)
    w_o = a_out.reshape(d, d)
    w_dn = a_down.reshape(f_half, d)
    wu_a = jnp.concatenate([a_up[0], a_up[1]], axis=1)
    wu_b = jnp.concatenate([a_up[2], a_up[3]], axis=1)
    cos_t, sin_t = rope_cos_sin(s)
    yb, lse_b = dil_fwd_all(z, cos_t, sin_t)
    pa, pb, mixed = gate_mix(ya, yb, wo_a, wo_b, z)
    y1, x1, h2 = proj_norm_res(mixed, w_o, g_post_mix, xs, g_pre_ffn, tm=1024, name="out_proj")
    ua, ub, conv_a, conv_bh, mid = ffn_up(h2, wu_a, wu_b, cw, cb)
    dout, dy2, gg_post_ffn, sq = proj_norm_loss(mid, w_dn, g_post_ffn, x1, target, name="down_proj")
    loss = lax.psum(0.5 * sq[0, 0] / d, ("x", "y", "c"))

    dmid = mm([(dy2, d, 0)], [(w_dn, d, 0)], nt=True, out_dtype=BF16, tm=2048, tn=f_half // 2, name="down_dgrad")
    dw_down, dw_down16 = wgrad((mid, f_half, 0), dy2, tk=f_half // 2, tn=1024, ts=2048, name="down_wgrad", bf16_copy=True)
    dua, dub, gc_a, gc_b = ffn_bwd(dmid, ua, ub, conv_a, conv_bh, cw)
    dx1, dy1, gg_pre_ffn, gg_post_mix = mm_norm_bwd(
        [(dua, f_half, 0), (dub, f_half, 0)], [(wu_a, f_half, 0), (wu_b, f_half, 0)],
        [(x1, g_pre_ffn, dout, F32), (y1, g_post_mix, None, BF16)], name="up_dgrad")
    dw_up = None
    for k, du in enumerate((dua, dub)):
        dw_up = wgrad((h2, d, 0), du, tk=1024, tn=f_half // 2, ts=2048, name=f"up_wgrad_{k}", chip_major=True,
                      slabs=(4, 2 * k), into=dw_up, bf16_copy=True)
    g_ffn = [(dw_up[0], dw_up[1]), (dw_down.reshape(4, f_half // 4, d), dw_down16.reshape(4, f_half // 4, d))]
    dw_out, dw_out16 = wgrad((mixed, d, 0), dy1, tk=1024, tn=1024, ts=2048, name="out_wgrad", bf16_copy=True)
    dpa, dpb, dz_g, dya, dyb, dd_a = mix_bwd(dy1, w_o, z, pa, pb, wo_a, wo_b, ya)
    by_chip_cols = lambda a: jnp.stack([a[:, j * (d // 4):(j + 1) * (d // 4)] for j in range(4)], axis=0)
    dw_of = [by_chip_cols(a) for a in wgrad((ya, ATT_W, 0), dpa, tk=ATT_W, tn=d, ts=1024, name="fox_o_wgrad", bf16_copy=True)]
    dw_od = [by_chip_cols(a) for a in wgrad((yb, ATT_W, 0), dpb, tk=ATT_W, tn=d, ts=1024, name="dil_o_wgrad", bf16_copy=True)]
    g_mix = [dw_of, dw_od, (dw_out.reshape(4, d // 4, d), dw_out16.reshape(4, d // 4, d))]
    dq_aug, dk_aug, dv_a, *got_ffn = fox_bwd(q_aug, k_aug, z, dya, lse_a, dd_a, exchange=[g[1] for g in g_ffn], kind="to_owners")
    dz_a, dfa, gg_bf = fox_post(dq_aug, dk_aug, dv_a, fa, bfo)
    *dz_b, got_of, got_od, got_out = dil_bwd_all(z, cos_t, sin_t, dyb, lse_b, yb, exchange=[g[1] for g in g_mix],
                                                 kind="to_owners")
    got_mix = [got_of, got_od, got_out]
    dwt_a = wgrad((dz_a, e_a, 0), h1, tk=e_a // 2, tn=d, ts=2048, name="in_wgrad_a")
    dwt_b = [wgrad((part, ATT_W, 0), h1, tk=ATT_W, tn=d, ts=2048, name=f"in_wgrad_b{k}") for k, part in enumerate(dz_b)]
    dwt_g = wgrad((dz_g, 2 * d, 0), h1, tk=d, tn=d, ts=2048, name="in_wgrad_g")
    dwt_f = wgrad((dfa, LANES, 0), h1, tk=LANES, tn=d, ts=2048, name="in_wgrad_f")
    dwt_full = jnp.concatenate([dwt_a, dwt_f[:nf], *dwt_b, dwt_g], axis=0)
    dw_in = jnp.stack([dwt_full[j * cols_in:(j + 1) * cols_in] for j in range(4)], axis=0)
    from_sib = grads_to_sibling([dw_in], [True], name="grads_to_sibling_in")
    sum_in = chip_sum(dw_in, from_sib[0], c_arr, True, name="chip_sum_w_in")
    grad_x, gg_pre_mix, got_in = mm_norm_bwd(
        [(dz_a, e_a, 0), *[(part, ATT_W, 0) for part in dz_b], (dz_g, d, 0), (dz_g, d, 1), (dfa, LANES, 0)],
        [(wz, e_a, 0), *[(wz, ATT_W, Z_QB + k) for k in range(3)], (wz, d, 3), (wz, d, 4), (wf, LANES, 0)],
        [(xs, g_pre_mix, dx1, F32)], exchange=[sum_in[1]], name="in_dgrad")

    names = ("w_in", "w_o_fox", "w_o_dil", "w_out", "w_up", "w_down")
    pos_arr = jnp.concatenate([chip_arr, c_arr])
    halves = [final_sum(sum_in[0], got_in, chip_arr, name="final_sum_w_in")] + [
        owner_sum(g[0], got, pos_arr, name=f"owner_sum_{nm}") for g, got, nm in zip(g_mix + g_ffn, got_mix + got_ffn, names[1:])]
    from_half = halves_to_full(halves, [True] + [False] * 5, name="halves_to_full")
    g_big = [None] + [lax.dynamic_update_slice_in_dim(full, mine, ci * mine.shape[0], axis=0)
                      for full, mine in zip(from_half[1:], halves[1:])]
    upd_big = [adamw(w[0], g, m[0], v[0], name=f"adamw_{nm}") for w, g, m, v, nm in list(zip(
        big, g_big, (m_w_in, m_w_o_fox, m_w_o_dil, m_w_out, m_w_up, m_w_down),
        (v_w_in, v_w_o_fox, v_w_o_dil, v_w_out, v_w_up, v_w_down), names))[1:]]
    to_t = lambda a: jnp.transpose(a, (2, 0, 1))
    from_t = lambda a: jnp.transpose(a, (1, 2, 0))
    *upd_in, g_in_t = adamw_rows_view(to_t(w_in), halves[0], from_half[0], to_t(m_w_in), to_t(v_w_in), c_arr,
                                      name="adamw_w_in")

    g_cw_loc = jnp.concatenate([gc_a[0:3], gc_b[0:3]], axis=1)
    g_cb_loc = jnp.concatenate([gc_a[3:4], gc_b[3:4]], axis=1)
    small_loc = [gg_pre_mix, gg_post_mix, gg_pre_ffn, gg_post_ffn, g_cb_loc, gg_bf[:, :nf], g_cw_loc]
    red_rows = (8, 8, 8, 8, 48, 8, 136)
    red = allreduce_small(_pack_rows(small_loc, red_rows), name="allreduce_small")
    g_pm, g_qm, g_pf, g_qf, g_cb, g_bf, g_cw_full = _unpack_rows(red, [a.shape for a in small_loc], red_rows)
    cols_cw = conv_w.shape[2]
    g_cw = lax.dynamic_slice_in_dim(g_cw_full, chip * cols_cw, cols_cw, axis=1)
    small_w = (g_pre_mix, g_post_mix, g_pre_ffn, g_post_ffn, conv_b, b_forget, conv_w[0])
    small_m = (m_g_pre_mix, m_g_post_mix, m_g_pre_ffn, m_g_post_ffn, m_conv_b, m_b_forget, m_conv_w[0])
    small_v = (v_g_pre_mix, v_g_post_mix, v_g_pre_ffn, v_g_post_ffn, v_conv_b, v_b_forget, v_conv_w[0])
    small_g = (g_pm, g_qm, g_pf, g_qf, g_cb, g_bf, g_cw)
    small_names = ("g_pre_mix", "g_post_mix", "g_pre_ffn", "g_post_ffn", "conv_b", "b_forget", "conv_w")
    per_param = [adamw(w, g, m, v, name=f"adamw_{nm}") for w, g, m, v, nm in zip(small_w, small_g, small_m, small_v, small_names)]
    upd_small = [[u[j] for u in per_param] for j in range(3)]

    order = ("g_pre_mix", "w_in", "b_forget", "w_o_fox", "w_o_dil", "w_out", "g_post_mix", "g_pre_ffn", "w_up", "conv_w",
             "conv_b", "w_down", "g_post_ffn")
    grads, deltas, new_ms, new_vs = {}, {}, {}, {}
    grads["w_in"] = from_t(g_in_t)
    deltas["w_in"], new_ms["w_in"], new_vs["w_in"] = (from_t(a) for a in upd_in)
    for k, nm in enumerate(names[1:]):
        grads[nm] = g_big[k + 1][None]
        deltas[nm], new_ms[nm], new_vs[nm] = (a[None] for a in upd_big[k])
    for k, nm in enumerate(small_names):
        lead = (lambda a: a[None]) if nm == "conv_w" else (lambda a: a)
        grads[nm] = lead(small_g[k])
        deltas[nm], new_ms[nm], new_vs[nm] = (lead(upd_small[j][k]) for j in range(3))
    return (loss, grad_x[None], *[grads[nm] for nm in order], *[deltas[nm] for nm in order],
            *[new_ms[nm] for nm in order], *[new_vs[nm] for nm in order])
```

```python
import functools
import math

import numpy as np
import jax
import jax.numpy as jnp
from jax import lax
from jax.experimental import pallas as pl
from jax.experimental.pallas import tpu as pltpu

F32 = jnp.float32
BF16 = jnp.bfloat16
SDS = jax.ShapeDtypeStruct
MESH = pl.DeviceIdType.MESH

HEAD_DIM = 64
N_HEADS = 8
LANES = 128
ATT_W = N_HEADS * HEAD_DIM
DIL_PATTERNS = ((128, 1), (512, 4), (2048, 16))
DIL_BLK = 128
ROPE_DIM = HEAD_DIM // 4
ROPE_THETA = 500000.0
RMS_EPS = 1e-6
NEG = -1e30
QK_SCALE = 1.0 / math.sqrt(HEAD_DIM)
ADAM_LR, ADAM_B1, ADAM_B2, ADAM_EPS, ADAM_WD, ADAM_STEP = 0.001, 0.9, 0.999, 1e-08, 0.01, 10
VMEM_LIMIT = 56 * 1024 * 1024

Z_QA, Z_KA, Z_VA, Z_QB, Z_KB, Z_VB = 0, 1, 2, 3, 4, 5
Z_W = 5120


def _cp(sem):
    return pltpu.CompilerParams(dimension_semantics=sem, vmem_limit_bytes=VMEM_LIMIT)


def _nt(a, b):
    return lax.dot_general(a, b, (((1,), (1,)), ((), ())), preferred_element_type=F32)


def _tn(a, b):
    return lax.dot_general(a, b, (((0,), (0,)), ((), ())), preferred_element_type=F32)


def _nn(a, b):
    return jnp.dot(a, b, preferred_element_type=F32)


def _lane(shape):
    return lax.broadcasted_iota(jnp.int32, shape, 1)


def _row(shape):
    return lax.broadcasted_iota(jnp.int32, shape, 0)


def rmsnorm_fwd(x, g, *, tm=1024):
    s, d = x.shape

    def body(x_ref, g_ref, h_ref):
        xv = x_ref[...]
        inv = lax.rsqrt(jnp.mean(xv * xv, axis=-1, keepdims=True) + RMS_EPS)
        h_ref[...] = (xv * inv * g_ref[...]).astype(h_ref.dtype)

    return pl.pallas_call(
        body, grid=(s // tm,),
        in_specs=[pl.BlockSpec((tm, d), lambda i: (i, 0)), pl.BlockSpec((1, d), lambda i: (0, 0))],
        out_specs=pl.BlockSpec((tm, d), lambda i: (i, 0)),
        out_shape=SDS((s, d), BF16), name="rmsnorm_fwd", compiler_params=_cp(("parallel",)))(x, g)


def mm(a_views, b_views, *, nt, out_dtype, tm, tn, name):
    n_p = len(a_views)
    m = a_views[0][0].shape[0]
    n = b_views[0][0].shape[0] if nt else b_views[0][0].shape[1]

    def body(*refs):
        o_ref = refs[-1]
        acc = None
        for p in range(n_p):
            av = refs[p][...].astype(BF16)
            bv = refs[n_p + p][...].astype(BF16)
            dv = _nt(av, bv) if nt else _nn(av, bv)
            acc = dv if acc is None else acc + dv
        o_ref[...] = acc.astype(o_ref.dtype)

    in_specs = []
    for arr, w, blk in a_views:
        in_specs.append(pl.BlockSpec((tm, w), functools.partial(lambda i, j, blk: (i, blk), blk=blk)))
    for arr, w, blk in b_views:
        if nt:
            in_specs.append(pl.BlockSpec((tn, w), functools.partial(lambda i, j, blk: (j, blk), blk=blk)))
        else:
            in_specs.append(pl.BlockSpec((w, tn), lambda i, j: (0, j)))
    return pl.pallas_call(
        body, grid=(m // tm, n // tn), in_specs=in_specs,
        out_specs=pl.BlockSpec((tm, tn), lambda i, j: (i, j)),
        out_shape=SDS((m, n), out_dtype), name=name,
        compiler_params=_cp(("parallel", "parallel")))(*[a[0] for a in a_views], *[b[0] for b in b_views])


def wgrad(a_view, g, *, tk, tn, ts, name, chip_major=False, slabs=None, into=None, bf16_copy=False):
    arr, ka, blk = a_view
    s, n = g.shape
    ns = s // ts
    total, first = slabs if slabs else (n // tn, 0)
    n_into = 0 if into is None else (2 if bf16_copy else 1)

    def body(a_ref, g_ref, *rest):
        o_ref = rest[n_into]

        @pl.when(pl.program_id(2) == 0)
        def _():
            o_ref[...] = jnp.zeros_like(o_ref)

        o_ref[...] += _tn(a_ref[...].astype(BF16), g_ref[...].astype(BF16))
        if bf16_copy:
            @pl.when(pl.program_id(2) == ns - 1)
            def _():
                rest[n_into + 1][...] = o_ref[...].astype(BF16)

    if chip_major:
        out_spec = pl.BlockSpec((None, tk, tn), lambda i, j, k: (first + j, i, 0))
        shape = (total, ka, tn)
    else:
        out_spec = pl.BlockSpec((tk, tn), lambda i, j, k: (i, j))
        shape = (ka, n)
    in_specs = [pl.BlockSpec((ts, tk), lambda i, j, k: (k, blk * (ka // tk) + i)),
                pl.BlockSpec((ts, tn), lambda i, j, k: (k, j))]
    args = [arr, g]
    if into is not None:
        earlier = list(into) if bf16_copy else [into]
        in_specs += [pl.BlockSpec(memory_space=pl.ANY)] * len(earlier)
        args += earlier
    out = pl.pallas_call(
        body, grid=(ka // tk, n // tn, ns), in_specs=in_specs,
        out_specs=[out_spec, out_spec] if bf16_copy else out_spec,
        out_shape=[SDS(shape, F32), SDS(shape, BF16)] if bf16_copy else SDS(shape, F32), name=name,
        input_output_aliases={2 + k: k for k in range(n_into)},
        compiler_params=_cp(("parallel", "parallel", "arbitrary")))(*args)
    return out


def _norm_bwd_rows(dh, xh, inv, g):
    dxh = dh * g
    dx = inv * (dxh - xh * jnp.mean(dxh * xh, axis=-1, keepdims=True))
    return dx, jnp.sum((dh * xh).reshape(dh.shape[0] // 8, 8, dh.shape[1]), axis=0)


def proj_norm_res(a, w, g, xres, g_next, *, tm=512, name):
    s, k = a.shape
    d = w.shape[1]

    def body(a_ref, w_ref, g_ref, x_ref, gn_ref, y_ref, o_ref, h_ref):
        y = _nn(a_ref[...], w_ref[...])
        inv = lax.rsqrt(jnp.mean(y * y, axis=-1, keepdims=True) + RMS_EPS)
        xn = x_ref[...] + y * inv * g_ref[...]
        y_ref[...] = y
        o_ref[...] = xn
        inv_n = lax.rsqrt(jnp.mean(xn * xn, axis=-1, keepdims=True) + RMS_EPS)
        h_ref[...] = (xn * inv_n * gn_ref[...]).astype(h_ref.dtype)

    row = pl.BlockSpec((tm, d), lambda i: (i, 0))
    vec = pl.BlockSpec((1, d), lambda i: (0, 0))
    return pl.pallas_call(
        body, grid=(s // tm,),
        in_specs=[pl.BlockSpec((tm, k), lambda i: (i, 0)), pl.BlockSpec((k, d), lambda i: (0, 0)), vec, row, vec],
        out_specs=[row, row, row], out_shape=[SDS((s, d), F32), SDS((s, d), F32), SDS((s, d), BF16)], name=name,
        compiler_params=_cp(("parallel",)))(a, w, g, xres, g_next)


def proj_norm_loss(a, w, g, xres, target, *, tm=512, name):
    s, k = a.shape
    d = w.shape[1]
    n = s // tm

    def body(a_ref, w_ref, g_ref, x_ref, t_ref, do_ref, dy_ref, dg_ref, l_ref, acc):
        i = pl.program_id(0)

        @pl.when(i == 0)
        def _():
            acc[...] = jnp.zeros_like(acc)
            l_ref[...] = jnp.zeros_like(l_ref)

        y = _nn(a_ref[...], w_ref[...])
        inv = lax.rsqrt(jnp.mean(y * y, axis=-1, keepdims=True) + RMS_EPS)
        yh = y * inv
        err = x_ref[...] + yh * g_ref[...] - t_ref[...]
        dout = err * (1.0 / d)
        do_ref[...] = dout
        l_ref[...] += jnp.sum(jnp.sum(err * err, axis=1, keepdims=True), axis=0, keepdims=True)
        dy, part = _norm_bwd_rows(dout, yh, inv, g_ref[...])
        dy_ref[...] = dy.astype(dy_ref.dtype)
        acc[...] += part

        @pl.when(i == n - 1)
        def _():
            dg_ref[...] = jnp.sum(acc[...], axis=0, keepdims=True)

    row = pl.BlockSpec((tm, d), lambda i: (i, 0))
    vec = pl.BlockSpec((1, d), lambda i: (0, 0))
    return pl.pallas_call(
        body, grid=(n,),
        in_specs=[pl.BlockSpec((tm, k), lambda i: (i, 0)), pl.BlockSpec((k, d), lambda i: (0, 0)), vec, row, row],
        out_specs=[row, row, vec, pl.BlockSpec((1, 1), lambda i: (0, 0))],
        out_shape=[SDS((s, d), F32), SDS((s, d), BF16), SDS((1, d), F32), SDS((1, 1), F32)],
        scratch_shapes=[pltpu.VMEM((8, d), F32)], name=name, compiler_params=_cp(("arbitrary",)))(a, w, g, xres, target)


def mm_norm_bwd(a_views, b_views, stages, exchange=(), *, tm=256, name):
    n_p, n_s, ne = len(a_views), len(stages), len(exchange)
    s = a_views[0][0].shape[0]
    d = b_views[0][0].shape[0]
    n = s // tm
    has_res = [st[2] is not None for st in stages]

    def body(*refs):
        a_refs, b_refs = refs[:n_p], refs[n_p:2 * n_p]
        at = 2 * n_p
        st_refs = []
        for k in range(n_s):
            cnt = 3 if has_res[k] else 2
            st_refs.append(refs[at:at + cnt])
            at += cnt
        e_ins = refs[at:at + ne]
        at += ne
        dx_refs, dg_refs = refs[at:at + n_s], refs[at + n_s:at + 2 * n_s]
        at += 2 * n_s
        e_outs = refs[at:at + ne]
        at += ne
        accs = refs[at:at + n_s]
        comm = (e_ins, e_outs) + tuple(refs[at + n_s:])
        i = pl.program_id(0)

        @pl.when(i == 0)
        def _():
            for acc in accs:
                acc[...] = jnp.zeros_like(acc)
            if ne:
                _to_chips_start(*comm)

        dh = None
        for p in range(n_p):
            part = _nt(a_refs[p][...].astype(BF16), b_refs[p][...].astype(BF16))
            dh = part if dh is None else dh + part
        for k in range(n_s):
            xv = st_refs[k][0][...]
            inv = lax.rsqrt(jnp.mean(xv * xv, axis=-1, keepdims=True) + RMS_EPS)
            dx, part = _norm_bwd_rows(dh, xv * inv, inv, st_refs[k][1][...])
            if has_res[k]:
                dx = dx + st_refs[k][2][...]
            dx_refs[k][...] = dx.astype(dx_refs[k].dtype)
            accs[k][...] += part
            dh = dx

        @pl.when(i == n - 1)
        def _():
            for k in range(n_s):
                dg_refs[k][...] = jnp.sum(accs[k][...], axis=0, keepdims=True)
            if ne:
                _to_chips_finish(*comm)

    row = pl.BlockSpec((tm, d), lambda i: (i, 0))
    vec = pl.BlockSpec((1, d), lambda i: (0, 0))
    in_specs, args = [], []
    for arr, w, blk in a_views:
        in_specs.append(pl.BlockSpec((tm, w), functools.partial(lambda i, blk: (i, blk), blk=blk)))
        args.append(arr)
    for arr, w, blk in b_views:
        in_specs.append(pl.BlockSpec((d, w), functools.partial(lambda i, blk: (0, blk), blk=blk)))
        args.append(arr)
    for x, g, res, _ in stages:
        in_specs += [row, vec] + ([row] if res is not None else [])
        args += [x, g] + ([res] if res is not None else [])
    return pl.pallas_call(
        body, grid=(n,), in_specs=in_specs + [ANY] * ne,
        out_specs=[row] * n_s + [vec] * n_s + [ANY] * ne,
        out_shape=[SDS((s, d), st[3]) for st in stages] + [SDS((1, d), F32)] * n_s + _to_chips_shapes(exchange),
        scratch_shapes=[pltpu.VMEM((8, d), F32)] * n_s + (_to_chips_sems(ne) if ne else []), name=name,
        compiler_params=_cp(("arbitrary",)))(*args, *exchange)


def _split3(v):
    hi = v.astype(BF16).astype(F32)
    r = v - hi
    mid = r.astype(BF16).astype(F32)
    lo = (r - mid).astype(BF16).astype(F32)
    return hi, mid, lo


def _tri(n, upper):
    r = np.arange(n)
    m = (r[:, None] <= r[None, :]) if upper else (r[:, None] >= r[None, :])
    return jnp.asarray(m.astype(np.float32))


def fox_prep(z, fa, bfo, *, tb=512):
    s = z.shape[0]
    n = s // tb

    def body(q_ref, k_ref, v_ref, fa_ref, b_ref, tri_ref, qa_ref, ka_ref, va_ref, carry):
        @pl.when(pl.program_id(0) == 0)
        def _():
            carry[...] = jnp.zeros_like(carry)

        xv = fa_ref[...] + b_ref[...]
        logf = jnp.minimum(xv, 0.0) - jnp.log(1.0 + jnp.exp(-jnp.abs(xv)))
        csum = jnp.dot(tri_ref[...], logf, preferred_element_type=F32, precision=lax.Precision.HIGHEST) + carry[0:1, :]
        carry[0:1, :] = csum[tb - 1:tb, :]
        lane = _lane((tb, LANES))
        for h in range(N_HEADS):
            hi, mid, lo = _split3(csum[:, h:h + 1])
            pair = (h // 2) * LANES
            qv = q_ref[:, pair:pair + LANES].astype(F32)
            kv = k_ref[:, pair:pair + LANES].astype(F32)
            vv = v_ref[:, pair:pair + LANES].astype(F32)
            if h % 2:
                qv = pltpu.roll(qv, 64, axis=1)
                kv = pltpu.roll(kv, 64, axis=1)
                vv = pltpu.roll(vv, 64, axis=1)
            va_ref[:, h * LANES:(h + 1) * LANES] = jnp.where(lane < 64, vv, jnp.where(lane == 64, 1.0, 0.0)).astype(BF16)
            one = jnp.where((lane >= 67) & (lane < 70), 1.0, 0.0)
            q_x = jnp.where(lane == 64, hi, jnp.where(lane == 65, mid, jnp.where(lane == 66, lo, one)))
            one = jnp.where((lane >= 64) & (lane < 67), 1.0, 0.0)
            k_x = jnp.where(lane == 67, -hi, jnp.where(lane == 68, -mid, jnp.where(lane == 69, -lo, one)))
            qa_ref[:, h * LANES:(h + 1) * LANES] = jnp.where(lane < 64, qv * QK_SCALE, q_x).astype(BF16)
            ka_ref[:, h * LANES:(h + 1) * LANES] = jnp.where(lane < 64, kv, k_x).astype(BF16)

    return pl.pallas_call(
        body, grid=(n,),
        in_specs=[pl.BlockSpec((tb, ATT_W), lambda i: (i, Z_QA)), pl.BlockSpec((tb, ATT_W), lambda i: (i, Z_KA)),
                  pl.BlockSpec((tb, ATT_W), lambda i: (i, Z_VA)),
                  pl.BlockSpec((tb, LANES), lambda i: (i, 0)), pl.BlockSpec((1, LANES), lambda i: (0, 0)),
                  pl.BlockSpec((tb, tb), lambda i: (0, 0))],
        out_specs=[pl.BlockSpec((tb, N_HEADS * LANES), lambda i: (i, 0))] * 3,
        out_shape=[SDS((s, N_HEADS * LANES), BF16)] * 3,
        scratch_shapes=[pltpu.VMEM((8, LANES), F32)],
        name="fox_prep", compiler_params=_cp(("arbitrary",)))(z, z, z, fa, bfo, _tri(tb, False))


def _causal_pairs(n, k_major):
    if k_major:
        pairs = [(qi, kj) for kj in range(n) for qi in range(kj, n)]
    else:
        pairs = [(qi, kj) for qi in range(n) for kj in range(qi + 1)]
    return (jnp.asarray([p[0] for p in pairs], jnp.int32), jnp.asarray([p[1] for p in pairs], jnp.int32), len(pairs))


def fox_fwd(q_aug, k_aug, v_aug, gather=(), halved=(), *, t=512, hps=4):
    s = v_aug.shape[0]
    qi_arr, kj_arr, n_pairs = _causal_pairs(s // t, False)
    ng = len(gather)
    n_groups = N_HEADS // hps

    def body(qi_ref, kj_ref, q_ref, k_ref, v_ref, *rest):
        g_ins, (o_ref, lse_ref), g_outs = rest[:ng], rest[ng:ng + 2], rest[ng + 2:2 * ng + 2]
        m_scr, acc_scr = rest[2 * ng + 2:2 * ng + 4]
        comm = (g_ins, g_outs) + tuple(rest[2 * ng + 4:]) + (list(halved),)
        step = pl.program_id(1)
        qi = qi_ref[step]
        kj = kj_ref[step]
        if ng:
            @pl.when((pl.program_id(0) == 0) & (step == 0))
            def _():
                _allgather_start(*comm)

        @pl.when(kj == 0)
        def _():
            m_scr[...] = jnp.full_like(m_scr, NEG)
            acc_scr[...] = jnp.zeros_like(acc_scr)

        def update(masked):
            for i in range(hps):
                sc = _nt(q_ref[:, i * LANES:(i + 1) * LANES], k_ref[:, i * LANES:(i + 1) * LANES])
                if masked:
                    sc = jnp.where(_row((t, t)) >= _lane((t, t)), sc, NEG)
                m_prev = m_scr[i]
                m_new = jnp.maximum(m_prev, jnp.max(sc, axis=-1, keepdims=True))
                p = jnp.exp((sc - jnp.tile(m_new, (1, t // LANES))).astype(BF16))
                acc_scr[i] = jnp.exp(m_prev - m_new) * acc_scr[i] + _nn(p, v_ref[:, i * LANES:(i + 1) * LANES])
                m_scr[i] = m_new

        @pl.when(kj < qi)
        def _():
            update(False)

        @pl.when(kj == qi)
        def _():
            update(True)
            lane = _lane((t, LANES))
            for pr in range(hps // 2):
                den = [acc_scr[2 * pr + i][:, 64:65] for i in range(2)]
                o_ref[:, pr * LANES:(pr + 1) * LANES] = jnp.where(
                    lane < 64, acc_scr[2 * pr] / den[0], pltpu.roll(acc_scr[2 * pr + 1] / den[1], 64, axis=1)).astype(o_ref.dtype)
                lse_ref[:, pr * LANES:(pr + 1) * LANES] = jnp.where(
                    lane < 64, m_scr[2 * pr] + jnp.log(den[0]), m_scr[2 * pr + 1] + jnp.log(den[1]))

        if ng:
            @pl.when((pl.program_id(0) == n_groups - 1) & (step == n_pairs - 1))
            def _():
                _allgather_finish(*comm)

    wide = hps * LANES
    grid_spec = pltpu.PrefetchScalarGridSpec(
        num_scalar_prefetch=2, grid=(n_groups, n_pairs),
        in_specs=[pl.BlockSpec((t, wide), lambda hg, st, qi, kj: (qi[st], hg)),
                  pl.BlockSpec((t, wide), lambda hg, st, qi, kj: (kj[st], hg)),
                  pl.BlockSpec((t, wide), lambda hg, st, qi, kj: (kj[st], hg))] + [ANY] * ng,
        out_specs=[pl.BlockSpec((t, wide // 2), lambda hg, st, qi, kj: (qi[st], hg))] * 2 + [ANY] * ng,
        scratch_shapes=[pltpu.VMEM((hps, t, LANES), F32)] * 2 + (_allgather_sems(ng) if ng else []))
    return pl.pallas_call(
        body, grid_spec=grid_spec, out_shape=[SDS((s, ATT_W), BF16), SDS((s, ATT_W), F32)] + _allgather_shapes(gather),
        name="fox_fwd", compiler_params=_cp(("arbitrary", "arbitrary")))(qi_arr, kj_arr, q_aug, k_aug, v_aug, *gather)


def fox_bwd(q_aug, k_aug, z, dy, lse, dd, exchange=(), kind="to_chips", *, t=512, hps=4):
    s = z.shape[0]
    qi_arr, kj_arr, n_pairs = _causal_pairs(s // t, True)
    ne = len(exchange)
    n_groups = N_HEADS // hps
    x_shapes, x_sems, x_start, x_finish = EXCHANGES[kind]

    def body(qi_ref, kj_ref, q_ref, k_ref, v_ref, do_ref, lse_ref, dd_ref, *rest):
        e_ins, (dq_ref, dk_ref, dv_ref), e_outs = rest[:ne], rest[ne:ne + 3], rest[ne + 3:2 * ne + 3]
        comm = (e_ins, e_outs) + tuple(rest[2 * ne + 3:])
        step = pl.program_id(1)
        qi = qi_ref[step]
        kj = kj_ref[step]
        if ne:
            @pl.when((pl.program_id(0) == 0) & (step == 0))
            def _():
                x_start(*comm)

        @pl.when(step == 0)
        def _():
            dq_ref[...] = jnp.zeros_like(dq_ref)

        @pl.when(qi == kj)
        def _():
            dk_ref[...] = jnp.zeros_like(dk_ref)
            dv_ref[...] = jnp.zeros_like(dv_ref)

        def update(masked):
            lane = _lane((t, LANES))
            rows = pl.ds(pl.multiple_of(qi * t, t), t)
            for pr in range(hps // 2):
                pair = slice(pr * LANES, (pr + 1) * LANES)
                dov = do_ref[:, pair]
                dv_new = None
                for i in range(2):
                    head = (lane < 64) if i == 0 else (lane >= 64)
                    own = slice((2 * pr + i) * LANES, (2 * pr + i + 1) * LANES)
                    col = slice(pr * LANES + i * 64, pr * LANES + i * 64 + 1)
                    qv = q_ref[:, own]
                    kv = k_ref[:, own]
                    sc = _nt(qv, kv)
                    if masked:
                        sc = jnp.where(_row((t, t)) >= _lane((t, t)), sc, NEG)
                    p = jnp.exp(sc - lse_ref[:, col])
                    dp = _nt(jnp.where(head, dov, jnp.zeros_like(dov)), v_ref[:, pair])
                    ds = (p * (dp - dd_ref[:, col])).astype(BF16)
                    dq_ref[rows, own] += _nn(ds, kv)
                    dk_ref[:, own] += _tn(ds, qv)
                    dvi = _tn(p.astype(BF16), dov)
                    dv_new = dvi if dv_new is None else jnp.where(head, dvi, dv_new)
                dv_ref[:, pair] += dv_new

        @pl.when(kj < qi)
        def _():
            update(False)

        @pl.when(kj == qi)
        def _():
            update(True)

        if ne:
            @pl.when((pl.program_id(0) == n_groups - 1) & (step == n_pairs - 1))
            def _():
                x_finish(*comm)

    wide, half = hps * LANES, hps // 2 * LANES
    v_blk = Z_VA * ATT_W // half
    grid_spec = pltpu.PrefetchScalarGridSpec(
        num_scalar_prefetch=2, grid=(n_groups, n_pairs),
        in_specs=[pl.BlockSpec((t, wide), lambda hg, st, qi, kj: (qi[st], hg)),
                  pl.BlockSpec((t, wide), lambda hg, st, qi, kj: (kj[st], hg)),
                  pl.BlockSpec((t, half), lambda hg, st, qi, kj: (kj[st], v_blk + hg)),
                  pl.BlockSpec((t, half), lambda hg, st, qi, kj: (qi[st], hg)),
                  pl.BlockSpec((t, half), lambda hg, st, qi, kj: (qi[st], hg)),
                  pl.BlockSpec((t, half), lambda hg, st, qi, kj: (qi[st], hg))] + [ANY] * ne,
        out_specs=[pl.BlockSpec((s, wide), lambda hg, st, qi, kj: (0, hg)),
                   pl.BlockSpec((t, wide), lambda hg, st, qi, kj: (kj[st], hg)),
                   pl.BlockSpec((t, half), lambda hg, st, qi, kj: (kj[st], hg))] + [ANY] * ne,
        scratch_shapes=x_sems(ne) if ne else [])
    return pl.pallas_call(
        body, grid_spec=grid_spec,
        out_shape=[SDS((s, N_HEADS * LANES), F32), SDS((s, N_HEADS * LANES), F32), SDS((s, ATT_W), F32)]
        + x_shapes(exchange),
        name="fox_bwd", compiler_params=_cp(("arbitrary", "arbitrary")))(qi_arr, kj_arr, q_aug, k_aug, z, dy, lse, dd, *exchange)


def fox_post(dq_aug, dk_aug, dv, fa, bfo, *, tb=512):
    s = dv.shape[0]
    n = s // tb

    def body(dq_ref, dk_ref, dv_ref, fa_ref, b_ref, tri_ref, dz_ref, dfa_ref, gb_ref, carry, acc):
        i = pl.program_id(0)

        @pl.when(i == 0)
        def _():
            carry[...] = jnp.zeros_like(carry)
            acc[...] = jnp.zeros_like(acc)

        lane = _lane((tb, LANES))
        d_f = jnp.zeros((tb, LANES), F32)
        for h in range(N_HEADS):
            col = dq_ref[:, h * LANES + 64:h * LANES + 65] - dk_ref[:, h * LANES + 67:h * LANES + 68]
            d_f = jnp.where(lane == h, col, d_f)
        suffix = jnp.dot(tri_ref[...], d_f, preferred_element_type=F32, precision=lax.Precision.HIGHEST) + carry[0:1, :]
        carry[0:1, :] = suffix[0:1, :]
        xv = fa_ref[...] + b_ref[...]
        dx = suffix * (1.0 / (1.0 + jnp.exp(xv)))
        dfa_ref[...] = dx.astype(dfa_ref.dtype)
        acc[...] += jnp.sum(dx.reshape(tb // 8, 8, LANES), axis=0)
        for hp in range(4):
            for src, off, scale in ((dq_ref, 0, QK_SCALE), (dk_ref, ATT_W, 1.0)):
                even = src[:, (2 * hp) * LANES:(2 * hp + 1) * LANES]
                odd = pltpu.roll(src[:, (2 * hp + 1) * LANES:(2 * hp + 2) * LANES], 64, axis=1)
                dz_ref[:, off + hp * LANES:off + (hp + 1) * LANES] = (jnp.where(lane < 64, even, odd) * scale).astype(BF16)
        dz_ref[:, 2 * ATT_W:3 * ATT_W] = dv_ref[...].astype(BF16)

        @pl.when(i == n - 1)
        def _():
            gb_ref[...] = jnp.sum(acc[...], axis=0, keepdims=True)

    rev = lambda i: (n - 1 - i, 0)
    return pl.pallas_call(
        body, grid=(n,),
        in_specs=[pl.BlockSpec((tb, N_HEADS * LANES), rev), pl.BlockSpec((tb, N_HEADS * LANES), rev),
                  pl.BlockSpec((tb, ATT_W), rev), pl.BlockSpec((tb, LANES), rev),
                  pl.BlockSpec((1, LANES), lambda i: (0, 0)), pl.BlockSpec((tb, tb), lambda i: (0, 0))],
        out_specs=[pl.BlockSpec((tb, 3 * ATT_W), rev), pl.BlockSpec((tb, LANES), rev),
                   pl.BlockSpec((1, LANES), lambda i: (0, 0))],
        out_shape=[SDS((s, 3 * ATT_W), BF16), SDS((s, LANES), BF16), SDS((1, LANES), F32)],
        scratch_shapes=[pltpu.VMEM((8, LANES), F32), pltpu.VMEM((8, LANES), F32)],
        name="fox_post", compiler_params=_cp(("arbitrary",)))(dq_aug, dk_aug, dv, fa, bfo, _tri(tb, True))


def rope_cos_sin(s):
    half = ROPE_DIM // 2
    inv_freq = ROPE_THETA ** (-jnp.arange(half, dtype=F32) * 2.0 / ROPE_DIM)
    ang = jnp.arange(s, dtype=F32)[:, None] * inv_freq[None, :]
    return jnp.tile(jnp.cos(ang), (1, LANES // half)), jnp.tile(jnp.sin(ang), (1, LANES // half))


def _rotate(x, cos, sin, sign):
    l64 = _lane(x.shape) & (HEAD_DIM - 1)
    first = l64 < ROPE_DIM // 2
    second = (l64 >= ROPE_DIM // 2) & (l64 < ROPE_DIM)
    from_next = jnp.where(first, -sign * sin, 0.0)
    from_prev = jnp.where(second, sign * sin, 0.0)
    return (x * jnp.where(first | second, cos, 1.0) + pltpu.roll(x, LANES - 8, axis=1) * from_next
            + pltpu.roll(x, 8, axis=1) * from_prev)


def _dil_rows(base, r):
    if r == 1:
        return pl.ds(pl.multiple_of(base, DIL_BLK), DIL_BLK)
    return pl.ds(base, DIL_BLK, stride=r)


def _dil_block(idx, r, nb):
    shift = nb.bit_length() - 1
    rho = idx >> shift
    n = idx & (nb - 1)
    base = rho + n * (r * DIL_BLK)
    return _dil_rows(base, r), _dil_rows(jnp.maximum(base - r * DIL_BLK, rho), r), n > 0


def _cat(a, b):
    return jnp.concatenate([a, b], axis=0)


def _two_heads(v, first_head):
    zero = jnp.zeros_like(v)
    return _cat(jnp.where(first_head, v, zero), jnp.where(first_head, zero, v))


def _dil_bands():
    b = DIL_BLK
    q = _row((2 * b, 2 * b)) & (b - 1)
    col = _lane((2 * b, 2 * b))
    return (col < b) & (col >= q), (col >= b) & (col - b <= q)


def _dil_load_qkv(zq_ref, zk_ref, zv_ref, cos_ref, sin_ref, q_ref, k_ref, v_ref, *, chunk=512):
    def step(i, carry):
        rows = pl.ds(pl.multiple_of(i * chunk, chunk), chunk)
        cos, sin = cos_ref[rows, :], sin_ref[rows, :]
        q_ref[rows, :] = _rotate(zq_ref[rows, :].astype(F32), cos, sin, 1.0) * QK_SCALE
        k_ref[rows, :] = _rotate(zk_ref[rows, :].astype(F32), cos, sin, 1.0)
        v_ref[rows, :] = zv_ref[rows, :].astype(F32)
        return carry

    lax.fori_loop(0, q_ref.shape[0] // chunk, step, 0)


def dil_fwd_all(z, cos_t, sin_t, *, unroll=32):
    s = z.shape[0]
    b = DIL_BLK
    n_blk = s // b

    def body(zq_ref, zk_ref, zv_ref, cos_ref, sin_ref, o_ref, l_ref, q_ref, k_ref, v_ref):
        _dil_load_qkv(zq_ref, zk_ref, zv_ref, cos_ref, sin_ref, q_ref, k_ref, v_ref)
        first_head = _lane((b, LANES)) < 64
        band_prev, band_cur = _dil_bands()
        for g, (_, r) in enumerate(DIL_PATTERNS):
            nb = n_blk // r

            def group(it, carry, g=g, r=r, nb=nb):
                loaded = []
                kc = vc = None
                for u in range(unroll):
                    rows_c, rows_p, has_prev = _dil_block(it * unroll + u, r, nb)
                    if u % min(nb, unroll):
                        kp, vp = kc, vc
                    else:
                        kp, vp = k_ref[rows_p, :].astype(BF16), v_ref[rows_p, :].astype(BF16)
                    kc, vc = k_ref[rows_c, :].astype(BF16), v_ref[rows_c, :].astype(BF16)
                    state = (o_ref[rows_c, :], l_ref[rows_c, :]) if g else None
                    loaded.append((rows_c, has_prev, [q_ref[rows_c, :].astype(BF16), kp, kc, vp, vc], state))
                done = []
                for rows_c, has_prev, (qv, kp, kc, vp, vc), state in loaded:
                    sc = jnp.where(band_cur | (band_prev & has_prev), _nt(_two_heads(qv, first_head), _cat(kp, kc)), NEG)
                    m = jnp.max(sc, axis=-1, keepdims=True)
                    p = jnp.exp(sc - m)
                    den = jnp.sum(p, axis=-1, keepdims=True)
                    both = _nn(p.astype(BF16), _cat(vp, vc)) / den
                    lse2 = m + jnp.log(den)
                    ov = jnp.where(first_head, both[:b], both[b:])
                    lse = jnp.where(first_head, lse2[:b], lse2[b:])
                    if state is not None:
                        m2 = jnp.maximum(state[1], lse)
                        wp = jnp.exp(state[1] - m2)
                        wn = jnp.exp(lse - m2)
                        ov = (wp * state[0] + wn * ov) / (wp + wn)
                        lse = m2 + jnp.log(wp + wn)
                    done.append((rows_c, ov, lse))
                for rows_c, ov, lse in done:
                    o_ref[rows_c, :] = ov
                    l_ref[rows_c, :] = lse
                return carry

            lax.fori_loop(0, n_blk // unroll, group, 0)

    col_blk = lambda k: pl.BlockSpec((s, LANES), lambda hp: (0, 4 * k + hp))
    table = pl.BlockSpec((s, LANES), lambda hp: (0, 0))
    out = pl.BlockSpec((s, LANES), lambda hp: (0, hp))
    return pl.pallas_call(
        body, grid=(4,), in_specs=[col_blk(Z_QB), col_blk(Z_KB), col_blk(Z_VB), table, table], out_specs=[out, out],
        out_shape=[SDS((s, ATT_W), F32)] * 2, scratch_shapes=[pltpu.VMEM((s, LANES), F32)] * 3, name="dil_fwd",
        compiler_params=_cp(("parallel",)))(z, z, z, cos_t, sin_t)


def dil_bwd_all(z, cos_t, sin_t, dy, lse, y, exchange=(), kind="to_chips", *, unroll=32):
    s = z.shape[0]
    b = DIL_BLK
    n_blk = s // b
    ne = len(exchange)
    x_shapes, x_sems, x_start, x_finish = EXCHANGES[kind]

    def body(zq_ref, zk_ref, zv_ref, cos_ref, sin_ref, do_ref, l_ref, y_ref, *rest):
        e_ins, (gq_ref, gk_ref, gv_ref), e_outs = rest[:ne], rest[ne:ne + 3], rest[ne + 3:2 * ne + 3]
        q_ref, k_ref, v_ref, dq_ref, dk_ref, dv_ref = rest[2 * ne + 3:2 * ne + 9]
        comm = (e_ins, e_outs) + tuple(rest[2 * ne + 9:])
        if ne:
            @pl.when(pl.program_id(0) == 0)
            def _():
                x_start(*comm)

        _dil_load_qkv(zq_ref, zk_ref, zv_ref, cos_ref, sin_ref, q_ref, k_ref, v_ref)
        dq_ref[...] = jnp.zeros_like(dq_ref)
        dk_ref[...] = jnp.zeros_like(dk_ref)
        dv_ref[...] = jnp.zeros_like(dv_ref)
        first_head = _lane((b, LANES)) < 64
        band_prev, band_cur = _dil_bands()
        for _, r in DIL_PATTERNS:
            nb = n_blk // r

            def group(it, carry, r=r, nb=nb):
                loaded = []
                kc = vc = None
                for u in range(unroll):
                    rows_c, rows_p, has_prev = _dil_block(it * unroll + u, r, nb)
                    if u % min(nb, unroll):
                        kp, vp = kc, vc
                    else:
                        kp, vp = k_ref[rows_p, :].astype(BF16), v_ref[rows_p, :].astype(BF16)
                    kc, vc = k_ref[rows_c, :].astype(BF16), v_ref[rows_c, :].astype(BF16)
                    vals = [q_ref[rows_c, :].astype(BF16), kp, kc, vp, vc, do_ref[rows_c, :], l_ref[rows_c, :], y_ref[rows_c, :]]
                    loaded.append((rows_c, rows_p, has_prev, vals))
                done = []
                for rows_c, rows_p, has_prev, (qv, kp, kc, vp, vc, dof, lv, yv) in loaded:
                    q2 = _two_heads(qv, first_head)
                    do2 = _two_heads(dof.astype(BF16), first_head)
                    kcat, vcat = _cat(kp, kc), _cat(vp, vc)
                    lse2 = _cat(lv[:, 0:1], lv[:, 64:65])
                    dd2 = jnp.sum(_two_heads(dof * yv, first_head), axis=-1, keepdims=True)
                    p = jnp.exp(jnp.where(band_cur | (band_prev & has_prev), _nt(q2, kcat), NEG) - lse2)
                    ds = (p * (_nt(do2, vcat) - dd2)).astype(BF16)
                    dq2 = _nn(ds, kcat)
                    dkcat = _tn(ds, q2)
                    dvcat = _tn(p.astype(BF16), do2)
                    done.append((rows_c, rows_p, (jnp.where(first_head, dq2[:b], dq2[b:]), dkcat[:b], dkcat[b:],
                                                  dvcat[:b], dvcat[b:])))
                for rows_c, rows_p, (dq, dk_p, dk_c, dv_p, dv_c) in done:
                    dq_ref[rows_c, :] += dq
                    dk_ref[rows_p, :] += dk_p
                    dk_ref[rows_c, :] += dk_c
                    dv_ref[rows_p, :] += dv_p
                    dv_ref[rows_c, :] += dv_c
                return carry

            lax.fori_loop(0, n_blk // unroll, group, 0)

        def finish(i, carry, chunk=512):
            rows = pl.ds(pl.multiple_of(i * chunk, chunk), chunk)
            cos, sin = cos_ref[rows, :], sin_ref[rows, :]
            gq_ref[rows, :] = (_rotate(dq_ref[rows, :], cos, sin, -1.0) * QK_SCALE).astype(BF16)
            gk_ref[rows, :] = _rotate(dk_ref[rows, :], cos, sin, -1.0).astype(BF16)
            gv_ref[rows, :] = dv_ref[rows, :].astype(BF16)
            return carry

        lax.fori_loop(0, s // 512, finish, 0)
        if ne:
            @pl.when(pl.program_id(0) == 3)
            def _():
                x_finish(*comm)

    col_blk = lambda k: pl.BlockSpec((s, LANES), lambda hp: (0, 4 * k + hp))
    table = pl.BlockSpec((s, LANES), lambda hp: (0, 0))
    nat = pl.BlockSpec((s, LANES), lambda hp: (0, hp))
    return pl.pallas_call(
        body, grid=(4,), in_specs=[col_blk(Z_QB), col_blk(Z_KB), col_blk(Z_VB), table, table, nat, nat, nat] + [ANY] * ne,
        out_specs=[nat, nat, nat] + [ANY] * ne, out_shape=[SDS((s, ATT_W), BF16)] * 3 + x_shapes(exchange),
        scratch_shapes=[pltpu.VMEM((s, LANES), F32)] * 6 + (x_sems(ne) if ne else []), name="dil_bwd",
        compiler_params=_cp(("arbitrary",)))(z, z, z, cos_t, sin_t, dy, lse, y, *exchange)


def _sigmoid(v):
    return 1.0 / (1.0 + jnp.exp(-v))


def gate_mix(ya, yb, wa, wb, z, *, tm=2048, tn=512):
    s = ya.shape[0]
    d = wa.shape[1]
    ga_blk = 3 * ATT_W * 2 // tn
    gb_blk = ga_blk + d // tn

    def body(ya_ref, yb_ref, wa_ref, wb_ref, ga_ref, gb_ref, pa_ref, pb_ref, mx_ref):
        pa = _nn(ya_ref[...], wa_ref[...])
        pb = _nn(yb_ref[...].astype(BF16), wb_ref[...])
        pa_ref[...] = pa.astype(BF16)
        pb_ref[...] = pb.astype(BF16)
        mx_ref[...] = (_sigmoid(ga_ref[...].astype(F32)) * pa + _sigmoid(gb_ref[...].astype(F32)) * pb).astype(BF16)

    out = pl.BlockSpec((tm, tn), lambda i, j: (i, j))
    return pl.pallas_call(
        body, grid=(s // tm, d // tn),
        in_specs=[pl.BlockSpec((tm, ATT_W), lambda i, j: (i, 0)), pl.BlockSpec((tm, ATT_W), lambda i, j: (i, 0)),
                  pl.BlockSpec((ATT_W, tn), lambda i, j: (0, j)), pl.BlockSpec((ATT_W, tn), lambda i, j: (0, j)),
                  pl.BlockSpec((tm, tn), lambda i, j: (i, ga_blk + j)), pl.BlockSpec((tm, tn), lambda i, j: (i, gb_blk + j))],
        out_specs=[out, out, out], out_shape=[SDS((s, d), BF16)] * 3, name="gate_mix",
        compiler_params=_cp(("parallel", "parallel")))(ya, yb, wa, wb, z, z)


def mix_bwd(dy, w_o, z, pa, pb, wo_a, wo_b, ya, *, tm=512):
    s, d = dy.shape

    def body(dy_ref, wo_ref, ga_ref, gb_ref, pa_ref, pb_ref, wa_ref, wb_ref, ya_ref,
             dpa_ref, dpb_ref, dg_ref, dya_ref, dyb_ref, dd_ref):
        dm = _nt(dy_ref[...], wo_ref[...])
        sa = _sigmoid(ga_ref[...].astype(F32))
        sb = _sigmoid(gb_ref[...].astype(F32))
        dpa = (dm * sa).astype(BF16)
        dpb = (dm * sb).astype(BF16)
        dpa_ref[...] = dpa
        dpb_ref[...] = dpb
        dg_ref[:, 0:d] = (dm * pa_ref[...].astype(F32) * sa * (1.0 - sa)).astype(BF16)
        dg_ref[:, d:2 * d] = (dm * pb_ref[...].astype(F32) * sb * (1.0 - sb)).astype(BF16)
        dya = _nt(dpa, wa_ref[...]).astype(BF16)
        dya_ref[...] = dya
        dyb_ref[...] = _nt(dpb, wb_ref[...])
        lane = _lane((tm, LANES))
        for pr in range(ATT_W // LANES):
            pair = slice(pr * LANES, (pr + 1) * LANES)
            prod = dya[:, pair].astype(F32) * ya_ref[:, pair].astype(F32)
            lo = jnp.sum(jnp.where(lane < 64, prod, 0.0), axis=-1, keepdims=True)
            hi = jnp.sum(jnp.where(lane >= 64, prod, 0.0), axis=-1, keepdims=True)
            dd_ref[:, pair] = jnp.where(lane < 64, lo, hi)

    row = pl.BlockSpec((tm, d), lambda i: (i, 0))
    att = pl.BlockSpec((tm, ATT_W), lambda i: (i, 0))
    whole = lambda a: pl.BlockSpec(a.shape, lambda i: (0, 0))
    return pl.pallas_call(
        body, grid=(s // tm,),
        in_specs=[row, whole(w_o), pl.BlockSpec((tm, d), lambda i: (i, 3)), pl.BlockSpec((tm, d), lambda i: (i, 4)), row, row,
                  whole(wo_a), whole(wo_b), att],
        out_specs=[row, row, pl.BlockSpec((tm, 2 * d), lambda i: (i, 0)), att, att, att],
        out_shape=[SDS((s, d), BF16), SDS((s, d), BF16), SDS((s, 2 * d), BF16), SDS((s, ATT_W), BF16),
                   SDS((s, ATT_W), F32), SDS((s, ATT_W), F32)], name="mix_bwd",
        compiler_params=_cp(("parallel",)))(dy, w_o, z, z, pa, pb, wo_a, wo_b, ya)


GELU_C = math.sqrt(2.0 / math.pi)


def _gelu_parts(a):
    a2 = a * a
    th = jnp.tanh(a * (GELU_C + (GELU_C * 0.044715) * a2))
    half = 0.5 * a
    gelu = half + half * th
    dgelu = (0.5 + 0.5 * th) + half * (1.0 - th * th) * (GELU_C + (3.0 * GELU_C * 0.044715) * a2)
    return gelu, dgelu


def _causal_taps(u, before):
    row = _row(u.shape)
    r1 = jnp.where(row == 0, before[7:8, :], pltpu.roll(u, 1, axis=0))
    r2 = jnp.where(row == 0, before[6:7, :], jnp.where(row == 1, before[7:8, :], pltpu.roll(u, 2, axis=0)))
    return r1, r2


def ffn_up(h, wa, wb, cw, cb, *, tm=2048, tn=256):
    s, d = h.shape
    f = wa.shape[1]
    nj = f // tn

    def body(h_ref, wa_ref, wb_ref, cwa_ref, cwb_ref, cba_ref, cbb_ref, ua_ref, ub_ref, ca_ref, cbo_ref, m_ref, carry):
        @pl.when(pl.program_id(1) == 0)
        def _():
            carry[...] = jnp.zeros_like(carry)

        conv = []
        for k, (w_ref, cw_ref, cb_ref, u_ref, c_ref) in enumerate(((wa_ref, cwa_ref, cba_ref, ua_ref, ca_ref),
                                                                   (wb_ref, cwb_ref, cbb_ref, ub_ref, cbo_ref))):
            u16 = _nn(h_ref[...], w_ref[...]).astype(BF16)
            u_ref[...] = u16
            u = u16.astype(F32)
            r1, r2 = _causal_taps(u, carry[k])
            carry[k] = u[tm - 8:tm, :]
            c16 = (cw_ref[0:1, :] * r2 + cw_ref[1:2, :] * r1 + cw_ref[2:3, :] * u + cb_ref[...]).astype(BF16)
            c_ref[...] = c16
            conv.append(c16.astype(F32))
        m_ref[...] = (_gelu_parts(conv[0])[0] * conv[1]).astype(BF16)

    out = pl.BlockSpec((tm, tn), lambda j, i: (i, j))
    return pl.pallas_call(
        body, grid=(nj, s // tm),
        in_specs=[pl.BlockSpec((tm, d), lambda j, i: (i, 0)),
                  pl.BlockSpec((d, tn), lambda j, i: (0, j)), pl.BlockSpec((d, tn), lambda j, i: (0, j)),
                  pl.BlockSpec((3, tn), lambda j, i: (0, j)), pl.BlockSpec((3, tn), lambda j, i: (0, nj + j)),
                  pl.BlockSpec((1, tn), lambda j, i: (0, j)), pl.BlockSpec((1, tn), lambda j, i: (0, nj + j))],
        out_specs=[out] * 5, out_shape=[SDS((s, f), BF16)] * 5,
        scratch_shapes=[pltpu.VMEM((2, 8, tn), F32)], name="ffn_up",
        compiler_params=_cp(("parallel", "arbitrary")))(h, wa, wb, cw, cw, cb, cb)


def ffn_bwd(dm, ua, ub, ca, cbo, cw, *, tm=2048, tn=256):
    s, f = dm.shape
    nj = f // tn
    ni = s // tm

    def body(dm_ref, ua_ref, ub_ref, ca_ref, cbo_ref, cwa_ref, cwb_ref, dua_ref, dub_ref, ga_ref, gb_ref, carry):
        @pl.when(pl.program_id(1) == 0)
        def _():
            carry[...] = jnp.zeros_like(carry)
            ga_ref[...] = jnp.zeros_like(ga_ref)
            gb_ref[...] = jnp.zeros_like(gb_ref)

        row = _row((tm, tn))
        dmv = dm_ref[...].astype(F32)
        gelu, dgelu = _gelu_parts(ca_ref[...].astype(F32))
        dcs = (dmv * cbo_ref[...].astype(F32) * dgelu, dmv * gelu)
        for k, (dc, u_ref, cw_ref, du_ref, g_ref) in enumerate(((dcs[0], ua_ref, cwa_ref, dua_ref, ga_ref),
                                                                (dcs[1], ub_ref, cwb_ref, dub_ref, gb_ref))):
            u = u_ref[...].astype(F32)
            after = carry[k]
            n1 = jnp.where(row == tm - 1, after[0:1, :], pltpu.roll(dc, tm - 1, axis=0))
            n2 = jnp.where(row == tm - 2, after[0:1, :], jnp.where(row == tm - 1, after[1:2, :], pltpu.roll(dc, tm - 2, axis=0)))
            g_ref[0:1, :] += jnp.sum(n2 * u, axis=0, keepdims=True)
            g_ref[1:2, :] += jnp.sum(n1 * u, axis=0, keepdims=True)
            g_ref[2:3, :] += jnp.sum(dc * u, axis=0, keepdims=True)
            g_ref[3:4, :] += jnp.sum(dc, axis=0, keepdims=True)
            du_ref[...] = (cw_ref[2:3, :] * dc + cw_ref[1:2, :] * n1 + cw_ref[0:1, :] * n2).astype(BF16)
            carry[k] = dc[0:8, :]

    tile = pl.BlockSpec((tm, tn), lambda j, i: (ni - 1 - i, j))
    gspec = pl.BlockSpec((8, tn), lambda j, i: (0, j))
    return pl.pallas_call(
        body, grid=(nj, ni),
        in_specs=[tile] * 5 + [pl.BlockSpec((3, tn), lambda j, i: (0, j)), pl.BlockSpec((3, tn), lambda j, i: (0, nj + j))],
        out_specs=[tile, tile, gspec, gspec],
        out_shape=[SDS((s, f), BF16), SDS((s, f), BF16), SDS((8, f), F32), SDS((8, f), F32)],
        scratch_shapes=[pltpu.VMEM((2, 8, tn), F32)], name="ffn_bwd",
        compiler_params=_cp(("parallel", "arbitrary")))(dm, ua, ub, ca, cbo, cw, cw)


def adamw(w, g, m, v, *, name, tr=None):
    r = w.shape[0]
    rest = w.shape[1:]
    if tr is None:
        tr = r
        for cand in (256, 128, 64, 32, 16, 8):
            if r % cand == 0:
                tr = cand
                break

    def body(w_ref, g_ref, m_ref, v_ref, d_ref, nm_ref, nv_ref):
        gv = g_ref[...]
        mn = ADAM_B1 * m_ref[...] + (1.0 - ADAM_B1) * gv
        vn = ADAM_B2 * v_ref[...] + (1.0 - ADAM_B2) * (gv * gv)
        m_hat = mn / (1.0 - ADAM_B1 ** ADAM_STEP)
        v_hat = vn / (1.0 - ADAM_B2 ** ADAM_STEP)
        d_ref[...] = -ADAM_LR * (m_hat / (jnp.sqrt(v_hat) + ADAM_EPS) + ADAM_WD * w_ref[...])
        nm_ref[...] = mn
        nv_ref[...] = vn

    blk = pl.BlockSpec((tr,) + rest, lambda i: (i,) + (0,) * len(rest))
    return pl.pallas_call(body, grid=(r // tr,), in_specs=[blk] * 4, out_specs=[blk] * 3, out_shape=[SDS(w.shape, F32)] * 3,
                          name=name, compiler_params=_cp(("parallel",)))(w, g, m, v)


def adamw_rows_view(w, g_mine, g_full, m, v, c_arr, *, name, tc=256):
    r, _, c = w.shape
    per_half = c // 2 // tc

    def body(c_ref, w_ref, gm_ref, gf_ref, m_ref, v_ref, d_ref, nm_ref, nv_ref, go_ref):
        mine = (pl.program_id(0) >> (per_half.bit_length() - 1)) == c_ref[0]
        gv = jnp.where(mine, gm_ref[...], gf_ref[...])
        mn = ADAM_B1 * m_ref[:, 0, :] + (1.0 - ADAM_B1) * gv
        vn = ADAM_B2 * v_ref[:, 0, :] + (1.0 - ADAM_B2) * (gv * gv)
        m_hat = mn / (1.0 - ADAM_B1 ** ADAM_STEP)
        v_hat = vn / (1.0 - ADAM_B2 ** ADAM_STEP)
        d_ref[:, 0, :] = -ADAM_LR * (m_hat / (jnp.sqrt(v_hat) + ADAM_EPS) + ADAM_WD * w_ref[:, 0, :])
        nm_ref[:, 0, :] = mn
        nv_ref[:, 0, :] = vn
        go_ref[:, 0, :] = gv

    b3 = pl.BlockSpec((r, 1, tc), lambda i, c_ref: (0, 0, i))
    own = pl.BlockSpec((r, tc), lambda i, c_ref: (0, jnp.clip(i - c_ref[0] * per_half, 0, per_half - 1)))
    full = pl.BlockSpec((r, tc), lambda i, c_ref: (0, i))
    grid_spec = pltpu.PrefetchScalarGridSpec(num_scalar_prefetch=1, grid=(c // tc,), in_specs=[b3, own, full, b3, b3],
                                             out_specs=[b3] * 4)
    return pl.pallas_call(body, grid_spec=grid_spec, out_shape=[SDS(w.shape, F32)] * 4, name=name,
                          compiler_params=_cp(("parallel",)))(c_arr, w, g_mine, g_full, m, v)


ANY = pl.BlockSpec(memory_space=pl.ANY)
ICI_KINDS = ("x", "y", "xy")


def _coords():
    return lax.axis_index("x"), lax.axis_index("y"), lax.axis_index("c")


def _peer(kind, x, y, c):
    if kind == "c":
        return (x, y, 1 - c)
    if kind == "x":
        return (1 - x, y, c)
    if kind == "y":
        return (x, 1 - y, c)
    return (1 - x, 1 - y, c)


def _chip_of(p):
    return 2 * p[0] + p[1]


def _half(rows, which):
    h = rows // 2
    return pl.ds(pl.multiple_of(which * h, 16), h)


def _remote(src, dst, send_sem, recv_sem, to):
    return pltpu.make_async_remote_copy(src_ref=src, dst_ref=dst, send_sem=send_sem, recv_sem=recv_sem,
                                        device_id=to, device_id_type=MESH)


def allgather_balanced(shard, *, name):
    r, cols = shard.shape
    h, q = r // 2, r // 4

    def body(in_ref, out_ref, send_sems, recv_sems):
        x, y, c = _coords()
        me, sibling = (x, y, c), (x, y, 1 - c)
        nbr = ((1 - x, y, c), (x, 1 - y, c))
        chip = (2 * (1 - x) + y, 2 * x + (1 - y), 2 * (1 - x) + (1 - y))
        quarter = lambda core, i: pl.ds(pl.multiple_of(core * h + i * q, 16), q)
        sent = []

        def go(src, dst, slot, to):
            cp = _remote(src, dst, send_sems.at[slot], recv_sems.at[slot], to)
            cp.start()
            sent.append(cp)

        def landed(region, slot):
            _remote(region, region, send_sems.at[slot], recv_sems.at[slot], me).wait_recv()

        for i in range(2):
            for k in range(2):
                qi = k if i == 0 else 1 - k
                go(in_ref.at[quarter(c, qi)], out_ref.at[2 * x + y, quarter(c, qi)], 2 * k + qi, nbr[k])
        for k in range(2):
            piece = out_ref.at[chip[k], quarter(c, k)]
            landed(piece, 2 * k + k)
            go(piece, piece, 4 + k, nbr[1 - k])
            go(piece, piece, 6 + 2 * k + k, sibling)
        for k in range(2):
            piece = out_ref.at[chip[k], quarter(c, 1 - k)]
            landed(piece, 2 * k + 1 - k)
            go(piece, piece, 6 + 2 * k + 1 - k, sibling)
        for k in range(2):
            piece = out_ref.at[chip[2], quarter(c, k)]
            landed(piece, 4 + k)
            go(piece, piece, 10 + k, sibling)
        for k in range(2):
            for i in range(2):
                landed(out_ref.at[chip[k], quarter(1 - c, i)], 6 + 2 * k + i)
            landed(out_ref.at[chip[2], quarter(1 - c, k)], 10 + k)
        for cp in sent:
            cp.wait_send()

    return pl.pallas_call(
        body, in_specs=[ANY], out_specs=ANY, out_shape=SDS((4,) + shard.shape, shard.dtype),
        scratch_shapes=[pltpu.SemaphoreType.DMA((12,)), pltpu.SemaphoreType.DMA((12,))], name=name)(shard)


def _allgather_shapes(shards):
    return [SDS((4,) + a.shape, a.dtype) for a in shards]


def _allgather_sems(n):
    return [pltpu.SemaphoreType.DMA((n, 6)), pltpu.SemaphoreType.DMA((n, 6))]


def _allgather_rows(ref, is_halved, which):
    r = ref.shape[0]
    return _half(r, which) if is_halved else pl.ds(0, r)


def _allgather_first(ins, outs, send_sems, recv_sems, halved):
    x, y, c = _coords()
    my_chip = 2 * x + y
    cps = []
    for w in range(len(ins)):
        rows = _allgather_rows(ins[w], halved[w], c)
        for k, kind in enumerate(ICI_KINDS):
            cps.append(_remote(ins[w].at[rows], outs[w].at[my_chip, rows], send_sems.at[w, k], recv_sems.at[w, k],
                               _peer(kind, x, y, c)))
    return cps


def _allgather_start(ins, outs, send_sems, recv_sems, halved):
    for cp in _allgather_first(ins, outs, send_sems, recv_sems, halved):
        cp.start()


def _allgather_finish(ins, outs, send_sems, recv_sems, halved):
    x, y, c = _coords()
    me = (x, y, c)
    second = []
    for w in range(len(ins)):
        for k, kind in enumerate(ICI_KINDS):
            landed = outs[w].at[_chip_of(_peer(kind, x, y, c)), _allgather_rows(ins[w], halved[w], c)]
            _remote(landed, landed, send_sems.at[w, k], recv_sems.at[w, k], me).wait_recv()
            if halved[w]:
                cp = _remote(landed, landed, send_sems.at[w, 3 + k], recv_sems.at[w, 3 + k], _peer("c", x, y, c))
                cp.start()
                second.append(cp)
    for w in range(len(ins)):
        if halved[w]:
            for k, kind in enumerate(ICI_KINDS):
                other = outs[w].at[_chip_of(_peer(kind, x, y, c)), _allgather_rows(ins[w], True, 1 - c)]
                _remote(other, other, send_sems.at[w, 3 + k], recv_sems.at[w, 3 + k], me).wait_recv()
    for cp in _allgather_first(ins, outs, send_sems, recv_sems, halved) + second:
        cp.wait_send()


def _half_of(ref, by_cols, which):
    lead = (slice(None),) * (len(ref.shape) - 2)
    if by_cols:
        h = ref.shape[-1] // 2
        return ref.at[lead + (slice(None), pl.ds(pl.multiple_of(which * h, LANES), h))]
    return ref.at[lead + (_half(ref.shape[-2], which),)]


def _half_shape(shape, by_cols):
    return shape[:-1] + (shape[-1] // 2,) if by_cols else shape[:-2] + (shape[-2] // 2, shape[-1])


def grads_to_sibling(gs, by_cols, *, name):
    n = len(gs)

    def body(*refs):
        ins, outs = refs[:n], refs[n:2 * n]
        send_sems, recv_sems = refs[2 * n:]
        x, y, c = _coords()
        cps = []
        for w in range(n):
            cp = _remote(_half_of(ins[w], by_cols[w], 1 - c), outs[w], send_sems.at[w], recv_sems.at[w], _peer("c", x, y, c))
            cp.start()
            cps.append(cp)
        for cp in cps:
            cp.wait()

    return pl.pallas_call(
        body, in_specs=[ANY] * n, out_specs=[ANY] * n,
        out_shape=[SDS(_half_shape(a.shape, bc), a.dtype) for a, bc in zip(gs, by_cols)],
        scratch_shapes=[pltpu.SemaphoreType.DMA((n,)), pltpu.SemaphoreType.DMA((n,))], name=name)(*gs)


def _to_chips_shapes(ps):
    return [SDS((3,) + a.shape[1:], a.dtype) for a in ps]


def _to_chips_sems(n):
    return [pltpu.SemaphoreType.DMA((n, 3)), pltpu.SemaphoreType.DMA((n, 3))]


def _to_chips_copies(ins, outs, send_sems, recv_sems):
    x, y, c = _coords()
    cps = []
    for w in range(len(ins)):
        for k, kind in enumerate(ICI_KINDS):
            to = _peer(kind, x, y, c)
            cps.append(_remote(ins[w].at[_chip_of(to)], outs[w].at[k], send_sems.at[w, k], recv_sems.at[w, k], to))
    return cps


def _to_chips_start(ins, outs, send_sems, recv_sems):
    for cp in _to_chips_copies(ins, outs, send_sems, recv_sems):
        cp.start()


def _to_chips_finish(ins, outs, send_sems, recv_sems):
    for cp in _to_chips_copies(ins, outs, send_sems, recv_sems):
        cp.wait()


def _to_owners_shapes(ps):
    return [SDS((7, a.shape[1] // 2, a.shape[2]), a.dtype) for a in ps]


def _to_owners_sems(n):
    return [pltpu.SemaphoreType.DMA((n, 7)), pltpu.SemaphoreType.DMA((n, 7))]


def _to_owners_copies(ins, outs, send_sems, recv_sems):
    x, y, c = _coords()
    cps = []
    for w in range(len(ins)):
        rows = ins[w].shape[1]
        for k, kind in enumerate(ICI_KINDS):
            px, py, _ = _peer(kind, x, y, c)
            for h in range(2):
                cps.append(_remote(ins[w].at[2 * px + py, _half(rows, h)], outs[w].at[2 * k + c],
                                   send_sems.at[w, 2 * k + h], recv_sems.at[w, 2 * k + c], (px, py, h)))
        cps.append(_remote(ins[w].at[2 * x + y, _half(rows, 1 - c)], outs[w].at[6], send_sems.at[w, 6], recv_sems.at[w, 6],
                           _peer("c", x, y, c)))
    return cps


def _to_owners_start(ins, outs, send_sems, recv_sems):
    for cp in _to_owners_copies(ins, outs, send_sems, recv_sems):
        cp.start()


def _to_owners_finish(ins, outs, send_sems, recv_sems):
    for cp in _to_owners_copies(ins, outs, send_sems, recv_sems):
        cp.wait_send()
    for w in range(len(ins)):
        for slot in range(7):
            got = outs[w].at[slot]
            _remote(got, got, send_sems.at[w, slot], recv_sems.at[w, slot], _coords()).wait_recv()


EXCHANGES = {"to_chips": (_to_chips_shapes, _to_chips_sems, _to_chips_start, _to_chips_finish),
             "to_owners": (_to_owners_shapes, _to_owners_sems, _to_owners_start, _to_owners_finish)}


def halves_to_full(hs, by_cols, *, name):
    n = len(hs)

    def body(*refs):
        ins, outs = refs[:n], refs[n:2 * n]
        send_sems, recv_sems = refs[2 * n:]
        x, y, c = _coords()
        cps = []
        for w in range(n):
            cp = _remote(ins[w], _half_of(outs[w], by_cols[w], c), send_sems.at[w], recv_sems.at[w], _peer("c", x, y, c))
            cp.start()
            cps.append(cp)
        for cp in cps:
            cp.wait()

    return pl.pallas_call(
        body, in_specs=[ANY] * n, out_specs=[ANY] * n,
        out_shape=[SDS((a.shape[0], 2 * a.shape[1]) if bc else (2 * a.shape[0], a.shape[1]), a.dtype)
                   for a, bc in zip(hs, by_cols)],
        scratch_shapes=[pltpu.SemaphoreType.DMA((n,)), pltpu.SemaphoreType.DMA((n,))],
        name=name)(*hs)


def _row_tile(rows):
    for cand in (256, 192, 176, 128, 64, 32, 16):
        if rows % cand == 0:
            return cand
    return rows


def chip_sum(g, recv, c_arr, by_cols, *, name):
    _, r, cols = g.shape

    def body(c_ref, g_ref, r_ref, f_ref, b_ref):
        tot = g_ref[...] + r_ref[...]
        f_ref[...] = tot
        b_ref[...] = tot.astype(BF16)

    if by_cols:
        tc = 4 * LANES
        nblk = cols // 2 // tc
        shape = (4, r, cols // 2)
        blk = pl.BlockSpec((None, r, tc), lambda j, i, c_ref: (j, 0, i))
        mine = pl.BlockSpec((None, r, tc), lambda j, i, c_ref: (j, 0, c_ref[0] * nblk + i))
    else:
        tr = _row_tile(r // 2)
        nblk = r // 2 // tr
        shape = (4, r // 2, cols)
        blk = pl.BlockSpec((None, tr, cols), lambda j, i, c_ref: (j, i, 0))
        mine = pl.BlockSpec((None, tr, cols), lambda j, i, c_ref: (j, c_ref[0] * nblk + i, 0))
    grid_spec = pltpu.PrefetchScalarGridSpec(num_scalar_prefetch=1, grid=(4, nblk), in_specs=[mine, blk], out_specs=[blk, blk])
    return pl.pallas_call(body, grid_spec=grid_spec, out_shape=[SDS(shape, F32), SDS(shape, BF16)],
                          name=name, compiler_params=_cp(("parallel", "parallel")))(c_arr, g, recv)


def final_sum(pf, recv, chip_arr, *, name):
    _, h, cols = pf.shape
    tr = _row_tile(h)

    def body(chip_ref, p_ref, r_ref, o_ref):
        o_ref[...] = ((p_ref[...] + r_ref[0].astype(F32)) + r_ref[1].astype(F32)) + r_ref[2].astype(F32)

    grid_spec = pltpu.PrefetchScalarGridSpec(
        num_scalar_prefetch=1, grid=(h // tr,),
        in_specs=[pl.BlockSpec((None, tr, cols), lambda i, chip_ref: (chip_ref[0], i, 0)),
                  pl.BlockSpec((3, tr, cols), lambda i, chip_ref: (0, i, 0))],
        out_specs=pl.BlockSpec((tr, cols), lambda i, chip_ref: (i, 0)))
    return pl.pallas_call(body, grid_spec=grid_spec, out_shape=SDS((h, cols), F32), name=name,
                          compiler_params=_cp(("parallel",)))(chip_arr, pf, recv)


def owner_sum(g, recv, pos_arr, *, name):
    _, r, cols = g.shape
    h = r // 2
    tr = _row_tile(h)
    nblk = h // tr

    def body(pos_ref, g_ref, r_ref, o_ref):
        tot = g_ref[...]
        for slot in range(7):
            tot = tot + r_ref[slot].astype(F32)
        o_ref[...] = tot

    grid_spec = pltpu.PrefetchScalarGridSpec(
        num_scalar_prefetch=1, grid=(nblk,),
        in_specs=[pl.BlockSpec((None, tr, cols), lambda i, pos: (pos[0], pos[1] * nblk + i, 0)),
                  pl.BlockSpec((7, tr, cols), lambda i, pos: (0, i, 0))],
        out_specs=pl.BlockSpec((tr, cols), lambda i, pos: (i, 0)))
    return pl.pallas_call(body, grid_spec=grid_spec, out_shape=SDS((h, cols), F32), name=name,
                          compiler_params=_cp(("parallel",)))(pos_arr, g, recv)


def allreduce_small(v, *, name):
    rws, cols = v.shape

    def body(v_ref, all_ref, sum_ref, send_sems, recv_sems, local_sem):
        x, y, c = _coords()
        me, sibling = (x, y, c), (x, y, 1 - c)
        chips = [(1 - x, y), (x, 1 - y), (1 - x, 1 - y)]

        def rows(px, py, pc):
            return all_ref.at[pl.ds(pl.multiple_of((4 * px + 2 * py + pc) * rws, 8), rws), :]

        def copy(k, block, to, src=None):
            return _remote(rows(*block) if src is None else src, rows(*block), send_sems.at[k], recv_sems.at[k], to)

        mine = pltpu.make_async_copy(v_ref, rows(*me), local_sem)
        mine.start()
        first = [copy(0, me, sibling, src=v_ref)]
        first += [copy(1 + j, me, (*chip, c), src=v_ref) for j, chip in enumerate(chips)]
        for cp in first:
            cp.start()
        passed = [copy(4 + j, (*chip, c), sibling) for j, chip in enumerate(chips)]
        for j, chip in enumerate(chips):
            copy(1 + j, (*chip, c), me).wait_recv()
            passed[j].start()
        copy(0, sibling, me).wait_recv()
        for j, chip in enumerate(chips):
            copy(4 + j, (*chip, 1 - c), me).wait_recv()
        for cp in first + passed:
            cp.wait_send()
        mine.wait()
        tot = all_ref[0:rws, :]
        for dev in range(1, 8):
            tot = tot + all_ref[dev * rws:(dev + 1) * rws, :]
        sum_ref[...] = tot

    vm = pl.BlockSpec(memory_space=pltpu.VMEM)
    return pl.pallas_call(
        body, in_specs=[vm], out_specs=[vm, vm],
        out_shape=[SDS((8 * rws, cols), v.dtype), SDS((rws, cols), v.dtype)],
        scratch_shapes=[pltpu.SemaphoreType.DMA((7,)), pltpu.SemaphoreType.DMA((7,)), pltpu.SemaphoreType.DMA],
        name=name)(v)[1]


def _pack_rows(parts, rows):
    out = []
    for a, r in zip(parts, rows):
        flat = a.reshape(-1)
        flat = jnp.pad(flat, (0, r * LANES - flat.shape[0]))
        out.append(flat.reshape(r, LANES))
    return jnp.concatenate(out, axis=0)


def _unpack_rows(packed, shapes, rows):
    out, at = [], 0
    for shp, r in zip(shapes, rows):
        size = int(np.prod(shp))
        out.append(packed[at:at + r].reshape(-1)[:size].reshape(shp))
        at += r
    return out


def kernel(x, g_pre_mix, w_in, b_forget, w_o_fox, w_o_dil, w_out, g_post_mix, g_pre_ffn, w_up, conv_w, conv_b, w_down, g_post_ffn, loss_target, m_g_pre_mix, m_w_in, m_b_forget, m_w_o_fox, m_w_o_dil, m_w_out, m_g_post_mix, m_g_pre_ffn, m_w_up, m_conv_w, m_conv_b, m_w_down, m_g_post_ffn, v_g_pre_mix, v_w_in, v_b_forget, v_w_o_fox, v_w_o_dil, v_w_out, v_g_post_mix, v_g_pre_ffn, v_w_up, v_conv_w, v_conv_b, v_w_down, v_g_post_ffn):
    xi, yi, ci = _coords()
    chip = 2 * xi + yi
    c_arr = jnp.reshape(ci, (1,)).astype(jnp.int32)
    chip_arr = jnp.reshape(chip, (1,)).astype(jnp.int32)
    xs = x[0]
    target = loss_target[0]
    s, d = xs.shape
    f_half = w_down.shape[1] * 4
    cols_in = w_in.shape[2]

    big = (w_in, w_o_fox, w_o_dil, w_out, w_up, w_down)
    shards = [w[0].astype(BF16) for w in big]
    a_in = allgather_balanced(shards[0], name="allgather_w_in")
    w_in_full = jnp.concatenate([jnp.where(chip == j, shards[0], a_in[j]) for j in range(4)], axis=1)
    nf = N_HEADS
    e_a, e_b = 3 * ATT_W, 3 * ATT_W + nf
    wz = jnp.concatenate([w_in_full[:, :e_a], w_in_full[:, e_b:]], axis=1)
    wf = jnp.pad(w_in_full[:, e_a:e_b], ((0, 0), (0, LANES - nf)))
    cb = conv_b
    bfo = jnp.pad(b_forget, ((0, 0), (0, LANES - nf)))

    h1 = rmsnorm_fwd(xs, g_pre_mix)
    z = mm([(h1, d, 0)], [(wz, d, 0)], nt=False, out_dtype=BF16, tm=s, tn=512, name="in_proj")
    fa = mm([(h1, d, 0)], [(wf, d, 0)], nt=False, out_dtype=F32, tm=s, tn=LANES, name="in_proj_forget")
    q_aug, k_aug, v_aug = fox_prep(z, fa, bfo)
    later = shards[1:] + [conv_w[0]]
    ya, lse_a, *late = fox_fwd(q_aug, k_aug, v_aug, gather=later, halved=[True] * 5 + [False], hps=N_HEADS)
    a_of, a_od, a_out, a_up, a_down, a_cw = [
        lax.dynamic_update_index_in_dim(a4, own, chip, 0) for a4, own in zip(late, later)]
    cw = jnp.concatenate([a_cw[j] for j in range(4)], axis=1)
    wo_a = jnp.concatenate([a_of[j] for j in range(4)], axis=1)
    wo_b = jnp.concatenate([a_od[j] for j in range(4)], axis=1)
    w_o = a_out.reshape(d, d)
    w_dn = a_down.reshape(f_half, d)
    wu_a = jnp.concatenate([a_up[0], a_up[1]], axis=1)
    wu_b = jnp.concatenate([a_up[2], a_up[3]], axis=1)
    cos_t, sin_t = rope_cos_sin(s)
    yb, lse_b = dil_fwd_all(z, cos_t, sin_t)
    pa, pb, mixed = gate_mix(ya, yb, wo_a, wo_b, z)
    y1, x1, h2 = proj_norm_res(mixed, w_o, g_post_mix, xs, g_pre_ffn, tm=1024, name="out_proj")
    ua, ub, conv_a, conv_bh, mid = ffn_up(h2, wu_a, wu_b, cw, cb)
    dout, dy2, gg_post_ffn, sq = proj_norm_loss(mid, w_dn, g_post_ffn, x1, target, name="down_proj")
    loss = lax.psum(0.5 * sq[0, 0] / d, ("x", "y", "c"))

    dmid = mm([(dy2, d, 0)], [(w_dn, d, 0)], nt=True, out_dtype=BF16, tm=2048, tn=f_half // 2, name="down_dgrad")
    dw_down, dw_down16 = wgrad((mid, f_half, 0), dy2, tk=f_half // 2, tn=1024, ts=2048, name="down_wgrad", bf16_copy=True)
    dua, dub, gc_a, gc_b = ffn_bwd(dmid, ua, ub, conv_a, conv_bh, cw)
    dx1, dy1, gg_pre_ffn, gg_post_mix = mm_norm_bwd(
        [(dua, f_half, 0), (dub, f_half, 0)], [(wu_a, f_half, 0), (wu_b, f_half, 0)],
        [(x1, g_pre_ffn, dout, F32), (y1, g_post_mix, None, BF16)], name="up_dgrad")
    dw_up = None
    for k, du in enumerate((dua, dub)):
        dw_up = wgrad((h2, d, 0), du, tk=1024, tn=f_half // 2, ts=2048, name=f"up_wgrad_{k}", chip_major=True,
                      slabs=(4, 2 * k), into=dw_up, bf16_copy=True)
    g_ffn = [(dw_up[0], dw_up[1]), (dw_down.reshape(4, f_half // 4, d), dw_down16.reshape(4, f_half // 4, d))]
    dw_out, dw_out16 = wgrad((mixed, d, 0), dy1, tk=1024, tn=1024, ts=2048, name="out_wgrad", bf16_copy=True)
    dpa, dpb, dz_g, dya, dyb, dd_a = mix_bwd(dy1, w_o, z, pa, pb, wo_a, wo_b, ya)
    by_chip_cols = lambda a: jnp.stack([a[:, j * (d // 4):(j + 1) * (d // 4)] for j in range(4)], axis=0)
    dw_of = [by_chip_cols(a) for a in wgrad((ya, ATT_W, 0), dpa, tk=ATT_W, tn=d, ts=1024, name="fox_o_wgrad", bf16_copy=True)]
    dw_od = [by_chip_cols(a) for a in wgrad((yb, ATT_W, 0), dpb, tk=ATT_W, tn=d, ts=1024, name="dil_o_wgrad", bf16_copy=True)]
    g_mix = [dw_of, dw_od, (dw_out.reshape(4, d // 4, d), dw_out16.reshape(4, d // 4, d))]
    dq_aug, dk_aug, dv_a, *got_ffn = fox_bwd(q_aug, k_aug, z, dya, lse_a, dd_a, exchange=[g[1] for g in g_ffn], kind="to_owners")
    dz_a, dfa, gg_bf = fox_post(dq_aug, dk_aug, dv_a, fa, bfo)
    *dz_b, got_of, got_od, got_out = dil_bwd_all(z, cos_t, sin_t, dyb, lse_b, yb, exchange=[g[1] for g in g_mix],
                                                 kind="to_owners")
    got_mix = [got_of, got_od, got_out]
    dwt_a = wgrad((dz_a, e_a, 0), h1, tk=e_a // 2, tn=d, ts=2048, name="in_wgrad_a")
    dwt_b = [wgrad((part, ATT_W, 0), h1, tk=ATT_W, tn=d, ts=2048, name=f"in_wgrad_b{k}") for k, part in enumerate(dz_b)]
    dwt_g = wgrad((dz_g, 2 * d, 0), h1, tk=d, tn=d, ts=2048, name="in_wgrad_g")
    dwt_f = wgrad((dfa, LANES, 0), h1, tk=LANES, tn=d, ts=2048, name="in_wgrad_f")
    dwt_full = jnp.concatenate([dwt_a, dwt_f[:nf], *dwt_b, dwt_g], axis=0)
    dw_in = jnp.stack([dwt_full[j * cols_in:(j + 1) * cols_in] for j in range(4)], axis=0)
    from_sib = grads_to_sibling([dw_in], [True], name="grads_to_sibling_in")
    sum_in = chip_sum(dw_in, from_sib[0], c_arr, True, name="chip_sum_w_in")
    grad_x, gg_pre_mix, got_in = mm_norm_bwd(
        [(dz_a, e_a, 0), *[(part, ATT_W, 0) for part in dz_b], (dz_g, d, 0), (dz_g, d, 1), (dfa, LANES, 0)],
        [(wz, e_a, 0), *[(wz, ATT_W, Z_QB + k) for k in range(3)], (wz, d, 3), (wz, d, 4), (wf, LANES, 0)],
        [(xs, g_pre_mix, dx1, F32)], exchange=[sum_in[1]], name="in_dgrad")

    names = ("w_in", "w_o_fox", "w_o_dil", "w_out", "w_up", "w_down")
    pos_arr = jnp.concatenate([chip_arr, c_arr])
    halves = [final_sum(sum_in[0], got_in, chip_arr, name="final_sum_w_in")] + [
        owner_sum(g[0], got, pos_arr, name=f"owner_sum_{nm}") for g, got, nm in zip(g_mix + g_ffn, got_mix + got_ffn, names[1:])]
    from_half = halves_to_full(halves, [True] + [False] * 5, name="halves_to_full")
    g_big = [None] + [lax.dynamic_update_slice_in_dim(full, mine, ci * mine.shape[0], axis=0)
                      for full, mine in zip(from_half[1:], halves[1:])]
    upd_big = [adamw(w[0], g, m[0], v[0], name=f"adamw_{nm}") for w, g, m, v, nm in list(zip(
        big, g_big, (m_w_in, m_w_o_fox, m_w_o_dil, m_w_out, m_w_up, m_w_down),
        (v_w_in, v_w_o_fox, v_w_o_dil, v_w_out, v_w_up, v_w_down), names))[1:]]
    to_t = lambda a: jnp.transpose(a, (2, 0, 1))
    from_t = lambda a: jnp.transpose(a, (1, 2, 0))
    *upd_in, g_in_t = adamw_rows_view(to_t(w_in), halves[0], from_half[0], to_t(m_w_in), to_t(v_w_in), c_arr,
                                      name="adamw_w_in")

    g_cw_loc = jnp.concatenate([gc_a[0:3], gc_b[0:3]], axis=1)
    g_cb_loc = jnp.concatenate([gc_a[3:4], gc_b[3:4]], axis=1)
    small_loc = [gg_pre_mix, gg_post_mix, gg_pre_ffn, gg_post_ffn, g_cb_loc, gg_bf[:, :nf], g_cw_loc]
    red_rows = (8, 8, 8, 8, 48, 8, 136)
    red = allreduce_small(_pack_rows(small_loc, red_rows), name="allreduce_small")
    g_pm, g_qm, g_pf, g_qf, g_cb, g_bf, g_cw_full = _unpack_rows(red, [a.shape for a in small_loc], red_rows)
    cols_cw = conv_w.shape[2]
    g_cw = lax.dynamic_slice_in_dim(g_cw_full, chip * cols_cw, cols_cw, axis=1)
    small_w = (g_pre_mix, g_post_mix, g_pre_ffn, g_post_ffn, conv_b, b_forget, conv_w[0])
    small_m = (m_g_pre_mix, m_g_post_mix, m_g_pre_ffn, m_g_post_ffn, m_conv_b, m_b_forget, m_conv_w[0])
    small_v = (v_g_pre_mix, v_g_post_mix, v_g_pre_ffn, v_g_post_ffn, v_conv_b, v_b_forget, v_conv_w[0])
    small_g = (g_pm, g_qm, g_pf, g_qf, g_cb, g_bf, g_cw)
    small_names = ("g_pre_mix", "g_post_mix", "g_pre_ffn", "g_post_ffn", "conv_b", "b_forget", "conv_w")
    per_param = [adamw(w, g, m, v, name=f"adamw_{nm}") for w, g, m, v, nm in zip(small_w, small_g, small_m, small_v, small_names)]
    upd_small = [[u[j] for u in per_param] for j in range(3)]

    order = ("g_pre_mix", "w_in", "b_forget", "w_o_fox", "w_o_dil", "w_out", "g_post_mix", "g_pre_ffn", "w_up", "conv_w",
             "conv_b", "w_down", "g_post_ffn")
    grads, deltas, new_ms, new_vs = {}, {}, {}, {}
    grads["w_in"] = from_t(g_in_t)
    deltas["w_in"], new_ms["w_in"], new_vs["w_in"] = (from_t(a) for a in upd_in)
    for k, nm in enumerate(names[1:]):
        grads[nm] = g_big[k + 1][None]
        deltas[nm], new_ms[nm], new_vs[nm] = (a[None] for a in upd_big[k])
    for k, nm in enumerate(small_names):
        lead = (lambda a: a[None]) if nm == "conv_w" else (lambda a: a)
        grads[nm] = lead(small_g[k])
        deltas[nm], new_ms[nm], new_vs[nm] = (lead(upd_small[j][k]) for j in range(3))
    return (loss, grad_x[None], *[grads[nm] for nm in order], *[deltas[nm] for nm in order],
            *[new_ms[nm] for nm in order], *[new_vs[nm] for nm in order])
```

```python
import functools
import math

import numpy as np
import jax
import jax.numpy as jnp
from jax import lax
from jax.experimental import pallas as pl
from jax.experimental.pallas import tpu as pltpu

F32 = jnp.float32
BF16 = jnp.bfloat16
SDS = jax.ShapeDtypeStruct
MESH = pl.DeviceIdType.MESH

HEAD_DIM = 64
N_HEADS = 8
LANES = 128
ATT_W = N_HEADS * HEAD_DIM
DIL_PATTERNS = ((128, 1), (512, 4), (2048, 16))
DIL_BLK = 128
ROPE_DIM = HEAD_DIM // 4
ROPE_THETA = 500000.0
RMS_EPS = 1e-6
NEG = -1e30
QK_SCALE = 1.0 / math.sqrt(HEAD_DIM)
ADAM_LR, ADAM_B1, ADAM_B2, ADAM_EPS, ADAM_WD, ADAM_STEP = 0.001, 0.9, 0.999, 1e-08, 0.01, 10
VMEM_LIMIT = 56 * 1024 * 1024

Z_QA, Z_KA, Z_VA, Z_QB, Z_KB, Z_VB = 0, 1, 2, 3, 4, 5
Z_W = 5120


def _cp(sem):
    return pltpu.CompilerParams(dimension_semantics=sem, vmem_limit_bytes=VMEM_LIMIT)


def _nt(a, b):
    return lax.dot_general(a, b, (((1,), (1,)), ((), ())), preferred_element_type=F32)


def _tn(a, b):
    return lax.dot_general(a, b, (((0,), (0,)), ((), ())), preferred_element_type=F32)


def _nn(a, b):
    return jnp.dot(a, b, preferred_element_type=F32)


def _lane(shape):
    return lax.broadcasted_iota(jnp.int32, shape, 1)


def _row(shape):
    return lax.broadcasted_iota(jnp.int32, shape, 0)


def rmsnorm_fwd(x, g, *, tm=1024):
    s, d = x.shape

    def body(x_ref, g_ref, h_ref):
        xv = x_ref[...]
        inv = lax.rsqrt(jnp.mean(xv * xv, axis=-1, keepdims=True) + RMS_EPS)
        h_ref[...] = (xv * inv * g_ref[...]).astype(h_ref.dtype)

    return pl.pallas_call(
        body, grid=(s // tm,),
        in_specs=[pl.BlockSpec((tm, d), lambda i: (i, 0)), pl.BlockSpec((1, d), lambda i: (0, 0))],
        out_specs=pl.BlockSpec((tm, d), lambda i: (i, 0)),
        out_shape=SDS((s, d), BF16), name="rmsnorm_fwd", compiler_params=_cp(("parallel",)))(x, g)


def mm(a_views, b_views, *, nt, out_dtype, tm, tn, name):
    n_p = len(a_views)
    m = a_views[0][0].shape[0]
    n = b_views[0][0].shape[0] if nt else b_views[0][0].shape[1]

    def body(*refs):
        o_ref = refs[-1]
        acc = None
        for p in range(n_p):
            av = refs[p][...].astype(BF16)
            bv = refs[n_p + p][...].astype(BF16)
            dv = _nt(av, bv) if nt else _nn(av, bv)
            acc = dv if acc is None else acc + dv
        o_ref[...] = acc.astype(o_ref.dtype)

    in_specs = []
    for arr, w, blk in a_views:
        in_specs.append(pl.BlockSpec((tm, w), functools.partial(lambda i, j, blk: (i, blk), blk=blk)))
    for arr, w, blk in b_views:
        if nt:
            in_specs.append(pl.BlockSpec((tn, w), functools.partial(lambda i, j, blk: (j, blk), blk=blk)))
        else:
            in_specs.append(pl.BlockSpec((w, tn), lambda i, j: (0, j)))
    return pl.pallas_call(
        body, grid=(m // tm, n // tn), in_specs=in_specs,
        out_specs=pl.BlockSpec((tm, tn), lambda i, j: (i, j)),
        out_shape=SDS((m, n), out_dtype), name=name,
        compiler_params=_cp(("parallel", "parallel")))(*[a[0] for a in a_views], *[b[0] for b in b_views])


def wgrad(a_view, g, *, tk, tn, ts, name, chip_major=False, slabs=None, into=None, bf16_copy=False):
    arr, ka, blk = a_view
    s, n = g.shape
    ns = s // ts
    total, first = slabs if slabs else (n // tn, 0)
    n_into = 0 if into is None else (2 if bf16_copy else 1)

    def body(a_ref, g_ref, *rest):
        o_ref = rest[n_into]

        @pl.when(pl.program_id(2) == 0)
        def _():
            o_ref[...] = jnp.zeros_like(o_ref)

        o_ref[...] += _tn(a_ref[...].astype(BF16), g_ref[...].astype(BF16))
        if bf16_copy:
            @pl.when(pl.program_id(2) == ns - 1)
            def _():
                rest[n_into + 1][...] = o_ref[...].astype(BF16)

    if chip_major:
        out_spec = pl.BlockSpec((None, tk, tn), lambda i, j, k: (first + j, i, 0))
        shape = (total, ka, tn)
    else:
        out_spec = pl.BlockSpec((tk, tn), lambda i, j, k: (i, j))
        shape = (ka, n)
    in_specs = [pl.BlockSpec((ts, tk), lambda i, j, k: (k, blk * (ka // tk) + i)),
                pl.BlockSpec((ts, tn), lambda i, j, k: (k, j))]
    args = [arr, g]
    if into is not None:
        earlier = list(into) if bf16_copy else [into]
        in_specs += [pl.BlockSpec(memory_space=pl.ANY)] * len(earlier)
        args += earlier
    out = pl.pallas_call(
        body, grid=(ka // tk, n // tn, ns), in_specs=in_specs,
        out_specs=[out_spec, out_spec] if bf16_copy else out_spec,
        out_shape=[SDS(shape, F32), SDS(shape, BF16)] if bf16_copy else SDS(shape, F32), name=name,
        input_output_aliases={2 + k: k for k in range(n_into)},
        compiler_params=_cp(("parallel", "parallel", "arbitrary")))(*args)
    return out


def _norm_bwd_rows(dh, xh, inv, g):
    dxh = dh * g
    dx = inv * (dxh - xh * jnp.mean(dxh * xh, axis=-1, keepdims=True))
    return dx, jnp.sum((dh * xh).reshape(dh.shape[0] // 8, 8, dh.shape[1]), axis=0)


def proj_norm_res(a, w, g, xres, g_next, *, tm=512, name):
    s, k = a.shape
    d = w.shape[1]

    def body(a_ref, w_ref, g_ref, x_ref, gn_ref, y_ref, o_ref, h_ref):
        y = _nn(a_ref[...], w_ref[...])
        inv = lax.rsqrt(jnp.mean(y * y, axis=-1, keepdims=True) + RMS_EPS)
        xn = x_ref[...] + y * inv * g_ref[...]
        y_ref[...] = y
        o_ref[...] = xn
        inv_n = lax.rsqrt(jnp.mean(xn * xn, axis=-1, keepdims=True) + RMS_EPS)
        h_ref[...] = (xn * inv_n * gn_ref[...]).astype(h_ref.dtype)

    row = pl.BlockSpec((tm, d), lambda i: (i, 0))
    vec = pl.BlockSpec((1, d), lambda i: (0, 0))
    return pl.pallas_call(
        body, grid=(s // tm,),
        in_specs=[pl.BlockSpec((tm, k), lambda i: (i, 0)), pl.BlockSpec((k, d), lambda i: (0, 0)), vec, row, vec],
        out_specs=[row, row, row], out_shape=[SDS((s, d), F32), SDS((s, d), F32), SDS((s, d), BF16)], name=name,
        compiler_params=_cp(("parallel",)))(a, w, g, xres, g_next)


def proj_norm_loss(a, w, g, xres, target, *, tm=512, name):
    s, k = a.shape
    d = w.shape[1]
    n = s // tm

    def body(a_ref, w_ref, g_ref, x_ref, t_ref, do_ref, dy_ref, dg_ref, l_ref, acc):
        i = pl.program_id(0)

        @pl.when(i == 0)
        def _():
            acc[...] = jnp.zeros_like(acc)
            l_ref[...] = jnp.zeros_like(l_ref)

        y = _nn(a_ref[...], w_ref[...])
        inv = lax.rsqrt(jnp.mean(y * y, axis=-1, keepdims=True) + RMS_EPS)
        yh = y * inv
        err = x_ref[...] + yh * g_ref[...] - t_ref[...]
        dout = err * (1.0 / d)
        do_ref[...] = dout
        l_ref[...] += jnp.sum(jnp.sum(err * err, axis=1, keepdims=True), axis=0, keepdims=True)
        dy, part = _norm_bwd_rows(dout, yh, inv, g_ref[...])
        dy_ref[...] = dy.astype(dy_ref.dtype)
        acc[...] += part

        @pl.when(i == n - 1)
        def _():
            dg_ref[...] = jnp.sum(acc[...], axis=0, keepdims=True)

    row = pl.BlockSpec((tm, d), lambda i: (i, 0))
    vec = pl.BlockSpec((1, d), lambda i: (0, 0))
    return pl.pallas_call(
        body, grid=(n,),
        in_specs=[pl.BlockSpec((tm, k), lambda i: (i, 0)), pl.BlockSpec((k, d), lambda i: (0, 0)), vec, row, row],
        out_specs=[row, row, vec, pl.BlockSpec((1, 1), lambda i: (0, 0))],
        out_shape=[SDS((s, d), F32), SDS((s, d), BF16), SDS((1, d), F32), SDS((1, 1), F32)],
        scratch_shapes=[pltpu.VMEM((8, d), F32)], name=name, compiler_params=_cp(("arbitrary",)))(a, w, g, xres, target)


def mm_norm_bwd(a_views, b_views, stages, exchange=(), *, tm=256, name):
    n_p, n_s, ne = len(a_views), len(stages), len(exchange)
    s = a_views[0][0].shape[0]
    d = b_views[0][0].shape[0]
    n = s // tm
    has_res = [st[2] is not None for st in stages]

    def body(*refs):
        a_refs, b_refs = refs[:n_p], refs[n_p:2 * n_p]
        at = 2 * n_p
        st_refs = []
        for k in range(n_s):
            cnt = 3 if has_res[k] else 2
            st_refs.append(refs[at:at + cnt])
            at += cnt
        e_ins = refs[at:at + ne]
        at += ne
        dx_refs, dg_refs = refs[at:at + n_s], refs[at + n_s:at + 2 * n_s]
        at += 2 * n_s
        e_outs = refs[at:at + ne]
        at += ne
        accs = refs[at:at + n_s]
        comm = (e_ins, e_outs) + tuple(refs[at + n_s:])
        i = pl.program_id(0)

        @pl.when(i == 0)
        def _():
            for acc in accs:
                acc[...] = jnp.zeros_like(acc)
            if ne:
                _to_chips_start(*comm)

        dh = None
        for p in range(n_p):
            part = _nt(a_refs[p][...].astype(BF16), b_refs[p][...].astype(BF16))
            dh = part if dh is None else dh + part
        for k in range(n_s):
            xv = st_refs[k][0][...]
            inv = lax.rsqrt(jnp.mean(xv * xv, axis=-1, keepdims=True) + RMS_EPS)
            dx, part = _norm_bwd_rows(dh, xv * inv, inv, st_refs[k][1][...])
            if has_res[k]:
                dx = dx + st_refs[k][2][...]
            dx_refs[k][...] = dx.astype(dx_refs[k].dtype)
            accs[k][...] += part
            dh = dx

        @pl.when(i == n - 1)
        def _():
            for k in range(n_s):
                dg_refs[k][...] = jnp.sum(accs[k][...], axis=0, keepdims=True)
            if ne:
                _to_chips_finish(*comm)

    row = pl.BlockSpec((tm, d), lambda i: (i, 0))
    vec = pl.BlockSpec((1, d), lambda i: (0, 0))
    in_specs, args = [], []
    for arr, w, blk in a_views:
        in_specs.append(pl.BlockSpec((tm, w), functools.partial(lambda i, blk: (i, blk), blk=blk)))
        args.append(arr)
    for arr, w, blk in b_views:
        in_specs.append(pl.BlockSpec((d, w), functools.partial(lambda i, blk: (0, blk), blk=blk)))
        args.append(arr)
    for x, g, res, _ in stages:
        in_specs += [row, vec] + ([row] if res is not None else [])
        args += [x, g] + ([res] if res is not None else [])
    return pl.pallas_call(
        body, grid=(n,), in_specs=in_specs + [ANY] * ne,
        out_specs=[row] * n_s + [vec] * n_s + [ANY] * ne,
        out_shape=[SDS((s, d), st[3]) for st in stages] + [SDS((1, d), F32)] * n_s + _to_chips_shapes(exchange),
        scratch_shapes=[pltpu.VMEM((8, d), F32)] * n_s + (_to_chips_sems(ne) if ne else []), name=name,
        compiler_params=_cp(("arbitrary",)))(*args, *exchange)


def _split3(v):
    hi = v.astype(BF16).astype(F32)
    r = v - hi
    mid = r.astype(BF16).astype(F32)
    lo = (r - mid).astype(BF16).astype(F32)
    return hi, mid, lo


def _tri(n, upper):
    r = np.arange(n)
    m = (r[:, None] <= r[None, :]) if upper else (r[:, None] >= r[None, :])
    return jnp.asarray(m.astype(np.float32))


def fox_prep(z, fa, bfo, *, tb=512):
    s = z.shape[0]
    n = s // tb

    def body(q_ref, k_ref, v_ref, fa_ref, b_ref, tri_ref, qa_ref, ka_ref, va_ref, carry):
        @pl.when(pl.program_id(0) == 0)
        def _():
            carry[...] = jnp.zeros_like(carry)

        xv = fa_ref[...] + b_ref[...]
        logf = jnp.minimum(xv, 0.0) - jnp.log(1.0 + jnp.exp(-jnp.abs(xv)))
        csum = jnp.dot(tri_ref[...], logf, preferred_element_type=F32, precision=lax.Precision.HIGHEST) + carry[0:1, :]
        carry[0:1, :] = csum[tb - 1:tb, :]
        lane = _lane((tb, LANES))
        for h in range(N_HEADS):
            hi, mid, lo = _split3(csum[:, h:h + 1])
            pair = (h // 2) * LANES
            qv = q_ref[:, pair:pair + LANES].astype(F32)
            kv = k_ref[:, pair:pair + LANES].astype(F32)
            vv = v_ref[:, pair:pair + LANES].astype(F32)
            if h % 2:
                qv = pltpu.roll(qv, 64, axis=1)
                kv = pltpu.roll(kv, 64, axis=1)
                vv = pltpu.roll(vv, 64, axis=1)
            va_ref[:, h * LANES:(h + 1) * LANES] = jnp.where(lane < 64, vv, jnp.where(lane == 64, 1.0, 0.0)).astype(BF16)
            one = jnp.where((lane >= 67) & (lane < 70), 1.0, 0.0)
            q_x = jnp.where(lane == 64, hi, jnp.where(lane == 65, mid, jnp.where(lane == 66, lo, one)))
            one = jnp.where((lane >= 64) & (lane < 67), 1.0, 0.0)
            k_x = jnp.where(lane == 67, -hi, jnp.where(lane == 68, -mid, jnp.where(lane == 69, -lo, one)))
            qa_ref[:, h * LANES:(h + 1) * LANES] = jnp.where(lane < 64, qv * QK_SCALE, q_x).astype(BF16)
            ka_ref[:, h * LANES:(h + 1) * LANES] = jnp.where(lane < 64, kv, k_x).astype(BF16)

    return pl.pallas_call(
        body, grid=(n,),
        in_specs=[pl.BlockSpec((tb, ATT_W), lambda i: (i, Z_QA)), pl.BlockSpec((tb, ATT_W), lambda i: (i, Z_KA)),
                  pl.BlockSpec((tb, ATT_W), lambda i: (i, Z_VA)),
                  pl.BlockSpec((tb, LANES), lambda i: (i, 0)), pl.BlockSpec((1, LANES), lambda i: (0, 0)),
                  pl.BlockSpec((tb, tb), lambda i: (0, 0))],
        out_specs=[pl.BlockSpec((tb, N_HEADS * LANES), lambda i: (i, 0))] * 3,
        out_shape=[SDS((s, N_HEADS * LANES), BF16)] * 3,
        scratch_shapes=[pltpu.VMEM((8, LANES), F32)],
        name="fox_prep", compiler_params=_cp(("arbitrary",)))(z, z, z, fa, bfo, _tri(tb, False))


def _causal_pairs(n, k_major):
    if k_major:
        pairs = [(qi, kj) for kj in range(n) for qi in range(kj, n)]
    else:
        pairs = [(qi, kj) for qi in range(n) for kj in range(qi + 1)]
    return (jnp.asarray([p[0] for p in pairs], jnp.int32), jnp.asarray([p[1] for p in pairs], jnp.int32), len(pairs))


def fox_fwd(q_aug, k_aug, v_aug, gather=(), halved=(), *, t=512, hps=4):
    s = v_aug.shape[0]
    qi_arr, kj_arr, n_pairs = _causal_pairs(s // t, False)
    ng = len(gather)
    n_groups = N_HEADS // hps

    def body(qi_ref, kj_ref, q_ref, k_ref, v_ref, *rest):
        g_ins, (o_ref, lse_ref), g_outs = rest[:ng], rest[ng:ng + 2], rest[ng + 2:2 * ng + 2]
        m_scr, acc_scr = rest[2 * ng + 2:2 * ng + 4]
        comm = (g_ins, g_outs) + tuple(rest[2 * ng + 4:]) + (list(halved),)
        step = pl.program_id(1)
        qi = qi_ref[step]
        kj = kj_ref[step]
        if ng:
            @pl.when((pl.program_id(0) == 0) & (step == 0))
            def _():
                _allgather_start(*comm)

        @pl.when(kj == 0)
        def _():
            m_scr[...] = jnp.full_like(m_scr, NEG)
            acc_scr[...] = jnp.zeros_like(acc_scr)

        def update(masked):
            for i in range(hps):
                sc = _nt(q_ref[:, i * LANES:(i + 1) * LANES], k_ref[:, i * LANES:(i + 1) * LANES])
                if masked:
                    sc = jnp.where(_row((t, t)) >= _lane((t, t)), sc, NEG)
                m_prev = m_scr[i]
                m_new = jnp.maximum(m_prev, jnp.max(sc, axis=-1, keepdims=True))
                p = jnp.exp((sc - jnp.tile(m_new, (1, t // LANES))).astype(BF16))
                acc_scr[i] = jnp.exp(m_prev - m_new) * acc_scr[i] + _nn(p, v_ref[:, i * LANES:(i + 1) * LANES])
                m_scr[i] = m_new

        @pl.when(kj < qi)
        def _():
            update(False)

        @pl.when(kj == qi)
        def _():
            update(True)
            lane = _lane((t, LANES))
            for pr in range(hps // 2):
                den = [acc_scr[2 * pr + i][:, 64:65] for i in range(2)]
                o_ref[:, pr * LANES:(pr + 1) * LANES] = jnp.where(
                    lane < 64, acc_scr[2 * pr] / den[0], pltpu.roll(acc_scr[2 * pr + 1] / den[1], 64, axis=1)).astype(o_ref.dtype)
                lse_ref[:, pr * LANES:(pr + 1) * LANES] = jnp.where(
                    lane < 64, m_scr[2 * pr] + jnp.log(den[0]), m_scr[2 * pr + 1] + jnp.log(den[1]))

        if ng:
            @pl.when((pl.program_id(0) == n_groups - 1) & (step == n_pairs - 1))
            def _():
                _allgather_finish(*comm)

    wide = hps * LANES
    grid_spec = pltpu.PrefetchScalarGridSpec(
        num_scalar_prefetch=2, grid=(n_groups, n_pairs),
        in_specs=[pl.BlockSpec((t, wide), lambda hg, st, qi, kj: (qi[st], hg)),
                  pl.BlockSpec((t, wide), lambda hg, st, qi, kj: (kj[st], hg)),
                  pl.BlockSpec((t, wide), lambda hg, st, qi, kj: (kj[st], hg))] + [ANY] * ng,
        out_specs=[pl.BlockSpec((t, wide // 2), lambda hg, st, qi, kj: (qi[st], hg))] * 2 + [ANY] * ng,
        scratch_shapes=[pltpu.VMEM((hps, t, LANES), F32)] * 2 + (_allgather_sems(ng) if ng else []))
    return pl.pallas_call(
        body, grid_spec=grid_spec, out_shape=[SDS((s, ATT_W), BF16), SDS((s, ATT_W), F32)] + _allgather_shapes(gather),
        name="fox_fwd", compiler_params=_cp(("arbitrary", "arbitrary")))(qi_arr, kj_arr, q_aug, k_aug, v_aug, *gather)


def fox_bwd(q_aug, k_aug, z, dy, lse, dd, exchange=(), kind="to_chips", *, t=512, hps=4):
    s = z.shape[0]
    qi_arr, kj_arr, n_pairs = _causal_pairs(s // t, True)
    ne = len(exchange)
    n_groups = N_HEADS // hps
    x_shapes, x_sems, x_start, x_finish = EXCHANGES[kind]

    def body(qi_ref, kj_ref, q_ref, k_ref, v_ref, do_ref, lse_ref, dd_ref, *rest):
        e_ins, (dq_ref, dk_ref, dv_ref), e_outs = rest[:ne], rest[ne:ne + 3], rest[ne + 3:2 * ne + 3]
        comm = (e_ins, e_outs) + tuple(rest[2 * ne + 3:])
        step = pl.program_id(1)
        qi = qi_ref[step]
        kj = kj_ref[step]
        if ne:
            @pl.when((pl.program_id(0) == 0) & (step == 0))
            def _():
                x_start(*comm)

        @pl.when(step == 0)
        def _():
            dq_ref[...] = jnp.zeros_like(dq_ref)

        @pl.when(qi == kj)
        def _():
            dk_ref[...] = jnp.zeros_like(dk_ref)
            dv_ref[...] = jnp.zeros_like(dv_ref)

        def update(masked):
            lane = _lane((t, LANES))
            rows = pl.ds(pl.multiple_of(qi * t, t), t)
            for pr in range(hps // 2):
                pair = slice(pr * LANES, (pr + 1) * LANES)
                dov = do_ref[:, pair]
                dv_new = None
                for i in range(2):
                    head = (lane < 64) if i == 0 else (lane >= 64)
                    own = slice((2 * pr + i) * LANES, (2 * pr + i + 1) * LANES)
                    col = slice(pr * LANES + i * 64, pr * LANES + i * 64 + 1)
                    qv = q_ref[:, own]
                    kv = k_ref[:, own]
                    sc = _nt(qv, kv)
                    if masked:
                        sc = jnp.where(_row((t, t)) >= _lane((t, t)), sc, NEG)
                    p = jnp.exp(sc - lse_ref[:, col])
                    dp = _nt(jnp.where(head, dov, jnp.zeros_like(dov)), v_ref[:, pair])
                    ds = (p * (dp - dd_ref[:, col])).astype(BF16)
                    dq_ref[rows, own] += _nn(ds, kv)
                    dk_ref[:, own] += _tn(ds, qv)
                    dvi = _tn(p.astype(BF16), dov)
                    dv_new = dvi if dv_new is None else jnp.where(head, dvi, dv_new)
                dv_ref[:, pair] += dv_new

        @pl.when(kj < qi)
        def _():
            update(False)

        @pl.when(kj == qi)
        def _():
            update(True)

        if ne:
            @pl.when((pl.program_id(0) == n_groups - 1) & (step == n_pairs - 1))
            def _():
                x_finish(*comm)

    wide, half = hps * LANES, hps // 2 * LANES
    v_blk = Z_VA * ATT_W // half
    grid_spec = pltpu.PrefetchScalarGridSpec(
        num_scalar_prefetch=2, grid=(n_groups, n_pairs),
        in_specs=[pl.BlockSpec((t, wide), lambda hg, st, qi, kj: (qi[st], hg)),
                  pl.BlockSpec((t, wide), lambda hg, st, qi, kj: (kj[st], hg)),
                  pl.BlockSpec((t, half), lambda hg, st, qi, kj: (kj[st], v_blk + hg)),
                  pl.BlockSpec((t, half), lambda hg, st, qi, kj: (qi[st], hg)),
                  pl.BlockSpec((t, half), lambda hg, st, qi, kj: (qi[st], hg)),
                  pl.BlockSpec((t, half), lambda hg, st, qi, kj: (qi[st], hg))] + [ANY] * ne,
        out_specs=[pl.BlockSpec((s, wide), lambda hg, st, qi, kj: (0, hg)),
                   pl.BlockSpec((t, wide), lambda hg, st, qi, kj: (kj[st], hg)),
                   pl.BlockSpec((t, half), lambda hg, st, qi, kj: (kj[st], hg))] + [ANY] * ne,
        scratch_shapes=x_sems(ne) if ne else [])
    return pl.pallas_call(
        body, grid_spec=grid_spec,
        out_shape=[SDS((s, N_HEADS * LANES), F32), SDS((s, N_HEADS * LANES), F32), SDS((s, ATT_W), F32)]
        + x_shapes(exchange),
        name="fox_bwd", compiler_params=_cp(("arbitrary", "arbitrary")))(qi_arr, kj_arr, q_aug, k_aug, z, dy, lse, dd, *exchange)


def fox_post(dq_aug, dk_aug, dv, fa, bfo, *, tb=512):
    s = dv.shape[0]
    n = s // tb

    def body(dq_ref, dk_ref, dv_ref, fa_ref, b_ref, tri_ref, dz_ref, dfa_ref, gb_ref, carry, acc):
        i = pl.program_id(0)

        @pl.when(i == 0)
        def _():
            carry[...] = jnp.zeros_like(carry)
            acc[...] = jnp.zeros_like(acc)

        lane = _lane((tb, LANES))
        d_f = jnp.zeros((tb, LANES), F32)
        for h in range(N_HEADS):
            col = dq_ref[:, h * LANES + 64:h * LANES + 65] - dk_ref[:, h * LANES + 67:h * LANES + 68]
            d_f = jnp.where(lane == h, col, d_f)
        suffix = jnp.dot(tri_ref[...], d_f, preferred_element_type=F32, precision=lax.Precision.HIGHEST) + carry[0:1, :]
        carry[0:1, :] = suffix[0:1, :]
        xv = fa_ref[...] + b_ref[...]
        dx = suffix * (1.0 / (1.0 + jnp.exp(xv)))
        dfa_ref[...] = dx.astype(dfa_ref.dtype)
        acc[...] += jnp.sum(dx.reshape(tb // 8, 8, LANES), axis=0)
        for hp in range(4):
            for src, off, scale in ((dq_ref, 0, QK_SCALE), (dk_ref, ATT_W, 1.0)):
                even = src[:, (2 * hp) * LANES:(2 * hp + 1) * LANES]
                odd = pltpu.roll(src[:, (2 * hp + 1) * LANES:(2 * hp + 2) * LANES], 64, axis=1)
                dz_ref[:, off + hp * LANES:off + (hp + 1) * LANES] = (jnp.where(lane < 64, even, odd) * scale).astype(BF16)
        dz_ref[:, 2 * ATT_W:3 * ATT_W] = dv_ref[...].astype(BF16)

        @pl.when(i == n - 1)
        def _():
            gb_ref[...] = jnp.sum(acc[...], axis=0, keepdims=True)

    rev = lambda i: (n - 1 - i, 0)
    return pl.pallas_call(
        body, grid=(n,),
        in_specs=[pl.BlockSpec((tb, N_HEADS * LANES), rev), pl.BlockSpec((tb, N_HEADS * LANES), rev),
                  pl.BlockSpec((tb, ATT_W), rev), pl.BlockSpec((tb, LANES), rev),
                  pl.BlockSpec((1, LANES), lambda i: (0, 0)), pl.BlockSpec((tb, tb), lambda i: (0, 0))],
        out_specs=[pl.BlockSpec((tb, 3 * ATT_W), rev), pl.BlockSpec((tb, LANES), rev),
                   pl.BlockSpec((1, LANES), lambda i: (0, 0))],
        out_shape=[SDS((s, 3 * ATT_W), BF16), SDS((s, LANES), BF16), SDS((1, LANES), F32)],
        scratch_shapes=[pltpu.VMEM((8, LANES), F32), pltpu.VMEM((8, LANES), F32)],
        name="fox_post", compiler_params=_cp(("arbitrary",)))(dq_aug, dk_aug, dv, fa, bfo, _tri(tb, True))


def rope_cos_sin(s):
    half = ROPE_DIM // 2
    inv_freq = ROPE_THETA ** (-jnp.arange(half, dtype=F32) * 2.0 / ROPE_DIM)
    ang = jnp.arange(s, dtype=F32)[:, None] * inv_freq[None, :]
    return jnp.tile(jnp.cos(ang), (1, LANES // half)), jnp.tile(jnp.sin(ang), (1, LANES // half))


def _rotate(x, cos, sin, sign):
    l64 = _lane(x.shape) & (HEAD_DIM - 1)
    first = l64 < ROPE_DIM // 2
    second = (l64 >= ROPE_DIM // 2) & (l64 < ROPE_DIM)
    from_next = jnp.where(first, -sign * sin, 0.0)
    from_prev = jnp.where(second, sign * sin, 0.0)
    return (x * jnp.where(first | second, cos, 1.0) + pltpu.roll(x, LANES - 8, axis=1) * from_next
            + pltpu.roll(x, 8, axis=1) * from_prev)


def _dil_rows(base, r):
    if r == 1:
        return pl.ds(pl.multiple_of(base, DIL_BLK), DIL_BLK)
    return pl.ds(base, DIL_BLK, stride=r)


def _dil_block(idx, r, nb):
    shift = nb.bit_length() - 1
    rho = idx >> shift
    n = idx & (nb - 1)
    base = rho + n * (r * DIL_BLK)
    return _dil_rows(base, r), _dil_rows(jnp.maximum(base - r * DIL_BLK, rho), r), n > 0


def _cat(a, b):
    return jnp.concatenate([a, b], axis=0)


def _two_heads(v, first_head):
    zero = jnp.zeros_like(v)
    return _cat(jnp.where(first_head, v, zero), jnp.where(first_head, zero, v))


def _dil_bands():
    b = DIL_BLK
    q = _row((2 * b, 2 * b)) & (b - 1)
    col = _lane((2 * b, 2 * b))
    return (col < b) & (col >= q), (col >= b) & (col - b <= q)


def _dil_load_qkv(zq_ref, zk_ref, zv_ref, cos_ref, sin_ref, q_ref, k_ref, v_ref, *, chunk=512):
    def step(i, carry):
        rows = pl.ds(pl.multiple_of(i * chunk, chunk), chunk)
        cos, sin = cos_ref[rows, :], sin_ref[rows, :]
        q_ref[rows, :] = _rotate(zq_ref[rows, :].astype(F32), cos, sin, 1.0) * QK_SCALE
        k_ref[rows, :] = _rotate(zk_ref[rows, :].astype(F32), cos, sin, 1.0)
        v_ref[rows, :] = zv_ref[rows, :].astype(F32)
        return carry

    lax.fori_loop(0, q_ref.shape[0] // chunk, step, 0)


def dil_fwd_all(z, cos_t, sin_t, *, unroll=32):
    s = z.shape[0]
    b = DIL_BLK
    n_blk = s // b

    def body(zq_ref, zk_ref, zv_ref, cos_ref, sin_ref, o_ref, l_ref, q_ref, k_ref, v_ref):
        _dil_load_qkv(zq_ref, zk_ref, zv_ref, cos_ref, sin_ref, q_ref, k_ref, v_ref)
        first_head = _lane((b, LANES)) < 64
        band_prev, band_cur = _dil_bands()
        for g, (_, r) in enumerate(DIL_PATTERNS):
            nb = n_blk // r

            def group(it, carry, g=g, r=r, nb=nb):
                loaded = []
                kc = vc = None
                for u in range(unroll):
                    rows_c, rows_p, has_prev = _dil_block(it * unroll + u, r, nb)
                    if u % min(nb, unroll):
                        kp, vp = kc, vc
                    else:
                        kp, vp = k_ref[rows_p, :].astype(BF16), v_ref[rows_p, :].astype(BF16)
                    kc, vc = k_ref[rows_c, :].astype(BF16), v_ref[rows_c, :].astype(BF16)
                    state = (o_ref[rows_c, :], l_ref[rows_c, :]) if g else None
                    loaded.append((rows_c, has_prev, [q_ref[rows_c, :].astype(BF16), kp, kc, vp, vc], state))
                done = []
                for rows_c, has_prev, (qv, kp, kc, vp, vc), state in loaded:
                    sc = jnp.where(band_cur | (band_prev & has_prev), _nt(_two_heads(qv, first_head), _cat(kp, kc)), NEG)
                    m = jnp.max(sc, axis=-1, keepdims=True)
                    p = jnp.exp(sc - m)
                    den = jnp.sum(p, axis=-1, keepdims=True)
                    both = _nn(p.astype(BF16), _cat(vp, vc)) / den
                    lse2 = m + jnp.log(den)
                    ov = jnp.where(first_head, both[:b], both[b:])
                    lse = jnp.where(first_head, lse2[:b], lse2[b:])
                    if state is not None:
                        m2 = jnp.maximum(state[1], lse)
                        wp = jnp.exp(state[1] - m2)
                        wn = jnp.exp(lse - m2)
                        ov = (wp * state[0] + wn * ov) / (wp + wn)
                        lse = m2 + jnp.log(wp + wn)
                    done.append((rows_c, ov, lse))
                for rows_c, ov, lse in done:
                    o_ref[rows_c, :] = ov
                    l_ref[rows_c, :] = lse
                return carry

            lax.fori_loop(0, n_blk // unroll, group, 0)

    col_blk = lambda k: pl.BlockSpec((s, LANES), lambda hp: (0, 4 * k + hp))
    table = pl.BlockSpec((s, LANES), lambda hp: (0, 0))
    out = pl.BlockSpec((s, LANES), lambda hp: (0, hp))
    return pl.pallas_call(
        body, grid=(4,), in_specs=[col_blk(Z_QB), col_blk(Z_KB), col_blk(Z_VB), table, table], out_specs=[out, out],
        out_shape=[SDS((s, ATT_W), F32)] * 2, scratch_shapes=[pltpu.VMEM((s, LANES), F32)] * 3, name="dil_fwd",
        compiler_params=_cp(("parallel",)))(z, z, z, cos_t, sin_t)


def dil_bwd_all(z, cos_t, sin_t, dy, lse, y, exchange=(), kind="to_chips", *, unroll=16):
    s = z.shape[0]
    b = DIL_BLK
    n_blk = s // b
    ne = len(exchange)
    x_shapes, x_sems, x_start, x_finish = EXCHANGES[kind]

    def body(zq_ref, zk_ref, zv_ref, cos_ref, sin_ref, do_ref, l_ref, y_ref, *rest):
        e_ins, (gq_ref, gk_ref, gv_ref), e_outs = rest[:ne], rest[ne:ne + 3], rest[ne + 3:2 * ne + 3]
        q_ref, k_ref, v_ref, dq_ref, dk_ref, dv_ref = rest[2 * ne + 3:2 * ne + 9]
        comm = (e_ins, e_outs) + tuple(rest[2 * ne + 9:])
        if ne:
            @pl.when(pl.program_id(0) == 0)
            def _():
                x_start(*comm)

        _dil_load_qkv(zq_ref, zk_ref, zv_ref, cos_ref, sin_ref, q_ref, k_ref, v_ref)
        dq_ref[...] = jnp.zeros_like(dq_ref)
        dk_ref[...] = jnp.zeros_like(dk_ref)
        dv_ref[...] = jnp.zeros_like(dv_ref)
        first_head = _lane((b, LANES)) < 64
        band_prev, band_cur = _dil_bands()
        for _, r in DIL_PATTERNS:
            nb = n_blk // r

            def group(it, carry, r=r, nb=nb):
                loaded = []
                kc = vc = None
                for u in range(unroll):
                    rows_c, rows_p, has_prev = _dil_block(it * unroll + u, r, nb)
                    if u % min(nb, unroll):
                        kp, vp = kc, vc
                    else:
                        kp, vp = k_ref[rows_p, :].astype(BF16), v_ref[rows_p, :].astype(BF16)
                    kc, vc = k_ref[rows_c, :].astype(BF16), v_ref[rows_c, :].astype(BF16)
                    vals = [q_ref[rows_c, :].astype(BF16), kp, kc, vp, vc, do_ref[rows_c, :], l_ref[rows_c, :], y_ref[rows_c, :]]
                    loaded.append((rows_c, rows_p, has_prev, vals))
                done = []
                for rows_c, rows_p, has_prev, (qv, kp, kc, vp, vc, dof, lv, yv) in loaded:
                    q2 = _two_heads(qv, first_head)
                    do2 = _two_heads(dof.astype(BF16), first_head)
                    kcat, vcat = _cat(kp, kc), _cat(vp, vc)
                    lse2 = _cat(lv[:, 0:1], lv[:, 64:65])
                    dd2 = jnp.sum(_two_heads(dof * yv, first_head), axis=-1, keepdims=True)
                    p = jnp.exp(jnp.where(band_cur | (band_prev & has_prev), _nt(q2, kcat), NEG) - lse2)
                    ds = (p * (_nt(do2, vcat) - dd2)).astype(BF16)
                    dq2 = _nn(ds, kcat)
                    dkcat = _tn(ds, q2)
                    dvcat = _tn(p.astype(BF16), do2)
                    done.append((rows_c, rows_p, (jnp.where(first_head, dq2[:b], dq2[b:]), dkcat[:b], dkcat[b:],
                                                  dvcat[:b], dvcat[b:])))
                for rows_c, rows_p, (dq, dk_p, dk_c, dv_p, dv_c) in done:
                    dq_ref[rows_c, :] += dq
                    dk_ref[rows_p, :] += dk_p
                    dk_ref[rows_c, :] += dk_c
                    dv_ref[rows_p, :] += dv_p
                    dv_ref[rows_c, :] += dv_c
                return carry

            lax.fori_loop(0, n_blk // unroll, group, 0)

        def finish(i, carry, chunk=512):
            rows = pl.ds(pl.multiple_of(i * chunk, chunk), chunk)
            cos, sin = cos_ref[rows, :], sin_ref[rows, :]
            gq_ref[rows, :] = (_rotate(dq_ref[rows, :], cos, sin, -1.0) * QK_SCALE).astype(BF16)
            gk_ref[rows, :] = _rotate(dk_ref[rows, :], cos, sin, -1.0).astype(BF16)
            gv_ref[rows, :] = dv_ref[rows, :].astype(BF16)
            return carry

        lax.fori_loop(0, s // 512, finish, 0)
        if ne:
            @pl.when(pl.program_id(0) == 3)
            def _():
                x_finish(*comm)

    col_blk = lambda k: pl.BlockSpec((s, LANES), lambda hp: (0, 4 * k + hp))
    table = pl.BlockSpec((s, LANES), lambda hp: (0, 0))
    nat = pl.BlockSpec((s, LANES), lambda hp: (0, hp))
    return pl.pallas_call(
        body, grid=(4,), in_specs=[col_blk(Z_QB), col_blk(Z_KB), col_blk(Z_VB), table, table, nat, nat, nat] + [ANY] * ne,
        out_specs=[nat, nat, nat] + [ANY] * ne, out_shape=[SDS((s, ATT_W), BF16)] * 3 + x_shapes(exchange),
        scratch_shapes=[pltpu.VMEM((s, LANES), F32)] * 6 + (x_sems(ne) if ne else []), name="dil_bwd",
        compiler_params=_cp(("arbitrary",)))(z, z, z, cos_t, sin_t, dy, lse, y, *exchange)


def _sigmoid(v):
    return 1.0 / (1.0 + jnp.exp(-v))


def gate_mix(ya, yb, wa, wb, z, *, tm=2048, tn=512):
    s = ya.shape[0]
    d = wa.shape[1]
    ga_blk = 3 * ATT_W * 2 // tn
    gb_blk = ga_blk + d // tn

    def body(ya_ref, yb_ref, wa_ref, wb_ref, ga_ref, gb_ref, pa_ref, pb_ref, mx_ref):
        pa = _nn(ya_ref[...], wa_ref[...])
        pb = _nn(yb_ref[...].astype(BF16), wb_ref[...])
        pa_ref[...] = pa.astype(BF16)
        pb_ref[...] = pb.astype(BF16)
        mx_ref[...] = (_sigmoid(ga_ref[...].astype(F32)) * pa + _sigmoid(gb_ref[...].astype(F32)) * pb).astype(BF16)

    out = pl.BlockSpec((tm, tn), lambda i, j: (i, j))
    return pl.pallas_call(
        body, grid=(s // tm, d // tn),
        in_specs=[pl.BlockSpec((tm, ATT_W), lambda i, j: (i, 0)), pl.BlockSpec((tm, ATT_W), lambda i, j: (i, 0)),
                  pl.BlockSpec((ATT_W, tn), lambda i, j: (0, j)), pl.BlockSpec((ATT_W, tn), lambda i, j: (0, j)),
                  pl.BlockSpec((tm, tn), lambda i, j: (i, ga_blk + j)), pl.BlockSpec((tm, tn), lambda i, j: (i, gb_blk + j))],
        out_specs=[out, out, out], out_shape=[SDS((s, d), BF16)] * 3, name="gate_mix",
        compiler_params=_cp(("parallel", "parallel")))(ya, yb, wa, wb, z, z)


def mix_bwd(dy, w_o, z, pa, pb, wo_a, wo_b, ya, *, tm=512):
    s, d = dy.shape

    def body(dy_ref, wo_ref, ga_ref, gb_ref, pa_ref, pb_ref, wa_ref, wb_ref, ya_ref,
             dpa_ref, dpb_ref, dg_ref, dya_ref, dyb_ref, dd_ref):
        dm = _nt(dy_ref[...], wo_ref[...])
        sa = _sigmoid(ga_ref[...].astype(F32))
        sb = _sigmoid(gb_ref[...].astype(F32))
        dpa = (dm * sa).astype(BF16)
        dpb = (dm * sb).astype(BF16)
        dpa_ref[...] = dpa
        dpb_ref[...] = dpb
        dg_ref[:, 0:d] = (dm * pa_ref[...].astype(F32) * sa * (1.0 - sa)).astype(BF16)
        dg_ref[:, d:2 * d] = (dm * pb_ref[...].astype(F32) * sb * (1.0 - sb)).astype(BF16)
        dya = _nt(dpa, wa_ref[...]).astype(BF16)
        dya_ref[...] = dya
        dyb_ref[...] = _nt(dpb, wb_ref[...])
        lane = _lane((tm, LANES))
        for pr in range(ATT_W // LANES):
            pair = slice(pr * LANES, (pr + 1) * LANES)
            prod = dya[:, pair].astype(F32) * ya_ref[:, pair].astype(F32)
            lo = jnp.sum(jnp.where(lane < 64, prod, 0.0), axis=-1, keepdims=True)
            hi = jnp.sum(jnp.where(lane >= 64, prod, 0.0), axis=-1, keepdims=True)
            dd_ref[:, pair] = jnp.where(lane < 64, lo, hi)

    row = pl.BlockSpec((tm, d), lambda i: (i, 0))
    att = pl.BlockSpec((tm, ATT_W), lambda i: (i, 0))
    whole = lambda a: pl.BlockSpec(a.shape, lambda i: (0, 0))
    return pl.pallas_call(
        body, grid=(s // tm,),
        in_specs=[row, whole(w_o), pl.BlockSpec((tm, d), lambda i: (i, 3)), pl.BlockSpec((tm, d), lambda i: (i, 4)), row, row,
                  whole(wo_a), whole(wo_b), att],
        out_specs=[row, row, pl.BlockSpec((tm, 2 * d), lambda i: (i, 0)), att, att, att],
        out_shape=[SDS((s, d), BF16), SDS((s, d), BF16), SDS((s, 2 * d), BF16), SDS((s, ATT_W), BF16),
                   SDS((s, ATT_W), F32), SDS((s, ATT_W), F32)], name="mix_bwd",
        compiler_params=_cp(("parallel",)))(dy, w_o, z, z, pa, pb, wo_a, wo_b, ya)


GELU_C = math.sqrt(2.0 / math.pi)


def _gelu_parts(a):
    a2 = a * a
    th = jnp.tanh(a * (GELU_C + (GELU_C * 0.044715) * a2))
    half = 0.5 * a
    gelu = half + half * th
    dgelu = (0.5 + 0.5 * th) + half * (1.0 - th * th) * (GELU_C + (3.0 * GELU_C * 0.044715) * a2)
    return gelu, dgelu


def _causal_taps(u, before):
    row = _row(u.shape)
    r1 = jnp.where(row == 0, before[7:8, :], pltpu.roll(u, 1, axis=0))
    r2 = jnp.where(row == 0, before[6:7, :], jnp.where(row == 1, before[7:8, :], pltpu.roll(u, 2, axis=0)))
    return r1, r2


def ffn_up(h, wa, wb, cw, cb, *, tm=2048, tn=256):
    s, d = h.shape
    f = wa.shape[1]
    nj = f // tn

    def body(h_ref, wa_ref, wb_ref, cwa_ref, cwb_ref, cba_ref, cbb_ref, ua_ref, ub_ref, ca_ref, cbo_ref, m_ref, carry):
        @pl.when(pl.program_id(1) == 0)
        def _():
            carry[...] = jnp.zeros_like(carry)

        conv = []
        for k, (w_ref, cw_ref, cb_ref, u_ref, c_ref) in enumerate(((wa_ref, cwa_ref, cba_ref, ua_ref, ca_ref),
                                                                   (wb_ref, cwb_ref, cbb_ref, ub_ref, cbo_ref))):
            u16 = _nn(h_ref[...], w_ref[...]).astype(BF16)
            u_ref[...] = u16
            u = u16.astype(F32)
            r1, r2 = _causal_taps(u, carry[k])
            carry[k] = u[tm - 8:tm, :]
            c16 = (cw_ref[0:1, :] * r2 + cw_ref[1:2, :] * r1 + cw_ref[2:3, :] * u + cb_ref[...]).astype(BF16)
            c_ref[...] = c16
            conv.append(c16.astype(F32))
        m_ref[...] = (_gelu_parts(conv[0])[0] * conv[1]).astype(BF16)

    out = pl.BlockSpec((tm, tn), lambda j, i: (i, j))
    return pl.pallas_call(
        body, grid=(nj, s // tm),
        in_specs=[pl.BlockSpec((tm, d), lambda j, i: (i, 0)),
                  pl.BlockSpec((d, tn), lambda j, i: (0, j)), pl.BlockSpec((d, tn), lambda j, i: (0, j)),
                  pl.BlockSpec((3, tn), lambda j, i: (0, j)), pl.BlockSpec((3, tn), lambda j, i: (0, nj + j)),
                  pl.BlockSpec((1, tn), lambda j, i: (0, j)), pl.BlockSpec((1, tn), lambda j, i: (0, nj + j))],
        out_specs=[out] * 5, out_shape=[SDS((s, f), BF16)] * 5,
        scratch_shapes=[pltpu.VMEM((2, 8, tn), F32)], name="ffn_up",
        compiler_params=_cp(("parallel", "arbitrary")))(h, wa, wb, cw, cw, cb, cb)


def ffn_bwd(dm, ua, ub, ca, cbo, cw, *, tm=2048, tn=256):
    s, f = dm.shape
    nj = f // tn
    ni = s // tm

    def body(dm_ref, ua_ref, ub_ref, ca_ref, cbo_ref, cwa_ref, cwb_ref, dua_ref, dub_ref, ga_ref, gb_ref, carry):
        @pl.when(pl.program_id(1) == 0)
        def _():
            carry[...] = jnp.zeros_like(carry)
            ga_ref[...] = jnp.zeros_like(ga_ref)
            gb_ref[...] = jnp.zeros_like(gb_ref)

        row = _row((tm, tn))
        dmv = dm_ref[...].astype(F32)
        gelu, dgelu = _gelu_parts(ca_ref[...].astype(F32))
        dcs = (dmv * cbo_ref[...].astype(F32) * dgelu, dmv * gelu)
        for k, (dc, u_ref, cw_ref, du_ref, g_ref) in enumerate(((dcs[0], ua_ref, cwa_ref, dua_ref, ga_ref),
                                                                (dcs[1], ub_ref, cwb_ref, dub_ref, gb_ref))):
            u = u_ref[...].astype(F32)
            after = carry[k]
            n1 = jnp.where(row == tm - 1, after[0:1, :], pltpu.roll(dc, tm - 1, axis=0))
            n2 = jnp.where(row == tm - 2, after[0:1, :], jnp.where(row == tm - 1, after[1:2, :], pltpu.roll(dc, tm - 2, axis=0)))
            g_ref[0:1, :] += jnp.sum(n2 * u, axis=0, keepdims=True)
            g_ref[1:2, :] += jnp.sum(n1 * u, axis=0, keepdims=True)
            g_ref[2:3, :] += jnp.sum(dc * u, axis=0, keepdims=True)
            g_ref[3:4, :] += jnp.sum(dc, axis=0, keepdims=True)
            du_ref[...] = (cw_ref[2:3, :] * dc + cw_ref[1:2, :] * n1 + cw_ref[0:1, :] * n2).astype(BF16)
            carry[k] = dc[0:8, :]

    tile = pl.BlockSpec((tm, tn), lambda j, i: (ni - 1 - i, j))
    gspec = pl.BlockSpec((8, tn), lambda j, i: (0, j))
    return pl.pallas_call(
        body, grid=(nj, ni),
        in_specs=[tile] * 5 + [pl.BlockSpec((3, tn), lambda j, i: (0, j)), pl.BlockSpec((3, tn), lambda j, i: (0, nj + j))],
        out_specs=[tile, tile, gspec, gspec],
        out_shape=[SDS((s, f), BF16), SDS((s, f), BF16), SDS((8, f), F32), SDS((8, f), F32)],
        scratch_shapes=[pltpu.VMEM((2, 8, tn), F32)], name="ffn_bwd",
        compiler_params=_cp(("parallel", "arbitrary")))(dm, ua, ub, ca, cbo, cw, cw)


def adamw(w, g, m, v, *, name, tr=None):
    r = w.shape[0]
    rest = w.shape[1:]
    if tr is None:
        tr = r
        for cand in (256, 128, 64, 32, 16, 8):
            if r % cand == 0:
                tr = cand
                break

    def body(w_ref, g_ref, m_ref, v_ref, d_ref, nm_ref, nv_ref):
        gv = g_ref[...]
        mn = ADAM_B1 * m_ref[...] + (1.0 - ADAM_B1) * gv
        vn = ADAM_B2 * v_ref[...] + (1.0 - ADAM_B2) * (gv * gv)
        m_hat = mn / (1.0 - ADAM_B1 ** ADAM_STEP)
        v_hat = vn / (1.0 - ADAM_B2 ** ADAM_STEP)
        d_ref[...] = -ADAM_LR * (m_hat / (jnp.sqrt(v_hat) + ADAM_EPS) + ADAM_WD * w_ref[...])
        nm_ref[...] = mn
        nv_ref[...] = vn

    blk = pl.BlockSpec((tr,) + rest, lambda i: (i,) + (0,) * len(rest))
    return pl.pallas_call(body, grid=(r // tr,), in_specs=[blk] * 4, out_specs=[blk] * 3, out_shape=[SDS(w.shape, F32)] * 3,
                          name=name, compiler_params=_cp(("parallel",)))(w, g, m, v)


def adamw_rows_view(w, g_mine, g_full, m, v, c_arr, *, name, tc=256):
    r, _, c = w.shape
    per_half = c // 2 // tc

    def body(c_ref, w_ref, gm_ref, gf_ref, m_ref, v_ref, d_ref, nm_ref, nv_ref, go_ref):
        mine = (pl.program_id(0) >> (per_half.bit_length() - 1)) == c_ref[0]
        gv = jnp.where(mine, gm_ref[...], gf_ref[...])
        mn = ADAM_B1 * m_ref[:, 0, :] + (1.0 - ADAM_B1) * gv
        vn = ADAM_B2 * v_ref[:, 0, :] + (1.0 - ADAM_B2) * (gv * gv)
        m_hat = mn / (1.0 - ADAM_B1 ** ADAM_STEP)
        v_hat = vn / (1.0 - ADAM_B2 ** ADAM_STEP)
        d_ref[:, 0, :] = -ADAM_LR * (m_hat / (jnp.sqrt(v_hat) + ADAM_EPS) + ADAM_WD * w_ref[:, 0, :])
        nm_ref[:, 0, :] = mn
        nv_ref[:, 0, :] = vn
        go_ref[:, 0, :] = gv

    b3 = pl.BlockSpec((r, 1, tc), lambda i, c_ref: (0, 0, i))
    own = pl.BlockSpec((r, tc), lambda i, c_ref: (0, jnp.clip(i - c_ref[0] * per_half, 0, per_half - 1)))
    full = pl.BlockSpec((r, tc), lambda i, c_ref: (0, i))
    grid_spec = pltpu.PrefetchScalarGridSpec(num_scalar_prefetch=1, grid=(c // tc,), in_specs=[b3, own, full, b3, b3],
                                             out_specs=[b3] * 4)
    return pl.pallas_call(body, grid_spec=grid_spec, out_shape=[SDS(w.shape, F32)] * 4, name=name,
                          compiler_params=_cp(("parallel",)))(c_arr, w, g_mine, g_full, m, v)


ANY = pl.BlockSpec(memory_space=pl.ANY)
ICI_KINDS = ("x", "y", "xy")


def _coords():
    return lax.axis_index("x"), lax.axis_index("y"), lax.axis_index("c")


def _peer(kind, x, y, c):
    if kind == "c":
        return (x, y, 1 - c)
    if kind == "x":
        return (1 - x, y, c)
    if kind == "y":
        return (x, 1 - y, c)
    return (1 - x, 1 - y, c)


def _chip_of(p):
    return 2 * p[0] + p[1]


def _half(rows, which):
    h = rows // 2
    return pl.ds(pl.multiple_of(which * h, 16), h)


def _remote(src, dst, send_sem, recv_sem, to):
    return pltpu.make_async_remote_copy(src_ref=src, dst_ref=dst, send_sem=send_sem, recv_sem=recv_sem,
                                        device_id=to, device_id_type=MESH)


def allgather_balanced(shard, *, name):
    r, cols = shard.shape
    h, q = r // 2, r // 4

    def body(in_ref, out_ref, send_sems, recv_sems):
        x, y, c = _coords()
        me, sibling = (x, y, c), (x, y, 1 - c)
        nbr = ((1 - x, y, c), (x, 1 - y, c))
        chip = (2 * (1 - x) + y, 2 * x + (1 - y), 2 * (1 - x) + (1 - y))
        quarter = lambda core, i: pl.ds(pl.multiple_of(core * h + i * q, 16), q)
        sent = []

        def go(src, dst, slot, to):
            cp = _remote(src, dst, send_sems.at[slot], recv_sems.at[slot], to)
            cp.start()
            sent.append(cp)

        def landed(region, slot):
            _remote(region, region, send_sems.at[slot], recv_sems.at[slot], me).wait_recv()

        for i in range(2):
            for k in range(2):
                qi = k if i == 0 else 1 - k
                go(in_ref.at[quarter(c, qi)], out_ref.at[2 * x + y, quarter(c, qi)], 2 * k + qi, nbr[k])
        for k in range(2):
            piece = out_ref.at[chip[k], quarter(c, k)]
            landed(piece, 2 * k + k)
            go(piece, piece, 4 + k, nbr[1 - k])
            go(piece, piece, 6 + 2 * k + k, sibling)
        for k in range(2):
            piece = out_ref.at[chip[k], quarter(c, 1 - k)]
            landed(piece, 2 * k + 1 - k)
            go(piece, piece, 6 + 2 * k + 1 - k, sibling)
        for k in range(2):
            piece = out_ref.at[chip[2], quarter(c, k)]
            landed(piece, 4 + k)
            go(piece, piece, 10 + k, sibling)
        for k in range(2):
            for i in range(2):
                landed(out_ref.at[chip[k], quarter(1 - c, i)], 6 + 2 * k + i)
            landed(out_ref.at[chip[2], quarter(1 - c, k)], 10 + k)
        for cp in sent:
            cp.wait_send()

    return pl.pallas_call(
        body, in_specs=[ANY], out_specs=ANY, out_shape=SDS((4,) + shard.shape, shard.dtype),
        scratch_shapes=[pltpu.SemaphoreType.DMA((12,)), pltpu.SemaphoreType.DMA((12,))], name=name)(shard)


def _allgather_shapes(shards):
    return [SDS((4,) + a.shape, a.dtype) for a in shards]


def _allgather_sems(n):
    return [pltpu.SemaphoreType.DMA((n, 6)), pltpu.SemaphoreType.DMA((n, 6))]


def _allgather_rows(ref, is_halved, which):
    r = ref.shape[0]
    return _half(r, which) if is_halved else pl.ds(0, r)


def _allgather_first(ins, outs, send_sems, recv_sems, halved):
    x, y, c = _coords()
    my_chip = 2 * x + y
    cps = []
    for w in range(len(ins)):
        rows = _allgather_rows(ins[w], halved[w], c)
        for k, kind in enumerate(ICI_KINDS):
            cps.append(_remote(ins[w].at[rows], outs[w].at[my_chip, rows], send_sems.at[w, k], recv_sems.at[w, k],
                               _peer(kind, x, y, c)))
    return cps


def _allgather_start(ins, outs, send_sems, recv_sems, halved):
    for cp in _allgather_first(ins, outs, send_sems, recv_sems, halved):
        cp.start()


def _allgather_finish(ins, outs, send_sems, recv_sems, halved):
    x, y, c = _coords()
    me = (x, y, c)
    second = []
    for w in range(len(ins)):
        for k, kind in enumerate(ICI_KINDS):
            landed = outs[w].at[_chip_of(_peer(kind, x, y, c)), _allgather_rows(ins[w], halved[w], c)]
            _remote(landed, landed, send_sems.at[w, k], recv_sems.at[w, k], me).wait_recv()
            if halved[w]:
                cp = _remote(landed, landed, send_sems.at[w, 3 + k], recv_sems.at[w, 3 + k], _peer("c", x, y, c))
                cp.start()
                second.append(cp)
    for w in range(len(ins)):
        if halved[w]:
            for k, kind in enumerate(ICI_KINDS):
                other = outs[w].at[_chip_of(_peer(kind, x, y, c)), _allgather_rows(ins[w], True, 1 - c)]
                _remote(other, other, send_sems.at[w, 3 + k], recv_sems.at[w, 3 + k], me).wait_recv()
    for cp in _allgather_first(ins, outs, send_sems, recv_sems, halved) + second:
        cp.wait_send()


def _half_of(ref, by_cols, which):
    lead = (slice(None),) * (len(ref.shape) - 2)
    if by_cols:
        h = ref.shape[-1] // 2
        return ref.at[lead + (slice(None), pl.ds(pl.multiple_of(which * h, LANES), h))]
    return ref.at[lead + (_half(ref.shape[-2], which),)]


def _half_shape(shape, by_cols):
    return shape[:-1] + (shape[-1] // 2,) if by_cols else shape[:-2] + (shape[-2] // 2, shape[-1])


def grads_to_sibling(gs, by_cols, *, name):
    n = len(gs)

    def body(*refs):
        ins, outs = refs[:n], refs[n:2 * n]
        send_sems, recv_sems = refs[2 * n:]
        x, y, c = _coords()
        cps = []
        for w in range(n):
            cp = _remote(_half_of(ins[w], by_cols[w], 1 - c), outs[w], send_sems.at[w], recv_sems.at[w], _peer("c", x, y, c))
            cp.start()
            cps.append(cp)
        for cp in cps:
            cp.wait()

    return pl.pallas_call(
        body, in_specs=[ANY] * n, out_specs=[ANY] * n,
        out_shape=[SDS(_half_shape(a.shape, bc), a.dtype) for a, bc in zip(gs, by_cols)],
        scratch_shapes=[pltpu.SemaphoreType.DMA((n,)), pltpu.SemaphoreType.DMA((n,))], name=name)(*gs)


def _to_chips_shapes(ps):
    return [SDS((3,) + a.shape[1:], a.dtype) for a in ps]


def _to_chips_sems(n):
    return [pltpu.SemaphoreType.DMA((n, 3)), pltpu.SemaphoreType.DMA((n, 3))]


def _to_chips_copies(ins, outs, send_sems, recv_sems):
    x, y, c = _coords()
    cps = []
    for w in range(len(ins)):
        for k, kind in enumerate(ICI_KINDS):
            to = _peer(kind, x, y, c)
            cps.append(_remote(ins[w].at[_chip_of(to)], outs[w].at[k], send_sems.at[w, k], recv_sems.at[w, k], to))
    return cps


def _to_chips_start(ins, outs, send_sems, recv_sems):
    for cp in _to_chips_copies(ins, outs, send_sems, recv_sems):
        cp.start()


def _to_chips_finish(ins, outs, send_sems, recv_sems):
    for cp in _to_chips_copies(ins, outs, send_sems, recv_sems):
        cp.wait()


def _to_owners_shapes(ps):
    return [SDS((7, a.shape[1] // 2, a.shape[2]), a.dtype) for a in ps]


def _to_owners_sems(n):
    return [pltpu.SemaphoreType.DMA((n, 7)), pltpu.SemaphoreType.DMA((n, 7))]


def _to_owners_copies(ins, outs, send_sems, recv_sems):
    x, y, c = _coords()
    cps = []
    for w in range(len(ins)):
        rows = ins[w].shape[1]
        for k, kind in enumerate(ICI_KINDS):
            px, py, _ = _peer(kind, x, y, c)
            for h in range(2):
                cps.append(_remote(ins[w].at[2 * px + py, _half(rows, h)], outs[w].at[2 * k + c],
                                   send_sems.at[w, 2 * k + h], recv_sems.at[w, 2 * k + c], (px, py, h)))
        cps.append(_remote(ins[w].at[2 * x + y, _half(rows, 1 - c)], outs[w].at[6], send_sems.at[w, 6], recv_sems.at[w, 6],
                           _peer("c", x, y, c)))
    return cps


def _to_owners_start(ins, outs, send_sems, recv_sems):
    for cp in _to_owners_copies(ins, outs, send_sems, recv_sems):
        cp.start()


def _to_owners_finish(ins, outs, send_sems, recv_sems):
    for cp in _to_owners_copies(ins, outs, send_sems, recv_sems):
        cp.wait_send()
    for w in range(len(ins)):
        for slot in range(7):
            got = outs[w].at[slot]
            _remote(got, got, send_sems.at[w, slot], recv_sems.at[w, slot], _coords()).wait_recv()


EXCHANGES = {"to_chips": (_to_chips_shapes, _to_chips_sems, _to_chips_start, _to_chips_finish),
             "to_owners": (_to_owners_shapes, _to_owners_sems, _to_owners_start, _to_owners_finish)}


def halves_to_full(hs, by_cols, *, name):
    n = len(hs)

    def body(*refs):
        ins, outs = refs[:n], refs[n:2 * n]
        send_sems, recv_sems = refs[2 * n:]
        x, y, c = _coords()
        cps = []
        for w in range(n):
            cp = _remote(ins[w], _half_of(outs[w], by_cols[w], c), send_sems.at[w], recv_sems.at[w], _peer("c", x, y, c))
            cp.start()
            cps.append(cp)
        for cp in cps:
            cp.wait()

    return pl.pallas_call(
        body, in_specs=[ANY] * n, out_specs=[ANY] * n,
        out_shape=[SDS((a.shape[0], 2 * a.shape[1]) if bc else (2 * a.shape[0], a.shape[1]), a.dtype)
                   for a, bc in zip(hs, by_cols)],
        scratch_shapes=[pltpu.SemaphoreType.DMA((n,)), pltpu.SemaphoreType.DMA((n,))],
        name=name)(*hs)


def _row_tile(rows):
    for cand in (256, 192, 176, 128, 64, 32, 16):
        if rows % cand == 0:
            return cand
    return rows


def chip_sum(g, recv, c_arr, by_cols, *, name):
    _, r, cols = g.shape

    def body(c_ref, g_ref, r_ref, f_ref, b_ref):
        tot = g_ref[...] + r_ref[...]
        f_ref[...] = tot
        b_ref[...] = tot.astype(BF16)

    if by_cols:
        tc = 4 * LANES
        nblk = cols // 2 // tc
        shape = (4, r, cols // 2)
        blk = pl.BlockSpec((None, r, tc), lambda j, i, c_ref: (j, 0, i))
        mine = pl.BlockSpec((None, r, tc), lambda j, i, c_ref: (j, 0, c_ref[0] * nblk + i))
    else:
        tr = _row_tile(r // 2)
        nblk = r // 2 // tr
        shape = (4, r // 2, cols)
        blk = pl.BlockSpec((None, tr, cols), lambda j, i, c_ref: (j, i, 0))
        mine = pl.BlockSpec((None, tr, cols), lambda j, i, c_ref: (j, c_ref[0] * nblk + i, 0))
    grid_spec = pltpu.PrefetchScalarGridSpec(num_scalar_prefetch=1, grid=(4, nblk), in_specs=[mine, blk], out_specs=[blk, blk])
    return pl.pallas_call(body, grid_spec=grid_spec, out_shape=[SDS(shape, F32), SDS(shape, BF16)],
                          name=name, compiler_params=_cp(("parallel", "parallel")))(c_arr, g, recv)


def final_sum(pf, recv, chip_arr, *, name):
    _, h, cols = pf.shape
    tr = _row_tile(h)

    def body(chip_ref, p_ref, r_ref, o_ref):
        o_ref[...] = ((p_ref[...] + r_ref[0].astype(F32)) + r_ref[1].astype(F32)) + r_ref[2].astype(F32)

    grid_spec = pltpu.PrefetchScalarGridSpec(
        num_scalar_prefetch=1, grid=(h // tr,),
        in_specs=[pl.BlockSpec((None, tr, cols), lambda i, chip_ref: (chip_ref[0], i, 0)),
                  pl.BlockSpec((3, tr, cols), lambda i, chip_ref: (0, i, 0))],
        out_specs=pl.BlockSpec((tr, cols), lambda i, chip_ref: (i, 0)))
    return pl.pallas_call(body, grid_spec=grid_spec, out_shape=SDS((h, cols), F32), name=name,
                          compiler_params=_cp(("parallel",)))(chip_arr, pf, recv)


def owner_sum(g, recv, pos_arr, *, name):
    _, r, cols = g.shape
    h = r // 2
    tr = _row_tile(h)
    nblk = h // tr

    def body(pos_ref, g_ref, r_ref, o_ref):
        tot = g_ref[...]
        for slot in range(7):
            tot = tot + r_ref[slot].astype(F32)
        o_ref[...] = tot

    grid_spec = pltpu.PrefetchScalarGridSpec(
        num_scalar_prefetch=1, grid=(nblk,),
        in_specs=[pl.BlockSpec((None, tr, cols), lambda i, pos: (pos[0], pos[1] * nblk + i, 0)),
                  pl.BlockSpec((7, tr, cols), lambda i, pos: (0, i, 0))],
        out_specs=pl.BlockSpec((tr, cols), lambda i, pos: (i, 0)))
    return pl.pallas_call(body, grid_spec=grid_spec, out_shape=SDS((h, cols), F32), name=name,
                          compiler_params=_cp(("parallel",)))(pos_arr, g, recv)


def allreduce_small(v, *, name):
    rws, cols = v.shape

    def body(v_ref, all_ref, sum_ref, send_sems, recv_sems, local_sem):
        x, y, c = _coords()
        me, sibling = (x, y, c), (x, y, 1 - c)
        chips = [(1 - x, y), (x, 1 - y), (1 - x, 1 - y)]

        def rows(px, py, pc):
            return all_ref.at[pl.ds(pl.multiple_of((4 * px + 2 * py + pc) * rws, 8), rws), :]

        def copy(k, block, to, src=None):
            return _remote(rows(*block) if src is None else src, rows(*block), send_sems.at[k], recv_sems.at[k], to)

        mine = pltpu.make_async_copy(v_ref, rows(*me), local_sem)
        mine.start()
        first = [copy(0, me, sibling, src=v_ref)]
        first += [copy(1 + j, me, (*chip, c), src=v_ref) for j, chip in enumerate(chips)]
        for cp in first:
            cp.start()
        passed = [copy(4 + j, (*chip, c), sibling) for j, chip in enumerate(chips)]
        for j, chip in enumerate(chips):
            copy(1 + j, (*chip, c), me).wait_recv()
            passed[j].start()
        copy(0, sibling, me).wait_recv()
        for j, chip in enumerate(chips):
            copy(4 + j, (*chip, 1 - c), me).wait_recv()
        for cp in first + passed:
            cp.wait_send()
        mine.wait()
        tot = all_ref[0:rws, :]
        for dev in range(1, 8):
            tot = tot + all_ref[dev * rws:(dev + 1) * rws, :]
        sum_ref[...] = tot

    vm = pl.BlockSpec(memory_space=pltpu.VMEM)
    return pl.pallas_call(
        body, in_specs=[vm], out_specs=[vm, vm],
        out_shape=[SDS((8 * rws, cols), v.dtype), SDS((rws, cols), v.dtype)],
        scratch_shapes=[pltpu.SemaphoreType.DMA((7,)), pltpu.SemaphoreType.DMA((7,)), pltpu.SemaphoreType.DMA],
        name=name)(v)[1]


def _pack_rows(parts, rows):
    out = []
    for a, r in zip(parts, rows):
        flat = a.reshape(-1)
        flat = jnp.pad(flat, (0, r * LANES - flat.shape[0]))
        out.append(flat.reshape(r, LANES))
    return jnp.concatenate(out, axis=0)


def _unpack_rows(packed, shapes, rows):
    out, at = [], 0
    for shp, r in zip(shapes, rows):
        size = int(np.prod(shp))
        out.append(packed[at:at + r].reshape(-1)[:size].reshape(shp))
        at += r
    return out


def kernel(x, g_pre_mix, w_in, b_forget, w_o_fox, w_o_dil, w_out, g_post_mix, g_pre_ffn, w_up, conv_w, conv_b, w_down, g_post_ffn, loss_target, m_g_pre_mix, m_w_in, m_b_forget, m_w_o_fox, m_w_o_dil, m_w_out, m_g_post_mix, m_g_pre_ffn, m_w_up, m_conv_w, m_conv_b, m_w_down, m_g_post_ffn, v_g_pre_mix, v_w_in, v_b_forget, v_w_o_fox, v_w_o_dil, v_w_out, v_g_post_mix, v_g_pre_ffn, v_w_up, v_conv_w, v_conv_b, v_w_down, v_g_post_ffn):
    xi, yi, ci = _coords()
    chip = 2 * xi + yi
    c_arr = jnp.reshape(ci, (1,)).astype(jnp.int32)
    chip_arr = jnp.reshape(chip, (1,)).astype(jnp.int32)
    xs = x[0]
    target = loss_target[0]
    s, d = xs.shape
    f_half = w_down.shape[1] * 4
    cols_in = w_in.shape[2]

    big = (w_in, w_o_fox, w_o_dil, w_out, w_up, w_down)
    shards = [w[0].astype(BF16) for w in big]
    a_in = allgather_balanced(shards[0], name="allgather_w_in")
    w_in_full = jnp.concatenate([jnp.where(chip == j, shards[0], a_in[j]) for j in range(4)], axis=1)
    nf = N_HEADS
    e_a, e_b = 3 * ATT_W, 3 * ATT_W + nf
    wz = jnp.concatenate([w_in_full[:, :e_a], w_in_full[:, e_b:]], axis=1)
    wf = jnp.pad(w_in_full[:, e_a:e_b], ((0, 0), (0, LANES - nf)))
    cb = conv_b
    bfo = jnp.pad(b_forget, ((0, 0), (0, LANES - nf)))

    h1 = rmsnorm_fwd(xs, g_pre_mix)
    z = mm([(h1, d, 0)], [(wz, d, 0)], nt=False, out_dtype=BF16, tm=s, tn=512, name="in_proj")
    fa = mm([(h1, d, 0)], [(wf, d, 0)], nt=False, out_dtype=F32, tm=s, tn=LANES, name="in_proj_forget")
    q_aug, k_aug, v_aug = fox_prep(z, fa, bfo)
    later = shards[1:] + [conv_w[0]]
    ya, lse_a, *late = fox_fwd(q_aug, k_aug, v_aug, gather=later, halved=[True] * 5 + [False], hps=N_HEADS)
    a_of, a_od, a_out, a_up, a_down, a_cw = [
        lax.dynamic_update_index_in_dim(a4, own, chip, 0) for a4, own in zip(late, later)]
    cw = jnp.concatenate([a_cw[j] for j in range(4)], axis=1)
    wo_a = jnp.concatenate([a_of[j] for j in range(4)], axis=1)
    wo_b = jnp.concatenate([a_od[j] for j in range(4)], axis=1)
    w_o = a_out.reshape(d, d)
    w_dn = a_down.reshape(f_half, d)
    wu_a = jnp.concatenate([a_up[0], a_up[1]], axis=1)
    wu_b = jnp.concatenate([a_up[2], a_up[3]], axis=1)
    cos_t, sin_t = rope_cos_sin(s)
    yb, lse_b = dil_fwd_all(z, cos_t, sin_t)
    pa, pb, mixed = gate_mix(ya, yb, wo_a, wo_b, z)
    y1, x1, h2 = proj_norm_res(mixed, w_o, g_post_mix, xs, g_pre_ffn, tm=1024, name="out_proj")
    ua, ub, conv_a, conv_bh, mid = ffn_up(h2, wu_a, wu_b, cw, cb)
    dout, dy2, gg_post_ffn, sq = proj_norm_loss(mid, w_dn, g_post_ffn, x1, target, name="down_proj")
    loss = lax.psum(0.5 * sq[0, 0] / d, ("x", "y", "c"))

    dmid = mm([(dy2, d, 0)], [(w_dn, d, 0)], nt=True, out_dtype=BF16, tm=2048, tn=f_half // 2, name="down_dgrad")
    dw_down, dw_down16 = wgrad((mid, f_half, 0), dy2, tk=f_half // 2, tn=1024, ts=2048, name="down_wgrad", bf16_copy=True)
    dua, dub, gc_a, gc_b = ffn_bwd(dmid, ua, ub, conv_a, conv_bh, cw)
    dx1, dy1, gg_pre_ffn, gg_post_mix = mm_norm_bwd(
        [(dua, f_half, 0), (dub, f_half, 0)], [(wu_a, f_half, 0), (wu_b, f_half, 0)],
        [(x1, g_pre_ffn, dout, F32), (y1, g_post_mix, None, BF16)], name="up_dgrad")
    dw_up = None
    for k, du in enumerate((dua, dub)):
        dw_up = wgrad((h2, d, 0), du, tk=1024, tn=f_half // 2, ts=2048, name=f"up_wgrad_{k}", chip_major=True,
                      slabs=(4, 2 * k), into=dw_up, bf16_copy=True)
    g_ffn = [(dw_up[0], dw_up[1]), (dw_down.reshape(4, f_half // 4, d), dw_down16.reshape(4, f_half // 4, d))]
    dw_out, dw_out16 = wgrad((mixed, d, 0), dy1, tk=1024, tn=1024, ts=2048, name="out_wgrad", bf16_copy=True)
    dpa, dpb, dz_g, dya, dyb, dd_a = mix_bwd(dy1, w_o, z, pa, pb, wo_a, wo_b, ya)
    by_chip_cols = lambda a: jnp.stack([a[:, j * (d // 4):(j + 1) * (d // 4)] for j in range(4)], axis=0)
    dw_of = [by_chip_cols(a) for a in wgrad((ya, ATT_W, 0), dpa, tk=ATT_W, tn=d, ts=1024, name="fox_o_wgrad", bf16_copy=True)]
    dw_od = [by_chip_cols(a) for a in wgrad((yb, ATT_W, 0), dpb, tk=ATT_W, tn=d, ts=1024, name="dil_o_wgrad", bf16_copy=True)]
    g_mix = [dw_of, dw_od, (dw_out.reshape(4, d // 4, d), dw_out16.reshape(4, d // 4, d))]
    dq_aug, dk_aug, dv_a, *got_ffn = fox_bwd(q_aug, k_aug, z, dya, lse_a, dd_a, exchange=[g[1] for g in g_ffn], kind="to_owners")
    dz_a, dfa, gg_bf = fox_post(dq_aug, dk_aug, dv_a, fa, bfo)
    *dz_b, got_of, got_od, got_out = dil_bwd_all(z, cos_t, sin_t, dyb, lse_b, yb, exchange=[g[1] for g in g_mix],
                                                 kind="to_owners")
    got_mix = [got_of, got_od, got_out]
    dwt_a = wgrad((dz_a, e_a, 0), h1, tk=e_a // 2, tn=d, ts=2048, name="in_wgrad_a")
    dwt_b = [wgrad((part, ATT_W, 0), h1, tk=ATT_W, tn=d, ts=2048, name=f"in_wgrad_b{k}") for k, part in enumerate(dz_b)]
    dwt_g = wgrad((dz_g, 2 * d, 0), h1, tk=d, tn=d, ts=2048, name="in_wgrad_g")
    dwt_f = wgrad((dfa, LANES, 0), h1, tk=LANES, tn=d, ts=2048, name="in_wgrad_f")
    dwt_full = jnp.concatenate([dwt_a, dwt_f[:nf], *dwt_b, dwt_g], axis=0)
    dw_in = jnp.stack([dwt_full[j * cols_in:(j + 1) * cols_in] for j in range(4)], axis=0)
    from_sib = grads_to_sibling([dw_in], [True], name="grads_to_sibling_in")
    sum_in = chip_sum(dw_in, from_sib[0], c_arr, True, name="chip_sum_w_in")
    grad_x, gg_pre_mix, got_in = mm_norm_bwd(
        [(dz_a, e_a, 0), *[(part, ATT_W, 0) for part in dz_b], (dz_g, d, 0), (dz_g, d, 1), (dfa, LANES, 0)],
        [(wz, e_a, 0), *[(wz, ATT_W, Z_QB + k) for k in range(3)], (wz, d, 3), (wz, d, 4), (wf, LANES, 0)],
        [(xs, g_pre_mix, dx1, F32)], exchange=[sum_in[1]], name="in_dgrad")

    names = ("w_in", "w_o_fox", "w_o_dil", "w_out", "w_up", "w_down")
    pos_arr = jnp.concatenate([chip_arr, c_arr])
    halves = [final_sum(sum_in[0], got_in, chip_arr, name="final_sum_w_in")] + [
        owner_sum(g[0], got, pos_arr, name=f"owner_sum_{nm}") for g, got, nm in zip(g_mix + g_ffn, got_mix + got_ffn, names[1:])]
    from_half = halves_to_full(halves, [True] + [False] * 5, name="halves_to_full")
    g_big = [None] + [lax.dynamic_update_slice_in_dim(full, mine, ci * mine.shape[0], axis=0)
                      for full, mine in zip(from_half[1:], halves[1:])]
    upd_big = [adamw(w[0], g, m[0], v[0], name=f"adamw_{nm}") for w, g, m, v, nm in list(zip(
        big, g_big, (m_w_in, m_w_o_fox, m_w_o_dil, m_w_out, m_w_up, m_w_down),
        (v_w_in, v_w_o_fox, v_w_o_dil, v_w_out, v_w_up, v_w_down), names))[1:]]
    to_t = lambda a: jnp.transpose(a, (2, 0, 1))
    from_t = lambda a: jnp.transpose(a, (1, 2, 0))
    *upd_in, g_in_t = adamw_rows_view(to_t(w_in), halves[0], from_half[0], to_t(m_w_in), to_t(v_w_in), c_arr,
                                      name="adamw_w_in")

    g_cw_loc = jnp.concatenate([gc_a[0:3], gc_b[0:3]], axis=1)
    g_cb_loc = jnp.concatenate([gc_a[3:4], gc_b[3:4]], axis=1)
    small_loc = [gg_pre_mix, gg_post_mix, gg_pre_ffn, gg_post_ffn, g_cb_loc, gg_bf[:, :nf], g_cw_loc]
    red_rows = (8, 8, 8, 8, 48, 8, 136)
    red = allreduce_small(_pack_rows(small_loc, red_rows), name="allreduce_small")
    g_pm, g_qm, g_pf, g_qf, g_cb, g_bf, g_cw_full = _unpack_rows(red, [a.shape for a in small_loc], red_rows)
    cols_cw = conv_w.shape[2]
    g_cw = lax.dynamic_slice_in_dim(g_cw_full, chip * cols_cw, cols_cw, axis=1)
    small_w = (g_pre_mix, g_post_mix, g_pre_ffn, g_post_ffn, conv_b, b_forget, conv_w[0])
    small_m = (m_g_pre_mix, m_g_post_mix, m_g_pre_ffn, m_g_post_ffn, m_conv_b, m_b_forget, m_conv_w[0])
    small_v = (v_g_pre_mix, v_g_post_mix, v_g_pre_ffn, v_g_post_ffn, v_conv_b, v_b_forget, v_conv_w[0])
    small_g = (g_pm, g_qm, g_pf, g_qf, g_cb, g_bf, g_cw)
    small_names = ("g_pre_mix", "g_post_mix", "g_pre_ffn", "g_post_ffn", "conv_b", "b_forget", "conv_w")
    per_param = [adamw(w, g, m, v, name=f"adamw_{nm}") for w, g, m, v, nm in zip(small_w, small_g, small_m, small_v, small_names)]
    upd_small = [[u[j] for u in per_param] for j in range(3)]

    order = ("g_pre_mix", "w_in", "b_forget", "w_o_fox", "w_o_dil", "w_out", "g_post_mix", "g_pre_ffn", "w_up", "conv_w",
             "conv_b", "w_down", "g_post_ffn")
    grads, deltas, new_ms, new_vs = {}, {}, {}, {}
    grads["w_in"] = from_t(g_in_t)
    deltas["w_in"], new_ms["w_in"], new_vs["w_in"] = (from_t(a) for a in upd_in)
    for k, nm in enumerate(names[1:]):
        grads[nm] = g_big[k + 1][None]
        deltas[nm], new_ms[nm], new_vs[nm] = (a[None] for a in upd_big[k])
    for k, nm in enumerate(small_names):
        lead = (lambda a: a[None]) if nm == "conv_w" else (lambda a: a)
        grads[nm] = lead(small_g[k])
        deltas[nm], new_ms[nm], new_vs[nm] = (lead(upd_small[j][k]) for j in range(3))
    return (loss, grad_x[None], *[grads[nm] for nm in order], *[deltas[nm] for nm in order],
            *[new_ms[nm] for nm in order], *[new_vs[nm] for nm in order])
```

```python
import functools
import math

import numpy as np
import jax
import jax.numpy as jnp
from jax import lax
from jax.experimental import pallas as pl
from jax.experimental.pallas import tpu as pltpu

F32 = jnp.float32
BF16 = jnp.bfloat16
SDS = jax.ShapeDtypeStruct
MESH = pl.DeviceIdType.MESH

HEAD_DIM = 64
N_HEADS = 8
LANES = 128
ATT_W = N_HEADS * HEAD_DIM
DIL_PATTERNS = ((128, 1), (512, 4), (2048, 16))
DIL_BLK = 128
ROPE_DIM = HEAD_DIM // 4
ROPE_THETA = 500000.0
RMS_EPS = 1e-6
NEG = -1e30
QK_SCALE = 1.0 / math.sqrt(HEAD_DIM)
ADAM_LR, ADAM_B1, ADAM_B2, ADAM_EPS, ADAM_WD, ADAM_STEP = 0.001, 0.9, 0.999, 1e-08, 0.01, 10
VMEM_LIMIT = 56 * 1024 * 1024

Z_QA, Z_KA, Z_VA, Z_QB, Z_KB, Z_VB = 0, 1, 2, 3, 4, 5
Z_W = 5120


def _cp(sem):
    return pltpu.CompilerParams(dimension_semantics=sem, vmem_limit_bytes=VMEM_LIMIT)


def _nt(a, b):
    return lax.dot_general(a, b, (((1,), (1,)), ((), ())), preferred_element_type=F32)


def _tn(a, b):
    return lax.dot_general(a, b, (((0,), (0,)), ((), ())), preferred_element_type=F32)


def _nn(a, b):
    return jnp.dot(a, b, preferred_element_type=F32)


def _lane(shape):
    return lax.broadcasted_iota(jnp.int32, shape, 1)


def _row(shape):
    return lax.broadcasted_iota(jnp.int32, shape, 0)


def rmsnorm_fwd(x, g, *, tm=1024):
    s, d = x.shape

    def body(x_ref, g_ref, h_ref):
        xv = x_ref[...]
        inv = lax.rsqrt(jnp.mean(xv * xv, axis=-1, keepdims=True) + RMS_EPS)
        h_ref[...] = (xv * inv * g_ref[...]).astype(h_ref.dtype)

    return pl.pallas_call(
        body, grid=(s // tm,),
        in_specs=[pl.BlockSpec((tm, d), lambda i: (i, 0)), pl.BlockSpec((1, d), lambda i: (0, 0))],
        out_specs=pl.BlockSpec((tm, d), lambda i: (i, 0)),
        out_shape=SDS((s, d), BF16), name="rmsnorm_fwd", compiler_params=_cp(("parallel",)))(x, g)


def mm(a_views, b_views, *, nt, out_dtype, tm, tn, name):
    n_p = len(a_views)
    m = a_views[0][0].shape[0]
    n = b_views[0][0].shape[0] if nt else b_views[0][0].shape[1]

    def body(*refs):
        o_ref = refs[-1]
        acc = None
        for p in range(n_p):
            av = refs[p][...].astype(BF16)
            bv = refs[n_p + p][...].astype(BF16)
            dv = _nt(av, bv) if nt else _nn(av, bv)
            acc = dv if acc is None else acc + dv
        o_ref[...] = acc.astype(o_ref.dtype)

    in_specs = []
    for arr, w, blk in a_views:
        in_specs.append(pl.BlockSpec((tm, w), functools.partial(lambda i, j, blk: (i, blk), blk=blk)))
    for arr, w, blk in b_views:
        if nt:
            in_specs.append(pl.BlockSpec((tn, w), functools.partial(lambda i, j, blk: (j, blk), blk=blk)))
        else:
            in_specs.append(pl.BlockSpec((w, tn), lambda i, j: (0, j)))
    return pl.pallas_call(
        body, grid=(m // tm, n // tn), in_specs=in_specs,
        out_specs=pl.BlockSpec((tm, tn), lambda i, j: (i, j)),
        out_shape=SDS((m, n), out_dtype), name=name,
        compiler_params=_cp(("parallel", "parallel")))(*[a[0] for a in a_views], *[b[0] for b in b_views])


def wgrad(a_view, g, *, tk, tn, ts, name, chip_major=False, slabs=None, into=None, bf16_copy=False):
    arr, ka, blk = a_view
    s, n = g.shape
    ns = s // ts
    total, first = slabs if slabs else (n // tn, 0)
    n_into = 0 if into is None else (2 if bf16_copy else 1)

    def body(a_ref, g_ref, *rest):
        o_ref = rest[n_into]

        @pl.when(pl.program_id(2) == 0)
        def _():
            o_ref[...] = jnp.zeros_like(o_ref)

        o_ref[...] += _tn(a_ref[...].astype(BF16), g_ref[...].astype(BF16))
        if bf16_copy:
            @pl.when(pl.program_id(2) == ns - 1)
            def _():
                rest[n_into + 1][...] = o_ref[...].astype(BF16)

    if chip_major:
        out_spec = pl.BlockSpec((None, tk, tn), lambda i, j, k: (first + j, i, 0))
        shape = (total, ka, tn)
    else:
        out_spec = pl.BlockSpec((tk, tn), lambda i, j, k: (i, j))
        shape = (ka, n)
    in_specs = [pl.BlockSpec((ts, tk), lambda i, j, k: (k, blk * (ka // tk) + i)),
                pl.BlockSpec((ts, tn), lambda i, j, k: (k, j))]
    args = [arr, g]
    if into is not None:
        earlier = list(into) if bf16_copy else [into]
        in_specs += [pl.BlockSpec(memory_space=pl.ANY)] * len(earlier)
        args += earlier
    out = pl.pallas_call(
        body, grid=(ka // tk, n // tn, ns), in_specs=in_specs,
        out_specs=[out_spec, out_spec] if bf16_copy else out_spec,
        out_shape=[SDS(shape, F32), SDS(shape, BF16)] if bf16_copy else SDS(shape, F32), name=name,
        input_output_aliases={2 + k: k for k in range(n_into)},
        compiler_params=_cp(("parallel", "parallel", "arbitrary")))(*args)
    return out


def _norm_bwd_rows(dh, xh, inv, g):
    dxh = dh * g
    dx = inv * (dxh - xh * jnp.mean(dxh * xh, axis=-1, keepdims=True))
    return dx, jnp.sum((dh * xh).reshape(dh.shape[0] // 8, 8, dh.shape[1]), axis=0)


def proj_norm_res(a, w, g, xres, g_next, *, tm=512, name):
    s, k = a.shape
    d = w.shape[1]

    def body(a_ref, w_ref, g_ref, x_ref, gn_ref, y_ref, o_ref, h_ref):
        y = _nn(a_ref[...], w_ref[...])
        inv = lax.rsqrt(jnp.mean(y * y, axis=-1, keepdims=True) + RMS_EPS)
        xn = x_ref[...] + y * inv * g_ref[...]
        y_ref[...] = y
        o_ref[...] = xn
        inv_n = lax.rsqrt(jnp.mean(xn * xn, axis=-1, keepdims=True) + RMS_EPS)
        h_ref[...] = (xn * inv_n * gn_ref[...]).astype(h_ref.dtype)

    row = pl.BlockSpec((tm, d), lambda i: (i, 0))
    vec = pl.BlockSpec((1, d), lambda i: (0, 0))
    return pl.pallas_call(
        body, grid=(s // tm,),
        in_specs=[pl.BlockSpec((tm, k), lambda i: (i, 0)), pl.BlockSpec((k, d), lambda i: (0, 0)), vec, row, vec],
        out_specs=[row, row, row], out_shape=[SDS((s, d), F32), SDS((s, d), F32), SDS((s, d), BF16)], name=name,
        compiler_params=_cp(("parallel",)))(a, w, g, xres, g_next)


def proj_norm_loss(a, w, g, xres, target, *, tm=512, name):
    s, k = a.shape
    d = w.shape[1]
    n = s // tm

    def body(a_ref, w_ref, g_ref, x_ref, t_ref, do_ref, dy_ref, dg_ref, l_ref, acc):
        i = pl.program_id(0)

        @pl.when(i == 0)
        def _():
            acc[...] = jnp.zeros_like(acc)
            l_ref[...] = jnp.zeros_like(l_ref)

        y = _nn(a_ref[...], w_ref[...])
        inv = lax.rsqrt(jnp.mean(y * y, axis=-1, keepdims=True) + RMS_EPS)
        yh = y * inv
        err = x_ref[...] + yh * g_ref[...] - t_ref[...]
        dout = err * (1.0 / d)
        do_ref[...] = dout
        l_ref[...] += jnp.sum(jnp.sum(err * err, axis=1, keepdims=True), axis=0, keepdims=True)
        dy, part = _norm_bwd_rows(dout, yh, inv, g_ref[...])
        dy_ref[...] = dy.astype(dy_ref.dtype)
        acc[...] += part

        @pl.when(i == n - 1)
        def _():
            dg_ref[...] = jnp.sum(acc[...], axis=0, keepdims=True)

    row = pl.BlockSpec((tm, d), lambda i: (i, 0))
    vec = pl.BlockSpec((1, d), lambda i: (0, 0))
    return pl.pallas_call(
        body, grid=(n,),
        in_specs=[pl.BlockSpec((tm, k), lambda i: (i, 0)), pl.BlockSpec((k, d), lambda i: (0, 0)), vec, row, row],
        out_specs=[row, row, vec, pl.BlockSpec((1, 1), lambda i: (0, 0))],
        out_shape=[SDS((s, d), F32), SDS((s, d), BF16), SDS((1, d), F32), SDS((1, 1), F32)],
        scratch_shapes=[pltpu.VMEM((8, d), F32)], name=name, compiler_params=_cp(("arbitrary",)))(a, w, g, xres, target)


def mm_norm_bwd(a_views, b_views, stages, exchange=(), *, tm=256, name):
    n_p, n_s, ne = len(a_views), len(stages), len(exchange)
    s = a_views[0][0].shape[0]
    d = b_views[0][0].shape[0]
    n = s // tm
    has_res = [st[2] is not None for st in stages]

    def body(*refs):
        a_refs, b_refs = refs[:n_p], refs[n_p:2 * n_p]
        at = 2 * n_p
        st_refs = []
        for k in range(n_s):
            cnt = 3 if has_res[k] else 2
            st_refs.append(refs[at:at + cnt])
            at += cnt
        e_ins = refs[at:at + ne]
        at += ne
        dx_refs, dg_refs = refs[at:at + n_s], refs[at + n_s:at + 2 * n_s]
        at += 2 * n_s
        e_outs = refs[at:at + ne]
        at += ne
        accs = refs[at:at + n_s]
        comm = (e_ins, e_outs) + tuple(refs[at + n_s:])
        i = pl.program_id(0)

        @pl.when(i == 0)
        def _():
            for acc in accs:
                acc[...] = jnp.zeros_like(acc)
            if ne:
                _to_chips_start(*comm)

        dh = None
        for p in range(n_p):
            part = _nt(a_refs[p][...].astype(BF16), b_refs[p][...].astype(BF16))
            dh = part if dh is None else dh + part
        for k in range(n_s):
            xv = st_refs[k][0][...]
            inv = lax.rsqrt(jnp.mean(xv * xv, axis=-1, keepdims=True) + RMS_EPS)
            dx, part = _norm_bwd_rows(dh, xv * inv, inv, st_refs[k][1][...])
            if has_res[k]:
                dx = dx + st_refs[k][2][...]
            dx_refs[k][...] = dx.astype(dx_refs[k].dtype)
            accs[k][...] += part
            dh = dx

        @pl.when(i == n - 1)
        def _():
            for k in range(n_s):
                dg_refs[k][...] = jnp.sum(accs[k][...], axis=0, keepdims=True)
            if ne:
                _to_chips_finish(*comm)

    row = pl.BlockSpec((tm, d), lambda i: (i, 0))
    vec = pl.BlockSpec((1, d), lambda i: (0, 0))
    in_specs, args = [], []
    for arr, w, blk in a_views:
        in_specs.append(pl.BlockSpec((tm, w), functools.partial(lambda i, blk: (i, blk), blk=blk)))
        args.append(arr)
    for arr, w, blk in b_views:
        in_specs.append(pl.BlockSpec((d, w), functools.partial(lambda i, blk: (0, blk), blk=blk)))
        args.append(arr)
    for x, g, res, _ in stages:
        in_specs += [row, vec] + ([row] if res is not None else [])
        args += [x, g] + ([res] if res is not None else [])
    return pl.pallas_call(
        body, grid=(n,), in_specs=in_specs + [ANY] * ne,
        out_specs=[row] * n_s + [vec] * n_s + [ANY] * ne,
        out_shape=[SDS((s, d), st[3]) for st in stages] + [SDS((1, d), F32)] * n_s + _to_chips_shapes(exchange),
        scratch_shapes=[pltpu.VMEM((8, d), F32)] * n_s + (_to_chips_sems(ne) if ne else []), name=name,
        compiler_params=_cp(("arbitrary",)))(*args, *exchange)


def _split3(v):
    hi = v.astype(BF16).astype(F32)
    r = v - hi
    mid = r.astype(BF16).astype(F32)
    lo = (r - mid).astype(BF16).astype(F32)
    return hi, mid, lo


def _tri(n, upper):
    r = np.arange(n)
    m = (r[:, None] <= r[None, :]) if upper else (r[:, None] >= r[None, :])
    return jnp.asarray(m.astype(np.float32))


def fox_prep(z, fa, bfo, *, tb=512):
    s = z.shape[0]
    n = s // tb

    def body(q_ref, k_ref, v_ref, fa_ref, b_ref, tri_ref, qa_ref, ka_ref, va_ref, carry):
        @pl.when(pl.program_id(0) == 0)
        def _():
            carry[...] = jnp.zeros_like(carry)

        xv = fa_ref[...] + b_ref[...]
        logf = jnp.minimum(xv, 0.0) - jnp.log(1.0 + jnp.exp(-jnp.abs(xv)))
        csum = jnp.dot(tri_ref[...], logf, preferred_element_type=F32, precision=lax.Precision.HIGHEST) + carry[0:1, :]
        carry[0:1, :] = csum[tb - 1:tb, :]
        lane = _lane((tb, LANES))
        for h in range(N_HEADS):
            hi, mid, lo = _split3(csum[:, h:h + 1])
            pair = (h // 2) * LANES
            qv = q_ref[:, pair:pair + LANES].astype(F32)
            kv = k_ref[:, pair:pair + LANES].astype(F32)
            vv = v_ref[:, pair:pair + LANES].astype(F32)
            if h % 2:
                qv = pltpu.roll(qv, 64, axis=1)
                kv = pltpu.roll(kv, 64, axis=1)
                vv = pltpu.roll(vv, 64, axis=1)
            va_ref[:, h * LANES:(h + 1) * LANES] = jnp.where(lane < 64, vv, jnp.where(lane == 64, 1.0, 0.0)).astype(BF16)
            one = jnp.where((lane >= 67) & (lane < 70), 1.0, 0.0)
            q_x = jnp.where(lane == 64, hi, jnp.where(lane == 65, mid, jnp.where(lane == 66, lo, one)))
            one = jnp.where((lane >= 64) & (lane < 67), 1.0, 0.0)
            k_x = jnp.where(lane == 67, -hi, jnp.where(lane == 68, -mid, jnp.where(lane == 69, -lo, one)))
            qa_ref[:, h * LANES:(h + 1) * LANES] = jnp.where(lane < 64, qv * QK_SCALE, q_x).astype(BF16)
            ka_ref[:, h * LANES:(h + 1) * LANES] = jnp.where(lane < 64, kv, k_x).astype(BF16)

    return pl.pallas_call(
        body, grid=(n,),
        in_specs=[pl.BlockSpec((tb, ATT_W), lambda i: (i, Z_QA)), pl.BlockSpec((tb, ATT_W), lambda i: (i, Z_KA)),
                  pl.BlockSpec((tb, ATT_W), lambda i: (i, Z_VA)),
                  pl.BlockSpec((tb, LANES), lambda i: (i, 0)), pl.BlockSpec((1, LANES), lambda i: (0, 0)),
                  pl.BlockSpec((tb, tb), lambda i: (0, 0))],
        out_specs=[pl.BlockSpec((tb, N_HEADS * LANES), lambda i: (i, 0))] * 3,
        out_shape=[SDS((s, N_HEADS * LANES), BF16)] * 3,
        scratch_shapes=[pltpu.VMEM((8, LANES), F32)],
        name="fox_prep", compiler_params=_cp(("arbitrary",)))(z, z, z, fa, bfo, _tri(tb, False))


def _causal_pairs(n, k_major):
    if k_major:
        pairs = [(qi, kj) for kj in range(n) for qi in range(kj, n)]
    else:
        pairs = [(qi, kj) for qi in range(n) for kj in range(qi + 1)]
    return (jnp.asarray([p[0] for p in pairs], jnp.int32), jnp.asarray([p[1] for p in pairs], jnp.int32), len(pairs))


def fox_fwd(q_aug, k_aug, v_aug, gather=(), halved=(), *, t=256, hps=4):
    s = v_aug.shape[0]
    qi_arr, kj_arr, n_pairs = _causal_pairs(s // t, False)
    ng = len(gather)
    n_groups = N_HEADS // hps

    def body(qi_ref, kj_ref, q_ref, k_ref, v_ref, *rest):
        g_ins, (o_ref, lse_ref), g_outs = rest[:ng], rest[ng:ng + 2], rest[ng + 2:2 * ng + 2]
        m_scr, acc_scr = rest[2 * ng + 2:2 * ng + 4]
        comm = (g_ins, g_outs) + tuple(rest[2 * ng + 4:]) + (list(halved),)
        step = pl.program_id(1)
        qi = qi_ref[step]
        kj = kj_ref[step]
        if ng:
            @pl.when((pl.program_id(0) == 0) & (step == 0))
            def _():
                _allgather_start(*comm)

        @pl.when(kj == 0)
        def _():
            m_scr[...] = jnp.full_like(m_scr, NEG)
            acc_scr[...] = jnp.zeros_like(acc_scr)

        def update(masked):
            for i in range(hps):
                sc = _nt(q_ref[:, i * LANES:(i + 1) * LANES], k_ref[:, i * LANES:(i + 1) * LANES])
                if masked:
                    sc = jnp.where(_row((t, t)) >= _lane((t, t)), sc, NEG)
                m_prev = m_scr[i]
                m_new = jnp.maximum(m_prev, jnp.max(sc, axis=-1, keepdims=True))
                p = jnp.exp((sc - jnp.tile(m_new, (1, t // LANES))).astype(BF16))
                acc_scr[i] = jnp.exp(m_prev - m_new) * acc_scr[i] + _nn(p, v_ref[:, i * LANES:(i + 1) * LANES])
                m_scr[i] = m_new

        @pl.when(kj < qi)
        def _():
            update(False)

        @pl.when(kj == qi)
        def _():
            update(True)
            lane = _lane((t, LANES))
            for pr in range(hps // 2):
                den = [acc_scr[2 * pr + i][:, 64:65] for i in range(2)]
                o_ref[:, pr * LANES:(pr + 1) * LANES] = jnp.where(
                    lane < 64, acc_scr[2 * pr] / den[0], pltpu.roll(acc_scr[2 * pr + 1] / den[1], 64, axis=1)).astype(o_ref.dtype)
                lse_ref[:, pr * LANES:(pr + 1) * LANES] = jnp.where(
                    lane < 64, m_scr[2 * pr] + jnp.log(den[0]), m_scr[2 * pr + 1] + jnp.log(den[1]))

        if ng:
            @pl.when((pl.program_id(0) == n_groups - 1) & (step == n_pairs - 1))
            def _():
                _allgather_finish(*comm)

    wide = hps * LANES
    grid_spec = pltpu.PrefetchScalarGridSpec(
        num_scalar_prefetch=2, grid=(n_groups, n_pairs),
        in_specs=[pl.BlockSpec((t, wide), lambda hg, st, qi, kj: (qi[st], hg)),
                  pl.BlockSpec((t, wide), lambda hg, st, qi, kj: (kj[st], hg)),
                  pl.BlockSpec((t, wide), lambda hg, st, qi, kj: (kj[st], hg))] + [ANY] * ng,
        out_specs=[pl.BlockSpec((t, wide // 2), lambda hg, st, qi, kj: (qi[st], hg))] * 2 + [ANY] * ng,
        scratch_shapes=[pltpu.VMEM((hps, t, LANES), F32)] * 2 + (_allgather_sems(ng) if ng else []))
    return pl.pallas_call(
        body, grid_spec=grid_spec, out_shape=[SDS((s, ATT_W), BF16), SDS((s, ATT_W), F32)] + _allgather_shapes(gather),
        name="fox_fwd", compiler_params=_cp(("arbitrary", "arbitrary")))(qi_arr, kj_arr, q_aug, k_aug, v_aug, *gather)


def fox_bwd(q_aug, k_aug, z, dy, lse, dd, exchange=(), kind="to_chips", *, t=256, hps=4):
    s = z.shape[0]
    qi_arr, kj_arr, n_pairs = _causal_pairs(s // t, True)
    ne = len(exchange)
    n_groups = N_HEADS // hps
    x_shapes, x_sems, x_start, x_finish = EXCHANGES[kind]

    def body(qi_ref, kj_ref, q_ref, k_ref, v_ref, do_ref, lse_ref, dd_ref, *rest):
        e_ins, (dq_ref, dk_ref, dv_ref), e_outs = rest[:ne], rest[ne:ne + 3], rest[ne + 3:2 * ne + 3]
        comm = (e_ins, e_outs) + tuple(rest[2 * ne + 3:])
        step = pl.program_id(1)
        qi = qi_ref[step]
        kj = kj_ref[step]
        if ne:
            @pl.when((pl.program_id(0) == 0) & (step == 0))
            def _():
                x_start(*comm)

        @pl.when(step == 0)
        def _():
            dq_ref[...] = jnp.zeros_like(dq_ref)

        @pl.when(qi == kj)
        def _():
            dk_ref[...] = jnp.zeros_like(dk_ref)
            dv_ref[...] = jnp.zeros_like(dv_ref)

        def update(masked):
            lane = _lane((t, LANES))
            rows = pl.ds(pl.multiple_of(qi * t, t), t)
            for pr in range(hps // 2):
                pair = slice(pr * LANES, (pr + 1) * LANES)
                dov = do_ref[:, pair]
                dv_new = None
                for i in range(2):
                    head = (lane < 64) if i == 0 else (lane >= 64)
                    own = slice((2 * pr + i) * LANES, (2 * pr + i + 1) * LANES)
                    col = slice(pr * LANES + i * 64, pr * LANES + i * 64 + 1)
                    qv = q_ref[:, own]
                    kv = k_ref[:, own]
                    sc = _nt(qv, kv)
                    if masked:
                        sc = jnp.where(_row((t, t)) >= _lane((t, t)), sc, NEG)
                    p = jnp.exp(sc - lse_ref[:, col])
                    dp = _nt(jnp.where(head, dov, jnp.zeros_like(dov)), v_ref[:, pair])
                    ds = (p * (dp - dd_ref[:, col])).astype(BF16)
                    dq_ref[rows, own] += _nn(ds, kv)
                    dk_ref[:, own] += _tn(ds, qv)
                    dvi = _tn(p.astype(BF16), dov)
                    dv_new = dvi if dv_new is None else jnp.where(head, dvi, dv_new)
                dv_ref[:, pair] += dv_new

        @pl.when(kj < qi)
        def _():
            update(False)

        @pl.when(kj == qi)
        def _():
            update(True)

        if ne:
            @pl.when((pl.program_id(0) == n_groups - 1) & (step == n_pairs - 1))
            def _():
                x_finish(*comm)

    wide, half = hps * LANES, hps // 2 * LANES
    v_blk = Z_VA * ATT_W // half
    grid_spec = pltpu.PrefetchScalarGridSpec(
        num_scalar_prefetch=2, grid=(n_groups, n_pairs),
        in_specs=[pl.BlockSpec((t, wide), lambda hg, st, qi, kj: (qi[st], hg)),
                  pl.BlockSpec((t, wide), lambda hg, st, qi, kj: (kj[st], hg)),
                  pl.BlockSpec((t, half), lambda hg, st, qi, kj: (kj[st], v_blk + hg)),
                  pl.BlockSpec((t, half), lambda hg, st, qi, kj: (qi[st], hg)),
                  pl.BlockSpec((t, half), lambda hg, st, qi, kj: (qi[st], hg)),
                  pl.BlockSpec((t, half), lambda hg, st, qi, kj: (qi[st], hg))] + [ANY] * ne,
        out_specs=[pl.BlockSpec((s, wide), lambda hg, st, qi, kj: (0, hg)),
                   pl.BlockSpec((t, wide), lambda hg, st, qi, kj: (kj[st], hg)),
                   pl.BlockSpec((t, half), lambda hg, st, qi, kj: (kj[st], hg))] + [ANY] * ne,
        scratch_shapes=x_sems(ne) if ne else [])
    return pl.pallas_call(
        body, grid_spec=grid_spec,
        out_shape=[SDS((s, N_HEADS * LANES), F32), SDS((s, N_HEADS * LANES), F32), SDS((s, ATT_W), F32)]
        + x_shapes(exchange),
        name="fox_bwd", compiler_params=_cp(("arbitrary", "arbitrary")))(qi_arr, kj_arr, q_aug, k_aug, z, dy, lse, dd, *exchange)


def fox_post(dq_aug, dk_aug, dv, fa, bfo, *, tb=512):
    s = dv.shape[0]
    n = s // tb

    def body(dq_ref, dk_ref, dv_ref, fa_ref, b_ref, tri_ref, dz_ref, dfa_ref, gb_ref, carry, acc):
        i = pl.program_id(0)

        @pl.when(i == 0)
        def _():
            carry[...] = jnp.zeros_like(carry)
            acc[...] = jnp.zeros_like(acc)

        lane = _lane((tb, LANES))
        d_f = jnp.zeros((tb, LANES), F32)
        for h in range(N_HEADS):
            col = dq_ref[:, h * LANES + 64:h * LANES + 65] - dk_ref[:, h * LANES + 67:h * LANES + 68]
            d_f = jnp.where(lane == h, col, d_f)
        suffix = jnp.dot(tri_ref[...], d_f, preferred_element_type=F32, precision=lax.Precision.HIGHEST) + carry[0:1, :]
        carry[0:1, :] = suffix[0:1, :]
        xv = fa_ref[...] + b_ref[...]
        dx = suffix * (1.0 / (1.0 + jnp.exp(xv)))
        dfa_ref[...] = dx.astype(dfa_ref.dtype)
        acc[...] += jnp.sum(dx.reshape(tb // 8, 8, LANES), axis=0)
        for hp in range(4):
            for src, off, scale in ((dq_ref, 0, QK_SCALE), (dk_ref, ATT_W, 1.0)):
                even = src[:, (2 * hp) * LANES:(2 * hp + 1) * LANES]
                odd = pltpu.roll(src[:, (2 * hp + 1) * LANES:(2 * hp + 2) * LANES], 64, axis=1)
                dz_ref[:, off + hp * LANES:off + (hp + 1) * LANES] = (jnp.where(lane < 64, even, odd) * scale).astype(BF16)
        dz_ref[:, 2 * ATT_W:3 * ATT_W] = dv_ref[...].astype(BF16)

        @pl.when(i == n - 1)
        def _():
            gb_ref[...] = jnp.sum(acc[...], axis=0, keepdims=True)

    rev = lambda i: (n - 1 - i, 0)
    return pl.pallas_call(
        body, grid=(n,),
        in_specs=[pl.BlockSpec((tb, N_HEADS * LANES), rev), pl.BlockSpec((tb, N_HEADS * LANES), rev),
                  pl.BlockSpec((tb, ATT_W), rev), pl.BlockSpec((tb, LANES), rev),
                  pl.BlockSpec((1, LANES), lambda i: (0, 0)), pl.BlockSpec((tb, tb), lambda i: (0, 0))],
        out_specs=[pl.BlockSpec((tb, 3 * ATT_W), rev), pl.BlockSpec((tb, LANES), rev),
                   pl.BlockSpec((1, LANES), lambda i: (0, 0))],
        out_shape=[SDS((s, 3 * ATT_W), BF16), SDS((s, LANES), BF16), SDS((1, LANES), F32)],
        scratch_shapes=[pltpu.VMEM((8, LANES), F32), pltpu.VMEM((8, LANES), F32)],
        name="fox_post", compiler_params=_cp(("arbitrary",)))(dq_aug, dk_aug, dv, fa, bfo, _tri(tb, True))


def rope_cos_sin(s):
    half = ROPE_DIM // 2
    inv_freq = ROPE_THETA ** (-jnp.arange(half, dtype=F32) * 2.0 / ROPE_DIM)
    ang = jnp.arange(s, dtype=F32)[:, None] * inv_freq[None, :]
    return jnp.tile(jnp.cos(ang), (1, LANES // half)), jnp.tile(jnp.sin(ang), (1, LANES // half))


def _rotate(x, cos, sin, sign):
    l64 = _lane(x.shape) & (HEAD_DIM - 1)
    first = l64 < ROPE_DIM // 2
    second = (l64 >= ROPE_DIM // 2) & (l64 < ROPE_DIM)
    from_next = jnp.where(first, -sign * sin, 0.0)
    from_prev = jnp.where(second, sign * sin, 0.0)
    return (x * jnp.where(first | second, cos, 1.0) + pltpu.roll(x, LANES - 8, axis=1) * from_next
            + pltpu.roll(x, 8, axis=1) * from_prev)


def _dil_rows(base, r):
    if r == 1:
        return pl.ds(pl.multiple_of(base, DIL_BLK), DIL_BLK)
    return pl.ds(base, DIL_BLK, stride=r)


def _dil_block(idx, r, nb):
    shift = nb.bit_length() - 1
    rho = idx >> shift
    n = idx & (nb - 1)
    base = rho + n * (r * DIL_BLK)
    return _dil_rows(base, r), _dil_rows(jnp.maximum(base - r * DIL_BLK, rho), r), n > 0


def _cat(a, b):
    return jnp.concatenate([a, b], axis=0)


def _two_heads(v, first_head):
    zero = jnp.zeros_like(v)
    return _cat(jnp.where(first_head, v, zero), jnp.where(first_head, zero, v))


def _dil_bands():
    b = DIL_BLK
    q = _row((2 * b, 2 * b)) & (b - 1)
    col = _lane((2 * b, 2 * b))
    return (col < b) & (col >= q), (col >= b) & (col - b <= q)


def _dil_load_qkv(zq_ref, zk_ref, zv_ref, cos_ref, sin_ref, q_ref, k_ref, v_ref, *, chunk=512):
    def step(i, carry):
        rows = pl.ds(pl.multiple_of(i * chunk, chunk), chunk)
        cos, sin = cos_ref[rows, :], sin_ref[rows, :]
        q_ref[rows, :] = _rotate(zq_ref[rows, :].astype(F32), cos, sin, 1.0) * QK_SCALE
        k_ref[rows, :] = _rotate(zk_ref[rows, :].astype(F32), cos, sin, 1.0)
        v_ref[rows, :] = zv_ref[rows, :].astype(F32)
        return carry

    lax.fori_loop(0, q_ref.shape[0] // chunk, step, 0)


def dil_fwd_all(z, cos_t, sin_t, *, unroll=32):
    s = z.shape[0]
    b = DIL_BLK
    n_blk = s // b

    def body(zq_ref, zk_ref, zv_ref, cos_ref, sin_ref, o_ref, l_ref, q_ref, k_ref, v_ref):
        _dil_load_qkv(zq_ref, zk_ref, zv_ref, cos_ref, sin_ref, q_ref, k_ref, v_ref)
        first_head = _lane((b, LANES)) < 64
        band_prev, band_cur = _dil_bands()
        for g, (_, r) in enumerate(DIL_PATTERNS):
            nb = n_blk // r

            def group(it, carry, g=g, r=r, nb=nb):
                loaded = []
                kc = vc = None
                for u in range(unroll):
                    rows_c, rows_p, has_prev = _dil_block(it * unroll + u, r, nb)
                    if u % min(nb, unroll):
                        kp, vp = kc, vc
                    else:
                        kp, vp = k_ref[rows_p, :].astype(BF16), v_ref[rows_p, :].astype(BF16)
                    kc, vc = k_ref[rows_c, :].astype(BF16), v_ref[rows_c, :].astype(BF16)
                    state = (o_ref[rows_c, :], l_ref[rows_c, :]) if g else None
                    loaded.append((rows_c, has_prev, [q_ref[rows_c, :].astype(BF16), kp, kc, vp, vc], state))
                done = []
                for rows_c, has_prev, (qv, kp, kc, vp, vc), state in loaded:
                    sc = jnp.where(band_cur | (band_prev & has_prev), _nt(_two_heads(qv, first_head), _cat(kp, kc)), NEG)
                    m = jnp.max(sc, axis=-1, keepdims=True)
                    p = jnp.exp(sc - m)
                    den = jnp.sum(p, axis=-1, keepdims=True)
                    both = _nn(p.astype(BF16), _cat(vp, vc)) / den
                    lse2 = m + jnp.log(den)
                    ov = jnp.where(first_head, both[:b], both[b:])
                    lse = jnp.where(first_head, lse2[:b], lse2[b:])
                    if state is not None:
                        m2 = jnp.maximum(state[1], lse)
                        wp = jnp.exp(state[1] - m2)
                        wn = jnp.exp(lse - m2)
                        ov = (wp * state[0] + wn * ov) / (wp + wn)
                        lse = m2 + jnp.log(wp + wn)
                    done.append((rows_c, ov, lse))
                for rows_c, ov, lse in done:
                    o_ref[rows_c, :] = ov
                    l_ref[rows_c, :] = lse
                return carry

            lax.fori_loop(0, n_blk // unroll, group, 0)

    col_blk = lambda k: pl.BlockSpec((s, LANES), lambda hp: (0, 4 * k + hp))
    table = pl.BlockSpec((s, LANES), lambda hp: (0, 0))
    out = pl.BlockSpec((s, LANES), lambda hp: (0, hp))
    return pl.pallas_call(
        body, grid=(4,), in_specs=[col_blk(Z_QB), col_blk(Z_KB), col_blk(Z_VB), table, table], out_specs=[out, out],
        out_shape=[SDS((s, ATT_W), F32)] * 2, scratch_shapes=[pltpu.VMEM((s, LANES), F32)] * 3, name="dil_fwd",
        compiler_params=_cp(("parallel",)))(z, z, z, cos_t, sin_t)


def dil_bwd_all(z, cos_t, sin_t, dy, lse, y, exchange=(), kind="to_chips", *, unroll=16):
    s = z.shape[0]
    b = DIL_BLK
    n_blk = s // b
    ne = len(exchange)
    x_shapes, x_sems, x_start, x_finish = EXCHANGES[kind]

    def body(zq_ref, zk_ref, zv_ref, cos_ref, sin_ref, do_ref, l_ref, y_ref, *rest):
        e_ins, (gq_ref, gk_ref, gv_ref), e_outs = rest[:ne], rest[ne:ne + 3], rest[ne + 3:2 * ne + 3]
        q_ref, k_ref, v_ref, dq_ref, dk_ref, dv_ref = rest[2 * ne + 3:2 * ne + 9]
        comm = (e_ins, e_outs) + tuple(rest[2 * ne + 9:])
        if ne:
            @pl.when(pl.program_id(0) == 0)
            def _():
                x_start(*comm)

        _dil_load_qkv(zq_ref, zk_ref, zv_ref, cos_ref, sin_ref, q_ref, k_ref, v_ref)
        dq_ref[...] = jnp.zeros_like(dq_ref)
        dk_ref[...] = jnp.zeros_like(dk_ref)
        dv_ref[...] = jnp.zeros_like(dv_ref)
        first_head = _lane((b, LANES)) < 64
        band_prev, band_cur = _dil_bands()
        for _, r in DIL_PATTERNS:
            nb = n_blk // r

            def group(it, carry, r=r, nb=nb):
                loaded = []
                kc = vc = None
                for u in range(unroll):
                    rows_c, rows_p, has_prev = _dil_block(it * unroll + u, r, nb)
                    if u % min(nb, unroll):
                        kp, vp = kc, vc
                    else:
                        kp, vp = k_ref[rows_p, :].astype(BF16), v_ref[rows_p, :].astype(BF16)
                    kc, vc = k_ref[rows_c, :].astype(BF16), v_ref[rows_c, :].astype(BF16)
                    vals = [q_ref[rows_c, :].astype(BF16), kp, kc, vp, vc, do_ref[rows_c, :], l_ref[rows_c, :], y_ref[rows_c, :]]
                    loaded.append((rows_c, rows_p, has_prev, vals))
                done = []
                for rows_c, rows_p, has_prev, (qv, kp, kc, vp, vc, dof, lv, yv) in loaded:
                    q2 = _two_heads(qv, first_head)
                    do2 = _two_heads(dof.astype(BF16), first_head)
                    kcat, vcat = _cat(kp, kc), _cat(vp, vc)
                    lse2 = _cat(lv[:, 0:1], lv[:, 64:65])
                    dd2 = jnp.sum(_two_heads(dof * yv, first_head), axis=-1, keepdims=True)
                    p = jnp.exp(jnp.where(band_cur | (band_prev & has_prev), _nt(q2, kcat), NEG) - lse2)
                    ds = (p * (_nt(do2, vcat) - dd2)).astype(BF16)
                    dq2 = _nn(ds, kcat)
                    dkcat = _tn(ds, q2)
                    dvcat = _tn(p.astype(BF16), do2)
                    done.append((rows_c, rows_p, (jnp.where(first_head, dq2[:b], dq2[b:]), dkcat[:b], dkcat[b:],
                                                  dvcat[:b], dvcat[b:])))
                for rows_c, rows_p, (dq, dk_p, dk_c, dv_p, dv_c) in done:
                    dq_ref[rows_c, :] += dq
                    dk_ref[rows_p, :] += dk_p
                    dk_ref[rows_c, :] += dk_c
                    dv_ref[rows_p, :] += dv_p
                    dv_ref[rows_c, :] += dv_c
                return carry

            lax.fori_loop(0, n_blk // unroll, group, 0)

        def finish(i, carry, chunk=512):
            rows = pl.ds(pl.multiple_of(i * chunk, chunk), chunk)
            cos, sin = cos_ref[rows, :], sin_ref[rows, :]
            gq_ref[rows, :] = (_rotate(dq_ref[rows, :], cos, sin, -1.0) * QK_SCALE).astype(BF16)
            gk_ref[rows, :] = _rotate(dk_ref[rows, :], cos, sin, -1.0).astype(BF16)
            gv_ref[rows, :] = dv_ref[rows, :].astype(BF16)
            return carry

        lax.fori_loop(0, s // 512, finish, 0)
        if ne:
            @pl.when(pl.program_id(0) == 3)
            def _():
                x_finish(*comm)

    col_blk = lambda k: pl.BlockSpec((s, LANES), lambda hp: (0, 4 * k + hp))
    table = pl.BlockSpec((s, LANES), lambda hp: (0, 0))
    nat = pl.BlockSpec((s, LANES), lambda hp: (0, hp))
    return pl.pallas_call(
        body, grid=(4,), in_specs=[col_blk(Z_QB), col_blk(Z_KB), col_blk(Z_VB), table, table, nat, nat, nat] + [ANY] * ne,
        out_specs=[nat, nat, nat] + [ANY] * ne, out_shape=[SDS((s, ATT_W), BF16)] * 3 + x_shapes(exchange),
        scratch_shapes=[pltpu.VMEM((s, LANES), F32)] * 6 + (x_sems(ne) if ne else []), name="dil_bwd",
        compiler_params=_cp(("arbitrary",)))(z, z, z, cos_t, sin_t, dy, lse, y, *exchange)


def _sigmoid(v):
    return 1.0 / (1.0 + jnp.exp(-v))


def gate_mix(ya, yb, wa, wb, z, *, tm=2048, tn=512):
    s = ya.shape[0]
    d = wa.shape[1]
    ga_blk = 3 * ATT_W * 2 // tn
    gb_blk = ga_blk + d // tn

    def body(ya_ref, yb_ref, wa_ref, wb_ref, ga_ref, gb_ref, pa_ref, pb_ref, mx_ref):
        pa = _nn(ya_ref[...], wa_ref[...])
        pb = _nn(yb_ref[...].astype(BF16), wb_ref[...])
        pa_ref[...] = pa.astype(BF16)
        pb_ref[...] = pb.astype(BF16)
        mx_ref[...] = (_sigmoid(ga_ref[...].astype(F32)) * pa + _sigmoid(gb_ref[...].astype(F32)) * pb).astype(BF16)

    out = pl.BlockSpec((tm, tn), lambda i, j: (i, j))
    return pl.pallas_call(
        body, grid=(s // tm, d // tn),
        in_specs=[pl.BlockSpec((tm, ATT_W), lambda i, j: (i, 0)), pl.BlockSpec((tm, ATT_W), lambda i, j: (i, 0)),
                  pl.BlockSpec((ATT_W, tn), lambda i, j: (0, j)), pl.BlockSpec((ATT_W, tn), lambda i, j: (0, j)),
                  pl.BlockSpec((tm, tn), lambda i, j: (i, ga_blk + j)), pl.BlockSpec((tm, tn), lambda i, j: (i, gb_blk + j))],
        out_specs=[out, out, out], out_shape=[SDS((s, d), BF16)] * 3, name="gate_mix",
        compiler_params=_cp(("parallel", "parallel")))(ya, yb, wa, wb, z, z)


def mix_bwd(dy, w_o, z, pa, pb, wo_a, wo_b, ya, *, tm=512):
    s, d = dy.shape

    def body(dy_ref, wo_ref, ga_ref, gb_ref, pa_ref, pb_ref, wa_ref, wb_ref, ya_ref,
             dpa_ref, dpb_ref, dg_ref, dya_ref, dyb_ref, dd_ref):
        dm = _nt(dy_ref[...], wo_ref[...])
        sa = _sigmoid(ga_ref[...].astype(F32))
        sb = _sigmoid(gb_ref[...].astype(F32))
        dpa = (dm * sa).astype(BF16)
        dpb = (dm * sb).astype(BF16)
        dpa_ref[...] = dpa
        dpb_ref[...] = dpb
        dg_ref[:, 0:d] = (dm * pa_ref[...].astype(F32) * sa * (1.0 - sa)).astype(BF16)
        dg_ref[:, d:2 * d] = (dm * pb_ref[...].astype(F32) * sb * (1.0 - sb)).astype(BF16)
        dya = _nt(dpa, wa_ref[...]).astype(BF16)
        dya_ref[...] = dya
        dyb_ref[...] = _nt(dpb, wb_ref[...])
        lane = _lane((tm, LANES))
        for pr in range(ATT_W // LANES):
            pair = slice(pr * LANES, (pr + 1) * LANES)
            prod = dya[:, pair].astype(F32) * ya_ref[:, pair].astype(F32)
            lo = jnp.sum(jnp.where(lane < 64, prod, 0.0), axis=-1, keepdims=True)
            hi = jnp.sum(jnp.where(lane >= 64, prod, 0.0), axis=-1, keepdims=True)
            dd_ref[:, pair] = jnp.where(lane < 64, lo, hi)

    row = pl.BlockSpec((tm, d), lambda i: (i, 0))
    att = pl.BlockSpec((tm, ATT_W), lambda i: (i, 0))
    whole = lambda a: pl.BlockSpec(a.shape, lambda i: (0, 0))
    return pl.pallas_call(
        body, grid=(s // tm,),
        in_specs=[row, whole(w_o), pl.BlockSpec((tm, d), lambda i: (i, 3)), pl.BlockSpec((tm, d), lambda i: (i, 4)), row, row,
                  whole(wo_a), whole(wo_b), att],
        out_specs=[row, row, pl.BlockSpec((tm, 2 * d), lambda i: (i, 0)), att, att, att],
        out_shape=[SDS((s, d), BF16), SDS((s, d), BF16), SDS((s, 2 * d), BF16), SDS((s, ATT_W), BF16),
                   SDS((s, ATT_W), F32), SDS((s, ATT_W), F32)], name="mix_bwd",
        compiler_params=_cp(("parallel",)))(dy, w_o, z, z, pa, pb, wo_a, wo_b, ya)


GELU_C = math.sqrt(2.0 / math.pi)


def _gelu_parts(a):
    a2 = a * a
    th = jnp.tanh(a * (GELU_C + (GELU_C * 0.044715) * a2))
    half = 0.5 * a
    gelu = half + half * th
    dgelu = (0.5 + 0.5 * th) + half * (1.0 - th * th) * (GELU_C + (3.0 * GELU_C * 0.044715) * a2)
    return gelu, dgelu


def _causal_taps(u, before):
    row = _row(u.shape)
    r1 = jnp.where(row == 0, before[7:8, :], pltpu.roll(u, 1, axis=0))
    r2 = jnp.where(row == 0, before[6:7, :], jnp.where(row == 1, before[7:8, :], pltpu.roll(u, 2, axis=0)))
    return r1, r2


def ffn_up(h, wa, wb, cw, cb, *, tm=2048, tn=256):
    s, d = h.shape
    f = wa.shape[1]
    nj = f // tn

    def body(h_ref, wa_ref, wb_ref, cwa_ref, cwb_ref, cba_ref, cbb_ref, ua_ref, ub_ref, ca_ref, cbo_ref, m_ref, carry):
        @pl.when(pl.program_id(1) == 0)
        def _():
            carry[...] = jnp.zeros_like(carry)

        conv = []
        for k, (w_ref, cw_ref, cb_ref, u_ref, c_ref) in enumerate(((wa_ref, cwa_ref, cba_ref, ua_ref, ca_ref),
                                                                   (wb_ref, cwb_ref, cbb_ref, ub_ref, cbo_ref))):
            u16 = _nn(h_ref[...], w_ref[...]).astype(BF16)
            u_ref[...] = u16
            u = u16.astype(F32)
            r1, r2 = _causal_taps(u, carry[k])
            carry[k] = u[tm - 8:tm, :]
            c16 = (cw_ref[0:1, :] * r2 + cw_ref[1:2, :] * r1 + cw_ref[2:3, :] * u + cb_ref[...]).astype(BF16)
            c_ref[...] = c16
            conv.append(c16.astype(F32))
        m_ref[...] = (_gelu_parts(conv[0])[0] * conv[1]).astype(BF16)

    out = pl.BlockSpec((tm, tn), lambda j, i: (i, j))
    return pl.pallas_call(
        body, grid=(nj, s // tm),
        in_specs=[pl.BlockSpec((tm, d), lambda j, i: (i, 0)),
                  pl.BlockSpec((d, tn), lambda j, i: (0, j)), pl.BlockSpec((d, tn), lambda j, i: (0, j)),
                  pl.BlockSpec((3, tn), lambda j, i: (0, j)), pl.BlockSpec((3, tn), lambda j, i: (0, nj + j)),
                  pl.BlockSpec((1, tn), lambda j, i: (0, j)), pl.BlockSpec((1, tn), lambda j, i: (0, nj + j))],
        out_specs=[out] * 5, out_shape=[SDS((s, f), BF16)] * 5,
        scratch_shapes=[pltpu.VMEM((2, 8, tn), F32)], name="ffn_up",
        compiler_params=_cp(("parallel", "arbitrary")))(h, wa, wb, cw, cw, cb, cb)


def ffn_bwd(dm, ua, ub, ca, cbo, cw, *, tm=2048, tn=256):
    s, f = dm.shape
    nj = f // tn
    ni = s // tm

    def body(dm_ref, ua_ref, ub_ref, ca_ref, cbo_ref, cwa_ref, cwb_ref, dua_ref, dub_ref, ga_ref, gb_ref, carry):
        @pl.when(pl.program_id(1) == 0)
        def _():
            carry[...] = jnp.zeros_like(carry)
            ga_ref[...] = jnp.zeros_like(ga_ref)
            gb_ref[...] = jnp.zeros_like(gb_ref)

        row = _row((tm, tn))
        dmv = dm_ref[...].astype(F32)
        gelu, dgelu = _gelu_parts(ca_ref[...].astype(F32))
        dcs = (dmv * cbo_ref[...].astype(F32) * dgelu, dmv * gelu)
        for k, (dc, u_ref, cw_ref, du_ref, g_ref) in enumerate(((dcs[0], ua_ref, cwa_ref, dua_ref, ga_ref),
                                                                (dcs[1], ub_ref, cwb_ref, dub_ref, gb_ref))):
            u = u_ref[...].astype(F32)
            after = carry[k]
            n1 = jnp.where(row == tm - 1, after[0:1, :], pltpu.roll(dc, tm - 1, axis=0))
            n2 = jnp.where(row == tm - 2, after[0:1, :], jnp.where(row == tm - 1, after[1:2, :], pltpu.roll(dc, tm - 2, axis=0)))
            g_ref[0:1, :] += jnp.sum(n2 * u, axis=0, keepdims=True)
            g_ref[1:2, :] += jnp.sum(n1 * u, axis=0, keepdims=True)
            g_ref[2:3, :] += jnp.sum(dc * u, axis=0, keepdims=True)
            g_ref[3:4, :] += jnp.sum(dc, axis=0, keepdims=True)
            du_ref[...] = (cw_ref[2:3, :] * dc + cw_ref[1:2, :] * n1 + cw_ref[0:1, :] * n2).astype(BF16)
            carry[k] = dc[0:8, :]

    tile = pl.BlockSpec((tm, tn), lambda j, i: (ni - 1 - i, j))
    gspec = pl.BlockSpec((8, tn), lambda j, i: (0, j))
    return pl.pallas_call(
        body, grid=(nj, ni),
        in_specs=[tile] * 5 + [pl.BlockSpec((3, tn), lambda j, i: (0, j)), pl.BlockSpec((3, tn), lambda j, i: (0, nj + j))],
        out_specs=[tile, tile, gspec, gspec],
        out_shape=[SDS((s, f), BF16), SDS((s, f), BF16), SDS((8, f), F32), SDS((8, f), F32)],
        scratch_shapes=[pltpu.VMEM((2, 8, tn), F32)], name="ffn_bwd",
        compiler_params=_cp(("parallel", "arbitrary")))(dm, ua, ub, ca, cbo, cw, cw)


def adamw(w, g, m, v, *, name, tr=None):
    r = w.shape[0]
    rest = w.shape[1:]
    if tr is None:
        tr = r
        for cand in (256, 128, 64, 32, 16, 8):
            if r % cand == 0:
                tr = cand
                break

    def body(w_ref, g_ref, m_ref, v_ref, d_ref, nm_ref, nv_ref):
        gv = g_ref[...]
        mn = ADAM_B1 * m_ref[...] + (1.0 - ADAM_B1) * gv
        vn = ADAM_B2 * v_ref[...] + (1.0 - ADAM_B2) * (gv * gv)
        m_hat = mn / (1.0 - ADAM_B1 ** ADAM_STEP)
        v_hat = vn / (1.0 - ADAM_B2 ** ADAM_STEP)
        d_ref[...] = -ADAM_LR * (m_hat / (jnp.sqrt(v_hat) + ADAM_EPS) + ADAM_WD * w_ref[...])
        nm_ref[...] = mn
        nv_ref[...] = vn

    blk = pl.BlockSpec((tr,) + rest, lambda i: (i,) + (0,) * len(rest))
    return pl.pallas_call(body, grid=(r // tr,), in_specs=[blk] * 4, out_specs=[blk] * 3, out_shape=[SDS(w.shape, F32)] * 3,
                          name=name, compiler_params=_cp(("parallel",)))(w, g, m, v)


def adamw_rows_view(w, g_mine, g_full, m, v, c_arr, *, name, tc=256):
    r, _, c = w.shape
    per_half = c // 2 // tc

    def body(c_ref, w_ref, gm_ref, gf_ref, m_ref, v_ref, d_ref, nm_ref, nv_ref, go_ref):
        mine = (pl.program_id(0) >> (per_half.bit_length() - 1)) == c_ref[0]
        gv = jnp.where(mine, gm_ref[...], gf_ref[...])
        mn = ADAM_B1 * m_ref[:, 0, :] + (1.0 - ADAM_B1) * gv
        vn = ADAM_B2 * v_ref[:, 0, :] + (1.0 - ADAM_B2) * (gv * gv)
        m_hat = mn / (1.0 - ADAM_B1 ** ADAM_STEP)
        v_hat = vn / (1.0 - ADAM_B2 ** ADAM_STEP)
        d_ref[:, 0, :] = -ADAM_LR * (m_hat / (jnp.sqrt(v_hat) + ADAM_EPS) + ADAM_WD * w_ref[:, 0, :])
        nm_ref[:, 0, :] = mn
        nv_ref[:, 0, :] = vn
        go_ref[:, 0, :] = gv

    b3 = pl.BlockSpec((r, 1, tc), lambda i, c_ref: (0, 0, i))
    own = pl.BlockSpec((r, tc), lambda i, c_ref: (0, jnp.clip(i - c_ref[0] * per_half, 0, per_half - 1)))
    full = pl.BlockSpec((r, tc), lambda i, c_ref: (0, i))
    grid_spec = pltpu.PrefetchScalarGridSpec(num_scalar_prefetch=1, grid=(c // tc,), in_specs=[b3, own, full, b3, b3],
                                             out_specs=[b3] * 4)
    return pl.pallas_call(body, grid_spec=grid_spec, out_shape=[SDS(w.shape, F32)] * 4, name=name,
                          compiler_params=_cp(("parallel",)))(c_arr, w, g_mine, g_full, m, v)


ANY = pl.BlockSpec(memory_space=pl.ANY)
ICI_KINDS = ("x", "y", "xy")


def _coords():
    return lax.axis_index("x"), lax.axis_index("y"), lax.axis_index("c")


def _peer(kind, x, y, c):
    if kind == "c":
        return (x, y, 1 - c)
    if kind == "x":
        return (1 - x, y, c)
    if kind == "y":
        return (x, 1 - y, c)
    return (1 - x, 1 - y, c)


def _chip_of(p):
    return 2 * p[0] + p[1]


def _half(rows, which):
    h = rows // 2
    return pl.ds(pl.multiple_of(which * h, 16), h)


def _remote(src, dst, send_sem, recv_sem, to):
    return pltpu.make_async_remote_copy(src_ref=src, dst_ref=dst, send_sem=send_sem, recv_sem=recv_sem,
                                        device_id=to, device_id_type=MESH)


def allgather_balanced(shard, *, name):
    r, cols = shard.shape
    h, q = r // 2, r // 4

    def body(in_ref, out_ref, send_sems, recv_sems):
        x, y, c = _coords()
        me, sibling = (x, y, c), (x, y, 1 - c)
        nbr = ((1 - x, y, c), (x, 1 - y, c))
        chip = (2 * (1 - x) + y, 2 * x + (1 - y), 2 * (1 - x) + (1 - y))
        quarter = lambda core, i: pl.ds(pl.multiple_of(core * h + i * q, 16), q)
        sent = []

        def go(src, dst, slot, to):
            cp = _remote(src, dst, send_sems.at[slot], recv_sems.at[slot], to)
            cp.start()
            sent.append(cp)

        def landed(region, slot):
            _remote(region, region, send_sems.at[slot], recv_sems.at[slot], me).wait_recv()

        for i in range(2):
            for k in range(2):
                qi = k if i == 0 else 1 - k
                go(in_ref.at[quarter(c, qi)], out_ref.at[2 * x + y, quarter(c, qi)], 2 * k + qi, nbr[k])
        for k in range(2):
            piece = out_ref.at[chip[k], quarter(c, k)]
            landed(piece, 2 * k + k)
            go(piece, piece, 4 + k, nbr[1 - k])
            go(piece, piece, 6 + 2 * k + k, sibling)
        for k in range(2):
            piece = out_ref.at[chip[k], quarter(c, 1 - k)]
            landed(piece, 2 * k + 1 - k)
            go(piece, piece, 6 + 2 * k + 1 - k, sibling)
        for k in range(2):
            piece = out_ref.at[chip[2], quarter(c, k)]
            landed(piece, 4 + k)
            go(piece, piece, 10 + k, sibling)
        for k in range(2):
            for i in range(2):
                landed(out_ref.at[chip[k], quarter(1 - c, i)], 6 + 2 * k + i)
            landed(out_ref.at[chip[2], quarter(1 - c, k)], 10 + k)
        for cp in sent:
            cp.wait_send()

    return pl.pallas_call(
        body, in_specs=[ANY], out_specs=ANY, out_shape=SDS((4,) + shard.shape, shard.dtype),
        scratch_shapes=[pltpu.SemaphoreType.DMA((12,)), pltpu.SemaphoreType.DMA((12,))], name=name)(shard)


def _allgather_shapes(shards):
    return [SDS((4,) + a.shape, a.dtype) for a in shards]


def _allgather_sems(n):
    return [pltpu.SemaphoreType.DMA((n, 6)), pltpu.SemaphoreType.DMA((n, 6))]


def _allgather_rows(ref, is_halved, which):
    r = ref.shape[0]
    return _half(r, which) if is_halved else pl.ds(0, r)


def _allgather_first(ins, outs, send_sems, recv_sems, halved):
    x, y, c = _coords()
    my_chip = 2 * x + y
    cps = []
    for w in range(len(ins)):
        rows = _allgather_rows(ins[w], halved[w], c)
        for k, kind in enumerate(ICI_KINDS):
            cps.append(_remote(ins[w].at[rows], outs[w].at[my_chip, rows], send_sems.at[w, k], recv_sems.at[w, k],
                               _peer(kind, x, y, c)))
    return cps


def _allgather_start(ins, outs, send_sems, recv_sems, halved):
    for cp in _allgather_first(ins, outs, send_sems, recv_sems, halved):
        cp.start()


def _allgather_finish(ins, outs, send_sems, recv_sems, halved):
    x, y, c = _coords()
    me = (x, y, c)
    second = []
    for w in range(len(ins)):
        for k, kind in enumerate(ICI_KINDS):
            landed = outs[w].at[_chip_of(_peer(kind, x, y, c)), _allgather_rows(ins[w], halved[w], c)]
            _remote(landed, landed, send_sems.at[w, k], recv_sems.at[w, k], me).wait_recv()
            if halved[w]:
                cp = _remote(landed, landed, send_sems.at[w, 3 + k], recv_sems.at[w, 3 + k], _peer("c", x, y, c))
                cp.start()
                second.append(cp)
    for w in range(len(ins)):
        if halved[w]:
            for k, kind in enumerate(ICI_KINDS):
                other = outs[w].at[_chip_of(_peer(kind, x, y, c)), _allgather_rows(ins[w], True, 1 - c)]
                _remote(other, other, send_sems.at[w, 3 + k], recv_sems.at[w, 3 + k], me).wait_recv()
    for cp in _allgather_first(ins, outs, send_sems, recv_sems, halved) + second:
        cp.wait_send()


def _half_of(ref, by_cols, which):
    lead = (slice(None),) * (len(ref.shape) - 2)
    if by_cols:
        h = ref.shape[-1] // 2
        return ref.at[lead + (slice(None), pl.ds(pl.multiple_of(which * h, LANES), h))]
    return ref.at[lead + (_half(ref.shape[-2], which),)]


def _half_shape(shape, by_cols):
    return shape[:-1] + (shape[-1] // 2,) if by_cols else shape[:-2] + (shape[-2] // 2, shape[-1])


def grads_to_sibling(gs, by_cols, *, name):
    n = len(gs)

    def body(*refs):
        ins, outs = refs[:n], refs[n:2 * n]
        send_sems, recv_sems = refs[2 * n:]
        x, y, c = _coords()
        cps = []
        for w in range(n):
            cp = _remote(_half_of(ins[w], by_cols[w], 1 - c), outs[w], send_sems.at[w], recv_sems.at[w], _peer("c", x, y, c))
            cp.start()
            cps.append(cp)
        for cp in cps:
            cp.wait()

    return pl.pallas_call(
        body, in_specs=[ANY] * n, out_specs=[ANY] * n,
        out_shape=[SDS(_half_shape(a.shape, bc), a.dtype) for a, bc in zip(gs, by_cols)],
        scratch_shapes=[pltpu.SemaphoreType.DMA((n,)), pltpu.SemaphoreType.DMA((n,))], name=name)(*gs)


def _to_chips_shapes(ps):
    return [SDS((3,) + a.shape[1:], a.dtype) for a in ps]


def _to_chips_sems(n):
    return [pltpu.SemaphoreType.DMA((n, 3)), pltpu.SemaphoreType.DMA((n, 3))]


def _to_chips_copies(ins, outs, send_sems, recv_sems):
    x, y, c = _coords()
    cps = []
    for w in range(len(ins)):
        for k, kind in enumerate(ICI_KINDS):
            to = _peer(kind, x, y, c)
            cps.append(_remote(ins[w].at[_chip_of(to)], outs[w].at[k], send_sems.at[w, k], recv_sems.at[w, k], to))
    return cps


def _to_chips_start(ins, outs, send_sems, recv_sems):
    for cp in _to_chips_copies(ins, outs, send_sems, recv_sems):
        cp.start()


def _to_chips_finish(ins, outs, send_sems, recv_sems):
    for cp in _to_chips_copies(ins, outs, send_sems, recv_sems):
        cp.wait()


def _to_owners_shapes(ps):
    return [SDS((7, a.shape[1] // 2, a.shape[2]), a.dtype) for a in ps]


def _to_owners_sems(n):
    return [pltpu.SemaphoreType.DMA((n, 7)), pltpu.SemaphoreType.DMA((n, 7))]


def _to_owners_copies(ins, outs, send_sems, recv_sems):
    x, y, c = _coords()
    cps = []
    for w in range(len(ins)):
        rows = ins[w].shape[1]
        for k, kind in enumerate(ICI_KINDS):
            px, py, _ = _peer(kind, x, y, c)
            for h in range(2):
                cps.append(_remote(ins[w].at[2 * px + py, _half(rows, h)], outs[w].at[2 * k + c],
                                   send_sems.at[w, 2 * k + h], recv_sems.at[w, 2 * k + c], (px, py, h)))
        cps.append(_remote(ins[w].at[2 * x + y, _half(rows, 1 - c)], outs[w].at[6], send_sems.at[w, 6], recv_sems.at[w, 6],
                           _peer("c", x, y, c)))
    return cps


def _to_owners_start(ins, outs, send_sems, recv_sems):
    for cp in _to_owners_copies(ins, outs, send_sems, recv_sems):
        cp.start()


def _to_owners_finish(ins, outs, send_sems, recv_sems):
    for cp in _to_owners_copies(ins, outs, send_sems, recv_sems):
        cp.wait_send()
    for w in range(len(ins)):
        for slot in range(7):
            got = outs[w].at[slot]
            _remote(got, got, send_sems.at[w, slot], recv_sems.at[w, slot], _coords()).wait_recv()


EXCHANGES = {"to_chips": (_to_chips_shapes, _to_chips_sems, _to_chips_start, _to_chips_finish),
             "to_owners": (_to_owners_shapes, _to_owners_sems, _to_owners_start, _to_owners_finish)}


def halves_to_full(hs, by_cols, *, name):
    n = len(hs)

    def body(*refs):
        ins, outs = refs[:n], refs[n:2 * n]
        send_sems, recv_sems = refs[2 * n:]
        x, y, c = _coords()
        cps = []
        for w in range(n):
            cp = _remote(ins[w], _half_of(outs[w], by_cols[w], c), send_sems.at[w], recv_sems.at[w], _peer("c", x, y, c))
            cp.start()
            cps.append(cp)
        for cp in cps:
            cp.wait()

    return pl.pallas_call(
        body, in_specs=[ANY] * n, out_specs=[ANY] * n,
        out_shape=[SDS((a.shape[0], 2 * a.shape[1]) if bc else (2 * a.shape[0], a.shape[1]), a.dtype)
                   for a, bc in zip(hs, by_cols)],
        scratch_shapes=[pltpu.SemaphoreType.DMA((n,)), pltpu.SemaphoreType.DMA((n,))],
        name=name)(*hs)


def _row_tile(rows):
    for cand in (256, 192, 176, 128, 64, 32, 16):
        if rows % cand == 0:
            return cand
    return rows


def chip_sum(g, recv, c_arr, by_cols, *, name):
    _, r, cols = g.shape

    def body(c_ref, g_ref, r_ref, f_ref, b_ref):
        tot = g_ref[...] + r_ref[...]
        f_ref[...] = tot
        b_ref[...] = tot.astype(BF16)

    if by_cols:
        tc = 4 * LANES
        nblk = cols // 2 // tc
        shape = (4, r, cols // 2)
        blk = pl.BlockSpec((None, r, tc), lambda j, i, c_ref: (j, 0, i))
        mine = pl.BlockSpec((None, r, tc), lambda j, i, c_ref: (j, 0, c_ref[0] * nblk + i))
    else:
        tr = _row_tile(r // 2)
        nblk = r // 2 // tr
        shape = (4, r // 2, cols)
        blk = pl.BlockSpec((None, tr, cols), lambda j, i, c_ref: (j, i, 0))
        mine = pl.BlockSpec((None, tr, cols), lambda j, i, c_ref: (j, c_ref[0] * nblk + i, 0))
    grid_spec = pltpu.PrefetchScalarGridSpec(num_scalar_prefetch=1, grid=(4, nblk), in_specs=[mine, blk], out_specs=[blk, blk])
    return pl.pallas_call(body, grid_spec=grid_spec, out_shape=[SDS(shape, F32), SDS(shape, BF16)],
                          name=name, compiler_params=_cp(("parallel", "parallel")))(c_arr, g, recv)


def final_sum(pf, recv, chip_arr, *, name):
    _, h, cols = pf.shape
    tr = _row_tile(h)

    def body(chip_ref, p_ref, r_ref, o_ref):
        o_ref[...] = ((p_ref[...] + r_ref[0].astype(F32)) + r_ref[1].astype(F32)) + r_ref[2].astype(F32)

    grid_spec = pltpu.PrefetchScalarGridSpec(
        num_scalar_prefetch=1, grid=(h // tr,),
        in_specs=[pl.BlockSpec((None, tr, cols), lambda i, chip_ref: (chip_ref[0], i, 0)),
                  pl.BlockSpec((3, tr, cols), lambda i, chip_ref: (0, i, 0))],
        out_specs=pl.BlockSpec((tr, cols), lambda i, chip_ref: (i, 0)))
    return pl.pallas_call(body, grid_spec=grid_spec, out_shape=SDS((h, cols), F32), name=name,
                          compiler_params=_cp(("parallel",)))(chip_arr, pf, recv)


def owner_sum(g, recv, pos_arr, *, name):
    _, r, cols = g.shape
    h = r // 2
    tr = _row_tile(h)
    nblk = h // tr

    def body(pos_ref, g_ref, r_ref, o_ref):
        tot = g_ref[...]
        for slot in range(7):
            tot = tot + r_ref[slot].astype(F32)
        o_ref[...] = tot

    grid_spec = pltpu.PrefetchScalarGridSpec(
        num_scalar_prefetch=1, grid=(nblk,),
        in_specs=[pl.BlockSpec((None, tr, cols), lambda i, pos: (pos[0], pos[1] * nblk + i, 0)),
                  pl.BlockSpec((7, tr, cols), lambda i, pos: (0, i, 0))],
        out_specs=pl.BlockSpec((tr, cols), lambda i, pos: (i, 0)))
    return pl.pallas_call(body, grid_spec=grid_spec, out_shape=SDS((h, cols), F32), name=name,
                          compiler_params=_cp(("parallel",)))(pos_arr, g, recv)


def allreduce_small(v, *, name):
    rws, cols = v.shape

    def body(v_ref, all_ref, sum_ref, send_sems, recv_sems, local_sem):
        x, y, c = _coords()
        me, sibling = (x, y, c), (x, y, 1 - c)
        chips = [(1 - x, y), (x, 1 - y), (1 - x, 1 - y)]

        def rows(px, py, pc):
            return all_ref.at[pl.ds(pl.multiple_of((4 * px + 2 * py + pc) * rws, 8), rws), :]

        def copy(k, block, to, src=None):
            return _remote(rows(*block) if src is None else src, rows(*block), send_sems.at[k], recv_sems.at[k], to)

        mine = pltpu.make_async_copy(v_ref, rows(*me), local_sem)
        mine.start()
        first = [copy(0, me, sibling, src=v_ref)]
        first += [copy(1 + j, me, (*chip, c), src=v_ref) for j, chip in enumerate(chips)]
        for cp in first:
            cp.start()
        passed = [copy(4 + j, (*chip, c), sibling) for j, chip in enumerate(chips)]
        for j, chip in enumerate(chips):
            copy(1 + j, (*chip, c), me).wait_recv()
            passed[j].start()
        copy(0, sibling, me).wait_recv()
        for j, chip in enumerate(chips):
            copy(4 + j, (*chip, 1 - c), me).wait_recv()
        for cp in first + passed:
            cp.wait_send()
        mine.wait()
        tot = all_ref[0:rws, :]
        for dev in range(1, 8):
            tot = tot + all_ref[dev * rws:(dev + 1) * rws, :]
        sum_ref[...] = tot

    vm = pl.BlockSpec(memory_space=pltpu.VMEM)
    return pl.pallas_call(
        body, in_specs=[vm], out_specs=[vm, vm],
        out_shape=[SDS((8 * rws, cols), v.dtype), SDS((rws, cols), v.dtype)],
        scratch_shapes=[pltpu.SemaphoreType.DMA((7,)), pltpu.SemaphoreType.DMA((7,)), pltpu.SemaphoreType.DMA],
        name=name)(v)[1]


def _pack_rows(parts, rows):
    out = []
    for a, r in zip(parts, rows):
        flat = a.reshape(-1)
        flat = jnp.pad(flat, (0, r * LANES - flat.shape[0]))
        out.append(flat.reshape(r, LANES))
    return jnp.concatenate(out, axis=0)


def _unpack_rows(packed, shapes, rows):
    out, at = [], 0
    for shp, r in zip(shapes, rows):
        size = int(np.prod(shp))
        out.append(packed[at:at + r].reshape(-1)[:size].reshape(shp))
        at += r
    return out


def kernel(x, g_pre_mix, w_in, b_forget, w_o_fox, w_o_dil, w_out, g_post_mix, g_pre_ffn, w_up, conv_w, conv_b, w_down, g_post_ffn, loss_target, m_g_pre_mix, m_w_in, m_b_forget, m_w_o_fox, m_w_o_dil, m_w_out, m_g_post_mix, m_g_pre_ffn, m_w_up, m_conv_w, m_conv_b, m_w_down, m_g_post_ffn, v_g_pre_mix, v_w_in, v_b_forget, v_w_o_fox, v_w_o_dil, v_w_out, v_g_post_mix, v_g_pre_ffn, v_w_up, v_conv_w, v_conv_b, v_w_down, v_g_post_ffn):
    xi, yi, ci = _coords()
    chip = 2 * xi + yi
    c_arr = jnp.reshape(ci, (1,)).astype(jnp.int32)
    chip_arr = jnp.reshape(chip, (1,)).astype(jnp.int32)
    xs = x[0]
    target = loss_target[0]
    s, d = xs.shape
    f_half = w_down.shape[1] * 4
    cols_in = w_in.shape[2]

    big = (w_in, w_o_fox, w_o_dil, w_out, w_up, w_down)
    shards = [w[0].astype(BF16) for w in big]
    a_in = allgather_balanced(shards[0], name="allgather_w_in")
    w_in_full = jnp.concatenate([jnp.where(chip == j, shards[0], a_in[j]) for j in range(4)], axis=1)
    nf = N_HEADS
    e_a, e_b = 3 * ATT_W, 3 * ATT_W + nf
    wz = jnp.concatenate([w_in_full[:, :e_a], w_in_full[:, e_b:]], axis=1)
    wf = jnp.pad(w_in_full[:, e_a:e_b], ((0, 0), (0, LANES - nf)))
    cb = conv_b
    bfo = jnp.pad(b_forget, ((0, 0), (0, LANES - nf)))

    h1 = rmsnorm_fwd(xs, g_pre_mix)
    z = mm([(h1, d, 0)], [(wz, d, 0)], nt=False, out_dtype=BF16, tm=s, tn=512, name="in_proj")
    fa = mm([(h1, d, 0)], [(wf, d, 0)], nt=False, out_dtype=F32, tm=s, tn=LANES, name="in_proj_forget")
    q_aug, k_aug, v_aug = fox_prep(z, fa, bfo)
    later = shards[1:] + [conv_w[0]]
    ya, lse_a, *late = fox_fwd(q_aug, k_aug, v_aug, gather=later, halved=[True] * 5 + [False], hps=N_HEADS)
    a_of, a_od, a_out, a_up, a_down, a_cw = [
        lax.dynamic_update_index_in_dim(a4, own, chip, 0) for a4, own in zip(late, later)]
    cw = jnp.concatenate([a_cw[j] for j in range(4)], axis=1)
    wo_a = jnp.concatenate([a_of[j] for j in range(4)], axis=1)
    wo_b = jnp.concatenate([a_od[j] for j in range(4)], axis=1)
    w_o = a_out.reshape(d, d)
    w_dn = a_down.reshape(f_half, d)
    wu_a = jnp.concatenate([a_up[0], a_up[1]], axis=1)
    wu_b = jnp.concatenate([a_up[2], a_up[3]], axis=1)
    cos_t, sin_t = rope_cos_sin(s)
    yb, lse_b = dil_fwd_all(z, cos_t, sin_t)
    pa, pb, mixed = gate_mix(ya, yb, wo_a, wo_b, z)
    y1, x1, h2 = proj_norm_res(mixed, w_o, g_post_mix, xs, g_pre_ffn, tm=1024, name="out_proj")
    ua, ub, conv_a, conv_bh, mid = ffn_up(h2, wu_a, wu_b, cw, cb)
    dout, dy2, gg_post_ffn, sq = proj_norm_loss(mid, w_dn, g_post_ffn, x1, target, name="down_proj")
    loss = lax.psum(0.5 * sq[0, 0] / d, ("x", "y", "c"))

    dmid = mm([(dy2, d, 0)], [(w_dn, d, 0)], nt=True, out_dtype=BF16, tm=2048, tn=f_half // 2, name="down_dgrad")
    dw_down, dw_down16 = wgrad((mid, f_half, 0), dy2, tk=f_half // 2, tn=1024, ts=2048, name="down_wgrad", bf16_copy=True)
    dua, dub, gc_a, gc_b = ffn_bwd(dmid, ua, ub, conv_a, conv_bh, cw)
    dx1, dy1, gg_pre_ffn, gg_post_mix = mm_norm_bwd(
        [(dua, f_half, 0), (dub, f_half, 0)], [(wu_a, f_half, 0), (wu_b, f_half, 0)],
        [(x1, g_pre_ffn, dout, F32), (y1, g_post_mix, None, BF16)], name="up_dgrad")
    dw_up = None
    for k, du in enumerate((dua, dub)):
        dw_up = wgrad((h2, d, 0), du, tk=1024, tn=f_half // 2, ts=2048, name=f"up_wgrad_{k}", chip_major=True,
                      slabs=(4, 2 * k), into=dw_up, bf16_copy=True)
    g_ffn = [(dw_up[0], dw_up[1]), (dw_down.reshape(4, f_half // 4, d), dw_down16.reshape(4, f_half // 4, d))]
    dw_out, dw_out16 = wgrad((mixed, d, 0), dy1, tk=1024, tn=1024, ts=2048, name="out_wgrad", bf16_copy=True)
    dpa, dpb, dz_g, dya, dyb, dd_a = mix_bwd(dy1, w_o, z, pa, pb, wo_a, wo_b, ya)
    by_chip_cols = lambda a: jnp.stack([a[:, j * (d // 4):(j + 1) * (d // 4)] for j in range(4)], axis=0)
    dw_of = [by_chip_cols(a) for a in wgrad((ya, ATT_W, 0), dpa, tk=ATT_W, tn=d, ts=1024, name="fox_o_wgrad", bf16_copy=True)]
    dw_od = [by_chip_cols(a) for a in wgrad((yb, ATT_W, 0), dpb, tk=ATT_W, tn=d, ts=1024, name="dil_o_wgrad", bf16_copy=True)]
    g_mix = [dw_of, dw_od, (dw_out.reshape(4, d // 4, d), dw_out16.reshape(4, d // 4, d))]
    dq_aug, dk_aug, dv_a, *got_ffn = fox_bwd(q_aug, k_aug, z, dya, lse_a, dd_a, exchange=[g[1] for g in g_ffn], kind="to_owners")
    dz_a, dfa, gg_bf = fox_post(dq_aug, dk_aug, dv_a, fa, bfo)
    *dz_b, got_of, got_od, got_out = dil_bwd_all(z, cos_t, sin_t, dyb, lse_b, yb, exchange=[g[1] for g in g_mix],
                                                 kind="to_owners")
    got_mix = [got_of, got_od, got_out]
    dwt_a = wgrad((dz_a, e_a, 0), h1, tk=e_a // 2, tn=d, ts=2048, name="in_wgrad_a")
    dwt_b = [wgrad((part, ATT_W, 0), h1, tk=ATT_W, tn=d, ts=2048, name=f"in_wgrad_b{k}") for k, part in enumerate(dz_b)]
    dwt_g = wgrad((dz_g, 2 * d, 0), h1, tk=d, tn=d, ts=2048, name="in_wgrad_g")
    dwt_f = wgrad((dfa, LANES, 0), h1, tk=LANES, tn=d, ts=2048, name="in_wgrad_f")
    dwt_full = jnp.concatenate([dwt_a, dwt_f[:nf], *dwt_b, dwt_g], axis=0)
    dw_in = jnp.stack([dwt_full[j * cols_in:(j + 1) * cols_in] for j in range(4)], axis=0)
    from_sib = grads_to_sibling([dw_in], [True], name="grads_to_sibling_in")
    sum_in = chip_sum(dw_in, from_sib[0], c_arr, True, name="chip_sum_w_in")
    grad_x, gg_pre_mix, got_in = mm_norm_bwd(
        [(dz_a, e_a, 0), *[(part, ATT_W, 0) for part in dz_b], (dz_g, d, 0), (dz_g, d, 1), (dfa, LANES, 0)],
        [(wz, e_a, 0), *[(wz, ATT_W, Z_QB + k) for k in range(3)], (wz, d, 3), (wz, d, 4), (wf, LANES, 0)],
        [(xs, g_pre_mix, dx1, F32)], exchange=[sum_in[1]], name="in_dgrad")

    names = ("w_in", "w_o_fox", "w_o_dil", "w_out", "w_up", "w_down")
    pos_arr = jnp.concatenate([chip_arr, c_arr])
    halves = [final_sum(sum_in[0], got_in, chip_arr, name="final_sum_w_in")] + [
        owner_sum(g[0], got, pos_arr, name=f"owner_sum_{nm}") for g, got, nm in zip(g_mix + g_ffn, got_mix + got_ffn, names[1:])]
    from_half = halves_to_full(halves, [True] + [False] * 5, name="halves_to_full")
    g_big = [None] + [lax.dynamic_update_slice_in_dim(full, mine, ci * mine.shape[0], axis=0)
                      for full, mine in zip(from_half[1:], halves[1:])]
    upd_big = [adamw(w[0], g, m[0], v[0], name=f"adamw_{nm}") for w, g, m, v, nm in list(zip(
        big, g_big, (m_w_in, m_w_o_fox, m_w_o_dil, m_w_out, m_w_up, m_w_down),
        (v_w_in, v_w_o_fox, v_w_o_dil, v_w_out, v_w_up, v_w_down), names))[1:]]
    to_t = lambda a: jnp.transpose(a, (2, 0, 1))
    from_t = lambda a: jnp.transpose(a, (1, 2, 0))
    *upd_in, g_in_t = adamw_rows_view(to_t(w_in), halves[0], from_half[0], to_t(m_w_in), to_t(v_w_in), c_arr,
                                      name="adamw_w_in")

    g_cw_loc = jnp.concatenate([gc_a[0:3], gc_b[0:3]], axis=1)
    g_cb_loc = jnp.concatenate([gc_a[3:4], gc_b[3:4]], axis=1)
    small_loc = [gg_pre_mix, gg_post_mix, gg_pre_ffn, gg_post_ffn, g_cb_loc, gg_bf[:, :nf], g_cw_loc]
    red_rows = (8, 8, 8, 8, 48, 8, 136)
    red = allreduce_small(_pack_rows(small_loc, red_rows), name="allreduce_small")
    g_pm, g_qm, g_pf, g_qf, g_cb, g_bf, g_cw_full = _unpack_rows(red, [a.shape for a in small_loc], red_rows)
    cols_cw = conv_w.shape[2]
    g_cw = lax.dynamic_slice_in_dim(g_cw_full, chip * cols_cw, cols_cw, axis=1)
    small_w = (g_pre_mix, g_post_mix, g_pre_ffn, g_post_ffn, conv_b, b_forget, conv_w[0])
    small_m = (m_g_pre_mix, m_g_post_mix, m_g_pre_ffn, m_g_post_ffn, m_conv_b, m_b_forget, m_conv_w[0])
    small_v = (v_g_pre_mix, v_g_post_mix, v_g_pre_ffn, v_g_post_ffn, v_conv_b, v_b_forget, v_conv_w[0])
    small_g = (g_pm, g_qm, g_pf, g_qf, g_cb, g_bf, g_cw)
    small_names = ("g_pre_mix", "g_post_mix", "g_pre_ffn", "g_post_ffn", "conv_b", "b_forget", "conv_w")
    per_param = [adamw(w, g, m, v, name=f"adamw_{nm}") for w, g, m, v, nm in zip(small_w, small_g, small_m, small_v, small_names)]
    upd_small = [[u[j] for u in per_param] for j in range(3)]

    order = ("g_pre_mix", "w_in", "b_forget", "w_o_fox", "w_o_dil", "w_out", "g_post_mix", "g_pre_ffn", "w_up", "conv_w",
             "conv_b", "w_down", "g_post_ffn")
    grads, deltas, new_ms, new_vs = {}, {}, {}, {}
    grads["w_in"] = from_t(g_in_t)
    deltas["w_in"], new_ms["w_in"], new_vs["w_in"] = (from_t(a) for a in upd_in)
    for k, nm in enumerate(names[1:]):
        grads[nm] = g_big[k + 1][None]
        deltas[nm], new_ms[nm], new_vs[nm] = (a[None] for a in upd_big[k])
    for k, nm in enumerate(small_names):
        lead = (lambda a: a[None]) if nm == "conv_w" else (lambda a: a)
        grads[nm] = lead(small_g[k])
        deltas[nm], new_ms[nm], new_vs[nm] = (lead(upd_small[j][k]) for j in range(3))
    return (loss, grad_x[None], *[grads[nm] for nm in order], *[deltas[nm] for nm in order],
            *[new_ms[nm] for nm in order], *[new_vs[nm] for nm in order])
```

```python
import functools
import math

import numpy as np
import jax
import jax.numpy as jnp
from jax import lax
from jax.experimental import pallas as pl
from jax.experimental.pallas import tpu as pltpu

F32 = jnp.float32
BF16 = jnp.bfloat16
SDS = jax.ShapeDtypeStruct
MESH = pl.DeviceIdType.MESH

HEAD_DIM = 64
N_HEADS = 8
LANES = 128
ATT_W = N_HEADS * HEAD_DIM
DIL_PATTERNS = ((128, 1), (512, 4), (2048, 16))
DIL_BLK = 128
ROPE_DIM = HEAD_DIM // 4
ROPE_THETA = 500000.0
RMS_EPS = 1e-6
NEG = -1e30
QK_SCALE = 1.0 / math.sqrt(HEAD_DIM)
ADAM_LR, ADAM_B1, ADAM_B2, ADAM_EPS, ADAM_WD, ADAM_STEP = 0.001, 0.9, 0.999, 1e-08, 0.01, 10
VMEM_LIMIT = 56 * 1024 * 1024

Z_QA, Z_KA, Z_VA, Z_QB, Z_KB, Z_VB = 0, 1, 2, 3, 4, 5
Z_W = 5120


def _cp(sem):
    return pltpu.CompilerParams(dimension_semantics=sem, vmem_limit_bytes=VMEM_LIMIT)


def _nt(a, b):
    return lax.dot_general(a, b, (((1,), (1,)), ((), ())), preferred_element_type=F32)


def _tn(a, b):
    return lax.dot_general(a, b, (((0,), (0,)), ((), ())), preferred_element_type=F32)


def _nn(a, b):
    return jnp.dot(a, b, preferred_element_type=F32)


def _lane(shape):
    return lax.broadcasted_iota(jnp.int32, shape, 1)


def _row(shape):
    return lax.broadcasted_iota(jnp.int32, shape, 0)


def rmsnorm_fwd(x, g, *, tm=1024):
    s, d = x.shape

    def body(x_ref, g_ref, h_ref):
        xv = x_ref[...]
        inv = lax.rsqrt(jnp.mean(xv * xv, axis=-1, keepdims=True) + RMS_EPS)
        h_ref[...] = (xv * inv * g_ref[...]).astype(h_ref.dtype)

    return pl.pallas_call(
        body, grid=(s // tm,),
        in_specs=[pl.BlockSpec((tm, d), lambda i: (i, 0)), pl.BlockSpec((1, d), lambda i: (0, 0))],
        out_specs=pl.BlockSpec((tm, d), lambda i: (i, 0)),
        out_shape=SDS((s, d), BF16), name="rmsnorm_fwd", compiler_params=_cp(("parallel",)))(x, g)


def mm(a_views, b_views, *, nt, out_dtype, tm, tn, name):
    n_p = len(a_views)
    m = a_views[0][0].shape[0]
    n = b_views[0][0].shape[0] if nt else b_views[0][0].shape[1]

    def body(*refs):
        o_ref = refs[-1]
        acc = None
        for p in range(n_p):
            av = refs[p][...].astype(BF16)
            bv = refs[n_p + p][...].astype(BF16)
            dv = _nt(av, bv) if nt else _nn(av, bv)
            acc = dv if acc is None else acc + dv
        o_ref[...] = acc.astype(o_ref.dtype)

    in_specs = []
    for arr, w, blk in a_views:
        in_specs.append(pl.BlockSpec((tm, w), functools.partial(lambda i, j, blk: (i, blk), blk=blk)))
    for arr, w, blk in b_views:
        if nt:
            in_specs.append(pl.BlockSpec((tn, w), functools.partial(lambda i, j, blk: (j, blk), blk=blk)))
        else:
            in_specs.append(pl.BlockSpec((w, tn), lambda i, j: (0, j)))
    return pl.pallas_call(
        body, grid=(m // tm, n // tn), in_specs=in_specs,
        out_specs=pl.BlockSpec((tm, tn), lambda i, j: (i, j)),
        out_shape=SDS((m, n), out_dtype), name=name,
        compiler_params=_cp(("parallel", "parallel")))(*[a[0] for a in a_views], *[b[0] for b in b_views])


def wgrad(a_view, g, *, tk, tn, ts, name, chip_major=False, slabs=None, into=None, bf16_copy=False):
    arr, ka, blk = a_view
    s, n = g.shape
    ns = s // ts
    total, first = slabs if slabs else (n // tn, 0)
    n_into = 0 if into is None else (2 if bf16_copy else 1)

    def body(a_ref, g_ref, *rest):
        o_ref = rest[n_into]

        @pl.when(pl.program_id(2) == 0)
        def _():
            o_ref[...] = jnp.zeros_like(o_ref)

        o_ref[...] += _tn(a_ref[...].astype(BF16), g_ref[...].astype(BF16))
        if bf16_copy:
            @pl.when(pl.program_id(2) == ns - 1)
            def _():
                rest[n_into + 1][...] = o_ref[...].astype(BF16)

    if chip_major:
        out_spec = pl.BlockSpec((None, tk, tn), lambda i, j, k: (first + j, i, 0))
        shape = (total, ka, tn)
    else:
        out_spec = pl.BlockSpec((tk, tn), lambda i, j, k: (i, j))
        shape = (ka, n)
    in_specs = [pl.BlockSpec((ts, tk), lambda i, j, k: (k, blk * (ka // tk) + i)),
                pl.BlockSpec((ts, tn), lambda i, j, k: (k, j))]
    args = [arr, g]
    if into is not None:
        earlier = list(into) if bf16_copy else [into]
        in_specs += [pl.BlockSpec(memory_space=pl.ANY)] * len(earlier)
        args += earlier
    out = pl.pallas_call(
        body, grid=(ka // tk, n // tn, ns), in_specs=in_specs,
        out_specs=[out_spec, out_spec] if bf16_copy else out_spec,
        out_shape=[SDS(shape, F32), SDS(shape, BF16)] if bf16_copy else SDS(shape, F32), name=name,
        input_output_aliases={2 + k: k for k in range(n_into)},
        compiler_params=_cp(("parallel", "parallel", "arbitrary")))(*args)
    return out


def _norm_bwd_rows(dh, xh, inv, g):
    dxh = dh * g
    dx = inv * (dxh - xh * jnp.mean(dxh * xh, axis=-1, keepdims=True))
    return dx, jnp.sum((dh * xh).reshape(dh.shape[0] // 8, 8, dh.shape[1]), axis=0)


def proj_norm_res(a, w, g, xres, g_next, *, tm=512, name):
    s, k = a.shape
    d = w.shape[1]

    def body(a_ref, w_ref, g_ref, x_ref, gn_ref, y_ref, o_ref, h_ref):
        y = _nn(a_ref[...], w_ref[...])
        inv = lax.rsqrt(jnp.mean(y * y, axis=-1, keepdims=True) + RMS_EPS)
        xn = x_ref[...] + y * inv * g_ref[...]
        y_ref[...] = y
        o_ref[...] = xn
        inv_n = lax.rsqrt(jnp.mean(xn * xn, axis=-1, keepdims=True) + RMS_EPS)
        h_ref[...] = (xn * inv_n * gn_ref[...]).astype(h_ref.dtype)

    row = pl.BlockSpec((tm, d), lambda i: (i, 0))
    vec = pl.BlockSpec((1, d), lambda i: (0, 0))
    return pl.pallas_call(
        body, grid=(s // tm,),
        in_specs=[pl.BlockSpec((tm, k), lambda i: (i, 0)), pl.BlockSpec((k, d), lambda i: (0, 0)), vec, row, vec],
        out_specs=[row, row, row], out_shape=[SDS((s, d), F32), SDS((s, d), F32), SDS((s, d), BF16)], name=name,
        compiler_params=_cp(("parallel",)))(a, w, g, xres, g_next)


def proj_norm_loss(a, w, g, xres, target, *, tm=512, name):
    s, k = a.shape
    d = w.shape[1]
    n = s // tm

    def body(a_ref, w_ref, g_ref, x_ref, t_ref, do_ref, dy_ref, dg_ref, l_ref, acc):
        i = pl.program_id(0)

        @pl.when(i == 0)
        def _():
            acc[...] = jnp.zeros_like(acc)
            l_ref[...] = jnp.zeros_like(l_ref)

        y = _nn(a_ref[...], w_ref[...])
        inv = lax.rsqrt(jnp.mean(y * y, axis=-1, keepdims=True) + RMS_EPS)
        yh = y * inv
        err = x_ref[...] + yh * g_ref[...] - t_ref[...]
        dout = err * (1.0 / d)
        do_ref[...] = dout
        l_ref[...] += jnp.sum(jnp.sum(err * err, axis=1, keepdims=True), axis=0, keepdims=True)
        dy, part = _norm_bwd_rows(dout, yh, inv, g_ref[...])
        dy_ref[...] = dy.astype(dy_ref.dtype)
        acc[...] += part

        @pl.when(i == n - 1)
        def _():
            dg_ref[...] = jnp.sum(acc[...], axis=0, keepdims=True)

    row = pl.BlockSpec((tm, d), lambda i: (i, 0))
    vec = pl.BlockSpec((1, d), lambda i: (0, 0))
    return pl.pallas_call(
        body, grid=(n,),
        in_specs=[pl.BlockSpec((tm, k), lambda i: (i, 0)), pl.BlockSpec((k, d), lambda i: (0, 0)), vec, row, row],
        out_specs=[row, row, vec, pl.BlockSpec((1, 1), lambda i: (0, 0))],
        out_shape=[SDS((s, d), F32), SDS((s, d), BF16), SDS((1, d), F32), SDS((1, 1), F32)],
        scratch_shapes=[pltpu.VMEM((8, d), F32)], name=name, compiler_params=_cp(("arbitrary",)))(a, w, g, xres, target)


def mm_norm_bwd(a_views, b_views, stages, exchange=(), *, tm=256, name):
    n_p, n_s, ne = len(a_views), len(stages), len(exchange)
    s = a_views[0][0].shape[0]
    d = b_views[0][0].shape[0]
    n = s // tm
    has_res = [st[2] is not None for st in stages]

    def body(*refs):
        a_refs, b_refs = refs[:n_p], refs[n_p:2 * n_p]
        at = 2 * n_p
        st_refs = []
        for k in range(n_s):
            cnt = 3 if has_res[k] else 2
            st_refs.append(refs[at:at + cnt])
            at += cnt
        e_ins = refs[at:at + ne]
        at += ne
        dx_refs, dg_refs = refs[at:at + n_s], refs[at + n_s:at + 2 * n_s]
        at += 2 * n_s
        e_outs = refs[at:at + ne]
        at += ne
        accs = refs[at:at + n_s]
        comm = (e_ins, e_outs) + tuple(refs[at + n_s:])
        i = pl.program_id(0)

        @pl.when(i == 0)
        def _():
            for acc in accs:
                acc[...] = jnp.zeros_like(acc)
            if ne:
                _to_chips_start(*comm)

        dh = None
        for p in range(n_p):
            part = _nt(a_refs[p][...].astype(BF16), b_refs[p][...].astype(BF16))
            dh = part if dh is None else dh + part
        for k in range(n_s):
            xv = st_refs[k][0][...]
            inv = lax.rsqrt(jnp.mean(xv * xv, axis=-1, keepdims=True) + RMS_EPS)
            dx, part = _norm_bwd_rows(dh, xv * inv, inv, st_refs[k][1][...])
            if has_res[k]:
                dx = dx + st_refs[k][2][...]
            dx_refs[k][...] = dx.astype(dx_refs[k].dtype)
            accs[k][...] += part
            dh = dx

        @pl.when(i == n - 1)
        def _():
            for k in range(n_s):
                dg_refs[k][...] = jnp.sum(accs[k][...], axis=0, keepdims=True)
            if ne:
                _to_chips_finish(*comm)

    row = pl.BlockSpec((tm, d), lambda i: (i, 0))
    vec = pl.BlockSpec((1, d), lambda i: (0, 0))
    in_specs, args = [], []
    for arr, w, blk in a_views:
        in_specs.append(pl.BlockSpec((tm, w), functools.partial(lambda i, blk: (i, blk), blk=blk)))
        args.append(arr)
    for arr, w, blk in b_views:
        in_specs.append(pl.BlockSpec((d, w), functools.partial(lambda i, blk: (0, blk), blk=blk)))
        args.append(arr)
    for x, g, res, _ in stages:
        in_specs += [row, vec] + ([row] if res is not None else [])
        args += [x, g] + ([res] if res is not None else [])
    return pl.pallas_call(
        body, grid=(n,), in_specs=in_specs + [ANY] * ne,
        out_specs=[row] * n_s + [vec] * n_s + [ANY] * ne,
        out_shape=[SDS((s, d), st[3]) for st in stages] + [SDS((1, d), F32)] * n_s + _to_chips_shapes(exchange),
        scratch_shapes=[pltpu.VMEM((8, d), F32)] * n_s + (_to_chips_sems(ne) if ne else []), name=name,
        compiler_params=_cp(("arbitrary",)))(*args, *exchange)


def _split3(v):
    hi = v.astype(BF16).astype(F32)
    r = v - hi
    mid = r.astype(BF16).astype(F32)
    lo = (r - mid).astype(BF16).astype(F32)
    return hi, mid, lo


def _tri(n, upper):
    r = np.arange(n)
    m = (r[:, None] <= r[None, :]) if upper else (r[:, None] >= r[None, :])
    return jnp.asarray(m.astype(np.float32))


def fox_prep(z, fa, bfo, *, tb=512):
    s = z.shape[0]
    n = s // tb

    def body(q_ref, k_ref, v_ref, fa_ref, b_ref, tri_ref, qa_ref, ka_ref, va_ref, carry):
        @pl.when(pl.program_id(0) == 0)
        def _():
            carry[...] = jnp.zeros_like(carry)

        xv = fa_ref[...] + b_ref[...]
        logf = jnp.minimum(xv, 0.0) - jnp.log(1.0 + jnp.exp(-jnp.abs(xv)))
        csum = jnp.dot(tri_ref[...], logf, preferred_element_type=F32, precision=lax.Precision.HIGHEST) + carry[0:1, :]
        carry[0:1, :] = csum[tb - 1:tb, :]
        lane = _lane((tb, LANES))
        for h in range(N_HEADS):
            hi, mid, lo = _split3(csum[:, h:h + 1])
            pair = (h // 2) * LANES
            qv = q_ref[:, pair:pair + LANES].astype(F32)
            kv = k_ref[:, pair:pair + LANES].astype(F32)
            vv = v_ref[:, pair:pair + LANES].astype(F32)
            if h % 2:
                qv = pltpu.roll(qv, 64, axis=1)
                kv = pltpu.roll(kv, 64, axis=1)
                vv = pltpu.roll(vv, 64, axis=1)
            va_ref[:, h * LANES:(h + 1) * LANES] = jnp.where(lane < 64, vv, jnp.where(lane == 64, 1.0, 0.0)).astype(BF16)
            one = jnp.where((lane >= 67) & (lane < 70), 1.0, 0.0)
            q_x = jnp.where(lane == 64, hi, jnp.where(lane == 65, mid, jnp.where(lane == 66, lo, one)))
            one = jnp.where((lane >= 64) & (lane < 67), 1.0, 0.0)
            k_x = jnp.where(lane == 67, -hi, jnp.where(lane == 68, -mid, jnp.where(lane == 69, -lo, one)))
            qa_ref[:, h * LANES:(h + 1) * LANES] = jnp.where(lane < 64, qv * QK_SCALE, q_x).astype(BF16)
            ka_ref[:, h * LANES:(h + 1) * LANES] = jnp.where(lane < 64, kv, k_x).astype(BF16)

    return pl.pallas_call(
        body, grid=(n,),
        in_specs=[pl.BlockSpec((tb, ATT_W), lambda i: (i, Z_QA)), pl.BlockSpec((tb, ATT_W), lambda i: (i, Z_KA)),
                  pl.BlockSpec((tb, ATT_W), lambda i: (i, Z_VA)),
                  pl.BlockSpec((tb, LANES), lambda i: (i, 0)), pl.BlockSpec((1, LANES), lambda i: (0, 0)),
                  pl.BlockSpec((tb, tb), lambda i: (0, 0))],
        out_specs=[pl.BlockSpec((tb, N_HEADS * LANES), lambda i: (i, 0))] * 3,
        out_shape=[SDS((s, N_HEADS * LANES), BF16)] * 3,
        scratch_shapes=[pltpu.VMEM((8, LANES), F32)],
        name="fox_prep", compiler_params=_cp(("arbitrary",)))(z, z, z, fa, bfo, _tri(tb, False))


def _causal_pairs(n, k_major):
    if k_major:
        pairs = [(qi, kj) for kj in range(n) for qi in range(kj, n)]
    else:
        pairs = [(qi, kj) for qi in range(n) for kj in range(qi + 1)]
    return (jnp.asarray([p[0] for p in pairs], jnp.int32), jnp.asarray([p[1] for p in pairs], jnp.int32), len(pairs))


def fox_fwd(q_aug, k_aug, v_aug, gather=(), halved=(), *, t=1024, hps=4):
    s = v_aug.shape[0]
    qi_arr, kj_arr, n_pairs = _causal_pairs(s // t, False)
    ng = len(gather)
    n_groups = N_HEADS // hps

    def body(qi_ref, kj_ref, q_ref, k_ref, v_ref, *rest):
        g_ins, (o_ref, lse_ref), g_outs = rest[:ng], rest[ng:ng + 2], rest[ng + 2:2 * ng + 2]
        m_scr, acc_scr = rest[2 * ng + 2:2 * ng + 4]
        comm = (g_ins, g_outs) + tuple(rest[2 * ng + 4:]) + (list(halved),)
        step = pl.program_id(1)
        qi = qi_ref[step]
        kj = kj_ref[step]
        if ng:
            @pl.when((pl.program_id(0) == 0) & (step == 0))
            def _():
                _allgather_start(*comm)

        @pl.when(kj == 0)
        def _():
            m_scr[...] = jnp.full_like(m_scr, NEG)
            acc_scr[...] = jnp.zeros_like(acc_scr)

        def update(masked):
            for i in range(hps):
                sc = _nt(q_ref[:, i * LANES:(i + 1) * LANES], k_ref[:, i * LANES:(i + 1) * LANES])
                if masked:
                    sc = jnp.where(_row((t, t)) >= _lane((t, t)), sc, NEG)
                m_prev = m_scr[i]
                m_new = jnp.maximum(m_prev, jnp.max(sc, axis=-1, keepdims=True))
                p = jnp.exp((sc - jnp.tile(m_new, (1, t // LANES))).astype(BF16))
                acc_scr[i] = jnp.exp(m_prev - m_new) * acc_scr[i] + _nn(p, v_ref[:, i * LANES:(i + 1) * LANES])
                m_scr[i] = m_new

        @pl.when(kj < qi)
        def _():
            update(False)

        @pl.when(kj == qi)
        def _():
            update(True)
            lane = _lane((t, LANES))
            for pr in range(hps // 2):
                den = [acc_scr[2 * pr + i][:, 64:65] for i in range(2)]
                o_ref[:, pr * LANES:(pr + 1) * LANES] = jnp.where(
                    lane < 64, acc_scr[2 * pr] / den[0], pltpu.roll(acc_scr[2 * pr + 1] / den[1], 64, axis=1)).astype(o_ref.dtype)
                lse_ref[:, pr * LANES:(pr + 1) * LANES] = jnp.where(
                    lane < 64, m_scr[2 * pr] + jnp.log(den[0]), m_scr[2 * pr + 1] + jnp.log(den[1]))

        if ng:
            @pl.when((pl.program_id(0) == n_groups - 1) & (step == n_pairs - 1))
            def _():
                _allgather_finish(*comm)

    wide = hps * LANES
    grid_spec = pltpu.PrefetchScalarGridSpec(
        num_scalar_prefetch=2, grid=(n_groups, n_pairs),
        in_specs=[pl.BlockSpec((t, wide), lambda hg, st, qi, kj: (qi[st], hg)),
                  pl.BlockSpec((t, wide), lambda hg, st, qi, kj: (kj[st], hg)),
                  pl.BlockSpec((t, wide), lambda hg, st, qi, kj: (kj[st], hg))] + [ANY] * ng,
        out_specs=[pl.BlockSpec((t, wide // 2), lambda hg, st, qi, kj: (qi[st], hg))] * 2 + [ANY] * ng,
        scratch_shapes=[pltpu.VMEM((hps, t, LANES), F32)] * 2 + (_allgather_sems(ng) if ng else []))
    return pl.pallas_call(
        body, grid_spec=grid_spec, out_shape=[SDS((s, ATT_W), BF16), SDS((s, ATT_W), F32)] + _allgather_shapes(gather),
        name="fox_fwd", compiler_params=_cp(("arbitrary", "arbitrary")))(qi_arr, kj_arr, q_aug, k_aug, v_aug, *gather)


def fox_bwd(q_aug, k_aug, z, dy, lse, dd, exchange=(), kind="to_chips", *, t=1024, hps=4):
    s = z.shape[0]
    qi_arr, kj_arr, n_pairs = _causal_pairs(s // t, True)
    ne = len(exchange)
    n_groups = N_HEADS // hps
    x_shapes, x_sems, x_start, x_finish = EXCHANGES[kind]

    def body(qi_ref, kj_ref, q_ref, k_ref, v_ref, do_ref, lse_ref, dd_ref, *rest):
        e_ins, (dq_ref, dk_ref, dv_ref), e_outs = rest[:ne], rest[ne:ne + 3], rest[ne + 3:2 * ne + 3]
        comm = (e_ins, e_outs) + tuple(rest[2 * ne + 3:])
        step = pl.program_id(1)
        qi = qi_ref[step]
        kj = kj_ref[step]
        if ne:
            @pl.when((pl.program_id(0) == 0) & (step == 0))
            def _():
                x_start(*comm)

        @pl.when(step == 0)
        def _():
            dq_ref[...] = jnp.zeros_like(dq_ref)

        @pl.when(qi == kj)
        def _():
            dk_ref[...] = jnp.zeros_like(dk_ref)
            dv_ref[...] = jnp.zeros_like(dv_ref)

        def update(masked):
            lane = _lane((t, LANES))
            rows = pl.ds(pl.multiple_of(qi * t, t), t)
            for pr in range(hps // 2):
                pair = slice(pr * LANES, (pr + 1) * LANES)
                dov = do_ref[:, pair]
                dv_new = None
                for i in range(2):
                    head = (lane < 64) if i == 0 else (lane >= 64)
                    own = slice((2 * pr + i) * LANES, (2 * pr + i + 1) * LANES)
                    col = slice(pr * LANES + i * 64, pr * LANES + i * 64 + 1)
                    qv = q_ref[:, own]
                    kv = k_ref[:, own]
                    sc = _nt(qv, kv)
                    if masked:
                        sc = jnp.where(_row((t, t)) >= _lane((t, t)), sc, NEG)
                    p = jnp.exp(sc - lse_ref[:, col])
                    dp = _nt(jnp.where(head, dov, jnp.zeros_like(dov)), v_ref[:, pair])
                    ds = (p * (dp - dd_ref[:, col])).astype(BF16)
                    dq_ref[rows, own] += _nn(ds, kv)
                    dk_ref[:, own] += _tn(ds, qv)
                    dvi = _tn(p.astype(BF16), dov)
                    dv_new = dvi if dv_new is None else jnp.where(head, dvi, dv_new)
                dv_ref[:, pair] += dv_new

        @pl.when(kj < qi)
        def _():
            update(False)

        @pl.when(kj == qi)
        def _():
            update(True)

        if ne:
            @pl.when((pl.program_id(0) == n_groups - 1) & (step == n_pairs - 1))
            def _():
                x_finish(*comm)

    wide, half = hps * LANES, hps // 2 * LANES
    v_blk = Z_VA * ATT_W // half
    grid_spec = pltpu.PrefetchScalarGridSpec(
        num_scalar_prefetch=2, grid=(n_groups, n_pairs),
        in_specs=[pl.BlockSpec((t, wide), lambda hg, st, qi, kj: (qi[st], hg)),
                  pl.BlockSpec((t, wide), lambda hg, st, qi, kj: (kj[st], hg)),
                  pl.BlockSpec((t, half), lambda hg, st, qi, kj: (kj[st], v_blk + hg)),
                  pl.BlockSpec((t, half), lambda hg, st, qi, kj: (qi[st], hg)),
                  pl.BlockSpec((t, half), lambda hg, st, qi, kj: (qi[st], hg)),
                  pl.BlockSpec((t, half), lambda hg, st, qi, kj: (qi[st], hg))] + [ANY] * ne,
        out_specs=[pl.BlockSpec((s, wide), lambda hg, st, qi, kj: (0, hg)),
                   pl.BlockSpec((t, wide), lambda hg, st, qi, kj: (kj[st], hg)),
                   pl.BlockSpec((t, half), lambda hg, st, qi, kj: (kj[st], hg))] + [ANY] * ne,
        scratch_shapes=x_sems(ne) if ne else [])
    return pl.pallas_call(
        body, grid_spec=grid_spec,
        out_shape=[SDS((s, N_HEADS * LANES), F32), SDS((s, N_HEADS * LANES), F32), SDS((s, ATT_W), F32)]
        + x_shapes(exchange),
        name="fox_bwd", compiler_params=_cp(("arbitrary", "arbitrary")))(qi_arr, kj_arr, q_aug, k_aug, z, dy, lse, dd, *exchange)


def fox_post(dq_aug, dk_aug, dv, fa, bfo, *, tb=512):
    s = dv.shape[0]
    n = s // tb

    def body(dq_ref, dk_ref, dv_ref, fa_ref, b_ref, tri_ref, dz_ref, dfa_ref, gb_ref, carry, acc):
        i = pl.program_id(0)

        @pl.when(i == 0)
        def _():
            carry[...] = jnp.zeros_like(carry)
            acc[...] = jnp.zeros_like(acc)

        lane = _lane((tb, LANES))
        d_f = jnp.zeros((tb, LANES), F32)
        for h in range(N_HEADS):
            col = dq_ref[:, h * LANES + 64:h * LANES + 65] - dk_ref[:, h * LANES + 67:h * LANES + 68]
            d_f = jnp.where(lane == h, col, d_f)
        suffix = jnp.dot(tri_ref[...], d_f, preferred_element_type=F32, precision=lax.Precision.HIGHEST) + carry[0:1, :]
        carry[0:1, :] = suffix[0:1, :]
        xv = fa_ref[...] + b_ref[...]
        dx = suffix * (1.0 / (1.0 + jnp.exp(xv)))
        dfa_ref[...] = dx.astype(dfa_ref.dtype)
        acc[...] += jnp.sum(dx.reshape(tb // 8, 8, LANES), axis=0)
        for hp in range(4):
            for src, off, scale in ((dq_ref, 0, QK_SCALE), (dk_ref, ATT_W, 1.0)):
                even = src[:, (2 * hp) * LANES:(2 * hp + 1) * LANES]
                odd = pltpu.roll(src[:, (2 * hp + 1) * LANES:(2 * hp + 2) * LANES], 64, axis=1)
                dz_ref[:, off + hp * LANES:off + (hp + 1) * LANES] = (jnp.where(lane < 64, even, odd) * scale).astype(BF16)
        dz_ref[:, 2 * ATT_W:3 * ATT_W] = dv_ref[...].astype(BF16)

        @pl.when(i == n - 1)
        def _():
            gb_ref[...] = jnp.sum(acc[...], axis=0, keepdims=True)

    rev = lambda i: (n - 1 - i, 0)
    return pl.pallas_call(
        body, grid=(n,),
        in_specs=[pl.BlockSpec((tb, N_HEADS * LANES), rev), pl.BlockSpec((tb, N_HEADS * LANES), rev),
                  pl.BlockSpec((tb, ATT_W), rev), pl.BlockSpec((tb, LANES), rev),
                  pl.BlockSpec((1, LANES), lambda i: (0, 0)), pl.BlockSpec((tb, tb), lambda i: (0, 0))],
        out_specs=[pl.BlockSpec((tb, 3 * ATT_W), rev), pl.BlockSpec((tb, LANES), rev),
                   pl.BlockSpec((1, LANES), lambda i: (0, 0))],
        out_shape=[SDS((s, 3 * ATT_W), BF16), SDS((s, LANES), BF16), SDS((1, LANES), F32)],
        scratch_shapes=[pltpu.VMEM((8, LANES), F32), pltpu.VMEM((8, LANES), F32)],
        name="fox_post", compiler_params=_cp(("arbitrary",)))(dq_aug, dk_aug, dv, fa, bfo, _tri(tb, True))


def rope_cos_sin(s):
    half = ROPE_DIM // 2
    inv_freq = ROPE_THETA ** (-jnp.arange(half, dtype=F32) * 2.0 / ROPE_DIM)
    ang = jnp.arange(s, dtype=F32)[:, None] * inv_freq[None, :]
    return jnp.tile(jnp.cos(ang), (1, LANES // half)), jnp.tile(jnp.sin(ang), (1, LANES // half))


def _rotate(x, cos, sin, sign):
    l64 = _lane(x.shape) & (HEAD_DIM - 1)
    first = l64 < ROPE_DIM // 2
    second = (l64 >= ROPE_DIM // 2) & (l64 < ROPE_DIM)
    from_next = jnp.where(first, -sign * sin, 0.0)
    from_prev = jnp.where(second, sign * sin, 0.0)
    return (x * jnp.where(first | second, cos, 1.0) + pltpu.roll(x, LANES - 8, axis=1) * from_next
            + pltpu.roll(x, 8, axis=1) * from_prev)


def _dil_rows(base, r):
    if r == 1:
        return pl.ds(pl.multiple_of(base, DIL_BLK), DIL_BLK)
    return pl.ds(base, DIL_BLK, stride=r)


def _dil_block(idx, r, nb):
    shift = nb.bit_length() - 1
    rho = idx >> shift
    n = idx & (nb - 1)
    base = rho + n * (r * DIL_BLK)
    return _dil_rows(base, r), _dil_rows(jnp.maximum(base - r * DIL_BLK, rho), r), n > 0


def _cat(a, b):
    return jnp.concatenate([a, b], axis=0)


def _two_heads(v, first_head):
    zero = jnp.zeros_like(v)
    return _cat(jnp.where(first_head, v, zero), jnp.where(first_head, zero, v))


def _dil_bands():
    b = DIL_BLK
    q = _row((2 * b, 2 * b)) & (b - 1)
    col = _lane((2 * b, 2 * b))
    return (col < b) & (col >= q), (col >= b) & (col - b <= q)


def _dil_load_qkv(zq_ref, zk_ref, zv_ref, cos_ref, sin_ref, q_ref, k_ref, v_ref, *, chunk=512):
    def step(i, carry):
        rows = pl.ds(pl.multiple_of(i * chunk, chunk), chunk)
        cos, sin = cos_ref[rows, :], sin_ref[rows, :]
        q_ref[rows, :] = _rotate(zq_ref[rows, :].astype(F32), cos, sin, 1.0) * QK_SCALE
        k_ref[rows, :] = _rotate(zk_ref[rows, :].astype(F32), cos, sin, 1.0)
        v_ref[rows, :] = zv_ref[rows, :].astype(F32)
        return carry

    lax.fori_loop(0, q_ref.shape[0] // chunk, step, 0)


def dil_fwd_all(z, cos_t, sin_t, *, unroll=32):
    s = z.shape[0]
    b = DIL_BLK
    n_blk = s // b

    def body(zq_ref, zk_ref, zv_ref, cos_ref, sin_ref, o_ref, l_ref, q_ref, k_ref, v_ref):
        _dil_load_qkv(zq_ref, zk_ref, zv_ref, cos_ref, sin_ref, q_ref, k_ref, v_ref)
        first_head = _lane((b, LANES)) < 64
        band_prev, band_cur = _dil_bands()
        for g, (_, r) in enumerate(DIL_PATTERNS):
            nb = n_blk // r

            def group(it, carry, g=g, r=r, nb=nb):
                loaded = []
                kc = vc = None
                for u in range(unroll):
                    rows_c, rows_p, has_prev = _dil_block(it * unroll + u, r, nb)
                    if u % min(nb, unroll):
                        kp, vp = kc, vc
                    else:
                        kp, vp = k_ref[rows_p, :].astype(BF16), v_ref[rows_p, :].astype(BF16)
                    kc, vc = k_ref[rows_c, :].astype(BF16), v_ref[rows_c, :].astype(BF16)
                    state = (o_ref[rows_c, :], l_ref[rows_c, :]) if g else None
                    loaded.append((rows_c, has_prev, [q_ref[rows_c, :].astype(BF16), kp, kc, vp, vc], state))
                done = []
                for rows_c, has_prev, (qv, kp, kc, vp, vc), state in loaded:
                    sc = jnp.where(band_cur | (band_prev & has_prev), _nt(_two_heads(qv, first_head), _cat(kp, kc)), NEG)
                    m = jnp.max(sc, axis=-1, keepdims=True)
                    p = jnp.exp(sc - m)
                    den = jnp.sum(p, axis=-1, keepdims=True)
                    both = _nn(p.astype(BF16), _cat(vp, vc)) / den
                    lse2 = m + jnp.log(den)
                    ov = jnp.where(first_head, both[:b], both[b:])
                    lse = jnp.where(first_head, lse2[:b], lse2[b:])
                    if state is not None:
                        m2 = jnp.maximum(state[1], lse)
                        wp = jnp.exp(state[1] - m2)
                        wn = jnp.exp(lse - m2)
                        ov = (wp * state[0] + wn * ov) / (wp + wn)
                        lse = m2 + jnp.log(wp + wn)
                    done.append((rows_c, ov, lse))
                for rows_c, ov, lse in done:
                    o_ref[rows_c, :] = ov
                    l_ref[rows_c, :] = lse
                return carry

            lax.fori_loop(0, n_blk // unroll, group, 0)

    col_blk = lambda k: pl.BlockSpec((s, LANES), lambda hp: (0, 4 * k + hp))
    table = pl.BlockSpec((s, LANES), lambda hp: (0, 0))
    out = pl.BlockSpec((s, LANES), lambda hp: (0, hp))
    return pl.pallas_call(
        body, grid=(4,), in_specs=[col_blk(Z_QB), col_blk(Z_KB), col_blk(Z_VB), table, table], out_specs=[out, out],
        out_shape=[SDS((s, ATT_W), F32)] * 2, scratch_shapes=[pltpu.VMEM((s, LANES), F32)] * 3, name="dil_fwd",
        compiler_params=_cp(("parallel",)))(z, z, z, cos_t, sin_t)


def dil_bwd_all(z, cos_t, sin_t, dy, lse, y, exchange=(), kind="to_chips", *, unroll=16):
    s = z.shape[0]
    b = DIL_BLK
    n_blk = s // b
    ne = len(exchange)
    x_shapes, x_sems, x_start, x_finish = EXCHANGES[kind]

    def body(zq_ref, zk_ref, zv_ref, cos_ref, sin_ref, do_ref, l_ref, y_ref, *rest):
        e_ins, (gq_ref, gk_ref, gv_ref), e_outs = rest[:ne], rest[ne:ne + 3], rest[ne + 3:2 * ne + 3]
        q_ref, k_ref, v_ref, dq_ref, dk_ref, dv_ref = rest[2 * ne + 3:2 * ne + 9]
        comm = (e_ins, e_outs) + tuple(rest[2 * ne + 9:])
        if ne:
            @pl.when(pl.program_id(0) == 0)
            def _():
                x_start(*comm)

        _dil_load_qkv(zq_ref, zk_ref, zv_ref, cos_ref, sin_ref, q_ref, k_ref, v_ref)
        dq_ref[...] = jnp.zeros_like(dq_ref)
        dk_ref[...] = jnp.zeros_like(dk_ref)
        dv_ref[...] = jnp.zeros_like(dv_ref)
        first_head = _lane((b, LANES)) < 64
        band_prev, band_cur = _dil_bands()
        for _, r in DIL_PATTERNS:
            nb = n_blk // r

            def group(it, carry, r=r, nb=nb):
                loaded = []
                kc = vc = None
                for u in range(unroll):
                    rows_c, rows_p, has_prev = _dil_block(it * unroll + u, r, nb)
                    if u % min(nb, unroll):
                        kp, vp = kc, vc
                    else:
                        kp, vp = k_ref[rows_p, :].astype(BF16), v_ref[rows_p, :].astype(BF16)
                    kc, vc = k_ref[rows_c, :].astype(BF16), v_ref[rows_c, :].astype(BF16)
                    vals = [q_ref[rows_c, :].astype(BF16), kp, kc, vp, vc, do_ref[rows_c, :], l_ref[rows_c, :], y_ref[rows_c, :]]
                    loaded.append((rows_c, rows_p, has_prev, vals))
                done = []
                for rows_c, rows_p, has_prev, (qv, kp, kc, vp, vc, dof, lv, yv) in loaded:
                    q2 = _two_heads(qv, first_head)
                    do2 = _two_heads(dof.astype(BF16), first_head)
                    kcat, vcat = _cat(kp, kc), _cat(vp, vc)
                    lse2 = _cat(lv[:, 0:1], lv[:, 64:65])
                    dd2 = jnp.sum(_two_heads(dof * yv, first_head), axis=-1, keepdims=True)
                    p = jnp.exp(jnp.where(band_cur | (band_prev & has_prev), _nt(q2, kcat), NEG) - lse2)
                    ds = (p * (_nt(do2, vcat) - dd2)).astype(BF16)
                    dq2 = _nn(ds, kcat)
                    dkcat = _tn(ds, q2)
                    dvcat = _tn(p.astype(BF16), do2)
                    done.append((rows_c, rows_p, (jnp.where(first_head, dq2[:b], dq2[b:]), dkcat[:b], dkcat[b:],
                                                  dvcat[:b], dvcat[b:])))
                for rows_c, rows_p, (dq, dk_p, dk_c, dv_p, dv_c) in done:
                    dq_ref[rows_c, :] += dq
                    dk_ref[rows_p, :] += dk_p
                    dk_ref[rows_c, :] += dk_c
                    dv_ref[rows_p, :] += dv_p
                    dv_ref[rows_c, :] += dv_c
                return carry

            lax.fori_loop(0, n_blk // unroll, group, 0)

        def finish(i, carry, chunk=512):
            rows = pl.ds(pl.multiple_of(i * chunk, chunk), chunk)
            cos, sin = cos_ref[rows, :], sin_ref[rows, :]
            gq_ref[rows, :] = (_rotate(dq_ref[rows, :], cos, sin, -1.0) * QK_SCALE).astype(BF16)
            gk_ref[rows, :] = _rotate(dk_ref[rows, :], cos, sin, -1.0).astype(BF16)
            gv_ref[rows, :] = dv_ref[rows, :].astype(BF16)
            return carry

        lax.fori_loop(0, s // 512, finish, 0)
        if ne:
            @pl.when(pl.program_id(0) == 3)
            def _():
                x_finish(*comm)

    col_blk = lambda k: pl.BlockSpec((s, LANES), lambda hp: (0, 4 * k + hp))
    table = pl.BlockSpec((s, LANES), lambda hp: (0, 0))
    nat = pl.BlockSpec((s, LANES), lambda hp: (0, hp))
    return pl.pallas_call(
        body, grid=(4,), in_specs=[col_blk(Z_QB), col_blk(Z_KB), col_blk(Z_VB), table, table, nat, nat, nat] + [ANY] * ne,
        out_specs=[nat, nat, nat] + [ANY] * ne, out_shape=[SDS((s, ATT_W), BF16)] * 3 + x_shapes(exchange),
        scratch_shapes=[pltpu.VMEM((s, LANES), F32)] * 6 + (x_sems(ne) if ne else []), name="dil_bwd",
        compiler_params=_cp(("arbitrary",)))(z, z, z, cos_t, sin_t, dy, lse, y, *exchange)


def _sigmoid(v):
    return 1.0 / (1.0 + jnp.exp(-v))


def gate_mix(ya, yb, wa, wb, z, *, tm=2048, tn=512):
    s = ya.shape[0]
    d = wa.shape[1]
    ga_blk = 3 * ATT_W * 2 // tn
    gb_blk = ga_blk + d // tn

    def body(ya_ref, yb_ref, wa_ref, wb_ref, ga_ref, gb_ref, pa_ref, pb_ref, mx_ref):
        pa = _nn(ya_ref[...], wa_ref[...])
        pb = _nn(yb_ref[...].astype(BF16), wb_ref[...])
        pa_ref[...] = pa.astype(BF16)
        pb_ref[...] = pb.astype(BF16)
        mx_ref[...] = (_sigmoid(ga_ref[...].astype(F32)) * pa + _sigmoid(gb_ref[...].astype(F32)) * pb).astype(BF16)

    out = pl.BlockSpec((tm, tn), lambda i, j: (i, j))
    return pl.pallas_call(
        body, grid=(s // tm, d // tn),
        in_specs=[pl.BlockSpec((tm, ATT_W), lambda i, j: (i, 0)), pl.BlockSpec((tm, ATT_W), lambda i, j: (i, 0)),
                  pl.BlockSpec((ATT_W, tn), lambda i, j: (0, j)), pl.BlockSpec((ATT_W, tn), lambda i, j: (0, j)),
                  pl.BlockSpec((tm, tn), lambda i, j: (i, ga_blk + j)), pl.BlockSpec((tm, tn), lambda i, j: (i, gb_blk + j))],
        out_specs=[out, out, out], out_shape=[SDS((s, d), BF16)] * 3, name="gate_mix",
        compiler_params=_cp(("parallel", "parallel")))(ya, yb, wa, wb, z, z)


def mix_bwd(dy, w_o, z, pa, pb, wo_a, wo_b, ya, *, tm=512):
    s, d = dy.shape

    def body(dy_ref, wo_ref, ga_ref, gb_ref, pa_ref, pb_ref, wa_ref, wb_ref, ya_ref,
             dpa_ref, dpb_ref, dg_ref, dya_ref, dyb_ref, dd_ref):
        dm = _nt(dy_ref[...], wo_ref[...])
        sa = _sigmoid(ga_ref[...].astype(F32))
        sb = _sigmoid(gb_ref[...].astype(F32))
        dpa = (dm * sa).astype(BF16)
        dpb = (dm * sb).astype(BF16)
        dpa_ref[...] = dpa
        dpb_ref[...] = dpb
        dg_ref[:, 0:d] = (dm * pa_ref[...].astype(F32) * sa * (1.0 - sa)).astype(BF16)
        dg_ref[:, d:2 * d] = (dm * pb_ref[...].astype(F32) * sb * (1.0 - sb)).astype(BF16)
        dya = _nt(dpa, wa_ref[...]).astype(BF16)
        dya_ref[...] = dya
        dyb_ref[...] = _nt(dpb, wb_ref[...])
        lane = _lane((tm, LANES))
        for pr in range(ATT_W // LANES):
            pair = slice(pr * LANES, (pr + 1) * LANES)
            prod = dya[:, pair].astype(F32) * ya_ref[:, pair].astype(F32)
            lo = jnp.sum(jnp.where(lane < 64, prod, 0.0), axis=-1, keepdims=True)
            hi = jnp.sum(jnp.where(lane >= 64, prod, 0.0), axis=-1, keepdims=True)
            dd_ref[:, pair] = jnp.where(lane < 64, lo, hi)

    row = pl.BlockSpec((tm, d), lambda i: (i, 0))
    att = pl.BlockSpec((tm, ATT_W), lambda i: (i, 0))
    whole = lambda a: pl.BlockSpec(a.shape, lambda i: (0, 0))
    return pl.pallas_call(
        body, grid=(s // tm,),
        in_specs=[row, whole(w_o), pl.BlockSpec((tm, d), lambda i: (i, 3)), pl.BlockSpec((tm, d), lambda i: (i, 4)), row, row,
                  whole(wo_a), whole(wo_b), att],
        out_specs=[row, row, pl.BlockSpec((tm, 2 * d), lambda i: (i, 0)), att, att, att],
        out_shape=[SDS((s, d), BF16), SDS((s, d), BF16), SDS((s, 2 * d), BF16), SDS((s, ATT_W), BF16),
                   SDS((s, ATT_W), F32), SDS((s, ATT_W), F32)], name="mix_bwd",
        compiler_params=_cp(("parallel",)))(dy, w_o, z, z, pa, pb, wo_a, wo_b, ya)


GELU_C = math.sqrt(2.0 / math.pi)


def _gelu_parts(a):
    a2 = a * a
    th = jnp.tanh(a * (GELU_C + (GELU_C * 0.044715) * a2))
    half = 0.5 * a
    gelu = half + half * th
    dgelu = (0.5 + 0.5 * th) + half * (1.0 - th * th) * (GELU_C + (3.0 * GELU_C * 0.044715) * a2)
    return gelu, dgelu


def _causal_taps(u, before):
    row = _row(u.shape)
    r1 = jnp.where(row == 0, before[7:8, :], pltpu.roll(u, 1, axis=0))
    r2 = jnp.where(row == 0, before[6:7, :], jnp.where(row == 1, before[7:8, :], pltpu.roll(u, 2, axis=0)))
    return r1, r2


def ffn_up(h, wa, wb, cw, cb, *, tm=2048, tn=256):
    s, d = h.shape
    f = wa.shape[1]
    nj = f // tn

    def body(h_ref, wa_ref, wb_ref, cwa_ref, cwb_ref, cba_ref, cbb_ref, ua_ref, ub_ref, ca_ref, cbo_ref, m_ref, carry):
        @pl.when(pl.program_id(1) == 0)
        def _():
            carry[...] = jnp.zeros_like(carry)

        conv = []
        for k, (w_ref, cw_ref, cb_ref, u_ref, c_ref) in enumerate(((wa_ref, cwa_ref, cba_ref, ua_ref, ca_ref),
                                                                   (wb_ref, cwb_ref, cbb_ref, ub_ref, cbo_ref))):
            u16 = _nn(h_ref[...], w_ref[...]).astype(BF16)
            u_ref[...] = u16
            u = u16.astype(F32)
            r1, r2 = _causal_taps(u, carry[k])
            carry[k] = u[tm - 8:tm, :]
            c16 = (cw_ref[0:1, :] * r2 + cw_ref[1:2, :] * r1 + cw_ref[2:3, :] * u + cb_ref[...]).astype(BF16)
            c_ref[...] = c16
            conv.append(c16.astype(F32))
        m_ref[...] = (_gelu_parts(conv[0])[0] * conv[1]).astype(BF16)

    out = pl.BlockSpec((tm, tn), lambda j, i: (i, j))
    return pl.pallas_call(
        body, grid=(nj, s // tm),
        in_specs=[pl.BlockSpec((tm, d), lambda j, i: (i, 0)),
                  pl.BlockSpec((d, tn), lambda j, i: (0, j)), pl.BlockSpec((d, tn), lambda j, i: (0, j)),
                  pl.BlockSpec((3, tn), lambda j, i: (0, j)), pl.BlockSpec((3, tn), lambda j, i: (0, nj + j)),
                  pl.BlockSpec((1, tn), lambda j, i: (0, j)), pl.BlockSpec((1, tn), lambda j, i: (0, nj + j))],
        out_specs=[out] * 5, out_shape=[SDS((s, f), BF16)] * 5,
        scratch_shapes=[pltpu.VMEM((2, 8, tn), F32)], name="ffn_up",
        compiler_params=_cp(("parallel", "arbitrary")))(h, wa, wb, cw, cw, cb, cb)


def ffn_bwd(dm, ua, ub, ca, cbo, cw, *, tm=2048, tn=256):
    s, f = dm.shape
    nj = f // tn
    ni = s // tm

    def body(dm_ref, ua_ref, ub_ref, ca_ref, cbo_ref, cwa_ref, cwb_ref, dua_ref, dub_ref, ga_ref, gb_ref, carry):
        @pl.when(pl.program_id(1) == 0)
        def _():
            carry[...] = jnp.zeros_like(carry)
            ga_ref[...] = jnp.zeros_like(ga_ref)
            gb_ref[...] = jnp.zeros_like(gb_ref)

        row = _row((tm, tn))
        dmv = dm_ref[...].astype(F32)
        gelu, dgelu = _gelu_parts(ca_ref[...].astype(F32))
        dcs = (dmv * cbo_ref[...].astype(F32) * dgelu, dmv * gelu)
        for k, (dc, u_ref, cw_ref, du_ref, g_ref) in enumerate(((dcs[0], ua_ref, cwa_ref, dua_ref, ga_ref),
                                                                (dcs[1], ub_ref, cwb_ref, dub_ref, gb_ref))):
            u = u_ref[...].astype(F32)
            after = carry[k]
            n1 = jnp.where(row == tm - 1, after[0:1, :], pltpu.roll(dc, tm - 1, axis=0))
            n2 = jnp.where(row == tm - 2, after[0:1, :], jnp.where(row == tm - 1, after[1:2, :], pltpu.roll(dc, tm - 2, axis=0)))
            g_ref[0:1, :] += jnp.sum(n2 * u, axis=0, keepdims=True)
            g_ref[1:2, :] += jnp.sum(n1 * u, axis=0, keepdims=True)
            g_ref[2:3, :] += jnp.sum(dc * u, axis=0, keepdims=True)
            g_ref[3:4, :] += jnp.sum(dc, axis=0, keepdims=True)
            du_ref[...] = (cw_ref[2:3, :] * dc + cw_ref[1:2, :] * n1 + cw_ref[0:1, :] * n2).astype(BF16)
            carry[k] = dc[0:8, :]

    tile = pl.BlockSpec((tm, tn), lambda j, i: (ni - 1 - i, j))
    gspec = pl.BlockSpec((8, tn), lambda j, i: (0, j))
    return pl.pallas_call(
        body, grid=(nj, ni),
        in_specs=[tile] * 5 + [pl.BlockSpec((3, tn), lambda j, i: (0, j)), pl.BlockSpec((3, tn), lambda j, i: (0, nj + j))],
        out_specs=[tile, tile, gspec, gspec],
        out_shape=[SDS((s, f), BF16), SDS((s, f), BF16), SDS((8, f), F32), SDS((8, f), F32)],
        scratch_shapes=[pltpu.VMEM((2, 8, tn), F32)], name="ffn_bwd",
        compiler_params=_cp(("parallel", "arbitrary")))(dm, ua, ub, ca, cbo, cw, cw)


def adamw(w, g, m, v, *, name, tr=None):
    r = w.shape[0]
    rest = w.shape[1:]
    if tr is None:
        tr = r
        for cand in (256, 128, 64, 32, 16, 8):
            if r % cand == 0:
                tr = cand
                break

    def body(w_ref, g_ref, m_ref, v_ref, d_ref, nm_ref, nv_ref):
        gv = g_ref[...]
        mn = ADAM_B1 * m_ref[...] + (1.0 - ADAM_B1) * gv
        vn = ADAM_B2 * v_ref[...] + (1.0 - ADAM_B2) * (gv * gv)
        m_hat = mn / (1.0 - ADAM_B1 ** ADAM_STEP)
        v_hat = vn / (1.0 - ADAM_B2 ** ADAM_STEP)
        d_ref[...] = -ADAM_LR * (m_hat / (jnp.sqrt(v_hat) + ADAM_EPS) + ADAM_WD * w_ref[...])
        nm_ref[...] = mn
        nv_ref[...] = vn

    blk = pl.BlockSpec((tr,) + rest, lambda i: (i,) + (0,) * len(rest))
    return pl.pallas_call(body, grid=(r // tr,), in_specs=[blk] * 4, out_specs=[blk] * 3, out_shape=[SDS(w.shape, F32)] * 3,
                          name=name, compiler_params=_cp(("parallel",)))(w, g, m, v)


def adamw_rows_view(w, g_mine, g_full, m, v, c_arr, *, name, tc=256):
    r, _, c = w.shape
    per_half = c // 2 // tc

    def body(c_ref, w_ref, gm_ref, gf_ref, m_ref, v_ref, d_ref, nm_ref, nv_ref, go_ref):
        mine = (pl.program_id(0) >> (per_half.bit_length() - 1)) == c_ref[0]
        gv = jnp.where(mine, gm_ref[...], gf_ref[...])
        mn = ADAM_B1 * m_ref[:, 0, :] + (1.0 - ADAM_B1) * gv
        vn = ADAM_B2 * v_ref[:, 0, :] + (1.0 - ADAM_B2) * (gv * gv)
        m_hat = mn / (1.0 - ADAM_B1 ** ADAM_STEP)
        v_hat = vn / (1.0 - ADAM_B2 ** ADAM_STEP)
        d_ref[:, 0, :] = -ADAM_LR * (m_hat / (jnp.sqrt(v_hat) + ADAM_EPS) + ADAM_WD * w_ref[:, 0, :])
        nm_ref[:, 0, :] = mn
        nv_ref[:, 0, :] = vn
        go_ref[:, 0, :] = gv

    b3 = pl.BlockSpec((r, 1, tc), lambda i, c_ref: (0, 0, i))
    own = pl.BlockSpec((r, tc), lambda i, c_ref: (0, jnp.clip(i - c_ref[0] * per_half, 0, per_half - 1)))
    full = pl.BlockSpec((r, tc), lambda i, c_ref: (0, i))
    grid_spec = pltpu.PrefetchScalarGridSpec(num_scalar_prefetch=1, grid=(c // tc,), in_specs=[b3, own, full, b3, b3],
                                             out_specs=[b3] * 4)
    return pl.pallas_call(body, grid_spec=grid_spec, out_shape=[SDS(w.shape, F32)] * 4, name=name,
                          compiler_params=_cp(("parallel",)))(c_arr, w, g_mine, g_full, m, v)


ANY = pl.BlockSpec(memory_space=pl.ANY)
ICI_KINDS = ("x", "y", "xy")


def _coords():
    return lax.axis_index("x"), lax.axis_index("y"), lax.axis_index("c")


def _peer(kind, x, y, c):
    if kind == "c":
        return (x, y, 1 - c)
    if kind == "x":
        return (1 - x, y, c)
    if kind == "y":
        return (x, 1 - y, c)
    return (1 - x, 1 - y, c)


def _chip_of(p):
    return 2 * p[0] + p[1]


def _half(rows, which):
    h = rows // 2
    return pl.ds(pl.multiple_of(which * h, 16), h)


def _remote(src, dst, send_sem, recv_sem, to):
    return pltpu.make_async_remote_copy(src_ref=src, dst_ref=dst, send_sem=send_sem, recv_sem=recv_sem,
                                        device_id=to, device_id_type=MESH)


def allgather_balanced(shard, *, name):
    r, cols = shard.shape
    h, q = r // 2, r // 4

    def body(in_ref, out_ref, send_sems, recv_sems):
        x, y, c = _coords()
        me, sibling = (x, y, c), (x, y, 1 - c)
        nbr = ((1 - x, y, c), (x, 1 - y, c))
        chip = (2 * (1 - x) + y, 2 * x + (1 - y), 2 * (1 - x) + (1 - y))
        quarter = lambda core, i: pl.ds(pl.multiple_of(core * h + i * q, 16), q)
        sent = []

        def go(src, dst, slot, to):
            cp = _remote(src, dst, send_sems.at[slot], recv_sems.at[slot], to)
            cp.start()
            sent.append(cp)

        def landed(region, slot):
            _remote(region, region, send_sems.at[slot], recv_sems.at[slot], me).wait_recv()

        for i in range(2):
            for k in range(2):
                qi = k if i == 0 else 1 - k
                go(in_ref.at[quarter(c, qi)], out_ref.at[2 * x + y, quarter(c, qi)], 2 * k + qi, nbr[k])
        for k in range(2):
            piece = out_ref.at[chip[k], quarter(c, k)]
            landed(piece, 2 * k + k)
            go(piece, piece, 4 + k, nbr[1 - k])
            go(piece, piece, 6 + 2 * k + k, sibling)
        for k in range(2):
            piece = out_ref.at[chip[k], quarter(c, 1 - k)]
            landed(piece, 2 * k + 1 - k)
            go(piece, piece, 6 + 2 * k + 1 - k, sibling)
        for k in range(2):
            piece = out_ref.at[chip[2], quarter(c, k)]
            landed(piece, 4 + k)
            go(piece, piece, 10 + k, sibling)
        for k in range(2):
            for i in range(2):
                landed(out_ref.at[chip[k], quarter(1 - c, i)], 6 + 2 * k + i)
            landed(out_ref.at[chip[2], quarter(1 - c, k)], 10 + k)
        for cp in sent:
            cp.wait_send()

    return pl.pallas_call(
        body, in_specs=[ANY], out_specs=ANY, out_shape=SDS((4,) + shard.shape, shard.dtype),
        scratch_shapes=[pltpu.SemaphoreType.DMA((12,)), pltpu.SemaphoreType.DMA((12,))], name=name)(shard)


def _allgather_shapes(shards):
    return [SDS((4,) + a.shape, a.dtype) for a in shards]


def _allgather_sems(n):
    return [pltpu.SemaphoreType.DMA((n, 6)), pltpu.SemaphoreType.DMA((n, 6))]


def _allgather_rows(ref, is_halved, which):
    r = ref.shape[0]
    return _half(r, which) if is_halved else pl.ds(0, r)


def _allgather_first(ins, outs, send_sems, recv_sems, halved):
    x, y, c = _coords()
    my_chip = 2 * x + y
    cps = []
    for w in range(len(ins)):
        rows = _allgather_rows(ins[w], halved[w], c)
        for k, kind in enumerate(ICI_KINDS):
            cps.append(_remote(ins[w].at[rows], outs[w].at[my_chip, rows], send_sems.at[w, k], recv_sems.at[w, k],
                               _peer(kind, x, y, c)))
    return cps


def _allgather_start(ins, outs, send_sems, recv_sems, halved):
    for cp in _allgather_first(ins, outs, send_sems, recv_sems, halved):
        cp.start()


def _allgather_finish(ins, outs, send_sems, recv_sems, halved):
    x, y, c = _coords()
    me = (x, y, c)
    second = []
    for w in range(len(ins)):
        for k, kind in enumerate(ICI_KINDS):
            landed = outs[w].at[_chip_of(_peer(kind, x, y, c)), _allgather_rows(ins[w], halved[w], c)]
            _remote(landed, landed, send_sems.at[w, k], recv_sems.at[w, k], me).wait_recv()
            if halved[w]:
                cp = _remote(landed, landed, send_sems.at[w, 3 + k], recv_sems.at[w, 3 + k], _peer("c", x, y, c))
                cp.start()
                second.append(cp)
    for w in range(len(ins)):
        if halved[w]:
            for k, kind in enumerate(ICI_KINDS):
                other = outs[w].at[_chip_of(_peer(kind, x, y, c)), _allgather_rows(ins[w], True, 1 - c)]
                _remote(other, other, send_sems.at[w, 3 + k], recv_sems.at[w, 3 + k], me).wait_recv()
    for cp in _allgather_first(ins, outs, send_sems, recv_sems, halved) + second:
        cp.wait_send()


def _half_of(ref, by_cols, which):
    lead = (slice(None),) * (len(ref.shape) - 2)
    if by_cols:
        h = ref.shape[-1] // 2
        return ref.at[lead + (slice(None), pl.ds(pl.multiple_of(which * h, LANES), h))]
    return ref.at[lead + (_half(ref.shape[-2], which),)]


def _half_shape(shape, by_cols):
    return shape[:-1] + (shape[-1] // 2,) if by_cols else shape[:-2] + (shape[-2] // 2, shape[-1])


def grads_to_sibling(gs, by_cols, *, name):
    n = len(gs)

    def body(*refs):
        ins, outs = refs[:n], refs[n:2 * n]
        send_sems, recv_sems = refs[2 * n:]
        x, y, c = _coords()
        cps = []
        for w in range(n):
            cp = _remote(_half_of(ins[w], by_cols[w], 1 - c), outs[w], send_sems.at[w], recv_sems.at[w], _peer("c", x, y, c))
            cp.start()
            cps.append(cp)
        for cp in cps:
            cp.wait()

    return pl.pallas_call(
        body, in_specs=[ANY] * n, out_specs=[ANY] * n,
        out_shape=[SDS(_half_shape(a.shape, bc), a.dtype) for a, bc in zip(gs, by_cols)],
        scratch_shapes=[pltpu.SemaphoreType.DMA((n,)), pltpu.SemaphoreType.DMA((n,))], name=name)(*gs)


def _to_chips_shapes(ps):
    return [SDS((3,) + a.shape[1:], a.dtype) for a in ps]


def _to_chips_sems(n):
    return [pltpu.SemaphoreType.DMA((n, 3)), pltpu.SemaphoreType.DMA((n, 3))]


def _to_chips_copies(ins, outs, send_sems, recv_sems):
    x, y, c = _coords()
    cps = []
    for w in range(len(ins)):
        for k, kind in enumerate(ICI_KINDS):
            to = _peer(kind, x, y, c)
            cps.append(_remote(ins[w].at[_chip_of(to)], outs[w].at[k], send_sems.at[w, k], recv_sems.at[w, k], to))
    return cps


def _to_chips_start(ins, outs, send_sems, recv_sems):
    for cp in _to_chips_copies(ins, outs, send_sems, recv_sems):
        cp.start()


def _to_chips_finish(ins, outs, send_sems, recv_sems):
    for cp in _to_chips_copies(ins, outs, send_sems, recv_sems):
        cp.wait()


def _to_owners_shapes(ps):
    return [SDS((7, a.shape[1] // 2, a.shape[2]), a.dtype) for a in ps]


def _to_owners_sems(n):
    return [pltpu.SemaphoreType.DMA((n, 7)), pltpu.SemaphoreType.DMA((n, 7))]


def _to_owners_copies(ins, outs, send_sems, recv_sems):
    x, y, c = _coords()
    cps = []
    for w in range(len(ins)):
        rows = ins[w].shape[1]
        for k, kind in enumerate(ICI_KINDS):
            px, py, _ = _peer(kind, x, y, c)
            for h in range(2):
                cps.append(_remote(ins[w].at[2 * px + py, _half(rows, h)], outs[w].at[2 * k + c],
                                   send_sems.at[w, 2 * k + h], recv_sems.at[w, 2 * k + c], (px, py, h)))
        cps.append(_remote(ins[w].at[2 * x + y, _half(rows, 1 - c)], outs[w].at[6], send_sems.at[w, 6], recv_sems.at[w, 6],
                           _peer("c", x, y, c)))
    return cps


def _to_owners_start(ins, outs, send_sems, recv_sems):
    for cp in _to_owners_copies(ins, outs, send_sems, recv_sems):
        cp.start()


def _to_owners_finish(ins, outs, send_sems, recv_sems):
    for cp in _to_owners_copies(ins, outs, send_sems, recv_sems):
        cp.wait_send()
    for w in range(len(ins)):
        for slot in range(7):
            got = outs[w].at[slot]
            _remote(got, got, send_sems.at[w, slot], recv_sems.at[w, slot], _coords()).wait_recv()


EXCHANGES = {"to_chips": (_to_chips_shapes, _to_chips_sems, _to_chips_start, _to_chips_finish),
             "to_owners": (_to_owners_shapes, _to_owners_sems, _to_owners_start, _to_owners_finish)}


def halves_to_full(hs, by_cols, *, name):
    n = len(hs)

    def body(*refs):
        ins, outs = refs[:n], refs[n:2 * n]
        send_sems, recv_sems = refs[2 * n:]
        x, y, c = _coords()
        cps = []
        for w in range(n):
            cp = _remote(ins[w], _half_of(outs[w], by_cols[w], c), send_sems.at[w], recv_sems.at[w], _peer("c", x, y, c))
            cp.start()
            cps.append(cp)
        for cp in cps:
            cp.wait()

    return pl.pallas_call(
        body, in_specs=[ANY] * n, out_specs=[ANY] * n,
        out_shape=[SDS((a.shape[0], 2 * a.shape[1]) if bc else (2 * a.shape[0], a.shape[1]), a.dtype)
                   for a, bc in zip(hs, by_cols)],
        scratch_shapes=[pltpu.SemaphoreType.DMA((n,)), pltpu.SemaphoreType.DMA((n,))],
        name=name)(*hs)


def _row_tile(rows):
    for cand in (256, 192, 176, 128, 64, 32, 16):
        if rows % cand == 0:
            return cand
    return rows


def chip_sum(g, recv, c_arr, by_cols, *, name):
    _, r, cols = g.shape

    def body(c_ref, g_ref, r_ref, f_ref, b_ref):
        tot = g_ref[...] + r_ref[...]
        f_ref[...] = tot
        b_ref[...] = tot.astype(BF16)

    if by_cols:
        tc = 4 * LANES
        nblk = cols // 2 // tc
        shape = (4, r, cols // 2)
        blk = pl.BlockSpec((None, r, tc), lambda j, i, c_ref: (j, 0, i))
        mine = pl.BlockSpec((None, r, tc), lambda j, i, c_ref: (j, 0, c_ref[0] * nblk + i))
    else:
        tr = _row_tile(r // 2)
        nblk = r // 2 // tr
        shape = (4, r // 2, cols)
        blk = pl.BlockSpec((None, tr, cols), lambda j, i, c_ref: (j, i, 0))
        mine = pl.BlockSpec((None, tr, cols), lambda j, i, c_ref: (j, c_ref[0] * nblk + i, 0))
    grid_spec = pltpu.PrefetchScalarGridSpec(num_scalar_prefetch=1, grid=(4, nblk), in_specs=[mine, blk], out_specs=[blk, blk])
    return pl.pallas_call(body, grid_spec=grid_spec, out_shape=[SDS(shape, F32), SDS(shape, BF16)],
                          name=name, compiler_params=_cp(("parallel", "parallel")))(c_arr, g, recv)


def final_sum(pf, recv, chip_arr, *, name):
    _, h, cols = pf.shape
    tr = _row_tile(h)

    def body(chip_ref, p_ref, r_ref, o_ref):
        o_ref[...] = ((p_ref[...] + r_ref[0].astype(F32)) + r_ref[1].astype(F32)) + r_ref[2].astype(F32)

    grid_spec = pltpu.PrefetchScalarGridSpec(
        num_scalar_prefetch=1, grid=(h // tr,),
        in_specs=[pl.BlockSpec((None, tr, cols), lambda i, chip_ref: (chip_ref[0], i, 0)),
                  pl.BlockSpec((3, tr, cols), lambda i, chip_ref: (0, i, 0))],
        out_specs=pl.BlockSpec((tr, cols), lambda i, chip_ref: (i, 0)))
    return pl.pallas_call(body, grid_spec=grid_spec, out_shape=SDS((h, cols), F32), name=name,
                          compiler_params=_cp(("parallel",)))(chip_arr, pf, recv)


def owner_sum(g, recv, pos_arr, *, name):
    _, r, cols = g.shape
    h = r // 2
    tr = _row_tile(h)
    nblk = h // tr

    def body(pos_ref, g_ref, r_ref, o_ref):
        tot = g_ref[...]
        for slot in range(7):
            tot = tot + r_ref[slot].astype(F32)
        o_ref[...] = tot

    grid_spec = pltpu.PrefetchScalarGridSpec(
        num_scalar_prefetch=1, grid=(nblk,),
        in_specs=[pl.BlockSpec((None, tr, cols), lambda i, pos: (pos[0], pos[1] * nblk + i, 0)),
                  pl.BlockSpec((7, tr, cols), lambda i, pos: (0, i, 0))],
        out_specs=pl.BlockSpec((tr, cols), lambda i, pos: (i, 0)))
    return pl.pallas_call(body, grid_spec=grid_spec, out_shape=SDS((h, cols), F32), name=name,
                          compiler_params=_cp(("parallel",)))(pos_arr, g, recv)


def allreduce_small(v, *, name):
    rws, cols = v.shape

    def body(v_ref, all_ref, sum_ref, send_sems, recv_sems, local_sem):
        x, y, c = _coords()
        me, sibling = (x, y, c), (x, y, 1 - c)
        chips = [(1 - x, y), (x, 1 - y), (1 - x, 1 - y)]

        def rows(px, py, pc):
            return all_ref.at[pl.ds(pl.multiple_of((4 * px + 2 * py + pc) * rws, 8), rws), :]

        def copy(k, block, to, src=None):
            return _remote(rows(*block) if src is None else src, rows(*block), send_sems.at[k], recv_sems.at[k], to)

        mine = pltpu.make_async_copy(v_ref, rows(*me), local_sem)
        mine.start()
        first = [copy(0, me, sibling, src=v_ref)]
        first += [copy(1 + j, me, (*chip, c), src=v_ref) for j, chip in enumerate(chips)]
        for cp in first:
            cp.start()
        passed = [copy(4 + j, (*chip, c), sibling) for j, chip in enumerate(chips)]
        for j, chip in enumerate(chips):
            copy(1 + j, (*chip, c), me).wait_recv()
            passed[j].start()
        copy(0, sibling, me).wait_recv()
        for j, chip in enumerate(chips):
            copy(4 + j, (*chip, 1 - c), me).wait_recv()
        for cp in first + passed:
            cp.wait_send()
        mine.wait()
        tot = all_ref[0:rws, :]
        for dev in range(1, 8):
            tot = tot + all_ref[dev * rws:(dev + 1) * rws, :]
        sum_ref[...] = tot

    vm = pl.BlockSpec(memory_space=pltpu.VMEM)
    return pl.pallas_call(
        body, in_specs=[vm], out_specs=[vm, vm],
        out_shape=[SDS((8 * rws, cols), v.dtype), SDS((rws, cols), v.dtype)],
        scratch_shapes=[pltpu.SemaphoreType.DMA((7,)), pltpu.SemaphoreType.DMA((7,)), pltpu.SemaphoreType.DMA],
        name=name)(v)[1]


def _pack_rows(parts, rows):
    out = []
    for a, r in zip(parts, rows):
        flat = a.reshape(-1)
        flat = jnp.pad(flat, (0, r * LANES - flat.shape[0]))
        out.append(flat.reshape(r, LANES))
    return jnp.concatenate(out, axis=0)


def _unpack_rows(packed, shapes, rows):
    out, at = [], 0
    for shp, r in zip(shapes, rows):
        size = int(np.prod(shp))
        out.append(packed[at:at + r].reshape(-1)[:size].reshape(shp))
        at += r
    return out


def kernel(x, g_pre_mix, w_in, b_forget, w_o_fox, w_o_dil, w_out, g_post_mix, g_pre_ffn, w_up, conv_w, conv_b, w_down, g_post_ffn, loss_target, m_g_pre_mix, m_w_in, m_b_forget, m_w_o_fox, m_w_o_dil, m_w_out, m_g_post_mix, m_g_pre_ffn, m_w_up, m_conv_w, m_conv_b, m_w_down, m_g_post_ffn, v_g_pre_mix, v_w_in, v_b_forget, v_w_o_fox, v_w_o_dil, v_w_out, v_g_post_mix, v_g_pre_ffn, v_w_up, v_conv_w, v_conv_b, v_w_down, v_g_post_ffn):
    xi, yi, ci = _coords()
    chip = 2 * xi + yi
    c_arr = jnp.reshape(ci, (1,)).astype(jnp.int32)
    chip_arr = jnp.reshape(chip, (1,)).astype(jnp.int32)
    xs = x[0]
    target = loss_target[0]
    s, d = xs.shape
    f_half = w_down.shape[1] * 4
    cols_in = w_in.shape[2]

    big = (w_in, w_o_fox, w_o_dil, w_out, w_up, w_down)
    shards = [w[0].astype(BF16) for w in big]
    a_in = allgather_balanced(shards[0], name="allgather_w_in")
    w_in_full = jnp.concatenate([jnp.where(chip == j, shards[0], a_in[j]) for j in range(4)], axis=1)
    nf = N_HEADS
    e_a, e_b = 3 * ATT_W, 3 * ATT_W + nf
    wz = jnp.concatenate([w_in_full[:, :e_a], w_in_full[:, e_b:]], axis=1)
    wf = jnp.pad(w_in_full[:, e_a:e_b], ((0, 0), (0, LANES - nf)))
    cb = conv_b
    bfo = jnp.pad(b_forget, ((0, 0), (0, LANES - nf)))

    h1 = rmsnorm_fwd(xs, g_pre_mix)
    z = mm([(h1, d, 0)], [(wz, d, 0)], nt=False, out_dtype=BF16, tm=s, tn=512, name="in_proj")
    fa = mm([(h1, d, 0)], [(wf, d, 0)], nt=False, out_dtype=F32, tm=s, tn=LANES, name="in_proj_forget")
    q_aug, k_aug, v_aug = fox_prep(z, fa, bfo)
    later = shards[1:] + [conv_w[0]]
    ya, lse_a, *late = fox_fwd(q_aug, k_aug, v_aug, gather=later, halved=[True] * 5 + [False], hps=N_HEADS)
    a_of, a_od, a_out, a_up, a_down, a_cw = [
        lax.dynamic_update_index_in_dim(a4, own, chip, 0) for a4, own in zip(late, later)]
    cw = jnp.concatenate([a_cw[j] for j in range(4)], axis=1)
    wo_a = jnp.concatenate([a_of[j] for j in range(4)], axis=1)
    wo_b = jnp.concatenate([a_od[j] for j in range(4)], axis=1)
    w_o = a_out.reshape(d, d)
    w_dn = a_down.reshape(f_half, d)
    wu_a = jnp.concatenate([a_up[0], a_up[1]], axis=1)
    wu_b = jnp.concatenate([a_up[2], a_up[3]], axis=1)
    cos_t, sin_t = rope_cos_sin(s)
    yb, lse_b = dil_fwd_all(z, cos_t, sin_t)
    pa, pb, mixed = gate_mix(ya, yb, wo_a, wo_b, z)
    y1, x1, h2 = proj_norm_res(mixed, w_o, g_post_mix, xs, g_pre_ffn, tm=1024, name="out_proj")
    ua, ub, conv_a, conv_bh, mid = ffn_up(h2, wu_a, wu_b, cw, cb)
    dout, dy2, gg_post_ffn, sq = proj_norm_loss(mid, w_dn, g_post_ffn, x1, target, name="down_proj")
    loss = lax.psum(0.5 * sq[0, 0] / d, ("x", "y", "c"))

    dmid = mm([(dy2, d, 0)], [(w_dn, d, 0)], nt=True, out_dtype=BF16, tm=2048, tn=f_half // 2, name="down_dgrad")
    dw_down, dw_down16 = wgrad((mid, f_half, 0), dy2, tk=f_half // 2, tn=1024, ts=2048, name="down_wgrad", bf16_copy=True)
    dua, dub, gc_a, gc_b = ffn_bwd(dmid, ua, ub, conv_a, conv_bh, cw)
    dx1, dy1, gg_pre_ffn, gg_post_mix = mm_norm_bwd(
        [(dua, f_half, 0), (dub, f_half, 0)], [(wu_a, f_half, 0), (wu_b, f_half, 0)],
        [(x1, g_pre_ffn, dout, F32), (y1, g_post_mix, None, BF16)], name="up_dgrad")
    dw_up = None
    for k, du in enumerate((dua, dub)):
        dw_up = wgrad((h2, d, 0), du, tk=1024, tn=f_half // 2, ts=2048, name=f"up_wgrad_{k}", chip_major=True,
                      slabs=(4, 2 * k), into=dw_up, bf16_copy=True)
    g_ffn = [(dw_up[0], dw_up[1]), (dw_down.reshape(4, f_half // 4, d), dw_down16.reshape(4, f_half // 4, d))]
    dw_out, dw_out16 = wgrad((mixed, d, 0), dy1, tk=1024, tn=1024, ts=2048, name="out_wgrad", bf16_copy=True)
    dpa, dpb, dz_g, dya, dyb, dd_a = mix_bwd(dy1, w_o, z, pa, pb, wo_a, wo_b, ya)
    by_chip_cols = lambda a: jnp.stack([a[:, j * (d // 4):(j + 1) * (d // 4)] for j in range(4)], axis=0)
    dw_of = [by_chip_cols(a) for a in wgrad((ya, ATT_W, 0), dpa, tk=ATT_W, tn=d, ts=1024, name="fox_o_wgrad", bf16_copy=True)]
    dw_od = [by_chip_cols(a) for a in wgrad((yb, ATT_W, 0), dpb, tk=ATT_W, tn=d, ts=1024, name="dil_o_wgrad", bf16_copy=True)]
    g_mix = [dw_of, dw_od, (dw_out.reshape(4, d // 4, d), dw_out16.reshape(4, d // 4, d))]
    dq_aug, dk_aug, dv_a, *got_ffn = fox_bwd(q_aug, k_aug, z, dya, lse_a, dd_a, exchange=[g[1] for g in g_ffn], kind="to_owners")
    dz_a, dfa, gg_bf = fox_post(dq_aug, dk_aug, dv_a, fa, bfo)
    *dz_b, got_of, got_od, got_out = dil_bwd_all(z, cos_t, sin_t, dyb, lse_b, yb, exchange=[g[1] for g in g_mix],
                                                 kind="to_owners")
    got_mix = [got_of, got_od, got_out]
    dwt_a = wgrad((dz_a, e_a, 0), h1, tk=e_a // 2, tn=d, ts=2048, name="in_wgrad_a")
    dwt_b = [wgrad((part, ATT_W, 0), h1, tk=ATT_W, tn=d, ts=2048, name=f"in_wgrad_b{k}") for k, part in enumerate(dz_b)]
    dwt_g = wgrad((dz_g, 2 * d, 0), h1, tk=d, tn=d, ts=2048, name="in_wgrad_g")
    dwt_f = wgrad((dfa, LANES, 0), h1, tk=LANES, tn=d, ts=2048, name="in_wgrad_f")
    dwt_full = jnp.concatenate([dwt_a, dwt_f[:nf], *dwt_b, dwt_g], axis=0)
    dw_in = jnp.stack([dwt_full[j * cols_in:(j + 1) * cols_in] for j in range(4)], axis=0)
    from_sib = grads_to_sibling([dw_in], [True], name="grads_to_sibling_in")
    sum_in = chip_sum(dw_in, from_sib[0], c_arr, True, name="chip_sum_w_in")
    grad_x, gg_pre_mix, got_in = mm_norm_bwd(
        [(dz_a, e_a, 0), *[(part, ATT_W, 0) for part in dz_b], (dz_g, d, 0), (dz_g, d, 1), (dfa, LANES, 0)],
        [(wz, e_a, 0), *[(wz, ATT_W, Z_QB + k) for k in range(3)], (wz, d, 3), (wz, d, 4), (wf, LANES, 0)],
        [(xs, g_pre_mix, dx1, F32)], exchange=[sum_in[1]], name="in_dgrad")

    names = ("w_in", "w_o_fox", "w_o_dil", "w_out", "w_up", "w_down")
    pos_arr = jnp.concatenate([chip_arr, c_arr])
    halves = [final_sum(sum_in[0], got_in, chip_arr, name="final_sum_w_in")] + [
        owner_sum(g[0], got, pos_arr, name=f"owner_sum_{nm}") for g, got, nm in zip(g_mix + g_ffn, got_mix + got_ffn, names[1:])]
    from_half = halves_to_full(halves, [True] + [False] * 5, name="halves_to_full")
    g_big = [None] + [lax.dynamic_update_slice_in_dim(full, mine, ci * mine.shape[0], axis=0)
                      for full, mine in zip(from_half[1:], halves[1:])]
    upd_big = [adamw(w[0], g, m[0], v[0], name=f"adamw_{nm}") for w, g, m, v, nm in list(zip(
        big, g_big, (m_w_in, m_w_o_fox, m_w_o_dil, m_w_out, m_w_up, m_w_down),
        (v_w_in, v_w_o_fox, v_w_o_dil, v_w_out, v_w_up, v_w_down), names))[1:]]
    to_t = lambda a: jnp.transpose(a, (2, 0, 1))
    from_t = lambda a: jnp.transpose(a, (1, 2, 0))
    *upd_in, g_in_t = adamw_rows_view(to_t(w_in), halves[0], from_half[0], to_t(m_w_in), to_t(v_w_in), c_arr,
                                      name="adamw_w_in")

    g_cw_loc = jnp.concatenate([gc_a[0:3], gc_b[0:3]], axis=1)
    g_cb_loc = jnp.concatenate([gc_a[3:4], gc_b[3:4]], axis=1)
    small_loc = [gg_pre_mix, gg_post_mix, gg_pre_ffn, gg_post_ffn, g_cb_loc, gg_bf[:, :nf], g_cw_loc]
    red_rows = (8, 8, 8, 8, 48, 8, 136)
    red = allreduce_small(_pack_rows(small_loc, red_rows), name="allreduce_small")
    g_pm, g_qm, g_pf, g_qf, g_cb, g_bf, g_cw_full = _unpack_rows(red, [a.shape for a in small_loc], red_rows)
    cols_cw = conv_w.shape[2]
    g_cw = lax.dynamic_slice_in_dim(g_cw_full, chip * cols_cw, cols_cw, axis=1)
    small_w = (g_pre_mix, g_post_mix, g_pre_ffn, g_post_ffn, conv_b, b_forget, conv_w[0])
    small_m = (m_g_pre_mix, m_g_post_mix, m_g_pre_ffn, m_g_post_ffn, m_conv_b, m_b_forget, m_conv_w[0])
    small_v = (v_g_pre_mix, v_g_post_mix, v_g_pre_ffn, v_g_post_ffn, v_conv_b, v_b_forget, v_conv_w[0])
    small_g = (g_pm, g_qm, g_pf, g_qf, g_cb, g_bf, g_cw)
    small_names = ("g_pre_mix", "g_post_mix", "g_pre_ffn", "g_post_ffn", "conv_b", "b_forget", "conv_w")
    per_param = [adamw(w, g, m, v, name=f"adamw_{nm}") for w, g, m, v, nm in zip(small_w, small_g, small_m, small_v, small_names)]
    upd_small = [[u[j] for u in per_param] for j in range(3)]

    order = ("g_pre_mix", "w_in", "b_forget", "w_o_fox", "w_o_dil", "w_out", "g_post_mix", "g_pre_ffn", "w_up", "conv_w",
             "conv_b", "w_down", "g_post_ffn")
    grads, deltas, new_ms, new_vs = {}, {}, {}, {}
    grads["w_in"] = from_t(g_in_t)
    deltas["w_in"], new_ms["w_in"], new_vs["w_in"] = (from_t(a) for a in upd_in)
    for k, nm in enumerate(names[1:]):
        grads[nm] = g_big[k + 1][None]
        deltas[nm], new_ms[nm], new_vs[nm] = (a[None] for a in upd_big[k])
    for k, nm in enumerate(small_names):
        lead = (lambda a: a[None]) if nm == "conv_w" else (lambda a: a)
        grads[nm] = lead(small_g[k])
        deltas[nm], new_ms[nm], new_vs[nm] = (lead(upd_small[j][k]) for j in range(3))
    return (loss, grad_x[None], *[grads[nm] for nm in order], *[deltas[nm] for nm in order],
            *[new_ms[nm] for nm in order], *[new_vs[nm] for nm in order])
```

```python
import functools
import math

import numpy as np
import jax
import jax.numpy as jnp
from jax import lax
from jax.experimental import pallas as pl
from jax.experimental.pallas import tpu as pltpu

F32 = jnp.float32
BF16 = jnp.bfloat16
SDS = jax.ShapeDtypeStruct
MESH = pl.DeviceIdType.MESH

HEAD_DIM = 64
N_HEADS = 8
LANES = 128
ATT_W = N_HEADS * HEAD_DIM
DIL_PATTERNS = ((128, 1), (512, 4), (2048, 16))
DIL_BLK = 128
ROPE_DIM = HEAD_DIM // 4
ROPE_THETA = 500000.0
RMS_EPS = 1e-6
NEG = -1e30
QK_SCALE = 1.0 / math.sqrt(HEAD_DIM)
ADAM_LR, ADAM_B1, ADAM_B2, ADAM_EPS, ADAM_WD, ADAM_STEP = 0.001, 0.9, 0.999, 1e-08, 0.01, 10
VMEM_LIMIT = 56 * 1024 * 1024

Z_QA, Z_KA, Z_VA, Z_QB, Z_KB, Z_VB = 0, 1, 2, 3, 4, 5
Z_W = 5120


def _cp(sem):
    return pltpu.CompilerParams(dimension_semantics=sem, vmem_limit_bytes=VMEM_LIMIT)


def _nt(a, b):
    return lax.dot_general(a, b, (((1,), (1,)), ((), ())), preferred_element_type=F32)


def _tn(a, b):
    return lax.dot_general(a, b, (((0,), (0,)), ((), ())), preferred_element_type=F32)


def _nn(a, b):
    return jnp.dot(a, b, preferred_element_type=F32)


def _lane(shape):
    return lax.broadcasted_iota(jnp.int32, shape, 1)


def _row(shape):
    return lax.broadcasted_iota(jnp.int32, shape, 0)


def rmsnorm_fwd(x, g, *, tm=1024):
    s, d = x.shape

    def body(x_ref, g_ref, h_ref):
        xv = x_ref[...]
        inv = lax.rsqrt(jnp.mean(xv * xv, axis=-1, keepdims=True) + RMS_EPS)
        h_ref[...] = (xv * inv * g_ref[...]).astype(h_ref.dtype)

    return pl.pallas_call(
        body, grid=(s // tm,),
        in_specs=[pl.BlockSpec((tm, d), lambda i: (i, 0)), pl.BlockSpec((1, d), lambda i: (0, 0))],
        out_specs=pl.BlockSpec((tm, d), lambda i: (i, 0)),
        out_shape=SDS((s, d), BF16), name="rmsnorm_fwd", compiler_params=_cp(("parallel",)))(x, g)


def mm(a_views, b_views, *, nt, out_dtype, tm, tn, name):
    n_p = len(a_views)
    m = a_views[0][0].shape[0]
    n = b_views[0][0].shape[0] if nt else b_views[0][0].shape[1]

    def body(*refs):
        o_ref = refs[-1]
        acc = None
        for p in range(n_p):
            av = refs[p][...].astype(BF16)
            bv = refs[n_p + p][...].astype(BF16)
            dv = _nt(av, bv) if nt else _nn(av, bv)
            acc = dv if acc is None else acc + dv
        o_ref[...] = acc.astype(o_ref.dtype)

    in_specs = []
    for arr, w, blk in a_views:
        in_specs.append(pl.BlockSpec((tm, w), functools.partial(lambda i, j, blk: (i, blk), blk=blk)))
    for arr, w, blk in b_views:
        if nt:
            in_specs.append(pl.BlockSpec((tn, w), functools.partial(lambda i, j, blk: (j, blk), blk=blk)))
        else:
            in_specs.append(pl.BlockSpec((w, tn), lambda i, j: (0, j)))
    return pl.pallas_call(
        body, grid=(m // tm, n // tn), in_specs=in_specs,
        out_specs=pl.BlockSpec((tm, tn), lambda i, j: (i, j)),
        out_shape=SDS((m, n), out_dtype), name=name,
        compiler_params=_cp(("parallel", "parallel")))(*[a[0] for a in a_views], *[b[0] for b in b_views])


def wgrad(a_view, g, *, tk, tn, ts, name, chip_major=False, slabs=None, into=None, bf16_copy=False):
    arr, ka, blk = a_view
    s, n = g.shape
    ns = s // ts
    total, first = slabs if slabs else (n // tn, 0)
    n_into = 0 if into is None else (2 if bf16_copy else 1)

    def body(a_ref, g_ref, *rest):
        o_ref = rest[n_into]

        @pl.when(pl.program_id(2) == 0)
        def _():
            o_ref[...] = jnp.zeros_like(o_ref)

        o_ref[...] += _tn(a_ref[...].astype(BF16), g_ref[...].astype(BF16))
        if bf16_copy:
            @pl.when(pl.program_id(2) == ns - 1)
            def _():
                rest[n_into + 1][...] = o_ref[...].astype(BF16)

    if chip_major:
        out_spec = pl.BlockSpec((None, tk, tn), lambda i, j, k: (first + j, i, 0))
        shape = (total, ka, tn)
    else:
        out_spec = pl.BlockSpec((tk, tn), lambda i, j, k: (i, j))
        shape = (ka, n)
    in_specs = [pl.BlockSpec((ts, tk), lambda i, j, k: (k, blk * (ka // tk) + i)),
                pl.BlockSpec((ts, tn), lambda i, j, k: (k, j))]
    args = [arr, g]
    if into is not None:
        earlier = list(into) if bf16_copy else [into]
        in_specs += [pl.BlockSpec(memory_space=pl.ANY)] * len(earlier)
        args += earlier
    out = pl.pallas_call(
        body, grid=(ka // tk, n // tn, ns), in_specs=in_specs,
        out_specs=[out_spec, out_spec] if bf16_copy else out_spec,
        out_shape=[SDS(shape, F32), SDS(shape, BF16)] if bf16_copy else SDS(shape, F32), name=name,
        input_output_aliases={2 + k: k for k in range(n_into)},
        compiler_params=_cp(("parallel", "parallel", "arbitrary")))(*args)
    return out


def _norm_bwd_rows(dh, xh, inv, g):
    dxh = dh * g
    dx = inv * (dxh - xh * jnp.mean(dxh * xh, axis=-1, keepdims=True))
    return dx, jnp.sum((dh * xh).reshape(dh.shape[0] // 8, 8, dh.shape[1]), axis=0)


def proj_norm_res(a, w, g, xres, g_next, *, tm=512, name):
    s, k = a.shape
    d = w.shape[1]

    def body(a_ref, w_ref, g_ref, x_ref, gn_ref, y_ref, o_ref, h_ref):
        y = _nn(a_ref[...], w_ref[...])
        inv = lax.rsqrt(jnp.mean(y * y, axis=-1, keepdims=True) + RMS_EPS)
        xn = x_ref[...] + y * inv * g_ref[...]
        y_ref[...] = y
        o_ref[...] = xn
        inv_n = lax.rsqrt(jnp.mean(xn * xn, axis=-1, keepdims=True) + RMS_EPS)
        h_ref[...] = (xn * inv_n * gn_ref[...]).astype(h_ref.dtype)

    row = pl.BlockSpec((tm, d), lambda i: (i, 0))
    vec = pl.BlockSpec((1, d), lambda i: (0, 0))
    return pl.pallas_call(
        body, grid=(s // tm,),
        in_specs=[pl.BlockSpec((tm, k), lambda i: (i, 0)), pl.BlockSpec((k, d), lambda i: (0, 0)), vec, row, vec],
        out_specs=[row, row, row], out_shape=[SDS((s, d), F32), SDS((s, d), F32), SDS((s, d), BF16)], name=name,
        compiler_params=_cp(("parallel",)))(a, w, g, xres, g_next)


def proj_norm_loss(a, w, g, xres, target, *, tm=512, name):
    s, k = a.shape
    d = w.shape[1]
    n = s // tm

    def body(a_ref, w_ref, g_ref, x_ref, t_ref, do_ref, dy_ref, dg_ref, l_ref, acc):
        i = pl.program_id(0)

        @pl.when(i == 0)
        def _():
            acc[...] = jnp.zeros_like(acc)
            l_ref[...] = jnp.zeros_like(l_ref)

        y = _nn(a_ref[...], w_ref[...])
        inv = lax.rsqrt(jnp.mean(y * y, axis=-1, keepdims=True) + RMS_EPS)
        yh = y * inv
        err = x_ref[...] + yh * g_ref[...] - t_ref[...]
        dout = err * (1.0 / d)
        do_ref[...] = dout
        l_ref[...] += jnp.sum(jnp.sum(err * err, axis=1, keepdims=True), axis=0, keepdims=True)
        dy, part = _norm_bwd_rows(dout, yh, inv, g_ref[...])
        dy_ref[...] = dy.astype(dy_ref.dtype)
        acc[...] += part

        @pl.when(i == n - 1)
        def _():
            dg_ref[...] = jnp.sum(acc[...], axis=0, keepdims=True)

    row = pl.BlockSpec((tm, d), lambda i: (i, 0))
    vec = pl.BlockSpec((1, d), lambda i: (0, 0))
    return pl.pallas_call(
        body, grid=(n,),
        in_specs=[pl.BlockSpec((tm, k), lambda i: (i, 0)), pl.BlockSpec((k, d), lambda i: (0, 0)), vec, row, row],
        out_specs=[row, row, vec, pl.BlockSpec((1, 1), lambda i: (0, 0))],
        out_shape=[SDS((s, d), F32), SDS((s, d), BF16), SDS((1, d), F32), SDS((1, 1), F32)],
        scratch_shapes=[pltpu.VMEM((8, d), F32)], name=name, compiler_params=_cp(("arbitrary",)))(a, w, g, xres, target)


def mm_norm_bwd(a_views, b_views, stages, exchange=(), *, tm=256, name):
    n_p, n_s, ne = len(a_views), len(stages), len(exchange)
    s = a_views[0][0].shape[0]
    d = b_views[0][0].shape[0]
    n = s // tm
    has_res = [st[2] is not None for st in stages]

    def body(*refs):
        a_refs, b_refs = refs[:n_p], refs[n_p:2 * n_p]
        at = 2 * n_p
        st_refs = []
        for k in range(n_s):
            cnt = 3 if has_res[k] else 2
            st_refs.append(refs[at:at + cnt])
            at += cnt
        e_ins = refs[at:at + ne]
        at += ne
        dx_refs, dg_refs = refs[at:at + n_s], refs[at + n_s:at + 2 * n_s]
        at += 2 * n_s
        e_outs = refs[at:at + ne]
        at += ne
        accs = refs[at:at + n_s]
        comm = (e_ins, e_outs) + tuple(refs[at + n_s:])
        i = pl.program_id(0)

        @pl.when(i == 0)
        def _():
            for acc in accs:
                acc[...] = jnp.zeros_like(acc)
            if ne:
                _to_chips_start(*comm)

        dh = None
        for p in range(n_p):
            part = _nt(a_refs[p][...].astype(BF16), b_refs[p][...].astype(BF16))
            dh = part if dh is None else dh + part
        for k in range(n_s):
            xv = st_refs[k][0][...]
            inv = lax.rsqrt(jnp.mean(xv * xv, axis=-1, keepdims=True) + RMS_EPS)
            dx, part = _norm_bwd_rows(dh, xv * inv, inv, st_refs[k][1][...])
            if has_res[k]:
                dx = dx + st_refs[k][2][...]
            dx_refs[k][...] = dx.astype(dx_refs[k].dtype)
            accs[k][...] += part
            dh = dx

        @pl.when(i == n - 1)
        def _():
            for k in range(n_s):
                dg_refs[k][...] = jnp.sum(accs[k][...], axis=0, keepdims=True)
            if ne:
                _to_chips_finish(*comm)

    row = pl.BlockSpec((tm, d), lambda i: (i, 0))
    vec = pl.BlockSpec((1, d), lambda i: (0, 0))
    in_specs, args = [], []
    for arr, w, blk in a_views:
        in_specs.append(pl.BlockSpec((tm, w), functools.partial(lambda i, blk: (i, blk), blk=blk)))
        args.append(arr)
    for arr, w, blk in b_views:
        in_specs.append(pl.BlockSpec((d, w), functools.partial(lambda i, blk: (0, blk), blk=blk)))
        args.append(arr)
    for x, g, res, _ in stages:
        in_specs += [row, vec] + ([row] if res is not None else [])
        args += [x, g] + ([res] if res is not None else [])
    return pl.pallas_call(
        body, grid=(n,), in_specs=in_specs + [ANY] * ne,
        out_specs=[row] * n_s + [vec] * n_s + [ANY] * ne,
        out_shape=[SDS((s, d), st[3]) for st in stages] + [SDS((1, d), F32)] * n_s + _to_chips_shapes(exchange),
        scratch_shapes=[pltpu.VMEM((8, d), F32)] * n_s + (_to_chips_sems(ne) if ne else []), name=name,
        compiler_params=_cp(("arbitrary",)))(*args, *exchange)


def _split3(v):
    hi = v.astype(BF16).astype(F32)
    r = v - hi
    mid = r.astype(BF16).astype(F32)
    lo = (r - mid).astype(BF16).astype(F32)
    return hi, mid, lo


def _tri(n, upper):
    r = np.arange(n)
    m = (r[:, None] <= r[None, :]) if upper else (r[:, None] >= r[None, :])
    return jnp.asarray(m.astype(np.float32))


def fox_prep(z, fa, bfo, *, tb=512):
    s = z.shape[0]
    n = s // tb

    def body(q_ref, k_ref, v_ref, fa_ref, b_ref, tri_ref, qa_ref, ka_ref, va_ref, carry):
        @pl.when(pl.program_id(0) == 0)
        def _():
            carry[...] = jnp.zeros_like(carry)

        xv = fa_ref[...] + b_ref[...]
        logf = jnp.minimum(xv, 0.0) - jnp.log(1.0 + jnp.exp(-jnp.abs(xv)))
        csum = jnp.dot(tri_ref[...], logf, preferred_element_type=F32, precision=lax.Precision.HIGHEST) + carry[0:1, :]
        carry[0:1, :] = csum[tb - 1:tb, :]
        lane = _lane((tb, LANES))
        for h in range(N_HEADS):
            hi, mid, lo = _split3(csum[:, h:h + 1])
            pair = (h // 2) * LANES
            qv = q_ref[:, pair:pair + LANES].astype(F32)
            kv = k_ref[:, pair:pair + LANES].astype(F32)
            vv = v_ref[:, pair:pair + LANES].astype(F32)
            if h % 2:
                qv = pltpu.roll(qv, 64, axis=1)
                kv = pltpu.roll(kv, 64, axis=1)
                vv = pltpu.roll(vv, 64, axis=1)
            va_ref[:, h * LANES:(h + 1) * LANES] = jnp.where(lane < 64, vv, jnp.where(lane == 64, 1.0, 0.0)).astype(BF16)
            one = jnp.where((lane >= 67) & (lane < 70), 1.0, 0.0)
            q_x = jnp.where(lane == 64, hi, jnp.where(lane == 65, mid, jnp.where(lane == 66, lo, one)))
            one = jnp.where((lane >= 64) & (lane < 67), 1.0, 0.0)
            k_x = jnp.where(lane == 67, -hi, jnp.where(lane == 68, -mid, jnp.where(lane == 69, -lo, one)))
            qa_ref[:, h * LANES:(h + 1) * LANES] = jnp.where(lane < 64, qv * QK_SCALE, q_x).astype(BF16)
            ka_ref[:, h * LANES:(h + 1) * LANES] = jnp.where(lane < 64, kv, k_x).astype(BF16)

    return pl.pallas_call(
        body, grid=(n,),
        in_specs=[pl.BlockSpec((tb, ATT_W), lambda i: (i, Z_QA)), pl.BlockSpec((tb, ATT_W), lambda i: (i, Z_KA)),
                  pl.BlockSpec((tb, ATT_W), lambda i: (i, Z_VA)),
                  pl.BlockSpec((tb, LANES), lambda i: (i, 0)), pl.BlockSpec((1, LANES), lambda i: (0, 0)),
                  pl.BlockSpec((tb, tb), lambda i: (0, 0))],
        out_specs=[pl.BlockSpec((tb, N_HEADS * LANES), lambda i: (i, 0))] * 3,
        out_shape=[SDS((s, N_HEADS * LANES), BF16)] * 3,
        scratch_shapes=[pltpu.VMEM((8, LANES), F32)],
        name="fox_prep", compiler_params=_cp(("arbitrary",)))(z, z, z, fa, bfo, _tri(tb, False))


def _causal_pairs(n, k_major):
    if k_major:
        pairs = [(qi, kj) for kj in range(n) for qi in range(kj, n)]
    else:
        pairs = [(qi, kj) for qi in range(n) for kj in range(qi + 1)]
    return (jnp.asarray([p[0] for p in pairs], jnp.int32), jnp.asarray([p[1] for p in pairs], jnp.int32), len(pairs))


def fox_fwd(q_aug, k_aug, v_aug, gather=(), halved=(), *, t=1024, hps=4):
    s = v_aug.shape[0]
    qi_arr, kj_arr, n_pairs = _causal_pairs(s // t, False)
    ng = len(gather)
    n_groups = N_HEADS // hps

    def body(qi_ref, kj_ref, q_ref, k_ref, v_ref, *rest):
        g_ins, (o_ref, lse_ref), g_outs = rest[:ng], rest[ng:ng + 2], rest[ng + 2:2 * ng + 2]
        m_scr, acc_scr = rest[2 * ng + 2:2 * ng + 4]
        comm = (g_ins, g_outs) + tuple(rest[2 * ng + 4:]) + (list(halved),)
        step = pl.program_id(1)
        qi = qi_ref[step]
        kj = kj_ref[step]
        if ng:
            @pl.when((pl.program_id(0) == 0) & (step == 0))
            def _():
                _allgather_start(*comm)

        @pl.when(kj == 0)
        def _():
            m_scr[...] = jnp.full_like(m_scr, NEG)
            acc_scr[...] = jnp.zeros_like(acc_scr)

        def update(qs, ks, masked):
            nq, nk = qs.stop - qs.start, ks.stop - ks.start
            for i in range(hps):
                own = slice(i * LANES, (i + 1) * LANES)
                sc = _nt(q_ref[qs, own], k_ref[ks, own])
                if masked:
                    sc = jnp.where(_row((nq, nk)) >= _lane((nq, nk)), sc, NEG)
                m_prev = m_scr[i, qs]
                m_new = jnp.maximum(m_prev, jnp.max(sc, axis=-1, keepdims=True))
                p = jnp.exp((sc - jnp.tile(m_new, (1, nk // LANES))).astype(BF16))
                acc_scr[i, qs] = jnp.exp(m_prev - m_new) * acc_scr[i, qs] + _nn(p, v_ref[ks, own])
                m_scr[i, qs] = m_new

        whole, upper, lower = slice(0, t), slice(0, t // 2), slice(t // 2, t)

        @pl.when(kj < qi)
        def _():
            update(whole, whole, False)

        @pl.when(kj == qi)
        def _():
            update(upper, upper, True)
            update(lower, upper, False)
            update(lower, lower, True)
            lane = _lane((t, LANES))
            for pr in range(hps // 2):
                den = [acc_scr[2 * pr + i][:, 64:65] for i in range(2)]
                o_ref[:, pr * LANES:(pr + 1) * LANES] = jnp.where(
                    lane < 64, acc_scr[2 * pr] / den[0], pltpu.roll(acc_scr[2 * pr + 1] / den[1], 64, axis=1)).astype(o_ref.dtype)
                lse_ref[:, pr * LANES:(pr + 1) * LANES] = jnp.where(
                    lane < 64, m_scr[2 * pr] + jnp.log(den[0]), m_scr[2 * pr + 1] + jnp.log(den[1]))

        if ng:
            @pl.when((pl.program_id(0) == n_groups - 1) & (step == n_pairs - 1))
            def _():
                _allgather_finish(*comm)

    wide = hps * LANES
    grid_spec = pltpu.PrefetchScalarGridSpec(
        num_scalar_prefetch=2, grid=(n_groups, n_pairs),
        in_specs=[pl.BlockSpec((t, wide), lambda hg, st, qi, kj: (qi[st], hg)),
                  pl.BlockSpec((t, wide), lambda hg, st, qi, kj: (kj[st], hg)),
                  pl.BlockSpec((t, wide), lambda hg, st, qi, kj: (kj[st], hg))] + [ANY] * ng,
        out_specs=[pl.BlockSpec((t, wide // 2), lambda hg, st, qi, kj: (qi[st], hg))] * 2 + [ANY] * ng,
        scratch_shapes=[pltpu.VMEM((hps, t, LANES), F32)] * 2 + (_allgather_sems(ng) if ng else []))
    return pl.pallas_call(
        body, grid_spec=grid_spec, out_shape=[SDS((s, ATT_W), BF16), SDS((s, ATT_W), F32)] + _allgather_shapes(gather),
        name="fox_fwd", compiler_params=_cp(("arbitrary", "arbitrary")))(qi_arr, kj_arr, q_aug, k_aug, v_aug, *gather)


def fox_bwd(q_aug, k_aug, z, dy, lse, dd, exchange=(), kind="to_chips", *, t=1024, hps=4):
    s = z.shape[0]
    qi_arr, kj_arr, n_pairs = _causal_pairs(s // t, True)
    ne = len(exchange)
    n_groups = N_HEADS // hps
    x_shapes, x_sems, x_start, x_finish = EXCHANGES[kind]

    def body(qi_ref, kj_ref, q_ref, k_ref, v_ref, do_ref, lse_ref, dd_ref, *rest):
        e_ins, (dq_ref, dk_ref, dv_ref), e_outs = rest[:ne], rest[ne:ne + 3], rest[ne + 3:2 * ne + 3]
        comm = (e_ins, e_outs) + tuple(rest[2 * ne + 3:])
        step = pl.program_id(1)
        qi = qi_ref[step]
        kj = kj_ref[step]
        if ne:
            @pl.when((pl.program_id(0) == 0) & (step == 0))
            def _():
                x_start(*comm)

        @pl.when(step == 0)
        def _():
            dq_ref[...] = jnp.zeros_like(dq_ref)

        @pl.when(qi == kj)
        def _():
            dk_ref[...] = jnp.zeros_like(dk_ref)
            dv_ref[...] = jnp.zeros_like(dv_ref)

        def update(qs, ks, masked):
            nq, nk = qs.stop - qs.start, ks.stop - ks.start
            lane = _lane((nq, LANES))
            rows = pl.ds(pl.multiple_of(qi * t + qs.start, nq), nq)
            for pr in range(hps // 2):
                pair = slice(pr * LANES, (pr + 1) * LANES)
                dov = do_ref[qs, pair]
                dv_new = None
                for i in range(2):
                    head = (lane < 64) if i == 0 else (lane >= 64)
                    own = slice((2 * pr + i) * LANES, (2 * pr + i + 1) * LANES)
                    col = slice(pr * LANES + i * 64, pr * LANES + i * 64 + 1)
                    qv = q_ref[qs, own]
                    kv = k_ref[ks, own]
                    sc = _nt(qv, kv)
                    if masked:
                        sc = jnp.where(_row((nq, nk)) >= _lane((nq, nk)), sc, NEG)
                    p = jnp.exp(sc - lse_ref[qs, col])
                    dp = _nt(jnp.where(head, dov, jnp.zeros_like(dov)), v_ref[ks, pair])
                    ds = (p * (dp - dd_ref[qs, col])).astype(BF16)
                    dq_ref[rows, own] += _nn(ds, kv)
                    dk_ref[ks, own] += _tn(ds, qv)
                    dvi = _tn(p.astype(BF16), dov)
                    dv_new = dvi if dv_new is None else jnp.where(head, dvi, dv_new)
                dv_ref[ks, pair] += dv_new

        whole, upper, lower = slice(0, t), slice(0, t // 2), slice(t // 2, t)

        @pl.when(kj < qi)
        def _():
            update(whole, whole, False)

        @pl.when(kj == qi)
        def _():
            update(upper, upper, True)
            update(lower, upper, False)
            update(lower, lower, True)

        if ne:
            @pl.when((pl.program_id(0) == n_groups - 1) & (step == n_pairs - 1))
            def _():
                x_finish(*comm)

    wide, half = hps * LANES, hps // 2 * LANES
    v_blk = Z_VA * ATT_W // half
    grid_spec = pltpu.PrefetchScalarGridSpec(
        num_scalar_prefetch=2, grid=(n_groups, n_pairs),
        in_specs=[pl.BlockSpec((t, wide), lambda hg, st, qi, kj: (qi[st], hg)),
                  pl.BlockSpec((t, wide), lambda hg, st, qi, kj: (kj[st], hg)),
                  pl.BlockSpec((t, half), lambda hg, st, qi, kj: (kj[st], v_blk + hg)),
                  pl.BlockSpec((t, half), lambda hg, st, qi, kj: (qi[st], hg)),
                  pl.BlockSpec((t, half), lambda hg, st, qi, kj: (qi[st], hg)),
                  pl.BlockSpec((t, half), lambda hg, st, qi, kj: (qi[st], hg))] + [ANY] * ne,
        out_specs=[pl.BlockSpec((s, wide), lambda hg, st, qi, kj: (0, hg)),
                   pl.BlockSpec((t, wide), lambda hg, st, qi, kj: (kj[st], hg)),
                   pl.BlockSpec((t, half), lambda hg, st, qi, kj: (kj[st], hg))] + [ANY] * ne,
        scratch_shapes=x_sems(ne) if ne else [])
    return pl.pallas_call(
        body, grid_spec=grid_spec,
        out_shape=[SDS((s, N_HEADS * LANES), F32), SDS((s, N_HEADS * LANES), F32), SDS((s, ATT_W), F32)]
        + x_shapes(exchange),
        name="fox_bwd", compiler_params=_cp(("arbitrary", "arbitrary")))(qi_arr, kj_arr, q_aug, k_aug, z, dy, lse, dd, *exchange)


def fox_post(dq_aug, dk_aug, dv, fa, bfo, *, tb=512):
    s = dv.shape[0]
    n = s // tb

    def body(dq_ref, dk_ref, dv_ref, fa_ref, b_ref, tri_ref, dz_ref, dfa_ref, gb_ref, carry, acc):
        i = pl.program_id(0)

        @pl.when(i == 0)
        def _():
            carry[...] = jnp.zeros_like(carry)
            acc[...] = jnp.zeros_like(acc)

        lane = _lane((tb, LANES))
        d_f = jnp.zeros((tb, LANES), F32)
        for h in range(N_HEADS):
            col = dq_ref[:, h * LANES + 64:h * LANES + 65] - dk_ref[:, h * LANES + 67:h * LANES + 68]
            d_f = jnp.where(lane == h, col, d_f)
        suffix = jnp.dot(tri_ref[...], d_f, preferred_element_type=F32, precision=lax.Precision.HIGHEST) + carry[0:1, :]
        carry[0:1, :] = suffix[0:1, :]
        xv = fa_ref[...] + b_ref[...]
        dx = suffix * (1.0 / (1.0 + jnp.exp(xv)))
        dfa_ref[...] = dx.astype(dfa_ref.dtype)
        acc[...] += jnp.sum(dx.reshape(tb // 8, 8, LANES), axis=0)
        for hp in range(4):
            for src, off, scale in ((dq_ref, 0, QK_SCALE), (dk_ref, ATT_W, 1.0)):
                even = src[:, (2 * hp) * LANES:(2 * hp + 1) * LANES]
                odd = pltpu.roll(src[:, (2 * hp + 1) * LANES:(2 * hp + 2) * LANES], 64, axis=1)
                dz_ref[:, off + hp * LANES:off + (hp + 1) * LANES] = (jnp.where(lane < 64, even, odd) * scale).astype(BF16)
        dz_ref[:, 2 * ATT_W:3 * ATT_W] = dv_ref[...].astype(BF16)

        @pl.when(i == n - 1)
        def _():
            gb_ref[...] = jnp.sum(acc[...], axis=0, keepdims=True)

    rev = lambda i: (n - 1 - i, 0)
    return pl.pallas_call(
        body, grid=(n,),
        in_specs=[pl.BlockSpec((tb, N_HEADS * LANES), rev), pl.BlockSpec((tb, N_HEADS * LANES), rev),
                  pl.BlockSpec((tb, ATT_W), rev), pl.BlockSpec((tb, LANES), rev),
                  pl.BlockSpec((1, LANES), lambda i: (0, 0)), pl.BlockSpec((tb, tb), lambda i: (0, 0))],
        out_specs=[pl.BlockSpec((tb, 3 * ATT_W), rev), pl.BlockSpec((tb, LANES), rev),
                   pl.BlockSpec((1, LANES), lambda i: (0, 0))],
        out_shape=[SDS((s, 3 * ATT_W), BF16), SDS((s, LANES), BF16), SDS((1, LANES), F32)],
        scratch_shapes=[pltpu.VMEM((8, LANES), F32), pltpu.VMEM((8, LANES), F32)],
        name="fox_post", compiler_params=_cp(("arbitrary",)))(dq_aug, dk_aug, dv, fa, bfo, _tri(tb, True))


def rope_cos_sin(s):
    half = ROPE_DIM // 2
    inv_freq = ROPE_THETA ** (-jnp.arange(half, dtype=F32) * 2.0 / ROPE_DIM)
    ang = jnp.arange(s, dtype=F32)[:, None] * inv_freq[None, :]
    return jnp.tile(jnp.cos(ang), (1, LANES // half)), jnp.tile(jnp.sin(ang), (1, LANES // half))


def _rotate(x, cos, sin, sign):
    l64 = _lane(x.shape) & (HEAD_DIM - 1)
    first = l64 < ROPE_DIM // 2
    second = (l64 >= ROPE_DIM // 2) & (l64 < ROPE_DIM)
    from_next = jnp.where(first, -sign * sin, 0.0)
    from_prev = jnp.where(second, sign * sin, 0.0)
    return (x * jnp.where(first | second, cos, 1.0) + pltpu.roll(x, LANES - 8, axis=1) * from_next
            + pltpu.roll(x, 8, axis=1) * from_prev)


def _dil_rows(base, r):
    if r == 1:
        return pl.ds(pl.multiple_of(base, DIL_BLK), DIL_BLK)
    return pl.ds(base, DIL_BLK, stride=r)


def _dil_block(idx, r, nb):
    shift = nb.bit_length() - 1
    rho = idx >> shift
    n = idx & (nb - 1)
    base = rho + n * (r * DIL_BLK)
    return _dil_rows(base, r), _dil_rows(jnp.maximum(base - r * DIL_BLK, rho), r), n > 0


def _cat(a, b):
    return jnp.concatenate([a, b], axis=0)


def _two_heads(v, first_head):
    zero = jnp.zeros_like(v)
    return _cat(jnp.where(first_head, v, zero), jnp.where(first_head, zero, v))


def _dil_bands():
    b = DIL_BLK
    q = _row((2 * b, 2 * b)) & (b - 1)
    col = _lane((2 * b, 2 * b))
    return (col < b) & (col >= q), (col >= b) & (col - b <= q)


def _dil_load_qkv(zq_ref, zk_ref, zv_ref, cos_ref, sin_ref, q_ref, k_ref, v_ref, *, chunk=512):
    def step(i, carry):
        rows = pl.ds(pl.multiple_of(i * chunk, chunk), chunk)
        cos, sin = cos_ref[rows, :], sin_ref[rows, :]
        q_ref[rows, :] = _rotate(zq_ref[rows, :].astype(F32), cos, sin, 1.0) * QK_SCALE
        k_ref[rows, :] = _rotate(zk_ref[rows, :].astype(F32), cos, sin, 1.0)
        v_ref[rows, :] = zv_ref[rows, :].astype(F32)
        return carry

    lax.fori_loop(0, q_ref.shape[0] // chunk, step, 0)


def dil_fwd_all(z, cos_t, sin_t, *, unroll=32):
    s = z.shape[0]
    b = DIL_BLK
    n_blk = s // b

    def body(zq_ref, zk_ref, zv_ref, cos_ref, sin_ref, o_ref, l_ref, q_ref, k_ref, v_ref):
        _dil_load_qkv(zq_ref, zk_ref, zv_ref, cos_ref, sin_ref, q_ref, k_ref, v_ref)
        first_head = _lane((b, LANES)) < 64
        band_prev, band_cur = _dil_bands()
        for g, (_, r) in enumerate(DIL_PATTERNS):
            nb = n_blk // r

            def group(it, carry, g=g, r=r, nb=nb):
                loaded = []
                kc = vc = None
                for u in range(unroll):
                    rows_c, rows_p, has_prev = _dil_block(it * unroll + u, r, nb)
                    if u % min(nb, unroll):
                        kp, vp = kc, vc
                    else:
                        kp, vp = k_ref[rows_p, :].astype(BF16), v_ref[rows_p, :].astype(BF16)
                    kc, vc = k_ref[rows_c, :].astype(BF16), v_ref[rows_c, :].astype(BF16)
                    state = (o_ref[rows_c, :], l_ref[rows_c, :]) if g else None
                    loaded.append((rows_c, has_prev, [q_ref[rows_c, :].astype(BF16), kp, kc, vp, vc], state))
                done = []
                for rows_c, has_prev, (qv, kp, kc, vp, vc), state in loaded:
                    sc = jnp.where(band_cur | (band_prev & has_prev), _nt(_two_heads(qv, first_head), _cat(kp, kc)), NEG)
                    m = jnp.max(sc, axis=-1, keepdims=True)
                    p = jnp.exp(sc - m)
                    den = jnp.sum(p, axis=-1, keepdims=True)
                    both = _nn(p.astype(BF16), _cat(vp, vc)) / den
                    lse2 = m + jnp.log(den)
                    ov = jnp.where(first_head, both[:b], both[b:])
                    lse = jnp.where(first_head, lse2[:b], lse2[b:])
                    if state is not None:
                        m2 = jnp.maximum(state[1], lse)
                        wp = jnp.exp(state[1] - m2)
                        wn = jnp.exp(lse - m2)
                        ov = (wp * state[0] + wn * ov) / (wp + wn)
                        lse = m2 + jnp.log(wp + wn)
                    done.append((rows_c, ov, lse))
                for rows_c, ov, lse in done:
                    o_ref[rows_c, :] = ov
                    l_ref[rows_c, :] = lse
                return carry

            lax.fori_loop(0, n_blk // unroll, group, 0)

    col_blk = lambda k: pl.BlockSpec((s, LANES), lambda hp: (0, 4 * k + hp))
    table = pl.BlockSpec((s, LANES), lambda hp: (0, 0))
    out = pl.BlockSpec((s, LANES), lambda hp: (0, hp))
    return pl.pallas_call(
        body, grid=(4,), in_specs=[col_blk(Z_QB), col_blk(Z_KB), col_blk(Z_VB), table, table], out_specs=[out, out],
        out_shape=[SDS((s, ATT_W), F32)] * 2, scratch_shapes=[pltpu.VMEM((s, LANES), F32)] * 3, name="dil_fwd",
        compiler_params=_cp(("parallel",)))(z, z, z, cos_t, sin_t)


def dil_bwd_all(z, cos_t, sin_t, dy, lse, y, exchange=(), kind="to_chips", *, unroll=16):
    s = z.shape[0]
    b = DIL_BLK
    n_blk = s // b
    ne = len(exchange)
    x_shapes, x_sems, x_start, x_finish = EXCHANGES[kind]

    def body(zq_ref, zk_ref, zv_ref, cos_ref, sin_ref, do_ref, l_ref, y_ref, *rest):
        e_ins, (gq_ref, gk_ref, gv_ref), e_outs = rest[:ne], rest[ne:ne + 3], rest[ne + 3:2 * ne + 3]
        q_ref, k_ref, v_ref, dq_ref, dk_ref, dv_ref = rest[2 * ne + 3:2 * ne + 9]
        comm = (e_ins, e_outs) + tuple(rest[2 * ne + 9:])
        if ne:
            @pl.when(pl.program_id(0) == 0)
            def _():
                x_start(*comm)

        _dil_load_qkv(zq_ref, zk_ref, zv_ref, cos_ref, sin_ref, q_ref, k_ref, v_ref)
        dq_ref[...] = jnp.zeros_like(dq_ref)
        dk_ref[...] = jnp.zeros_like(dk_ref)
        dv_ref[...] = jnp.zeros_like(dv_ref)
        first_head = _lane((b, LANES)) < 64
        band_prev, band_cur = _dil_bands()
        for _, r in DIL_PATTERNS:
            nb = n_blk // r

            def group(it, carry, r=r, nb=nb):
                loaded = []
                kc = vc = None
                for u in range(unroll):
                    rows_c, rows_p, has_prev = _dil_block(it * unroll + u, r, nb)
                    if u % min(nb, unroll):
                        kp, vp = kc, vc
                    else:
                        kp, vp = k_ref[rows_p, :].astype(BF16), v_ref[rows_p, :].astype(BF16)
                    kc, vc = k_ref[rows_c, :].astype(BF16), v_ref[rows_c, :].astype(BF16)
                    vals = [q_ref[rows_c, :].astype(BF16), kp, kc, vp, vc, do_ref[rows_c, :], l_ref[rows_c, :], y_ref[rows_c, :]]
                    loaded.append((rows_c, rows_p, has_prev, vals))
                done = []
                for rows_c, rows_p, has_prev, (qv, kp, kc, vp, vc, dof, lv, yv) in loaded:
                    q2 = _two_heads(qv, first_head)
                    do2 = _two_heads(dof.astype(BF16), first_head)
                    kcat, vcat = _cat(kp, kc), _cat(vp, vc)
                    lse2 = _cat(lv[:, 0:1], lv[:, 64:65])
                    dd2 = jnp.sum(_two_heads(dof * yv, first_head), axis=-1, keepdims=True)
                    p = jnp.exp(jnp.where(band_cur | (band_prev & has_prev), _nt(q2, kcat), NEG) - lse2)
                    ds = (p * (_nt(do2, vcat) - dd2)).astype(BF16)
                    dq2 = _nn(ds, kcat)
                    dkcat = _tn(ds, q2)
                    dvcat = _tn(p.astype(BF16), do2)
                    done.append((rows_c, rows_p, (jnp.where(first_head, dq2[:b], dq2[b:]), dkcat[:b], dkcat[b:],
                                                  dvcat[:b], dvcat[b:])))
                for rows_c, rows_p, (dq, dk_p, dk_c, dv_p, dv_c) in done:
                    dq_ref[rows_c, :] += dq
                    dk_ref[rows_p, :] += dk_p
                    dk_ref[rows_c, :] += dk_c
                    dv_ref[rows_p, :] += dv_p
                    dv_ref[rows_c, :] += dv_c
                return carry

            lax.fori_loop(0, n_blk // unroll, group, 0)

        def finish(i, carry, chunk=512):
            rows = pl.ds(pl.multiple_of(i * chunk, chunk), chunk)
            cos, sin = cos_ref[rows, :], sin_ref[rows, :]
            gq_ref[rows, :] = (_rotate(dq_ref[rows, :], cos, sin, -1.0) * QK_SCALE).astype(BF16)
            gk_ref[rows, :] = _rotate(dk_ref[rows, :], cos, sin, -1.0).astype(BF16)
            gv_ref[rows, :] = dv_ref[rows, :].astype(BF16)
            return carry

        lax.fori_loop(0, s // 512, finish, 0)
        if ne:
            @pl.when(pl.program_id(0) == 3)
            def _():
                x_finish(*comm)

    col_blk = lambda k: pl.BlockSpec((s, LANES), lambda hp: (0, 4 * k + hp))
    table = pl.BlockSpec((s, LANES), lambda hp: (0, 0))
    nat = pl.BlockSpec((s, LANES), lambda hp: (0, hp))
    return pl.pallas_call(
        body, grid=(4,), in_specs=[col_blk(Z_QB), col_blk(Z_KB), col_blk(Z_VB), table, table, nat, nat, nat] + [ANY] * ne,
        out_specs=[nat, nat, nat] + [ANY] * ne, out_shape=[SDS((s, ATT_W), BF16)] * 3 + x_shapes(exchange),
        scratch_shapes=[pltpu.VMEM((s, LANES), F32)] * 6 + (x_sems(ne) if ne else []), name="dil_bwd",
        compiler_params=_cp(("arbitrary",)))(z, z, z, cos_t, sin_t, dy, lse, y, *exchange)


def _sigmoid(v):
    return 1.0 / (1.0 + jnp.exp(-v))


def gate_mix(ya, yb, wa, wb, z, *, tm=2048, tn=512):
    s = ya.shape[0]
    d = wa.shape[1]
    ga_blk = 3 * ATT_W * 2 // tn
    gb_blk = ga_blk + d // tn

    def body(ya_ref, yb_ref, wa_ref, wb_ref, ga_ref, gb_ref, pa_ref, pb_ref, mx_ref):
        pa = _nn(ya_ref[...], wa_ref[...])
        pb = _nn(yb_ref[...].astype(BF16), wb_ref[...])
        pa_ref[...] = pa.astype(BF16)
        pb_ref[...] = pb.astype(BF16)
        mx_ref[...] = (_sigmoid(ga_ref[...].astype(F32)) * pa + _sigmoid(gb_ref[...].astype(F32)) * pb).astype(BF16)

    out = pl.BlockSpec((tm, tn), lambda i, j: (i, j))
    return pl.pallas_call(
        body, grid=(s // tm, d // tn),
        in_specs=[pl.BlockSpec((tm, ATT_W), lambda i, j: (i, 0)), pl.BlockSpec((tm, ATT_W), lambda i, j: (i, 0)),
                  pl.BlockSpec((ATT_W, tn), lambda i, j: (0, j)), pl.BlockSpec((ATT_W, tn), lambda i, j: (0, j)),
                  pl.BlockSpec((tm, tn), lambda i, j: (i, ga_blk + j)), pl.BlockSpec((tm, tn), lambda i, j: (i, gb_blk + j))],
        out_specs=[out, out, out], out_shape=[SDS((s, d), BF16)] * 3, name="gate_mix",
        compiler_params=_cp(("parallel", "parallel")))(ya, yb, wa, wb, z, z)


def mix_bwd(dy, w_o, z, pa, pb, wo_a, wo_b, ya, *, tm=512):
    s, d = dy.shape

    def body(dy_ref, wo_ref, ga_ref, gb_ref, pa_ref, pb_ref, wa_ref, wb_ref, ya_ref,
             dpa_ref, dpb_ref, dg_ref, dya_ref, dyb_ref, dd_ref):
        dm = _nt(dy_ref[...], wo_ref[...])
        sa = _sigmoid(ga_ref[...].astype(F32))
        sb = _sigmoid(gb_ref[...].astype(F32))
        dpa = (dm * sa).astype(BF16)
        dpb = (dm * sb).astype(BF16)
        dpa_ref[...] = dpa
        dpb_ref[...] = dpb
        dg_ref[:, 0:d] = (dm * pa_ref[...].astype(F32) * sa * (1.0 - sa)).astype(BF16)
        dg_ref[:, d:2 * d] = (dm * pb_ref[...].astype(F32) * sb * (1.0 - sb)).astype(BF16)
        dya = _nt(dpa, wa_ref[...]).astype(BF16)
        dya_ref[...] = dya
        dyb_ref[...] = _nt(dpb, wb_ref[...])
        lane = _lane((tm, LANES))
        for pr in range(ATT_W // LANES):
            pair = slice(pr * LANES, (pr + 1) * LANES)
            prod = dya[:, pair].astype(F32) * ya_ref[:, pair].astype(F32)
            lo = jnp.sum(jnp.where(lane < 64, prod, 0.0), axis=-1, keepdims=True)
            hi = jnp.sum(jnp.where(lane >= 64, prod, 0.0), axis=-1, keepdims=True)
            dd_ref[:, pair] = jnp.where(lane < 64, lo, hi)

    row = pl.BlockSpec((tm, d), lambda i: (i, 0))
    att = pl.BlockSpec((tm, ATT_W), lambda i: (i, 0))
    whole = lambda a: pl.BlockSpec(a.shape, lambda i: (0, 0))
    return pl.pallas_call(
        body, grid=(s // tm,),
        in_specs=[row, whole(w_o), pl.BlockSpec((tm, d), lambda i: (i, 3)), pl.BlockSpec((tm, d), lambda i: (i, 4)), row, row,
                  whole(wo_a), whole(wo_b), att],
        out_specs=[row, row, pl.BlockSpec((tm, 2 * d), lambda i: (i, 0)), att, att, att],
        out_shape=[SDS((s, d), BF16), SDS((s, d), BF16), SDS((s, 2 * d), BF16), SDS((s, ATT_W), BF16),
                   SDS((s, ATT_W), F32), SDS((s, ATT_W), F32)], name="mix_bwd",
        compiler_params=_cp(("parallel",)))(dy, w_o, z, z, pa, pb, wo_a, wo_b, ya)


GELU_C = math.sqrt(2.0 / math.pi)


def _gelu_parts(a):
    a2 = a * a
    th = jnp.tanh(a * (GELU_C + (GELU_C * 0.044715) * a2))
    half = 0.5 * a
    gelu = half + half * th
    dgelu = (0.5 + 0.5 * th) + half * (1.0 - th * th) * (GELU_C + (3.0 * GELU_C * 0.044715) * a2)
    return gelu, dgelu


def _causal_taps(u, before):
    row = _row(u.shape)
    r1 = jnp.where(row == 0, before[7:8, :], pltpu.roll(u, 1, axis=0))
    r2 = jnp.where(row == 0, before[6:7, :], jnp.where(row == 1, before[7:8, :], pltpu.roll(u, 2, axis=0)))
    return r1, r2


def ffn_up(h, wa, wb, cw, cb, *, tm=2048, tn=256):
    s, d = h.shape
    f = wa.shape[1]
    nj = f // tn

    def body(h_ref, wa_ref, wb_ref, cwa_ref, cwb_ref, cba_ref, cbb_ref, ua_ref, ub_ref, ca_ref, cbo_ref, m_ref, carry):
        @pl.when(pl.program_id(1) == 0)
        def _():
            carry[...] = jnp.zeros_like(carry)

        conv = []
        for k, (w_ref, cw_ref, cb_ref, u_ref, c_ref) in enumerate(((wa_ref, cwa_ref, cba_ref, ua_ref, ca_ref),
                                                                   (wb_ref, cwb_ref, cbb_ref, ub_ref, cbo_ref))):
            u16 = _nn(h_ref[...], w_ref[...]).astype(BF16)
            u_ref[...] = u16
            u = u16.astype(F32)
            r1, r2 = _causal_taps(u, carry[k])
            carry[k] = u[tm - 8:tm, :]
            c16 = (cw_ref[0:1, :] * r2 + cw_ref[1:2, :] * r1 + cw_ref[2:3, :] * u + cb_ref[...]).astype(BF16)
            c_ref[...] = c16
            conv.append(c16.astype(F32))
        m_ref[...] = (_gelu_parts(conv[0])[0] * conv[1]).astype(BF16)

    out = pl.BlockSpec((tm, tn), lambda j, i: (i, j))
    return pl.pallas_call(
        body, grid=(nj, s // tm),
        in_specs=[pl.BlockSpec((tm, d), lambda j, i: (i, 0)),
                  pl.BlockSpec((d, tn), lambda j, i: (0, j)), pl.BlockSpec((d, tn), lambda j, i: (0, j)),
                  pl.BlockSpec((3, tn), lambda j, i: (0, j)), pl.BlockSpec((3, tn), lambda j, i: (0, nj + j)),
                  pl.BlockSpec((1, tn), lambda j, i: (0, j)), pl.BlockSpec((1, tn), lambda j, i: (0, nj + j))],
        out_specs=[out] * 5, out_shape=[SDS((s, f), BF16)] * 5,
        scratch_shapes=[pltpu.VMEM((2, 8, tn), F32)], name="ffn_up",
        compiler_params=_cp(("parallel", "arbitrary")))(h, wa, wb, cw, cw, cb, cb)


def ffn_bwd(dm, ua, ub, ca, cbo, cw, *, tm=2048, tn=256):
    s, f = dm.shape
    nj = f // tn
    ni = s // tm

    def body(dm_ref, ua_ref, ub_ref, ca_ref, cbo_ref, cwa_ref, cwb_ref, dua_ref, dub_ref, ga_ref, gb_ref, carry):
        @pl.when(pl.program_id(1) == 0)
        def _():
            carry[...] = jnp.zeros_like(carry)
            ga_ref[...] = jnp.zeros_like(ga_ref)
            gb_ref[...] = jnp.zeros_like(gb_ref)

        row = _row((tm, tn))
        dmv = dm_ref[...].astype(F32)
        gelu, dgelu = _gelu_parts(ca_ref[...].astype(F32))
        dcs = (dmv * cbo_ref[...].astype(F32) * dgelu, dmv * gelu)
        for k, (dc, u_ref, cw_ref, du_ref, g_ref) in enumerate(((dcs[0], ua_ref, cwa_ref, dua_ref, ga_ref),
                                                                (dcs[1], ub_ref, cwb_ref, dub_ref, gb_ref))):
            u = u_ref[...].astype(F32)
            after = carry[k]
            n1 = jnp.where(row == tm - 1, after[0:1, :], pltpu.roll(dc, tm - 1, axis=0))
            n2 = jnp.where(row == tm - 2, after[0:1, :], jnp.where(row == tm - 1, after[1:2, :], pltpu.roll(dc, tm - 2, axis=0)))
            g_ref[0:1, :] += jnp.sum(n2 * u, axis=0, keepdims=True)
            g_ref[1:2, :] += jnp.sum(n1 * u, axis=0, keepdims=True)
            g_ref[2:3, :] += jnp.sum(dc * u, axis=0, keepdims=True)
            g_ref[3:4, :] += jnp.sum(dc, axis=0, keepdims=True)
            du_ref[...] = (cw_ref[2:3, :] * dc + cw_ref[1:2, :] * n1 + cw_ref[0:1, :] * n2).astype(BF16)
            carry[k] = dc[0:8, :]

    tile = pl.BlockSpec((tm, tn), lambda j, i: (ni - 1 - i, j))
    gspec = pl.BlockSpec((8, tn), lambda j, i: (0, j))
    return pl.pallas_call(
        body, grid=(nj, ni),
        in_specs=[tile] * 5 + [pl.BlockSpec((3, tn), lambda j, i: (0, j)), pl.BlockSpec((3, tn), lambda j, i: (0, nj + j))],
        out_specs=[tile, tile, gspec, gspec],
        out_shape=[SDS((s, f), BF16), SDS((s, f), BF16), SDS((8, f), F32), SDS((8, f), F32)],
        scratch_shapes=[pltpu.VMEM((2, 8, tn), F32)], name="ffn_bwd",
        compiler_params=_cp(("parallel", "arbitrary")))(dm, ua, ub, ca, cbo, cw, cw)


def adamw(w, g, m, v, *, name, tr=None):
    r = w.shape[0]
    rest = w.shape[1:]
    if tr is None:
        tr = r
        for cand in (256, 128, 64, 32, 16, 8):
            if r % cand == 0:
                tr = cand
                break

    def body(w_ref, g_ref, m_ref, v_ref, d_ref, nm_ref, nv_ref):
        gv = g_ref[...]
        mn = ADAM_B1 * m_ref[...] + (1.0 - ADAM_B1) * gv
        vn = ADAM_B2 * v_ref[...] + (1.0 - ADAM_B2) * (gv * gv)
        m_hat = mn / (1.0 - ADAM_B1 ** ADAM_STEP)
        v_hat = vn / (1.0 - ADAM_B2 ** ADAM_STEP)
        d_ref[...] = -ADAM_LR * (m_hat / (jnp.sqrt(v_hat) + ADAM_EPS) + ADAM_WD * w_ref[...])
        nm_ref[...] = mn
        nv_ref[...] = vn

    blk = pl.BlockSpec((tr,) + rest, lambda i: (i,) + (0,) * len(rest))
    return pl.pallas_call(body, grid=(r // tr,), in_specs=[blk] * 4, out_specs=[blk] * 3, out_shape=[SDS(w.shape, F32)] * 3,
                          name=name, compiler_params=_cp(("parallel",)))(w, g, m, v)


def adamw_rows_view(w, g_mine, g_full, m, v, c_arr, *, name, tc=256):
    r, _, c = w.shape
    per_half = c // 2 // tc

    def body(c_ref, w_ref, gm_ref, gf_ref, m_ref, v_ref, d_ref, nm_ref, nv_ref, go_ref):
        mine = (pl.program_id(0) >> (per_half.bit_length() - 1)) == c_ref[0]
        gv = jnp.where(mine, gm_ref[...], gf_ref[...])
        mn = ADAM_B1 * m_ref[:, 0, :] + (1.0 - ADAM_B1) * gv
        vn = ADAM_B2 * v_ref[:, 0, :] + (1.0 - ADAM_B2) * (gv * gv)
        m_hat = mn / (1.0 - ADAM_B1 ** ADAM_STEP)
        v_hat = vn / (1.0 - ADAM_B2 ** ADAM_STEP)
        d_ref[:, 0, :] = -ADAM_LR * (m_hat / (jnp.sqrt(v_hat) + ADAM_EPS) + ADAM_WD * w_ref[:, 0, :])
        nm_ref[:, 0, :] = mn
        nv_ref[:, 0, :] = vn
        go_ref[:, 0, :] = gv

    b3 = pl.BlockSpec((r, 1, tc), lambda i, c_ref: (0, 0, i))
    own = pl.BlockSpec((r, tc), lambda i, c_ref: (0, jnp.clip(i - c_ref[0] * per_half, 0, per_half - 1)))
    full = pl.BlockSpec((r, tc), lambda i, c_ref: (0, i))
    grid_spec = pltpu.PrefetchScalarGridSpec(num_scalar_prefetch=1, grid=(c // tc,), in_specs=[b3, own, full, b3, b3],
                                             out_specs=[b3] * 4)
    return pl.pallas_call(body, grid_spec=grid_spec, out_shape=[SDS(w.shape, F32)] * 4, name=name,
                          compiler_params=_cp(("parallel",)))(c_arr, w, g_mine, g_full, m, v)


ANY = pl.BlockSpec(memory_space=pl.ANY)
ICI_KINDS = ("x", "y", "xy")


def _coords():
    return lax.axis_index("x"), lax.axis_index("y"), lax.axis_index("c")


def _peer(kind, x, y, c):
    if kind == "c":
        return (x, y, 1 - c)
    if kind == "x":
        return (1 - x, y, c)
    if kind == "y":
        return (x, 1 - y, c)
    return (1 - x, 1 - y, c)


def _chip_of(p):
    return 2 * p[0] + p[1]


def _half(rows, which):
    h = rows // 2
    return pl.ds(pl.multiple_of(which * h, 16), h)


def _remote(src, dst, send_sem, recv_sem, to):
    return pltpu.make_async_remote_copy(src_ref=src, dst_ref=dst, send_sem=send_sem, recv_sem=recv_sem,
                                        device_id=to, device_id_type=MESH)


def allgather_balanced(shard, *, name):
    r, cols = shard.shape
    h, q = r // 2, r // 4

    def body(in_ref, out_ref, send_sems, recv_sems):
        x, y, c = _coords()
        me, sibling = (x, y, c), (x, y, 1 - c)
        nbr = ((1 - x, y, c), (x, 1 - y, c))
        chip = (2 * (1 - x) + y, 2 * x + (1 - y), 2 * (1 - x) + (1 - y))
        quarter = lambda core, i: pl.ds(pl.multiple_of(core * h + i * q, 16), q)
        sent = []

        def go(src, dst, slot, to):
            cp = _remote(src, dst, send_sems.at[slot], recv_sems.at[slot], to)
            cp.start()
            sent.append(cp)

        def landed(region, slot):
            _remote(region, region, send_sems.at[slot], recv_sems.at[slot], me).wait_recv()

        for i in range(2):
            for k in range(2):
                qi = k if i == 0 else 1 - k
                go(in_ref.at[quarter(c, qi)], out_ref.at[2 * x + y, quarter(c, qi)], 2 * k + qi, nbr[k])
        for k in range(2):
            piece = out_ref.at[chip[k], quarter(c, k)]
            landed(piece, 2 * k + k)
            go(piece, piece, 4 + k, nbr[1 - k])
            go(piece, piece, 6 + 2 * k + k, sibling)
        for k in range(2):
            piece = out_ref.at[chip[k], quarter(c, 1 - k)]
            landed(piece, 2 * k + 1 - k)
            go(piece, piece, 6 + 2 * k + 1 - k, sibling)
        for k in range(2):
            piece = out_ref.at[chip[2], quarter(c, k)]
            landed(piece, 4 + k)
            go(piece, piece, 10 + k, sibling)
        for k in range(2):
            for i in range(2):
                landed(out_ref.at[chip[k], quarter(1 - c, i)], 6 + 2 * k + i)
            landed(out_ref.at[chip[2], quarter(1 - c, k)], 10 + k)
        for cp in sent:
            cp.wait_send()

    return pl.pallas_call(
        body, in_specs=[ANY], out_specs=ANY, out_shape=SDS((4,) + shard.shape, shard.dtype),
        scratch_shapes=[pltpu.SemaphoreType.DMA((12,)), pltpu.SemaphoreType.DMA((12,))], name=name)(shard)


def _allgather_shapes(shards):
    return [SDS((4,) + a.shape, a.dtype) for a in shards]


def _allgather_sems(n):
    return [pltpu.SemaphoreType.DMA((n, 6)), pltpu.SemaphoreType.DMA((n, 6))]


def _allgather_rows(ref, is_halved, which):
    r = ref.shape[0]
    return _half(r, which) if is_halved else pl.ds(0, r)


def _allgather_first(ins, outs, send_sems, recv_sems, halved):
    x, y, c = _coords()
    my_chip = 2 * x + y
    cps = []
    for w in range(len(ins)):
        rows = _allgather_rows(ins[w], halved[w], c)
        for k, kind in enumerate(ICI_KINDS):
            cps.append(_remote(ins[w].at[rows], outs[w].at[my_chip, rows], send_sems.at[w, k], recv_sems.at[w, k],
                               _peer(kind, x, y, c)))
    return cps


def _allgather_start(ins, outs, send_sems, recv_sems, halved):
    for cp in _allgather_first(ins, outs, send_sems, recv_sems, halved):
        cp.start()


def _allgather_finish(ins, outs, send_sems, recv_sems, halved):
    x, y, c = _coords()
    me = (x, y, c)
    second = []
    for w in range(len(ins)):
        for k, kind in enumerate(ICI_KINDS):
            landed = outs[w].at[_chip_of(_peer(kind, x, y, c)), _allgather_rows(ins[w], halved[w], c)]
            _remote(landed, landed, send_sems.at[w, k], recv_sems.at[w, k], me).wait_recv()
            if halved[w]:
                cp = _remote(landed, landed, send_sems.at[w, 3 + k], recv_sems.at[w, 3 + k], _peer("c", x, y, c))
                cp.start()
                second.append(cp)
    for w in range(len(ins)):
        if halved[w]:
            for k, kind in enumerate(ICI_KINDS):
                other = outs[w].at[_chip_of(_peer(kind, x, y, c)), _allgather_rows(ins[w], True, 1 - c)]
                _remote(other, other, send_sems.at[w, 3 + k], recv_sems.at[w, 3 + k], me).wait_recv()
    for cp in _allgather_first(ins, outs, send_sems, recv_sems, halved) + second:
        cp.wait_send()


def _half_of(ref, by_cols, which):
    lead = (slice(None),) * (len(ref.shape) - 2)
    if by_cols:
        h = ref.shape[-1] // 2
        return ref.at[lead + (slice(None), pl.ds(pl.multiple_of(which * h, LANES), h))]
    return ref.at[lead + (_half(ref.shape[-2], which),)]


def _half_shape(shape, by_cols):
    return shape[:-1] + (shape[-1] // 2,) if by_cols else shape[:-2] + (shape[-2] // 2, shape[-1])


def grads_to_sibling(gs, by_cols, *, name):
    n = len(gs)

    def body(*refs):
        ins, outs = refs[:n], refs[n:2 * n]
        send_sems, recv_sems = refs[2 * n:]
        x, y, c = _coords()
        cps = []
        for w in range(n):
            cp = _remote(_half_of(ins[w], by_cols[w], 1 - c), outs[w], send_sems.at[w], recv_sems.at[w], _peer("c", x, y, c))
            cp.start()
            cps.append(cp)
        for cp in cps:
            cp.wait()

    return pl.pallas_call(
        body, in_specs=[ANY] * n, out_specs=[ANY] * n,
        out_shape=[SDS(_half_shape(a.shape, bc), a.dtype) for a, bc in zip(gs, by_cols)],
        scratch_shapes=[pltpu.SemaphoreType.DMA((n,)), pltpu.SemaphoreType.DMA((n,))], name=name)(*gs)


def _to_chips_shapes(ps):
    return [SDS((3,) + a.shape[1:], a.dtype) for a in ps]


def _to_chips_sems(n):
    return [pltpu.SemaphoreType.DMA((n, 3)), pltpu.SemaphoreType.DMA((n, 3))]


def _to_chips_copies(ins, outs, send_sems, recv_sems):
    x, y, c = _coords()
    cps = []
    for w in range(len(ins)):
        for k, kind in enumerate(ICI_KINDS):
            to = _peer(kind, x, y, c)
            cps.append(_remote(ins[w].at[_chip_of(to)], outs[w].at[k], send_sems.at[w, k], recv_sems.at[w, k], to))
    return cps


def _to_chips_start(ins, outs, send_sems, recv_sems):
    for cp in _to_chips_copies(ins, outs, send_sems, recv_sems):
        cp.start()


def _to_chips_finish(ins, outs, send_sems, recv_sems):
    for cp in _to_chips_copies(ins, outs, send_sems, recv_sems):
        cp.wait()


def _to_owners_shapes(ps):
    return [SDS((7, a.shape[1] // 2, a.shape[2]), a.dtype) for a in ps]


def _to_owners_sems(n):
    return [pltpu.SemaphoreType.DMA((n, 7)), pltpu.SemaphoreType.DMA((n, 7))]


def _to_owners_copies(ins, outs, send_sems, recv_sems):
    x, y, c = _coords()
    cps = []
    for w in range(len(ins)):
        rows = ins[w].shape[1]
        for k, kind in enumerate(ICI_KINDS):
            px, py, _ = _peer(kind, x, y, c)
            for h in range(2):
                cps.append(_remote(ins[w].at[2 * px + py, _half(rows, h)], outs[w].at[2 * k + c],
                                   send_sems.at[w, 2 * k + h], recv_sems.at[w, 2 * k + c], (px, py, h)))
        cps.append(_remote(ins[w].at[2 * x + y, _half(rows, 1 - c)], outs[w].at[6], send_sems.at[w, 6], recv_sems.at[w, 6],
                           _peer("c", x, y, c)))
    return cps


def _to_owners_start(ins, outs, send_sems, recv_sems):
    for cp in _to_owners_copies(ins, outs, send_sems, recv_sems):
        cp.start()


def _to_owners_finish(ins, outs, send_sems, recv_sems):
    for cp in _to_owners_copies(ins, outs, send_sems, recv_sems):
        cp.wait_send()
    for w in range(len(ins)):
        for slot in range(7):
            got = outs[w].at[slot]
            _remote(got, got, send_sems.at[w, slot], recv_sems.at[w, slot], _coords()).wait_recv()


EXCHANGES = {"to_chips": (_to_chips_shapes, _to_chips_sems, _to_chips_start, _to_chips_finish),
             "to_owners": (_to_owners_shapes, _to_owners_sems, _to_owners_start, _to_owners_finish)}


def halves_to_full(hs, by_cols, *, name):
    n = len(hs)

    def body(*refs):
        ins, outs = refs[:n], refs[n:2 * n]
        send_sems, recv_sems = refs[2 * n:]
        x, y, c = _coords()
        cps = []
        for w in range(n):
            cp = _remote(ins[w], _half_of(outs[w], by_cols[w], c), send_sems.at[w], recv_sems.at[w], _peer("c", x, y, c))
            cp.start()
            cps.append(cp)
        for cp in cps:
            cp.wait()

    return pl.pallas_call(
        body, in_specs=[ANY] * n, out_specs=[ANY] * n,
        out_shape=[SDS((a.shape[0], 2 * a.shape[1]) if bc else (2 * a.shape[0], a.shape[1]), a.dtype)
                   for a, bc in zip(hs, by_cols)],
        scratch_shapes=[pltpu.SemaphoreType.DMA((n,)), pltpu.SemaphoreType.DMA((n,))],
        name=name)(*hs)


def _row_tile(rows):
    for cand in (256, 192, 176, 128, 64, 32, 16):
        if rows % cand == 0:
            return cand
    return rows


def chip_sum(g, recv, c_arr, by_cols, *, name):
    _, r, cols = g.shape

    def body(c_ref, g_ref, r_ref, f_ref, b_ref):
        tot = g_ref[...] + r_ref[...]
        f_ref[...] = tot
        b_ref[...] = tot.astype(BF16)

    if by_cols:
        tc = 4 * LANES
        nblk = cols // 2 // tc
        shape = (4, r, cols // 2)
        blk = pl.BlockSpec((None, r, tc), lambda j, i, c_ref: (j, 0, i))
        mine = pl.BlockSpec((None, r, tc), lambda j, i, c_ref: (j, 0, c_ref[0] * nblk + i))
    else:
        tr = _row_tile(r // 2)
        nblk = r // 2 // tr
        shape = (4, r // 2, cols)
        blk = pl.BlockSpec((None, tr, cols), lambda j, i, c_ref: (j, i, 0))
        mine = pl.BlockSpec((None, tr, cols), lambda j, i, c_ref: (j, c_ref[0] * nblk + i, 0))
    grid_spec = pltpu.PrefetchScalarGridSpec(num_scalar_prefetch=1, grid=(4, nblk), in_specs=[mine, blk], out_specs=[blk, blk])
    return pl.pallas_call(body, grid_spec=grid_spec, out_shape=[SDS(shape, F32), SDS(shape, BF16)],
                          name=name, compiler_params=_cp(("parallel", "parallel")))(c_arr, g, recv)


def final_sum(pf, recv, chip_arr, *, name):
    _, h, cols = pf.shape
    tr = _row_tile(h)

    def body(chip_ref, p_ref, r_ref, o_ref):
        o_ref[...] = ((p_ref[...] + r_ref[0].astype(F32)) + r_ref[1].astype(F32)) + r_ref[2].astype(F32)

    grid_spec = pltpu.PrefetchScalarGridSpec(
        num_scalar_prefetch=1, grid=(h // tr,),
        in_specs=[pl.BlockSpec((None, tr, cols), lambda i, chip_ref: (chip_ref[0], i, 0)),
                  pl.BlockSpec((3, tr, cols), lambda i, chip_ref: (0, i, 0))],
        out_specs=pl.BlockSpec((tr, cols), lambda i, chip_ref: (i, 0)))
    return pl.pallas_call(body, grid_spec=grid_spec, out_shape=SDS((h, cols), F32), name=name,
                          compiler_params=_cp(("parallel",)))(chip_arr, pf, recv)


def owner_sum(g, recv, pos_arr, *, name):
    _, r, cols = g.shape
    h = r // 2
    tr = _row_tile(h)
    nblk = h // tr

    def body(pos_ref, g_ref, r_ref, o_ref):
        tot = g_ref[...]
        for slot in range(7):
            tot = tot + r_ref[slot].astype(F32)
        o_ref[...] = tot

    grid_spec = pltpu.PrefetchScalarGridSpec(
        num_scalar_prefetch=1, grid=(nblk,),
        in_specs=[pl.BlockSpec((None, tr, cols), lambda i, pos: (pos[0], pos[1] * nblk + i, 0)),
                  pl.BlockSpec((7, tr, cols), lambda i, pos: (0, i, 0))],
        out_specs=pl.BlockSpec((tr, cols), lambda i, pos: (i, 0)))
    return pl.pallas_call(body, grid_spec=grid_spec, out_shape=SDS((h, cols), F32), name=name,
                          compiler_params=_cp(("parallel",)))(pos_arr, g, recv)


def allreduce_small(v, *, name):
    rws, cols = v.shape

    def body(v_ref, all_ref, sum_ref, send_sems, recv_sems, local_sem):
        x, y, c = _coords()
        me, sibling = (x, y, c), (x, y, 1 - c)
        chips = [(1 - x, y), (x, 1 - y), (1 - x, 1 - y)]

        def rows(px, py, pc):
            return all_ref.at[pl.ds(pl.multiple_of((4 * px + 2 * py + pc) * rws, 8), rws), :]

        def copy(k, block, to, src=None):
            return _remote(rows(*block) if src is None else src, rows(*block), send_sems.at[k], recv_sems.at[k], to)

        mine = pltpu.make_async_copy(v_ref, rows(*me), local_sem)
        mine.start()
        first = [copy(0, me, sibling, src=v_ref)]
        first += [copy(1 + j, me, (*chip, c), src=v_ref) for j, chip in enumerate(chips)]
        for cp in first:
            cp.start()
        passed = [copy(4 + j, (*chip, c), sibling) for j, chip in enumerate(chips)]
        for j, chip in enumerate(chips):
            copy(1 + j, (*chip, c), me).wait_recv()
            passed[j].start()
        copy(0, sibling, me).wait_recv()
        for j, chip in enumerate(chips):
            copy(4 + j, (*chip, 1 - c), me).wait_recv()
        for cp in first + passed:
            cp.wait_send()
        mine.wait()
        tot = all_ref[0:rws, :]
        for dev in range(1, 8):
            tot = tot + all_ref[dev * rws:(dev + 1) * rws, :]
        sum_ref[...] = tot

    vm = pl.BlockSpec(memory_space=pltpu.VMEM)
    return pl.pallas_call(
        body, in_specs=[vm], out_specs=[vm, vm],
        out_shape=[SDS((8 * rws, cols), v.dtype), SDS((rws, cols), v.dtype)],
        scratch_shapes=[pltpu.SemaphoreType.DMA((7,)), pltpu.SemaphoreType.DMA((7,)), pltpu.SemaphoreType.DMA],
        name=name)(v)[1]


def _pack_rows(parts, rows):
    out = []
    for a, r in zip(parts, rows):
        flat = a.reshape(-1)
        flat = jnp.pad(flat, (0, r * LANES - flat.shape[0]))
        out.append(flat.reshape(r, LANES))
    return jnp.concatenate(out, axis=0)


def _unpack_rows(packed, shapes, rows):
    out, at = [], 0
    for shp, r in zip(shapes, rows):
        size = int(np.prod(shp))
        out.append(packed[at:at + r].reshape(-1)[:size].reshape(shp))
        at += r
    return out


def kernel(x, g_pre_mix, w_in, b_forget, w_o_fox, w_o_dil, w_out, g_post_mix, g_pre_ffn, w_up, conv_w, conv_b, w_down, g_post_ffn, loss_target, m_g_pre_mix, m_w_in, m_b_forget, m_w_o_fox, m_w_o_dil, m_w_out, m_g_post_mix, m_g_pre_ffn, m_w_up, m_conv_w, m_conv_b, m_w_down, m_g_post_ffn, v_g_pre_mix, v_w_in, v_b_forget, v_w_o_fox, v_w_o_dil, v_w_out, v_g_post_mix, v_g_pre_ffn, v_w_up, v_conv_w, v_conv_b, v_w_down, v_g_post_ffn):
    xi, yi, ci = _coords()
    chip = 2 * xi + yi
    c_arr = jnp.reshape(ci, (1,)).astype(jnp.int32)
    chip_arr = jnp.reshape(chip, (1,)).astype(jnp.int32)
    xs = x[0]
    target = loss_target[0]
    s, d = xs.shape
    f_half = w_down.shape[1] * 4
    cols_in = w_in.shape[2]

    big = (w_in, w_o_fox, w_o_dil, w_out, w_up, w_down)
    shards = [w[0].astype(BF16) for w in big]
    a_in = allgather_balanced(shards[0], name="allgather_w_in")
    w_in_full = jnp.concatenate([jnp.where(chip == j, shards[0], a_in[j]) for j in range(4)], axis=1)
    nf = N_HEADS
    e_a, e_b = 3 * ATT_W, 3 * ATT_W + nf
    wz = jnp.concatenate([w_in_full[:, :e_a], w_in_full[:, e_b:]], axis=1)
    wf = jnp.pad(w_in_full[:, e_a:e_b], ((0, 0), (0, LANES - nf)))
    cb = conv_b
    bfo = jnp.pad(b_forget, ((0, 0), (0, LANES - nf)))

    h1 = rmsnorm_fwd(xs, g_pre_mix)
    z = mm([(h1, d, 0)], [(wz, d, 0)], nt=False, out_dtype=BF16, tm=s, tn=512, name="in_proj")
    fa = mm([(h1, d, 0)], [(wf, d, 0)], nt=False, out_dtype=F32, tm=s, tn=LANES, name="in_proj_forget")
    q_aug, k_aug, v_aug = fox_prep(z, fa, bfo)
    later = shards[1:] + [conv_w[0]]
    ya, lse_a, *late = fox_fwd(q_aug, k_aug, v_aug, gather=later, halved=[True] * 5 + [False], hps=N_HEADS)
    a_of, a_od, a_out, a_up, a_down, a_cw = [
        lax.dynamic_update_index_in_dim(a4, own, chip, 0) for a4, own in zip(late, later)]
    cw = jnp.concatenate([a_cw[j] for j in range(4)], axis=1)
    wo_a = jnp.concatenate([a_of[j] for j in range(4)], axis=1)
    wo_b = jnp.concatenate([a_od[j] for j in range(4)], axis=1)
    w_o = a_out.reshape(d, d)
    w_dn = a_down.reshape(f_half, d)
    wu_a = jnp.concatenate([a_up[0], a_up[1]], axis=1)
    wu_b = jnp.concatenate([a_up[2], a_up[3]], axis=1)
    cos_t, sin_t = rope_cos_sin(s)
    yb, lse_b = dil_fwd_all(z, cos_t, sin_t)
    pa, pb, mixed = gate_mix(ya, yb, wo_a, wo_b, z)
    y1, x1, h2 = proj_norm_res(mixed, w_o, g_post_mix, xs, g_pre_ffn, tm=1024, name="out_proj")
    ua, ub, conv_a, conv_bh, mid = ffn_up(h2, wu_a, wu_b, cw, cb)
    dout, dy2, gg_post_ffn, sq = proj_norm_loss(mid, w_dn, g_post_ffn, x1, target, name="down_proj")
    loss = lax.psum(0.5 * sq[0, 0] / d, ("x", "y", "c"))

    dmid = mm([(dy2, d, 0)], [(w_dn, d, 0)], nt=True, out_dtype=BF16, tm=2048, tn=f_half // 2, name="down_dgrad")
    dw_down, dw_down16 = wgrad((mid, f_half, 0), dy2, tk=f_half // 2, tn=1024, ts=2048, name="down_wgrad", bf16_copy=True)
    dua, dub, gc_a, gc_b = ffn_bwd(dmid, ua, ub, conv_a, conv_bh, cw)
    dx1, dy1, gg_pre_ffn, gg_post_mix = mm_norm_bwd(
        [(dua, f_half, 0), (dub, f_half, 0)], [(wu_a, f_half, 0), (wu_b, f_half, 0)],
        [(x1, g_pre_ffn, dout, F32), (y1, g_post_mix, None, BF16)], name="up_dgrad")
    dw_up = None
    for k, du in enumerate((dua, dub)):
        dw_up = wgrad((h2, d, 0), du, tk=1024, tn=f_half // 2, ts=2048, name=f"up_wgrad_{k}", chip_major=True,
                      slabs=(4, 2 * k), into=dw_up, bf16_copy=True)
    g_ffn = [(dw_up[0], dw_up[1]), (dw_down.reshape(4, f_half // 4, d), dw_down16.reshape(4, f_half // 4, d))]
    dw_out, dw_out16 = wgrad((mixed, d, 0), dy1, tk=1024, tn=1024, ts=2048, name="out_wgrad", bf16_copy=True)
    dpa, dpb, dz_g, dya, dyb, dd_a = mix_bwd(dy1, w_o, z, pa, pb, wo_a, wo_b, ya)
    by_chip_cols = lambda a: jnp.stack([a[:, j * (d // 4):(j + 1) * (d // 4)] for j in range(4)], axis=0)
    dw_of = [by_chip_cols(a) for a in wgrad((ya, ATT_W, 0), dpa, tk=ATT_W, tn=d, ts=1024, name="fox_o_wgrad", bf16_copy=True)]
    dw_od = [by_chip_cols(a) for a in wgrad((yb, ATT_W, 0), dpb, tk=ATT_W, tn=d, ts=1024, name="dil_o_wgrad", bf16_copy=True)]
    g_mix = [dw_of, dw_od, (dw_out.reshape(4, d // 4, d), dw_out16.reshape(4, d // 4, d))]
    dq_aug, dk_aug, dv_a, *got_ffn = fox_bwd(q_aug, k_aug, z, dya, lse_a, dd_a, exchange=[g[1] for g in g_ffn], kind="to_owners")
    dz_a, dfa, gg_bf = fox_post(dq_aug, dk_aug, dv_a, fa, bfo)
    *dz_b, got_of, got_od, got_out = dil_bwd_all(z, cos_t, sin_t, dyb, lse_b, yb, exchange=[g[1] for g in g_mix],
                                                 kind="to_owners")
    got_mix = [got_of, got_od, got_out]
    dwt_a = wgrad((dz_a, e_a, 0), h1, tk=e_a // 2, tn=d, ts=2048, name="in_wgrad_a")
    dwt_b = [wgrad((part, ATT_W, 0), h1, tk=ATT_W, tn=d, ts=2048, name=f"in_wgrad_b{k}") for k, part in enumerate(dz_b)]
    dwt_g = wgrad((dz_g, 2 * d, 0), h1, tk=d, tn=d, ts=2048, name="in_wgrad_g")
    dwt_f = wgrad((dfa, LANES, 0), h1, tk=LANES, tn=d, ts=2048, name="in_wgrad_f")
    dwt_full = jnp.concatenate([dwt_a, dwt_f[:nf], *dwt_b, dwt_g], axis=0)
    dw_in = jnp.stack([dwt_full[j * cols_in:(j + 1) * cols_in] for j in range(4)], axis=0)
    from_sib = grads_to_sibling([dw_in], [True], name="grads_to_sibling_in")
    sum_in = chip_sum(dw_in, from_sib[0], c_arr, True, name="chip_sum_w_in")
    grad_x, gg_pre_mix, got_in = mm_norm_bwd(
        [(dz_a, e_a, 0), *[(part, ATT_W, 0) for part in dz_b], (dz_g, d, 0), (dz_g, d, 1), (dfa, LANES, 0)],
        [(wz, e_a, 0), *[(wz, ATT_W, Z_QB + k) for k in range(3)], (wz, d, 3), (wz, d, 4), (wf, LANES, 0)],
        [(xs, g_pre_mix, dx1, F32)], exchange=[sum_in[1]], name="in_dgrad")

    names = ("w_in", "w_o_fox", "w_o_dil", "w_out", "w_up", "w_down")
    pos_arr = jnp.concatenate([chip_arr, c_arr])
    halves = [final_sum(sum_in[0], got_in, chip_arr, name="final_sum_w_in")] + [
        owner_sum(g[0], got, pos_arr, name=f"owner_sum_{nm}") for g, got, nm in zip(g_mix + g_ffn, got_mix + got_ffn, names[1:])]
    from_half = halves_to_full(halves, [True] + [False] * 5, name="halves_to_full")
    g_big = [None] + [lax.dynamic_update_slice_in_dim(full, mine, ci * mine.shape[0], axis=0)
                      for full, mine in zip(from_half[1:], halves[1:])]
    upd_big = [adamw(w[0], g, m[0], v[0], name=f"adamw_{nm}") for w, g, m, v, nm in list(zip(
        big, g_big, (m_w_in, m_w_o_fox, m_w_o_dil, m_w_out, m_w_up, m_w_down),
        (v_w_in, v_w_o_fox, v_w_o_dil, v_w_out, v_w_up, v_w_down), names))[1:]]
    to_t = lambda a: jnp.transpose(a, (2, 0, 1))
    from_t = lambda a: jnp.transpose(a, (1, 2, 0))
    *upd_in, g_in_t = adamw_rows_view(to_t(w_in), halves[0], from_half[0], to_t(m_w_in), to_t(v_w_in), c_arr,
                                      name="adamw_w_in")

    g_cw_loc = jnp.concatenate([gc_a[0:3], gc_b[0:3]], axis=1)
    g_cb_loc = jnp.concatenate([gc_a[3:4], gc_b[3:4]], axis=1)
    small_loc = [gg_pre_mix, gg_post_mix, gg_pre_ffn, gg_post_ffn, g_cb_loc, gg_bf[:, :nf], g_cw_loc]
    red_rows = (8, 8, 8, 8, 48, 8, 136)
    red = allreduce_small(_pack_rows(small_loc, red_rows), name="allreduce_small")
    g_pm, g_qm, g_pf, g_qf, g_cb, g_bf, g_cw_full = _unpack_rows(red, [a.shape for a in small_loc], red_rows)
    cols_cw = conv_w.shape[2]
    g_cw = lax.dynamic_slice_in_dim(g_cw_full, chip * cols_cw, cols_cw, axis=1)
    small_w = (g_pre_mix, g_post_mix, g_pre_ffn, g_post_ffn, conv_b, b_forget, conv_w[0])
    small_m = (m_g_pre_mix, m_g_post_mix, m_g_pre_ffn, m_g_post_ffn, m_conv_b, m_b_forget, m_conv_w[0])
    small_v = (v_g_pre_mix, v_g_post_mix, v_g_pre_ffn, v_g_post_ffn, v_conv_b, v_b_forget, v_conv_w[0])
    small_g = (g_pm, g_qm, g_pf, g_qf, g_cb, g_bf, g_cw)
    small_names = ("g_pre_mix", "g_post_mix", "g_pre_ffn", "g_post_ffn", "conv_b", "b_forget", "conv_w")
    per_param = [adamw(w, g, m, v, name=f"adamw_{nm}") for w, g, m, v, nm in zip(small_w, small_g, small_m, small_v, small_names)]
    upd_small = [[u[j] for u in per_param] for j in range(3)]

    order = ("g_pre_mix", "w_in", "b_forget", "w_o_fox", "w_o_dil", "w_out", "g_post_mix", "g_pre_ffn", "w_up", "conv_w",
             "conv_b", "w_down", "g_post_ffn")
    grads, deltas, new_ms, new_vs = {}, {}, {}, {}
    grads["w_in"] = from_t(g_in_t)
    deltas["w_in"], new_ms["w_in"], new_vs["w_in"] = (from_t(a) for a in upd_in)
    for k, nm in enumerate(names[1:]):
        grads[nm] = g_big[k + 1][None]
        deltas[nm], new_ms[nm], new_vs[nm] = (a[None] for a in upd_big[k])
    for k, nm in enumerate(small_names):
        lead = (lambda a: a[None]) if nm == "conv_w" else (lambda a: a)
        grads[nm] = lead(small_g[k])
        deltas[nm], new_ms[nm], new_vs[nm] = (lead(upd_small[j][k]) for j in range(3))
    return (loss, grad_x[None], *[grads[nm] for nm in order], *[deltas[nm] for nm in order],
            *[new_ms[nm] for nm in order], *[new_vs[nm] for nm in order])
```

```python
import functools
import math

import numpy as np
import jax
import jax.numpy as jnp
from jax import lax
from jax.experimental import pallas as pl
from jax.experimental.pallas import tpu as pltpu

F32 = jnp.float32
BF16 = jnp.bfloat16
SDS = jax.ShapeDtypeStruct
MESH = pl.DeviceIdType.MESH

HEAD_DIM = 64
N_HEADS = 8
LANES = 128
ATT_W = N_HEADS * HEAD_DIM
DIL_PATTERNS = ((128, 1), (512, 4), (2048, 16))
DIL_BLK = 128
ROPE_DIM = HEAD_DIM // 4
ROPE_THETA = 500000.0
RMS_EPS = 1e-6
NEG = -1e30
QK_SCALE = 1.0 / math.sqrt(HEAD_DIM)
ADAM_LR, ADAM_B1, ADAM_B2, ADAM_EPS, ADAM_WD, ADAM_STEP = 0.001, 0.9, 0.999, 1e-08, 0.01, 10
VMEM_LIMIT = 56 * 1024 * 1024

Z_QA, Z_KA, Z_VA, Z_QB, Z_KB, Z_VB = 0, 1, 2, 3, 4, 5
Z_W = 5120


def _cp(sem):
    return pltpu.CompilerParams(dimension_semantics=sem, vmem_limit_bytes=VMEM_LIMIT)


def _nt(a, b):
    return lax.dot_general(a, b, (((1,), (1,)), ((), ())), preferred_element_type=F32)


def _tn(a, b):
    return lax.dot_general(a, b, (((0,), (0,)), ((), ())), preferred_element_type=F32)


def _nn(a, b):
    return jnp.dot(a, b, preferred_element_type=F32)


def _lane(shape):
    return lax.broadcasted_iota(jnp.int32, shape, 1)


def _row(shape):
    return lax.broadcasted_iota(jnp.int32, shape, 0)


def rmsnorm_fwd(x, g, *, tm=1024):
    s, d = x.shape

    def body(x_ref, g_ref, h_ref):
        xv = x_ref[...]
        inv = lax.rsqrt(jnp.mean(xv * xv, axis=-1, keepdims=True) + RMS_EPS)
        h_ref[...] = (xv * inv * g_ref[...]).astype(h_ref.dtype)

    return pl.pallas_call(
        body, grid=(s // tm,),
        in_specs=[pl.BlockSpec((tm, d), lambda i: (i, 0)), pl.BlockSpec((1, d), lambda i: (0, 0))],
        out_specs=pl.BlockSpec((tm, d), lambda i: (i, 0)),
        out_shape=SDS((s, d), BF16), name="rmsnorm_fwd", compiler_params=_cp(("parallel",)))(x, g)


def mm(a_views, b_views, *, nt, out_dtype, tm, tn, name):
    n_p = len(a_views)
    m = a_views[0][0].shape[0]
    n = b_views[0][0].shape[0] if nt else b_views[0][0].shape[1]

    def body(*refs):
        o_ref = refs[-1]
        acc = None
        for p in range(n_p):
            av = refs[p][...].astype(BF16)
            bv = refs[n_p + p][...].astype(BF16)
            dv = _nt(av, bv) if nt else _nn(av, bv)
            acc = dv if acc is None else acc + dv
        o_ref[...] = acc.astype(o_ref.dtype)

    in_specs = []
    for arr, w, blk in a_views:
        in_specs.append(pl.BlockSpec((tm, w), functools.partial(lambda i, j, blk: (i, blk), blk=blk)))
    for arr, w, blk in b_views:
        if nt:
            in_specs.append(pl.BlockSpec((tn, w), functools.partial(lambda i, j, blk: (j, blk), blk=blk)))
        else:
            in_specs.append(pl.BlockSpec((w, tn), lambda i, j: (0, j)))
    return pl.pallas_call(
        body, grid=(m // tm, n // tn), in_specs=in_specs,
        out_specs=pl.BlockSpec((tm, tn), lambda i, j: (i, j)),
        out_shape=SDS((m, n), out_dtype), name=name,
        compiler_params=_cp(("parallel", "parallel")))(*[a[0] for a in a_views], *[b[0] for b in b_views])


def wgrad(a_view, g, *, tk, tn, ts, name, chip_major=False, slabs=None, into=None, bf16_copy=False):
    arr, ka, blk = a_view
    s, n = g.shape
    ns = s // ts
    total, first = slabs if slabs else (n // tn, 0)
    n_into = 0 if into is None else (2 if bf16_copy else 1)

    def body(a_ref, g_ref, *rest):
        o_ref = rest[n_into]

        @pl.when(pl.program_id(2) == 0)
        def _():
            o_ref[...] = jnp.zeros_like(o_ref)

        o_ref[...] += _tn(a_ref[...].astype(BF16), g_ref[...].astype(BF16))
        if bf16_copy:
            @pl.when(pl.program_id(2) == ns - 1)
            def _():
                rest[n_into + 1][...] = o_ref[...].astype(BF16)

    if chip_major:
        out_spec = pl.BlockSpec((None, tk, tn), lambda i, j, k: (first + j, i, 0))
        shape = (total, ka, tn)
    else:
        out_spec = pl.BlockSpec((tk, tn), lambda i, j, k: (i, j))
        shape = (ka, n)
    in_specs = [pl.BlockSpec((ts, tk), lambda i, j, k: (k, blk * (ka // tk) + i)),
                pl.BlockSpec((ts, tn), lambda i, j, k: (k, j))]
    args = [arr, g]
    if into is not None:
        earlier = list(into) if bf16_copy else [into]
        in_specs += [pl.BlockSpec(memory_space=pl.ANY)] * len(earlier)
        args += earlier
    out = pl.pallas_call(
        body, grid=(ka // tk, n // tn, ns), in_specs=in_specs,
        out_specs=[out_spec, out_spec] if bf16_copy else out_spec,
        out_shape=[SDS(shape, F32), SDS(shape, BF16)] if bf16_copy else SDS(shape, F32), name=name,
        input_output_aliases={2 + k: k for k in range(n_into)},
        compiler_params=_cp(("parallel", "parallel", "arbitrary")))(*args)
    return out


def _norm_bwd_rows(dh, xh, inv, g):
    dxh = dh * g
    dx = inv * (dxh - xh * jnp.mean(dxh * xh, axis=-1, keepdims=True))
    return dx, jnp.sum((dh * xh).reshape(dh.shape[0] // 8, 8, dh.shape[1]), axis=0)


def proj_norm_res(a, w, g, xres, g_next, *, tm=512, name):
    s, k = a.shape
    d = w.shape[1]

    def body(a_ref, w_ref, g_ref, x_ref, gn_ref, y_ref, o_ref, h_ref):
        y = _nn(a_ref[...], w_ref[...])
        inv = lax.rsqrt(jnp.mean(y * y, axis=-1, keepdims=True) + RMS_EPS)
        xn = x_ref[...] + y * inv * g_ref[...]
        y_ref[...] = y
        o_ref[...] = xn
        inv_n = lax.rsqrt(jnp.mean(xn * xn, axis=-1, keepdims=True) + RMS_EPS)
        h_ref[...] = (xn * inv_n * gn_ref[...]).astype(h_ref.dtype)

    row = pl.BlockSpec((tm, d), lambda i: (i, 0))
    vec = pl.BlockSpec((1, d), lambda i: (0, 0))
    return pl.pallas_call(
        body, grid=(s // tm,),
        in_specs=[pl.BlockSpec((tm, k), lambda i: (i, 0)), pl.BlockSpec((k, d), lambda i: (0, 0)), vec, row, vec],
        out_specs=[row, row, row], out_shape=[SDS((s, d), F32), SDS((s, d), F32), SDS((s, d), BF16)], name=name,
        compiler_params=_cp(("parallel",)))(a, w, g, xres, g_next)


def proj_norm_loss(a, w, g, xres, target, *, tm=512, name):
    s, k = a.shape
    d = w.shape[1]
    n = s // tm

    def body(a_ref, w_ref, g_ref, x_ref, t_ref, do_ref, dy_ref, dg_ref, l_ref, acc):
        i = pl.program_id(0)

        @pl.when(i == 0)
        def _():
            acc[...] = jnp.zeros_like(acc)
            l_ref[...] = jnp.zeros_like(l_ref)

        y = _nn(a_ref[...], w_ref[...])
        inv = lax.rsqrt(jnp.mean(y * y, axis=-1, keepdims=True) + RMS_EPS)
        yh = y * inv
        err = x_ref[...] + yh * g_ref[...] - t_ref[...]
        dout = err * (1.0 / d)
        do_ref[...] = dout
        l_ref[...] += jnp.sum(jnp.sum(err * err, axis=1, keepdims=True), axis=0, keepdims=True)
        dy, part = _norm_bwd_rows(dout, yh, inv, g_ref[...])
        dy_ref[...] = dy.astype(dy_ref.dtype)
        acc[...] += part

        @pl.when(i == n - 1)
        def _():
            dg_ref[...] = jnp.sum(acc[...], axis=0, keepdims=True)

    row = pl.BlockSpec((tm, d), lambda i: (i, 0))
    vec = pl.BlockSpec((1, d), lambda i: (0, 0))
    return pl.pallas_call(
        body, grid=(n,),
        in_specs=[pl.BlockSpec((tm, k), lambda i: (i, 0)), pl.BlockSpec((k, d), lambda i: (0, 0)), vec, row, row],
        out_specs=[row, row, vec, pl.BlockSpec((1, 1), lambda i: (0, 0))],
        out_shape=[SDS((s, d), F32), SDS((s, d), BF16), SDS((1, d), F32), SDS((1, 1), F32)],
        scratch_shapes=[pltpu.VMEM((8, d), F32)], name=name, compiler_params=_cp(("arbitrary",)))(a, w, g, xres, target)


def mm_norm_bwd(a_views, b_views, stages, exchange=(), *, tm=256, name):
    n_p, n_s, ne = len(a_views), len(stages), len(exchange)
    s = a_views[0][0].shape[0]
    d = b_views[0][0].shape[0]
    n = s // tm
    has_res = [st[2] is not None for st in stages]

    def body(*refs):
        a_refs, b_refs = refs[:n_p], refs[n_p:2 * n_p]
        at = 2 * n_p
        st_refs = []
        for k in range(n_s):
            cnt = 3 if has_res[k] else 2
            st_refs.append(refs[at:at + cnt])
            at += cnt
        e_ins = refs[at:at + ne]
        at += ne
        dx_refs, dg_refs = refs[at:at + n_s], refs[at + n_s:at + 2 * n_s]
        at += 2 * n_s
        e_outs = refs[at:at + ne]
        at += ne
        accs = refs[at:at + n_s]
        comm = (e_ins, e_outs) + tuple(refs[at + n_s:])
        i = pl.program_id(0)

        @pl.when(i == 0)
        def _():
            for acc in accs:
                acc[...] = jnp.zeros_like(acc)
            if ne:
                _to_chips_start(*comm)

        dh = None
        for p in range(n_p):
            part = _nt(a_refs[p][...].astype(BF16), b_refs[p][...].astype(BF16))
            dh = part if dh is None else dh + part
        for k in range(n_s):
            xv = st_refs[k][0][...]
            inv = lax.rsqrt(jnp.mean(xv * xv, axis=-1, keepdims=True) + RMS_EPS)
            dx, part = _norm_bwd_rows(dh, xv * inv, inv, st_refs[k][1][...])
            if has_res[k]:
                dx = dx + st_refs[k][2][...]
            dx_refs[k][...] = dx.astype(dx_refs[k].dtype)
            accs[k][...] += part
            dh = dx

        @pl.when(i == n - 1)
        def _():
            for k in range(n_s):
                dg_refs[k][...] = jnp.sum(accs[k][...], axis=0, keepdims=True)
            if ne:
                _to_chips_finish(*comm)

    row = pl.BlockSpec((tm, d), lambda i: (i, 0))
    vec = pl.BlockSpec((1, d), lambda i: (0, 0))
    in_specs, args = [], []
    for arr, w, blk in a_views:
        in_specs.append(pl.BlockSpec((tm, w), functools.partial(lambda i, blk: (i, blk), blk=blk)))
        args.append(arr)
    for arr, w, blk in b_views:
        in_specs.append(pl.BlockSpec((d, w), functools.partial(lambda i, blk: (0, blk), blk=blk)))
        args.append(arr)
    for x, g, res, _ in stages:
        in_specs += [row, vec] + ([row] if res is not None else [])
        args += [x, g] + ([res] if res is not None else [])
    return pl.pallas_call(
        body, grid=(n,), in_specs=in_specs + [ANY] * ne,
        out_specs=[row] * n_s + [vec] * n_s + [ANY] * ne,
        out_shape=[SDS((s, d), st[3]) for st in stages] + [SDS((1, d), F32)] * n_s + _to_chips_shapes(exchange),
        scratch_shapes=[pltpu.VMEM((8, d), F32)] * n_s + (_to_chips_sems(ne) if ne else []), name=name,
        compiler_params=_cp(("arbitrary",)))(*args, *exchange)


def _split3(v):
    hi = v.astype(BF16).astype(F32)
    r = v - hi
    mid = r.astype(BF16).astype(F32)
    lo = (r - mid).astype(BF16).astype(F32)
    return hi, mid, lo


def _tri(n, upper):
    r = np.arange(n)
    m = (r[:, None] <= r[None, :]) if upper else (r[:, None] >= r[None, :])
    return jnp.asarray(m.astype(np.float32))


def fox_prep(z, fa, bfo, *, tb=512):
    s = z.shape[0]
    n = s // tb

    def body(q_ref, k_ref, v_ref, fa_ref, b_ref, tri_ref, qa_ref, ka_ref, va_ref, carry):
        @pl.when(pl.program_id(0) == 0)
        def _():
            carry[...] = jnp.zeros_like(carry)

        xv = fa_ref[...] + b_ref[...]
        logf = jnp.minimum(xv, 0.0) - jnp.log(1.0 + jnp.exp(-jnp.abs(xv)))
        csum = jnp.dot(tri_ref[...], logf, preferred_element_type=F32, precision=lax.Precision.HIGHEST) + carry[0:1, :]
        carry[0:1, :] = csum[tb - 1:tb, :]
        lane = _lane((tb, LANES))
        for h in range(N_HEADS):
            hi, mid, lo = _split3(csum[:, h:h + 1])
            pair = (h // 2) * LANES
            qv = q_ref[:, pair:pair + LANES].astype(F32)
            kv = k_ref[:, pair:pair + LANES].astype(F32)
            vv = v_ref[:, pair:pair + LANES].astype(F32)
            if h % 2:
                qv = pltpu.roll(qv, 64, axis=1)
                kv = pltpu.roll(kv, 64, axis=1)
                vv = pltpu.roll(vv, 64, axis=1)
            va_ref[:, h * LANES:(h + 1) * LANES] = jnp.where(lane < 64, vv, jnp.where(lane == 64, 1.0, 0.0)).astype(BF16)
            one = jnp.where((lane >= 67) & (lane < 70), 1.0, 0.0)
            q_x = jnp.where(lane == 64, hi, jnp.where(lane == 65, mid, jnp.where(lane == 66, lo, one)))
            one = jnp.where((lane >= 64) & (lane < 67), 1.0, 0.0)
            k_x = jnp.where(lane == 67, -hi, jnp.where(lane == 68, -mid, jnp.where(lane == 69, -lo, one)))
            qa_ref[:, h * LANES:(h + 1) * LANES] = jnp.where(lane < 64, qv * QK_SCALE, q_x).astype(BF16)
            ka_ref[:, h * LANES:(h + 1) * LANES] = jnp.where(lane < 64, kv, k_x).astype(BF16)

    return pl.pallas_call(
        body, grid=(n,),
        in_specs=[pl.BlockSpec((tb, ATT_W), lambda i: (i, Z_QA)), pl.BlockSpec((tb, ATT_W), lambda i: (i, Z_KA)),
                  pl.BlockSpec((tb, ATT_W), lambda i: (i, Z_VA)),
                  pl.BlockSpec((tb, LANES), lambda i: (i, 0)), pl.BlockSpec((1, LANES), lambda i: (0, 0)),
                  pl.BlockSpec((tb, tb), lambda i: (0, 0))],
        out_specs=[pl.BlockSpec((tb, N_HEADS * LANES), lambda i: (i, 0))] * 3,
        out_shape=[SDS((s, N_HEADS * LANES), BF16)] * 3,
        scratch_shapes=[pltpu.VMEM((8, LANES), F32)],
        name="fox_prep", compiler_params=_cp(("arbitrary",)))(z, z, z, fa, bfo, _tri(tb, False))


def _causal_pairs(n, k_major):
    if k_major:
        pairs = [(qi, kj) for kj in range(n) for qi in range(kj, n)]
    else:
        pairs = [(qi, kj) for qi in range(n) for kj in range(qi + 1)]
    return (jnp.asarray([p[0] for p in pairs], jnp.int32), jnp.asarray([p[1] for p in pairs], jnp.int32), len(pairs))


def fox_fwd(q_aug, k_aug, v_aug, gather=(), halved=(), *, t=1024, hps=4):
    s = v_aug.shape[0]
    qi_arr, kj_arr, n_pairs = _causal_pairs(s // t, False)
    ng = len(gather)
    n_groups = N_HEADS // hps

    def body(qi_ref, kj_ref, q_ref, k_ref, v_ref, *rest):
        g_ins, (o_ref, lse_ref), g_outs = rest[:ng], rest[ng:ng + 2], rest[ng + 2:2 * ng + 2]
        m_scr, acc_scr = rest[2 * ng + 2:2 * ng + 4]
        comm = (g_ins, g_outs) + tuple(rest[2 * ng + 4:]) + (list(halved),)
        step = pl.program_id(1)
        qi = qi_ref[step]
        kj = kj_ref[step]
        if ng:
            @pl.when((pl.program_id(0) == 0) & (step == 0))
            def _():
                _allgather_start(*comm)

        @pl.when(kj == 0)
        def _():
            m_scr[...] = jnp.full_like(m_scr, NEG)
            acc_scr[...] = jnp.zeros_like(acc_scr)

        def update(qs, ks, masked):
            nq, nk = qs.stop - qs.start, ks.stop - ks.start
            for i in range(hps):
                own = slice(i * LANES, (i + 1) * LANES)
                sc = _nt(q_ref[qs, own], k_ref[ks, own])
                if masked:
                    sc = jnp.where(_row((nq, nk)) >= _lane((nq, nk)), sc, NEG)
                m_prev = m_scr[i, qs]
                m_new = jnp.maximum(m_prev, jnp.max(sc, axis=-1, keepdims=True))
                p = jnp.exp((sc - jnp.tile(m_new, (1, nk // LANES))).astype(BF16))
                acc_scr[i, qs] = jnp.exp(m_prev - m_new) * acc_scr[i, qs] + _nn(p, v_ref[ks, own])
                m_scr[i, qs] = m_new

        whole, upper, lower = slice(0, t), slice(0, t // 2), slice(t // 2, t)

        @pl.when(kj < qi)
        def _():
            update(whole, whole, False)

        @pl.when(kj == qi)
        def _():
            update(upper, upper, True)
            update(lower, upper, False)
            update(lower, lower, True)
            lane = _lane((t, LANES))
            for pr in range(hps // 2):
                den = [acc_scr[2 * pr + i][:, 64:65] for i in range(2)]
                o_ref[:, pr * LANES:(pr + 1) * LANES] = jnp.where(
                    lane < 64, acc_scr[2 * pr] / den[0], pltpu.roll(acc_scr[2 * pr + 1] / den[1], 64, axis=1)).astype(o_ref.dtype)
                lse_ref[:, pr * LANES:(pr + 1) * LANES] = jnp.where(
                    lane < 64, m_scr[2 * pr] + jnp.log(den[0]), m_scr[2 * pr + 1] + jnp.log(den[1]))

        if ng:
            @pl.when((pl.program_id(0) == n_groups - 1) & (step == n_pairs - 1))
            def _():
                _allgather_finish(*comm)

    wide = hps * LANES
    grid_spec = pltpu.PrefetchScalarGridSpec(
        num_scalar_prefetch=2, grid=(n_groups, n_pairs),
        in_specs=[pl.BlockSpec((t, wide), lambda hg, st, qi, kj: (qi[st], hg)),
                  pl.BlockSpec((t, wide), lambda hg, st, qi, kj: (kj[st], hg)),
                  pl.BlockSpec((t, wide), lambda hg, st, qi, kj: (kj[st], hg))] + [ANY] * ng,
        out_specs=[pl.BlockSpec((t, wide // 2), lambda hg, st, qi, kj: (qi[st], hg))] * 2 + [ANY] * ng,
        scratch_shapes=[pltpu.VMEM((hps, t, LANES), F32)] * 2 + (_allgather_sems(ng) if ng else []))
    return pl.pallas_call(
        body, grid_spec=grid_spec, out_shape=[SDS((s, ATT_W), BF16), SDS((s, ATT_W), F32)] + _allgather_shapes(gather),
        name="fox_fwd", compiler_params=_cp(("arbitrary", "arbitrary")))(qi_arr, kj_arr, q_aug, k_aug, v_aug, *gather)


def fox_bwd(q_aug, k_aug, z, dy, lse, dd, exchange=(), kind="to_chips", *, t=1024, hps=4):
    s = z.shape[0]
    qi_arr, kj_arr, n_pairs = _causal_pairs(s // t, True)
    ne = len(exchange)
    n_groups = N_HEADS // hps
    x_shapes, x_sems, x_start, x_finish = EXCHANGES[kind]

    def body(qi_ref, kj_ref, q_ref, k_ref, v_ref, do_ref, lse_ref, dd_ref, *rest):
        e_ins, (dq_ref, dk_ref, dv_ref), e_outs = rest[:ne], rest[ne:ne + 3], rest[ne + 3:2 * ne + 3]
        comm = (e_ins, e_outs) + tuple(rest[2 * ne + 3:])
        step = pl.program_id(1)
        qi = qi_ref[step]
        kj = kj_ref[step]
        if ne:
            @pl.when((pl.program_id(0) == 0) & (step == 0))
            def _():
                x_start(*comm)

        @pl.when(step == 0)
        def _():
            dq_ref[...] = jnp.zeros_like(dq_ref)

        @pl.when(qi == kj)
        def _():
            dk_ref[...] = jnp.zeros_like(dk_ref)
            dv_ref[...] = jnp.zeros_like(dv_ref)

        def update(qs, ks, masked):
            nq, nk = qs.stop - qs.start, ks.stop - ks.start
            lane = _lane((nq, LANES))
            rows = pl.ds(pl.multiple_of(qi * t + qs.start, nq), nq)
            for pr in range(hps // 2):
                pair = slice(pr * LANES, (pr + 1) * LANES)
                dov = do_ref[qs, pair]
                dv_new = None
                for i in range(2):
                    head = (lane < 64) if i == 0 else (lane >= 64)
                    own = slice((2 * pr + i) * LANES, (2 * pr + i + 1) * LANES)
                    col = slice(pr * LANES + i * 64, pr * LANES + i * 64 + 1)
                    qv = q_ref[qs, own]
                    kv = k_ref[ks, own]
                    sc = _nt(qv, kv)
                    if masked:
                        sc = jnp.where(_row((nq, nk)) >= _lane((nq, nk)), sc, NEG)
                    p = jnp.exp(sc - lse_ref[qs, col])
                    dp = _nt(jnp.where(head, dov, jnp.zeros_like(dov)), v_ref[ks, pair])
                    ds = (p * (dp - dd_ref[qs, col])).astype(BF16)
                    dq_ref[rows, own] += _nn(ds, kv)
                    dk_ref[ks, own] += _tn(ds, qv)
                    dvi = _tn(p.astype(BF16), dov)
                    dv_new = dvi if dv_new is None else jnp.where(head, dvi, dv_new)
                dv_ref[ks, pair] += dv_new

        whole, upper, lower = slice(0, t), slice(0, t // 2), slice(t // 2, t)

        @pl.when(kj < qi)
        def _():
            update(whole, whole, False)

        @pl.when(kj == qi)
        def _():
            update(upper, upper, True)
            update(lower, upper, False)
            update(lower, lower, True)

        if ne:
            @pl.when((pl.program_id(0) == n_groups - 1) & (step == n_pairs - 1))
            def _():
                x_finish(*comm)

    wide, half = hps * LANES, hps // 2 * LANES
    v_blk = Z_VA * ATT_W // half
    grid_spec = pltpu.PrefetchScalarGridSpec(
        num_scalar_prefetch=2, grid=(n_groups, n_pairs),
        in_specs=[pl.BlockSpec((t, wide), lambda hg, st, qi, kj: (qi[st], hg)),
                  pl.BlockSpec((t, wide), lambda hg, st, qi, kj: (kj[st], hg)),
                  pl.BlockSpec((t, half), lambda hg, st, qi, kj: (kj[st], v_blk + hg)),
                  pl.BlockSpec((t, half), lambda hg, st, qi, kj: (qi[st], hg)),
                  pl.BlockSpec((t, half), lambda hg, st, qi, kj: (qi[st], hg)),
                  pl.BlockSpec((t, half), lambda hg, st, qi, kj: (qi[st], hg))] + [ANY] * ne,
        out_specs=[pl.BlockSpec((s, wide), lambda hg, st, qi, kj: (0, hg)),
                   pl.BlockSpec((t, wide), lambda hg, st, qi, kj: (kj[st], hg)),
                   pl.BlockSpec((t, half), lambda hg, st, qi, kj: (kj[st], hg))] + [ANY] * ne,
        scratch_shapes=x_sems(ne) if ne else [])
    return pl.pallas_call(
        body, grid_spec=grid_spec,
        out_shape=[SDS((s, N_HEADS * LANES), F32), SDS((s, N_HEADS * LANES), F32), SDS((s, ATT_W), F32)]
        + x_shapes(exchange),
        name="fox_bwd", compiler_params=_cp(("arbitrary", "arbitrary")))(qi_arr, kj_arr, q_aug, k_aug, z, dy, lse, dd, *exchange)


def fox_post(dq_aug, dk_aug, dv, fa, bfo, *, tb=512):
    s = dv.shape[0]
    n = s // tb

    def body(dq_ref, dk_ref, dv_ref, fa_ref, b_ref, tri_ref, dz_ref, dfa_ref, gb_ref, carry, acc):
        i = pl.program_id(0)

        @pl.when(i == 0)
        def _():
            carry[...] = jnp.zeros_like(carry)
            acc[...] = jnp.zeros_like(acc)

        lane = _lane((tb, LANES))
        d_f = jnp.zeros((tb, LANES), F32)
        for h in range(N_HEADS):
            col = dq_ref[:, h * LANES + 64:h * LANES + 65] - dk_ref[:, h * LANES + 67:h * LANES + 68]
            d_f = jnp.where(lane == h, col, d_f)
        suffix = jnp.dot(tri_ref[...], d_f, preferred_element_type=F32, precision=lax.Precision.HIGHEST) + carry[0:1, :]
        carry[0:1, :] = suffix[0:1, :]
        xv = fa_ref[...] + b_ref[...]
        dx = suffix * (1.0 / (1.0 + jnp.exp(xv)))
        dfa_ref[...] = dx.astype(dfa_ref.dtype)
        acc[...] += jnp.sum(dx.reshape(tb // 8, 8, LANES), axis=0)
        for hp in range(4):
            for src, off, scale in ((dq_ref, 0, QK_SCALE), (dk_ref, ATT_W, 1.0)):
                even = src[:, (2 * hp) * LANES:(2 * hp + 1) * LANES]
                odd = pltpu.roll(src[:, (2 * hp + 1) * LANES:(2 * hp + 2) * LANES], 64, axis=1)
                dz_ref[:, off + hp * LANES:off + (hp + 1) * LANES] = (jnp.where(lane < 64, even, odd) * scale).astype(BF16)
        dz_ref[:, 2 * ATT_W:3 * ATT_W] = dv_ref[...].astype(BF16)

        @pl.when(i == n - 1)
        def _():
            gb_ref[...] = jnp.sum(acc[...], axis=0, keepdims=True)

    rev = lambda i: (n - 1 - i, 0)
    return pl.pallas_call(
        body, grid=(n,),
        in_specs=[pl.BlockSpec((tb, N_HEADS * LANES), rev), pl.BlockSpec((tb, N_HEADS * LANES), rev),
                  pl.BlockSpec((tb, ATT_W), rev), pl.BlockSpec((tb, LANES), rev),
                  pl.BlockSpec((1, LANES), lambda i: (0, 0)), pl.BlockSpec((tb, tb), lambda i: (0, 0))],
        out_specs=[pl.BlockSpec((tb, 3 * ATT_W), rev), pl.BlockSpec((tb, LANES), rev),
                   pl.BlockSpec((1, LANES), lambda i: (0, 0))],
        out_shape=[SDS((s, 3 * ATT_W), BF16), SDS((s, LANES), BF16), SDS((1, LANES), F32)],
        scratch_shapes=[pltpu.VMEM((8, LANES), F32), pltpu.VMEM((8, LANES), F32)],
        name="fox_post", compiler_params=_cp(("arbitrary",)))(dq_aug, dk_aug, dv, fa, bfo, _tri(tb, True))


def rope_cos_sin(s):
    half = ROPE_DIM // 2
    inv_freq = ROPE_THETA ** (-jnp.arange(half, dtype=F32) * 2.0 / ROPE_DIM)
    ang = jnp.arange(s, dtype=F32)[:, None] * inv_freq[None, :]
    return jnp.tile(jnp.cos(ang), (1, LANES // half)), jnp.tile(jnp.sin(ang), (1, LANES // half))


def _rotate(x, cos, sin, sign):
    l64 = _lane(x.shape) & (HEAD_DIM - 1)
    first = l64 < ROPE_DIM // 2
    second = (l64 >= ROPE_DIM // 2) & (l64 < ROPE_DIM)
    from_next = jnp.where(first, -sign * sin, 0.0)
    from_prev = jnp.where(second, sign * sin, 0.0)
    return (x * jnp.where(first | second, cos, 1.0) + pltpu.roll(x, LANES - 8, axis=1) * from_next
            + pltpu.roll(x, 8, axis=1) * from_prev)


def _dil_rows(base, r):
    if r == 1:
        return pl.ds(pl.multiple_of(base, DIL_BLK), DIL_BLK)
    return pl.ds(base, DIL_BLK, stride=r)


def _dil_block(idx, r, nb):
    shift = nb.bit_length() - 1
    rho = idx >> shift
    n = idx & (nb - 1)
    base = rho + n * (r * DIL_BLK)
    return _dil_rows(base, r), _dil_rows(jnp.maximum(base - r * DIL_BLK, rho), r), n > 0


def _cat(a, b):
    return jnp.concatenate([a, b], axis=0)


def _two_heads(v, first_head):
    zero = jnp.zeros_like(v)
    return _cat(jnp.where(first_head, v, zero), jnp.where(first_head, zero, v))


def _dil_bands():
    b = DIL_BLK
    q = _row((2 * b, 2 * b)) & (b - 1)
    col = _lane((2 * b, 2 * b))
    return (col < b) & (col >= q), (col >= b) & (col - b <= q)


def _dil_load_qkv(zq_ref, zk_ref, zv_ref, cos_ref, sin_ref, q_ref, k_ref, v_ref, *, chunk=512):
    def step(i, carry):
        rows = pl.ds(pl.multiple_of(i * chunk, chunk), chunk)
        cos, sin = cos_ref[rows, :], sin_ref[rows, :]
        q_ref[rows, :] = _rotate(zq_ref[rows, :].astype(F32), cos, sin, 1.0) * QK_SCALE
        k_ref[rows, :] = _rotate(zk_ref[rows, :].astype(F32), cos, sin, 1.0)
        v_ref[rows, :] = zv_ref[rows, :].astype(F32)
        return carry

    lax.fori_loop(0, q_ref.shape[0] // chunk, step, 0)


def dil_fwd_all(z, cos_t, sin_t, *, unroll=32):
    s = z.shape[0]
    b = DIL_BLK
    n_blk = s // b

    def body(zq_ref, zk_ref, zv_ref, cos_ref, sin_ref, o_ref, l_ref, q_ref, k_ref, v_ref):
        _dil_load_qkv(zq_ref, zk_ref, zv_ref, cos_ref, sin_ref, q_ref, k_ref, v_ref)
        first_head = _lane((b, LANES)) < 64
        band_prev, band_cur = _dil_bands()
        for g, (_, r) in enumerate(DIL_PATTERNS):
            nb = n_blk // r

            def group(it, carry, g=g, r=r, nb=nb):
                loaded = []
                kc = vc = None
                for u in range(unroll):
                    rows_c, rows_p, has_prev = _dil_block(it * unroll + u, r, nb)
                    if u % min(nb, unroll):
                        kp, vp = kc, vc
                    else:
                        kp, vp = k_ref[rows_p, :].astype(BF16), v_ref[rows_p, :].astype(BF16)
                    kc, vc = k_ref[rows_c, :].astype(BF16), v_ref[rows_c, :].astype(BF16)
                    state = (o_ref[rows_c, :], l_ref[rows_c, :]) if g else None
                    loaded.append((rows_c, has_prev, [q_ref[rows_c, :].astype(BF16), kp, kc, vp, vc], state))
                done = []
                for rows_c, has_prev, (qv, kp, kc, vp, vc), state in loaded:
                    sc = jnp.where(band_cur | (band_prev & has_prev), _nt(_two_heads(qv, first_head), _cat(kp, kc)), NEG)
                    m = jnp.max(sc, axis=-1, keepdims=True)
                    p = jnp.exp(sc - m)
                    den = jnp.sum(p, axis=-1, keepdims=True)
                    both = _nn(p.astype(BF16), _cat(vp, vc)) / den
                    lse2 = m + jnp.log(den)
                    ov = jnp.where(first_head, both[:b], both[b:])
                    lse = jnp.where(first_head, lse2[:b], lse2[b:])
                    if state is not None:
                        m2 = jnp.maximum(state[1], lse)
                        wp = jnp.exp(state[1] - m2)
                        wn = jnp.exp(lse - m2)
                        ov = (wp * state[0] + wn * ov) / (wp + wn)
                        lse = m2 + jnp.log(wp + wn)
                    done.append((rows_c, ov, lse))
                for rows_c, ov, lse in done:
                    o_ref[rows_c, :] = ov
                    l_ref[rows_c, :] = lse
                return carry

            lax.fori_loop(0, n_blk // unroll, group, 0)

    col_blk = lambda k: pl.BlockSpec((s, LANES), lambda hp: (0, 4 * k + hp))
    table = pl.BlockSpec((s, LANES), lambda hp: (0, 0))
    out = pl.BlockSpec((s, LANES), lambda hp: (0, hp))
    return pl.pallas_call(
        body, grid=(4,), in_specs=[col_blk(Z_QB), col_blk(Z_KB), col_blk(Z_VB), table, table], out_specs=[out, out],
        out_shape=[SDS((s, ATT_W), F32)] * 2, scratch_shapes=[pltpu.VMEM((s, LANES), F32)] * 3, name="dil_fwd",
        compiler_params=_cp(("parallel",)))(z, z, z, cos_t, sin_t)


def dil_bwd_all(z, cos_t, sin_t, dy, lse, y, exchange=(), kind="to_chips", *, unroll=16):
    s = z.shape[0]
    b = DIL_BLK
    n_blk = s // b
    ne = len(exchange)
    x_shapes, x_sems, x_start, x_finish = EXCHANGES[kind]

    def body(zq_ref, zk_ref, zv_ref, cos_ref, sin_ref, do_ref, l_ref, y_ref, *rest):
        e_ins, (gq_ref, gk_ref, gv_ref), e_outs = rest[:ne], rest[ne:ne + 3], rest[ne + 3:2 * ne + 3]
        q_ref, k_ref, v_ref, dq_ref, dk_ref, dv_ref = rest[2 * ne + 3:2 * ne + 9]
        comm = (e_ins, e_outs) + tuple(rest[2 * ne + 9:])
        if ne:
            @pl.when(pl.program_id(0) == 0)
            def _():
                x_start(*comm)

        _dil_load_qkv(zq_ref, zk_ref, zv_ref, cos_ref, sin_ref, q_ref, k_ref, v_ref)
        dq_ref[...] = jnp.zeros_like(dq_ref)
        dk_ref[...] = jnp.zeros_like(dk_ref)
        dv_ref[...] = jnp.zeros_like(dv_ref)
        first_head = _lane((b, LANES)) < 64
        band_prev, band_cur = _dil_bands()
        for _, r in DIL_PATTERNS:
            nb = n_blk // r

            def group(it, carry, r=r, nb=nb):
                loaded = []
                kc = vc = None
                for u in range(unroll):
                    rows_c, rows_p, has_prev = _dil_block(it * unroll + u, r, nb)
                    if u % min(nb, unroll):
                        kp, vp = kc, vc
                    else:
                        kp, vp = k_ref[rows_p, :].astype(BF16), v_ref[rows_p, :].astype(BF16)
                    kc, vc = k_ref[rows_c, :].astype(BF16), v_ref[rows_c, :].astype(BF16)
                    vals = [q_ref[rows_c, :].astype(BF16), kp, kc, vp, vc, do_ref[rows_c, :], l_ref[rows_c, :], y_ref[rows_c, :]]
                    loaded.append((rows_c, rows_p, has_prev, vals))
                done = []
                for rows_c, rows_p, has_prev, (qv, kp, kc, vp, vc, dof, lv, yv) in loaded:
                    q2 = _two_heads(qv, first_head)
                    do2 = _two_heads(dof.astype(BF16), first_head)
                    kcat, vcat = _cat(kp, kc), _cat(vp, vc)
                    lse2 = _cat(lv[:, 0:1], lv[:, 64:65])
                    dd2 = jnp.sum(_two_heads(dof * yv, first_head), axis=-1, keepdims=True)
                    p = jnp.exp(jnp.where(band_cur | (band_prev & has_prev), _nt(q2, kcat), NEG) - lse2)
                    ds = (p * (_nt(do2, vcat) - dd2)).astype(BF16)
                    dq2 = _nn(ds, kcat)
                    dkcat = _tn(ds, q2)
                    dvcat = _tn(p.astype(BF16), do2)
                    done.append((rows_c, rows_p, (jnp.where(first_head, dq2[:b], dq2[b:]), dkcat[:b], dkcat[b:],
                                                  dvcat[:b], dvcat[b:])))
                for rows_c, rows_p, (dq, dk_p, dk_c, dv_p, dv_c) in done:
                    dq_ref[rows_c, :] += dq
                    dk_ref[rows_p, :] += dk_p
                    dk_ref[rows_c, :] += dk_c
                    dv_ref[rows_p, :] += dv_p
                    dv_ref[rows_c, :] += dv_c
                return carry

            lax.fori_loop(0, n_blk // unroll, group, 0)

        def finish(i, carry, chunk=512):
            rows = pl.ds(pl.multiple_of(i * chunk, chunk), chunk)
            cos, sin = cos_ref[rows, :], sin_ref[rows, :]
            gq_ref[rows, :] = (_rotate(dq_ref[rows, :], cos, sin, -1.0) * QK_SCALE).astype(BF16)
            gk_ref[rows, :] = _rotate(dk_ref[rows, :], cos, sin, -1.0).astype(BF16)
            gv_ref[rows, :] = dv_ref[rows, :].astype(BF16)
            return carry

        lax.fori_loop(0, s // 512, finish, 0)
        if ne:
            @pl.when(pl.program_id(0) == 3)
            def _():
                x_finish(*comm)

    col_blk = lambda k: pl.BlockSpec((s, LANES), lambda hp: (0, 4 * k + hp))
    table = pl.BlockSpec((s, LANES), lambda hp: (0, 0))
    nat = pl.BlockSpec((s, LANES), lambda hp: (0, hp))
    return pl.pallas_call(
        body, grid=(4,), in_specs=[col_blk(Z_QB), col_blk(Z_KB), col_blk(Z_VB), table, table, nat, nat, nat] + [ANY] * ne,
        out_specs=[nat, nat, nat] + [ANY] * ne, out_shape=[SDS((s, ATT_W), BF16)] * 3 + x_shapes(exchange),
        scratch_shapes=[pltpu.VMEM((s, LANES), F32)] * 6 + (x_sems(ne) if ne else []), name="dil_bwd",
        compiler_params=_cp(("arbitrary",)))(z, z, z, cos_t, sin_t, dy, lse, y, *exchange)


def _sigmoid(v):
    return 1.0 / (1.0 + jnp.exp(-v))


def gate_mix(ya, yb, wa, wb, z, *, tm=2048, tn=512):
    s = ya.shape[0]
    d = wa.shape[1]
    ga_blk = 3 * ATT_W * 2 // tn
    gb_blk = ga_blk + d // tn

    def body(ya_ref, yb_ref, wa_ref, wb_ref, ga_ref, gb_ref, pa_ref, pb_ref, mx_ref):
        pa = _nn(ya_ref[...], wa_ref[...])
        pb = _nn(yb_ref[...].astype(BF16), wb_ref[...])
        pa_ref[...] = pa.astype(BF16)
        pb_ref[...] = pb.astype(BF16)
        mx_ref[...] = (_sigmoid(ga_ref[...].astype(F32)) * pa + _sigmoid(gb_ref[...].astype(F32)) * pb).astype(BF16)

    out = pl.BlockSpec((tm, tn), lambda i, j: (i, j))
    return pl.pallas_call(
        body, grid=(s // tm, d // tn),
        in_specs=[pl.BlockSpec((tm, ATT_W), lambda i, j: (i, 0)), pl.BlockSpec((tm, ATT_W), lambda i, j: (i, 0)),
                  pl.BlockSpec((ATT_W, tn), lambda i, j: (0, j)), pl.BlockSpec((ATT_W, tn), lambda i, j: (0, j)),
                  pl.BlockSpec((tm, tn), lambda i, j: (i, ga_blk + j)), pl.BlockSpec((tm, tn), lambda i, j: (i, gb_blk + j))],
        out_specs=[out, out, out], out_shape=[SDS((s, d), BF16)] * 3, name="gate_mix",
        compiler_params=_cp(("parallel", "parallel")))(ya, yb, wa, wb, z, z)


def mix_bwd(dy, w_o, z, pa, pb, wo_a, wo_b, ya, *, tm=512):
    s, d = dy.shape

    def body(dy_ref, wo_ref, ga_ref, gb_ref, pa_ref, pb_ref, wa_ref, wb_ref, ya_ref,
             dpa_ref, dpb_ref, dg_ref, dya_ref, dyb_ref, dd_ref):
        dm = _nt(dy_ref[...], wo_ref[...])
        sa = _sigmoid(ga_ref[...].astype(F32))
        sb = _sigmoid(gb_ref[...].astype(F32))
        dpa = (dm * sa).astype(BF16)
        dpb = (dm * sb).astype(BF16)
        dpa_ref[...] = dpa
        dpb_ref[...] = dpb
        dg_ref[:, 0:d] = (dm * pa_ref[...].astype(F32) * sa * (1.0 - sa)).astype(BF16)
        dg_ref[:, d:2 * d] = (dm * pb_ref[...].astype(F32) * sb * (1.0 - sb)).astype(BF16)
        dya = _nt(dpa, wa_ref[...]).astype(BF16)
        dya_ref[...] = dya
        dyb_ref[...] = _nt(dpb, wb_ref[...])
        lane = _lane((tm, LANES))
        for pr in range(ATT_W // LANES):
            pair = slice(pr * LANES, (pr + 1) * LANES)
            prod = dya[:, pair].astype(F32) * ya_ref[:, pair].astype(F32)
            lo = jnp.sum(jnp.where(lane < 64, prod, 0.0), axis=-1, keepdims=True)
            hi = jnp.sum(jnp.where(lane >= 64, prod, 0.0), axis=-1, keepdims=True)
            dd_ref[:, pair] = jnp.where(lane < 64, lo, hi)

    row = pl.BlockSpec((tm, d), lambda i: (i, 0))
    att = pl.BlockSpec((tm, ATT_W), lambda i: (i, 0))
    whole = lambda a: pl.BlockSpec(a.shape, lambda i: (0, 0))
    return pl.pallas_call(
        body, grid=(s // tm,),
        in_specs=[row, whole(w_o), pl.BlockSpec((tm, d), lambda i: (i, 3)), pl.BlockSpec((tm, d), lambda i: (i, 4)), row, row,
                  whole(wo_a), whole(wo_b), att],
        out_specs=[row, row, pl.BlockSpec((tm, 2 * d), lambda i: (i, 0)), att, att, att],
        out_shape=[SDS((s, d), BF16), SDS((s, d), BF16), SDS((s, 2 * d), BF16), SDS((s, ATT_W), BF16),
                   SDS((s, ATT_W), F32), SDS((s, ATT_W), F32)], name="mix_bwd",
        compiler_params=_cp(("parallel",)))(dy, w_o, z, z, pa, pb, wo_a, wo_b, ya)


GELU_C = math.sqrt(2.0 / math.pi)


def _gelu_parts(a):
    a2 = a * a
    th = jnp.tanh(a * (GELU_C + (GELU_C * 0.044715) * a2))
    half = 0.5 * a
    gelu = half + half * th
    dgelu = (0.5 + 0.5 * th) + half * (1.0 - th * th) * (GELU_C + (3.0 * GELU_C * 0.044715) * a2)
    return gelu, dgelu


def _causal_taps(u, before):
    row = _row(u.shape)
    r1 = jnp.where(row == 0, before[7:8, :], pltpu.roll(u, 1, axis=0))
    r2 = jnp.where(row == 0, before[6:7, :], jnp.where(row == 1, before[7:8, :], pltpu.roll(u, 2, axis=0)))
    return r1, r2


def ffn_up(h, wa, wb, cw, cb, *, tm=2048, tn=256):
    s, d = h.shape
    f = wa.shape[1]
    nj = f // tn

    def body(h_ref, wa_ref, wb_ref, cwa_ref, cwb_ref, cba_ref, cbb_ref, ua_ref, ub_ref, ca_ref, cbo_ref, m_ref, carry):
        @pl.when(pl.program_id(1) == 0)
        def _():
            carry[...] = jnp.zeros_like(carry)

        conv = []
        for k, (w_ref, cw_ref, cb_ref, u_ref, c_ref) in enumerate(((wa_ref, cwa_ref, cba_ref, ua_ref, ca_ref),
                                                                   (wb_ref, cwb_ref, cbb_ref, ub_ref, cbo_ref))):
            u16 = _nn(h_ref[...], w_ref[...]).astype(BF16)
            u_ref[...] = u16
            u = u16.astype(F32)
            r1, r2 = _causal_taps(u, carry[k])
            carry[k] = u[tm - 8:tm, :]
            c16 = (cw_ref[0:1, :] * r2 + cw_ref[1:2, :] * r1 + cw_ref[2:3, :] * u + cb_ref[...]).astype(BF16)
            c_ref[...] = c16
            conv.append(c16.astype(F32))
        m_ref[...] = (_gelu_parts(conv[0])[0] * conv[1]).astype(BF16)

    out = pl.BlockSpec((tm, tn), lambda j, i: (i, j))
    return pl.pallas_call(
        body, grid=(nj, s // tm),
        in_specs=[pl.BlockSpec((tm, d), lambda j, i: (i, 0)),
                  pl.BlockSpec((d, tn), lambda j, i: (0, j)), pl.BlockSpec((d, tn), lambda j, i: (0, j)),
                  pl.BlockSpec((3, tn), lambda j, i: (0, j)), pl.BlockSpec((3, tn), lambda j, i: (0, nj + j)),
                  pl.BlockSpec((1, tn), lambda j, i: (0, j)), pl.BlockSpec((1, tn), lambda j, i: (0, nj + j))],
        out_specs=[out] * 5, out_shape=[SDS((s, f), BF16)] * 5,
        scratch_shapes=[pltpu.VMEM((2, 8, tn), F32)], name="ffn_up",
        compiler_params=_cp(("parallel", "arbitrary")))(h, wa, wb, cw, cw, cb, cb)


def ffn_bwd(dm, ua, ub, ca, cbo, cw, *, tm=2048, tn=256):
    s, f = dm.shape
    nj = f // tn
    ni = s // tm

    def body(dm_ref, ua_ref, ub_ref, ca_ref, cbo_ref, cwa_ref, cwb_ref, dua_ref, dub_ref, ga_ref, gb_ref, carry):
        @pl.when(pl.program_id(1) == 0)
        def _():
            carry[...] = jnp.zeros_like(carry)
            ga_ref[...] = jnp.zeros_like(ga_ref)
            gb_ref[...] = jnp.zeros_like(gb_ref)

        row = _row((tm, tn))
        dmv = dm_ref[...].astype(F32)
        gelu, dgelu = _gelu_parts(ca_ref[...].astype(F32))
        dcs = (dmv * cbo_ref[...].astype(F32) * dgelu, dmv * gelu)
        for k, (dc, u_ref, cw_ref, du_ref, g_ref) in enumerate(((dcs[0], ua_ref, cwa_ref, dua_ref, ga_ref),
                                                                (dcs[1], ub_ref, cwb_ref, dub_ref, gb_ref))):
            u = u_ref[...].astype(F32)
            after = carry[k]
            n1 = jnp.where(row == tm - 1, after[0:1, :], pltpu.roll(dc, tm - 1, axis=0))
            n2 = jnp.where(row == tm - 2, after[0:1, :], jnp.where(row == tm - 1, after[1:2, :], pltpu.roll(dc, tm - 2, axis=0)))
            g_ref[0:1, :] += jnp.sum(n2 * u, axis=0, keepdims=True)
            g_ref[1:2, :] += jnp.sum(n1 * u, axis=0, keepdims=True)
            g_ref[2:3, :] += jnp.sum(dc * u, axis=0, keepdims=True)
            g_ref[3:4, :] += jnp.sum(dc, axis=0, keepdims=True)
            du_ref[...] = (cw_ref[2:3, :] * dc + cw_ref[1:2, :] * n1 + cw_ref[0:1, :] * n2).astype(BF16)
            carry[k] = dc[0:8, :]

    tile = pl.BlockSpec((tm, tn), lambda j, i: (ni - 1 - i, j))
    gspec = pl.BlockSpec((8, tn), lambda j, i: (0, j))
    return pl.pallas_call(
        body, grid=(nj, ni),
        in_specs=[tile] * 5 + [pl.BlockSpec((3, tn), lambda j, i: (0, j)), pl.BlockSpec((3, tn), lambda j, i: (0, nj + j))],
        out_specs=[tile, tile, gspec, gspec],
        out_shape=[SDS((s, f), BF16), SDS((s, f), BF16), SDS((8, f), F32), SDS((8, f), F32)],
        scratch_shapes=[pltpu.VMEM((2, 8, tn), F32)], name="ffn_bwd",
        compiler_params=_cp(("parallel", "arbitrary")))(dm, ua, ub, ca, cbo, cw, cw)


def adamw(w, g, m, v, *, name, tr=None):
    r = w.shape[0]
    rest = w.shape[1:]
    if tr is None:
        tr = r
        for cand in (256, 128, 64, 32, 16, 8):
            if r % cand == 0:
                tr = cand
                break

    def body(w_ref, g_ref, m_ref, v_ref, d_ref, nm_ref, nv_ref):
        gv = g_ref[...]
        mn = ADAM_B1 * m_ref[...] + (1.0 - ADAM_B1) * gv
        vn = ADAM_B2 * v_ref[...] + (1.0 - ADAM_B2) * (gv * gv)
        m_hat = mn / (1.0 - ADAM_B1 ** ADAM_STEP)
        v_hat = vn / (1.0 - ADAM_B2 ** ADAM_STEP)
        d_ref[...] = -ADAM_LR * (m_hat / (jnp.sqrt(v_hat) + ADAM_EPS) + ADAM_WD * w_ref[...])
        nm_ref[...] = mn
        nv_ref[...] = vn

    blk = pl.BlockSpec((tr,) + rest, lambda i: (i,) + (0,) * len(rest))
    return pl.pallas_call(body, grid=(r // tr,), in_specs=[blk] * 4, out_specs=[blk] * 3, out_shape=[SDS(w.shape, F32)] * 3,
                          name=name, compiler_params=_cp(("parallel",)))(w, g, m, v)


def adamw_rows_view(w, g_mine, g_full, m, v, c_arr, *, name, tc=256):
    r, _, c = w.shape
    per_half = c // 2 // tc

    def body(c_ref, w_ref, gm_ref, gf_ref, m_ref, v_ref, d_ref, nm_ref, nv_ref, go_ref):
        mine = (pl.program_id(0) >> (per_half.bit_length() - 1)) == c_ref[0]
        gv = jnp.where(mine, gm_ref[...], gf_ref[...])
        mn = ADAM_B1 * m_ref[:, 0, :] + (1.0 - ADAM_B1) * gv
        vn = ADAM_B2 * v_ref[:, 0, :] + (1.0 - ADAM_B2) * (gv * gv)
        m_hat = mn / (1.0 - ADAM_B1 ** ADAM_STEP)
        v_hat = vn / (1.0 - ADAM_B2 ** ADAM_STEP)
        d_ref[:, 0, :] = -ADAM_LR * (m_hat / (jnp.sqrt(v_hat) + ADAM_EPS) + ADAM_WD * w_ref[:, 0, :])
        nm_ref[:, 0, :] = mn
        nv_ref[:, 0, :] = vn
        go_ref[:, 0, :] = gv

    b3 = pl.BlockSpec((r, 1, tc), lambda i, c_ref: (0, 0, i))
    own = pl.BlockSpec((r, tc), lambda i, c_ref: (0, jnp.clip(i - c_ref[0] * per_half, 0, per_half - 1)))
    full = pl.BlockSpec((r, tc), lambda i, c_ref: (0, i))
    grid_spec = pltpu.PrefetchScalarGridSpec(num_scalar_prefetch=1, grid=(c // tc,), in_specs=[b3, own, full, b3, b3],
                                             out_specs=[b3] * 4)
    return pl.pallas_call(body, grid_spec=grid_spec, out_shape=[SDS(w.shape, F32)] * 4, name=name,
                          compiler_params=_cp(("parallel",)))(c_arr, w, g_mine, g_full, m, v)


ANY = pl.BlockSpec(memory_space=pl.ANY)
ICI_KINDS = ("x", "y", "xy")


def _coords():
    return lax.axis_index("x"), lax.axis_index("y"), lax.axis_index("c")


def _peer(kind, x, y, c):
    if kind == "c":
        return (x, y, 1 - c)
    if kind == "x":
        return (1 - x, y, c)
    if kind == "y":
        return (x, 1 - y, c)
    return (1 - x, 1 - y, c)


def _chip_of(p):
    return 2 * p[0] + p[1]


def _half(rows, which):
    h = rows // 2
    return pl.ds(pl.multiple_of(which * h, 16), h)


def _remote(src, dst, send_sem, recv_sem, to):
    return pltpu.make_async_remote_copy(src_ref=src, dst_ref=dst, send_sem=send_sem, recv_sem=recv_sem,
                                        device_id=to, device_id_type=MESH)


def allgather_balanced(shard, *, name):
    r, cols = shard.shape
    h, q = r // 2, r // 4

    def body(in_ref, out_ref, send_sems, recv_sems):
        x, y, c = _coords()
        me, sibling = (x, y, c), (x, y, 1 - c)
        nbr = ((1 - x, y, c), (x, 1 - y, c))
        chip = (2 * (1 - x) + y, 2 * x + (1 - y), 2 * (1 - x) + (1 - y))
        quarter = lambda core, i: pl.ds(pl.multiple_of(core * h + i * q, 16), q)
        sent = []

        def go(src, dst, slot, to):
            cp = _remote(src, dst, send_sems.at[slot], recv_sems.at[slot], to)
            cp.start()
            sent.append(cp)

        def landed(region, slot):
            _remote(region, region, send_sems.at[slot], recv_sems.at[slot], me).wait_recv()

        for i in range(2):
            for k in range(2):
                qi = k if i == 0 else 1 - k
                go(in_ref.at[quarter(c, qi)], out_ref.at[2 * x + y, quarter(c, qi)], 2 * k + qi, nbr[k])
        for k in range(2):
            piece = out_ref.at[chip[k], quarter(c, k)]
            landed(piece, 2 * k + k)
            go(piece, piece, 4 + k, nbr[1 - k])
            go(piece, piece, 6 + 2 * k + k, sibling)
        for k in range(2):
            piece = out_ref.at[chip[k], quarter(c, 1 - k)]
            landed(piece, 2 * k + 1 - k)
            go(piece, piece, 6 + 2 * k + 1 - k, sibling)
        for k in range(2):
            piece = out_ref.at[chip[2], quarter(c, k)]
            landed(piece, 4 + k)
            go(piece, piece, 10 + k, sibling)
        for k in range(2):
            for i in range(2):
                landed(out_ref.at[chip[k], quarter(1 - c, i)], 6 + 2 * k + i)
            landed(out_ref.at[chip[2], quarter(1 - c, k)], 10 + k)
        for cp in sent:
            cp.wait_send()

    return pl.pallas_call(
        body, in_specs=[ANY], out_specs=ANY, out_shape=SDS((4,) + shard.shape, shard.dtype),
        scratch_shapes=[pltpu.SemaphoreType.DMA((12,)), pltpu.SemaphoreType.DMA((12,))], name=name)(shard)


def _allgather_shapes(shards):
    return [SDS((4,) + a.shape, a.dtype) for a in shards]


def _allgather_sems(n):
    return [pltpu.SemaphoreType.DMA((n, 6)), pltpu.SemaphoreType.DMA((n, 6))]


def _allgather_rows(ref, is_halved, which):
    r = ref.shape[0]
    return _half(r, which) if is_halved else pl.ds(0, r)


def _allgather_first(ins, outs, send_sems, recv_sems, halved):
    x, y, c = _coords()
    my_chip = 2 * x + y
    cps = []
    for w in range(len(ins)):
        rows = _allgather_rows(ins[w], halved[w], c)
        for k, kind in enumerate(ICI_KINDS):
            cps.append(_remote(ins[w].at[rows], outs[w].at[my_chip, rows], send_sems.at[w, k], recv_sems.at[w, k],
                               _peer(kind, x, y, c)))
    return cps


def _allgather_start(ins, outs, send_sems, recv_sems, halved):
    for cp in _allgather_first(ins, outs, send_sems, recv_sems, halved):
        cp.start()


def _allgather_finish(ins, outs, send_sems, recv_sems, halved):
    x, y, c = _coords()
    me = (x, y, c)
    second = []
    for w in range(len(ins)):
        for k, kind in enumerate(ICI_KINDS):
            landed = outs[w].at[_chip_of(_peer(kind, x, y, c)), _allgather_rows(ins[w], halved[w], c)]
            _remote(landed, landed, send_sems.at[w, k], recv_sems.at[w, k], me).wait_recv()
            if halved[w]:
                cp = _remote(landed, landed, send_sems.at[w, 3 + k], recv_sems.at[w, 3 + k], _peer("c", x, y, c))
                cp.start()
                second.append(cp)
    for w in range(len(ins)):
        if halved[w]:
            for k, kind in enumerate(ICI_KINDS):
                other = outs[w].at[_chip_of(_peer(kind, x, y, c)), _allgather_rows(ins[w], True, 1 - c)]
                _remote(other, other, send_sems.at[w, 3 + k], recv_sems.at[w, 3 + k], me).wait_recv()
    for cp in _allgather_first(ins, outs, send_sems, recv_sems, halved) + second:
        cp.wait_send()


def _half_of(ref, by_cols, which):
    lead = (slice(None),) * (len(ref.shape) - 2)
    if by_cols:
        h = ref.shape[-1] // 2
        return ref.at[lead + (slice(None), pl.ds(pl.multiple_of(which * h, LANES), h))]
    return ref.at[lead + (_half(ref.shape[-2], which),)]


def _half_shape(shape, by_cols):
    return shape[:-1] + (shape[-1] // 2,) if by_cols else shape[:-2] + (shape[-2] // 2, shape[-1])


def grads_to_sibling(gs, by_cols, *, name):
    n = len(gs)

    def body(*refs):
        ins, outs = refs[:n], refs[n:2 * n]
        send_sems, recv_sems = refs[2 * n:]
        x, y, c = _coords()
        cps = []
        for w in range(n):
            cp = _remote(_half_of(ins[w], by_cols[w], 1 - c), outs[w], send_sems.at[w], recv_sems.at[w], _peer("c", x, y, c))
            cp.start()
            cps.append(cp)
        for cp in cps:
            cp.wait()

    return pl.pallas_call(
        body, in_specs=[ANY] * n, out_specs=[ANY] * n,
        out_shape=[SDS(_half_shape(a.shape, bc), a.dtype) for a, bc in zip(gs, by_cols)],
        scratch_shapes=[pltpu.SemaphoreType.DMA((n,)), pltpu.SemaphoreType.DMA((n,))], name=name)(*gs)


def _to_chips_shapes(ps):
    return [SDS((3,) + a.shape[1:], a.dtype) for a in ps]


def _to_chips_sems(n):
    return [pltpu.SemaphoreType.DMA((n, 3)), pltpu.SemaphoreType.DMA((n, 3))]


def _to_chips_copies(ins, outs, send_sems, recv_sems):
    x, y, c = _coords()
    cps = []
    for w in range(len(ins)):
        for k, kind in enumerate(ICI_KINDS):
            to = _peer(kind, x, y, c)
            cps.append(_remote(ins[w].at[_chip_of(to)], outs[w].at[k], send_sems.at[w, k], recv_sems.at[w, k], to))
    return cps


def _to_chips_start(ins, outs, send_sems, recv_sems):
    for cp in _to_chips_copies(ins, outs, send_sems, recv_sems):
        cp.start()


def _to_chips_finish(ins, outs, send_sems, recv_sems):
    for cp in _to_chips_copies(ins, outs, send_sems, recv_sems):
        cp.wait()


def _to_owners_shapes(ps):
    return [SDS((7, a.shape[1] // 2, a.shape[2]), a.dtype) for a in ps]


def _to_owners_sems(n):
    return [pltpu.SemaphoreType.DMA((n, 7)), pltpu.SemaphoreType.DMA((n, 7))]


def _to_owners_copies(ins, outs, send_sems, recv_sems):
    x, y, c = _coords()
    cps = []
    for w in range(len(ins)):
        rows = ins[w].shape[1]
        for k, kind in enumerate(ICI_KINDS):
            px, py, _ = _peer(kind, x, y, c)
            for h in range(2):
                cps.append(_remote(ins[w].at[2 * px + py, _half(rows, h)], outs[w].at[2 * k + c],
                                   send_sems.at[w, 2 * k + h], recv_sems.at[w, 2 * k + c], (px, py, h)))
        cps.append(_remote(ins[w].at[2 * x + y, _half(rows, 1 - c)], outs[w].at[6], send_sems.at[w, 6], recv_sems.at[w, 6],
                           _peer("c", x, y, c)))
    return cps


def _to_owners_start(ins, outs, send_sems, recv_sems):
    for cp in _to_owners_copies(ins, outs, send_sems, recv_sems):
        cp.start()


def _to_owners_finish(ins, outs, send_sems, recv_sems):
    for cp in _to_owners_copies(ins, outs, send_sems, recv_sems):
        cp.wait_send()
    for w in range(len(ins)):
        for slot in range(7):
            got = outs[w].at[slot]
            _remote(got, got, send_sems.at[w, slot], recv_sems.at[w, slot], _coords()).wait_recv()


EXCHANGES = {"to_chips": (_to_chips_shapes, _to_chips_sems, _to_chips_start, _to_chips_finish),
             "to_owners": (_to_owners_shapes, _to_owners_sems, _to_owners_start, _to_owners_finish)}


def halves_to_full(hs, by_cols, *, name):
    n = len(hs)

    def body(*refs):
        ins, outs = refs[:n], refs[n:2 * n]
        send_sems, recv_sems = refs[2 * n:]
        x, y, c = _coords()
        cps = []
        for w in range(n):
            cp = _remote(ins[w], _half_of(outs[w], by_cols[w], c), send_sems.at[w], recv_sems.at[w], _peer("c", x, y, c))
            cp.start()
            cps.append(cp)
        for cp in cps:
            cp.wait()

    return pl.pallas_call(
        body, in_specs=[ANY] * n, out_specs=[ANY] * n,
        out_shape=[SDS((a.shape[0], 2 * a.shape[1]) if bc else (2 * a.shape[0], a.shape[1]), a.dtype)
                   for a, bc in zip(hs, by_cols)],
        scratch_shapes=[pltpu.SemaphoreType.DMA((n,)), pltpu.SemaphoreType.DMA((n,))],
        name=name)(*hs)


def _row_tile(rows):
    for cand in (256, 192, 176, 128, 64, 32, 16):
        if rows % cand == 0:
            return cand
    return rows


def chip_sum(g, recv, c_arr, by_cols, *, name):
    _, r, cols = g.shape

    def body(c_ref, g_ref, r_ref, f_ref, b_ref):
        tot = g_ref[...] + r_ref[...]
        f_ref[...] = tot
        b_ref[...] = tot.astype(BF16)

    if by_cols:
        tc = 4 * LANES
        nblk = cols // 2 // tc
        shape = (4, r, cols // 2)
        blk = pl.BlockSpec((None, r, tc), lambda j, i, c_ref: (j, 0, i))
        mine = pl.BlockSpec((None, r, tc), lambda j, i, c_ref: (j, 0, c_ref[0] * nblk + i))
    else:
        tr = _row_tile(r // 2)
        nblk = r // 2 // tr
        shape = (4, r // 2, cols)
        blk = pl.BlockSpec((None, tr, cols), lambda j, i, c_ref: (j, i, 0))
        mine = pl.BlockSpec((None, tr, cols), lambda j, i, c_ref: (j, c_ref[0] * nblk + i, 0))
    grid_spec = pltpu.PrefetchScalarGridSpec(num_scalar_prefetch=1, grid=(4, nblk), in_specs=[mine, blk], out_specs=[blk, blk])
    return pl.pallas_call(body, grid_spec=grid_spec, out_shape=[SDS(shape, F32), SDS(shape, BF16)],
                          name=name, compiler_params=_cp(("parallel", "parallel")))(c_arr, g, recv)


def final_sum(pf, recv, chip_arr, *, name):
    _, h, cols = pf.shape
    tr = _row_tile(h)

    def body(chip_ref, p_ref, r_ref, o_ref):
        o_ref[...] = ((p_ref[...] + r_ref[0].astype(F32)) + r_ref[1].astype(F32)) + r_ref[2].astype(F32)

    grid_spec = pltpu.PrefetchScalarGridSpec(
        num_scalar_prefetch=1, grid=(h // tr,),
        in_specs=[pl.BlockSpec((None, tr, cols), lambda i, chip_ref: (chip_ref[0], i, 0)),
                  pl.BlockSpec((3, tr, cols), lambda i, chip_ref: (0, i, 0))],
        out_specs=pl.BlockSpec((tr, cols), lambda i, chip_ref: (i, 0)))
    return pl.pallas_call(body, grid_spec=grid_spec, out_shape=SDS((h, cols), F32), name=name,
                          compiler_params=_cp(("parallel",)))(chip_arr, pf, recv)


def owner_sum(g, recv, pos_arr, *, name):
    _, r, cols = g.shape
    h = r // 2
    tr = _row_tile(h)
    nblk = h // tr

    def body(pos_ref, g_ref, r_ref, o_ref):
        tot = g_ref[...]
        for slot in range(7):
            tot = tot + r_ref[slot].astype(F32)
        o_ref[...] = tot

    grid_spec = pltpu.PrefetchScalarGridSpec(
        num_scalar_prefetch=1, grid=(nblk,),
        in_specs=[pl.BlockSpec((None, tr, cols), lambda i, pos: (pos[0], pos[1] * nblk + i, 0)),
                  pl.BlockSpec((7, tr, cols), lambda i, pos: (0, i, 0))],
        out_specs=pl.BlockSpec((tr, cols), lambda i, pos: (i, 0)))
    return pl.pallas_call(body, grid_spec=grid_spec, out_shape=SDS((h, cols), F32), name=name,
                          compiler_params=_cp(("parallel",)))(pos_arr, g, recv)


def allreduce_small(v, *, name):
    rws, cols = v.shape

    def body(v_ref, all_ref, sum_ref, send_sems, recv_sems, local_sem):
        x, y, c = _coords()
        me, sibling = (x, y, c), (x, y, 1 - c)
        chips = [(1 - x, y), (x, 1 - y), (1 - x, 1 - y)]

        def rows(px, py, pc):
            return all_ref.at[pl.ds(pl.multiple_of((4 * px + 2 * py + pc) * rws, 8), rws), :]

        def copy(k, block, to, src=None):
            return _remote(rows(*block) if src is None else src, rows(*block), send_sems.at[k], recv_sems.at[k], to)

        mine = pltpu.make_async_copy(v_ref, rows(*me), local_sem)
        mine.start()
        first = [copy(0, me, sibling, src=v_ref)]
        first += [copy(1 + j, me, (*chip, c), src=v_ref) for j, chip in enumerate(chips)]
        for cp in first:
            cp.start()
        passed = [copy(4 + j, (*chip, c), sibling) for j, chip in enumerate(chips)]
        for j, chip in enumerate(chips):
            copy(1 + j, (*chip, c), me).wait_recv()
            passed[j].start()
        copy(0, sibling, me).wait_recv()
        for j, chip in enumerate(chips):
            copy(4 + j, (*chip, 1 - c), me).wait_recv()
        for cp in first + passed:
            cp.wait_send()
        mine.wait()
        tot = all_ref[0:rws, :]
        for dev in range(1, 8):
            tot = tot + all_ref[dev * rws:(dev + 1) * rws, :]
        sum_ref[...] = tot

    vm = pl.BlockSpec(memory_space=pltpu.VMEM)
    return pl.pallas_call(
        body, in_specs=[vm], out_specs=[vm, vm],
        out_shape=[SDS((8 * rws, cols), v.dtype), SDS((rws, cols), v.dtype)],
        scratch_shapes=[pltpu.SemaphoreType.DMA((7,)), pltpu.SemaphoreType.DMA((7,)), pltpu.SemaphoreType.DMA],
        name=name)(v)[1]


def _pack_rows(parts, rows):
    out = []
    for a, r in zip(parts, rows):
        flat = a.reshape(-1)
        flat = jnp.pad(flat, (0, r * LANES - flat.shape[0]))
        out.append(flat.reshape(r, LANES))
    return jnp.concatenate(out, axis=0)


def _unpack_rows(packed, shapes, rows):
    out, at = [], 0
    for shp, r in zip(shapes, rows):
        size = int(np.prod(shp))
        out.append(packed[at:at + r].reshape(-1)[:size].reshape(shp))
        at += r
    return out


def kernel(x, g_pre_mix, w_in, b_forget, w_o_fox, w_o_dil, w_out, g_post_mix, g_pre_ffn, w_up, conv_w, conv_b, w_down, g_post_ffn, loss_target, m_g_pre_mix, m_w_in, m_b_forget, m_w_o_fox, m_w_o_dil, m_w_out, m_g_post_mix, m_g_pre_ffn, m_w_up, m_conv_w, m_conv_b, m_w_down, m_g_post_ffn, v_g_pre_mix, v_w_in, v_b_forget, v_w_o_fox, v_w_o_dil, v_w_out, v_g_post_mix, v_g_pre_ffn, v_w_up, v_conv_w, v_conv_b, v_w_down, v_g_post_ffn):
    xi, yi, ci = _coords()
    chip = 2 * xi + yi
    c_arr = jnp.reshape(ci, (1,)).astype(jnp.int32)
    chip_arr = jnp.reshape(chip, (1,)).astype(jnp.int32)
    xs = x[0]
    target = loss_target[0]
    s, d = xs.shape
    f_half = w_down.shape[1] * 4
    cols_in = w_in.shape[2]

    big = (w_in, w_o_fox, w_o_dil, w_out, w_up, w_down)
    shards = [w[0].astype(BF16) for w in big]
    a_in = allgather_balanced(shards[0], name="allgather_w_in")
    w_in_full = jnp.concatenate([jnp.where(chip == j, shards[0], a_in[j]) for j in range(4)], axis=1)
    nf = N_HEADS
    e_a, e_b = 3 * ATT_W, 3 * ATT_W + nf
    wz = jnp.concatenate([w_in_full[:, :e_a], w_in_full[:, e_b:]], axis=1)
    wf = jnp.pad(w_in_full[:, e_a:e_b], ((0, 0), (0, LANES - nf)))
    cb = conv_b
    bfo = jnp.pad(b_forget, ((0, 0), (0, LANES - nf)))

    h1 = rmsnorm_fwd(xs, g_pre_mix)
    z = mm([(h1, d, 0)], [(wz, d, 0)], nt=False, out_dtype=BF16, tm=s, tn=512, name="in_proj")
    fa = mm([(h1, d, 0)], [(wf, d, 0)], nt=False, out_dtype=F32, tm=s, tn=LANES, name="in_proj_forget")
    q_aug, k_aug, v_aug = fox_prep(z, fa, bfo)
    later = shards[1:] + [conv_w[0]]
    ya, lse_a, *late = fox_fwd(q_aug, k_aug, v_aug, gather=later, halved=[True] * 5 + [False], hps=4)
    a_of, a_od, a_out, a_up, a_down, a_cw = [
        lax.dynamic_update_index_in_dim(a4, own, chip, 0) for a4, own in zip(late, later)]
    cw = jnp.concatenate([a_cw[j] for j in range(4)], axis=1)
    wo_a = jnp.concatenate([a_of[j] for j in range(4)], axis=1)
    wo_b = jnp.concatenate([a_od[j] for j in range(4)], axis=1)
    w_o = a_out.reshape(d, d)
    w_dn = a_down.reshape(f_half, d)
    wu_a = jnp.concatenate([a_up[0], a_up[1]], axis=1)
    wu_b = jnp.concatenate([a_up[2], a_up[3]], axis=1)
    cos_t, sin_t = rope_cos_sin(s)
    yb, lse_b = dil_fwd_all(z, cos_t, sin_t)
    pa, pb, mixed = gate_mix(ya, yb, wo_a, wo_b, z)
    y1, x1, h2 = proj_norm_res(mixed, w_o, g_post_mix, xs, g_pre_ffn, tm=1024, name="out_proj")
    ua, ub, conv_a, conv_bh, mid = ffn_up(h2, wu_a, wu_b, cw, cb)
    dout, dy2, gg_post_ffn, sq = proj_norm_loss(mid, w_dn, g_post_ffn, x1, target, name="down_proj")

    dmid = mm([(dy2, d, 0)], [(w_dn, d, 0)], nt=True, out_dtype=BF16, tm=2048, tn=f_half // 2, name="down_dgrad")
    dw_down, dw_down16 = wgrad((mid, f_half, 0), dy2, tk=f_half // 2, tn=1024, ts=2048, name="down_wgrad", bf16_copy=True)
    dua, dub, gc_a, gc_b = ffn_bwd(dmid, ua, ub, conv_a, conv_bh, cw)
    dx1, dy1, gg_pre_ffn, gg_post_mix = mm_norm_bwd(
        [(dua, f_half, 0), (dub, f_half, 0)], [(wu_a, f_half, 0), (wu_b, f_half, 0)],
        [(x1, g_pre_ffn, dout, F32), (y1, g_post_mix, None, BF16)], name="up_dgrad")
    dw_up = None
    for k, du in enumerate((dua, dub)):
        dw_up = wgrad((h2, d, 0), du, tk=1024, tn=f_half // 2, ts=2048, name=f"up_wgrad_{k}", chip_major=True,
                      slabs=(4, 2 * k), into=dw_up, bf16_copy=True)
    g_ffn = [(dw_up[0], dw_up[1]), (dw_down.reshape(4, f_half // 4, d), dw_down16.reshape(4, f_half // 4, d))]
    dw_out, dw_out16 = wgrad((mixed, d, 0), dy1, tk=1024, tn=1024, ts=2048, name="out_wgrad", bf16_copy=True)
    dpa, dpb, dz_g, dya, dyb, dd_a = mix_bwd(dy1, w_o, z, pa, pb, wo_a, wo_b, ya)
    by_chip_cols = lambda a: jnp.stack([a[:, j * (d // 4):(j + 1) * (d // 4)] for j in range(4)], axis=0)
    dw_of = [by_chip_cols(a) for a in wgrad((ya, ATT_W, 0), dpa, tk=ATT_W, tn=d, ts=1024, name="fox_o_wgrad", bf16_copy=True)]
    dw_od = [by_chip_cols(a) for a in wgrad((yb, ATT_W, 0), dpb, tk=ATT_W, tn=d, ts=1024, name="dil_o_wgrad", bf16_copy=True)]
    g_mix = [dw_of, dw_od, (dw_out.reshape(4, d // 4, d), dw_out16.reshape(4, d // 4, d))]
    dq_aug, dk_aug, dv_a, *got_ffn = fox_bwd(q_aug, k_aug, z, dya, lse_a, dd_a, exchange=[g[1] for g in g_ffn], kind="to_owners")
    dz_a, dfa, gg_bf = fox_post(dq_aug, dk_aug, dv_a, fa, bfo)
    *dz_b, got_of, got_od, got_out = dil_bwd_all(z, cos_t, sin_t, dyb, lse_b, yb, exchange=[g[1] for g in g_mix],
                                                 kind="to_owners")
    got_mix = [got_of, got_od, got_out]
    dwt_a = wgrad((dz_a, e_a, 0), h1, tk=e_a // 2, tn=d, ts=2048, name="in_wgrad_a")
    dwt_b = [wgrad((part, ATT_W, 0), h1, tk=ATT_W, tn=d, ts=2048, name=f"in_wgrad_b{k}") for k, part in enumerate(dz_b)]
    dwt_g = wgrad((dz_g, 2 * d, 0), h1, tk=d, tn=d, ts=2048, name="in_wgrad_g")
    dwt_f = wgrad((dfa, LANES, 0), h1, tk=LANES, tn=d, ts=2048, name="in_wgrad_f")
    dwt_full = jnp.concatenate([dwt_a, dwt_f[:nf], *dwt_b, dwt_g], axis=0)
    dw_in = jnp.stack([dwt_full[j * cols_in:(j + 1) * cols_in] for j in range(4)], axis=0)
    from_sib = grads_to_sibling([dw_in], [True], name="grads_to_sibling_in")
    sum_in = chip_sum(dw_in, from_sib[0], c_arr, True, name="chip_sum_w_in")
    grad_x, gg_pre_mix, got_in = mm_norm_bwd(
        [(dz_a, e_a, 0), *[(part, ATT_W, 0) for part in dz_b], (dz_g, d, 0), (dz_g, d, 1), (dfa, LANES, 0)],
        [(wz, e_a, 0), *[(wz, ATT_W, Z_QB + k) for k in range(3)], (wz, d, 3), (wz, d, 4), (wf, LANES, 0)],
        [(xs, g_pre_mix, dx1, F32)], exchange=[sum_in[1]], name="in_dgrad")

    names = ("w_in", "w_o_fox", "w_o_dil", "w_out", "w_up", "w_down")
    pos_arr = jnp.concatenate([chip_arr, c_arr])
    halves = [final_sum(sum_in[0], got_in, chip_arr, name="final_sum_w_in")] + [
        owner_sum(g[0], got, pos_arr, name=f"owner_sum_{nm}") for g, got, nm in zip(g_mix + g_ffn, got_mix + got_ffn, names[1:])]
    from_half = halves_to_full(halves, [True] + [False] * 5, name="halves_to_full")
    g_big = [None] + [lax.dynamic_update_slice_in_dim(full, mine, ci * mine.shape[0], axis=0)
                      for full, mine in zip(from_half[1:], halves[1:])]
    upd_big = [adamw(w[0], g, m[0], v[0], name=f"adamw_{nm}") for w, g, m, v, nm in list(zip(
        big, g_big, (m_w_in, m_w_o_fox, m_w_o_dil, m_w_out, m_w_up, m_w_down),
        (v_w_in, v_w_o_fox, v_w_o_dil, v_w_out, v_w_up, v_w_down), names))[1:]]
    to_t = lambda a: jnp.transpose(a, (2, 0, 1))
    from_t = lambda a: jnp.transpose(a, (1, 2, 0))
    *upd_in, g_in_t = adamw_rows_view(to_t(w_in), halves[0], from_half[0], to_t(m_w_in), to_t(v_w_in), c_arr,
                                      name="adamw_w_in")

    g_cw_loc = jnp.concatenate([gc_a[0:3], gc_b[0:3]], axis=1)
    g_cb_loc = jnp.concatenate([gc_a[3:4], gc_b[3:4]], axis=1)
    small_loc = [gg_pre_mix, gg_post_mix, gg_pre_ffn, gg_post_ffn, g_cb_loc, gg_bf[:, :nf], g_cw_loc, sq * (0.5 / d)]
    red_rows = (8, 8, 8, 8, 48, 8, 136, 8)
    red = allreduce_small(_pack_rows(small_loc, red_rows), name="allreduce_small")
    g_pm, g_qm, g_pf, g_qf, g_cb, g_bf, g_cw_full, loss_11 = _unpack_rows(red, [a.shape for a in small_loc], red_rows)
    loss = loss_11[0, 0]
    cols_cw = conv_w.shape[2]
    g_cw = lax.dynamic_slice_in_dim(g_cw_full, chip * cols_cw, cols_cw, axis=1)
    small_w = (g_pre_mix, g_post_mix, g_pre_ffn, g_post_ffn, conv_b, b_forget, conv_w[0])
    small_m = (m_g_pre_mix, m_g_post_mix, m_g_pre_ffn, m_g_post_ffn, m_conv_b, m_b_forget, m_conv_w[0])
    small_v = (v_g_pre_mix, v_g_post_mix, v_g_pre_ffn, v_g_post_ffn, v_conv_b, v_b_forget, v_conv_w[0])
    small_g = (g_pm, g_qm, g_pf, g_qf, g_cb, g_bf, g_cw)
    small_names = ("g_pre_mix", "g_post_mix", "g_pre_ffn", "g_post_ffn", "conv_b", "b_forget", "conv_w")
    per_param = [adamw(w, g, m, v, name=f"adamw_{nm}") for w, g, m, v, nm in zip(small_w, small_g, small_m, small_v, small_names)]
    upd_small = [[u[j] for u in per_param] for j in range(3)]

    order = ("g_pre_mix", "w_in", "b_forget", "w_o_fox", "w_o_dil", "w_out", "g_post_mix", "g_pre_ffn", "w_up", "conv_w",
             "conv_b", "w_down", "g_post_ffn")
    grads, deltas, new_ms, new_vs = {}, {}, {}, {}
    grads["w_in"] = from_t(g_in_t)
    deltas["w_in"], new_ms["w_in"], new_vs["w_in"] = (from_t(a) for a in upd_in)
    for k, nm in enumerate(names[1:]):
        grads[nm] = g_big[k + 1][None]
        deltas[nm], new_ms[nm], new_vs[nm] = (a[None] for a in upd_big[k])
    for k, nm in enumerate(small_names):
        lead = (lambda a: a[None]) if nm == "conv_w" else (lambda a: a)
        grads[nm] = lead(small_g[k])
        deltas[nm], new_ms[nm], new_vs[nm] = (lead(upd_small[j][k]) for j in range(3))
    return (loss, grad_x[None], *[grads[nm] for nm in order], *[deltas[nm] for nm in order],
            *[new_ms[nm] for nm in order], *[new_vs[nm] for nm in order])
```

```python
import functools
import math

import numpy as np
import jax
import jax.numpy as jnp
from jax import lax
from jax.experimental import pallas as pl
from jax.experimental.pallas import tpu as pltpu

F32 = jnp.float32
BF16 = jnp.bfloat16
SDS = jax.ShapeDtypeStruct
MESH = pl.DeviceIdType.MESH

HEAD_DIM = 64
N_HEADS = 8
LANES = 128
ATT_W = N_HEADS * HEAD_DIM
DIL_PATTERNS = ((128, 1), (512, 4), (2048, 16))
DIL_BLK = 128
ROPE_DIM = HEAD_DIM // 4
ROPE_THETA = 500000.0
RMS_EPS = 1e-6
NEG = -1e30
QK_SCALE = 1.0 / math.sqrt(HEAD_DIM)
ADAM_LR, ADAM_B1, ADAM_B2, ADAM_EPS, ADAM_WD, ADAM_STEP = 0.001, 0.9, 0.999, 1e-08, 0.01, 10
VMEM_LIMIT = 56 * 1024 * 1024

Z_QA, Z_KA, Z_VA, Z_QB, Z_KB, Z_VB = 0, 1, 2, 3, 4, 5
Z_W = 5120


def _cp(sem):
    return pltpu.CompilerParams(dimension_semantics=sem, vmem_limit_bytes=VMEM_LIMIT)


def _nt(a, b):
    return lax.dot_general(a, b, (((1,), (1,)), ((), ())), preferred_element_type=F32)


def _tn(a, b):
    return lax.dot_general(a, b, (((0,), (0,)), ((), ())), preferred_element_type=F32)


def _nn(a, b):
    return jnp.dot(a, b, preferred_element_type=F32)


def _lane(shape):
    return lax.broadcasted_iota(jnp.int32, shape, 1)


def _row(shape):
    return lax.broadcasted_iota(jnp.int32, shape, 0)


def rmsnorm_fwd(x, g, *, tm=1024):
    s, d = x.shape

    def body(x_ref, g_ref, h_ref):
        xv = x_ref[...]
        inv = lax.rsqrt(jnp.mean(xv * xv, axis=-1, keepdims=True) + RMS_EPS)
        h_ref[...] = (xv * inv * g_ref[...]).astype(h_ref.dtype)

    return pl.pallas_call(
        body, grid=(s // tm,),
        in_specs=[pl.BlockSpec((tm, d), lambda i: (i, 0)), pl.BlockSpec((1, d), lambda i: (0, 0))],
        out_specs=pl.BlockSpec((tm, d), lambda i: (i, 0)),
        out_shape=SDS((s, d), BF16), name="rmsnorm_fwd", compiler_params=_cp(("parallel",)))(x, g)


def mm(a_views, b_views, *, nt, out_dtype, tm, tn, name):
    n_p = len(a_views)
    m = a_views[0][0].shape[0]
    n = b_views[0][0].shape[0] if nt else b_views[0][0].shape[1]

    def body(*refs):
        o_ref = refs[-1]
        acc = None
        for p in range(n_p):
            av = refs[p][...].astype(BF16)
            bv = refs[n_p + p][...].astype(BF16)
            dv = _nt(av, bv) if nt else _nn(av, bv)
            acc = dv if acc is None else acc + dv
        o_ref[...] = acc.astype(o_ref.dtype)

    in_specs = []
    for arr, w, blk in a_views:
        in_specs.append(pl.BlockSpec((tm, w), functools.partial(lambda i, j, blk: (i, blk), blk=blk)))
    for arr, w, blk in b_views:
        if nt:
            in_specs.append(pl.BlockSpec((tn, w), functools.partial(lambda i, j, blk: (j, blk), blk=blk)))
        else:
            in_specs.append(pl.BlockSpec((w, tn), lambda i, j: (0, j)))
    return pl.pallas_call(
        body, grid=(m // tm, n // tn), in_specs=in_specs,
        out_specs=pl.BlockSpec((tm, tn), lambda i, j: (i, j)),
        out_shape=SDS((m, n), out_dtype), name=name,
        compiler_params=_cp(("parallel", "parallel")))(*[a[0] for a in a_views], *[b[0] for b in b_views])


def wgrad(a_view, g, *, tk, tn, ts, name, chip_major=False, slabs=None, into=None, bf16_copy=False):
    arr, ka, blk = a_view
    s, n = g.shape
    ns = s // ts
    total, first = slabs if slabs else (n // tn, 0)
    n_into = 0 if into is None else (2 if bf16_copy else 1)

    def body(a_ref, g_ref, *rest):
        o_ref = rest[n_into]

        @pl.when(pl.program_id(2) == 0)
        def _():
            o_ref[...] = jnp.zeros_like(o_ref)

        o_ref[...] += _tn(a_ref[...].astype(BF16), g_ref[...].astype(BF16))
        if bf16_copy:
            @pl.when(pl.program_id(2) == ns - 1)
            def _():
                rest[n_into + 1][...] = o_ref[...].astype(BF16)

    if chip_major:
        out_spec = pl.BlockSpec((None, tk, tn), lambda i, j, k: (first + j, i, 0))
        shape = (total, ka, tn)
    else:
        out_spec = pl.BlockSpec((tk, tn), lambda i, j, k: (i, j))
        shape = (ka, n)
    in_specs = [pl.BlockSpec((ts, tk), lambda i, j, k: (k, blk * (ka // tk) + i)),
                pl.BlockSpec((ts, tn), lambda i, j, k: (k, j))]
    args = [arr, g]
    if into is not None:
        earlier = list(into) if bf16_copy else [into]
        in_specs += [pl.BlockSpec(memory_space=pl.ANY)] * len(earlier)
        args += earlier
    out = pl.pallas_call(
        body, grid=(ka // tk, n // tn, ns), in_specs=in_specs,
        out_specs=[out_spec, out_spec] if bf16_copy else out_spec,
        out_shape=[SDS(shape, F32), SDS(shape, BF16)] if bf16_copy else SDS(shape, F32), name=name,
        input_output_aliases={2 + k: k for k in range(n_into)},
        compiler_params=_cp(("parallel", "parallel", "arbitrary")))(*args)
    return out


def _norm_bwd_rows(dh, xh, inv, g):
    dxh = dh * g
    dx = inv * (dxh - xh * jnp.mean(dxh * xh, axis=-1, keepdims=True))
    return dx, jnp.sum((dh * xh).reshape(dh.shape[0] // 8, 8, dh.shape[1]), axis=0)


def proj_norm_res(a, w, g, xres, g_next, *, tm=512, name):
    s, k = a.shape
    d = w.shape[1]

    def body(a_ref, w_ref, g_ref, x_ref, gn_ref, y_ref, o_ref, h_ref):
        y = _nn(a_ref[...], w_ref[...])
        inv = lax.rsqrt(jnp.mean(y * y, axis=-1, keepdims=True) + RMS_EPS)
        xn = x_ref[...] + y * inv * g_ref[...]
        y_ref[...] = y
        o_ref[...] = xn
        inv_n = lax.rsqrt(jnp.mean(xn * xn, axis=-1, keepdims=True) + RMS_EPS)
        h_ref[...] = (xn * inv_n * gn_ref[...]).astype(h_ref.dtype)

    row = pl.BlockSpec((tm, d), lambda i: (i, 0))
    vec = pl.BlockSpec((1, d), lambda i: (0, 0))
    return pl.pallas_call(
        body, grid=(s // tm,),
        in_specs=[pl.BlockSpec((tm, k), lambda i: (i, 0)), pl.BlockSpec((k, d), lambda i: (0, 0)), vec, row, vec],
        out_specs=[row, row, row], out_shape=[SDS((s, d), F32), SDS((s, d), F32), SDS((s, d), BF16)], name=name,
        compiler_params=_cp(("parallel",)))(a, w, g, xres, g_next)


def proj_norm_loss(a, w, g, xres, target, *, tm=512, name):
    s, k = a.shape
    d = w.shape[1]
    n = s // tm

    def body(a_ref, w_ref, g_ref, x_ref, t_ref, do_ref, dy_ref, dg_ref, l_ref, acc):
        i = pl.program_id(0)

        @pl.when(i == 0)
        def _():
            acc[...] = jnp.zeros_like(acc)
            l_ref[...] = jnp.zeros_like(l_ref)

        y = _nn(a_ref[...], w_ref[...])
        inv = lax.rsqrt(jnp.mean(y * y, axis=-1, keepdims=True) + RMS_EPS)
        yh = y * inv
        err = x_ref[...] + yh * g_ref[...] - t_ref[...]
        dout = err * (1.0 / d)
        do_ref[...] = dout
        l_ref[...] += jnp.sum(jnp.sum(err * err, axis=1, keepdims=True), axis=0, keepdims=True)
        dy, part = _norm_bwd_rows(dout, yh, inv, g_ref[...])
        dy_ref[...] = dy.astype(dy_ref.dtype)
        acc[...] += part

        @pl.when(i == n - 1)
        def _():
            dg_ref[...] = jnp.sum(acc[...], axis=0, keepdims=True)

    row = pl.BlockSpec((tm, d), lambda i: (i, 0))
    vec = pl.BlockSpec((1, d), lambda i: (0, 0))
    return pl.pallas_call(
        body, grid=(n,),
        in_specs=[pl.BlockSpec((tm, k), lambda i: (i, 0)), pl.BlockSpec((k, d), lambda i: (0, 0)), vec, row, row],
        out_specs=[row, row, vec, pl.BlockSpec((1, 1), lambda i: (0, 0))],
        out_shape=[SDS((s, d), F32), SDS((s, d), BF16), SDS((1, d), F32), SDS((1, 1), F32)],
        scratch_shapes=[pltpu.VMEM((8, d), F32)], name=name, compiler_params=_cp(("arbitrary",)))(a, w, g, xres, target)


def mm_norm_bwd(a_views, b_views, stages, exchange=(), *, tm=256, name):
    n_p, n_s, ne = len(a_views), len(stages), len(exchange)
    s = a_views[0][0].shape[0]
    d = b_views[0][0].shape[0]
    n = s // tm
    has_res = [st[2] is not None for st in stages]

    def body(*refs):
        a_refs, b_refs = refs[:n_p], refs[n_p:2 * n_p]
        at = 2 * n_p
        st_refs = []
        for k in range(n_s):
            cnt = 3 if has_res[k] else 2
            st_refs.append(refs[at:at + cnt])
            at += cnt
        e_ins = refs[at:at + ne]
        at += ne
        dx_refs, dg_refs = refs[at:at + n_s], refs[at + n_s:at + 2 * n_s]
        at += 2 * n_s
        e_outs = refs[at:at + ne]
        at += ne
        accs = refs[at:at + n_s]
        comm = (e_ins, e_outs) + tuple(refs[at + n_s:])
        i = pl.program_id(0)

        @pl.when(i == 0)
        def _():
            for acc in accs:
                acc[...] = jnp.zeros_like(acc)
            if ne:
                _to_chips_start(*comm)

        dh = None
        for p in range(n_p):
            part = _nt(a_refs[p][...].astype(BF16), b_refs[p][...].astype(BF16))
            dh = part if dh is None else dh + part
        for k in range(n_s):
            xv = st_refs[k][0][...]
            inv = lax.rsqrt(jnp.mean(xv * xv, axis=-1, keepdims=True) + RMS_EPS)
            dx, part = _norm_bwd_rows(dh, xv * inv, inv, st_refs[k][1][...])
            if has_res[k]:
                dx = dx + st_refs[k][2][...]
            dx_refs[k][...] = dx.astype(dx_refs[k].dtype)
            accs[k][...] += part
            dh = dx

        @pl.when(i == n - 1)
        def _():
            for k in range(n_s):
                dg_refs[k][...] = jnp.sum(accs[k][...], axis=0, keepdims=True)
            if ne:
                _to_chips_finish(*comm)

    row = pl.BlockSpec((tm, d), lambda i: (i, 0))
    vec = pl.BlockSpec((1, d), lambda i: (0, 0))
    in_specs, args = [], []
    for arr, w, blk in a_views:
        in_specs.append(pl.BlockSpec((tm, w), functools.partial(lambda i, blk: (i, blk), blk=blk)))
        args.append(arr)
    for arr, w, blk in b_views:
        in_specs.append(pl.BlockSpec((d, w), functools.partial(lambda i, blk: (0, blk), blk=blk)))
        args.append(arr)
    for x, g, res, _ in stages:
        in_specs += [row, vec] + ([row] if res is not None else [])
        args += [x, g] + ([res] if res is not None else [])
    return pl.pallas_call(
        body, grid=(n,), in_specs=in_specs + [ANY] * ne,
        out_specs=[row] * n_s + [vec] * n_s + [ANY] * ne,
        out_shape=[SDS((s, d), st[3]) for st in stages] + [SDS((1, d), F32)] * n_s + _to_chips_shapes(exchange),
        scratch_shapes=[pltpu.VMEM((8, d), F32)] * n_s + (_to_chips_sems(ne) if ne else []), name=name,
        compiler_params=_cp(("arbitrary",)))(*args, *exchange)


def _split3(v):
    hi = v.astype(BF16).astype(F32)
    r = v - hi
    mid = r.astype(BF16).astype(F32)
    lo = (r - mid).astype(BF16).astype(F32)
    return hi, mid, lo


def _tri(n, upper):
    r = np.arange(n)
    m = (r[:, None] <= r[None, :]) if upper else (r[:, None] >= r[None, :])
    return jnp.asarray(m.astype(np.float32))


def fox_prep(z, fa, bfo, *, tb=512):
    s = z.shape[0]
    n = s // tb

    def body(q_ref, k_ref, v_ref, fa_ref, b_ref, tri_ref, qa_ref, ka_ref, va_ref, carry):
        @pl.when(pl.program_id(0) == 0)
        def _():
            carry[...] = jnp.zeros_like(carry)

        xv = fa_ref[...] + b_ref[...]
        logf = jnp.minimum(xv, 0.0) - jnp.log(1.0 + jnp.exp(-jnp.abs(xv)))
        csum = jnp.dot(tri_ref[...], logf, preferred_element_type=F32, precision=lax.Precision.HIGHEST) + carry[0:1, :]
        carry[0:1, :] = csum[tb - 1:tb, :]
        lane = _lane((tb, LANES))
        for h in range(N_HEADS):
            hi, mid, lo = _split3(csum[:, h:h + 1])
            pair = (h // 2) * LANES
            qv = q_ref[:, pair:pair + LANES].astype(F32)
            kv = k_ref[:, pair:pair + LANES].astype(F32)
            vv = v_ref[:, pair:pair + LANES].astype(F32)
            if h % 2:
                qv = pltpu.roll(qv, 64, axis=1)
                kv = pltpu.roll(kv, 64, axis=1)
                vv = pltpu.roll(vv, 64, axis=1)
            va_ref[:, h * LANES:(h + 1) * LANES] = jnp.where(lane < 64, vv, jnp.where(lane == 64, 1.0, 0.0)).astype(BF16)
            one = jnp.where((lane >= 67) & (lane < 70), 1.0, 0.0)
            q_x = jnp.where(lane == 64, hi, jnp.where(lane == 65, mid, jnp.where(lane == 66, lo, one)))
            one = jnp.where((lane >= 64) & (lane < 67), 1.0, 0.0)
            k_x = jnp.where(lane == 67, -hi, jnp.where(lane == 68, -mid, jnp.where(lane == 69, -lo, one)))
            qa_ref[:, h * LANES:(h + 1) * LANES] = jnp.where(lane < 64, qv * QK_SCALE, q_x).astype(BF16)
            ka_ref[:, h * LANES:(h + 1) * LANES] = jnp.where(lane < 64, kv, k_x).astype(BF16)

    return pl.pallas_call(
        body, grid=(n,),
        in_specs=[pl.BlockSpec((tb, ATT_W), lambda i: (i, Z_QA)), pl.BlockSpec((tb, ATT_W), lambda i: (i, Z_KA)),
                  pl.BlockSpec((tb, ATT_W), lambda i: (i, Z_VA)),
                  pl.BlockSpec((tb, LANES), lambda i: (i, 0)), pl.BlockSpec((1, LANES), lambda i: (0, 0)),
                  pl.BlockSpec((tb, tb), lambda i: (0, 0))],
        out_specs=[pl.BlockSpec((tb, N_HEADS * LANES), lambda i: (i, 0))] * 3,
        out_shape=[SDS((s, N_HEADS * LANES), BF16)] * 3,
        scratch_shapes=[pltpu.VMEM((8, LANES), F32)],
        name="fox_prep", compiler_params=_cp(("arbitrary",)))(z, z, z, fa, bfo, _tri(tb, False))


def _causal_pairs(n, k_major):
    if k_major:
        pairs = [(qi, kj) for kj in range(n) for qi in range(kj, n)]
    else:
        pairs = [(qi, kj) for qi in range(n) for kj in range(qi + 1)]
    return (jnp.asarray([p[0] for p in pairs], jnp.int32), jnp.asarray([p[1] for p in pairs], jnp.int32), len(pairs))


def fox_fwd(q_aug, k_aug, v_aug, gather=(), halved=(), *, t=1024, hps=4):
    s = v_aug.shape[0]
    qi_arr, kj_arr, n_pairs = _causal_pairs(s // t, False)
    ng = len(gather)
    n_groups = N_HEADS // hps

    def body(qi_ref, kj_ref, q_ref, k_ref, v_ref, *rest):
        g_ins, (o_ref, lse_ref), g_outs = rest[:ng], rest[ng:ng + 2], rest[ng + 2:2 * ng + 2]
        m_scr, acc_scr = rest[2 * ng + 2:2 * ng + 4]
        comm = (g_ins, g_outs) + tuple(rest[2 * ng + 4:]) + (list(halved),)
        step = pl.program_id(1)
        qi = qi_ref[step]
        kj = kj_ref[step]
        if ng:
            @pl.when((pl.program_id(0) == 0) & (step == 0))
            def _():
                _allgather_start(*comm)

        @pl.when(kj == 0)
        def _():
            m_scr[...] = jnp.full_like(m_scr, NEG)
            acc_scr[...] = jnp.zeros_like(acc_scr)

        def update(qs, ks, masked):
            nq, nk = qs.stop - qs.start, ks.stop - ks.start
            for i in range(hps):
                own = slice(i * LANES, (i + 1) * LANES)
                sc = _nt(q_ref[qs, own], k_ref[ks, own])
                if masked:
                    sc = jnp.where(_row((nq, nk)) >= _lane((nq, nk)), sc, NEG)
                m_prev = m_scr[i, qs]
                m_new = jnp.maximum(m_prev, jnp.max(sc, axis=-1, keepdims=True))
                p = jnp.exp((sc - jnp.tile(m_new, (1, nk // LANES))).astype(BF16))
                acc_scr[i, qs] = jnp.exp(m_prev - m_new) * acc_scr[i, qs] + _nn(p, v_ref[ks, own])
                m_scr[i, qs] = m_new

        whole, upper, lower = slice(0, t), slice(0, t // 2), slice(t // 2, t)

        @pl.when(kj < qi)
        def _():
            update(whole, whole, False)

        @pl.when(kj == qi)
        def _():
            update(upper, upper, True)
            update(lower, upper, False)
            update(lower, lower, True)
            lane = _lane((t, LANES))
            for pr in range(hps // 2):
                den = [acc_scr[2 * pr + i][:, 64:65] for i in range(2)]
                o_ref[:, pr * LANES:(pr + 1) * LANES] = jnp.where(
                    lane < 64, acc_scr[2 * pr] / den[0], pltpu.roll(acc_scr[2 * pr + 1] / den[1], 64, axis=1)).astype(o_ref.dtype)
                lse_ref[:, pr * LANES:(pr + 1) * LANES] = jnp.where(
                    lane < 64, m_scr[2 * pr] + jnp.log(den[0]), m_scr[2 * pr + 1] + jnp.log(den[1]))

        if ng:
            @pl.when((pl.program_id(0) == n_groups - 1) & (step == n_pairs - 1))
            def _():
                _allgather_finish(*comm)

    wide = hps * LANES
    grid_spec = pltpu.PrefetchScalarGridSpec(
        num_scalar_prefetch=2, grid=(n_groups, n_pairs),
        in_specs=[pl.BlockSpec((t, wide), lambda hg, st, qi, kj: (qi[st], hg)),
                  pl.BlockSpec((t, wide), lambda hg, st, qi, kj: (kj[st], hg)),
                  pl.BlockSpec((t, wide), lambda hg, st, qi, kj: (kj[st], hg))] + [ANY] * ng,
        out_specs=[pl.BlockSpec((t, wide // 2), lambda hg, st, qi, kj: (qi[st], hg))] * 2 + [ANY] * ng,
        scratch_shapes=[pltpu.VMEM((hps, t, LANES), F32)] * 2 + (_allgather_sems(ng) if ng else []))
    return pl.pallas_call(
        body, grid_spec=grid_spec, out_shape=[SDS((s, ATT_W), BF16), SDS((s, ATT_W), F32)] + _allgather_shapes(gather),
        name="fox_fwd", compiler_params=_cp(("arbitrary", "arbitrary")))(qi_arr, kj_arr, q_aug, k_aug, v_aug, *gather)


def fox_bwd(q_aug, k_aug, z, dy, lse, dd, exchange=(), kind="to_chips", *, t=1024, hps=4):
    s = z.shape[0]
    qi_arr, kj_arr, n_pairs = _causal_pairs(s // t, True)
    ne = len(exchange)
    n_groups = N_HEADS // hps
    x_shapes, x_sems, x_start, x_finish = EXCHANGES[kind]

    def body(qi_ref, kj_ref, q_ref, k_ref, v_ref, do_ref, lse_ref, dd_ref, *rest):
        e_ins, (dq_ref, dk_ref, dv_ref), e_outs = rest[:ne], rest[ne:ne + 3], rest[ne + 3:2 * ne + 3]
        comm = (e_ins, e_outs) + tuple(rest[2 * ne + 3:])
        step = pl.program_id(1)
        qi = qi_ref[step]
        kj = kj_ref[step]
        if ne:
            @pl.when((pl.program_id(0) == 0) & (step == 0))
            def _():
                x_start(*comm)

        @pl.when(step == 0)
        def _():
            dq_ref[...] = jnp.zeros_like(dq_ref)

        @pl.when(qi == kj)
        def _():
            dk_ref[...] = jnp.zeros_like(dk_ref)
            dv_ref[...] = jnp.zeros_like(dv_ref)

        def update(qs, ks, masked):
            nq, nk = qs.stop - qs.start, ks.stop - ks.start
            lane = _lane((nq, LANES))
            rows = pl.ds(pl.multiple_of(qi * t + qs.start, nq), nq)
            for pr in range(hps // 2):
                pair = slice(pr * LANES, (pr + 1) * LANES)
                dov = do_ref[qs, pair]
                dv_new = None
                for i in range(2):
                    head = (lane < 64) if i == 0 else (lane >= 64)
                    own = slice((2 * pr + i) * LANES, (2 * pr + i + 1) * LANES)
                    col = slice(pr * LANES + i * 64, pr * LANES + i * 64 + 1)
                    qv = q_ref[qs, own]
                    kv = k_ref[ks, own]
                    sc = _nt(qv, kv)
                    if masked:
                        sc = jnp.where(_row((nq, nk)) >= _lane((nq, nk)), sc, NEG)
                    p = jnp.exp(sc - lse_ref[qs, col])
                    dp = _nt(jnp.where(head, dov, jnp.zeros_like(dov)), v_ref[ks, pair])
                    ds = (p * (dp - dd_ref[qs, col])).astype(BF16)
                    dq_ref[rows, own] += _nn(ds, kv)
                    dk_ref[ks, own] += _tn(ds, qv)
                    dvi = _tn(p.astype(BF16), dov)
                    dv_new = dvi if dv_new is None else jnp.where(head, dvi, dv_new)
                dv_ref[ks, pair] += dv_new

        whole, upper, lower = slice(0, t), slice(0, t // 2), slice(t // 2, t)

        @pl.when(kj < qi)
        def _():
            update(whole, whole, False)

        @pl.when(kj == qi)
        def _():
            update(upper, upper, True)
            update(lower, upper, False)
            update(lower, lower, True)

        if ne:
            @pl.when((pl.program_id(0) == n_groups - 1) & (step == n_pairs - 1))
            def _():
                x_finish(*comm)

    wide, half = hps * LANES, hps // 2 * LANES
    v_blk = Z_VA * ATT_W // half
    grid_spec = pltpu.PrefetchScalarGridSpec(
        num_scalar_prefetch=2, grid=(n_groups, n_pairs),
        in_specs=[pl.BlockSpec((t, wide), lambda hg, st, qi, kj: (qi[st], hg)),
                  pl.BlockSpec((t, wide), lambda hg, st, qi, kj: (kj[st], hg)),
                  pl.BlockSpec((t, half), lambda hg, st, qi, kj: (kj[st], v_blk + hg)),
                  pl.BlockSpec((t, half), lambda hg, st, qi, kj: (qi[st], hg)),
                  pl.BlockSpec((t, half), lambda hg, st, qi, kj: (qi[st], hg)),
                  pl.BlockSpec((t, half), lambda hg, st, qi, kj: (qi[st], hg))] + [ANY] * ne,
        out_specs=[pl.BlockSpec((s, wide), lambda hg, st, qi, kj: (0, hg)),
                   pl.BlockSpec((t, wide), lambda hg, st, qi, kj: (kj[st], hg)),
                   pl.BlockSpec((t, half), lambda hg, st, qi, kj: (kj[st], hg))] + [ANY] * ne,
        scratch_shapes=x_sems(ne) if ne else [])
    return pl.pallas_call(
        body, grid_spec=grid_spec,
        out_shape=[SDS((s, N_HEADS * LANES), F32), SDS((s, N_HEADS * LANES), F32), SDS((s, ATT_W), F32)]
        + x_shapes(exchange),
        name="fox_bwd", compiler_params=_cp(("arbitrary", "arbitrary")))(qi_arr, kj_arr, q_aug, k_aug, z, dy, lse, dd, *exchange)


def fox_post(dq_aug, dk_aug, dv, fa, bfo, *, tb=512):
    s = dv.shape[0]
    n = s // tb

    def body(dq_ref, dk_ref, dv_ref, fa_ref, b_ref, tri_ref, dz_ref, dfa_ref, gb_ref, carry, acc):
        i = pl.program_id(0)

        @pl.when(i == 0)
        def _():
            carry[...] = jnp.zeros_like(carry)
            acc[...] = jnp.zeros_like(acc)

        lane = _lane((tb, LANES))
        d_f = jnp.zeros((tb, LANES), F32)
        for h in range(N_HEADS):
            col = dq_ref[:, h * LANES + 64:h * LANES + 65] - dk_ref[:, h * LANES + 67:h * LANES + 68]
            d_f = jnp.where(lane == h, col, d_f)
        suffix = jnp.dot(tri_ref[...], d_f, preferred_element_type=F32, precision=lax.Precision.HIGHEST) + carry[0:1, :]
        carry[0:1, :] = suffix[0:1, :]
        xv = fa_ref[...] + b_ref[...]
        dx = suffix * (1.0 / (1.0 + jnp.exp(xv)))
        dfa_ref[...] = dx.astype(dfa_ref.dtype)
        acc[...] += jnp.sum(dx.reshape(tb // 8, 8, LANES), axis=0)
        for hp in range(4):
            for src, off, scale in ((dq_ref, 0, QK_SCALE), (dk_ref, ATT_W, 1.0)):
                even = src[:, (2 * hp) * LANES:(2 * hp + 1) * LANES]
                odd = pltpu.roll(src[:, (2 * hp + 1) * LANES:(2 * hp + 2) * LANES], 64, axis=1)
                dz_ref[:, off + hp * LANES:off + (hp + 1) * LANES] = (jnp.where(lane < 64, even, odd) * scale).astype(BF16)
        dz_ref[:, 2 * ATT_W:3 * ATT_W] = dv_ref[...].astype(BF16)

        @pl.when(i == n - 1)
        def _():
            gb_ref[...] = jnp.sum(acc[...], axis=0, keepdims=True)

    rev = lambda i: (n - 1 - i, 0)
    return pl.pallas_call(
        body, grid=(n,),
        in_specs=[pl.BlockSpec((tb, N_HEADS * LANES), rev), pl.BlockSpec((tb, N_HEADS * LANES), rev),
                  pl.BlockSpec((tb, ATT_W), rev), pl.BlockSpec((tb, LANES), rev),
                  pl.BlockSpec((1, LANES), lambda i: (0, 0)), pl.BlockSpec((tb, tb), lambda i: (0, 0))],
        out_specs=[pl.BlockSpec((tb, 3 * ATT_W), rev), pl.BlockSpec((tb, LANES), rev),
                   pl.BlockSpec((1, LANES), lambda i: (0, 0))],
        out_shape=[SDS((s, 3 * ATT_W), BF16), SDS((s, LANES), BF16), SDS((1, LANES), F32)],
        scratch_shapes=[pltpu.VMEM((8, LANES), F32), pltpu.VMEM((8, LANES), F32)],
        name="fox_post", compiler_params=_cp(("arbitrary",)))(dq_aug, dk_aug, dv, fa, bfo, _tri(tb, True))


def rope_cos_sin(s):
    half = ROPE_DIM // 2
    inv_freq = ROPE_THETA ** (-jnp.arange(half, dtype=F32) * 2.0 / ROPE_DIM)
    ang = jnp.arange(s, dtype=F32)[:, None] * inv_freq[None, :]
    return jnp.tile(jnp.cos(ang), (1, LANES // half)), jnp.tile(jnp.sin(ang), (1, LANES // half))


def _rotate(x, cos, sin, sign):
    l64 = _lane(x.shape) & (HEAD_DIM - 1)
    first = l64 < ROPE_DIM // 2
    second = (l64 >= ROPE_DIM // 2) & (l64 < ROPE_DIM)
    from_next = jnp.where(first, -sign * sin, 0.0)
    from_prev = jnp.where(second, sign * sin, 0.0)
    return (x * jnp.where(first | second, cos, 1.0) + pltpu.roll(x, LANES - 8, axis=1) * from_next
            + pltpu.roll(x, 8, axis=1) * from_prev)


def _dil_rows(base, r):
    if r == 1:
        return pl.ds(pl.multiple_of(base, DIL_BLK), DIL_BLK)
    return pl.ds(base, DIL_BLK, stride=r)


def _dil_block(idx, r, nb):
    shift = nb.bit_length() - 1
    rho = idx >> shift
    n = idx & (nb - 1)
    base = rho + n * (r * DIL_BLK)
    return _dil_rows(base, r), _dil_rows(jnp.maximum(base - r * DIL_BLK, rho), r), n > 0


def _cat(a, b):
    return jnp.concatenate([a, b], axis=0)


def _two_heads(v, first_head):
    zero = jnp.zeros_like(v)
    return _cat(jnp.where(first_head, v, zero), jnp.where(first_head, zero, v))


def _dil_bands():
    b = DIL_BLK
    q = _row((2 * b, 2 * b)) & (b - 1)
    col = _lane((2 * b, 2 * b))
    return (col < b) & (col >= q), (col >= b) & (col - b <= q)


def _dil_load_qkv(zq_ref, zk_ref, zv_ref, cos_ref, sin_ref, q_ref, k_ref, v_ref, *, chunk=512):
    def step(i, carry):
        rows = pl.ds(pl.multiple_of(i * chunk, chunk), chunk)
        cos, sin = cos_ref[rows, :], sin_ref[rows, :]
        q_ref[rows, :] = _rotate(zq_ref[rows, :].astype(F32), cos, sin, 1.0) * QK_SCALE
        k_ref[rows, :] = _rotate(zk_ref[rows, :].astype(F32), cos, sin, 1.0)
        v_ref[rows, :] = zv_ref[rows, :].astype(F32)
        return carry

    lax.fori_loop(0, q_ref.shape[0] // chunk, step, 0)


def dil_fwd_all(z, cos_t, sin_t, *, unroll=32):
    s = z.shape[0]
    b = DIL_BLK
    n_blk = s // b

    def body(zq_ref, zk_ref, zv_ref, cos_ref, sin_ref, o_ref, l_ref, q_ref, k_ref, v_ref):
        _dil_load_qkv(zq_ref, zk_ref, zv_ref, cos_ref, sin_ref, q_ref, k_ref, v_ref)
        first_head = _lane((b, LANES)) < 64
        band_prev, band_cur = _dil_bands()
        for g, (_, r) in enumerate(DIL_PATTERNS):
            nb = n_blk // r

            def group(it, carry, g=g, r=r, nb=nb):
                loaded = []
                kc = vc = None
                for u in range(unroll):
                    rows_c, rows_p, has_prev = _dil_block(it * unroll + u, r, nb)
                    if u % min(nb, unroll):
                        kp, vp = kc, vc
                    else:
                        kp, vp = k_ref[rows_p, :].astype(BF16), v_ref[rows_p, :].astype(BF16)
                    kc, vc = k_ref[rows_c, :].astype(BF16), v_ref[rows_c, :].astype(BF16)
                    state = (o_ref[rows_c, :], l_ref[rows_c, :]) if g else None
                    loaded.append((rows_c, has_prev, [q_ref[rows_c, :].astype(BF16), kp, kc, vp, vc], state))
                done = []
                for rows_c, has_prev, (qv, kp, kc, vp, vc), state in loaded:
                    sc = jnp.where(band_cur | (band_prev & has_prev), _nt(_two_heads(qv, first_head), _cat(kp, kc)), NEG)
                    m = jnp.max(sc, axis=-1, keepdims=True)
                    p = jnp.exp(sc - m)
                    den = jnp.sum(p, axis=-1, keepdims=True)
                    both = _nn(p.astype(BF16), _cat(vp, vc)) / den
                    lse2 = m + jnp.log(den)
                    ov = jnp.where(first_head, both[:b], both[b:])
                    lse = jnp.where(first_head, lse2[:b], lse2[b:])
                    if state is not None:
                        m2 = jnp.maximum(state[1], lse)
                        wp = jnp.exp(state[1] - m2)
                        wn = jnp.exp(lse - m2)
                        ov = (wp * state[0] + wn * ov) / (wp + wn)
                        lse = m2 + jnp.log(wp + wn)
                    done.append((rows_c, ov, lse))
                for rows_c, ov, lse in done:
                    o_ref[rows_c, :] = ov
                    l_ref[rows_c, :] = lse
                return carry

            lax.fori_loop(0, n_blk // unroll, group, 0)

    col_blk = lambda k: pl.BlockSpec((s, LANES), lambda hp: (0, 4 * k + hp))
    table = pl.BlockSpec((s, LANES), lambda hp: (0, 0))
    out = pl.BlockSpec((s, LANES), lambda hp: (0, hp))
    return pl.pallas_call(
        body, grid=(4,), in_specs=[col_blk(Z_QB), col_blk(Z_KB), col_blk(Z_VB), table, table], out_specs=[out, out],
        out_shape=[SDS((s, ATT_W), F32)] * 2, scratch_shapes=[pltpu.VMEM((s, LANES), F32)] * 3, name="dil_fwd",
        compiler_params=_cp(("parallel",)))(z, z, z, cos_t, sin_t)


def dil_bwd_all(z, cos_t, sin_t, dy, lse, y, exchange=(), kind="to_chips", *, unroll=16):
    s = z.shape[0]
    b = DIL_BLK
    n_blk = s // b
    ne = len(exchange)
    x_shapes, x_sems, x_start, x_finish = EXCHANGES[kind]

    def body(zq_ref, zk_ref, zv_ref, cos_ref, sin_ref, do_ref, l_ref, y_ref, *rest):
        e_ins, (gq_ref, gk_ref, gv_ref), e_outs = rest[:ne], rest[ne:ne + 3], rest[ne + 3:2 * ne + 3]
        q_ref, k_ref, v_ref, dq_ref, dk_ref, dv_ref = rest[2 * ne + 3:2 * ne + 9]
        comm = (e_ins, e_outs) + tuple(rest[2 * ne + 9:])
        if ne:
            @pl.when(pl.program_id(0) == 0)
            def _():
                x_start(*comm)

        _dil_load_qkv(zq_ref, zk_ref, zv_ref, cos_ref, sin_ref, q_ref, k_ref, v_ref)
        dq_ref[...] = jnp.zeros_like(dq_ref)
        dk_ref[...] = jnp.zeros_like(dk_ref)
        dv_ref[...] = jnp.zeros_like(dv_ref)
        first_head = _lane((b, LANES)) < 64
        band_prev, band_cur = _dil_bands()
        for _, r in DIL_PATTERNS:
            nb = n_blk // r

            def group(it, carry, r=r, nb=nb):
                loaded = []
                kc = vc = None
                for u in range(unroll):
                    rows_c, rows_p, has_prev = _dil_block(it * unroll + u, r, nb)
                    if u % min(nb, unroll):
                        kp, vp = kc, vc
                    else:
                        kp, vp = k_ref[rows_p, :].astype(BF16), v_ref[rows_p, :].astype(BF16)
                    kc, vc = k_ref[rows_c, :].astype(BF16), v_ref[rows_c, :].astype(BF16)
                    vals = [q_ref[rows_c, :].astype(BF16), kp, kc, vp, vc, do_ref[rows_c, :], l_ref[rows_c, :], y_ref[rows_c, :]]
                    loaded.append((rows_c, rows_p, has_prev, vals))
                done = []
                for rows_c, rows_p, has_prev, (qv, kp, kc, vp, vc, dof, lv, yv) in loaded:
                    q2 = _two_heads(qv, first_head)
                    do2 = _two_heads(dof.astype(BF16), first_head)
                    kcat, vcat = _cat(kp, kc), _cat(vp, vc)
                    lse2 = _cat(lv[:, 0:1], lv[:, 64:65])
                    dd2 = jnp.sum(_two_heads(dof * yv, first_head), axis=-1, keepdims=True)
                    p = jnp.exp(jnp.where(band_cur | (band_prev & has_prev), _nt(q2, kcat), NEG) - lse2)
                    ds = (p * (_nt(do2, vcat) - dd2)).astype(BF16)
                    dq2 = _nn(ds, kcat)
                    dkcat = _tn(ds, q2)
                    dvcat = _tn(p.astype(BF16), do2)
                    done.append((rows_c, rows_p, (jnp.where(first_head, dq2[:b], dq2[b:]), dkcat[:b], dkcat[b:],
                                                  dvcat[:b], dvcat[b:])))
                held = None
                for u, (rows_c, rows_p, (dq, dk_p, dk_c, dv_p, dv_c)) in enumerate(done):
                    dq_ref[rows_c, :] += dq
                    if u % min(nb, unroll):
                        rows_h, dk_h, dv_h = held
                        dk_ref[rows_h, :] += dk_h + dk_p
                        dv_ref[rows_h, :] += dv_h + dv_p
                    else:
                        if held is not None:
                            dk_ref[held[0], :] += held[1]
                            dv_ref[held[0], :] += held[2]
                        dk_ref[rows_p, :] += dk_p
                        dv_ref[rows_p, :] += dv_p
                    held = (rows_c, dk_c, dv_c)
                dk_ref[held[0], :] += held[1]
                dv_ref[held[0], :] += held[2]
                return carry

            lax.fori_loop(0, n_blk // unroll, group, 0)

        def finish(i, carry, chunk=512):
            rows = pl.ds(pl.multiple_of(i * chunk, chunk), chunk)
            cos, sin = cos_ref[rows, :], sin_ref[rows, :]
            gq_ref[rows, :] = (_rotate(dq_ref[rows, :], cos, sin, -1.0) * QK_SCALE).astype(BF16)
            gk_ref[rows, :] = _rotate(dk_ref[rows, :], cos, sin, -1.0).astype(BF16)
            gv_ref[rows, :] = dv_ref[rows, :].astype(BF16)
            return carry

        lax.fori_loop(0, s // 512, finish, 0)
        if ne:
            @pl.when(pl.program_id(0) == 3)
            def _():
                x_finish(*comm)

    col_blk = lambda k: pl.BlockSpec((s, LANES), lambda hp: (0, 4 * k + hp))
    table = pl.BlockSpec((s, LANES), lambda hp: (0, 0))
    nat = pl.BlockSpec((s, LANES), lambda hp: (0, hp))
    return pl.pallas_call(
        body, grid=(4,), in_specs=[col_blk(Z_QB), col_blk(Z_KB), col_blk(Z_VB), table, table, nat, nat, nat] + [ANY] * ne,
        out_specs=[nat, nat, nat] + [ANY] * ne, out_shape=[SDS((s, ATT_W), BF16)] * 3 + x_shapes(exchange),
        scratch_shapes=[pltpu.VMEM((s, LANES), F32)] * 6 + (x_sems(ne) if ne else []), name="dil_bwd",
        compiler_params=_cp(("arbitrary",)))(z, z, z, cos_t, sin_t, dy, lse, y, *exchange)


def _sigmoid(v):
    return 1.0 / (1.0 + jnp.exp(-v))


def gate_mix(ya, yb, wa, wb, z, *, tm=2048, tn=512):
    s = ya.shape[0]
    d = wa.shape[1]
    ga_blk = 3 * ATT_W * 2 // tn
    gb_blk = ga_blk + d // tn

    def body(ya_ref, yb_ref, wa_ref, wb_ref, ga_ref, gb_ref, pa_ref, pb_ref, mx_ref):
        pa = _nn(ya_ref[...], wa_ref[...])
        pb = _nn(yb_ref[...].astype(BF16), wb_ref[...])
        pa_ref[...] = pa.astype(BF16)
        pb_ref[...] = pb.astype(BF16)
        mx_ref[...] = (_sigmoid(ga_ref[...].astype(F32)) * pa + _sigmoid(gb_ref[...].astype(F32)) * pb).astype(BF16)

    out = pl.BlockSpec((tm, tn), lambda i, j: (i, j))
    return pl.pallas_call(
        body, grid=(s // tm, d // tn),
        in_specs=[pl.BlockSpec((tm, ATT_W), lambda i, j: (i, 0)), pl.BlockSpec((tm, ATT_W), lambda i, j: (i, 0)),
                  pl.BlockSpec((ATT_W, tn), lambda i, j: (0, j)), pl.BlockSpec((ATT_W, tn), lambda i, j: (0, j)),
                  pl.BlockSpec((tm, tn), lambda i, j: (i, ga_blk + j)), pl.BlockSpec((tm, tn), lambda i, j: (i, gb_blk + j))],
        out_specs=[out, out, out], out_shape=[SDS((s, d), BF16)] * 3, name="gate_mix",
        compiler_params=_cp(("parallel", "parallel")))(ya, yb, wa, wb, z, z)


def mix_bwd(dy, w_o, z, pa, pb, wo_a, wo_b, ya, *, tm=512):
    s, d = dy.shape

    def body(dy_ref, wo_ref, ga_ref, gb_ref, pa_ref, pb_ref, wa_ref, wb_ref, ya_ref,
             dpa_ref, dpb_ref, dg_ref, dya_ref, dyb_ref, dd_ref):
        dm = _nt(dy_ref[...], wo_ref[...])
        sa = _sigmoid(ga_ref[...].astype(F32))
        sb = _sigmoid(gb_ref[...].astype(F32))
        dpa = (dm * sa).astype(BF16)
        dpb = (dm * sb).astype(BF16)
        dpa_ref[...] = dpa
        dpb_ref[...] = dpb
        dg_ref[:, 0:d] = (dm * pa_ref[...].astype(F32) * sa * (1.0 - sa)).astype(BF16)
        dg_ref[:, d:2 * d] = (dm * pb_ref[...].astype(F32) * sb * (1.0 - sb)).astype(BF16)
        dya = _nt(dpa, wa_ref[...]).astype(BF16)
        dya_ref[...] = dya
        dyb_ref[...] = _nt(dpb, wb_ref[...])
        lane = _lane((tm, LANES))
        for pr in range(ATT_W // LANES):
            pair = slice(pr * LANES, (pr + 1) * LANES)
            prod = dya[:, pair].astype(F32) * ya_ref[:, pair].astype(F32)
            lo = jnp.sum(jnp.where(lane < 64, prod, 0.0), axis=-1, keepdims=True)
            hi = jnp.sum(jnp.where(lane >= 64, prod, 0.0), axis=-1, keepdims=True)
            dd_ref[:, pair] = jnp.where(lane < 64, lo, hi)

    row = pl.BlockSpec((tm, d), lambda i: (i, 0))
    att = pl.BlockSpec((tm, ATT_W), lambda i: (i, 0))
    whole = lambda a: pl.BlockSpec(a.shape, lambda i: (0, 0))
    return pl.pallas_call(
        body, grid=(s // tm,),
        in_specs=[row, whole(w_o), pl.BlockSpec((tm, d), lambda i: (i, 3)), pl.BlockSpec((tm, d), lambda i: (i, 4)), row, row,
                  whole(wo_a), whole(wo_b), att],
        out_specs=[row, row, pl.BlockSpec((tm, 2 * d), lambda i: (i, 0)), att, att, att],
        out_shape=[SDS((s, d), BF16), SDS((s, d), BF16), SDS((s, 2 * d), BF16), SDS((s, ATT_W), BF16),
                   SDS((s, ATT_W), F32), SDS((s, ATT_W), F32)], name="mix_bwd",
        compiler_params=_cp(("parallel",)))(dy, w_o, z, z, pa, pb, wo_a, wo_b, ya)


GELU_C = math.sqrt(2.0 / math.pi)


def _gelu_parts(a):
    a2 = a * a
    th = jnp.tanh(a * (GELU_C + (GELU_C * 0.044715) * a2))
    half = 0.5 * a
    gelu = half + half * th
    dgelu = (0.5 + 0.5 * th) + half * (1.0 - th * th) * (GELU_C + (3.0 * GELU_C * 0.044715) * a2)
    return gelu, dgelu


def _causal_taps(u, before):
    row = _row(u.shape)
    r1 = jnp.where(row == 0, before[7:8, :], pltpu.roll(u, 1, axis=0))
    r2 = jnp.where(row == 0, before[6:7, :], jnp.where(row == 1, before[7:8, :], pltpu.roll(u, 2, axis=0)))
    return r1, r2


def ffn_up(h, wa, wb, cw, cb, *, tm=2048, tn=256):
    s, d = h.shape
    f = wa.shape[1]
    nj = f // tn

    def body(h_ref, wa_ref, wb_ref, cwa_ref, cwb_ref, cba_ref, cbb_ref, ua_ref, ub_ref, ca_ref, cbo_ref, m_ref, carry):
        @pl.when(pl.program_id(1) == 0)
        def _():
            carry[...] = jnp.zeros_like(carry)

        conv = []
        for k, (w_ref, cw_ref, cb_ref, u_ref, c_ref) in enumerate(((wa_ref, cwa_ref, cba_ref, ua_ref, ca_ref),
                                                                   (wb_ref, cwb_ref, cbb_ref, ub_ref, cbo_ref))):
            u16 = _nn(h_ref[...], w_ref[...]).astype(BF16)
            u_ref[...] = u16
            u = u16.astype(F32)
            r1, r2 = _causal_taps(u, carry[k])
            carry[k] = u[tm - 8:tm, :]
            c16 = (cw_ref[0:1, :] * r2 + cw_ref[1:2, :] * r1 + cw_ref[2:3, :] * u + cb_ref[...]).astype(BF16)
            c_ref[...] = c16
            conv.append(c16.astype(F32))
        m_ref[...] = (_gelu_parts(conv[0])[0] * conv[1]).astype(BF16)

    out = pl.BlockSpec((tm, tn), lambda j, i: (i, j))
    return pl.pallas_call(
        body, grid=(nj, s // tm),
        in_specs=[pl.BlockSpec((tm, d), lambda j, i: (i, 0)),
                  pl.BlockSpec((d, tn), lambda j, i: (0, j)), pl.BlockSpec((d, tn), lambda j, i: (0, j)),
                  pl.BlockSpec((3, tn), lambda j, i: (0, j)), pl.BlockSpec((3, tn), lambda j, i: (0, nj + j)),
                  pl.BlockSpec((1, tn), lambda j, i: (0, j)), pl.BlockSpec((1, tn), lambda j, i: (0, nj + j))],
        out_specs=[out] * 5, out_shape=[SDS((s, f), BF16)] * 5,
        scratch_shapes=[pltpu.VMEM((2, 8, tn), F32)], name="ffn_up",
        compiler_params=_cp(("parallel", "arbitrary")))(h, wa, wb, cw, cw, cb, cb)


def ffn_bwd(dm, ua, ub, ca, cbo, cw, *, tm=2048, tn=256):
    s, f = dm.shape
    nj = f // tn
    ni = s // tm

    def body(dm_ref, ua_ref, ub_ref, ca_ref, cbo_ref, cwa_ref, cwb_ref, dua_ref, dub_ref, ga_ref, gb_ref, carry):
        @pl.when(pl.program_id(1) == 0)
        def _():
            carry[...] = jnp.zeros_like(carry)
            ga_ref[...] = jnp.zeros_like(ga_ref)
            gb_ref[...] = jnp.zeros_like(gb_ref)

        row = _row((tm, tn))
        dmv = dm_ref[...].astype(F32)
        gelu, dgelu = _gelu_parts(ca_ref[...].astype(F32))
        dcs = (dmv * cbo_ref[...].astype(F32) * dgelu, dmv * gelu)
        for k, (dc, u_ref, cw_ref, du_ref, g_ref) in enumerate(((dcs[0], ua_ref, cwa_ref, dua_ref, ga_ref),
                                                                (dcs[1], ub_ref, cwb_ref, dub_ref, gb_ref))):
            u = u_ref[...].astype(F32)
            after = carry[k]
            n1 = jnp.where(row == tm - 1, after[0:1, :], pltpu.roll(dc, tm - 1, axis=0))
            n2 = jnp.where(row == tm - 2, after[0:1, :], jnp.where(row == tm - 1, after[1:2, :], pltpu.roll(dc, tm - 2, axis=0)))
            g_ref[0:1, :] += jnp.sum(n2 * u, axis=0, keepdims=True)
            g_ref[1:2, :] += jnp.sum(n1 * u, axis=0, keepdims=True)
            g_ref[2:3, :] += jnp.sum(dc * u, axis=0, keepdims=True)
            g_ref[3:4, :] += jnp.sum(dc, axis=0, keepdims=True)
            du_ref[...] = (cw_ref[2:3, :] * dc + cw_ref[1:2, :] * n1 + cw_ref[0:1, :] * n2).astype(BF16)
            carry[k] = dc[0:8, :]

    tile = pl.BlockSpec((tm, tn), lambda j, i: (ni - 1 - i, j))
    gspec = pl.BlockSpec((8, tn), lambda j, i: (0, j))
    return pl.pallas_call(
        body, grid=(nj, ni),
        in_specs=[tile] * 5 + [pl.BlockSpec((3, tn), lambda j, i: (0, j)), pl.BlockSpec((3, tn), lambda j, i: (0, nj + j))],
        out_specs=[tile, tile, gspec, gspec],
        out_shape=[SDS((s, f), BF16), SDS((s, f), BF16), SDS((8, f), F32), SDS((8, f), F32)],
        scratch_shapes=[pltpu.VMEM((2, 8, tn), F32)], name="ffn_bwd",
        compiler_params=_cp(("parallel", "arbitrary")))(dm, ua, ub, ca, cbo, cw, cw)


def adamw(w, g, m, v, *, name, tr=None):
    r = w.shape[0]
    rest = w.shape[1:]
    if tr is None:
        tr = r
        for cand in (256, 128, 64, 32, 16, 8):
            if r % cand == 0:
                tr = cand
                break

    def body(w_ref, g_ref, m_ref, v_ref, d_ref, nm_ref, nv_ref):
        gv = g_ref[...]
        mn = ADAM_B1 * m_ref[...] + (1.0 - ADAM_B1) * gv
        vn = ADAM_B2 * v_ref[...] + (1.0 - ADAM_B2) * (gv * gv)
        m_hat = mn / (1.0 - ADAM_B1 ** ADAM_STEP)
        v_hat = vn / (1.0 - ADAM_B2 ** ADAM_STEP)
        d_ref[...] = -ADAM_LR * (m_hat / (jnp.sqrt(v_hat) + ADAM_EPS) + ADAM_WD * w_ref[...])
        nm_ref[...] = mn
        nv_ref[...] = vn

    blk = pl.BlockSpec((tr,) + rest, lambda i: (i,) + (0,) * len(rest))
    return pl.pallas_call(body, grid=(r // tr,), in_specs=[blk] * 4, out_specs=[blk] * 3, out_shape=[SDS(w.shape, F32)] * 3,
                          name=name, compiler_params=_cp(("parallel",)))(w, g, m, v)


def adamw_rows_view(w, g_mine, g_full, m, v, c_arr, *, name, tc=256):
    r, _, c = w.shape
    per_half = c // 2 // tc

    def body(c_ref, w_ref, gm_ref, gf_ref, m_ref, v_ref, d_ref, nm_ref, nv_ref, go_ref):
        mine = (pl.program_id(0) >> (per_half.bit_length() - 1)) == c_ref[0]
        gv = jnp.where(mine, gm_ref[...], gf_ref[...])
        mn = ADAM_B1 * m_ref[:, 0, :] + (1.0 - ADAM_B1) * gv
        vn = ADAM_B2 * v_ref[:, 0, :] + (1.0 - ADAM_B2) * (gv * gv)
        m_hat = mn / (1.0 - ADAM_B1 ** ADAM_STEP)
        v_hat = vn / (1.0 - ADAM_B2 ** ADAM_STEP)
        d_ref[:, 0, :] = -ADAM_LR * (m_hat / (jnp.sqrt(v_hat) + ADAM_EPS) + ADAM_WD * w_ref[:, 0, :])
        nm_ref[:, 0, :] = mn
        nv_ref[:, 0, :] = vn
        go_ref[:, 0, :] = gv

    b3 = pl.BlockSpec((r, 1, tc), lambda i, c_ref: (0, 0, i))
    own = pl.BlockSpec((r, tc), lambda i, c_ref: (0, jnp.clip(i - c_ref[0] * per_half, 0, per_half - 1)))
    full = pl.BlockSpec((r, tc), lambda i, c_ref: (0, i))
    grid_spec = pltpu.PrefetchScalarGridSpec(num_scalar_prefetch=1, grid=(c // tc,), in_specs=[b3, own, full, b3, b3],
                                             out_specs=[b3] * 4)
    return pl.pallas_call(body, grid_spec=grid_spec, out_shape=[SDS(w.shape, F32)] * 4, name=name,
                          compiler_params=_cp(("parallel",)))(c_arr, w, g_mine, g_full, m, v)


ANY = pl.BlockSpec(memory_space=pl.ANY)
ICI_KINDS = ("x", "y", "xy")


def _coords():
    return lax.axis_index("x"), lax.axis_index("y"), lax.axis_index("c")


def _peer(kind, x, y, c):
    if kind == "c":
        return (x, y, 1 - c)
    if kind == "x":
        return (1 - x, y, c)
    if kind == "y":
        return (x, 1 - y, c)
    return (1 - x, 1 - y, c)


def _chip_of(p):
    return 2 * p[0] + p[1]


def _half(rows, which):
    h = rows // 2
    return pl.ds(pl.multiple_of(which * h, 16), h)


def _remote(src, dst, send_sem, recv_sem, to):
    return pltpu.make_async_remote_copy(src_ref=src, dst_ref=dst, send_sem=send_sem, recv_sem=recv_sem,
                                        device_id=to, device_id_type=MESH)


def allgather_balanced(shard, *, name):
    r, cols = shard.shape
    h, q = r // 2, r // 4

    def body(in_ref, out_ref, send_sems, recv_sems):
        x, y, c = _coords()
        me, sibling = (x, y, c), (x, y, 1 - c)
        nbr = ((1 - x, y, c), (x, 1 - y, c))
        chip = (2 * (1 - x) + y, 2 * x + (1 - y), 2 * (1 - x) + (1 - y))
        quarter = lambda core, i: pl.ds(pl.multiple_of(core * h + i * q, 16), q)
        sent = []

        def go(src, dst, slot, to):
            cp = _remote(src, dst, send_sems.at[slot], recv_sems.at[slot], to)
            cp.start()
            sent.append(cp)

        def landed(region, slot):
            _remote(region, region, send_sems.at[slot], recv_sems.at[slot], me).wait_recv()

        for i in range(2):
            for k in range(2):
                qi = k if i == 0 else 1 - k
                go(in_ref.at[quarter(c, qi)], out_ref.at[2 * x + y, quarter(c, qi)], 2 * k + qi, nbr[k])
        for k in range(2):
            piece = out_ref.at[chip[k], quarter(c, k)]
            landed(piece, 2 * k + k)
            go(piece, piece, 4 + k, nbr[1 - k])
            go(piece, piece, 6 + 2 * k + k, sibling)
        for k in range(2):
            piece = out_ref.at[chip[k], quarter(c, 1 - k)]
            landed(piece, 2 * k + 1 - k)
            go(piece, piece, 6 + 2 * k + 1 - k, sibling)
        for k in range(2):
            piece = out_ref.at[chip[2], quarter(c, k)]
            landed(piece, 4 + k)
            go(piece, piece, 10 + k, sibling)
        for k in range(2):
            for i in range(2):
                landed(out_ref.at[chip[k], quarter(1 - c, i)], 6 + 2 * k + i)
            landed(out_ref.at[chip[2], quarter(1 - c, k)], 10 + k)
        for cp in sent:
            cp.wait_send()

    return pl.pallas_call(
        body, in_specs=[ANY], out_specs=ANY, out_shape=SDS((4,) + shard.shape, shard.dtype),
        scratch_shapes=[pltpu.SemaphoreType.DMA((12,)), pltpu.SemaphoreType.DMA((12,))], name=name)(shard)


def _allgather_shapes(shards):
    return [SDS((4,) + a.shape, a.dtype) for a in shards]


def _allgather_sems(n):
    return [pltpu.SemaphoreType.DMA((n, 6)), pltpu.SemaphoreType.DMA((n, 6))]


def _allgather_rows(ref, is_halved, which):
    r = ref.shape[0]
    return _half(r, which) if is_halved else pl.ds(0, r)


def _allgather_first(ins, outs, send_sems, recv_sems, halved):
    x, y, c = _coords()
    my_chip = 2 * x + y
    cps = []
    for w in range(len(ins)):
        rows = _allgather_rows(ins[w], halved[w], c)
        for k, kind in enumerate(ICI_KINDS):
            cps.append(_remote(ins[w].at[rows], outs[w].at[my_chip, rows], send_sems.at[w, k], recv_sems.at[w, k],
                               _peer(kind, x, y, c)))
    return cps


def _allgather_start(ins, outs, send_sems, recv_sems, halved):
    for cp in _allgather_first(ins, outs, send_sems, recv_sems, halved):
        cp.start()


def _allgather_finish(ins, outs, send_sems, recv_sems, halved):
    x, y, c = _coords()
    me = (x, y, c)
    second = []
    for w in range(len(ins)):
        for k, kind in enumerate(ICI_KINDS):
            landed = outs[w].at[_chip_of(_peer(kind, x, y, c)), _allgather_rows(ins[w], halved[w], c)]
            _remote(landed, landed, send_sems.at[w, k], recv_sems.at[w, k], me).wait_recv()
            if halved[w]:
                cp = _remote(landed, landed, send_sems.at[w, 3 + k], recv_sems.at[w, 3 + k], _peer("c", x, y, c))
                cp.start()
                second.append(cp)
    for w in range(len(ins)):
        if halved[w]:
            for k, kind in enumerate(ICI_KINDS):
                other = outs[w].at[_chip_of(_peer(kind, x, y, c)), _allgather_rows(ins[w], True, 1 - c)]
                _remote(other, other, send_sems.at[w, 3 + k], recv_sems.at[w, 3 + k], me).wait_recv()
    for cp in _allgather_first(ins, outs, send_sems, recv_sems, halved) + second:
        cp.wait_send()


def _half_of(ref, by_cols, which):
    lead = (slice(None),) * (len(ref.shape) - 2)
    if by_cols:
        h = ref.shape[-1] // 2
        return ref.at[lead + (slice(None), pl.ds(pl.multiple_of(which * h, LANES), h))]
    return ref.at[lead + (_half(ref.shape[-2], which),)]


def _half_shape(shape, by_cols):
    return shape[:-1] + (shape[-1] // 2,) if by_cols else shape[:-2] + (shape[-2] // 2, shape[-1])


def grads_to_sibling(gs, by_cols, *, name):
    n = len(gs)

    def body(*refs):
        ins, outs = refs[:n], refs[n:2 * n]
        send_sems, recv_sems = refs[2 * n:]
        x, y, c = _coords()
        cps = []
        for w in range(n):
            cp = _remote(_half_of(ins[w], by_cols[w], 1 - c), outs[w], send_sems.at[w], recv_sems.at[w], _peer("c", x, y, c))
            cp.start()
            cps.append(cp)
        for cp in cps:
            cp.wait()

    return pl.pallas_call(
        body, in_specs=[ANY] * n, out_specs=[ANY] * n,
        out_shape=[SDS(_half_shape(a.shape, bc), a.dtype) for a, bc in zip(gs, by_cols)],
        scratch_shapes=[pltpu.SemaphoreType.DMA((n,)), pltpu.SemaphoreType.DMA((n,))], name=name)(*gs)


def _to_chips_shapes(ps):
    return [SDS((3,) + a.shape[1:], a.dtype) for a in ps]


def _to_chips_sems(n):
    return [pltpu.SemaphoreType.DMA((n, 3)), pltpu.SemaphoreType.DMA((n, 3))]


def _to_chips_copies(ins, outs, send_sems, recv_sems):
    x, y, c = _coords()
    cps = []
    for w in range(len(ins)):
        for k, kind in enumerate(ICI_KINDS):
            to = _peer(kind, x, y, c)
            cps.append(_remote(ins[w].at[_chip_of(to)], outs[w].at[k], send_sems.at[w, k], recv_sems.at[w, k], to))
    return cps


def _to_chips_start(ins, outs, send_sems, recv_sems):
    for cp in _to_chips_copies(ins, outs, send_sems, recv_sems):
        cp.start()


def _to_chips_finish(ins, outs, send_sems, recv_sems):
    for cp in _to_chips_copies(ins, outs, send_sems, recv_sems):
        cp.wait()


def _to_owners_shapes(ps):
    return [SDS((7, a.shape[1] // 2, a.shape[2]), a.dtype) for a in ps]


def _to_owners_sems(n):
    return [pltpu.SemaphoreType.DMA((n, 7)), pltpu.SemaphoreType.DMA((n, 7))]


def _to_owners_copies(ins, outs, send_sems, recv_sems):
    x, y, c = _coords()
    cps = []
    for w in range(len(ins)):
        rows = ins[w].shape[1]
        for k, kind in enumerate(ICI_KINDS):
            px, py, _ = _peer(kind, x, y, c)
            for h in range(2):
                cps.append(_remote(ins[w].at[2 * px + py, _half(rows, h)], outs[w].at[2 * k + c],
                                   send_sems.at[w, 2 * k + h], recv_sems.at[w, 2 * k + c], (px, py, h)))
        cps.append(_remote(ins[w].at[2 * x + y, _half(rows, 1 - c)], outs[w].at[6], send_sems.at[w, 6], recv_sems.at[w, 6],
                           _peer("c", x, y, c)))
    return cps


def _to_owners_start(ins, outs, send_sems, recv_sems):
    for cp in _to_owners_copies(ins, outs, send_sems, recv_sems):
        cp.start()


def _to_owners_finish(ins, outs, send_sems, recv_sems):
    for cp in _to_owners_copies(ins, outs, send_sems, recv_sems):
        cp.wait_send()
    for w in range(len(ins)):
        for slot in range(7):
            got = outs[w].at[slot]
            _remote(got, got, send_sems.at[w, slot], recv_sems.at[w, slot], _coords()).wait_recv()


EXCHANGES = {"to_chips": (_to_chips_shapes, _to_chips_sems, _to_chips_start, _to_chips_finish),
             "to_owners": (_to_owners_shapes, _to_owners_sems, _to_owners_start, _to_owners_finish)}


def halves_to_full(hs, by_cols, *, name):
    n = len(hs)

    def body(*refs):
        ins, outs = refs[:n], refs[n:2 * n]
        send_sems, recv_sems = refs[2 * n:]
        x, y, c = _coords()
        cps = []
        for w in range(n):
            cp = _remote(ins[w], _half_of(outs[w], by_cols[w], c), send_sems.at[w], recv_sems.at[w], _peer("c", x, y, c))
            cp.start()
            cps.append(cp)
        for cp in cps:
            cp.wait()

    return pl.pallas_call(
        body, in_specs=[ANY] * n, out_specs=[ANY] * n,
        out_shape=[SDS((a.shape[0], 2 * a.shape[1]) if bc else (2 * a.shape[0], a.shape[1]), a.dtype)
                   for a, bc in zip(hs, by_cols)],
        scratch_shapes=[pltpu.SemaphoreType.DMA((n,)), pltpu.SemaphoreType.DMA((n,))],
        name=name)(*hs)


def _row_tile(rows):
    for cand in (256, 192, 176, 128, 64, 32, 16):
        if rows % cand == 0:
            return cand
    return rows


def chip_sum(g, recv, c_arr, by_cols, *, name):
    _, r, cols = g.shape

    def body(c_ref, g_ref, r_ref, f_ref, b_ref):
        tot = g_ref[...] + r_ref[...]
        f_ref[...] = tot
        b_ref[...] = tot.astype(BF16)

    if by_cols:
        tc = 4 * LANES
        nblk = cols // 2 // tc
        shape = (4, r, cols // 2)
        blk = pl.BlockSpec((None, r, tc), lambda j, i, c_ref: (j, 0, i))
        mine = pl.BlockSpec((None, r, tc), lambda j, i, c_ref: (j, 0, c_ref[0] * nblk + i))
    else:
        tr = _row_tile(r // 2)
        nblk = r // 2 // tr
        shape = (4, r // 2, cols)
        blk = pl.BlockSpec((None, tr, cols), lambda j, i, c_ref: (j, i, 0))
        mine = pl.BlockSpec((None, tr, cols), lambda j, i, c_ref: (j, c_ref[0] * nblk + i, 0))
    grid_spec = pltpu.PrefetchScalarGridSpec(num_scalar_prefetch=1, grid=(4, nblk), in_specs=[mine, blk], out_specs=[blk, blk])
    return pl.pallas_call(body, grid_spec=grid_spec, out_shape=[SDS(shape, F32), SDS(shape, BF16)],
                          name=name, compiler_params=_cp(("parallel", "parallel")))(c_arr, g, recv)


def final_sum(pf, recv, chip_arr, *, name):
    _, h, cols = pf.shape
    tr = _row_tile(h)

    def body(chip_ref, p_ref, r_ref, o_ref):
        o_ref[...] = ((p_ref[...] + r_ref[0].astype(F32)) + r_ref[1].astype(F32)) + r_ref[2].astype(F32)

    grid_spec = pltpu.PrefetchScalarGridSpec(
        num_scalar_prefetch=1, grid=(h // tr,),
        in_specs=[pl.BlockSpec((None, tr, cols), lambda i, chip_ref: (chip_ref[0], i, 0)),
                  pl.BlockSpec((3, tr, cols), lambda i, chip_ref: (0, i, 0))],
        out_specs=pl.BlockSpec((tr, cols), lambda i, chip_ref: (i, 0)))
    return pl.pallas_call(body, grid_spec=grid_spec, out_shape=SDS((h, cols), F32), name=name,
                          compiler_params=_cp(("parallel",)))(chip_arr, pf, recv)


def owner_sum(g, recv, pos_arr, *, name):
    _, r, cols = g.shape
    h = r // 2
    tr = _row_tile(h)
    nblk = h // tr

    def body(pos_ref, g_ref, r_ref, o_ref):
        tot = g_ref[...]
        for slot in range(7):
            tot = tot + r_ref[slot].astype(F32)
        o_ref[...] = tot

    grid_spec = pltpu.PrefetchScalarGridSpec(
        num_scalar_prefetch=1, grid=(nblk,),
        in_specs=[pl.BlockSpec((None, tr, cols), lambda i, pos: (pos[0], pos[1] * nblk + i, 0)),
                  pl.BlockSpec((7, tr, cols), lambda i, pos: (0, i, 0))],
        out_specs=pl.BlockSpec((tr, cols), lambda i, pos: (i, 0)))
    return pl.pallas_call(body, grid_spec=grid_spec, out_shape=SDS((h, cols), F32), name=name,
                          compiler_params=_cp(("parallel",)))(pos_arr, g, recv)


def allreduce_small(v, *, name):
    rws, cols = v.shape

    def body(v_ref, all_ref, sum_ref, send_sems, recv_sems, local_sem):
        x, y, c = _coords()
        me, sibling = (x, y, c), (x, y, 1 - c)
        chips = [(1 - x, y), (x, 1 - y), (1 - x, 1 - y)]

        def rows(px, py, pc):
            return all_ref.at[pl.ds(pl.multiple_of((4 * px + 2 * py + pc) * rws, 8), rws), :]

        def copy(k, block, to, src=None):
            return _remote(rows(*block) if src is None else src, rows(*block), send_sems.at[k], recv_sems.at[k], to)

        mine = pltpu.make_async_copy(v_ref, rows(*me), local_sem)
        mine.start()
        first = [copy(0, me, sibling, src=v_ref)]
        first += [copy(1 + j, me, (*chip, c), src=v_ref) for j, chip in enumerate(chips)]
        for cp in first:
            cp.start()
        passed = [copy(4 + j, (*chip, c), sibling) for j, chip in enumerate(chips)]
        for j, chip in enumerate(chips):
            copy(1 + j, (*chip, c), me).wait_recv()
            passed[j].start()
        copy(0, sibling, me).wait_recv()
        for j, chip in enumerate(chips):
            copy(4 + j, (*chip, 1 - c), me).wait_recv()
        for cp in first + passed:
            cp.wait_send()
        mine.wait()
        tot = all_ref[0:rws, :]
        for dev in range(1, 8):
            tot = tot + all_ref[dev * rws:(dev + 1) * rws, :]
        sum_ref[...] = tot

    vm = pl.BlockSpec(memory_space=pltpu.VMEM)
    return pl.pallas_call(
        body, in_specs=[vm], out_specs=[vm, vm],
        out_shape=[SDS((8 * rws, cols), v.dtype), SDS((rws, cols), v.dtype)],
        scratch_shapes=[pltpu.SemaphoreType.DMA((7,)), pltpu.SemaphoreType.DMA((7,)), pltpu.SemaphoreType.DMA],
        name=name)(v)[1]


def _pack_rows(parts, rows):
    out = []
    for a, r in zip(parts, rows):
        flat = a.reshape(-1)
        flat = jnp.pad(flat, (0, r * LANES - flat.shape[0]))
        out.append(flat.reshape(r, LANES))
    return jnp.concatenate(out, axis=0)


def _unpack_rows(packed, shapes, rows):
    out, at = [], 0
    for shp, r in zip(shapes, rows):
        size = int(np.prod(shp))
        out.append(packed[at:at + r].reshape(-1)[:size].reshape(shp))
        at += r
    return out


def kernel(x, g_pre_mix, w_in, b_forget, w_o_fox, w_o_dil, w_out, g_post_mix, g_pre_ffn, w_up, conv_w, conv_b, w_down, g_post_ffn, loss_target, m_g_pre_mix, m_w_in, m_b_forget, m_w_o_fox, m_w_o_dil, m_w_out, m_g_post_mix, m_g_pre_ffn, m_w_up, m_conv_w, m_conv_b, m_w_down, m_g_post_ffn, v_g_pre_mix, v_w_in, v_b_forget, v_w_o_fox, v_w_o_dil, v_w_out, v_g_post_mix, v_g_pre_ffn, v_w_up, v_conv_w, v_conv_b, v_w_down, v_g_post_ffn):
    xi, yi, ci = _coords()
    chip = 2 * xi + yi
    c_arr = jnp.reshape(ci, (1,)).astype(jnp.int32)
    chip_arr = jnp.reshape(chip, (1,)).astype(jnp.int32)
    xs = x[0]
    target = loss_target[0]
    s, d = xs.shape
    f_half = w_down.shape[1] * 4
    cols_in = w_in.shape[2]

    big = (w_in, w_o_fox, w_o_dil, w_out, w_up, w_down)
    shards = [w[0].astype(BF16) for w in big]
    a_in = allgather_balanced(shards[0], name="allgather_w_in")
    w_in_full = jnp.concatenate([jnp.where(chip == j, shards[0], a_in[j]) for j in range(4)], axis=1)
    nf = N_HEADS
    e_a, e_b = 3 * ATT_W, 3 * ATT_W + nf
    wz = jnp.concatenate([w_in_full[:, :e_a], w_in_full[:, e_b:]], axis=1)
    wf = jnp.pad(w_in_full[:, e_a:e_b], ((0, 0), (0, LANES - nf)))
    cb = conv_b
    bfo = jnp.pad(b_forget, ((0, 0), (0, LANES - nf)))

    h1 = rmsnorm_fwd(xs, g_pre_mix)
    z = mm([(h1, d, 0)], [(wz, d, 0)], nt=False, out_dtype=BF16, tm=s, tn=512, name="in_proj")
    fa = mm([(h1, d, 0)], [(wf, d, 0)], nt=False, out_dtype=F32, tm=s, tn=LANES, name="in_proj_forget")
    q_aug, k_aug, v_aug = fox_prep(z, fa, bfo)
    later = shards[1:] + [conv_w[0]]
    ya, lse_a, *late = fox_fwd(q_aug, k_aug, v_aug, gather=later, halved=[True] * 5 + [False], hps=4)
    a_of, a_od, a_out, a_up, a_down, a_cw = [
        lax.dynamic_update_index_in_dim(a4, own, chip, 0) for a4, own in zip(late, later)]
    cw = jnp.concatenate([a_cw[j] for j in range(4)], axis=1)
    wo_a = jnp.concatenate([a_of[j] for j in range(4)], axis=1)
    wo_b = jnp.concatenate([a_od[j] for j in range(4)], axis=1)
    w_o = a_out.reshape(d, d)
    w_dn = a_down.reshape(f_half, d)
    wu_a = jnp.concatenate([a_up[0], a_up[1]], axis=1)
    wu_b = jnp.concatenate([a_up[2], a_up[3]], axis=1)
    cos_t, sin_t = rope_cos_sin(s)
    yb, lse_b = dil_fwd_all(z, cos_t, sin_t)
    pa, pb, mixed = gate_mix(ya, yb, wo_a, wo_b, z)
    y1, x1, h2 = proj_norm_res(mixed, w_o, g_post_mix, xs, g_pre_ffn, tm=1024, name="out_proj")
    ua, ub, conv_a, conv_bh, mid = ffn_up(h2, wu_a, wu_b, cw, cb)
    dout, dy2, gg_post_ffn, sq = proj_norm_loss(mid, w_dn, g_post_ffn, x1, target, name="down_proj")

    dmid = mm([(dy2, d, 0)], [(w_dn, d, 0)], nt=True, out_dtype=BF16, tm=2048, tn=f_half // 2, name="down_dgrad")
    dw_down, dw_down16 = wgrad((mid, f_half, 0), dy2, tk=f_half // 2, tn=1024, ts=2048, name="down_wgrad", bf16_copy=True)
    dua, dub, gc_a, gc_b = ffn_bwd(dmid, ua, ub, conv_a, conv_bh, cw)
    dx1, dy1, gg_pre_ffn, gg_post_mix = mm_norm_bwd(
        [(dua, f_half, 0), (dub, f_half, 0)], [(wu_a, f_half, 0), (wu_b, f_half, 0)],
        [(x1, g_pre_ffn, dout, F32), (y1, g_post_mix, None, BF16)], name="up_dgrad")
    dw_up = None
    for k, du in enumerate((dua, dub)):
        dw_up = wgrad((h2, d, 0), du, tk=1024, tn=f_half // 2, ts=2048, name=f"up_wgrad_{k}", chip_major=True,
                      slabs=(4, 2 * k), into=dw_up, bf16_copy=True)
    g_ffn = [(dw_up[0], dw_up[1]), (dw_down.reshape(4, f_half // 4, d), dw_down16.reshape(4, f_half // 4, d))]
    dw_out, dw_out16 = wgrad((mixed, d, 0), dy1, tk=1024, tn=1024, ts=2048, name="out_wgrad", bf16_copy=True)
    dpa, dpb, dz_g, dya, dyb, dd_a = mix_bwd(dy1, w_o, z, pa, pb, wo_a, wo_b, ya)
    by_chip_cols = lambda a: jnp.stack([a[:, j * (d // 4):(j + 1) * (d // 4)] for j in range(4)], axis=0)
    dw_of = [by_chip_cols(a) for a in wgrad((ya, ATT_W, 0), dpa, tk=ATT_W, tn=d, ts=1024, name="fox_o_wgrad", bf16_copy=True)]
    dw_od = [by_chip_cols(a) for a in wgrad((yb, ATT_W, 0), dpb, tk=ATT_W, tn=d, ts=1024, name="dil_o_wgrad", bf16_copy=True)]
    g_mix = [dw_of, dw_od, (dw_out.reshape(4, d // 4, d), dw_out16.reshape(4, d // 4, d))]
    dq_aug, dk_aug, dv_a, *got_ffn = fox_bwd(q_aug, k_aug, z, dya, lse_a, dd_a, exchange=[g[1] for g in g_ffn], kind="to_owners")
    dz_a, dfa, gg_bf = fox_post(dq_aug, dk_aug, dv_a, fa, bfo)
    *dz_b, got_of, got_od, got_out = dil_bwd_all(z, cos_t, sin_t, dyb, lse_b, yb, exchange=[g[1] for g in g_mix],
                                                 kind="to_owners")
    got_mix = [got_of, got_od, got_out]
    dwt_a = wgrad((dz_a, e_a, 0), h1, tk=e_a // 2, tn=d, ts=2048, name="in_wgrad_a")
    dwt_b = [wgrad((part, ATT_W, 0), h1, tk=ATT_W, tn=d, ts=2048, name=f"in_wgrad_b{k}") for k, part in enumerate(dz_b)]
    dwt_g = wgrad((dz_g, 2 * d, 0), h1, tk=d, tn=d, ts=2048, name="in_wgrad_g")
    dwt_f = wgrad((dfa, LANES, 0), h1, tk=LANES, tn=d, ts=2048, name="in_wgrad_f")
    dwt_full = jnp.concatenate([dwt_a, dwt_f[:nf], *dwt_b, dwt_g], axis=0)
    dw_in = jnp.stack([dwt_full[j * cols_in:(j + 1) * cols_in] for j in range(4)], axis=0)
    from_sib = grads_to_sibling([dw_in], [True], name="grads_to_sibling_in")
    sum_in = chip_sum(dw_in, from_sib[0], c_arr, True, name="chip_sum_w_in")
    grad_x, gg_pre_mix, got_in = mm_norm_bwd(
        [(dz_a, e_a, 0), *[(part, ATT_W, 0) for part in dz_b], (dz_g, d, 0), (dz_g, d, 1), (dfa, LANES, 0)],
        [(wz, e_a, 0), *[(wz, ATT_W, Z_QB + k) for k in range(3)], (wz, d, 3), (wz, d, 4), (wf, LANES, 0)],
        [(xs, g_pre_mix, dx1, F32)], exchange=[sum_in[1]], name="in_dgrad")

    names = ("w_in", "w_o_fox", "w_o_dil", "w_out", "w_up", "w_down")
    pos_arr = jnp.concatenate([chip_arr, c_arr])
    halves = [final_sum(sum_in[0], got_in, chip_arr, name="final_sum_w_in")] + [
        owner_sum(g[0], got, pos_arr, name=f"owner_sum_{nm}") for g, got, nm in zip(g_mix + g_ffn, got_mix + got_ffn, names[1:])]
    from_half = halves_to_full(halves, [True] + [False] * 5, name="halves_to_full")
    g_big = [None] + [lax.dynamic_update_slice_in_dim(full, mine, ci * mine.shape[0], axis=0)
                      for full, mine in zip(from_half[1:], halves[1:])]
    upd_big = [adamw(w[0], g, m[0], v[0], name=f"adamw_{nm}") for w, g, m, v, nm in list(zip(
        big, g_big, (m_w_in, m_w_o_fox, m_w_o_dil, m_w_out, m_w_up, m_w_down),
        (v_w_in, v_w_o_fox, v_w_o_dil, v_w_out, v_w_up, v_w_down), names))[1:]]
    to_t = lambda a: jnp.transpose(a, (2, 0, 1))
    from_t = lambda a: jnp.transpose(a, (1, 2, 0))
    *upd_in, g_in_t = adamw_rows_view(to_t(w_in), halves[0], from_half[0], to_t(m_w_in), to_t(v_w_in), c_arr,
                                      name="adamw_w_in")

    g_cw_loc = jnp.concatenate([gc_a[0:3], gc_b[0:3]], axis=1)
    g_cb_loc = jnp.concatenate([gc_a[3:4], gc_b[3:4]], axis=1)
    small_loc = [gg_pre_mix, gg_post_mix, gg_pre_ffn, gg_post_ffn, g_cb_loc, gg_bf[:, :nf], g_cw_loc, sq * (0.5 / d)]
    red_rows = (8, 8, 8, 8, 48, 8, 136, 8)
    red = allreduce_small(_pack_rows(small_loc, red_rows), name="allreduce_small")
    g_pm, g_qm, g_pf, g_qf, g_cb, g_bf, g_cw_full, loss_11 = _unpack_rows(red, [a.shape for a in small_loc], red_rows)
    loss = loss_11[0, 0]
    cols_cw = conv_w.shape[2]
    g_cw = lax.dynamic_slice_in_dim(g_cw_full, chip * cols_cw, cols_cw, axis=1)
    small_w = (g_pre_mix, g_post_mix, g_pre_ffn, g_post_ffn, conv_b, b_forget, conv_w[0])
    small_m = (m_g_pre_mix, m_g_post_mix, m_g_pre_ffn, m_g_post_ffn, m_conv_b, m_b_forget, m_conv_w[0])
    small_v = (v_g_pre_mix, v_g_post_mix, v_g_pre_ffn, v_g_post_ffn, v_conv_b, v_b_forget, v_conv_w[0])
    small_g = (g_pm, g_qm, g_pf, g_qf, g_cb, g_bf, g_cw)
    small_names = ("g_pre_mix", "g_post_mix", "g_pre_ffn", "g_post_ffn", "conv_b", "b_forget", "conv_w")
    per_param = [adamw(w, g, m, v, name=f"adamw_{nm}") for w, g, m, v, nm in zip(small_w, small_g, small_m, small_v, small_names)]
    upd_small = [[u[j] for u in per_param] for j in range(3)]

    order = ("g_pre_mix", "w_in", "b_forget", "w_o_fox", "w_o_dil", "w_out", "g_post_mix", "g_pre_ffn", "w_up", "conv_w",
             "conv_b", "w_down", "g_post_ffn")
    grads, deltas, new_ms, new_vs = {}, {}, {}, {}
    grads["w_in"] = from_t(g_in_t)
    deltas["w_in"], new_ms["w_in"], new_vs["w_in"] = (from_t(a) for a in upd_in)
    for k, nm in enumerate(names[1:]):
        grads[nm] = g_big[k + 1][None]
        deltas[nm], new_ms[nm], new_vs[nm] = (a[None] for a in upd_big[k])
    for k, nm in enumerate(small_names):
        lead = (lambda a: a[None]) if nm == "conv_w" else (lambda a: a)
        grads[nm] = lead(small_g[k])
        deltas[nm], new_ms[nm], new_vs[nm] = (lead(upd_small[j][k]) for j in range(3))
    return (loss, grad_x[None], *[grads[nm] for nm in order], *[deltas[nm] for nm in order],
            *[new_ms[nm] for nm in order], *[new_vs[nm] for nm in order])
```

```python
import functools
import math

import numpy as np
import jax
import jax.numpy as jnp
from jax import lax
from jax.experimental import pallas as pl
from jax.experimental.pallas import tpu as pltpu

F32 = jnp.float32
BF16 = jnp.bfloat16
SDS = jax.ShapeDtypeStruct
MESH = pl.DeviceIdType.MESH

HEAD_DIM = 64
N_HEADS = 8
LANES = 128
ATT_W = N_HEADS * HEAD_DIM
DIL_PATTERNS = ((128, 1), (512, 4), (2048, 16))
DIL_BLK = 128
ROPE_DIM = HEAD_DIM // 4
ROPE_THETA = 500000.0
RMS_EPS = 1e-6
NEG = -1e30
QK_SCALE = 1.0 / math.sqrt(HEAD_DIM)
ADAM_LR, ADAM_B1, ADAM_B2, ADAM_EPS, ADAM_WD, ADAM_STEP = 0.001, 0.9, 0.999, 1e-08, 0.01, 10
VMEM_LIMIT = 56 * 1024 * 1024

Z_QA, Z_KA, Z_VA, Z_QB, Z_KB, Z_VB = 0, 1, 2, 3, 4, 5
Z_W = 5120


def _cp(sem):
    return pltpu.CompilerParams(dimension_semantics=sem, vmem_limit_bytes=VMEM_LIMIT)


def _nt(a, b):
    return lax.dot_general(a, b, (((1,), (1,)), ((), ())), preferred_element_type=F32)


def _tn(a, b):
    return lax.dot_general(a, b, (((0,), (0,)), ((), ())), preferred_element_type=F32)


def _nn(a, b):
    return jnp.dot(a, b, preferred_element_type=F32)


def _lane(shape):
    return lax.broadcasted_iota(jnp.int32, shape, 1)


def _row(shape):
    return lax.broadcasted_iota(jnp.int32, shape, 0)


def rmsnorm_fwd(x, g, *, tm=1024):
    s, d = x.shape

    def body(x_ref, g_ref, h_ref):
        xv = x_ref[...]
        inv = lax.rsqrt(jnp.mean(xv * xv, axis=-1, keepdims=True) + RMS_EPS)
        h_ref[...] = (xv * inv * g_ref[...]).astype(h_ref.dtype)

    return pl.pallas_call(
        body, grid=(s // tm,),
        in_specs=[pl.BlockSpec((tm, d), lambda i: (i, 0)), pl.BlockSpec((1, d), lambda i: (0, 0))],
        out_specs=pl.BlockSpec((tm, d), lambda i: (i, 0)),
        out_shape=SDS((s, d), BF16), name="rmsnorm_fwd", compiler_params=_cp(("parallel",)))(x, g)


def mm(a_views, b_views, *, nt, out_dtype, tm, tn, name):
    n_p = len(a_views)
    m = a_views[0][0].shape[0]
    n = b_views[0][0].shape[0] if nt else b_views[0][0].shape[1]

    def body(*refs):
        o_ref = refs[-1]
        acc = None
        for p in range(n_p):
            av = refs[p][...].astype(BF16)
            bv = refs[n_p + p][...].astype(BF16)
            dv = _nt(av, bv) if nt else _nn(av, bv)
            acc = dv if acc is None else acc + dv
        o_ref[...] = acc.astype(o_ref.dtype)

    in_specs = []
    for arr, w, blk in a_views:
        in_specs.append(pl.BlockSpec((tm, w), functools.partial(lambda i, j, blk: (i, blk), blk=blk)))
    for arr, w, blk in b_views:
        if nt:
            in_specs.append(pl.BlockSpec((tn, w), functools.partial(lambda i, j, blk: (j, blk), blk=blk)))
        else:
            in_specs.append(pl.BlockSpec((w, tn), lambda i, j: (0, j)))
    return pl.pallas_call(
        body, grid=(m // tm, n // tn), in_specs=in_specs,
        out_specs=pl.BlockSpec((tm, tn), lambda i, j: (i, j)),
        out_shape=SDS((m, n), out_dtype), name=name,
        compiler_params=_cp(("parallel", "parallel")))(*[a[0] for a in a_views], *[b[0] for b in b_views])


def wgrad(a_view, g, *, tk, tn, ts, name, chip_major=False, slabs=None, into=None, bf16_copy=False):
    arr, ka, blk = a_view
    s, n = g.shape
    ns = s // ts
    total, first = slabs if slabs else (n // tn, 0)
    n_into = 0 if into is None else (2 if bf16_copy else 1)

    def body(a_ref, g_ref, *rest):
        o_ref = rest[n_into]

        @pl.when(pl.program_id(2) == 0)
        def _():
            o_ref[...] = jnp.zeros_like(o_ref)

        o_ref[...] += _tn(a_ref[...].astype(BF16), g_ref[...].astype(BF16))
        if bf16_copy:
            @pl.when(pl.program_id(2) == ns - 1)
            def _():
                rest[n_into + 1][...] = o_ref[...].astype(BF16)

    if chip_major:
        out_spec = pl.BlockSpec((None, tk, tn), lambda i, j, k: (first + j, i, 0))
        shape = (total, ka, tn)
    else:
        out_spec = pl.BlockSpec((tk, tn), lambda i, j, k: (i, j))
        shape = (ka, n)
    in_specs = [pl.BlockSpec((ts, tk), lambda i, j, k: (k, blk * (ka // tk) + i)),
                pl.BlockSpec((ts, tn), lambda i, j, k: (k, j))]
    args = [arr, g]
    if into is not None:
        earlier = list(into) if bf16_copy else [into]
        in_specs += [pl.BlockSpec(memory_space=pl.ANY)] * len(earlier)
        args += earlier
    out = pl.pallas_call(
        body, grid=(ka // tk, n // tn, ns), in_specs=in_specs,
        out_specs=[out_spec, out_spec] if bf16_copy else out_spec,
        out_shape=[SDS(shape, F32), SDS(shape, BF16)] if bf16_copy else SDS(shape, F32), name=name,
        input_output_aliases={2 + k: k for k in range(n_into)},
        compiler_params=_cp(("parallel", "parallel", "arbitrary")))(*args)
    return out


def _norm_bwd_rows(dh, xh, inv, g):
    dxh = dh * g
    dx = inv * (dxh - xh * jnp.mean(dxh * xh, axis=-1, keepdims=True))
    return dx, jnp.sum((dh * xh).reshape(dh.shape[0] // 8, 8, dh.shape[1]), axis=0)


def proj_norm_res(a, w, g, xres, g_next, *, tm=512, name):
    s, k = a.shape
    d = w.shape[1]

    def body(a_ref, w_ref, g_ref, x_ref, gn_ref, y_ref, o_ref, h_ref):
        y = _nn(a_ref[...], w_ref[...])
        inv = lax.rsqrt(jnp.mean(y * y, axis=-1, keepdims=True) + RMS_EPS)
        xn = x_ref[...] + y * inv * g_ref[...]
        y_ref[...] = y
        o_ref[...] = xn
        inv_n = lax.rsqrt(jnp.mean(xn * xn, axis=-1, keepdims=True) + RMS_EPS)
        h_ref[...] = (xn * inv_n * gn_ref[...]).astype(h_ref.dtype)

    row = pl.BlockSpec((tm, d), lambda i: (i, 0))
    vec = pl.BlockSpec((1, d), lambda i: (0, 0))
    return pl.pallas_call(
        body, grid=(s // tm,),
        in_specs=[pl.BlockSpec((tm, k), lambda i: (i, 0)), pl.BlockSpec((k, d), lambda i: (0, 0)), vec, row, vec],
        out_specs=[row, row, row], out_shape=[SDS((s, d), F32), SDS((s, d), F32), SDS((s, d), BF16)], name=name,
        compiler_params=_cp(("parallel",)))(a, w, g, xres, g_next)


def proj_norm_loss(a, w, g, xres, target, *, tm=512, name):
    s, k = a.shape
    d = w.shape[1]
    n = s // tm

    def body(a_ref, w_ref, g_ref, x_ref, t_ref, do_ref, dy_ref, dg_ref, l_ref, acc):
        i = pl.program_id(0)

        @pl.when(i == 0)
        def _():
            acc[...] = jnp.zeros_like(acc)
            l_ref[...] = jnp.zeros_like(l_ref)

        y = _nn(a_ref[...], w_ref[...])
        inv = lax.rsqrt(jnp.mean(y * y, axis=-1, keepdims=True) + RMS_EPS)
        yh = y * inv
        err = x_ref[...] + yh * g_ref[...] - t_ref[...]
        dout = err * (1.0 / d)
        do_ref[...] = dout
        l_ref[...] += jnp.sum(jnp.sum(err * err, axis=1, keepdims=True), axis=0, keepdims=True)
        dy, part = _norm_bwd_rows(dout, yh, inv, g_ref[...])
        dy_ref[...] = dy.astype(dy_ref.dtype)
        acc[...] += part

        @pl.when(i == n - 1)
        def _():
            dg_ref[...] = jnp.sum(acc[...], axis=0, keepdims=True)

    row = pl.BlockSpec((tm, d), lambda i: (i, 0))
    vec = pl.BlockSpec((1, d), lambda i: (0, 0))
    return pl.pallas_call(
        body, grid=(n,),
        in_specs=[pl.BlockSpec((tm, k), lambda i: (i, 0)), pl.BlockSpec((k, d), lambda i: (0, 0)), vec, row, row],
        out_specs=[row, row, vec, pl.BlockSpec((1, 1), lambda i: (0, 0))],
        out_shape=[SDS((s, d), F32), SDS((s, d), BF16), SDS((1, d), F32), SDS((1, 1), F32)],
        scratch_shapes=[pltpu.VMEM((8, d), F32)], name=name, compiler_params=_cp(("arbitrary",)))(a, w, g, xres, target)


def mm_norm_bwd(a_views, b_views, stages, exchange=(), *, tm=256, name):
    n_p, n_s, ne = len(a_views), len(stages), len(exchange)
    s = a_views[0][0].shape[0]
    d = b_views[0][0].shape[0]
    n = s // tm
    has_res = [st[2] is not None for st in stages]

    def body(*refs):
        a_refs, b_refs = refs[:n_p], refs[n_p:2 * n_p]
        at = 2 * n_p
        st_refs = []
        for k in range(n_s):
            cnt = 3 if has_res[k] else 2
            st_refs.append(refs[at:at + cnt])
            at += cnt
        e_ins = refs[at:at + ne]
        at += ne
        dx_refs, dg_refs = refs[at:at + n_s], refs[at + n_s:at + 2 * n_s]
        at += 2 * n_s
        e_outs = refs[at:at + ne]
        at += ne
        accs = refs[at:at + n_s]
        comm = (e_ins, e_outs) + tuple(refs[at + n_s:])
        i = pl.program_id(0)

        @pl.when(i == 0)
        def _():
            for acc in accs:
                acc[...] = jnp.zeros_like(acc)
            if ne:
                _to_chips_start(*comm)

        dh = None
        for p in range(n_p):
            part = _nt(a_refs[p][...].astype(BF16), b_refs[p][...].astype(BF16))
            dh = part if dh is None else dh + part
        for k in range(n_s):
            xv = st_refs[k][0][...]
            inv = lax.rsqrt(jnp.mean(xv * xv, axis=-1, keepdims=True) + RMS_EPS)
            dx, part = _norm_bwd_rows(dh, xv * inv, inv, st_refs[k][1][...])
            if has_res[k]:
                dx = dx + st_refs[k][2][...]
            dx_refs[k][...] = dx.astype(dx_refs[k].dtype)
            accs[k][...] += part
            dh = dx

        @pl.when(i == n - 1)
        def _():
            for k in range(n_s):
                dg_refs[k][...] = jnp.sum(accs[k][...], axis=0, keepdims=True)
            if ne:
                _to_chips_finish(*comm)

    row = pl.BlockSpec((tm, d), lambda i: (i, 0))
    vec = pl.BlockSpec((1, d), lambda i: (0, 0))
    in_specs, args = [], []
    for arr, w, blk in a_views:
        in_specs.append(pl.BlockSpec((tm, w), functools.partial(lambda i, blk: (i, blk), blk=blk)))
        args.append(arr)
    for arr, w, blk in b_views:
        in_specs.append(pl.BlockSpec((d, w), functools.partial(lambda i, blk: (0, blk), blk=blk)))
        args.append(arr)
    for x, g, res, _ in stages:
        in_specs += [row, vec] + ([row] if res is not None else [])
        args += [x, g] + ([res] if res is not None else [])
    return pl.pallas_call(
        body, grid=(n,), in_specs=in_specs + [ANY] * ne,
        out_specs=[row] * n_s + [vec] * n_s + [ANY] * ne,
        out_shape=[SDS((s, d), st[3]) for st in stages] + [SDS((1, d), F32)] * n_s + _to_chips_shapes(exchange),
        scratch_shapes=[pltpu.VMEM((8, d), F32)] * n_s + (_to_chips_sems(ne) if ne else []), name=name,
        compiler_params=_cp(("arbitrary",)))(*args, *exchange)


def _split3(v):
    hi = v.astype(BF16).astype(F32)
    r = v - hi
    mid = r.astype(BF16).astype(F32)
    lo = (r - mid).astype(BF16).astype(F32)
    return hi, mid, lo


def _tri(n, upper):
    r = np.arange(n)
    m = (r[:, None] <= r[None, :]) if upper else (r[:, None] >= r[None, :])
    return jnp.asarray(m.astype(np.float32))


def fox_prep(z, fa, bfo, *, tb=512):
    s = z.shape[0]
    n = s // tb

    def body(q_ref, k_ref, v_ref, fa_ref, b_ref, tri_ref, qa_ref, ka_ref, va_ref, carry):
        @pl.when(pl.program_id(0) == 0)
        def _():
            carry[...] = jnp.zeros_like(carry)

        xv = fa_ref[...] + b_ref[...]
        logf = jnp.minimum(xv, 0.0) - jnp.log(1.0 + jnp.exp(-jnp.abs(xv)))
        csum = jnp.dot(tri_ref[...], logf, preferred_element_type=F32, precision=lax.Precision.HIGHEST) + carry[0:1, :]
        carry[0:1, :] = csum[tb - 1:tb, :]
        lane = _lane((tb, LANES))
        for h in range(N_HEADS):
            hi, mid, lo = _split3(csum[:, h:h + 1])
            pair = (h // 2) * LANES
            qv = q_ref[:, pair:pair + LANES].astype(F32)
            kv = k_ref[:, pair:pair + LANES].astype(F32)
            vv = v_ref[:, pair:pair + LANES].astype(F32)
            if h % 2:
                qv = pltpu.roll(qv, 64, axis=1)
                kv = pltpu.roll(kv, 64, axis=1)
                vv = pltpu.roll(vv, 64, axis=1)
            va_ref[:, h * LANES:(h + 1) * LANES] = jnp.where(lane < 64, vv, jnp.where(lane == 64, 1.0, 0.0)).astype(BF16)
            one = jnp.where((lane >= 67) & (lane < 70), 1.0, 0.0)
            q_x = jnp.where(lane == 64, hi, jnp.where(lane == 65, mid, jnp.where(lane == 66, lo, one)))
            one = jnp.where((lane >= 64) & (lane < 67), 1.0, 0.0)
            k_x = jnp.where(lane == 67, -hi, jnp.where(lane == 68, -mid, jnp.where(lane == 69, -lo, one)))
            qa_ref[:, h * LANES:(h + 1) * LANES] = jnp.where(lane < 64, qv * QK_SCALE, q_x).astype(BF16)
            ka_ref[:, h * LANES:(h + 1) * LANES] = jnp.where(lane < 64, kv, k_x).astype(BF16)

    return pl.pallas_call(
        body, grid=(n,),
        in_specs=[pl.BlockSpec((tb, ATT_W), lambda i: (i, Z_QA)), pl.BlockSpec((tb, ATT_W), lambda i: (i, Z_KA)),
                  pl.BlockSpec((tb, ATT_W), lambda i: (i, Z_VA)),
                  pl.BlockSpec((tb, LANES), lambda i: (i, 0)), pl.BlockSpec((1, LANES), lambda i: (0, 0)),
                  pl.BlockSpec((tb, tb), lambda i: (0, 0))],
        out_specs=[pl.BlockSpec((tb, N_HEADS * LANES), lambda i: (i, 0))] * 3,
        out_shape=[SDS((s, N_HEADS * LANES), BF16)] * 3,
        scratch_shapes=[pltpu.VMEM((8, LANES), F32)],
        name="fox_prep", compiler_params=_cp(("arbitrary",)))(z, z, z, fa, bfo, _tri(tb, False))


def _causal_pairs(n, k_major):
    if k_major:
        pairs = [(qi, kj) for kj in range(n) for qi in range(kj, n)]
    else:
        pairs = [(qi, kj) for qi in range(n) for kj in range(qi + 1)]
    return (jnp.asarray([p[0] for p in pairs], jnp.int32), jnp.asarray([p[1] for p in pairs], jnp.int32), len(pairs))


def fox_fwd(q_aug, k_aug, v_aug, gather=(), halved=(), *, t=1024, hps=4):
    s = v_aug.shape[0]
    qi_arr, kj_arr, n_pairs = _causal_pairs(s // t, False)
    ng = len(gather)
    n_groups = N_HEADS // hps

    def body(qi_ref, kj_ref, q_ref, k_ref, v_ref, *rest):
        g_ins, (o_ref, lse_ref), g_outs = rest[:ng], rest[ng:ng + 2], rest[ng + 2:2 * ng + 2]
        m_scr, acc_scr = rest[2 * ng + 2:2 * ng + 4]
        comm = (g_ins, g_outs) + tuple(rest[2 * ng + 4:]) + (list(halved),)
        step = pl.program_id(1)
        qi = qi_ref[step]
        kj = kj_ref[step]
        if ng:
            @pl.when((pl.program_id(0) == 0) & (step == 0))
            def _():
                _allgather_start(*comm)

        @pl.when(kj == 0)
        def _():
            m_scr[...] = jnp.full_like(m_scr, NEG)
            acc_scr[...] = jnp.zeros_like(acc_scr)

        def update(qs, ks, masked):
            nq, nk = qs.stop - qs.start, ks.stop - ks.start
            for i in range(hps):
                own = slice(i * LANES, (i + 1) * LANES)
                sc = _nt(q_ref[qs, own], k_ref[ks, own])
                if masked:
                    sc = jnp.where(_row((nq, nk)) >= _lane((nq, nk)), sc, NEG)
                m_prev = m_scr[i, qs]
                m_new = jnp.maximum(m_prev, jnp.max(sc, axis=-1, keepdims=True))
                p = jnp.exp((sc - jnp.tile(m_new, (1, nk // LANES))).astype(BF16))
                acc_scr[i, qs] = jnp.exp(m_prev - m_new) * acc_scr[i, qs] + _nn(p, v_ref[ks, own])
                m_scr[i, qs] = m_new

        whole, upper, lower = slice(0, t), slice(0, t // 2), slice(t // 2, t)

        @pl.when(kj < qi)
        def _():
            update(whole, whole, False)

        @pl.when(kj == qi)
        def _():
            update(upper, upper, True)
            update(lower, upper, False)
            update(lower, lower, True)
            lane = _lane((t, LANES))
            for pr in range(hps // 2):
                den = [acc_scr[2 * pr + i][:, 64:65] for i in range(2)]
                o_ref[:, pr * LANES:(pr + 1) * LANES] = jnp.where(
                    lane < 64, acc_scr[2 * pr] / den[0], pltpu.roll(acc_scr[2 * pr + 1] / den[1], 64, axis=1)).astype(o_ref.dtype)
                lse_ref[:, pr * LANES:(pr + 1) * LANES] = jnp.where(
                    lane < 64, m_scr[2 * pr] + jnp.log(den[0]), m_scr[2 * pr + 1] + jnp.log(den[1]))

        if ng:
            @pl.when((pl.program_id(0) == n_groups - 1) & (step == n_pairs - 1))
            def _():
                _allgather_finish(*comm)

    wide = hps * LANES
    grid_spec = pltpu.PrefetchScalarGridSpec(
        num_scalar_prefetch=2, grid=(n_groups, n_pairs),
        in_specs=[pl.BlockSpec((t, wide), lambda hg, st, qi, kj: (qi[st], hg)),
                  pl.BlockSpec((t, wide), lambda hg, st, qi, kj: (kj[st], hg)),
                  pl.BlockSpec((t, wide), lambda hg, st, qi, kj: (kj[st], hg))] + [ANY] * ng,
        out_specs=[pl.BlockSpec((t, wide // 2), lambda hg, st, qi, kj: (qi[st], hg))] * 2 + [ANY] * ng,
        scratch_shapes=[pltpu.VMEM((hps, t, LANES), F32)] * 2 + (_allgather_sems(ng) if ng else []))
    return pl.pallas_call(
        body, grid_spec=grid_spec, out_shape=[SDS((s, ATT_W), BF16), SDS((s, ATT_W), F32)] + _allgather_shapes(gather),
        name="fox_fwd", compiler_params=_cp(("arbitrary", "arbitrary")))(qi_arr, kj_arr, q_aug, k_aug, v_aug, *gather)


def fox_bwd(q_aug, k_aug, z, dy, lse, dd, exchange=(), kind="to_chips", *, t=1024, hps=4):
    s = z.shape[0]
    qi_arr, kj_arr, n_pairs = _causal_pairs(s // t, True)
    ne = len(exchange)
    n_groups = N_HEADS // hps
    x_shapes, x_sems, x_start, x_finish = EXCHANGES[kind]

    def body(qi_ref, kj_ref, q_ref, k_ref, v_ref, do_ref, lse_ref, dd_ref, *rest):
        e_ins, (dq_ref, dk_ref, dv_ref), e_outs = rest[:ne], rest[ne:ne + 3], rest[ne + 3:2 * ne + 3]
        comm = (e_ins, e_outs) + tuple(rest[2 * ne + 3:])
        step = pl.program_id(1)
        qi = qi_ref[step]
        kj = kj_ref[step]
        if ne:
            @pl.when((pl.program_id(0) == 0) & (step == 0))
            def _():
                x_start(*comm)

        @pl.when(step == 0)
        def _():
            dq_ref[...] = jnp.zeros_like(dq_ref)

        @pl.when(qi == kj)
        def _():
            dk_ref[...] = jnp.zeros_like(dk_ref)
            dv_ref[...] = jnp.zeros_like(dv_ref)

        def update(qs, ks, masked):
            nq, nk = qs.stop - qs.start, ks.stop - ks.start
            lane = _lane((nq, LANES))
            rows = pl.ds(pl.multiple_of(qi * t + qs.start, nq), nq)
            for pr in range(hps // 2):
                pair = slice(pr * LANES, (pr + 1) * LANES)
                dov = do_ref[qs, pair]
                dv_new = None
                for i in range(2):
                    head = (lane < 64) if i == 0 else (lane >= 64)
                    own = slice((2 * pr + i) * LANES, (2 * pr + i + 1) * LANES)
                    col = slice(pr * LANES + i * 64, pr * LANES + i * 64 + 1)
                    qv = q_ref[qs, own]
                    kv = k_ref[ks, own]
                    sc = _nt(qv, kv)
                    if masked:
                        sc = jnp.where(_row((nq, nk)) >= _lane((nq, nk)), sc, NEG)
                    p = jnp.exp(sc - lse_ref[qs, col])
                    dp = _nt(jnp.where(head, dov, jnp.zeros_like(dov)), v_ref[ks, pair])
                    ds = (p * (dp - dd_ref[qs, col])).astype(BF16)
                    dq_ref[rows, own] += _nn(ds, kv)
                    dk_ref[ks, own] += _tn(ds, qv)
                    dvi = _tn(p.astype(BF16), dov)
                    dv_new = dvi if dv_new is None else jnp.where(head, dvi, dv_new)
                dv_ref[ks, pair] += dv_new

        whole, upper, lower = slice(0, t), slice(0, t // 2), slice(t // 2, t)

        @pl.when(kj < qi)
        def _():
            update(whole, whole, False)

        @pl.when(kj == qi)
        def _():
            update(upper, upper, True)
            update(lower, upper, False)
            update(lower, lower, True)

        if ne:
            @pl.when((pl.program_id(0) == n_groups - 1) & (step == n_pairs - 1))
            def _():
                x_finish(*comm)

    wide, half = hps * LANES, hps // 2 * LANES
    v_blk = Z_VA * ATT_W // half
    grid_spec = pltpu.PrefetchScalarGridSpec(
        num_scalar_prefetch=2, grid=(n_groups, n_pairs),
        in_specs=[pl.BlockSpec((t, wide), lambda hg, st, qi, kj: (qi[st], hg)),
                  pl.BlockSpec((t, wide), lambda hg, st, qi, kj: (kj[st], hg)),
                  pl.BlockSpec((t, half), lambda hg, st, qi, kj: (kj[st], v_blk + hg)),
                  pl.BlockSpec((t, half), lambda hg, st, qi, kj: (qi[st], hg)),
                  pl.BlockSpec((t, half), lambda hg, st, qi, kj: (qi[st], hg)),
                  pl.BlockSpec((t, half), lambda hg, st, qi, kj: (qi[st], hg))] + [ANY] * ne,
        out_specs=[pl.BlockSpec((s, wide), lambda hg, st, qi, kj: (0, hg)),
                   pl.BlockSpec((t, wide), lambda hg, st, qi, kj: (kj[st], hg)),
                   pl.BlockSpec((t, half), lambda hg, st, qi, kj: (kj[st], hg))] + [ANY] * ne,
        scratch_shapes=x_sems(ne) if ne else [])
    return pl.pallas_call(
        body, grid_spec=grid_spec,
        out_shape=[SDS((s, N_HEADS * LANES), F32), SDS((s, N_HEADS * LANES), F32), SDS((s, ATT_W), F32)]
        + x_shapes(exchange),
        name="fox_bwd", compiler_params=_cp(("arbitrary", "arbitrary")))(qi_arr, kj_arr, q_aug, k_aug, z, dy, lse, dd, *exchange)


def fox_post(dq_aug, dk_aug, dv, fa, bfo, *, tb=512):
    s = dv.shape[0]
    n = s // tb

    def body(dq_ref, dk_ref, dv_ref, fa_ref, b_ref, tri_ref, dz_ref, dfa_ref, gb_ref, carry, acc):
        i = pl.program_id(0)

        @pl.when(i == 0)
        def _():
            carry[...] = jnp.zeros_like(carry)
            acc[...] = jnp.zeros_like(acc)

        lane = _lane((tb, LANES))
        d_f = jnp.zeros((tb, LANES), F32)
        for h in range(N_HEADS):
            col = dq_ref[:, h * LANES + 64:h * LANES + 65] - dk_ref[:, h * LANES + 67:h * LANES + 68]
            d_f = jnp.where(lane == h, col, d_f)
        suffix = jnp.dot(tri_ref[...], d_f, preferred_element_type=F32, precision=lax.Precision.HIGHEST) + carry[0:1, :]
        carry[0:1, :] = suffix[0:1, :]
        xv = fa_ref[...] + b_ref[...]
        dx = suffix * (1.0 / (1.0 + jnp.exp(xv)))
        dfa_ref[...] = dx.astype(dfa_ref.dtype)
        acc[...] += jnp.sum(dx.reshape(tb // 8, 8, LANES), axis=0)
        for hp in range(4):
            for src, off, scale in ((dq_ref, 0, QK_SCALE), (dk_ref, ATT_W, 1.0)):
                even = src[:, (2 * hp) * LANES:(2 * hp + 1) * LANES]
                odd = pltpu.roll(src[:, (2 * hp + 1) * LANES:(2 * hp + 2) * LANES], 64, axis=1)
                dz_ref[:, off + hp * LANES:off + (hp + 1) * LANES] = (jnp.where(lane < 64, even, odd) * scale).astype(BF16)
        dz_ref[:, 2 * ATT_W:3 * ATT_W] = dv_ref[...].astype(BF16)

        @pl.when(i == n - 1)
        def _():
            gb_ref[...] = jnp.sum(acc[...], axis=0, keepdims=True)

    rev = lambda i: (n - 1 - i, 0)
    return pl.pallas_call(
        body, grid=(n,),
        in_specs=[pl.BlockSpec((tb, N_HEADS * LANES), rev), pl.BlockSpec((tb, N_HEADS * LANES), rev),
                  pl.BlockSpec((tb, ATT_W), rev), pl.BlockSpec((tb, LANES), rev),
                  pl.BlockSpec((1, LANES), lambda i: (0, 0)), pl.BlockSpec((tb, tb), lambda i: (0, 0))],
        out_specs=[pl.BlockSpec((tb, 3 * ATT_W), rev), pl.BlockSpec((tb, LANES), rev),
                   pl.BlockSpec((1, LANES), lambda i: (0, 0))],
        out_shape=[SDS((s, 3 * ATT_W), BF16), SDS((s, LANES), BF16), SDS((1, LANES), F32)],
        scratch_shapes=[pltpu.VMEM((8, LANES), F32), pltpu.VMEM((8, LANES), F32)],
        name="fox_post", compiler_params=_cp(("arbitrary",)))(dq_aug, dk_aug, dv, fa, bfo, _tri(tb, True))


def rope_cos_sin(s):
    half = ROPE_DIM // 2
    inv_freq = ROPE_THETA ** (-jnp.arange(half, dtype=F32) * 2.0 / ROPE_DIM)
    ang = jnp.arange(s, dtype=F32)[:, None] * inv_freq[None, :]
    return jnp.tile(jnp.cos(ang), (1, LANES // half)), jnp.tile(jnp.sin(ang), (1, LANES // half))


def _rotate(x, cos, sin, sign):
    l64 = _lane(x.shape) & (HEAD_DIM - 1)
    first = l64 < ROPE_DIM // 2
    second = (l64 >= ROPE_DIM // 2) & (l64 < ROPE_DIM)
    from_next = jnp.where(first, -sign * sin, 0.0)
    from_prev = jnp.where(second, sign * sin, 0.0)
    return (x * jnp.where(first | second, cos, 1.0) + pltpu.roll(x, LANES - 8, axis=1) * from_next
            + pltpu.roll(x, 8, axis=1) * from_prev)


def _dil_rows(base, r):
    if r == 1:
        return pl.ds(pl.multiple_of(base, DIL_BLK), DIL_BLK)
    return pl.ds(base, DIL_BLK, stride=r)


def _dil_block(idx, r, nb):
    shift = nb.bit_length() - 1
    rho = idx >> shift
    n = idx & (nb - 1)
    base = rho + n * (r * DIL_BLK)
    return _dil_rows(base, r), _dil_rows(jnp.maximum(base - r * DIL_BLK, rho), r), n > 0


def _cat(a, b):
    return jnp.concatenate([a, b], axis=0)


def _two_heads(v, first_head):
    zero = jnp.zeros_like(v)
    return _cat(jnp.where(first_head, v, zero), jnp.where(first_head, zero, v))


def _dil_bands():
    b = DIL_BLK
    q = _row((2 * b, 2 * b)) & (b - 1)
    col = _lane((2 * b, 2 * b))
    return (col < b) & (col >= q), (col >= b) & (col - b <= q)


def _dil_load_qkv(zq_ref, zk_ref, zv_ref, cos_ref, sin_ref, q_ref, k_ref, v_ref, *, chunk=512):
    def step(i, carry):
        rows = pl.ds(pl.multiple_of(i * chunk, chunk), chunk)
        cos, sin = cos_ref[rows, :], sin_ref[rows, :]
        q_ref[rows, :] = _rotate(zq_ref[rows, :].astype(F32), cos, sin, 1.0) * QK_SCALE
        k_ref[rows, :] = _rotate(zk_ref[rows, :].astype(F32), cos, sin, 1.0)
        v_ref[rows, :] = zv_ref[rows, :].astype(F32)
        return carry

    lax.fori_loop(0, q_ref.shape[0] // chunk, step, 0)


def dil_fwd_all(z, cos_t, sin_t, *, unroll=32):
    s = z.shape[0]
    b = DIL_BLK
    n_blk = s // b

    def body(zq_ref, zk_ref, zv_ref, cos_ref, sin_ref, o_ref, l_ref, q_ref, k_ref, v_ref):
        _dil_load_qkv(zq_ref, zk_ref, zv_ref, cos_ref, sin_ref, q_ref, k_ref, v_ref)
        first_head = _lane((b, LANES)) < 64
        band_prev, band_cur = _dil_bands()
        for g, (_, r) in enumerate(DIL_PATTERNS):
            nb = n_blk // r

            def group(it, carry, g=g, r=r, nb=nb):
                loaded = []
                kc = vc = None
                for u in range(unroll):
                    rows_c, rows_p, has_prev = _dil_block(it * unroll + u, r, nb)
                    if u % min(nb, unroll):
                        kp, vp = kc, vc
                    else:
                        kp, vp = k_ref[rows_p, :].astype(BF16), v_ref[rows_p, :].astype(BF16)
                    kc, vc = k_ref[rows_c, :].astype(BF16), v_ref[rows_c, :].astype(BF16)
                    state = (o_ref[rows_c, :], l_ref[rows_c, :]) if g else None
                    loaded.append((rows_c, has_prev, [q_ref[rows_c, :].astype(BF16), kp, kc, vp, vc], state))
                done = []
                for rows_c, has_prev, (qv, kp, kc, vp, vc), state in loaded:
                    sc = jnp.where(band_cur | (band_prev & has_prev), _nt(_two_heads(qv, first_head), _cat(kp, kc)), NEG)
                    m = jnp.max(sc, axis=-1, keepdims=True)
                    p = jnp.exp(sc - m)
                    den = jnp.sum(p, axis=-1, keepdims=True)
                    both = _nn(p.astype(BF16), _cat(vp, vc)) / den
                    lse2 = m + jnp.log(den)
                    ov = jnp.where(first_head, both[:b], both[b:])
                    lse = jnp.where(first_head, lse2[:b], lse2[b:])
                    if state is not None:
                        m2 = jnp.maximum(state[1], lse)
                        wp = jnp.exp(state[1] - m2)
                        wn = jnp.exp(lse - m2)
                        ov = (wp * state[0] + wn * ov) / (wp + wn)
                        lse = m2 + jnp.log(wp + wn)
                    done.append((rows_c, ov, lse))
                for rows_c, ov, lse in done:
                    o_ref[rows_c, :] = ov
                    l_ref[rows_c, :] = lse
                return carry

            lax.fori_loop(0, n_blk // unroll, group, 0)

    col_blk = lambda k: pl.BlockSpec((s, LANES), lambda hp: (0, 4 * k + hp))
    table = pl.BlockSpec((s, LANES), lambda hp: (0, 0))
    out = pl.BlockSpec((s, LANES), lambda hp: (0, hp))
    return pl.pallas_call(
        body, grid=(4,), in_specs=[col_blk(Z_QB), col_blk(Z_KB), col_blk(Z_VB), table, table], out_specs=[out, out],
        out_shape=[SDS((s, ATT_W), F32)] * 2, scratch_shapes=[pltpu.VMEM((s, LANES), F32)] * 3, name="dil_fwd",
        compiler_params=_cp(("parallel",)))(z, z, z, cos_t, sin_t)


def dil_bwd_all(z, cos_t, sin_t, dy, lse, y, exchange=(), kind="to_chips", *, unroll=16):
    s = z.shape[0]
    b = DIL_BLK
    n_blk = s // b
    ne = len(exchange)
    x_shapes, x_sems, x_start, x_finish = EXCHANGES[kind]

    def body(zq_ref, zk_ref, zv_ref, cos_ref, sin_ref, do_ref, l_ref, y_ref, *rest):
        e_ins, (gq_ref, gk_ref, gv_ref), e_outs = rest[:ne], rest[ne:ne + 3], rest[ne + 3:2 * ne + 3]
        q_ref, k_ref, v_ref, dq_ref, dk_ref, dv_ref = rest[2 * ne + 3:2 * ne + 9]
        comm = (e_ins, e_outs) + tuple(rest[2 * ne + 9:])
        if ne:
            @pl.when(pl.program_id(0) == 0)
            def _():
                x_start(*comm)

        _dil_load_qkv(zq_ref, zk_ref, zv_ref, cos_ref, sin_ref, q_ref, k_ref, v_ref)
        dq_ref[...] = jnp.zeros_like(dq_ref)
        dk_ref[...] = jnp.zeros_like(dk_ref)
        dv_ref[...] = jnp.zeros_like(dv_ref)
        first_head = _lane((b, LANES)) < 64
        band_prev, band_cur = _dil_bands()
        for _, r in DIL_PATTERNS:
            nb = n_blk // r

            def group(it, carry, r=r, nb=nb):
                loaded = []
                kc = vc = None
                for u in range(unroll):
                    rows_c, rows_p, has_prev = _dil_block(it * unroll + u, r, nb)
                    if u % min(nb, unroll):
                        kp, vp = kc, vc
                    else:
                        kp, vp = k_ref[rows_p, :].astype(BF16), v_ref[rows_p, :].astype(BF16)
                    kc, vc = k_ref[rows_c, :].astype(BF16), v_ref[rows_c, :].astype(BF16)
                    vals = [q_ref[rows_c, :].astype(BF16), kp, kc, vp, vc, do_ref[rows_c, :], l_ref[rows_c, :], y_ref[rows_c, :]]
                    loaded.append((rows_c, rows_p, has_prev, vals))
                done = []
                for rows_c, rows_p, has_prev, (qv, kp, kc, vp, vc, dof, lv, yv) in loaded:
                    q2 = _two_heads(qv, first_head)
                    do2 = _two_heads(dof.astype(BF16), first_head)
                    kcat, vcat = _cat(kp, kc), _cat(vp, vc)
                    lse2 = _cat(lv[:, 0:1], lv[:, 64:65])
                    dd2 = jnp.sum(_two_heads(dof * yv, first_head), axis=-1, keepdims=True)
                    p = jnp.exp(jnp.where(band_cur | (band_prev & has_prev), _nt(q2, kcat), NEG) - lse2)
                    ds = (p * (_nt(do2, vcat) - dd2)).astype(BF16)
                    dq2 = _nn(ds, kcat)
                    dkcat = _tn(ds, q2)
                    dvcat = _tn(p.astype(BF16), do2)
                    done.append((rows_c, rows_p, (jnp.where(first_head, dq2[:b], dq2[b:]), dkcat[:b], dkcat[b:],
                                                  dvcat[:b], dvcat[b:])))
                held = None
                for u, (rows_c, rows_p, (dq, dk_p, dk_c, dv_p, dv_c)) in enumerate(done):
                    dq_ref[rows_c, :] += dq
                    if u % min(nb, unroll):
                        rows_h, dk_h, dv_h = held
                        dk_ref[rows_h, :] += dk_h + dk_p
                        dv_ref[rows_h, :] += dv_h + dv_p
                    else:
                        if held is not None:
                            dk_ref[held[0], :] += held[1]
                            dv_ref[held[0], :] += held[2]
                        dk_ref[rows_p, :] += dk_p
                        dv_ref[rows_p, :] += dv_p
                    held = (rows_c, dk_c, dv_c)
                dk_ref[held[0], :] += held[1]
                dv_ref[held[0], :] += held[2]
                return carry

            lax.fori_loop(0, n_blk // unroll, group, 0)

        def finish(i, carry, chunk=512):
            rows = pl.ds(pl.multiple_of(i * chunk, chunk), chunk)
            cos, sin = cos_ref[rows, :], sin_ref[rows, :]
            gq_ref[rows, :] = (_rotate(dq_ref[rows, :], cos, sin, -1.0) * QK_SCALE).astype(BF16)
            gk_ref[rows, :] = _rotate(dk_ref[rows, :], cos, sin, -1.0).astype(BF16)
            gv_ref[rows, :] = dv_ref[rows, :].astype(BF16)
            return carry

        lax.fori_loop(0, s // 512, finish, 0)
        if ne:
            @pl.when(pl.program_id(0) == 3)
            def _():
                x_finish(*comm)

    col_blk = lambda k: pl.BlockSpec((s, LANES), lambda hp: (0, 4 * k + hp))
    table = pl.BlockSpec((s, LANES), lambda hp: (0, 0))
    nat = pl.BlockSpec((s, LANES), lambda hp: (0, hp))
    return pl.pallas_call(
        body, grid=(4,), in_specs=[col_blk(Z_QB), col_blk(Z_KB), col_blk(Z_VB), table, table, nat, nat, nat] + [ANY] * ne,
        out_specs=[nat, nat, nat] + [ANY] * ne, out_shape=[SDS((s, ATT_W), BF16)] * 3 + x_shapes(exchange),
        scratch_shapes=[pltpu.VMEM((s, LANES), F32)] * 6 + (x_sems(ne) if ne else []), name="dil_bwd",
        compiler_params=_cp(("arbitrary",)))(z, z, z, cos_t, sin_t, dy, lse, y, *exchange)


def _sigmoid(v):
    return 1.0 / (1.0 + jnp.exp(-v))


def gate_mix(ya, yb, wa, wb, z, *, tm=2048, tn=512):
    s = ya.shape[0]
    d = wa.shape[1]
    ga_blk = 3 * ATT_W * 2 // tn
    gb_blk = ga_blk + d // tn

    def body(ya_ref, yb_ref, wa_ref, wb_ref, ga_ref, gb_ref, pa_ref, pb_ref, mx_ref):
        pa = _nn(ya_ref[...], wa_ref[...])
        pb = _nn(yb_ref[...].astype(BF16), wb_ref[...])
        pa_ref[...] = pa.astype(BF16)
        pb_ref[...] = pb.astype(BF16)
        mx_ref[...] = (_sigmoid(ga_ref[...].astype(F32)) * pa + _sigmoid(gb_ref[...].astype(F32)) * pb).astype(BF16)

    out = pl.BlockSpec((tm, tn), lambda i, j: (i, j))
    return pl.pallas_call(
        body, grid=(s // tm, d // tn),
        in_specs=[pl.BlockSpec((tm, ATT_W), lambda i, j: (i, 0)), pl.BlockSpec((tm, ATT_W), lambda i, j: (i, 0)),
                  pl.BlockSpec((ATT_W, tn), lambda i, j: (0, j)), pl.BlockSpec((ATT_W, tn), lambda i, j: (0, j)),
                  pl.BlockSpec((tm, tn), lambda i, j: (i, ga_blk + j)), pl.BlockSpec((tm, tn), lambda i, j: (i, gb_blk + j))],
        out_specs=[out, out, out], out_shape=[SDS((s, d), BF16)] * 3, name="gate_mix",
        compiler_params=_cp(("parallel", "parallel")))(ya, yb, wa, wb, z, z)


def mix_bwd(dy, w_o, z, pa, pb, wo_a, wo_b, ya, *, tm=512):
    s, d = dy.shape

    def body(dy_ref, wo_ref, ga_ref, gb_ref, pa_ref, pb_ref, wa_ref, wb_ref, ya_ref,
             dpa_ref, dpb_ref, dg_ref, dya_ref, dyb_ref, dd_ref):
        dm = _nt(dy_ref[...], wo_ref[...])
        sa = _sigmoid(ga_ref[...].astype(F32))
        sb = _sigmoid(gb_ref[...].astype(F32))
        dpa = (dm * sa).astype(BF16)
        dpb = (dm * sb).astype(BF16)
        dpa_ref[...] = dpa
        dpb_ref[...] = dpb
        dg_ref[:, 0:d] = (dm * pa_ref[...].astype(F32) * sa * (1.0 - sa)).astype(BF16)
        dg_ref[:, d:2 * d] = (dm * pb_ref[...].astype(F32) * sb * (1.0 - sb)).astype(BF16)
        dya = _nt(dpa, wa_ref[...]).astype(BF16)
        dya_ref[...] = dya
        dyb_ref[...] = _nt(dpb, wb_ref[...])
        lane = _lane((tm, LANES))
        for pr in range(ATT_W // LANES):
            pair = slice(pr * LANES, (pr + 1) * LANES)
            prod = dya[:, pair].astype(F32) * ya_ref[:, pair].astype(F32)
            lo = jnp.sum(jnp.where(lane < 64, prod, 0.0), axis=-1, keepdims=True)
            hi = jnp.sum(jnp.where(lane >= 64, prod, 0.0), axis=-1, keepdims=True)
            dd_ref[:, pair] = jnp.where(lane < 64, lo, hi)

    row = pl.BlockSpec((tm, d), lambda i: (i, 0))
    att = pl.BlockSpec((tm, ATT_W), lambda i: (i, 0))
    whole = lambda a: pl.BlockSpec(a.shape, lambda i: (0, 0))
    return pl.pallas_call(
        body, grid=(s // tm,),
        in_specs=[row, whole(w_o), pl.BlockSpec((tm, d), lambda i: (i, 3)), pl.BlockSpec((tm, d), lambda i: (i, 4)), row, row,
                  whole(wo_a), whole(wo_b), att],
        out_specs=[row, row, pl.BlockSpec((tm, 2 * d), lambda i: (i, 0)), att, att, att],
        out_shape=[SDS((s, d), BF16), SDS((s, d), BF16), SDS((s, 2 * d), BF16), SDS((s, ATT_W), BF16),
                   SDS((s, ATT_W), F32), SDS((s, ATT_W), F32)], name="mix_bwd",
        compiler_params=_cp(("parallel",)))(dy, w_o, z, z, pa, pb, wo_a, wo_b, ya)


GELU_C = math.sqrt(2.0 / math.pi)


def _gelu_parts(a):
    a2 = a * a
    th = jnp.tanh(a * (GELU_C + (GELU_C * 0.044715) * a2))
    half = 0.5 * a
    gelu = half + half * th
    dgelu = (0.5 + 0.5 * th) + half * (1.0 - th * th) * (GELU_C + (3.0 * GELU_C * 0.044715) * a2)
    return gelu, dgelu


def _causal_taps(u, before):
    row = _row(u.shape)
    r1 = jnp.where(row == 0, before[7:8, :], pltpu.roll(u, 1, axis=0))
    r2 = jnp.where(row == 0, before[6:7, :], jnp.where(row == 1, before[7:8, :], pltpu.roll(u, 2, axis=0)))
    return r1, r2


def ffn_up(h, wa, wb, cw, cb, *, tm=2048, tn=256):
    s, d = h.shape
    f = wa.shape[1]
    nj = f // tn

    def body(h_ref, wa_ref, wb_ref, cwa_ref, cwb_ref, cba_ref, cbb_ref, ua_ref, ub_ref, ca_ref, cbo_ref, m_ref, carry):
        @pl.when(pl.program_id(1) == 0)
        def _():
            carry[...] = jnp.zeros_like(carry)

        conv = []
        for k, (w_ref, cw_ref, cb_ref, u_ref, c_ref) in enumerate(((wa_ref, cwa_ref, cba_ref, ua_ref, ca_ref),
                                                                   (wb_ref, cwb_ref, cbb_ref, ub_ref, cbo_ref))):
            u = _nn(h_ref[...], w_ref[...])
            u_ref[...] = u.astype(BF16)
            r1, r2 = _causal_taps(u, carry[k])
            carry[k] = u[tm - 8:tm, :]
            conv.append(cw_ref[0:1, :] * r2 + cw_ref[1:2, :] * r1 + cw_ref[2:3, :] * u + cb_ref[...])
            c_ref[...] = conv[k].astype(BF16)
        m_ref[...] = (_gelu_parts(conv[0])[0] * conv[1]).astype(BF16)

    out = pl.BlockSpec((tm, tn), lambda j, i: (i, j))
    return pl.pallas_call(
        body, grid=(nj, s // tm),
        in_specs=[pl.BlockSpec((tm, d), lambda j, i: (i, 0)),
                  pl.BlockSpec((d, tn), lambda j, i: (0, j)), pl.BlockSpec((d, tn), lambda j, i: (0, j)),
                  pl.BlockSpec((3, tn), lambda j, i: (0, j)), pl.BlockSpec((3, tn), lambda j, i: (0, nj + j)),
                  pl.BlockSpec((1, tn), lambda j, i: (0, j)), pl.BlockSpec((1, tn), lambda j, i: (0, nj + j))],
        out_specs=[out] * 5, out_shape=[SDS((s, f), BF16)] * 5,
        scratch_shapes=[pltpu.VMEM((2, 8, tn), F32)], name="ffn_up",
        compiler_params=_cp(("parallel", "arbitrary")))(h, wa, wb, cw, cw, cb, cb)


def ffn_bwd(dm, ua, ub, ca, cbo, cw, *, tm=2048, tn=256):
    s, f = dm.shape
    nj = f // tn
    ni = s // tm

    def body(dm_ref, ua_ref, ub_ref, ca_ref, cbo_ref, cwa_ref, cwb_ref, dua_ref, dub_ref, ga_ref, gb_ref, carry):
        @pl.when(pl.program_id(1) == 0)
        def _():
            carry[...] = jnp.zeros_like(carry)
            ga_ref[...] = jnp.zeros_like(ga_ref)
            gb_ref[...] = jnp.zeros_like(gb_ref)

        row = _row((tm, tn))
        dmv = dm_ref[...].astype(F32)
        gelu, dgelu = _gelu_parts(ca_ref[...].astype(F32))
        dcs = (dmv * cbo_ref[...].astype(F32) * dgelu, dmv * gelu)
        for k, (dc, u_ref, cw_ref, du_ref, g_ref) in enumerate(((dcs[0], ua_ref, cwa_ref, dua_ref, ga_ref),
                                                                (dcs[1], ub_ref, cwb_ref, dub_ref, gb_ref))):
            u = u_ref[...].astype(F32)
            after = carry[k]
            n1 = jnp.where(row == tm - 1, after[0:1, :], pltpu.roll(dc, tm - 1, axis=0))
            n2 = jnp.where(row == tm - 2, after[0:1, :], jnp.where(row == tm - 1, after[1:2, :], pltpu.roll(dc, tm - 2, axis=0)))
            g_ref[0:1, :] += jnp.sum(n2 * u, axis=0, keepdims=True)
            g_ref[1:2, :] += jnp.sum(n1 * u, axis=0, keepdims=True)
            g_ref[2:3, :] += jnp.sum(dc * u, axis=0, keepdims=True)
            g_ref[3:4, :] += jnp.sum(dc, axis=0, keepdims=True)
            du_ref[...] = (cw_ref[2:3, :] * dc + cw_ref[1:2, :] * n1 + cw_ref[0:1, :] * n2).astype(BF16)
            carry[k] = dc[0:8, :]

    tile = pl.BlockSpec((tm, tn), lambda j, i: (ni - 1 - i, j))
    gspec = pl.BlockSpec((8, tn), lambda j, i: (0, j))
    return pl.pallas_call(
        body, grid=(nj, ni),
        in_specs=[tile] * 5 + [pl.BlockSpec((3, tn), lambda j, i: (0, j)), pl.BlockSpec((3, tn), lambda j, i: (0, nj + j))],
        out_specs=[tile, tile, gspec, gspec],
        out_shape=[SDS((s, f), BF16), SDS((s, f), BF16), SDS((8, f), F32), SDS((8, f), F32)],
        scratch_shapes=[pltpu.VMEM((2, 8, tn), F32)], name="ffn_bwd",
        compiler_params=_cp(("parallel", "arbitrary")))(dm, ua, ub, ca, cbo, cw, cw)


def adamw(w, g, m, v, *, name, tr=None):
    r = w.shape[0]
    rest = w.shape[1:]
    if tr is None:
        tr = r
        for cand in (256, 128, 64, 32, 16, 8):
            if r % cand == 0:
                tr = cand
                break

    def body(w_ref, g_ref, m_ref, v_ref, d_ref, nm_ref, nv_ref):
        gv = g_ref[...]
        mn = ADAM_B1 * m_ref[...] + (1.0 - ADAM_B1) * gv
        vn = ADAM_B2 * v_ref[...] + (1.0 - ADAM_B2) * (gv * gv)
        m_hat = mn / (1.0 - ADAM_B1 ** ADAM_STEP)
        v_hat = vn / (1.0 - ADAM_B2 ** ADAM_STEP)
        d_ref[...] = -ADAM_LR * (m_hat / (jnp.sqrt(v_hat) + ADAM_EPS) + ADAM_WD * w_ref[...])
        nm_ref[...] = mn
        nv_ref[...] = vn

    blk = pl.BlockSpec((tr,) + rest, lambda i: (i,) + (0,) * len(rest))
    return pl.pallas_call(body, grid=(r // tr,), in_specs=[blk] * 4, out_specs=[blk] * 3, out_shape=[SDS(w.shape, F32)] * 3,
                          name=name, compiler_params=_cp(("parallel",)))(w, g, m, v)


def adamw_rows_view(w, g_mine, g_full, m, v, c_arr, *, name, tc=256):
    r, _, c = w.shape
    per_half = c // 2 // tc

    def body(c_ref, w_ref, gm_ref, gf_ref, m_ref, v_ref, d_ref, nm_ref, nv_ref, go_ref):
        mine = (pl.program_id(0) >> (per_half.bit_length() - 1)) == c_ref[0]
        gv = jnp.where(mine, gm_ref[...], gf_ref[...])
        mn = ADAM_B1 * m_ref[:, 0, :] + (1.0 - ADAM_B1) * gv
        vn = ADAM_B2 * v_ref[:, 0, :] + (1.0 - ADAM_B2) * (gv * gv)
        m_hat = mn / (1.0 - ADAM_B1 ** ADAM_STEP)
        v_hat = vn / (1.0 - ADAM_B2 ** ADAM_STEP)
        d_ref[:, 0, :] = -ADAM_LR * (m_hat / (jnp.sqrt(v_hat) + ADAM_EPS) + ADAM_WD * w_ref[:, 0, :])
        nm_ref[:, 0, :] = mn
        nv_ref[:, 0, :] = vn
        go_ref[:, 0, :] = gv

    b3 = pl.BlockSpec((r, 1, tc), lambda i, c_ref: (0, 0, i))
    own = pl.BlockSpec((r, tc), lambda i, c_ref: (0, jnp.clip(i - c_ref[0] * per_half, 0, per_half - 1)))
    full = pl.BlockSpec((r, tc), lambda i, c_ref: (0, i))
    grid_spec = pltpu.PrefetchScalarGridSpec(num_scalar_prefetch=1, grid=(c // tc,), in_specs=[b3, own, full, b3, b3],
                                             out_specs=[b3] * 4)
    return pl.pallas_call(body, grid_spec=grid_spec, out_shape=[SDS(w.shape, F32)] * 4, name=name,
                          compiler_params=_cp(("parallel",)))(c_arr, w, g_mine, g_full, m, v)


ANY = pl.BlockSpec(memory_space=pl.ANY)
ICI_KINDS = ("x", "y", "xy")


def _coords():
    return lax.axis_index("x"), lax.axis_index("y"), lax.axis_index("c")


def _peer(kind, x, y, c):
    if kind == "c":
        return (x, y, 1 - c)
    if kind == "x":
        return (1 - x, y, c)
    if kind == "y":
        return (x, 1 - y, c)
    return (1 - x, 1 - y, c)


def _chip_of(p):
    return 2 * p[0] + p[1]


def _half(rows, which):
    h = rows // 2
    return pl.ds(pl.multiple_of(which * h, 16), h)


def _remote(src, dst, send_sem, recv_sem, to):
    return pltpu.make_async_remote_copy(src_ref=src, dst_ref=dst, send_sem=send_sem, recv_sem=recv_sem,
                                        device_id=to, device_id_type=MESH)


def allgather_balanced(shard, *, name):
    r, cols = shard.shape
    h, q = r // 2, r // 4

    def body(in_ref, out_ref, send_sems, recv_sems):
        x, y, c = _coords()
        me, sibling = (x, y, c), (x, y, 1 - c)
        nbr = ((1 - x, y, c), (x, 1 - y, c))
        chip = (2 * (1 - x) + y, 2 * x + (1 - y), 2 * (1 - x) + (1 - y))
        quarter = lambda core, i: pl.ds(pl.multiple_of(core * h + i * q, 16), q)
        sent = []

        def go(src, dst, slot, to):
            cp = _remote(src, dst, send_sems.at[slot], recv_sems.at[slot], to)
            cp.start()
            sent.append(cp)

        def landed(region, slot):
            _remote(region, region, send_sems.at[slot], recv_sems.at[slot], me).wait_recv()

        for i in range(2):
            for k in range(2):
                qi = k if i == 0 else 1 - k
                go(in_ref.at[quarter(c, qi)], out_ref.at[2 * x + y, quarter(c, qi)], 2 * k + qi, nbr[k])
        for k in range(2):
            piece = out_ref.at[chip[k], quarter(c, k)]
            landed(piece, 2 * k + k)
            go(piece, piece, 4 + k, nbr[1 - k])
            go(piece, piece, 6 + 2 * k + k, sibling)
        for k in range(2):
            piece = out_ref.at[chip[k], quarter(c, 1 - k)]
            landed(piece, 2 * k + 1 - k)
            go(piece, piece, 6 + 2 * k + 1 - k, sibling)
        for k in range(2):
            piece = out_ref.at[chip[2], quarter(c, k)]
            landed(piece, 4 + k)
            go(piece, piece, 10 + k, sibling)
        for k in range(2):
            for i in range(2):
                landed(out_ref.at[chip[k], quarter(1 - c, i)], 6 + 2 * k + i)
            landed(out_ref.at[chip[2], quarter(1 - c, k)], 10 + k)
        for cp in sent:
            cp.wait_send()

    return pl.pallas_call(
        body, in_specs=[ANY], out_specs=ANY, out_shape=SDS((4,) + shard.shape, shard.dtype),
        scratch_shapes=[pltpu.SemaphoreType.DMA((12,)), pltpu.SemaphoreType.DMA((12,))], name=name)(shard)


def _allgather_shapes(shards):
    return [SDS((4,) + a.shape, a.dtype) for a in shards]


def _allgather_sems(n):
    return [pltpu.SemaphoreType.DMA((n, 6)), pltpu.SemaphoreType.DMA((n, 6))]


def _allgather_rows(ref, is_halved, which):
    r = ref.shape[0]
    return _half(r, which) if is_halved else pl.ds(0, r)


def _allgather_first(ins, outs, send_sems, recv_sems, halved):
    x, y, c = _coords()
    my_chip = 2 * x + y
    cps = []
    for w in range(len(ins)):
        rows = _allgather_rows(ins[w], halved[w], c)
        for k, kind in enumerate(ICI_KINDS):
            cps.append(_remote(ins[w].at[rows], outs[w].at[my_chip, rows], send_sems.at[w, k], recv_sems.at[w, k],
                               _peer(kind, x, y, c)))
    return cps


def _allgather_start(ins, outs, send_sems, recv_sems, halved):
    for cp in _allgather_first(ins, outs, send_sems, recv_sems, halved):
        cp.start()


def _allgather_finish(ins, outs, send_sems, recv_sems, halved):
    x, y, c = _coords()
    me = (x, y, c)
    second = []
    for w in range(len(ins)):
        for k, kind in enumerate(ICI_KINDS):
            landed = outs[w].at[_chip_of(_peer(kind, x, y, c)), _allgather_rows(ins[w], halved[w], c)]
            _remote(landed, landed, send_sems.at[w, k], recv_sems.at[w, k], me).wait_recv()
            if halved[w]:
                cp = _remote(landed, landed, send_sems.at[w, 3 + k], recv_sems.at[w, 3 + k], _peer("c", x, y, c))
                cp.start()
                second.append(cp)
    for w in range(len(ins)):
        if halved[w]:
            for k, kind in enumerate(ICI_KINDS):
                other = outs[w].at[_chip_of(_peer(kind, x, y, c)), _allgather_rows(ins[w], True, 1 - c)]
                _remote(other, other, send_sems.at[w, 3 + k], recv_sems.at[w, 3 + k], me).wait_recv()
    for cp in _allgather_first(ins, outs, send_sems, recv_sems, halved) + second:
        cp.wait_send()


def _half_of(ref, by_cols, which):
    lead = (slice(None),) * (len(ref.shape) - 2)
    if by_cols:
        h = ref.shape[-1] // 2
        return ref.at[lead + (slice(None), pl.ds(pl.multiple_of(which * h, LANES), h))]
    return ref.at[lead + (_half(ref.shape[-2], which),)]


def _half_shape(shape, by_cols):
    return shape[:-1] + (shape[-1] // 2,) if by_cols else shape[:-2] + (shape[-2] // 2, shape[-1])


def grads_to_sibling(gs, by_cols, *, name):
    n = len(gs)

    def body(*refs):
        ins, outs = refs[:n], refs[n:2 * n]
        send_sems, recv_sems = refs[2 * n:]
        x, y, c = _coords()
        cps = []
        for w in range(n):
            cp = _remote(_half_of(ins[w], by_cols[w], 1 - c), outs[w], send_sems.at[w], recv_sems.at[w], _peer("c", x, y, c))
            cp.start()
            cps.append(cp)
        for cp in cps:
            cp.wait()

    return pl.pallas_call(
        body, in_specs=[ANY] * n, out_specs=[ANY] * n,
        out_shape=[SDS(_half_shape(a.shape, bc), a.dtype) for a, bc in zip(gs, by_cols)],
        scratch_shapes=[pltpu.SemaphoreType.DMA((n,)), pltpu.SemaphoreType.DMA((n,))], name=name)(*gs)


def _to_chips_shapes(ps):
    return [SDS((3,) + a.shape[1:], a.dtype) for a in ps]


def _to_chips_sems(n):
    return [pltpu.SemaphoreType.DMA((n, 3)), pltpu.SemaphoreType.DMA((n, 3))]


def _to_chips_copies(ins, outs, send_sems, recv_sems):
    x, y, c = _coords()
    cps = []
    for w in range(len(ins)):
        for k, kind in enumerate(ICI_KINDS):
            to = _peer(kind, x, y, c)
            cps.append(_remote(ins[w].at[_chip_of(to)], outs[w].at[k], send_sems.at[w, k], recv_sems.at[w, k], to))
    return cps


def _to_chips_start(ins, outs, send_sems, recv_sems):
    for cp in _to_chips_copies(ins, outs, send_sems, recv_sems):
        cp.start()


def _to_chips_finish(ins, outs, send_sems, recv_sems):
    for cp in _to_chips_copies(ins, outs, send_sems, recv_sems):
        cp.wait()


def _to_owners_shapes(ps):
    return [SDS((7, a.shape[1] // 2, a.shape[2]), a.dtype) for a in ps]


def _to_owners_sems(n):
    return [pltpu.SemaphoreType.DMA((n, 7)), pltpu.SemaphoreType.DMA((n, 7))]


def _to_owners_copies(ins, outs, send_sems, recv_sems):
    x, y, c = _coords()
    cps = []
    for w in range(len(ins)):
        rows = ins[w].shape[1]
        for k, kind in enumerate(ICI_KINDS):
            px, py, _ = _peer(kind, x, y, c)
            for h in range(2):
                cps.append(_remote(ins[w].at[2 * px + py, _half(rows, h)], outs[w].at[2 * k + c],
                                   send_sems.at[w, 2 * k + h], recv_sems.at[w, 2 * k + c], (px, py, h)))
        cps.append(_remote(ins[w].at[2 * x + y, _half(rows, 1 - c)], outs[w].at[6], send_sems.at[w, 6], recv_sems.at[w, 6],
                           _peer("c", x, y, c)))
    return cps


def _to_owners_start(ins, outs, send_sems, recv_sems):
    for cp in _to_owners_copies(ins, outs, send_sems, recv_sems):
        cp.start()


def _to_owners_finish(ins, outs, send_sems, recv_sems):
    for cp in _to_owners_copies(ins, outs, send_sems, recv_sems):
        cp.wait_send()
    for w in range(len(ins)):
        for slot in range(7):
            got = outs[w].at[slot]
            _remote(got, got, send_sems.at[w, slot], recv_sems.at[w, slot], _coords()).wait_recv()


EXCHANGES = {"to_chips": (_to_chips_shapes, _to_chips_sems, _to_chips_start, _to_chips_finish),
             "to_owners": (_to_owners_shapes, _to_owners_sems, _to_owners_start, _to_owners_finish)}


def halves_to_full(hs, by_cols, *, name):
    n = len(hs)

    def body(*refs):
        ins, outs = refs[:n], refs[n:2 * n]
        send_sems, recv_sems = refs[2 * n:]
        x, y, c = _coords()
        cps = []
        for w in range(n):
            cp = _remote(ins[w], _half_of(outs[w], by_cols[w], c), send_sems.at[w], recv_sems.at[w], _peer("c", x, y, c))
            cp.start()
            cps.append(cp)
        for cp in cps:
            cp.wait()

    return pl.pallas_call(
        body, in_specs=[ANY] * n, out_specs=[ANY] * n,
        out_shape=[SDS((a.shape[0], 2 * a.shape[1]) if bc else (2 * a.shape[0], a.shape[1]), a.dtype)
                   for a, bc in zip(hs, by_cols)],
        scratch_shapes=[pltpu.SemaphoreType.DMA((n,)), pltpu.SemaphoreType.DMA((n,))],
        name=name)(*hs)


def _row_tile(rows):
    for cand in (256, 192, 176, 128, 64, 32, 16):
        if rows % cand == 0:
            return cand
    return rows


def chip_sum(g, recv, c_arr, by_cols, *, name):
    _, r, cols = g.shape

    def body(c_ref, g_ref, r_ref, f_ref, b_ref):
        tot = g_ref[...] + r_ref[...]
        f_ref[...] = tot
        b_ref[...] = tot.astype(BF16)

    if by_cols:
        tc = 4 * LANES
        nblk = cols // 2 // tc
        shape = (4, r, cols // 2)
        blk = pl.BlockSpec((None, r, tc), lambda j, i, c_ref: (j, 0, i))
        mine = pl.BlockSpec((None, r, tc), lambda j, i, c_ref: (j, 0, c_ref[0] * nblk + i))
    else:
        tr = _row_tile(r // 2)
        nblk = r // 2 // tr
        shape = (4, r // 2, cols)
        blk = pl.BlockSpec((None, tr, cols), lambda j, i, c_ref: (j, i, 0))
        mine = pl.BlockSpec((None, tr, cols), lambda j, i, c_ref: (j, c_ref[0] * nblk + i, 0))
    grid_spec = pltpu.PrefetchScalarGridSpec(num_scalar_prefetch=1, grid=(4, nblk), in_specs=[mine, blk], out_specs=[blk, blk])
    return pl.pallas_call(body, grid_spec=grid_spec, out_shape=[SDS(shape, F32), SDS(shape, BF16)],
                          name=name, compiler_params=_cp(("parallel", "parallel")))(c_arr, g, recv)


def final_sum(pf, recv, chip_arr, *, name):
    _, h, cols = pf.shape
    tr = _row_tile(h)

    def body(chip_ref, p_ref, r_ref, o_ref):
        o_ref[...] = ((p_ref[...] + r_ref[0].astype(F32)) + r_ref[1].astype(F32)) + r_ref[2].astype(F32)

    grid_spec = pltpu.PrefetchScalarGridSpec(
        num_scalar_prefetch=1, grid=(h // tr,),
        in_specs=[pl.BlockSpec((None, tr, cols), lambda i, chip_ref: (chip_ref[0], i, 0)),
                  pl.BlockSpec((3, tr, cols), lambda i, chip_ref: (0, i, 0))],
        out_specs=pl.BlockSpec((tr, cols), lambda i, chip_ref: (i, 0)))
    return pl.pallas_call(body, grid_spec=grid_spec, out_shape=SDS((h, cols), F32), name=name,
                          compiler_params=_cp(("parallel",)))(chip_arr, pf, recv)


def owner_sum(g, recv, pos_arr, *, name):
    _, r, cols = g.shape
    h = r // 2
    tr = _row_tile(h)
    nblk = h // tr

    def body(pos_ref, g_ref, r_ref, o_ref):
        tot = g_ref[...]
        for slot in range(7):
            tot = tot + r_ref[slot].astype(F32)
        o_ref[...] = tot

    grid_spec = pltpu.PrefetchScalarGridSpec(
        num_scalar_prefetch=1, grid=(nblk,),
        in_specs=[pl.BlockSpec((None, tr, cols), lambda i, pos: (pos[0], pos[1] * nblk + i, 0)),
                  pl.BlockSpec((7, tr, cols), lambda i, pos: (0, i, 0))],
        out_specs=pl.BlockSpec((tr, cols), lambda i, pos: (i, 0)))
    return pl.pallas_call(body, grid_spec=grid_spec, out_shape=SDS((h, cols), F32), name=name,
                          compiler_params=_cp(("parallel",)))(pos_arr, g, recv)


def allreduce_small(v, *, name):
    rws, cols = v.shape

    def body(v_ref, all_ref, sum_ref, send_sems, recv_sems, local_sem):
        x, y, c = _coords()
        me, sibling = (x, y, c), (x, y, 1 - c)
        chips = [(1 - x, y), (x, 1 - y), (1 - x, 1 - y)]

        def rows(px, py, pc):
            return all_ref.at[pl.ds(pl.multiple_of((4 * px + 2 * py + pc) * rws, 8), rws), :]

        def copy(k, block, to, src=None):
            return _remote(rows(*block) if src is None else src, rows(*block), send_sems.at[k], recv_sems.at[k], to)

        mine = pltpu.make_async_copy(v_ref, rows(*me), local_sem)
        mine.start()
        first = [copy(0, me, sibling, src=v_ref)]
        first += [copy(1 + j, me, (*chip, c), src=v_ref) for j, chip in enumerate(chips)]
        for cp in first:
            cp.start()
        passed = [copy(4 + j, (*chip, c), sibling) for j, chip in enumerate(chips)]
        for j, chip in enumerate(chips):
            copy(1 + j, (*chip, c), me).wait_recv()
            passed[j].start()
        copy(0, sibling, me).wait_recv()
        for j, chip in enumerate(chips):
            copy(4 + j, (*chip, 1 - c), me).wait_recv()
        for cp in first + passed:
            cp.wait_send()
        mine.wait()
        tot = all_ref[0:rws, :]
        for dev in range(1, 8):
            tot = tot + all_ref[dev * rws:(dev + 1) * rws, :]
        sum_ref[...] = tot

    vm = pl.BlockSpec(memory_space=pltpu.VMEM)
    return pl.pallas_call(
        body, in_specs=[vm], out_specs=[vm, vm],
        out_shape=[SDS((8 * rws, cols), v.dtype), SDS((rws, cols), v.dtype)],
        scratch_shapes=[pltpu.SemaphoreType.DMA((7,)), pltpu.SemaphoreType.DMA((7,)), pltpu.SemaphoreType.DMA],
        name=name)(v)[1]


def _pack_rows(parts, rows):
    out = []
    for a, r in zip(parts, rows):
        flat = a.reshape(-1)
        flat = jnp.pad(flat, (0, r * LANES - flat.shape[0]))
        out.append(flat.reshape(r, LANES))
    return jnp.concatenate(out, axis=0)


def _unpack_rows(packed, shapes, rows):
    out, at = [], 0
    for shp, r in zip(shapes, rows):
        size = int(np.prod(shp))
        out.append(packed[at:at + r].reshape(-1)[:size].reshape(shp))
        at += r
    return out


def kernel(x, g_pre_mix, w_in, b_forget, w_o_fox, w_o_dil, w_out, g_post_mix, g_pre_ffn, w_up, conv_w, conv_b, w_down, g_post_ffn, loss_target, m_g_pre_mix, m_w_in, m_b_forget, m_w_o_fox, m_w_o_dil, m_w_out, m_g_post_mix, m_g_pre_ffn, m_w_up, m_conv_w, m_conv_b, m_w_down, m_g_post_ffn, v_g_pre_mix, v_w_in, v_b_forget, v_w_o_fox, v_w_o_dil, v_w_out, v_g_post_mix, v_g_pre_ffn, v_w_up, v_conv_w, v_conv_b, v_w_down, v_g_post_ffn):
    xi, yi, ci = _coords()
    chip = 2 * xi + yi
    c_arr = jnp.reshape(ci, (1,)).astype(jnp.int32)
    chip_arr = jnp.reshape(chip, (1,)).astype(jnp.int32)
    xs = x[0]
    target = loss_target[0]
    s, d = xs.shape
    f_half = w_down.shape[1] * 4
    cols_in = w_in.shape[2]

    big = (w_in, w_o_fox, w_o_dil, w_out, w_up, w_down)
    shards = [w[0].astype(BF16) for w in big]
    a_in = allgather_balanced(shards[0], name="allgather_w_in")
    w_in_full = jnp.concatenate([jnp.where(chip == j, shards[0], a_in[j]) for j in range(4)], axis=1)
    nf = N_HEADS
    e_a, e_b = 3 * ATT_W, 3 * ATT_W + nf
    wz = jnp.concatenate([w_in_full[:, :e_a], w_in_full[:, e_b:]], axis=1)
    wf = jnp.pad(w_in_full[:, e_a:e_b], ((0, 0), (0, LANES - nf)))
    cb = conv_b
    bfo = jnp.pad(b_forget, ((0, 0), (0, LANES - nf)))

    h1 = rmsnorm_fwd(xs, g_pre_mix)
    z = mm([(h1, d, 0)], [(wz, d, 0)], nt=False, out_dtype=BF16, tm=s, tn=512, name="in_proj")
    fa = mm([(h1, d, 0)], [(wf, d, 0)], nt=False, out_dtype=F32, tm=s, tn=LANES, name="in_proj_forget")
    q_aug, k_aug, v_aug = fox_prep(z, fa, bfo)
    later = shards[1:] + [conv_w[0]]
    ya, lse_a, *late = fox_fwd(q_aug, k_aug, v_aug, gather=later, halved=[True] * 5 + [False], hps=4)
    a_of, a_od, a_out, a_up, a_down, a_cw = [
        lax.dynamic_update_index_in_dim(a4, own, chip, 0) for a4, own in zip(late, later)]
    cw = jnp.concatenate([a_cw[j] for j in range(4)], axis=1)
    wo_a = jnp.concatenate([a_of[j] for j in range(4)], axis=1)
    wo_b = jnp.concatenate([a_od[j] for j in range(4)], axis=1)
    w_o = a_out.reshape(d, d)
    w_dn = a_down.reshape(f_half, d)
    wu_a = jnp.concatenate([a_up[0], a_up[1]], axis=1)
    wu_b = jnp.concatenate([a_up[2], a_up[3]], axis=1)
    cos_t, sin_t = rope_cos_sin(s)
    yb, lse_b = dil_fwd_all(z, cos_t, sin_t)
    pa, pb, mixed = gate_mix(ya, yb, wo_a, wo_b, z)
    y1, x1, h2 = proj_norm_res(mixed, w_o, g_post_mix, xs, g_pre_ffn, tm=1024, name="out_proj")
    ua, ub, conv_a, conv_bh, mid = ffn_up(h2, wu_a, wu_b, cw, cb)
    dout, dy2, gg_post_ffn, sq = proj_norm_loss(mid, w_dn, g_post_ffn, x1, target, name="down_proj")

    dmid = mm([(dy2, d, 0)], [(w_dn, d, 0)], nt=True, out_dtype=BF16, tm=2048, tn=f_half // 2, name="down_dgrad")
    dw_down, dw_down16 = wgrad((mid, f_half, 0), dy2, tk=f_half // 2, tn=1024, ts=2048, name="down_wgrad", bf16_copy=True)
    dua, dub, gc_a, gc_b = ffn_bwd(dmid, ua, ub, conv_a, conv_bh, cw)
    dx1, dy1, gg_pre_ffn, gg_post_mix = mm_norm_bwd(
        [(dua, f_half, 0), (dub, f_half, 0)], [(wu_a, f_half, 0), (wu_b, f_half, 0)],
        [(x1, g_pre_ffn, dout, F32), (y1, g_post_mix, None, BF16)], name="up_dgrad")
    dw_up = None
    for k, du in enumerate((dua, dub)):
        dw_up = wgrad((h2, d, 0), du, tk=1024, tn=f_half // 2, ts=2048, name=f"up_wgrad_{k}", chip_major=True,
                      slabs=(4, 2 * k), into=dw_up, bf16_copy=True)
    g_ffn = [(dw_up[0], dw_up[1]), (dw_down.reshape(4, f_half // 4, d), dw_down16.reshape(4, f_half // 4, d))]
    dw_out, dw_out16 = wgrad((mixed, d, 0), dy1, tk=1024, tn=1024, ts=2048, name="out_wgrad", bf16_copy=True)
    dpa, dpb, dz_g, dya, dyb, dd_a = mix_bwd(dy1, w_o, z, pa, pb, wo_a, wo_b, ya)
    by_chip_cols = lambda a: jnp.stack([a[:, j * (d // 4):(j + 1) * (d // 4)] for j in range(4)], axis=0)
    dw_of = [by_chip_cols(a) for a in wgrad((ya, ATT_W, 0), dpa, tk=ATT_W, tn=d, ts=1024, name="fox_o_wgrad", bf16_copy=True)]
    dw_od = [by_chip_cols(a) for a in wgrad((yb, ATT_W, 0), dpb, tk=ATT_W, tn=d, ts=1024, name="dil_o_wgrad", bf16_copy=True)]
    g_mix = [dw_of, dw_od, (dw_out.reshape(4, d // 4, d), dw_out16.reshape(4, d // 4, d))]
    dq_aug, dk_aug, dv_a, *got_ffn = fox_bwd(q_aug, k_aug, z, dya, lse_a, dd_a, exchange=[g[1] for g in g_ffn], kind="to_owners")
    dz_a, dfa, gg_bf = fox_post(dq_aug, dk_aug, dv_a, fa, bfo)
    *dz_b, got_of, got_od, got_out = dil_bwd_all(z, cos_t, sin_t, dyb, lse_b, yb, exchange=[g[1] for g in g_mix],
                                                 kind="to_owners")
    got_mix = [got_of, got_od, got_out]
    dwt_a = wgrad((dz_a, e_a, 0), h1, tk=e_a // 2, tn=d, ts=2048, name="in_wgrad_a")
    dwt_b = [wgrad((part, ATT_W, 0), h1, tk=ATT_W, tn=d, ts=2048, name=f"in_wgrad_b{k}") for k, part in enumerate(dz_b)]
    dwt_g = wgrad((dz_g, 2 * d, 0), h1, tk=d, tn=d, ts=2048, name="in_wgrad_g")
    dwt_f = wgrad((dfa, LANES, 0), h1, tk=LANES, tn=d, ts=2048, name="in_wgrad_f")
    dwt_full = jnp.concatenate([dwt_a, dwt_f[:nf], *dwt_b, dwt_g], axis=0)
    dw_in = jnp.stack([dwt_full[j * cols_in:(j + 1) * cols_in] for j in range(4)], axis=0)
    from_sib = grads_to_sibling([dw_in], [True], name="grads_to_sibling_in")
    sum_in = chip_sum(dw_in, from_sib[0], c_arr, True, name="chip_sum_w_in")
    grad_x, gg_pre_mix, got_in = mm_norm_bwd(
        [(dz_a, e_a, 0), *[(part, ATT_W, 0) for part in dz_b], (dz_g, d, 0), (dz_g, d, 1), (dfa, LANES, 0)],
        [(wz, e_a, 0), *[(wz, ATT_W, Z_QB + k) for k in range(3)], (wz, d, 3), (wz, d, 4), (wf, LANES, 0)],
        [(xs, g_pre_mix, dx1, F32)], exchange=[sum_in[1]], name="in_dgrad")

    names = ("w_in", "w_o_fox", "w_o_dil", "w_out", "w_up", "w_down")
    pos_arr = jnp.concatenate([chip_arr, c_arr])
    halves = [final_sum(sum_in[0], got_in, chip_arr, name="final_sum_w_in")] + [
        owner_sum(g[0], got, pos_arr, name=f"owner_sum_{nm}") for g, got, nm in zip(g_mix + g_ffn, got_mix + got_ffn, names[1:])]
    from_half = halves_to_full(halves, [True] + [False] * 5, name="halves_to_full")
    g_big = [None] + [lax.dynamic_update_slice_in_dim(full, mine, ci * mine.shape[0], axis=0)
                      for full, mine in zip(from_half[1:], halves[1:])]
    upd_big = [adamw(w[0], g, m[0], v[0], name=f"adamw_{nm}") for w, g, m, v, nm in list(zip(
        big, g_big, (m_w_in, m_w_o_fox, m_w_o_dil, m_w_out, m_w_up, m_w_down),
        (v_w_in, v_w_o_fox, v_w_o_dil, v_w_out, v_w_up, v_w_down), names))[1:]]
    to_t = lambda a: jnp.transpose(a, (2, 0, 1))
    from_t = lambda a: jnp.transpose(a, (1, 2, 0))
    *upd_in, g_in_t = adamw_rows_view(to_t(w_in), halves[0], from_half[0], to_t(m_w_in), to_t(v_w_in), c_arr,
                                      name="adamw_w_in")

    g_cw_loc = jnp.concatenate([gc_a[0:3], gc_b[0:3]], axis=1)
    g_cb_loc = jnp.concatenate([gc_a[3:4], gc_b[3:4]], axis=1)
    small_loc = [gg_pre_mix, gg_post_mix, gg_pre_ffn, gg_post_ffn, g_cb_loc, gg_bf[:, :nf], g_cw_loc, sq * (0.5 / d)]
    red_rows = (8, 8, 8, 8, 48, 8, 136, 8)
    red = allreduce_small(_pack_rows(small_loc, red_rows), name="allreduce_small")
    g_pm, g_qm, g_pf, g_qf, g_cb, g_bf, g_cw_full, loss_11 = _unpack_rows(red, [a.shape for a in small_loc], red_rows)
    loss = loss_11[0, 0]
    cols_cw = conv_w.shape[2]
    g_cw = lax.dynamic_slice_in_dim(g_cw_full, chip * cols_cw, cols_cw, axis=1)
    small_w = (g_pre_mix, g_post_mix, g_pre_ffn, g_post_ffn, conv_b, b_forget, conv_w[0])
    small_m = (m_g_pre_mix, m_g_post_mix, m_g_pre_ffn, m_g_post_ffn, m_conv_b, m_b_forget, m_conv_w[0])
    small_v = (v_g_pre_mix, v_g_post_mix, v_g_pre_ffn, v_g_post_ffn, v_conv_b, v_b_forget, v_conv_w[0])
    small_g = (g_pm, g_qm, g_pf, g_qf, g_cb, g_bf, g_cw)
    small_names = ("g_pre_mix", "g_post_mix", "g_pre_ffn", "g_post_ffn", "conv_b", "b_forget", "conv_w")
    per_param = [adamw(w, g, m, v, name=f"adamw_{nm}") for w, g, m, v, nm in zip(small_w, small_g, small_m, small_v, small_names)]
    upd_small = [[u[j] for u in per_param] for j in range(3)]

    order = ("g_pre_mix", "w_in", "b_forget", "w_o_fox", "w_o_dil", "w_out", "g_post_mix", "g_pre_ffn", "w_up", "conv_w",
             "conv_b", "w_down", "g_post_ffn")
    grads, deltas, new_ms, new_vs = {}, {}, {}, {}
    grads["w_in"] = from_t(g_in_t)
    deltas["w_in"], new_ms["w_in"], new_vs["w_in"] = (from_t(a) for a in upd_in)
    for k, nm in enumerate(names[1:]):
        grads[nm] = g_big[k + 1][None]
        deltas[nm], new_ms[nm], new_vs[nm] = (a[None] for a in upd_big[k])
    for k, nm in enumerate(small_names):
        lead = (lambda a: a[None]) if nm == "conv_w" else (lambda a: a)
        grads[nm] = lead(small_g[k])
        deltas[nm], new_ms[nm], new_vs[nm] = (lead(upd_small[j][k]) for j in range(3))
    return (loss, grad_x[None], *[grads[nm] for nm in order], *[deltas[nm] for nm in order],
            *[new_ms[nm] for nm in order], *[new_vs[nm] for nm in order])
```

```python
import functools
import math

import numpy as np
import jax
import jax.numpy as jnp
from jax import lax
from jax.experimental import pallas as pl
from jax.experimental.pallas import tpu as pltpu

F32 = jnp.float32
BF16 = jnp.bfloat16
SDS = jax.ShapeDtypeStruct
MESH = pl.DeviceIdType.MESH

HEAD_DIM = 64
N_HEADS = 8
LANES = 128
ATT_W = N_HEADS * HEAD_DIM
DIL_PATTERNS = ((128, 1), (512, 4), (2048, 16))
DIL_BLK = 128
ROPE_DIM = HEAD_DIM // 4
ROPE_THETA = 500000.0
RMS_EPS = 1e-6
NEG = -1e30
QK_SCALE = 1.0 / math.sqrt(HEAD_DIM)
ADAM_LR, ADAM_B1, ADAM_B2, ADAM_EPS, ADAM_WD, ADAM_STEP = 0.001, 0.9, 0.999, 1e-08, 0.01, 10
VMEM_LIMIT = 56 * 1024 * 1024

Z_QA, Z_KA, Z_VA, Z_QB, Z_KB, Z_VB = 0, 1, 2, 3, 4, 5
Z_W = 5120


def _cp(sem):
    return pltpu.CompilerParams(dimension_semantics=sem, vmem_limit_bytes=VMEM_LIMIT)


def _nt(a, b):
    return lax.dot_general(a, b, (((1,), (1,)), ((), ())), preferred_element_type=F32)


def _tn(a, b):
    return lax.dot_general(a, b, (((0,), (0,)), ((), ())), preferred_element_type=F32)


def _nn(a, b):
    return jnp.dot(a, b, preferred_element_type=F32)


def _lane(shape):
    return lax.broadcasted_iota(jnp.int32, shape, 1)


def _row(shape):
    return lax.broadcasted_iota(jnp.int32, shape, 0)


def rmsnorm_fwd(x, g, *, tm=1024):
    s, d = x.shape

    def body(x_ref, g_ref, h_ref):
        xv = x_ref[...]
        inv = lax.rsqrt(jnp.mean(xv * xv, axis=-1, keepdims=True) + RMS_EPS)
        h_ref[...] = (xv * inv * g_ref[...]).astype(h_ref.dtype)

    return pl.pallas_call(
        body, grid=(s // tm,),
        in_specs=[pl.BlockSpec((tm, d), lambda i: (i, 0)), pl.BlockSpec((1, d), lambda i: (0, 0))],
        out_specs=pl.BlockSpec((tm, d), lambda i: (i, 0)),
        out_shape=SDS((s, d), BF16), name="rmsnorm_fwd", compiler_params=_cp(("parallel",)))(x, g)


def mm(a_views, b_views, *, nt, out_dtype, tm, tn, name):
    n_p = len(a_views)
    m = a_views[0][0].shape[0]
    n = b_views[0][0].shape[0] if nt else b_views[0][0].shape[1]

    def body(*refs):
        o_ref = refs[-1]
        acc = None
        for p in range(n_p):
            av = refs[p][...].astype(BF16)
            bv = refs[n_p + p][...].astype(BF16)
            dv = _nt(av, bv) if nt else _nn(av, bv)
            acc = dv if acc is None else acc + dv
        o_ref[...] = acc.astype(o_ref.dtype)

    in_specs = []
    for arr, w, blk in a_views:
        in_specs.append(pl.BlockSpec((tm, w), functools.partial(lambda i, j, blk: (i, blk), blk=blk)))
    for arr, w, blk in b_views:
        if nt:
            in_specs.append(pl.BlockSpec((tn, w), functools.partial(lambda i, j, blk: (j, blk), blk=blk)))
        else:
            in_specs.append(pl.BlockSpec((w, tn), lambda i, j: (0, j)))
    return pl.pallas_call(
        body, grid=(m // tm, n // tn), in_specs=in_specs,
        out_specs=pl.BlockSpec((tm, tn), lambda i, j: (i, j)),
        out_shape=SDS((m, n), out_dtype), name=name,
        compiler_params=_cp(("parallel", "parallel")))(*[a[0] for a in a_views], *[b[0] for b in b_views])


def wgrad(a_view, g, *, tk, tn, ts, name, chip_major=False, slabs=None, into=None, bf16_copy=False):
    arr, ka, blk = a_view
    s, n = g.shape
    ns = s // ts
    total, first = slabs if slabs else (n // tn, 0)
    n_into = 0 if into is None else (2 if bf16_copy else 1)

    def body(a_ref, g_ref, *rest):
        o_ref = rest[n_into]

        @pl.when(pl.program_id(2) == 0)
        def _():
            o_ref[...] = jnp.zeros_like(o_ref)

        o_ref[...] += _tn(a_ref[...].astype(BF16), g_ref[...].astype(BF16))
        if bf16_copy:
            @pl.when(pl.program_id(2) == ns - 1)
            def _():
                rest[n_into + 1][...] = o_ref[...].astype(BF16)

    if chip_major:
        out_spec = pl.BlockSpec((None, tk, tn), lambda i, j, k: (first + j, i, 0))
        shape = (total, ka, tn)
    else:
        out_spec = pl.BlockSpec((tk, tn), lambda i, j, k: (i, j))
        shape = (ka, n)
    in_specs = [pl.BlockSpec((ts, tk), lambda i, j, k: (k, blk * (ka // tk) + i)),
                pl.BlockSpec((ts, tn), lambda i, j, k: (k, j))]
    args = [arr, g]
    if into is not None:
        earlier = list(into) if bf16_copy else [into]
        in_specs += [pl.BlockSpec(memory_space=pl.ANY)] * len(earlier)
        args += earlier
    out = pl.pallas_call(
        body, grid=(ka // tk, n // tn, ns), in_specs=in_specs,
        out_specs=[out_spec, out_spec] if bf16_copy else out_spec,
        out_shape=[SDS(shape, F32), SDS(shape, BF16)] if bf16_copy else SDS(shape, F32), name=name,
        input_output_aliases={2 + k: k for k in range(n_into)},
        compiler_params=_cp(("parallel", "parallel", "arbitrary")))(*args)
    return out


def _norm_bwd_rows(dh, xh, inv, g):
    dxh = dh * g
    dx = inv * (dxh - xh * jnp.mean(dxh * xh, axis=-1, keepdims=True))
    return dx, jnp.sum((dh * xh).reshape(dh.shape[0] // 8, 8, dh.shape[1]), axis=0)


def proj_norm_res(a, w, g, xres, g_next, *, tm=512, name):
    s, k = a.shape
    d = w.shape[1]

    def body(a_ref, w_ref, g_ref, x_ref, gn_ref, y_ref, o_ref, h_ref):
        y = _nn(a_ref[...], w_ref[...])
        inv = lax.rsqrt(jnp.mean(y * y, axis=-1, keepdims=True) + RMS_EPS)
        xn = x_ref[...] + y * inv * g_ref[...]
        y_ref[...] = y
        o_ref[...] = xn
        inv_n = lax.rsqrt(jnp.mean(xn * xn, axis=-1, keepdims=True) + RMS_EPS)
        h_ref[...] = (xn * inv_n * gn_ref[...]).astype(h_ref.dtype)

    row = pl.BlockSpec((tm, d), lambda i: (i, 0))
    vec = pl.BlockSpec((1, d), lambda i: (0, 0))
    return pl.pallas_call(
        body, grid=(s // tm,),
        in_specs=[pl.BlockSpec((tm, k), lambda i: (i, 0)), pl.BlockSpec((k, d), lambda i: (0, 0)), vec, row, vec],
        out_specs=[row, row, row], out_shape=[SDS((s, d), F32), SDS((s, d), F32), SDS((s, d), BF16)], name=name,
        compiler_params=_cp(("parallel",)))(a, w, g, xres, g_next)


def proj_norm_loss(a, w, g, xres, target, *, tm=512, name):
    s, k = a.shape
    d = w.shape[1]
    n = s // tm

    def body(a_ref, w_ref, g_ref, x_ref, t_ref, do_ref, dy_ref, dg_ref, l_ref, acc):
        i = pl.program_id(0)

        @pl.when(i == 0)
        def _():
            acc[...] = jnp.zeros_like(acc)
            l_ref[...] = jnp.zeros_like(l_ref)

        y = _nn(a_ref[...], w_ref[...])
        inv = lax.rsqrt(jnp.mean(y * y, axis=-1, keepdims=True) + RMS_EPS)
        yh = y * inv
        err = x_ref[...] + yh * g_ref[...] - t_ref[...]
        dout = err * (1.0 / d)
        do_ref[...] = dout
        l_ref[...] += jnp.sum(jnp.sum(err * err, axis=1, keepdims=True), axis=0, keepdims=True)
        dy, part = _norm_bwd_rows(dout, yh, inv, g_ref[...])
        dy_ref[...] = dy.astype(dy_ref.dtype)
        acc[...] += part

        @pl.when(i == n - 1)
        def _():
            dg_ref[...] = jnp.sum(acc[...], axis=0, keepdims=True)

    row = pl.BlockSpec((tm, d), lambda i: (i, 0))
    vec = pl.BlockSpec((1, d), lambda i: (0, 0))
    return pl.pallas_call(
        body, grid=(n,),
        in_specs=[pl.BlockSpec((tm, k), lambda i: (i, 0)), pl.BlockSpec((k, d), lambda i: (0, 0)), vec, row, row],
        out_specs=[row, row, vec, pl.BlockSpec((1, 1), lambda i: (0, 0))],
        out_shape=[SDS((s, d), F32), SDS((s, d), BF16), SDS((1, d), F32), SDS((1, 1), F32)],
        scratch_shapes=[pltpu.VMEM((8, d), F32)], name=name, compiler_params=_cp(("arbitrary",)))(a, w, g, xres, target)


def mm_norm_bwd(a_views, b_views, stages, exchange=(), *, tm=256, name):
    n_p, n_s, ne = len(a_views), len(stages), len(exchange)
    s = a_views[0][0].shape[0]
    d = b_views[0][0].shape[0]
    n = s // tm
    has_res = [st[2] is not None for st in stages]

    def body(*refs):
        a_refs, b_refs = refs[:n_p], refs[n_p:2 * n_p]
        at = 2 * n_p
        st_refs = []
        for k in range(n_s):
            cnt = 3 if has_res[k] else 2
            st_refs.append(refs[at:at + cnt])
            at += cnt
        e_ins = refs[at:at + ne]
        at += ne
        dx_refs, dg_refs = refs[at:at + n_s], refs[at + n_s:at + 2 * n_s]
        at += 2 * n_s
        e_outs = refs[at:at + ne]
        at += ne
        accs = refs[at:at + n_s]
        comm = (e_ins, e_outs) + tuple(refs[at + n_s:])
        i = pl.program_id(0)

        @pl.when(i == 0)
        def _():
            for acc in accs:
                acc[...] = jnp.zeros_like(acc)
            if ne:
                _to_chips_start(*comm)

        dh = None
        for p in range(n_p):
            part = _nt(a_refs[p][...].astype(BF16), b_refs[p][...].astype(BF16))
            dh = part if dh is None else dh + part
        for k in range(n_s):
            xv = st_refs[k][0][...]
            inv = lax.rsqrt(jnp.mean(xv * xv, axis=-1, keepdims=True) + RMS_EPS)
            dx, part = _norm_bwd_rows(dh, xv * inv, inv, st_refs[k][1][...])
            if has_res[k]:
                dx = dx + st_refs[k][2][...]
            dx_refs[k][...] = dx.astype(dx_refs[k].dtype)
            accs[k][...] += part
            dh = dx

        @pl.when(i == n - 1)
        def _():
            for k in range(n_s):
                dg_refs[k][...] = jnp.sum(accs[k][...], axis=0, keepdims=True)
            if ne:
                _to_chips_finish(*comm)

    row = pl.BlockSpec((tm, d), lambda i: (i, 0))
    vec = pl.BlockSpec((1, d), lambda i: (0, 0))
    in_specs, args = [], []
    for arr, w, blk in a_views:
        in_specs.append(pl.BlockSpec((tm, w), functools.partial(lambda i, blk: (i, blk), blk=blk)))
        args.append(arr)
    for arr, w, blk in b_views:
        in_specs.append(pl.BlockSpec((d, w), functools.partial(lambda i, blk: (0, blk), blk=blk)))
        args.append(arr)
    for x, g, res, _ in stages:
        in_specs += [row, vec] + ([row] if res is not None else [])
        args += [x, g] + ([res] if res is not None else [])
    return pl.pallas_call(
        body, grid=(n,), in_specs=in_specs + [ANY] * ne,
        out_specs=[row] * n_s + [vec] * n_s + [ANY] * ne,
        out_shape=[SDS((s, d), st[3]) for st in stages] + [SDS((1, d), F32)] * n_s + _to_chips_shapes(exchange),
        scratch_shapes=[pltpu.VMEM((8, d), F32)] * n_s + (_to_chips_sems(ne) if ne else []), name=name,
        compiler_params=_cp(("arbitrary",)))(*args, *exchange)


def _split3(v):
    hi = v.astype(BF16).astype(F32)
    r = v - hi
    mid = r.astype(BF16).astype(F32)
    lo = (r - mid).astype(BF16).astype(F32)
    return hi, mid, lo


def _tri(n, upper):
    r = np.arange(n)
    m = (r[:, None] <= r[None, :]) if upper else (r[:, None] >= r[None, :])
    return jnp.asarray(m.astype(np.float32))


def fox_prep(z, fa, bfo, *, tb=512):
    s = z.shape[0]
    n = s // tb

    def body(q_ref, k_ref, v_ref, fa_ref, b_ref, tri_ref, qa_ref, ka_ref, va_ref, carry):
        @pl.when(pl.program_id(0) == 0)
        def _():
            carry[...] = jnp.zeros_like(carry)

        xv = fa_ref[...] + b_ref[...]
        logf = jnp.minimum(xv, 0.0) - jnp.log(1.0 + jnp.exp(-jnp.abs(xv)))
        csum = jnp.dot(tri_ref[...], logf, preferred_element_type=F32, precision=lax.Precision.HIGHEST) + carry[0:1, :]
        carry[0:1, :] = csum[tb - 1:tb, :]
        lane = _lane((tb, LANES))
        for h in range(N_HEADS):
            hi, mid, lo = _split3(csum[:, h:h + 1])
            pair = (h // 2) * LANES
            qv = q_ref[:, pair:pair + LANES].astype(F32)
            kv = k_ref[:, pair:pair + LANES].astype(F32)
            vv = v_ref[:, pair:pair + LANES].astype(F32)
            if h % 2:
                qv = pltpu.roll(qv, 64, axis=1)
                kv = pltpu.roll(kv, 64, axis=1)
                vv = pltpu.roll(vv, 64, axis=1)
            va_ref[:, h * LANES:(h + 1) * LANES] = jnp.where(lane < 64, vv, jnp.where(lane == 64, 1.0, 0.0)).astype(BF16)
            one = jnp.where((lane >= 67) & (lane < 70), 1.0, 0.0)
            q_x = jnp.where(lane == 64, hi, jnp.where(lane == 65, mid, jnp.where(lane == 66, lo, one)))
            one = jnp.where((lane >= 64) & (lane < 67), 1.0, 0.0)
            k_x = jnp.where(lane == 67, -hi, jnp.where(lane == 68, -mid, jnp.where(lane == 69, -lo, one)))
            qa_ref[:, h * LANES:(h + 1) * LANES] = jnp.where(lane < 64, qv * QK_SCALE, q_x).astype(BF16)
            ka_ref[:, h * LANES:(h + 1) * LANES] = jnp.where(lane < 64, kv, k_x).astype(BF16)

    return pl.pallas_call(
        body, grid=(n,),
        in_specs=[pl.BlockSpec((tb, ATT_W), lambda i: (i, Z_QA)), pl.BlockSpec((tb, ATT_W), lambda i: (i, Z_KA)),
                  pl.BlockSpec((tb, ATT_W), lambda i: (i, Z_VA)),
                  pl.BlockSpec((tb, LANES), lambda i: (i, 0)), pl.BlockSpec((1, LANES), lambda i: (0, 0)),
                  pl.BlockSpec((tb, tb), lambda i: (0, 0))],
        out_specs=[pl.BlockSpec((tb, N_HEADS * LANES), lambda i: (i, 0))] * 3,
        out_shape=[SDS((s, N_HEADS * LANES), BF16)] * 3,
        scratch_shapes=[pltpu.VMEM((8, LANES), F32)],
        name="fox_prep", compiler_params=_cp(("arbitrary",)))(z, z, z, fa, bfo, _tri(tb, False))


def _causal_pairs(n, k_major):
    if k_major:
        pairs = [(qi, kj) for kj in range(n) for qi in range(kj, n)]
    else:
        pairs = [(qi, kj) for qi in range(n) for kj in range(qi + 1)]
    return (jnp.asarray([p[0] for p in pairs], jnp.int32), jnp.asarray([p[1] for p in pairs], jnp.int32), len(pairs))


def fox_fwd(q_aug, k_aug, v_aug, gather=(), halved=(), *, t=1024, hps=4):
    s = v_aug.shape[0]
    qi_arr, kj_arr, n_pairs = _causal_pairs(s // t, False)
    ng = len(gather)
    n_groups = N_HEADS // hps

    def body(qi_ref, kj_ref, q_ref, k_ref, v_ref, *rest):
        g_ins, (o_ref, lse_ref), g_outs = rest[:ng], rest[ng:ng + 2], rest[ng + 2:2 * ng + 2]
        m_scr, acc_scr = rest[2 * ng + 2:2 * ng + 4]
        comm = (g_ins, g_outs) + tuple(rest[2 * ng + 4:]) + (list(halved),)
        step = pl.program_id(1)
        qi = qi_ref[step]
        kj = kj_ref[step]
        if ng:
            @pl.when((pl.program_id(0) == 0) & (step == 0))
            def _():
                _allgather_start(*comm)

        @pl.when(kj == 0)
        def _():
            m_scr[...] = jnp.full_like(m_scr, NEG)
            acc_scr[...] = jnp.zeros_like(acc_scr)

        def update(qs, ks, masked):
            nq, nk = qs.stop - qs.start, ks.stop - ks.start
            for i in range(hps):
                own = slice(i * LANES, (i + 1) * LANES)
                sc = _nt(q_ref[qs, own], k_ref[ks, own])
                if masked:
                    sc = jnp.where(_row((nq, nk)) >= _lane((nq, nk)), sc, NEG)
                m_prev = m_scr[i, qs]
                m_new = jnp.maximum(m_prev, jnp.max(sc, axis=-1, keepdims=True))
                p = jnp.exp((sc - jnp.tile(m_new, (1, nk // LANES))).astype(BF16))
                acc_scr[i, qs] = jnp.exp(m_prev - m_new) * acc_scr[i, qs] + _nn(p, v_ref[ks, own])
                m_scr[i, qs] = m_new

        whole, upper, lower = slice(0, t), slice(0, t // 2), slice(t // 2, t)

        @pl.when(kj < qi)
        def _():
            update(whole, whole, False)

        @pl.when(kj == qi)
        def _():
            update(upper, upper, True)
            update(lower, upper, False)
            update(lower, lower, True)
            lane = _lane((t, LANES))
            for pr in range(hps // 2):
                den = [acc_scr[2 * pr + i][:, 64:65] for i in range(2)]
                o_ref[:, pr * LANES:(pr + 1) * LANES] = jnp.where(
                    lane < 64, acc_scr[2 * pr] / den[0], pltpu.roll(acc_scr[2 * pr + 1] / den[1], 64, axis=1)).astype(o_ref.dtype)
                lse_ref[:, pr * LANES:(pr + 1) * LANES] = jnp.where(
                    lane < 64, m_scr[2 * pr] + jnp.log(den[0]), m_scr[2 * pr + 1] + jnp.log(den[1]))

        if ng:
            @pl.when((pl.program_id(0) == n_groups - 1) & (step == n_pairs - 1))
            def _():
                _allgather_finish(*comm)

    wide = hps * LANES
    grid_spec = pltpu.PrefetchScalarGridSpec(
        num_scalar_prefetch=2, grid=(n_groups, n_pairs),
        in_specs=[pl.BlockSpec((t, wide), lambda hg, st, qi, kj: (qi[st], hg)),
                  pl.BlockSpec((t, wide), lambda hg, st, qi, kj: (kj[st], hg)),
                  pl.BlockSpec((t, wide), lambda hg, st, qi, kj: (kj[st], hg))] + [ANY] * ng,
        out_specs=[pl.BlockSpec((t, wide // 2), lambda hg, st, qi, kj: (qi[st], hg))] * 2 + [ANY] * ng,
        scratch_shapes=[pltpu.VMEM((hps, t, LANES), F32)] * 2 + (_allgather_sems(ng) if ng else []))
    return pl.pallas_call(
        body, grid_spec=grid_spec, out_shape=[SDS((s, ATT_W), BF16), SDS((s, ATT_W), F32)] + _allgather_shapes(gather),
        name="fox_fwd", compiler_params=_cp(("arbitrary", "arbitrary")))(qi_arr, kj_arr, q_aug, k_aug, v_aug, *gather)


def fox_bwd(q_aug, k_aug, z, dy, lse, dd, exchange=(), kind="to_chips", *, t=1024, hps=4):
    s = z.shape[0]
    qi_arr, kj_arr, n_pairs = _causal_pairs(s // t, True)
    ne = len(exchange)
    n_groups = N_HEADS // hps
    x_shapes, x_sems, x_start, x_finish = EXCHANGES[kind]

    def body(qi_ref, kj_ref, q_ref, k_ref, v_ref, do_ref, lse_ref, dd_ref, *rest):
        e_ins, (dq_ref, dk_ref, dv_ref), e_outs = rest[:ne], rest[ne:ne + 3], rest[ne + 3:2 * ne + 3]
        comm = (e_ins, e_outs) + tuple(rest[2 * ne + 3:])
        step = pl.program_id(1)
        qi = qi_ref[step]
        kj = kj_ref[step]
        if ne:
            @pl.when((pl.program_id(0) == 0) & (step == 0))
            def _():
                x_start(*comm)

        @pl.when(step == 0)
        def _():
            dq_ref[...] = jnp.zeros_like(dq_ref)

        @pl.when(qi == kj)
        def _():
            dk_ref[...] = jnp.zeros_like(dk_ref)
            dv_ref[...] = jnp.zeros_like(dv_ref)

        def update(qs, ks, masked):
            nq, nk = qs.stop - qs.start, ks.stop - ks.start
            lane = _lane((nq, LANES))
            rows = pl.ds(pl.multiple_of(qi * t + qs.start, nq), nq)
            for pr in range(hps // 2):
                pair = slice(pr * LANES, (pr + 1) * LANES)
                dov = do_ref[qs, pair]
                dv_new = None
                for i in range(2):
                    head = (lane < 64) if i == 0 else (lane >= 64)
                    own = slice((2 * pr + i) * LANES, (2 * pr + i + 1) * LANES)
                    col = slice(pr * LANES + i * 64, pr * LANES + i * 64 + 1)
                    qv = q_ref[qs, own]
                    kv = k_ref[ks, own]
                    sc = _nt(qv, kv)
                    if masked:
                        sc = jnp.where(_row((nq, nk)) >= _lane((nq, nk)), sc, NEG)
                    p = jnp.exp(sc - lse_ref[qs, col])
                    dp = _nt(jnp.where(head, dov, jnp.zeros_like(dov)), v_ref[ks, pair])
                    ds = (p * (dp - dd_ref[qs, col])).astype(BF16)
                    dq_ref[rows, own] += _nn(ds, kv)
                    dk_ref[ks, own] += _tn(ds, qv)
                    dvi = _tn(p.astype(BF16), dov)
                    dv_new = dvi if dv_new is None else jnp.where(head, dvi, dv_new)
                dv_ref[ks, pair] += dv_new

        whole, upper, lower = slice(0, t), slice(0, t // 2), slice(t // 2, t)

        @pl.when(kj < qi)
        def _():
            update(whole, whole, False)

        @pl.when(kj == qi)
        def _():
            update(upper, upper, True)
            update(lower, upper, False)
            update(lower, lower, True)

        if ne:
            @pl.when((pl.program_id(0) == n_groups - 1) & (step == n_pairs - 1))
            def _():
                x_finish(*comm)

    wide, half = hps * LANES, hps // 2 * LANES
    v_blk = Z_VA * ATT_W // half
    grid_spec = pltpu.PrefetchScalarGridSpec(
        num_scalar_prefetch=2, grid=(n_groups, n_pairs),
        in_specs=[pl.BlockSpec((t, wide), lambda hg, st, qi, kj: (qi[st], hg)),
                  pl.BlockSpec((t, wide), lambda hg, st, qi, kj: (kj[st], hg)),
                  pl.BlockSpec((t, half), lambda hg, st, qi, kj: (kj[st], v_blk + hg)),
                  pl.BlockSpec((t, half), lambda hg, st, qi, kj: (qi[st], hg)),
                  pl.BlockSpec((t, half), lambda hg, st, qi, kj: (qi[st], hg)),
                  pl.BlockSpec((t, half), lambda hg, st, qi, kj: (qi[st], hg))] + [ANY] * ne,
        out_specs=[pl.BlockSpec((s, wide), lambda hg, st, qi, kj: (0, hg)),
                   pl.BlockSpec((t, wide), lambda hg, st, qi, kj: (kj[st], hg)),
                   pl.BlockSpec((t, half), lambda hg, st, qi, kj: (kj[st], hg))] + [ANY] * ne,
        scratch_shapes=x_sems(ne) if ne else [])
    return pl.pallas_call(
        body, grid_spec=grid_spec,
        out_shape=[SDS((s, N_HEADS * LANES), F32), SDS((s, N_HEADS * LANES), F32), SDS((s, ATT_W), F32)]
        + x_shapes(exchange),
        name="fox_bwd", compiler_params=_cp(("arbitrary", "arbitrary")))(qi_arr, kj_arr, q_aug, k_aug, z, dy, lse, dd, *exchange)


def fox_post(dq_aug, dk_aug, dv, fa, bfo, *, tb=512):
    s = dv.shape[0]
    n = s // tb

    def body(dq_ref, dk_ref, dv_ref, fa_ref, b_ref, tri_ref, dz_ref, dfa_ref, gb_ref, carry, acc):
        i = pl.program_id(0)

        @pl.when(i == 0)
        def _():
            carry[...] = jnp.zeros_like(carry)
            acc[...] = jnp.zeros_like(acc)

        lane = _lane((tb, LANES))
        d_f = jnp.zeros((tb, LANES), F32)
        for h in range(N_HEADS):
            col = dq_ref[:, h * LANES + 64:h * LANES + 65] - dk_ref[:, h * LANES + 67:h * LANES + 68]
            d_f = jnp.where(lane == h, col, d_f)
        suffix = jnp.dot(tri_ref[...], d_f, preferred_element_type=F32, precision=lax.Precision.HIGHEST) + carry[0:1, :]
        carry[0:1, :] = suffix[0:1, :]
        xv = fa_ref[...] + b_ref[...]
        dx = suffix * (1.0 / (1.0 + jnp.exp(xv)))
        dfa_ref[...] = dx.astype(dfa_ref.dtype)
        acc[...] += jnp.sum(dx.reshape(tb // 8, 8, LANES), axis=0)
        for hp in range(4):
            for src, off, scale in ((dq_ref, 0, QK_SCALE), (dk_ref, ATT_W, 1.0)):
                even = src[:, (2 * hp) * LANES:(2 * hp + 1) * LANES]
                odd = pltpu.roll(src[:, (2 * hp + 1) * LANES:(2 * hp + 2) * LANES], 64, axis=1)
                dz_ref[:, off + hp * LANES:off + (hp + 1) * LANES] = (jnp.where(lane < 64, even, odd) * scale).astype(BF16)
        dz_ref[:, 2 * ATT_W:3 * ATT_W] = dv_ref[...].astype(BF16)

        @pl.when(i == n - 1)
        def _():
            gb_ref[...] = jnp.sum(acc[...], axis=0, keepdims=True)

    rev = lambda i: (n - 1 - i, 0)
    return pl.pallas_call(
        body, grid=(n,),
        in_specs=[pl.BlockSpec((tb, N_HEADS * LANES), rev), pl.BlockSpec((tb, N_HEADS * LANES), rev),
                  pl.BlockSpec((tb, ATT_W), rev), pl.BlockSpec((tb, LANES), rev),
                  pl.BlockSpec((1, LANES), lambda i: (0, 0)), pl.BlockSpec((tb, tb), lambda i: (0, 0))],
        out_specs=[pl.BlockSpec((tb, 3 * ATT_W), rev), pl.BlockSpec((tb, LANES), rev),
                   pl.BlockSpec((1, LANES), lambda i: (0, 0))],
        out_shape=[SDS((s, 3 * ATT_W), BF16), SDS((s, LANES), BF16), SDS((1, LANES), F32)],
        scratch_shapes=[pltpu.VMEM((8, LANES), F32), pltpu.VMEM((8, LANES), F32)],
        name="fox_post", compiler_params=_cp(("arbitrary",)))(dq_aug, dk_aug, dv, fa, bfo, _tri(tb, True))


def rope_cos_sin(s):
    half = ROPE_DIM // 2
    inv_freq = ROPE_THETA ** (-jnp.arange(half, dtype=F32) * 2.0 / ROPE_DIM)
    ang = jnp.arange(s, dtype=F32)[:, None] * inv_freq[None, :]
    return jnp.tile(jnp.cos(ang), (1, LANES // half)), jnp.tile(jnp.sin(ang), (1, LANES // half))


def _rotate(x, cos, sin, sign):
    l64 = _lane(x.shape) & (HEAD_DIM - 1)
    first = l64 < ROPE_DIM // 2
    second = (l64 >= ROPE_DIM // 2) & (l64 < ROPE_DIM)
    from_next = jnp.where(first, -sign * sin, 0.0)
    from_prev = jnp.where(second, sign * sin, 0.0)
    return (x * jnp.where(first | second, cos, 1.0) + pltpu.roll(x, LANES - 8, axis=1) * from_next
            + pltpu.roll(x, 8, axis=1) * from_prev)


def _dil_rows(base, r):
    if r == 1:
        return pl.ds(pl.multiple_of(base, DIL_BLK), DIL_BLK)
    return pl.ds(base, DIL_BLK, stride=r)


def _dil_block(idx, r, nb):
    shift = nb.bit_length() - 1
    rho = idx >> shift
    n = idx & (nb - 1)
    base = rho + n * (r * DIL_BLK)
    return _dil_rows(base, r), _dil_rows(jnp.maximum(base - r * DIL_BLK, rho), r), n > 0


def _cat(a, b):
    return jnp.concatenate([a, b], axis=0)


def _two_heads(v, first_head):
    zero = jnp.zeros_like(v)
    return _cat(jnp.where(first_head, v, zero), jnp.where(first_head, zero, v))


def _dil_bands():
    b = DIL_BLK
    q = _row((2 * b, 2 * b)) & (b - 1)
    col = _lane((2 * b, 2 * b))
    return (col < b) & (col >= q), (col >= b) & (col - b <= q)


def _dil_load_qkv(zq_ref, zk_ref, zv_ref, cos_ref, sin_ref, q_ref, k_ref, v_ref, *, chunk=512):
    def step(i, carry):
        rows = pl.ds(pl.multiple_of(i * chunk, chunk), chunk)
        cos, sin = cos_ref[rows, :], sin_ref[rows, :]
        q_ref[rows, :] = _rotate(zq_ref[rows, :].astype(F32), cos, sin, 1.0) * QK_SCALE
        k_ref[rows, :] = _rotate(zk_ref[rows, :].astype(F32), cos, sin, 1.0)
        v_ref[rows, :] = zv_ref[rows, :].astype(F32)
        return carry

    lax.fori_loop(0, q_ref.shape[0] // chunk, step, 0)


def dil_fwd_all(z, cos_t, sin_t, *, unroll=32):
    s = z.shape[0]
    b = DIL_BLK
    n_blk = s // b

    def body(zq_ref, zk_ref, zv_ref, cos_ref, sin_ref, o_ref, l_ref, q_ref, k_ref, v_ref):
        _dil_load_qkv(zq_ref, zk_ref, zv_ref, cos_ref, sin_ref, q_ref, k_ref, v_ref)
        first_head = _lane((b, LANES)) < 64
        band_prev, band_cur = _dil_bands()
        for g, (_, r) in enumerate(DIL_PATTERNS):
            nb = n_blk // r

            def group(it, carry, g=g, r=r, nb=nb):
                loaded = []
                kc = vc = None
                for u in range(unroll):
                    rows_c, rows_p, has_prev = _dil_block(it * unroll + u, r, nb)
                    if u % min(nb, unroll):
                        kp, vp = kc, vc
                    else:
                        kp, vp = k_ref[rows_p, :].astype(BF16), v_ref[rows_p, :].astype(BF16)
                    kc, vc = k_ref[rows_c, :].astype(BF16), v_ref[rows_c, :].astype(BF16)
                    state = (o_ref[rows_c, :], l_ref[rows_c, :]) if g else None
                    loaded.append((rows_c, has_prev, [q_ref[rows_c, :].astype(BF16), kp, kc, vp, vc], state))
                done = []
                for rows_c, has_prev, (qv, kp, kc, vp, vc), state in loaded:
                    sc = jnp.where(band_cur | (band_prev & has_prev), _nt(_two_heads(qv, first_head), _cat(kp, kc)), NEG)
                    m = jnp.max(sc, axis=-1, keepdims=True)
                    p = jnp.exp(sc - m)
                    den = jnp.sum(p, axis=-1, keepdims=True)
                    both = _nn(p.astype(BF16), _cat(vp, vc)) / den
                    lse2 = m + jnp.log(den)
                    ov = jnp.where(first_head, both[:b], both[b:])
                    lse = jnp.where(first_head, lse2[:b], lse2[b:])
                    if state is not None:
                        m2 = jnp.maximum(state[1], lse)
                        wp = jnp.exp(state[1] - m2)
                        wn = jnp.exp(lse - m2)
                        ov = (wp * state[0] + wn * ov) / (wp + wn)
                        lse = m2 + jnp.log(wp + wn)
                    done.append((rows_c, ov, lse))
                for rows_c, ov, lse in done:
                    o_ref[rows_c, :] = ov
                    l_ref[rows_c, :] = lse
                return carry

            lax.fori_loop(0, n_blk // unroll, group, 0)

    col_blk = lambda k: pl.BlockSpec((s, LANES), lambda hp: (0, 4 * k + hp))
    table = pl.BlockSpec((s, LANES), lambda hp: (0, 0))
    out = pl.BlockSpec((s, LANES), lambda hp: (0, hp))
    return pl.pallas_call(
        body, grid=(4,), in_specs=[col_blk(Z_QB), col_blk(Z_KB), col_blk(Z_VB), table, table], out_specs=[out, out],
        out_shape=[SDS((s, ATT_W), F32)] * 2, scratch_shapes=[pltpu.VMEM((s, LANES), F32)] * 3, name="dil_fwd",
        compiler_params=_cp(("parallel",)))(z, z, z, cos_t, sin_t)


def dil_bwd_all(z, cos_t, sin_t, dy, lse, y, exchange=(), kind="to_chips", *, unroll=16):
    s = z.shape[0]
    b = DIL_BLK
    n_blk = s // b
    ne = len(exchange)
    x_shapes, x_sems, x_start, x_finish = EXCHANGES[kind]

    def body(zq_ref, zk_ref, zv_ref, cos_ref, sin_ref, do_ref, l_ref, y_ref, *rest):
        e_ins, (gq_ref, gk_ref, gv_ref), e_outs = rest[:ne], rest[ne:ne + 3], rest[ne + 3:2 * ne + 3]
        q_ref, k_ref, v_ref, dq_ref, dk_ref, dv_ref = rest[2 * ne + 3:2 * ne + 9]
        comm = (e_ins, e_outs) + tuple(rest[2 * ne + 9:])
        if ne:
            @pl.when(pl.program_id(0) == 0)
            def _():
                x_start(*comm)

        _dil_load_qkv(zq_ref, zk_ref, zv_ref, cos_ref, sin_ref, q_ref, k_ref, v_ref)
        dq_ref[...] = jnp.zeros_like(dq_ref)
        dk_ref[...] = jnp.zeros_like(dk_ref)
        dv_ref[...] = jnp.zeros_like(dv_ref)
        first_head = _lane((b, LANES)) < 64
        band_prev, band_cur = _dil_bands()
        for _, r in DIL_PATTERNS:
            nb = n_blk // r

            def group(it, carry, r=r, nb=nb):
                loaded = []
                kc = vc = None
                for u in range(unroll):
                    rows_c, rows_p, has_prev = _dil_block(it * unroll + u, r, nb)
                    if u % min(nb, unroll):
                        kp, vp = kc, vc
                    else:
                        kp, vp = k_ref[rows_p, :].astype(BF16), v_ref[rows_p, :].astype(BF16)
                    kc, vc = k_ref[rows_c, :].astype(BF16), v_ref[rows_c, :].astype(BF16)
                    vals = [q_ref[rows_c, :].astype(BF16), kp, kc, vp, vc, do_ref[rows_c, :], l_ref[rows_c, :], y_ref[rows_c, :]]
                    loaded.append((rows_c, rows_p, has_prev, vals))
                done = []
                for rows_c, rows_p, has_prev, (qv, kp, kc, vp, vc, dof, lv, yv) in loaded:
                    q2 = _two_heads(qv, first_head)
                    do2 = _two_heads(dof.astype(BF16), first_head)
                    kcat, vcat = _cat(kp, kc), _cat(vp, vc)
                    lse2 = _cat(lv[:, 0:1], lv[:, 64:65])
                    dd2 = jnp.sum(_two_heads(dof * yv, first_head), axis=-1, keepdims=True)
                    p = jnp.exp(jnp.where(band_cur | (band_prev & has_prev), _nt(q2, kcat), NEG) - lse2)
                    ds = (p * (_nt(do2, vcat) - dd2)).astype(BF16)
                    dq2 = _nn(ds, kcat)
                    dkcat = _tn(ds, q2)
                    dvcat = _tn(p.astype(BF16), do2)
                    done.append((rows_c, rows_p, (jnp.where(first_head, dq2[:b], dq2[b:]), dkcat[:b], dkcat[b:],
                                                  dvcat[:b], dvcat[b:])))
                held = None
                for u, (rows_c, rows_p, (dq, dk_p, dk_c, dv_p, dv_c)) in enumerate(done):
                    dq_ref[rows_c, :] += dq
                    if u % min(nb, unroll):
                        rows_h, dk_h, dv_h = held
                        dk_ref[rows_h, :] += dk_h + dk_p
                        dv_ref[rows_h, :] += dv_h + dv_p
                    else:
                        if held is not None:
                            dk_ref[held[0], :] += held[1]
                            dv_ref[held[0], :] += held[2]
                        dk_ref[rows_p, :] += dk_p
                        dv_ref[rows_p, :] += dv_p
                    held = (rows_c, dk_c, dv_c)
                dk_ref[held[0], :] += held[1]
                dv_ref[held[0], :] += held[2]
                return carry

            lax.fori_loop(0, n_blk // unroll, group, 0)

        def finish(i, carry, chunk=512):
            rows = pl.ds(pl.multiple_of(i * chunk, chunk), chunk)
            cos, sin = cos_ref[rows, :], sin_ref[rows, :]
            gq_ref[rows, :] = (_rotate(dq_ref[rows, :], cos, sin, -1.0) * QK_SCALE).astype(BF16)
            gk_ref[rows, :] = _rotate(dk_ref[rows, :], cos, sin, -1.0).astype(BF16)
            gv_ref[rows, :] = dv_ref[rows, :].astype(BF16)
            return carry

        lax.fori_loop(0, s // 512, finish, 0)
        if ne:
            @pl.when(pl.program_id(0) == 3)
            def _():
                x_finish(*comm)

    col_blk = lambda k: pl.BlockSpec((s, LANES), lambda hp: (0, 4 * k + hp))
    table = pl.BlockSpec((s, LANES), lambda hp: (0, 0))
    nat = pl.BlockSpec((s, LANES), lambda hp: (0, hp))
    return pl.pallas_call(
        body, grid=(4,), in_specs=[col_blk(Z_QB), col_blk(Z_KB), col_blk(Z_VB), table, table, nat, nat, nat] + [ANY] * ne,
        out_specs=[nat, nat, nat] + [ANY] * ne, out_shape=[SDS((s, ATT_W), BF16)] * 3 + x_shapes(exchange),
        scratch_shapes=[pltpu.VMEM((s, LANES), F32)] * 6 + (x_sems(ne) if ne else []), name="dil_bwd",
        compiler_params=_cp(("arbitrary",)))(z, z, z, cos_t, sin_t, dy, lse, y, *exchange)


def _sigmoid(v):
    return 1.0 / (1.0 + jnp.exp(-v))


def gate_mix(ya, yb, wa, wb, z, *, tm=2048, tn=512):
    s = ya.shape[0]
    d = wa.shape[1]
    ga_blk = 3 * ATT_W * 2 // tn
    gb_blk = ga_blk + d // tn

    def body(ya_ref, yb_ref, wa_ref, wb_ref, ga_ref, gb_ref, pa_ref, pb_ref, mx_ref):
        pa = _nn(ya_ref[...], wa_ref[...])
        pb = _nn(yb_ref[...].astype(BF16), wb_ref[...])
        pa_ref[...] = pa.astype(BF16)
        pb_ref[...] = pb.astype(BF16)
        mx_ref[...] = (_sigmoid(ga_ref[...].astype(F32)) * pa + _sigmoid(gb_ref[...].astype(F32)) * pb).astype(BF16)

    out = pl.BlockSpec((tm, tn), lambda i, j: (i, j))
    return pl.pallas_call(
        body, grid=(s // tm, d // tn),
        in_specs=[pl.BlockSpec((tm, ATT_W), lambda i, j: (i, 0)), pl.BlockSpec((tm, ATT_W), lambda i, j: (i, 0)),
                  pl.BlockSpec((ATT_W, tn), lambda i, j: (0, j)), pl.BlockSpec((ATT_W, tn), lambda i, j: (0, j)),
                  pl.BlockSpec((tm, tn), lambda i, j: (i, ga_blk + j)), pl.BlockSpec((tm, tn), lambda i, j: (i, gb_blk + j))],
        out_specs=[out, out, out], out_shape=[SDS((s, d), BF16)] * 3, name="gate_mix",
        compiler_params=_cp(("parallel", "parallel")))(ya, yb, wa, wb, z, z)


def mix_bwd(dy, w_o, z, pa, pb, wo_a, wo_b, ya, *, tm=512):
    s, d = dy.shape

    def body(dy_ref, wo_ref, ga_ref, gb_ref, pa_ref, pb_ref, wa_ref, wb_ref, ya_ref,
             dpa_ref, dpb_ref, dg_ref, dya_ref, dyb_ref, dd_ref):
        dm = _nt(dy_ref[...], wo_ref[...])
        sa = _sigmoid(ga_ref[...].astype(F32))
        sb = _sigmoid(gb_ref[...].astype(F32))
        dpa = (dm * sa).astype(BF16)
        dpb = (dm * sb).astype(BF16)
        dpa_ref[...] = dpa
        dpb_ref[...] = dpb
        dg_ref[:, 0:d] = (dm * pa_ref[...].astype(F32) * sa * (1.0 - sa)).astype(BF16)
        dg_ref[:, d:2 * d] = (dm * pb_ref[...].astype(F32) * sb * (1.0 - sb)).astype(BF16)
        dya = _nt(dpa, wa_ref[...]).astype(BF16)
        dya_ref[...] = dya
        dyb_ref[...] = _nt(dpb, wb_ref[...])
        lane = _lane((tm, LANES))
        for pr in range(ATT_W // LANES):
            pair = slice(pr * LANES, (pr + 1) * LANES)
            prod = dya[:, pair].astype(F32) * ya_ref[:, pair].astype(F32)
            lo = jnp.sum(jnp.where(lane < 64, prod, 0.0), axis=-1, keepdims=True)
            hi = jnp.sum(jnp.where(lane >= 64, prod, 0.0), axis=-1, keepdims=True)
            dd_ref[:, pair] = jnp.where(lane < 64, lo, hi)

    row = pl.BlockSpec((tm, d), lambda i: (i, 0))
    att = pl.BlockSpec((tm, ATT_W), lambda i: (i, 0))
    whole = lambda a: pl.BlockSpec(a.shape, lambda i: (0, 0))
    return pl.pallas_call(
        body, grid=(s // tm,),
        in_specs=[row, whole(w_o), pl.BlockSpec((tm, d), lambda i: (i, 3)), pl.BlockSpec((tm, d), lambda i: (i, 4)), row, row,
                  whole(wo_a), whole(wo_b), att],
        out_specs=[row, row, pl.BlockSpec((tm, 2 * d), lambda i: (i, 0)), att, att, att],
        out_shape=[SDS((s, d), BF16), SDS((s, d), BF16), SDS((s, 2 * d), BF16), SDS((s, ATT_W), BF16),
                   SDS((s, ATT_W), F32), SDS((s, ATT_W), F32)], name="mix_bwd",
        compiler_params=_cp(("parallel",)))(dy, w_o, z, z, pa, pb, wo_a, wo_b, ya)


GELU_C = math.sqrt(2.0 / math.pi)


def _gelu_parts(a):
    a2 = a * a
    th = jnp.tanh(a * (GELU_C + (GELU_C * 0.044715) * a2))
    half = 0.5 * a
    gelu = half + half * th
    dgelu = (0.5 + 0.5 * th) + half * (1.0 - th * th) * (GELU_C + (3.0 * GELU_C * 0.044715) * a2)
    return gelu, dgelu


def _causal_taps(u, before):
    row = _row(u.shape)
    r1 = jnp.where(row == 0, before[7:8, :], pltpu.roll(u, 1, axis=0))
    r2 = jnp.where(row == 0, before[6:7, :], jnp.where(row == 1, before[7:8, :], pltpu.roll(u, 2, axis=0)))
    return r1, r2


def ffn_up(h, wa, wb, cw, cb, *, tm=2048, tn=256):
    s, d = h.shape
    f = wa.shape[1]
    nj = f // tn

    def body(h_ref, wa_ref, wb_ref, cwa_ref, cwb_ref, cba_ref, cbb_ref, ua_ref, ub_ref, ca_ref, cbo_ref, m_ref, carry):
        @pl.when(pl.program_id(1) == 0)
        def _():
            carry[...] = jnp.zeros_like(carry)

        conv = []
        for k, (w_ref, cw_ref, cb_ref, u_ref, c_ref) in enumerate(((wa_ref, cwa_ref, cba_ref, ua_ref, ca_ref),
                                                                   (wb_ref, cwb_ref, cbb_ref, ub_ref, cbo_ref))):
            u = _nn(h_ref[...], w_ref[...])
            u_ref[...] = u.astype(BF16)
            r1, r2 = _causal_taps(u, carry[k])
            carry[k] = u[tm - 8:tm, :]
            conv.append(cw_ref[0:1, :] * r2 + cw_ref[1:2, :] * r1 + cw_ref[2:3, :] * u + cb_ref[...])
            c_ref[...] = conv[k].astype(BF16)
        m_ref[...] = (_gelu_parts(conv[0])[0] * conv[1]).astype(BF16)

    out = pl.BlockSpec((tm, tn), lambda j, i: (i, j))
    return pl.pallas_call(
        body, grid=(nj, s // tm),
        in_specs=[pl.BlockSpec((tm, d), lambda j, i: (i, 0)),
                  pl.BlockSpec((d, tn), lambda j, i: (0, j)), pl.BlockSpec((d, tn), lambda j, i: (0, j)),
                  pl.BlockSpec((3, tn), lambda j, i: (0, j)), pl.BlockSpec((3, tn), lambda j, i: (0, nj + j)),
                  pl.BlockSpec((1, tn), lambda j, i: (0, j)), pl.BlockSpec((1, tn), lambda j, i: (0, nj + j))],
        out_specs=[out] * 5, out_shape=[SDS((s, f), BF16)] * 5,
        scratch_shapes=[pltpu.VMEM((2, 8, tn), F32)], name="ffn_up",
        compiler_params=_cp(("parallel", "arbitrary")))(h, wa, wb, cw, cw, cb, cb)


def ffn_bwd(dm, ua, ub, ca, cbo, cw, *, tm=2048, tn=256):
    s, f = dm.shape
    nj = f // tn
    ni = s // tm

    def body(dm_ref, ua_ref, ub_ref, ca_ref, cbo_ref, cwa_ref, cwb_ref, dua_ref, dub_ref, ga_ref, gb_ref, carry):
        @pl.when(pl.program_id(1) == 0)
        def _():
            carry[...] = jnp.zeros_like(carry)
            ga_ref[...] = jnp.zeros_like(ga_ref)
            gb_ref[...] = jnp.zeros_like(gb_ref)

        row = _row((tm, tn))
        dmv = dm_ref[...].astype(F32)
        gelu, dgelu = _gelu_parts(ca_ref[...].astype(F32))
        dcs = (dmv * cbo_ref[...].astype(F32) * dgelu, dmv * gelu)
        for k, (dc, u_ref, cw_ref, du_ref, g_ref) in enumerate(((dcs[0], ua_ref, cwa_ref, dua_ref, ga_ref),
                                                                (dcs[1], ub_ref, cwb_ref, dub_ref, gb_ref))):
            u = u_ref[...].astype(F32)
            after = carry[k]
            n1 = jnp.where(row == tm - 1, after[0:1, :], pltpu.roll(dc, tm - 1, axis=0))
            n2 = jnp.where(row == tm - 2, after[0:1, :], jnp.where(row == tm - 1, after[1:2, :], pltpu.roll(dc, tm - 2, axis=0)))
            g_ref[0:1, :] += jnp.sum(n2 * u, axis=0, keepdims=True)
            g_ref[1:2, :] += jnp.sum(n1 * u, axis=0, keepdims=True)
            g_ref[2:3, :] += jnp.sum(dc * u, axis=0, keepdims=True)
            g_ref[3:4, :] += jnp.sum(dc, axis=0, keepdims=True)
            du_ref[...] = (cw_ref[2:3, :] * dc + cw_ref[1:2, :] * n1 + cw_ref[0:1, :] * n2).astype(BF16)
            carry[k] = dc[0:8, :]

    tile = pl.BlockSpec((tm, tn), lambda j, i: (ni - 1 - i, j))
    gspec = pl.BlockSpec((8, tn), lambda j, i: (0, j))
    return pl.pallas_call(
        body, grid=(nj, ni),
        in_specs=[tile] * 5 + [pl.BlockSpec((3, tn), lambda j, i: (0, j)), pl.BlockSpec((3, tn), lambda j, i: (0, nj + j))],
        out_specs=[tile, tile, gspec, gspec],
        out_shape=[SDS((s, f), BF16), SDS((s, f), BF16), SDS((8, f), F32), SDS((8, f), F32)],
        scratch_shapes=[pltpu.VMEM((2, 8, tn), F32)], name="ffn_bwd",
        compiler_params=_cp(("parallel", "arbitrary")))(dm, ua, ub, ca, cbo, cw, cw)


def adamw(w, g, m, v, *, name, tr=None):
    r = w.shape[0]
    rest = w.shape[1:]
    if tr is None:
        tr = r
        for cand in (512, 352, 256, 128, 64, 32, 16, 8):
            if r % cand == 0:
                tr = cand
                break

    def body(w_ref, g_ref, m_ref, v_ref, d_ref, nm_ref, nv_ref):
        gv = g_ref[...]
        mn = ADAM_B1 * m_ref[...] + (1.0 - ADAM_B1) * gv
        vn = ADAM_B2 * v_ref[...] + (1.0 - ADAM_B2) * (gv * gv)
        m_hat = mn / (1.0 - ADAM_B1 ** ADAM_STEP)
        v_hat = vn / (1.0 - ADAM_B2 ** ADAM_STEP)
        d_ref[...] = -ADAM_LR * (m_hat / (jnp.sqrt(v_hat) + ADAM_EPS) + ADAM_WD * w_ref[...])
        nm_ref[...] = mn
        nv_ref[...] = vn

    blk = pl.BlockSpec((tr,) + rest, lambda i: (i,) + (0,) * len(rest))
    return pl.pallas_call(body, grid=(r // tr,), in_specs=[blk] * 4, out_specs=[blk] * 3, out_shape=[SDS(w.shape, F32)] * 3,
                          name=name, compiler_params=_cp(("parallel",)))(w, g, m, v)


def adamw_rows_view(w, g_mine, g_full, m, v, c_arr, *, name, tc=256):
    r, _, c = w.shape
    per_half = c // 2 // tc

    def body(c_ref, w_ref, gm_ref, gf_ref, m_ref, v_ref, d_ref, nm_ref, nv_ref, go_ref):
        mine = (pl.program_id(0) >> (per_half.bit_length() - 1)) == c_ref[0]
        gv = jnp.where(mine, gm_ref[...], gf_ref[...])
        mn = ADAM_B1 * m_ref[:, 0, :] + (1.0 - ADAM_B1) * gv
        vn = ADAM_B2 * v_ref[:, 0, :] + (1.0 - ADAM_B2) * (gv * gv)
        m_hat = mn / (1.0 - ADAM_B1 ** ADAM_STEP)
        v_hat = vn / (1.0 - ADAM_B2 ** ADAM_STEP)
        d_ref[:, 0, :] = -ADAM_LR * (m_hat / (jnp.sqrt(v_hat) + ADAM_EPS) + ADAM_WD * w_ref[:, 0, :])
        nm_ref[:, 0, :] = mn
        nv_ref[:, 0, :] = vn
        go_ref[:, 0, :] = gv

    b3 = pl.BlockSpec((r, 1, tc), lambda i, c_ref: (0, 0, i))
    own = pl.BlockSpec((r, tc), lambda i, c_ref: (0, jnp.clip(i - c_ref[0] * per_half, 0, per_half - 1)))
    full = pl.BlockSpec((r, tc), lambda i, c_ref: (0, i))
    grid_spec = pltpu.PrefetchScalarGridSpec(num_scalar_prefetch=1, grid=(c // tc,), in_specs=[b3, own, full, b3, b3],
                                             out_specs=[b3] * 4)
    return pl.pallas_call(body, grid_spec=grid_spec, out_shape=[SDS(w.shape, F32)] * 4, name=name,
                          compiler_params=_cp(("parallel",)))(c_arr, w, g_mine, g_full, m, v)


ANY = pl.BlockSpec(memory_space=pl.ANY)
ICI_KINDS = ("x", "y", "xy")


def _coords():
    return lax.axis_index("x"), lax.axis_index("y"), lax.axis_index("c")


def _peer(kind, x, y, c):
    if kind == "c":
        return (x, y, 1 - c)
    if kind == "x":
        return (1 - x, y, c)
    if kind == "y":
        return (x, 1 - y, c)
    return (1 - x, 1 - y, c)


def _chip_of(p):
    return 2 * p[0] + p[1]


def _half(rows, which):
    h = rows // 2
    return pl.ds(pl.multiple_of(which * h, 16), h)


def _remote(src, dst, send_sem, recv_sem, to):
    return pltpu.make_async_remote_copy(src_ref=src, dst_ref=dst, send_sem=send_sem, recv_sem=recv_sem,
                                        device_id=to, device_id_type=MESH)


def allgather_balanced(shard, *, name):
    r, cols = shard.shape
    h, q = r // 2, r // 4

    def body(in_ref, out_ref, send_sems, recv_sems):
        x, y, c = _coords()
        me, sibling = (x, y, c), (x, y, 1 - c)
        nbr = ((1 - x, y, c), (x, 1 - y, c))
        chip = (2 * (1 - x) + y, 2 * x + (1 - y), 2 * (1 - x) + (1 - y))
        quarter = lambda core, i: pl.ds(pl.multiple_of(core * h + i * q, 16), q)
        sent = []

        def go(src, dst, slot, to):
            cp = _remote(src, dst, send_sems.at[slot], recv_sems.at[slot], to)
            cp.start()
            sent.append(cp)

        def landed(region, slot):
            _remote(region, region, send_sems.at[slot], recv_sems.at[slot], me).wait_recv()

        for i in range(2):
            for k in range(2):
                qi = k if i == 0 else 1 - k
                go(in_ref.at[quarter(c, qi)], out_ref.at[2 * x + y, quarter(c, qi)], 2 * k + qi, nbr[k])
        for k in range(2):
            piece = out_ref.at[chip[k], quarter(c, k)]
            landed(piece, 2 * k + k)
            go(piece, piece, 4 + k, nbr[1 - k])
            go(piece, piece, 6 + 2 * k + k, sibling)
        for k in range(2):
            piece = out_ref.at[chip[k], quarter(c, 1 - k)]
            landed(piece, 2 * k + 1 - k)
            go(piece, piece, 6 + 2 * k + 1 - k, sibling)
        for k in range(2):
            piece = out_ref.at[chip[2], quarter(c, k)]
            landed(piece, 4 + k)
            go(piece, piece, 10 + k, sibling)
        for k in range(2):
            for i in range(2):
                landed(out_ref.at[chip[k], quarter(1 - c, i)], 6 + 2 * k + i)
            landed(out_ref.at[chip[2], quarter(1 - c, k)], 10 + k)
        for cp in sent:
            cp.wait_send()

    return pl.pallas_call(
        body, in_specs=[ANY], out_specs=ANY, out_shape=SDS((4,) + shard.shape, shard.dtype),
        scratch_shapes=[pltpu.SemaphoreType.DMA((12,)), pltpu.SemaphoreType.DMA((12,))], name=name)(shard)


def _allgather_shapes(shards):
    return [SDS((4,) + a.shape, a.dtype) for a in shards]


def _allgather_sems(n):
    return [pltpu.SemaphoreType.DMA((n, 6)), pltpu.SemaphoreType.DMA((n, 6))]


def _allgather_rows(ref, is_halved, which):
    r = ref.shape[0]
    return _half(r, which) if is_halved else pl.ds(0, r)


def _allgather_first(ins, outs, send_sems, recv_sems, halved):
    x, y, c = _coords()
    my_chip = 2 * x + y
    cps = []
    for w in range(len(ins)):
        rows = _allgather_rows(ins[w], halved[w], c)
        for k, kind in enumerate(ICI_KINDS):
            cps.append(_remote(ins[w].at[rows], outs[w].at[my_chip, rows], send_sems.at[w, k], recv_sems.at[w, k],
                               _peer(kind, x, y, c)))
    return cps


def _allgather_start(ins, outs, send_sems, recv_sems, halved):
    for cp in _allgather_first(ins, outs, send_sems, recv_sems, halved):
        cp.start()


def _allgather_finish(ins, outs, send_sems, recv_sems, halved):
    x, y, c = _coords()
    me = (x, y, c)
    second = []
    for w in range(len(ins)):
        for k, kind in enumerate(ICI_KINDS):
            landed = outs[w].at[_chip_of(_peer(kind, x, y, c)), _allgather_rows(ins[w], halved[w], c)]
            _remote(landed, landed, send_sems.at[w, k], recv_sems.at[w, k], me).wait_recv()
            if halved[w]:
                cp = _remote(landed, landed, send_sems.at[w, 3 + k], recv_sems.at[w, 3 + k], _peer("c", x, y, c))
                cp.start()
                second.append(cp)
    for w in range(len(ins)):
        if halved[w]:
            for k, kind in enumerate(ICI_KINDS):
                other = outs[w].at[_chip_of(_peer(kind, x, y, c)), _allgather_rows(ins[w], True, 1 - c)]
                _remote(other, other, send_sems.at[w, 3 + k], recv_sems.at[w, 3 + k], me).wait_recv()
    for cp in _allgather_first(ins, outs, send_sems, recv_sems, halved) + second:
        cp.wait_send()


def _half_of(ref, by_cols, which):
    lead = (slice(None),) * (len(ref.shape) - 2)
    if by_cols:
        h = ref.shape[-1] // 2
        return ref.at[lead + (slice(None), pl.ds(pl.multiple_of(which * h, LANES), h))]
    return ref.at[lead + (_half(ref.shape[-2], which),)]


def _half_shape(shape, by_cols):
    return shape[:-1] + (shape[-1] // 2,) if by_cols else shape[:-2] + (shape[-2] // 2, shape[-1])


def grads_to_sibling(gs, by_cols, *, name):
    n = len(gs)

    def body(*refs):
        ins, outs = refs[:n], refs[n:2 * n]
        send_sems, recv_sems = refs[2 * n:]
        x, y, c = _coords()
        cps = []
        for w in range(n):
            cp = _remote(_half_of(ins[w], by_cols[w], 1 - c), outs[w], send_sems.at[w], recv_sems.at[w], _peer("c", x, y, c))
            cp.start()
            cps.append(cp)
        for cp in cps:
            cp.wait()

    return pl.pallas_call(
        body, in_specs=[ANY] * n, out_specs=[ANY] * n,
        out_shape=[SDS(_half_shape(a.shape, bc), a.dtype) for a, bc in zip(gs, by_cols)],
        scratch_shapes=[pltpu.SemaphoreType.DMA((n,)), pltpu.SemaphoreType.DMA((n,))], name=name)(*gs)


def _to_chips_shapes(ps):
    return [SDS((3,) + a.shape[1:], a.dtype) for a in ps]


def _to_chips_sems(n):
    return [pltpu.SemaphoreType.DMA((n, 3)), pltpu.SemaphoreType.DMA((n, 3))]


def _to_chips_copies(ins, outs, send_sems, recv_sems):
    x, y, c = _coords()
    cps = []
    for w in range(len(ins)):
        for k, kind in enumerate(ICI_KINDS):
            to = _peer(kind, x, y, c)
            cps.append(_remote(ins[w].at[_chip_of(to)], outs[w].at[k], send_sems.at[w, k], recv_sems.at[w, k], to))
    return cps


def _to_chips_start(ins, outs, send_sems, recv_sems):
    for cp in _to_chips_copies(ins, outs, send_sems, recv_sems):
        cp.start()


def _to_chips_finish(ins, outs, send_sems, recv_sems):
    for cp in _to_chips_copies(ins, outs, send_sems, recv_sems):
        cp.wait()


def _to_owners_shapes(ps):
    return [SDS((7, a.shape[1] // 2, a.shape[2]), a.dtype) for a in ps]


def _to_owners_sems(n):
    return [pltpu.SemaphoreType.DMA((n, 7)), pltpu.SemaphoreType.DMA((n, 7))]


def _to_owners_copies(ins, outs, send_sems, recv_sems):
    x, y, c = _coords()
    cps = []
    for w in range(len(ins)):
        rows = ins[w].shape[1]
        for k, kind in enumerate(ICI_KINDS):
            px, py, _ = _peer(kind, x, y, c)
            for h in range(2):
                cps.append(_remote(ins[w].at[2 * px + py, _half(rows, h)], outs[w].at[2 * k + c],
                                   send_sems.at[w, 2 * k + h], recv_sems.at[w, 2 * k + c], (px, py, h)))
        cps.append(_remote(ins[w].at[2 * x + y, _half(rows, 1 - c)], outs[w].at[6], send_sems.at[w, 6], recv_sems.at[w, 6],
                           _peer("c", x, y, c)))
    return cps


def _to_owners_start(ins, outs, send_sems, recv_sems):
    for cp in _to_owners_copies(ins, outs, send_sems, recv_sems):
        cp.start()


def _to_owners_finish(ins, outs, send_sems, recv_sems):
    for cp in _to_owners_copies(ins, outs, send_sems, recv_sems):
        cp.wait_send()
    for w in range(len(ins)):
        for slot in range(7):
            got = outs[w].at[slot]
            _remote(got, got, send_sems.at[w, slot], recv_sems.at[w, slot], _coords()).wait_recv()


EXCHANGES = {"to_chips": (_to_chips_shapes, _to_chips_sems, _to_chips_start, _to_chips_finish),
             "to_owners": (_to_owners_shapes, _to_owners_sems, _to_owners_start, _to_owners_finish)}


def halves_to_full(hs, by_cols, *, name):
    n = len(hs)

    def body(*refs):
        ins, outs = refs[:n], refs[n:2 * n]
        send_sems, recv_sems = refs[2 * n:]
        x, y, c = _coords()
        cps = []
        for w in range(n):
            cp = _remote(ins[w], _half_of(outs[w], by_cols[w], c), send_sems.at[w], recv_sems.at[w], _peer("c", x, y, c))
            cp.start()
            cps.append(cp)
        for cp in cps:
            cp.wait()

    return pl.pallas_call(
        body, in_specs=[ANY] * n, out_specs=[ANY] * n,
        out_shape=[SDS((a.shape[0], 2 * a.shape[1]) if bc else (2 * a.shape[0], a.shape[1]), a.dtype)
                   for a, bc in zip(hs, by_cols)],
        scratch_shapes=[pltpu.SemaphoreType.DMA((n,)), pltpu.SemaphoreType.DMA((n,))],
        name=name)(*hs)


def _row_tile(rows):
    for cand in (256, 192, 176, 128, 64, 32, 16):
        if rows % cand == 0:
            return cand
    return rows


def chip_sum(g, recv, c_arr, by_cols, *, name):
    _, r, cols = g.shape

    def body(c_ref, g_ref, r_ref, f_ref, b_ref):
        tot = g_ref[...] + r_ref[...]
        f_ref[...] = tot
        b_ref[...] = tot.astype(BF16)

    if by_cols:
        tc = 4 * LANES
        nblk = cols // 2 // tc
        shape = (4, r, cols // 2)
        blk = pl.BlockSpec((None, r, tc), lambda j, i, c_ref: (j, 0, i))
        mine = pl.BlockSpec((None, r, tc), lambda j, i, c_ref: (j, 0, c_ref[0] * nblk + i))
    else:
        tr = _row_tile(r // 2)
        nblk = r // 2 // tr
        shape = (4, r // 2, cols)
        blk = pl.BlockSpec((None, tr, cols), lambda j, i, c_ref: (j, i, 0))
        mine = pl.BlockSpec((None, tr, cols), lambda j, i, c_ref: (j, c_ref[0] * nblk + i, 0))
    grid_spec = pltpu.PrefetchScalarGridSpec(num_scalar_prefetch=1, grid=(4, nblk), in_specs=[mine, blk], out_specs=[blk, blk])
    return pl.pallas_call(body, grid_spec=grid_spec, out_shape=[SDS(shape, F32), SDS(shape, BF16)],
                          name=name, compiler_params=_cp(("parallel", "parallel")))(c_arr, g, recv)


def final_sum(pf, recv, chip_arr, *, name):
    _, h, cols = pf.shape
    tr = _row_tile(h)

    def body(chip_ref, p_ref, r_ref, o_ref):
        o_ref[...] = ((p_ref[...] + r_ref[0].astype(F32)) + r_ref[1].astype(F32)) + r_ref[2].astype(F32)

    grid_spec = pltpu.PrefetchScalarGridSpec(
        num_scalar_prefetch=1, grid=(h // tr,),
        in_specs=[pl.BlockSpec((None, tr, cols), lambda i, chip_ref: (chip_ref[0], i, 0)),
                  pl.BlockSpec((3, tr, cols), lambda i, chip_ref: (0, i, 0))],
        out_specs=pl.BlockSpec((tr, cols), lambda i, chip_ref: (i, 0)))
    return pl.pallas_call(body, grid_spec=grid_spec, out_shape=SDS((h, cols), F32), name=name,
                          compiler_params=_cp(("parallel",)))(chip_arr, pf, recv)


def owner_sum(g, recv, pos_arr, *, name):
    _, r, cols = g.shape
    h = r // 2
    tr = _row_tile(h)
    nblk = h // tr

    def body(pos_ref, g_ref, r_ref, o_ref):
        tot = g_ref[...]
        for slot in range(7):
            tot = tot + r_ref[slot].astype(F32)
        o_ref[...] = tot

    grid_spec = pltpu.PrefetchScalarGridSpec(
        num_scalar_prefetch=1, grid=(nblk,),
        in_specs=[pl.BlockSpec((None, tr, cols), lambda i, pos: (pos[0], pos[1] * nblk + i, 0)),
                  pl.BlockSpec((7, tr, cols), lambda i, pos: (0, i, 0))],
        out_specs=pl.BlockSpec((tr, cols), lambda i, pos: (i, 0)))
    return pl.pallas_call(body, grid_spec=grid_spec, out_shape=SDS((h, cols), F32), name=name,
                          compiler_params=_cp(("parallel",)))(pos_arr, g, recv)


def allreduce_small(v, *, name):
    rws, cols = v.shape

    def body(v_ref, all_ref, sum_ref, send_sems, recv_sems, local_sem):
        x, y, c = _coords()
        me, sibling = (x, y, c), (x, y, 1 - c)
        chips = [(1 - x, y), (x, 1 - y), (1 - x, 1 - y)]

        def rows(px, py, pc):
            return all_ref.at[pl.ds(pl.multiple_of((4 * px + 2 * py + pc) * rws, 8), rws), :]

        def copy(k, block, to, src=None):
            return _remote(rows(*block) if src is None else src, rows(*block), send_sems.at[k], recv_sems.at[k], to)

        mine = pltpu.make_async_copy(v_ref, rows(*me), local_sem)
        mine.start()
        first = [copy(0, me, sibling, src=v_ref)]
        first += [copy(1 + j, me, (*chip, c), src=v_ref) for j, chip in enumerate(chips)]
        for cp in first:
            cp.start()
        passed = [copy(4 + j, (*chip, c), sibling) for j, chip in enumerate(chips)]
        for j, chip in enumerate(chips):
            copy(1 + j, (*chip, c), me).wait_recv()
            passed[j].start()
        copy(0, sibling, me).wait_recv()
        for j, chip in enumerate(chips):
            copy(4 + j, (*chip, 1 - c), me).wait_recv()
        for cp in first + passed:
            cp.wait_send()
        mine.wait()
        tot = all_ref[0:rws, :]
        for dev in range(1, 8):
            tot = tot + all_ref[dev * rws:(dev + 1) * rws, :]
        sum_ref[...] = tot

    vm = pl.BlockSpec(memory_space=pltpu.VMEM)
    return pl.pallas_call(
        body, in_specs=[vm], out_specs=[vm, vm],
        out_shape=[SDS((8 * rws, cols), v.dtype), SDS((rws, cols), v.dtype)],
        scratch_shapes=[pltpu.SemaphoreType.DMA((7,)), pltpu.SemaphoreType.DMA((7,)), pltpu.SemaphoreType.DMA],
        name=name)(v)[1]


def _pack_rows(parts, rows):
    out = []
    for a, r in zip(parts, rows):
        flat = a.reshape(-1)
        flat = jnp.pad(flat, (0, r * LANES - flat.shape[0]))
        out.append(flat.reshape(r, LANES))
    return jnp.concatenate(out, axis=0)


def _unpack_rows(packed, shapes, rows):
    out, at = [], 0
    for shp, r in zip(shapes, rows):
        size = int(np.prod(shp))
        out.append(packed[at:at + r].reshape(-1)[:size].reshape(shp))
        at += r
    return out


def kernel(x, g_pre_mix, w_in, b_forget, w_o_fox, w_o_dil, w_out, g_post_mix, g_pre_ffn, w_up, conv_w, conv_b, w_down, g_post_ffn, loss_target, m_g_pre_mix, m_w_in, m_b_forget, m_w_o_fox, m_w_o_dil, m_w_out, m_g_post_mix, m_g_pre_ffn, m_w_up, m_conv_w, m_conv_b, m_w_down, m_g_post_ffn, v_g_pre_mix, v_w_in, v_b_forget, v_w_o_fox, v_w_o_dil, v_w_out, v_g_post_mix, v_g_pre_ffn, v_w_up, v_conv_w, v_conv_b, v_w_down, v_g_post_ffn):
    xi, yi, ci = _coords()
    chip = 2 * xi + yi
    c_arr = jnp.reshape(ci, (1,)).astype(jnp.int32)
    chip_arr = jnp.reshape(chip, (1,)).astype(jnp.int32)
    xs = x[0]
    target = loss_target[0]
    s, d = xs.shape
    f_half = w_down.shape[1] * 4
    cols_in = w_in.shape[2]

    big = (w_in, w_o_fox, w_o_dil, w_out, w_up, w_down)
    shards = [w[0].astype(BF16) for w in big]
    a_in = allgather_balanced(shards[0], name="allgather_w_in")
    w_in_full = jnp.concatenate([jnp.where(chip == j, shards[0], a_in[j]) for j in range(4)], axis=1)
    nf = N_HEADS
    e_a, e_b = 3 * ATT_W, 3 * ATT_W + nf
    wz = jnp.concatenate([w_in_full[:, :e_a], w_in_full[:, e_b:]], axis=1)
    wf = jnp.pad(w_in_full[:, e_a:e_b], ((0, 0), (0, LANES - nf)))
    cb = conv_b
    bfo = jnp.pad(b_forget, ((0, 0), (0, LANES - nf)))

    h1 = rmsnorm_fwd(xs, g_pre_mix)
    z = mm([(h1, d, 0)], [(wz, d, 0)], nt=False, out_dtype=BF16, tm=s, tn=1024, name="in_proj")
    fa = mm([(h1, d, 0)], [(wf, d, 0)], nt=False, out_dtype=F32, tm=s, tn=LANES, name="in_proj_forget")
    q_aug, k_aug, v_aug = fox_prep(z, fa, bfo)
    later = shards[1:] + [conv_w[0]]
    ya, lse_a, *late = fox_fwd(q_aug, k_aug, v_aug, gather=later, halved=[True] * 5 + [False], hps=4)
    a_of, a_od, a_out, a_up, a_down, a_cw = [
        lax.dynamic_update_index_in_dim(a4, own, chip, 0) for a4, own in zip(late, later)]
    cw = jnp.concatenate([a_cw[j] for j in range(4)], axis=1)
    wo_a = jnp.concatenate([a_of[j] for j in range(4)], axis=1)
    wo_b = jnp.concatenate([a_od[j] for j in range(4)], axis=1)
    w_o = a_out.reshape(d, d)
    w_dn = a_down.reshape(f_half, d)
    wu_a = jnp.concatenate([a_up[0], a_up[1]], axis=1)
    wu_b = jnp.concatenate([a_up[2], a_up[3]], axis=1)
    cos_t, sin_t = rope_cos_sin(s)
    yb, lse_b = dil_fwd_all(z, cos_t, sin_t)
    pa, pb, mixed = gate_mix(ya, yb, wo_a, wo_b, z)
    y1, x1, h2 = proj_norm_res(mixed, w_o, g_post_mix, xs, g_pre_ffn, tm=1024, name="out_proj")
    ua, ub, conv_a, conv_bh, mid = ffn_up(h2, wu_a, wu_b, cw, cb)
    dout, dy2, gg_post_ffn, sq = proj_norm_loss(mid, w_dn, g_post_ffn, x1, target, name="down_proj")

    dmid = mm([(dy2, d, 0)], [(w_dn, d, 0)], nt=True, out_dtype=BF16, tm=2048, tn=f_half // 2, name="down_dgrad")
    dw_down, dw_down16 = wgrad((mid, f_half, 0), dy2, tk=f_half // 2, tn=1024, ts=2048, name="down_wgrad", bf16_copy=True)
    dua, dub, gc_a, gc_b = ffn_bwd(dmid, ua, ub, conv_a, conv_bh, cw)
    dx1, dy1, gg_pre_ffn, gg_post_mix = mm_norm_bwd(
        [(dua, f_half, 0), (dub, f_half, 0)], [(wu_a, f_half, 0), (wu_b, f_half, 0)],
        [(x1, g_pre_ffn, dout, F32), (y1, g_post_mix, None, BF16)], name="up_dgrad")
    dw_up = None
    for k, du in enumerate((dua, dub)):
        dw_up = wgrad((h2, d, 0), du, tk=1024, tn=f_half // 2, ts=2048, name=f"up_wgrad_{k}", chip_major=True,
                      slabs=(4, 2 * k), into=dw_up, bf16_copy=True)
    g_ffn = [(dw_up[0], dw_up[1]), (dw_down.reshape(4, f_half // 4, d), dw_down16.reshape(4, f_half // 4, d))]
    dw_out, dw_out16 = wgrad((mixed, d, 0), dy1, tk=1024, tn=1024, ts=2048, name="out_wgrad", bf16_copy=True)
    dpa, dpb, dz_g, dya, dyb, dd_a = mix_bwd(dy1, w_o, z, pa, pb, wo_a, wo_b, ya)
    by_chip_cols = lambda a: jnp.stack([a[:, j * (d // 4):(j + 1) * (d // 4)] for j in range(4)], axis=0)
    dw_of = [by_chip_cols(a) for a in wgrad((ya, ATT_W, 0), dpa, tk=ATT_W, tn=d, ts=1024, name="fox_o_wgrad", bf16_copy=True)]
    dw_od = [by_chip_cols(a) for a in wgrad((yb, ATT_W, 0), dpb, tk=ATT_W, tn=d, ts=1024, name="dil_o_wgrad", bf16_copy=True)]
    g_mix = [dw_of, dw_od, (dw_out.reshape(4, d // 4, d), dw_out16.reshape(4, d // 4, d))]
    dq_aug, dk_aug, dv_a, *got_ffn = fox_bwd(q_aug, k_aug, z, dya, lse_a, dd_a, exchange=[g[1] for g in g_ffn], kind="to_owners")
    dz_a, dfa, gg_bf = fox_post(dq_aug, dk_aug, dv_a, fa, bfo)
    *dz_b, got_of, got_od, got_out = dil_bwd_all(z, cos_t, sin_t, dyb, lse_b, yb, exchange=[g[1] for g in g_mix],
                                                 kind="to_owners")
    got_mix = [got_of, got_od, got_out]
    dwt_a = wgrad((dz_a, e_a, 0), h1, tk=e_a // 2, tn=d, ts=2048, name="in_wgrad_a")
    dwt_b = [wgrad((part, ATT_W, 0), h1, tk=ATT_W, tn=d, ts=2048, name=f"in_wgrad_b{k}") for k, part in enumerate(dz_b)]
    dwt_g = wgrad((dz_g, 2 * d, 0), h1, tk=d, tn=d, ts=2048, name="in_wgrad_g")
    dwt_f = wgrad((dfa, LANES, 0), h1, tk=LANES, tn=d, ts=2048, name="in_wgrad_f")
    dwt_full = jnp.concatenate([dwt_a, dwt_f[:nf], *dwt_b, dwt_g], axis=0)
    dw_in = jnp.stack([dwt_full[j * cols_in:(j + 1) * cols_in] for j in range(4)], axis=0)
    from_sib = grads_to_sibling([dw_in], [True], name="grads_to_sibling_in")
    sum_in = chip_sum(dw_in, from_sib[0], c_arr, True, name="chip_sum_w_in")
    grad_x, gg_pre_mix, got_in = mm_norm_bwd(
        [(dz_a, e_a, 0), *[(part, ATT_W, 0) for part in dz_b], (dz_g, d, 0), (dz_g, d, 1), (dfa, LANES, 0)],
        [(wz, e_a, 0), *[(wz, ATT_W, Z_QB + k) for k in range(3)], (wz, d, 3), (wz, d, 4), (wf, LANES, 0)],
        [(xs, g_pre_mix, dx1, F32)], exchange=[sum_in[1]], name="in_dgrad")

    names = ("w_in", "w_o_fox", "w_o_dil", "w_out", "w_up", "w_down")
    pos_arr = jnp.concatenate([chip_arr, c_arr])
    halves = [final_sum(sum_in[0], got_in, chip_arr, name="final_sum_w_in")] + [
        owner_sum(g[0], got, pos_arr, name=f"owner_sum_{nm}") for g, got, nm in zip(g_mix + g_ffn, got_mix + got_ffn, names[1:])]
    from_half = halves_to_full(halves, [True] + [False] * 5, name="halves_to_full")
    g_big = [None] + [lax.dynamic_update_slice_in_dim(full, mine, ci * mine.shape[0], axis=0)
                      for full, mine in zip(from_half[1:], halves[1:])]
    upd_big = [adamw(w[0], g, m[0], v[0], name=f"adamw_{nm}") for w, g, m, v, nm in list(zip(
        big, g_big, (m_w_in, m_w_o_fox, m_w_o_dil, m_w_out, m_w_up, m_w_down),
        (v_w_in, v_w_o_fox, v_w_o_dil, v_w_out, v_w_up, v_w_down), names))[1:]]
    to_t = lambda a: jnp.transpose(a, (2, 0, 1))
    from_t = lambda a: jnp.transpose(a, (1, 2, 0))
    *upd_in, g_in_t = adamw_rows_view(to_t(w_in), halves[0], from_half[0], to_t(m_w_in), to_t(v_w_in), c_arr,
                                      name="adamw_w_in")

    g_cw_loc = jnp.concatenate([gc_a[0:3], gc_b[0:3]], axis=1)
    g_cb_loc = jnp.concatenate([gc_a[3:4], gc_b[3:4]], axis=1)
    small_loc = [gg_pre_mix, gg_post_mix, gg_pre_ffn, gg_post_ffn, g_cb_loc, gg_bf[:, :nf], g_cw_loc, sq * (0.5 / d)]
    red_rows = (8, 8, 8, 8, 48, 8, 136, 8)
    red = allreduce_small(_pack_rows(small_loc, red_rows), name="allreduce_small")
    g_pm, g_qm, g_pf, g_qf, g_cb, g_bf, g_cw_full, loss_11 = _unpack_rows(red, [a.shape for a in small_loc], red_rows)
    loss = loss_11[0, 0]
    cols_cw = conv_w.shape[2]
    g_cw = lax.dynamic_slice_in_dim(g_cw_full, chip * cols_cw, cols_cw, axis=1)
    small_w = (g_pre_mix, g_post_mix, g_pre_ffn, g_post_ffn, conv_b, b_forget, conv_w[0])
    small_m = (m_g_pre_mix, m_g_post_mix, m_g_pre_ffn, m_g_post_ffn, m_conv_b, m_b_forget, m_conv_w[0])
    small_v = (v_g_pre_mix, v_g_post_mix, v_g_pre_ffn, v_g_post_ffn, v_conv_b, v_b_forget, v_conv_w[0])
    small_g = (g_pm, g_qm, g_pf, g_qf, g_cb, g_bf, g_cw)
    small_names = ("g_pre_mix", "g_post_mix", "g_pre_ffn", "g_post_ffn", "conv_b", "b_forget", "conv_w")
    per_param = [adamw(w, g, m, v, name=f"adamw_{nm}") for w, g, m, v, nm in zip(small_w, small_g, small_m, small_v, small_names)]
    upd_small = [[u[j] for u in per_param] for j in range(3)]

    order = ("g_pre_mix", "w_in", "b_forget", "w_o_fox", "w_o_dil", "w_out", "g_post_mix", "g_pre_ffn", "w_up", "conv_w",
             "conv_b", "w_down", "g_post_ffn")
    grads, deltas, new_ms, new_vs = {}, {}, {}, {}
    grads["w_in"] = from_t(g_in_t)
    deltas["w_in"], new_ms["w_in"], new_vs["w_in"] = (from_t(a) for a in upd_in)
    for k, nm in enumerate(names[1:]):
        grads[nm] = g_big[k + 1][None]
        deltas[nm], new_ms[nm], new_vs[nm] = (a[None] for a in upd_big[k])
    for k, nm in enumerate(small_names):
        lead = (lambda a: a[None]) if nm == "conv_w" else (lambda a: a)
        grads[nm] = lead(small_g[k])
        deltas[nm], new_ms[nm], new_vs[nm] = (lead(upd_small[j][k]) for j in range(3))
    return (loss, grad_x[None], *[grads[nm] for nm in order], *[deltas[nm] for nm in order],
            *[new_ms[nm] for nm in order], *[new_vs[nm] for nm in order])
```

```python
import functools
import math

import numpy as np
import jax
import jax.numpy as jnp
from jax import lax
from jax.experimental import pallas as pl
from jax.experimental.pallas import tpu as pltpu

F32 = jnp.float32
BF16 = jnp.bfloat16
SDS = jax.ShapeDtypeStruct
MESH = pl.DeviceIdType.MESH

HEAD_DIM = 64
N_HEADS = 8
LANES = 128
ATT_W = N_HEADS * HEAD_DIM
DIL_PATTERNS = ((128, 1), (512, 4), (2048, 16))
DIL_BLK = 128
ROPE_DIM = HEAD_DIM // 4
ROPE_THETA = 500000.0
RMS_EPS = 1e-6
NEG = -1e30
QK_SCALE = 1.0 / math.sqrt(HEAD_DIM)
ADAM_LR, ADAM_B1, ADAM_B2, ADAM_EPS, ADAM_WD, ADAM_STEP = 0.001, 0.9, 0.999, 1e-08, 0.01, 10
VMEM_LIMIT = 56 * 1024 * 1024

Z_QA, Z_KA, Z_VA, Z_QB, Z_KB, Z_VB = 0, 1, 2, 3, 4, 5
Z_W = 5120


def _cp(sem):
    return pltpu.CompilerParams(dimension_semantics=sem, vmem_limit_bytes=VMEM_LIMIT)


def _nt(a, b):
    return lax.dot_general(a, b, (((1,), (1,)), ((), ())), preferred_element_type=F32)


def _tn(a, b):
    return lax.dot_general(a, b, (((0,), (0,)), ((), ())), preferred_element_type=F32)


def _nn(a, b):
    return jnp.dot(a, b, preferred_element_type=F32)


def _lane(shape):
    return lax.broadcasted_iota(jnp.int32, shape, 1)


def _row(shape):
    return lax.broadcasted_iota(jnp.int32, shape, 0)


def rmsnorm_fwd(x, g, *, tm=1024):
    s, d = x.shape

    def body(x_ref, g_ref, h_ref):
        xv = x_ref[...]
        inv = lax.rsqrt(jnp.mean(xv * xv, axis=-1, keepdims=True) + RMS_EPS)
        h_ref[...] = (xv * inv * g_ref[...]).astype(h_ref.dtype)

    return pl.pallas_call(
        body, grid=(s // tm,),
        in_specs=[pl.BlockSpec((tm, d), lambda i: (i, 0)), pl.BlockSpec((1, d), lambda i: (0, 0))],
        out_specs=pl.BlockSpec((tm, d), lambda i: (i, 0)),
        out_shape=SDS((s, d), BF16), name="rmsnorm_fwd", compiler_params=_cp(("parallel",)))(x, g)


def mm(a_views, b_views, *, nt, out_dtype, tm, tn, name):
    n_p = len(a_views)
    m = a_views[0][0].shape[0]
    n = b_views[0][0].shape[0] if nt else b_views[0][0].shape[1]

    def body(*refs):
        o_ref = refs[-1]
        acc = None
        for p in range(n_p):
            av = refs[p][...].astype(BF16)
            bv = refs[n_p + p][...].astype(BF16)
            dv = _nt(av, bv) if nt else _nn(av, bv)
            acc = dv if acc is None else acc + dv
        o_ref[...] = acc.astype(o_ref.dtype)

    in_specs = []
    for arr, w, blk in a_views:
        in_specs.append(pl.BlockSpec((tm, w), functools.partial(lambda i, j, blk: (i, blk), blk=blk)))
    for arr, w, blk in b_views:
        if nt:
            in_specs.append(pl.BlockSpec((tn, w), functools.partial(lambda i, j, blk: (j, blk), blk=blk)))
        else:
            in_specs.append(pl.BlockSpec((w, tn), lambda i, j: (0, j)))
    return pl.pallas_call(
        body, grid=(m // tm, n // tn), in_specs=in_specs,
        out_specs=pl.BlockSpec((tm, tn), lambda i, j: (i, j)),
        out_shape=SDS((m, n), out_dtype), name=name,
        compiler_params=_cp(("parallel", "parallel")))(*[a[0] for a in a_views], *[b[0] for b in b_views])


def wgrad(a_view, g, *, tk, tn, ts, name, chip_major=False, slabs=None, into=None, bf16_copy=False):
    arr, ka, blk = a_view
    s, n = g.shape
    ns = s // ts
    total, first = slabs if slabs else (n // tn, 0)
    n_into = 0 if into is None else (2 if bf16_copy else 1)

    def body(a_ref, g_ref, *rest):
        o_ref = rest[n_into]

        @pl.when(pl.program_id(2) == 0)
        def _():
            o_ref[...] = jnp.zeros_like(o_ref)

        o_ref[...] += _tn(a_ref[...].astype(BF16), g_ref[...].astype(BF16))
        if bf16_copy:
            @pl.when(pl.program_id(2) == ns - 1)
            def _():
                rest[n_into + 1][...] = o_ref[...].astype(BF16)

    if chip_major:
        out_spec = pl.BlockSpec((None, tk, tn), lambda i, j, k: (first + j, i, 0))
        shape = (total, ka, tn)
    else:
        out_spec = pl.BlockSpec((tk, tn), lambda i, j, k: (i, j))
        shape = (ka, n)
    in_specs = [pl.BlockSpec((ts, tk), lambda i, j, k: (k, blk * (ka // tk) + i)),
                pl.BlockSpec((ts, tn), lambda i, j, k: (k, j))]
    args = [arr, g]
    if into is not None:
        earlier = list(into) if bf16_copy else [into]
        in_specs += [pl.BlockSpec(memory_space=pl.ANY)] * len(earlier)
        args += earlier
    out = pl.pallas_call(
        body, grid=(ka // tk, n // tn, ns), in_specs=in_specs,
        out_specs=[out_spec, out_spec] if bf16_copy else out_spec,
        out_shape=[SDS(shape, F32), SDS(shape, BF16)] if bf16_copy else SDS(shape, F32), name=name,
        input_output_aliases={2 + k: k for k in range(n_into)},
        compiler_params=_cp(("parallel", "parallel", "arbitrary")))(*args)
    return out


def _norm_bwd_rows(dh, xh, inv, g):
    dxh = dh * g
    dx = inv * (dxh - xh * jnp.mean(dxh * xh, axis=-1, keepdims=True))
    return dx, jnp.sum((dh * xh).reshape(dh.shape[0] // 8, 8, dh.shape[1]), axis=0)


def proj_norm_res(a, w, g, xres, g_next, *, tm=512, name):
    s, k = a.shape
    d = w.shape[1]

    def body(a_ref, w_ref, g_ref, x_ref, gn_ref, y_ref, o_ref, h_ref):
        y = _nn(a_ref[...], w_ref[...])
        inv = lax.rsqrt(jnp.mean(y * y, axis=-1, keepdims=True) + RMS_EPS)
        xn = x_ref[...] + y * inv * g_ref[...]
        y_ref[...] = y
        o_ref[...] = xn
        inv_n = lax.rsqrt(jnp.mean(xn * xn, axis=-1, keepdims=True) + RMS_EPS)
        h_ref[...] = (xn * inv_n * gn_ref[...]).astype(h_ref.dtype)

    row = pl.BlockSpec((tm, d), lambda i: (i, 0))
    vec = pl.BlockSpec((1, d), lambda i: (0, 0))
    return pl.pallas_call(
        body, grid=(s // tm,),
        in_specs=[pl.BlockSpec((tm, k), lambda i: (i, 0)), pl.BlockSpec((k, d), lambda i: (0, 0)), vec, row, vec],
        out_specs=[row, row, row], out_shape=[SDS((s, d), F32), SDS((s, d), F32), SDS((s, d), BF16)], name=name,
        compiler_params=_cp(("parallel",)))(a, w, g, xres, g_next)


def proj_norm_loss(a, w, g, xres, target, *, tm=512, name):
    s, k = a.shape
    d = w.shape[1]
    n = s // tm

    def body(a_ref, w_ref, g_ref, x_ref, t_ref, do_ref, dy_ref, dg_ref, l_ref, acc):
        i = pl.program_id(0)

        @pl.when(i == 0)
        def _():
            acc[...] = jnp.zeros_like(acc)
            l_ref[...] = jnp.zeros_like(l_ref)

        y = _nn(a_ref[...], w_ref[...])
        inv = lax.rsqrt(jnp.mean(y * y, axis=-1, keepdims=True) + RMS_EPS)
        yh = y * inv
        err = x_ref[...] + yh * g_ref[...] - t_ref[...]
        dout = err * (1.0 / d)
        do_ref[...] = dout
        l_ref[...] += jnp.sum(jnp.sum(err * err, axis=1, keepdims=True), axis=0, keepdims=True)
        dy, part = _norm_bwd_rows(dout, yh, inv, g_ref[...])
        dy_ref[...] = dy.astype(dy_ref.dtype)
        acc[...] += part

        @pl.when(i == n - 1)
        def _():
            dg_ref[...] = jnp.sum(acc[...], axis=0, keepdims=True)

    row = pl.BlockSpec((tm, d), lambda i: (i, 0))
    vec = pl.BlockSpec((1, d), lambda i: (0, 0))
    return pl.pallas_call(
        body, grid=(n,),
        in_specs=[pl.BlockSpec((tm, k), lambda i: (i, 0)), pl.BlockSpec((k, d), lambda i: (0, 0)), vec, row, row],
        out_specs=[row, row, vec, pl.BlockSpec((1, 1), lambda i: (0, 0))],
        out_shape=[SDS((s, d), F32), SDS((s, d), BF16), SDS((1, d), F32), SDS((1, 1), F32)],
        scratch_shapes=[pltpu.VMEM((8, d), F32)], name=name, compiler_params=_cp(("arbitrary",)))(a, w, g, xres, target)


def mm_norm_bwd(a_views, b_views, stages, exchange=(), *, tm=256, name):
    n_p, n_s, ne = len(a_views), len(stages), len(exchange)
    s = a_views[0][0].shape[0]
    d = b_views[0][0].shape[0]
    n = s // tm
    has_res = [st[2] is not None for st in stages]

    def body(*refs):
        a_refs, b_refs = refs[:n_p], refs[n_p:2 * n_p]
        at = 2 * n_p
        st_refs = []
        for k in range(n_s):
            cnt = 3 if has_res[k] else 2
            st_refs.append(refs[at:at + cnt])
            at += cnt
        e_ins = refs[at:at + ne]
        at += ne
        dx_refs, dg_refs = refs[at:at + n_s], refs[at + n_s:at + 2 * n_s]
        at += 2 * n_s
        e_outs = refs[at:at + ne]
        at += ne
        accs = refs[at:at + n_s]
        comm = (e_ins, e_outs) + tuple(refs[at + n_s:])
        i = pl.program_id(0)

        @pl.when(i == 0)
        def _():
            for acc in accs:
                acc[...] = jnp.zeros_like(acc)
            if ne:
                _to_chips_start(*comm)

        dh = None
        for p in range(n_p):
            part = _nt(a_refs[p][...].astype(BF16), b_refs[p][...].astype(BF16))
            dh = part if dh is None else dh + part
        for k in range(n_s):
            xv = st_refs[k][0][...]
            inv = lax.rsqrt(jnp.mean(xv * xv, axis=-1, keepdims=True) + RMS_EPS)
            dx, part = _norm_bwd_rows(dh, xv * inv, inv, st_refs[k][1][...])
            if has_res[k]:
                dx = dx + st_refs[k][2][...]
            dx_refs[k][...] = dx.astype(dx_refs[k].dtype)
            accs[k][...] += part
            dh = dx

        @pl.when(i == n - 1)
        def _():
            for k in range(n_s):
                dg_refs[k][...] = jnp.sum(accs[k][...], axis=0, keepdims=True)
            if ne:
                _to_chips_finish(*comm)

    row = pl.BlockSpec((tm, d), lambda i: (i, 0))
    vec = pl.BlockSpec((1, d), lambda i: (0, 0))
    in_specs, args = [], []
    for arr, w, blk in a_views:
        in_specs.append(pl.BlockSpec((tm, w), functools.partial(lambda i, blk: (i, blk), blk=blk)))
        args.append(arr)
    for arr, w, blk in b_views:
        in_specs.append(pl.BlockSpec((d, w), functools.partial(lambda i, blk: (0, blk), blk=blk)))
        args.append(arr)
    for x, g, res, _ in stages:
        in_specs += [row, vec] + ([row] if res is not None else [])
        args += [x, g] + ([res] if res is not None else [])
    return pl.pallas_call(
        body, grid=(n,), in_specs=in_specs + [ANY] * ne,
        out_specs=[row] * n_s + [vec] * n_s + [ANY] * ne,
        out_shape=[SDS((s, d), st[3]) for st in stages] + [SDS((1, d), F32)] * n_s + _to_chips_shapes(exchange),
        scratch_shapes=[pltpu.VMEM((8, d), F32)] * n_s + (_to_chips_sems(ne) if ne else []), name=name,
        compiler_params=_cp(("arbitrary",)))(*args, *exchange)


def _split3(v):
    hi = v.astype(BF16).astype(F32)
    r = v - hi
    mid = r.astype(BF16).astype(F32)
    lo = (r - mid).astype(BF16).astype(F32)
    return hi, mid, lo


def _tri(n, upper):
    r = np.arange(n)
    m = (r[:, None] <= r[None, :]) if upper else (r[:, None] >= r[None, :])
    return jnp.asarray(m.astype(np.float32))


def fox_prep(z, fa, bfo, *, tb=512):
    s = z.shape[0]
    n = s // tb

    def body(q_ref, k_ref, v_ref, fa_ref, b_ref, tri_ref, qa_ref, ka_ref, va_ref, carry):
        @pl.when(pl.program_id(0) == 0)
        def _():
            carry[...] = jnp.zeros_like(carry)

        xv = fa_ref[...] + b_ref[...]
        logf = jnp.minimum(xv, 0.0) - jnp.log(1.0 + jnp.exp(-jnp.abs(xv)))
        csum = jnp.dot(tri_ref[...], logf, preferred_element_type=F32, precision=lax.Precision.HIGHEST) + carry[0:1, :]
        carry[0:1, :] = csum[tb - 1:tb, :]
        lane = _lane((tb, LANES))
        for h in range(N_HEADS):
            hi, mid, lo = _split3(csum[:, h:h + 1])
            pair = (h // 2) * LANES
            qv = q_ref[:, pair:pair + LANES].astype(F32)
            kv = k_ref[:, pair:pair + LANES].astype(F32)
            vv = v_ref[:, pair:pair + LANES].astype(F32)
            if h % 2:
                qv = pltpu.roll(qv, 64, axis=1)
                kv = pltpu.roll(kv, 64, axis=1)
                vv = pltpu.roll(vv, 64, axis=1)
            va_ref[:, h * LANES:(h + 1) * LANES] = jnp.where(lane < 64, vv, jnp.where(lane == 64, 1.0, 0.0)).astype(BF16)
            one = jnp.where((lane >= 67) & (lane < 70), 1.0, 0.0)
            q_x = jnp.where(lane == 64, hi, jnp.where(lane == 65, mid, jnp.where(lane == 66, lo, one)))
            one = jnp.where((lane >= 64) & (lane < 67), 1.0, 0.0)
            k_x = jnp.where(lane == 67, -hi, jnp.where(lane == 68, -mid, jnp.where(lane == 69, -lo, one)))
            qa_ref[:, h * LANES:(h + 1) * LANES] = jnp.where(lane < 64, qv * QK_SCALE, q_x).astype(BF16)
            ka_ref[:, h * LANES:(h + 1) * LANES] = jnp.where(lane < 64, kv, k_x).astype(BF16)

    return pl.pallas_call(
        body, grid=(n,),
        in_specs=[pl.BlockSpec((tb, ATT_W), lambda i: (i, Z_QA)), pl.BlockSpec((tb, ATT_W), lambda i: (i, Z_KA)),
                  pl.BlockSpec((tb, ATT_W), lambda i: (i, Z_VA)),
                  pl.BlockSpec((tb, LANES), lambda i: (i, 0)), pl.BlockSpec((1, LANES), lambda i: (0, 0)),
                  pl.BlockSpec((tb, tb), lambda i: (0, 0))],
        out_specs=[pl.BlockSpec((tb, N_HEADS * LANES), lambda i: (i, 0))] * 3,
        out_shape=[SDS((s, N_HEADS * LANES), BF16)] * 3,
        scratch_shapes=[pltpu.VMEM((8, LANES), F32)],
        name="fox_prep", compiler_params=_cp(("arbitrary",)))(z, z, z, fa, bfo, _tri(tb, False))


def _causal_pairs(n, k_major):
    if k_major:
        pairs = [(qi, kj) for kj in range(n) for qi in range(kj, n)]
    else:
        pairs = [(qi, kj) for qi in range(n) for kj in range(qi + 1)]
    return (jnp.asarray([p[0] for p in pairs], jnp.int32), jnp.asarray([p[1] for p in pairs], jnp.int32), len(pairs))


def fox_fwd(q_aug, k_aug, v_aug, gather=(), halved=(), *, t=1024, hps=4):
    s = v_aug.shape[0]
    qi_arr, kj_arr, n_pairs = _causal_pairs(s // t, False)
    ng = len(gather)
    n_groups = N_HEADS // hps

    def body(qi_ref, kj_ref, q_ref, k_ref, v_ref, *rest):
        g_ins, (o_ref, lse_ref), g_outs = rest[:ng], rest[ng:ng + 2], rest[ng + 2:2 * ng + 2]
        m_scr, acc_scr = rest[2 * ng + 2:2 * ng + 4]
        comm = (g_ins, g_outs) + tuple(rest[2 * ng + 4:]) + (list(halved),)
        step = pl.program_id(1)
        qi = qi_ref[step]
        kj = kj_ref[step]
        if ng:
            @pl.when((pl.program_id(0) == 0) & (step == 0))
            def _():
                _allgather_start(*comm)

        @pl.when(kj == 0)
        def _():
            m_scr[...] = jnp.full_like(m_scr, NEG)
            acc_scr[...] = jnp.zeros_like(acc_scr)

        def update(qs, ks, masked):
            nq, nk = qs.stop - qs.start, ks.stop - ks.start
            for i in range(hps):
                own = slice(i * LANES, (i + 1) * LANES)
                sc = _nt(q_ref[qs, own], k_ref[ks, own])
                if masked:
                    sc = jnp.where(_row((nq, nk)) >= _lane((nq, nk)), sc, NEG)
                m_prev = m_scr[i, qs]
                m_new = jnp.maximum(m_prev, jnp.max(sc, axis=-1, keepdims=True))
                p = jnp.exp((sc - jnp.tile(m_new, (1, nk // LANES))).astype(BF16))
                acc_scr[i, qs] = jnp.exp(m_prev - m_new) * acc_scr[i, qs] + _nn(p, v_ref[ks, own])
                m_scr[i, qs] = m_new

        whole, upper, lower = slice(0, t), slice(0, t // 2), slice(t // 2, t)

        @pl.when(kj < qi)
        def _():
            update(whole, whole, False)

        @pl.when(kj == qi)
        def _():
            update(upper, upper, True)
            update(lower, upper, False)
            update(lower, lower, True)
            lane = _lane((t, LANES))
            for pr in range(hps // 2):
                den = [acc_scr[2 * pr + i][:, 64:65] for i in range(2)]
                o_ref[:, pr * LANES:(pr + 1) * LANES] = jnp.where(
                    lane < 64, acc_scr[2 * pr] / den[0], pltpu.roll(acc_scr[2 * pr + 1] / den[1], 64, axis=1)).astype(o_ref.dtype)
                lse_ref[:, pr * LANES:(pr + 1) * LANES] = jnp.where(
                    lane < 64, m_scr[2 * pr] + jnp.log(den[0]), m_scr[2 * pr + 1] + jnp.log(den[1]))

        if ng:
            @pl.when((pl.program_id(0) == n_groups - 1) & (step == n_pairs - 1))
            def _():
                _allgather_finish(*comm)

    wide = hps * LANES
    grid_spec = pltpu.PrefetchScalarGridSpec(
        num_scalar_prefetch=2, grid=(n_groups, n_pairs),
        in_specs=[pl.BlockSpec((t, wide), lambda hg, st, qi, kj: (qi[st], hg)),
                  pl.BlockSpec((t, wide), lambda hg, st, qi, kj: (kj[st], hg)),
                  pl.BlockSpec((t, wide), lambda hg, st, qi, kj: (kj[st], hg))] + [ANY] * ng,
        out_specs=[pl.BlockSpec((t, wide // 2), lambda hg, st, qi, kj: (qi[st], hg))] * 2 + [ANY] * ng,
        scratch_shapes=[pltpu.VMEM((hps, t, LANES), F32)] * 2 + (_allgather_sems(ng) if ng else []))
    return pl.pallas_call(
        body, grid_spec=grid_spec, out_shape=[SDS((s, ATT_W), BF16), SDS((s, ATT_W), F32)] + _allgather_shapes(gather),
        name="fox_fwd", compiler_params=_cp(("arbitrary", "arbitrary")))(qi_arr, kj_arr, q_aug, k_aug, v_aug, *gather)


def fox_bwd(q_aug, k_aug, z, dy, lse, dd, exchange=(), kind="to_chips", *, t=1024, hps=4):
    s = z.shape[0]
    qi_arr, kj_arr, n_pairs = _causal_pairs(s // t, True)
    ne = len(exchange)
    n_groups = N_HEADS // hps
    x_shapes, x_sems, x_start, x_finish = EXCHANGES[kind]

    def body(qi_ref, kj_ref, q_ref, k_ref, v_ref, do_ref, lse_ref, dd_ref, *rest):
        e_ins, (dq_ref, dk_ref, dv_ref), e_outs = rest[:ne], rest[ne:ne + 3], rest[ne + 3:2 * ne + 3]
        comm = (e_ins, e_outs) + tuple(rest[2 * ne + 3:])
        step = pl.program_id(1)
        qi = qi_ref[step]
        kj = kj_ref[step]
        if ne:
            @pl.when((pl.program_id(0) == 0) & (step == 0))
            def _():
                x_start(*comm)

        @pl.when(step == 0)
        def _():
            dq_ref[...] = jnp.zeros_like(dq_ref)

        @pl.when(qi == kj)
        def _():
            dk_ref[...] = jnp.zeros_like(dk_ref)
            dv_ref[...] = jnp.zeros_like(dv_ref)

        def update(qs, ks, masked):
            nq, nk = qs.stop - qs.start, ks.stop - ks.start
            lane = _lane((nq, LANES))
            rows = pl.ds(pl.multiple_of(qi * t + qs.start, nq), nq)
            for pr in range(hps // 2):
                pair = slice(pr * LANES, (pr + 1) * LANES)
                dov = do_ref[qs, pair]
                dv_new = None
                for i in range(2):
                    head = (lane < 64) if i == 0 else (lane >= 64)
                    own = slice((2 * pr + i) * LANES, (2 * pr + i + 1) * LANES)
                    col = slice(pr * LANES + i * 64, pr * LANES + i * 64 + 1)
                    qv = q_ref[qs, own]
                    kv = k_ref[ks, own]
                    sc = _nt(qv, kv)
                    if masked:
                        sc = jnp.where(_row((nq, nk)) >= _lane((nq, nk)), sc, NEG)
                    p = jnp.exp(sc - lse_ref[qs, col])
                    dp = _nt(jnp.where(head, dov, jnp.zeros_like(dov)), v_ref[ks, pair])
                    ds = (p * (dp - dd_ref[qs, col])).astype(BF16)
                    dq_ref[rows, own] += _nn(ds, kv)
                    dk_ref[ks, own] += _tn(ds, qv)
                    dvi = _tn(p.astype(BF16), dov)
                    dv_new = dvi if dv_new is None else jnp.where(head, dvi, dv_new)
                dv_ref[ks, pair] += dv_new

        whole, upper, lower = slice(0, t), slice(0, t // 2), slice(t // 2, t)

        @pl.when(kj < qi)
        def _():
            update(whole, whole, False)

        @pl.when(kj == qi)
        def _():
            update(upper, upper, True)
            update(lower, upper, False)
            update(lower, lower, True)

        if ne:
            @pl.when((pl.program_id(0) == n_groups - 1) & (step == n_pairs - 1))
            def _():
                x_finish(*comm)

    wide, half = hps * LANES, hps // 2 * LANES
    v_blk = Z_VA * ATT_W // half
    grid_spec = pltpu.PrefetchScalarGridSpec(
        num_scalar_prefetch=2, grid=(n_groups, n_pairs),
        in_specs=[pl.BlockSpec((t, wide), lambda hg, st, qi, kj: (qi[st], hg)),
                  pl.BlockSpec((t, wide), lambda hg, st, qi, kj: (kj[st], hg)),
                  pl.BlockSpec((t, half), lambda hg, st, qi, kj: (kj[st], v_blk + hg)),
                  pl.BlockSpec((t, half), lambda hg, st, qi, kj: (qi[st], hg)),
                  pl.BlockSpec((t, half), lambda hg, st, qi, kj: (qi[st], hg)),
                  pl.BlockSpec((t, half), lambda hg, st, qi, kj: (qi[st], hg))] + [ANY] * ne,
        out_specs=[pl.BlockSpec((s, wide), lambda hg, st, qi, kj: (0, hg)),
                   pl.BlockSpec((t, wide), lambda hg, st, qi, kj: (kj[st], hg)),
                   pl.BlockSpec((t, half), lambda hg, st, qi, kj: (kj[st], hg))] + [ANY] * ne,
        scratch_shapes=x_sems(ne) if ne else [])
    return pl.pallas_call(
        body, grid_spec=grid_spec,
        out_shape=[SDS((s, N_HEADS * LANES), F32), SDS((s, N_HEADS * LANES), F32), SDS((s, ATT_W), F32)]
        + x_shapes(exchange),
        name="fox_bwd", compiler_params=_cp(("arbitrary", "arbitrary")))(qi_arr, kj_arr, q_aug, k_aug, z, dy, lse, dd, *exchange)


def fox_post(dq_aug, dk_aug, dv, fa, bfo, *, tb=512):
    s = dv.shape[0]
    n = s // tb

    def body(dq_ref, dk_ref, dv_ref, fa_ref, b_ref, tri_ref, dz_ref, dfa_ref, gb_ref, carry, acc):
        i = pl.program_id(0)

        @pl.when(i == 0)
        def _():
            carry[...] = jnp.zeros_like(carry)
            acc[...] = jnp.zeros_like(acc)

        lane = _lane((tb, LANES))
        d_f = jnp.zeros((tb, LANES), F32)
        for h in range(N_HEADS):
            col = dq_ref[:, h * LANES + 64:h * LANES + 65] - dk_ref[:, h * LANES + 67:h * LANES + 68]
            d_f = jnp.where(lane == h, col, d_f)
        suffix = jnp.dot(tri_ref[...], d_f, preferred_element_type=F32, precision=lax.Precision.HIGHEST) + carry[0:1, :]
        carry[0:1, :] = suffix[0:1, :]
        xv = fa_ref[...] + b_ref[...]
        dx = suffix * (1.0 / (1.0 + jnp.exp(xv)))
        dfa_ref[...] = dx.astype(dfa_ref.dtype)
        acc[...] += jnp.sum(dx.reshape(tb // 8, 8, LANES), axis=0)
        for hp in range(4):
            for src, off, scale in ((dq_ref, 0, QK_SCALE), (dk_ref, ATT_W, 1.0)):
                even = src[:, (2 * hp) * LANES:(2 * hp + 1) * LANES]
                odd = pltpu.roll(src[:, (2 * hp + 1) * LANES:(2 * hp + 2) * LANES], 64, axis=1)
                dz_ref[:, off + hp * LANES:off + (hp + 1) * LANES] = (jnp.where(lane < 64, even, odd) * scale).astype(BF16)
        dz_ref[:, 2 * ATT_W:3 * ATT_W] = dv_ref[...].astype(BF16)

        @pl.when(i == n - 1)
        def _():
            gb_ref[...] = jnp.sum(acc[...], axis=0, keepdims=True)

    rev = lambda i: (n - 1 - i, 0)
    return pl.pallas_call(
        body, grid=(n,),
        in_specs=[pl.BlockSpec((tb, N_HEADS * LANES), rev), pl.BlockSpec((tb, N_HEADS * LANES), rev),
                  pl.BlockSpec((tb, ATT_W), rev), pl.BlockSpec((tb, LANES), rev),
                  pl.BlockSpec((1, LANES), lambda i: (0, 0)), pl.BlockSpec((tb, tb), lambda i: (0, 0))],
        out_specs=[pl.BlockSpec((tb, 3 * ATT_W), rev), pl.BlockSpec((tb, LANES), rev),
                   pl.BlockSpec((1, LANES), lambda i: (0, 0))],
        out_shape=[SDS((s, 3 * ATT_W), BF16), SDS((s, LANES), BF16), SDS((1, LANES), F32)],
        scratch_shapes=[pltpu.VMEM((8, LANES), F32), pltpu.VMEM((8, LANES), F32)],
        name="fox_post", compiler_params=_cp(("arbitrary",)))(dq_aug, dk_aug, dv, fa, bfo, _tri(tb, True))


def rope_cos_sin(s):
    half = ROPE_DIM // 2
    inv_freq = ROPE_THETA ** (-jnp.arange(half, dtype=F32) * 2.0 / ROPE_DIM)
    ang = jnp.arange(s, dtype=F32)[:, None] * inv_freq[None, :]
    return jnp.tile(jnp.cos(ang), (1, LANES // half)), jnp.tile(jnp.sin(ang), (1, LANES // half))


def _rotate(x, cos, sin, sign):
    l64 = _lane(x.shape) & (HEAD_DIM - 1)
    first = l64 < ROPE_DIM // 2
    second = (l64 >= ROPE_DIM // 2) & (l64 < ROPE_DIM)
    from_next = jnp.where(first, -sign * sin, 0.0)
    from_prev = jnp.where(second, sign * sin, 0.0)
    return (x * jnp.where(first | second, cos, 1.0) + pltpu.roll(x, LANES - 8, axis=1) * from_next
            + pltpu.roll(x, 8, axis=1) * from_prev)


def _dil_rows(base, r):
    if r == 1:
        return pl.ds(pl.multiple_of(base, DIL_BLK), DIL_BLK)
    return pl.ds(base, DIL_BLK, stride=r)


def _dil_block(idx, r, nb):
    shift = nb.bit_length() - 1
    rho = idx >> shift
    n = idx & (nb - 1)
    base = rho + n * (r * DIL_BLK)
    return _dil_rows(base, r), _dil_rows(jnp.maximum(base - r * DIL_BLK, rho), r), n > 0


def _cat(a, b):
    return jnp.concatenate([a, b], axis=0)


def _two_heads(v, first_head):
    zero = jnp.zeros_like(v)
    return _cat(jnp.where(first_head, v, zero), jnp.where(first_head, zero, v))


def _dil_bands():
    b = DIL_BLK
    q = _row((2 * b, 2 * b)) & (b - 1)
    col = _lane((2 * b, 2 * b))
    return (col < b) & (col >= q), (col >= b) & (col - b <= q)


def _dil_load_qkv(zq_ref, zk_ref, zv_ref, cos_ref, sin_ref, q_ref, k_ref, v_ref, *, chunk=512):
    def step(i, carry):
        rows = pl.ds(pl.multiple_of(i * chunk, chunk), chunk)
        cos, sin = cos_ref[rows, :], sin_ref[rows, :]
        q_ref[rows, :] = _rotate(zq_ref[rows, :].astype(F32), cos, sin, 1.0) * QK_SCALE
        k_ref[rows, :] = _rotate(zk_ref[rows, :].astype(F32), cos, sin, 1.0)
        v_ref[rows, :] = zv_ref[rows, :].astype(F32)
        return carry

    lax.fori_loop(0, q_ref.shape[0] // chunk, step, 0)


def dil_fwd_all(z, cos_t, sin_t, *, unroll=32):
    s = z.shape[0]
    b = DIL_BLK
    n_blk = s // b

    def body(zq_ref, zk_ref, zv_ref, cos_ref, sin_ref, o_ref, l_ref, q_ref, k_ref, v_ref):
        _dil_load_qkv(zq_ref, zk_ref, zv_ref, cos_ref, sin_ref, q_ref, k_ref, v_ref)
        first_head = _lane((b, LANES)) < 64
        band_prev, band_cur = _dil_bands()
        for g, (_, r) in enumerate(DIL_PATTERNS):
            nb = n_blk // r

            def group(it, carry, g=g, r=r, nb=nb):
                loaded = []
                kc = vc = None
                for u in range(unroll):
                    rows_c, rows_p, has_prev = _dil_block(it * unroll + u, r, nb)
                    if u % min(nb, unroll):
                        kp, vp = kc, vc
                    else:
                        kp, vp = k_ref[rows_p, :].astype(BF16), v_ref[rows_p, :].astype(BF16)
                    kc, vc = k_ref[rows_c, :].astype(BF16), v_ref[rows_c, :].astype(BF16)
                    state = (o_ref[rows_c, :], l_ref[rows_c, :]) if g else None
                    loaded.append((rows_c, has_prev, [q_ref[rows_c, :].astype(BF16), kp, kc, vp, vc], state))
                done = []
                for rows_c, has_prev, (qv, kp, kc, vp, vc), state in loaded:
                    sc = jnp.where(band_cur | (band_prev & has_prev), _nt(_two_heads(qv, first_head), _cat(kp, kc)), NEG)
                    m = jnp.max(sc, axis=-1, keepdims=True)
                    p = jnp.exp(sc - m)
                    den = jnp.sum(p, axis=-1, keepdims=True)
                    both = _nn(p.astype(BF16), _cat(vp, vc)) / den
                    lse2 = m + jnp.log(den)
                    ov = jnp.where(first_head, both[:b], both[b:])
                    lse = jnp.where(first_head, lse2[:b], lse2[b:])
                    if state is not None:
                        m2 = jnp.maximum(state[1], lse)
                        wp = jnp.exp(state[1] - m2)
                        wn = jnp.exp(lse - m2)
                        ov = (wp * state[0] + wn * ov) / (wp + wn)
                        lse = m2 + jnp.log(wp + wn)
                    done.append((rows_c, ov, lse))
                for rows_c, ov, lse in done:
                    o_ref[rows_c, :] = ov
                    l_ref[rows_c, :] = lse
                return carry

            lax.fori_loop(0, n_blk // unroll, group, 0)

    col_blk = lambda k: pl.BlockSpec((s, LANES), lambda hp: (0, 4 * k + hp))
    table = pl.BlockSpec((s, LANES), lambda hp: (0, 0))
    out = pl.BlockSpec((s, LANES), lambda hp: (0, hp))
    return pl.pallas_call(
        body, grid=(4,), in_specs=[col_blk(Z_QB), col_blk(Z_KB), col_blk(Z_VB), table, table], out_specs=[out, out],
        out_shape=[SDS((s, ATT_W), F32)] * 2, scratch_shapes=[pltpu.VMEM((s, LANES), F32)] * 3, name="dil_fwd",
        compiler_params=_cp(("parallel",)))(z, z, z, cos_t, sin_t)


def dil_bwd_all(z, cos_t, sin_t, dy, lse, y, exchange=(), kind="to_chips", *, unroll=16):
    s = z.shape[0]
    b = DIL_BLK
    n_blk = s // b
    ne = len(exchange)
    x_shapes, x_sems, x_start, x_finish = EXCHANGES[kind]

    def body(zq_ref, zk_ref, zv_ref, cos_ref, sin_ref, do_ref, l_ref, y_ref, *rest):
        e_ins, (gq_ref, gk_ref, gv_ref), e_outs = rest[:ne], rest[ne:ne + 3], rest[ne + 3:2 * ne + 3]
        q_ref, k_ref, v_ref, dq_ref, dk_ref, dv_ref = rest[2 * ne + 3:2 * ne + 9]
        comm = (e_ins, e_outs) + tuple(rest[2 * ne + 9:])
        if ne:
            @pl.when(pl.program_id(0) == 0)
            def _():
                x_start(*comm)

        _dil_load_qkv(zq_ref, zk_ref, zv_ref, cos_ref, sin_ref, q_ref, k_ref, v_ref)
        dq_ref[...] = jnp.zeros_like(dq_ref)
        dk_ref[...] = jnp.zeros_like(dk_ref)
        dv_ref[...] = jnp.zeros_like(dv_ref)
        first_head = _lane((b, LANES)) < 64
        band_prev, band_cur = _dil_bands()
        for _, r in DIL_PATTERNS:
            nb = n_blk // r

            def group(it, carry, r=r, nb=nb):
                loaded = []
                kc = vc = None
                for u in range(unroll):
                    rows_c, rows_p, has_prev = _dil_block(it * unroll + u, r, nb)
                    if u % min(nb, unroll):
                        kp, vp = kc, vc
                    else:
                        kp, vp = k_ref[rows_p, :].astype(BF16), v_ref[rows_p, :].astype(BF16)
                    kc, vc = k_ref[rows_c, :].astype(BF16), v_ref[rows_c, :].astype(BF16)
                    vals = [q_ref[rows_c, :].astype(BF16), kp, kc, vp, vc, do_ref[rows_c, :], l_ref[rows_c, :], y_ref[rows_c, :]]
                    loaded.append((rows_c, rows_p, has_prev, vals))
                done = []
                for rows_c, rows_p, has_prev, (qv, kp, kc, vp, vc, dof, lv, yv) in loaded:
                    q2 = _two_heads(qv, first_head)
                    do2 = _two_heads(dof.astype(BF16), first_head)
                    kcat, vcat = _cat(kp, kc), _cat(vp, vc)
                    lse2 = _cat(lv[:, 0:1], lv[:, 64:65])
                    dd2 = jnp.sum(_two_heads(dof * yv, first_head), axis=-1, keepdims=True)
                    p = jnp.exp(jnp.where(band_cur | (band_prev & has_prev), _nt(q2, kcat), NEG) - lse2)
                    ds = (p * (_nt(do2, vcat) - dd2)).astype(BF16)
                    dq2 = _nn(ds, kcat)
                    dkcat = _tn(ds, q2)
                    dvcat = _tn(p.astype(BF16), do2)
                    done.append((rows_c, rows_p, (jnp.where(first_head, dq2[:b], dq2[b:]), dkcat[:b], dkcat[b:],
                                                  dvcat[:b], dvcat[b:])))
                held = None
                for u, (rows_c, rows_p, (dq, dk_p, dk_c, dv_p, dv_c)) in enumerate(done):
                    dq_ref[rows_c, :] += dq
                    if u % min(nb, unroll):
                        rows_h, dk_h, dv_h = held
                        dk_ref[rows_h, :] += dk_h + dk_p
                        dv_ref[rows_h, :] += dv_h + dv_p
                    else:
                        if held is not None:
                            dk_ref[held[0], :] += held[1]
                            dv_ref[held[0], :] += held[2]
                        dk_ref[rows_p, :] += dk_p
                        dv_ref[rows_p, :] += dv_p
                    held = (rows_c, dk_c, dv_c)
                dk_ref[held[0], :] += held[1]
                dv_ref[held[0], :] += held[2]
                return carry

            lax.fori_loop(0, n_blk // unroll, group, 0)

        def finish(i, carry, chunk=512):
            rows = pl.ds(pl.multiple_of(i * chunk, chunk), chunk)
            cos, sin = cos_ref[rows, :], sin_ref[rows, :]
            gq_ref[rows, :] = (_rotate(dq_ref[rows, :], cos, sin, -1.0) * QK_SCALE).astype(BF16)
            gk_ref[rows, :] = _rotate(dk_ref[rows, :], cos, sin, -1.0).astype(BF16)
            gv_ref[rows, :] = dv_ref[rows, :].astype(BF16)
            return carry

        lax.fori_loop(0, s // 512, finish, 0)
        if ne:
            @pl.when(pl.program_id(0) == 3)
            def _():
                x_finish(*comm)

    col_blk = lambda k: pl.BlockSpec((s, LANES), lambda hp: (0, 4 * k + hp))
    table = pl.BlockSpec((s, LANES), lambda hp: (0, 0))
    nat = pl.BlockSpec((s, LANES), lambda hp: (0, hp))
    return pl.pallas_call(
        body, grid=(4,), in_specs=[col_blk(Z_QB), col_blk(Z_KB), col_blk(Z_VB), table, table, nat, nat, nat] + [ANY] * ne,
        out_specs=[nat, nat, nat] + [ANY] * ne, out_shape=[SDS((s, ATT_W), BF16)] * 3 + x_shapes(exchange),
        scratch_shapes=[pltpu.VMEM((s, LANES), F32)] * 6 + (x_sems(ne) if ne else []), name="dil_bwd",
        compiler_params=_cp(("arbitrary",)))(z, z, z, cos_t, sin_t, dy, lse, y, *exchange)


def _sigmoid(v):
    return 1.0 / (1.0 + jnp.exp(-v))


def gate_mix(ya, yb, wa, wb, z, *, tm=2048, tn=512):
    s = ya.shape[0]
    d = wa.shape[1]
    ga_blk = 3 * ATT_W * 2 // tn
    gb_blk = ga_blk + d // tn

    def body(ya_ref, yb_ref, wa_ref, wb_ref, ga_ref, gb_ref, pa_ref, pb_ref, mx_ref):
        pa = _nn(ya_ref[...], wa_ref[...])
        pb = _nn(yb_ref[...].astype(BF16), wb_ref[...])
        pa_ref[...] = pa.astype(BF16)
        pb_ref[...] = pb.astype(BF16)
        mx_ref[...] = (_sigmoid(ga_ref[...].astype(F32)) * pa + _sigmoid(gb_ref[...].astype(F32)) * pb).astype(BF16)

    out = pl.BlockSpec((tm, tn), lambda i, j: (i, j))
    return pl.pallas_call(
        body, grid=(s // tm, d // tn),
        in_specs=[pl.BlockSpec((tm, ATT_W), lambda i, j: (i, 0)), pl.BlockSpec((tm, ATT_W), lambda i, j: (i, 0)),
                  pl.BlockSpec((ATT_W, tn), lambda i, j: (0, j)), pl.BlockSpec((ATT_W, tn), lambda i, j: (0, j)),
                  pl.BlockSpec((tm, tn), lambda i, j: (i, ga_blk + j)), pl.BlockSpec((tm, tn), lambda i, j: (i, gb_blk + j))],
        out_specs=[out, out, out], out_shape=[SDS((s, d), BF16)] * 3, name="gate_mix",
        compiler_params=_cp(("parallel", "parallel")))(ya, yb, wa, wb, z, z)


def mix_bwd(dy, w_o, z, pa, pb, wo_a, wo_b, ya, *, tm=512):
    s, d = dy.shape

    def body(dy_ref, wo_ref, ga_ref, gb_ref, pa_ref, pb_ref, wa_ref, wb_ref, ya_ref,
             dpa_ref, dpb_ref, dg_ref, dya_ref, dyb_ref, dd_ref):
        dm = _nt(dy_ref[...], wo_ref[...])
        sa = _sigmoid(ga_ref[...].astype(F32))
        sb = _sigmoid(gb_ref[...].astype(F32))
        dpa = (dm * sa).astype(BF16)
        dpb = (dm * sb).astype(BF16)
        dpa_ref[...] = dpa
        dpb_ref[...] = dpb
        dg_ref[:, 0:d] = (dm * pa_ref[...].astype(F32) * sa * (1.0 - sa)).astype(BF16)
        dg_ref[:, d:2 * d] = (dm * pb_ref[...].astype(F32) * sb * (1.0 - sb)).astype(BF16)
        dya = _nt(dpa, wa_ref[...]).astype(BF16)
        dya_ref[...] = dya
        dyb_ref[...] = _nt(dpb, wb_ref[...])
        lane = _lane((tm, LANES))
        for pr in range(ATT_W // LANES):
            pair = slice(pr * LANES, (pr + 1) * LANES)
            prod = dya[:, pair].astype(F32) * ya_ref[:, pair].astype(F32)
            lo = jnp.sum(jnp.where(lane < 64, prod, 0.0), axis=-1, keepdims=True)
            hi = jnp.sum(jnp.where(lane >= 64, prod, 0.0), axis=-1, keepdims=True)
            dd_ref[:, pair] = jnp.where(lane < 64, lo, hi)

    row = pl.BlockSpec((tm, d), lambda i: (i, 0))
    att = pl.BlockSpec((tm, ATT_W), lambda i: (i, 0))
    whole = lambda a: pl.BlockSpec(a.shape, lambda i: (0, 0))
    return pl.pallas_call(
        body, grid=(s // tm,),
        in_specs=[row, whole(w_o), pl.BlockSpec((tm, d), lambda i: (i, 3)), pl.BlockSpec((tm, d), lambda i: (i, 4)), row, row,
                  whole(wo_a), whole(wo_b), att],
        out_specs=[row, row, pl.BlockSpec((tm, 2 * d), lambda i: (i, 0)), att, att, att],
        out_shape=[SDS((s, d), BF16), SDS((s, d), BF16), SDS((s, 2 * d), BF16), SDS((s, ATT_W), BF16),
                   SDS((s, ATT_W), F32), SDS((s, ATT_W), F32)], name="mix_bwd",
        compiler_params=_cp(("parallel",)))(dy, w_o, z, z, pa, pb, wo_a, wo_b, ya)


GELU_C = math.sqrt(2.0 / math.pi)


def _gelu_parts(a):
    a2 = a * a
    th = jnp.tanh(a * (GELU_C + (GELU_C * 0.044715) * a2))
    half = 0.5 * a
    gelu = half + half * th
    dgelu = (0.5 + 0.5 * th) + half * (1.0 - th * th) * (GELU_C + (3.0 * GELU_C * 0.044715) * a2)
    return gelu, dgelu


def _causal_taps(u, before):
    row = _row(u.shape)
    r1 = jnp.where(row == 0, before[7:8, :], pltpu.roll(u, 1, axis=0))
    r2 = jnp.where(row == 0, before[6:7, :], jnp.where(row == 1, before[7:8, :], pltpu.roll(u, 2, axis=0)))
    return r1, r2


def ffn_up(h, wa, wb, cw, cb, *, tm=2048, tn=256):
    s, d = h.shape
    f = wa.shape[1]
    nj = f // tn

    def body(h_ref, wa_ref, wb_ref, cwa_ref, cwb_ref, cba_ref, cbb_ref, ua_ref, ub_ref, ca_ref, cbo_ref, m_ref, carry):
        @pl.when(pl.program_id(1) == 0)
        def _():
            carry[...] = jnp.zeros_like(carry)

        conv = []
        for k, (w_ref, cw_ref, cb_ref, u_ref, c_ref) in enumerate(((wa_ref, cwa_ref, cba_ref, ua_ref, ca_ref),
                                                                   (wb_ref, cwb_ref, cbb_ref, ub_ref, cbo_ref))):
            u = _nn(h_ref[...], w_ref[...])
            u_ref[...] = u.astype(BF16)
            r1, r2 = _causal_taps(u, carry[k])
            carry[k] = u[tm - 8:tm, :]
            conv.append(cw_ref[0:1, :] * r2 + cw_ref[1:2, :] * r1 + cw_ref[2:3, :] * u + cb_ref[...])
            c_ref[...] = conv[k].astype(BF16)
        m_ref[...] = (_gelu_parts(conv[0])[0] * conv[1]).astype(BF16)

    out = pl.BlockSpec((tm, tn), lambda j, i: (i, j))
    return pl.pallas_call(
        body, grid=(nj, s // tm),
        in_specs=[pl.BlockSpec((tm, d), lambda j, i: (i, 0)),
                  pl.BlockSpec((d, tn), lambda j, i: (0, j)), pl.BlockSpec((d, tn), lambda j, i: (0, j)),
                  pl.BlockSpec((3, tn), lambda j, i: (0, j)), pl.BlockSpec((3, tn), lambda j, i: (0, nj + j)),
                  pl.BlockSpec((1, tn), lambda j, i: (0, j)), pl.BlockSpec((1, tn), lambda j, i: (0, nj + j))],
        out_specs=[out] * 5, out_shape=[SDS((s, f), BF16)] * 5,
        scratch_shapes=[pltpu.VMEM((2, 8, tn), F32)], name="ffn_up",
        compiler_params=_cp(("parallel", "arbitrary")))(h, wa, wb, cw, cw, cb, cb)


def ffn_bwd(dm, ua, ub, ca, cbo, cw, *, tm=2048, tn=256):
    s, f = dm.shape
    nj = f // tn
    ni = s // tm

    def body(dm_ref, ua_ref, ub_ref, ca_ref, cbo_ref, cwa_ref, cwb_ref, dua_ref, dub_ref, ga_ref, gb_ref, carry):
        @pl.when(pl.program_id(1) == 0)
        def _():
            carry[...] = jnp.zeros_like(carry)
            ga_ref[...] = jnp.zeros_like(ga_ref)
            gb_ref[...] = jnp.zeros_like(gb_ref)

        row = _row((tm, tn))
        dmv = dm_ref[...].astype(F32)
        gelu, dgelu = _gelu_parts(ca_ref[...].astype(F32))
        dcs = (dmv * cbo_ref[...].astype(F32) * dgelu, dmv * gelu)
        for k, (dc, u_ref, cw_ref, du_ref, g_ref) in enumerate(((dcs[0], ua_ref, cwa_ref, dua_ref, ga_ref),
                                                                (dcs[1], ub_ref, cwb_ref, dub_ref, gb_ref))):
            u = u_ref[...].astype(F32)
            after = carry[k]
            n1 = jnp.where(row == tm - 1, after[0:1, :], pltpu.roll(dc, tm - 1, axis=0))
            n2 = jnp.where(row == tm - 2, after[0:1, :], jnp.where(row == tm - 1, after[1:2, :], pltpu.roll(dc, tm - 2, axis=0)))
            g_ref[0:1, :] += jnp.sum(n2 * u, axis=0, keepdims=True)
            g_ref[1:2, :] += jnp.sum(n1 * u, axis=0, keepdims=True)
            g_ref[2:3, :] += jnp.sum(dc * u, axis=0, keepdims=True)
            g_ref[3:4, :] += jnp.sum(dc, axis=0, keepdims=True)
            du_ref[...] = (cw_ref[2:3, :] * dc + cw_ref[1:2, :] * n1 + cw_ref[0:1, :] * n2).astype(BF16)
            carry[k] = dc[0:8, :]

    tile = pl.BlockSpec((tm, tn), lambda j, i: (ni - 1 - i, j))
    gspec = pl.BlockSpec((8, tn), lambda j, i: (0, j))
    return pl.pallas_call(
        body, grid=(nj, ni),
        in_specs=[tile] * 5 + [pl.BlockSpec((3, tn), lambda j, i: (0, j)), pl.BlockSpec((3, tn), lambda j, i: (0, nj + j))],
        out_specs=[tile, tile, gspec, gspec],
        out_shape=[SDS((s, f), BF16), SDS((s, f), BF16), SDS((8, f), F32), SDS((8, f), F32)],
        scratch_shapes=[pltpu.VMEM((2, 8, tn), F32)], name="ffn_bwd",
        compiler_params=_cp(("parallel", "arbitrary")))(dm, ua, ub, ca, cbo, cw, cw)


def adamw(w, g, m, v, *, name, tr=None):
    r = w.shape[0]
    rest = w.shape[1:]
    if tr is None:
        tr = r
        for cand in (512, 352, 256, 128, 64, 32, 16, 8):
            if r % cand == 0:
                tr = cand
                break

    def body(w_ref, g_ref, m_ref, v_ref, d_ref, nm_ref, nv_ref):
        gv = g_ref[...]
        mn = ADAM_B1 * m_ref[...] + (1.0 - ADAM_B1) * gv
        vn = ADAM_B2 * v_ref[...] + (1.0 - ADAM_B2) * (gv * gv)
        m_hat = mn / (1.0 - ADAM_B1 ** ADAM_STEP)
        v_hat = vn / (1.0 - ADAM_B2 ** ADAM_STEP)
        d_ref[...] = -ADAM_LR * (m_hat / (jnp.sqrt(v_hat) + ADAM_EPS) + ADAM_WD * w_ref[...])
        nm_ref[...] = mn
        nv_ref[...] = vn

    blk = pl.BlockSpec((tr,) + rest, lambda i: (i,) + (0,) * len(rest))
    return pl.pallas_call(body, grid=(r // tr,), in_specs=[blk] * 4, out_specs=[blk] * 3, out_shape=[SDS(w.shape, F32)] * 3,
                          name=name, compiler_params=_cp(("parallel",)))(w, g, m, v)


def adamw_rows_view(w, g_mine, g_full, m, v, c_arr, *, name):
    _, c, r = w.shape
    flat = lambda a: jnp.transpose(a, (2, 0, 1)).reshape(r, c // LANES, LANES)
    unflat = lambda a: jnp.transpose(a, (1, 2, 0)).reshape(1, c, r)

    def body(c_ref, w_ref, gm_ref, gf_ref, m_ref, v_ref, d_ref, nm_ref, nv_ref, go_ref):
        low = lax.broadcasted_iota(jnp.int32, (r, c), 1) < c // 2
        gm = gm_ref[...]
        g2 = jnp.where(low == (c_ref[0] == 0), jnp.concatenate([gm, gm], axis=1), gf_ref[...])
        gv = g2.reshape(r, c // LANES, LANES)
        mn = ADAM_B1 * m_ref[...] + (1.0 - ADAM_B1) * gv
        vn = ADAM_B2 * v_ref[...] + (1.0 - ADAM_B2) * (gv * gv)
        m_hat = mn / (1.0 - ADAM_B1 ** ADAM_STEP)
        v_hat = vn / (1.0 - ADAM_B2 ** ADAM_STEP)
        d_ref[...] = -ADAM_LR * (m_hat / (jnp.sqrt(v_hat) + ADAM_EPS) + ADAM_WD * w_ref[...])
        nm_ref[...] = mn
        nv_ref[...] = vn
        go_ref[...] = gv

    once = pl.Buffered(1)
    b3 = pl.BlockSpec((r, c // LANES, LANES), lambda i, c_ref: (0, 0, 0), pipeline_mode=once)
    own = pl.BlockSpec((r, c // 2), lambda i, c_ref: (0, 0), pipeline_mode=once)
    full = pl.BlockSpec((r, c), lambda i, c_ref: (0, 0), pipeline_mode=once)
    grid_spec = pltpu.PrefetchScalarGridSpec(num_scalar_prefetch=1, grid=(1,), in_specs=[b3, own, full, b3, b3],
                                             out_specs=[b3] * 4)
    outs = pl.pallas_call(body, grid_spec=grid_spec, out_shape=[SDS((r, c // LANES, LANES), F32)] * 4, name=name,
                          compiler_params=_cp(("arbitrary",)))(c_arr, flat(w), g_mine, g_full, flat(m), flat(v))
    return [unflat(a) for a in outs]


ANY = pl.BlockSpec(memory_space=pl.ANY)
ICI_KINDS = ("x", "y", "xy")


def _coords():
    return lax.axis_index("x"), lax.axis_index("y"), lax.axis_index("c")


def _peer(kind, x, y, c):
    if kind == "c":
        return (x, y, 1 - c)
    if kind == "x":
        return (1 - x, y, c)
    if kind == "y":
        return (x, 1 - y, c)
    return (1 - x, 1 - y, c)


def _chip_of(p):
    return 2 * p[0] + p[1]


def _half(rows, which):
    h = rows // 2
    return pl.ds(pl.multiple_of(which * h, 16), h)


def _remote(src, dst, send_sem, recv_sem, to):
    return pltpu.make_async_remote_copy(src_ref=src, dst_ref=dst, send_sem=send_sem, recv_sem=recv_sem,
                                        device_id=to, device_id_type=MESH)


def allgather_balanced(shard, *, name):
    r, cols = shard.shape
    h, q = r // 2, r // 4

    def body(in_ref, out_ref, send_sems, recv_sems):
        x, y, c = _coords()
        me, sibling = (x, y, c), (x, y, 1 - c)
        nbr = ((1 - x, y, c), (x, 1 - y, c))
        chip = (2 * (1 - x) + y, 2 * x + (1 - y), 2 * (1 - x) + (1 - y))
        quarter = lambda core, i: pl.ds(pl.multiple_of(core * h + i * q, 16), q)
        sent = []

        def go(src, dst, slot, to):
            cp = _remote(src, dst, send_sems.at[slot], recv_sems.at[slot], to)
            cp.start()
            sent.append(cp)

        def landed(region, slot):
            _remote(region, region, send_sems.at[slot], recv_sems.at[slot], me).wait_recv()

        for i in range(2):
            for k in range(2):
                qi = k if i == 0 else 1 - k
                go(in_ref.at[quarter(c, qi)], out_ref.at[2 * x + y, quarter(c, qi)], 2 * k + qi, nbr[k])
        for k in range(2):
            piece = out_ref.at[chip[k], quarter(c, k)]
            landed(piece, 2 * k + k)
            go(piece, piece, 4 + k, nbr[1 - k])
            go(piece, piece, 6 + 2 * k + k, sibling)
        for k in range(2):
            piece = out_ref.at[chip[k], quarter(c, 1 - k)]
            landed(piece, 2 * k + 1 - k)
            go(piece, piece, 6 + 2 * k + 1 - k, sibling)
        for k in range(2):
            piece = out_ref.at[chip[2], quarter(c, k)]
            landed(piece, 4 + k)
            go(piece, piece, 10 + k, sibling)
        for k in range(2):
            for i in range(2):
                landed(out_ref.at[chip[k], quarter(1 - c, i)], 6 + 2 * k + i)
            landed(out_ref.at[chip[2], quarter(1 - c, k)], 10 + k)
        for cp in sent:
            cp.wait_send()

    return pl.pallas_call(
        body, in_specs=[ANY], out_specs=ANY, out_shape=SDS((4,) + shard.shape, shard.dtype),
        scratch_shapes=[pltpu.SemaphoreType.DMA((12,)), pltpu.SemaphoreType.DMA((12,))], name=name)(shard)


def _allgather_shapes(shards):
    return [SDS((4,) + a.shape, a.dtype) for a in shards]


def _allgather_sems(n):
    return [pltpu.SemaphoreType.DMA((n, 6)), pltpu.SemaphoreType.DMA((n, 6))]


def _allgather_rows(ref, is_halved, which):
    r = ref.shape[0]
    return _half(r, which) if is_halved else pl.ds(0, r)


def _allgather_first(ins, outs, send_sems, recv_sems, halved):
    x, y, c = _coords()
    my_chip = 2 * x + y
    cps = []
    for w in range(len(ins)):
        rows = _allgather_rows(ins[w], halved[w], c)
        for k, kind in enumerate(ICI_KINDS):
            cps.append(_remote(ins[w].at[rows], outs[w].at[my_chip, rows], send_sems.at[w, k], recv_sems.at[w, k],
                               _peer(kind, x, y, c)))
    return cps


def _allgather_start(ins, outs, send_sems, recv_sems, halved):
    for cp in _allgather_first(ins, outs, send_sems, recv_sems, halved):
        cp.start()


def _allgather_finish(ins, outs, send_sems, recv_sems, halved):
    x, y, c = _coords()
    me = (x, y, c)
    second = []
    for w in range(len(ins)):
        for k, kind in enumerate(ICI_KINDS):
            landed = outs[w].at[_chip_of(_peer(kind, x, y, c)), _allgather_rows(ins[w], halved[w], c)]
            _remote(landed, landed, send_sems.at[w, k], recv_sems.at[w, k], me).wait_recv()
            if halved[w]:
                cp = _remote(landed, landed, send_sems.at[w, 3 + k], recv_sems.at[w, 3 + k], _peer("c", x, y, c))
                cp.start()
                second.append(cp)
    for w in range(len(ins)):
        if halved[w]:
            for k, kind in enumerate(ICI_KINDS):
                other = outs[w].at[_chip_of(_peer(kind, x, y, c)), _allgather_rows(ins[w], True, 1 - c)]
                _remote(other, other, send_sems.at[w, 3 + k], recv_sems.at[w, 3 + k], me).wait_recv()
    for cp in _allgather_first(ins, outs, send_sems, recv_sems, halved) + second:
        cp.wait_send()


def _half_of(ref, by_cols, which):
    lead = (slice(None),) * (len(ref.shape) - 2)
    if by_cols:
        h = ref.shape[-1] // 2
        return ref.at[lead + (slice(None), pl.ds(pl.multiple_of(which * h, LANES), h))]
    return ref.at[lead + (_half(ref.shape[-2], which),)]


def _half_shape(shape, by_cols):
    return shape[:-1] + (shape[-1] // 2,) if by_cols else shape[:-2] + (shape[-2] // 2, shape[-1])


def grads_to_sibling(gs, by_cols, *, name):
    n = len(gs)

    def body(*refs):
        ins, outs = refs[:n], refs[n:2 * n]
        send_sems, recv_sems = refs[2 * n:]
        x, y, c = _coords()
        cps = []
        for w in range(n):
            cp = _remote(_half_of(ins[w], by_cols[w], 1 - c), outs[w], send_sems.at[w], recv_sems.at[w], _peer("c", x, y, c))
            cp.start()
            cps.append(cp)
        for cp in cps:
            cp.wait()

    return pl.pallas_call(
        body, in_specs=[ANY] * n, out_specs=[ANY] * n,
        out_shape=[SDS(_half_shape(a.shape, bc), a.dtype) for a, bc in zip(gs, by_cols)],
        scratch_shapes=[pltpu.SemaphoreType.DMA((n,)), pltpu.SemaphoreType.DMA((n,))], name=name)(*gs)


def _to_chips_shapes(ps):
    return [SDS((3,) + a.shape[1:], a.dtype) for a in ps]


def _to_chips_sems(n):
    return [pltpu.SemaphoreType.DMA((n, 3)), pltpu.SemaphoreType.DMA((n, 3))]


def _to_chips_copies(ins, outs, send_sems, recv_sems):
    x, y, c = _coords()
    cps = []
    for w in range(len(ins)):
        for k, kind in enumerate(ICI_KINDS):
            to = _peer(kind, x, y, c)
            cps.append(_remote(ins[w].at[_chip_of(to)], outs[w].at[k], send_sems.at[w, k], recv_sems.at[w, k], to))
    return cps


def _to_chips_start(ins, outs, send_sems, recv_sems):
    for cp in _to_chips_copies(ins, outs, send_sems, recv_sems):
        cp.start()


def _to_chips_finish(ins, outs, send_sems, recv_sems):
    for cp in _to_chips_copies(ins, outs, send_sems, recv_sems):
        cp.wait()


def _to_owners_shapes(ps):
    return [SDS((7, a.shape[1] // 2, a.shape[2]), a.dtype) for a in ps]


def _to_owners_sems(n):
    return [pltpu.SemaphoreType.DMA((n, 7)), pltpu.SemaphoreType.DMA((n, 7))]


def _to_owners_copies(ins, outs, send_sems, recv_sems):
    x, y, c = _coords()
    cps = []
    for w in range(len(ins)):
        rows = ins[w].shape[1]
        for k, kind in enumerate(ICI_KINDS):
            px, py, _ = _peer(kind, x, y, c)
            for h in range(2):
                cps.append(_remote(ins[w].at[2 * px + py, _half(rows, h)], outs[w].at[2 * k + c],
                                   send_sems.at[w, 2 * k + h], recv_sems.at[w, 2 * k + c], (px, py, h)))
        cps.append(_remote(ins[w].at[2 * x + y, _half(rows, 1 - c)], outs[w].at[6], send_sems.at[w, 6], recv_sems.at[w, 6],
                           _peer("c", x, y, c)))
    return cps


def _to_owners_start(ins, outs, send_sems, recv_sems):
    for cp in _to_owners_copies(ins, outs, send_sems, recv_sems):
        cp.start()


def _to_owners_finish(ins, outs, send_sems, recv_sems):
    for cp in _to_owners_copies(ins, outs, send_sems, recv_sems):
        cp.wait_send()
    for w in range(len(ins)):
        for slot in range(7):
            got = outs[w].at[slot]
            _remote(got, got, send_sems.at[w, slot], recv_sems.at[w, slot], _coords()).wait_recv()


EXCHANGES = {"to_chips": (_to_chips_shapes, _to_chips_sems, _to_chips_start, _to_chips_finish),
             "to_owners": (_to_owners_shapes, _to_owners_sems, _to_owners_start, _to_owners_finish)}


def halves_to_full(hs, by_cols, *, name):
    n = len(hs)

    def body(*refs):
        ins, outs = refs[:n], refs[n:2 * n]
        send_sems, recv_sems = refs[2 * n:]
        x, y, c = _coords()
        cps = []
        for w in range(n):
            cp = _remote(ins[w], _half_of(outs[w], by_cols[w], c), send_sems.at[w], recv_sems.at[w], _peer("c", x, y, c))
            cp.start()
            cps.append(cp)
        for cp in cps:
            cp.wait()

    return pl.pallas_call(
        body, in_specs=[ANY] * n, out_specs=[ANY] * n,
        out_shape=[SDS((a.shape[0], 2 * a.shape[1]) if bc else (2 * a.shape[0], a.shape[1]), a.dtype)
                   for a, bc in zip(hs, by_cols)],
        scratch_shapes=[pltpu.SemaphoreType.DMA((n,)), pltpu.SemaphoreType.DMA((n,))],
        name=name)(*hs)


def _row_tile(rows):
    for cand in (256, 192, 176, 128, 64, 32, 16):
        if rows % cand == 0:
            return cand
    return rows


def chip_sum(g, recv, c_arr, by_cols, *, name):
    _, r, cols = g.shape

    def body(c_ref, g_ref, r_ref, f_ref, b_ref):
        tot = g_ref[...] + r_ref[...]
        f_ref[...] = tot
        b_ref[...] = tot.astype(BF16)

    if by_cols:
        tc = 4 * LANES
        nblk = cols // 2 // tc
        shape = (4, r, cols // 2)
        blk = pl.BlockSpec((None, r, tc), lambda j, i, c_ref: (j, 0, i))
        mine = pl.BlockSpec((None, r, tc), lambda j, i, c_ref: (j, 0, c_ref[0] * nblk + i))
    else:
        tr = _row_tile(r // 2)
        nblk = r // 2 // tr
        shape = (4, r // 2, cols)
        blk = pl.BlockSpec((None, tr, cols), lambda j, i, c_ref: (j, i, 0))
        mine = pl.BlockSpec((None, tr, cols), lambda j, i, c_ref: (j, c_ref[0] * nblk + i, 0))
    grid_spec = pltpu.PrefetchScalarGridSpec(num_scalar_prefetch=1, grid=(4, nblk), in_specs=[mine, blk], out_specs=[blk, blk])
    return pl.pallas_call(body, grid_spec=grid_spec, out_shape=[SDS(shape, F32), SDS(shape, BF16)],
                          name=name, compiler_params=_cp(("parallel", "parallel")))(c_arr, g, recv)


def final_sum(pf, recv, chip_arr, *, name):
    _, h, cols = pf.shape
    tr = _row_tile(h)

    def body(chip_ref, p_ref, r_ref, o_ref):
        o_ref[...] = ((p_ref[...] + r_ref[0].astype(F32)) + r_ref[1].astype(F32)) + r_ref[2].astype(F32)

    grid_spec = pltpu.PrefetchScalarGridSpec(
        num_scalar_prefetch=1, grid=(h // tr,),
        in_specs=[pl.BlockSpec((None, tr, cols), lambda i, chip_ref: (chip_ref[0], i, 0)),
                  pl.BlockSpec((3, tr, cols), lambda i, chip_ref: (0, i, 0))],
        out_specs=pl.BlockSpec((tr, cols), lambda i, chip_ref: (i, 0)))
    return pl.pallas_call(body, grid_spec=grid_spec, out_shape=SDS((h, cols), F32), name=name,
                          compiler_params=_cp(("parallel",)))(chip_arr, pf, recv)


def owner_sum(g, recv, pos_arr, *, name):
    _, r, cols = g.shape
    h = r // 2
    tr = _row_tile(h)
    nblk = h // tr

    def body(pos_ref, g_ref, r_ref, o_ref):
        tot = g_ref[...]
        for slot in range(7):
            tot = tot + r_ref[slot].astype(F32)
        o_ref[...] = tot

    grid_spec = pltpu.PrefetchScalarGridSpec(
        num_scalar_prefetch=1, grid=(nblk,),
        in_specs=[pl.BlockSpec((None, tr, cols), lambda i, pos: (pos[0], pos[1] * nblk + i, 0)),
                  pl.BlockSpec((7, tr, cols), lambda i, pos: (0, i, 0))],
        out_specs=pl.BlockSpec((tr, cols), lambda i, pos: (i, 0)))
    return pl.pallas_call(body, grid_spec=grid_spec, out_shape=SDS((h, cols), F32), name=name,
                          compiler_params=_cp(("parallel",)))(pos_arr, g, recv)


def allreduce_small(v, *, name):
    rws, cols = v.shape

    def body(v_ref, all_ref, sum_ref, send_sems, recv_sems, local_sem):
        x, y, c = _coords()
        me, sibling = (x, y, c), (x, y, 1 - c)
        chips = [(1 - x, y), (x, 1 - y), (1 - x, 1 - y)]

        def rows(px, py, pc):
            return all_ref.at[pl.ds(pl.multiple_of((4 * px + 2 * py + pc) * rws, 8), rws), :]

        def copy(k, block, to, src=None):
            return _remote(rows(*block) if src is None else src, rows(*block), send_sems.at[k], recv_sems.at[k], to)

        mine = pltpu.make_async_copy(v_ref, rows(*me), local_sem)
        mine.start()
        first = [copy(0, me, sibling, src=v_ref)]
        first += [copy(1 + j, me, (*chip, c), src=v_ref) for j, chip in enumerate(chips)]
        for cp in first:
            cp.start()
        passed = [copy(4 + j, (*chip, c), sibling) for j, chip in enumerate(chips)]
        for j, chip in enumerate(chips):
            copy(1 + j, (*chip, c), me).wait_recv()
            passed[j].start()
        copy(0, sibling, me).wait_recv()
        for j, chip in enumerate(chips):
            copy(4 + j, (*chip, 1 - c), me).wait_recv()
        for cp in first + passed:
            cp.wait_send()
        mine.wait()
        tot = all_ref[0:rws, :]
        for dev in range(1, 8):
            tot = tot + all_ref[dev * rws:(dev + 1) * rws, :]
        sum_ref[...] = tot

    vm = pl.BlockSpec(memory_space=pltpu.VMEM)
    return pl.pallas_call(
        body, in_specs=[vm], out_specs=[vm, vm],
        out_shape=[SDS((8 * rws, cols), v.dtype), SDS((rws, cols), v.dtype)],
        scratch_shapes=[pltpu.SemaphoreType.DMA((7,)), pltpu.SemaphoreType.DMA((7,)), pltpu.SemaphoreType.DMA],
        name=name)(v)[1]


def _pack_rows(parts, rows):
    out = []
    for a, r in zip(parts, rows):
        flat = a.reshape(-1)
        flat = jnp.pad(flat, (0, r * LANES - flat.shape[0]))
        out.append(flat.reshape(r, LANES))
    return jnp.concatenate(out, axis=0)


def _unpack_rows(packed, shapes, rows):
    out, at = [], 0
    for shp, r in zip(shapes, rows):
        size = int(np.prod(shp))
        out.append(packed[at:at + r].reshape(-1)[:size].reshape(shp))
        at += r
    return out


def kernel(x, g_pre_mix, w_in, b_forget, w_o_fox, w_o_dil, w_out, g_post_mix, g_pre_ffn, w_up, conv_w, conv_b, w_down, g_post_ffn, loss_target, m_g_pre_mix, m_w_in, m_b_forget, m_w_o_fox, m_w_o_dil, m_w_out, m_g_post_mix, m_g_pre_ffn, m_w_up, m_conv_w, m_conv_b, m_w_down, m_g_post_ffn, v_g_pre_mix, v_w_in, v_b_forget, v_w_o_fox, v_w_o_dil, v_w_out, v_g_post_mix, v_g_pre_ffn, v_w_up, v_conv_w, v_conv_b, v_w_down, v_g_post_ffn):
    xi, yi, ci = _coords()
    chip = 2 * xi + yi
    c_arr = jnp.reshape(ci, (1,)).astype(jnp.int32)
    chip_arr = jnp.reshape(chip, (1,)).astype(jnp.int32)
    xs = x[0]
    target = loss_target[0]
    s, d = xs.shape
    f_half = w_down.shape[1] * 4
    cols_in = w_in.shape[2]

    big = (w_in, w_o_fox, w_o_dil, w_out, w_up, w_down)
    shards = [w[0].astype(BF16) for w in big]
    a_in = allgather_balanced(shards[0], name="allgather_w_in")
    w_in_full = jnp.concatenate([jnp.where(chip == j, shards[0], a_in[j]) for j in range(4)], axis=1)
    nf = N_HEADS
    e_a, e_b = 3 * ATT_W, 3 * ATT_W + nf
    wz = jnp.concatenate([w_in_full[:, :e_a], w_in_full[:, e_b:]], axis=1)
    wf = jnp.pad(w_in_full[:, e_a:e_b], ((0, 0), (0, LANES - nf)))
    cb = conv_b
    bfo = jnp.pad(b_forget, ((0, 0), (0, LANES - nf)))

    h1 = rmsnorm_fwd(xs, g_pre_mix)
    z = mm([(h1, d, 0)], [(wz, d, 0)], nt=False, out_dtype=BF16, tm=s, tn=1024, name="in_proj")
    fa = mm([(h1, d, 0)], [(wf, d, 0)], nt=False, out_dtype=F32, tm=s, tn=LANES, name="in_proj_forget")
    q_aug, k_aug, v_aug = fox_prep(z, fa, bfo)
    later = shards[1:] + [conv_w[0]]
    ya, lse_a, *late = fox_fwd(q_aug, k_aug, v_aug, gather=later, halved=[True] * 5 + [False], hps=4)
    a_of, a_od, a_out, a_up, a_down, a_cw = [
        lax.dynamic_update_index_in_dim(a4, own, chip, 0) for a4, own in zip(late, later)]
    cw = jnp.concatenate([a_cw[j] for j in range(4)], axis=1)
    wo_a = jnp.concatenate([a_of[j] for j in range(4)], axis=1)
    wo_b = jnp.concatenate([a_od[j] for j in range(4)], axis=1)
    w_o = a_out.reshape(d, d)
    w_dn = a_down.reshape(f_half, d)
    wu_a = jnp.concatenate([a_up[0], a_up[1]], axis=1)
    wu_b = jnp.concatenate([a_up[2], a_up[3]], axis=1)
    cos_t, sin_t = rope_cos_sin(s)
    yb, lse_b = dil_fwd_all(z, cos_t, sin_t)
    pa, pb, mixed = gate_mix(ya, yb, wo_a, wo_b, z)
    y1, x1, h2 = proj_norm_res(mixed, w_o, g_post_mix, xs, g_pre_ffn, tm=1024, name="out_proj")
    ua, ub, conv_a, conv_bh, mid = ffn_up(h2, wu_a, wu_b, cw, cb)
    dout, dy2, gg_post_ffn, sq = proj_norm_loss(mid, w_dn, g_post_ffn, x1, target, name="down_proj")

    dmid = mm([(dy2, d, 0)], [(w_dn, d, 0)], nt=True, out_dtype=BF16, tm=2048, tn=f_half // 2, name="down_dgrad")
    dw_down, dw_down16 = wgrad((mid, f_half, 0), dy2, tk=f_half // 2, tn=1024, ts=2048, name="down_wgrad", bf16_copy=True)
    dua, dub, gc_a, gc_b = ffn_bwd(dmid, ua, ub, conv_a, conv_bh, cw)
    dx1, dy1, gg_pre_ffn, gg_post_mix = mm_norm_bwd(
        [(dua, f_half, 0), (dub, f_half, 0)], [(wu_a, f_half, 0), (wu_b, f_half, 0)],
        [(x1, g_pre_ffn, dout, F32), (y1, g_post_mix, None, BF16)], name="up_dgrad")
    dw_up = None
    for k, du in enumerate((dua, dub)):
        dw_up = wgrad((h2, d, 0), du, tk=1024, tn=f_half // 2, ts=2048, name=f"up_wgrad_{k}", chip_major=True,
                      slabs=(4, 2 * k), into=dw_up, bf16_copy=True)
    g_ffn = [(dw_up[0], dw_up[1]), (dw_down.reshape(4, f_half // 4, d), dw_down16.reshape(4, f_half // 4, d))]
    dw_out, dw_out16 = wgrad((mixed, d, 0), dy1, tk=1024, tn=1024, ts=2048, name="out_wgrad", bf16_copy=True)
    dpa, dpb, dz_g, dya, dyb, dd_a = mix_bwd(dy1, w_o, z, pa, pb, wo_a, wo_b, ya)
    by_chip_cols = lambda a: jnp.stack([a[:, j * (d // 4):(j + 1) * (d // 4)] for j in range(4)], axis=0)
    dw_of = [by_chip_cols(a) for a in wgrad((ya, ATT_W, 0), dpa, tk=ATT_W, tn=d, ts=1024, name="fox_o_wgrad", bf16_copy=True)]
    dw_od = [by_chip_cols(a) for a in wgrad((yb, ATT_W, 0), dpb, tk=ATT_W, tn=d, ts=1024, name="dil_o_wgrad", bf16_copy=True)]
    g_mix = [dw_of, dw_od, (dw_out.reshape(4, d // 4, d), dw_out16.reshape(4, d // 4, d))]
    dq_aug, dk_aug, dv_a, *got_ffn = fox_bwd(q_aug, k_aug, z, dya, lse_a, dd_a, exchange=[g[1] for g in g_ffn], kind="to_owners")
    dz_a, dfa, gg_bf = fox_post(dq_aug, dk_aug, dv_a, fa, bfo)
    *dz_b, got_of, got_od, got_out = dil_bwd_all(z, cos_t, sin_t, dyb, lse_b, yb, exchange=[g[1] for g in g_mix],
                                                 kind="to_owners")
    got_mix = [got_of, got_od, got_out]
    dwt_a = wgrad((dz_a, e_a, 0), h1, tk=e_a // 2, tn=d, ts=2048, name="in_wgrad_a")
    dwt_b = [wgrad((part, ATT_W, 0), h1, tk=ATT_W, tn=d, ts=2048, name=f"in_wgrad_b{k}") for k, part in enumerate(dz_b)]
    dwt_g = wgrad((dz_g, 2 * d, 0), h1, tk=d, tn=d, ts=2048, name="in_wgrad_g")
    dwt_f = wgrad((dfa, LANES, 0), h1, tk=LANES, tn=d, ts=2048, name="in_wgrad_f")
    dwt_full = jnp.concatenate([dwt_a, dwt_f[:nf], *dwt_b, dwt_g], axis=0)
    dw_in = jnp.stack([dwt_full[j * cols_in:(j + 1) * cols_in] for j in range(4)], axis=0)
    from_sib = grads_to_sibling([dw_in], [True], name="grads_to_sibling_in")
    sum_in = chip_sum(dw_in, from_sib[0], c_arr, True, name="chip_sum_w_in")
    grad_x, gg_pre_mix, got_in = mm_norm_bwd(
        [(dz_a, e_a, 0), *[(part, ATT_W, 0) for part in dz_b], (dz_g, d, 0), (dz_g, d, 1), (dfa, LANES, 0)],
        [(wz, e_a, 0), *[(wz, ATT_W, Z_QB + k) for k in range(3)], (wz, d, 3), (wz, d, 4), (wf, LANES, 0)],
        [(xs, g_pre_mix, dx1, F32)], exchange=[sum_in[1]], name="in_dgrad")

    names = ("w_in", "w_o_fox", "w_o_dil", "w_out", "w_up", "w_down")
    pos_arr = jnp.concatenate([chip_arr, c_arr])
    halves = [final_sum(sum_in[0], got_in, chip_arr, name="final_sum_w_in")] + [
        owner_sum(g[0], got, pos_arr, name=f"owner_sum_{nm}") for g, got, nm in zip(g_mix + g_ffn, got_mix + got_ffn, names[1:])]
    from_half = halves_to_full(halves, [True] + [False] * 5, name="halves_to_full")
    g_big = [None] + [lax.dynamic_update_slice_in_dim(full, mine, ci * mine.shape[0], axis=0)
                      for full, mine in zip(from_half[1:], halves[1:])]
    upd_big = [adamw(w[0], g, m[0], v[0], name=f"adamw_{nm}") for w, g, m, v, nm in list(zip(
        big, g_big, (m_w_in, m_w_o_fox, m_w_o_dil, m_w_out, m_w_up, m_w_down),
        (v_w_in, v_w_o_fox, v_w_o_dil, v_w_out, v_w_up, v_w_down), names))[1:]]
    *upd_in, g_in = adamw_rows_view(w_in, halves[0], from_half[0], m_w_in, v_w_in, c_arr, name="adamw_w_in")

    g_cw_loc = jnp.concatenate([gc_a[0:3], gc_b[0:3]], axis=1)
    g_cb_loc = jnp.concatenate([gc_a[3:4], gc_b[3:4]], axis=1)
    small_loc = [gg_pre_mix, gg_post_mix, gg_pre_ffn, gg_post_ffn, g_cb_loc, gg_bf[:, :nf], g_cw_loc, sq * (0.5 / d)]
    red_rows = (8, 8, 8, 8, 48, 8, 136, 8)
    red = allreduce_small(_pack_rows(small_loc, red_rows), name="allreduce_small")
    g_pm, g_qm, g_pf, g_qf, g_cb, g_bf, g_cw_full, loss_11 = _unpack_rows(red, [a.shape for a in small_loc], red_rows)
    loss = loss_11[0, 0]
    cols_cw = conv_w.shape[2]
    g_cw = lax.dynamic_slice_in_dim(g_cw_full, chip * cols_cw, cols_cw, axis=1)
    small_w = (g_pre_mix, g_post_mix, g_pre_ffn, g_post_ffn, conv_b, b_forget, conv_w[0])
    small_m = (m_g_pre_mix, m_g_post_mix, m_g_pre_ffn, m_g_post_ffn, m_conv_b, m_b_forget, m_conv_w[0])
    small_v = (v_g_pre_mix, v_g_post_mix, v_g_pre_ffn, v_g_post_ffn, v_conv_b, v_b_forget, v_conv_w[0])
    small_g = (g_pm, g_qm, g_pf, g_qf, g_cb, g_bf, g_cw)
    small_names = ("g_pre_mix", "g_post_mix", "g_pre_ffn", "g_post_ffn", "conv_b", "b_forget", "conv_w")
    per_param = [adamw(w, g, m, v, name=f"adamw_{nm}") for w, g, m, v, nm in zip(small_w, small_g, small_m, small_v, small_names)]
    upd_small = [[u[j] for u in per_param] for j in range(3)]

    order = ("g_pre_mix", "w_in", "b_forget", "w_o_fox", "w_o_dil", "w_out", "g_post_mix", "g_pre_ffn", "w_up", "conv_w",
             "conv_b", "w_down", "g_post_ffn")
    grads, deltas, new_ms, new_vs = {}, {}, {}, {}
    grads["w_in"] = g_in
    deltas["w_in"], new_ms["w_in"], new_vs["w_in"] = upd_in
    for k, nm in enumerate(names[1:]):
        grads[nm] = g_big[k + 1][None]
        deltas[nm], new_ms[nm], new_vs[nm] = (a[None] for a in upd_big[k])
    for k, nm in enumerate(small_names):
        lead = (lambda a: a[None]) if nm == "conv_w" else (lambda a: a)
        grads[nm] = lead(small_g[k])
        deltas[nm], new_ms[nm], new_vs[nm] = (lead(upd_small[j][k]) for j in range(3))
    return (loss, grad_x[None], *[grads[nm] for nm in order], *[deltas[nm] for nm in order],
            *[new_ms[nm] for nm in order], *[new_vs[nm] for nm in order])
```

```python
import functools
import math

import numpy as np
import jax
import jax.numpy as jnp
from jax import lax
from jax.experimental import pallas as pl
from jax.experimental.pallas import tpu as pltpu

F32 = jnp.float32
BF16 = jnp.bfloat16
SDS = jax.ShapeDtypeStruct
MESH = pl.DeviceIdType.MESH

HEAD_DIM = 64
N_HEADS = 8
LANES = 128
ATT_W = N_HEADS * HEAD_DIM
DIL_PATTERNS = ((128, 1), (512, 4), (2048, 16))
DIL_BLK = 128
ROPE_DIM = HEAD_DIM // 4
ROPE_THETA = 500000.0
RMS_EPS = 1e-6
NEG = -1e30
QK_SCALE = 1.0 / math.sqrt(HEAD_DIM)
ADAM_LR, ADAM_B1, ADAM_B2, ADAM_EPS, ADAM_WD, ADAM_STEP = 0.001, 0.9, 0.999, 1e-08, 0.01, 10
VMEM_LIMIT = 56 * 1024 * 1024

Z_QA, Z_KA, Z_VA, Z_QB, Z_KB, Z_VB = 0, 1, 2, 3, 4, 5
Z_W = 5120


def _cp(sem):
    return pltpu.CompilerParams(dimension_semantics=sem, vmem_limit_bytes=VMEM_LIMIT)


def _nt(a, b):
    return lax.dot_general(a, b, (((1,), (1,)), ((), ())), preferred_element_type=F32)


def _tn(a, b):
    return lax.dot_general(a, b, (((0,), (0,)), ((), ())), preferred_element_type=F32)


def _nn(a, b):
    return jnp.dot(a, b, preferred_element_type=F32)


def _lane(shape):
    return lax.broadcasted_iota(jnp.int32, shape, 1)


def _row(shape):
    return lax.broadcasted_iota(jnp.int32, shape, 0)


def rmsnorm_fwd(x, g, *, tm=1024):
    s, d = x.shape

    def body(x_ref, g_ref, h_ref):
        xv = x_ref[...]
        inv = lax.rsqrt(jnp.mean(xv * xv, axis=-1, keepdims=True) + RMS_EPS)
        h_ref[...] = (xv * inv * g_ref[...]).astype(h_ref.dtype)

    return pl.pallas_call(
        body, grid=(s // tm,),
        in_specs=[pl.BlockSpec((tm, d), lambda i: (i, 0)), pl.BlockSpec((1, d), lambda i: (0, 0))],
        out_specs=pl.BlockSpec((tm, d), lambda i: (i, 0)),
        out_shape=SDS((s, d), BF16), name="rmsnorm_fwd", compiler_params=_cp(("parallel",)))(x, g)


def mm(a_views, b_views, *, nt, out_dtype, tm, tn, name):
    n_p = len(a_views)
    m = a_views[0][0].shape[0]
    n = b_views[0][0].shape[0] if nt else b_views[0][0].shape[1]

    def body(*refs):
        o_ref = refs[-1]
        acc = None
        for p in range(n_p):
            av = refs[p][...].astype(BF16)
            bv = refs[n_p + p][...].astype(BF16)
            dv = _nt(av, bv) if nt else _nn(av, bv)
            acc = dv if acc is None else acc + dv
        o_ref[...] = acc.astype(o_ref.dtype)

    in_specs = []
    for arr, w, blk in a_views:
        in_specs.append(pl.BlockSpec((tm, w), functools.partial(lambda i, j, blk: (i, blk), blk=blk)))
    for arr, w, blk in b_views:
        if nt:
            in_specs.append(pl.BlockSpec((tn, w), functools.partial(lambda i, j, blk: (j, blk), blk=blk)))
        else:
            in_specs.append(pl.BlockSpec((w, tn), lambda i, j: (0, j)))
    return pl.pallas_call(
        body, grid=(m // tm, n // tn), in_specs=in_specs,
        out_specs=pl.BlockSpec((tm, tn), lambda i, j: (i, j)),
        out_shape=SDS((m, n), out_dtype), name=name,
        compiler_params=_cp(("parallel", "parallel")))(*[a[0] for a in a_views], *[b[0] for b in b_views])


def wgrad(a_view, g, *, tk, tn, ts, name, chip_major=False, slabs=None, into=None, bf16_copy=False):
    arr, ka, blk = a_view
    s, n = g.shape
    ns = s // ts
    total, first = slabs if slabs else (n // tn, 0)
    n_into = 0 if into is None else (2 if bf16_copy else 1)

    def body(a_ref, g_ref, *rest):
        o_ref = rest[n_into]

        @pl.when(pl.program_id(2) == 0)
        def _():
            o_ref[...] = jnp.zeros_like(o_ref)

        o_ref[...] += _tn(a_ref[...].astype(BF16), g_ref[...].astype(BF16))
        if bf16_copy:
            @pl.when(pl.program_id(2) == ns - 1)
            def _():
                rest[n_into + 1][...] = o_ref[...].astype(BF16)

    if chip_major:
        out_spec = pl.BlockSpec((None, tk, tn), lambda i, j, k: (first + j, i, 0))
        shape = (total, ka, tn)
    else:
        out_spec = pl.BlockSpec((tk, tn), lambda i, j, k: (i, j))
        shape = (ka, n)
    in_specs = [pl.BlockSpec((ts, tk), lambda i, j, k: (k, blk * (ka // tk) + i)),
                pl.BlockSpec((ts, tn), lambda i, j, k: (k, j))]
    args = [arr, g]
    if into is not None:
        earlier = list(into) if bf16_copy else [into]
        in_specs += [pl.BlockSpec(memory_space=pl.ANY)] * len(earlier)
        args += earlier
    out = pl.pallas_call(
        body, grid=(ka // tk, n // tn, ns), in_specs=in_specs,
        out_specs=[out_spec, out_spec] if bf16_copy else out_spec,
        out_shape=[SDS(shape, F32), SDS(shape, BF16)] if bf16_copy else SDS(shape, F32), name=name,
        input_output_aliases={2 + k: k for k in range(n_into)},
        compiler_params=_cp(("parallel", "parallel", "arbitrary")))(*args)
    return out


def _norm_bwd_rows(dh, xh, inv, g):
    dxh = dh * g
    dx = inv * (dxh - xh * jnp.mean(dxh * xh, axis=-1, keepdims=True))
    return dx, jnp.sum((dh * xh).reshape(dh.shape[0] // 8, 8, dh.shape[1]), axis=0)


def proj_norm_res(a, w, g, xres, g_next, *, tm=512, name):
    s, k = a.shape
    d = w.shape[1]

    def body(a_ref, w_ref, g_ref, x_ref, gn_ref, y_ref, o_ref, h_ref):
        y = _nn(a_ref[...], w_ref[...])
        inv = lax.rsqrt(jnp.mean(y * y, axis=-1, keepdims=True) + RMS_EPS)
        xn = x_ref[...] + y * inv * g_ref[...]
        y_ref[...] = y
        o_ref[...] = xn
        inv_n = lax.rsqrt(jnp.mean(xn * xn, axis=-1, keepdims=True) + RMS_EPS)
        h_ref[...] = (xn * inv_n * gn_ref[...]).astype(h_ref.dtype)

    row = pl.BlockSpec((tm, d), lambda i: (i, 0))
    vec = pl.BlockSpec((1, d), lambda i: (0, 0))
    return pl.pallas_call(
        body, grid=(s // tm,),
        in_specs=[pl.BlockSpec((tm, k), lambda i: (i, 0)), pl.BlockSpec((k, d), lambda i: (0, 0)), vec, row, vec],
        out_specs=[row, row, row], out_shape=[SDS((s, d), F32), SDS((s, d), F32), SDS((s, d), BF16)], name=name,
        compiler_params=_cp(("parallel",)))(a, w, g, xres, g_next)


def proj_norm_loss(a, w, g, xres, target, *, tm=512, name):
    s, k = a.shape
    d = w.shape[1]
    n = s // tm

    def body(a_ref, w_ref, g_ref, x_ref, t_ref, do_ref, dy_ref, dg_ref, l_ref, acc):
        i = pl.program_id(0)

        @pl.when(i == 0)
        def _():
            acc[...] = jnp.zeros_like(acc)
            l_ref[...] = jnp.zeros_like(l_ref)

        y = _nn(a_ref[...], w_ref[...])
        inv = lax.rsqrt(jnp.mean(y * y, axis=-1, keepdims=True) + RMS_EPS)
        yh = y * inv
        err = x_ref[...] + yh * g_ref[...] - t_ref[...]
        dout = err * (1.0 / d)
        do_ref[...] = dout
        l_ref[...] += jnp.sum(jnp.sum(err * err, axis=1, keepdims=True), axis=0, keepdims=True)
        dy, part = _norm_bwd_rows(dout, yh, inv, g_ref[...])
        dy_ref[...] = dy.astype(dy_ref.dtype)
        acc[...] += part

        @pl.when(i == n - 1)
        def _():
            dg_ref[...] = jnp.sum(acc[...], axis=0, keepdims=True)

    row = pl.BlockSpec((tm, d), lambda i: (i, 0))
    vec = pl.BlockSpec((1, d), lambda i: (0, 0))
    return pl.pallas_call(
        body, grid=(n,),
        in_specs=[pl.BlockSpec((tm, k), lambda i: (i, 0)), pl.BlockSpec((k, d), lambda i: (0, 0)), vec, row, row],
        out_specs=[row, row, vec, pl.BlockSpec((1, 1), lambda i: (0, 0))],
        out_shape=[SDS((s, d), F32), SDS((s, d), BF16), SDS((1, d), F32), SDS((1, 1), F32)],
        scratch_shapes=[pltpu.VMEM((8, d), F32)], name=name, compiler_params=_cp(("arbitrary",)))(a, w, g, xres, target)


def mm_norm_bwd(a_views, b_views, stages, exchange=(), *, tm=256, name):
    n_p, n_s, ne = len(a_views), len(stages), len(exchange)
    s = a_views[0][0].shape[0]
    d = b_views[0][0].shape[0]
    n = s // tm
    has_res = [st[2] is not None for st in stages]

    def body(*refs):
        a_refs, b_refs = refs[:n_p], refs[n_p:2 * n_p]
        at = 2 * n_p
        st_refs = []
        for k in range(n_s):
            cnt = 3 if has_res[k] else 2
            st_refs.append(refs[at:at + cnt])
            at += cnt
        e_ins = refs[at:at + ne]
        at += ne
        dx_refs, dg_refs = refs[at:at + n_s], refs[at + n_s:at + 2 * n_s]
        at += 2 * n_s
        e_outs = refs[at:at + ne]
        at += ne
        accs = refs[at:at + n_s]
        comm = (e_ins, e_outs) + tuple(refs[at + n_s:])
        i = pl.program_id(0)

        @pl.when(i == 0)
        def _():
            for acc in accs:
                acc[...] = jnp.zeros_like(acc)
            if ne:
                _to_chips_start(*comm)

        dh = None
        for p in range(n_p):
            part = _nt(a_refs[p][...].astype(BF16), b_refs[p][...].astype(BF16))
            dh = part if dh is None else dh + part
        for k in range(n_s):
            xv = st_refs[k][0][...]
            inv = lax.rsqrt(jnp.mean(xv * xv, axis=-1, keepdims=True) + RMS_EPS)
            dx, part = _norm_bwd_rows(dh, xv * inv, inv, st_refs[k][1][...])
            if has_res[k]:
                dx = dx + st_refs[k][2][...]
            dx_refs[k][...] = dx.astype(dx_refs[k].dtype)
            accs[k][...] += part
            dh = dx

        @pl.when(i == n - 1)
        def _():
            for k in range(n_s):
                dg_refs[k][...] = jnp.sum(accs[k][...], axis=0, keepdims=True)
            if ne:
                _to_chips_finish(*comm)

    row = pl.BlockSpec((tm, d), lambda i: (i, 0))
    vec = pl.BlockSpec((1, d), lambda i: (0, 0))
    in_specs, args = [], []
    for arr, w, blk in a_views:
        in_specs.append(pl.BlockSpec((tm, w), functools.partial(lambda i, blk: (i, blk), blk=blk)))
        args.append(arr)
    for arr, w, blk in b_views:
        in_specs.append(pl.BlockSpec((d, w), functools.partial(lambda i, blk: (0, blk), blk=blk)))
        args.append(arr)
    for x, g, res, _ in stages:
        in_specs += [row, vec] + ([row] if res is not None else [])
        args += [x, g] + ([res] if res is not None else [])
    return pl.pallas_call(
        body, grid=(n,), in_specs=in_specs + [ANY] * ne,
        out_specs=[row] * n_s + [vec] * n_s + [ANY] * ne,
        out_shape=[SDS((s, d), st[3]) for st in stages] + [SDS((1, d), F32)] * n_s + _to_chips_shapes(exchange),
        scratch_shapes=[pltpu.VMEM((8, d), F32)] * n_s + (_to_chips_sems(ne) if ne else []), name=name,
        compiler_params=_cp(("arbitrary",)))(*args, *exchange)


def _split3(v):
    hi = v.astype(BF16).astype(F32)
    r = v - hi
    mid = r.astype(BF16).astype(F32)
    lo = (r - mid).astype(BF16).astype(F32)
    return hi, mid, lo


def _tri(n, upper):
    r = np.arange(n)
    m = (r[:, None] <= r[None, :]) if upper else (r[:, None] >= r[None, :])
    return jnp.asarray(m.astype(np.float32))


def fox_prep(z, fa, bfo, *, tb=512):
    s = z.shape[0]
    n = s // tb

    def body(q_ref, k_ref, v_ref, fa_ref, b_ref, tri_ref, qa_ref, ka_ref, va_ref, carry):
        @pl.when(pl.program_id(0) == 0)
        def _():
            carry[...] = jnp.zeros_like(carry)

        xv = fa_ref[...] + b_ref[...]
        logf = jnp.minimum(xv, 0.0) - jnp.log(1.0 + jnp.exp(-jnp.abs(xv)))
        csum = jnp.dot(tri_ref[...], logf, preferred_element_type=F32, precision=lax.Precision.HIGHEST) + carry[0:1, :]
        carry[0:1, :] = csum[tb - 1:tb, :]
        lane = _lane((tb, LANES))
        for h in range(N_HEADS):
            hi, mid, lo = _split3(csum[:, h:h + 1])
            pair = (h // 2) * LANES
            qv = q_ref[:, pair:pair + LANES].astype(F32)
            kv = k_ref[:, pair:pair + LANES].astype(F32)
            vv = v_ref[:, pair:pair + LANES].astype(F32)
            if h % 2:
                qv = pltpu.roll(qv, 64, axis=1)
                kv = pltpu.roll(kv, 64, axis=1)
                vv = pltpu.roll(vv, 64, axis=1)
            va_ref[:, h * LANES:(h + 1) * LANES] = jnp.where(lane < 64, vv, jnp.where(lane == 64, 1.0, 0.0)).astype(BF16)
            one = jnp.where((lane >= 67) & (lane < 70), 1.0, 0.0)
            q_x = jnp.where(lane == 64, hi, jnp.where(lane == 65, mid, jnp.where(lane == 66, lo, one)))
            one = jnp.where((lane >= 64) & (lane < 67), 1.0, 0.0)
            k_x = jnp.where(lane == 67, -hi, jnp.where(lane == 68, -mid, jnp.where(lane == 69, -lo, one)))
            qa_ref[:, h * LANES:(h + 1) * LANES] = jnp.where(lane < 64, qv * QK_SCALE, q_x).astype(BF16)
            ka_ref[:, h * LANES:(h + 1) * LANES] = jnp.where(lane < 64, kv, k_x).astype(BF16)

    return pl.pallas_call(
        body, grid=(n,),
        in_specs=[pl.BlockSpec((tb, ATT_W), lambda i: (i, Z_QA)), pl.BlockSpec((tb, ATT_W), lambda i: (i, Z_KA)),
                  pl.BlockSpec((tb, ATT_W), lambda i: (i, Z_VA)),
                  pl.BlockSpec((tb, LANES), lambda i: (i, 0)), pl.BlockSpec((1, LANES), lambda i: (0, 0)),
                  pl.BlockSpec((tb, tb), lambda i: (0, 0))],
        out_specs=[pl.BlockSpec((tb, N_HEADS * LANES), lambda i: (i, 0))] * 3,
        out_shape=[SDS((s, N_HEADS * LANES), BF16)] * 3,
        scratch_shapes=[pltpu.VMEM((8, LANES), F32)],
        name="fox_prep", compiler_params=_cp(("arbitrary",)))(z, z, z, fa, bfo, _tri(tb, False))


def _causal_pairs(n, k_major):
    if k_major:
        pairs = [(qi, kj) for kj in range(n) for qi in range(kj, n)]
    else:
        pairs = [(qi, kj) for qi in range(n) for kj in range(qi + 1)]
    return (jnp.asarray([p[0] for p in pairs], jnp.int32), jnp.asarray([p[1] for p in pairs], jnp.int32), len(pairs))


def fox_fwd(q_aug, k_aug, v_aug, gather=(), halved=(), *, t=1024, hps=4):
    s = v_aug.shape[0]
    qi_arr, kj_arr, n_pairs = _causal_pairs(s // t, False)
    ng = len(gather)
    n_groups = N_HEADS // hps

    def body(qi_ref, kj_ref, q_ref, k_ref, v_ref, *rest):
        g_ins, (o_ref, lse_ref), g_outs = rest[:ng], rest[ng:ng + 2], rest[ng + 2:2 * ng + 2]
        m_scr, acc_scr = rest[2 * ng + 2:2 * ng + 4]
        comm = (g_ins, g_outs) + tuple(rest[2 * ng + 4:]) + (list(halved),)
        step = pl.program_id(1)
        qi = qi_ref[step]
        kj = kj_ref[step]
        if ng:
            @pl.when((pl.program_id(0) == 0) & (step == 0))
            def _():
                _allgather_start(*comm)

        @pl.when(kj == 0)
        def _():
            m_scr[...] = jnp.full_like(m_scr, NEG)
            acc_scr[...] = jnp.zeros_like(acc_scr)

        def update(qs, ks, masked):
            nq, nk = qs.stop - qs.start, ks.stop - ks.start
            for i in range(hps):
                own = slice(i * LANES, (i + 1) * LANES)
                sc = _nt(q_ref[qs, own], k_ref[ks, own])
                if masked:
                    sc = jnp.where(_row((nq, nk)) >= _lane((nq, nk)), sc, NEG)
                m_prev = m_scr[i, qs]
                m_new = jnp.maximum(m_prev, jnp.max(sc, axis=-1, keepdims=True))
                p = jnp.exp((sc - jnp.tile(m_new, (1, nk // LANES))).astype(BF16))
                acc_scr[i, qs] = jnp.exp(m_prev - m_new) * acc_scr[i, qs] + _nn(p, v_ref[ks, own])
                m_scr[i, qs] = m_new

        whole, upper, lower = slice(0, t), slice(0, t // 2), slice(t // 2, t)

        @pl.when(kj < qi)
        def _():
            update(whole, whole, False)

        @pl.when(kj == qi)
        def _():
            update(upper, upper, True)
            update(lower, upper, False)
            update(lower, lower, True)
            lane = _lane((t, LANES))
            for pr in range(hps // 2):
                den = [acc_scr[2 * pr + i][:, 64:65] for i in range(2)]
                o_ref[:, pr * LANES:(pr + 1) * LANES] = jnp.where(
                    lane < 64, acc_scr[2 * pr] / den[0], pltpu.roll(acc_scr[2 * pr + 1] / den[1], 64, axis=1)).astype(o_ref.dtype)
                lse_ref[:, pr * LANES:(pr + 1) * LANES] = jnp.where(
                    lane < 64, m_scr[2 * pr] + jnp.log(den[0]), m_scr[2 * pr + 1] + jnp.log(den[1]))

        if ng:
            @pl.when((pl.program_id(0) == n_groups - 1) & (step == n_pairs - 1))
            def _():
                _allgather_finish(*comm)

    wide = hps * LANES
    grid_spec = pltpu.PrefetchScalarGridSpec(
        num_scalar_prefetch=2, grid=(n_groups, n_pairs),
        in_specs=[pl.BlockSpec((t, wide), lambda hg, st, qi, kj: (qi[st], hg)),
                  pl.BlockSpec((t, wide), lambda hg, st, qi, kj: (kj[st], hg)),
                  pl.BlockSpec((t, wide), lambda hg, st, qi, kj: (kj[st], hg))] + [ANY] * ng,
        out_specs=[pl.BlockSpec((t, wide // 2), lambda hg, st, qi, kj: (qi[st], hg))] * 2 + [ANY] * ng,
        scratch_shapes=[pltpu.VMEM((hps, t, LANES), F32)] * 2 + (_allgather_sems(ng) if ng else []))
    return pl.pallas_call(
        body, grid_spec=grid_spec, out_shape=[SDS((s, ATT_W), BF16), SDS((s, ATT_W), F32)] + _allgather_shapes(gather),
        name="fox_fwd", compiler_params=_cp(("arbitrary", "arbitrary")))(qi_arr, kj_arr, q_aug, k_aug, v_aug, *gather)


def fox_bwd(q_aug, k_aug, z, dy, lse, dd, exchange=(), kind="to_chips", *, t=1024, hps=4):
    s = z.shape[0]
    qi_arr, kj_arr, n_pairs = _causal_pairs(s // t, True)
    ne = len(exchange)
    n_groups = N_HEADS // hps
    x_shapes, x_sems, x_start, x_finish = EXCHANGES[kind]

    def body(qi_ref, kj_ref, q_ref, k_ref, v_ref, do_ref, lse_ref, dd_ref, *rest):
        e_ins, (dq_ref, dk_ref, dv_ref), e_outs = rest[:ne], rest[ne:ne + 3], rest[ne + 3:2 * ne + 3]
        comm = (e_ins, e_outs) + tuple(rest[2 * ne + 3:])
        step = pl.program_id(1)
        qi = qi_ref[step]
        kj = kj_ref[step]
        if ne:
            @pl.when((pl.program_id(0) == 0) & (step == 0))
            def _():
                x_start(*comm)

        @pl.when(step == 0)
        def _():
            dq_ref[...] = jnp.zeros_like(dq_ref)

        @pl.when(qi == kj)
        def _():
            dk_ref[...] = jnp.zeros_like(dk_ref)
            dv_ref[...] = jnp.zeros_like(dv_ref)

        def update(qs, ks, masked):
            nq, nk = qs.stop - qs.start, ks.stop - ks.start
            lane = _lane((nq, LANES))
            rows = pl.ds(pl.multiple_of(qi * t + qs.start, nq), nq)
            for pr in range(hps // 2):
                pair = slice(pr * LANES, (pr + 1) * LANES)
                dov = do_ref[qs, pair]
                dv_new = None
                for i in range(2):
                    head = (lane < 64) if i == 0 else (lane >= 64)
                    own = slice((2 * pr + i) * LANES, (2 * pr + i + 1) * LANES)
                    col = slice(pr * LANES + i * 64, pr * LANES + i * 64 + 1)
                    qv = q_ref[qs, own]
                    kv = k_ref[ks, own]
                    sc = _nt(qv, kv)
                    if masked:
                        sc = jnp.where(_row((nq, nk)) >= _lane((nq, nk)), sc, NEG)
                    p = jnp.exp(sc - lse_ref[qs, col])
                    dp = _nt(jnp.where(head, dov, jnp.zeros_like(dov)), v_ref[ks, pair])
                    ds = (p * (dp - dd_ref[qs, col])).astype(BF16)
                    dq_ref[rows, own] += _nn(ds, kv)
                    dk_ref[ks, own] += _tn(ds, qv)
                    dvi = _tn(p.astype(BF16), dov)
                    dv_new = dvi if dv_new is None else jnp.where(head, dvi, dv_new)
                dv_ref[ks, pair] += dv_new

        whole, upper, lower = slice(0, t), slice(0, t // 2), slice(t // 2, t)

        @pl.when(kj < qi)
        def _():
            update(whole, whole, False)

        @pl.when(kj == qi)
        def _():
            update(upper, upper, True)
            update(lower, upper, False)
            update(lower, lower, True)

        if ne:
            @pl.when((pl.program_id(0) == n_groups - 1) & (step == n_pairs - 1))
            def _():
                x_finish(*comm)

    wide, half = hps * LANES, hps // 2 * LANES
    v_blk = Z_VA * ATT_W // half
    grid_spec = pltpu.PrefetchScalarGridSpec(
        num_scalar_prefetch=2, grid=(n_groups, n_pairs),
        in_specs=[pl.BlockSpec((t, wide), lambda hg, st, qi, kj: (qi[st], hg)),
                  pl.BlockSpec((t, wide), lambda hg, st, qi, kj: (kj[st], hg)),
                  pl.BlockSpec((t, half), lambda hg, st, qi, kj: (kj[st], v_blk + hg)),
                  pl.BlockSpec((t, half), lambda hg, st, qi, kj: (qi[st], hg)),
                  pl.BlockSpec((t, half), lambda hg, st, qi, kj: (qi[st], hg)),
                  pl.BlockSpec((t, half), lambda hg, st, qi, kj: (qi[st], hg))] + [ANY] * ne,
        out_specs=[pl.BlockSpec((s, wide), lambda hg, st, qi, kj: (0, hg)),
                   pl.BlockSpec((t, wide), lambda hg, st, qi, kj: (kj[st], hg)),
                   pl.BlockSpec((t, half), lambda hg, st, qi, kj: (kj[st], hg))] + [ANY] * ne,
        scratch_shapes=x_sems(ne) if ne else [])
    return pl.pallas_call(
        body, grid_spec=grid_spec,
        out_shape=[SDS((s, N_HEADS * LANES), F32), SDS((s, N_HEADS * LANES), F32), SDS((s, ATT_W), F32)]
        + x_shapes(exchange),
        name="fox_bwd", compiler_params=_cp(("arbitrary", "arbitrary")))(qi_arr, kj_arr, q_aug, k_aug, z, dy, lse, dd, *exchange)


def fox_post(dq_aug, dk_aug, dv, fa, bfo, *, tb=512):
    s = dv.shape[0]
    n = s // tb

    def body(dq_ref, dk_ref, dv_ref, fa_ref, b_ref, tri_ref, dz_ref, dfa_ref, gb_ref, carry, acc):
        i = pl.program_id(0)

        @pl.when(i == 0)
        def _():
            carry[...] = jnp.zeros_like(carry)
            acc[...] = jnp.zeros_like(acc)

        lane = _lane((tb, LANES))
        d_f = jnp.zeros((tb, LANES), F32)
        for h in range(N_HEADS):
            col = dq_ref[:, h * LANES + 64:h * LANES + 65] - dk_ref[:, h * LANES + 67:h * LANES + 68]
            d_f = jnp.where(lane == h, col, d_f)
        suffix = jnp.dot(tri_ref[...], d_f, preferred_element_type=F32, precision=lax.Precision.HIGHEST) + carry[0:1, :]
        carry[0:1, :] = suffix[0:1, :]
        xv = fa_ref[...] + b_ref[...]
        dx = suffix * (1.0 / (1.0 + jnp.exp(xv)))
        dfa_ref[...] = dx.astype(dfa_ref.dtype)
        acc[...] += jnp.sum(dx.reshape(tb // 8, 8, LANES), axis=0)
        for hp in range(4):
            for src, off, scale in ((dq_ref, 0, QK_SCALE), (dk_ref, ATT_W, 1.0)):
                even = src[:, (2 * hp) * LANES:(2 * hp + 1) * LANES]
                odd = pltpu.roll(src[:, (2 * hp + 1) * LANES:(2 * hp + 2) * LANES], 64, axis=1)
                dz_ref[:, off + hp * LANES:off + (hp + 1) * LANES] = (jnp.where(lane < 64, even, odd) * scale).astype(BF16)
        dz_ref[:, 2 * ATT_W:3 * ATT_W] = dv_ref[...].astype(BF16)

        @pl.when(i == n - 1)
        def _():
            gb_ref[...] = jnp.sum(acc[...], axis=0, keepdims=True)

    rev = lambda i: (n - 1 - i, 0)
    return pl.pallas_call(
        body, grid=(n,),
        in_specs=[pl.BlockSpec((tb, N_HEADS * LANES), rev), pl.BlockSpec((tb, N_HEADS * LANES), rev),
                  pl.BlockSpec((tb, ATT_W), rev), pl.BlockSpec((tb, LANES), rev),
                  pl.BlockSpec((1, LANES), lambda i: (0, 0)), pl.BlockSpec((tb, tb), lambda i: (0, 0))],
        out_specs=[pl.BlockSpec((tb, 3 * ATT_W), rev), pl.BlockSpec((tb, LANES), rev),
                   pl.BlockSpec((1, LANES), lambda i: (0, 0))],
        out_shape=[SDS((s, 3 * ATT_W), BF16), SDS((s, LANES), BF16), SDS((1, LANES), F32)],
        scratch_shapes=[pltpu.VMEM((8, LANES), F32), pltpu.VMEM((8, LANES), F32)],
        name="fox_post", compiler_params=_cp(("arbitrary",)))(dq_aug, dk_aug, dv, fa, bfo, _tri(tb, True))


def rope_cos_sin(s):
    half = ROPE_DIM // 2
    inv_freq = ROPE_THETA ** (-jnp.arange(half, dtype=F32) * 2.0 / ROPE_DIM)
    ang = jnp.arange(s, dtype=F32)[:, None] * inv_freq[None, :]
    return jnp.tile(jnp.cos(ang), (1, LANES // half)), jnp.tile(jnp.sin(ang), (1, LANES // half))


def _rotate(x, cos, sin, sign):
    l64 = _lane(x.shape) & (HEAD_DIM - 1)
    first = l64 < ROPE_DIM // 2
    second = (l64 >= ROPE_DIM // 2) & (l64 < ROPE_DIM)
    from_next = jnp.where(first, -sign * sin, 0.0)
    from_prev = jnp.where(second, sign * sin, 0.0)
    return (x * jnp.where(first | second, cos, 1.0) + pltpu.roll(x, LANES - 8, axis=1) * from_next
            + pltpu.roll(x, 8, axis=1) * from_prev)


def _dil_rows(base, r):
    if r == 1:
        return pl.ds(pl.multiple_of(base, DIL_BLK), DIL_BLK)
    return pl.ds(base, DIL_BLK, stride=r)


def _dil_block(idx, r, nb):
    shift = nb.bit_length() - 1
    rho = idx >> shift
    n = idx & (nb - 1)
    base = rho + n * (r * DIL_BLK)
    return _dil_rows(base, r), _dil_rows(jnp.maximum(base - r * DIL_BLK, rho), r), n > 0


def _cat(a, b):
    return jnp.concatenate([a, b], axis=0)


def _two_heads(v, first_head):
    zero = jnp.zeros_like(v)
    return _cat(jnp.where(first_head, v, zero), jnp.where(first_head, zero, v))


def _dil_bands():
    b = DIL_BLK
    q = _row((2 * b, 2 * b)) & (b - 1)
    col = _lane((2 * b, 2 * b))
    return (col < b) & (col >= q), (col >= b) & (col - b <= q)


def _dil_load_qkv(zq_ref, zk_ref, zv_ref, cos_ref, sin_ref, q_ref, k_ref, v_ref, *, chunk=512):
    def step(i, carry):
        rows = pl.ds(pl.multiple_of(i * chunk, chunk), chunk)
        cos, sin = cos_ref[rows, :], sin_ref[rows, :]
        q_ref[rows, :] = _rotate(zq_ref[rows, :].astype(F32), cos, sin, 1.0) * QK_SCALE
        k_ref[rows, :] = _rotate(zk_ref[rows, :].astype(F32), cos, sin, 1.0)
        v_ref[rows, :] = zv_ref[rows, :].astype(F32)
        return carry

    lax.fori_loop(0, q_ref.shape[0] // chunk, step, 0)


def dil_fwd_all(z, cos_t, sin_t, *, unroll=32):
    s = z.shape[0]
    b = DIL_BLK
    n_blk = s // b

    def body(zq_ref, zk_ref, zv_ref, cos_ref, sin_ref, o_ref, l_ref, q_ref, k_ref, v_ref):
        _dil_load_qkv(zq_ref, zk_ref, zv_ref, cos_ref, sin_ref, q_ref, k_ref, v_ref)
        first_head = _lane((b, LANES)) < 64
        band_prev, band_cur = _dil_bands()
        for g, (_, r) in enumerate(DIL_PATTERNS):
            nb = n_blk // r

            def group(it, carry, g=g, r=r, nb=nb):
                loaded = []
                kc = vc = None
                for u in range(unroll):
                    rows_c, rows_p, has_prev = _dil_block(it * unroll + u, r, nb)
                    if u % min(nb, unroll):
                        kp, vp = kc, vc
                    else:
                        kp, vp = k_ref[rows_p, :].astype(BF16), v_ref[rows_p, :].astype(BF16)
                    kc, vc = k_ref[rows_c, :].astype(BF16), v_ref[rows_c, :].astype(BF16)
                    state = (o_ref[rows_c, :], l_ref[rows_c, :]) if g else None
                    loaded.append((rows_c, has_prev, [q_ref[rows_c, :].astype(BF16), kp, kc, vp, vc], state))
                done = []
                for rows_c, has_prev, (qv, kp, kc, vp, vc), state in loaded:
                    sc = jnp.where(band_cur | (band_prev & has_prev), _nt(_two_heads(qv, first_head), _cat(kp, kc)), NEG)
                    m = jnp.max(sc, axis=-1, keepdims=True)
                    p = jnp.exp(sc - m)
                    den = jnp.sum(p, axis=-1, keepdims=True)
                    both = _nn(p.astype(BF16), _cat(vp, vc)) / den
                    lse2 = m + jnp.log(den)
                    ov = jnp.where(first_head, both[:b], both[b:])
                    lse = jnp.where(first_head, lse2[:b], lse2[b:])
                    if state is not None:
                        m2 = jnp.maximum(state[1], lse)
                        wp = jnp.exp(state[1] - m2)
                        wn = jnp.exp(lse - m2)
                        ov = (wp * state[0] + wn * ov) / (wp + wn)
                        lse = m2 + jnp.log(wp + wn)
                    done.append((rows_c, ov, lse))
                for rows_c, ov, lse in done:
                    o_ref[rows_c, :] = ov
                    l_ref[rows_c, :] = lse
                return carry

            lax.fori_loop(0, n_blk // unroll, group, 0)

    col_blk = lambda k: pl.BlockSpec((s, LANES), lambda hp: (0, 4 * k + hp))
    table = pl.BlockSpec((s, LANES), lambda hp: (0, 0))
    out = pl.BlockSpec((s, LANES), lambda hp: (0, hp))
    return pl.pallas_call(
        body, grid=(4,), in_specs=[col_blk(Z_QB), col_blk(Z_KB), col_blk(Z_VB), table, table], out_specs=[out, out],
        out_shape=[SDS((s, ATT_W), F32)] * 2, scratch_shapes=[pltpu.VMEM((s, LANES), F32)] * 3, name="dil_fwd",
        compiler_params=_cp(("parallel",)))(z, z, z, cos_t, sin_t)


def dil_bwd_all(z, cos_t, sin_t, dy, lse, y, exchange=(), kind="to_chips", *, unroll=16):
    s = z.shape[0]
    b = DIL_BLK
    n_blk = s // b
    ne = len(exchange)
    x_shapes, x_sems, x_start, x_finish = EXCHANGES[kind]

    def body(zq_ref, zk_ref, zv_ref, cos_ref, sin_ref, do_ref, l_ref, y_ref, *rest):
        e_ins, (gq_ref, gk_ref, gv_ref), e_outs = rest[:ne], rest[ne:ne + 3], rest[ne + 3:2 * ne + 3]
        q_ref, k_ref, v_ref, dq_ref, dk_ref, dv_ref = rest[2 * ne + 3:2 * ne + 9]
        comm = (e_ins, e_outs) + tuple(rest[2 * ne + 9:])
        if ne:
            @pl.when(pl.program_id(0) == 0)
            def _():
                x_start(*comm)

        _dil_load_qkv(zq_ref, zk_ref, zv_ref, cos_ref, sin_ref, q_ref, k_ref, v_ref)
        dq_ref[...] = jnp.zeros_like(dq_ref)
        dk_ref[...] = jnp.zeros_like(dk_ref)
        dv_ref[...] = jnp.zeros_like(dv_ref)
        first_head = _lane((b, LANES)) < 64
        band_prev, band_cur = _dil_bands()
        for _, r in DIL_PATTERNS:
            nb = n_blk // r

            def group(it, carry, r=r, nb=nb):
                loaded = []
                kc = vc = None
                for u in range(unroll):
                    rows_c, rows_p, has_prev = _dil_block(it * unroll + u, r, nb)
                    if u % min(nb, unroll):
                        kp, vp = kc, vc
                    else:
                        kp, vp = k_ref[rows_p, :].astype(BF16), v_ref[rows_p, :].astype(BF16)
                    kc, vc = k_ref[rows_c, :].astype(BF16), v_ref[rows_c, :].astype(BF16)
                    vals = [q_ref[rows_c, :].astype(BF16), kp, kc, vp, vc, do_ref[rows_c, :], l_ref[rows_c, :], y_ref[rows_c, :]]
                    loaded.append((rows_c, rows_p, has_prev, vals))
                done = []
                for rows_c, rows_p, has_prev, (qv, kp, kc, vp, vc, dof, lv, yv) in loaded:
                    q2 = _two_heads(qv, first_head)
                    do2 = _two_heads(dof.astype(BF16), first_head)
                    kcat, vcat = _cat(kp, kc), _cat(vp, vc)
                    lse2 = _cat(lv[:, 0:1], lv[:, 64:65])
                    dd2 = jnp.sum(_two_heads(dof * yv, first_head), axis=-1, keepdims=True)
                    p = jnp.exp(jnp.where(band_cur | (band_prev & has_prev), _nt(q2, kcat), NEG) - lse2)
                    ds = (p * (_nt(do2, vcat) - dd2)).astype(BF16)
                    dq2 = _nn(ds, kcat)
                    dkcat = _tn(ds, q2)
                    dvcat = _tn(p.astype(BF16), do2)
                    done.append((rows_c, rows_p, (jnp.where(first_head, dq2[:b], dq2[b:]), dkcat[:b], dkcat[b:],
                                                  dvcat[:b], dvcat[b:])))
                held = None
                for u, (rows_c, rows_p, (dq, dk_p, dk_c, dv_p, dv_c)) in enumerate(done):
                    dq_ref[rows_c, :] += dq
                    if u % min(nb, unroll):
                        rows_h, dk_h, dv_h = held
                        dk_ref[rows_h, :] += dk_h + dk_p
                        dv_ref[rows_h, :] += dv_h + dv_p
                    else:
                        if held is not None:
                            dk_ref[held[0], :] += held[1]
                            dv_ref[held[0], :] += held[2]
                        dk_ref[rows_p, :] += dk_p
                        dv_ref[rows_p, :] += dv_p
                    held = (rows_c, dk_c, dv_c)
                dk_ref[held[0], :] += held[1]
                dv_ref[held[0], :] += held[2]
                return carry

            lax.fori_loop(0, n_blk // unroll, group, 0)

        def finish(i, carry, chunk=512):
            rows = pl.ds(pl.multiple_of(i * chunk, chunk), chunk)
            cos, sin = cos_ref[rows, :], sin_ref[rows, :]
            gq_ref[rows, :] = (_rotate(dq_ref[rows, :], cos, sin, -1.0) * QK_SCALE).astype(BF16)
            gk_ref[rows, :] = _rotate(dk_ref[rows, :], cos, sin, -1.0).astype(BF16)
            gv_ref[rows, :] = dv_ref[rows, :].astype(BF16)
            return carry

        lax.fori_loop(0, s // 512, finish, 0)
        if ne:
            @pl.when(pl.program_id(0) == 3)
            def _():
                x_finish(*comm)

    col_blk = lambda k: pl.BlockSpec((s, LANES), lambda hp: (0, 4 * k + hp))
    table = pl.BlockSpec((s, LANES), lambda hp: (0, 0))
    nat = pl.BlockSpec((s, LANES), lambda hp: (0, hp))
    return pl.pallas_call(
        body, grid=(4,), in_specs=[col_blk(Z_QB), col_blk(Z_KB), col_blk(Z_VB), table, table, nat, nat, nat] + [ANY] * ne,
        out_specs=[nat, nat, nat] + [ANY] * ne, out_shape=[SDS((s, ATT_W), BF16)] * 3 + x_shapes(exchange),
        scratch_shapes=[pltpu.VMEM((s, LANES), F32)] * 6 + (x_sems(ne) if ne else []), name="dil_bwd",
        compiler_params=_cp(("arbitrary",)))(z, z, z, cos_t, sin_t, dy, lse, y, *exchange)


def _sigmoid(v):
    return 1.0 / (1.0 + jnp.exp(-v))


def gate_mix(ya, yb, wa, wb, z, *, tm=2048, tn=512):
    s = ya.shape[0]
    d = wa.shape[1]
    ga_blk = 3 * ATT_W * 2 // tn
    gb_blk = ga_blk + d // tn

    def body(ya_ref, yb_ref, wa_ref, wb_ref, ga_ref, gb_ref, pa_ref, pb_ref, mx_ref):
        pa = _nn(ya_ref[...], wa_ref[...])
        pb = _nn(yb_ref[...].astype(BF16), wb_ref[...])
        pa_ref[...] = pa.astype(BF16)
        pb_ref[...] = pb.astype(BF16)
        mx_ref[...] = (_sigmoid(ga_ref[...].astype(F32)) * pa + _sigmoid(gb_ref[...].astype(F32)) * pb).astype(BF16)

    out = pl.BlockSpec((tm, tn), lambda i, j: (i, j))
    return pl.pallas_call(
        body, grid=(s // tm, d // tn),
        in_specs=[pl.BlockSpec((tm, ATT_W), lambda i, j: (i, 0)), pl.BlockSpec((tm, ATT_W), lambda i, j: (i, 0)),
                  pl.BlockSpec((ATT_W, tn), lambda i, j: (0, j)), pl.BlockSpec((ATT_W, tn), lambda i, j: (0, j)),
                  pl.BlockSpec((tm, tn), lambda i, j: (i, ga_blk + j)), pl.BlockSpec((tm, tn), lambda i, j: (i, gb_blk + j))],
        out_specs=[out, out, out], out_shape=[SDS((s, d), BF16)] * 3, name="gate_mix",
        compiler_params=_cp(("parallel", "parallel")))(ya, yb, wa, wb, z, z)


def mix_bwd(dy, w_o, z, pa, pb, wo_a, wo_b, ya, *, tm=512):
    s, d = dy.shape

    def body(dy_ref, wo_ref, ga_ref, gb_ref, pa_ref, pb_ref, wa_ref, wb_ref, ya_ref,
             dpa_ref, dpb_ref, dg_ref, dya_ref, dyb_ref, dd_ref):
        dm = _nt(dy_ref[...], wo_ref[...])
        sa = _sigmoid(ga_ref[...].astype(F32))
        sb = _sigmoid(gb_ref[...].astype(F32))
        dpa = (dm * sa).astype(BF16)
        dpb = (dm * sb).astype(BF16)
        dpa_ref[...] = dpa
        dpb_ref[...] = dpb
        dg_ref[:, 0:d] = (dm * pa_ref[...].astype(F32) * sa * (1.0 - sa)).astype(BF16)
        dg_ref[:, d:2 * d] = (dm * pb_ref[...].astype(F32) * sb * (1.0 - sb)).astype(BF16)
        dya = _nt(dpa, wa_ref[...]).astype(BF16)
        dya_ref[...] = dya
        dyb_ref[...] = _nt(dpb, wb_ref[...])
        lane = _lane((tm, LANES))
        for pr in range(ATT_W // LANES):
            pair = slice(pr * LANES, (pr + 1) * LANES)
            prod = dya[:, pair].astype(F32) * ya_ref[:, pair].astype(F32)
            lo = jnp.sum(jnp.where(lane < 64, prod, 0.0), axis=-1, keepdims=True)
            hi = jnp.sum(jnp.where(lane >= 64, prod, 0.0), axis=-1, keepdims=True)
            dd_ref[:, pair] = jnp.where(lane < 64, lo, hi)

    row = pl.BlockSpec((tm, d), lambda i: (i, 0))
    att = pl.BlockSpec((tm, ATT_W), lambda i: (i, 0))
    whole = lambda a: pl.BlockSpec(a.shape, lambda i: (0, 0))
    return pl.pallas_call(
        body, grid=(s // tm,),
        in_specs=[row, whole(w_o), pl.BlockSpec((tm, d), lambda i: (i, 3)), pl.BlockSpec((tm, d), lambda i: (i, 4)), row, row,
                  whole(wo_a), whole(wo_b), att],
        out_specs=[row, row, pl.BlockSpec((tm, 2 * d), lambda i: (i, 0)), att, att, att],
        out_shape=[SDS((s, d), BF16), SDS((s, d), BF16), SDS((s, 2 * d), BF16), SDS((s, ATT_W), BF16),
                   SDS((s, ATT_W), F32), SDS((s, ATT_W), F32)], name="mix_bwd",
        compiler_params=_cp(("parallel",)))(dy, w_o, z, z, pa, pb, wo_a, wo_b, ya)


GELU_C = math.sqrt(2.0 / math.pi)


def _gelu_parts(a):
    a2 = a * a
    th = jnp.tanh(a * (GELU_C + (GELU_C * 0.044715) * a2))
    half = 0.5 * a
    gelu = half + half * th
    dgelu = (0.5 + 0.5 * th) + half * (1.0 - th * th) * (GELU_C + (3.0 * GELU_C * 0.044715) * a2)
    return gelu, dgelu


def _causal_taps(u, before):
    row = _row(u.shape)
    r1 = jnp.where(row == 0, before[7:8, :], pltpu.roll(u, 1, axis=0))
    r2 = jnp.where(row == 0, before[6:7, :], jnp.where(row == 1, before[7:8, :], pltpu.roll(u, 2, axis=0)))
    return r1, r2


def ffn_up(h, wa, wb, cw, cb, *, tm=2048, tn=256):
    s, d = h.shape
    f = wa.shape[1]
    nj = f // tn

    def body(h_ref, wa_ref, wb_ref, cwa_ref, cwb_ref, cba_ref, cbb_ref, ua_ref, ub_ref, ca_ref, cbo_ref, m_ref, carry):
        @pl.when(pl.program_id(1) == 0)
        def _():
            carry[...] = jnp.zeros_like(carry)

        conv = []
        for k, (w_ref, cw_ref, cb_ref, u_ref, c_ref) in enumerate(((wa_ref, cwa_ref, cba_ref, ua_ref, ca_ref),
                                                                   (wb_ref, cwb_ref, cbb_ref, ub_ref, cbo_ref))):
            u = _nn(h_ref[...], w_ref[...])
            u_ref[...] = u.astype(BF16)
            r1, r2 = _causal_taps(u, carry[k])
            carry[k] = u[tm - 8:tm, :]
            conv.append(cw_ref[0:1, :] * r2 + cw_ref[1:2, :] * r1 + cw_ref[2:3, :] * u + cb_ref[...])
            c_ref[...] = conv[k].astype(BF16)
        m_ref[...] = (_gelu_parts(conv[0])[0] * conv[1]).astype(BF16)

    out = pl.BlockSpec((tm, tn), lambda j, i: (i, j))
    return pl.pallas_call(
        body, grid=(nj, s // tm),
        in_specs=[pl.BlockSpec((tm, d), lambda j, i: (i, 0)),
                  pl.BlockSpec((d, tn), lambda j, i: (0, j)), pl.BlockSpec((d, tn), lambda j, i: (0, j)),
                  pl.BlockSpec((3, tn), lambda j, i: (0, j)), pl.BlockSpec((3, tn), lambda j, i: (0, nj + j)),
                  pl.BlockSpec((1, tn), lambda j, i: (0, j)), pl.BlockSpec((1, tn), lambda j, i: (0, nj + j))],
        out_specs=[out] * 5, out_shape=[SDS((s, f), BF16)] * 5,
        scratch_shapes=[pltpu.VMEM((2, 8, tn), F32)], name="ffn_up",
        compiler_params=_cp(("parallel", "arbitrary")))(h, wa, wb, cw, cw, cb, cb)


def ffn_bwd(dm, ua, ub, ca, cbo, cw, *, tm=2048, tn=256):
    s, f = dm.shape
    nj = f // tn
    ni = s // tm

    def body(dm_ref, ua_ref, ub_ref, ca_ref, cbo_ref, cwa_ref, cwb_ref, dua_ref, dub_ref, ga_ref, gb_ref, carry):
        @pl.when(pl.program_id(1) == 0)
        def _():
            carry[...] = jnp.zeros_like(carry)
            ga_ref[...] = jnp.zeros_like(ga_ref)
            gb_ref[...] = jnp.zeros_like(gb_ref)

        row = _row((tm, tn))
        dmv = dm_ref[...].astype(F32)
        gelu, dgelu = _gelu_parts(ca_ref[...].astype(F32))
        dcs = (dmv * cbo_ref[...].astype(F32) * dgelu, dmv * gelu)
        for k, (dc, u_ref, cw_ref, du_ref, g_ref) in enumerate(((dcs[0], ua_ref, cwa_ref, dua_ref, ga_ref),
                                                                (dcs[1], ub_ref, cwb_ref, dub_ref, gb_ref))):
            u = u_ref[...].astype(F32)
            after = carry[k]
            n1 = jnp.where(row == tm - 1, after[0:1, :], pltpu.roll(dc, tm - 1, axis=0))
            n2 = jnp.where(row == tm - 2, after[0:1, :], jnp.where(row == tm - 1, after[1:2, :], pltpu.roll(dc, tm - 2, axis=0)))
            g_ref[0:1, :] += jnp.sum(n2 * u, axis=0, keepdims=True)
            g_ref[1:2, :] += jnp.sum(n1 * u, axis=0, keepdims=True)
            g_ref[2:3, :] += jnp.sum(dc * u, axis=0, keepdims=True)
            g_ref[3:4, :] += jnp.sum(dc, axis=0, keepdims=True)
            du_ref[...] = (cw_ref[2:3, :] * dc + cw_ref[1:2, :] * n1 + cw_ref[0:1, :] * n2).astype(BF16)
            carry[k] = dc[0:8, :]

    tile = pl.BlockSpec((tm, tn), lambda j, i: (ni - 1 - i, j))
    gspec = pl.BlockSpec((8, tn), lambda j, i: (0, j))
    return pl.pallas_call(
        body, grid=(nj, ni),
        in_specs=[tile] * 5 + [pl.BlockSpec((3, tn), lambda j, i: (0, j)), pl.BlockSpec((3, tn), lambda j, i: (0, nj + j))],
        out_specs=[tile, tile, gspec, gspec],
        out_shape=[SDS((s, f), BF16), SDS((s, f), BF16), SDS((8, f), F32), SDS((8, f), F32)],
        scratch_shapes=[pltpu.VMEM((2, 8, tn), F32)], name="ffn_bwd",
        compiler_params=_cp(("parallel", "arbitrary")))(dm, ua, ub, ca, cbo, cw, cw)


def adamw(w, g, m, v, *, name, tr=None):
    r = w.shape[0]
    rest = w.shape[1:]
    if tr is None:
        tr = r
        for cand in (512, 352, 256, 128, 64, 32, 16, 8):
            if r % cand == 0:
                tr = cand
                break

    def body(w_ref, g_ref, m_ref, v_ref, d_ref, nm_ref, nv_ref):
        gv = g_ref[...]
        mn = ADAM_B1 * m_ref[...] + (1.0 - ADAM_B1) * gv
        vn = ADAM_B2 * v_ref[...] + (1.0 - ADAM_B2) * (gv * gv)
        m_hat = mn / (1.0 - ADAM_B1 ** ADAM_STEP)
        v_hat = vn / (1.0 - ADAM_B2 ** ADAM_STEP)
        d_ref[...] = -ADAM_LR * (m_hat / (jnp.sqrt(v_hat) + ADAM_EPS) + ADAM_WD * w_ref[...])
        nm_ref[...] = mn
        nv_ref[...] = vn

    blk = pl.BlockSpec((tr,) + rest, lambda i: (i,) + (0,) * len(rest))
    return pl.pallas_call(body, grid=(r // tr,), in_specs=[blk] * 4, out_specs=[blk] * 3, out_shape=[SDS(w.shape, F32)] * 3,
                          name=name, compiler_params=_cp(("parallel",)))(w, g, m, v)


def adamw_many(ws, gs, ms, vs, *, name):
    n = len(ws)

    def body(*refs):
        for k in range(n):
            w_ref, g_ref, m_ref, v_ref = (refs[j * n + k] for j in range(4))
            d_ref, nm_ref, nv_ref = refs[4 * n + 3 * k:4 * n + 3 * k + 3]
            gv = g_ref[...]
            mn = ADAM_B1 * m_ref[...] + (1.0 - ADAM_B1) * gv
            vn = ADAM_B2 * v_ref[...] + (1.0 - ADAM_B2) * (gv * gv)
            m_hat = mn / (1.0 - ADAM_B1 ** ADAM_STEP)
            v_hat = vn / (1.0 - ADAM_B2 ** ADAM_STEP)
            d_ref[...] = -ADAM_LR * (m_hat / (jnp.sqrt(v_hat) + ADAM_EPS) + ADAM_WD * w_ref[...])
            nm_ref[...] = mn
            nv_ref[...] = vn

    whole = [pl.BlockSpec(a.shape, functools.partial(lambda i, nd: (0,) * nd, nd=a.ndim)) for a in ws]
    outs = pl.pallas_call(body, grid=(1,), in_specs=whole * 4, out_specs=[s for s in whole for _ in range(3)],
                          out_shape=[SDS(a.shape, F32) for a in ws for _ in range(3)], name=name,
                          compiler_params=_cp(("arbitrary",)))(*ws, *gs, *ms, *vs)
    return [outs[3 * k:3 * k + 3] for k in range(n)]


def adamw_rows_view(w, g_mine, g_full, m, v, c_arr, *, name):
    _, c, r = w.shape
    flat = lambda a: jnp.transpose(a, (2, 0, 1)).reshape(r, c // LANES, LANES)
    unflat = lambda a: jnp.transpose(a, (1, 2, 0)).reshape(1, c, r)

    def body(c_ref, w_ref, gm_ref, gf_ref, m_ref, v_ref, d_ref, nm_ref, nv_ref, go_ref):
        low = lax.broadcasted_iota(jnp.int32, (r, c), 1) < c // 2
        gm = gm_ref[...]
        g2 = jnp.where(low == (c_ref[0] == 0), jnp.concatenate([gm, gm], axis=1), gf_ref[...])
        gv = g2.reshape(r, c // LANES, LANES)
        mn = ADAM_B1 * m_ref[...] + (1.0 - ADAM_B1) * gv
        vn = ADAM_B2 * v_ref[...] + (1.0 - ADAM_B2) * (gv * gv)
        m_hat = mn / (1.0 - ADAM_B1 ** ADAM_STEP)
        v_hat = vn / (1.0 - ADAM_B2 ** ADAM_STEP)
        d_ref[...] = -ADAM_LR * (m_hat / (jnp.sqrt(v_hat) + ADAM_EPS) + ADAM_WD * w_ref[...])
        nm_ref[...] = mn
        nv_ref[...] = vn
        go_ref[...] = gv

    once = pl.Buffered(1)
    b3 = pl.BlockSpec((r, c // LANES, LANES), lambda i, c_ref: (0, 0, 0), pipeline_mode=once)
    own = pl.BlockSpec((r, c // 2), lambda i, c_ref: (0, 0), pipeline_mode=once)
    full = pl.BlockSpec((r, c), lambda i, c_ref: (0, 0), pipeline_mode=once)
    grid_spec = pltpu.PrefetchScalarGridSpec(num_scalar_prefetch=1, grid=(1,), in_specs=[b3, own, full, b3, b3],
                                             out_specs=[b3] * 4)
    outs = pl.pallas_call(body, grid_spec=grid_spec, out_shape=[SDS((r, c // LANES, LANES), F32)] * 4, name=name,
                          compiler_params=_cp(("arbitrary",)))(c_arr, flat(w), g_mine, g_full, flat(m), flat(v))
    return [unflat(a) for a in outs]


ANY = pl.BlockSpec(memory_space=pl.ANY)
ICI_KINDS = ("x", "y", "xy")


def _coords():
    return lax.axis_index("x"), lax.axis_index("y"), lax.axis_index("c")


def _peer(kind, x, y, c):
    if kind == "c":
        return (x, y, 1 - c)
    if kind == "x":
        return (1 - x, y, c)
    if kind == "y":
        return (x, 1 - y, c)
    return (1 - x, 1 - y, c)


def _chip_of(p):
    return 2 * p[0] + p[1]


def _half(rows, which):
    h = rows // 2
    return pl.ds(pl.multiple_of(which * h, 16), h)


def _remote(src, dst, send_sem, recv_sem, to):
    return pltpu.make_async_remote_copy(src_ref=src, dst_ref=dst, send_sem=send_sem, recv_sem=recv_sem,
                                        device_id=to, device_id_type=MESH)


def allgather_balanced(shard, *, name):
    r, cols = shard.shape
    h, q = r // 2, r // 4

    def body(in_ref, out_ref, send_sems, recv_sems):
        x, y, c = _coords()
        me, sibling = (x, y, c), (x, y, 1 - c)
        nbr = ((1 - x, y, c), (x, 1 - y, c))
        chip = (2 * (1 - x) + y, 2 * x + (1 - y), 2 * (1 - x) + (1 - y))
        quarter = lambda core, i: pl.ds(pl.multiple_of(core * h + i * q, 16), q)
        sent = []

        def go(src, dst, slot, to):
            cp = _remote(src, dst, send_sems.at[slot], recv_sems.at[slot], to)
            cp.start()
            sent.append(cp)

        def landed(region, slot):
            _remote(region, region, send_sems.at[slot], recv_sems.at[slot], me).wait_recv()

        for i in range(2):
            for k in range(2):
                qi = k if i == 0 else 1 - k
                go(in_ref.at[quarter(c, qi)], out_ref.at[2 * x + y, quarter(c, qi)], 2 * k + qi, nbr[k])
        for k in range(2):
            piece = out_ref.at[chip[k], quarter(c, k)]
            landed(piece, 2 * k + k)
            go(piece, piece, 4 + k, nbr[1 - k])
            go(piece, piece, 6 + 2 * k + k, sibling)
        for k in range(2):
            piece = out_ref.at[chip[k], quarter(c, 1 - k)]
            landed(piece, 2 * k + 1 - k)
            go(piece, piece, 6 + 2 * k + 1 - k, sibling)
        for k in range(2):
            piece = out_ref.at[chip[2], quarter(c, k)]
            landed(piece, 4 + k)
            go(piece, piece, 10 + k, sibling)
        for k in range(2):
            for i in range(2):
                landed(out_ref.at[chip[k], quarter(1 - c, i)], 6 + 2 * k + i)
            landed(out_ref.at[chip[2], quarter(1 - c, k)], 10 + k)
        for cp in sent:
            cp.wait_send()

    return pl.pallas_call(
        body, in_specs=[ANY], out_specs=ANY, out_shape=SDS((4,) + shard.shape, shard.dtype),
        scratch_shapes=[pltpu.SemaphoreType.DMA((12,)), pltpu.SemaphoreType.DMA((12,))], name=name)(shard)


def _allgather_shapes(shards):
    return [SDS((4,) + a.shape, a.dtype) for a in shards]


def _allgather_sems(n):
    return [pltpu.SemaphoreType.DMA((n, 6)), pltpu.SemaphoreType.DMA((n, 6))]


def _allgather_rows(ref, is_halved, which):
    r = ref.shape[0]
    return _half(r, which) if is_halved else pl.ds(0, r)


def _allgather_first(ins, outs, send_sems, recv_sems, halved):
    x, y, c = _coords()
    my_chip = 2 * x + y
    cps = []
    for w in range(len(ins)):
        rows = _allgather_rows(ins[w], halved[w], c)
        for k, kind in enumerate(ICI_KINDS):
            cps.append(_remote(ins[w].at[rows], outs[w].at[my_chip, rows], send_sems.at[w, k], recv_sems.at[w, k],
                               _peer(kind, x, y, c)))
    return cps


def _allgather_start(ins, outs, send_sems, recv_sems, halved):
    for cp in _allgather_first(ins, outs, send_sems, recv_sems, halved):
        cp.start()


def _allgather_finish(ins, outs, send_sems, recv_sems, halved):
    x, y, c = _coords()
    me = (x, y, c)
    second = []
    for w in range(len(ins)):
        for k, kind in enumerate(ICI_KINDS):
            landed = outs[w].at[_chip_of(_peer(kind, x, y, c)), _allgather_rows(ins[w], halved[w], c)]
            _remote(landed, landed, send_sems.at[w, k], recv_sems.at[w, k], me).wait_recv()
            if halved[w]:
                cp = _remote(landed, landed, send_sems.at[w, 3 + k], recv_sems.at[w, 3 + k], _peer("c", x, y, c))
                cp.start()
                second.append(cp)
    for w in range(len(ins)):
        if halved[w]:
            for k, kind in enumerate(ICI_KINDS):
                other = outs[w].at[_chip_of(_peer(kind, x, y, c)), _allgather_rows(ins[w], True, 1 - c)]
                _remote(other, other, send_sems.at[w, 3 + k], recv_sems.at[w, 3 + k], me).wait_recv()
    for cp in _allgather_first(ins, outs, send_sems, recv_sems, halved) + second:
        cp.wait_send()


def _half_of(ref, by_cols, which):
    lead = (slice(None),) * (len(ref.shape) - 2)
    if by_cols:
        h = ref.shape[-1] // 2
        return ref.at[lead + (slice(None), pl.ds(pl.multiple_of(which * h, LANES), h))]
    return ref.at[lead + (_half(ref.shape[-2], which),)]


def _half_shape(shape, by_cols):
    return shape[:-1] + (shape[-1] // 2,) if by_cols else shape[:-2] + (shape[-2] // 2, shape[-1])


def grads_to_sibling(gs, by_cols, *, name):
    n = len(gs)

    def body(*refs):
        ins, outs = refs[:n], refs[n:2 * n]
        send_sems, recv_sems = refs[2 * n:]
        x, y, c = _coords()
        cps = []
        for w in range(n):
            cp = _remote(_half_of(ins[w], by_cols[w], 1 - c), outs[w], send_sems.at[w], recv_sems.at[w], _peer("c", x, y, c))
            cp.start()
            cps.append(cp)
        for cp in cps:
            cp.wait()

    return pl.pallas_call(
        body, in_specs=[ANY] * n, out_specs=[ANY] * n,
        out_shape=[SDS(_half_shape(a.shape, bc), a.dtype) for a, bc in zip(gs, by_cols)],
        scratch_shapes=[pltpu.SemaphoreType.DMA((n,)), pltpu.SemaphoreType.DMA((n,))], name=name)(*gs)


def _to_chips_shapes(ps):
    return [SDS((3,) + a.shape[1:], a.dtype) for a in ps]


def _to_chips_sems(n):
    return [pltpu.SemaphoreType.DMA((n, 3)), pltpu.SemaphoreType.DMA((n, 3))]


def _to_chips_copies(ins, outs, send_sems, recv_sems):
    x, y, c = _coords()
    cps = []
    for w in range(len(ins)):
        for k, kind in enumerate(ICI_KINDS):
            to = _peer(kind, x, y, c)
            cps.append(_remote(ins[w].at[_chip_of(to)], outs[w].at[k], send_sems.at[w, k], recv_sems.at[w, k], to))
    return cps


def _to_chips_start(ins, outs, send_sems, recv_sems):
    for cp in _to_chips_copies(ins, outs, send_sems, recv_sems):
        cp.start()


def _to_chips_finish(ins, outs, send_sems, recv_sems):
    for cp in _to_chips_copies(ins, outs, send_sems, recv_sems):
        cp.wait()


def _to_owners_shapes(ps):
    return [SDS((7, a.shape[1] // 2, a.shape[2]), a.dtype) for a in ps]


def _to_owners_sems(n):
    return [pltpu.SemaphoreType.DMA((n, 7)), pltpu.SemaphoreType.DMA((n, 7))]


def _to_owners_copies(ins, outs, send_sems, recv_sems):
    x, y, c = _coords()
    cps = []
    for w in range(len(ins)):
        rows = ins[w].shape[1]
        for k, kind in enumerate(ICI_KINDS):
            px, py, _ = _peer(kind, x, y, c)
            for h in range(2):
                cps.append(_remote(ins[w].at[2 * px + py, _half(rows, h)], outs[w].at[2 * k + c],
                                   send_sems.at[w, 2 * k + h], recv_sems.at[w, 2 * k + c], (px, py, h)))
        cps.append(_remote(ins[w].at[2 * x + y, _half(rows, 1 - c)], outs[w].at[6], send_sems.at[w, 6], recv_sems.at[w, 6],
                           _peer("c", x, y, c)))
    return cps


def _to_owners_start(ins, outs, send_sems, recv_sems):
    for cp in _to_owners_copies(ins, outs, send_sems, recv_sems):
        cp.start()


def _to_owners_finish(ins, outs, send_sems, recv_sems):
    for cp in _to_owners_copies(ins, outs, send_sems, recv_sems):
        cp.wait_send()
    for w in range(len(ins)):
        for slot in range(7):
            got = outs[w].at[slot]
            _remote(got, got, send_sems.at[w, slot], recv_sems.at[w, slot], _coords()).wait_recv()


EXCHANGES = {"to_chips": (_to_chips_shapes, _to_chips_sems, _to_chips_start, _to_chips_finish),
             "to_owners": (_to_owners_shapes, _to_owners_sems, _to_owners_start, _to_owners_finish)}


def halves_to_full(hs, by_cols, *, name):
    n = len(hs)

    def body(*refs):
        ins, outs = refs[:n], refs[n:2 * n]
        send_sems, recv_sems = refs[2 * n:]
        x, y, c = _coords()
        cps = []
        for w in range(n):
            cp = _remote(ins[w], _half_of(outs[w], by_cols[w], c), send_sems.at[w], recv_sems.at[w], _peer("c", x, y, c))
            cp.start()
            cps.append(cp)
        for cp in cps:
            cp.wait()

    return pl.pallas_call(
        body, in_specs=[ANY] * n, out_specs=[ANY] * n,
        out_shape=[SDS((a.shape[0], 2 * a.shape[1]) if bc else (2 * a.shape[0], a.shape[1]), a.dtype)
                   for a, bc in zip(hs, by_cols)],
        scratch_shapes=[pltpu.SemaphoreType.DMA((n,)), pltpu.SemaphoreType.DMA((n,))],
        name=name)(*hs)


def _row_tile(rows):
    for cand in (256, 192, 176, 128, 64, 32, 16):
        if rows % cand == 0:
            return cand
    return rows


def chip_sum(g, recv, c_arr, by_cols, *, name):
    _, r, cols = g.shape

    def body(c_ref, g_ref, r_ref, f_ref, b_ref):
        tot = g_ref[...] + r_ref[...]
        f_ref[...] = tot
        b_ref[...] = tot.astype(BF16)

    if by_cols:
        tc = 4 * LANES
        nblk = cols // 2 // tc
        shape = (4, r, cols // 2)
        blk = pl.BlockSpec((None, r, tc), lambda j, i, c_ref: (j, 0, i))
        mine = pl.BlockSpec((None, r, tc), lambda j, i, c_ref: (j, 0, c_ref[0] * nblk + i))
    else:
        tr = _row_tile(r // 2)
        nblk = r // 2 // tr
        shape = (4, r // 2, cols)
        blk = pl.BlockSpec((None, tr, cols), lambda j, i, c_ref: (j, i, 0))
        mine = pl.BlockSpec((None, tr, cols), lambda j, i, c_ref: (j, c_ref[0] * nblk + i, 0))
    grid_spec = pltpu.PrefetchScalarGridSpec(num_scalar_prefetch=1, grid=(4, nblk), in_specs=[mine, blk], out_specs=[blk, blk])
    return pl.pallas_call(body, grid_spec=grid_spec, out_shape=[SDS(shape, F32), SDS(shape, BF16)],
                          name=name, compiler_params=_cp(("parallel", "parallel")))(c_arr, g, recv)


def final_sum(pf, recv, chip_arr, *, name):
    _, h, cols = pf.shape
    tr = _row_tile(h)

    def body(chip_ref, p_ref, r_ref, o_ref):
        o_ref[...] = ((p_ref[...] + r_ref[0].astype(F32)) + r_ref[1].astype(F32)) + r_ref[2].astype(F32)

    grid_spec = pltpu.PrefetchScalarGridSpec(
        num_scalar_prefetch=1, grid=(h // tr,),
        in_specs=[pl.BlockSpec((None, tr, cols), lambda i, chip_ref: (chip_ref[0], i, 0)),
                  pl.BlockSpec((3, tr, cols), lambda i, chip_ref: (0, i, 0))],
        out_specs=pl.BlockSpec((tr, cols), lambda i, chip_ref: (i, 0)))
    return pl.pallas_call(body, grid_spec=grid_spec, out_shape=SDS((h, cols), F32), name=name,
                          compiler_params=_cp(("parallel",)))(chip_arr, pf, recv)


def owner_sum(g, recv, pos_arr, *, name):
    _, r, cols = g.shape
    h = r // 2
    tr = _row_tile(h)
    nblk = h // tr

    def body(pos_ref, g_ref, r_ref, o_ref):
        tot = g_ref[...]
        for slot in range(7):
            tot = tot + r_ref[slot].astype(F32)
        o_ref[...] = tot

    grid_spec = pltpu.PrefetchScalarGridSpec(
        num_scalar_prefetch=1, grid=(nblk,),
        in_specs=[pl.BlockSpec((None, tr, cols), lambda i, pos: (pos[0], pos[1] * nblk + i, 0)),
                  pl.BlockSpec((7, tr, cols), lambda i, pos: (0, i, 0))],
        out_specs=pl.BlockSpec((tr, cols), lambda i, pos: (i, 0)))
    return pl.pallas_call(body, grid_spec=grid_spec, out_shape=SDS((h, cols), F32), name=name,
                          compiler_params=_cp(("parallel",)))(pos_arr, g, recv)


def allreduce_small(v, *, name):
    rws, cols = v.shape

    def body(v_ref, all_ref, sum_ref, send_sems, recv_sems, local_sem):
        x, y, c = _coords()
        me, sibling = (x, y, c), (x, y, 1 - c)
        chips = [(1 - x, y), (x, 1 - y), (1 - x, 1 - y)]

        def rows(px, py, pc):
            return all_ref.at[pl.ds(pl.multiple_of((4 * px + 2 * py + pc) * rws, 8), rws), :]

        def copy(k, block, to, src=None):
            return _remote(rows(*block) if src is None else src, rows(*block), send_sems.at[k], recv_sems.at[k], to)

        mine = pltpu.make_async_copy(v_ref, rows(*me), local_sem)
        mine.start()
        first = [copy(0, me, sibling, src=v_ref)]
        first += [copy(1 + j, me, (*chip, c), src=v_ref) for j, chip in enumerate(chips)]
        for cp in first:
            cp.start()
        passed = [copy(4 + j, (*chip, c), sibling) for j, chip in enumerate(chips)]
        for j, chip in enumerate(chips):
            copy(1 + j, (*chip, c), me).wait_recv()
            passed[j].start()
        copy(0, sibling, me).wait_recv()
        for j, chip in enumerate(chips):
            copy(4 + j, (*chip, 1 - c), me).wait_recv()
        for cp in first + passed:
            cp.wait_send()
        mine.wait()
        tot = all_ref[0:rws, :]
        for dev in range(1, 8):
            tot = tot + all_ref[dev * rws:(dev + 1) * rws, :]
        sum_ref[...] = tot

    vm = pl.BlockSpec(memory_space=pltpu.VMEM)
    return pl.pallas_call(
        body, in_specs=[vm], out_specs=[vm, vm],
        out_shape=[SDS((8 * rws, cols), v.dtype), SDS((rws, cols), v.dtype)],
        scratch_shapes=[pltpu.SemaphoreType.DMA((7,)), pltpu.SemaphoreType.DMA((7,)), pltpu.SemaphoreType.DMA],
        name=name)(v)[1]


def _pack_rows(parts, rows):
    out = []
    for a, r in zip(parts, rows):
        flat = a.reshape(-1)
        flat = jnp.pad(flat, (0, r * LANES - flat.shape[0]))
        out.append(flat.reshape(r, LANES))
    return jnp.concatenate(out, axis=0)


def _unpack_rows(packed, shapes, rows):
    out, at = [], 0
    for shp, r in zip(shapes, rows):
        size = int(np.prod(shp))
        out.append(packed[at:at + r].reshape(-1)[:size].reshape(shp))
        at += r
    return out


def kernel(x, g_pre_mix, w_in, b_forget, w_o_fox, w_o_dil, w_out, g_post_mix, g_pre_ffn, w_up, conv_w, conv_b, w_down, g_post_ffn, loss_target, m_g_pre_mix, m_w_in, m_b_forget, m_w_o_fox, m_w_o_dil, m_w_out, m_g_post_mix, m_g_pre_ffn, m_w_up, m_conv_w, m_conv_b, m_w_down, m_g_post_ffn, v_g_pre_mix, v_w_in, v_b_forget, v_w_o_fox, v_w_o_dil, v_w_out, v_g_post_mix, v_g_pre_ffn, v_w_up, v_conv_w, v_conv_b, v_w_down, v_g_post_ffn):
    xi, yi, ci = _coords()
    chip = 2 * xi + yi
    c_arr = jnp.reshape(ci, (1,)).astype(jnp.int32)
    chip_arr = jnp.reshape(chip, (1,)).astype(jnp.int32)
    xs = x[0]
    target = loss_target[0]
    s, d = xs.shape
    f_half = w_down.shape[1] * 4
    cols_in = w_in.shape[2]

    big = (w_in, w_o_fox, w_o_dil, w_out, w_up, w_down)
    shards = [w[0].astype(BF16) for w in big]
    a_in = allgather_balanced(shards[0], name="allgather_w_in")
    w_in_full = jnp.concatenate([jnp.where(chip == j, shards[0], a_in[j]) for j in range(4)], axis=1)
    nf = N_HEADS
    e_a, e_b = 3 * ATT_W, 3 * ATT_W + nf
    wz = jnp.concatenate([w_in_full[:, :e_a], w_in_full[:, e_b:]], axis=1)
    wf = jnp.pad(w_in_full[:, e_a:e_b], ((0, 0), (0, LANES - nf)))
    cb = conv_b
    bfo = jnp.pad(b_forget, ((0, 0), (0, LANES - nf)))

    h1 = rmsnorm_fwd(xs, g_pre_mix)
    z = mm([(h1, d, 0)], [(wz, d, 0)], nt=False, out_dtype=BF16, tm=s, tn=1024, name="in_proj")
    fa = mm([(h1, d, 0)], [(wf, d, 0)], nt=False, out_dtype=F32, tm=s, tn=LANES, name="in_proj_forget")
    q_aug, k_aug, v_aug = fox_prep(z, fa, bfo)
    later = shards[1:] + [conv_w[0]]
    ya, lse_a, *late = fox_fwd(q_aug, k_aug, v_aug, gather=later, halved=[True] * 5 + [False], hps=4)
    a_of, a_od, a_out, a_up, a_down, a_cw = [
        lax.dynamic_update_index_in_dim(a4, own, chip, 0) for a4, own in zip(late, later)]
    cw = jnp.concatenate([a_cw[j] for j in range(4)], axis=1)
    wo_a = jnp.concatenate([a_of[j] for j in range(4)], axis=1)
    wo_b = jnp.concatenate([a_od[j] for j in range(4)], axis=1)
    w_o = a_out.reshape(d, d)
    w_dn = a_down.reshape(f_half, d)
    wu_a = jnp.concatenate([a_up[0], a_up[1]], axis=1)
    wu_b = jnp.concatenate([a_up[2], a_up[3]], axis=1)
    cos_t, sin_t = rope_cos_sin(s)
    yb, lse_b = dil_fwd_all(z, cos_t, sin_t)
    pa, pb, mixed = gate_mix(ya, yb, wo_a, wo_b, z)
    y1, x1, h2 = proj_norm_res(mixed, w_o, g_post_mix, xs, g_pre_ffn, tm=1024, name="out_proj")
    ua, ub, conv_a, conv_bh, mid = ffn_up(h2, wu_a, wu_b, cw, cb)
    dout, dy2, gg_post_ffn, sq = proj_norm_loss(mid, w_dn, g_post_ffn, x1, target, name="down_proj")

    dmid = mm([(dy2, d, 0)], [(w_dn, d, 0)], nt=True, out_dtype=BF16, tm=2048, tn=f_half // 2, name="down_dgrad")
    dw_down, dw_down16 = wgrad((mid, f_half, 0), dy2, tk=f_half // 2, tn=1024, ts=2048, name="down_wgrad", bf16_copy=True)
    dua, dub, gc_a, gc_b = ffn_bwd(dmid, ua, ub, conv_a, conv_bh, cw)
    dx1, dy1, gg_pre_ffn, gg_post_mix = mm_norm_bwd(
        [(dua, f_half, 0), (dub, f_half, 0)], [(wu_a, f_half, 0), (wu_b, f_half, 0)],
        [(x1, g_pre_ffn, dout, F32), (y1, g_post_mix, None, BF16)], name="up_dgrad")
    dw_up = None
    for k, du in enumerate((dua, dub)):
        dw_up = wgrad((h2, d, 0), du, tk=1024, tn=f_half // 2, ts=2048, name=f"up_wgrad_{k}", chip_major=True,
                      slabs=(4, 2 * k), into=dw_up, bf16_copy=True)
    g_ffn = [(dw_up[0], dw_up[1]), (dw_down.reshape(4, f_half // 4, d), dw_down16.reshape(4, f_half // 4, d))]
    dw_out, dw_out16 = wgrad((mixed, d, 0), dy1, tk=1024, tn=1024, ts=2048, name="out_wgrad", bf16_copy=True)
    dpa, dpb, dz_g, dya, dyb, dd_a = mix_bwd(dy1, w_o, z, pa, pb, wo_a, wo_b, ya)
    by_chip_cols = lambda a: jnp.stack([a[:, j * (d // 4):(j + 1) * (d // 4)] for j in range(4)], axis=0)
    dw_of = [by_chip_cols(a) for a in wgrad((ya, ATT_W, 0), dpa, tk=ATT_W, tn=d, ts=1024, name="fox_o_wgrad", bf16_copy=True)]
    dw_od = [by_chip_cols(a) for a in wgrad((yb, ATT_W, 0), dpb, tk=ATT_W, tn=d, ts=1024, name="dil_o_wgrad", bf16_copy=True)]
    g_mix = [dw_of, dw_od, (dw_out.reshape(4, d // 4, d), dw_out16.reshape(4, d // 4, d))]
    dq_aug, dk_aug, dv_a, *got_ffn = fox_bwd(q_aug, k_aug, z, dya, lse_a, dd_a, exchange=[g[1] for g in g_ffn], kind="to_owners")
    dz_a, dfa, gg_bf = fox_post(dq_aug, dk_aug, dv_a, fa, bfo)
    *dz_b, got_of, got_od, got_out = dil_bwd_all(z, cos_t, sin_t, dyb, lse_b, yb, exchange=[g[1] for g in g_mix],
                                                 kind="to_owners")
    got_mix = [got_of, got_od, got_out]
    dwt_a = wgrad((dz_a, e_a, 0), h1, tk=e_a // 2, tn=d, ts=2048, name="in_wgrad_a")
    dwt_b = [wgrad((part, ATT_W, 0), h1, tk=ATT_W, tn=d, ts=2048, name=f"in_wgrad_b{k}") for k, part in enumerate(dz_b)]
    dwt_g = wgrad((dz_g, 2 * d, 0), h1, tk=d, tn=d, ts=2048, name="in_wgrad_g")
    dwt_f = wgrad((dfa, LANES, 0), h1, tk=LANES, tn=d, ts=2048, name="in_wgrad_f")
    dwt_full = jnp.concatenate([dwt_a, dwt_f[:nf], *dwt_b, dwt_g], axis=0)
    dw_in = jnp.stack([dwt_full[j * cols_in:(j + 1) * cols_in] for j in range(4)], axis=0)
    from_sib = grads_to_sibling([dw_in], [True], name="grads_to_sibling_in")
    sum_in = chip_sum(dw_in, from_sib[0], c_arr, True, name="chip_sum_w_in")
    grad_x, gg_pre_mix, got_in = mm_norm_bwd(
        [(dz_a, e_a, 0), *[(part, ATT_W, 0) for part in dz_b], (dz_g, d, 0), (dz_g, d, 1), (dfa, LANES, 0)],
        [(wz, e_a, 0), *[(wz, ATT_W, Z_QB + k) for k in range(3)], (wz, d, 3), (wz, d, 4), (wf, LANES, 0)],
        [(xs, g_pre_mix, dx1, F32)], exchange=[sum_in[1]], name="in_dgrad")

    names = ("w_in", "w_o_fox", "w_o_dil", "w_out", "w_up", "w_down")
    pos_arr = jnp.concatenate([chip_arr, c_arr])
    halves = [final_sum(sum_in[0], got_in, chip_arr, name="final_sum_w_in")] + [
        owner_sum(g[0], got, pos_arr, name=f"owner_sum_{nm}") for g, got, nm in zip(g_mix + g_ffn, got_mix + got_ffn, names[1:])]
    from_half = halves_to_full(halves, [True] + [False] * 5, name="halves_to_full")
    g_big = [None] + [lax.dynamic_update_slice_in_dim(full, mine, ci * mine.shape[0], axis=0)
                      for full, mine in zip(from_half[1:], halves[1:])]
    upd_big = [adamw(w[0], g, m[0], v[0], name=f"adamw_{nm}") for w, g, m, v, nm in list(zip(
        big, g_big, (m_w_in, m_w_o_fox, m_w_o_dil, m_w_out, m_w_up, m_w_down),
        (v_w_in, v_w_o_fox, v_w_o_dil, v_w_out, v_w_up, v_w_down), names))[1:]]
    *upd_in, g_in = adamw_rows_view(w_in, halves[0], from_half[0], m_w_in, v_w_in, c_arr, name="adamw_w_in")

    g_cw_loc = jnp.concatenate([gc_a[0:3], gc_b[0:3]], axis=1)
    g_cb_loc = jnp.concatenate([gc_a[3:4], gc_b[3:4]], axis=1)
    small_loc = [gg_pre_mix, gg_post_mix, gg_pre_ffn, gg_post_ffn, g_cb_loc, gg_bf[:, :nf], g_cw_loc, sq * (0.5 / d)]
    red_rows = (8, 8, 8, 8, 48, 8, 136, 8)
    red = allreduce_small(_pack_rows(small_loc, red_rows), name="allreduce_small")
    g_pm, g_qm, g_pf, g_qf, g_cb, g_bf, g_cw_full, loss_11 = _unpack_rows(red, [a.shape for a in small_loc], red_rows)
    loss = loss_11[0, 0]
    cols_cw = conv_w.shape[2]
    g_cw = lax.dynamic_slice_in_dim(g_cw_full, chip * cols_cw, cols_cw, axis=1)
    small_w = (g_pre_mix, g_post_mix, g_pre_ffn, g_post_ffn, conv_b, b_forget, conv_w[0])
    small_m = (m_g_pre_mix, m_g_post_mix, m_g_pre_ffn, m_g_post_ffn, m_conv_b, m_b_forget, m_conv_w[0])
    small_v = (v_g_pre_mix, v_g_post_mix, v_g_pre_ffn, v_g_post_ffn, v_conv_b, v_b_forget, v_conv_w[0])
    small_g = (g_pm, g_qm, g_pf, g_qf, g_cb, g_bf, g_cw)
    small_names = ("g_pre_mix", "g_post_mix", "g_pre_ffn", "g_post_ffn", "conv_b", "b_forget", "conv_w")
    per_param = adamw_many(small_w, small_g, small_m, small_v, name="adamw_small")
    upd_small = [[u[j] for u in per_param] for j in range(3)]

    order = ("g_pre_mix", "w_in", "b_forget", "w_o_fox", "w_o_dil", "w_out", "g_post_mix", "g_pre_ffn", "w_up", "conv_w",
             "conv_b", "w_down", "g_post_ffn")
    grads, deltas, new_ms, new_vs = {}, {}, {}, {}
    grads["w_in"] = g_in
    deltas["w_in"], new_ms["w_in"], new_vs["w_in"] = upd_in
    for k, nm in enumerate(names[1:]):
        grads[nm] = g_big[k + 1][None]
        deltas[nm], new_ms[nm], new_vs[nm] = (a[None] for a in upd_big[k])
    for k, nm in enumerate(small_names):
        lead = (lambda a: a[None]) if nm == "conv_w" else (lambda a: a)
        grads[nm] = lead(small_g[k])
        deltas[nm], new_ms[nm], new_vs[nm] = (lead(upd_small[j][k]) for j in range(3))
    return (loss, grad_x[None], *[grads[nm] for nm in order], *[deltas[nm] for nm in order],
            *[new_ms[nm] for nm in order], *[new_vs[nm] for nm in order])
```
